```python
import jax, jax.numpy as jnp
from jax import lax
import numpy as np

D_MODEL = 2048
BATCH = 8
SEQ = 2048
DEPTH = 2

N_EVEN = (DEPTH + 1) // 2
N_ODD = DEPTH // 2
EPS = 1e-6

A_HEADS = 8
A_DK = 128
A_DV = 128
A_QK = A_HEADS * A_DK
A_WIDTH = A_HEADS * A_DV
A_CHUNK = 64
A_SUB = 16

B_GROUPS = 8
B_GROUP_DIM = 128
B_WIDTH = B_GROUPS * B_GROUP_DIM
B_CHUNK = 128

EVEN_IN = 2 * A_QK + 2 * A_WIDTH + 3 * B_WIDTH

C_HEADS = 16
C_Q_RANK = 512
C_KV_RANK = 512
C_NOPE = 128
C_ROPE = 64
C_QK = C_NOPE + C_ROPE
C_V = 128
C_WIDTH = C_HEADS * C_V
ROPE_THETA = 10000.0
Q_BLOCK = 128
ODD_IN = C_Q_RANK + C_KV_RANK + C_ROPE + C_WIDTH

MAX_POS_OFFSET = 4096

kernel_name = 'hybrid_hgrn2_gmlp_mla_sandwich'


def rms_norm(x, w):
    xf = x.astype(jnp.float32)
    y = xf * lax.rsqrt(jnp.mean(xf * xf, axis=-1, keepdims=True) + EPS)
    return (y * w.astype(jnp.float32)).astype(x.dtype)


def layer_norm(x, w, b):
    xf = x.astype(jnp.float32)
    mu = jnp.mean(xf, axis=-1, keepdims=True)
    xc = xf - mu
    y = xc * lax.rsqrt(jnp.mean(xc * xc, axis=-1, keepdims=True) + EPS)
    return (y * w.astype(jnp.float32) + b.astype(jnp.float32)).astype(x.dtype)


def split_sizes(z, sizes):
    offs, acc = [], 0
    for s in sizes[:-1]:
        acc += s
        offs.append(acc)
    return jnp.split(z, offs, axis=-1)


def hgrn2_chunked(q, k, v, log_f):
    bsz, T, H, K = q.shape
    V = v.shape[-1]
    C, c = A_CHUNK, A_SUB
    N, M = T // C, C // c

    def to_chunks(a):
        return a.reshape(bsz, N, C, H, a.shape[-1]).transpose(0, 3, 1, 2, 4)

    q, k, v, log_f = (to_chunks(a) for a in (q, k, v, log_f))
    b = jnp.cumsum(log_f, axis=3)
    b_last = b[:, :, :, -1, :]

    chunk_kv = jnp.einsum('bhnck,bhncv->bhnkv', k * jnp.exp(b_last[:, :, :, None, :] - b), v)

    def carry_state(state, inputs):
        decay, kv = inputs
        return decay[..., None] * state + kv, state

    _, s_prev = lax.scan(carry_state, jnp.zeros((bsz, H, K, V), q.dtype),
                         (jnp.moveaxis(jnp.exp(b_last), 2, 0), jnp.moveaxis(chunk_kv, 2, 0)))
    s_prev = jnp.moveaxis(s_prev, 0, 2)
    o_inter = jnp.einsum('bhnck,bhnkv->bhncv', q * jnp.exp(b), s_prev)

    g = (b - log_f)[:, :, :, ::c, :]
    q_sub = q.reshape(bsz, H, N, M, c, K) * jnp.exp(b.reshape(bsz, H, N, M, c, K) - g[:, :, :, :, None, :])
    sub_idx = jnp.arange(C) // c
    reach = sub_idx[None, :] <= jnp.arange(M)[:, None]
    expo = g[:, :, :, :, None, :] - b[:, :, :, None, :, :]
    k_sub = k[:, :, :, None] * jnp.exp(jnp.where(reach[:, :, None], expo, -jnp.inf))
    scores = jnp.einsum('bhnitk,bhnisk->bhnits', q_sub, k_sub).reshape(bsz, H, N, C, C)
    scores = jnp.where(jnp.tril(jnp.ones((C, C), bool)), scores, 0.0)
    o_intra = jnp.einsum('bhnts,bhnsv->bhntv', scores, v)
    o = o_inter + o_intra
    return o.transpose(0, 2, 3, 1, 4).reshape(bsz, T, H, V)


def even_mixer(h, w_in, lb, a_onorm, b_ln_w, b_ln_b, b_ws, b_bias, w_out):
    bsz, T, _ = h.shape
    f32 = jnp.float32
    z = h @ w_in
    qa, fa, ia, ga, ub, vb, gb = split_sizes(
        z, [A_QK, A_QK, A_WIDTH, A_WIDTH, B_WIDTH, B_WIDTH, B_WIDTH])

    zf = fa.astype(f32).reshape(bsz, T, A_HEADS, A_DK)
    lb = lb.reshape(A_HEADS, A_DK)
    log_f = jnp.log(lb + (1.0 - lb) * jax.nn.sigmoid(zf))
    k = (1.0 - lb) * jax.nn.sigmoid(-zf)
    q = qa.astype(f32).reshape(bsz, T, A_HEADS, A_DK)
    v = ia.astype(f32).reshape(bsz, T, A_HEADS, A_DV)
    o = hgrn2_chunked(q, k, v, log_f)
    o = rms_norm(o, a_onorm).reshape(bsz, T, A_WIDTH).astype(h.dtype)
    out_a = o * jax.nn.silu(ga)

    vg = layer_norm(vb.reshape(bsz, T, B_GROUPS, B_GROUP_DIM),
                    b_ln_w.reshape(B_GROUPS, B_GROUP_DIM), b_ln_b.reshape(B_GROUPS, B_GROUP_DIM))
    nc = T // B_CHUNK
    vg = vg.reshape(bsz, nc, B_CHUNK, B_GROUPS, B_GROUP_DIM)
    ws_causal = jnp.where(jnp.tril(jnp.ones((B_CHUNK, B_CHUNK), bool)), b_ws, 0.0)
    sv = jnp.einsum('gts,bnsgd->bntgd', ws_causal, vg) + b_bias.T[:, :, None]
    sv = sv.reshape(bsz, T, B_WIDTH)
    out_b = ub * sv * jax.nn.silu(gb)

    return jnp.concatenate([out_a, out_b], axis=-1) @ w_out


def rope_tables(positions):
    inv_freq = ROPE_THETA ** (-jnp.arange(0, C_ROPE, 2, dtype=jnp.float32) / C_ROPE)
    ang = positions.astype(jnp.float32)[..., None] * inv_freq
    return jnp.cos(ang)[:, :, None, :], jnp.sin(ang)[:, :, None, :]


def apply_rope(x, cos, sin):
    x1, x2 = jnp.split(x, 2, axis=-1)
    cos = cos.astype(x.dtype)
    sin = sin.astype(x.dtype)
    return jnp.concatenate([x1 * cos - x2 * sin, x2 * cos + x1 * sin], axis=-1)


def causal_block_attention(q, k, v):
    bsz, T, H, dqk = q.shape
    nb = T // Q_BLOCK
    scale = dqk ** -0.5
    qb = jnp.moveaxis(q.reshape(bsz, nb, Q_BLOCK, H, dqk), 1, 0)
    kf = k.astype(jnp.float32)
    k_idx = jnp.arange(T)
    neg = jnp.finfo(jnp.float32).min

    def one_block(args):
        q_blk, blk = args
        s = jnp.einsum('bqhd,bkhd->bhqk', q_blk.astype(jnp.float32), kf) * scale
        q_idx = blk * Q_BLOCK + jnp.arange(Q_BLOCK)
        s = jnp.where(k_idx[None, :] <= q_idx[:, None], s, neg)
        p = jax.nn.softmax(s, axis=-1)
        return jnp.einsum('bhqk,bkhv->bqhv', p.astype(v.dtype), v)

    out = lax.map(one_block, (qb, jnp.arange(nb)))
    return jnp.moveaxis(out, 0, 1).reshape(bsz, T, H, v.shape[-1])


def odd_mixer(h, cos, sin, w_in, q_norm, w_qb, kv_norm, w_kvb, w_out):
    bsz, T, _ = h.shape
    z = h @ w_in
    cq, ckv, kpe, gate = split_sizes(z, [C_Q_RANK, C_KV_RANK, C_ROPE, C_WIDTH])
    q = (rms_norm(cq, q_norm) @ w_qb).reshape(bsz, T, C_HEADS, C_QK)
    q_nope, q_pe = q[..., :C_NOPE], q[..., C_NOPE:]
    kv = (rms_norm(ckv, kv_norm) @ w_kvb).reshape(bsz, T, C_HEADS, C_NOPE + C_V)
    k_nope, v = kv[..., :C_NOPE], kv[..., C_NOPE:]
    q_pe = apply_rope(q_pe, cos, sin)
    k_pe = apply_rope(kpe[:, :, None, :], cos, sin)
    q = jnp.concatenate([q_nope, q_pe], axis=-1)
    k = jnp.concatenate([k_nope, jnp.broadcast_to(k_pe, (bsz, T, C_HEADS, C_ROPE))], axis=-1)
    o = causal_block_attention(q, k, v).reshape(bsz, T, C_WIDTH)
    return (o * jax.nn.silu(gate)) @ w_out


def _fwd_setup_inputs(seed: int = 0) -> dict:
    key = jax.random.key(seed)
    ks = jax.random.split(key, 18)
    f32 = jnp.float32

    def normal(k, shape, scale):
        return scale * jax.random.normal(k, shape, f32)

    x = normal(ks[0], (BATCH, SEQ, D_MODEL), 1.0)
    positions = (jax.random.randint(ks[1], (BATCH, 1), 0, MAX_POS_OFFSET)
                 + jnp.arange(SEQ)[None, :]).astype(jnp.int32)
    norm_pre = 1.0 + normal(ks[2], (DEPTH, D_MODEL), 0.05)
    norm_post = 1.0 + normal(ks[3], (DEPTH, D_MODEL), 0.05)
    ev_w_in = normal(ks[4], (N_EVEN, D_MODEL, EVEN_IN), D_MODEL ** -0.5)
    ev_lb_logits = normal(ks[5], (N_EVEN + 1, A_QK), 0.1)
    ev_a_onorm = 1.0 + normal(ks[6], (N_EVEN, A_DV), 0.05)
    ev_b_ln_w = 1.0 + normal(ks[7], (N_EVEN, B_WIDTH), 0.05)
    ev_b_ln_b = normal(ks[8], (N_EVEN, B_WIDTH), 0.02)
    ev_b_ws = normal(ks[9], (N_EVEN, B_GROUPS, B_CHUNK, B_CHUNK), B_CHUNK ** -0.5)
    ev_b_bias = 1.0 + normal(ks[10], (N_EVEN, B_GROUPS, B_CHUNK), 0.02)
    ev_w_out = normal(ks[11], (N_EVEN, A_WIDTH + B_WIDTH, D_MODEL), (A_WIDTH + B_WIDTH) ** -0.5)
    od_w_in = normal(ks[12], (N_ODD, D_MODEL, ODD_IN), D_MODEL ** -0.5)
    od_q_norm = 1.0 + normal(ks[13], (N_ODD, C_Q_RANK), 0.05)
    od_w_qb = normal(ks[14], (N_ODD, C_Q_RANK, C_HEADS * C_QK), C_Q_RANK ** -0.5)
    od_kv_norm = 1.0 + normal(ks[15], (N_ODD, C_KV_RANK), 0.05)
    od_w_kvb = normal(ks[16], (N_ODD, C_KV_RANK, C_HEADS * (C_NOPE + C_V)), C_KV_RANK ** -0.5)
    od_w_out = normal(ks[17], (N_ODD, C_WIDTH, D_MODEL), C_WIDTH ** -0.5)
    return {'x': x, 'positions': positions, 'norm_pre': norm_pre, 'norm_post': norm_post,
            'ev_w_in': ev_w_in, 'ev_lb_logits': ev_lb_logits, 'ev_a_onorm': ev_a_onorm,
            'ev_b_ln_w': ev_b_ln_w, 'ev_b_ln_b': ev_b_ln_b, 'ev_b_ws': ev_b_ws,
            'ev_b_bias': ev_b_bias, 'ev_w_out': ev_w_out, 'od_w_in': od_w_in,
            'od_q_norm': od_q_norm, 'od_w_qb': od_w_qb, 'od_kv_norm': od_kv_norm,
            'od_w_kvb': od_w_kvb, 'od_w_out': od_w_out}


def _fwd_reference(x, positions, norm_pre, norm_post, ev_w_in, ev_lb_logits, ev_a_onorm,
              ev_b_ln_w, ev_b_ln_b, ev_b_ws, ev_b_bias, ev_w_out, od_w_in, od_q_norm,
              od_w_qb, od_kv_norm, od_w_kvb, od_w_out):
    lower_bounds = jnp.cumsum(jax.nn.softmax(ev_lb_logits.astype(jnp.float32), axis=0), axis=0)[:N_EVEN]
    cos, sin = rope_tables(positions)
    for layer in range(DEPTH):
        j = layer // 2
        h = rms_norm(x, norm_pre[layer])
        if layer % 2 == 0:
            y = even_mixer(h, ev_w_in[j], lower_bounds[j], ev_a_onorm[j], ev_b_ln_w[j],
                           ev_b_ln_b[j], ev_b_ws[j], ev_b_bias[j], ev_w_out[j])
        else:
            y = odd_mixer(h, cos, sin, od_w_in[j], od_q_norm[j], od_w_qb[j], od_kv_norm[j],
                          od_w_kvb[j], od_w_out[j])
        x = x + rms_norm(y, norm_post[layer])
    return x


import jax as _jax
import jax.numpy as _jnp

TWIN_FORMAT = 'train_step'
FWD_PARAMS = ['x', 'positions', 'norm_pre', 'norm_post', 'ev_w_in', 'ev_lb_logits', 'ev_a_onorm', 'ev_b_ln_w', 'ev_b_ln_b', 'ev_b_ws', 'ev_b_bias', 'ev_w_out', 'od_w_in', 'od_q_norm', 'od_w_qb', 'od_kv_norm', 'od_w_kvb', 'od_w_out']
TWIN_WEIGHTS = ['norm_pre', 'norm_post', 'ev_w_in', 'ev_lb_logits', 'ev_a_onorm', 'ev_b_ln_w', 'ev_b_ln_b', 'ev_b_ws', 'ev_b_bias', 'ev_w_out', 'od_w_in', 'od_q_norm', 'od_w_qb', 'od_kv_norm', 'od_w_kvb', 'od_w_out']
TWIN_DIFF_INPUT = 'x'
TWIN_INPUTS = ['x', 'positions', 'norm_pre', 'norm_post', 'ev_w_in', 'ev_lb_logits', 'ev_a_onorm', 'ev_b_ln_w', 'ev_b_ln_b', 'ev_b_ws', 'ev_b_bias', 'ev_w_out', 'od_w_in', 'od_q_norm', 'od_w_qb', 'od_kv_norm', 'od_w_kvb', 'od_w_out', 'loss_target', 'm_norm_pre', 'm_norm_post', 'm_ev_w_in', 'm_ev_lb_logits', 'm_ev_a_onorm', 'm_ev_b_ln_w', 'm_ev_b_ln_b', 'm_ev_b_ws', 'm_ev_b_bias', 'm_ev_w_out', 'm_od_w_in', 'm_od_q_norm', 'm_od_w_qb', 'm_od_kv_norm', 'm_od_w_kvb', 'm_od_w_out', 'v_norm_pre', 'v_norm_post', 'v_ev_w_in', 'v_ev_lb_logits', 'v_ev_a_onorm', 'v_ev_b_ln_w', 'v_ev_b_ln_b', 'v_ev_b_ws', 'v_ev_b_bias', 'v_ev_w_out', 'v_od_w_in', 'v_od_q_norm', 'v_od_w_qb', 'v_od_kv_norm', 'v_od_w_kvb', 'v_od_w_out']
TWIN_OUTPUTS = ['loss', 'grad_x', 'grad_norm_pre', 'grad_norm_post', 'grad_ev_w_in', 'grad_ev_lb_logits', 'grad_ev_a_onorm', 'grad_ev_b_ln_w', 'grad_ev_b_ln_b', 'grad_ev_b_ws', 'grad_ev_b_bias', 'grad_ev_w_out', 'grad_od_w_in', 'grad_od_q_norm', 'grad_od_w_qb', 'grad_od_kv_norm', 'grad_od_w_kvb', 'grad_od_w_out', 'delta_norm_pre', 'delta_norm_post', 'delta_ev_w_in', 'delta_ev_lb_logits', 'delta_ev_a_onorm', 'delta_ev_b_ln_w', 'delta_ev_b_ln_b', 'delta_ev_b_ws', 'delta_ev_b_bias', 'delta_ev_w_out', 'delta_od_w_in', 'delta_od_q_norm', 'delta_od_w_qb', 'delta_od_kv_norm', 'delta_od_w_kvb', 'delta_od_w_out', 'new_m_norm_pre', 'new_m_norm_post', 'new_m_ev_w_in', 'new_m_ev_lb_logits', 'new_m_ev_a_onorm', 'new_m_ev_b_ln_w', 'new_m_ev_b_ln_b', 'new_m_ev_b_ws', 'new_m_ev_b_bias', 'new_m_ev_w_out', 'new_m_od_w_in', 'new_m_od_q_norm', 'new_m_od_w_qb', 'new_m_od_kv_norm', 'new_m_od_w_kvb', 'new_m_od_w_out', 'new_v_norm_pre', 'new_v_norm_post', 'new_v_ev_w_in', 'new_v_ev_lb_logits', 'new_v_ev_a_onorm', 'new_v_ev_b_ln_w', 'new_v_ev_b_ln_b', 'new_v_ev_b_ws', 'new_v_ev_b_bias', 'new_v_ev_w_out', 'new_v_od_w_in', 'new_v_od_q_norm', 'new_v_od_w_qb', 'new_v_od_kv_norm', 'new_v_od_w_kvb', 'new_v_od_w_out']
TWIN_LEAF_KINDS = {'loss': 'loss', 'grad_x': 'grad_x', 'grad_norm_pre': 'grad_w', 'grad_norm_post': 'grad_w', 'grad_ev_w_in': 'grad_w', 'grad_ev_lb_logits': 'grad_w', 'grad_ev_a_onorm': 'grad_w', 'grad_ev_b_ln_w': 'grad_w', 'grad_ev_b_ln_b': 'grad_w', 'grad_ev_b_ws': 'grad_w', 'grad_ev_b_bias': 'grad_w', 'grad_ev_w_out': 'grad_w', 'grad_od_w_in': 'grad_w', 'grad_od_q_norm': 'grad_w', 'grad_od_w_qb': 'grad_w', 'grad_od_kv_norm': 'grad_w', 'grad_od_w_kvb': 'grad_w', 'grad_od_w_out': 'grad_w', 'delta_norm_pre': 'delta_w', 'delta_norm_post': 'delta_w', 'delta_ev_w_in': 'delta_w', 'delta_ev_lb_logits': 'delta_w', 'delta_ev_a_onorm': 'delta_w', 'delta_ev_b_ln_w': 'delta_w', 'delta_ev_b_ln_b': 'delta_w', 'delta_ev_b_ws': 'delta_w', 'delta_ev_b_bias': 'delta_w', 'delta_ev_w_out': 'delta_w', 'delta_od_w_in': 'delta_w', 'delta_od_q_norm': 'delta_w', 'delta_od_w_qb': 'delta_w', 'delta_od_kv_norm': 'delta_w', 'delta_od_w_kvb': 'delta_w', 'delta_od_w_out': 'delta_w', 'new_m_norm_pre': 'new_m', 'new_m_norm_post': 'new_m', 'new_m_ev_w_in': 'new_m', 'new_m_ev_lb_logits': 'new_m', 'new_m_ev_a_onorm': 'new_m', 'new_m_ev_b_ln_w': 'new_m', 'new_m_ev_b_ln_b': 'new_m', 'new_m_ev_b_ws': 'new_m', 'new_m_ev_b_bias': 'new_m', 'new_m_ev_w_out': 'new_m', 'new_m_od_w_in': 'new_m', 'new_m_od_q_norm': 'new_m', 'new_m_od_w_qb': 'new_m', 'new_m_od_kv_norm': 'new_m', 'new_m_od_w_kvb': 'new_m', 'new_m_od_w_out': 'new_m', 'new_v_norm_pre': 'new_v', 'new_v_norm_post': 'new_v', 'new_v_ev_w_in': 'new_v', 'new_v_ev_lb_logits': 'new_v', 'new_v_ev_a_onorm': 'new_v', 'new_v_ev_b_ln_w': 'new_v', 'new_v_ev_b_ln_b': 'new_v', 'new_v_ev_b_ws': 'new_v', 'new_v_ev_b_bias': 'new_v', 'new_v_ev_w_out': 'new_v', 'new_v_od_w_in': 'new_v', 'new_v_od_q_norm': 'new_v', 'new_v_od_w_qb': 'new_v', 'new_v_od_kv_norm': 'new_v', 'new_v_od_w_kvb': 'new_v', 'new_v_od_w_out': 'new_v'}


def _forward(args):
    return _fwd_reference(*[args[k] for k in FWD_PARAMS])


def _output_shape():
    out = _jax.eval_shape(lambda: _forward(_fwd_setup_inputs(0)))
    return out.shape, out.dtype

N_MICROBATCH = 1
ADAM_LR = 0.001
ADAM_B1 = 0.9
ADAM_B2 = 0.999
ADAM_EPS = 1e-08
ADAM_WD = 0.01
ADAM_STEP = 10
PER_EXAMPLE_BATCH_AXIS = {'x': 0, 'positions': 0, 'loss_target': 0}
SHARED_INPUTS = []
_WEIGHT_DTYPES = {'norm_pre': _jnp.float32, 'norm_post': _jnp.float32, 'ev_w_in': _jnp.float32, 'ev_lb_logits': _jnp.float32, 'ev_a_onorm': _jnp.float32, 'ev_b_ln_w': _jnp.float32, 'ev_b_ln_b': _jnp.float32, 'ev_b_ws': _jnp.float32, 'ev_b_bias': _jnp.float32, 'ev_w_out': _jnp.float32, 'od_w_in': _jnp.float32, 'od_q_norm': _jnp.float32, 'od_w_qb': _jnp.float32, 'od_kv_norm': _jnp.float32, 'od_w_kvb': _jnp.float32, 'od_w_out': _jnp.float32}
MOMENT_SCALE = {'norm_pre': 1.107641e+00, 'norm_post': 7.994160e+00, 'ev_w_in': 7.188488e-01, 'ev_lb_logits': 4.256118e-01, 'ev_a_onorm': 4.908677e-01, 'ev_b_ln_w': 1.109863e-01, 'ev_b_ln_b': 1.160742e-01, 'ev_b_ws': 1.107697e-01, 'ev_b_bias': 1.558404e-01, 'ev_w_out': 1.878356e-01, 'od_w_in': 1.749091e-01, 'od_q_norm': 2.199976e-01, 'od_w_qb': 9.532839e-02, 'od_kv_norm': 4.142420e-01, 'od_w_kvb': 1.121219e-01, 'od_w_out': 1.201254e-01}


def _to_microbatches(a, axis):
    t = _jnp.moveaxis(a, axis, 0)
    t = t.reshape((N_MICROBATCH, t.shape[0] // N_MICROBATCH) + t.shape[1:])
    return _jnp.moveaxis(t, 1, axis + 1)


def setup_inputs(seed: int = 0) -> dict:
    inp = _fwd_setup_inputs(seed)
    key = _jax.random.fold_in(_jax.random.key(seed), 7919)
    shape, _ = _output_shape()
    out = dict(inp)
    out["loss_target"] = _jax.random.normal(_jax.random.fold_in(key, 0), shape, _jnp.float32)
    for i, name in enumerate(TWIN_WEIGHTS):
        w = inp[name].astype(_jnp.float32)
        if MOMENT_SCALE is None:
            s = _jnp.sqrt(_jnp.mean(_jnp.square(w)) + 1e-30)
        else:
            s = MOMENT_SCALE[name]
        km, kv = _jax.random.split(_jax.random.fold_in(key, i + 1))
        out[name] = w
        out["m_" + name] = s * _jax.random.normal(km, w.shape, _jnp.float32)
        out["v_" + name] = (s * s) * _jax.random.uniform(kv, w.shape, _jnp.float32, 0.5, 1.5)
    if N_MICROBATCH > 1:
        for name, axis in PER_EXAMPLE_BATCH_AXIS.items():
            out[name] = _to_microbatches(out[name], axis)
    return {'x': out['x'], 'positions': out['positions'], 'norm_pre': out['norm_pre'], 'norm_post': out['norm_post'], 'ev_w_in': out['ev_w_in'], 'ev_lb_logits': out['ev_lb_logits'], 'ev_a_onorm': out['ev_a_onorm'], 'ev_b_ln_w': out['ev_b_ln_w'], 'ev_b_ln_b': out['ev_b_ln_b'], 'ev_b_ws': out['ev_b_ws'], 'ev_b_bias': out['ev_b_bias'], 'ev_w_out': out['ev_w_out'], 'od_w_in': out['od_w_in'], 'od_q_norm': out['od_q_norm'], 'od_w_qb': out['od_w_qb'], 'od_kv_norm': out['od_kv_norm'], 'od_w_kvb': out['od_w_kvb'], 'od_w_out': out['od_w_out'], 'loss_target': out['loss_target'], 'm_norm_pre': out['m_norm_pre'], 'm_norm_post': out['m_norm_post'], 'm_ev_w_in': out['m_ev_w_in'], 'm_ev_lb_logits': out['m_ev_lb_logits'], 'm_ev_a_onorm': out['m_ev_a_onorm'], 'm_ev_b_ln_w': out['m_ev_b_ln_w'], 'm_ev_b_ln_b': out['m_ev_b_ln_b'], 'm_ev_b_ws': out['m_ev_b_ws'], 'm_ev_b_bias': out['m_ev_b_bias'], 'm_ev_w_out': out['m_ev_w_out'], 'm_od_w_in': out['m_od_w_in'], 'm_od_q_norm': out['m_od_q_norm'], 'm_od_w_qb': out['m_od_w_qb'], 'm_od_kv_norm': out['m_od_kv_norm'], 'm_od_w_kvb': out['m_od_w_kvb'], 'm_od_w_out': out['m_od_w_out'], 'v_norm_pre': out['v_norm_pre'], 'v_norm_post': out['v_norm_post'], 'v_ev_w_in': out['v_ev_w_in'], 'v_ev_lb_logits': out['v_ev_lb_logits'], 'v_ev_a_onorm': out['v_ev_a_onorm'], 'v_ev_b_ln_w': out['v_ev_b_ln_w'], 'v_ev_b_ln_b': out['v_ev_b_ln_b'], 'v_ev_b_ws': out['v_ev_b_ws'], 'v_ev_b_bias': out['v_ev_b_bias'], 'v_ev_w_out': out['v_ev_w_out'], 'v_od_w_in': out['v_od_w_in'], 'v_od_q_norm': out['v_od_q_norm'], 'v_od_w_qb': out['v_od_w_qb'], 'v_od_kv_norm': out['v_od_kv_norm'], 'v_od_w_kvb': out['v_od_w_kvb'], 'v_od_w_out': out['v_od_w_out']}


def _loss(weights, diff, rest, loss_target):
    with _jax.named_scope("forward"):
        args = {**rest, TWIN_DIFF_INPUT: diff, **{k: w.astype(_WEIGHT_DTYPES[k]) for k, w in weights.items()}}
        y = _forward(args)
    with _jax.named_scope("loss_head"):
        err = _jnp.square(y.astype(_jnp.float32) - loss_target)
        return 0.5 * _jnp.sum(_jnp.mean(err, axis=-1)) if err.ndim else 0.5 * err


def _adamw(w, g, m, v):
    m = ADAM_B1 * m + (1.0 - ADAM_B1) * g
    v = ADAM_B2 * v + (1.0 - ADAM_B2) * _jnp.square(g)
    m_hat = m / (1.0 - ADAM_B1 ** ADAM_STEP)
    v_hat = v / (1.0 - ADAM_B2 ** ADAM_STEP)
    delta = -ADAM_LR * (m_hat / (_jnp.sqrt(v_hat) + ADAM_EPS) + ADAM_WD * w)
    return delta, m, v


def reference(x, positions, norm_pre, norm_post, ev_w_in, ev_lb_logits, ev_a_onorm, ev_b_ln_w, ev_b_ln_b, ev_b_ws, ev_b_bias, ev_w_out, od_w_in, od_q_norm, od_w_qb, od_kv_norm, od_w_kvb, od_w_out, loss_target, m_norm_pre, m_norm_post, m_ev_w_in, m_ev_lb_logits, m_ev_a_onorm, m_ev_b_ln_w, m_ev_b_ln_b, m_ev_b_ws, m_ev_b_bias, m_ev_w_out, m_od_w_in, m_od_q_norm, m_od_w_qb, m_od_kv_norm, m_od_w_kvb, m_od_w_out, v_norm_pre, v_norm_post, v_ev_w_in, v_ev_lb_logits, v_ev_a_onorm, v_ev_b_ln_w, v_ev_b_ln_b, v_ev_b_ws, v_ev_b_bias, v_ev_w_out, v_od_w_in, v_od_q_norm, v_od_w_qb, v_od_kv_norm, v_od_w_kvb, v_od_w_out):
    given = dict(x=x, positions=positions, norm_pre=norm_pre, norm_post=norm_post, ev_w_in=ev_w_in, ev_lb_logits=ev_lb_logits, ev_a_onorm=ev_a_onorm, ev_b_ln_w=ev_b_ln_w, ev_b_ln_b=ev_b_ln_b, ev_b_ws=ev_b_ws, ev_b_bias=ev_b_bias, ev_w_out=ev_w_out, od_w_in=od_w_in, od_q_norm=od_q_norm, od_w_qb=od_w_qb, od_kv_norm=od_kv_norm, od_w_kvb=od_w_kvb, od_w_out=od_w_out, loss_target=loss_target, m_norm_pre=m_norm_pre, m_norm_post=m_norm_post, m_ev_w_in=m_ev_w_in, m_ev_lb_logits=m_ev_lb_logits, m_ev_a_onorm=m_ev_a_onorm, m_ev_b_ln_w=m_ev_b_ln_w, m_ev_b_ln_b=m_ev_b_ln_b, m_ev_b_ws=m_ev_b_ws, m_ev_b_bias=m_ev_b_bias, m_ev_w_out=m_ev_w_out, m_od_w_in=m_od_w_in, m_od_q_norm=m_od_q_norm, m_od_w_qb=m_od_w_qb, m_od_kv_norm=m_od_kv_norm, m_od_w_kvb=m_od_w_kvb, m_od_w_out=m_od_w_out, v_norm_pre=v_norm_pre, v_norm_post=v_norm_post, v_ev_w_in=v_ev_w_in, v_ev_lb_logits=v_ev_lb_logits, v_ev_a_onorm=v_ev_a_onorm, v_ev_b_ln_w=v_ev_b_ln_w, v_ev_b_ln_b=v_ev_b_ln_b, v_ev_b_ws=v_ev_b_ws, v_ev_b_bias=v_ev_b_bias, v_ev_w_out=v_ev_w_out, v_od_w_in=v_od_w_in, v_od_q_norm=v_od_q_norm, v_od_w_qb=v_od_w_qb, v_od_kv_norm=v_od_kv_norm, v_od_w_kvb=v_od_w_kvb, v_od_w_out=v_od_w_out)
    weights = {n: given[n] for n in TWIN_WEIGHTS}
    shared = {n: given[n] for n in SHARED_INPUTS}
    per_example = {n: given[n] for n in ['x', 'positions']}
    grad_fn = _jax.value_and_grad(_loss, argnums=(0, 1))

    def one_microbatch(ex, loss_target):
        ex = dict(ex)
        diff = ex.pop(TWIN_DIFF_INPUT)
        return grad_fn(weights, diff, {**shared, **ex}, loss_target)

    if N_MICROBATCH == 1:
        loss, (grad_w, grad_x) = one_microbatch(per_example, given["loss_target"])
    else:
        def body(carry, xs):
            loss_sum, grad_sum = carry
            l_k, (gw_k, gx_k) = one_microbatch(xs[0], xs[1])
            with _jax.named_scope("update"):
                return (loss_sum + l_k, _jax.tree.map(_jnp.add, grad_sum, gw_k)), gx_k

        init = (_jnp.zeros((), _jnp.float32), _jax.tree.map(_jnp.zeros_like, weights))
        (loss, grad_w), grad_x = _jax.lax.scan(body, init, (per_example, given["loss_target"]))
    with _jax.named_scope("update"):
        delta_w, new_m, new_v = {}, {}, {}
        for n in TWIN_WEIGHTS:
            delta_w[n], new_m[n], new_v[n] = _adamw(weights[n], grad_w[n], given["m_" + n], given["v_" + n])
    return (loss, grad_x, *[grad_w[n] for n in TWIN_WEIGHTS], *[delta_w[n] for n in TWIN_WEIGHTS],
            *[new_m[n] for n in TWIN_WEIGHTS], *[new_v[n] for n in TWIN_WEIGHTS])
```

```python
import functools

import jax
import jax.numpy as jnp
from jax import lax
from jax.experimental import pallas as pl
from jax.experimental.pallas import tpu as pltpu

F32 = jnp.float32
BF16 = jnp.bfloat16

N_DEV = 8
T = 2048
D = 2048
EPS = 1e-6
A_HEADS = 8
HD = 128
A_CHUNK = 64
A_SUB = 16
B_GROUPS = 8
B_CHUNK = 128
EVEN_IN = 7168
C_HEADS = 16
C_RANK = 512
C_NOPE = 128
C_ROPE = 64
C_QK = C_NOPE + C_ROPE
C_V = 128
ODD_IN = 3136
ODD_IN_PAD = 3200
QP = 256
ROPE_THETA = 10000.0
ATT_SCALE = C_QK ** -0.5

ADAM_LR = 0.001
ADAM_B1 = 0.9
ADAM_B2 = 0.999
ADAM_EPS = 1e-08
ADAM_WD = 0.01
ADAM_STEP = 10

VMEM_LIMIT_V7X = 56 * 1024 * 1024
MESH_ID = pl.DeviceIdType.MESH


def _params(n_grid):
    return pltpu.CompilerParams(dimension_semantics=("arbitrary",) * n_grid,
                                vmem_limit_bytes=VMEM_LIMIT_V7X)


def _dg(a, b, ca, cb):
    return lax.dot_general(a.astype(BF16), b.astype(BF16), (((ca,), (cb,)), ((), ())),
                           preferred_element_type=F32)


def _raw_nn(a, b):
    return _dg(a, b, 1, 0)


def _raw_nt(a, b):
    return _dg(a, b, 1, 1)


def _raw_tn(a, b):
    return _dg(a, b, 0, 0)


@jax.custom_vjp
def _dot_nn(a, b):
    return _raw_nn(a, b)


def _dot_nn_fwd(a, b):
    return _raw_nn(a, b), (a.astype(BF16), b.astype(BF16))


def _dot_nn_bwd(res, g):
    a, b = res
    return _raw_nt(g, b), _raw_tn(a, g)


_dot_nn.defvjp(_dot_nn_fwd, _dot_nn_bwd)


@jax.custom_vjp
def _dot_nt(a, b):
    return _raw_nt(a, b)


def _dot_nt_fwd(a, b):
    return _raw_nt(a, b), (a.astype(BF16), b.astype(BF16))


def _dot_nt_bwd(res, g):
    a, b = res
    return _raw_nn(g, b), _raw_tn(g, a)


_dot_nt.defvjp(_dot_nt_fwd, _dot_nt_bwd)


@jax.custom_vjp
def _dot_tn(a, b):
    return _raw_tn(a, b)


def _dot_tn_fwd(a, b):
    return _raw_tn(a, b), (a.astype(BF16), b.astype(BF16))


def _dot_tn_bwd(res, g):
    a, b = res
    return _raw_nt(b, g), _raw_nn(a, g)


_dot_tn.defvjp(_dot_tn_fwd, _dot_tn_bwd)


@jax.custom_vjp
def _sigmoid(x):
    e = jnp.exp(-jnp.abs(x))
    return jnp.where(x >= 0, 1.0 / (1.0 + e), e / (1.0 + e))


def _sigmoid_fwd(x):
    s = _sigmoid(x)
    return s, s


def _sigmoid_bwd(s, g):
    return (g * s * (1.0 - s),)


_sigmoid.defvjp(_sigmoid_fwd, _sigmoid_bwd)


def _silu(x):
    return x * _sigmoid(x)


def _rms(x, w):
    return x * lax.rsqrt(jnp.mean(x * x, axis=-1, keepdims=True) + EPS) * w


def _split3(x):
    hi = x.astype(BF16)
    r = x - hi.astype(F32)
    mid = r.astype(BF16)
    lo = (r - mid.astype(F32)).astype(BF16)
    return hi, mid, lo


def _mask_apply(mask_bf16, x, contract):
    out = None
    for piece in _split3(x):
        d = lax.dot_general(mask_bf16, piece, (((contract,), (0,)), ((), ())),
                            preferred_element_type=F32)
        out = d if out is None else out + d
    return out


def _chunk_tri(rows):
    r = lax.broadcasted_iota(jnp.int32, (rows, rows), 0)
    c = lax.broadcasted_iota(jnp.int32, (rows, rows), 1)
    return ((r >= c) & (r // A_CHUNK == c // A_CHUNK)).astype(BF16)


@jax.custom_vjp
def _chunk_cumsum(x):
    return _mask_apply(_chunk_tri(x.shape[0]), x, 1)


def _chunk_cumsum_fwd(x):
    return _chunk_cumsum(x), None


def _chunk_cumsum_bwd(_, g):
    return (_mask_apply(_chunk_tri(g.shape[0]), g, 0),)


_chunk_cumsum.defvjp(_chunk_cumsum_fwd, _chunk_cumsum_bwd)


def _hgrn2_rows(q, zf, v, ga, st, l0, l1, onorm):
    rows = q.shape[0]
    n_sub = A_CHUNK // A_SUB
    mx = jnp.maximum(l0, l1)
    e0 = jnp.exp(l0 - mx)
    e1 = jnp.exp(l1 - mx)
    lb = e0 / (e0 + e1)
    lf = jnp.log(lb + (1.0 - lb) * _sigmoid(zf))
    k = (1.0 - lb) * _sigmoid(-zf)
    b = _chunk_cumsum(lf)

    t_idx = lax.broadcasted_iota(jnp.int32, (A_CHUNK, n_sub * A_CHUNK), 0)
    c_idx = lax.broadcasted_iota(jnp.int32, (A_CHUNK, n_sub * A_CHUNK), 1)
    sel = (c_idx // A_CHUNK == t_idx // A_SUB) & (c_idx % A_CHUNK <= t_idx)
    key_row = lax.broadcasted_iota(jnp.int32, (A_CHUNK, HD), 0)

    outs = []
    for n in range(rows // A_CHUNK):
        lo = n * A_CHUNK
        qc, kc, vc = q[lo:lo + A_CHUNK], k[lo:lo + A_CHUNK], v[lo:lo + A_CHUNK]
        lfc, bc = lf[lo:lo + A_CHUNK], b[lo:lo + A_CHUNK]
        b_last = bc[A_CHUNK - 1:A_CHUNK]
        o_inter = _dot_nt(qc * jnp.exp(bc), st)
        kv_t = _dot_tn(vc, kc * jnp.exp(b_last - bc))
        st = st * jnp.exp(b_last) + kv_t
        g_rows, k_subs = [], []
        for i in range(n_sub):
            g_i = bc[i * A_SUB:i * A_SUB + 1] - lfc[i * A_SUB:i * A_SUB + 1]
            g_rows.append(jnp.broadcast_to(g_i, (A_SUB, HD)))
            expo = jnp.where(key_row < (i + 1) * A_SUB, g_i - bc, -jnp.inf)
            k_subs.append(kc * jnp.exp(expo))
        q_sub = qc * jnp.exp(bc - jnp.concatenate(g_rows, axis=0))
        scores = _dot_nt(q_sub, jnp.concatenate(k_subs, axis=0))
        scores = jnp.where(sel, scores, 0.0)
        o_intra = _dot_nn(scores, jnp.concatenate([vc] * n_sub, axis=0))
        outs.append(o_inter + o_intra)
    o = jnp.concatenate(outs, axis=0)
    return _rms(o, onorm) * _silu(ga), st


def _gmlp_rows(u, vb, gb, lnw, lnb, ws, bias):
    rows = u.shape[0]
    mu = jnp.mean(vb, axis=-1, keepdims=True)
    xc = vb - mu
    vg = xc * lax.rsqrt(jnp.mean(xc * xc, axis=-1, keepdims=True) + EPS) * lnw + lnb
    r = lax.broadcasted_iota(jnp.int32, (B_CHUNK, B_CHUNK), 0)
    c = lax.broadcasted_iota(jnp.int32, (B_CHUNK, B_CHUNK), 1)
    ws_causal = jnp.where(r >= c, ws, 0.0)
    svs = [_dot_nn(ws_causal, vg[n * B_CHUNK:(n + 1) * B_CHUNK]) + bias
           for n in range(rows // B_CHUNK)]
    return u * jnp.concatenate(svs, axis=0) * _silu(gb)


def _rope(x, cos_t, sin_t):
    return x * cos_t + pltpu.roll(x, 64, 1) * sin_t


def _rope_transpose(g, cos_t, sin_t):
    return g * cos_t + pltpu.roll(g * sin_t, 64, 1)


def _pure_call(name, fn, grid, in_specs, out_specs, out_shape, args, n_acc=0):
    n_in, n_out = len(in_specs), len(out_specs)

    def body(*refs):
        res = fn(*[r[...] for r in refs[:n_in]])
        if not isinstance(res, (tuple, list)):
            res = (res,)
        outs = refs[n_in:n_in + n_out]
        for o, r in zip(outs[:n_out - n_acc], res[:n_out - n_acc]):
            o[...] = r.astype(o.dtype)
        if n_acc:
            first = functools.reduce(jnp.logical_and, [pl.program_id(i) == 0 for i in range(len(grid))])
            for o, r in zip(outs[n_out - n_acc:], res[n_out - n_acc:]):
                @pl.when(first)
                def _(o=o, r=r):
                    o[...] = r.astype(o.dtype)

                @pl.when(jnp.logical_not(first))
                def _(o=o, r=r):
                    o[...] += r.astype(o.dtype)

    return pl.pallas_call(body, name=name, grid=grid, in_specs=in_specs, out_specs=out_specs,
                          out_shape=out_shape, compiler_params=_params(len(grid)))(*args)


def _sds(shape, dtype):
    return jax.ShapeDtypeStruct(shape, dtype)


def _row_spec(tm, width, col=0):
    return pl.BlockSpec((tm, width), lambda i, col=col: (i, col))


def _full_spec(shape):
    nd = len(shape)
    return pl.BlockSpec(shape, lambda *_: (0,) * nd)


def _mm_nn(name, a, b, out_dtype, tm, tn):
    m, k = a.shape
    j, _, n = b.shape
    per = n // tn

    def body(a_ref, b_ref, o_ref):
        o_ref[...] = _raw_nn(a_ref[...], b_ref[...]).astype(o_ref.dtype)

    return pl.pallas_call(
        body, name=name, grid=(m // tm, j * per),
        in_specs=[pl.BlockSpec((tm, k), lambda i, c: (i, 0)),
                  pl.BlockSpec((None, k, tn), lambda i, c: (c // per, 0, c % per))],
        out_specs=pl.BlockSpec((tm, tn), lambda i, c: (i, c)),
        out_shape=_sds((m, j * n), out_dtype), compiler_params=_params(2))(a, b)


def _mm_nt(name, a, b, out_dtype, tm, tn):
    m = a.shape[0]
    j, nn, n = b.shape

    def body(a_ref, b_ref, o_ref, acc_ref):
        part = _raw_nt(a_ref[...], b_ref[...])
        if j == 1:
            o_ref[...] = part.astype(o_ref.dtype)
        else:
            kk = pl.program_id(2)

            @pl.when(kk == 0)
            def _():
                acc_ref[...] = part

            @pl.when(kk > 0)
            def _():
                acc_ref[...] += part

            @pl.when(kk == j - 1)
            def _():
                o_ref[...] = acc_ref[...].astype(o_ref.dtype)

    acc_shape = (tm, tn) if j > 1 else (8, 128)
    return pl.pallas_call(
        body, name=name, grid=(m // tm, nn // tn, j),
        in_specs=[pl.BlockSpec((tm, n), lambda i, c, kk: (i, kk)),
                  pl.BlockSpec((None, tn, n), lambda i, c, kk: (kk, c, 0))],
        out_specs=pl.BlockSpec((tm, tn), lambda i, c, kk: (i, c)),
        out_shape=_sds((m, nn), out_dtype),
        scratch_shapes=[pltpu.VMEM(acc_shape, F32)], compiler_params=_params(3))(a, b)


def _mm_tn(name, a, b, j, out_dtype, tm, tn):
    k, m = a.shape
    n = b.shape[1] // j
    per = n // tn

    def body(a_ref, b_ref, o_ref):
        o_ref[...] = _raw_tn(a_ref[...], b_ref[...]).astype(o_ref.dtype)

    return pl.pallas_call(
        body, name=name, grid=(m // tm, j * per),
        in_specs=[pl.BlockSpec((k, tm), lambda i, c: (0, i)),
                  pl.BlockSpec((k, tn), lambda i, c: (0, c))],
        out_specs=pl.BlockSpec((None, tm, tn), lambda i, c: (c // per, i, c % per)),
        out_shape=_sds((j, m, n), out_dtype), compiler_params=_params(2))(a, b)


TM = 256


def _pre_norm(name, x, w_row):
    def fn(xv, w):
        return _rms(xv, w)
    return _pure_call(name, fn, (T // TM,), [_row_spec(TM, D), _full_spec((1, D))],
                      [_row_spec(TM, D)], [_sds((T, D), BF16)], (x, w_row))[0]


def _post_pre_norm(x, y, w_post, w_pre):
    def fn(xv, yv, wp, wn):
        x1 = xv + _rms(yv, wp)
        return x1, _rms(x1, wn)
    return _pure_call("post_pre_norm", fn, (T // TM,),
                      [_row_spec(TM, D), _row_spec(TM, D), _full_spec((1, D)), _full_spec((1, D))],
                      [_row_spec(TM, D), _row_spec(TM, D)],
                      [_sds((T, D), F32), _sds((T, D), BF16)], (x, y, w_post, w_pre))


def _post_pre_norm_bwd(y, x1, w_post, w_pre, dx1_in, dh1):
    def fn(yv, x1v, wp, wn, dx1v, dh1v):
        _, vjp_pre = jax.vjp(_rms, x1v, wn)
        dx1_h, dwn = vjp_pre(dh1v)
        dx1 = dx1v + dx1_h
        _, vjp_post = jax.vjp(_rms, yv, wp)
        dy, dwp = vjp_post(dx1)
        return dx1, dy, dwp, dwn
    return _pure_call("post_pre_norm_bwd", fn, (T // TM,),
                      [_row_spec(TM, D), _row_spec(TM, D), _full_spec((1, D)), _full_spec((1, D)),
                       _row_spec(TM, D), _row_spec(TM, D)],
                      [_row_spec(TM, D), _row_spec(TM, D), _full_spec((1, D)), _full_spec((1, D))],
                      [_sds((T, D), F32), _sds((T, D), BF16), _sds((1, D), F32), _sds((1, D), F32)],
                      (y, x1, w_post, w_pre, dx1_in, dh1), n_acc=2)


def _final_loss(x1, y, w_post, target):
    def fn(x1v, yv, wp, tv):
        r, vjp = jax.vjp(_rms, yv, wp)
        err = x1v + r - tv
        part = 0.5 * jnp.sum(jnp.mean(err * err, axis=-1, keepdims=True), axis=0, keepdims=True)
        dx2 = err * (1.0 / D)
        dy, dwp = vjp(dx2)
        return dx2, dy, jnp.broadcast_to(part, (1, 128)), dwp
    return _pure_call("final_loss", fn, (T // TM,),
                      [_row_spec(TM, D), _row_spec(TM, D), _full_spec((1, D)), _row_spec(TM, D)],
                      [_row_spec(TM, D), _row_spec(TM, D), _full_spec((1, 128)), _full_spec((1, D))],
                      [_sds((T, D), F32), _sds((T, D), BF16), _sds((1, 128), F32), _sds((1, D), F32)],
                      (x1, y, w_post, target), n_acc=2)


def _pre_norm_bwd(x, w_row, dh, dx_res):
    def fn(xv, w, dhv, dxv):
        _, vjp = jax.vjp(_rms, xv, w)
        dx, dw = vjp(dhv)
        return dxv + dx, dw
    return _pure_call("pre_norm_bwd", fn, (T // TM,),
                      [_row_spec(TM, D), _full_spec((1, D)), _row_spec(TM, D), _row_spec(TM, D)],
                      [_row_spec(TM, D), _full_spec((1, D))],
                      [_sds((T, D), F32), _sds((1, D), F32)], (x, w_row, dh, dx_res), n_acc=1)


RA = 256
RB = 512


def _col_spec(rows, col_of):
    return pl.BlockSpec((rows, HD), lambda h, r, col_of=col_of: (r, col_of(h)))


def _hgrn2_fwd(z, l0, l1, onorm):
    nb = T // RA

    def body(q_ref, f_ref, v_ref, g_ref, l0_ref, l1_ref, on_ref, cat_ref, sst_ref, st_scr):
        @pl.when(pl.program_id(1) == 0)
        def _():
            st_scr[...] = jnp.zeros_like(st_scr)

        st = st_scr[...]
        sst_ref[...] = st
        out, st_new = _hgrn2_rows(q_ref[...], f_ref[...], v_ref[...], g_ref[...], st,
                                  l0_ref[...], l1_ref[...], on_ref[...])
        cat_ref[...] = out.astype(cat_ref.dtype)
        st_scr[...] = st_new

    vec = pl.BlockSpec((1, HD), lambda h, r: (0, h))
    return pl.pallas_call(
        body, name="hgrn2_fwd", grid=(A_HEADS, nb),
        in_specs=[_col_spec(RA, lambda h: h), _col_spec(RA, lambda h: 8 + h),
                  _col_spec(RA, lambda h: 16 + h), _col_spec(RA, lambda h: 24 + h),
                  vec, vec, _full_spec((1, HD))],
        out_specs=[_col_spec(RA, lambda h: h),
                   pl.BlockSpec((None, None, HD, HD), lambda h, r: (h, r, 0, 0))],
        out_shape=[_sds((T, 2 * A_HEADS * HD), BF16), _sds((A_HEADS, nb, HD, HD), F32)],
        scratch_shapes=[pltpu.VMEM((HD, HD), F32)],
        compiler_params=_params(2))(z, z, z, z, l0, l1, onorm)


def _hgrn2_bwd(z, l0, l1, onorm, sst, dcat):
    nb = T // RA

    def body(q_ref, f_ref, v_ref, g_ref, l0_ref, l1_ref, on_ref, sst_ref, dcat_ref,
             dq_ref, df_ref, dv_ref, dg_ref, dl0_ref, dl1_ref, don_ref, ds_scr):
        h, r = pl.program_id(0), pl.program_id(1)

        @pl.when(r == 0)
        def _():
            ds_scr[...] = jnp.zeros_like(ds_scr)

        _, vjp = jax.vjp(_hgrn2_rows, q_ref[...], f_ref[...], v_ref[...], g_ref[...], sst_ref[...],
                         l0_ref[...], l1_ref[...], on_ref[...])
        dq, dzf, dv, dga, dst, dl0, dl1, don = vjp((dcat_ref[...], ds_scr[...]))
        dq_ref[...] = dq.astype(dq_ref.dtype)
        df_ref[...] = dzf.astype(df_ref.dtype)
        dv_ref[...] = dv.astype(dv_ref.dtype)
        dg_ref[...] = dga.astype(dg_ref.dtype)
        ds_scr[...] = dst

        @pl.when(r == 0)
        def _():
            dl0_ref[...] = dl0
            dl1_ref[...] = dl1

        @pl.when(r > 0)
        def _():
            dl0_ref[...] += dl0
            dl1_ref[...] += dl1

        first = jnp.logical_and(h == 0, r == 0)

        @pl.when(first)
        def _():
            don_ref[...] = don

        @pl.when(jnp.logical_not(first))
        def _():
            don_ref[...] += don

    def rev(col_of):
        return pl.BlockSpec((RA, HD), lambda h, r, col_of=col_of: (nb - 1 - r, col_of(h)))

    vec = pl.BlockSpec((1, HD), lambda h, r: (0, h))
    grad = _sds((T, A_HEADS * HD), BF16)
    return pl.pallas_call(
        body, name="hgrn2_bwd", grid=(A_HEADS, nb),
        in_specs=[rev(lambda h: h), rev(lambda h: 8 + h), rev(lambda h: 16 + h), rev(lambda h: 24 + h),
                  vec, vec, _full_spec((1, HD)),
                  pl.BlockSpec((None, None, HD, HD), lambda h, r: (h, nb - 1 - r, 0, 0)),
                  rev(lambda h: h)],
        out_specs=[rev(lambda h: h)] * 4 + [vec, vec, _full_spec((1, HD))],
        out_shape=[grad] * 4 + [_sds((1, A_HEADS * HD), F32)] * 2 + [_sds((1, HD), F32)],
        scratch_shapes=[pltpu.VMEM((HD, HD), F32)],
        compiler_params=_params(2))(z, z, z, z, l0, l1, onorm, sst, dcat)


def _gmlp_specs():
    vec = pl.BlockSpec((1, HD), lambda g, r: (0, g))
    ws = pl.BlockSpec((None, B_CHUNK, B_CHUNK), lambda g, r: (g, 0, 0))
    bias = pl.BlockSpec((None, B_CHUNK, 1), lambda g, r: (g, 0, 0))
    return vec, ws, bias


def _gmlp_fwd(z, cat, lnw, lnb, ws, bias):
    vec, ws_spec, bias_spec = _gmlp_specs()

    def body(u_ref, v_ref, g_ref, lnw_ref, lnb_ref, ws_ref, bias_ref, cat_in_ref, cat_ref):
        del cat_in_ref
        out = _gmlp_rows(u_ref[...], v_ref[...], g_ref[...], lnw_ref[...], lnb_ref[...],
                         ws_ref[...], bias_ref[...])
        cat_ref[...] = out.astype(cat_ref.dtype)

    return pl.pallas_call(
        body, name="gmlp_fwd", grid=(B_GROUPS, T // RB),
        in_specs=[_col_spec(RB, lambda g: 32 + g), _col_spec(RB, lambda g: 40 + g),
                  _col_spec(RB, lambda g: 48 + g), vec, vec, ws_spec, bias_spec,
                  pl.BlockSpec(memory_space=pl.ANY)],
        out_specs=_col_spec(RB, lambda g: A_HEADS + g),
        out_shape=_sds(cat.shape, cat.dtype), input_output_aliases={7: 0},
        compiler_params=_params(2))(z, z, z, lnw, lnb, ws, bias, cat)


def _gmlp_bwd(z, lnw, lnb, ws, bias, dcat):
    vec, ws_spec, bias_spec = _gmlp_specs()

    def fn(u, vb, gb, w, b, wsv, bv, dout):
        _, vjp = jax.vjp(_gmlp_rows, u, vb, gb, w, b, wsv, bv)
        return vjp(dout)

    def body(*refs):
        ins, outs = refs[:8], refs[8:]
        res = fn(*[r[...] for r in ins])
        for o, r in zip(outs[:3], res[:3]):
            o[...] = r.astype(o.dtype)
        first = pl.program_id(1) == 0
        for o, r in zip(outs[3:], res[3:]):
            @pl.when(first)
            def _(o=o, r=r):
                o[...] = r

            @pl.when(jnp.logical_not(first))
            def _(o=o, r=r):
                o[...] += r

    grad = _sds((T, B_GROUPS * HD), BF16)
    row_out = pl.BlockSpec((RB, HD), lambda g, r: (r, g))
    return pl.pallas_call(
        body, name="gmlp_bwd", grid=(B_GROUPS, T // RB),
        in_specs=[_col_spec(RB, lambda g: 32 + g), _col_spec(RB, lambda g: 40 + g),
                  _col_spec(RB, lambda g: 48 + g), vec, vec, ws_spec, bias_spec,
                  _col_spec(RB, lambda g: A_HEADS + g)],
        out_specs=[row_out] * 3 + [vec, vec, ws_spec, bias_spec],
        out_shape=[grad] * 3 + [_sds((1, B_GROUPS * HD), F32)] * 2
        + [_sds((B_GROUPS, B_CHUNK, B_CHUNK), F32), _sds((B_GROUPS, B_CHUNK, 1), F32)],
        compiler_params=_params(2))(z, z, z, lnw, lnb, ws, bias, dcat)


def _mla_pre(z1, qn, kvn, cos_t, sin_t):
    def fn(cq, ckv, kpe, cs, sn, wq, wkv):
        return _rms(cq, wq), _rms(ckv, wkv), _rope(kpe, cs, sn)
    return _pure_call("mla_pre", fn, (T // TM,),
                      [_row_spec(TM, C_RANK, 4), _row_spec(TM, C_RANK, 5), _row_spec(TM, HD, 24),
                       _row_spec(TM, HD), _row_spec(TM, HD),
                       _full_spec((1, C_RANK)), _full_spec((1, C_RANK))],
                      [_row_spec(TM, C_RANK), _row_spec(TM, C_RANK), _row_spec(TM, HD)],
                      [_sds((T, C_RANK), BF16), _sds((T, C_RANK), BF16), _sds((T, HD), BF16)],
                      (z1, z1, z1, cos_t, sin_t, qn, kvn))


def _mla_pre_bwd(z1, qn, kvn, cos_t, sin_t, dcqn, dckvn, dkp):
    def fn(cq, ckv, cs, sn, wq, wkv, g_q, g_kv, g_kp):
        _, vjp_q = jax.vjp(_rms, cq, wq)
        dcq, dwq = vjp_q(g_q)
        _, vjp_kv = jax.vjp(_rms, ckv, wkv)
        dckv, dwkv = vjp_kv(g_kv)
        return dcq, dckv, _rope_transpose(g_kp, cs, sn), dwq, dwkv
    return _pure_call("mla_pre_bwd", fn, (T // TM,),
                      [_row_spec(TM, C_RANK, 4), _row_spec(TM, C_RANK, 5),
                       _row_spec(TM, HD), _row_spec(TM, HD),
                       _full_spec((1, C_RANK)), _full_spec((1, C_RANK)),
                       _row_spec(TM, C_RANK), _row_spec(TM, C_RANK), _row_spec(TM, HD)],
                      [_row_spec(TM, C_RANK), _row_spec(TM, C_RANK), _row_spec(TM, HD),
                       _full_spec((1, C_RANK)), _full_spec((1, C_RANK))],
                      [_sds((T, C_RANK), BF16), _sds((T, C_RANK), BF16), _sds((T, HD), BF16),
                       _sds((1, C_RANK), F32), _sds((1, C_RANK), F32)],
                      (z1, z1, cos_t, sin_t, qn, kvn, dcqn, dckvn, dkp), n_acc=2)


def _gate_out(o, z1):
    def fn(ov, gate):
        return ov * _silu(gate)
    return _pure_call("gate_out", fn, (T // TM,), [_row_spec(TM, D), _row_spec(TM, D, 0)],
                      [_row_spec(TM, D)], [_sds((T, D), BF16)], (o, z1))[0]


def _gate_out_bwd(o, z1, dog):
    def fn(ov, gate, g):
        _, vjp = jax.vjp(lambda a, b: a * _silu(b), ov, gate)
        return vjp(g)
    return _pure_call("gate_out_bwd", fn, (T // TM,),
                      [_row_spec(TM, D), _row_spec(TM, D, 0), _row_spec(TM, D)],
                      [_row_spec(TM, D), _row_spec(TM, D)],
                      [_sds((T, D), F32), _sds((T, D), BF16)], (o, z1, dog))


TQ = 256


def _att_scores(q_ref, cos_ref, sin_ref, kn_ref, kp_ref):
    q = q_ref[...]
    qn = q[:, :C_NOPE].astype(BF16)
    qp = _rope(q[:, C_NOPE:], cos_ref[...], sin_ref[...]).astype(BF16)
    s = (_raw_nt(qn, kn_ref[...]) + _raw_nt(qp, kp_ref[...])) * ATT_SCALE
    row = pl.program_id(1) * TQ + lax.broadcasted_iota(jnp.int32, (TQ, T), 0)
    col = lax.broadcasted_iota(jnp.int32, (TQ, T), 1)
    return qn, qp, s, col <= row


def _att_in_specs():
    return [pl.BlockSpec((TQ, QP), lambda h, i: (i, h)),
            pl.BlockSpec((TQ, HD), lambda h, i: (i, 0)),
            pl.BlockSpec((TQ, HD), lambda h, i: (i, 0)),
            pl.BlockSpec((T, C_NOPE), lambda h, i: (0, 2 * h)),
            pl.BlockSpec((T, HD), lambda h, i: (0, 0)),
            pl.BlockSpec((T, C_V), lambda h, i: (0, 2 * h + 1))]


def _attention_fwd(q, cos_t, sin_t, kv, kp):
    def body(q_ref, cos_ref, sin_ref, kn_ref, kp_ref, v_ref, o_ref, lse_ref):
        _, _, s, mask = _att_scores(q_ref, cos_ref, sin_ref, kn_ref, kp_ref)
        s = jnp.where(mask, s, jnp.finfo(F32).min)
        m = jnp.max(s, axis=-1, keepdims=True)
        p = jnp.exp(s - m)
        l = jnp.sum(p, axis=-1, keepdims=True)
        o_ref[...] = _raw_nn(p / l, v_ref[...])
        lse_ref[...] = m + jnp.log(l)

    return pl.pallas_call(
        body, name="attention_fwd", grid=(C_HEADS, T // TQ), in_specs=_att_in_specs(),
        out_specs=[pl.BlockSpec((TQ, C_V), lambda h, i: (i, h)),
                   pl.BlockSpec((None, TQ, 1), lambda h, i: (h, i, 0))],
        out_shape=[_sds((T, C_HEADS * C_V), F32), _sds((C_HEADS, T, 1), F32)],
        compiler_params=_params(2))(q, cos_t, sin_t, kv, kp, kv)


def _attention_bwd(q, cos_t, sin_t, kv, kp, o, lse, do):
    nq = T // TQ

    def body(q_ref, cos_ref, sin_ref, kn_ref, kp_ref, v_ref, o_ref, lse_ref, do_ref,
             dq_ref, dkv_ref, dkp_ref, dkn_scr, dv_scr):
        h, i = pl.program_id(0), pl.program_id(1)
        qn, qp, s, mask = _att_scores(q_ref, cos_ref, sin_ref, kn_ref, kp_ref)
        p = jnp.where(mask, jnp.exp(s - lse_ref[...]), 0.0)
        dov = do_ref[...]
        delta = jnp.sum(dov * o_ref[...], axis=-1, keepdims=True)
        dp = _raw_nt(dov, v_ref[...])
        ds = p * (dp - delta) * ATT_SCALE
        dqn = _raw_nn(ds, kn_ref[...])
        dqp = _rope_transpose(_raw_nn(ds, kp_ref[...]), cos_ref[...], sin_ref[...])
        dq_ref[...] = jnp.concatenate([dqn, dqp], axis=1).astype(dq_ref.dtype)
        dv_part = _raw_tn(p, dov)
        dkn_part = _raw_tn(ds, qn)
        dkp_part = _raw_tn(ds, qp)

        @pl.when(i == 0)
        def _():
            dv_scr[...] = dv_part
            dkn_scr[...] = dkn_part

        @pl.when(i > 0)
        def _():
            dv_scr[...] += dv_part
            dkn_scr[...] += dkn_part

        @pl.when(i == nq - 1)
        def _():
            dkv_ref[...] = jnp.concatenate([dkn_scr[...], dv_scr[...]], axis=1).astype(dkv_ref.dtype)

        first = jnp.logical_and(h == 0, i == 0)

        @pl.when(first)
        def _():
            dkp_ref[...] = dkp_part

        @pl.when(jnp.logical_not(first))
        def _():
            dkp_ref[...] += dkp_part

    return pl.pallas_call(
        body, name="attention_bwd", grid=(C_HEADS, nq),
        in_specs=_att_in_specs() + [pl.BlockSpec((TQ, C_V), lambda h, i: (i, h)),
                                    pl.BlockSpec((None, TQ, 1), lambda h, i: (h, i, 0)),
                                    pl.BlockSpec((TQ, C_V), lambda h, i: (i, h))],
        out_specs=[pl.BlockSpec((TQ, QP), lambda h, i: (i, h)),
                   pl.BlockSpec((T, C_NOPE + C_V), lambda h, i: (0, h)),
                   _full_spec((T, HD))],
        out_shape=[_sds((T, C_HEADS * QP), BF16), _sds((T, C_HEADS * (C_NOPE + C_V)), BF16),
                   _sds((T, HD), F32)],
        scratch_shapes=[pltpu.VMEM((T, C_NOPE), F32), pltpu.VMEM((T, C_V), F32)],
        compiler_params=_params(2))(q, cos_t, sin_t, kv, kp, kv, o, lse, do)


def _adamw_math(w, g, m, v):
    m = ADAM_B1 * m + (1.0 - ADAM_B1) * g
    v = ADAM_B2 * v + (1.0 - ADAM_B2) * (g * g)
    m_hat = m / (1.0 - ADAM_B1 ** ADAM_STEP)
    v_hat = v / (1.0 - ADAM_B2 ** ADAM_STEP)
    delta = -ADAM_LR * (m_hat / (jnp.sqrt(v_hat) + ADAM_EPS) + ADAM_WD * w)
    return delta, m, v


def _adamw(name, parts, w, m, v, tr):
    rows, cols = w.shape

    def fn(pv, wv, mv, vv):
        g = pv[0].astype(F32)
        for d in range(1, N_DEV):
            g = g + pv[d].astype(F32)
        return (g,) + _adamw_math(wv, g, mv, vv)

    blk = pl.BlockSpec((tr, cols), lambda i: (i, 0))
    return _pure_call(name, fn, (rows // tr,),
                      [pl.BlockSpec((N_DEV, tr, cols), lambda i: (0, i, 0)), blk, blk, blk],
                      [blk] * 4, [_sds((rows, cols), F32)] * 4, (parts, w, m, v))


def _exchange(name, arrs, gather):
    n = len(arrs)

    def body(*refs):
        ins, outs = refs[:n], refs[n:2 * n]
        send_sems, recv_sems, local_sems = refs[2 * n:]
        x, y, c = lax.axis_index("x"), lax.axis_index("y"), lax.axis_index("c")
        me = 4 * x + 2 * y + c

        def peer(k):
            return (x ^ (k >> 2), y ^ ((k >> 1) & 1), c ^ (k & 1))

        def copy(a, k):
            src = ins[a] if gather else ins[a].at[me ^ k]
            return pltpu.make_async_remote_copy(
                src_ref=src, dst_ref=outs[a].at[me], send_sem=send_sems.at[a, k - 1],
                recv_sem=recv_sems.at[a, k - 1], device_id=peer(k), device_id_type=MESH_ID)

        def arrival(a, k):
            src = ins[a] if gather else ins[a].at[me]
            return pltpu.make_async_remote_copy(
                src_ref=src, dst_ref=outs[a].at[me ^ k], send_sem=send_sems.at[a, k - 1],
                recv_sem=recv_sems.at[a, k - 1], device_id=peer(k), device_id_type=MESH_ID)

        own = [pltpu.make_async_copy(ins[a] if gather else ins[a].at[me], outs[a].at[me], local_sems.at[a])
               for a in range(n)]
        for cp in own:
            cp.start()
        for k in range(1, N_DEV):
            for a in range(n):
                copy(a, k).start()
        for k in range(1, N_DEV):
            for a in range(n):
                arrival(a, k).wait_recv()
        for k in range(1, N_DEV):
            for a in range(n):
                copy(a, k).wait_send()
        for cp in own:
            cp.wait()

    any_spec = pl.BlockSpec(memory_space=pl.ANY)
    out_shape = [_sds((N_DEV,) + a.shape if gather else a.shape, a.dtype) for a in arrs]
    return pl.pallas_call(
        body, name=name, in_specs=[any_spec] * n, out_specs=[any_spec] * n, out_shape=out_shape,
        scratch_shapes=[pltpu.SemaphoreType.DMA((n, N_DEV - 1)), pltpu.SemaphoreType.DMA((n, N_DEV - 1)),
                        pltpu.SemaphoreType.DMA((n,))],
        compiler_params=pltpu.CompilerParams(has_side_effects=True))(*arrs)


def _pad_rope(p):
    z = jnp.zeros(p.shape[:-1] + (32,), p.dtype)
    return jnp.concatenate([p[..., :32], z, p[..., 32:], z], axis=-1)


def _unpad_rope(p):
    return jnp.concatenate([p[..., :32], p[..., 64:96]], axis=-1)


def _odd_in_layout(w):
    w = w.transpose(1, 0, 2).reshape(D, ODD_IN)
    cq, ckv, kpe, gate = w[:, :512], w[:, 512:1024], w[:, 1024:1088], w[:, 1088:]
    return jnp.concatenate([gate, cq, ckv, _pad_rope(kpe)], axis=1)


def _odd_in_unlayout(dw):
    gate, cq, ckv, kpe = dw[:, :2048], dw[:, 2048:2560], dw[:, 2560:3072], _unpad_rope(dw[:, 3072:])
    w = jnp.concatenate([cq, ckv, kpe, gate], axis=1)
    return w.reshape(D, N_DEV, ODD_IN // N_DEV).transpose(1, 0, 2)


def _qb_layout(w):
    w = w.transpose(1, 0, 2).reshape(C_RANK, C_HEADS, C_QK)
    w = jnp.concatenate([w[..., :C_NOPE], _pad_rope(w[..., C_NOPE:])], axis=-1)
    return w.reshape(C_RANK, C_HEADS * QP)


def _qb_unlayout(dw):
    dw = dw.reshape(C_RANK, C_HEADS, QP)
    dw = jnp.concatenate([dw[..., :C_NOPE], _unpad_rope(dw[..., C_NOPE:])], axis=-1)
    return dw.reshape(C_RANK, N_DEV, C_HEADS * C_QK // N_DEV).transpose(1, 0, 2)


def _rope_tables(positions):
    inv_freq = ROPE_THETA ** (-jnp.arange(0, C_ROPE, 2, dtype=F32) / C_ROPE)
    ang = positions.astype(F32)[0][:, None] * inv_freq
    cos, sin = jnp.cos(ang), jnp.sin(ang)
    z = jnp.zeros_like(cos)
    return jnp.concatenate([cos, z, cos, z], axis=1), jnp.concatenate([-sin, z, sin, z], axis=1)


SMALL_ROWS = 1144


def _pack_small(parts):
    flat = jnp.concatenate([p.reshape(-1) for p in parts])
    return jnp.pad(flat, (0, SMALL_ROWS * 128 - flat.shape[0])).reshape(SMALL_ROWS, 128)


def _unpack_small(packed, shapes):
    flat = packed.reshape(-1)
    out, off = [], 0
    for s in shapes:
        size = 1
        for d in s:
            size *= d
        out.append(flat[off:off + size].reshape(s))
        off += size
    return out


def _forward_backward(x, cos_t, sin_t, target, norm_pre, norm_post, lb_logits, a_onorm, ln_w, ln_b,
                      b_ws, b_bias, q_norm, kv_norm, w_ev_in, w_ev_out, w_od_in, w_qb, w_kvb, w_od_out):
    npre0, npre1 = norm_pre[0:1], norm_pre[1:2]
    npost0, npost1 = norm_post[0:1], norm_post[1:2]
    l0, l1 = lb_logits[0:1], lb_logits[1:2]
    bias_col = b_bias.reshape(B_GROUPS, B_CHUNK, 1)
    ws = b_ws.reshape(B_GROUPS, B_CHUNK, B_CHUNK)

    h0 = _pre_norm("pre_norm0", x, npre0)
    z0 = _mm_nn("ev_in", h0, w_ev_in, F32, 1024, 896)
    cat, sst = _hgrn2_fwd(z0, l0, l1, a_onorm)
    cat = _gmlp_fwd(z0, cat, ln_w, ln_b, ws, bias_col)
    y0 = _mm_nn("ev_out", cat, w_ev_out, F32, 1024, 1024)
    x1, h1 = _post_pre_norm(x, y0, npost0, npre1)
    z1 = _mm_nn("od_in", h1, w_od_in[None], F32, 1024, 640)
    cqn, ckvn, kp = _mla_pre(z1, q_norm, kv_norm, cos_t, sin_t)
    q = _mm_nn("od_qb", cqn, w_qb[None], F32, 1024, 1024)
    kv = _mm_nn("od_kvb", ckvn, w_kvb, BF16, 1024, 512)
    o, lse = _attention_fwd(q, cos_t, sin_t, kv, kp)
    og = _gate_out(o, z1)
    y1 = _mm_nn("od_out", og, w_od_out, F32, 1024, 1024)
    dx2, dy1, loss_part, dnpost1 = _final_loss(x1, y1, npost1, target)

    dog = _mm_nt("od_out_dx", dy1, w_od_out, F32, 1024, 1024)
    g_od_out = _mm_tn("od_out_dw", og, dy1, 1, BF16, 1024, 1024)
    do, dgate = _gate_out_bwd(o, z1, dog)
    dq, dkv, dkp = _attention_bwd(q, cos_t, sin_t, kv, kp, o, lse, do)
    dcqn = _mm_nt("od_qb_dx", dq, w_qb[None], F32, 1024, 512)
    g_qb = _mm_tn("od_qb_dw", cqn, dq, 1, F32, 512, 1024)
    dckvn = _mm_nt("od_kvb_dx", dkv, w_kvb, F32, 1024, 512)
    g_kvb = _mm_tn("od_kvb_dw", ckvn, dkv, N_DEV, BF16, 512, 512)
    dcq, dckv, dkpe, dqn, dkvn = _mla_pre_bwd(z1, q_norm, kv_norm, cos_t, sin_t, dcqn, dckvn, dkp)
    dz1 = jnp.concatenate([dgate, dcq, dckv, dkpe], axis=1)
    dh1 = _mm_nt("od_in_dx", dz1, w_od_in[None], F32, 1024, 1024)
    g_od_in = _mm_tn("od_in_dw", h1, dz1, 1, F32, 1024, 640)
    dx1, dy0, dnpost0, dnpre1 = _post_pre_norm_bwd(y0, x1, npost0, npre1, dx2, dh1)

    dcat = _mm_nt("ev_out_dx", dy0, w_ev_out, F32, 1024, 1024)
    g_ev_out = _mm_tn("ev_out_dw", cat, dy0, 1, BF16, 1024, 1024)
    dqa, dfa, dia, dga, dl0, dl1, donorm = _hgrn2_bwd(z0, l0, l1, a_onorm, sst, dcat)
    dub, dvb, dgb, dlnw, dlnb, dws, dbias = _gmlp_bwd(z0, ln_w, ln_b, ws, bias_col, dcat)
    dz0 = jnp.concatenate([dqa, dfa, dia, dga, dub, dvb, dgb], axis=1)
    dh0 = _mm_nt("ev_in_dx", dz0, w_ev_in, F32, 1024, 1024)
    g_ev_in = _mm_tn("ev_in_dw", h0, dz0, N_DEV, BF16, 1024, 896)
    grad_x, dnpre0 = _pre_norm_bwd(x, npre0, dh0, dx1)

    big = (g_ev_in, g_ev_out.reshape(N_DEV, D // N_DEV, D),
           _odd_in_unlayout(g_od_in[0]).astype(BF16), _qb_unlayout(g_qb[0]).astype(BF16),
           g_kvb, g_od_out.reshape(N_DEV, D // N_DEV, D))
    small = (jnp.concatenate([dnpre0, dnpre1], axis=0), jnp.concatenate([dnpost0, dnpost1], axis=0),
             jnp.concatenate([dl0, dl1], axis=0), donorm, dlnw, dlnb,
             dws.reshape(1, B_GROUPS, B_CHUNK, B_CHUNK), dbias.reshape(1, B_GROUPS, B_CHUNK), dqn, dkvn)
    return loss_part[0, 0], grad_x, big, small


def kernel(x, positions, norm_pre, norm_post, ev_w_in, ev_lb_logits, ev_a_onorm, ev_b_ln_w, ev_b_ln_b, ev_b_ws, ev_b_bias, ev_w_out, od_w_in, od_q_norm, od_w_qb, od_kv_norm, od_w_kvb, od_w_out, loss_target, m_norm_pre, m_norm_post, m_ev_w_in, m_ev_lb_logits, m_ev_a_onorm, m_ev_b_ln_w, m_ev_b_ln_b, m_ev_b_ws, m_ev_b_bias, m_ev_w_out, m_od_w_in, m_od_q_norm, m_od_w_qb, m_od_kv_norm, m_od_w_kvb, m_od_w_out, v_norm_pre, v_norm_post, v_ev_w_in, v_ev_lb_logits, v_ev_a_onorm, v_ev_b_ln_w, v_ev_b_ln_b, v_ev_b_ws, v_ev_b_bias, v_ev_w_out, v_od_w_in, v_od_q_norm, v_od_w_qb, v_od_kv_norm, v_od_w_kvb, v_od_w_out):
    me = 4 * lax.axis_index("x") + 2 * lax.axis_index("y") + lax.axis_index("c")

    big_w = (ev_w_in[0], ev_w_out[0], od_w_in[0], od_w_qb[0], od_w_kvb[0], od_w_out[0])
    norms = jnp.pad(jnp.concatenate([od_q_norm, od_kv_norm], axis=1), ((0, 7), (0, 0)))
    gathered = _exchange("gather_weights", [w.astype(BF16) for w in big_w] + [norms], gather=True)
    w_ev_in, w_ev_out, w_od_in, w_qb, w_kvb, w_od_out, norms_all = gathered
    q_norm = norms_all[:, 0, :64].reshape(1, C_RANK)
    kv_norm = norms_all[:, 0, 64:].reshape(1, C_RANK)

    cos_t, sin_t = _rope_tables(positions)
    loss_part, grad_x, big_g, small_g = _forward_backward(
        x[0], cos_t, sin_t, loss_target[0], norm_pre, norm_post, ev_lb_logits, ev_a_onorm, ev_b_ln_w,
        ev_b_ln_b, ev_b_ws, ev_b_bias, q_norm, kv_norm, w_ev_in, w_ev_out.reshape(1, D, D),
        _odd_in_layout(w_od_in), _qb_layout(w_qb), w_kvb, w_od_out.reshape(1, D, D))
    loss = lax.psum(loss_part, ("x", "y", "c"))

    packed = _pack_small(small_g)
    parts = _exchange("scatter_grads", list(big_g), gather=False)
    small_all = _exchange("gather_small_grads", [packed], gather=True)[0]

    names = ("ev_w_in", "ev_w_out", "od_w_in", "od_w_qb", "od_w_kvb", "od_w_out")
    big_m = (m_ev_w_in[0], m_ev_w_out[0], m_od_w_in[0], m_od_w_qb[0], m_od_w_kvb[0], m_od_w_out[0])
    big_v = (v_ev_w_in[0], v_ev_w_out[0], v_od_w_in[0], v_od_w_qb[0], v_od_w_kvb[0], v_od_w_out[0])
    big_out = {}
    for nm, p, w, m, v in zip(names, parts, big_w, big_m, big_v):
        big_out[nm] = [r[None] for r in _adamw("adamw_" + nm, p, w, m, v, w.shape[0] // 8)]

    small_w = (norm_pre, norm_post, ev_lb_logits, ev_a_onorm, ev_b_ln_w, ev_b_ln_b, ev_b_ws, ev_b_bias)
    small_m = (m_norm_pre, m_norm_post, m_ev_lb_logits, m_ev_a_onorm, m_ev_b_ln_w, m_ev_b_ln_b, m_ev_b_ws, m_ev_b_bias)
    small_v = (v_norm_pre, v_norm_post, v_ev_lb_logits, v_ev_a_onorm, v_ev_b_ln_w, v_ev_b_ln_b, v_ev_b_ws, v_ev_b_bias)
    zero512 = jnp.zeros((1, C_RANK), F32)

    def packed_of(ws_):
        return _pack_small(list(ws_) + [zero512, zero512])

    res = _adamw("adamw_small", small_all, packed_of(small_w), packed_of(small_m), packed_of(small_v), SMALL_ROWS // 11)
    shapes = [w.shape for w in small_w] + [(1, C_RANK), (1, C_RANK)]
    g_s, d_s, m_s, v_s = (_unpack_small(r, shapes) for r in res)

    g_norms = jnp.concatenate([lax.dynamic_slice(g_s[8], (0, 64 * me), (1, 64)),
                               lax.dynamic_slice(g_s[9], (0, 64 * me), (1, 64))], axis=1)
    g_norms8 = jnp.concatenate([g_norms[None], jnp.zeros((N_DEV - 1, 1, 128), F32)], axis=0)
    res_n = _adamw("adamw_norms", g_norms8,
                   jnp.concatenate([od_q_norm, od_kv_norm], axis=1),
                   jnp.concatenate([m_od_q_norm, m_od_kv_norm], axis=1),
                   jnp.concatenate([v_od_q_norm, v_od_kv_norm], axis=1), 1)
    qn_out = [r[:, :64] for r in res_n]
    kvn_out = [r[:, 64:] for r in res_n]

    order = ("norm_pre", "norm_post", "ev_w_in", "ev_lb_logits", "ev_a_onorm", "ev_b_ln_w", "ev_b_ln_b",
             "ev_b_ws", "ev_b_bias", "ev_w_out", "od_w_in", "od_q_norm", "od_w_qb", "od_kv_norm",
             "od_w_kvb", "od_w_out")
    small_names = ("norm_pre", "norm_post", "ev_lb_logits", "ev_a_onorm", "ev_b_ln_w", "ev_b_ln_b",
                   "ev_b_ws", "ev_b_bias")
    outs = [loss, grad_x[None]]
    for kind in range(4):
        small_kind = (g_s, d_s, m_s, v_s)[kind]
        for nm in order:
            if nm in big_out:
                outs.append(big_out[nm][kind])
            elif nm == "od_q_norm":
                outs.append(qn_out[kind])
            elif nm == "od_kv_norm":
                outs.append(kvn_out[kind])
            else:
                outs.append(small_kind[small_names.index(nm)])
    return tuple(outs)
```

```python
import functools

import jax
import jax.numpy as jnp
from jax import lax
from jax.experimental import pallas as pl
from jax.experimental.pallas import tpu as pltpu

F32 = jnp.float32
BF16 = jnp.bfloat16

N_DEV = 8
T = 2048
D = 2048
EPS = 1e-6
A_HEADS = 8
HD = 128
A_CHUNK = 64
A_SUB = 16
B_GROUPS = 8
B_CHUNK = 128
EVEN_IN = 7168
C_HEADS = 16
C_RANK = 512
C_NOPE = 128
C_ROPE = 64
C_QK = C_NOPE + C_ROPE
C_V = 128
ODD_IN = 3136
ODD_IN_PAD = 3200
QP = 256
ROPE_THETA = 10000.0
ATT_SCALE = C_QK ** -0.5

ADAM_LR = 0.001
ADAM_B1 = 0.9
ADAM_B2 = 0.999
ADAM_EPS = 1e-08
ADAM_WD = 0.01
ADAM_STEP = 10

VMEM_LIMIT_V7X = 56 * 1024 * 1024
MESH_ID = pl.DeviceIdType.MESH


def _params(n_grid):
    return pltpu.CompilerParams(dimension_semantics=("arbitrary",) * n_grid,
                                vmem_limit_bytes=VMEM_LIMIT_V7X)


def _dg(a, b, ca, cb):
    return lax.dot_general(a.astype(BF16), b.astype(BF16), (((ca,), (cb,)), ((), ())),
                           preferred_element_type=F32)


def _raw_nn(a, b):
    return _dg(a, b, 1, 0)


def _raw_nt(a, b):
    return _dg(a, b, 1, 1)


def _raw_tn(a, b):
    return _dg(a, b, 0, 0)


@jax.custom_vjp
def _dot_nn(a, b):
    return _raw_nn(a, b)


def _dot_nn_fwd(a, b):
    return _raw_nn(a, b), (a.astype(BF16), b.astype(BF16))


def _dot_nn_bwd(res, g):
    a, b = res
    return _raw_nt(g, b), _raw_tn(a, g)


_dot_nn.defvjp(_dot_nn_fwd, _dot_nn_bwd)


@jax.custom_vjp
def _dot_nt(a, b):
    return _raw_nt(a, b)


def _dot_nt_fwd(a, b):
    return _raw_nt(a, b), (a.astype(BF16), b.astype(BF16))


def _dot_nt_bwd(res, g):
    a, b = res
    return _raw_nn(g, b), _raw_tn(g, a)


_dot_nt.defvjp(_dot_nt_fwd, _dot_nt_bwd)


@jax.custom_vjp
def _dot_tn(a, b):
    return _raw_tn(a, b)


def _dot_tn_fwd(a, b):
    return _raw_tn(a, b), (a.astype(BF16), b.astype(BF16))


def _dot_tn_bwd(res, g):
    a, b = res
    return _raw_nt(b, g), _raw_nn(a, g)


_dot_tn.defvjp(_dot_tn_fwd, _dot_tn_bwd)


@jax.custom_vjp
def _sigmoid(x):
    e = jnp.exp(-jnp.abs(x))
    return jnp.where(x >= 0, 1.0 / (1.0 + e), e / (1.0 + e))


def _sigmoid_fwd(x):
    s = _sigmoid(x)
    return s, s


def _sigmoid_bwd(s, g):
    return (g * s * (1.0 - s),)


_sigmoid.defvjp(_sigmoid_fwd, _sigmoid_bwd)


def _silu(x):
    return x * _sigmoid(x)


def _rms(x, w):
    return x * lax.rsqrt(jnp.mean(x * x, axis=-1, keepdims=True) + EPS) * w


def _split3(x):
    hi = x.astype(BF16)
    r = x - hi.astype(F32)
    mid = r.astype(BF16)
    lo = (r - mid.astype(F32)).astype(BF16)
    return hi, mid, lo


def _mask_apply(mask_bf16, x, contract):
    out = None
    for piece in _split3(x):
        d = lax.dot_general(mask_bf16, piece, (((contract,), (0,)), ((), ())),
                            preferred_element_type=F32)
        out = d if out is None else out + d
    return out


def _chunk_tri(rows):
    r = lax.broadcasted_iota(jnp.int32, (rows, rows), 0)
    c = lax.broadcasted_iota(jnp.int32, (rows, rows), 1)
    return ((r >= c) & (r // A_CHUNK == c // A_CHUNK)).astype(BF16)


@jax.custom_vjp
def _chunk_cumsum(x):
    return _mask_apply(_chunk_tri(x.shape[0]), x, 1)


def _chunk_cumsum_fwd(x):
    return _chunk_cumsum(x), None


def _chunk_cumsum_bwd(_, g):
    return (_mask_apply(_chunk_tri(g.shape[0]), g, 0),)


_chunk_cumsum.defvjp(_chunk_cumsum_fwd, _chunk_cumsum_bwd)


def _hgrn2_rows(q, zf, v, ga, st, l0, l1, onorm):
    rows = q.shape[0]
    n_sub = A_CHUNK // A_SUB
    mx = jnp.maximum(l0, l1)
    e0 = jnp.exp(l0 - mx)
    e1 = jnp.exp(l1 - mx)
    lb = e0 / (e0 + e1)
    lf = jnp.log(lb + (1.0 - lb) * _sigmoid(zf))
    k = (1.0 - lb) * _sigmoid(-zf)
    b = _chunk_cumsum(lf)

    t_idx = lax.broadcasted_iota(jnp.int32, (A_CHUNK, n_sub * A_CHUNK), 0)
    c_idx = lax.broadcasted_iota(jnp.int32, (A_CHUNK, n_sub * A_CHUNK), 1)
    sel = (c_idx // A_CHUNK == t_idx // A_SUB) & (c_idx % A_CHUNK <= t_idx)
    key_row = lax.broadcasted_iota(jnp.int32, (A_CHUNK, HD), 0)

    outs = []
    for n in range(rows // A_CHUNK):
        lo = n * A_CHUNK
        qc, kc, vc = q[lo:lo + A_CHUNK], k[lo:lo + A_CHUNK], v[lo:lo + A_CHUNK]
        lfc, bc = lf[lo:lo + A_CHUNK], b[lo:lo + A_CHUNK]
        b_last = bc[A_CHUNK - 1:A_CHUNK]
        o_inter = _dot_nt(qc * jnp.exp(bc), st)
        kv_t = _dot_tn(vc, kc * jnp.exp(b_last - bc))
        st = st * jnp.exp(b_last) + kv_t
        g_rows, k_subs = [], []
        for i in range(n_sub):
            g_i = bc[i * A_SUB:i * A_SUB + 1] - lfc[i * A_SUB:i * A_SUB + 1]
            g_rows.append(jnp.broadcast_to(g_i, (A_SUB, HD)))
            expo = jnp.where(key_row < (i + 1) * A_SUB, g_i - bc, -jnp.inf)
            k_subs.append(kc * jnp.exp(expo))
        q_sub = qc * jnp.exp(bc - jnp.concatenate(g_rows, axis=0))
        scores = _dot_nt(q_sub, jnp.concatenate(k_subs, axis=0))
        scores = jnp.where(sel, scores, 0.0)
        o_intra = _dot_nn(scores, jnp.concatenate([vc] * n_sub, axis=0))
        outs.append(o_inter + o_intra)
    o = jnp.concatenate(outs, axis=0)
    return _rms(o, onorm) * _silu(ga), st


def _gmlp_rows(u, vb, gb, lnw, lnb, ws, bias):
    rows = u.shape[0]
    mu = jnp.mean(vb, axis=-1, keepdims=True)
    xc = vb - mu
    vg = xc * lax.rsqrt(jnp.mean(xc * xc, axis=-1, keepdims=True) + EPS) * lnw + lnb
    r = lax.broadcasted_iota(jnp.int32, (B_CHUNK, B_CHUNK), 0)
    c = lax.broadcasted_iota(jnp.int32, (B_CHUNK, B_CHUNK), 1)
    ws_causal = jnp.where(r >= c, ws, 0.0)
    svs = [_dot_nn(ws_causal, vg[n * B_CHUNK:(n + 1) * B_CHUNK]) + bias
           for n in range(rows // B_CHUNK)]
    return u * jnp.concatenate(svs, axis=0) * _silu(gb)


def _rope(x, cos_t, sin_t):
    return x * cos_t + pltpu.roll(x, 64, 1) * sin_t


def _rope_transpose(g, cos_t, sin_t):
    return g * cos_t + pltpu.roll(g * sin_t, 64, 1)


ANY_SPEC = pl.BlockSpec(memory_space=pl.ANY)


def _live(deps):
    return [d for d in deps if d is not None]


def _skip_deps(body, n_in, n_deps):
    def wrapped(*refs):
        return body(*refs[:n_in], *refs[n_in + n_deps:])
    return wrapped


def _pure_call(name, fn, grid, in_specs, out_specs, out_shape, args, n_acc=0, deps=()):
    deps = _live(deps)
    n_in, n_out, n_deps = len(in_specs), len(out_specs), len(deps)
    in_specs = list(in_specs) + [ANY_SPEC] * n_deps
    args = tuple(args) + tuple(deps)

    def body(*refs):
        res = fn(*[r[...] for r in refs[:n_in]])
        if not isinstance(res, (tuple, list)):
            res = (res,)
        outs = refs[n_in + n_deps:n_in + n_deps + n_out]
        for o, r in zip(outs[:n_out - n_acc], res[:n_out - n_acc]):
            o[...] = r.astype(o.dtype)
        if n_acc:
            first = functools.reduce(jnp.logical_and, [pl.program_id(i) == 0 for i in range(len(grid))])
            for o, r in zip(outs[n_out - n_acc:], res[n_out - n_acc:]):
                @pl.when(first)
                def _(o=o, r=r):
                    o[...] = r.astype(o.dtype)

                @pl.when(jnp.logical_not(first))
                def _(o=o, r=r):
                    o[...] += r.astype(o.dtype)

    return pl.pallas_call(body, name=name, grid=grid, in_specs=in_specs, out_specs=out_specs,
                          out_shape=out_shape, compiler_params=_params(len(grid)))(*args)


def _sds(shape, dtype):
    return jax.ShapeDtypeStruct(shape, dtype)


def _row_spec(tm, width, col=0):
    return pl.BlockSpec((tm, width), lambda i, col=col: (i, col))


def _full_spec(shape):
    nd = len(shape)
    return pl.BlockSpec(shape, lambda *_: (0,) * nd)


def _mm_nn(name, a, b, out_dtype, tm, tn, deps=()):
    deps = _live(deps)
    m, k = a.shape
    j, _, n = b.shape
    per = n // tn

    def body(a_ref, b_ref, o_ref):
        o_ref[...] = _raw_nn(a_ref[...], b_ref[...]).astype(o_ref.dtype)

    return pl.pallas_call(
        _skip_deps(body, 2, len(deps)), name=name, grid=(m // tm, j * per),
        in_specs=[pl.BlockSpec((tm, k), lambda i, c: (i, 0)),
                  pl.BlockSpec((None, k, tn), lambda i, c: (c // per, 0, c % per))] + [ANY_SPEC] * len(deps),
        out_specs=pl.BlockSpec((tm, tn), lambda i, c: (i, c)),
        out_shape=_sds((m, j * n), out_dtype), compiler_params=_params(2))(a, b, *deps)


def _mm_nt(name, a, b, out_dtype, tm, tn, deps=()):
    deps = _live(deps)
    m = a.shape[0]
    j, nn, n = b.shape

    def body(a_ref, b_ref, o_ref, acc_ref):
        part = _raw_nt(a_ref[...], b_ref[...])
        if j == 1:
            o_ref[...] = part.astype(o_ref.dtype)
        else:
            kk = pl.program_id(2)

            @pl.when(kk == 0)
            def _():
                acc_ref[...] = part

            @pl.when(kk > 0)
            def _():
                acc_ref[...] += part

            @pl.when(kk == j - 1)
            def _():
                o_ref[...] = acc_ref[...].astype(o_ref.dtype)

    acc_shape = (tm, tn) if j > 1 else (8, 128)
    return pl.pallas_call(
        _skip_deps(body, 2, len(deps)), name=name, grid=(m // tm, nn // tn, j),
        in_specs=[pl.BlockSpec((tm, n), lambda i, c, kk: (i, kk)),
                  pl.BlockSpec((None, tn, n), lambda i, c, kk: (kk, c, 0))] + [ANY_SPEC] * len(deps),
        out_specs=pl.BlockSpec((tm, tn), lambda i, c, kk: (i, c)),
        out_shape=_sds((m, nn), out_dtype),
        scratch_shapes=[pltpu.VMEM(acc_shape, F32)], compiler_params=_params(3))(a, b, *deps)


def _mm_tn(name, a, b, j, out_dtype, tm, tn, deps=()):
    deps = _live(deps)
    k, m = a.shape
    n = b.shape[1] // j
    per = n // tn

    def body(a_ref, b_ref, o_ref):
        o_ref[...] = _raw_tn(a_ref[...], b_ref[...]).astype(o_ref.dtype)

    return pl.pallas_call(
        _skip_deps(body, 2, len(deps)), name=name, grid=(m // tm, j * per),
        in_specs=[pl.BlockSpec((k, tm), lambda i, c: (0, i)),
                  pl.BlockSpec((k, tn), lambda i, c: (0, c))] + [ANY_SPEC] * len(deps),
        out_specs=pl.BlockSpec((None, tm, tn), lambda i, c: (c // per, i, c % per)),
        out_shape=_sds((j, m, n), out_dtype), compiler_params=_params(2))(a, b, *deps)


TM = 256


def _pre_norm(name, x, w_row, deps=()):
    def fn(xv, w):
        return _rms(xv, w)
    return _pure_call(name, fn, (T // TM,), [_row_spec(TM, D), _full_spec((1, D))],
                      [_row_spec(TM, D)], [_sds((T, D), BF16)], (x, w_row), deps=deps)[0]


def _post_pre_norm(x, y, w_post, w_pre):
    def fn(xv, yv, wp, wn):
        x1 = xv + _rms(yv, wp)
        return x1, _rms(x1, wn)
    return _pure_call("post_pre_norm", fn, (T // TM,),
                      [_row_spec(TM, D), _row_spec(TM, D), _full_spec((1, D)), _full_spec((1, D))],
                      [_row_spec(TM, D), _row_spec(TM, D)],
                      [_sds((T, D), F32), _sds((T, D), BF16)], (x, y, w_post, w_pre))


def _post_pre_norm_bwd(y, x1, w_post, w_pre, dx1_in, dh1, deps=()):
    def fn(yv, x1v, wp, wn, dx1v, dh1v):
        _, vjp_pre = jax.vjp(_rms, x1v, wn)
        dx1_h, dwn = vjp_pre(dh1v)
        dx1 = dx1v + dx1_h
        _, vjp_post = jax.vjp(_rms, yv, wp)
        dy, dwp = vjp_post(dx1)
        return dx1, dy, dwp, dwn
    return _pure_call("post_pre_norm_bwd", fn, (T // TM,),
                      [_row_spec(TM, D), _row_spec(TM, D), _full_spec((1, D)), _full_spec((1, D)),
                       _row_spec(TM, D), _row_spec(TM, D)],
                      [_row_spec(TM, D), _row_spec(TM, D), _full_spec((1, D)), _full_spec((1, D))],
                      [_sds((T, D), F32), _sds((T, D), BF16), _sds((1, D), F32), _sds((1, D), F32)],
                      (y, x1, w_post, w_pre, dx1_in, dh1), n_acc=2, deps=deps)


def _final_loss(x1, y, w_post, target):
    def fn(x1v, yv, wp, tv):
        r, vjp = jax.vjp(_rms, yv, wp)
        err = x1v + r - tv
        part = 0.5 * jnp.sum(jnp.mean(err * err, axis=-1, keepdims=True), axis=0, keepdims=True)
        dx2 = err * (1.0 / D)
        dy, dwp = vjp(dx2)
        return dx2, dy, jnp.broadcast_to(part, (1, 128)), dwp
    return _pure_call("final_loss", fn, (T // TM,),
                      [_row_spec(TM, D), _row_spec(TM, D), _full_spec((1, D)), _row_spec(TM, D)],
                      [_row_spec(TM, D), _row_spec(TM, D), _full_spec((1, 128)), _full_spec((1, D))],
                      [_sds((T, D), F32), _sds((T, D), BF16), _sds((1, 128), F32), _sds((1, D), F32)],
                      (x1, y, w_post, target), n_acc=2)


def _pre_norm_bwd(x, w_row, dh, dx_res, deps=()):
    def fn(xv, w, dhv, dxv):
        _, vjp = jax.vjp(_rms, xv, w)
        dx, dw = vjp(dhv)
        return dxv + dx, dw
    return _pure_call("pre_norm_bwd", fn, (T // TM,),
                      [_row_spec(TM, D), _full_spec((1, D)), _row_spec(TM, D), _row_spec(TM, D)],
                      [_row_spec(TM, D), _full_spec((1, D))],
                      [_sds((T, D), F32), _sds((1, D), F32)], (x, w_row, dh, dx_res), n_acc=1, deps=deps)


RA = 256
RB = 512


def _col_spec(rows, col_of):
    return pl.BlockSpec((rows, HD), lambda h, r, col_of=col_of: (r, col_of(h)))


def _hgrn2_fwd(z, l0, l1, onorm):
    nb = T // RA

    def body(q_ref, f_ref, v_ref, g_ref, l0_ref, l1_ref, on_ref, cat_ref, sst_ref, st_scr):
        @pl.when(pl.program_id(1) == 0)
        def _():
            st_scr[...] = jnp.zeros_like(st_scr)

        st = st_scr[...]
        sst_ref[...] = st
        out, st_new = _hgrn2_rows(q_ref[...], f_ref[...], v_ref[...], g_ref[...], st,
                                  l0_ref[...], l1_ref[...], on_ref[...])
        cat_ref[...] = out.astype(cat_ref.dtype)
        st_scr[...] = st_new

    vec = pl.BlockSpec((1, HD), lambda h, r: (0, h))
    return pl.pallas_call(
        body, name="hgrn2_fwd", grid=(A_HEADS, nb),
        in_specs=[_col_spec(RA, lambda h: h), _col_spec(RA, lambda h: 8 + h),
                  _col_spec(RA, lambda h: 16 + h), _col_spec(RA, lambda h: 24 + h),
                  vec, vec, _full_spec((1, HD))],
        out_specs=[_col_spec(RA, lambda h: h),
                   pl.BlockSpec((None, None, HD, HD), lambda h, r: (h, r, 0, 0))],
        out_shape=[_sds((T, 2 * A_HEADS * HD), BF16), _sds((A_HEADS, nb, HD, HD), F32)],
        scratch_shapes=[pltpu.VMEM((HD, HD), F32)],
        compiler_params=_params(2))(z, z, z, z, l0, l1, onorm)


def _hgrn2_bwd(z, l0, l1, onorm, sst, dcat, deps=()):
    nb = T // RA
    deps = _live(deps)

    def body(q_ref, f_ref, v_ref, g_ref, l0_ref, l1_ref, on_ref, sst_ref, dcat_ref,
             dq_ref, df_ref, dv_ref, dg_ref, dl0_ref, dl1_ref, don_ref, ds_scr):
        h, r = pl.program_id(0), pl.program_id(1)

        @pl.when(r == 0)
        def _():
            ds_scr[...] = jnp.zeros_like(ds_scr)

        _, vjp = jax.vjp(_hgrn2_rows, q_ref[...], f_ref[...], v_ref[...], g_ref[...], sst_ref[...],
                         l0_ref[...], l1_ref[...], on_ref[...])
        dq, dzf, dv, dga, dst, dl0, dl1, don = vjp((dcat_ref[...], ds_scr[...]))
        dq_ref[...] = dq.astype(dq_ref.dtype)
        df_ref[...] = dzf.astype(df_ref.dtype)
        dv_ref[...] = dv.astype(dv_ref.dtype)
        dg_ref[...] = dga.astype(dg_ref.dtype)
        ds_scr[...] = dst

        @pl.when(r == 0)
        def _():
            dl0_ref[...] = dl0
            dl1_ref[...] = dl1

        @pl.when(r > 0)
        def _():
            dl0_ref[...] += dl0
            dl1_ref[...] += dl1

        first = jnp.logical_and(h == 0, r == 0)

        @pl.when(first)
        def _():
            don_ref[...] = don

        @pl.when(jnp.logical_not(first))
        def _():
            don_ref[...] += don

    def rev(col_of):
        return pl.BlockSpec((RA, HD), lambda h, r, col_of=col_of: (nb - 1 - r, col_of(h)))

    vec = pl.BlockSpec((1, HD), lambda h, r: (0, h))
    grad = _sds((T, A_HEADS * HD), BF16)
    return pl.pallas_call(
        _skip_deps(body, 9, len(deps)), name="hgrn2_bwd", grid=(A_HEADS, nb),
        in_specs=[rev(lambda h: h), rev(lambda h: 8 + h), rev(lambda h: 16 + h), rev(lambda h: 24 + h),
                  vec, vec, _full_spec((1, HD)),
                  pl.BlockSpec((None, None, HD, HD), lambda h, r: (h, nb - 1 - r, 0, 0)),
                  rev(lambda h: h)] + [ANY_SPEC] * len(deps),
        out_specs=[rev(lambda h: h)] * 4 + [vec, vec, _full_spec((1, HD))],
        out_shape=[grad] * 4 + [_sds((1, A_HEADS * HD), F32)] * 2 + [_sds((1, HD), F32)],
        scratch_shapes=[pltpu.VMEM((HD, HD), F32)],
        compiler_params=_params(2))(z, z, z, z, l0, l1, onorm, sst, dcat, *deps)


def _gmlp_specs():
    vec = pl.BlockSpec((1, HD), lambda g, r: (0, g))
    ws = pl.BlockSpec((None, B_CHUNK, B_CHUNK), lambda g, r: (g, 0, 0))
    bias = pl.BlockSpec((None, B_CHUNK, 1), lambda g, r: (g, 0, 0))
    return vec, ws, bias


def _gmlp_fwd(z, cat, lnw, lnb, ws, bias):
    vec, ws_spec, bias_spec = _gmlp_specs()

    def body(u_ref, v_ref, g_ref, lnw_ref, lnb_ref, ws_ref, bias_ref, cat_in_ref, cat_ref):
        del cat_in_ref
        out = _gmlp_rows(u_ref[...], v_ref[...], g_ref[...], lnw_ref[...], lnb_ref[...],
                         ws_ref[...], bias_ref[...])
        cat_ref[...] = out.astype(cat_ref.dtype)

    return pl.pallas_call(
        body, name="gmlp_fwd", grid=(B_GROUPS, T // RB),
        in_specs=[_col_spec(RB, lambda g: 32 + g), _col_spec(RB, lambda g: 40 + g),
                  _col_spec(RB, lambda g: 48 + g), vec, vec, ws_spec, bias_spec,
                  pl.BlockSpec(memory_space=pl.ANY)],
        out_specs=_col_spec(RB, lambda g: A_HEADS + g),
        out_shape=_sds(cat.shape, cat.dtype), input_output_aliases={7: 0},
        compiler_params=_params(2))(z, z, z, lnw, lnb, ws, bias, cat)


def _gmlp_bwd(z, lnw, lnb, ws, bias, dcat):
    vec, ws_spec, bias_spec = _gmlp_specs()

    def fn(u, vb, gb, w, b, wsv, bv, dout):
        _, vjp = jax.vjp(_gmlp_rows, u, vb, gb, w, b, wsv, bv)
        return vjp(dout)

    def body(*refs):
        ins, outs = refs[:8], refs[8:]
        res = fn(*[r[...] for r in ins])
        for o, r in zip(outs[:3], res[:3]):
            o[...] = r.astype(o.dtype)
        first = pl.program_id(1) == 0
        for o, r in zip(outs[3:], res[3:]):
            @pl.when(first)
            def _(o=o, r=r):
                o[...] = r

            @pl.when(jnp.logical_not(first))
            def _(o=o, r=r):
                o[...] += r

    grad = _sds((T, B_GROUPS * HD), BF16)
    row_out = pl.BlockSpec((RB, HD), lambda g, r: (r, g))
    return pl.pallas_call(
        body, name="gmlp_bwd", grid=(B_GROUPS, T // RB),
        in_specs=[_col_spec(RB, lambda g: 32 + g), _col_spec(RB, lambda g: 40 + g),
                  _col_spec(RB, lambda g: 48 + g), vec, vec, ws_spec, bias_spec,
                  _col_spec(RB, lambda g: A_HEADS + g)],
        out_specs=[row_out] * 3 + [vec, vec, ws_spec, bias_spec],
        out_shape=[grad] * 3 + [_sds((1, B_GROUPS * HD), F32)] * 2
        + [_sds((B_GROUPS, B_CHUNK, B_CHUNK), F32), _sds((B_GROUPS, B_CHUNK, 1), F32)],
        compiler_params=_params(2))(z, z, z, lnw, lnb, ws, bias, dcat)


def _mla_pre(z1, qn, kvn, cos_t, sin_t):
    def fn(cq, ckv, kpe, cs, sn, wq, wkv):
        return _rms(cq, wq), _rms(ckv, wkv), _rope(kpe, cs, sn)
    return _pure_call("mla_pre", fn, (T // TM,),
                      [_row_spec(TM, C_RANK, 4), _row_spec(TM, C_RANK, 5), _row_spec(TM, HD, 24),
                       _row_spec(TM, HD), _row_spec(TM, HD),
                       _full_spec((1, C_RANK)), _full_spec((1, C_RANK))],
                      [_row_spec(TM, C_RANK), _row_spec(TM, C_RANK), _row_spec(TM, HD)],
                      [_sds((T, C_RANK), BF16), _sds((T, C_RANK), BF16), _sds((T, HD), BF16)],
                      (z1, z1, z1, cos_t, sin_t, qn, kvn))


def _mla_pre_bwd(z1, qn, kvn, cos_t, sin_t, dcqn, dckvn, dkp, deps=()):
    def fn(cq, ckv, cs, sn, wq, wkv, g_q, g_kv, g_kp):
        _, vjp_q = jax.vjp(_rms, cq, wq)
        dcq, dwq = vjp_q(g_q)
        _, vjp_kv = jax.vjp(_rms, ckv, wkv)
        dckv, dwkv = vjp_kv(g_kv)
        return dcq, dckv, _rope_transpose(g_kp, cs, sn), dwq, dwkv
    return _pure_call("mla_pre_bwd", fn, (T // TM,),
                      [_row_spec(TM, C_RANK, 4), _row_spec(TM, C_RANK, 5),
                       _row_spec(TM, HD), _row_spec(TM, HD),
                       _full_spec((1, C_RANK)), _full_spec((1, C_RANK)),
                       _row_spec(TM, C_RANK), _row_spec(TM, C_RANK), _row_spec(TM, HD)],
                      [_row_spec(TM, C_RANK), _row_spec(TM, C_RANK), _row_spec(TM, HD),
                       _full_spec((1, C_RANK)), _full_spec((1, C_RANK))],
                      [_sds((T, C_RANK), BF16), _sds((T, C_RANK), BF16), _sds((T, HD), BF16),
                       _sds((1, C_RANK), F32), _sds((1, C_RANK), F32)],
                      (z1, z1, cos_t, sin_t, qn, kvn, dcqn, dckvn, dkp), n_acc=2, deps=deps)


def _gate_out(o, z1):
    def fn(ov, gate):
        return ov * _silu(gate)
    return _pure_call("gate_out", fn, (T // TM,), [_row_spec(TM, D), _row_spec(TM, D, 0)],
                      [_row_spec(TM, D)], [_sds((T, D), BF16)], (o, z1))[0]


def _gate_out_bwd(o, z1, dog, deps=()):
    def fn(ov, gate, g):
        _, vjp = jax.vjp(lambda a, b: a * _silu(b), ov, gate)
        return vjp(g)
    return _pure_call("gate_out_bwd", fn, (T // TM,),
                      [_row_spec(TM, D), _row_spec(TM, D, 0), _row_spec(TM, D)],
                      [_row_spec(TM, D), _row_spec(TM, D)],
                      [_sds((T, D), F32), _sds((T, D), BF16)], (o, z1, dog), deps=deps)


TQ = 256


def _att_scores(q_ref, cos_ref, sin_ref, kn_ref, kp_ref):
    q = q_ref[...]
    qn = q[:, :C_NOPE].astype(BF16)
    qp = _rope(q[:, C_NOPE:], cos_ref[...], sin_ref[...]).astype(BF16)
    s = (_raw_nt(qn, kn_ref[...]) + _raw_nt(qp, kp_ref[...])) * ATT_SCALE
    row = pl.program_id(1) * TQ + lax.broadcasted_iota(jnp.int32, (TQ, T), 0)
    col = lax.broadcasted_iota(jnp.int32, (TQ, T), 1)
    return qn, qp, s, col <= row


def _att_in_specs():
    return [pl.BlockSpec((TQ, QP), lambda h, i: (i, h)),
            pl.BlockSpec((TQ, HD), lambda h, i: (i, 0)),
            pl.BlockSpec((TQ, HD), lambda h, i: (i, 0)),
            pl.BlockSpec((T, C_NOPE), lambda h, i: (0, 2 * h)),
            pl.BlockSpec((T, HD), lambda h, i: (0, 0)),
            pl.BlockSpec((T, C_V), lambda h, i: (0, 2 * h + 1))]


def _attention_fwd(q, cos_t, sin_t, kv, kp):
    def body(q_ref, cos_ref, sin_ref, kn_ref, kp_ref, v_ref, o_ref, lse_ref):
        _, _, s, mask = _att_scores(q_ref, cos_ref, sin_ref, kn_ref, kp_ref)
        s = jnp.where(mask, s, jnp.finfo(F32).min)
        m = jnp.max(s, axis=-1, keepdims=True)
        p = jnp.exp(s - m)
        l = jnp.sum(p, axis=-1, keepdims=True)
        o_ref[...] = _raw_nn(p / l, v_ref[...])
        lse_ref[...] = m + jnp.log(l)

    return pl.pallas_call(
        body, name="attention_fwd", grid=(C_HEADS, T // TQ), in_specs=_att_in_specs(),
        out_specs=[pl.BlockSpec((TQ, C_V), lambda h, i: (i, h)),
                   pl.BlockSpec((None, TQ, 1), lambda h, i: (h, i, 0))],
        out_shape=[_sds((T, C_HEADS * C_V), F32), _sds((C_HEADS, T, 1), F32)],
        compiler_params=_params(2))(q, cos_t, sin_t, kv, kp, kv)


def _attention_bwd(q, cos_t, sin_t, kv, kp, o, lse, do):
    nq = T // TQ

    def body(q_ref, cos_ref, sin_ref, kn_ref, kp_ref, v_ref, o_ref, lse_ref, do_ref,
             dq_ref, dkv_ref, dkp_ref, dkn_scr, dv_scr):
        h, i = pl.program_id(0), pl.program_id(1)
        qn, qp, s, mask = _att_scores(q_ref, cos_ref, sin_ref, kn_ref, kp_ref)
        p = jnp.where(mask, jnp.exp(s - lse_ref[...]), 0.0)
        dov = do_ref[...]
        delta = jnp.sum(dov * o_ref[...], axis=-1, keepdims=True)
        dp = _raw_nt(dov, v_ref[...])
        ds = p * (dp - delta) * ATT_SCALE
        dqn = _raw_nn(ds, kn_ref[...])
        dqp = _rope_transpose(_raw_nn(ds, kp_ref[...]), cos_ref[...], sin_ref[...])
        dq_ref[...] = jnp.concatenate([dqn, dqp], axis=1).astype(dq_ref.dtype)
        dv_part = _raw_tn(p, dov)
        dkn_part = _raw_tn(ds, qn)
        dkp_part = _raw_tn(ds, qp)

        @pl.when(i == 0)
        def _():
            dv_scr[...] = dv_part
            dkn_scr[...] = dkn_part

        @pl.when(i > 0)
        def _():
            dv_scr[...] += dv_part
            dkn_scr[...] += dkn_part

        @pl.when(i == nq - 1)
        def _():
            dkv_ref[...] = jnp.concatenate([dkn_scr[...], dv_scr[...]], axis=1).astype(dkv_ref.dtype)

        first = jnp.logical_and(h == 0, i == 0)

        @pl.when(first)
        def _():
            dkp_ref[...] = dkp_part

        @pl.when(jnp.logical_not(first))
        def _():
            dkp_ref[...] += dkp_part

    return pl.pallas_call(
        body, name="attention_bwd", grid=(C_HEADS, nq),
        in_specs=_att_in_specs() + [pl.BlockSpec((TQ, C_V), lambda h, i: (i, h)),
                                    pl.BlockSpec((None, TQ, 1), lambda h, i: (h, i, 0)),
                                    pl.BlockSpec((TQ, C_V), lambda h, i: (i, h))],
        out_specs=[pl.BlockSpec((TQ, QP), lambda h, i: (i, h)),
                   pl.BlockSpec((T, C_NOPE + C_V), lambda h, i: (0, h)),
                   _full_spec((T, HD))],
        out_shape=[_sds((T, C_HEADS * QP), BF16), _sds((T, C_HEADS * (C_NOPE + C_V)), BF16),
                   _sds((T, HD), F32)],
        scratch_shapes=[pltpu.VMEM((T, C_NOPE), F32), pltpu.VMEM((T, C_V), F32)],
        compiler_params=_params(2))(q, cos_t, sin_t, kv, kp, kv, o, lse, do)


def _adamw_math(w, g, m, v):
    m = ADAM_B1 * m + (1.0 - ADAM_B1) * g
    v = ADAM_B2 * v + (1.0 - ADAM_B2) * (g * g)
    m_hat = m / (1.0 - ADAM_B1 ** ADAM_STEP)
    v_hat = v / (1.0 - ADAM_B2 ** ADAM_STEP)
    delta = -ADAM_LR * (m_hat / (jnp.sqrt(v_hat) + ADAM_EPS) + ADAM_WD * w)
    return delta, m, v


def _adamw(name, parts, w, m, v, tr):
    rows, cols = w.shape

    def fn(pv, wv, mv, vv):
        g = pv[0].astype(F32)
        for d in range(1, N_DEV):
            g = g + pv[d].astype(F32)
        return (g,) + _adamw_math(wv, g, mv, vv)

    blk = pl.BlockSpec((tr, cols), lambda i: (i, 0))
    return _pure_call(name, fn, (rows // tr,),
                      [pl.BlockSpec((N_DEV, tr, cols), lambda i: (0, i, 0)), blk, blk, blk],
                      [blk] * 4, [_sds((rows, cols), F32)] * 4, (parts, w, m, v))


def _exchange(name, arrs, gather, deps=()):
    n = len(arrs)
    deps = _live(deps)

    def body(*refs):
        ins, outs = refs[:n], refs[n + len(deps):2 * n + len(deps)]
        send_sems, recv_sems, local_sems = refs[2 * n + len(deps):]
        x, y, c = lax.axis_index("x"), lax.axis_index("y"), lax.axis_index("c")
        me = 4 * x + 2 * y + c

        def peer(k):
            return (x ^ (k >> 2), y ^ ((k >> 1) & 1), c ^ (k & 1))

        def copy(a, k):
            src = ins[a] if gather else ins[a].at[me ^ k]
            return pltpu.make_async_remote_copy(
                src_ref=src, dst_ref=outs[a].at[me], send_sem=send_sems.at[a, k - 1],
                recv_sem=recv_sems.at[a, k - 1], device_id=peer(k), device_id_type=MESH_ID)

        def arrival(a, k):
            src = ins[a] if gather else ins[a].at[me]
            return pltpu.make_async_remote_copy(
                src_ref=src, dst_ref=outs[a].at[me ^ k], send_sem=send_sems.at[a, k - 1],
                recv_sem=recv_sems.at[a, k - 1], device_id=peer(k), device_id_type=MESH_ID)

        own = [pltpu.make_async_copy(ins[a] if gather else ins[a].at[me], outs[a].at[me], local_sems.at[a])
               for a in range(n)]
        for cp in own:
            cp.start()
        for k in range(1, N_DEV):
            for a in range(n):
                copy(a, k).start()
        for k in range(1, N_DEV):
            for a in range(n):
                arrival(a, k).wait_recv()
        for k in range(1, N_DEV):
            for a in range(n):
                copy(a, k).wait_send()
        for cp in own:
            cp.wait()

    any_spec = pl.BlockSpec(memory_space=pl.ANY)
    out_shape = [_sds((N_DEV,) + a.shape if gather else a.shape, a.dtype) for a in arrs]
    return pl.pallas_call(
        body, name=name, in_specs=[any_spec] * (n + len(deps)), out_specs=[any_spec] * n, out_shape=out_shape,
        scratch_shapes=[pltpu.SemaphoreType.DMA((n, N_DEV - 1)), pltpu.SemaphoreType.DMA((n, N_DEV - 1)),
                        pltpu.SemaphoreType.DMA((n,))],
        compiler_params=pltpu.CompilerParams(has_side_effects=True))(*arrs, *deps)


HBM_SPEC = pl.BlockSpec(memory_space=pltpu.HBM)
SEM_SPEC = pl.BlockSpec(memory_space=pltpu.SEMAPHORE)
DATAFLOW = pltpu.SideEffectType.DATAFLOW_SIDE_EFFECTING


def _split_copies(ins, lands, send_sems, recv_sems, gather):
    x, y, c = lax.axis_index("x"), lax.axis_index("y"), lax.axis_index("c")
    me = 4 * x + 2 * y + c

    def copy(a, k):
        return pltpu.make_async_remote_copy(
            src_ref=ins[a] if gather else ins[a].at[me ^ k], dst_ref=lands[a].at[me],
            send_sem=send_sems.at[a * (N_DEV - 1) + k - 1], recv_sem=recv_sems.at[a * (N_DEV - 1) + k - 1],
            device_id=(x ^ (k >> 2), y ^ ((k >> 1) & 1), c ^ (k & 1)), device_id_type=MESH_ID)

    return [(a, k, copy(a, k)) for k in range(1, N_DEV) for a in range(len(ins))]


def _exchange_start(name, arrs, gather, deps=()):
    n = len(arrs)
    deps = _live(deps)
    me = 4 * lax.axis_index("x") + 2 * lax.axis_index("y") + lax.axis_index("c")
    lands = []
    for a in arrs:
        if gather:
            land = lax.dynamic_update_slice_in_dim(lax.empty((N_DEV,) + a.shape, a.dtype), a[None], me, 0)
        else:
            land = lax.dynamic_update_slice_in_dim(lax.empty(a.shape, a.dtype),
                                                   lax.dynamic_slice_in_dim(a, me, 1, 0), me, 0)
        lands.append(land)

    def body(*refs):
        ins, zones = refs[:n], refs[n:2 * n]
        send_sems, recv_sems = refs[2 * n + len(deps):2 * n + len(deps) + 2]
        token = refs[-1]
        for _, _, cp in _split_copies(ins, zones, send_sems, recv_sems, gather):
            cp.start()
        token[...] = jnp.zeros_like(token)

    hbm = lambda v: pltpu.with_memory_space_constraint(v, pltpu.HBM)
    sems = pltpu.SemaphoreType.DMA((n * (N_DEV - 1),))
    res = pl.pallas_call(
        body, name=name,
        in_specs=[HBM_SPEC] * (2 * n) + [ANY_SPEC] * len(deps),
        out_specs=[SEM_SPEC, SEM_SPEC] + [HBM_SPEC] * (2 * n) + [pl.BlockSpec(memory_space=pltpu.VMEM)],
        out_shape=[sems, sems] + [pltpu.HBM(a.shape, a.dtype) for a in arrs]
        + [pltpu.HBM(z.shape, z.dtype) for z in lands] + [_sds((8, 128), F32)],
        input_output_aliases={i: 2 + i for i in range(2 * n)},
        compiler_params=pltpu.CompilerParams(has_side_effects=DATAFLOW),
    )(*[hbm(a) for a in arrs], *[hbm(z) for z in lands], *deps)
    return (n, gather, res[0], res[1], res[2:2 + n], res[2 + n:2 + 2 * n]), res[-1]


def _exchange_wait(name, handle, after):
    n, gather, send_sems, recv_sems, srcs, lands = handle

    def body(*refs):
        ins, zones = refs[:n], refs[n:2 * n]
        s_sems, r_sems = refs[2 * n], refs[2 * n + 1]
        for _, _, cp in _split_copies(ins, zones, s_sems, r_sems, gather):
            cp.wait_send()
            cp.wait_recv()

    res = pl.pallas_call(
        body, name=name,
        in_specs=[HBM_SPEC] * (2 * n) + [SEM_SPEC, SEM_SPEC, ANY_SPEC],
        out_specs=[HBM_SPEC] * (2 * n),
        out_shape=[pltpu.HBM(a.shape, a.dtype) for a in srcs] + [pltpu.HBM(z.shape, z.dtype) for z in lands],
        input_output_aliases={i: i for i in range(2 * n)},
        compiler_params=pltpu.CompilerParams(has_side_effects=DATAFLOW),
    )(*srcs, *lands, send_sems, recv_sems, after)
    return list(res[n:])


def _pad_rope(p):
    z = jnp.zeros(p.shape[:-1] + (32,), p.dtype)
    return jnp.concatenate([p[..., :32], z, p[..., 32:], z], axis=-1)


def _unpad_rope(p):
    return jnp.concatenate([p[..., :32], p[..., 64:96]], axis=-1)


def _odd_in_layout(w):
    w = w.transpose(1, 0, 2).reshape(D, ODD_IN)
    cq, ckv, kpe, gate = w[:, :512], w[:, 512:1024], w[:, 1024:1088], w[:, 1088:]
    return jnp.concatenate([gate, cq, ckv, _pad_rope(kpe)], axis=1)


def _odd_in_unlayout(dw):
    gate, cq, ckv, kpe = dw[:, :2048], dw[:, 2048:2560], dw[:, 2560:3072], _unpad_rope(dw[:, 3072:])
    w = jnp.concatenate([cq, ckv, kpe, gate], axis=1)
    return w.reshape(D, N_DEV, ODD_IN // N_DEV).transpose(1, 0, 2)


def _qb_layout(w):
    w = w.transpose(1, 0, 2).reshape(C_RANK, C_HEADS, C_QK)
    w = jnp.concatenate([w[..., :C_NOPE], _pad_rope(w[..., C_NOPE:])], axis=-1)
    return w.reshape(C_RANK, C_HEADS * QP)


def _qb_unlayout(dw):
    dw = dw.reshape(C_RANK, C_HEADS, QP)
    dw = jnp.concatenate([dw[..., :C_NOPE], _unpad_rope(dw[..., C_NOPE:])], axis=-1)
    return dw.reshape(C_RANK, N_DEV, C_HEADS * C_QK // N_DEV).transpose(1, 0, 2)


def _rope_tables(positions):
    inv_freq = ROPE_THETA ** (-jnp.arange(0, C_ROPE, 2, dtype=F32) / C_ROPE)
    ang = positions.astype(F32)[0][:, None] * inv_freq
    cos, sin = jnp.cos(ang), jnp.sin(ang)
    z = jnp.zeros_like(cos)
    return jnp.concatenate([cos, z, cos, z], axis=1), jnp.concatenate([-sin, z, sin, z], axis=1)


SMALL_ROWS = 1144


def _pack_small(parts):
    flat = jnp.concatenate([p.reshape(-1) for p in parts])
    return jnp.pad(flat, (0, SMALL_ROWS * 128 - flat.shape[0])).reshape(SMALL_ROWS, 128)


def _unpack_small(packed, shapes):
    flat = packed.reshape(-1)
    out, off = [], 0
    for s in shapes:
        size = 1
        for d in s:
            size *= d
        out.append(flat[off:off + size].reshape(s))
        off += size
    return out


def _forward_backward(x, cos_t, sin_t, target, norm_pre, norm_post, lb_logits, a_onorm, ln_w, ln_b,
                      b_ws, b_bias, get_w, put_g, start_dep=None):
    npre0, npre1 = norm_pre[0:1], norm_pre[1:2]
    npost0, npost1 = norm_post[0:1], norm_post[1:2]
    l0, l1 = lb_logits[0:1], lb_logits[1:2]
    bias_col = b_bias.reshape(B_GROUPS, B_CHUNK, 1)
    ws = b_ws.reshape(B_GROUPS, B_CHUNK, B_CHUNK)

    h0 = _pre_norm("pre_norm0", x, npre0, deps=[start_dep])
    w_ev_in = get_w("ev_in", h0)
    z0 = _mm_nn("ev_in", h0, w_ev_in, F32, 1024, 896)
    cat, sst = _hgrn2_fwd(z0, l0, l1, a_onorm)
    cat = _gmlp_fwd(z0, cat, ln_w, ln_b, ws, bias_col)
    w_ev_out = get_w("ev_out", cat)
    y0 = _mm_nn("ev_out", cat, w_ev_out, F32, 1024, 1024)
    x1, h1 = _post_pre_norm(x, y0, npost0, npre1)
    w_od_in, w_qb, w_kvb, q_norm, kv_norm = get_w("od_mid", h1)
    z1 = _mm_nn("od_in", h1, w_od_in[None], F32, 1024, 640)
    cqn, ckvn, kp = _mla_pre(z1, q_norm, kv_norm, cos_t, sin_t)
    q = _mm_nn("od_qb", cqn, w_qb[None], F32, 1024, 1024)
    kv = _mm_nn("od_kvb", ckvn, w_kvb, BF16, 1024, 512)
    o, lse = _attention_fwd(q, cos_t, sin_t, kv, kp)
    og = _gate_out(o, z1)
    w_od_out = get_w("od_out", og)
    y1 = _mm_nn("od_out", og, w_od_out, F32, 1024, 1024)
    dx2, dy1, loss_part, dnpost1 = _final_loss(x1, y1, npost1, target)

    g_od_out = _mm_tn("od_out_dw", og, dy1, 1, BF16, 1024, 1024)
    tok = put_g("od_out", [g_od_out.reshape(N_DEV, D // N_DEV, D)])
    dog = _mm_nt("od_out_dx", dy1, w_od_out, F32, 1024, 1024, deps=[tok])
    do, dgate = _gate_out_bwd(o, z1, dog)
    dq, dkv, dkp = _attention_bwd(q, cos_t, sin_t, kv, kp, o, lse, do)
    g_qb = _mm_tn("od_qb_dw", cqn, dq, 1, F32, 512, 1024)
    g_kvb = _mm_tn("od_kvb_dw", ckvn, dkv, N_DEV, BF16, 512, 512)
    tok = put_g("od_qkv", [_qb_unlayout(g_qb[0]).astype(BF16), g_kvb])
    dcqn = _mm_nt("od_qb_dx", dq, w_qb[None], F32, 1024, 512, deps=[tok])
    dckvn = _mm_nt("od_kvb_dx", dkv, w_kvb, F32, 1024, 512)
    dcq, dckv, dkpe, dqn, dkvn = _mla_pre_bwd(z1, q_norm, kv_norm, cos_t, sin_t, dcqn, dckvn, dkp)
    dz1 = jnp.concatenate([dgate, dcq, dckv, dkpe], axis=1)
    g_od_in = _mm_tn("od_in_dw", h1, dz1, 1, F32, 1024, 640)
    tok = put_g("od_in", [_odd_in_unlayout(g_od_in[0]).astype(BF16)])
    dh1 = _mm_nt("od_in_dx", dz1, w_od_in[None], F32, 1024, 1024, deps=[tok])
    dx1, dy0, dnpost0, dnpre1 = _post_pre_norm_bwd(y0, x1, npost0, npre1, dx2, dh1)

    g_ev_out = _mm_tn("ev_out_dw", cat, dy0, 1, BF16, 1024, 1024)
    tok = put_g("ev_out", [g_ev_out.reshape(N_DEV, D // N_DEV, D)])
    dcat = _mm_nt("ev_out_dx", dy0, w_ev_out, F32, 1024, 1024, deps=[tok])
    dqa, dfa, dia, dga, dl0, dl1, donorm = _hgrn2_bwd(z0, l0, l1, a_onorm, sst, dcat)
    dub, dvb, dgb, dlnw, dlnb, dws, dbias = _gmlp_bwd(z0, ln_w, ln_b, ws, bias_col, dcat)
    dz0 = jnp.concatenate([dqa, dfa, dia, dga, dub, dvb, dgb], axis=1)
    g_ev_in = _mm_tn("ev_in_dw", h0, dz0, N_DEV, BF16, 1024, 896)
    tok = put_g("ev_in", [g_ev_in])
    dh0 = _mm_nt("ev_in_dx", dz0, w_ev_in, F32, 1024, 1024, deps=[tok])
    grad_x, dnpre0 = _pre_norm_bwd(x, npre0, dh0, dx1)

    small = (jnp.concatenate([dnpre0, dnpre1], axis=0), jnp.concatenate([dnpost0, dnpost1], axis=0),
             jnp.concatenate([dl0, dl1], axis=0), donorm, dlnw, dlnb,
             dws.reshape(1, B_GROUPS, B_CHUNK, B_CHUNK), dbias.reshape(1, B_GROUPS, B_CHUNK), dqn, dkvn)
    return loss_part[0, 0], grad_x, small


def kernel(x, positions, norm_pre, norm_post, ev_w_in, ev_lb_logits, ev_a_onorm, ev_b_ln_w, ev_b_ln_b, ev_b_ws, ev_b_bias, ev_w_out, od_w_in, od_q_norm, od_w_qb, od_kv_norm, od_w_kvb, od_w_out, loss_target, m_norm_pre, m_norm_post, m_ev_w_in, m_ev_lb_logits, m_ev_a_onorm, m_ev_b_ln_w, m_ev_b_ln_b, m_ev_b_ws, m_ev_b_bias, m_ev_w_out, m_od_w_in, m_od_q_norm, m_od_w_qb, m_od_kv_norm, m_od_w_kvb, m_od_w_out, v_norm_pre, v_norm_post, v_ev_w_in, v_ev_lb_logits, v_ev_a_onorm, v_ev_b_ln_w, v_ev_b_ln_b, v_ev_b_ws, v_ev_b_bias, v_ev_w_out, v_od_w_in, v_od_q_norm, v_od_w_qb, v_od_kv_norm, v_od_w_kvb, v_od_w_out):
    me = 4 * lax.axis_index("x") + 2 * lax.axis_index("y") + lax.axis_index("c")
    bf = lambda w: w[0].astype(BF16)

    norms = jnp.pad(jnp.concatenate([od_q_norm, od_kv_norm], axis=1), ((0, 7), (0, 0)))
    gathers = {}
    gathers["ev_in"], tok = _exchange_start("gather_ev_in", [bf(ev_w_in)], True)
    gathers["ev_out"], tok = _exchange_start("gather_ev_out", [bf(ev_w_out)], True, deps=[tok])
    gathers["od_mid"], tok = _exchange_start("gather_od_mid", [bf(od_w_in), bf(od_w_qb), bf(od_w_kvb), norms],
                                            True, deps=[tok])
    gathers["od_out"], tok = _exchange_start("gather_od_out", [bf(od_w_out)], True, deps=[tok])

    def get_w(group, after):
        got = _exchange_wait("arrived_" + group, gathers[group], after)
        if group == "ev_in":
            return got[0]
        if group in ("ev_out", "od_out"):
            return got[0].reshape(1, D, D)
        w_od_in, w_qb, w_kvb, norms_all = got
        return (_odd_in_layout(w_od_in), _qb_layout(w_qb), w_kvb,
                norms_all[:, 0, :64].reshape(1, C_RANK), norms_all[:, 0, 64:].reshape(1, C_RANK))

    scatters = {}

    def put_g(group, grads):
        scatters[group], token = _exchange_start("scatter_" + group, grads, False)
        return token

    cos_t, sin_t = _rope_tables(positions)
    loss_part, grad_x, small_g = _forward_backward(
        x[0], cos_t, sin_t, loss_target[0], norm_pre, norm_post, ev_lb_logits, ev_a_onorm, ev_b_ln_w,
        ev_b_ln_b, ev_b_ws, ev_b_bias, get_w, put_g, start_dep=tok)
    loss = lax.psum(loss_part, ("x", "y", "c"))

    big_w = {"ev_w_in": ev_w_in, "ev_w_out": ev_w_out, "od_w_in": od_w_in, "od_w_qb": od_w_qb,
             "od_w_kvb": od_w_kvb, "od_w_out": od_w_out}
    big_m = {"ev_w_in": m_ev_w_in, "ev_w_out": m_ev_w_out, "od_w_in": m_od_w_in, "od_w_qb": m_od_w_qb,
             "od_w_kvb": m_od_w_kvb, "od_w_out": m_od_w_out}
    big_v = {"ev_w_in": v_ev_w_in, "ev_w_out": v_ev_w_out, "od_w_in": v_od_w_in, "od_w_qb": v_od_w_qb,
             "od_w_kvb": v_od_w_kvb, "od_w_out": v_od_w_out}
    big_out = {}
    after = grad_x
    for group, names in (("od_out", ["od_w_out"]), ("od_qkv", ["od_w_qb", "od_w_kvb"]), ("od_in", ["od_w_in"]),
                         ("ev_out", ["ev_w_out"])):
        parts = _exchange_wait("summed_" + group, scatters[group], after)
        for nm, p in zip(names, parts):
            w = big_w[nm][0]
            big_out[nm] = [r[None] for r in _adamw("adamw_" + nm, p, w, big_m[nm][0], big_v[nm][0], w.shape[0] // 8)]
            after = big_out[nm][0]

    packed = _pack_small(small_g)
    small_all = _exchange("gather_small_grads", [packed], gather=True, deps=[after])[0]

    small_w = (norm_pre, norm_post, ev_lb_logits, ev_a_onorm, ev_b_ln_w, ev_b_ln_b, ev_b_ws, ev_b_bias)
    small_m = (m_norm_pre, m_norm_post, m_ev_lb_logits, m_ev_a_onorm, m_ev_b_ln_w, m_ev_b_ln_b, m_ev_b_ws, m_ev_b_bias)
    small_v = (v_norm_pre, v_norm_post, v_ev_lb_logits, v_ev_a_onorm, v_ev_b_ln_w, v_ev_b_ln_b, v_ev_b_ws, v_ev_b_bias)
    zero512 = jnp.zeros((1, C_RANK), F32)

    def packed_of(ws_):
        return _pack_small(list(ws_) + [zero512, zero512])

    res = _adamw("adamw_small", small_all, packed_of(small_w), packed_of(small_m), packed_of(small_v), SMALL_ROWS // 11)
    shapes = [w.shape for w in small_w] + [(1, C_RANK), (1, C_RANK)]
    g_s, d_s, m_s, v_s = (_unpack_small(r, shapes) for r in res)

    g_norms = jnp.concatenate([lax.dynamic_slice(g_s[8], (0, 64 * me), (1, 64)),
                               lax.dynamic_slice(g_s[9], (0, 64 * me), (1, 64))], axis=1)
    g_norms8 = jnp.concatenate([g_norms[None], jnp.zeros((N_DEV - 1, 1, 128), F32)], axis=0)
    res_n = _adamw("adamw_norms", g_norms8,
                   jnp.concatenate([od_q_norm, od_kv_norm], axis=1),
                   jnp.concatenate([m_od_q_norm, m_od_kv_norm], axis=1),
                   jnp.concatenate([v_od_q_norm, v_od_kv_norm], axis=1), 1)
    qn_out = [r[:, :64] for r in res_n]
    kvn_out = [r[:, 64:] for r in res_n]

    parts = _exchange_wait("summed_ev_in", scatters["ev_in"], res[0])
    w = ev_w_in[0]
    big_out["ev_w_in"] = [r[None] for r in _adamw("adamw_ev_w_in", parts[0], w, m_ev_w_in[0], v_ev_w_in[0],
                                                  w.shape[0] // 8)]

    order = ("norm_pre", "norm_post", "ev_w_in", "ev_lb_logits", "ev_a_onorm", "ev_b_ln_w", "ev_b_ln_b",
             "ev_b_ws", "ev_b_bias", "ev_w_out", "od_w_in", "od_q_norm", "od_w_qb", "od_kv_norm",
             "od_w_kvb", "od_w_out")
    small_names = ("norm_pre", "norm_post", "ev_lb_logits", "ev_a_onorm", "ev_b_ln_w", "ev_b_ln_b",
                   "ev_b_ws", "ev_b_bias")
    outs = [loss, grad_x[None]]
    for kind in range(4):
        small_kind = (g_s, d_s, m_s, v_s)[kind]
        for nm in order:
            if nm in big_out:
                outs.append(big_out[nm][kind])
            elif nm == "od_q_norm":
                outs.append(qn_out[kind])
            elif nm == "od_kv_norm":
                outs.append(kvn_out[kind])
            else:
                outs.append(small_kind[small_names.index(nm)])
    return tuple(outs)
```

```python
import functools

import jax
import jax.numpy as jnp
from jax import lax
from jax.experimental import pallas as pl
from jax.experimental.pallas import tpu as pltpu

F32 = jnp.float32
BF16 = jnp.bfloat16

N_DEV = 8
T = 2048
D = 2048
EPS = 1e-6
A_HEADS = 8
HD = 128
A_CHUNK = 64
A_SUB = 16
B_GROUPS = 8
B_CHUNK = 128
EVEN_IN = 7168
C_HEADS = 16
C_RANK = 512
C_NOPE = 128
C_ROPE = 64
C_QK = C_NOPE + C_ROPE
C_V = 128
ODD_IN = 3136
ODD_IN_PAD = 3200
QP = 256
ROPE_THETA = 10000.0
ATT_SCALE = C_QK ** -0.5

ADAM_LR = 0.001
ADAM_B1 = 0.9
ADAM_B2 = 0.999
ADAM_EPS = 1e-08
ADAM_WD = 0.01
ADAM_STEP = 10

VMEM_LIMIT_V7X = 56 * 1024 * 1024
MESH_ID = pl.DeviceIdType.MESH


def _params(n_grid):
    return pltpu.CompilerParams(dimension_semantics=("arbitrary",) * n_grid,
                                vmem_limit_bytes=VMEM_LIMIT_V7X)


def _dg(a, b, ca, cb):
    return lax.dot_general(a.astype(BF16), b.astype(BF16), (((ca,), (cb,)), ((), ())),
                           preferred_element_type=F32)


def _raw_nn(a, b):
    return _dg(a, b, 1, 0)


def _raw_nt(a, b):
    return _dg(a, b, 1, 1)


def _raw_tn(a, b):
    return _dg(a, b, 0, 0)


@jax.custom_vjp
def _dot_nn(a, b):
    return _raw_nn(a, b)


def _dot_nn_fwd(a, b):
    return _raw_nn(a, b), (a.astype(BF16), b.astype(BF16))


def _dot_nn_bwd(res, g):
    a, b = res
    return _raw_nt(g, b), _raw_tn(a, g)


_dot_nn.defvjp(_dot_nn_fwd, _dot_nn_bwd)


@jax.custom_vjp
def _dot_nt(a, b):
    return _raw_nt(a, b)


def _dot_nt_fwd(a, b):
    return _raw_nt(a, b), (a.astype(BF16), b.astype(BF16))


def _dot_nt_bwd(res, g):
    a, b = res
    return _raw_nn(g, b), _raw_tn(g, a)


_dot_nt.defvjp(_dot_nt_fwd, _dot_nt_bwd)


@jax.custom_vjp
def _dot_tn(a, b):
    return _raw_tn(a, b)


def _dot_tn_fwd(a, b):
    return _raw_tn(a, b), (a.astype(BF16), b.astype(BF16))


def _dot_tn_bwd(res, g):
    a, b = res
    return _raw_nt(b, g), _raw_nn(a, g)


_dot_tn.defvjp(_dot_tn_fwd, _dot_tn_bwd)


@jax.custom_vjp
def _sigmoid(x):
    e = jnp.exp(-jnp.abs(x))
    return jnp.where(x >= 0, 1.0 / (1.0 + e), e / (1.0 + e))


def _sigmoid_fwd(x):
    s = _sigmoid(x)
    return s, s


def _sigmoid_bwd(s, g):
    return (g * s * (1.0 - s),)


_sigmoid.defvjp(_sigmoid_fwd, _sigmoid_bwd)


def _silu(x):
    return x * _sigmoid(x)


def _rms(x, w):
    return x * lax.rsqrt(jnp.mean(x * x, axis=-1, keepdims=True) + EPS) * w


def _split3(x):
    hi = x.astype(BF16)
    r = x - hi.astype(F32)
    mid = r.astype(BF16)
    lo = (r - mid.astype(F32)).astype(BF16)
    return hi, mid, lo


def _mask_apply(mask_bf16, x, contract):
    out = None
    for piece in _split3(x):
        d = lax.dot_general(mask_bf16, piece, (((contract,), (0,)), ((), ())),
                            preferred_element_type=F32)
        out = d if out is None else out + d
    return out


def _chunk_tri(rows):
    r = lax.broadcasted_iota(jnp.int32, (rows, rows), 0)
    c = lax.broadcasted_iota(jnp.int32, (rows, rows), 1)
    return ((r >= c) & (r // A_CHUNK == c // A_CHUNK)).astype(BF16)


@jax.custom_vjp
def _chunk_cumsum(x):
    return _mask_apply(_chunk_tri(x.shape[0]), x, 1)


def _chunk_cumsum_fwd(x):
    return _chunk_cumsum(x), None


def _chunk_cumsum_bwd(_, g):
    return (_mask_apply(_chunk_tri(g.shape[0]), g, 0),)


_chunk_cumsum.defvjp(_chunk_cumsum_fwd, _chunk_cumsum_bwd)


def _hgrn2_rows(q, zf, v, ga, st, l0, l1, onorm):
    rows = q.shape[0]
    n_sub = A_CHUNK // A_SUB
    mx = jnp.maximum(l0, l1)
    e0 = jnp.exp(l0 - mx)
    e1 = jnp.exp(l1 - mx)
    lb = e0 / (e0 + e1)
    lf = jnp.log(lb + (1.0 - lb) * _sigmoid(zf))
    k = (1.0 - lb) * _sigmoid(-zf)
    b = _chunk_cumsum(lf)

    t_idx = lax.broadcasted_iota(jnp.int32, (A_CHUNK, n_sub * A_CHUNK), 0)
    c_idx = lax.broadcasted_iota(jnp.int32, (A_CHUNK, n_sub * A_CHUNK), 1)
    sel = (c_idx // A_CHUNK == t_idx // A_SUB) & (c_idx % A_CHUNK <= t_idx)
    key_row = lax.broadcasted_iota(jnp.int32, (A_CHUNK, HD), 0)

    outs = []
    for n in range(rows // A_CHUNK):
        lo = n * A_CHUNK
        qc, kc, vc = q[lo:lo + A_CHUNK], k[lo:lo + A_CHUNK], v[lo:lo + A_CHUNK]
        lfc, bc = lf[lo:lo + A_CHUNK], b[lo:lo + A_CHUNK]
        b_last = bc[A_CHUNK - 1:A_CHUNK]
        o_inter = _dot_nt(qc * jnp.exp(bc), st)
        kv_t = _dot_tn(vc, kc * jnp.exp(b_last - bc))
        st = st * jnp.exp(b_last) + kv_t
        g_rows, k_subs = [], []
        for i in range(n_sub):
            g_i = bc[i * A_SUB:i * A_SUB + 1] - lfc[i * A_SUB:i * A_SUB + 1]
            g_rows.append(jnp.broadcast_to(g_i, (A_SUB, HD)))
            expo = jnp.where(key_row < (i + 1) * A_SUB, g_i - bc, -jnp.inf)
            k_subs.append(kc * jnp.exp(expo))
        q_sub = qc * jnp.exp(bc - jnp.concatenate(g_rows, axis=0))
        scores = _dot_nt(q_sub, jnp.concatenate(k_subs, axis=0))
        scores = jnp.where(sel, scores, 0.0)
        o_intra = _dot_nn(scores, jnp.concatenate([vc] * n_sub, axis=0))
        outs.append(o_inter + o_intra)
    o = jnp.concatenate(outs, axis=0)
    return _rms(o, onorm) * _silu(ga), st


def _gmlp_rows(u, vb, gb, lnw, lnb, ws, bias):
    rows = u.shape[0]
    mu = jnp.mean(vb, axis=-1, keepdims=True)
    xc = vb - mu
    vg = xc * lax.rsqrt(jnp.mean(xc * xc, axis=-1, keepdims=True) + EPS) * lnw + lnb
    r = lax.broadcasted_iota(jnp.int32, (B_CHUNK, B_CHUNK), 0)
    c = lax.broadcasted_iota(jnp.int32, (B_CHUNK, B_CHUNK), 1)
    ws_causal = jnp.where(r >= c, ws, 0.0)
    svs = [_dot_nn(ws_causal, vg[n * B_CHUNK:(n + 1) * B_CHUNK]) + bias
           for n in range(rows // B_CHUNK)]
    return u * jnp.concatenate(svs, axis=0) * _silu(gb)


def _rope(x, cos_t, sin_t):
    return x * cos_t + pltpu.roll(x, 64, 1) * sin_t


def _rope_transpose(g, cos_t, sin_t):
    return g * cos_t + pltpu.roll(g * sin_t, 64, 1)


ANY_SPEC = pl.BlockSpec(memory_space=pl.ANY)


def _live(deps):
    return [d for d in deps if d is not None]


def _skip_deps(body, n_in, n_deps):
    def wrapped(*refs):
        return body(*refs[:n_in], *refs[n_in + n_deps:])
    return wrapped


def _pure_call(name, fn, grid, in_specs, out_specs, out_shape, args, n_acc=0, deps=()):
    deps = _live(deps)
    n_in, n_out, n_deps = len(in_specs), len(out_specs), len(deps)
    in_specs = list(in_specs) + [ANY_SPEC] * n_deps
    args = tuple(args) + tuple(deps)

    def body(*refs):
        res = fn(*[r[...] for r in refs[:n_in]])
        if not isinstance(res, (tuple, list)):
            res = (res,)
        outs = refs[n_in + n_deps:n_in + n_deps + n_out]
        for o, r in zip(outs[:n_out - n_acc], res[:n_out - n_acc]):
            o[...] = r.astype(o.dtype)
        if n_acc:
            first = functools.reduce(jnp.logical_and, [pl.program_id(i) == 0 for i in range(len(grid))])
            for o, r in zip(outs[n_out - n_acc:], res[n_out - n_acc:]):
                @pl.when(first)
                def _(o=o, r=r):
                    o[...] = r.astype(o.dtype)

                @pl.when(jnp.logical_not(first))
                def _(o=o, r=r):
                    o[...] += r.astype(o.dtype)

    return pl.pallas_call(body, name=name, grid=grid, in_specs=in_specs, out_specs=out_specs,
                          out_shape=out_shape, compiler_params=_params(len(grid)))(*args)


def _sds(shape, dtype):
    return jax.ShapeDtypeStruct(shape, dtype)


def _row_spec(tm, width, col=0):
    return pl.BlockSpec((tm, width), lambda i, col=col: (i, col))


def _full_spec(shape):
    nd = len(shape)
    return pl.BlockSpec(shape, lambda *_: (0,) * nd)


def _mm_nn(name, a, b, out_dtype, tm, tn, deps=()):
    deps = _live(deps)
    m, k = a.shape
    j, _, n = b.shape
    per = n // tn

    def body(a_ref, b_ref, o_ref):
        o_ref[...] = _raw_nn(a_ref[...], b_ref[...]).astype(o_ref.dtype)

    return pl.pallas_call(
        _skip_deps(body, 2, len(deps)), name=name, grid=(m // tm, j * per),
        in_specs=[pl.BlockSpec((tm, k), lambda i, c: (i, 0)),
                  pl.BlockSpec((None, k, tn), lambda i, c: (c // per, 0, c % per))] + [ANY_SPEC] * len(deps),
        out_specs=pl.BlockSpec((tm, tn), lambda i, c: (i, c)),
        out_shape=_sds((m, j * n), out_dtype), compiler_params=_params(2))(a, b, *deps)


def _mm_nt(name, a, b, out_dtype, tm, tn, deps=()):
    deps = _live(deps)
    m = a.shape[0]
    j, nn, n = b.shape

    def body(a_ref, b_ref, o_ref, acc_ref):
        part = _raw_nt(a_ref[...], b_ref[...])
        if j == 1:
            o_ref[...] = part.astype(o_ref.dtype)
        else:
            kk = pl.program_id(2)

            @pl.when(kk == 0)
            def _():
                acc_ref[...] = part

            @pl.when(kk > 0)
            def _():
                acc_ref[...] += part

            @pl.when(kk == j - 1)
            def _():
                o_ref[...] = acc_ref[...].astype(o_ref.dtype)

    acc_shape = (tm, tn) if j > 1 else (8, 128)
    return pl.pallas_call(
        _skip_deps(body, 2, len(deps)), name=name, grid=(m // tm, nn // tn, j),
        in_specs=[pl.BlockSpec((tm, n), lambda i, c, kk: (i, kk)),
                  pl.BlockSpec((None, tn, n), lambda i, c, kk: (kk, c, 0))] + [ANY_SPEC] * len(deps),
        out_specs=pl.BlockSpec((tm, tn), lambda i, c, kk: (i, c)),
        out_shape=_sds((m, nn), out_dtype),
        scratch_shapes=[pltpu.VMEM(acc_shape, F32)], compiler_params=_params(3))(a, b, *deps)


def _mm_tn(name, a, b, j, out_dtype, tm, tn, deps=()):
    deps = _live(deps)
    k, m = a.shape
    n = b.shape[1] // j
    per = n // tn

    def body(a_ref, b_ref, o_ref):
        o_ref[...] = _raw_tn(a_ref[...], b_ref[...]).astype(o_ref.dtype)

    return pl.pallas_call(
        _skip_deps(body, 2, len(deps)), name=name, grid=(m // tm, j * per),
        in_specs=[pl.BlockSpec((k, tm), lambda i, c: (0, i)),
                  pl.BlockSpec((k, tn), lambda i, c: (0, c))] + [ANY_SPEC] * len(deps),
        out_specs=pl.BlockSpec((None, tm, tn), lambda i, c: (c // per, i, c % per)),
        out_shape=_sds((j, m, n), out_dtype), compiler_params=_params(2))(a, b, *deps)


TM = 256


def _pre_norm(name, x, w_row, deps=()):
    def fn(xv, w):
        return _rms(xv, w)
    return _pure_call(name, fn, (T // TM,), [_row_spec(TM, D), _full_spec((1, D))],
                      [_row_spec(TM, D)], [_sds((T, D), BF16)], (x, w_row), deps=deps)[0]


def _post_pre_norm(x, y, w_post, w_pre):
    def fn(xv, yv, wp, wn):
        x1 = xv + _rms(yv, wp)
        return x1, _rms(x1, wn)
    return _pure_call("post_pre_norm", fn, (T // TM,),
                      [_row_spec(TM, D), _row_spec(TM, D), _full_spec((1, D)), _full_spec((1, D))],
                      [_row_spec(TM, D), _row_spec(TM, D)],
                      [_sds((T, D), F32), _sds((T, D), BF16)], (x, y, w_post, w_pre))


def _post_pre_norm_bwd(y, x1, w_post, w_pre, dx1_in, dh1, deps=()):
    def fn(yv, x1v, wp, wn, dx1v, dh1v):
        _, vjp_pre = jax.vjp(_rms, x1v, wn)
        dx1_h, dwn = vjp_pre(dh1v)
        dx1 = dx1v + dx1_h
        _, vjp_post = jax.vjp(_rms, yv, wp)
        dy, dwp = vjp_post(dx1)
        return dx1, dy, dwp, dwn
    return _pure_call("post_pre_norm_bwd", fn, (T // TM,),
                      [_row_spec(TM, D), _row_spec(TM, D), _full_spec((1, D)), _full_spec((1, D)),
                       _row_spec(TM, D), _row_spec(TM, D)],
                      [_row_spec(TM, D), _row_spec(TM, D), _full_spec((1, D)), _full_spec((1, D))],
                      [_sds((T, D), F32), _sds((T, D), BF16), _sds((1, D), F32), _sds((1, D), F32)],
                      (y, x1, w_post, w_pre, dx1_in, dh1), n_acc=2, deps=deps)


def _final_loss(x1, y, w_post, target):
    def fn(x1v, yv, wp, tv):
        r, vjp = jax.vjp(_rms, yv, wp)
        err = x1v + r - tv
        part = 0.5 * jnp.sum(jnp.mean(err * err, axis=-1, keepdims=True), axis=0, keepdims=True)
        dx2 = err * (1.0 / D)
        dy, dwp = vjp(dx2)
        return dx2, dy, jnp.broadcast_to(part, (1, 128)), dwp
    return _pure_call("final_loss", fn, (T // TM,),
                      [_row_spec(TM, D), _row_spec(TM, D), _full_spec((1, D)), _row_spec(TM, D)],
                      [_row_spec(TM, D), _row_spec(TM, D), _full_spec((1, 128)), _full_spec((1, D))],
                      [_sds((T, D), F32), _sds((T, D), BF16), _sds((1, 128), F32), _sds((1, D), F32)],
                      (x1, y, w_post, target), n_acc=2)


def _pre_norm_bwd(x, w_row, dh, dx_res, deps=()):
    def fn(xv, w, dhv, dxv):
        _, vjp = jax.vjp(_rms, xv, w)
        dx, dw = vjp(dhv)
        return dxv + dx, dw
    return _pure_call("pre_norm_bwd", fn, (T // TM,),
                      [_row_spec(TM, D), _full_spec((1, D)), _row_spec(TM, D), _row_spec(TM, D)],
                      [_row_spec(TM, D), _full_spec((1, D))],
                      [_sds((T, D), F32), _sds((1, D), F32)], (x, w_row, dh, dx_res), n_acc=1, deps=deps)


RA = 256
RB = 512


def _col_spec(rows, col_of):
    return pl.BlockSpec((rows, HD), lambda h, r, col_of=col_of: (r, col_of(h)))


def _hgrn2_fwd(z, l0, l1, onorm):
    nb = T // RA

    def body(q_ref, f_ref, v_ref, g_ref, l0_ref, l1_ref, on_ref, cat_ref, sst_ref, st_scr):
        @pl.when(pl.program_id(1) == 0)
        def _():
            st_scr[...] = jnp.zeros_like(st_scr)

        st = st_scr[...]
        sst_ref[...] = st
        out, st_new = _hgrn2_rows(q_ref[...], f_ref[...], v_ref[...], g_ref[...], st,
                                  l0_ref[...], l1_ref[...], on_ref[...])
        cat_ref[...] = out.astype(cat_ref.dtype)
        st_scr[...] = st_new

    vec = pl.BlockSpec((1, HD), lambda h, r: (0, h))
    return pl.pallas_call(
        body, name="hgrn2_fwd", grid=(A_HEADS, nb),
        in_specs=[_col_spec(RA, lambda h: h), _col_spec(RA, lambda h: 8 + h),
                  _col_spec(RA, lambda h: 16 + h), _col_spec(RA, lambda h: 24 + h),
                  vec, vec, _full_spec((1, HD))],
        out_specs=[_col_spec(RA, lambda h: h),
                   pl.BlockSpec((None, None, HD, HD), lambda h, r: (h, r, 0, 0))],
        out_shape=[_sds((T, 2 * A_HEADS * HD), BF16), _sds((A_HEADS, nb, HD, HD), F32)],
        scratch_shapes=[pltpu.VMEM((HD, HD), F32)],
        compiler_params=_params(2))(z, z, z, z, l0, l1, onorm)


def _hgrn2_bwd(z, l0, l1, onorm, sst, dcat, deps=()):
    nb = T // RA
    deps = _live(deps)

    def body(q_ref, f_ref, v_ref, g_ref, l0_ref, l1_ref, on_ref, sst_ref, dcat_ref,
             dq_ref, df_ref, dv_ref, dg_ref, dl0_ref, dl1_ref, don_ref, ds_scr):
        h, r = pl.program_id(0), pl.program_id(1)

        @pl.when(r == 0)
        def _():
            ds_scr[...] = jnp.zeros_like(ds_scr)

        _, vjp = jax.vjp(_hgrn2_rows, q_ref[...], f_ref[...], v_ref[...], g_ref[...], sst_ref[...],
                         l0_ref[...], l1_ref[...], on_ref[...])
        dq, dzf, dv, dga, dst, dl0, dl1, don = vjp((dcat_ref[...], ds_scr[...]))
        dq_ref[...] = dq.astype(dq_ref.dtype)
        df_ref[...] = dzf.astype(df_ref.dtype)
        dv_ref[...] = dv.astype(dv_ref.dtype)
        dg_ref[...] = dga.astype(dg_ref.dtype)
        ds_scr[...] = dst

        @pl.when(r == 0)
        def _():
            dl0_ref[...] = dl0
            dl1_ref[...] = dl1

        @pl.when(r > 0)
        def _():
            dl0_ref[...] += dl0
            dl1_ref[...] += dl1

        first = jnp.logical_and(h == 0, r == 0)

        @pl.when(first)
        def _():
            don_ref[...] = don

        @pl.when(jnp.logical_not(first))
        def _():
            don_ref[...] += don

    def rev(col_of):
        return pl.BlockSpec((RA, HD), lambda h, r, col_of=col_of: (nb - 1 - r, col_of(h)))

    vec = pl.BlockSpec((1, HD), lambda h, r: (0, h))
    grad = _sds((T, A_HEADS * HD), BF16)
    return pl.pallas_call(
        _skip_deps(body, 9, len(deps)), name="hgrn2_bwd", grid=(A_HEADS, nb),
        in_specs=[rev(lambda h: h), rev(lambda h: 8 + h), rev(lambda h: 16 + h), rev(lambda h: 24 + h),
                  vec, vec, _full_spec((1, HD)),
                  pl.BlockSpec((None, None, HD, HD), lambda h, r: (h, nb - 1 - r, 0, 0)),
                  rev(lambda h: h)] + [ANY_SPEC] * len(deps),
        out_specs=[rev(lambda h: h)] * 4 + [vec, vec, _full_spec((1, HD))],
        out_shape=[grad] * 4 + [_sds((1, A_HEADS * HD), F32)] * 2 + [_sds((1, HD), F32)],
        scratch_shapes=[pltpu.VMEM((HD, HD), F32)],
        compiler_params=_params(2))(z, z, z, z, l0, l1, onorm, sst, dcat, *deps)


def _gmlp_specs():
    vec = pl.BlockSpec((1, HD), lambda g, r: (0, g))
    ws = pl.BlockSpec((None, B_CHUNK, B_CHUNK), lambda g, r: (g, 0, 0))
    bias = pl.BlockSpec((None, B_CHUNK, 1), lambda g, r: (g, 0, 0))
    return vec, ws, bias


def _gmlp_fwd(z, cat, lnw, lnb, ws, bias):
    vec, ws_spec, bias_spec = _gmlp_specs()

    def body(u_ref, v_ref, g_ref, lnw_ref, lnb_ref, ws_ref, bias_ref, cat_in_ref, cat_ref):
        del cat_in_ref
        out = _gmlp_rows(u_ref[...], v_ref[...], g_ref[...], lnw_ref[...], lnb_ref[...],
                         ws_ref[...], bias_ref[...])
        cat_ref[...] = out.astype(cat_ref.dtype)

    return pl.pallas_call(
        body, name="gmlp_fwd", grid=(B_GROUPS, T // RB),
        in_specs=[_col_spec(RB, lambda g: 32 + g), _col_spec(RB, lambda g: 40 + g),
                  _col_spec(RB, lambda g: 48 + g), vec, vec, ws_spec, bias_spec,
                  pl.BlockSpec(memory_space=pl.ANY)],
        out_specs=_col_spec(RB, lambda g: A_HEADS + g),
        out_shape=_sds(cat.shape, cat.dtype), input_output_aliases={7: 0},
        compiler_params=_params(2))(z, z, z, lnw, lnb, ws, bias, cat)


def _gmlp_bwd(z, lnw, lnb, ws, bias, dcat):
    vec, ws_spec, bias_spec = _gmlp_specs()

    def fn(u, vb, gb, w, b, wsv, bv, dout):
        _, vjp = jax.vjp(_gmlp_rows, u, vb, gb, w, b, wsv, bv)
        return vjp(dout)

    def body(*refs):
        ins, outs = refs[:8], refs[8:]
        res = fn(*[r[...] for r in ins])
        for o, r in zip(outs[:3], res[:3]):
            o[...] = r.astype(o.dtype)
        first = pl.program_id(1) == 0
        for o, r in zip(outs[3:], res[3:]):
            @pl.when(first)
            def _(o=o, r=r):
                o[...] = r

            @pl.when(jnp.logical_not(first))
            def _(o=o, r=r):
                o[...] += r

    grad = _sds((T, B_GROUPS * HD), BF16)
    row_out = pl.BlockSpec((RB, HD), lambda g, r: (r, g))
    return pl.pallas_call(
        body, name="gmlp_bwd", grid=(B_GROUPS, T // RB),
        in_specs=[_col_spec(RB, lambda g: 32 + g), _col_spec(RB, lambda g: 40 + g),
                  _col_spec(RB, lambda g: 48 + g), vec, vec, ws_spec, bias_spec,
                  _col_spec(RB, lambda g: A_HEADS + g)],
        out_specs=[row_out] * 3 + [vec, vec, ws_spec, bias_spec],
        out_shape=[grad] * 3 + [_sds((1, B_GROUPS * HD), F32)] * 2
        + [_sds((B_GROUPS, B_CHUNK, B_CHUNK), F32), _sds((B_GROUPS, B_CHUNK, 1), F32)],
        compiler_params=_params(2))(z, z, z, lnw, lnb, ws, bias, dcat)


def _mla_pre(z1, qn, kvn, cos_t, sin_t):
    def fn(cq, ckv, kpe, cs, sn, wq, wkv):
        return _rms(cq, wq), _rms(ckv, wkv), _rope(kpe, cs, sn)
    return _pure_call("mla_pre", fn, (T // TM,),
                      [_row_spec(TM, C_RANK, 4), _row_spec(TM, C_RANK, 5), _row_spec(TM, HD, 24),
                       _row_spec(TM, HD), _row_spec(TM, HD),
                       _full_spec((1, C_RANK)), _full_spec((1, C_RANK))],
                      [_row_spec(TM, C_RANK), _row_spec(TM, C_RANK), _row_spec(TM, HD)],
                      [_sds((T, C_RANK), BF16), _sds((T, C_RANK), BF16), _sds((T, HD), BF16)],
                      (z1, z1, z1, cos_t, sin_t, qn, kvn))


def _mla_pre_bwd(z1, qn, kvn, cos_t, sin_t, dcqn, dckvn, dkp, deps=()):
    def fn(cq, ckv, cs, sn, wq, wkv, g_q, g_kv, g_kp):
        _, vjp_q = jax.vjp(_rms, cq, wq)
        dcq, dwq = vjp_q(g_q)
        _, vjp_kv = jax.vjp(_rms, ckv, wkv)
        dckv, dwkv = vjp_kv(g_kv)
        return dcq, dckv, _rope_transpose(g_kp, cs, sn), dwq, dwkv
    return _pure_call("mla_pre_bwd", fn, (T // TM,),
                      [_row_spec(TM, C_RANK, 4), _row_spec(TM, C_RANK, 5),
                       _row_spec(TM, HD), _row_spec(TM, HD),
                       _full_spec((1, C_RANK)), _full_spec((1, C_RANK)),
                       _row_spec(TM, C_RANK), _row_spec(TM, C_RANK), _row_spec(TM, HD)],
                      [_row_spec(TM, C_RANK), _row_spec(TM, C_RANK), _row_spec(TM, HD),
                       _full_spec((1, C_RANK)), _full_spec((1, C_RANK))],
                      [_sds((T, C_RANK), BF16), _sds((T, C_RANK), BF16), _sds((T, HD), BF16),
                       _sds((1, C_RANK), F32), _sds((1, C_RANK), F32)],
                      (z1, z1, cos_t, sin_t, qn, kvn, dcqn, dckvn, dkp), n_acc=2, deps=deps)


def _gate_out(o, z1):
    def fn(ov, gate):
        return ov * _silu(gate)
    return _pure_call("gate_out", fn, (T // TM,), [_row_spec(TM, D), _row_spec(TM, D, 0)],
                      [_row_spec(TM, D)], [_sds((T, D), BF16)], (o, z1))[0]


def _gate_out_bwd(o, z1, dog, deps=()):
    def fn(ov, gate, g):
        _, vjp = jax.vjp(lambda a, b: a * _silu(b), ov, gate)
        return vjp(g)
    return _pure_call("gate_out_bwd", fn, (T // TM,),
                      [_row_spec(TM, D), _row_spec(TM, D, 0), _row_spec(TM, D)],
                      [_row_spec(TM, D), _row_spec(TM, D)],
                      [_sds((T, D), F32), _sds((T, D), BF16)], (o, z1, dog), deps=deps)


TQ = 256


def _att_scores(q_ref, cos_ref, sin_ref, kn_ref, kp_ref, n):
    keys = (n + 1) * TQ
    q = q_ref[...]
    qn = q[:, :C_NOPE].astype(BF16)
    qp = _rope(q[:, C_NOPE:], cos_ref[...], sin_ref[...]).astype(BF16)
    s = (_raw_nt(qn, kn_ref[0:keys, :]) + _raw_nt(qp, kp_ref[0:keys, :])) * ATT_SCALE
    row = n * TQ + lax.broadcasted_iota(jnp.int32, (TQ, keys), 0)
    col = lax.broadcasted_iota(jnp.int32, (TQ, keys), 1)
    return qn, qp, s, col <= row


def _per_query_block(fn):
    for n in range(T // TQ):
        pl.when(pl.program_id(1) == n)(functools.partial(fn, n))


def _att_in_specs():
    return [pl.BlockSpec((TQ, QP), lambda h, i: (i, h)),
            pl.BlockSpec((TQ, HD), lambda h, i: (i, 0)),
            pl.BlockSpec((TQ, HD), lambda h, i: (i, 0)),
            pl.BlockSpec((T, C_NOPE), lambda h, i: (0, 2 * h)),
            pl.BlockSpec((T, HD), lambda h, i: (0, 0)),
            pl.BlockSpec((T, C_V), lambda h, i: (0, 2 * h + 1))]


def _attention_fwd(q, cos_t, sin_t, kv, kp):
    def body(q_ref, cos_ref, sin_ref, kn_ref, kp_ref, v_ref, o_ref, lse_ref):
        def block(n):
            _, _, s, mask = _att_scores(q_ref, cos_ref, sin_ref, kn_ref, kp_ref, n)
            s = jnp.where(mask, s, jnp.finfo(F32).min)
            m = jnp.max(s, axis=-1, keepdims=True)
            p = jnp.exp(s - m)
            l = jnp.sum(p, axis=-1, keepdims=True)
            o_ref[...] = _raw_nn(p / l, v_ref[0:(n + 1) * TQ, :])
            lse_ref[...] = m + jnp.log(l)

        _per_query_block(block)

    return pl.pallas_call(
        body, name="attention_fwd", grid=(C_HEADS, T // TQ), in_specs=_att_in_specs(),
        out_specs=[pl.BlockSpec((TQ, C_V), lambda h, i: (i, h)),
                   pl.BlockSpec((None, TQ, 1), lambda h, i: (h, i, 0))],
        out_shape=[_sds((T, C_HEADS * C_V), F32), _sds((C_HEADS, T, 1), F32)],
        compiler_params=_params(2))(q, cos_t, sin_t, kv, kp, kv)


def _attention_bwd(q, cos_t, sin_t, kv, kp, o, lse, do):
    nq = T // TQ

    def body(q_ref, cos_ref, sin_ref, kn_ref, kp_ref, v_ref, o_ref, lse_ref, do_ref,
             dq_ref, dkv_ref, dkp_ref, dkn_scr, dv_scr):
        h, i = pl.program_id(0), pl.program_id(1)

        @pl.when(i == 0)
        def _():
            dv_scr[...] = jnp.zeros_like(dv_scr)
            dkn_scr[...] = jnp.zeros_like(dkn_scr)

        @pl.when(jnp.logical_and(h == 0, i == 0))
        def _():
            dkp_ref[...] = jnp.zeros_like(dkp_ref)

        def block(n):
            keys = (n + 1) * TQ
            qn, qp, s, mask = _att_scores(q_ref, cos_ref, sin_ref, kn_ref, kp_ref, n)
            p = jnp.where(mask, jnp.exp(s - lse_ref[...]), 0.0)
            dov = do_ref[...]
            delta = jnp.sum(dov * o_ref[...], axis=-1, keepdims=True)
            dp = _raw_nt(dov, v_ref[0:keys, :])
            ds = p * (dp - delta) * ATT_SCALE
            dqn = _raw_nn(ds, kn_ref[0:keys, :])
            dqp = _rope_transpose(_raw_nn(ds, kp_ref[0:keys, :]), cos_ref[...], sin_ref[...])
            dq_ref[...] = jnp.concatenate([dqn, dqp], axis=1).astype(dq_ref.dtype)
            dv_scr[0:keys, :] += _raw_tn(p, dov)
            dkn_scr[0:keys, :] += _raw_tn(ds, qn)
            dkp_ref[0:keys, :] += _raw_tn(ds, qp)

        _per_query_block(block)

        @pl.when(i == nq - 1)
        def _():
            dkv_ref[...] = jnp.concatenate([dkn_scr[...], dv_scr[...]], axis=1).astype(dkv_ref.dtype)

    return pl.pallas_call(
        body, name="attention_bwd", grid=(C_HEADS, nq),
        in_specs=_att_in_specs() + [pl.BlockSpec((TQ, C_V), lambda h, i: (i, h)),
                                    pl.BlockSpec((None, TQ, 1), lambda h, i: (h, i, 0)),
                                    pl.BlockSpec((TQ, C_V), lambda h, i: (i, h))],
        out_specs=[pl.BlockSpec((TQ, QP), lambda h, i: (i, h)),
                   pl.BlockSpec((T, C_NOPE + C_V), lambda h, i: (0, h)),
                   _full_spec((T, HD))],
        out_shape=[_sds((T, C_HEADS * QP), BF16), _sds((T, C_HEADS * (C_NOPE + C_V)), BF16),
                   _sds((T, HD), F32)],
        scratch_shapes=[pltpu.VMEM((T, C_NOPE), F32), pltpu.VMEM((T, C_V), F32)],
        compiler_params=_params(2))(q, cos_t, sin_t, kv, kp, kv, o, lse, do)


def _adamw_math(w, g, m, v):
    m = ADAM_B1 * m + (1.0 - ADAM_B1) * g
    v = ADAM_B2 * v + (1.0 - ADAM_B2) * (g * g)
    m_hat = m / (1.0 - ADAM_B1 ** ADAM_STEP)
    v_hat = v / (1.0 - ADAM_B2 ** ADAM_STEP)
    delta = -ADAM_LR * (m_hat / (jnp.sqrt(v_hat) + ADAM_EPS) + ADAM_WD * w)
    return delta, m, v


def _adamw(name, parts, w, m, v, tr):
    rows, cols = w.shape

    def fn(pv, wv, mv, vv):
        g = pv[0].astype(F32)
        for d in range(1, N_DEV):
            g = g + pv[d].astype(F32)
        return (g,) + _adamw_math(wv, g, mv, vv)

    blk = pl.BlockSpec((tr, cols), lambda i: (i, 0))
    return _pure_call(name, fn, (rows // tr,),
                      [pl.BlockSpec((N_DEV, tr, cols), lambda i: (0, i, 0)), blk, blk, blk],
                      [blk] * 4, [_sds((rows, cols), F32)] * 4, (parts, w, m, v))


def _exchange(name, arrs, gather, deps=()):
    n = len(arrs)
    deps = _live(deps)

    def body(*refs):
        ins, outs = refs[:n], refs[n + len(deps):2 * n + len(deps)]
        send_sems, recv_sems, local_sems = refs[2 * n + len(deps):]
        x, y, c = lax.axis_index("x"), lax.axis_index("y"), lax.axis_index("c")
        me = 4 * x + 2 * y + c

        def peer(k):
            return (x ^ (k >> 2), y ^ ((k >> 1) & 1), c ^ (k & 1))

        def copy(a, k):
            src = ins[a] if gather else ins[a].at[me ^ k]
            return pltpu.make_async_remote_copy(
                src_ref=src, dst_ref=outs[a].at[me], send_sem=send_sems.at[a, k - 1],
                recv_sem=recv_sems.at[a, k - 1], device_id=peer(k), device_id_type=MESH_ID)

        def arrival(a, k):
            src = ins[a] if gather else ins[a].at[me]
            return pltpu.make_async_remote_copy(
                src_ref=src, dst_ref=outs[a].at[me ^ k], send_sem=send_sems.at[a, k - 1],
                recv_sem=recv_sems.at[a, k - 1], device_id=peer(k), device_id_type=MESH_ID)

        own = [pltpu.make_async_copy(ins[a] if gather else ins[a].at[me], outs[a].at[me], local_sems.at[a])
               for a in range(n)]
        for cp in own:
            cp.start()
        for k in range(1, N_DEV):
            for a in range(n):
                copy(a, k).start()
        for k in range(1, N_DEV):
            for a in range(n):
                arrival(a, k).wait_recv()
        for k in range(1, N_DEV):
            for a in range(n):
                copy(a, k).wait_send()
        for cp in own:
            cp.wait()

    any_spec = pl.BlockSpec(memory_space=pl.ANY)
    out_shape = [_sds((N_DEV,) + a.shape if gather else a.shape, a.dtype) for a in arrs]
    return pl.pallas_call(
        body, name=name, in_specs=[any_spec] * (n + len(deps)), out_specs=[any_spec] * n, out_shape=out_shape,
        scratch_shapes=[pltpu.SemaphoreType.DMA((n, N_DEV - 1)), pltpu.SemaphoreType.DMA((n, N_DEV - 1)),
                        pltpu.SemaphoreType.DMA((n,))],
        compiler_params=pltpu.CompilerParams(has_side_effects=True))(*arrs, *deps)


HBM_SPEC = pl.BlockSpec(memory_space=pltpu.HBM)
SEM_SPEC = pl.BlockSpec(memory_space=pltpu.SEMAPHORE)
DATAFLOW = pltpu.SideEffectType.DATAFLOW_SIDE_EFFECTING


def _split_copies(ins, lands, send_sems, recv_sems, gather):
    x, y, c = lax.axis_index("x"), lax.axis_index("y"), lax.axis_index("c")
    me = 4 * x + 2 * y + c

    def copy(a, k):
        return pltpu.make_async_remote_copy(
            src_ref=ins[a] if gather else ins[a].at[me ^ k], dst_ref=lands[a].at[me],
            send_sem=send_sems.at[a * (N_DEV - 1) + k - 1], recv_sem=recv_sems.at[a * (N_DEV - 1) + k - 1],
            device_id=(x ^ (k >> 2), y ^ ((k >> 1) & 1), c ^ (k & 1)), device_id_type=MESH_ID)

    return [(a, k, copy(a, k)) for k in range(1, N_DEV) for a in range(len(ins))]


def _exchange_start(name, arrs, gather, deps=()):
    n = len(arrs)
    deps = _live(deps)
    me = 4 * lax.axis_index("x") + 2 * lax.axis_index("y") + lax.axis_index("c")
    lands = []
    for a in arrs:
        if gather:
            land = lax.dynamic_update_slice_in_dim(lax.empty((N_DEV,) + a.shape, a.dtype), a[None], me, 0)
        else:
            land = lax.dynamic_update_slice_in_dim(lax.empty(a.shape, a.dtype),
                                                   lax.dynamic_slice_in_dim(a, me, 1, 0), me, 0)
        lands.append(land)

    def body(*refs):
        ins, zones = refs[:n], refs[n:2 * n]
        send_sems, recv_sems = refs[2 * n + len(deps):2 * n + len(deps) + 2]
        token = refs[-1]
        for _, _, cp in _split_copies(ins, zones, send_sems, recv_sems, gather):
            cp.start()
        token[...] = jnp.zeros_like(token)

    hbm = lambda v: pltpu.with_memory_space_constraint(v, pltpu.HBM)
    sems = pltpu.SemaphoreType.DMA((n * (N_DEV - 1),))
    res = pl.pallas_call(
        body, name=name,
        in_specs=[HBM_SPEC] * (2 * n) + [ANY_SPEC] * len(deps),
        out_specs=[SEM_SPEC, SEM_SPEC] + [HBM_SPEC] * (2 * n) + [pl.BlockSpec(memory_space=pltpu.VMEM)],
        out_shape=[sems, sems] + [pltpu.HBM(a.shape, a.dtype) for a in arrs]
        + [pltpu.HBM(z.shape, z.dtype) for z in lands] + [_sds((8, 128), F32)],
        input_output_aliases={i: 2 + i for i in range(2 * n)},
        compiler_params=pltpu.CompilerParams(has_side_effects=DATAFLOW),
    )(*[hbm(a) for a in arrs], *[hbm(z) for z in lands], *deps)
    return (n, gather, res[0], res[1], res[2:2 + n], res[2 + n:2 + 2 * n]), res[-1]


def _exchange_wait(name, handle, after):
    n, gather, send_sems, recv_sems, srcs, lands = handle

    def body(*refs):
        ins, zones = refs[:n], refs[n:2 * n]
        s_sems, r_sems = refs[2 * n], refs[2 * n + 1]
        for _, _, cp in _split_copies(ins, zones, s_sems, r_sems, gather):
            cp.wait_send()
            cp.wait_recv()

    res = pl.pallas_call(
        body, name=name,
        in_specs=[HBM_SPEC] * (2 * n) + [SEM_SPEC, SEM_SPEC, ANY_SPEC],
        out_specs=[HBM_SPEC] * (2 * n),
        out_shape=[pltpu.HBM(a.shape, a.dtype) for a in srcs] + [pltpu.HBM(z.shape, z.dtype) for z in lands],
        input_output_aliases={i: i for i in range(2 * n)},
        compiler_params=pltpu.CompilerParams(has_side_effects=DATAFLOW),
    )(*srcs, *lands, send_sems, recv_sems, after)
    return list(res[n:])


def _pad_rope(p):
    z = jnp.zeros(p.shape[:-1] + (32,), p.dtype)
    return jnp.concatenate([p[..., :32], z, p[..., 32:], z], axis=-1)


def _unpad_rope(p):
    return jnp.concatenate([p[..., :32], p[..., 64:96]], axis=-1)


def _odd_in_layout(w):
    w = w.transpose(1, 0, 2).reshape(D, ODD_IN)
    cq, ckv, kpe, gate = w[:, :512], w[:, 512:1024], w[:, 1024:1088], w[:, 1088:]
    return jnp.concatenate([gate, cq, ckv, _pad_rope(kpe)], axis=1)


def _odd_in_unlayout(dw):
    gate, cq, ckv, kpe = dw[:, :2048], dw[:, 2048:2560], dw[:, 2560:3072], _unpad_rope(dw[:, 3072:])
    w = jnp.concatenate([cq, ckv, kpe, gate], axis=1)
    return w.reshape(D, N_DEV, ODD_IN // N_DEV).transpose(1, 0, 2)


def _qb_layout(w):
    w = w.transpose(1, 0, 2).reshape(C_RANK, C_HEADS, C_QK)
    w = jnp.concatenate([w[..., :C_NOPE], _pad_rope(w[..., C_NOPE:])], axis=-1)
    return w.reshape(C_RANK, C_HEADS * QP)


def _qb_unlayout(dw):
    dw = dw.reshape(C_RANK, C_HEADS, QP)
    dw = jnp.concatenate([dw[..., :C_NOPE], _unpad_rope(dw[..., C_NOPE:])], axis=-1)
    return dw.reshape(C_RANK, N_DEV, C_HEADS * C_QK // N_DEV).transpose(1, 0, 2)


def _rope_tables(positions):
    inv_freq = ROPE_THETA ** (-jnp.arange(0, C_ROPE, 2, dtype=F32) / C_ROPE)
    ang = positions.astype(F32)[0][:, None] * inv_freq
    cos, sin = jnp.cos(ang), jnp.sin(ang)
    z = jnp.zeros_like(cos)
    return jnp.concatenate([cos, z, cos, z], axis=1), jnp.concatenate([-sin, z, sin, z], axis=1)


SMALL_ROWS = 1144


def _pack_small(parts):
    flat = jnp.concatenate([p.reshape(-1) for p in parts])
    return jnp.pad(flat, (0, SMALL_ROWS * 128 - flat.shape[0])).reshape(SMALL_ROWS, 128)


def _unpack_small(packed, shapes):
    flat = packed.reshape(-1)
    out, off = [], 0
    for s in shapes:
        size = 1
        for d in s:
            size *= d
        out.append(flat[off:off + size].reshape(s))
        off += size
    return out


def _forward_backward(x, cos_t, sin_t, target, norm_pre, norm_post, lb_logits, a_onorm, ln_w, ln_b,
                      b_ws, b_bias, get_w, put_g, start_dep=None):
    npre0, npre1 = norm_pre[0:1], norm_pre[1:2]
    npost0, npost1 = norm_post[0:1], norm_post[1:2]
    l0, l1 = lb_logits[0:1], lb_logits[1:2]
    bias_col = b_bias.reshape(B_GROUPS, B_CHUNK, 1)
    ws = b_ws.reshape(B_GROUPS, B_CHUNK, B_CHUNK)

    h0 = _pre_norm("pre_norm0", x, npre0, deps=[start_dep])
    w_ev_in = get_w("ev_in", h0)
    z0 = _mm_nn("ev_in", h0, w_ev_in, F32, 1024, 896)
    cat, sst = _hgrn2_fwd(z0, l0, l1, a_onorm)
    cat = _gmlp_fwd(z0, cat, ln_w, ln_b, ws, bias_col)
    w_ev_out = get_w("ev_out", cat)
    y0 = _mm_nn("ev_out", cat, w_ev_out, F32, 1024, 1024)
    x1, h1 = _post_pre_norm(x, y0, npost0, npre1)
    w_od_in, w_qb, w_kvb, q_norm, kv_norm = get_w("od_mid", h1)
    z1 = _mm_nn("od_in", h1, w_od_in[None], F32, 1024, 640)
    cqn, ckvn, kp = _mla_pre(z1, q_norm, kv_norm, cos_t, sin_t)
    q = _mm_nn("od_qb", cqn, w_qb[None], F32, 1024, 1024)
    kv = _mm_nn("od_kvb", ckvn, w_kvb, BF16, 1024, 512)
    o, lse = _attention_fwd(q, cos_t, sin_t, kv, kp)
    og = _gate_out(o, z1)
    w_od_out = get_w("od_out", og)
    y1 = _mm_nn("od_out", og, w_od_out, F32, 1024, 1024)
    dx2, dy1, loss_part, dnpost1 = _final_loss(x1, y1, npost1, target)

    g_od_out = _mm_tn("od_out_dw", og, dy1, 1, BF16, 1024, 1024)
    tok = put_g("od_out", [g_od_out.reshape(N_DEV, D // N_DEV, D)])
    dog = _mm_nt("od_out_dx", dy1, w_od_out, F32, 1024, 1024, deps=[tok])
    do, dgate = _gate_out_bwd(o, z1, dog)
    dq, dkv, dkp = _attention_bwd(q, cos_t, sin_t, kv, kp, o, lse, do)
    g_qb = _mm_tn("od_qb_dw", cqn, dq, 1, F32, 512, 1024)
    g_kvb = _mm_tn("od_kvb_dw", ckvn, dkv, N_DEV, BF16, 512, 512)
    tok = put_g("od_qkv", [_qb_unlayout(g_qb[0]).astype(BF16), g_kvb])
    dcqn = _mm_nt("od_qb_dx", dq, w_qb[None], F32, 1024, 512, deps=[tok])
    dckvn = _mm_nt("od_kvb_dx", dkv, w_kvb, F32, 1024, 512)
    dcq, dckv, dkpe, dqn, dkvn = _mla_pre_bwd(z1, q_norm, kv_norm, cos_t, sin_t, dcqn, dckvn, dkp)
    dz1 = jnp.concatenate([dgate, dcq, dckv, dkpe], axis=1)
    g_od_in = _mm_tn("od_in_dw", h1, dz1, 1, F32, 1024, 640)
    tok = put_g("od_in", [_odd_in_unlayout(g_od_in[0]).astype(BF16)])
    dh1 = _mm_nt("od_in_dx", dz1, w_od_in[None], F32, 1024, 1024, deps=[tok])
    dx1, dy0, dnpost0, dnpre1 = _post_pre_norm_bwd(y0, x1, npost0, npre1, dx2, dh1)

    g_ev_out = _mm_tn("ev_out_dw", cat, dy0, 1, BF16, 1024, 1024)
    tok = put_g("ev_out", [g_ev_out.reshape(N_DEV, D // N_DEV, D)])
    dcat = _mm_nt("ev_out_dx", dy0, w_ev_out, F32, 1024, 1024, deps=[tok])
    dqa, dfa, dia, dga, dl0, dl1, donorm = _hgrn2_bwd(z0, l0, l1, a_onorm, sst, dcat)
    dub, dvb, dgb, dlnw, dlnb, dws, dbias = _gmlp_bwd(z0, ln_w, ln_b, ws, bias_col, dcat)
    dz0 = jnp.concatenate([dqa, dfa, dia, dga, dub, dvb, dgb], axis=1)
    g_ev_in = _mm_tn("ev_in_dw", h0, dz0, N_DEV, BF16, 1024, 896)
    tok = put_g("ev_in", [g_ev_in])
    dh0 = _mm_nt("ev_in_dx", dz0, w_ev_in, F32, 1024, 1024, deps=[tok])
    grad_x, dnpre0 = _pre_norm_bwd(x, npre0, dh0, dx1)

    small = (jnp.concatenate([dnpre0, dnpre1], axis=0), jnp.concatenate([dnpost0, dnpost1], axis=0),
             jnp.concatenate([dl0, dl1], axis=0), donorm, dlnw, dlnb,
             dws.reshape(1, B_GROUPS, B_CHUNK, B_CHUNK), dbias.reshape(1, B_GROUPS, B_CHUNK), dqn, dkvn)
    return loss_part[0, 0], grad_x, small


def kernel(x, positions, norm_pre, norm_post, ev_w_in, ev_lb_logits, ev_a_onorm, ev_b_ln_w, ev_b_ln_b, ev_b_ws, ev_b_bias, ev_w_out, od_w_in, od_q_norm, od_w_qb, od_kv_norm, od_w_kvb, od_w_out, loss_target, m_norm_pre, m_norm_post, m_ev_w_in, m_ev_lb_logits, m_ev_a_onorm, m_ev_b_ln_w, m_ev_b_ln_b, m_ev_b_ws, m_ev_b_bias, m_ev_w_out, m_od_w_in, m_od_q_norm, m_od_w_qb, m_od_kv_norm, m_od_w_kvb, m_od_w_out, v_norm_pre, v_norm_post, v_ev_w_in, v_ev_lb_logits, v_ev_a_onorm, v_ev_b_ln_w, v_ev_b_ln_b, v_ev_b_ws, v_ev_b_bias, v_ev_w_out, v_od_w_in, v_od_q_norm, v_od_w_qb, v_od_kv_norm, v_od_w_kvb, v_od_w_out):
    me = 4 * lax.axis_index("x") + 2 * lax.axis_index("y") + lax.axis_index("c")
    bf = lambda w: w[0].astype(BF16)

    norms = jnp.pad(jnp.concatenate([od_q_norm, od_kv_norm], axis=1), ((0, 7), (0, 0)))
    gathers = {}
    gathers["ev_in"], tok = _exchange_start("gather_ev_in", [bf(ev_w_in)], True)
    gathers["ev_out"], tok = _exchange_start("gather_ev_out", [bf(ev_w_out)], True, deps=[tok])
    gathers["od_mid"], tok = _exchange_start("gather_od_mid", [bf(od_w_in), bf(od_w_qb), bf(od_w_kvb), norms],
                                            True, deps=[tok])
    gathers["od_out"], tok = _exchange_start("gather_od_out", [bf(od_w_out)], True, deps=[tok])

    def get_w(group, after):
        got = _exchange_wait("arrived_" + group, gathers[group], after)
        if group == "ev_in":
            return got[0]
        if group in ("ev_out", "od_out"):
            return got[0].reshape(1, D, D)
        w_od_in, w_qb, w_kvb, norms_all = got
        return (_odd_in_layout(w_od_in), _qb_layout(w_qb), w_kvb,
                norms_all[:, 0, :64].reshape(1, C_RANK), norms_all[:, 0, 64:].reshape(1, C_RANK))

    scatters = {}

    def put_g(group, grads):
        scatters[group], token = _exchange_start("scatter_" + group, grads, False)
        return token

    cos_t, sin_t = _rope_tables(positions)
    loss_part, grad_x, small_g = _forward_backward(
        x[0], cos_t, sin_t, loss_target[0], norm_pre, norm_post, ev_lb_logits, ev_a_onorm, ev_b_ln_w,
        ev_b_ln_b, ev_b_ws, ev_b_bias, get_w, put_g, start_dep=tok)
    loss = lax.psum(loss_part, ("x", "y", "c"))

    big_w = {"ev_w_in": ev_w_in, "ev_w_out": ev_w_out, "od_w_in": od_w_in, "od_w_qb": od_w_qb,
             "od_w_kvb": od_w_kvb, "od_w_out": od_w_out}
    big_m = {"ev_w_in": m_ev_w_in, "ev_w_out": m_ev_w_out, "od_w_in": m_od_w_in, "od_w_qb": m_od_w_qb,
             "od_w_kvb": m_od_w_kvb, "od_w_out": m_od_w_out}
    big_v = {"ev_w_in": v_ev_w_in, "ev_w_out": v_ev_w_out, "od_w_in": v_od_w_in, "od_w_qb": v_od_w_qb,
             "od_w_kvb": v_od_w_kvb, "od_w_out": v_od_w_out}
    big_out = {}
    after = grad_x
    for group, names in (("od_out", ["od_w_out"]), ("od_qkv", ["od_w_qb", "od_w_kvb"]), ("od_in", ["od_w_in"]),
                         ("ev_out", ["ev_w_out"])):
        parts = _exchange_wait("summed_" + group, scatters[group], after)
        for nm, p in zip(names, parts):
            w = big_w[nm][0]
            big_out[nm] = [r[None] for r in _adamw("adamw_" + nm, p, w, big_m[nm][0], big_v[nm][0], w.shape[0] // 8)]
            after = big_out[nm][0]

    packed = _pack_small(small_g)
    small_all = _exchange("gather_small_grads", [packed], gather=True, deps=[after])[0]

    small_w = (norm_pre, norm_post, ev_lb_logits, ev_a_onorm, ev_b_ln_w, ev_b_ln_b, ev_b_ws, ev_b_bias)
    small_m = (m_norm_pre, m_norm_post, m_ev_lb_logits, m_ev_a_onorm, m_ev_b_ln_w, m_ev_b_ln_b, m_ev_b_ws, m_ev_b_bias)
    small_v = (v_norm_pre, v_norm_post, v_ev_lb_logits, v_ev_a_onorm, v_ev_b_ln_w, v_ev_b_ln_b, v_ev_b_ws, v_ev_b_bias)
    zero512 = jnp.zeros((1, C_RANK), F32)

    def packed_of(ws_):
        return _pack_small(list(ws_) + [zero512, zero512])

    res = _adamw("adamw_small", small_all, packed_of(small_w), packed_of(small_m), packed_of(small_v), SMALL_ROWS // 11)
    shapes = [w.shape for w in small_w] + [(1, C_RANK), (1, C_RANK)]
    g_s, d_s, m_s, v_s = (_unpack_small(r, shapes) for r in res)

    g_norms = jnp.concatenate([lax.dynamic_slice(g_s[8], (0, 64 * me), (1, 64)),
                               lax.dynamic_slice(g_s[9], (0, 64 * me), (1, 64))], axis=1)
    g_norms8 = jnp.concatenate([g_norms[None], jnp.zeros((N_DEV - 1, 1, 128), F32)], axis=0)
    res_n = _adamw("adamw_norms", g_norms8,
                   jnp.concatenate([od_q_norm, od_kv_norm], axis=1),
                   jnp.concatenate([m_od_q_norm, m_od_kv_norm], axis=1),
                   jnp.concatenate([v_od_q_norm, v_od_kv_norm], axis=1), 1)
    qn_out = [r[:, :64] for r in res_n]
    kvn_out = [r[:, 64:] for r in res_n]

    parts = _exchange_wait("summed_ev_in", scatters["ev_in"], res[0])
    w = ev_w_in[0]
    big_out["ev_w_in"] = [r[None] for r in _adamw("adamw_ev_w_in", parts[0], w, m_ev_w_in[0], v_ev_w_in[0],
                                                  w.shape[0] // 8)]

    order = ("norm_pre", "norm_post", "ev_w_in", "ev_lb_logits", "ev_a_onorm", "ev_b_ln_w", "ev_b_ln_b",
             "ev_b_ws", "ev_b_bias", "ev_w_out", "od_w_in", "od_q_norm", "od_w_qb", "od_kv_norm",
             "od_w_kvb", "od_w_out")
    small_names = ("norm_pre", "norm_post", "ev_lb_logits", "ev_a_onorm", "ev_b_ln_w", "ev_b_ln_b",
                   "ev_b_ws", "ev_b_bias")
    outs = [loss, grad_x[None]]
    for kind in range(4):
        small_kind = (g_s, d_s, m_s, v_s)[kind]
        for nm in order:
            if nm in big_out:
                outs.append(big_out[nm][kind])
            elif nm == "od_q_norm":
                outs.append(qn_out[kind])
            elif nm == "od_kv_norm":
                outs.append(kvn_out[kind])
            else:
                outs.append(small_kind[small_names.index(nm)])
    return tuple(outs)
```

```python
import functools

import jax
import jax.numpy as jnp
from jax import lax
from jax.experimental import pallas as pl
from jax.experimental.pallas import tpu as pltpu

F32 = jnp.float32
BF16 = jnp.bfloat16

N_DEV = 8
T = 2048
D = 2048
EPS = 1e-6
A_HEADS = 8
HD = 128
A_CHUNK = 64
A_SUB = 16
B_GROUPS = 8
B_CHUNK = 128
EVEN_IN = 7168
C_HEADS = 16
C_RANK = 512
C_NOPE = 128
C_ROPE = 64
C_QK = C_NOPE + C_ROPE
C_V = 128
ODD_IN = 3136
ODD_IN_PAD = 3200
QP = 256
ROPE_THETA = 10000.0
ATT_SCALE = C_QK ** -0.5

ADAM_LR = 0.001
ADAM_B1 = 0.9
ADAM_B2 = 0.999
ADAM_EPS = 1e-08
ADAM_WD = 0.01
ADAM_STEP = 10

VMEM_LIMIT_V7X = 56 * 1024 * 1024
MESH_ID = pl.DeviceIdType.MESH


def _params(n_grid):
    return pltpu.CompilerParams(dimension_semantics=("arbitrary",) * n_grid,
                                vmem_limit_bytes=VMEM_LIMIT_V7X)


def _dg(a, b, ca, cb):
    return lax.dot_general(a.astype(BF16), b.astype(BF16), (((ca,), (cb,)), ((), ())),
                           preferred_element_type=F32)


def _raw_nn(a, b):
    return _dg(a, b, 1, 0)


def _raw_nt(a, b):
    return _dg(a, b, 1, 1)


def _raw_tn(a, b):
    return _dg(a, b, 0, 0)


@jax.custom_vjp
def _dot_nn(a, b):
    return _raw_nn(a, b)


def _dot_nn_fwd(a, b):
    return _raw_nn(a, b), (a.astype(BF16), b.astype(BF16))


def _dot_nn_bwd(res, g):
    a, b = res
    return _raw_nt(g, b), _raw_tn(a, g)


_dot_nn.defvjp(_dot_nn_fwd, _dot_nn_bwd)


@jax.custom_vjp
def _dot_nt(a, b):
    return _raw_nt(a, b)


def _dot_nt_fwd(a, b):
    return _raw_nt(a, b), (a.astype(BF16), b.astype(BF16))


def _dot_nt_bwd(res, g):
    a, b = res
    return _raw_nn(g, b), _raw_tn(g, a)


_dot_nt.defvjp(_dot_nt_fwd, _dot_nt_bwd)


@jax.custom_vjp
def _dot_tn(a, b):
    return _raw_tn(a, b)


def _dot_tn_fwd(a, b):
    return _raw_tn(a, b), (a.astype(BF16), b.astype(BF16))


def _dot_tn_bwd(res, g):
    a, b = res
    return _raw_nt(b, g), _raw_nn(a, g)


_dot_tn.defvjp(_dot_tn_fwd, _dot_tn_bwd)


@jax.custom_vjp
def _sigmoid(x):
    e = jnp.exp(-jnp.abs(x))
    return jnp.where(x >= 0, 1.0 / (1.0 + e), e / (1.0 + e))


def _sigmoid_fwd(x):
    s = _sigmoid(x)
    return s, s


def _sigmoid_bwd(s, g):
    return (g * s * (1.0 - s),)


_sigmoid.defvjp(_sigmoid_fwd, _sigmoid_bwd)


def _silu(x):
    return x * _sigmoid(x)


def _rms(x, w):
    return x * lax.rsqrt(jnp.mean(x * x, axis=-1, keepdims=True) + EPS) * w


def _split3(x):
    hi = x.astype(BF16)
    r = x - hi.astype(F32)
    mid = r.astype(BF16)
    lo = (r - mid.astype(F32)).astype(BF16)
    return hi, mid, lo


def _mask_apply(mask_bf16, x, contract):
    out = None
    for piece in _split3(x):
        d = lax.dot_general(mask_bf16, piece, (((contract,), (0,)), ((), ())),
                            preferred_element_type=F32)
        out = d if out is None else out + d
    return out


def _chunk_tri(rows):
    r = lax.broadcasted_iota(jnp.int32, (rows, rows), 0)
    c = lax.broadcasted_iota(jnp.int32, (rows, rows), 1)
    return ((r >= c) & (r // A_CHUNK == c // A_CHUNK)).astype(BF16)


@jax.custom_vjp
def _chunk_cumsum(x):
    return _mask_apply(_chunk_tri(x.shape[0]), x, 1)


def _chunk_cumsum_fwd(x):
    return _chunk_cumsum(x), None


def _chunk_cumsum_bwd(_, g):
    return (_mask_apply(_chunk_tri(g.shape[0]), g, 0),)


_chunk_cumsum.defvjp(_chunk_cumsum_fwd, _chunk_cumsum_bwd)


def _hgrn2_rows(q, zf, v, ga, st, l0, l1, onorm):
    rows = q.shape[0]
    n_sub = A_CHUNK // A_SUB
    mx = jnp.maximum(l0, l1)
    e0 = jnp.exp(l0 - mx)
    e1 = jnp.exp(l1 - mx)
    lb = e0 / (e0 + e1)
    lf = jnp.log(lb + (1.0 - lb) * _sigmoid(zf))
    k = (1.0 - lb) * _sigmoid(-zf)
    b = _chunk_cumsum(lf)

    t_idx = lax.broadcasted_iota(jnp.int32, (A_CHUNK, n_sub * A_CHUNK), 0)
    c_idx = lax.broadcasted_iota(jnp.int32, (A_CHUNK, n_sub * A_CHUNK), 1)
    sel = (c_idx // A_CHUNK == t_idx // A_SUB) & (c_idx % A_CHUNK <= t_idx)
    key_row = lax.broadcasted_iota(jnp.int32, (A_CHUNK, HD), 0)

    outs = []
    for n in range(rows // A_CHUNK):
        lo = n * A_CHUNK
        qc, kc, vc = q[lo:lo + A_CHUNK], k[lo:lo + A_CHUNK], v[lo:lo + A_CHUNK]
        lfc, bc = lf[lo:lo + A_CHUNK], b[lo:lo + A_CHUNK]
        b_last = bc[A_CHUNK - 1:A_CHUNK]
        o_inter = _dot_nt(qc * jnp.exp(bc), st)
        kv_t = _dot_tn(vc, kc * jnp.exp(b_last - bc))
        st = st * jnp.exp(b_last) + kv_t
        g_rows, k_subs = [], []
        for i in range(n_sub):
            g_i = bc[i * A_SUB:i * A_SUB + 1] - lfc[i * A_SUB:i * A_SUB + 1]
            g_rows.append(jnp.broadcast_to(g_i, (A_SUB, HD)))
            expo = jnp.where(key_row < (i + 1) * A_SUB, g_i - bc, -jnp.inf)
            k_subs.append(kc * jnp.exp(expo))
        q_sub = qc * jnp.exp(bc - jnp.concatenate(g_rows, axis=0))
        scores = _dot_nt(q_sub, jnp.concatenate(k_subs, axis=0))
        scores = jnp.where(sel, scores, 0.0)
        o_intra = _dot_nn(scores, jnp.concatenate([vc] * n_sub, axis=0))
        outs.append(o_inter + o_intra)
    o = jnp.concatenate(outs, axis=0)
    return _rms(o, onorm) * _silu(ga), st


def _gmlp_rows(u, vb, gb, lnw, lnb, ws, bias):
    rows = u.shape[0]
    mu = jnp.mean(vb, axis=-1, keepdims=True)
    xc = vb - mu
    vg = xc * lax.rsqrt(jnp.mean(xc * xc, axis=-1, keepdims=True) + EPS) * lnw + lnb
    r = lax.broadcasted_iota(jnp.int32, (B_CHUNK, B_CHUNK), 0)
    c = lax.broadcasted_iota(jnp.int32, (B_CHUNK, B_CHUNK), 1)
    ws_causal = jnp.where(r >= c, ws, 0.0)
    svs = [_dot_nn(ws_causal, vg[n * B_CHUNK:(n + 1) * B_CHUNK]) + bias
           for n in range(rows // B_CHUNK)]
    return u * jnp.concatenate(svs, axis=0) * _silu(gb)


def _rope(x, cos_t, sin_t):
    return x * cos_t + pltpu.roll(x, 64, 1) * sin_t


def _rope_transpose(g, cos_t, sin_t):
    return g * cos_t + pltpu.roll(g * sin_t, 64, 1)


ANY_SPEC = pl.BlockSpec(memory_space=pl.ANY)


def _live(deps):
    return [d for d in deps if d is not None]


def _skip_deps(body, n_in, n_deps):
    def wrapped(*refs):
        return body(*refs[:n_in], *refs[n_in + n_deps:])
    return wrapped


def _pure_call(name, fn, grid, in_specs, out_specs, out_shape, args, n_acc=0, deps=()):
    deps = _live(deps)
    n_in, n_out, n_deps = len(in_specs), len(out_specs), len(deps)
    in_specs = list(in_specs) + [ANY_SPEC] * n_deps
    args = tuple(args) + tuple(deps)

    def body(*refs):
        res = fn(*[r[...] for r in refs[:n_in]])
        if not isinstance(res, (tuple, list)):
            res = (res,)
        outs = refs[n_in + n_deps:n_in + n_deps + n_out]
        for o, r in zip(outs[:n_out - n_acc], res[:n_out - n_acc]):
            o[...] = r.astype(o.dtype)
        if n_acc:
            first = functools.reduce(jnp.logical_and, [pl.program_id(i) == 0 for i in range(len(grid))])
            for o, r in zip(outs[n_out - n_acc:], res[n_out - n_acc:]):
                @pl.when(first)
                def _(o=o, r=r):
                    o[...] = r.astype(o.dtype)

                @pl.when(jnp.logical_not(first))
                def _(o=o, r=r):
                    o[...] += r.astype(o.dtype)

    return pl.pallas_call(body, name=name, grid=grid, in_specs=in_specs, out_specs=out_specs,
                          out_shape=out_shape, compiler_params=_params(len(grid)))(*args)


def _sds(shape, dtype):
    return jax.ShapeDtypeStruct(shape, dtype)


def _row_spec(tm, width, col=0):
    return pl.BlockSpec((tm, width), lambda i, col=col: (i, col))


def _full_spec(shape):
    nd = len(shape)
    return pl.BlockSpec(shape, lambda *_: (0,) * nd)


def _mm_nn(name, a, b, out_dtype, tm, tn, deps=()):
    deps = _live(deps)
    m, k = a.shape
    j, _, n = b.shape
    per = n // tn

    def body(a_ref, b_ref, o_ref):
        o_ref[...] = _raw_nn(a_ref[...], b_ref[...]).astype(o_ref.dtype)

    return pl.pallas_call(
        _skip_deps(body, 2, len(deps)), name=name, grid=(m // tm, j * per),
        in_specs=[pl.BlockSpec((tm, k), lambda i, c: (i, 0)),
                  pl.BlockSpec((None, k, tn), lambda i, c: (c // per, 0, c % per))] + [ANY_SPEC] * len(deps),
        out_specs=pl.BlockSpec((tm, tn), lambda i, c: (i, c)),
        out_shape=_sds((m, j * n), out_dtype), compiler_params=_params(2))(a, b, *deps)


def _mm_nt(name, a, b, out_dtype, tm, tn, deps=()):
    deps = _live(deps)
    m = a.shape[0]
    j, nn, n = b.shape

    def body(a_ref, b_ref, o_ref, acc_ref):
        part = _raw_nt(a_ref[...], b_ref[...])
        if j == 1:
            o_ref[...] = part.astype(o_ref.dtype)
        else:
            kk = pl.program_id(2)

            @pl.when(kk == 0)
            def _():
                acc_ref[...] = part

            @pl.when(kk > 0)
            def _():
                acc_ref[...] += part

            @pl.when(kk == j - 1)
            def _():
                o_ref[...] = acc_ref[...].astype(o_ref.dtype)

    acc_shape = (tm, tn) if j > 1 else (8, 128)
    return pl.pallas_call(
        _skip_deps(body, 2, len(deps)), name=name, grid=(m // tm, nn // tn, j),
        in_specs=[pl.BlockSpec((tm, n), lambda i, c, kk: (i, kk)),
                  pl.BlockSpec((None, tn, n), lambda i, c, kk: (kk, c, 0))] + [ANY_SPEC] * len(deps),
        out_specs=pl.BlockSpec((tm, tn), lambda i, c, kk: (i, c)),
        out_shape=_sds((m, nn), out_dtype),
        scratch_shapes=[pltpu.VMEM(acc_shape, F32)], compiler_params=_params(3))(a, b, *deps)


def _mm_tn(name, a, b, j, out_dtype, tm, tn, deps=()):
    deps = _live(deps)
    k, m = a.shape
    n = b.shape[1] // j
    per = n // tn

    def body(a_ref, b_ref, o_ref):
        o_ref[...] = _raw_tn(a_ref[...], b_ref[...]).astype(o_ref.dtype)

    return pl.pallas_call(
        _skip_deps(body, 2, len(deps)), name=name, grid=(m // tm, j * per),
        in_specs=[pl.BlockSpec((k, tm), lambda i, c: (0, i)),
                  pl.BlockSpec((k, tn), lambda i, c: (0, c))] + [ANY_SPEC] * len(deps),
        out_specs=pl.BlockSpec((None, tm, tn), lambda i, c: (c // per, i, c % per)),
        out_shape=_sds((j, m, n), out_dtype), compiler_params=_params(2))(a, b, *deps)


TM = 256


def _pre_norm(name, x, w_row, deps=()):
    def fn(xv, w):
        return _rms(xv, w)
    return _pure_call(name, fn, (T // TM,), [_row_spec(TM, D), _full_spec((1, D))],
                      [_row_spec(TM, D)], [_sds((T, D), BF16)], (x, w_row), deps=deps)[0]


def _post_pre_norm(x, y, w_post, w_pre):
    def fn(xv, yv, wp, wn):
        x1 = xv + _rms(yv, wp)
        return x1, _rms(x1, wn)
    return _pure_call("post_pre_norm", fn, (T // TM,),
                      [_row_spec(TM, D), _row_spec(TM, D), _full_spec((1, D)), _full_spec((1, D))],
                      [_row_spec(TM, D), _row_spec(TM, D)],
                      [_sds((T, D), F32), _sds((T, D), BF16)], (x, y, w_post, w_pre))


def _post_pre_norm_bwd(y, x1, w_post, w_pre, dx1_in, dh1, deps=()):
    def fn(yv, x1v, wp, wn, dx1v, dh1v):
        _, vjp_pre = jax.vjp(_rms, x1v, wn)
        dx1_h, dwn = vjp_pre(dh1v)
        dx1 = dx1v + dx1_h
        _, vjp_post = jax.vjp(_rms, yv, wp)
        dy, dwp = vjp_post(dx1)
        return dx1, dy, dwp, dwn
    return _pure_call("post_pre_norm_bwd", fn, (T // TM,),
                      [_row_spec(TM, D), _row_spec(TM, D), _full_spec((1, D)), _full_spec((1, D)),
                       _row_spec(TM, D), _row_spec(TM, D)],
                      [_row_spec(TM, D), _row_spec(TM, D), _full_spec((1, D)), _full_spec((1, D))],
                      [_sds((T, D), F32), _sds((T, D), BF16), _sds((1, D), F32), _sds((1, D), F32)],
                      (y, x1, w_post, w_pre, dx1_in, dh1), n_acc=2, deps=deps)


def _final_loss(x1, y, w_post, target):
    def fn(x1v, yv, wp, tv):
        r, vjp = jax.vjp(_rms, yv, wp)
        err = x1v + r - tv
        part = 0.5 * jnp.sum(jnp.mean(err * err, axis=-1, keepdims=True), axis=0, keepdims=True)
        dx2 = err * (1.0 / D)
        dy, dwp = vjp(dx2)
        return dx2, dy, jnp.broadcast_to(part, (1, 128)), dwp
    return _pure_call("final_loss", fn, (T // TM,),
                      [_row_spec(TM, D), _row_spec(TM, D), _full_spec((1, D)), _row_spec(TM, D)],
                      [_row_spec(TM, D), _row_spec(TM, D), _full_spec((1, 128)), _full_spec((1, D))],
                      [_sds((T, D), F32), _sds((T, D), BF16), _sds((1, 128), F32), _sds((1, D), F32)],
                      (x1, y, w_post, target), n_acc=2)


def _pre_norm_bwd(x, w_row, dh, dx_res, deps=()):
    def fn(xv, w, dhv, dxv):
        _, vjp = jax.vjp(_rms, xv, w)
        dx, dw = vjp(dhv)
        return dxv + dx, dw
    return _pure_call("pre_norm_bwd", fn, (T // TM,),
                      [_row_spec(TM, D), _full_spec((1, D)), _row_spec(TM, D), _row_spec(TM, D)],
                      [_row_spec(TM, D), _full_spec((1, D))],
                      [_sds((T, D), F32), _sds((1, D), F32)], (x, w_row, dh, dx_res), n_acc=1, deps=deps)


RA = 256
RB = 512


def _col_spec(rows, col_of):
    return pl.BlockSpec((rows, HD), lambda h, r, col_of=col_of: (r, col_of(h)))


def _hgrn2_fwd(z, l0, l1, onorm):
    nb = T // RA

    def body(q_ref, f_ref, v_ref, g_ref, l0_ref, l1_ref, on_ref, cat_ref, sst_ref, st_scr):
        @pl.when(pl.program_id(1) == 0)
        def _():
            st_scr[...] = jnp.zeros_like(st_scr)

        st = st_scr[...]
        sst_ref[...] = st
        out, st_new = _hgrn2_rows(q_ref[...], f_ref[...], v_ref[...], g_ref[...], st,
                                  l0_ref[...], l1_ref[...], on_ref[...])
        cat_ref[...] = out.astype(cat_ref.dtype)
        st_scr[...] = st_new

    vec = pl.BlockSpec((1, HD), lambda h, r: (0, h))
    return pl.pallas_call(
        body, name="hgrn2_fwd", grid=(A_HEADS, nb),
        in_specs=[_col_spec(RA, lambda h: h), _col_spec(RA, lambda h: 8 + h),
                  _col_spec(RA, lambda h: 16 + h), _col_spec(RA, lambda h: 24 + h),
                  vec, vec, _full_spec((1, HD))],
        out_specs=[_col_spec(RA, lambda h: h),
                   pl.BlockSpec((None, None, HD, HD), lambda h, r: (h, r, 0, 0))],
        out_shape=[_sds((T, 2 * A_HEADS * HD), BF16), _sds((A_HEADS, nb, HD, HD), F32)],
        scratch_shapes=[pltpu.VMEM((HD, HD), F32)],
        compiler_params=_params(2))(z, z, z, z, l0, l1, onorm)


def _hgrn2_bwd(z, l0, l1, onorm, sst, dcat, deps=()):
    nb = T // RA
    deps = _live(deps)

    def body(q_ref, f_ref, v_ref, g_ref, l0_ref, l1_ref, on_ref, sst_ref, dcat_ref,
             dq_ref, df_ref, dv_ref, dg_ref, dl0_ref, dl1_ref, don_ref, ds_scr):
        h, r = pl.program_id(0), pl.program_id(1)

        @pl.when(r == 0)
        def _():
            ds_scr[...] = jnp.zeros_like(ds_scr)

        _, vjp = jax.vjp(_hgrn2_rows, q_ref[...], f_ref[...], v_ref[...], g_ref[...], sst_ref[...],
                         l0_ref[...], l1_ref[...], on_ref[...])
        dq, dzf, dv, dga, dst, dl0, dl1, don = vjp((dcat_ref[...], ds_scr[...]))
        dq_ref[...] = dq.astype(dq_ref.dtype)
        df_ref[...] = dzf.astype(df_ref.dtype)
        dv_ref[...] = dv.astype(dv_ref.dtype)
        dg_ref[...] = dga.astype(dg_ref.dtype)
        ds_scr[...] = dst

        @pl.when(r == 0)
        def _():
            dl0_ref[...] = dl0
            dl1_ref[...] = dl1

        @pl.when(r > 0)
        def _():
            dl0_ref[...] += dl0
            dl1_ref[...] += dl1

        first = jnp.logical_and(h == 0, r == 0)

        @pl.when(first)
        def _():
            don_ref[...] = don

        @pl.when(jnp.logical_not(first))
        def _():
            don_ref[...] += don

    def rev(col_of):
        return pl.BlockSpec((RA, HD), lambda h, r, col_of=col_of: (nb - 1 - r, col_of(h)))

    vec = pl.BlockSpec((1, HD), lambda h, r: (0, h))
    grad = _sds((T, A_HEADS * HD), BF16)
    return pl.pallas_call(
        _skip_deps(body, 9, len(deps)), name="hgrn2_bwd", grid=(A_HEADS, nb),
        in_specs=[rev(lambda h: h), rev(lambda h: 8 + h), rev(lambda h: 16 + h), rev(lambda h: 24 + h),
                  vec, vec, _full_spec((1, HD)),
                  pl.BlockSpec((None, None, HD, HD), lambda h, r: (h, nb - 1 - r, 0, 0)),
                  rev(lambda h: h)] + [ANY_SPEC] * len(deps),
        out_specs=[rev(lambda h: h)] * 4 + [vec, vec, _full_spec((1, HD))],
        out_shape=[grad] * 4 + [_sds((1, A_HEADS * HD), F32)] * 2 + [_sds((1, HD), F32)],
        scratch_shapes=[pltpu.VMEM((HD, HD), F32)],
        compiler_params=_params(2))(z, z, z, z, l0, l1, onorm, sst, dcat, *deps)


def _gmlp_specs():
    vec = pl.BlockSpec((1, HD), lambda g, r: (0, g))
    ws = pl.BlockSpec((None, B_CHUNK, B_CHUNK), lambda g, r: (g, 0, 0))
    bias = pl.BlockSpec((None, B_CHUNK, 1), lambda g, r: (g, 0, 0))
    return vec, ws, bias


def _gmlp_fwd(z, cat, lnw, lnb, ws, bias):
    vec, ws_spec, bias_spec = _gmlp_specs()

    def body(u_ref, v_ref, g_ref, lnw_ref, lnb_ref, ws_ref, bias_ref, cat_in_ref, cat_ref):
        del cat_in_ref
        out = _gmlp_rows(u_ref[...], v_ref[...], g_ref[...], lnw_ref[...], lnb_ref[...],
                         ws_ref[...], bias_ref[...])
        cat_ref[...] = out.astype(cat_ref.dtype)

    return pl.pallas_call(
        body, name="gmlp_fwd", grid=(B_GROUPS, T // RB),
        in_specs=[_col_spec(RB, lambda g: 32 + g), _col_spec(RB, lambda g: 40 + g),
                  _col_spec(RB, lambda g: 48 + g), vec, vec, ws_spec, bias_spec,
                  pl.BlockSpec(memory_space=pl.ANY)],
        out_specs=_col_spec(RB, lambda g: A_HEADS + g),
        out_shape=_sds(cat.shape, cat.dtype), input_output_aliases={7: 0},
        compiler_params=_params(2))(z, z, z, lnw, lnb, ws, bias, cat)


def _gmlp_bwd(z, lnw, lnb, ws, bias, dcat):
    vec, ws_spec, bias_spec = _gmlp_specs()

    def fn(u, vb, gb, w, b, wsv, bv, dout):
        _, vjp = jax.vjp(_gmlp_rows, u, vb, gb, w, b, wsv, bv)
        return vjp(dout)

    def body(*refs):
        ins, outs = refs[:8], refs[8:]
        res = fn(*[r[...] for r in ins])
        for o, r in zip(outs[:3], res[:3]):
            o[...] = r.astype(o.dtype)
        first = pl.program_id(1) == 0
        for o, r in zip(outs[3:], res[3:]):
            @pl.when(first)
            def _(o=o, r=r):
                o[...] = r

            @pl.when(jnp.logical_not(first))
            def _(o=o, r=r):
                o[...] += r

    grad = _sds((T, B_GROUPS * HD), BF16)
    row_out = pl.BlockSpec((RB, HD), lambda g, r: (r, g))
    return pl.pallas_call(
        body, name="gmlp_bwd", grid=(B_GROUPS, T // RB),
        in_specs=[_col_spec(RB, lambda g: 32 + g), _col_spec(RB, lambda g: 40 + g),
                  _col_spec(RB, lambda g: 48 + g), vec, vec, ws_spec, bias_spec,
                  _col_spec(RB, lambda g: A_HEADS + g)],
        out_specs=[row_out] * 3 + [vec, vec, ws_spec, bias_spec],
        out_shape=[grad] * 3 + [_sds((1, B_GROUPS * HD), F32)] * 2
        + [_sds((B_GROUPS, B_CHUNK, B_CHUNK), F32), _sds((B_GROUPS, B_CHUNK, 1), F32)],
        compiler_params=_params(2))(z, z, z, lnw, lnb, ws, bias, dcat)


def _mla_pre(z1, qn, kvn, cos_t, sin_t):
    def fn(cq, ckv, kpe, cs, sn, wq, wkv):
        return _rms(cq, wq), _rms(ckv, wkv), _rope(kpe, cs, sn)
    return _pure_call("mla_pre", fn, (T // TM,),
                      [_row_spec(TM, C_RANK, 4), _row_spec(TM, C_RANK, 5), _row_spec(TM, HD, 24),
                       _row_spec(TM, HD), _row_spec(TM, HD),
                       _full_spec((1, C_RANK)), _full_spec((1, C_RANK))],
                      [_row_spec(TM, C_RANK), _row_spec(TM, C_RANK), _row_spec(TM, HD)],
                      [_sds((T, C_RANK), BF16), _sds((T, C_RANK), BF16), _sds((T, HD), BF16)],
                      (z1, z1, z1, cos_t, sin_t, qn, kvn))


def _mla_pre_bwd(z1, qn, kvn, cos_t, sin_t, dcqn, dckvn, dkp, deps=()):
    def fn(cq, ckv, cs, sn, wq, wkv, g_q, g_kv, g_kp):
        _, vjp_q = jax.vjp(_rms, cq, wq)
        dcq, dwq = vjp_q(g_q)
        _, vjp_kv = jax.vjp(_rms, ckv, wkv)
        dckv, dwkv = vjp_kv(g_kv)
        return dcq, dckv, _rope_transpose(g_kp, cs, sn), dwq, dwkv
    return _pure_call("mla_pre_bwd", fn, (T // TM,),
                      [_row_spec(TM, C_RANK, 4), _row_spec(TM, C_RANK, 5),
                       _row_spec(TM, HD), _row_spec(TM, HD),
                       _full_spec((1, C_RANK)), _full_spec((1, C_RANK)),
                       _row_spec(TM, C_RANK), _row_spec(TM, C_RANK), _row_spec(TM, HD)],
                      [_row_spec(TM, C_RANK), _row_spec(TM, C_RANK), _row_spec(TM, HD),
                       _full_spec((1, C_RANK)), _full_spec((1, C_RANK))],
                      [_sds((T, C_RANK), BF16), _sds((T, C_RANK), BF16), _sds((T, HD), BF16),
                       _sds((1, C_RANK), F32), _sds((1, C_RANK), F32)],
                      (z1, z1, cos_t, sin_t, qn, kvn, dcqn, dckvn, dkp), n_acc=2, deps=deps)


def _gate_out(o, z1):
    def fn(ov, gate):
        return ov * _silu(gate)
    return _pure_call("gate_out", fn, (T // TM,), [_row_spec(TM, D), _row_spec(TM, D, 0)],
                      [_row_spec(TM, D)], [_sds((T, D), BF16)], (o, z1))[0]


def _gate_out_bwd(o, z1, dog, deps=()):
    def fn(ov, gate, g):
        _, vjp = jax.vjp(lambda a, b: a * _silu(b), ov, gate)
        return vjp(g)
    return _pure_call("gate_out_bwd", fn, (T // TM,),
                      [_row_spec(TM, D), _row_spec(TM, D, 0), _row_spec(TM, D)],
                      [_row_spec(TM, D), _row_spec(TM, D)],
                      [_sds((T, D), F32), _sds((T, D), BF16)], (o, z1, dog), deps=deps)


TQ = 256


def _att_scores(q_ref, cos_ref, sin_ref, kn_ref, kp_ref, n):
    keys = (n + 1) * TQ
    q = q_ref[...]
    qn = q[:, :C_NOPE].astype(BF16)
    qp = _rope(q[:, C_NOPE:], cos_ref[...], sin_ref[...]).astype(BF16)
    s = (_raw_nt(qn, kn_ref[0:keys, :]) + _raw_nt(qp, kp_ref[0:keys, :])) * ATT_SCALE
    row = n * TQ + lax.broadcasted_iota(jnp.int32, (TQ, keys), 0)
    col = lax.broadcasted_iota(jnp.int32, (TQ, keys), 1)
    return qn, qp, s, col <= row


def _per_query_block(fn):
    for n in range(T // TQ):
        pl.when(pl.program_id(1) == n)(functools.partial(fn, n))


def _att_in_specs():
    return [pl.BlockSpec((TQ, QP), lambda h, i: (i, h)),
            pl.BlockSpec((TQ, HD), lambda h, i: (i, 0)),
            pl.BlockSpec((TQ, HD), lambda h, i: (i, 0)),
            pl.BlockSpec((T, C_NOPE), lambda h, i: (0, 2 * h)),
            pl.BlockSpec((T, HD), lambda h, i: (0, 0)),
            pl.BlockSpec((T, C_V), lambda h, i: (0, 2 * h + 1))]


def _attention_fwd(q, cos_t, sin_t, kv, kp):
    def body(q_ref, cos_ref, sin_ref, kn_ref, kp_ref, v_ref, o_ref, lse_ref):
        def block(n):
            _, _, s, mask = _att_scores(q_ref, cos_ref, sin_ref, kn_ref, kp_ref, n)
            s = jnp.where(mask, s, jnp.finfo(F32).min)
            m = jnp.max(s, axis=-1, keepdims=True)
            p = jnp.exp(s - m)
            l = jnp.sum(p, axis=-1, keepdims=True)
            o_ref[...] = _raw_nn(p / l, v_ref[0:(n + 1) * TQ, :])
            lse_ref[...] = m + jnp.log(l)

        _per_query_block(block)

    return pl.pallas_call(
        body, name="attention_fwd", grid=(C_HEADS, T // TQ), in_specs=_att_in_specs(),
        out_specs=[pl.BlockSpec((TQ, C_V), lambda h, i: (i, h)),
                   pl.BlockSpec((None, TQ, 1), lambda h, i: (h, i, 0))],
        out_shape=[_sds((T, C_HEADS * C_V), F32), _sds((C_HEADS, T, 1), F32)],
        compiler_params=_params(2))(q, cos_t, sin_t, kv, kp, kv)


def _attention_bwd(q, cos_t, sin_t, kv, kp, o, lse, do):
    nq = T // TQ

    def body(q_ref, cos_ref, sin_ref, kn_ref, kp_ref, v_ref, o_ref, lse_ref, do_ref,
             dq_ref, dkv_ref, dkp_ref, dkn_scr, dv_scr):
        h, i = pl.program_id(0), pl.program_id(1)

        @pl.when(i == 0)
        def _():
            dv_scr[...] = jnp.zeros_like(dv_scr)
            dkn_scr[...] = jnp.zeros_like(dkn_scr)

        @pl.when(jnp.logical_and(h == 0, i == 0))
        def _():
            dkp_ref[...] = jnp.zeros_like(dkp_ref)

        def block(n):
            keys = (n + 1) * TQ
            qn, qp, s, mask = _att_scores(q_ref, cos_ref, sin_ref, kn_ref, kp_ref, n)
            p = jnp.where(mask, jnp.exp(s - lse_ref[...]), 0.0)
            dov = do_ref[...]
            delta = jnp.sum(dov * o_ref[...], axis=-1, keepdims=True)
            dp = _raw_nt(dov, v_ref[0:keys, :])
            ds = p * (dp - delta) * ATT_SCALE
            dqn = _raw_nn(ds, kn_ref[0:keys, :])
            dqp = _rope_transpose(_raw_nn(ds, kp_ref[0:keys, :]), cos_ref[...], sin_ref[...])
            dq_ref[...] = jnp.concatenate([dqn, dqp], axis=1).astype(dq_ref.dtype)
            dv_scr[0:keys, :] += _raw_tn(p, dov)
            dkn_scr[0:keys, :] += _raw_tn(ds, qn)
            dkp_ref[0:keys, :] += _raw_tn(ds, qp)

        _per_query_block(block)

        @pl.when(i == nq - 1)
        def _():
            dkv_ref[...] = jnp.concatenate([dkn_scr[...], dv_scr[...]], axis=1).astype(dkv_ref.dtype)

    return pl.pallas_call(
        body, name="attention_bwd", grid=(C_HEADS, nq),
        in_specs=_att_in_specs() + [pl.BlockSpec((TQ, C_V), lambda h, i: (i, h)),
                                    pl.BlockSpec((None, TQ, 1), lambda h, i: (h, i, 0)),
                                    pl.BlockSpec((TQ, C_V), lambda h, i: (i, h))],
        out_specs=[pl.BlockSpec((TQ, QP), lambda h, i: (i, h)),
                   pl.BlockSpec((T, C_NOPE + C_V), lambda h, i: (0, h)),
                   _full_spec((T, HD))],
        out_shape=[_sds((T, C_HEADS * QP), BF16), _sds((T, C_HEADS * (C_NOPE + C_V)), BF16),
                   _sds((T, HD), F32)],
        scratch_shapes=[pltpu.VMEM((T, C_NOPE), F32), pltpu.VMEM((T, C_V), F32)],
        compiler_params=_params(2))(q, cos_t, sin_t, kv, kp, kv, o, lse, do)


def _adamw_math(w, g, m, v):
    m = ADAM_B1 * m + (1.0 - ADAM_B1) * g
    v = ADAM_B2 * v + (1.0 - ADAM_B2) * (g * g)
    m_hat = m / (1.0 - ADAM_B1 ** ADAM_STEP)
    v_hat = v / (1.0 - ADAM_B2 ** ADAM_STEP)
    delta = -ADAM_LR * (m_hat / (jnp.sqrt(v_hat) + ADAM_EPS) + ADAM_WD * w)
    return delta, m, v


def _adamw(name, parts, w, m, v, tr):
    rows, cols = w.shape
    n_parts = parts.shape[0]

    def fn(pv, wv, mv, vv):
        g = pv[0].astype(F32)
        for d in range(1, n_parts):
            g = g + pv[d].astype(F32)
        return (g,) + _adamw_math(wv, g, mv, vv)

    blk = pl.BlockSpec((tr, cols), lambda i: (i, 0))
    return _pure_call(name, fn, (rows // tr,),
                      [pl.BlockSpec((n_parts, tr, cols), lambda i: (0, i, 0)), blk, blk, blk],
                      [blk] * 4, [_sds((rows, cols), F32)] * 4, (parts, w, m, v))


def _exchange(name, arrs, gather, deps=()):
    n = len(arrs)
    deps = _live(deps)

    def body(*refs):
        ins, outs = refs[:n], refs[n + len(deps):2 * n + len(deps)]
        send_sems, recv_sems, local_sems = refs[2 * n + len(deps):]
        x, y, c = lax.axis_index("x"), lax.axis_index("y"), lax.axis_index("c")
        me = 4 * x + 2 * y + c

        def peer(k):
            return (x ^ (k >> 2), y ^ ((k >> 1) & 1), c ^ (k & 1))

        def copy(a, k):
            src = ins[a] if gather else ins[a].at[me ^ k]
            return pltpu.make_async_remote_copy(
                src_ref=src, dst_ref=outs[a].at[me], send_sem=send_sems.at[a, k - 1],
                recv_sem=recv_sems.at[a, k - 1], device_id=peer(k), device_id_type=MESH_ID)

        def arrival(a, k):
            src = ins[a] if gather else ins[a].at[me]
            return pltpu.make_async_remote_copy(
                src_ref=src, dst_ref=outs[a].at[me ^ k], send_sem=send_sems.at[a, k - 1],
                recv_sem=recv_sems.at[a, k - 1], device_id=peer(k), device_id_type=MESH_ID)

        own = [pltpu.make_async_copy(ins[a] if gather else ins[a].at[me], outs[a].at[me], local_sems.at[a])
               for a in range(n)]
        for cp in own:
            cp.start()
        for k in range(1, N_DEV):
            for a in range(n):
                copy(a, k).start()
        for k in range(1, N_DEV):
            for a in range(n):
                arrival(a, k).wait_recv()
        for k in range(1, N_DEV):
            for a in range(n):
                copy(a, k).wait_send()
        for cp in own:
            cp.wait()

    any_spec = pl.BlockSpec(memory_space=pl.ANY)
    out_shape = [_sds((N_DEV,) + a.shape if gather else a.shape, a.dtype) for a in arrs]
    return pl.pallas_call(
        body, name=name, in_specs=[any_spec] * (n + len(deps)), out_specs=[any_spec] * n, out_shape=out_shape,
        scratch_shapes=[pltpu.SemaphoreType.DMA((n, N_DEV - 1)), pltpu.SemaphoreType.DMA((n, N_DEV - 1)),
                        pltpu.SemaphoreType.DMA((n,))],
        compiler_params=pltpu.CompilerParams(has_side_effects=True))(*arrs, *deps)


HBM_SPEC = pl.BlockSpec(memory_space=pltpu.HBM)
SEM_SPEC = pl.BlockSpec(memory_space=pltpu.SEMAPHORE)
DATAFLOW = pltpu.SideEffectType.DATAFLOW_SIDE_EFFECTING


def _my_index():
    return 4 * lax.axis_index("x") + 2 * lax.axis_index("y") + lax.axis_index("c")


def _plan_copies(plan, refs, send_sems, recv_sems):
    x, y, c = lax.axis_index("x"), lax.axis_index("y"), lax.axis_index("c")
    return [pltpu.make_async_remote_copy(
        src_ref=src, dst_ref=dst, send_sem=send_sems.at[i], recv_sem=recv_sems.at[i],
        device_id=(x ^ (k >> 2), y ^ ((k >> 1) & 1), c ^ (k & 1)), device_id_type=MESH_ID)
        for i, (src, dst, k) in enumerate(plan(refs, 4 * x + 2 * y + c))]


def _split_call(name, bufs, waits=None, starts=None, deps=()):
    n = len(bufs)
    deps = _live(deps)
    n_wait = 2 if waits else 0

    def body(*refs):
        zones = refs[:n]
        if waits:
            for cp in _plan_copies(waits[2], zones, refs[n], refs[n + 1]):
                cp.wait_send()
                cp.wait_recv()
        if starts:
            first_out = n + n_wait + len(deps)
            for cp in _plan_copies(starts[0], zones, refs[first_out], refs[first_out + 1]):
                cp.start()
            refs[-1][...] = jnp.zeros_like(refs[-1])

    out_specs, out_shape = [], []
    if starts:
        sems = pltpu.SemaphoreType.DMA((starts[1],))
        out_specs, out_shape = [SEM_SPEC, SEM_SPEC], [sems, sems]
    out_specs += [HBM_SPEC] * n
    out_shape += [pltpu.HBM(b.shape, b.dtype) for b in bufs]
    if starts:
        out_specs.append(pl.BlockSpec(memory_space=pltpu.VMEM))
        out_shape.append(_sds((8, 128), F32))
    first_buf = 2 if starts else 0
    res = pl.pallas_call(
        body, name=name,
        in_specs=[HBM_SPEC] * n + [SEM_SPEC] * n_wait + [ANY_SPEC] * len(deps),
        out_specs=out_specs, out_shape=out_shape,
        input_output_aliases={i: first_buf + i for i in range(n)},
        compiler_params=pltpu.CompilerParams(has_side_effects=DATAFLOW),
    )(*[pltpu.with_memory_space_constraint(b, pltpu.HBM) for b in bufs], *(waits[:2] if waits else ()), *deps)
    out_bufs = list(res[first_buf:first_buf + n])
    return out_bufs, ((res[0], res[1]) if starts else None), (res[-1] if starts else None)


def _direct_plan(n, gather):
    def plan(refs, me):
        return [(refs[a] if gather else refs[a].at[me ^ k], refs[n + a].at[me], k)
                for k in range(1, N_DEV) for a in range(n)]
    return plan


def _exchange_start(name, arrs, gather, deps=()):
    n = len(arrs)
    me = _my_index()
    lands = []
    for a in arrs:
        if gather:
            land = lax.dynamic_update_slice_in_dim(lax.empty((N_DEV,) + a.shape, a.dtype), a[None], me, 0)
        else:
            land = lax.dynamic_update_slice_in_dim(lax.empty(a.shape, a.dtype),
                                                   lax.dynamic_slice_in_dim(a, me, 1, 0), me, 0)
        lands.append(land)
    plan = _direct_plan(n, gather)
    bufs, sems, token = _split_call(name, list(arrs) + lands, starts=(plan, n * (N_DEV - 1)), deps=deps)
    return (n, plan, sems, bufs), token


def _exchange_wait(name, handle, after):
    n, plan, sems, bufs = handle
    bufs, _, _ = _split_call(name, bufs, waits=(sems[0], sems[1], plan), deps=[after])
    return bufs[n:]


ICI_PEERS = (2, 4, 6)
SIBLING = 1


def _gather2_send(name, arrs, deps=()):
    n = len(arrs)
    me = _my_index()
    lands = [lax.dynamic_update_slice_in_dim(lax.empty((N_DEV,) + a.shape, a.dtype), a[None], me, 0) for a in arrs]

    def plan(refs, me_):
        return [(refs[a], refs[n + a].at[me_], k) for k in (SIBLING,) + ICI_PEERS for a in range(n)]

    bufs, sems, token = _split_call(name, list(arrs) + lands, starts=(plan, 4 * n), deps=deps)
    return (n, plan, sems, bufs), token


def _gather2_relay(name, handle, after):
    n, plan, sems, bufs = handle

    def relay(refs, me_):
        return [(refs[n + a].at[me_ ^ k], refs[n + a].at[me_ ^ k], SIBLING) for k in ICI_PEERS for a in range(n)]

    bufs, sems2, token = _split_call(name, bufs, waits=(sems[0], sems[1], plan), starts=(relay, 3 * n), deps=[after])
    return (n, relay, sems2, bufs), token


def _split_done(name, handle, after, all_bufs=False):
    n, plan, sems, bufs = handle
    bufs, _, _ = _split_call(name, bufs, waits=(sems[0], sems[1], plan), deps=[after])
    return bufs if all_bufs else bufs[n:]


def _scatter2_pair(name, stacks, deps=()):
    n = len(stacks)
    pairs = [lax.empty((4,) + s.shape[1:], s.dtype) for s in stacks]

    def plan(refs, me):
        return [(refs[a].at[(me ^ SIBLING) ^ (2 * j)], refs[n + a].at[j], SIBLING) for j in range(4) for a in range(n)]

    bufs, sems, token = _split_call(name, list(stacks) + pairs, starts=(plan, 4 * n), deps=deps)
    return (n, plan, sems, bufs), token


def _pair_add(name, stack, pair, me):
    _, rows, cols = stack.shape
    tr = rows // 8

    def body(me_ref, s_ref, p_ref, o_ref):
        del me_ref
        o_ref[...] = (s_ref[...].astype(F32) + p_ref[...].astype(F32)).astype(o_ref.dtype)

    grid_spec = pltpu.PrefetchScalarGridSpec(
        num_scalar_prefetch=1, grid=(4, rows // tr),
        in_specs=[pl.BlockSpec((None, tr, cols), lambda j, i, me_ref: (me_ref[0] ^ (2 * j), i, 0)),
                  pl.BlockSpec((None, tr, cols), lambda j, i, me_ref: (j, i, 0))],
        out_specs=pl.BlockSpec((None, tr, cols), lambda j, i, me_ref: (j, i, 0)))
    return pl.pallas_call(body, name=name, grid_spec=grid_spec, out_shape=_sds((4, rows, cols), stack.dtype),
                          compiler_params=_params(2))(me.reshape(1).astype(jnp.int32), stack, pair)


def _scatter2_send(name, chip_sums, deps=()):
    n = len(chip_sums)
    finals = [lax.dynamic_update_slice_in_dim(lax.empty(c.shape, c.dtype), c[0:1], 0, 0) for c in chip_sums]

    def plan(refs, me):
        del me
        return [(refs[a].at[j], refs[n + a].at[j], 2 * j) for j in range(1, 4) for a in range(n)]

    bufs, sems, token = _split_call(name, list(chip_sums) + finals, starts=(plan, 3 * n), deps=deps)
    return (n, plan, sems, bufs), token


def _pad_rope(p):
    z = jnp.zeros(p.shape[:-1] + (32,), p.dtype)
    return jnp.concatenate([p[..., :32], z, p[..., 32:], z], axis=-1)


def _unpad_rope(p):
    return jnp.concatenate([p[..., :32], p[..., 64:96]], axis=-1)


def _odd_in_layout(w):
    w = w.transpose(1, 0, 2).reshape(D, ODD_IN)
    cq, ckv, kpe, gate = w[:, :512], w[:, 512:1024], w[:, 1024:1088], w[:, 1088:]
    return jnp.concatenate([gate, cq, ckv, _pad_rope(kpe)], axis=1)


def _odd_in_unlayout(dw):
    gate, cq, ckv, kpe = dw[:, :2048], dw[:, 2048:2560], dw[:, 2560:3072], _unpad_rope(dw[:, 3072:])
    w = jnp.concatenate([cq, ckv, kpe, gate], axis=1)
    return w.reshape(D, N_DEV, ODD_IN // N_DEV).transpose(1, 0, 2)


def _qb_layout(w):
    w = w.transpose(1, 0, 2).reshape(C_RANK, C_HEADS, C_QK)
    w = jnp.concatenate([w[..., :C_NOPE], _pad_rope(w[..., C_NOPE:])], axis=-1)
    return w.reshape(C_RANK, C_HEADS * QP)


def _qb_unlayout(dw):
    dw = dw.reshape(C_RANK, C_HEADS, QP)
    dw = jnp.concatenate([dw[..., :C_NOPE], _unpad_rope(dw[..., C_NOPE:])], axis=-1)
    return dw.reshape(C_RANK, N_DEV, C_HEADS * C_QK // N_DEV).transpose(1, 0, 2)


def _rope_tables(positions):
    inv_freq = ROPE_THETA ** (-jnp.arange(0, C_ROPE, 2, dtype=F32) / C_ROPE)
    ang = positions.astype(F32)[0][:, None] * inv_freq
    cos, sin = jnp.cos(ang), jnp.sin(ang)
    z = jnp.zeros_like(cos)
    return jnp.concatenate([cos, z, cos, z], axis=1), jnp.concatenate([-sin, z, sin, z], axis=1)


SMALL_ROWS = 1144


def _pack_rows(parts):
    flat = jnp.concatenate([p.reshape(-1) for p in parts])
    rows = -(-flat.shape[0] // 1024) * 8
    return jnp.pad(flat, (0, rows * 128 - flat.shape[0])).reshape(rows, 128)


def _pack_small(parts):
    flat = jnp.concatenate([p.reshape(-1) for p in parts])
    return jnp.pad(flat, (0, SMALL_ROWS * 128 - flat.shape[0])).reshape(SMALL_ROWS, 128)


def _unpack_small(packed, shapes):
    flat = packed.reshape(-1)
    out, off = [], 0
    for s in shapes:
        size = 1
        for d in s:
            size *= d
        out.append(flat[off:off + size].reshape(s))
        off += size
    return out


def _forward_backward(x, cos_t, sin_t, target, norm_pre, norm_post, lb_logits, a_onorm, ln_w, ln_b,
                      b_ws, b_bias, get_w, put_g, put_small=None, start_dep=None):
    npre0, npre1 = norm_pre[0:1], norm_pre[1:2]
    npost0, npost1 = norm_post[0:1], norm_post[1:2]
    l0, l1 = lb_logits[0:1], lb_logits[1:2]
    bias_col = b_bias.reshape(B_GROUPS, B_CHUNK, 1)
    ws = b_ws.reshape(B_GROUPS, B_CHUNK, B_CHUNK)

    h0 = _pre_norm("pre_norm0", x, npre0, deps=[start_dep])
    w_ev_in = get_w("ev_in", h0)
    z0 = _mm_nn("ev_in", h0, w_ev_in, F32, 1024, 896)
    cat, sst = _hgrn2_fwd(z0, l0, l1, a_onorm)
    cat = _gmlp_fwd(z0, cat, ln_w, ln_b, ws, bias_col)
    w_ev_out = get_w("ev_out", cat)
    y0 = _mm_nn("ev_out", cat, w_ev_out, F32, 1024, 1024)
    x1, h1 = _post_pre_norm(x, y0, npost0, npre1)
    w_od_in, w_qb, w_kvb, q_norm, kv_norm = get_w("od_mid", h1)
    z1 = _mm_nn("od_in", h1, w_od_in[None], F32, 1024, 640)
    cqn, ckvn, kp = _mla_pre(z1, q_norm, kv_norm, cos_t, sin_t)
    q = _mm_nn("od_qb", cqn, w_qb[None], F32, 1024, 1024)
    kv = _mm_nn("od_kvb", ckvn, w_kvb, BF16, 1024, 512)
    o, lse = _attention_fwd(q, cos_t, sin_t, kv, kp)
    og = _gate_out(o, z1)
    w_od_out = get_w("od_out", og)
    y1 = _mm_nn("od_out", og, w_od_out, F32, 1024, 1024)
    dx2, dy1, loss_part, dnpost1 = _final_loss(x1, y1, npost1, target)

    g_od_out = _mm_tn("od_out_dw", og, dy1, 1, BF16, 1024, 1024)
    tok = put_g("od_out", [g_od_out.reshape(N_DEV, D // N_DEV, D)])
    dog = _mm_nt("od_out_dx", dy1, w_od_out, F32, 1024, 1024, deps=[tok])
    do, dgate = _gate_out_bwd(o, z1, dog)
    dq, dkv, dkp = _attention_bwd(q, cos_t, sin_t, kv, kp, o, lse, do)
    g_qb = _mm_tn("od_qb_dw", cqn, dq, 1, F32, 512, 1024)
    g_kvb = _mm_tn("od_kvb_dw", ckvn, dkv, N_DEV, BF16, 512, 512)
    tok = put_g("od_qkv", [_qb_unlayout(g_qb[0]).astype(BF16), g_kvb])
    dcqn = _mm_nt("od_qb_dx", dq, w_qb[None], F32, 1024, 512, deps=[tok])
    dckvn = _mm_nt("od_kvb_dx", dkv, w_kvb, F32, 1024, 512)
    dcq, dckv, dkpe, dqn, dkvn = _mla_pre_bwd(z1, q_norm, kv_norm, cos_t, sin_t, dcqn, dckvn, dkp)
    dz1 = jnp.concatenate([dgate, dcq, dckv, dkpe], axis=1)
    g_od_in = _mm_tn("od_in_dw", h1, dz1, 1, F32, 1024, 640)
    tok = put_g("od_in", [_odd_in_unlayout(g_od_in[0]).astype(BF16)])
    dh1 = _mm_nt("od_in_dx", dz1, w_od_in[None], F32, 1024, 1024, deps=[tok])
    dx1, dy0, dnpost0, dnpre1 = _post_pre_norm_bwd(y0, x1, npost0, npre1, dx2, dh1)

    g_ev_out = _mm_tn("ev_out_dw", cat, dy0, 1, BF16, 1024, 1024)
    tok = put_g("ev_out", [g_ev_out.reshape(N_DEV, D // N_DEV, D)])
    dcat = _mm_nt("ev_out_dx", dy0, w_ev_out, F32, 1024, 1024, deps=[tok])
    dqa, dfa, dia, dga, dl0, dl1, donorm = _hgrn2_bwd(z0, l0, l1, a_onorm, sst, dcat)
    dub, dvb, dgb, dlnw, dlnb, dws, dbias = _gmlp_bwd(z0, ln_w, ln_b, ws, bias_col, dcat)
    dz0 = jnp.concatenate([dqa, dfa, dia, dga, dub, dvb, dgb], axis=1)
    rest_small = [jnp.concatenate([dnpost0, dnpost1], axis=0), jnp.concatenate([dl0, dl1], axis=0), donorm, dlnw,
                  dlnb, dws.reshape(1, B_GROUPS, B_CHUNK, B_CHUNK), dbias.reshape(1, B_GROUPS, B_CHUNK), dqn, dkvn]
    tok = put_small([dnpre1] + rest_small) if put_small else None
    g_ev_in = _mm_tn("ev_in_dw", h0, dz0, N_DEV, BF16, 1024, 896, deps=[tok])
    tok = put_g("ev_in", [g_ev_in])
    dh0 = _mm_nt("ev_in_dx", dz0, w_ev_in, F32, 1024, 1024, deps=[tok])
    grad_x, dnpre0 = _pre_norm_bwd(x, npre0, dh0, dx1)

    small = [jnp.concatenate([dnpre0, dnpre1], axis=0)] + rest_small
    return loss_part[0, 0], grad_x, small


def kernel(x, positions, norm_pre, norm_post, ev_w_in, ev_lb_logits, ev_a_onorm, ev_b_ln_w, ev_b_ln_b, ev_b_ws, ev_b_bias, ev_w_out, od_w_in, od_q_norm, od_w_qb, od_kv_norm, od_w_kvb, od_w_out, loss_target, m_norm_pre, m_norm_post, m_ev_w_in, m_ev_lb_logits, m_ev_a_onorm, m_ev_b_ln_w, m_ev_b_ln_b, m_ev_b_ws, m_ev_b_bias, m_ev_w_out, m_od_w_in, m_od_q_norm, m_od_w_qb, m_od_kv_norm, m_od_w_kvb, m_od_w_out, v_norm_pre, v_norm_post, v_ev_w_in, v_ev_lb_logits, v_ev_a_onorm, v_ev_b_ln_w, v_ev_b_ln_b, v_ev_b_ws, v_ev_b_bias, v_ev_w_out, v_od_w_in, v_od_q_norm, v_od_w_qb, v_od_kv_norm, v_od_w_kvb, v_od_w_out):
    me = 4 * lax.axis_index("x") + 2 * lax.axis_index("y") + lax.axis_index("c")
    bf = lambda w: w[0].astype(BF16)

    norms = jnp.pad(jnp.concatenate([od_q_norm, od_kv_norm], axis=1), ((0, 7), (0, 0)))
    first_h, tok = _gather2_send("gather_ev_in", [bf(ev_w_in)])
    rest_h, tok = _gather2_send("gather_rest", [bf(ev_w_out), bf(od_w_in), bf(od_w_qb), bf(od_w_kvb), norms,
                                                bf(od_w_out)], deps=[tok])
    first_h, tok = _gather2_relay("relay_ev_in", first_h, tok)
    rest = []

    def get_w(group, after):
        if group == "ev_in":
            return _split_done("arrived_ev_in", first_h, after)[0]
        if not rest:
            relayed, token = _gather2_relay("relay_rest", rest_h, after)
            rest.extend(_split_done("arrived_rest", relayed, token))
        w_ev_out, w_od_in, w_qb, w_kvb, norms_all, w_od_out = rest
        if group == "ev_out":
            return w_ev_out.reshape(1, D, D)
        if group == "od_out":
            return w_od_out.reshape(1, D, D)
        return (_odd_in_layout(w_od_in), _qb_layout(w_qb), w_kvb,
                norms_all[:, 0, :64].reshape(1, C_RANK), norms_all[:, 0, 64:].reshape(1, C_RANK))

    scatters = {}

    def put_g(group, grads):
        if group == "ev_in":
            paired, token = _scatter2_pair("pair_ev_in", grads)
            n = len(grads)
            bufs = _split_done("paired_ev_in", paired, token, all_bufs=True)
            chip_sums = [_pair_add("pair_add_ev_in", bufs[a], bufs[n + a], me) for a in range(n)]
            scatters[group], token = _scatter2_send("scatter_ev_in", chip_sums)
        else:
            scatters[group], token = _exchange_start("scatter_" + group, grads, False)
        return token

    def put_small(early):
        scatters["small"], token = _exchange_start("gather_small_early", [_pack_rows(early)], True)
        return token

    cos_t, sin_t = _rope_tables(positions)
    loss_part, grad_x, small_g = _forward_backward(
        x[0], cos_t, sin_t, loss_target[0], norm_pre, norm_post, ev_lb_logits, ev_a_onorm, ev_b_ln_w,
        ev_b_ln_b, ev_b_ws, ev_b_bias, get_w, put_g, put_small, start_dep=tok)
    loss = lax.psum(loss_part, ("x", "y", "c"))

    big_w = {"ev_w_in": ev_w_in, "ev_w_out": ev_w_out, "od_w_in": od_w_in, "od_w_qb": od_w_qb,
             "od_w_kvb": od_w_kvb, "od_w_out": od_w_out}
    big_m = {"ev_w_in": m_ev_w_in, "ev_w_out": m_ev_w_out, "od_w_in": m_od_w_in, "od_w_qb": m_od_w_qb,
             "od_w_kvb": m_od_w_kvb, "od_w_out": m_od_w_out}
    big_v = {"ev_w_in": v_ev_w_in, "ev_w_out": v_ev_w_out, "od_w_in": v_od_w_in, "od_w_qb": v_od_w_qb,
             "od_w_kvb": v_od_w_kvb, "od_w_out": v_od_w_out}
    big_out = {}
    after = grad_x
    for group, names in (("od_out", ["od_w_out"]), ("od_qkv", ["od_w_qb", "od_w_kvb"]), ("od_in", ["od_w_in"]),
                         ("ev_out", ["ev_w_out"])):
        parts = _exchange_wait("summed_" + group, scatters[group], after)
        for nm, p in zip(names, parts):
            w = big_w[nm][0]
            big_out[nm] = [r[None] for r in _adamw("adamw_" + nm, p, w, big_m[nm][0], big_v[nm][0], w.shape[0] // 8)]
            after = big_out[nm][0]

    late = _pack_rows([small_g[0][0:1]])
    late_all = _exchange("gather_small_late", [late], gather=True, deps=[after])[0]
    early_all = _exchange_wait("arrived_small_early", scatters["small"], late_all)[0]
    small_all = jnp.concatenate([late_all, early_all], axis=1)

    small_w = (norm_pre, norm_post, ev_lb_logits, ev_a_onorm, ev_b_ln_w, ev_b_ln_b, ev_b_ws, ev_b_bias)
    small_m = (m_norm_pre, m_norm_post, m_ev_lb_logits, m_ev_a_onorm, m_ev_b_ln_w, m_ev_b_ln_b, m_ev_b_ws, m_ev_b_bias)
    small_v = (v_norm_pre, v_norm_post, v_ev_lb_logits, v_ev_a_onorm, v_ev_b_ln_w, v_ev_b_ln_b, v_ev_b_ws, v_ev_b_bias)
    zero512 = jnp.zeros((1, C_RANK), F32)

    def packed_of(ws_):
        return _pack_small(list(ws_) + [zero512, zero512])

    res = _adamw("adamw_small", small_all, packed_of(small_w), packed_of(small_m), packed_of(small_v), SMALL_ROWS // 11)
    shapes = [w.shape for w in small_w] + [(1, C_RANK), (1, C_RANK)]
    g_s, d_s, m_s, v_s = (_unpack_small(r, shapes) for r in res)

    g_norms = jnp.concatenate([lax.dynamic_slice(g_s[8], (0, 64 * me), (1, 64)),
                               lax.dynamic_slice(g_s[9], (0, 64 * me), (1, 64))], axis=1)
    g_norms8 = jnp.concatenate([g_norms[None], jnp.zeros((N_DEV - 1, 1, 128), F32)], axis=0)
    res_n = _adamw("adamw_norms", g_norms8,
                   jnp.concatenate([od_q_norm, od_kv_norm], axis=1),
                   jnp.concatenate([m_od_q_norm, m_od_kv_norm], axis=1),
                   jnp.concatenate([v_od_q_norm, v_od_kv_norm], axis=1), 1)
    qn_out = [r[:, :64] for r in res_n]
    kvn_out = [r[:, 64:] for r in res_n]

    parts = _split_done("summed_ev_in", scatters["ev_in"], res[0])
    w = ev_w_in[0]
    big_out["ev_w_in"] = [r[None] for r in _adamw("adamw_ev_w_in", parts[0], w, m_ev_w_in[0], v_ev_w_in[0],
                                                  w.shape[0] // 8)]

    order = ("norm_pre", "norm_post", "ev_w_in", "ev_lb_logits", "ev_a_onorm", "ev_b_ln_w", "ev_b_ln_b",
             "ev_b_ws", "ev_b_bias", "ev_w_out", "od_w_in", "od_q_norm", "od_w_qb", "od_kv_norm",
             "od_w_kvb", "od_w_out")
    small_names = ("norm_pre", "norm_post", "ev_lb_logits", "ev_a_onorm", "ev_b_ln_w", "ev_b_ln_b",
                   "ev_b_ws", "ev_b_bias")
    outs = [loss, grad_x[None]]
    for kind in range(4):
        small_kind = (g_s, d_s, m_s, v_s)[kind]
        for nm in order:
            if nm in big_out:
                outs.append(big_out[nm][kind])
            elif nm == "od_q_norm":
                outs.append(qn_out[kind])
            elif nm == "od_kv_norm":
                outs.append(kvn_out[kind])
            else:
                outs.append(small_kind[small_names.index(nm)])
    return tuple(outs)
```

```python
import functools

import jax
import jax.numpy as jnp
from jax import lax
from jax.experimental import pallas as pl
from jax.experimental.pallas import tpu as pltpu

F32 = jnp.float32
BF16 = jnp.bfloat16

N_DEV = 8
T = 2048
D = 2048
EPS = 1e-6
A_HEADS = 8
HD = 128
A_CHUNK = 64
A_SUB = 16
B_GROUPS = 8
B_CHUNK = 128
EVEN_IN = 7168
C_HEADS = 16
C_RANK = 512
C_NOPE = 128
C_ROPE = 64
C_QK = C_NOPE + C_ROPE
C_V = 128
ODD_IN = 3136
ODD_IN_PAD = 3200
QP = 256
ROPE_THETA = 10000.0
ATT_SCALE = C_QK ** -0.5

ADAM_LR = 0.001
ADAM_B1 = 0.9
ADAM_B2 = 0.999
ADAM_EPS = 1e-08
ADAM_WD = 0.01
ADAM_STEP = 10

VMEM_LIMIT_V7X = 56 * 1024 * 1024
MESH_ID = pl.DeviceIdType.MESH


def _params(n_grid):
    return pltpu.CompilerParams(dimension_semantics=("arbitrary",) * n_grid,
                                vmem_limit_bytes=VMEM_LIMIT_V7X)


def _dg(a, b, ca, cb):
    return lax.dot_general(a.astype(BF16), b.astype(BF16), (((ca,), (cb,)), ((), ())),
                           preferred_element_type=F32)


def _raw_nn(a, b):
    return _dg(a, b, 1, 0)


def _raw_nt(a, b):
    return _dg(a, b, 1, 1)


def _raw_tn(a, b):
    return _dg(a, b, 0, 0)


@jax.custom_vjp
def _dot_nn(a, b):
    return _raw_nn(a, b)


def _dot_nn_fwd(a, b):
    return _raw_nn(a, b), (a.astype(BF16), b.astype(BF16))


def _dot_nn_bwd(res, g):
    a, b = res
    return _raw_nt(g, b), _raw_tn(a, g)


_dot_nn.defvjp(_dot_nn_fwd, _dot_nn_bwd)


@jax.custom_vjp
def _dot_nt(a, b):
    return _raw_nt(a, b)


def _dot_nt_fwd(a, b):
    return _raw_nt(a, b), (a.astype(BF16), b.astype(BF16))


def _dot_nt_bwd(res, g):
    a, b = res
    return _raw_nn(g, b), _raw_tn(g, a)


_dot_nt.defvjp(_dot_nt_fwd, _dot_nt_bwd)


@jax.custom_vjp
def _dot_tn(a, b):
    return _raw_tn(a, b)


def _dot_tn_fwd(a, b):
    return _raw_tn(a, b), (a.astype(BF16), b.astype(BF16))


def _dot_tn_bwd(res, g):
    a, b = res
    return _raw_nt(b, g), _raw_nn(a, g)


_dot_tn.defvjp(_dot_tn_fwd, _dot_tn_bwd)


@jax.custom_vjp
def _sigmoid(x):
    e = jnp.exp(-jnp.abs(x))
    return jnp.where(x >= 0, 1.0 / (1.0 + e), e / (1.0 + e))


def _sigmoid_fwd(x):
    s = _sigmoid(x)
    return s, s


def _sigmoid_bwd(s, g):
    return (g * s * (1.0 - s),)


_sigmoid.defvjp(_sigmoid_fwd, _sigmoid_bwd)


def _silu(x):
    return x * _sigmoid(x)


def _rms(x, w):
    return x * lax.rsqrt(jnp.mean(x * x, axis=-1, keepdims=True) + EPS) * w


def _split3(x):
    hi = x.astype(BF16)
    r = x - hi.astype(F32)
    mid = r.astype(BF16)
    lo = (r - mid.astype(F32)).astype(BF16)
    return hi, mid, lo


def _mask_apply(mask_bf16, x, contract):
    out = None
    for piece in _split3(x):
        d = lax.dot_general(mask_bf16, piece, (((contract,), (0,)), ((), ())),
                            preferred_element_type=F32)
        out = d if out is None else out + d
    return out


def _chunk_tri(rows):
    r = lax.broadcasted_iota(jnp.int32, (rows, rows), 0)
    c = lax.broadcasted_iota(jnp.int32, (rows, rows), 1)
    return ((r >= c) & (r // A_CHUNK == c // A_CHUNK)).astype(BF16)


@jax.custom_vjp
def _chunk_cumsum(x):
    return _mask_apply(_chunk_tri(x.shape[0]), x, 1)


def _chunk_cumsum_fwd(x):
    return _chunk_cumsum(x), None


def _chunk_cumsum_bwd(_, g):
    return (_mask_apply(_chunk_tri(g.shape[0]), g, 0),)


_chunk_cumsum.defvjp(_chunk_cumsum_fwd, _chunk_cumsum_bwd)


def _hgrn2_rows(q, zf, v, ga, st, l0, l1, onorm):
    rows = q.shape[0]
    n_sub = A_CHUNK // A_SUB
    mx = jnp.maximum(l0, l1)
    e0 = jnp.exp(l0 - mx)
    e1 = jnp.exp(l1 - mx)
    lb = e0 / (e0 + e1)
    lf = jnp.log(lb + (1.0 - lb) * _sigmoid(zf))
    k = (1.0 - lb) * _sigmoid(-zf)
    b = _chunk_cumsum(lf)

    t_idx = lax.broadcasted_iota(jnp.int32, (A_CHUNK, n_sub * A_CHUNK), 0)
    c_idx = lax.broadcasted_iota(jnp.int32, (A_CHUNK, n_sub * A_CHUNK), 1)
    sel = (c_idx // A_CHUNK == t_idx // A_SUB) & (c_idx % A_CHUNK <= t_idx)
    key_row = lax.broadcasted_iota(jnp.int32, (A_CHUNK, HD), 0)

    outs = []
    for n in range(rows // A_CHUNK):
        lo = n * A_CHUNK
        qc, kc, vc = q[lo:lo + A_CHUNK], k[lo:lo + A_CHUNK], v[lo:lo + A_CHUNK]
        lfc, bc = lf[lo:lo + A_CHUNK], b[lo:lo + A_CHUNK]
        b_last = bc[A_CHUNK - 1:A_CHUNK]
        o_inter = _dot_nt(qc * jnp.exp(bc), st)
        kv_t = _dot_tn(vc, kc * jnp.exp(b_last - bc))
        st = st * jnp.exp(b_last) + kv_t
        g_rows, k_subs = [], []
        for i in range(n_sub):
            g_i = bc[i * A_SUB:i * A_SUB + 1] - lfc[i * A_SUB:i * A_SUB + 1]
            g_rows.append(jnp.broadcast_to(g_i, (A_SUB, HD)))
            expo = jnp.where(key_row < (i + 1) * A_SUB, g_i - bc, -jnp.inf)
            k_subs.append(kc * jnp.exp(expo))
        q_sub = qc * jnp.exp(bc - jnp.concatenate(g_rows, axis=0))
        scores = _dot_nt(q_sub, jnp.concatenate(k_subs, axis=0))
        scores = jnp.where(sel, scores, 0.0)
        o_intra = _dot_nn(scores, jnp.concatenate([vc] * n_sub, axis=0))
        outs.append(o_inter + o_intra)
    o = jnp.concatenate(outs, axis=0)
    return _rms(o, onorm) * _silu(ga), st


def _gmlp_rows(u, vb, gb, lnw, lnb, ws, bias):
    rows = u.shape[0]
    mu = jnp.mean(vb, axis=-1, keepdims=True)
    xc = vb - mu
    vg = xc * lax.rsqrt(jnp.mean(xc * xc, axis=-1, keepdims=True) + EPS) * lnw + lnb
    r = lax.broadcasted_iota(jnp.int32, (B_CHUNK, B_CHUNK), 0)
    c = lax.broadcasted_iota(jnp.int32, (B_CHUNK, B_CHUNK), 1)
    ws_causal = jnp.where(r >= c, ws, 0.0)
    svs = [_dot_nn(ws_causal, vg[n * B_CHUNK:(n + 1) * B_CHUNK]) + bias
           for n in range(rows // B_CHUNK)]
    return u * jnp.concatenate(svs, axis=0) * _silu(gb)


def _rope(x, cos_t, sin_t):
    return x * cos_t + pltpu.roll(x, 64, 1) * sin_t


def _rope_transpose(g, cos_t, sin_t):
    return g * cos_t + pltpu.roll(g * sin_t, 64, 1)


ANY_SPEC = pl.BlockSpec(memory_space=pl.ANY)


def _live(deps):
    return [d for d in deps if d is not None]


def _skip_deps(body, n_in, n_deps):
    def wrapped(*refs):
        return body(*refs[:n_in], *refs[n_in + n_deps:])
    return wrapped


def _pure_call(name, fn, grid, in_specs, out_specs, out_shape, args, n_acc=0, deps=()):
    deps = _live(deps)
    n_in, n_out, n_deps = len(in_specs), len(out_specs), len(deps)
    in_specs = list(in_specs) + [ANY_SPEC] * n_deps
    args = tuple(args) + tuple(deps)

    def body(*refs):
        res = fn(*[r[...] for r in refs[:n_in]])
        if not isinstance(res, (tuple, list)):
            res = (res,)
        outs = refs[n_in + n_deps:n_in + n_deps + n_out]
        for o, r in zip(outs[:n_out - n_acc], res[:n_out - n_acc]):
            o[...] = r.astype(o.dtype)
        if n_acc:
            first = functools.reduce(jnp.logical_and, [pl.program_id(i) == 0 for i in range(len(grid))])
            for o, r in zip(outs[n_out - n_acc:], res[n_out - n_acc:]):
                @pl.when(first)
                def _(o=o, r=r):
                    o[...] = r.astype(o.dtype)

                @pl.when(jnp.logical_not(first))
                def _(o=o, r=r):
                    o[...] += r.astype(o.dtype)

    return pl.pallas_call(body, name=name, grid=grid, in_specs=in_specs, out_specs=out_specs,
                          out_shape=out_shape, compiler_params=_params(len(grid)))(*args)


def _sds(shape, dtype):
    return jax.ShapeDtypeStruct(shape, dtype)


def _row_spec(tm, width, col=0):
    return pl.BlockSpec((tm, width), lambda i, col=col: (i, col))


def _full_spec(shape):
    nd = len(shape)
    return pl.BlockSpec(shape, lambda *_: (0,) * nd)


def _mm_nn(name, a, b, out_dtype, tm, tn, deps=()):
    deps = _live(deps)
    m, k = a.shape
    j, _, n = b.shape
    per = n // tn

    def body(a_ref, b_ref, o_ref):
        o_ref[...] = _raw_nn(a_ref[...], b_ref[...]).astype(o_ref.dtype)

    return pl.pallas_call(
        _skip_deps(body, 2, len(deps)), name=name, grid=(m // tm, j * per),
        in_specs=[pl.BlockSpec((tm, k), lambda i, c: (i, 0)),
                  pl.BlockSpec((None, k, tn), lambda i, c: (c // per, 0, c % per))] + [ANY_SPEC] * len(deps),
        out_specs=pl.BlockSpec((tm, tn), lambda i, c: (i, c)),
        out_shape=_sds((m, j * n), out_dtype), compiler_params=_params(2))(a, b, *deps)


def _mm_nt(name, a, b, out_dtype, tm, tn, deps=()):
    deps = _live(deps)
    m = a.shape[0]
    j, nn, n = b.shape

    def body(a_ref, b_ref, o_ref, acc_ref):
        part = _raw_nt(a_ref[...], b_ref[...])
        if j == 1:
            o_ref[...] = part.astype(o_ref.dtype)
        else:
            kk = pl.program_id(2)

            @pl.when(kk == 0)
            def _():
                acc_ref[...] = part

            @pl.when(kk > 0)
            def _():
                acc_ref[...] += part

            @pl.when(kk == j - 1)
            def _():
                o_ref[...] = acc_ref[...].astype(o_ref.dtype)

    acc_shape = (tm, tn) if j > 1 else (8, 128)
    return pl.pallas_call(
        _skip_deps(body, 2, len(deps)), name=name, grid=(m // tm, nn // tn, j),
        in_specs=[pl.BlockSpec((tm, n), lambda i, c, kk: (i, kk)),
                  pl.BlockSpec((None, tn, n), lambda i, c, kk: (kk, c, 0))] + [ANY_SPEC] * len(deps),
        out_specs=pl.BlockSpec((tm, tn), lambda i, c, kk: (i, c)),
        out_shape=_sds((m, nn), out_dtype),
        scratch_shapes=[pltpu.VMEM(acc_shape, F32)], compiler_params=_params(3))(a, b, *deps)


def _mm_tn(name, a, b, j, out_dtype, tm, tn, deps=()):
    deps = _live(deps)
    k, m = a.shape
    n = b.shape[1] // j
    per = n // tn

    def body(a_ref, b_ref, o_ref):
        o_ref[...] = _raw_tn(a_ref[...], b_ref[...]).astype(o_ref.dtype)

    return pl.pallas_call(
        _skip_deps(body, 2, len(deps)), name=name, grid=(m // tm, j * per),
        in_specs=[pl.BlockSpec((k, tm), lambda i, c: (0, i)),
                  pl.BlockSpec((k, tn), lambda i, c: (0, c))] + [ANY_SPEC] * len(deps),
        out_specs=pl.BlockSpec((None, tm, tn), lambda i, c: (c // per, i, c % per)),
        out_shape=_sds((j, m, n), out_dtype), compiler_params=_params(2))(a, b, *deps)


TM = 256


def _pre_norm(name, x, w_row, deps=()):
    def fn(xv, w):
        return _rms(xv, w)
    return _pure_call(name, fn, (T // TM,), [_row_spec(TM, D), _full_spec((1, D))],
                      [_row_spec(TM, D)], [_sds((T, D), BF16)], (x, w_row), deps=deps)[0]


def _post_pre_norm(x, y, w_post, w_pre):
    def fn(xv, yv, wp, wn):
        x1 = xv + _rms(yv, wp)
        return x1, _rms(x1, wn)
    return _pure_call("post_pre_norm", fn, (T // TM,),
                      [_row_spec(TM, D), _row_spec(TM, D), _full_spec((1, D)), _full_spec((1, D))],
                      [_row_spec(TM, D), _row_spec(TM, D)],
                      [_sds((T, D), F32), _sds((T, D), BF16)], (x, y, w_post, w_pre))


def _post_pre_norm_bwd(y, x1, w_post, w_pre, dx1_in, dh1, deps=()):
    def fn(yv, x1v, wp, wn, dx1v, dh1v):
        _, vjp_pre = jax.vjp(_rms, x1v, wn)
        dx1_h, dwn = vjp_pre(dh1v)
        dx1 = dx1v + dx1_h
        _, vjp_post = jax.vjp(_rms, yv, wp)
        dy, dwp = vjp_post(dx1)
        return dx1, dy, dwp, dwn
    return _pure_call("post_pre_norm_bwd", fn, (T // TM,),
                      [_row_spec(TM, D), _row_spec(TM, D), _full_spec((1, D)), _full_spec((1, D)),
                       _row_spec(TM, D), _row_spec(TM, D)],
                      [_row_spec(TM, D), _row_spec(TM, D), _full_spec((1, D)), _full_spec((1, D))],
                      [_sds((T, D), F32), _sds((T, D), BF16), _sds((1, D), F32), _sds((1, D), F32)],
                      (y, x1, w_post, w_pre, dx1_in, dh1), n_acc=2, deps=deps)


def _final_loss(x1, y, w_post, target):
    def fn(x1v, yv, wp, tv):
        r, vjp = jax.vjp(_rms, yv, wp)
        err = x1v + r - tv
        part = 0.5 * jnp.sum(jnp.mean(err * err, axis=-1, keepdims=True), axis=0, keepdims=True)
        dx2 = err * (1.0 / D)
        dy, dwp = vjp(dx2)
        return dx2, dy, jnp.broadcast_to(part, (1, 128)), dwp
    return _pure_call("final_loss", fn, (T // TM,),
                      [_row_spec(TM, D), _row_spec(TM, D), _full_spec((1, D)), _row_spec(TM, D)],
                      [_row_spec(TM, D), _row_spec(TM, D), _full_spec((1, 128)), _full_spec((1, D))],
                      [_sds((T, D), F32), _sds((T, D), BF16), _sds((1, 128), F32), _sds((1, D), F32)],
                      (x1, y, w_post, target), n_acc=2)


def _pre_norm_bwd(x, w_row, dh, dx_res, deps=()):
    def fn(xv, w, dhv, dxv):
        _, vjp = jax.vjp(_rms, xv, w)
        dx, dw = vjp(dhv)
        return dxv + dx, dw
    return _pure_call("pre_norm_bwd", fn, (T // TM,),
                      [_row_spec(TM, D), _full_spec((1, D)), _row_spec(TM, D), _row_spec(TM, D)],
                      [_row_spec(TM, D), _full_spec((1, D))],
                      [_sds((T, D), F32), _sds((1, D), F32)], (x, w_row, dh, dx_res), n_acc=1, deps=deps)


RA = 256
RB = 512


def _col_spec(rows, col_of):
    return pl.BlockSpec((rows, HD), lambda h, r, col_of=col_of: (r, col_of(h)))


def _hgrn2_fwd(z, l0, l1, onorm):
    nb = T // RA

    def body(q_ref, f_ref, v_ref, g_ref, l0_ref, l1_ref, on_ref, cat_ref, sst_ref, st_scr):
        @pl.when(pl.program_id(1) == 0)
        def _():
            st_scr[...] = jnp.zeros_like(st_scr)

        st = st_scr[...]
        sst_ref[...] = st
        out, st_new = _hgrn2_rows(q_ref[...], f_ref[...], v_ref[...], g_ref[...], st,
                                  l0_ref[...], l1_ref[...], on_ref[...])
        cat_ref[...] = out.astype(cat_ref.dtype)
        st_scr[...] = st_new

    vec = pl.BlockSpec((1, HD), lambda h, r: (0, h))
    return pl.pallas_call(
        body, name="hgrn2_fwd", grid=(A_HEADS, nb),
        in_specs=[_col_spec(RA, lambda h: h), _col_spec(RA, lambda h: 8 + h),
                  _col_spec(RA, lambda h: 16 + h), _col_spec(RA, lambda h: 24 + h),
                  vec, vec, _full_spec((1, HD))],
        out_specs=[_col_spec(RA, lambda h: h),
                   pl.BlockSpec((None, None, HD, HD), lambda h, r: (h, r, 0, 0))],
        out_shape=[_sds((T, 2 * A_HEADS * HD), BF16), _sds((A_HEADS, nb, HD, HD), F32)],
        scratch_shapes=[pltpu.VMEM((HD, HD), F32)],
        compiler_params=_params(2))(z, z, z, z, l0, l1, onorm)


def _hgrn2_bwd(z, l0, l1, onorm, sst, dcat, deps=()):
    nb = T // RA
    deps = _live(deps)

    def body(q_ref, f_ref, v_ref, g_ref, l0_ref, l1_ref, on_ref, sst_ref, dcat_ref,
             dq_ref, df_ref, dv_ref, dg_ref, dl0_ref, dl1_ref, don_ref, ds_scr):
        h, r = pl.program_id(0), pl.program_id(1)

        @pl.when(r == 0)
        def _():
            ds_scr[...] = jnp.zeros_like(ds_scr)

        _, vjp = jax.vjp(_hgrn2_rows, q_ref[...], f_ref[...], v_ref[...], g_ref[...], sst_ref[...],
                         l0_ref[...], l1_ref[...], on_ref[...])
        dq, dzf, dv, dga, dst, dl0, dl1, don = vjp((dcat_ref[...], ds_scr[...]))
        dq_ref[...] = dq.astype(dq_ref.dtype)
        df_ref[...] = dzf.astype(df_ref.dtype)
        dv_ref[...] = dv.astype(dv_ref.dtype)
        dg_ref[...] = dga.astype(dg_ref.dtype)
        ds_scr[...] = dst

        @pl.when(r == 0)
        def _():
            dl0_ref[...] = dl0
            dl1_ref[...] = dl1

        @pl.when(r > 0)
        def _():
            dl0_ref[...] += dl0
            dl1_ref[...] += dl1

        first = jnp.logical_and(h == 0, r == 0)

        @pl.when(first)
        def _():
            don_ref[...] = don

        @pl.when(jnp.logical_not(first))
        def _():
            don_ref[...] += don

    def rev(col_of):
        return pl.BlockSpec((RA, HD), lambda h, r, col_of=col_of: (nb - 1 - r, col_of(h)))

    vec = pl.BlockSpec((1, HD), lambda h, r: (0, h))
    grad = _sds((T, A_HEADS * HD), BF16)
    return pl.pallas_call(
        _skip_deps(body, 9, len(deps)), name="hgrn2_bwd", grid=(A_HEADS, nb),
        in_specs=[rev(lambda h: h), rev(lambda h: 8 + h), rev(lambda h: 16 + h), rev(lambda h: 24 + h),
                  vec, vec, _full_spec((1, HD)),
                  pl.BlockSpec((None, None, HD, HD), lambda h, r: (h, nb - 1 - r, 0, 0)),
                  rev(lambda h: h)] + [ANY_SPEC] * len(deps),
        out_specs=[rev(lambda h: h)] * 4 + [vec, vec, _full_spec((1, HD))],
        out_shape=[grad] * 4 + [_sds((1, A_HEADS * HD), F32)] * 2 + [_sds((1, HD), F32)],
        scratch_shapes=[pltpu.VMEM((HD, HD), F32)],
        compiler_params=_params(2))(z, z, z, z, l0, l1, onorm, sst, dcat, *deps)


def _gmlp_specs():
    vec = pl.BlockSpec((1, HD), lambda g, r: (0, g))
    ws = pl.BlockSpec((None, B_CHUNK, B_CHUNK), lambda g, r: (g, 0, 0))
    bias = pl.BlockSpec((None, B_CHUNK, 1), lambda g, r: (g, 0, 0))
    return vec, ws, bias


def _gmlp_fwd(z, cat, lnw, lnb, ws, bias):
    vec, ws_spec, bias_spec = _gmlp_specs()

    def body(u_ref, v_ref, g_ref, lnw_ref, lnb_ref, ws_ref, bias_ref, cat_in_ref, cat_ref):
        del cat_in_ref
        out = _gmlp_rows(u_ref[...], v_ref[...], g_ref[...], lnw_ref[...], lnb_ref[...],
                         ws_ref[...], bias_ref[...])
        cat_ref[...] = out.astype(cat_ref.dtype)

    return pl.pallas_call(
        body, name="gmlp_fwd", grid=(B_GROUPS, T // RB),
        in_specs=[_col_spec(RB, lambda g: 32 + g), _col_spec(RB, lambda g: 40 + g),
                  _col_spec(RB, lambda g: 48 + g), vec, vec, ws_spec, bias_spec,
                  pl.BlockSpec(memory_space=pl.ANY)],
        out_specs=_col_spec(RB, lambda g: A_HEADS + g),
        out_shape=_sds(cat.shape, cat.dtype), input_output_aliases={7: 0},
        compiler_params=_params(2))(z, z, z, lnw, lnb, ws, bias, cat)


def _gmlp_bwd(z, lnw, lnb, ws, bias, dcat):
    vec, ws_spec, bias_spec = _gmlp_specs()

    def fn(u, vb, gb, w, b, wsv, bv, dout):
        _, vjp = jax.vjp(_gmlp_rows, u, vb, gb, w, b, wsv, bv)
        return vjp(dout)

    def body(*refs):
        ins, outs = refs[:8], refs[8:]
        res = fn(*[r[...] for r in ins])
        for o, r in zip(outs[:3], res[:3]):
            o[...] = r.astype(o.dtype)
        first = pl.program_id(1) == 0
        for o, r in zip(outs[3:], res[3:]):
            @pl.when(first)
            def _(o=o, r=r):
                o[...] = r

            @pl.when(jnp.logical_not(first))
            def _(o=o, r=r):
                o[...] += r

    grad = _sds((T, B_GROUPS * HD), BF16)
    row_out = pl.BlockSpec((RB, HD), lambda g, r: (r, g))
    return pl.pallas_call(
        body, name="gmlp_bwd", grid=(B_GROUPS, T // RB),
        in_specs=[_col_spec(RB, lambda g: 32 + g), _col_spec(RB, lambda g: 40 + g),
                  _col_spec(RB, lambda g: 48 + g), vec, vec, ws_spec, bias_spec,
                  _col_spec(RB, lambda g: A_HEADS + g)],
        out_specs=[row_out] * 3 + [vec, vec, ws_spec, bias_spec],
        out_shape=[grad] * 3 + [_sds((1, B_GROUPS * HD), F32)] * 2
        + [_sds((B_GROUPS, B_CHUNK, B_CHUNK), F32), _sds((B_GROUPS, B_CHUNK, 1), F32)],
        compiler_params=_params(2))(z, z, z, lnw, lnb, ws, bias, dcat)


def _mla_pre(z1, qn, kvn, cos_t, sin_t):
    def fn(cq, ckv, kpe, cs, sn, wq, wkv):
        return _rms(cq, wq), _rms(ckv, wkv), _rope(kpe, cs, sn)
    return _pure_call("mla_pre", fn, (T // TM,),
                      [_row_spec(TM, C_RANK, 4), _row_spec(TM, C_RANK, 5), _row_spec(TM, HD, 24),
                       _row_spec(TM, HD), _row_spec(TM, HD),
                       _full_spec((1, C_RANK)), _full_spec((1, C_RANK))],
                      [_row_spec(TM, C_RANK), _row_spec(TM, C_RANK), _row_spec(TM, HD)],
                      [_sds((T, C_RANK), BF16), _sds((T, C_RANK), BF16), _sds((T, HD), BF16)],
                      (z1, z1, z1, cos_t, sin_t, qn, kvn))


def _mla_pre_bwd(z1, qn, kvn, cos_t, sin_t, dcqn, dckvn, dkp, deps=()):
    def fn(cq, ckv, cs, sn, wq, wkv, g_q, g_kv, g_kp):
        _, vjp_q = jax.vjp(_rms, cq, wq)
        dcq, dwq = vjp_q(g_q)
        _, vjp_kv = jax.vjp(_rms, ckv, wkv)
        dckv, dwkv = vjp_kv(g_kv)
        return dcq, dckv, _rope_transpose(g_kp, cs, sn), dwq, dwkv
    return _pure_call("mla_pre_bwd", fn, (T // TM,),
                      [_row_spec(TM, C_RANK, 4), _row_spec(TM, C_RANK, 5),
                       _row_spec(TM, HD), _row_spec(TM, HD),
                       _full_spec((1, C_RANK)), _full_spec((1, C_RANK)),
                       _row_spec(TM, C_RANK), _row_spec(TM, C_RANK), _row_spec(TM, HD)],
                      [_row_spec(TM, C_RANK), _row_spec(TM, C_RANK), _row_spec(TM, HD),
                       _full_spec((1, C_RANK)), _full_spec((1, C_RANK))],
                      [_sds((T, C_RANK), BF16), _sds((T, C_RANK), BF16), _sds((T, HD), BF16),
                       _sds((1, C_RANK), F32), _sds((1, C_RANK), F32)],
                      (z1, z1, cos_t, sin_t, qn, kvn, dcqn, dckvn, dkp), n_acc=2, deps=deps)


def _gate_out(o, z1):
    def fn(ov, gate):
        return ov * _silu(gate)
    return _pure_call("gate_out", fn, (T // TM,), [_row_spec(TM, D), _row_spec(TM, D, 0)],
                      [_row_spec(TM, D)], [_sds((T, D), BF16)], (o, z1))[0]


def _gate_out_bwd(o, z1, dog, deps=()):
    def fn(ov, gate, g):
        _, vjp = jax.vjp(lambda a, b: a * _silu(b), ov, gate)
        return vjp(g)
    return _pure_call("gate_out_bwd", fn, (T // TM,),
                      [_row_spec(TM, D), _row_spec(TM, D, 0), _row_spec(TM, D)],
                      [_row_spec(TM, D), _row_spec(TM, D)],
                      [_sds((T, D), F32), _sds((T, D), BF16)], (o, z1, dog), deps=deps)


TQ = 256


def _att_keys(kn_ref, kp_ref, k_scr):
    @pl.when(pl.program_id(1) == 0)
    def _():
        k_scr[:, 0:C_NOPE] = kn_ref[...]
        k_scr[:, C_NOPE:QP] = kp_ref[...]


def _att_scores(q_ref, cos_ref, sin_ref, k_scr, n):
    keys = (n + 1) * TQ
    q = q_ref[...]
    qr = jnp.concatenate([q[:, :C_NOPE], _rope(q[:, C_NOPE:], cos_ref[...], sin_ref[...])], axis=1).astype(BF16)
    return qr, _raw_nt(qr, k_scr[0:keys, :]) * ATT_SCALE


def _causal(x, n, fill):
    row = lax.broadcasted_iota(jnp.int32, (TQ, TQ), 0)
    col = lax.broadcasted_iota(jnp.int32, (TQ, TQ), 1)
    diag = jnp.where(col <= row, x[:, n * TQ:], fill)
    return diag if n == 0 else jnp.concatenate([x[:, :n * TQ], diag], axis=1)


def _per_query_block(fn):
    for n in range(T // TQ):
        pl.when(pl.program_id(1) == n)(functools.partial(fn, n))


def _att_in_specs():
    return [pl.BlockSpec((TQ, QP), lambda h, i: (i, h)),
            pl.BlockSpec((TQ, HD), lambda h, i: (i, 0)),
            pl.BlockSpec((TQ, HD), lambda h, i: (i, 0)),
            pl.BlockSpec((T, C_NOPE), lambda h, i: (0, 2 * h)),
            pl.BlockSpec((T, HD), lambda h, i: (0, 0)),
            pl.BlockSpec((T, C_V), lambda h, i: (0, 2 * h + 1))]


def _attention_fwd(q, cos_t, sin_t, kv, kp):
    def body(q_ref, cos_ref, sin_ref, kn_ref, kp_ref, v_ref, o_ref, lse_ref, k_scr):
        _att_keys(kn_ref, kp_ref, k_scr)

        def block(n):
            _, s = _att_scores(q_ref, cos_ref, sin_ref, k_scr, n)
            s = _causal(s, n, jnp.finfo(F32).min)
            m = jnp.max(s, axis=-1, keepdims=True)
            p = jnp.exp(s - m)
            l = jnp.sum(p, axis=-1, keepdims=True)
            o_ref[...] = _raw_nn(p, v_ref[0:(n + 1) * TQ, :]) / l
            lse_ref[...] = m + jnp.log(l)

        _per_query_block(block)

    return pl.pallas_call(
        body, name="attention_fwd", grid=(C_HEADS, T // TQ), in_specs=_att_in_specs(),
        out_specs=[pl.BlockSpec((TQ, C_V), lambda h, i: (i, h)),
                   pl.BlockSpec((None, TQ, 1), lambda h, i: (h, i, 0))],
        out_shape=[_sds((T, C_HEADS * C_V), F32), _sds((C_HEADS, T, 1), F32)],
        scratch_shapes=[pltpu.VMEM((T, QP), BF16)],
        compiler_params=_params(2))(q, cos_t, sin_t, kv, kp, kv)


def _attention_bwd(q, cos_t, sin_t, kv, kp, o, lse, do):
    nq = T // TQ

    def body(q_ref, cos_ref, sin_ref, kn_ref, kp_ref, v_ref, o_ref, lse_ref, do_ref,
             dq_ref, dkv_ref, dkp_ref, k_scr, dk_scr, dv_scr):
        h, i = pl.program_id(0), pl.program_id(1)
        _att_keys(kn_ref, kp_ref, k_scr)

        @pl.when(i == 0)
        def _():
            dv_scr[...] = jnp.zeros_like(dv_scr)
            dk_scr[...] = jnp.zeros_like(dk_scr)

        def block(n):
            keys = (n + 1) * TQ
            qr, s = _att_scores(q_ref, cos_ref, sin_ref, k_scr, n)
            p = _causal(jnp.exp(s - lse_ref[...]), n, 0.0)
            dov = do_ref[...]
            delta = jnp.sum(dov * o_ref[...], axis=-1, keepdims=True)
            dp = _raw_nt(dov, v_ref[0:keys, :])
            ds = p * (dp - delta) * ATT_SCALE
            dq = _raw_nn(ds, k_scr[0:keys, :])
            dq_ref[...] = jnp.concatenate(
                [dq[:, :C_NOPE], _rope_transpose(dq[:, C_NOPE:], cos_ref[...], sin_ref[...])], axis=1).astype(dq_ref.dtype)
            dv_scr[0:keys, :] += _raw_tn(p, dov)
            dk_scr[0:keys, :] += _raw_tn(ds, qr)

        _per_query_block(block)

        @pl.when(i == nq - 1)
        def _():
            dkv_ref[...] = jnp.concatenate([dk_scr[:, 0:C_NOPE], dv_scr[...]], axis=1).astype(dkv_ref.dtype)

        @pl.when(jnp.logical_and(i == nq - 1, h == 0))
        def _():
            dkp_ref[...] = dk_scr[:, C_NOPE:QP]

        @pl.when(jnp.logical_and(i == nq - 1, h > 0))
        def _():
            dkp_ref[...] += dk_scr[:, C_NOPE:QP]

    return pl.pallas_call(
        body, name="attention_bwd", grid=(C_HEADS, nq),
        in_specs=_att_in_specs() + [pl.BlockSpec((TQ, C_V), lambda h, i: (i, h)),
                                    pl.BlockSpec((None, TQ, 1), lambda h, i: (h, i, 0)),
                                    pl.BlockSpec((TQ, C_V), lambda h, i: (i, h))],
        out_specs=[pl.BlockSpec((TQ, QP), lambda h, i: (i, h)),
                   pl.BlockSpec((T, C_NOPE + C_V), lambda h, i: (0, h)),
                   _full_spec((T, HD))],
        out_shape=[_sds((T, C_HEADS * QP), BF16), _sds((T, C_HEADS * (C_NOPE + C_V)), BF16),
                   _sds((T, HD), F32)],
        scratch_shapes=[pltpu.VMEM((T, QP), BF16), pltpu.VMEM((T, QP), F32), pltpu.VMEM((T, C_V), F32)],
        compiler_params=_params(2))(q, cos_t, sin_t, kv, kp, kv, o, lse, do)


def _adamw_math(w, g, m, v):
    m = ADAM_B1 * m + (1.0 - ADAM_B1) * g
    v = ADAM_B2 * v + (1.0 - ADAM_B2) * (g * g)
    m_hat = m / (1.0 - ADAM_B1 ** ADAM_STEP)
    v_hat = v / (1.0 - ADAM_B2 ** ADAM_STEP)
    delta = -ADAM_LR * (m_hat / (jnp.sqrt(v_hat) + ADAM_EPS) + ADAM_WD * w)
    return delta, m, v


def _adamw(name, parts, w, m, v, tr):
    rows, cols = w.shape
    n_parts = parts.shape[0]

    def fn(pv, wv, mv, vv):
        g = pv[0].astype(F32)
        for d in range(1, n_parts):
            g = g + pv[d].astype(F32)
        return (g,) + _adamw_math(wv, g, mv, vv)

    blk = pl.BlockSpec((tr, cols), lambda i: (i, 0))
    return _pure_call(name, fn, (rows // tr,),
                      [pl.BlockSpec((n_parts, tr, cols), lambda i: (0, i, 0)), blk, blk, blk],
                      [blk] * 4, [_sds((rows, cols), F32)] * 4, (parts, w, m, v))


def _exchange(name, arrs, gather, deps=()):
    n = len(arrs)
    deps = _live(deps)

    def body(*refs):
        ins, outs = refs[:n], refs[n + len(deps):2 * n + len(deps)]
        send_sems, recv_sems, local_sems = refs[2 * n + len(deps):]
        x, y, c = lax.axis_index("x"), lax.axis_index("y"), lax.axis_index("c")
        me = 4 * x + 2 * y + c

        def peer(k):
            return (x ^ (k >> 2), y ^ ((k >> 1) & 1), c ^ (k & 1))

        def copy(a, k):
            src = ins[a] if gather else ins[a].at[me ^ k]
            return pltpu.make_async_remote_copy(
                src_ref=src, dst_ref=outs[a].at[me], send_sem=send_sems.at[a, k - 1],
                recv_sem=recv_sems.at[a, k - 1], device_id=peer(k), device_id_type=MESH_ID)

        def arrival(a, k):
            src = ins[a] if gather else ins[a].at[me]
            return pltpu.make_async_remote_copy(
                src_ref=src, dst_ref=outs[a].at[me ^ k], send_sem=send_sems.at[a, k - 1],
                recv_sem=recv_sems.at[a, k - 1], device_id=peer(k), device_id_type=MESH_ID)

        own = [pltpu.make_async_copy(ins[a] if gather else ins[a].at[me], outs[a].at[me], local_sems.at[a])
               for a in range(n)]
        for cp in own:
            cp.start()
        for k in range(1, N_DEV):
            for a in range(n):
                copy(a, k).start()
        for k in range(1, N_DEV):
            for a in range(n):
                arrival(a, k).wait_recv()
        for k in range(1, N_DEV):
            for a in range(n):
                copy(a, k).wait_send()
        for cp in own:
            cp.wait()

    any_spec = pl.BlockSpec(memory_space=pl.ANY)
    out_shape = [_sds((N_DEV,) + a.shape if gather else a.shape, a.dtype) for a in arrs]
    return pl.pallas_call(
        body, name=name, in_specs=[any_spec] * (n + len(deps)), out_specs=[any_spec] * n, out_shape=out_shape,
        scratch_shapes=[pltpu.SemaphoreType.DMA((n, N_DEV - 1)), pltpu.SemaphoreType.DMA((n, N_DEV - 1)),
                        pltpu.SemaphoreType.DMA((n,))],
        compiler_params=pltpu.CompilerParams(has_side_effects=True))(*arrs, *deps)


HBM_SPEC = pl.BlockSpec(memory_space=pltpu.HBM)
SEM_SPEC = pl.BlockSpec(memory_space=pltpu.SEMAPHORE)
DATAFLOW = pltpu.SideEffectType.DATAFLOW_SIDE_EFFECTING


def _my_index():
    return 4 * lax.axis_index("x") + 2 * lax.axis_index("y") + lax.axis_index("c")


def _plan_copies(plan, refs, send_sems, recv_sems):
    x, y, c = lax.axis_index("x"), lax.axis_index("y"), lax.axis_index("c")
    return [pltpu.make_async_remote_copy(
        src_ref=src, dst_ref=dst, send_sem=send_sems.at[i], recv_sem=recv_sems.at[i],
        device_id=(x ^ (k >> 2), y ^ ((k >> 1) & 1), c ^ (k & 1)), device_id_type=MESH_ID)
        for i, (src, dst, k) in enumerate(plan(refs, 4 * x + 2 * y + c))]


def _split_call(name, bufs, waits=None, starts=None, deps=()):
    n = len(bufs)
    deps = _live(deps)
    n_wait = 2 if waits else 0

    def body(*refs):
        zones = refs[:n]
        if waits:
            for cp in _plan_copies(waits[2], zones, refs[n], refs[n + 1]):
                cp.wait_send()
                cp.wait_recv()
        if starts:
            first_out = n + n_wait + len(deps)
            for cp in _plan_copies(starts[0], zones, refs[first_out], refs[first_out + 1]):
                cp.start()
            refs[-1][...] = jnp.zeros_like(refs[-1])

    out_specs, out_shape = [], []
    if starts:
        sems = pltpu.SemaphoreType.DMA((starts[1],))
        out_specs, out_shape = [SEM_SPEC, SEM_SPEC], [sems, sems]
    out_specs += [HBM_SPEC] * n
    out_shape += [pltpu.HBM(b.shape, b.dtype) for b in bufs]
    if starts:
        out_specs.append(pl.BlockSpec(memory_space=pltpu.VMEM))
        out_shape.append(_sds((8, 128), F32))
    first_buf = 2 if starts else 0
    res = pl.pallas_call(
        body, name=name,
        in_specs=[HBM_SPEC] * n + [SEM_SPEC] * n_wait + [ANY_SPEC] * len(deps),
        out_specs=out_specs, out_shape=out_shape,
        input_output_aliases={i: first_buf + i for i in range(n)},
        compiler_params=pltpu.CompilerParams(has_side_effects=DATAFLOW),
    )(*[pltpu.with_memory_space_constraint(b, pltpu.HBM) for b in bufs], *(waits[:2] if waits else ()), *deps)
    out_bufs = list(res[first_buf:first_buf + n])
    return out_bufs, ((res[0], res[1]) if starts else None), (res[-1] if starts else None)


def _direct_plan(n, gather):
    def plan(refs, me):
        return [(refs[a] if gather else refs[a].at[me ^ k], refs[n + a].at[me], k)
                for k in range(1, N_DEV) for a in range(n)]
    return plan


def _exchange_start(name, arrs, gather, deps=()):
    n = len(arrs)
    me = _my_index()
    lands = []
    for a in arrs:
        if gather:
            land = lax.dynamic_update_slice_in_dim(lax.empty((N_DEV,) + a.shape, a.dtype), a[None], me, 0)
        else:
            land = lax.dynamic_update_slice_in_dim(lax.empty(a.shape, a.dtype),
                                                   lax.dynamic_slice_in_dim(a, me, 1, 0), me, 0)
        lands.append(land)
    plan = _direct_plan(n, gather)
    bufs, sems, token = _split_call(name, list(arrs) + lands, starts=(plan, n * (N_DEV - 1)), deps=deps)
    return (n, plan, sems, bufs), token


def _exchange_wait(name, handle, after):
    n, plan, sems, bufs = handle
    bufs, _, _ = _split_call(name, bufs, waits=(sems[0], sems[1], plan), deps=[after])
    return bufs[n:]


ICI_PEERS = (2, 4, 6)
SIBLING = 1


def _gather2_send(name, arrs, deps=()):
    n = len(arrs)
    me = _my_index()
    lands = [lax.dynamic_update_slice_in_dim(lax.empty((N_DEV,) + a.shape, a.dtype), a[None], me, 0) for a in arrs]

    def plan(refs, me_):
        return [(refs[a], refs[n + a].at[me_], k) for k in (SIBLING,) + ICI_PEERS for a in range(n)]

    bufs, sems, token = _split_call(name, list(arrs) + lands, starts=(plan, 4 * n), deps=deps)
    return (n, plan, sems, bufs), token


def _gather2_relay(name, handle, after):
    n, plan, sems, bufs = handle

    def relay(refs, me_):
        return [(refs[n + a].at[me_ ^ k], refs[n + a].at[me_ ^ k], SIBLING) for k in ICI_PEERS for a in range(n)]

    bufs, sems2, token = _split_call(name, bufs, waits=(sems[0], sems[1], plan), starts=(relay, 3 * n), deps=[after])
    return (n, relay, sems2, bufs), token


def _split_done(name, handle, after, all_bufs=False):
    n, plan, sems, bufs = handle
    bufs, _, _ = _split_call(name, bufs, waits=(sems[0], sems[1], plan), deps=[after])
    return bufs if all_bufs else bufs[n:]


def _scatter2_pair(name, stacks, deps=()):
    n = len(stacks)
    pairs = [lax.empty((4,) + s.shape[1:], s.dtype) for s in stacks]

    def plan(refs, me):
        return [(refs[a].at[(me ^ SIBLING) ^ (2 * j)], refs[n + a].at[j], SIBLING) for j in range(4) for a in range(n)]

    bufs, sems, token = _split_call(name, list(stacks) + pairs, starts=(plan, 4 * n), deps=deps)
    return (n, plan, sems, bufs), token


def _pair_add(name, stack, pair, me):
    _, rows, cols = stack.shape
    tr = rows // 8

    def body(me_ref, s_ref, p_ref, o_ref):
        del me_ref
        o_ref[...] = (s_ref[...].astype(F32) + p_ref[...].astype(F32)).astype(o_ref.dtype)

    grid_spec = pltpu.PrefetchScalarGridSpec(
        num_scalar_prefetch=1, grid=(4, rows // tr),
        in_specs=[pl.BlockSpec((None, tr, cols), lambda j, i, me_ref: (me_ref[0] ^ (2 * j), i, 0)),
                  pl.BlockSpec((None, tr, cols), lambda j, i, me_ref: (j, i, 0))],
        out_specs=pl.BlockSpec((None, tr, cols), lambda j, i, me_ref: (j, i, 0)))
    return pl.pallas_call(body, name=name, grid_spec=grid_spec, out_shape=_sds((4, rows, cols), stack.dtype),
                          compiler_params=_params(2))(me.reshape(1).astype(jnp.int32), stack, pair)


def _scatter2_send(name, chip_sums, deps=()):
    n = len(chip_sums)
    finals = [lax.dynamic_update_slice_in_dim(lax.empty(c.shape, c.dtype), c[0:1], 0, 0) for c in chip_sums]

    def plan(refs, me):
        del me
        return [(refs[a].at[j], refs[n + a].at[j], 2 * j) for j in range(1, 4) for a in range(n)]

    bufs, sems, token = _split_call(name, list(chip_sums) + finals, starts=(plan, 3 * n), deps=deps)
    return (n, plan, sems, bufs), token


def _pad_rope(p):
    z = jnp.zeros(p.shape[:-1] + (32,), p.dtype)
    return jnp.concatenate([p[..., :32], z, p[..., 32:], z], axis=-1)


def _unpad_rope(p):
    return jnp.concatenate([p[..., :32], p[..., 64:96]], axis=-1)


def _odd_in_layout(w):
    w = w.transpose(1, 0, 2).reshape(D, ODD_IN)
    cq, ckv, kpe, gate = w[:, :512], w[:, 512:1024], w[:, 1024:1088], w[:, 1088:]
    return jnp.concatenate([gate, cq, ckv, _pad_rope(kpe)], axis=1)


def _odd_in_unlayout(dw):
    gate, cq, ckv, kpe = dw[:, :2048], dw[:, 2048:2560], dw[:, 2560:3072], _unpad_rope(dw[:, 3072:])
    w = jnp.concatenate([cq, ckv, kpe, gate], axis=1)
    return w.reshape(D, N_DEV, ODD_IN // N_DEV).transpose(1, 0, 2)


def _qb_layout(w):
    w = w.transpose(1, 0, 2).reshape(C_RANK, C_HEADS, C_QK)
    w = jnp.concatenate([w[..., :C_NOPE], _pad_rope(w[..., C_NOPE:])], axis=-1)
    return w.reshape(C_RANK, C_HEADS * QP)


def _qb_unlayout(dw):
    dw = dw.reshape(C_RANK, C_HEADS, QP)
    dw = jnp.concatenate([dw[..., :C_NOPE], _unpad_rope(dw[..., C_NOPE:])], axis=-1)
    return dw.reshape(C_RANK, N_DEV, C_HEADS * C_QK // N_DEV).transpose(1, 0, 2)


def _rope_tables(positions):
    inv_freq = ROPE_THETA ** (-jnp.arange(0, C_ROPE, 2, dtype=F32) / C_ROPE)
    ang = positions.astype(F32)[0][:, None] * inv_freq
    cos, sin = jnp.cos(ang), jnp.sin(ang)
    z = jnp.zeros_like(cos)
    return jnp.concatenate([cos, z, cos, z], axis=1), jnp.concatenate([-sin, z, sin, z], axis=1)


SMALL_ROWS = 1144


def _pack_rows(parts):
    flat = jnp.concatenate([p.reshape(-1) for p in parts])
    rows = -(-flat.shape[0] // 1024) * 8
    return jnp.pad(flat, (0, rows * 128 - flat.shape[0])).reshape(rows, 128)


def _pack_small(parts):
    flat = jnp.concatenate([p.reshape(-1) for p in parts])
    return jnp.pad(flat, (0, SMALL_ROWS * 128 - flat.shape[0])).reshape(SMALL_ROWS, 128)


def _unpack_small(packed, shapes):
    flat = packed.reshape(-1)
    out, off = [], 0
    for s in shapes:
        size = 1
        for d in s:
            size *= d
        out.append(flat[off:off + size].reshape(s))
        off += size
    return out


def _forward_backward(x, cos_t, sin_t, target, norm_pre, norm_post, lb_logits, a_onorm, ln_w, ln_b,
                      b_ws, b_bias, get_w, put_g, put_small=None, start_dep=None):
    npre0, npre1 = norm_pre[0:1], norm_pre[1:2]
    npost0, npost1 = norm_post[0:1], norm_post[1:2]
    l0, l1 = lb_logits[0:1], lb_logits[1:2]
    bias_col = b_bias.reshape(B_GROUPS, B_CHUNK, 1)
    ws = b_ws.reshape(B_GROUPS, B_CHUNK, B_CHUNK)

    h0 = _pre_norm("pre_norm0", x, npre0, deps=[start_dep])
    w_ev_in = get_w("ev_in", h0)
    z0 = _mm_nn("ev_in", h0, w_ev_in, F32, 1024, 896)
    cat, sst = _hgrn2_fwd(z0, l0, l1, a_onorm)
    cat = _gmlp_fwd(z0, cat, ln_w, ln_b, ws, bias_col)
    w_ev_out = get_w("ev_out", cat)
    y0 = _mm_nn("ev_out", cat, w_ev_out, F32, 1024, 1024)
    x1, h1 = _post_pre_norm(x, y0, npost0, npre1)
    w_od_in, w_qb, w_kvb, q_norm, kv_norm = get_w("od_mid", h1)
    z1 = _mm_nn("od_in", h1, w_od_in[None], F32, 1024, 640)
    cqn, ckvn, kp = _mla_pre(z1, q_norm, kv_norm, cos_t, sin_t)
    q = _mm_nn("od_qb", cqn, w_qb[None], F32, 1024, 1024)
    kv = _mm_nn("od_kvb", ckvn, w_kvb, BF16, 1024, 512)
    o, lse = _attention_fwd(q, cos_t, sin_t, kv, kp)
    og = _gate_out(o, z1)
    w_od_out = get_w("od_out", og)
    y1 = _mm_nn("od_out", og, w_od_out, F32, 1024, 1024)
    dx2, dy1, loss_part, dnpost1 = _final_loss(x1, y1, npost1, target)

    g_od_out = _mm_tn("od_out_dw", og, dy1, 1, BF16, 1024, 1024)
    tok = put_g("od_out", [g_od_out.reshape(N_DEV, D // N_DEV, D)])
    dog = _mm_nt("od_out_dx", dy1, w_od_out, F32, 1024, 1024, deps=[tok])
    do, dgate = _gate_out_bwd(o, z1, dog)
    dq, dkv, dkp = _attention_bwd(q, cos_t, sin_t, kv, kp, o, lse, do)
    g_qb = _mm_tn("od_qb_dw", cqn, dq, 1, F32, 512, 1024)
    g_kvb = _mm_tn("od_kvb_dw", ckvn, dkv, N_DEV, BF16, 512, 512)
    tok = put_g("od_qkv", [_qb_unlayout(g_qb[0]).astype(BF16), g_kvb])
    dcqn = _mm_nt("od_qb_dx", dq, w_qb[None], F32, 1024, 512, deps=[tok])
    dckvn = _mm_nt("od_kvb_dx", dkv, w_kvb, F32, 1024, 512)
    dcq, dckv, dkpe, dqn, dkvn = _mla_pre_bwd(z1, q_norm, kv_norm, cos_t, sin_t, dcqn, dckvn, dkp)
    dz1 = jnp.concatenate([dgate, dcq, dckv, dkpe], axis=1)
    g_od_in = _mm_tn("od_in_dw", h1, dz1, 1, F32, 1024, 640)
    tok = put_g("od_in", [_odd_in_unlayout(g_od_in[0]).astype(BF16)])
    dh1 = _mm_nt("od_in_dx", dz1, w_od_in[None], F32, 1024, 1024, deps=[tok])
    dx1, dy0, dnpost0, dnpre1 = _post_pre_norm_bwd(y0, x1, npost0, npre1, dx2, dh1)

    g_ev_out = _mm_tn("ev_out_dw", cat, dy0, 1, BF16, 1024, 1024)
    tok = put_g("ev_out", [g_ev_out.reshape(N_DEV, D // N_DEV, D)])
    dcat = _mm_nt("ev_out_dx", dy0, w_ev_out, F32, 1024, 1024, deps=[tok])
    dqa, dfa, dia, dga, dl0, dl1, donorm = _hgrn2_bwd(z0, l0, l1, a_onorm, sst, dcat)
    dub, dvb, dgb, dlnw, dlnb, dws, dbias = _gmlp_bwd(z0, ln_w, ln_b, ws, bias_col, dcat)
    dz0 = jnp.concatenate([dqa, dfa, dia, dga, dub, dvb, dgb], axis=1)
    rest_small = [jnp.concatenate([dnpost0, dnpost1], axis=0), jnp.concatenate([dl0, dl1], axis=0), donorm, dlnw,
                  dlnb, dws.reshape(1, B_GROUPS, B_CHUNK, B_CHUNK), dbias.reshape(1, B_GROUPS, B_CHUNK), dqn, dkvn]
    tok = put_small([dnpre1] + rest_small) if put_small else None
    g_ev_in = _mm_tn("ev_in_dw", h0, dz0, N_DEV, BF16, 1024, 896, deps=[tok])
    tok = put_g("ev_in", [g_ev_in])
    dh0 = _mm_nt("ev_in_dx", dz0, w_ev_in, F32, 1024, 1024, deps=[tok])
    grad_x, dnpre0 = _pre_norm_bwd(x, npre0, dh0, dx1)

    small = [jnp.concatenate([dnpre0, dnpre1], axis=0)] + rest_small
    return loss_part[0, 0], grad_x, small


def kernel(x, positions, norm_pre, norm_post, ev_w_in, ev_lb_logits, ev_a_onorm, ev_b_ln_w, ev_b_ln_b, ev_b_ws, ev_b_bias, ev_w_out, od_w_in, od_q_norm, od_w_qb, od_kv_norm, od_w_kvb, od_w_out, loss_target, m_norm_pre, m_norm_post, m_ev_w_in, m_ev_lb_logits, m_ev_a_onorm, m_ev_b_ln_w, m_ev_b_ln_b, m_ev_b_ws, m_ev_b_bias, m_ev_w_out, m_od_w_in, m_od_q_norm, m_od_w_qb, m_od_kv_norm, m_od_w_kvb, m_od_w_out, v_norm_pre, v_norm_post, v_ev_w_in, v_ev_lb_logits, v_ev_a_onorm, v_ev_b_ln_w, v_ev_b_ln_b, v_ev_b_ws, v_ev_b_bias, v_ev_w_out, v_od_w_in, v_od_q_norm, v_od_w_qb, v_od_kv_norm, v_od_w_kvb, v_od_w_out):
    me = 4 * lax.axis_index("x") + 2 * lax.axis_index("y") + lax.axis_index("c")
    bf = lambda w: w[0].astype(BF16)

    norms = jnp.pad(jnp.concatenate([od_q_norm, od_kv_norm], axis=1), ((0, 7), (0, 0)))
    first_h, tok = _gather2_send("gather_ev_in", [bf(ev_w_in)])
    rest_h, tok = _gather2_send("gather_rest", [bf(ev_w_out), bf(od_w_in), bf(od_w_qb), bf(od_w_kvb), norms,
                                                bf(od_w_out)], deps=[tok])
    first_h, tok = _gather2_relay("relay_ev_in", first_h, tok)
    rest = []

    def get_w(group, after):
        if group == "ev_in":
            return _split_done("arrived_ev_in", first_h, after)[0]
        if not rest:
            relayed, token = _gather2_relay("relay_rest", rest_h, after)
            rest.extend(_split_done("arrived_rest", relayed, token))
        w_ev_out, w_od_in, w_qb, w_kvb, norms_all, w_od_out = rest
        if group == "ev_out":
            return w_ev_out.reshape(1, D, D)
        if group == "od_out":
            return w_od_out.reshape(1, D, D)
        return (_odd_in_layout(w_od_in), _qb_layout(w_qb), w_kvb,
                norms_all[:, 0, :64].reshape(1, C_RANK), norms_all[:, 0, 64:].reshape(1, C_RANK))

    scatters = {}

    def put_g(group, grads):
        if group == "ev_in":
            paired, token = _scatter2_pair("pair_ev_in", grads)
            n = len(grads)
            bufs = _split_done("paired_ev_in", paired, token, all_bufs=True)
            chip_sums = [_pair_add("pair_add_ev_in", bufs[a], bufs[n + a], me) for a in range(n)]
            scatters[group], token = _scatter2_send("scatter_ev_in", chip_sums)
        else:
            scatters[group], token = _exchange_start("scatter_" + group, grads, False)
        return token

    def put_small(early):
        scatters["small"], token = _exchange_start("gather_small_early", [_pack_rows(early)], True)
        return token

    cos_t, sin_t = _rope_tables(positions)
    loss_part, grad_x, small_g = _forward_backward(
        x[0], cos_t, sin_t, loss_target[0], norm_pre, norm_post, ev_lb_logits, ev_a_onorm, ev_b_ln_w,
        ev_b_ln_b, ev_b_ws, ev_b_bias, get_w, put_g, put_small, start_dep=tok)
    loss = lax.psum(loss_part, ("x", "y", "c"))

    big_w = {"ev_w_in": ev_w_in, "ev_w_out": ev_w_out, "od_w_in": od_w_in, "od_w_qb": od_w_qb,
             "od_w_kvb": od_w_kvb, "od_w_out": od_w_out}
    big_m = {"ev_w_in": m_ev_w_in, "ev_w_out": m_ev_w_out, "od_w_in": m_od_w_in, "od_w_qb": m_od_w_qb,
             "od_w_kvb": m_od_w_kvb, "od_w_out": m_od_w_out}
    big_v = {"ev_w_in": v_ev_w_in, "ev_w_out": v_ev_w_out, "od_w_in": v_od_w_in, "od_w_qb": v_od_w_qb,
             "od_w_kvb": v_od_w_kvb, "od_w_out": v_od_w_out}
    big_out = {}
    after = grad_x
    for group, names in (("od_out", ["od_w_out"]), ("od_qkv", ["od_w_qb", "od_w_kvb"]), ("od_in", ["od_w_in"]),
                         ("ev_out", ["ev_w_out"])):
        parts = _exchange_wait("summed_" + group, scatters[group], after)
        for nm, p in zip(names, parts):
            w = big_w[nm][0]
            big_out[nm] = [r[None] for r in _adamw("adamw_" + nm, p, w, big_m[nm][0], big_v[nm][0], w.shape[0] // 8)]
            after = big_out[nm][0]

    late = _pack_rows([small_g[0][0:1]])
    late_all = _exchange("gather_small_late", [late], gather=True, deps=[after])[0]
    early_all = _exchange_wait("arrived_small_early", scatters["small"], late_all)[0]
    small_all = jnp.concatenate([late_all, early_all], axis=1)

    small_w = (norm_pre, norm_post, ev_lb_logits, ev_a_onorm, ev_b_ln_w, ev_b_ln_b, ev_b_ws, ev_b_bias)
    small_m = (m_norm_pre, m_norm_post, m_ev_lb_logits, m_ev_a_onorm, m_ev_b_ln_w, m_ev_b_ln_b, m_ev_b_ws, m_ev_b_bias)
    small_v = (v_norm_pre, v_norm_post, v_ev_lb_logits, v_ev_a_onorm, v_ev_b_ln_w, v_ev_b_ln_b, v_ev_b_ws, v_ev_b_bias)
    zero512 = jnp.zeros((1, C_RANK), F32)

    def packed_of(ws_):
        return _pack_small(list(ws_) + [zero512, zero512])

    res = _adamw("adamw_small", small_all, packed_of(small_w), packed_of(small_m), packed_of(small_v), SMALL_ROWS // 11)
    shapes = [w.shape for w in small_w] + [(1, C_RANK), (1, C_RANK)]
    g_s, d_s, m_s, v_s = (_unpack_small(r, shapes) for r in res)

    g_norms = jnp.concatenate([lax.dynamic_slice(g_s[8], (0, 64 * me), (1, 64)),
                               lax.dynamic_slice(g_s[9], (0, 64 * me), (1, 64))], axis=1)
    g_norms8 = jnp.concatenate([g_norms[None], jnp.zeros((N_DEV - 1, 1, 128), F32)], axis=0)
    res_n = _adamw("adamw_norms", g_norms8,
                   jnp.concatenate([od_q_norm, od_kv_norm], axis=1),
                   jnp.concatenate([m_od_q_norm, m_od_kv_norm], axis=1),
                   jnp.concatenate([v_od_q_norm, v_od_kv_norm], axis=1), 1)
    qn_out = [r[:, :64] for r in res_n]
    kvn_out = [r[:, 64:] for r in res_n]

    parts = _split_done("summed_ev_in", scatters["ev_in"], res[0])
    w = ev_w_in[0]
    big_out["ev_w_in"] = [r[None] for r in _adamw("adamw_ev_w_in", parts[0], w, m_ev_w_in[0], v_ev_w_in[0],
                                                  w.shape[0] // 8)]

    order = ("norm_pre", "norm_post", "ev_w_in", "ev_lb_logits", "ev_a_onorm", "ev_b_ln_w", "ev_b_ln_b",
             "ev_b_ws", "ev_b_bias", "ev_w_out", "od_w_in", "od_q_norm", "od_w_qb", "od_kv_norm",
             "od_w_kvb", "od_w_out")
    small_names = ("norm_pre", "norm_post", "ev_lb_logits", "ev_a_onorm", "ev_b_ln_w", "ev_b_ln_b",
                   "ev_b_ws", "ev_b_bias")
    outs = [loss, grad_x[None]]
    for kind in range(4):
        small_kind = (g_s, d_s, m_s, v_s)[kind]
        for nm in order:
            if nm in big_out:
                outs.append(big_out[nm][kind])
            elif nm == "od_q_norm":
                outs.append(qn_out[kind])
            elif nm == "od_kv_norm":
                outs.append(kvn_out[kind])
            else:
                outs.append(small_kind[small_names.index(nm)])
    return tuple(outs)
```

```python
import functools

import jax
import jax.numpy as jnp
from jax import lax
from jax.experimental import pallas as pl
from jax.experimental.pallas import tpu as pltpu

F32 = jnp.float32
BF16 = jnp.bfloat16

N_DEV = 8
T = 2048
D = 2048
EPS = 1e-6
A_HEADS = 8
HD = 128
A_CHUNK = 64
A_SUB = 16
B_GROUPS = 8
B_CHUNK = 128
EVEN_IN = 7168
C_HEADS = 16
C_RANK = 512
C_NOPE = 128
C_ROPE = 64
C_QK = C_NOPE + C_ROPE
C_V = 128
ODD_IN = 3136
ODD_IN_PAD = 3200
QP = 256
ROPE_THETA = 10000.0
ATT_SCALE = C_QK ** -0.5

ADAM_LR = 0.001
ADAM_B1 = 0.9
ADAM_B2 = 0.999
ADAM_EPS = 1e-08
ADAM_WD = 0.01
ADAM_STEP = 10

VMEM_LIMIT_V7X = 56 * 1024 * 1024
MESH_ID = pl.DeviceIdType.MESH


def _params(n_grid):
    return pltpu.CompilerParams(dimension_semantics=("arbitrary",) * n_grid,
                                vmem_limit_bytes=VMEM_LIMIT_V7X)


def _dg(a, b, ca, cb):
    return lax.dot_general(a.astype(BF16), b.astype(BF16), (((ca,), (cb,)), ((), ())),
                           preferred_element_type=F32)


def _raw_nn(a, b):
    return _dg(a, b, 1, 0)


def _raw_nt(a, b):
    return _dg(a, b, 1, 1)


def _raw_tn(a, b):
    return _dg(a, b, 0, 0)


@jax.custom_vjp
def _dot_nn(a, b):
    return _raw_nn(a, b)


def _dot_nn_fwd(a, b):
    return _raw_nn(a, b), (a.astype(BF16), b.astype(BF16))


def _dot_nn_bwd(res, g):
    a, b = res
    return _raw_nt(g, b), _raw_tn(a, g)


_dot_nn.defvjp(_dot_nn_fwd, _dot_nn_bwd)


@jax.custom_vjp
def _dot_nt(a, b):
    return _raw_nt(a, b)


def _dot_nt_fwd(a, b):
    return _raw_nt(a, b), (a.astype(BF16), b.astype(BF16))


def _dot_nt_bwd(res, g):
    a, b = res
    return _raw_nn(g, b), _raw_tn(g, a)


_dot_nt.defvjp(_dot_nt_fwd, _dot_nt_bwd)


@jax.custom_vjp
def _dot_tn(a, b):
    return _raw_tn(a, b)


def _dot_tn_fwd(a, b):
    return _raw_tn(a, b), (a.astype(BF16), b.astype(BF16))


def _dot_tn_bwd(res, g):
    a, b = res
    return _raw_nt(b, g), _raw_nn(a, g)


_dot_tn.defvjp(_dot_tn_fwd, _dot_tn_bwd)


@jax.custom_vjp
def _sigmoid(x):
    e = jnp.exp(-jnp.abs(x))
    return jnp.where(x >= 0, 1.0 / (1.0 + e), e / (1.0 + e))


def _sigmoid_fwd(x):
    s = _sigmoid(x)
    return s, s


def _sigmoid_bwd(s, g):
    return (g * s * (1.0 - s),)


_sigmoid.defvjp(_sigmoid_fwd, _sigmoid_bwd)


def _silu(x):
    return x * _sigmoid(x)


def _rms(x, w):
    return x * lax.rsqrt(jnp.mean(x * x, axis=-1, keepdims=True) + EPS) * w


def _split3(x):
    hi = x.astype(BF16)
    r = x - hi.astype(F32)
    mid = r.astype(BF16)
    lo = (r - mid.astype(F32)).astype(BF16)
    return hi, mid, lo


def _mask_apply(mask_bf16, x, contract):
    out = None
    for piece in _split3(x):
        d = lax.dot_general(mask_bf16, piece, (((contract,), (0,)), ((), ())),
                            preferred_element_type=F32)
        out = d if out is None else out + d
    return out


def _chunk_tri(rows):
    r = lax.broadcasted_iota(jnp.int32, (rows, rows), 0)
    c = lax.broadcasted_iota(jnp.int32, (rows, rows), 1)
    return ((r >= c) & (r // A_CHUNK == c // A_CHUNK)).astype(BF16)


@jax.custom_vjp
def _chunk_cumsum(x):
    return _mask_apply(_chunk_tri(x.shape[0]), x, 1)


def _chunk_cumsum_fwd(x):
    return _chunk_cumsum(x), None


def _chunk_cumsum_bwd(_, g):
    return (_mask_apply(_chunk_tri(g.shape[0]), g, 0),)


_chunk_cumsum.defvjp(_chunk_cumsum_fwd, _chunk_cumsum_bwd)


def _hgrn2_rows(q, zf, v, ga, st, l0, l1, onorm):
    rows = q.shape[0]
    n_sub = A_CHUNK // A_SUB
    mx = jnp.maximum(l0, l1)
    e0 = jnp.exp(l0 - mx)
    e1 = jnp.exp(l1 - mx)
    lb = e0 / (e0 + e1)
    lf = jnp.log(lb + (1.0 - lb) * _sigmoid(zf))
    k = (1.0 - lb) * _sigmoid(-zf)
    b = _chunk_cumsum(lf)

    t_idx = lax.broadcasted_iota(jnp.int32, (A_CHUNK, n_sub * A_CHUNK), 0)
    c_idx = lax.broadcasted_iota(jnp.int32, (A_CHUNK, n_sub * A_CHUNK), 1)
    sel = (c_idx // A_CHUNK == t_idx // A_SUB) & (c_idx % A_CHUNK <= t_idx)
    key_row = lax.broadcasted_iota(jnp.int32, (A_CHUNK, HD), 0)

    outs = []
    for n in range(rows // A_CHUNK):
        lo = n * A_CHUNK
        qc, kc, vc = q[lo:lo + A_CHUNK], k[lo:lo + A_CHUNK], v[lo:lo + A_CHUNK]
        lfc, bc = lf[lo:lo + A_CHUNK], b[lo:lo + A_CHUNK]
        b_last = bc[A_CHUNK - 1:A_CHUNK]
        o_inter = _dot_nt(qc * jnp.exp(bc), st)
        kv_t = _dot_tn(vc, kc * jnp.exp(b_last - bc))
        st = st * jnp.exp(b_last) + kv_t
        g_rows, k_subs = [], []
        for i in range(n_sub):
            g_i = bc[i * A_SUB:i * A_SUB + 1] - lfc[i * A_SUB:i * A_SUB + 1]
            g_rows.append(jnp.broadcast_to(g_i, (A_SUB, HD)))
            expo = jnp.where(key_row < (i + 1) * A_SUB, g_i - bc, -jnp.inf)
            k_subs.append(kc * jnp.exp(expo))
        q_sub = qc * jnp.exp(bc - jnp.concatenate(g_rows, axis=0))
        scores = _dot_nt(q_sub, jnp.concatenate(k_subs, axis=0))
        scores = jnp.where(sel, scores, 0.0)
        o_intra = _dot_nn(scores, jnp.concatenate([vc] * n_sub, axis=0))
        outs.append(o_inter + o_intra)
    o = jnp.concatenate(outs, axis=0)
    return _rms(o, onorm) * _silu(ga), st


def _gmlp_rows(u, vb, gb, lnw, lnb, ws, bias):
    rows = u.shape[0]
    mu = jnp.mean(vb, axis=-1, keepdims=True)
    xc = vb - mu
    vg = xc * lax.rsqrt(jnp.mean(xc * xc, axis=-1, keepdims=True) + EPS) * lnw + lnb
    r = lax.broadcasted_iota(jnp.int32, (B_CHUNK, B_CHUNK), 0)
    c = lax.broadcasted_iota(jnp.int32, (B_CHUNK, B_CHUNK), 1)
    ws_causal = jnp.where(r >= c, ws, 0.0)
    svs = [_dot_nn(ws_causal, vg[n * B_CHUNK:(n + 1) * B_CHUNK]) + bias
           for n in range(rows // B_CHUNK)]
    return u * jnp.concatenate(svs, axis=0) * _silu(gb)


def _rope(x, cos_t, sin_t):
    return x * cos_t + pltpu.roll(x, 64, 1) * sin_t


def _rope_transpose(g, cos_t, sin_t):
    return g * cos_t + pltpu.roll(g * sin_t, 64, 1)


ANY_SPEC = pl.BlockSpec(memory_space=pl.ANY)


def _live(deps):
    return [d for d in deps if d is not None]


def _skip_deps(body, n_in, n_deps):
    def wrapped(*refs):
        return body(*refs[:n_in], *refs[n_in + n_deps:])
    return wrapped


def _pure_call(name, fn, grid, in_specs, out_specs, out_shape, args, n_acc=0, deps=()):
    deps = _live(deps)
    n_in, n_out, n_deps = len(in_specs), len(out_specs), len(deps)
    in_specs = list(in_specs) + [ANY_SPEC] * n_deps
    args = tuple(args) + tuple(deps)

    def body(*refs):
        res = fn(*[r[...] for r in refs[:n_in]])
        if not isinstance(res, (tuple, list)):
            res = (res,)
        outs = refs[n_in + n_deps:n_in + n_deps + n_out]
        for o, r in zip(outs[:n_out - n_acc], res[:n_out - n_acc]):
            o[...] = r.astype(o.dtype)
        if n_acc:
            first = functools.reduce(jnp.logical_and, [pl.program_id(i) == 0 for i in range(len(grid))])
            for o, r in zip(outs[n_out - n_acc:], res[n_out - n_acc:]):
                @pl.when(first)
                def _(o=o, r=r):
                    o[...] = r.astype(o.dtype)

                @pl.when(jnp.logical_not(first))
                def _(o=o, r=r):
                    o[...] += r.astype(o.dtype)

    return pl.pallas_call(body, name=name, grid=grid, in_specs=in_specs, out_specs=out_specs,
                          out_shape=out_shape, compiler_params=_params(len(grid)))(*args)


def _sds(shape, dtype):
    return jax.ShapeDtypeStruct(shape, dtype)


def _row_spec(tm, width, col=0):
    return pl.BlockSpec((tm, width), lambda i, col=col: (i, col))


def _full_spec(shape):
    nd = len(shape)
    return pl.BlockSpec(shape, lambda *_: (0,) * nd)


def _mm_nn(name, a, b, out_dtype, tm, tn, deps=()):
    deps = _live(deps)
    m, k = a.shape
    j, _, n = b.shape
    per = n // tn

    def body(a_ref, b_ref, o_ref):
        o_ref[...] = _raw_nn(a_ref[...], b_ref[...]).astype(o_ref.dtype)

    return pl.pallas_call(
        _skip_deps(body, 2, len(deps)), name=name, grid=(m // tm, j * per),
        in_specs=[pl.BlockSpec((tm, k), lambda i, c: (i, 0)),
                  pl.BlockSpec((None, k, tn), lambda i, c: (c // per, 0, c % per))] + [ANY_SPEC] * len(deps),
        out_specs=pl.BlockSpec((tm, tn), lambda i, c: (i, c)),
        out_shape=_sds((m, j * n), out_dtype), compiler_params=_params(2))(a, b, *deps)


def _mm_nt(name, a, b, out_dtype, tm, tn, deps=()):
    deps = _live(deps)
    m = a.shape[0]
    j, nn, n = b.shape

    def body(a_ref, b_ref, o_ref, acc_ref):
        part = _raw_nt(a_ref[...], b_ref[...])
        if j == 1:
            o_ref[...] = part.astype(o_ref.dtype)
        else:
            kk = pl.program_id(2)

            @pl.when(kk == 0)
            def _():
                acc_ref[...] = part

            @pl.when(kk > 0)
            def _():
                acc_ref[...] += part

            @pl.when(kk == j - 1)
            def _():
                o_ref[...] = acc_ref[...].astype(o_ref.dtype)

    acc_shape = (tm, tn) if j > 1 else (8, 128)
    return pl.pallas_call(
        _skip_deps(body, 2, len(deps)), name=name, grid=(m // tm, nn // tn, j),
        in_specs=[pl.BlockSpec((tm, n), lambda i, c, kk: (i, kk)),
                  pl.BlockSpec((None, tn, n), lambda i, c, kk: (kk, c, 0))] + [ANY_SPEC] * len(deps),
        out_specs=pl.BlockSpec((tm, tn), lambda i, c, kk: (i, c)),
        out_shape=_sds((m, nn), out_dtype),
        scratch_shapes=[pltpu.VMEM(acc_shape, F32)], compiler_params=_params(3))(a, b, *deps)


def _mm_tn(name, a, b, j, out_dtype, tm, tn, deps=()):
    deps = _live(deps)
    k, m = a.shape
    n = b.shape[1] // j
    per = n // tn

    def body(a_ref, b_ref, o_ref):
        o_ref[...] = _raw_tn(a_ref[...], b_ref[...]).astype(o_ref.dtype)

    return pl.pallas_call(
        _skip_deps(body, 2, len(deps)), name=name, grid=(m // tm, j * per),
        in_specs=[pl.BlockSpec((k, tm), lambda i, c: (0, i)),
                  pl.BlockSpec((k, tn), lambda i, c: (0, c))] + [ANY_SPEC] * len(deps),
        out_specs=pl.BlockSpec((None, tm, tn), lambda i, c: (c // per, i, c % per)),
        out_shape=_sds((j, m, n), out_dtype), compiler_params=_params(2))(a, b, *deps)


TM = 256


def _pre_norm(name, x, w_row, deps=()):
    def fn(xv, w):
        return _rms(xv, w)
    return _pure_call(name, fn, (T // TM,), [_row_spec(TM, D), _full_spec((1, D))],
                      [_row_spec(TM, D)], [_sds((T, D), BF16)], (x, w_row), deps=deps)[0]


def _post_pre_norm(x, y, w_post, w_pre):
    def fn(xv, yv, wp, wn):
        x1 = xv + _rms(yv, wp)
        return x1, _rms(x1, wn)
    return _pure_call("post_pre_norm", fn, (T // TM,),
                      [_row_spec(TM, D), _row_spec(TM, D), _full_spec((1, D)), _full_spec((1, D))],
                      [_row_spec(TM, D), _row_spec(TM, D)],
                      [_sds((T, D), F32), _sds((T, D), BF16)], (x, y, w_post, w_pre))


def _post_pre_norm_bwd(y, x1, w_post, w_pre, dx1_in, dh1, deps=()):
    def fn(yv, x1v, wp, wn, dx1v, dh1v):
        _, vjp_pre = jax.vjp(_rms, x1v, wn)
        dx1_h, dwn = vjp_pre(dh1v)
        dx1 = dx1v + dx1_h
        _, vjp_post = jax.vjp(_rms, yv, wp)
        dy, dwp = vjp_post(dx1)
        return dx1, dy, dwp, dwn
    return _pure_call("post_pre_norm_bwd", fn, (T // TM,),
                      [_row_spec(TM, D), _row_spec(TM, D), _full_spec((1, D)), _full_spec((1, D)),
                       _row_spec(TM, D), _row_spec(TM, D)],
                      [_row_spec(TM, D), _row_spec(TM, D), _full_spec((1, D)), _full_spec((1, D))],
                      [_sds((T, D), F32), _sds((T, D), BF16), _sds((1, D), F32), _sds((1, D), F32)],
                      (y, x1, w_post, w_pre, dx1_in, dh1), n_acc=2, deps=deps)


def _final_loss(x1, y, w_post, target):
    def fn(x1v, yv, wp, tv):
        r, vjp = jax.vjp(_rms, yv, wp)
        err = x1v + r - tv
        part = 0.5 * jnp.sum(jnp.mean(err * err, axis=-1, keepdims=True), axis=0, keepdims=True)
        dx2 = err * (1.0 / D)
        dy, dwp = vjp(dx2)
        return dx2, dy, jnp.broadcast_to(part, (1, 128)), dwp
    return _pure_call("final_loss", fn, (T // TM,),
                      [_row_spec(TM, D), _row_spec(TM, D), _full_spec((1, D)), _row_spec(TM, D)],
                      [_row_spec(TM, D), _row_spec(TM, D), _full_spec((1, 128)), _full_spec((1, D))],
                      [_sds((T, D), F32), _sds((T, D), BF16), _sds((1, 128), F32), _sds((1, D), F32)],
                      (x1, y, w_post, target), n_acc=2)


def _pre_norm_bwd(x, w_row, dh, dx_res, deps=()):
    def fn(xv, w, dhv, dxv):
        _, vjp = jax.vjp(_rms, xv, w)
        dx, dw = vjp(dhv)
        return dxv + dx, dw
    return _pure_call("pre_norm_bwd", fn, (T // TM,),
                      [_row_spec(TM, D), _full_spec((1, D)), _row_spec(TM, D), _row_spec(TM, D)],
                      [_row_spec(TM, D), _full_spec((1, D))],
                      [_sds((T, D), F32), _sds((1, D), F32)], (x, w_row, dh, dx_res), n_acc=1, deps=deps)


RA = 256
RB = 512


def _col_spec(rows, col_of):
    return pl.BlockSpec((rows, HD), lambda h, r, col_of=col_of: (r, col_of(h)))


def _hgrn2_fwd(z, l0, l1, onorm):
    nb = T // RA

    def body(q_ref, f_ref, v_ref, g_ref, l0_ref, l1_ref, on_ref, cat_ref, sst_ref, st_scr):
        @pl.when(pl.program_id(1) == 0)
        def _():
            st_scr[...] = jnp.zeros_like(st_scr)

        st = st_scr[...]
        sst_ref[...] = st
        out, st_new = _hgrn2_rows(q_ref[...], f_ref[...], v_ref[...], g_ref[...], st,
                                  l0_ref[...], l1_ref[...], on_ref[...])
        cat_ref[...] = out.astype(cat_ref.dtype)
        st_scr[...] = st_new

    vec = pl.BlockSpec((1, HD), lambda h, r: (0, h))
    return pl.pallas_call(
        body, name="hgrn2_fwd", grid=(A_HEADS, nb),
        in_specs=[_col_spec(RA, lambda h: h), _col_spec(RA, lambda h: 8 + h),
                  _col_spec(RA, lambda h: 16 + h), _col_spec(RA, lambda h: 24 + h),
                  vec, vec, _full_spec((1, HD))],
        out_specs=[_col_spec(RA, lambda h: h),
                   pl.BlockSpec((None, None, HD, HD), lambda h, r: (h, r, 0, 0))],
        out_shape=[_sds((T, 2 * A_HEADS * HD), BF16), _sds((A_HEADS, nb, HD, HD), F32)],
        scratch_shapes=[pltpu.VMEM((HD, HD), F32)],
        compiler_params=_params(2))(z, z, z, z, l0, l1, onorm)


def _hgrn2_bwd(z, l0, l1, onorm, sst, dcat, deps=()):
    nb = T // RA
    deps = _live(deps)

    def body(q_ref, f_ref, v_ref, g_ref, l0_ref, l1_ref, on_ref, sst_ref, dcat_ref,
             dq_ref, df_ref, dv_ref, dg_ref, dl0_ref, dl1_ref, don_ref, ds_scr):
        h, r = pl.program_id(0), pl.program_id(1)

        @pl.when(r == 0)
        def _():
            ds_scr[...] = jnp.zeros_like(ds_scr)

        _, vjp = jax.vjp(_hgrn2_rows, q_ref[...], f_ref[...], v_ref[...], g_ref[...], sst_ref[...],
                         l0_ref[...], l1_ref[...], on_ref[...])
        dq, dzf, dv, dga, dst, dl0, dl1, don = vjp((dcat_ref[...], ds_scr[...]))
        dq_ref[...] = dq.astype(dq_ref.dtype)
        df_ref[...] = dzf.astype(df_ref.dtype)
        dv_ref[...] = dv.astype(dv_ref.dtype)
        dg_ref[...] = dga.astype(dg_ref.dtype)
        ds_scr[...] = dst

        @pl.when(r == 0)
        def _():
            dl0_ref[...] = dl0
            dl1_ref[...] = dl1

        @pl.when(r > 0)
        def _():
            dl0_ref[...] += dl0
            dl1_ref[...] += dl1

        first = jnp.logical_and(h == 0, r == 0)

        @pl.when(first)
        def _():
            don_ref[...] = don

        @pl.when(jnp.logical_not(first))
        def _():
            don_ref[...] += don

    def rev(col_of):
        return pl.BlockSpec((RA, HD), lambda h, r, col_of=col_of: (nb - 1 - r, col_of(h)))

    vec = pl.BlockSpec((1, HD), lambda h, r: (0, h))
    grad = _sds((T, A_HEADS * HD), BF16)
    return pl.pallas_call(
        _skip_deps(body, 9, len(deps)), name="hgrn2_bwd", grid=(A_HEADS, nb),
        in_specs=[rev(lambda h: h), rev(lambda h: 8 + h), rev(lambda h: 16 + h), rev(lambda h: 24 + h),
                  vec, vec, _full_spec((1, HD)),
                  pl.BlockSpec((None, None, HD, HD), lambda h, r: (h, nb - 1 - r, 0, 0)),
                  rev(lambda h: h)] + [ANY_SPEC] * len(deps),
        out_specs=[rev(lambda h: h)] * 4 + [vec, vec, _full_spec((1, HD))],
        out_shape=[grad] * 4 + [_sds((1, A_HEADS * HD), F32)] * 2 + [_sds((1, HD), F32)],
        scratch_shapes=[pltpu.VMEM((HD, HD), F32)],
        compiler_params=_params(2))(z, z, z, z, l0, l1, onorm, sst, dcat, *deps)


def _gmlp_specs():
    vec = pl.BlockSpec((1, HD), lambda g, r: (0, g))
    ws = pl.BlockSpec((None, B_CHUNK, B_CHUNK), lambda g, r: (g, 0, 0))
    bias = pl.BlockSpec((None, B_CHUNK, 1), lambda g, r: (g, 0, 0))
    return vec, ws, bias


def _gmlp_fwd(z, cat, lnw, lnb, ws, bias):
    vec, ws_spec, bias_spec = _gmlp_specs()

    def body(u_ref, v_ref, g_ref, lnw_ref, lnb_ref, ws_ref, bias_ref, cat_in_ref, cat_ref):
        del cat_in_ref
        out = _gmlp_rows(u_ref[...], v_ref[...], g_ref[...], lnw_ref[...], lnb_ref[...],
                         ws_ref[...], bias_ref[...])
        cat_ref[...] = out.astype(cat_ref.dtype)

    return pl.pallas_call(
        body, name="gmlp_fwd", grid=(B_GROUPS, T // RB),
        in_specs=[_col_spec(RB, lambda g: 32 + g), _col_spec(RB, lambda g: 40 + g),
                  _col_spec(RB, lambda g: 48 + g), vec, vec, ws_spec, bias_spec,
                  pl.BlockSpec(memory_space=pl.ANY)],
        out_specs=_col_spec(RB, lambda g: A_HEADS + g),
        out_shape=_sds(cat.shape, cat.dtype), input_output_aliases={7: 0},
        compiler_params=_params(2))(z, z, z, lnw, lnb, ws, bias, cat)


def _gmlp_bwd(z, lnw, lnb, ws, bias, dcat):
    vec, ws_spec, bias_spec = _gmlp_specs()

    def fn(u, vb, gb, w, b, wsv, bv, dout):
        _, vjp = jax.vjp(_gmlp_rows, u, vb, gb, w, b, wsv, bv)
        return vjp(dout)

    def body(*refs):
        ins, outs = refs[:8], refs[8:]
        res = fn(*[r[...] for r in ins])
        for o, r in zip(outs[:3], res[:3]):
            o[...] = r.astype(o.dtype)
        first = pl.program_id(1) == 0
        for o, r in zip(outs[3:], res[3:]):
            @pl.when(first)
            def _(o=o, r=r):
                o[...] = r

            @pl.when(jnp.logical_not(first))
            def _(o=o, r=r):
                o[...] += r

    grad = _sds((T, B_GROUPS * HD), BF16)
    row_out = pl.BlockSpec((RB, HD), lambda g, r: (r, g))
    return pl.pallas_call(
        body, name="gmlp_bwd", grid=(B_GROUPS, T // RB),
        in_specs=[_col_spec(RB, lambda g: 32 + g), _col_spec(RB, lambda g: 40 + g),
                  _col_spec(RB, lambda g: 48 + g), vec, vec, ws_spec, bias_spec,
                  _col_spec(RB, lambda g: A_HEADS + g)],
        out_specs=[row_out] * 3 + [vec, vec, ws_spec, bias_spec],
        out_shape=[grad] * 3 + [_sds((1, B_GROUPS * HD), F32)] * 2
        + [_sds((B_GROUPS, B_CHUNK, B_CHUNK), F32), _sds((B_GROUPS, B_CHUNK, 1), F32)],
        compiler_params=_params(2))(z, z, z, lnw, lnb, ws, bias, dcat)


def _mla_pre(z1, qn, kvn, cos_t, sin_t):
    def fn(cq, ckv, kpe, cs, sn, wq, wkv):
        return _rms(cq, wq), _rms(ckv, wkv), _rope(kpe, cs, sn)
    return _pure_call("mla_pre", fn, (T // TM,),
                      [_row_spec(TM, C_RANK, 4), _row_spec(TM, C_RANK, 5), _row_spec(TM, HD, 24),
                       _row_spec(TM, HD), _row_spec(TM, HD),
                       _full_spec((1, C_RANK)), _full_spec((1, C_RANK))],
                      [_row_spec(TM, C_RANK), _row_spec(TM, C_RANK), _row_spec(TM, HD)],
                      [_sds((T, C_RANK), BF16), _sds((T, C_RANK), BF16), _sds((T, HD), BF16)],
                      (z1, z1, z1, cos_t, sin_t, qn, kvn))


def _mla_pre_bwd(z1, qn, kvn, cos_t, sin_t, dcqn, dckvn, dkp, deps=()):
    def fn(cq, ckv, cs, sn, wq, wkv, g_q, g_kv, g_kp):
        _, vjp_q = jax.vjp(_rms, cq, wq)
        dcq, dwq = vjp_q(g_q)
        _, vjp_kv = jax.vjp(_rms, ckv, wkv)
        dckv, dwkv = vjp_kv(g_kv)
        return dcq, dckv, _rope_transpose(g_kp, cs, sn), dwq, dwkv
    return _pure_call("mla_pre_bwd", fn, (T // TM,),
                      [_row_spec(TM, C_RANK, 4), _row_spec(TM, C_RANK, 5),
                       _row_spec(TM, HD), _row_spec(TM, HD),
                       _full_spec((1, C_RANK)), _full_spec((1, C_RANK)),
                       _row_spec(TM, C_RANK), _row_spec(TM, C_RANK), _row_spec(TM, HD)],
                      [_row_spec(TM, C_RANK), _row_spec(TM, C_RANK), _row_spec(TM, HD),
                       _full_spec((1, C_RANK)), _full_spec((1, C_RANK))],
                      [_sds((T, C_RANK), BF16), _sds((T, C_RANK), BF16), _sds((T, HD), BF16),
                       _sds((1, C_RANK), F32), _sds((1, C_RANK), F32)],
                      (z1, z1, cos_t, sin_t, qn, kvn, dcqn, dckvn, dkp), n_acc=2, deps=deps)


def _gate_out(o, z1):
    def fn(ov, gate):
        return ov * _silu(gate)
    return _pure_call("gate_out", fn, (T // TM,), [_row_spec(TM, D), _row_spec(TM, D, 0)],
                      [_row_spec(TM, D)], [_sds((T, D), BF16)], (o, z1))[0]


def _gate_out_bwd(o, z1, dog, deps=()):
    def fn(ov, gate, g):
        _, vjp = jax.vjp(lambda a, b: a * _silu(b), ov, gate)
        return vjp(g)
    return _pure_call("gate_out_bwd", fn, (T // TM,),
                      [_row_spec(TM, D), _row_spec(TM, D, 0), _row_spec(TM, D)],
                      [_row_spec(TM, D), _row_spec(TM, D)],
                      [_sds((T, D), F32), _sds((T, D), BF16)], (o, z1, dog), deps=deps)


TQ = 256


def _att_keys(kn_ref, kp_ref, k_scr):
    @pl.when(pl.program_id(1) == 0)
    def _():
        k_scr[:, 0:C_NOPE] = kn_ref[...]
        k_scr[:, C_NOPE:QP] = kp_ref[...]


def _att_scores(q_ref, cos_ref, sin_ref, k_scr, n):
    keys = (n + 1) * TQ
    q = q_ref[...]
    qr = jnp.concatenate([q[:, :C_NOPE], _rope(q[:, C_NOPE:], cos_ref[...], sin_ref[...])], axis=1).astype(BF16)
    return qr, _raw_nt(qr, k_scr[0:keys, :]) * ATT_SCALE


def _causal(x, n, fill):
    row = lax.broadcasted_iota(jnp.int32, (TQ, TQ), 0)
    col = lax.broadcasted_iota(jnp.int32, (TQ, TQ), 1)
    diag = jnp.where(col <= row, x[:, n * TQ:], fill)
    return diag if n == 0 else jnp.concatenate([x[:, :n * TQ], diag], axis=1)


def _per_query_block(fn):
    for n in range(T // TQ):
        pl.when(pl.program_id(1) == n)(functools.partial(fn, n))


def _att_in_specs():
    return [pl.BlockSpec((TQ, QP), lambda h, i: (i, h)),
            pl.BlockSpec((TQ, HD), lambda h, i: (i, 0)),
            pl.BlockSpec((TQ, HD), lambda h, i: (i, 0)),
            pl.BlockSpec((T, C_NOPE), lambda h, i: (0, 2 * h)),
            pl.BlockSpec((T, HD), lambda h, i: (0, 0)),
            pl.BlockSpec((T, C_V), lambda h, i: (0, 2 * h + 1))]


def _attention_fwd(q, cos_t, sin_t, kv, kp):
    def body(q_ref, cos_ref, sin_ref, kn_ref, kp_ref, v_ref, o_ref, lse_ref, k_scr):
        _att_keys(kn_ref, kp_ref, k_scr)

        def block(n):
            _, s = _att_scores(q_ref, cos_ref, sin_ref, k_scr, n)
            s = _causal(s, n, jnp.finfo(F32).min)
            m = jnp.max(s, axis=-1, keepdims=True)
            p = jnp.exp(s - m)
            l = jnp.sum(p, axis=-1, keepdims=True)
            o_ref[...] = _raw_nn(p, v_ref[0:(n + 1) * TQ, :]) / l
            lse_ref[...] = m + jnp.log(l)

        _per_query_block(block)

    return pl.pallas_call(
        body, name="attention_fwd", grid=(C_HEADS, T // TQ), in_specs=_att_in_specs(),
        out_specs=[pl.BlockSpec((TQ, C_V), lambda h, i: (i, h)),
                   pl.BlockSpec((None, TQ, 1), lambda h, i: (h, i, 0))],
        out_shape=[_sds((T, C_HEADS * C_V), F32), _sds((C_HEADS, T, 1), F32)],
        scratch_shapes=[pltpu.VMEM((T, QP), BF16)],
        compiler_params=_params(2))(q, cos_t, sin_t, kv, kp, kv)


def _attention_bwd(q, cos_t, sin_t, kv, kp, o, lse, do):
    nq = T // TQ

    def body(q_ref, cos_ref, sin_ref, kn_ref, kp_ref, v_ref, o_ref, lse_ref, do_ref,
             dq_ref, dkv_ref, dkp_ref, k_scr, dk_scr, dv_scr):
        h, i = pl.program_id(0), pl.program_id(1)
        _att_keys(kn_ref, kp_ref, k_scr)

        @pl.when(i == 0)
        def _():
            dv_scr[...] = jnp.zeros_like(dv_scr)
            dk_scr[...] = jnp.zeros_like(dk_scr)

        def block(n):
            keys = (n + 1) * TQ
            qr, s = _att_scores(q_ref, cos_ref, sin_ref, k_scr, n)
            p = _causal(jnp.exp(s - lse_ref[...]), n, 0.0)
            dov = do_ref[...]
            delta = jnp.sum(dov * o_ref[...], axis=-1, keepdims=True)
            dp = _raw_nt(dov, v_ref[0:keys, :])
            ds = p * (dp - delta) * ATT_SCALE
            dq = _raw_nn(ds, k_scr[0:keys, :])
            dq_ref[...] = jnp.concatenate(
                [dq[:, :C_NOPE], _rope_transpose(dq[:, C_NOPE:], cos_ref[...], sin_ref[...])], axis=1).astype(dq_ref.dtype)
            dv_scr[0:keys, :] += _raw_tn(p, dov)
            dk_scr[0:keys, :] += _raw_tn(ds, qr)

        _per_query_block(block)

        @pl.when(i == nq - 1)
        def _():
            dkv_ref[...] = jnp.concatenate([dk_scr[:, 0:C_NOPE], dv_scr[...]], axis=1).astype(dkv_ref.dtype)

        @pl.when(jnp.logical_and(i == nq - 1, h == 0))
        def _():
            dkp_ref[...] = dk_scr[:, C_NOPE:QP]

        @pl.when(jnp.logical_and(i == nq - 1, h > 0))
        def _():
            dkp_ref[...] += dk_scr[:, C_NOPE:QP]

    return pl.pallas_call(
        body, name="attention_bwd", grid=(C_HEADS, nq),
        in_specs=_att_in_specs() + [pl.BlockSpec((TQ, C_V), lambda h, i: (i, h)),
                                    pl.BlockSpec((None, TQ, 1), lambda h, i: (h, i, 0)),
                                    pl.BlockSpec((TQ, C_V), lambda h, i: (i, h))],
        out_specs=[pl.BlockSpec((TQ, QP), lambda h, i: (i, h)),
                   pl.BlockSpec((T, C_NOPE + C_V), lambda h, i: (0, h)),
                   _full_spec((T, HD))],
        out_shape=[_sds((T, C_HEADS * QP), BF16), _sds((T, C_HEADS * (C_NOPE + C_V)), BF16),
                   _sds((T, HD), F32)],
        scratch_shapes=[pltpu.VMEM((T, QP), BF16), pltpu.VMEM((T, QP), F32), pltpu.VMEM((T, C_V), F32)],
        compiler_params=_params(2))(q, cos_t, sin_t, kv, kp, kv, o, lse, do)


def _adamw_math(w, g, m, v):
    m = ADAM_B1 * m + (1.0 - ADAM_B1) * g
    v = ADAM_B2 * v + (1.0 - ADAM_B2) * (g * g)
    m_hat = m / (1.0 - ADAM_B1 ** ADAM_STEP)
    v_hat = v / (1.0 - ADAM_B2 ** ADAM_STEP)
    delta = -ADAM_LR * (m_hat / (jnp.sqrt(v_hat) + ADAM_EPS) + ADAM_WD * w)
    return delta, m, v


def _adamw(name, parts, w, m, v, tr):
    rows, cols = w.shape
    n_parts = parts.shape[0]

    def fn(pv, wv, mv, vv):
        g = pv[0].astype(F32)
        for d in range(1, n_parts):
            g = g + pv[d].astype(F32)
        return (g,) + _adamw_math(wv, g, mv, vv)

    blk = pl.BlockSpec((tr, cols), lambda i: (i, 0))
    return _pure_call(name, fn, (rows // tr,),
                      [pl.BlockSpec((n_parts, tr, cols), lambda i: (0, i, 0)), blk, blk, blk],
                      [blk] * 4, [_sds((rows, cols), F32)] * 4, (parts, w, m, v))


SMALL_PARAM_SHAPES = ((2, D), (2, D), (2, A_HEADS * HD), (1, HD), (1, B_GROUPS * HD), (1, B_GROUPS * HD),
                      (B_GROUPS, B_CHUNK, B_CHUNK), (B_GROUPS, B_CHUNK))
SMALL_PIECES = ((0, 0, 0, 0), (0, 1, 1, 0), (1, 0, 1, 1), (1, 1, 1, 2), (2, 0, 2, 0), (2, 1, 2, 1),
                (3, 0, 3, 8), (4, 0, 2, 2), (5, 0, 2, 3))


def _small_rows(dnpre1, dnpost0, dnpost1, dl0, dl1, donorm, dlnw, dlnb, dws, dbias, dqn, dkvn, loss_part):
    return [jnp.concatenate([dnpre1, dnpost0, dnpost1], axis=0),
            jnp.concatenate([dl0, dl1, dlnw, dlnb], axis=0),
            jnp.concatenate([dbias.reshape(B_GROUPS, B_CHUNK), donorm, loss_part], axis=0),
            dws,
            jnp.concatenate([dqn, dkvn], axis=0)]


def _adamw_small(late_all, early_all, wmv):
    n_in = 6 + 3 * len(wmv)

    def body(*refs):
        gathered, params, outs = refs[:6], refs[6:n_in], refs[n_in:]

        def total(ref):
            s = ref[0]
            for d in range(1, N_DEV):
                s = s + ref[d]
            return s

        g_late, g2048, g1024, g128, g_ws, g512 = [total(r) for r in gathered]
        arrays = (g_late, g2048, g1024, g128)

        def update(p, rows, g):
            w_ref, m_ref, v_ref = params[3 * p:3 * p + 3]
            delta, m, v = _adamw_math(w_ref[rows], g, m_ref[rows], v_ref[rows])
            for out, val in zip(outs[4 * p:4 * p + 4], (g, delta, m, v)):
                out[rows] = val

        for p, row, arr, arr_row in SMALL_PIECES:
            update(p, pl.ds(row, 1), arrays[arr][arr_row:arr_row + 1])
        update(6, slice(None), g_ws)
        update(7, slice(None), g128[0:B_GROUPS])
        outs[32][...] = g128[B_GROUPS + 1:B_GROUPS + 2]
        outs[33][...] = g512

    vmem = pl.BlockSpec(memory_space=pltpu.VMEM)
    flat = [a for t in wmv for a in t]
    out_shape = [_sds(s, F32) for s in SMALL_PARAM_SHAPES for _ in range(4)] + [_sds((1, 128), F32), _sds((2, C_RANK), F32)]
    res = pl.pallas_call(body, name="adamw_small", in_specs=[vmem] * n_in, out_specs=[vmem] * len(out_shape),
                         out_shape=out_shape,
                         compiler_params=pltpu.CompilerParams(vmem_limit_bytes=VMEM_LIMIT_V7X))(late_all, *early_all, *flat)
    return [res[4 * p:4 * p + 4] for p in range(8)], res[32], res[33]


def _exchange(name, arrs, gather, deps=()):
    n = len(arrs)
    deps = _live(deps)

    def body(*refs):
        ins, outs = refs[:n], refs[n + len(deps):2 * n + len(deps)]
        send_sems, recv_sems, local_sems = refs[2 * n + len(deps):]
        x, y, c = lax.axis_index("x"), lax.axis_index("y"), lax.axis_index("c")
        me = 4 * x + 2 * y + c

        def peer(k):
            return (x ^ (k >> 2), y ^ ((k >> 1) & 1), c ^ (k & 1))

        def copy(a, k):
            src = ins[a] if gather else ins[a].at[me ^ k]
            return pltpu.make_async_remote_copy(
                src_ref=src, dst_ref=outs[a].at[me], send_sem=send_sems.at[a, k - 1],
                recv_sem=recv_sems.at[a, k - 1], device_id=peer(k), device_id_type=MESH_ID)

        def arrival(a, k):
            src = ins[a] if gather else ins[a].at[me]
            return pltpu.make_async_remote_copy(
                src_ref=src, dst_ref=outs[a].at[me ^ k], send_sem=send_sems.at[a, k - 1],
                recv_sem=recv_sems.at[a, k - 1], device_id=peer(k), device_id_type=MESH_ID)

        own = [pltpu.make_async_copy(ins[a] if gather else ins[a].at[me], outs[a].at[me], local_sems.at[a])
               for a in range(n)]
        for cp in own:
            cp.start()
        for k in range(1, N_DEV):
            for a in range(n):
                copy(a, k).start()
        for k in range(1, N_DEV):
            for a in range(n):
                arrival(a, k).wait_recv()
        for k in range(1, N_DEV):
            for a in range(n):
                copy(a, k).wait_send()
        for cp in own:
            cp.wait()

    any_spec = pl.BlockSpec(memory_space=pl.ANY)
    out_shape = [_sds((N_DEV,) + a.shape if gather else a.shape, a.dtype) for a in arrs]
    return pl.pallas_call(
        body, name=name, in_specs=[any_spec] * (n + len(deps)), out_specs=[any_spec] * n, out_shape=out_shape,
        scratch_shapes=[pltpu.SemaphoreType.DMA((n, N_DEV - 1)), pltpu.SemaphoreType.DMA((n, N_DEV - 1)),
                        pltpu.SemaphoreType.DMA((n,))],
        compiler_params=pltpu.CompilerParams(has_side_effects=True))(*arrs, *deps)


HBM_SPEC = pl.BlockSpec(memory_space=pltpu.HBM)
SEM_SPEC = pl.BlockSpec(memory_space=pltpu.SEMAPHORE)
DATAFLOW = pltpu.SideEffectType.DATAFLOW_SIDE_EFFECTING


def _my_index():
    return 4 * lax.axis_index("x") + 2 * lax.axis_index("y") + lax.axis_index("c")


def _plan_copies(plan, refs, send_sems, recv_sems):
    x, y, c = lax.axis_index("x"), lax.axis_index("y"), lax.axis_index("c")
    return [pltpu.make_async_remote_copy(
        src_ref=src, dst_ref=dst, send_sem=send_sems.at[i], recv_sem=recv_sems.at[i],
        device_id=(x ^ (k >> 2), y ^ ((k >> 1) & 1), c ^ (k & 1)), device_id_type=MESH_ID)
        for i, (src, dst, k) in enumerate(plan(refs, 4 * x + 2 * y + c))]


def _split_call(name, bufs, waits=None, starts=None, deps=(), fills=None):
    n = len(bufs)
    deps = _live(deps)
    n_wait = 2 if waits else 0

    def body(*refs):
        zones = refs[:n]
        local = []
        if fills:
            local = [pltpu.make_async_copy(src, dst, refs[-1].at[i])
                     for i, (src, dst) in enumerate(fills[0](zones, _my_index()))]
            for cp in local:
                cp.start()
        if waits:
            for cp in _plan_copies(waits[2], zones, refs[n], refs[n + 1]):
                cp.wait_send()
                cp.wait_recv()
        if starts:
            first_out = n + n_wait + len(deps)
            for cp in _plan_copies(starts[0], zones, refs[first_out], refs[first_out + 1]):
                cp.start()
            token = refs[first_out + 2 + n]
            token[...] = jnp.zeros_like(token)
        for cp in local:
            cp.wait()

    out_specs, out_shape = [], []
    if starts:
        sems = pltpu.SemaphoreType.DMA((starts[1],))
        out_specs, out_shape = [SEM_SPEC, SEM_SPEC], [sems, sems]
    out_specs += [HBM_SPEC] * n
    out_shape += [pltpu.HBM(b.shape, b.dtype) for b in bufs]
    if starts:
        out_specs.append(pl.BlockSpec(memory_space=pltpu.VMEM))
        out_shape.append(_sds((8, 128), F32))
    first_buf = 2 if starts else 0
    res = pl.pallas_call(
        body, name=name,
        in_specs=[HBM_SPEC] * n + [SEM_SPEC] * n_wait + [ANY_SPEC] * len(deps),
        out_specs=out_specs, out_shape=out_shape,
        input_output_aliases={i: first_buf + i for i in range(n)},
        scratch_shapes=[pltpu.SemaphoreType.DMA((fills[1],))] if fills else [],
        compiler_params=pltpu.CompilerParams(has_side_effects=DATAFLOW),
    )(*[pltpu.with_memory_space_constraint(b, pltpu.HBM) for b in bufs], *(waits[:2] if waits else ()), *deps)
    out_bufs = list(res[first_buf:first_buf + n])
    return out_bufs, ((res[0], res[1]) if starts else None), (res[-1] if starts else None)


def _direct_plan(n, gather):
    def plan(refs, me):
        return [(refs[a] if gather else refs[a].at[me ^ k], refs[n + a].at[me], k)
                for k in range(1, N_DEV) for a in range(n)]
    return plan


def _own_slot(n, gather):
    def fills(refs, me):
        return [(refs[a] if gather else refs[a].at[me], refs[n + a].at[me]) for a in range(n)]
    return fills


def _exchange_start(name, arrs, gather, deps=()):
    n = len(arrs)
    lands = [lax.empty((N_DEV,) + a.shape if gather else a.shape, a.dtype) for a in arrs]
    plan = _direct_plan(n, gather)
    bufs, sems, token = _split_call(name, list(arrs) + lands, starts=(plan, n * (N_DEV - 1)), deps=deps,
                                    fills=(_own_slot(n, gather), n))
    return (n, plan, sems, bufs), token


def _exchange_wait(name, handle, after):
    n, plan, sems, bufs = handle
    bufs, _, _ = _split_call(name, bufs, waits=(sems[0], sems[1], plan), deps=[after])
    return bufs[n:]


ICI_PEERS = (2, 4, 6)
SIBLING = 1


def _gather2_send(name, arrs, deps=()):
    n = len(arrs)
    lands = [lax.empty((N_DEV,) + a.shape, a.dtype) for a in arrs]

    def plan(refs, me_):
        return [(refs[a], refs[n + a].at[me_], k) for k in (SIBLING,) + ICI_PEERS for a in range(n)]

    bufs, sems, token = _split_call(name, list(arrs) + lands, starts=(plan, 4 * n), deps=deps,
                                    fills=(_own_slot(n, True), n))
    return (n, plan, sems, bufs), token


def _gather2_relay(name, handle, after):
    n, plan, sems, bufs = handle

    def relay(refs, me_):
        return [(refs[n + a].at[me_ ^ k], refs[n + a].at[me_ ^ k], SIBLING) for k in ICI_PEERS for a in range(n)]

    bufs, sems2, token = _split_call(name, bufs, waits=(sems[0], sems[1], plan), starts=(relay, 3 * n), deps=[after])
    return (n, relay, sems2, bufs), token


def _split_done(name, handle, after, all_bufs=False):
    n, plan, sems, bufs = handle
    bufs, _, _ = _split_call(name, bufs, waits=(sems[0], sems[1], plan), deps=[after])
    return bufs if all_bufs else bufs[n:]


def _scatter2_pair(name, stacks, deps=()):
    n = len(stacks)
    pairs = [lax.empty((4,) + s.shape[1:], s.dtype) for s in stacks]

    def plan(refs, me):
        return [(refs[a].at[(me ^ SIBLING) ^ (2 * j)], refs[n + a].at[j], SIBLING) for j in range(4) for a in range(n)]

    bufs, sems, token = _split_call(name, list(stacks) + pairs, starts=(plan, 4 * n), deps=deps)
    return (n, plan, sems, bufs), token


def _pair_add(name, stack, pair, me):
    _, rows, cols = stack.shape
    tr = rows // 8

    def body(me_ref, s_ref, p_ref, o_ref):
        del me_ref
        o_ref[...] = (s_ref[...].astype(F32) + p_ref[...].astype(F32)).astype(o_ref.dtype)

    grid_spec = pltpu.PrefetchScalarGridSpec(
        num_scalar_prefetch=1, grid=(4, rows // tr),
        in_specs=[pl.BlockSpec((None, tr, cols), lambda j, i, me_ref: (me_ref[0] ^ (2 * j), i, 0)),
                  pl.BlockSpec((None, tr, cols), lambda j, i, me_ref: (j, i, 0))],
        out_specs=pl.BlockSpec((None, tr, cols), lambda j, i, me_ref: (j, i, 0)))
    return pl.pallas_call(body, name=name, grid_spec=grid_spec, out_shape=_sds((4, rows, cols), stack.dtype),
                          compiler_params=_params(2))(me.reshape(1).astype(jnp.int32), stack, pair)


def _scatter2_send(name, chip_sums, deps=()):
    n = len(chip_sums)
    finals = [lax.empty(c.shape, c.dtype) for c in chip_sums]

    def plan(refs, me):
        del me
        return [(refs[a].at[j], refs[n + a].at[j], 2 * j) for j in range(1, 4) for a in range(n)]

    def fills(refs, me):
        del me
        return [(refs[a].at[0], refs[n + a].at[0]) for a in range(n)]

    bufs, sems, token = _split_call(name, list(chip_sums) + finals, starts=(plan, 3 * n), deps=deps, fills=(fills, n))
    return (n, plan, sems, bufs), token


def _pad_rope(p):
    z = jnp.zeros(p.shape[:-1] + (32,), p.dtype)
    return jnp.concatenate([p[..., :32], z, p[..., 32:], z], axis=-1)


def _unpad_rope(p):
    return jnp.concatenate([p[..., :32], p[..., 64:96]], axis=-1)


def _odd_in_layout(w):
    w = w.transpose(1, 0, 2).reshape(D, ODD_IN)
    cq, ckv, kpe, gate = w[:, :512], w[:, 512:1024], w[:, 1024:1088], w[:, 1088:]
    return jnp.concatenate([gate, cq, ckv, _pad_rope(kpe)], axis=1)


def _odd_in_unlayout(dw):
    gate, cq, ckv, kpe = dw[:, :2048], dw[:, 2048:2560], dw[:, 2560:3072], _unpad_rope(dw[:, 3072:])
    w = jnp.concatenate([cq, ckv, kpe, gate], axis=1)
    return w.reshape(D, N_DEV, ODD_IN // N_DEV).transpose(1, 0, 2)


def _qb_layout(w):
    w = w.transpose(1, 0, 2).reshape(C_RANK, C_HEADS, C_QK)
    w = jnp.concatenate([w[..., :C_NOPE], _pad_rope(w[..., C_NOPE:])], axis=-1)
    return w.reshape(C_RANK, C_HEADS * QP)


def _qb_unlayout(dw):
    dw = dw.reshape(C_RANK, C_HEADS, QP)
    dw = jnp.concatenate([dw[..., :C_NOPE], _unpad_rope(dw[..., C_NOPE:])], axis=-1)
    return dw.reshape(C_RANK, N_DEV, C_HEADS * C_QK // N_DEV).transpose(1, 0, 2)


def _rope_tables(positions):
    inv_freq = ROPE_THETA ** (-jnp.arange(0, C_ROPE, 2, dtype=F32) / C_ROPE)
    ang = positions.astype(F32)[0][:, None] * inv_freq
    cos, sin = jnp.cos(ang), jnp.sin(ang)
    z = jnp.zeros_like(cos)
    return jnp.concatenate([cos, z, cos, z], axis=1), jnp.concatenate([-sin, z, sin, z], axis=1)


def _forward_backward(x, cos_t, sin_t, target, norm_pre, norm_post, lb_logits, a_onorm, ln_w, ln_b,
                      b_ws, b_bias, get_w, put_g, put_small=None, start_dep=None):
    npre0, npre1 = norm_pre[0:1], norm_pre[1:2]
    npost0, npost1 = norm_post[0:1], norm_post[1:2]
    l0, l1 = lb_logits[0:1], lb_logits[1:2]
    bias_col = b_bias.reshape(B_GROUPS, B_CHUNK, 1)
    ws = b_ws.reshape(B_GROUPS, B_CHUNK, B_CHUNK)

    h0 = _pre_norm("pre_norm0", x, npre0, deps=[start_dep])
    w_ev_in = get_w("ev_in", h0)
    z0 = _mm_nn("ev_in", h0, w_ev_in, F32, 1024, 896)
    cat, sst = _hgrn2_fwd(z0, l0, l1, a_onorm)
    cat = _gmlp_fwd(z0, cat, ln_w, ln_b, ws, bias_col)
    w_ev_out = get_w("ev_out", cat)
    y0 = _mm_nn("ev_out", cat, w_ev_out, F32, 1024, 1024)
    x1, h1 = _post_pre_norm(x, y0, npost0, npre1)
    w_od_in, w_qb, w_kvb, q_norm, kv_norm = get_w("od_mid", h1)
    z1 = _mm_nn("od_in", h1, w_od_in[None], F32, 1024, 640)
    cqn, ckvn, kp = _mla_pre(z1, q_norm, kv_norm, cos_t, sin_t)
    q = _mm_nn("od_qb", cqn, w_qb[None], F32, 1024, 1024)
    kv = _mm_nn("od_kvb", ckvn, w_kvb, BF16, 1024, 512)
    o, lse = _attention_fwd(q, cos_t, sin_t, kv, kp)
    og = _gate_out(o, z1)
    w_od_out = get_w("od_out", og)
    y1 = _mm_nn("od_out", og, w_od_out, F32, 1024, 1024)
    dx2, dy1, loss_part, dnpost1 = _final_loss(x1, y1, npost1, target)

    g_od_out = _mm_tn("od_out_dw", og, dy1, 1, BF16, 1024, 1024)
    tok = put_g("od_out", [g_od_out.reshape(N_DEV, D // N_DEV, D)])
    dog = _mm_nt("od_out_dx", dy1, w_od_out, F32, 1024, 1024, deps=[tok])
    do, dgate = _gate_out_bwd(o, z1, dog)
    dq, dkv, dkp = _attention_bwd(q, cos_t, sin_t, kv, kp, o, lse, do)
    g_qb = _mm_tn("od_qb_dw", cqn, dq, 1, F32, 512, 1024)
    g_kvb = _mm_tn("od_kvb_dw", ckvn, dkv, N_DEV, BF16, 512, 512)
    tok = put_g("od_qkv", [_qb_unlayout(g_qb[0]).astype(BF16), g_kvb])
    dcqn = _mm_nt("od_qb_dx", dq, w_qb[None], F32, 1024, 512, deps=[tok])
    dckvn = _mm_nt("od_kvb_dx", dkv, w_kvb, F32, 1024, 512)
    dcq, dckv, dkpe, dqn, dkvn = _mla_pre_bwd(z1, q_norm, kv_norm, cos_t, sin_t, dcqn, dckvn, dkp)
    dz1 = jnp.concatenate([dgate, dcq, dckv, dkpe], axis=1)
    g_od_in = _mm_tn("od_in_dw", h1, dz1, 1, F32, 1024, 640)
    tok = put_g("od_in", [_odd_in_unlayout(g_od_in[0]).astype(BF16)])
    dh1 = _mm_nt("od_in_dx", dz1, w_od_in[None], F32, 1024, 1024, deps=[tok])
    dx1, dy0, dnpost0, dnpre1 = _post_pre_norm_bwd(y0, x1, npost0, npre1, dx2, dh1)

    g_ev_out = _mm_tn("ev_out_dw", cat, dy0, 1, BF16, 1024, 1024)
    tok = put_g("ev_out", [g_ev_out.reshape(N_DEV, D // N_DEV, D)])
    dcat = _mm_nt("ev_out_dx", dy0, w_ev_out, F32, 1024, 1024, deps=[tok])
    dqa, dfa, dia, dga, dl0, dl1, donorm = _hgrn2_bwd(z0, l0, l1, a_onorm, sst, dcat)
    dub, dvb, dgb, dlnw, dlnb, dws, dbias = _gmlp_bwd(z0, ln_w, ln_b, ws, bias_col, dcat)
    dz0 = jnp.concatenate([dqa, dfa, dia, dga, dub, dvb, dgb], axis=1)
    early = _small_rows(dnpre1, dnpost0, dnpost1, dl0, dl1, donorm, dlnw, dlnb, dws, dbias, dqn, dkvn, loss_part)
    tok = put_small(early) if put_small else None
    g_ev_in = _mm_tn("ev_in_dw", h0, dz0, N_DEV, BF16, 1024, 896, deps=[tok])
    tok = put_g("ev_in", [g_ev_in])
    dh0 = _mm_nt("ev_in_dx", dz0, w_ev_in, F32, 1024, 1024, deps=[tok])
    grad_x, dnpre0 = _pre_norm_bwd(x, npre0, dh0, dx1)
    return grad_x, early, dnpre0


def kernel(x, positions, norm_pre, norm_post, ev_w_in, ev_lb_logits, ev_a_onorm, ev_b_ln_w, ev_b_ln_b, ev_b_ws, ev_b_bias, ev_w_out, od_w_in, od_q_norm, od_w_qb, od_kv_norm, od_w_kvb, od_w_out, loss_target, m_norm_pre, m_norm_post, m_ev_w_in, m_ev_lb_logits, m_ev_a_onorm, m_ev_b_ln_w, m_ev_b_ln_b, m_ev_b_ws, m_ev_b_bias, m_ev_w_out, m_od_w_in, m_od_q_norm, m_od_w_qb, m_od_kv_norm, m_od_w_kvb, m_od_w_out, v_norm_pre, v_norm_post, v_ev_w_in, v_ev_lb_logits, v_ev_a_onorm, v_ev_b_ln_w, v_ev_b_ln_b, v_ev_b_ws, v_ev_b_bias, v_ev_w_out, v_od_w_in, v_od_q_norm, v_od_w_qb, v_od_kv_norm, v_od_w_kvb, v_od_w_out):
    me = 4 * lax.axis_index("x") + 2 * lax.axis_index("y") + lax.axis_index("c")
    bf = lambda w: w[0].astype(BF16)

    norms = jnp.pad(jnp.concatenate([od_q_norm, od_kv_norm], axis=1), ((0, 7), (0, 0)))
    first_h, tok = _gather2_send("gather_ev_in", [bf(ev_w_in)])
    rest_h, tok = _gather2_send("gather_rest", [bf(ev_w_out), bf(od_w_in), bf(od_w_qb), bf(od_w_kvb), norms,
                                                bf(od_w_out)], deps=[tok])
    first_h, tok = _gather2_relay("relay_ev_in", first_h, tok)
    rest = []

    def get_w(group, after):
        if group == "ev_in":
            return _split_done("arrived_ev_in", first_h, after)[0]
        if not rest:
            relayed, token = _gather2_relay("relay_rest", rest_h, after)
            rest.extend(_split_done("arrived_rest", relayed, token))
        w_ev_out, w_od_in, w_qb, w_kvb, norms_all, w_od_out = rest
        if group == "ev_out":
            return w_ev_out.reshape(1, D, D)
        if group == "od_out":
            return w_od_out.reshape(1, D, D)
        return (_odd_in_layout(w_od_in), _qb_layout(w_qb), w_kvb,
                norms_all[:, 0, :64].reshape(1, C_RANK), norms_all[:, 0, 64:].reshape(1, C_RANK))

    scatters = {}

    def put_g(group, grads):
        if group == "ev_in":
            paired, token = _scatter2_pair("pair_ev_in", grads)
            n = len(grads)
            bufs = _split_done("paired_ev_in", paired, token, all_bufs=True)
            chip_sums = [_pair_add("pair_add_ev_in", bufs[a], bufs[n + a], me) for a in range(n)]
            scatters[group], token = _scatter2_send("scatter_ev_in", chip_sums)
        else:
            scatters[group], token = _exchange_start("scatter_" + group, grads, False)
        return token

    def put_small(early):
        scatters["small"], token = _exchange_start("gather_small_early", early, True)
        return token

    cos_t, sin_t = _rope_tables(positions)
    grad_x, _, dnpre0 = _forward_backward(
        x[0], cos_t, sin_t, loss_target[0], norm_pre, norm_post, ev_lb_logits, ev_a_onorm, ev_b_ln_w,
        ev_b_ln_b, ev_b_ws, ev_b_bias, get_w, put_g, put_small, start_dep=tok)

    big_w = {"ev_w_in": ev_w_in, "ev_w_out": ev_w_out, "od_w_in": od_w_in, "od_w_qb": od_w_qb,
             "od_w_kvb": od_w_kvb, "od_w_out": od_w_out}
    big_m = {"ev_w_in": m_ev_w_in, "ev_w_out": m_ev_w_out, "od_w_in": m_od_w_in, "od_w_qb": m_od_w_qb,
             "od_w_kvb": m_od_w_kvb, "od_w_out": m_od_w_out}
    big_v = {"ev_w_in": v_ev_w_in, "ev_w_out": v_ev_w_out, "od_w_in": v_od_w_in, "od_w_qb": v_od_w_qb,
             "od_w_kvb": v_od_w_kvb, "od_w_out": v_od_w_out}
    big_out = {}
    after = grad_x
    for group, names in (("od_out", ["od_w_out"]), ("od_qkv", ["od_w_qb", "od_w_kvb"]), ("od_in", ["od_w_in"]),
                         ("ev_out", ["ev_w_out"])):
        parts = _exchange_wait("summed_" + group, scatters[group], after)
        for nm, p in zip(names, parts):
            w = big_w[nm][0]
            big_out[nm] = [r[None] for r in _adamw("adamw_" + nm, p, w, big_m[nm][0], big_v[nm][0], w.shape[0] // 8)]
            after = big_out[nm][0]

    late_all = _exchange("gather_small_late", [dnpre0], gather=True, deps=[after])[0]
    early_all = _exchange_wait("arrived_small_early", scatters["small"], late_all)

    small_w = (norm_pre, norm_post, ev_lb_logits, ev_a_onorm, ev_b_ln_w, ev_b_ln_b, ev_b_ws, ev_b_bias)
    small_m = (m_norm_pre, m_norm_post, m_ev_lb_logits, m_ev_a_onorm, m_ev_b_ln_w, m_ev_b_ln_b, m_ev_b_ws, m_ev_b_bias)
    small_v = (v_norm_pre, v_norm_post, v_ev_lb_logits, v_ev_a_onorm, v_ev_b_ln_w, v_ev_b_ln_b, v_ev_b_ws, v_ev_b_bias)
    wmv = [tuple(a.reshape(s) for a in t) for s, t in zip(SMALL_PARAM_SHAPES, zip(small_w, small_m, small_v))]
    small_res, loss_row, g_norm_rows = _adamw_small(late_all, early_all, wmv)
    small_out = [[r.reshape(w.shape) for r in four] for four, w in zip(small_res, small_w)]
    loss = loss_row[0, 0]

    g_norms = jnp.concatenate([lax.dynamic_slice(g_norm_rows, (0, 64 * me), (1, 64)),
                               lax.dynamic_slice(g_norm_rows, (1, 64 * me), (1, 64))], axis=1)
    res_n = _adamw("adamw_norms", g_norms[None],
                   jnp.concatenate([od_q_norm, od_kv_norm], axis=1),
                   jnp.concatenate([m_od_q_norm, m_od_kv_norm], axis=1),
                   jnp.concatenate([v_od_q_norm, v_od_kv_norm], axis=1), 1)
    qn_out = [r[:, :64] for r in res_n]
    kvn_out = [r[:, 64:] for r in res_n]

    parts = _split_done("summed_ev_in", scatters["ev_in"], loss_row)
    w = ev_w_in[0]
    big_out["ev_w_in"] = [r[None] for r in _adamw("adamw_ev_w_in", parts[0], w, m_ev_w_in[0], v_ev_w_in[0],
                                                  w.shape[0] // 8)]

    order = ("norm_pre", "norm_post", "ev_w_in", "ev_lb_logits", "ev_a_onorm", "ev_b_ln_w", "ev_b_ln_b",
             "ev_b_ws", "ev_b_bias", "ev_w_out", "od_w_in", "od_q_norm", "od_w_qb", "od_kv_norm",
             "od_w_kvb", "od_w_out")
    small_names = ("norm_pre", "norm_post", "ev_lb_logits", "ev_a_onorm", "ev_b_ln_w", "ev_b_ln_b",
                   "ev_b_ws", "ev_b_bias")
    outs = [loss, grad_x[None]]
    for kind in range(4):
        for nm in order:
            if nm in big_out:
                outs.append(big_out[nm][kind])
            elif nm == "od_q_norm":
                outs.append(qn_out[kind])
            elif nm == "od_kv_norm":
                outs.append(kvn_out[kind])
            else:
                outs.append(small_out[small_names.index(nm)][kind])
    return tuple(outs)
```

```python
import functools

import jax
import jax.numpy as jnp
from jax import lax
from jax.experimental import pallas as pl
from jax.experimental.pallas import tpu as pltpu

F32 = jnp.float32
BF16 = jnp.bfloat16

N_DEV = 8
T = 2048
D = 2048
EPS = 1e-6
A_HEADS = 8
HD = 128
A_CHUNK = 64
A_SUB = 16
B_GROUPS = 8
B_CHUNK = 128
EVEN_IN = 7168
C_HEADS = 16
C_RANK = 512
C_NOPE = 128
C_ROPE = 64
C_QK = C_NOPE + C_ROPE
C_V = 128
ODD_IN = 3136
ODD_IN_PAD = 3200
QP = 256
ROPE_THETA = 10000.0
ATT_SCALE = C_QK ** -0.5

ADAM_LR = 0.001
ADAM_B1 = 0.9
ADAM_B2 = 0.999
ADAM_EPS = 1e-08
ADAM_WD = 0.01
ADAM_STEP = 10

VMEM_LIMIT_V7X = 56 * 1024 * 1024
MESH_ID = pl.DeviceIdType.MESH


def _params(n_grid):
    return pltpu.CompilerParams(dimension_semantics=("arbitrary",) * n_grid,
                                vmem_limit_bytes=VMEM_LIMIT_V7X)


def _dg(a, b, ca, cb):
    return lax.dot_general(a.astype(BF16), b.astype(BF16), (((ca,), (cb,)), ((), ())),
                           preferred_element_type=F32)


def _raw_nn(a, b):
    return _dg(a, b, 1, 0)


def _raw_nt(a, b):
    return _dg(a, b, 1, 1)


def _raw_tn(a, b):
    return _dg(a, b, 0, 0)


@jax.custom_vjp
def _dot_nn(a, b):
    return _raw_nn(a, b)


def _dot_nn_fwd(a, b):
    return _raw_nn(a, b), (a.astype(BF16), b.astype(BF16))


def _dot_nn_bwd(res, g):
    a, b = res
    return _raw_nt(g, b), _raw_tn(a, g)


_dot_nn.defvjp(_dot_nn_fwd, _dot_nn_bwd)


@jax.custom_vjp
def _dot_nt(a, b):
    return _raw_nt(a, b)


def _dot_nt_fwd(a, b):
    return _raw_nt(a, b), (a.astype(BF16), b.astype(BF16))


def _dot_nt_bwd(res, g):
    a, b = res
    return _raw_nn(g, b), _raw_tn(g, a)


_dot_nt.defvjp(_dot_nt_fwd, _dot_nt_bwd)


@jax.custom_vjp
def _dot_tn(a, b):
    return _raw_tn(a, b)


def _dot_tn_fwd(a, b):
    return _raw_tn(a, b), (a.astype(BF16), b.astype(BF16))


def _dot_tn_bwd(res, g):
    a, b = res
    return _raw_nt(b, g), _raw_nn(a, g)


_dot_tn.defvjp(_dot_tn_fwd, _dot_tn_bwd)


@jax.custom_vjp
def _sigmoid(x):
    e = jnp.exp(-jnp.abs(x))
    return jnp.where(x >= 0, 1.0 / (1.0 + e), e / (1.0 + e))


def _sigmoid_fwd(x):
    s = _sigmoid(x)
    return s, s


def _sigmoid_bwd(s, g):
    return (g * s * (1.0 - s),)


_sigmoid.defvjp(_sigmoid_fwd, _sigmoid_bwd)


def _silu(x):
    return x * _sigmoid(x)


def _rms(x, w):
    return x * lax.rsqrt(jnp.mean(x * x, axis=-1, keepdims=True) + EPS) * w


def _split3(x):
    hi = x.astype(BF16)
    r = x - hi.astype(F32)
    mid = r.astype(BF16)
    lo = (r - mid.astype(F32)).astype(BF16)
    return hi, mid, lo


def _mask_apply(mask_bf16, x, contract):
    out = None
    for piece in _split3(x):
        d = lax.dot_general(mask_bf16, piece, (((contract,), (0,)), ((), ())),
                            preferred_element_type=F32)
        out = d if out is None else out + d
    return out


def _chunk_tri(rows):
    r = lax.broadcasted_iota(jnp.int32, (rows, rows), 0)
    c = lax.broadcasted_iota(jnp.int32, (rows, rows), 1)
    return ((r >= c) & (r // A_CHUNK == c // A_CHUNK)).astype(BF16)


@jax.custom_vjp
def _chunk_cumsum(x):
    return _mask_apply(_chunk_tri(x.shape[0]), x, 1)


def _chunk_cumsum_fwd(x):
    return _chunk_cumsum(x), None


def _chunk_cumsum_bwd(_, g):
    return (_mask_apply(_chunk_tri(g.shape[0]), g, 0),)


_chunk_cumsum.defvjp(_chunk_cumsum_fwd, _chunk_cumsum_bwd)


def _hgrn2_rows(q, zf, v, ga, st, l0, l1, onorm):
    rows = q.shape[0]
    n_sub = A_CHUNK // A_SUB
    mx = jnp.maximum(l0, l1)
    e0 = jnp.exp(l0 - mx)
    e1 = jnp.exp(l1 - mx)
    lb = e0 / (e0 + e1)
    lf = jnp.log(lb + (1.0 - lb) * _sigmoid(zf))
    k = (1.0 - lb) * _sigmoid(-zf)
    b = _chunk_cumsum(lf)

    t_idx = lax.broadcasted_iota(jnp.int32, (A_CHUNK, n_sub * A_CHUNK), 0)
    c_idx = lax.broadcasted_iota(jnp.int32, (A_CHUNK, n_sub * A_CHUNK), 1)
    sel = (c_idx // A_CHUNK == t_idx // A_SUB) & (c_idx % A_CHUNK <= t_idx)
    key_row = lax.broadcasted_iota(jnp.int32, (A_CHUNK, HD), 0)

    outs = []
    for n in range(rows // A_CHUNK):
        lo = n * A_CHUNK
        qc, kc, vc = q[lo:lo + A_CHUNK], k[lo:lo + A_CHUNK], v[lo:lo + A_CHUNK]
        lfc, bc = lf[lo:lo + A_CHUNK], b[lo:lo + A_CHUNK]
        b_last = bc[A_CHUNK - 1:A_CHUNK]
        o_inter = _dot_nt(qc * jnp.exp(bc), st)
        kv_t = _dot_tn(vc, kc * jnp.exp(b_last - bc))
        st = st * jnp.exp(b_last) + kv_t
        g_rows, k_subs = [], []
        for i in range(n_sub):
            g_i = bc[i * A_SUB:i * A_SUB + 1] - lfc[i * A_SUB:i * A_SUB + 1]
            g_rows.append(jnp.broadcast_to(g_i, (A_SUB, HD)))
            expo = jnp.where(key_row < (i + 1) * A_SUB, g_i - bc, -jnp.inf)
            k_subs.append(kc * jnp.exp(expo))
        q_sub = qc * jnp.exp(bc - jnp.concatenate(g_rows, axis=0))
        scores = _dot_nt(q_sub, jnp.concatenate(k_subs, axis=0))
        scores = jnp.where(sel, scores, 0.0)
        o_intra = _dot_nn(scores, jnp.concatenate([vc] * n_sub, axis=0))
        outs.append(o_inter + o_intra)
    o = jnp.concatenate(outs, axis=0)
    return _rms(o, onorm) * _silu(ga), st


def _gmlp_rows(u, vb, gb, lnw, lnb, ws, bias):
    rows = u.shape[0]
    mu = jnp.mean(vb, axis=-1, keepdims=True)
    xc = vb - mu
    vg = xc * lax.rsqrt(jnp.mean(xc * xc, axis=-1, keepdims=True) + EPS) * lnw + lnb
    r = lax.broadcasted_iota(jnp.int32, (B_CHUNK, B_CHUNK), 0)
    c = lax.broadcasted_iota(jnp.int32, (B_CHUNK, B_CHUNK), 1)
    ws_causal = jnp.where(r >= c, ws, 0.0)
    svs = [_dot_nn(ws_causal, vg[n * B_CHUNK:(n + 1) * B_CHUNK]) + bias
           for n in range(rows // B_CHUNK)]
    return u * jnp.concatenate(svs, axis=0) * _silu(gb)


def _rope(x, cos_t, sin_t):
    return x * cos_t + pltpu.roll(x, 64, 1) * sin_t


def _rope_transpose(g, cos_t, sin_t):
    return g * cos_t + pltpu.roll(g * sin_t, 64, 1)


ANY_SPEC = pl.BlockSpec(memory_space=pl.ANY)


def _live(deps):
    return [d for d in deps if d is not None]


def _skip_deps(body, n_in, n_deps):
    def wrapped(*refs):
        return body(*refs[:n_in], *refs[n_in + n_deps:])
    return wrapped


def _pure_call(name, fn, grid, in_specs, out_specs, out_shape, args, n_acc=0, deps=()):
    deps = _live(deps)
    n_in, n_out, n_deps = len(in_specs), len(out_specs), len(deps)
    in_specs = list(in_specs) + [ANY_SPEC] * n_deps
    args = tuple(args) + tuple(deps)

    def body(*refs):
        res = fn(*[r[...] for r in refs[:n_in]])
        if not isinstance(res, (tuple, list)):
            res = (res,)
        outs = refs[n_in + n_deps:n_in + n_deps + n_out]
        for o, r in zip(outs[:n_out - n_acc], res[:n_out - n_acc]):
            o[...] = r.astype(o.dtype)
        if n_acc:
            first = functools.reduce(jnp.logical_and, [pl.program_id(i) == 0 for i in range(len(grid))])
            for o, r in zip(outs[n_out - n_acc:], res[n_out - n_acc:]):
                @pl.when(first)
                def _(o=o, r=r):
                    o[...] = r.astype(o.dtype)

                @pl.when(jnp.logical_not(first))
                def _(o=o, r=r):
                    o[...] += r.astype(o.dtype)

    return pl.pallas_call(body, name=name, grid=grid, in_specs=in_specs, out_specs=out_specs,
                          out_shape=out_shape, compiler_params=_params(len(grid)))(*args)


def _sds(shape, dtype):
    return jax.ShapeDtypeStruct(shape, dtype)


def _row_spec(tm, width, col=0):
    return pl.BlockSpec((tm, width), lambda i, col=col: (i, col))


def _full_spec(shape):
    nd = len(shape)
    return pl.BlockSpec(shape, lambda *_: (0,) * nd)


def _mm_nn(name, a, b, out_dtype, tm, tn, deps=()):
    deps = _live(deps)
    m, k = a.shape
    j, _, n = b.shape
    per = n // tn

    def body(a_ref, b_ref, o_ref):
        o_ref[...] = _raw_nn(a_ref[...], b_ref[...]).astype(o_ref.dtype)

    return pl.pallas_call(
        _skip_deps(body, 2, len(deps)), name=name, grid=(m // tm, j * per),
        in_specs=[pl.BlockSpec((tm, k), lambda i, c: (i, 0)),
                  pl.BlockSpec((None, k, tn), lambda i, c: (c // per, 0, c % per))] + [ANY_SPEC] * len(deps),
        out_specs=pl.BlockSpec((tm, tn), lambda i, c: (i, c)),
        out_shape=_sds((m, j * n), out_dtype), compiler_params=_params(2))(a, b, *deps)


def _mm_nt(name, a, b, out_dtype, tm, tn, deps=()):
    deps = _live(deps)
    m = a.shape[0]
    j, nn, n = b.shape

    def body(a_ref, b_ref, o_ref, acc_ref):
        part = _raw_nt(a_ref[...], b_ref[...])
        if j == 1:
            o_ref[...] = part.astype(o_ref.dtype)
        else:
            kk = pl.program_id(2)

            @pl.when(kk == 0)
            def _():
                acc_ref[...] = part

            @pl.when(kk > 0)
            def _():
                acc_ref[...] += part

            @pl.when(kk == j - 1)
            def _():
                o_ref[...] = acc_ref[...].astype(o_ref.dtype)

    acc_shape = (tm, tn) if j > 1 else (8, 128)
    return pl.pallas_call(
        _skip_deps(body, 2, len(deps)), name=name, grid=(m // tm, nn // tn, j),
        in_specs=[pl.BlockSpec((tm, n), lambda i, c, kk: (i, kk)),
                  pl.BlockSpec((None, tn, n), lambda i, c, kk: (kk, c, 0))] + [ANY_SPEC] * len(deps),
        out_specs=pl.BlockSpec((tm, tn), lambda i, c, kk: (i, c)),
        out_shape=_sds((m, nn), out_dtype),
        scratch_shapes=[pltpu.VMEM(acc_shape, F32)], compiler_params=_params(3))(a, b, *deps)


def _mm_tn(name, a, b, j, out_dtype, tm, tn, deps=()):
    deps = _live(deps)
    k, m = a.shape
    n = b.shape[1] // j
    per = n // tn

    def body(a_ref, b_ref, o_ref):
        o_ref[...] = _raw_tn(a_ref[...], b_ref[...]).astype(o_ref.dtype)

    return pl.pallas_call(
        _skip_deps(body, 2, len(deps)), name=name, grid=(m // tm, j * per),
        in_specs=[pl.BlockSpec((k, tm), lambda i, c: (0, i)),
                  pl.BlockSpec((k, tn), lambda i, c: (0, c))] + [ANY_SPEC] * len(deps),
        out_specs=pl.BlockSpec((None, tm, tn), lambda i, c: (c // per, i, c % per)),
        out_shape=_sds((j, m, n), out_dtype), compiler_params=_params(2))(a, b, *deps)


TM = 256


def _pre_norm(name, x, w_row, deps=()):
    def fn(xv, w):
        return _rms(xv, w)
    return _pure_call(name, fn, (T // TM,), [_row_spec(TM, D), _full_spec((1, D))],
                      [_row_spec(TM, D)], [_sds((T, D), BF16)], (x, w_row), deps=deps)[0]


def _post_pre_norm(x, y, w_post, w_pre):
    def fn(xv, yv, wp, wn):
        x1 = xv + _rms(yv, wp)
        return x1, _rms(x1, wn)
    return _pure_call("post_pre_norm", fn, (T // TM,),
                      [_row_spec(TM, D), _row_spec(TM, D), _full_spec((1, D)), _full_spec((1, D))],
                      [_row_spec(TM, D), _row_spec(TM, D)],
                      [_sds((T, D), F32), _sds((T, D), BF16)], (x, y, w_post, w_pre))


def _post_pre_norm_bwd(y, x1, w_post, w_pre, dx1_in, dh1, deps=()):
    def fn(yv, x1v, wp, wn, dx1v, dh1v):
        _, vjp_pre = jax.vjp(_rms, x1v, wn)
        dx1_h, dwn = vjp_pre(dh1v)
        dx1 = dx1v + dx1_h
        _, vjp_post = jax.vjp(_rms, yv, wp)
        dy, dwp = vjp_post(dx1)
        return dx1, dy, dwp, dwn
    return _pure_call("post_pre_norm_bwd", fn, (T // TM,),
                      [_row_spec(TM, D), _row_spec(TM, D), _full_spec((1, D)), _full_spec((1, D)),
                       _row_spec(TM, D), _row_spec(TM, D)],
                      [_row_spec(TM, D), _row_spec(TM, D), _full_spec((1, D)), _full_spec((1, D))],
                      [_sds((T, D), F32), _sds((T, D), BF16), _sds((1, D), F32), _sds((1, D), F32)],
                      (y, x1, w_post, w_pre, dx1_in, dh1), n_acc=2, deps=deps)


def _final_loss(x1, y, w_post, target):
    def fn(x1v, yv, wp, tv):
        r, vjp = jax.vjp(_rms, yv, wp)
        err = x1v + r - tv
        part = 0.5 * jnp.sum(jnp.mean(err * err, axis=-1, keepdims=True), axis=0, keepdims=True)
        dx2 = err * (1.0 / D)
        dy, dwp = vjp(dx2)
        return dx2, dy, jnp.broadcast_to(part, (1, 128)), dwp
    return _pure_call("final_loss", fn, (T // TM,),
                      [_row_spec(TM, D), _row_spec(TM, D), _full_spec((1, D)), _row_spec(TM, D)],
                      [_row_spec(TM, D), _row_spec(TM, D), _full_spec((1, 128)), _full_spec((1, D))],
                      [_sds((T, D), F32), _sds((T, D), BF16), _sds((1, 128), F32), _sds((1, D), F32)],
                      (x1, y, w_post, target), n_acc=2)


def _pre_norm_bwd(x, w_row, dh, dx_res, deps=()):
    def fn(xv, w, dhv, dxv):
        _, vjp = jax.vjp(_rms, xv, w)
        dx, dw = vjp(dhv)
        return dxv + dx, dw
    return _pure_call("pre_norm_bwd", fn, (T // TM,),
                      [_row_spec(TM, D), _full_spec((1, D)), _row_spec(TM, D), _row_spec(TM, D)],
                      [_row_spec(TM, D), _full_spec((1, D))],
                      [_sds((T, D), F32), _sds((1, D), F32)], (x, w_row, dh, dx_res), n_acc=1, deps=deps)


RA = 256
RB = 512


def _col_spec(rows, col_of):
    return pl.BlockSpec((rows, HD), lambda h, r, col_of=col_of: (r, col_of(h)))


def _hgrn2_fwd(z, l0, l1, onorm):
    nb = T // RA

    def body(q_ref, f_ref, v_ref, g_ref, l0_ref, l1_ref, on_ref, cat_ref, sst_ref, st_scr):
        @pl.when(pl.program_id(1) == 0)
        def _():
            st_scr[...] = jnp.zeros_like(st_scr)

        st = st_scr[...]
        sst_ref[...] = st
        out, st_new = _hgrn2_rows(q_ref[...], f_ref[...], v_ref[...], g_ref[...], st,
                                  l0_ref[...], l1_ref[...], on_ref[...])
        cat_ref[...] = out.astype(cat_ref.dtype)
        st_scr[...] = st_new

    vec = pl.BlockSpec((1, HD), lambda h, r: (0, h))
    return pl.pallas_call(
        body, name="hgrn2_fwd", grid=(A_HEADS, nb),
        in_specs=[_col_spec(RA, lambda h: h), _col_spec(RA, lambda h: 8 + h),
                  _col_spec(RA, lambda h: 16 + h), _col_spec(RA, lambda h: 24 + h),
                  vec, vec, _full_spec((1, HD))],
        out_specs=[_col_spec(RA, lambda h: h),
                   pl.BlockSpec((None, None, HD, HD), lambda h, r: (h, r, 0, 0))],
        out_shape=[_sds((T, 2 * A_HEADS * HD), BF16), _sds((A_HEADS, nb, HD, HD), F32)],
        scratch_shapes=[pltpu.VMEM((HD, HD), F32)],
        compiler_params=_params(2))(z, z, z, z, l0, l1, onorm)


def _hgrn2_bwd(z, l0, l1, onorm, sst, dcat, deps=()):
    nb = T // RA
    deps = _live(deps)

    def body(q_ref, f_ref, v_ref, g_ref, l0_ref, l1_ref, on_ref, sst_ref, dcat_ref,
             dq_ref, df_ref, dv_ref, dg_ref, dl0_ref, dl1_ref, don_ref, ds_scr):
        h, r = pl.program_id(0), pl.program_id(1)

        @pl.when(r == 0)
        def _():
            ds_scr[...] = jnp.zeros_like(ds_scr)

        _, vjp = jax.vjp(_hgrn2_rows, q_ref[...], f_ref[...], v_ref[...], g_ref[...], sst_ref[...],
                         l0_ref[...], l1_ref[...], on_ref[...])
        dq, dzf, dv, dga, dst, dl0, dl1, don = vjp((dcat_ref[...], ds_scr[...]))
        dq_ref[...] = dq.astype(dq_ref.dtype)
        df_ref[...] = dzf.astype(df_ref.dtype)
        dv_ref[...] = dv.astype(dv_ref.dtype)
        dg_ref[...] = dga.astype(dg_ref.dtype)
        ds_scr[...] = dst

        @pl.when(r == 0)
        def _():
            dl0_ref[...] = dl0
            dl1_ref[...] = dl1

        @pl.when(r > 0)
        def _():
            dl0_ref[...] += dl0
            dl1_ref[...] += dl1

        first = jnp.logical_and(h == 0, r == 0)

        @pl.when(first)
        def _():
            don_ref[...] = don

        @pl.when(jnp.logical_not(first))
        def _():
            don_ref[...] += don

    def rev(col_of):
        return pl.BlockSpec((RA, HD), lambda h, r, col_of=col_of: (nb - 1 - r, col_of(h)))

    vec = pl.BlockSpec((1, HD), lambda h, r: (0, h))
    grad = _sds((T, A_HEADS * HD), BF16)
    return pl.pallas_call(
        _skip_deps(body, 9, len(deps)), name="hgrn2_bwd", grid=(A_HEADS, nb),
        in_specs=[rev(lambda h: h), rev(lambda h: 8 + h), rev(lambda h: 16 + h), rev(lambda h: 24 + h),
                  vec, vec, _full_spec((1, HD)),
                  pl.BlockSpec((None, None, HD, HD), lambda h, r: (h, nb - 1 - r, 0, 0)),
                  rev(lambda h: h)] + [ANY_SPEC] * len(deps),
        out_specs=[rev(lambda h: h)] * 4 + [vec, vec, _full_spec((1, HD))],
        out_shape=[grad] * 4 + [_sds((1, A_HEADS * HD), F32)] * 2 + [_sds((1, HD), F32)],
        scratch_shapes=[pltpu.VMEM((HD, HD), F32)],
        compiler_params=_params(2))(z, z, z, z, l0, l1, onorm, sst, dcat, *deps)


def _gmlp_specs():
    vec = pl.BlockSpec((1, HD), lambda g, r: (0, g))
    ws = pl.BlockSpec((None, B_CHUNK, B_CHUNK), lambda g, r: (g, 0, 0))
    bias = pl.BlockSpec((None, B_CHUNK, 1), lambda g, r: (g, 0, 0))
    return vec, ws, bias


def _gmlp_fwd(z, cat, lnw, lnb, ws, bias):
    vec, ws_spec, bias_spec = _gmlp_specs()

    def body(u_ref, v_ref, g_ref, lnw_ref, lnb_ref, ws_ref, bias_ref, cat_in_ref, cat_ref):
        del cat_in_ref
        out = _gmlp_rows(u_ref[...], v_ref[...], g_ref[...], lnw_ref[...], lnb_ref[...],
                         ws_ref[...], bias_ref[...])
        cat_ref[...] = out.astype(cat_ref.dtype)

    return pl.pallas_call(
        body, name="gmlp_fwd", grid=(B_GROUPS, T // RB),
        in_specs=[_col_spec(RB, lambda g: 32 + g), _col_spec(RB, lambda g: 40 + g),
                  _col_spec(RB, lambda g: 48 + g), vec, vec, ws_spec, bias_spec,
                  pl.BlockSpec(memory_space=pl.ANY)],
        out_specs=_col_spec(RB, lambda g: A_HEADS + g),
        out_shape=_sds(cat.shape, cat.dtype), input_output_aliases={7: 0},
        compiler_params=_params(2))(z, z, z, lnw, lnb, ws, bias, cat)


def _gmlp_bwd(z, lnw, lnb, ws, bias, dcat):
    vec, ws_spec, bias_spec = _gmlp_specs()

    def fn(u, vb, gb, w, b, wsv, bv, dout):
        _, vjp = jax.vjp(_gmlp_rows, u, vb, gb, w, b, wsv, bv)
        return vjp(dout)

    def body(*refs):
        ins, outs = refs[:8], refs[8:]
        res = fn(*[r[...] for r in ins])
        for o, r in zip(outs[:3], res[:3]):
            o[...] = r.astype(o.dtype)
        first = pl.program_id(1) == 0
        for o, r in zip(outs[3:], res[3:]):
            @pl.when(first)
            def _(o=o, r=r):
                o[...] = r

            @pl.when(jnp.logical_not(first))
            def _(o=o, r=r):
                o[...] += r

    grad = _sds((T, B_GROUPS * HD), BF16)
    row_out = pl.BlockSpec((RB, HD), lambda g, r: (r, g))
    return pl.pallas_call(
        body, name="gmlp_bwd", grid=(B_GROUPS, T // RB),
        in_specs=[_col_spec(RB, lambda g: 32 + g), _col_spec(RB, lambda g: 40 + g),
                  _col_spec(RB, lambda g: 48 + g), vec, vec, ws_spec, bias_spec,
                  _col_spec(RB, lambda g: A_HEADS + g)],
        out_specs=[row_out] * 3 + [vec, vec, ws_spec, bias_spec],
        out_shape=[grad] * 3 + [_sds((1, B_GROUPS * HD), F32)] * 2
        + [_sds((B_GROUPS, B_CHUNK, B_CHUNK), F32), _sds((B_GROUPS, B_CHUNK, 1), F32)],
        compiler_params=_params(2))(z, z, z, lnw, lnb, ws, bias, dcat)


def _mla_pre(z1, qn, kvn, cos_t, sin_t):
    def fn(cq, ckv, kpe, cs, sn, wq, wkv):
        return _rms(cq, wq), _rms(ckv, wkv), _rope(kpe, cs, sn)
    return _pure_call("mla_pre", fn, (T // TM,),
                      [_row_spec(TM, C_RANK, 4), _row_spec(TM, C_RANK, 5), _row_spec(TM, HD, 24),
                       _row_spec(TM, HD), _row_spec(TM, HD),
                       _full_spec((1, C_RANK)), _full_spec((1, C_RANK))],
                      [_row_spec(TM, C_RANK), _row_spec(TM, C_RANK), _row_spec(TM, HD)],
                      [_sds((T, C_RANK), BF16), _sds((T, C_RANK), BF16), _sds((T, HD), BF16)],
                      (z1, z1, z1, cos_t, sin_t, qn, kvn))


def _mla_pre_bwd(z1, qn, kvn, cos_t, sin_t, dcqn, dckvn, dkp, deps=()):
    def fn(cq, ckv, cs, sn, wq, wkv, g_q, g_kv, g_kp):
        _, vjp_q = jax.vjp(_rms, cq, wq)
        dcq, dwq = vjp_q(g_q)
        _, vjp_kv = jax.vjp(_rms, ckv, wkv)
        dckv, dwkv = vjp_kv(g_kv)
        return dcq, dckv, _rope_transpose(g_kp, cs, sn), dwq, dwkv
    return _pure_call("mla_pre_bwd", fn, (T // TM,),
                      [_row_spec(TM, C_RANK, 4), _row_spec(TM, C_RANK, 5),
                       _row_spec(TM, HD), _row_spec(TM, HD),
                       _full_spec((1, C_RANK)), _full_spec((1, C_RANK)),
                       _row_spec(TM, C_RANK), _row_spec(TM, C_RANK), _row_spec(TM, HD)],
                      [_row_spec(TM, C_RANK), _row_spec(TM, C_RANK), _row_spec(TM, HD),
                       _full_spec((1, C_RANK)), _full_spec((1, C_RANK))],
                      [_sds((T, C_RANK), BF16), _sds((T, C_RANK), BF16), _sds((T, HD), BF16),
                       _sds((1, C_RANK), F32), _sds((1, C_RANK), F32)],
                      (z1, z1, cos_t, sin_t, qn, kvn, dcqn, dckvn, dkp), n_acc=2, deps=deps)


def _gate_out(o, z1):
    def fn(ov, gate):
        return ov * _silu(gate)
    return _pure_call("gate_out", fn, (T // TM,), [_row_spec(TM, D), _row_spec(TM, D, 0)],
                      [_row_spec(TM, D)], [_sds((T, D), BF16)], (o, z1))[0]


def _gate_out_bwd(o, z1, dog, deps=()):
    def fn(ov, gate, g):
        _, vjp = jax.vjp(lambda a, b: a * _silu(b), ov, gate)
        return vjp(g)
    return _pure_call("gate_out_bwd", fn, (T // TM,),
                      [_row_spec(TM, D), _row_spec(TM, D, 0), _row_spec(TM, D)],
                      [_row_spec(TM, D), _row_spec(TM, D)],
                      [_sds((T, D), F32), _sds((T, D), BF16)], (o, z1, dog), deps=deps)


TQ = 256


def _att_keys(kn_ref, kp_ref, k_scr):
    @pl.when(pl.program_id(1) == 0)
    def _():
        k_scr[:, 0:C_NOPE] = kn_ref[...]
        k_scr[:, C_NOPE:QP] = kp_ref[...]


def _att_scores(q_ref, cos_ref, sin_ref, k_scr, n):
    keys = (n + 1) * TQ
    q = q_ref[...]
    qr = jnp.concatenate([q[:, :C_NOPE], _rope(q[:, C_NOPE:], cos_ref[...], sin_ref[...])], axis=1).astype(BF16)
    return qr, _raw_nt(qr, k_scr[0:keys, :]) * ATT_SCALE


def _causal(x, n, fill):
    row = lax.broadcasted_iota(jnp.int32, (TQ, TQ), 0)
    col = lax.broadcasted_iota(jnp.int32, (TQ, TQ), 1)
    diag = jnp.where(col <= row, x[:, n * TQ:], fill)
    return diag if n == 0 else jnp.concatenate([x[:, :n * TQ], diag], axis=1)


def _per_query_block(fn):
    for n in range(T // TQ):
        pl.when(pl.program_id(1) == n)(functools.partial(fn, n))


def _att_in_specs():
    return [pl.BlockSpec((TQ, QP), lambda h, i: (i, h)),
            pl.BlockSpec((TQ, HD), lambda h, i: (i, 0)),
            pl.BlockSpec((TQ, HD), lambda h, i: (i, 0)),
            pl.BlockSpec((T, C_NOPE), lambda h, i: (0, 2 * h)),
            pl.BlockSpec((T, HD), lambda h, i: (0, 0)),
            pl.BlockSpec((T, C_V), lambda h, i: (0, 2 * h + 1))]


def _attention_fwd(q, cos_t, sin_t, kv, kp):
    def body(q_ref, cos_ref, sin_ref, kn_ref, kp_ref, v_ref, o_ref, lse_ref, k_scr):
        _att_keys(kn_ref, kp_ref, k_scr)

        def block(n):
            _, s = _att_scores(q_ref, cos_ref, sin_ref, k_scr, n)
            s = _causal(s, n, jnp.finfo(F32).min)
            m = jnp.max(s, axis=-1, keepdims=True)
            p = jnp.exp(s - m)
            l = jnp.sum(p, axis=-1, keepdims=True)
            o_ref[...] = _raw_nn(p, v_ref[0:(n + 1) * TQ, :]) / l
            lse_ref[...] = m + jnp.log(l)

        _per_query_block(block)

    return pl.pallas_call(
        body, name="attention_fwd", grid=(C_HEADS, T // TQ), in_specs=_att_in_specs(),
        out_specs=[pl.BlockSpec((TQ, C_V), lambda h, i: (i, h)),
                   pl.BlockSpec((None, TQ, 1), lambda h, i: (h, i, 0))],
        out_shape=[_sds((T, C_HEADS * C_V), F32), _sds((C_HEADS, T, 1), F32)],
        scratch_shapes=[pltpu.VMEM((T, QP), BF16)],
        compiler_params=_params(2))(q, cos_t, sin_t, kv, kp, kv)


def _attention_bwd(q, cos_t, sin_t, kv, kp, o, lse, do):
    nq = T // TQ

    def body(q_ref, cos_ref, sin_ref, kn_ref, kp_ref, v_ref, o_ref, lse_ref, do_ref,
             dq_ref, dkv_ref, dkp_ref, k_scr, dk_scr, dv_scr):
        h, i = pl.program_id(0), pl.program_id(1)
        _att_keys(kn_ref, kp_ref, k_scr)

        @pl.when(i == 0)
        def _():
            dv_scr[...] = jnp.zeros_like(dv_scr)
            dk_scr[...] = jnp.zeros_like(dk_scr)

        def block(n):
            keys = (n + 1) * TQ
            qr, s = _att_scores(q_ref, cos_ref, sin_ref, k_scr, n)
            p = _causal(jnp.exp(s - lse_ref[...]), n, 0.0)
            dov = do_ref[...]
            delta = jnp.sum(dov * o_ref[...], axis=-1, keepdims=True)
            dp = _raw_nt(dov, v_ref[0:keys, :])
            ds = p * (dp - delta) * ATT_SCALE
            dq = _raw_nn(ds, k_scr[0:keys, :])
            dq_ref[...] = jnp.concatenate(
                [dq[:, :C_NOPE], _rope_transpose(dq[:, C_NOPE:], cos_ref[...], sin_ref[...])], axis=1).astype(dq_ref.dtype)
            dv_scr[0:keys, :] += _raw_tn(p, dov)
            dk_scr[0:keys, :] += _raw_tn(ds, qr)

        _per_query_block(block)

        @pl.when(i == nq - 1)
        def _():
            dkv_ref[...] = jnp.concatenate([dk_scr[:, 0:C_NOPE], dv_scr[...]], axis=1).astype(dkv_ref.dtype)

        @pl.when(jnp.logical_and(i == nq - 1, h == 0))
        def _():
            dkp_ref[...] = dk_scr[:, C_NOPE:QP]

        @pl.when(jnp.logical_and(i == nq - 1, h > 0))
        def _():
            dkp_ref[...] += dk_scr[:, C_NOPE:QP]

    return pl.pallas_call(
        body, name="attention_bwd", grid=(C_HEADS, nq),
        in_specs=_att_in_specs() + [pl.BlockSpec((TQ, C_V), lambda h, i: (i, h)),
                                    pl.BlockSpec((None, TQ, 1), lambda h, i: (h, i, 0)),
                                    pl.BlockSpec((TQ, C_V), lambda h, i: (i, h))],
        out_specs=[pl.BlockSpec((TQ, QP), lambda h, i: (i, h)),
                   pl.BlockSpec((T, C_NOPE + C_V), lambda h, i: (0, h)),
                   _full_spec((T, HD))],
        out_shape=[_sds((T, C_HEADS * QP), BF16), _sds((T, C_HEADS * (C_NOPE + C_V)), BF16),
                   _sds((T, HD), F32)],
        scratch_shapes=[pltpu.VMEM((T, QP), BF16), pltpu.VMEM((T, QP), F32), pltpu.VMEM((T, C_V), F32)],
        compiler_params=_params(2))(q, cos_t, sin_t, kv, kp, kv, o, lse, do)


def _adamw_math(w, g, m, v):
    m = ADAM_B1 * m + (1.0 - ADAM_B1) * g
    v = ADAM_B2 * v + (1.0 - ADAM_B2) * (g * g)
    m_hat = m / (1.0 - ADAM_B1 ** ADAM_STEP)
    v_hat = v / (1.0 - ADAM_B2 ** ADAM_STEP)
    delta = -ADAM_LR * (m_hat / (jnp.sqrt(v_hat) + ADAM_EPS) + ADAM_WD * w)
    return delta, m, v


def _adamw(name, parts, w, m, v, tr):
    rows, cols = w.shape
    n_parts = parts.shape[0]

    def fn(pv, wv, mv, vv):
        g = pv[0].astype(F32)
        for d in range(1, n_parts):
            g = g + pv[d].astype(F32)
        return (g,) + _adamw_math(wv, g, mv, vv)

    blk = pl.BlockSpec((tr, cols), lambda i: (i, 0))
    return _pure_call(name, fn, (rows // tr,),
                      [pl.BlockSpec((n_parts, tr, cols), lambda i: (0, i, 0)), blk, blk, blk],
                      [blk] * 4, [_sds((rows, cols), F32)] * 4, (parts, w, m, v))


SMALL_PARAM_SHAPES = ((2, D), (2, D), (2, A_HEADS * HD), (1, HD), (1, B_GROUPS * HD), (1, B_GROUPS * HD),
                      (B_GROUPS, B_CHUNK, B_CHUNK), (B_GROUPS, B_CHUNK))
SMALL_PIECES = ((0, 0, 0, 0), (0, 1, 1, 0), (1, 0, 1, 1), (1, 1, 1, 2), (2, 0, 2, 0), (2, 1, 2, 1),
                (3, 0, 3, 8), (4, 0, 2, 2), (5, 0, 2, 3))


def _small_rows(dnpre1, dnpost0, dnpost1, dl0, dl1, donorm, dlnw, dlnb, dws, dbias, dqn, dkvn, loss_part):
    return [jnp.concatenate([dnpre1, dnpost0, dnpost1], axis=0),
            jnp.concatenate([dl0, dl1, dlnw, dlnb], axis=0),
            jnp.concatenate([dbias.reshape(B_GROUPS, B_CHUNK), donorm, loss_part], axis=0),
            dws,
            jnp.concatenate([dqn, dkvn], axis=0)]


def _adamw_small(late_all, early_all, wmv):
    n_in = 6 + 3 * len(wmv)

    def body(*refs):
        gathered, params, outs = refs[:6], refs[6:n_in], refs[n_in:]

        def total(ref):
            s = ref[0]
            for d in range(1, N_DEV):
                s = s + ref[d]
            return s

        g_late, g2048, g1024, g128, g_ws, g512 = [total(r) for r in gathered]
        arrays = (g_late, g2048, g1024, g128)

        def update(p, rows, g):
            w_ref, m_ref, v_ref = params[3 * p:3 * p + 3]
            delta, m, v = _adamw_math(w_ref[rows], g, m_ref[rows], v_ref[rows])
            for out, val in zip(outs[4 * p:4 * p + 4], (g, delta, m, v)):
                out[rows] = val

        for p, row, arr, arr_row in SMALL_PIECES:
            update(p, pl.ds(row, 1), arrays[arr][arr_row:arr_row + 1])
        update(6, slice(None), g_ws)
        update(7, slice(None), g128[0:B_GROUPS])
        outs[32][...] = g128[B_GROUPS + 1:B_GROUPS + 2]
        outs[33][...] = g512

    vmem = pl.BlockSpec(memory_space=pltpu.VMEM)
    flat = [a for t in wmv for a in t]
    out_shape = [_sds(s, F32) for s in SMALL_PARAM_SHAPES for _ in range(4)] + [_sds((1, 128), F32), _sds((2, C_RANK), F32)]
    res = pl.pallas_call(body, name="adamw_small", in_specs=[vmem] * n_in, out_specs=[vmem] * len(out_shape),
                         out_shape=out_shape,
                         compiler_params=pltpu.CompilerParams(vmem_limit_bytes=VMEM_LIMIT_V7X))(late_all, *early_all, *flat)
    return [res[4 * p:4 * p + 4] for p in range(8)], res[32], res[33]


def _exchange(name, arrs, gather, deps=()):
    n = len(arrs)
    deps = _live(deps)

    def body(*refs):
        ins, outs = refs[:n], refs[n + len(deps):2 * n + len(deps)]
        send_sems, recv_sems, local_sems = refs[2 * n + len(deps):]
        x, y, c = lax.axis_index("x"), lax.axis_index("y"), lax.axis_index("c")
        me = 4 * x + 2 * y + c

        def peer(k):
            return (x ^ (k >> 2), y ^ ((k >> 1) & 1), c ^ (k & 1))

        def copy(a, k):
            src = ins[a] if gather else ins[a].at[me ^ k]
            return pltpu.make_async_remote_copy(
                src_ref=src, dst_ref=outs[a].at[me], send_sem=send_sems.at[a, k - 1],
                recv_sem=recv_sems.at[a, k - 1], device_id=peer(k), device_id_type=MESH_ID)

        def arrival(a, k):
            src = ins[a] if gather else ins[a].at[me]
            return pltpu.make_async_remote_copy(
                src_ref=src, dst_ref=outs[a].at[me ^ k], send_sem=send_sems.at[a, k - 1],
                recv_sem=recv_sems.at[a, k - 1], device_id=peer(k), device_id_type=MESH_ID)

        own = [pltpu.make_async_copy(ins[a] if gather else ins[a].at[me], outs[a].at[me], local_sems.at[a])
               for a in range(n)]
        for cp in own:
            cp.start()
        for k in range(1, N_DEV):
            for a in range(n):
                copy(a, k).start()
        for k in range(1, N_DEV):
            for a in range(n):
                arrival(a, k).wait_recv()
        for k in range(1, N_DEV):
            for a in range(n):
                copy(a, k).wait_send()
        for cp in own:
            cp.wait()

    any_spec = pl.BlockSpec(memory_space=pl.ANY)
    out_shape = [_sds((N_DEV,) + a.shape if gather else a.shape, a.dtype) for a in arrs]
    return pl.pallas_call(
        body, name=name, in_specs=[any_spec] * (n + len(deps)), out_specs=[any_spec] * n, out_shape=out_shape,
        scratch_shapes=[pltpu.SemaphoreType.DMA((n, N_DEV - 1)), pltpu.SemaphoreType.DMA((n, N_DEV - 1)),
                        pltpu.SemaphoreType.DMA((n,))],
        compiler_params=pltpu.CompilerParams(has_side_effects=True))(*arrs, *deps)


HBM_SPEC = pl.BlockSpec(memory_space=pltpu.HBM)
SEM_SPEC = pl.BlockSpec(memory_space=pltpu.SEMAPHORE)
DATAFLOW = pltpu.SideEffectType.DATAFLOW_SIDE_EFFECTING


def _my_index():
    return 4 * lax.axis_index("x") + 2 * lax.axis_index("y") + lax.axis_index("c")


def _plan_copies(plan, refs, send_sems, recv_sems):
    x, y, c = lax.axis_index("x"), lax.axis_index("y"), lax.axis_index("c")
    return [pltpu.make_async_remote_copy(
        src_ref=src, dst_ref=dst, send_sem=send_sems.at[i], recv_sem=recv_sems.at[i],
        device_id=(x ^ (k >> 2), y ^ ((k >> 1) & 1), c ^ (k & 1)), device_id_type=MESH_ID)
        for i, (src, dst, k) in enumerate(plan(refs, 4 * x + 2 * y + c))]


def _split_call(name, bufs, waits=None, starts=None, deps=(), fills=None):
    n = len(bufs)
    deps = _live(deps)
    n_wait = 2 if waits else 0

    def body(*refs):
        zones = refs[:n]
        local = []
        if fills:
            local = [pltpu.make_async_copy(src, dst, refs[-1].at[i])
                     for i, (src, dst) in enumerate(fills[0](zones, _my_index()))]
            for cp in local:
                cp.start()
        if waits:
            for cp in _plan_copies(waits[2], zones, refs[n], refs[n + 1]):
                cp.wait_send()
                cp.wait_recv()
        if starts:
            first_out = n + n_wait + len(deps)
            for cp in _plan_copies(starts[0], zones, refs[first_out], refs[first_out + 1]):
                cp.start()
            token = refs[first_out + 2 + n]
            token[...] = jnp.zeros_like(token)
        for cp in local:
            cp.wait()

    out_specs, out_shape = [], []
    if starts:
        sems = pltpu.SemaphoreType.DMA((starts[1],))
        out_specs, out_shape = [SEM_SPEC, SEM_SPEC], [sems, sems]
    out_specs += [HBM_SPEC] * n
    out_shape += [pltpu.HBM(b.shape, b.dtype) for b in bufs]
    if starts:
        out_specs.append(pl.BlockSpec(memory_space=pltpu.VMEM))
        out_shape.append(_sds((8, 128), F32))
    first_buf = 2 if starts else 0
    res = pl.pallas_call(
        body, name=name,
        in_specs=[HBM_SPEC] * n + [SEM_SPEC] * n_wait + [ANY_SPEC] * len(deps),
        out_specs=out_specs, out_shape=out_shape,
        input_output_aliases={i: first_buf + i for i in range(n)},
        scratch_shapes=[pltpu.SemaphoreType.DMA((fills[1],))] if fills else [],
        compiler_params=pltpu.CompilerParams(has_side_effects=DATAFLOW),
    )(*[pltpu.with_memory_space_constraint(b, pltpu.HBM) for b in bufs], *(waits[:2] if waits else ()), *deps)
    out_bufs = list(res[first_buf:first_buf + n])
    return out_bufs, ((res[0], res[1]) if starts else None), (res[-1] if starts else None)


def _direct_plan(n, gather):
    def plan(refs, me):
        return [(refs[a] if gather else refs[a].at[me ^ k], refs[n + a].at[me], k)
                for k in range(1, N_DEV) for a in range(n)]
    return plan


def _own_slot(n, gather):
    def fills(refs, me):
        return [(refs[a] if gather else refs[a].at[me], refs[n + a].at[me]) for a in range(n)]
    return fills


def _exchange_start(name, arrs, gather, deps=()):
    n = len(arrs)
    lands = [lax.empty((N_DEV,) + a.shape if gather else a.shape, a.dtype) for a in arrs]
    plan = _direct_plan(n, gather)
    bufs, sems, token = _split_call(name, list(arrs) + lands, starts=(plan, n * (N_DEV - 1)), deps=deps)
    return (n, plan, sems, bufs, (_own_slot(n, gather), n)), token


def _exchange_wait(name, handle, after):
    return _split_done(name, handle, after)


ICI_PEERS = (2, 4, 6)
SIBLING = 1


def _gather2_send(name, arrs, deps=()):
    n = len(arrs)
    lands = [lax.empty((N_DEV,) + a.shape, a.dtype) for a in arrs]

    def plan(refs, me_):
        return [(refs[a], refs[n + a].at[me_], k) for k in (SIBLING,) + ICI_PEERS for a in range(n)]

    bufs, sems, token = _split_call(name, list(arrs) + lands, starts=(plan, 4 * n), deps=deps)
    return (n, plan, sems, bufs, (_own_slot(n, True), n)), token


def _gather2_relay(name, handle, after):
    n, plan, sems, bufs, fills = handle

    def relay(refs, me_):
        return [(refs[n + a].at[me_ ^ k], refs[n + a].at[me_ ^ k], SIBLING) for k in ICI_PEERS for a in range(n)]

    bufs, sems2, token = _split_call(name, bufs, waits=(sems[0], sems[1], plan), starts=(relay, 3 * n), deps=[after])
    return (n, relay, sems2, bufs, fills), token


def _split_done(name, handle, after, all_bufs=False):
    n, plan, sems, bufs, fills = handle
    bufs, _, _ = _split_call(name, bufs, waits=(sems[0], sems[1], plan), deps=[after], fills=fills)
    return bufs if all_bufs else bufs[n:]


def _scatter2_pair(name, stacks, deps=()):
    n = len(stacks)
    pairs = [lax.empty((4,) + s.shape[1:], s.dtype) for s in stacks]

    def plan(refs, me):
        return [(refs[a].at[(me ^ SIBLING) ^ (2 * j)], refs[n + a].at[j], SIBLING) for j in range(4) for a in range(n)]

    bufs, sems, token = _split_call(name, list(stacks) + pairs, starts=(plan, 4 * n), deps=deps)
    return (n, plan, sems, bufs, None), token


def _pair_add(name, stack, pair, me):
    _, rows, cols = stack.shape
    tr = rows // 8

    def body(me_ref, s_ref, p_ref, o_ref):
        del me_ref
        o_ref[...] = (s_ref[...].astype(F32) + p_ref[...].astype(F32)).astype(o_ref.dtype)

    grid_spec = pltpu.PrefetchScalarGridSpec(
        num_scalar_prefetch=1, grid=(4, rows // tr),
        in_specs=[pl.BlockSpec((None, tr, cols), lambda j, i, me_ref: (me_ref[0] ^ (2 * j), i, 0)),
                  pl.BlockSpec((None, tr, cols), lambda j, i, me_ref: (j, i, 0))],
        out_specs=pl.BlockSpec((None, tr, cols), lambda j, i, me_ref: (j, i, 0)))
    return pl.pallas_call(body, name=name, grid_spec=grid_spec, out_shape=_sds((4, rows, cols), stack.dtype),
                          compiler_params=_params(2))(me.reshape(1).astype(jnp.int32), stack, pair)


def _scatter2_send(name, chip_sums, deps=()):
    n = len(chip_sums)
    finals = [lax.empty(c.shape, c.dtype) for c in chip_sums]

    def plan(refs, me):
        del me
        return [(refs[a].at[j], refs[n + a].at[j], 2 * j) for j in range(1, 4) for a in range(n)]

    def fills(refs, me):
        del me
        return [(refs[a].at[0], refs[n + a].at[0]) for a in range(n)]

    bufs, sems, token = _split_call(name, list(chip_sums) + finals, starts=(plan, 3 * n), deps=deps)
    return (n, plan, sems, bufs, (fills, n)), token


def _pad_rope(p):
    z = jnp.zeros(p.shape[:-1] + (32,), p.dtype)
    return jnp.concatenate([p[..., :32], z, p[..., 32:], z], axis=-1)


def _unpad_rope(p):
    return jnp.concatenate([p[..., :32], p[..., 64:96]], axis=-1)


def _odd_in_layout(w):
    w = w.transpose(1, 0, 2).reshape(D, ODD_IN)
    cq, ckv, kpe, gate = w[:, :512], w[:, 512:1024], w[:, 1024:1088], w[:, 1088:]
    return jnp.concatenate([gate, cq, ckv, _pad_rope(kpe)], axis=1)


def _odd_in_unlayout(dw):
    gate, cq, ckv, kpe = dw[:, :2048], dw[:, 2048:2560], dw[:, 2560:3072], _unpad_rope(dw[:, 3072:])
    w = jnp.concatenate([cq, ckv, kpe, gate], axis=1)
    return w.reshape(D, N_DEV, ODD_IN // N_DEV).transpose(1, 0, 2)


def _qb_layout(w):
    w = w.transpose(1, 0, 2).reshape(C_RANK, C_HEADS, C_QK)
    w = jnp.concatenate([w[..., :C_NOPE], _pad_rope(w[..., C_NOPE:])], axis=-1)
    return w.reshape(C_RANK, C_HEADS * QP)


def _qb_unlayout(dw):
    dw = dw.reshape(C_RANK, C_HEADS, QP)
    dw = jnp.concatenate([dw[..., :C_NOPE], _unpad_rope(dw[..., C_NOPE:])], axis=-1)
    return dw.reshape(C_RANK, N_DEV, C_HEADS * C_QK // N_DEV).transpose(1, 0, 2)


def _rope_tables(positions):
    inv_freq = ROPE_THETA ** (-jnp.arange(0, C_ROPE, 2, dtype=F32) / C_ROPE)
    ang = positions.astype(F32)[0][:, None] * inv_freq
    cos, sin = jnp.cos(ang), jnp.sin(ang)
    z = jnp.zeros_like(cos)
    return jnp.concatenate([cos, z, cos, z], axis=1), jnp.concatenate([-sin, z, sin, z], axis=1)


def _forward_backward(x, cos_t, sin_t, target, norm_pre, norm_post, lb_logits, a_onorm, ln_w, ln_b,
                      b_ws, b_bias, get_w, put_g, put_small=None, start_dep=None):
    npre0, npre1 = norm_pre[0:1], norm_pre[1:2]
    npost0, npost1 = norm_post[0:1], norm_post[1:2]
    l0, l1 = lb_logits[0:1], lb_logits[1:2]
    bias_col = b_bias.reshape(B_GROUPS, B_CHUNK, 1)
    ws = b_ws.reshape(B_GROUPS, B_CHUNK, B_CHUNK)

    h0 = _pre_norm("pre_norm0", x, npre0, deps=[start_dep])
    w_ev_in = get_w("ev_in", h0)
    z0 = _mm_nn("ev_in", h0, w_ev_in, F32, 1024, 896)
    cat, sst = _hgrn2_fwd(z0, l0, l1, a_onorm)
    cat = _gmlp_fwd(z0, cat, ln_w, ln_b, ws, bias_col)
    w_ev_out = get_w("ev_out", cat)
    y0 = _mm_nn("ev_out", cat, w_ev_out, F32, 1024, 1024)
    x1, h1 = _post_pre_norm(x, y0, npost0, npre1)
    w_od_in, w_qb, w_kvb, q_norm, kv_norm = get_w("od_mid", h1)
    z1 = _mm_nn("od_in", h1, w_od_in[None], F32, 1024, 640)
    cqn, ckvn, kp = _mla_pre(z1, q_norm, kv_norm, cos_t, sin_t)
    q = _mm_nn("od_qb", cqn, w_qb[None], F32, 1024, 1024)
    kv = _mm_nn("od_kvb", ckvn, w_kvb, BF16, 1024, 512)
    o, lse = _attention_fwd(q, cos_t, sin_t, kv, kp)
    og = _gate_out(o, z1)
    w_od_out = get_w("od_out", og)
    y1 = _mm_nn("od_out", og, w_od_out, F32, 1024, 1024)
    dx2, dy1, loss_part, dnpost1 = _final_loss(x1, y1, npost1, target)

    g_od_out = _mm_tn("od_out_dw", og, dy1, 1, BF16, 1024, 1024)
    tok = put_g("od_out", [g_od_out.reshape(N_DEV, D // N_DEV, D)])
    dog = _mm_nt("od_out_dx", dy1, w_od_out, F32, 1024, 1024, deps=[tok])
    do, dgate = _gate_out_bwd(o, z1, dog)
    dq, dkv, dkp = _attention_bwd(q, cos_t, sin_t, kv, kp, o, lse, do)
    g_qb = _mm_tn("od_qb_dw", cqn, dq, 1, F32, 512, 1024)
    g_kvb = _mm_tn("od_kvb_dw", ckvn, dkv, N_DEV, BF16, 512, 512)
    tok = put_g("od_qkv", [_qb_unlayout(g_qb[0]).astype(BF16), g_kvb])
    dcqn = _mm_nt("od_qb_dx", dq, w_qb[None], F32, 1024, 512, deps=[tok])
    dckvn = _mm_nt("od_kvb_dx", dkv, w_kvb, F32, 1024, 512)
    dcq, dckv, dkpe, dqn, dkvn = _mla_pre_bwd(z1, q_norm, kv_norm, cos_t, sin_t, dcqn, dckvn, dkp)
    dz1 = jnp.concatenate([dgate, dcq, dckv, dkpe], axis=1)
    g_od_in = _mm_tn("od_in_dw", h1, dz1, 1, F32, 1024, 640)
    tok = put_g("od_in", [_odd_in_unlayout(g_od_in[0]).astype(BF16)])
    dh1 = _mm_nt("od_in_dx", dz1, w_od_in[None], F32, 1024, 1024, deps=[tok])
    dx1, dy0, dnpost0, dnpre1 = _post_pre_norm_bwd(y0, x1, npost0, npre1, dx2, dh1)

    g_ev_out = _mm_tn("ev_out_dw", cat, dy0, 1, BF16, 1024, 1024)
    tok = put_g("ev_out", [g_ev_out.reshape(N_DEV, D // N_DEV, D)])
    dcat = _mm_nt("ev_out_dx", dy0, w_ev_out, F32, 1024, 1024, deps=[tok])
    dqa, dfa, dia, dga, dl0, dl1, donorm = _hgrn2_bwd(z0, l0, l1, a_onorm, sst, dcat)
    dub, dvb, dgb, dlnw, dlnb, dws, dbias = _gmlp_bwd(z0, ln_w, ln_b, ws, bias_col, dcat)
    dz0 = jnp.concatenate([dqa, dfa, dia, dga, dub, dvb, dgb], axis=1)
    early = _small_rows(dnpre1, dnpost0, dnpost1, dl0, dl1, donorm, dlnw, dlnb, dws, dbias, dqn, dkvn, loss_part)
    tok = put_small(early) if put_small else None
    g_ev_in = _mm_tn("ev_in_dw", h0, dz0, N_DEV, BF16, 1024, 896, deps=[tok])
    tok = put_g("ev_in", [g_ev_in])
    dh0 = _mm_nt("ev_in_dx", dz0, w_ev_in, F32, 1024, 1024, deps=[tok])
    grad_x, dnpre0 = _pre_norm_bwd(x, npre0, dh0, dx1)
    return grad_x, early, dnpre0


def kernel(x, positions, norm_pre, norm_post, ev_w_in, ev_lb_logits, ev_a_onorm, ev_b_ln_w, ev_b_ln_b, ev_b_ws, ev_b_bias, ev_w_out, od_w_in, od_q_norm, od_w_qb, od_kv_norm, od_w_kvb, od_w_out, loss_target, m_norm_pre, m_norm_post, m_ev_w_in, m_ev_lb_logits, m_ev_a_onorm, m_ev_b_ln_w, m_ev_b_ln_b, m_ev_b_ws, m_ev_b_bias, m_ev_w_out, m_od_w_in, m_od_q_norm, m_od_w_qb, m_od_kv_norm, m_od_w_kvb, m_od_w_out, v_norm_pre, v_norm_post, v_ev_w_in, v_ev_lb_logits, v_ev_a_onorm, v_ev_b_ln_w, v_ev_b_ln_b, v_ev_b_ws, v_ev_b_bias, v_ev_w_out, v_od_w_in, v_od_q_norm, v_od_w_qb, v_od_kv_norm, v_od_w_kvb, v_od_w_out):
    me = 4 * lax.axis_index("x") + 2 * lax.axis_index("y") + lax.axis_index("c")
    bf = lambda w: w[0].astype(BF16)

    norms = jnp.pad(jnp.concatenate([od_q_norm, od_kv_norm], axis=1), ((0, 7), (0, 0)))
    first_h, tok = _gather2_send("gather_ev_in", [bf(ev_w_in)])
    rest_h, tok = _gather2_send("gather_rest", [bf(ev_w_out), bf(od_w_in), bf(od_w_qb), bf(od_w_kvb), norms,
                                                bf(od_w_out)], deps=[tok])
    first_h, tok = _gather2_relay("relay_ev_in", first_h, tok)
    rest = []

    def get_w(group, after):
        if group == "ev_in":
            return _split_done("arrived_ev_in", first_h, after)[0]
        if not rest:
            relayed, token = _gather2_relay("relay_rest", rest_h, after)
            rest.extend(_split_done("arrived_rest", relayed, token))
        w_ev_out, w_od_in, w_qb, w_kvb, norms_all, w_od_out = rest
        if group == "ev_out":
            return w_ev_out.reshape(1, D, D)
        if group == "od_out":
            return w_od_out.reshape(1, D, D)
        return (_odd_in_layout(w_od_in), _qb_layout(w_qb), w_kvb,
                norms_all[:, 0, :64].reshape(1, C_RANK), norms_all[:, 0, 64:].reshape(1, C_RANK))

    scatters = {}

    def put_g(group, grads):
        if group == "ev_in":
            paired, token = _scatter2_pair("pair_ev_in", grads)
            n = len(grads)
            bufs = _split_done("paired_ev_in", paired, token, all_bufs=True)
            chip_sums = [_pair_add("pair_add_ev_in", bufs[a], bufs[n + a], me) for a in range(n)]
            scatters[group], token = _scatter2_send("scatter_ev_in", chip_sums)
        else:
            scatters[group], token = _exchange_start("scatter_" + group, grads, False)
        return token

    def put_small(early):
        scatters["small"], token = _exchange_start("gather_small_early", early, True)
        return token

    cos_t, sin_t = _rope_tables(positions)
    grad_x, _, dnpre0 = _forward_backward(
        x[0], cos_t, sin_t, loss_target[0], norm_pre, norm_post, ev_lb_logits, ev_a_onorm, ev_b_ln_w,
        ev_b_ln_b, ev_b_ws, ev_b_bias, get_w, put_g, put_small, start_dep=tok)

    big_w = {"ev_w_in": ev_w_in, "ev_w_out": ev_w_out, "od_w_in": od_w_in, "od_w_qb": od_w_qb,
             "od_w_kvb": od_w_kvb, "od_w_out": od_w_out}
    big_m = {"ev_w_in": m_ev_w_in, "ev_w_out": m_ev_w_out, "od_w_in": m_od_w_in, "od_w_qb": m_od_w_qb,
             "od_w_kvb": m_od_w_kvb, "od_w_out": m_od_w_out}
    big_v = {"ev_w_in": v_ev_w_in, "ev_w_out": v_ev_w_out, "od_w_in": v_od_w_in, "od_w_qb": v_od_w_qb,
             "od_w_kvb": v_od_w_kvb, "od_w_out": v_od_w_out}
    big_out = {}
    after = grad_x
    for group, names in (("od_out", ["od_w_out"]), ("od_qkv", ["od_w_qb", "od_w_kvb"]), ("od_in", ["od_w_in"]),
                         ("ev_out", ["ev_w_out"])):
        parts = _exchange_wait("summed_" + group, scatters[group], after)
        for nm, p in zip(names, parts):
            w = big_w[nm][0]
            big_out[nm] = [r[None] for r in _adamw("adamw_" + nm, p, w, big_m[nm][0], big_v[nm][0], w.shape[0] // 8)]
            after = big_out[nm][0]

    late_all = _exchange("gather_small_late", [dnpre0], gather=True, deps=[after])[0]
    early_all = _exchange_wait("arrived_small_early", scatters["small"], late_all)

    small_w = (norm_pre, norm_post, ev_lb_logits, ev_a_onorm, ev_b_ln_w, ev_b_ln_b, ev_b_ws, ev_b_bias)
    small_m = (m_norm_pre, m_norm_post, m_ev_lb_logits, m_ev_a_onorm, m_ev_b_ln_w, m_ev_b_ln_b, m_ev_b_ws, m_ev_b_bias)
    small_v = (v_norm_pre, v_norm_post, v_ev_lb_logits, v_ev_a_onorm, v_ev_b_ln_w, v_ev_b_ln_b, v_ev_b_ws, v_ev_b_bias)
    wmv = [tuple(a.reshape(s) for a in t) for s, t in zip(SMALL_PARAM_SHAPES, zip(small_w, small_m, small_v))]
    small_res, loss_row, g_norm_rows = _adamw_small(late_all, early_all, wmv)
    small_out = [[r.reshape(w.shape) for r in four] for four, w in zip(small_res, small_w)]
    loss = loss_row[0, 0]

    g_norms = jnp.concatenate([lax.dynamic_slice(g_norm_rows, (0, 64 * me), (1, 64)),
                               lax.dynamic_slice(g_norm_rows, (1, 64 * me), (1, 64))], axis=1)
    res_n = _adamw("adamw_norms", g_norms[None],
                   jnp.concatenate([od_q_norm, od_kv_norm], axis=1),
                   jnp.concatenate([m_od_q_norm, m_od_kv_norm], axis=1),
                   jnp.concatenate([v_od_q_norm, v_od_kv_norm], axis=1), 1)
    qn_out = [r[:, :64] for r in res_n]
    kvn_out = [r[:, 64:] for r in res_n]

    parts = _split_done("summed_ev_in", scatters["ev_in"], loss_row)
    w = ev_w_in[0]
    big_out["ev_w_in"] = [r[None] for r in _adamw("adamw_ev_w_in", parts[0], w, m_ev_w_in[0], v_ev_w_in[0],
                                                  w.shape[0] // 8)]

    order = ("norm_pre", "norm_post", "ev_w_in", "ev_lb_logits", "ev_a_onorm", "ev_b_ln_w", "ev_b_ln_b",
             "ev_b_ws", "ev_b_bias", "ev_w_out", "od_w_in", "od_q_norm", "od_w_qb", "od_kv_norm",
             "od_w_kvb", "od_w_out")
    small_names = ("norm_pre", "norm_post", "ev_lb_logits", "ev_a_onorm", "ev_b_ln_w", "ev_b_ln_b",
                   "ev_b_ws", "ev_b_bias")
    outs = [loss, grad_x[None]]
    for kind in range(4):
        for nm in order:
            if nm in big_out:
                outs.append(big_out[nm][kind])
            elif nm == "od_q_norm":
                outs.append(qn_out[kind])
            elif nm == "od_kv_norm":
                outs.append(kvn_out[kind])
            else:
                outs.append(small_out[small_names.index(nm)][kind])
    return tuple(outs)
```

```python
import functools

import jax
import jax.numpy as jnp
from jax import lax
from jax.experimental import pallas as pl
from jax.experimental.pallas import tpu as pltpu

F32 = jnp.float32
BF16 = jnp.bfloat16

N_DEV = 8
T = 2048
D = 2048
EPS = 1e-6
A_HEADS = 8
HD = 128
A_CHUNK = 64
A_SUB = 16
B_GROUPS = 8
B_CHUNK = 128
EVEN_IN = 7168
C_HEADS = 16
C_RANK = 512
C_NOPE = 128
C_ROPE = 64
C_QK = C_NOPE + C_ROPE
C_V = 128
ODD_IN = 3136
ODD_IN_PAD = 3200
QP = 256
ROPE_THETA = 10000.0
ATT_SCALE = C_QK ** -0.5

ADAM_LR = 0.001
ADAM_B1 = 0.9
ADAM_B2 = 0.999
ADAM_EPS = 1e-08
ADAM_WD = 0.01
ADAM_STEP = 10

VMEM_LIMIT_V7X = 56 * 1024 * 1024
MESH_ID = pl.DeviceIdType.MESH


def _params(n_grid):
    return pltpu.CompilerParams(dimension_semantics=("arbitrary",) * n_grid,
                                vmem_limit_bytes=VMEM_LIMIT_V7X)


def _dg(a, b, ca, cb):
    return lax.dot_general(a.astype(BF16), b.astype(BF16), (((ca,), (cb,)), ((), ())),
                           preferred_element_type=F32)


def _raw_nn(a, b):
    return _dg(a, b, 1, 0)


def _raw_nt(a, b):
    return _dg(a, b, 1, 1)


def _raw_tn(a, b):
    return _dg(a, b, 0, 0)


@jax.custom_vjp
def _dot_nn(a, b):
    return _raw_nn(a, b)


def _dot_nn_fwd(a, b):
    return _raw_nn(a, b), (a.astype(BF16), b.astype(BF16))


def _dot_nn_bwd(res, g):
    a, b = res
    return _raw_nt(g, b), _raw_tn(a, g)


_dot_nn.defvjp(_dot_nn_fwd, _dot_nn_bwd)


@jax.custom_vjp
def _dot_nt(a, b):
    return _raw_nt(a, b)


def _dot_nt_fwd(a, b):
    return _raw_nt(a, b), (a.astype(BF16), b.astype(BF16))


def _dot_nt_bwd(res, g):
    a, b = res
    return _raw_nn(g, b), _raw_tn(g, a)


_dot_nt.defvjp(_dot_nt_fwd, _dot_nt_bwd)


@jax.custom_vjp
def _dot_tn(a, b):
    return _raw_tn(a, b)


def _dot_tn_fwd(a, b):
    return _raw_tn(a, b), (a.astype(BF16), b.astype(BF16))


def _dot_tn_bwd(res, g):
    a, b = res
    return _raw_nt(b, g), _raw_nn(a, g)


_dot_tn.defvjp(_dot_tn_fwd, _dot_tn_bwd)


@jax.custom_vjp
def _sigmoid(x):
    e = jnp.exp(-jnp.abs(x))
    return jnp.where(x >= 0, 1.0 / (1.0 + e), e / (1.0 + e))


def _sigmoid_fwd(x):
    s = _sigmoid(x)
    return s, s


def _sigmoid_bwd(s, g):
    return (g * s * (1.0 - s),)


_sigmoid.defvjp(_sigmoid_fwd, _sigmoid_bwd)


def _silu(x):
    return x * _sigmoid(x)


def _rms(x, w):
    return x * lax.rsqrt(jnp.mean(x * x, axis=-1, keepdims=True) + EPS) * w


def _split3(x):
    hi = x.astype(BF16)
    r = x - hi.astype(F32)
    mid = r.astype(BF16)
    lo = (r - mid.astype(F32)).astype(BF16)
    return hi, mid, lo


def _mask_apply(mask_bf16, x, contract):
    out = None
    for piece in _split3(x):
        d = lax.dot_general(mask_bf16, piece, (((contract,), (0,)), ((), ())),
                            preferred_element_type=F32)
        out = d if out is None else out + d
    return out


def _chunk_tri(rows):
    r = lax.broadcasted_iota(jnp.int32, (rows, rows), 0)
    c = lax.broadcasted_iota(jnp.int32, (rows, rows), 1)
    return ((r >= c) & (r // A_CHUNK == c // A_CHUNK)).astype(BF16)


@jax.custom_vjp
def _chunk_cumsum(x):
    return _mask_apply(_chunk_tri(x.shape[0]), x, 1)


def _chunk_cumsum_fwd(x):
    return _chunk_cumsum(x), None


def _chunk_cumsum_bwd(_, g):
    return (_mask_apply(_chunk_tri(g.shape[0]), g, 0),)


_chunk_cumsum.defvjp(_chunk_cumsum_fwd, _chunk_cumsum_bwd)


def _hgrn2_rows(q, zf, v, ga, st, l0, l1, onorm):
    rows = q.shape[0]
    n_sub = A_CHUNK // A_SUB
    mx = jnp.maximum(l0, l1)
    e0 = jnp.exp(l0 - mx)
    e1 = jnp.exp(l1 - mx)
    lb = e0 / (e0 + e1)
    lf = jnp.log(lb + (1.0 - lb) * _sigmoid(zf))
    k = (1.0 - lb) * _sigmoid(-zf)
    b = _chunk_cumsum(lf)

    t_idx = lax.broadcasted_iota(jnp.int32, (A_CHUNK, n_sub * A_CHUNK), 0)
    c_idx = lax.broadcasted_iota(jnp.int32, (A_CHUNK, n_sub * A_CHUNK), 1)
    sel = (c_idx // A_CHUNK == t_idx // A_SUB) & (c_idx % A_CHUNK <= t_idx)
    key_row = lax.broadcasted_iota(jnp.int32, (A_CHUNK, HD), 0)

    outs = []
    for n in range(rows // A_CHUNK):
        lo = n * A_CHUNK
        qc, kc, vc = q[lo:lo + A_CHUNK], k[lo:lo + A_CHUNK], v[lo:lo + A_CHUNK]
        lfc, bc = lf[lo:lo + A_CHUNK], b[lo:lo + A_CHUNK]
        b_last = bc[A_CHUNK - 1:A_CHUNK]
        o_inter = _dot_nt(qc * jnp.exp(bc), st)
        kv_t = _dot_tn(vc, kc * jnp.exp(b_last - bc))
        st = st * jnp.exp(b_last) + kv_t
        g_rows, k_subs = [], []
        for i in range(n_sub):
            g_i = bc[i * A_SUB:i * A_SUB + 1] - lfc[i * A_SUB:i * A_SUB + 1]
            g_rows.append(jnp.broadcast_to(g_i, (A_SUB, HD)))
            expo = jnp.where(key_row < (i + 1) * A_SUB, g_i - bc, -jnp.inf)
            k_subs.append(kc * jnp.exp(expo))
        q_sub = qc * jnp.exp(bc - jnp.concatenate(g_rows, axis=0))
        scores = _dot_nt(q_sub, jnp.concatenate(k_subs, axis=0))
        scores = jnp.where(sel, scores, 0.0)
        o_intra = _dot_nn(scores, jnp.concatenate([vc] * n_sub, axis=0))
        outs.append(o_inter + o_intra)
    o = jnp.concatenate(outs, axis=0)
    return _rms(o, onorm) * _silu(ga), st


def _gmlp_rows(u, vb, gb, lnw, lnb, ws, bias):
    rows = u.shape[0]
    mu = jnp.mean(vb, axis=-1, keepdims=True)
    xc = vb - mu
    vg = xc * lax.rsqrt(jnp.mean(xc * xc, axis=-1, keepdims=True) + EPS) * lnw + lnb
    r = lax.broadcasted_iota(jnp.int32, (B_CHUNK, B_CHUNK), 0)
    c = lax.broadcasted_iota(jnp.int32, (B_CHUNK, B_CHUNK), 1)
    ws_causal = jnp.where(r >= c, ws, 0.0)
    svs = [_dot_nn(ws_causal, vg[n * B_CHUNK:(n + 1) * B_CHUNK]) + bias
           for n in range(rows // B_CHUNK)]
    return u * jnp.concatenate(svs, axis=0) * _silu(gb)


def _rope(x, cos_t, sin_t):
    return x * cos_t + pltpu.roll(x, 64, 1) * sin_t


def _rope_transpose(g, cos_t, sin_t):
    return g * cos_t + pltpu.roll(g * sin_t, 64, 1)


ANY_SPEC = pl.BlockSpec(memory_space=pl.ANY)


def _live(deps):
    return [d for d in deps if d is not None]


def _skip_deps(body, n_in, n_deps):
    def wrapped(*refs):
        return body(*refs[:n_in], *refs[n_in + n_deps:])
    return wrapped


def _pure_call(name, fn, grid, in_specs, out_specs, out_shape, args, n_acc=0, deps=()):
    deps = _live(deps)
    n_in, n_out, n_deps = len(in_specs), len(out_specs), len(deps)
    in_specs = list(in_specs) + [ANY_SPEC] * n_deps
    args = tuple(args) + tuple(deps)

    def body(*refs):
        res = fn(*[r[...] for r in refs[:n_in]])
        if not isinstance(res, (tuple, list)):
            res = (res,)
        outs = refs[n_in + n_deps:n_in + n_deps + n_out]
        for o, r in zip(outs[:n_out - n_acc], res[:n_out - n_acc]):
            o[...] = r.astype(o.dtype)
        if n_acc:
            first = functools.reduce(jnp.logical_and, [pl.program_id(i) == 0 for i in range(len(grid))])
            for o, r in zip(outs[n_out - n_acc:], res[n_out - n_acc:]):
                @pl.when(first)
                def _(o=o, r=r):
                    o[...] = r.astype(o.dtype)

                @pl.when(jnp.logical_not(first))
                def _(o=o, r=r):
                    o[...] += r.astype(o.dtype)

    return pl.pallas_call(body, name=name, grid=grid, in_specs=in_specs, out_specs=out_specs,
                          out_shape=out_shape, compiler_params=_params(len(grid)))(*args)


def _sds(shape, dtype):
    return jax.ShapeDtypeStruct(shape, dtype)


def _row_spec(tm, width, col=0):
    return pl.BlockSpec((tm, width), lambda i, col=col: (i, col))


def _full_spec(shape):
    nd = len(shape)
    return pl.BlockSpec(shape, lambda *_: (0,) * nd)


def _mm_nn(name, a, b, out_dtype, tm, tn, deps=()):
    deps = _live(deps)
    m, k = a.shape
    j, _, n = b.shape
    per = n // tn

    def body(a_ref, b_ref, o_ref):
        o_ref[...] = _raw_nn(a_ref[...], b_ref[...]).astype(o_ref.dtype)

    return pl.pallas_call(
        _skip_deps(body, 2, len(deps)), name=name, grid=(m // tm, j * per),
        in_specs=[pl.BlockSpec((tm, k), lambda i, c: (i, 0)),
                  pl.BlockSpec((None, k, tn), lambda i, c: (c // per, 0, c % per))] + [ANY_SPEC] * len(deps),
        out_specs=pl.BlockSpec((tm, tn), lambda i, c: (i, c)),
        out_shape=_sds((m, j * n), out_dtype), compiler_params=_params(2))(a, b, *deps)


def _mm_nt(name, a, b, out_dtype, tm, tn, deps=()):
    deps = _live(deps)
    m = a.shape[0]
    j, nn, n = b.shape

    def body(a_ref, b_ref, o_ref, acc_ref):
        part = _raw_nt(a_ref[...], b_ref[...])
        if j == 1:
            o_ref[...] = part.astype(o_ref.dtype)
        else:
            kk = pl.program_id(2)

            @pl.when(kk == 0)
            def _():
                acc_ref[...] = part

            @pl.when(kk > 0)
            def _():
                acc_ref[...] += part

            @pl.when(kk == j - 1)
            def _():
                o_ref[...] = acc_ref[...].astype(o_ref.dtype)

    acc_shape = (tm, tn) if j > 1 else (8, 128)
    return pl.pallas_call(
        _skip_deps(body, 2, len(deps)), name=name, grid=(m // tm, nn // tn, j),
        in_specs=[pl.BlockSpec((tm, n), lambda i, c, kk: (i, kk)),
                  pl.BlockSpec((None, tn, n), lambda i, c, kk: (kk, c, 0))] + [ANY_SPEC] * len(deps),
        out_specs=pl.BlockSpec((tm, tn), lambda i, c, kk: (i, c)),
        out_shape=_sds((m, nn), out_dtype),
        scratch_shapes=[pltpu.VMEM(acc_shape, F32)], compiler_params=_params(3))(a, b, *deps)


def _mm_tn(name, a, b, j, out_dtype, tm, tn, deps=()):
    deps = _live(deps)
    k, m = a.shape
    n = b.shape[1] // j
    per = n // tn

    def body(a_ref, b_ref, o_ref):
        o_ref[...] = _raw_tn(a_ref[...], b_ref[...]).astype(o_ref.dtype)

    return pl.pallas_call(
        _skip_deps(body, 2, len(deps)), name=name, grid=(m // tm, j * per),
        in_specs=[pl.BlockSpec((k, tm), lambda i, c: (0, i)),
                  pl.BlockSpec((k, tn), lambda i, c: (0, c))] + [ANY_SPEC] * len(deps),
        out_specs=pl.BlockSpec((None, tm, tn), lambda i, c: (c // per, i, c % per)),
        out_shape=_sds((j, m, n), out_dtype), compiler_params=_params(2))(a, b, *deps)


TM = 256


def _pre_norm(name, x, w_row, deps=()):
    def fn(xv, w):
        return _rms(xv, w)
    return _pure_call(name, fn, (T // TM,), [_row_spec(TM, D), _full_spec((1, D))],
                      [_row_spec(TM, D)], [_sds((T, D), BF16)], (x, w_row), deps=deps)[0]


def _post_pre_norm(x, y, w_post, w_pre):
    def fn(xv, yv, wp, wn):
        x1 = xv + _rms(yv, wp)
        return x1, _rms(x1, wn)
    return _pure_call("post_pre_norm", fn, (T // TM,),
                      [_row_spec(TM, D), _row_spec(TM, D), _full_spec((1, D)), _full_spec((1, D))],
                      [_row_spec(TM, D), _row_spec(TM, D)],
                      [_sds((T, D), F32), _sds((T, D), BF16)], (x, y, w_post, w_pre))


def _post_pre_norm_bwd(y, x1, w_post, w_pre, dx1_in, dh1, deps=()):
    def fn(yv, x1v, wp, wn, dx1v, dh1v):
        _, vjp_pre = jax.vjp(_rms, x1v, wn)
        dx1_h, dwn = vjp_pre(dh1v)
        dx1 = dx1v + dx1_h
        _, vjp_post = jax.vjp(_rms, yv, wp)
        dy, dwp = vjp_post(dx1)
        return dx1, dy, dwp, dwn
    return _pure_call("post_pre_norm_bwd", fn, (T // TM,),
                      [_row_spec(TM, D), _row_spec(TM, D), _full_spec((1, D)), _full_spec((1, D)),
                       _row_spec(TM, D), _row_spec(TM, D)],
                      [_row_spec(TM, D), _row_spec(TM, D), _full_spec((1, D)), _full_spec((1, D))],
                      [_sds((T, D), F32), _sds((T, D), BF16), _sds((1, D), F32), _sds((1, D), F32)],
                      (y, x1, w_post, w_pre, dx1_in, dh1), n_acc=2, deps=deps)


def _final_loss(x1, y, w_post, target):
    def fn(x1v, yv, wp, tv):
        r, vjp = jax.vjp(_rms, yv, wp)
        err = x1v + r - tv
        part = 0.5 * jnp.sum(jnp.mean(err * err, axis=-1, keepdims=True), axis=0, keepdims=True)
        dx2 = err * (1.0 / D)
        dy, dwp = vjp(dx2)
        return dx2, dy, jnp.broadcast_to(part, (1, 128)), dwp
    return _pure_call("final_loss", fn, (T // TM,),
                      [_row_spec(TM, D), _row_spec(TM, D), _full_spec((1, D)), _row_spec(TM, D)],
                      [_row_spec(TM, D), _row_spec(TM, D), _full_spec((1, 128)), _full_spec((1, D))],
                      [_sds((T, D), F32), _sds((T, D), BF16), _sds((1, 128), F32), _sds((1, D), F32)],
                      (x1, y, w_post, target), n_acc=2)


def _pre_norm_bwd(x, w_row, dh, dx_res, deps=()):
    def fn(xv, w, dhv, dxv):
        _, vjp = jax.vjp(_rms, xv, w)
        dx, dw = vjp(dhv)
        return dxv + dx, dw
    return _pure_call("pre_norm_bwd", fn, (T // TM,),
                      [_row_spec(TM, D), _full_spec((1, D)), _row_spec(TM, D), _row_spec(TM, D)],
                      [_row_spec(TM, D), _full_spec((1, D))],
                      [_sds((T, D), F32), _sds((1, D), F32)], (x, w_row, dh, dx_res), n_acc=1, deps=deps)


RA = 256
RB = 512


def _col_spec(rows, col_of):
    return pl.BlockSpec((rows, HD), lambda h, r, col_of=col_of: (r, col_of(h)))


def _hgrn2_fwd(z, l0, l1, onorm):
    nb = T // RA

    def body(q_ref, f_ref, v_ref, g_ref, l0_ref, l1_ref, on_ref, cat_ref, sst_ref, st_scr):
        @pl.when(pl.program_id(1) == 0)
        def _():
            st_scr[...] = jnp.zeros_like(st_scr)

        st = st_scr[...]
        sst_ref[...] = st
        out, st_new = _hgrn2_rows(q_ref[...], f_ref[...], v_ref[...], g_ref[...], st,
                                  l0_ref[...], l1_ref[...], on_ref[...])
        cat_ref[...] = out.astype(cat_ref.dtype)
        st_scr[...] = st_new

    vec = pl.BlockSpec((1, HD), lambda h, r: (0, h))
    return pl.pallas_call(
        body, name="hgrn2_fwd", grid=(A_HEADS, nb),
        in_specs=[_col_spec(RA, lambda h: h), _col_spec(RA, lambda h: 8 + h),
                  _col_spec(RA, lambda h: 16 + h), _col_spec(RA, lambda h: 24 + h),
                  vec, vec, _full_spec((1, HD))],
        out_specs=[_col_spec(RA, lambda h: h),
                   pl.BlockSpec((None, None, HD, HD), lambda h, r: (h, r, 0, 0))],
        out_shape=[_sds((T, 2 * A_HEADS * HD), BF16), _sds((A_HEADS, nb, HD, HD), F32)],
        scratch_shapes=[pltpu.VMEM((HD, HD), F32)],
        compiler_params=_params(2))(z, z, z, z, l0, l1, onorm)


def _hgrn2_bwd(z, l0, l1, onorm, sst, dcat, deps=()):
    nb = T // RA
    deps = _live(deps)

    def body(q_ref, f_ref, v_ref, g_ref, l0_ref, l1_ref, on_ref, sst_ref, dcat_ref,
             dq_ref, df_ref, dv_ref, dg_ref, dl0_ref, dl1_ref, don_ref, ds_scr):
        h, r = pl.program_id(0), pl.program_id(1)

        @pl.when(r == 0)
        def _():
            ds_scr[...] = jnp.zeros_like(ds_scr)

        _, vjp = jax.vjp(_hgrn2_rows, q_ref[...], f_ref[...], v_ref[...], g_ref[...], sst_ref[...],
                         l0_ref[...], l1_ref[...], on_ref[...])
        dq, dzf, dv, dga, dst, dl0, dl1, don = vjp((dcat_ref[...], ds_scr[...]))
        dq_ref[...] = dq.astype(dq_ref.dtype)
        df_ref[...] = dzf.astype(df_ref.dtype)
        dv_ref[...] = dv.astype(dv_ref.dtype)
        dg_ref[...] = dga.astype(dg_ref.dtype)
        ds_scr[...] = dst

        @pl.when(r == 0)
        def _():
            dl0_ref[...] = dl0
            dl1_ref[...] = dl1

        @pl.when(r > 0)
        def _():
            dl0_ref[...] += dl0
            dl1_ref[...] += dl1

        first = jnp.logical_and(h == 0, r == 0)

        @pl.when(first)
        def _():
            don_ref[...] = don

        @pl.when(jnp.logical_not(first))
        def _():
            don_ref[...] += don

    def rev(col_of):
        return pl.BlockSpec((RA, HD), lambda h, r, col_of=col_of: (nb - 1 - r, col_of(h)))

    vec = pl.BlockSpec((1, HD), lambda h, r: (0, h))
    grad = _sds((T, A_HEADS * HD), BF16)
    return pl.pallas_call(
        _skip_deps(body, 9, len(deps)), name="hgrn2_bwd", grid=(A_HEADS, nb),
        in_specs=[rev(lambda h: h), rev(lambda h: 8 + h), rev(lambda h: 16 + h), rev(lambda h: 24 + h),
                  vec, vec, _full_spec((1, HD)),
                  pl.BlockSpec((None, None, HD, HD), lambda h, r: (h, nb - 1 - r, 0, 0)),
                  rev(lambda h: h)] + [ANY_SPEC] * len(deps),
        out_specs=[rev(lambda h: h)] * 4 + [vec, vec, _full_spec((1, HD))],
        out_shape=[grad] * 4 + [_sds((1, A_HEADS * HD), F32)] * 2 + [_sds((1, HD), F32)],
        scratch_shapes=[pltpu.VMEM((HD, HD), F32)],
        compiler_params=_params(2))(z, z, z, z, l0, l1, onorm, sst, dcat, *deps)


def _gmlp_specs():
    vec = pl.BlockSpec((1, HD), lambda g, r: (0, g))
    ws = pl.BlockSpec((None, B_CHUNK, B_CHUNK), lambda g, r: (g, 0, 0))
    bias = pl.BlockSpec((None, B_CHUNK, 1), lambda g, r: (g, 0, 0))
    return vec, ws, bias


def _gmlp_fwd(z, cat, lnw, lnb, ws, bias):
    vec, ws_spec, bias_spec = _gmlp_specs()

    def body(u_ref, v_ref, g_ref, lnw_ref, lnb_ref, ws_ref, bias_ref, cat_in_ref, cat_ref):
        del cat_in_ref
        out = _gmlp_rows(u_ref[...], v_ref[...], g_ref[...], lnw_ref[...], lnb_ref[...],
                         ws_ref[...], bias_ref[...])
        cat_ref[...] = out.astype(cat_ref.dtype)

    return pl.pallas_call(
        body, name="gmlp_fwd", grid=(B_GROUPS, T // RB),
        in_specs=[_col_spec(RB, lambda g: 32 + g), _col_spec(RB, lambda g: 40 + g),
                  _col_spec(RB, lambda g: 48 + g), vec, vec, ws_spec, bias_spec,
                  pl.BlockSpec(memory_space=pl.ANY)],
        out_specs=_col_spec(RB, lambda g: A_HEADS + g),
        out_shape=_sds(cat.shape, cat.dtype), input_output_aliases={7: 0},
        compiler_params=_params(2))(z, z, z, lnw, lnb, ws, bias, cat)


def _gmlp_bwd(z, lnw, lnb, ws, bias, dcat):
    vec, ws_spec, bias_spec = _gmlp_specs()

    def fn(u, vb, gb, w, b, wsv, bv, dout):
        _, vjp = jax.vjp(_gmlp_rows, u, vb, gb, w, b, wsv, bv)
        return vjp(dout)

    def body(*refs):
        ins, outs = refs[:8], refs[8:]
        res = fn(*[r[...] for r in ins])
        for o, r in zip(outs[:3], res[:3]):
            o[...] = r.astype(o.dtype)
        first = pl.program_id(1) == 0
        for o, r in zip(outs[3:], res[3:]):
            @pl.when(first)
            def _(o=o, r=r):
                o[...] = r

            @pl.when(jnp.logical_not(first))
            def _(o=o, r=r):
                o[...] += r

    grad = _sds((T, B_GROUPS * HD), BF16)
    row_out = pl.BlockSpec((RB, HD), lambda g, r: (r, g))
    return pl.pallas_call(
        body, name="gmlp_bwd", grid=(B_GROUPS, T // RB),
        in_specs=[_col_spec(RB, lambda g: 32 + g), _col_spec(RB, lambda g: 40 + g),
                  _col_spec(RB, lambda g: 48 + g), vec, vec, ws_spec, bias_spec,
                  _col_spec(RB, lambda g: A_HEADS + g)],
        out_specs=[row_out] * 3 + [vec, vec, ws_spec, bias_spec],
        out_shape=[grad] * 3 + [_sds((1, B_GROUPS * HD), F32)] * 2
        + [_sds((B_GROUPS, B_CHUNK, B_CHUNK), F32), _sds((B_GROUPS, B_CHUNK, 1), F32)],
        compiler_params=_params(2))(z, z, z, lnw, lnb, ws, bias, dcat)


def _mla_pre(z1, qn, kvn, cos_t, sin_t):
    def fn(cq, ckv, kpe, cs, sn, wq, wkv):
        return _rms(cq, wq), _rms(ckv, wkv), _rope(kpe, cs, sn)
    return _pure_call("mla_pre", fn, (T // TM,),
                      [_row_spec(TM, C_RANK, 4), _row_spec(TM, C_RANK, 5), _row_spec(TM, HD, 24),
                       _row_spec(TM, HD), _row_spec(TM, HD),
                       _full_spec((1, C_RANK)), _full_spec((1, C_RANK))],
                      [_row_spec(TM, C_RANK), _row_spec(TM, C_RANK), _row_spec(TM, HD)],
                      [_sds((T, C_RANK), BF16), _sds((T, C_RANK), BF16), _sds((T, HD), BF16)],
                      (z1, z1, z1, cos_t, sin_t, qn, kvn))


def _mla_pre_bwd(z1, qn, kvn, cos_t, sin_t, dcqn, dckvn, dkp, deps=()):
    def fn(cq, ckv, cs, sn, wq, wkv, g_q, g_kv, g_kp):
        _, vjp_q = jax.vjp(_rms, cq, wq)
        dcq, dwq = vjp_q(g_q)
        _, vjp_kv = jax.vjp(_rms, ckv, wkv)
        dckv, dwkv = vjp_kv(g_kv)
        return dcq, dckv, _rope_transpose(g_kp, cs, sn), dwq, dwkv
    return _pure_call("mla_pre_bwd", fn, (T // TM,),
                      [_row_spec(TM, C_RANK, 4), _row_spec(TM, C_RANK, 5),
                       _row_spec(TM, HD), _row_spec(TM, HD),
                       _full_spec((1, C_RANK)), _full_spec((1, C_RANK)),
                       _row_spec(TM, C_RANK), _row_spec(TM, C_RANK), _row_spec(TM, HD)],
                      [_row_spec(TM, C_RANK), _row_spec(TM, C_RANK), _row_spec(TM, HD),
                       _full_spec((1, C_RANK)), _full_spec((1, C_RANK))],
                      [_sds((T, C_RANK), BF16), _sds((T, C_RANK), BF16), _sds((T, HD), BF16),
                       _sds((1, C_RANK), F32), _sds((1, C_RANK), F32)],
                      (z1, z1, cos_t, sin_t, qn, kvn, dcqn, dckvn, dkp), n_acc=2, deps=deps)


def _gate_out(o, z1):
    def fn(ov, gate):
        return ov * _silu(gate)
    return _pure_call("gate_out", fn, (T // TM,), [_row_spec(TM, D), _row_spec(TM, D, 0)],
                      [_row_spec(TM, D)], [_sds((T, D), BF16)], (o, z1))[0]


def _gate_out_bwd(o, z1, dog, deps=()):
    def fn(ov, gate, g):
        _, vjp = jax.vjp(lambda a, b: a * _silu(b), ov, gate)
        return vjp(g)
    return _pure_call("gate_out_bwd", fn, (T // TM,),
                      [_row_spec(TM, D), _row_spec(TM, D, 0), _row_spec(TM, D)],
                      [_row_spec(TM, D), _row_spec(TM, D)],
                      [_sds((T, D), F32), _sds((T, D), BF16)], (o, z1, dog), deps=deps)


TQ = 256


def _att_keys(kn_ref, kp_ref, k_scr):
    @pl.when(pl.program_id(1) == 0)
    def _():
        k_scr[:, 0:C_NOPE] = kn_ref[...]
        k_scr[:, C_NOPE:QP] = kp_ref[...]


def _att_scores(q_ref, cos_ref, sin_ref, k_scr, n):
    keys = (n + 1) * TQ
    q = q_ref[...]
    qr = jnp.concatenate([q[:, :C_NOPE], _rope(q[:, C_NOPE:], cos_ref[...], sin_ref[...])], axis=1).astype(BF16)
    return qr, _raw_nt(qr, k_scr[0:keys, :]) * ATT_SCALE


def _causal(x, n, fill):
    row = lax.broadcasted_iota(jnp.int32, (TQ, TQ), 0)
    col = lax.broadcasted_iota(jnp.int32, (TQ, TQ), 1)
    diag = jnp.where(col <= row, x[:, n * TQ:], fill)
    return diag if n == 0 else jnp.concatenate([x[:, :n * TQ], diag], axis=1)


def _per_query_block(fn):
    for n in range(T // TQ):
        pl.when(pl.program_id(1) == n)(functools.partial(fn, n))


def _att_in_specs():
    return [pl.BlockSpec((TQ, QP), lambda h, i: (i, h)),
            pl.BlockSpec((TQ, HD), lambda h, i: (i, 0)),
            pl.BlockSpec((TQ, HD), lambda h, i: (i, 0)),
            pl.BlockSpec((T, C_NOPE), lambda h, i: (0, 2 * h)),
            pl.BlockSpec((T, HD), lambda h, i: (0, 0)),
            pl.BlockSpec((T, C_V), lambda h, i: (0, 2 * h + 1))]


def _attention_fwd(q, cos_t, sin_t, kv, kp):
    def body(q_ref, cos_ref, sin_ref, kn_ref, kp_ref, v_ref, o_ref, lse_ref, k_scr):
        _att_keys(kn_ref, kp_ref, k_scr)

        def block(n):
            _, s = _att_scores(q_ref, cos_ref, sin_ref, k_scr, n)
            s = _causal(s, n, jnp.finfo(F32).min)
            m = jnp.max(s, axis=-1, keepdims=True)
            p = jnp.exp(s - m)
            l = jnp.sum(p, axis=-1, keepdims=True)
            o_ref[...] = _raw_nn(p, v_ref[0:(n + 1) * TQ, :]) / l
            lse_ref[...] = m + jnp.log(l)

        _per_query_block(block)

    return pl.pallas_call(
        body, name="attention_fwd", grid=(C_HEADS, T // TQ), in_specs=_att_in_specs(),
        out_specs=[pl.BlockSpec((TQ, C_V), lambda h, i: (i, h)),
                   pl.BlockSpec((None, TQ, 1), lambda h, i: (h, i, 0))],
        out_shape=[_sds((T, C_HEADS * C_V), F32), _sds((C_HEADS, T, 1), F32)],
        scratch_shapes=[pltpu.VMEM((T, QP), BF16)],
        compiler_params=_params(2))(q, cos_t, sin_t, kv, kp, kv)


def _attention_bwd(q, cos_t, sin_t, kv, kp, o, lse, do):
    nq = T // TQ

    def body(q_ref, cos_ref, sin_ref, kn_ref, kp_ref, v_ref, o_ref, lse_ref, do_ref,
             dq_ref, dkv_ref, dkp_ref, k_scr, dk_scr, dv_scr):
        h, i = pl.program_id(0), pl.program_id(1)
        _att_keys(kn_ref, kp_ref, k_scr)

        @pl.when(i == 0)
        def _():
            dv_scr[...] = jnp.zeros_like(dv_scr)
            dk_scr[...] = jnp.zeros_like(dk_scr)

        def block(n):
            keys = (n + 1) * TQ
            qr, s = _att_scores(q_ref, cos_ref, sin_ref, k_scr, n)
            p = _causal(jnp.exp(s - lse_ref[...]), n, 0.0)
            dov = do_ref[...]
            delta = jnp.sum(dov * o_ref[...], axis=-1, keepdims=True)
            dp = _raw_nt(dov, v_ref[0:keys, :])
            ds = p * (dp - delta) * ATT_SCALE
            dq = _raw_nn(ds, k_scr[0:keys, :])
            dq_ref[...] = jnp.concatenate(
                [dq[:, :C_NOPE], _rope_transpose(dq[:, C_NOPE:], cos_ref[...], sin_ref[...])], axis=1).astype(dq_ref.dtype)
            dv_scr[0:keys, :] += _raw_tn(p, dov)
            dk_scr[0:keys, :] += _raw_tn(ds, qr)

        _per_query_block(block)

        @pl.when(i == nq - 1)
        def _():
            dkv_ref[...] = jnp.concatenate([dk_scr[:, 0:C_NOPE], dv_scr[...]], axis=1).astype(dkv_ref.dtype)

        @pl.when(jnp.logical_and(i == nq - 1, h == 0))
        def _():
            dkp_ref[...] = dk_scr[:, C_NOPE:QP]

        @pl.when(jnp.logical_and(i == nq - 1, h > 0))
        def _():
            dkp_ref[...] += dk_scr[:, C_NOPE:QP]

    return pl.pallas_call(
        body, name="attention_bwd", grid=(C_HEADS, nq),
        in_specs=_att_in_specs() + [pl.BlockSpec((TQ, C_V), lambda h, i: (i, h)),
                                    pl.BlockSpec((None, TQ, 1), lambda h, i: (h, i, 0)),
                                    pl.BlockSpec((TQ, C_V), lambda h, i: (i, h))],
        out_specs=[pl.BlockSpec((TQ, QP), lambda h, i: (i, h)),
                   pl.BlockSpec((T, C_NOPE + C_V), lambda h, i: (0, h)),
                   _full_spec((T, HD))],
        out_shape=[_sds((T, C_HEADS * QP), BF16), _sds((T, C_HEADS * (C_NOPE + C_V)), BF16),
                   _sds((T, HD), F32)],
        scratch_shapes=[pltpu.VMEM((T, QP), BF16), pltpu.VMEM((T, QP), F32), pltpu.VMEM((T, C_V), F32)],
        compiler_params=_params(2))(q, cos_t, sin_t, kv, kp, kv, o, lse, do)


def _adamw_math(w, g, m, v):
    m = ADAM_B1 * m + (1.0 - ADAM_B1) * g
    v = ADAM_B2 * v + (1.0 - ADAM_B2) * (g * g)
    m_hat = m / (1.0 - ADAM_B1 ** ADAM_STEP)
    v_hat = v / (1.0 - ADAM_B2 ** ADAM_STEP)
    delta = -ADAM_LR * (m_hat / (jnp.sqrt(v_hat) + ADAM_EPS) + ADAM_WD * w)
    return delta, m, v


def _adamw(name, parts, w, m, v, tr):
    rows, cols = w.shape

    def fn(*vals):
        pvs, (wv, mv, vv) = vals[:len(parts)], vals[len(parts):]
        g = None
        for pv in pvs:
            for d in range(pv.shape[0]):
                term = pv[d].astype(F32)
                g = term if g is None else g + term
        return (g,) + _adamw_math(wv, g, mv, vv)

    blk = pl.BlockSpec((tr, cols), lambda i: (i, 0))
    part_specs = [pl.BlockSpec((n, tr, cols), lambda i: (0, i, 0)) for _, n in parts]
    return _pure_call(name, fn, (rows // tr,), part_specs + [blk, blk, blk],
                      [blk] * 4, [_sds((rows, cols), F32)] * 4, tuple(p for p, _ in parts) + (w, m, v))


SMALL_PARAM_SHAPES = ((2, D), (2, D), (2, A_HEADS * HD), (1, HD), (1, B_GROUPS * HD), (1, B_GROUPS * HD),
                      (B_GROUPS, B_CHUNK, B_CHUNK), (B_GROUPS, B_CHUNK))
SMALL_PIECES = ((0, 0, 0, 0), (0, 1, 1, 0), (1, 0, 1, 1), (1, 1, 1, 2), (2, 0, 2, 0), (2, 1, 2, 1),
                (3, 0, 3, 8), (4, 0, 2, 2), (5, 0, 2, 3))


def _small_rows(dnpre1, dnpost0, dnpost1, dl0, dl1, donorm, dlnw, dlnb, dws, dbias, dqn, dkvn, loss_part):
    return [jnp.concatenate([dnpre1, dnpost0, dnpost1], axis=0),
            jnp.concatenate([dl0, dl1, dlnw, dlnb], axis=0),
            jnp.concatenate([dbias.reshape(B_GROUPS, B_CHUNK), donorm, loss_part], axis=0),
            dws,
            jnp.concatenate([dqn, dkvn], axis=0)]


def _adamw_small(late_all, early_all, wmv):
    n_in = 6 + 3 * len(wmv)

    def body(*refs):
        gathered, params, outs = refs[:6], refs[6:n_in], refs[n_in:]

        def total(ref):
            s = ref[0]
            for d in range(1, N_DEV):
                s = s + ref[d]
            return s

        g_late, g2048, g1024, g128, g_ws, g512 = [total(r) for r in gathered]
        arrays = (g_late, g2048, g1024, g128)

        def update(p, rows, g):
            w_ref, m_ref, v_ref = params[3 * p:3 * p + 3]
            delta, m, v = _adamw_math(w_ref[rows], g, m_ref[rows], v_ref[rows])
            for out, val in zip(outs[4 * p:4 * p + 4], (g, delta, m, v)):
                out[rows] = val

        for p, row, arr, arr_row in SMALL_PIECES:
            update(p, pl.ds(row, 1), arrays[arr][arr_row:arr_row + 1])
        update(6, slice(None), g_ws)
        update(7, slice(None), g128[0:B_GROUPS])
        outs[32][...] = g128[B_GROUPS + 1:B_GROUPS + 2]
        outs[33][...] = g512

    vmem = pl.BlockSpec(memory_space=pltpu.VMEM)
    flat = [a for t in wmv for a in t]
    out_shape = [_sds(s, F32) for s in SMALL_PARAM_SHAPES for _ in range(4)] + [_sds((1, 128), F32), _sds((2, C_RANK), F32)]
    res = pl.pallas_call(body, name="adamw_small", in_specs=[vmem] * n_in, out_specs=[vmem] * len(out_shape),
                         out_shape=out_shape,
                         compiler_params=pltpu.CompilerParams(vmem_limit_bytes=VMEM_LIMIT_V7X))(late_all, *early_all, *flat)
    return [res[4 * p:4 * p + 4] for p in range(8)], res[32], res[33]


def _exchange(name, arrs, gather, deps=()):
    n = len(arrs)
    deps = _live(deps)

    def body(*refs):
        ins, outs = refs[:n], refs[n + len(deps):2 * n + len(deps)]
        send_sems, recv_sems, local_sems = refs[2 * n + len(deps):]
        x, y, c = lax.axis_index("x"), lax.axis_index("y"), lax.axis_index("c")
        me = 4 * x + 2 * y + c

        def peer(k):
            return (x ^ (k >> 2), y ^ ((k >> 1) & 1), c ^ (k & 1))

        def copy(a, k):
            src = ins[a] if gather else ins[a].at[me ^ k]
            return pltpu.make_async_remote_copy(
                src_ref=src, dst_ref=outs[a].at[me], send_sem=send_sems.at[a, k - 1],
                recv_sem=recv_sems.at[a, k - 1], device_id=peer(k), device_id_type=MESH_ID)

        def arrival(a, k):
            src = ins[a] if gather else ins[a].at[me]
            return pltpu.make_async_remote_copy(
                src_ref=src, dst_ref=outs[a].at[me ^ k], send_sem=send_sems.at[a, k - 1],
                recv_sem=recv_sems.at[a, k - 1], device_id=peer(k), device_id_type=MESH_ID)

        own = [pltpu.make_async_copy(ins[a] if gather else ins[a].at[me], outs[a].at[me], local_sems.at[a])
               for a in range(n)]
        for cp in own:
            cp.start()
        for k in range(1, N_DEV):
            for a in range(n):
                copy(a, k).start()
        for k in range(1, N_DEV):
            for a in range(n):
                arrival(a, k).wait_recv()
        for k in range(1, N_DEV):
            for a in range(n):
                copy(a, k).wait_send()
        for cp in own:
            cp.wait()

    any_spec = pl.BlockSpec(memory_space=pl.ANY)
    out_shape = [_sds((N_DEV,) + a.shape if gather else a.shape, a.dtype) for a in arrs]
    return pl.pallas_call(
        body, name=name, in_specs=[any_spec] * (n + len(deps)), out_specs=[any_spec] * n, out_shape=out_shape,
        scratch_shapes=[pltpu.SemaphoreType.DMA((n, N_DEV - 1)), pltpu.SemaphoreType.DMA((n, N_DEV - 1)),
                        pltpu.SemaphoreType.DMA((n,))],
        compiler_params=pltpu.CompilerParams(has_side_effects=True))(*arrs, *deps)


HBM_SPEC = pl.BlockSpec(memory_space=pltpu.HBM)
SEM_SPEC = pl.BlockSpec(memory_space=pltpu.SEMAPHORE)
DATAFLOW = pltpu.SideEffectType.DATAFLOW_SIDE_EFFECTING


def _my_index():
    return 4 * lax.axis_index("x") + 2 * lax.axis_index("y") + lax.axis_index("c")


def _plan_copies(plan, refs, send_sems, recv_sems):
    x, y, c = lax.axis_index("x"), lax.axis_index("y"), lax.axis_index("c")
    return [pltpu.make_async_remote_copy(
        src_ref=src, dst_ref=dst, send_sem=send_sems.at[i], recv_sem=recv_sems.at[i],
        device_id=(x ^ (k >> 2), y ^ ((k >> 1) & 1), c ^ (k & 1)), device_id_type=MESH_ID)
        for i, (src, dst, k) in enumerate(plan(refs, 4 * x + 2 * y + c))]


def _split_call(name, bufs, waits=None, starts=None, deps=()):
    n = len(bufs)
    deps = _live(deps)
    n_wait = 2 if waits else 0

    def body(*refs):
        zones = refs[:n]
        if waits:
            for cp in _plan_copies(waits[2], zones, refs[n], refs[n + 1]):
                cp.wait_send()
                cp.wait_recv()
        if starts:
            first_out = n + n_wait + len(deps)
            for cp in _plan_copies(starts[0], zones, refs[first_out], refs[first_out + 1]):
                cp.start()
            refs[-1][...] = jnp.zeros_like(refs[-1])

    out_specs, out_shape = [], []
    if starts:
        sems = pltpu.SemaphoreType.DMA((starts[1],))
        out_specs, out_shape = [SEM_SPEC, SEM_SPEC], [sems, sems]
    out_specs += [HBM_SPEC] * n
    out_shape += [pltpu.HBM(b.shape, b.dtype) for b in bufs]
    if starts:
        out_specs.append(pl.BlockSpec(memory_space=pltpu.VMEM))
        out_shape.append(_sds((8, 128), F32))
    first_buf = 2 if starts else 0
    res = pl.pallas_call(
        body, name=name,
        in_specs=[HBM_SPEC] * n + [SEM_SPEC] * n_wait + [ANY_SPEC] * len(deps),
        out_specs=out_specs, out_shape=out_shape,
        input_output_aliases={i: first_buf + i for i in range(n)},
        compiler_params=pltpu.CompilerParams(has_side_effects=DATAFLOW),
    )(*[pltpu.with_memory_space_constraint(b, pltpu.HBM) for b in bufs], *(waits[:2] if waits else ()), *deps)
    out_bufs = list(res[first_buf:first_buf + n])
    return out_bufs, ((res[0], res[1]) if starts else None), (res[-1] if starts else None)


def _direct_plan(n, gather):
    def plan(refs, me):
        return [(refs[a] if gather else refs[a].at[me ^ k], refs[n + a].at[me], k)
                for k in range(1, N_DEV) for a in range(n)]
    return plan


def _own_slot_filled(a, gather):
    me = _my_index()
    if gather:
        return lax.dynamic_update_slice_in_dim(lax.empty((N_DEV,) + a.shape, a.dtype), a[None], me, 0)
    return lax.dynamic_update_slice_in_dim(lax.empty(a.shape, a.dtype), lax.dynamic_slice_in_dim(a, me, 1, 0), me, 0)


def _exchange_start(name, arrs, gather, deps=()):
    n = len(arrs)
    lands = [_own_slot_filled(a, gather) for a in arrs]
    plan = _direct_plan(n, gather)
    bufs, sems, token = _split_call(name, list(arrs) + lands, starts=(plan, n * (N_DEV - 1)), deps=deps)
    return (n, plan, sems, bufs, None), token


def _exchange_wait(name, handle, after):
    return _split_done(name, handle, after)


ICI_PEERS = (2, 4, 6)
SIBLING = 1


def _gather2_send(name, arrs, deps=()):
    n = len(arrs)
    lands = [_own_slot_filled(a, True) for a in arrs]

    def plan(refs, me_):
        return [(refs[a], refs[n + a].at[me_], k) for k in (SIBLING,) + ICI_PEERS for a in range(n)]

    bufs, sems, token = _split_call(name, list(arrs) + lands, starts=(plan, 4 * n), deps=deps)
    return (n, plan, sems, bufs, None), token


def _gather2_relay(name, handle, after):
    n, plan, sems, bufs, _ = handle

    def relay(refs, me_):
        return [(refs[n + a].at[me_ ^ k], refs[n + a].at[me_ ^ k], SIBLING) for k in ICI_PEERS for a in range(n)]

    bufs, sems2, token = _split_call(name, bufs, waits=(sems[0], sems[1], plan), starts=(relay, 3 * n), deps=[after])
    return (n, relay, sems2, bufs, None), token


def _split_done(name, handle, after, all_bufs=False):
    n, plan, sems, bufs, _ = handle
    bufs, _, _ = _split_call(name, bufs, waits=(sems[0], sems[1], plan), deps=[after])
    return bufs if all_bufs else bufs[n:]


def _scatter2_pair(name, stacks, deps=()):
    n = len(stacks)
    pairs = [lax.empty((4,) + s.shape[1:], s.dtype) for s in stacks]

    def plan(refs, me):
        return [(refs[a].at[(me ^ SIBLING) ^ (2 * j)], refs[n + a].at[j], SIBLING) for j in range(4) for a in range(n)]

    bufs, sems, token = _split_call(name, list(stacks) + pairs, starts=(plan, 4 * n), deps=deps)
    return (n, plan, sems, bufs, None), token


def _pair_add(name, stack, pair, me):
    _, rows, cols = stack.shape
    tr = rows // 8

    def body(me_ref, s_ref, p_ref, o_ref):
        del me_ref
        o_ref[...] = (s_ref[...].astype(F32) + p_ref[...].astype(F32)).astype(o_ref.dtype)

    grid_spec = pltpu.PrefetchScalarGridSpec(
        num_scalar_prefetch=1, grid=(4, rows // tr),
        in_specs=[pl.BlockSpec((None, tr, cols), lambda j, i, me_ref: (me_ref[0] ^ (2 * j), i, 0)),
                  pl.BlockSpec((None, tr, cols), lambda j, i, me_ref: (j, i, 0))],
        out_specs=pl.BlockSpec((None, tr, cols), lambda j, i, me_ref: (j, i, 0)))
    return pl.pallas_call(body, name=name, grid_spec=grid_spec, out_shape=_sds((4, rows, cols), stack.dtype),
                          compiler_params=_params(2))(me.reshape(1).astype(jnp.int32), stack, pair)


def _scatter2_send(name, chip_sums, deps=()):
    n = len(chip_sums)
    finals = [lax.empty((3,) + c.shape[1:], c.dtype) for c in chip_sums]

    def plan(refs, me):
        del me
        return [(refs[a].at[j], refs[n + a].at[j - 1], 2 * j) for j in range(1, 4) for a in range(n)]

    bufs, sems, token = _split_call(name, list(chip_sums) + finals, starts=(plan, 3 * n), deps=deps)
    return (n, plan, sems, bufs, None), token


def _pad_rope(p):
    z = jnp.zeros(p.shape[:-1] + (32,), p.dtype)
    return jnp.concatenate([p[..., :32], z, p[..., 32:], z], axis=-1)


def _unpad_rope(p):
    return jnp.concatenate([p[..., :32], p[..., 64:96]], axis=-1)


def _odd_in_layout(w):
    w = w.transpose(1, 0, 2).reshape(D, ODD_IN)
    cq, ckv, kpe, gate = w[:, :512], w[:, 512:1024], w[:, 1024:1088], w[:, 1088:]
    return jnp.concatenate([gate, cq, ckv, _pad_rope(kpe)], axis=1)


def _odd_in_unlayout(dw):
    gate, cq, ckv, kpe = dw[:, :2048], dw[:, 2048:2560], dw[:, 2560:3072], _unpad_rope(dw[:, 3072:])
    w = jnp.concatenate([cq, ckv, kpe, gate], axis=1)
    return w.reshape(D, N_DEV, ODD_IN // N_DEV).transpose(1, 0, 2)


def _qb_layout(w):
    w = w.transpose(1, 0, 2).reshape(C_RANK, C_HEADS, C_QK)
    w = jnp.concatenate([w[..., :C_NOPE], _pad_rope(w[..., C_NOPE:])], axis=-1)
    return w.reshape(C_RANK, C_HEADS * QP)


def _qb_unlayout(dw):
    dw = dw.reshape(C_RANK, C_HEADS, QP)
    dw = jnp.concatenate([dw[..., :C_NOPE], _unpad_rope(dw[..., C_NOPE:])], axis=-1)
    return dw.reshape(C_RANK, N_DEV, C_HEADS * C_QK // N_DEV).transpose(1, 0, 2)


def _rope_tables(positions):
    inv_freq = ROPE_THETA ** (-jnp.arange(0, C_ROPE, 2, dtype=F32) / C_ROPE)
    ang = positions.astype(F32)[0][:, None] * inv_freq
    cos, sin = jnp.cos(ang), jnp.sin(ang)
    z = jnp.zeros_like(cos)
    return jnp.concatenate([cos, z, cos, z], axis=1), jnp.concatenate([-sin, z, sin, z], axis=1)


def _forward_backward(x, cos_t, sin_t, target, norm_pre, norm_post, lb_logits, a_onorm, ln_w, ln_b,
                      b_ws, b_bias, get_w, put_g, put_small=None, start_dep=None):
    npre0, npre1 = norm_pre[0:1], norm_pre[1:2]
    npost0, npost1 = norm_post[0:1], norm_post[1:2]
    l0, l1 = lb_logits[0:1], lb_logits[1:2]
    bias_col = b_bias.reshape(B_GROUPS, B_CHUNK, 1)
    ws = b_ws.reshape(B_GROUPS, B_CHUNK, B_CHUNK)

    h0 = _pre_norm("pre_norm0", x, npre0, deps=[start_dep])
    w_ev_in = get_w("ev_in", h0)
    z0 = _mm_nn("ev_in", h0, w_ev_in, F32, 1024, 896)
    cat, sst = _hgrn2_fwd(z0, l0, l1, a_onorm)
    cat = _gmlp_fwd(z0, cat, ln_w, ln_b, ws, bias_col)
    w_ev_out = get_w("ev_out", cat)
    y0 = _mm_nn("ev_out", cat, w_ev_out, F32, 1024, 1024)
    x1, h1 = _post_pre_norm(x, y0, npost0, npre1)
    w_od_in, w_qb, w_kvb, q_norm, kv_norm = get_w("od_mid", h1)
    z1 = _mm_nn("od_in", h1, w_od_in[None], F32, 1024, 640)
    cqn, ckvn, kp = _mla_pre(z1, q_norm, kv_norm, cos_t, sin_t)
    q = _mm_nn("od_qb", cqn, w_qb[None], F32, 1024, 1024)
    kv = _mm_nn("od_kvb", ckvn, w_kvb, BF16, 1024, 512)
    o, lse = _attention_fwd(q, cos_t, sin_t, kv, kp)
    og = _gate_out(o, z1)
    w_od_out = get_w("od_out", og)
    y1 = _mm_nn("od_out", og, w_od_out, F32, 1024, 1024)
    dx2, dy1, loss_part, dnpost1 = _final_loss(x1, y1, npost1, target)

    g_od_out = _mm_tn("od_out_dw", og, dy1, 1, BF16, 1024, 1024)
    tok = put_g("od_out", [g_od_out.reshape(N_DEV, D // N_DEV, D)])
    dog = _mm_nt("od_out_dx", dy1, w_od_out, F32, 1024, 1024, deps=[tok])
    do, dgate = _gate_out_bwd(o, z1, dog)
    dq, dkv, dkp = _attention_bwd(q, cos_t, sin_t, kv, kp, o, lse, do)
    g_qb = _mm_tn("od_qb_dw", cqn, dq, 1, F32, 512, 1024)
    g_kvb = _mm_tn("od_kvb_dw", ckvn, dkv, N_DEV, BF16, 512, 512)
    tok = put_g("od_qkv", [_qb_unlayout(g_qb[0]).astype(BF16), g_kvb])
    dcqn = _mm_nt("od_qb_dx", dq, w_qb[None], F32, 1024, 512, deps=[tok])
    dckvn = _mm_nt("od_kvb_dx", dkv, w_kvb, F32, 1024, 512)
    dcq, dckv, dkpe, dqn, dkvn = _mla_pre_bwd(z1, q_norm, kv_norm, cos_t, sin_t, dcqn, dckvn, dkp)
    dz1 = jnp.concatenate([dgate, dcq, dckv, dkpe], axis=1)
    g_od_in = _mm_tn("od_in_dw", h1, dz1, 1, F32, 1024, 640)
    tok = put_g("od_in", [_odd_in_unlayout(g_od_in[0]).astype(BF16)])
    dh1 = _mm_nt("od_in_dx", dz1, w_od_in[None], F32, 1024, 1024, deps=[tok])
    dx1, dy0, dnpost0, dnpre1 = _post_pre_norm_bwd(y0, x1, npost0, npre1, dx2, dh1)

    g_ev_out = _mm_tn("ev_out_dw", cat, dy0, 1, BF16, 1024, 1024)
    tok = put_g("ev_out", [g_ev_out.reshape(N_DEV, D // N_DEV, D)])
    dcat = _mm_nt("ev_out_dx", dy0, w_ev_out, F32, 1024, 1024, deps=[tok])
    dqa, dfa, dia, dga, dl0, dl1, donorm = _hgrn2_bwd(z0, l0, l1, a_onorm, sst, dcat)
    dub, dvb, dgb, dlnw, dlnb, dws, dbias = _gmlp_bwd(z0, ln_w, ln_b, ws, bias_col, dcat)
    dz0 = jnp.concatenate([dqa, dfa, dia, dga, dub, dvb, dgb], axis=1)
    early = _small_rows(dnpre1, dnpost0, dnpost1, dl0, dl1, donorm, dlnw, dlnb, dws, dbias, dqn, dkvn, loss_part)
    tok = put_small(early) if put_small else None
    g_ev_in = _mm_tn("ev_in_dw", h0, dz0, N_DEV, BF16, 1024, 896, deps=[tok])
    tok = put_g("ev_in", [g_ev_in])
    dh0 = _mm_nt("ev_in_dx", dz0, w_ev_in, F32, 1024, 1024, deps=[tok])
    grad_x, dnpre0 = _pre_norm_bwd(x, npre0, dh0, dx1)
    return grad_x, early, dnpre0


def kernel(x, positions, norm_pre, norm_post, ev_w_in, ev_lb_logits, ev_a_onorm, ev_b_ln_w, ev_b_ln_b, ev_b_ws, ev_b_bias, ev_w_out, od_w_in, od_q_norm, od_w_qb, od_kv_norm, od_w_kvb, od_w_out, loss_target, m_norm_pre, m_norm_post, m_ev_w_in, m_ev_lb_logits, m_ev_a_onorm, m_ev_b_ln_w, m_ev_b_ln_b, m_ev_b_ws, m_ev_b_bias, m_ev_w_out, m_od_w_in, m_od_q_norm, m_od_w_qb, m_od_kv_norm, m_od_w_kvb, m_od_w_out, v_norm_pre, v_norm_post, v_ev_w_in, v_ev_lb_logits, v_ev_a_onorm, v_ev_b_ln_w, v_ev_b_ln_b, v_ev_b_ws, v_ev_b_bias, v_ev_w_out, v_od_w_in, v_od_q_norm, v_od_w_qb, v_od_kv_norm, v_od_w_kvb, v_od_w_out):
    me = 4 * lax.axis_index("x") + 2 * lax.axis_index("y") + lax.axis_index("c")
    bf = lambda w: w[0].astype(BF16)

    norms = jnp.pad(jnp.concatenate([od_q_norm, od_kv_norm], axis=1), ((0, 7), (0, 0)))
    first_h, tok = _gather2_send("gather_ev_in", [bf(ev_w_in)])
    rest_h, tok = _gather2_send("gather_rest", [bf(ev_w_out), bf(od_w_in), bf(od_w_qb), bf(od_w_kvb), norms,
                                                bf(od_w_out)], deps=[tok])
    first_h, tok = _gather2_relay("relay_ev_in", first_h, tok)
    rest = []

    def get_w(group, after):
        if group == "ev_in":
            return _split_done("arrived_ev_in", first_h, after)[0]
        if not rest:
            relayed, token = _gather2_relay("relay_rest", rest_h, after)
            rest.extend(_split_done("arrived_rest", relayed, token))
        w_ev_out, w_od_in, w_qb, w_kvb, norms_all, w_od_out = rest
        if group == "ev_out":
            return w_ev_out.reshape(1, D, D)
        if group == "od_out":
            return w_od_out.reshape(1, D, D)
        return (_odd_in_layout(w_od_in), _qb_layout(w_qb), w_kvb,
                norms_all[:, 0, :64].reshape(1, C_RANK), norms_all[:, 0, 64:].reshape(1, C_RANK))

    scatters = {}

    def put_g(group, grads):
        if group == "ev_in":
            paired, token = _scatter2_pair("pair_ev_in", grads)
            n = len(grads)
            bufs = _split_done("paired_ev_in", paired, token, all_bufs=True)
            chip_sums = [_pair_add("pair_add_ev_in", bufs[a], bufs[n + a], me) for a in range(n)]
            scatters[group], token = _scatter2_send("scatter_ev_in", chip_sums)
        else:
            scatters[group], token = _exchange_start("scatter_" + group, grads, False)
        return token

    def put_small(early):
        scatters["small"], token = _exchange_start("gather_small_early", early, True)
        return token

    cos_t, sin_t = _rope_tables(positions)
    grad_x, _, dnpre0 = _forward_backward(
        x[0], cos_t, sin_t, loss_target[0], norm_pre, norm_post, ev_lb_logits, ev_a_onorm, ev_b_ln_w,
        ev_b_ln_b, ev_b_ws, ev_b_bias, get_w, put_g, put_small, start_dep=tok)

    big_w = {"ev_w_in": ev_w_in, "ev_w_out": ev_w_out, "od_w_in": od_w_in, "od_w_qb": od_w_qb,
             "od_w_kvb": od_w_kvb, "od_w_out": od_w_out}
    big_m = {"ev_w_in": m_ev_w_in, "ev_w_out": m_ev_w_out, "od_w_in": m_od_w_in, "od_w_qb": m_od_w_qb,
             "od_w_kvb": m_od_w_kvb, "od_w_out": m_od_w_out}
    big_v = {"ev_w_in": v_ev_w_in, "ev_w_out": v_ev_w_out, "od_w_in": v_od_w_in, "od_w_qb": v_od_w_qb,
             "od_w_kvb": v_od_w_kvb, "od_w_out": v_od_w_out}
    big_out = {}
    after = grad_x
    for group, names in (("od_out", ["od_w_out"]), ("od_qkv", ["od_w_qb", "od_w_kvb"]), ("od_in", ["od_w_in"]),
                         ("ev_out", ["ev_w_out"])):
        parts = _exchange_wait("summed_" + group, scatters[group], after)
        for nm, p in zip(names, parts):
            w = big_w[nm][0]
            big_out[nm] = [r[None] for r in _adamw("adamw_" + nm, [(p, N_DEV)], w, big_m[nm][0], big_v[nm][0],
                                                   w.shape[0] // 8)]
            after = big_out[nm][0]

    late_all = _exchange("gather_small_late", [dnpre0], gather=True, deps=[after])[0]
    early_all = _exchange_wait("arrived_small_early", scatters["small"], late_all)

    small_w = (norm_pre, norm_post, ev_lb_logits, ev_a_onorm, ev_b_ln_w, ev_b_ln_b, ev_b_ws, ev_b_bias)
    small_m = (m_norm_pre, m_norm_post, m_ev_lb_logits, m_ev_a_onorm, m_ev_b_ln_w, m_ev_b_ln_b, m_ev_b_ws, m_ev_b_bias)
    small_v = (v_norm_pre, v_norm_post, v_ev_lb_logits, v_ev_a_onorm, v_ev_b_ln_w, v_ev_b_ln_b, v_ev_b_ws, v_ev_b_bias)
    wmv = [tuple(a.reshape(s) for a in t) for s, t in zip(SMALL_PARAM_SHAPES, zip(small_w, small_m, small_v))]
    small_res, loss_row, g_norm_rows = _adamw_small(late_all, early_all, wmv)
    small_out = [[r.reshape(w.shape) for r in four] for four, w in zip(small_res, small_w)]
    loss = loss_row[0, 0]

    g_norms = jnp.concatenate([lax.dynamic_slice(g_norm_rows, (0, 64 * me), (1, 64)),
                               lax.dynamic_slice(g_norm_rows, (1, 64 * me), (1, 64))], axis=1)
    res_n = _adamw("adamw_norms", [(g_norms[None], 1)],
                   jnp.concatenate([od_q_norm, od_kv_norm], axis=1),
                   jnp.concatenate([m_od_q_norm, m_od_kv_norm], axis=1),
                   jnp.concatenate([v_od_q_norm, v_od_kv_norm], axis=1), 1)
    qn_out = [r[:, :64] for r in res_n]
    kvn_out = [r[:, 64:] for r in res_n]

    chip_sum, from_peers = _split_done("summed_ev_in", scatters["ev_in"], loss_row, all_bufs=True)
    w = ev_w_in[0]
    big_out["ev_w_in"] = [r[None] for r in _adamw("adamw_ev_w_in", [(chip_sum, 1), (from_peers, 3)], w, m_ev_w_in[0],
                                                  v_ev_w_in[0], w.shape[0] // 8)]

    order = ("norm_pre", "norm_post", "ev_w_in", "ev_lb_logits", "ev_a_onorm", "ev_b_ln_w", "ev_b_ln_b",
             "ev_b_ws", "ev_b_bias", "ev_w_out", "od_w_in", "od_q_norm", "od_w_qb", "od_kv_norm",
             "od_w_kvb", "od_w_out")
    small_names = ("norm_pre", "norm_post", "ev_lb_logits", "ev_a_onorm", "ev_b_ln_w", "ev_b_ln_b",
                   "ev_b_ws", "ev_b_bias")
    outs = [loss, grad_x[None]]
    for kind in range(4):
        for nm in order:
            if nm in big_out:
                outs.append(big_out[nm][kind])
            elif nm == "od_q_norm":
                outs.append(qn_out[kind])
            elif nm == "od_kv_norm":
                outs.append(kvn_out[kind])
            else:
                outs.append(small_out[small_names.index(nm)][kind])
    return tuple(outs)
```

```python
import functools

import jax
import jax.numpy as jnp
from jax import lax
from jax.experimental import pallas as pl
from jax.experimental.pallas import tpu as pltpu

F32 = jnp.float32
BF16 = jnp.bfloat16

N_DEV = 8
T = 2048
D = 2048
EPS = 1e-6
A_HEADS = 8
HD = 128
A_CHUNK = 64
A_SUB = 16
B_GROUPS = 8
B_CHUNK = 128
EVEN_IN = 7168
C_HEADS = 16
C_RANK = 512
C_NOPE = 128
C_ROPE = 64
C_QK = C_NOPE + C_ROPE
C_V = 128
ODD_IN = 3136
ODD_IN_PAD = 3200
QP = 256
ROPE_THETA = 10000.0
ATT_SCALE = C_QK ** -0.5

ADAM_LR = 0.001
ADAM_B1 = 0.9
ADAM_B2 = 0.999
ADAM_EPS = 1e-08
ADAM_WD = 0.01
ADAM_STEP = 10

VMEM_LIMIT_V7X = 56 * 1024 * 1024
MESH_ID = pl.DeviceIdType.MESH


def _params(n_grid):
    return pltpu.CompilerParams(dimension_semantics=("arbitrary",) * n_grid,
                                vmem_limit_bytes=VMEM_LIMIT_V7X)


def _dg(a, b, ca, cb):
    return lax.dot_general(a.astype(BF16), b.astype(BF16), (((ca,), (cb,)), ((), ())),
                           preferred_element_type=F32)


def _raw_nn(a, b):
    return _dg(a, b, 1, 0)


def _raw_nt(a, b):
    return _dg(a, b, 1, 1)


def _raw_tn(a, b):
    return _dg(a, b, 0, 0)


@jax.custom_vjp
def _dot_nn(a, b):
    return _raw_nn(a, b)


def _dot_nn_fwd(a, b):
    return _raw_nn(a, b), (a.astype(BF16), b.astype(BF16))


def _dot_nn_bwd(res, g):
    a, b = res
    return _raw_nt(g, b), _raw_tn(a, g)


_dot_nn.defvjp(_dot_nn_fwd, _dot_nn_bwd)


@jax.custom_vjp
def _dot_nt(a, b):
    return _raw_nt(a, b)


def _dot_nt_fwd(a, b):
    return _raw_nt(a, b), (a.astype(BF16), b.astype(BF16))


def _dot_nt_bwd(res, g):
    a, b = res
    return _raw_nn(g, b), _raw_tn(g, a)


_dot_nt.defvjp(_dot_nt_fwd, _dot_nt_bwd)


@jax.custom_vjp
def _dot_tn(a, b):
    return _raw_tn(a, b)


def _dot_tn_fwd(a, b):
    return _raw_tn(a, b), (a.astype(BF16), b.astype(BF16))


def _dot_tn_bwd(res, g):
    a, b = res
    return _raw_nt(b, g), _raw_nn(a, g)


_dot_tn.defvjp(_dot_tn_fwd, _dot_tn_bwd)


@jax.custom_vjp
def _sigmoid(x):
    e = jnp.exp(-jnp.abs(x))
    return jnp.where(x >= 0, 1.0 / (1.0 + e), e / (1.0 + e))


def _sigmoid_fwd(x):
    s = _sigmoid(x)
    return s, s


def _sigmoid_bwd(s, g):
    return (g * s * (1.0 - s),)


_sigmoid.defvjp(_sigmoid_fwd, _sigmoid_bwd)


def _silu(x):
    return x * _sigmoid(x)


def _rms(x, w):
    return x * lax.rsqrt(jnp.mean(x * x, axis=-1, keepdims=True) + EPS) * w


def _split3(x):
    hi = x.astype(BF16)
    r = x - hi.astype(F32)
    mid = r.astype(BF16)
    lo = (r - mid.astype(F32)).astype(BF16)
    return hi, mid, lo


def _mask_apply(mask_bf16, x, contract):
    out = None
    for piece in _split3(x):
        d = lax.dot_general(mask_bf16, piece, (((contract,), (0,)), ((), ())),
                            preferred_element_type=F32)
        out = d if out is None else out + d
    return out


def _chunk_tri(rows):
    r = lax.broadcasted_iota(jnp.int32, (rows, rows), 0)
    c = lax.broadcasted_iota(jnp.int32, (rows, rows), 1)
    return ((r >= c) & (r // A_CHUNK == c // A_CHUNK)).astype(BF16)


@jax.custom_vjp
def _chunk_cumsum(x):
    return _mask_apply(_chunk_tri(x.shape[0]), x, 1)


def _chunk_cumsum_fwd(x):
    return _chunk_cumsum(x), None


def _chunk_cumsum_bwd(_, g):
    return (_mask_apply(_chunk_tri(g.shape[0]), g, 0),)


_chunk_cumsum.defvjp(_chunk_cumsum_fwd, _chunk_cumsum_bwd)


def _hgrn2_rows(q, zf, v, ga, st, l0, l1, onorm):
    rows = q.shape[0]
    n_sub = A_CHUNK // A_SUB
    mx = jnp.maximum(l0, l1)
    e0 = jnp.exp(l0 - mx)
    e1 = jnp.exp(l1 - mx)
    lb = e0 / (e0 + e1)
    lf = jnp.log(lb + (1.0 - lb) * _sigmoid(zf))
    k = (1.0 - lb) * _sigmoid(-zf)
    b = _chunk_cumsum(lf)

    t_idx = lax.broadcasted_iota(jnp.int32, (A_CHUNK, n_sub * A_CHUNK), 0)
    c_idx = lax.broadcasted_iota(jnp.int32, (A_CHUNK, n_sub * A_CHUNK), 1)
    sel = (c_idx // A_CHUNK == t_idx // A_SUB) & (c_idx % A_CHUNK <= t_idx)
    key_row = lax.broadcasted_iota(jnp.int32, (A_CHUNK, HD), 0)

    outs = []
    for n in range(rows // A_CHUNK):
        lo = n * A_CHUNK
        qc, kc, vc = q[lo:lo + A_CHUNK], k[lo:lo + A_CHUNK], v[lo:lo + A_CHUNK]
        lfc, bc = lf[lo:lo + A_CHUNK], b[lo:lo + A_CHUNK]
        b_last = bc[A_CHUNK - 1:A_CHUNK]
        o_inter = _dot_nt(qc * jnp.exp(bc), st)
        kv_t = _dot_tn(vc, kc * jnp.exp(b_last - bc))
        st = st * jnp.exp(b_last) + kv_t
        g_rows, k_subs = [], []
        for i in range(n_sub):
            g_i = bc[i * A_SUB:i * A_SUB + 1] - lfc[i * A_SUB:i * A_SUB + 1]
            g_rows.append(jnp.broadcast_to(g_i, (A_SUB, HD)))
            expo = jnp.where(key_row < (i + 1) * A_SUB, g_i - bc, -jnp.inf)
            k_subs.append(kc * jnp.exp(expo))
        q_sub = qc * jnp.exp(bc - jnp.concatenate(g_rows, axis=0))
        scores = _dot_nt(q_sub, jnp.concatenate(k_subs, axis=0))
        scores = jnp.where(sel, scores, 0.0)
        o_intra = _dot_nn(scores, jnp.concatenate([vc] * n_sub, axis=0))
        outs.append(o_inter + o_intra)
    o = jnp.concatenate(outs, axis=0)
    return _rms(o, onorm) * _silu(ga), st


def _gmlp_rows(u, vb, gb, lnw, lnb, ws, bias):
    rows = u.shape[0]
    mu = jnp.mean(vb, axis=-1, keepdims=True)
    xc = vb - mu
    vg = xc * lax.rsqrt(jnp.mean(xc * xc, axis=-1, keepdims=True) + EPS) * lnw + lnb
    r = lax.broadcasted_iota(jnp.int32, (B_CHUNK, B_CHUNK), 0)
    c = lax.broadcasted_iota(jnp.int32, (B_CHUNK, B_CHUNK), 1)
    ws_causal = jnp.where(r >= c, ws, 0.0)
    svs = [_dot_nn(ws_causal, vg[n * B_CHUNK:(n + 1) * B_CHUNK]) + bias
           for n in range(rows // B_CHUNK)]
    return u * jnp.concatenate(svs, axis=0) * _silu(gb)


def _rope(x, cos_t, sin_t):
    return x * cos_t + pltpu.roll(x, 64, 1) * sin_t


def _rope_transpose(g, cos_t, sin_t):
    return g * cos_t + pltpu.roll(g * sin_t, 64, 1)


ANY_SPEC = pl.BlockSpec(memory_space=pl.ANY)


def _live(deps):
    return [d for d in deps if d is not None]


def _skip_deps(body, n_in, n_deps):
    def wrapped(*refs):
        return body(*refs[:n_in], *refs[n_in + n_deps:])
    return wrapped


def _pure_call(name, fn, grid, in_specs, out_specs, out_shape, args, n_acc=0, deps=()):
    deps = _live(deps)
    n_in, n_out, n_deps = len(in_specs), len(out_specs), len(deps)
    in_specs = list(in_specs) + [ANY_SPEC] * n_deps
    args = tuple(args) + tuple(deps)

    def body(*refs):
        res = fn(*[r[...] for r in refs[:n_in]])
        if not isinstance(res, (tuple, list)):
            res = (res,)
        outs = refs[n_in + n_deps:n_in + n_deps + n_out]
        for o, r in zip(outs[:n_out - n_acc], res[:n_out - n_acc]):
            o[...] = r.astype(o.dtype)
        if n_acc:
            first = functools.reduce(jnp.logical_and, [pl.program_id(i) == 0 for i in range(len(grid))])
            for o, r in zip(outs[n_out - n_acc:], res[n_out - n_acc:]):
                @pl.when(first)
                def _(o=o, r=r):
                    o[...] = r.astype(o.dtype)

                @pl.when(jnp.logical_not(first))
                def _(o=o, r=r):
                    o[...] += r.astype(o.dtype)

    return pl.pallas_call(body, name=name, grid=grid, in_specs=in_specs, out_specs=out_specs,
                          out_shape=out_shape, compiler_params=_params(len(grid)))(*args)


def _sds(shape, dtype):
    return jax.ShapeDtypeStruct(shape, dtype)


def _row_spec(tm, width, col=0):
    return pl.BlockSpec((tm, width), lambda i, col=col: (i, col))


def _full_spec(shape):
    nd = len(shape)
    return pl.BlockSpec(shape, lambda *_: (0,) * nd)


def _mm_nn(name, a, b, out_dtype, tm, tn, deps=()):
    deps = _live(deps)
    m, k = a.shape
    j, _, n = b.shape
    per = n // tn

    def body(a_ref, b_ref, o_ref):
        o_ref[...] = _raw_nn(a_ref[...], b_ref[...]).astype(o_ref.dtype)

    return pl.pallas_call(
        _skip_deps(body, 2, len(deps)), name=name, grid=(m // tm, j * per),
        in_specs=[pl.BlockSpec((tm, k), lambda i, c: (i, 0)),
                  pl.BlockSpec((None, k, tn), lambda i, c: (c // per, 0, c % per))] + [ANY_SPEC] * len(deps),
        out_specs=pl.BlockSpec((tm, tn), lambda i, c: (i, c)),
        out_shape=_sds((m, j * n), out_dtype), compiler_params=_params(2))(a, b, *deps)


def _mm_nt(name, a, b, out_dtype, tm, tn, deps=()):
    deps = _live(deps)
    m = a.shape[0]
    j, nn, n = b.shape

    def body(a_ref, b_ref, o_ref, acc_ref):
        part = _raw_nt(a_ref[...], b_ref[...])
        if j == 1:
            o_ref[...] = part.astype(o_ref.dtype)
        else:
            kk = pl.program_id(2)

            @pl.when(kk == 0)
            def _():
                acc_ref[...] = part

            @pl.when(kk > 0)
            def _():
                acc_ref[...] += part

            @pl.when(kk == j - 1)
            def _():
                o_ref[...] = acc_ref[...].astype(o_ref.dtype)

    acc_shape = (tm, tn) if j > 1 else (8, 128)
    return pl.pallas_call(
        _skip_deps(body, 2, len(deps)), name=name, grid=(m // tm, nn // tn, j),
        in_specs=[pl.BlockSpec((tm, n), lambda i, c, kk: (i, kk)),
                  pl.BlockSpec((None, tn, n), lambda i, c, kk: (kk, c, 0))] + [ANY_SPEC] * len(deps),
        out_specs=pl.BlockSpec((tm, tn), lambda i, c, kk: (i, c)),
        out_shape=_sds((m, nn), out_dtype),
        scratch_shapes=[pltpu.VMEM(acc_shape, F32)], compiler_params=_params(3))(a, b, *deps)


def _mm_tn(name, a, b, j, out_dtype, tm, tn, deps=()):
    deps = _live(deps)
    k, m = a.shape
    n = b.shape[1] // j
    per = n // tn

    def body(a_ref, b_ref, o_ref):
        o_ref[...] = _raw_tn(a_ref[...], b_ref[...]).astype(o_ref.dtype)

    return pl.pallas_call(
        _skip_deps(body, 2, len(deps)), name=name, grid=(m // tm, j * per),
        in_specs=[pl.BlockSpec((k, tm), lambda i, c: (0, i)),
                  pl.BlockSpec((k, tn), lambda i, c: (0, c))] + [ANY_SPEC] * len(deps),
        out_specs=pl.BlockSpec((None, tm, tn), lambda i, c: (c // per, i, c % per)),
        out_shape=_sds((j, m, n), out_dtype), compiler_params=_params(2))(a, b, *deps)


TM = 256


def _pre_norm(name, x, w_row, deps=()):
    def fn(xv, w):
        return _rms(xv, w)
    return _pure_call(name, fn, (T // TM,), [_row_spec(TM, D), _full_spec((1, D))],
                      [_row_spec(TM, D)], [_sds((T, D), BF16)], (x, w_row), deps=deps)[0]


def _post_pre_norm(x, y, w_post, w_pre):
    def fn(xv, yv, wp, wn):
        x1 = xv + _rms(yv, wp)
        return x1, _rms(x1, wn)
    return _pure_call("post_pre_norm", fn, (T // TM,),
                      [_row_spec(TM, D), _row_spec(TM, D), _full_spec((1, D)), _full_spec((1, D))],
                      [_row_spec(TM, D), _row_spec(TM, D)],
                      [_sds((T, D), F32), _sds((T, D), BF16)], (x, y, w_post, w_pre))


def _post_pre_norm_bwd(y, x1, w_post, w_pre, dx1_in, dh1, deps=()):
    def fn(yv, x1v, wp, wn, dx1v, dh1v):
        _, vjp_pre = jax.vjp(_rms, x1v, wn)
        dx1_h, dwn = vjp_pre(dh1v)
        dx1 = dx1v + dx1_h
        _, vjp_post = jax.vjp(_rms, yv, wp)
        dy, dwp = vjp_post(dx1)
        return dx1, dy, dwp, dwn
    return _pure_call("post_pre_norm_bwd", fn, (T // TM,),
                      [_row_spec(TM, D), _row_spec(TM, D), _full_spec((1, D)), _full_spec((1, D)),
                       _row_spec(TM, D), _row_spec(TM, D)],
                      [_row_spec(TM, D), _row_spec(TM, D), _full_spec((1, D)), _full_spec((1, D))],
                      [_sds((T, D), F32), _sds((T, D), BF16), _sds((1, D), F32), _sds((1, D), F32)],
                      (y, x1, w_post, w_pre, dx1_in, dh1), n_acc=2, deps=deps)


def _final_loss(x1, y, w_post, target):
    def fn(x1v, yv, wp, tv):
        r, vjp = jax.vjp(_rms, yv, wp)
        err = x1v + r - tv
        part = 0.5 * jnp.sum(jnp.mean(err * err, axis=-1, keepdims=True), axis=0, keepdims=True)
        dx2 = err * (1.0 / D)
        dy, dwp = vjp(dx2)
        return dx2, dy, jnp.broadcast_to(part, (1, 128)), dwp
    return _pure_call("final_loss", fn, (T // TM,),
                      [_row_spec(TM, D), _row_spec(TM, D), _full_spec((1, D)), _row_spec(TM, D)],
                      [_row_spec(TM, D), _row_spec(TM, D), _full_spec((1, 128)), _full_spec((1, D))],
                      [_sds((T, D), F32), _sds((T, D), BF16), _sds((1, 128), F32), _sds((1, D), F32)],
                      (x1, y, w_post, target), n_acc=2)


def _pre_norm_bwd(x, w_row, dh, dx_res, deps=()):
    def fn(xv, w, dhv, dxv):
        _, vjp = jax.vjp(_rms, xv, w)
        dx, dw = vjp(dhv)
        return dxv + dx, dw
    return _pure_call("pre_norm_bwd", fn, (T // TM,),
                      [_row_spec(TM, D), _full_spec((1, D)), _row_spec(TM, D), _row_spec(TM, D)],
                      [_row_spec(TM, D), _full_spec((1, D))],
                      [_sds((T, D), F32), _sds((1, D), F32)], (x, w_row, dh, dx_res), n_acc=1, deps=deps)


RA = 256
RB = 512


def _col_spec(rows, col_of):
    return pl.BlockSpec((rows, HD), lambda h, r, col_of=col_of: (r, col_of(h)))


def _hgrn2_fwd(z, l0, l1, onorm):
    nb = T // RA

    def body(q_ref, f_ref, v_ref, g_ref, l0_ref, l1_ref, on_ref, cat_ref, sst_ref, st_scr):
        @pl.when(pl.program_id(1) == 0)
        def _():
            st_scr[...] = jnp.zeros_like(st_scr)

        st = st_scr[...]
        sst_ref[...] = st
        out, st_new = _hgrn2_rows(q_ref[...], f_ref[...], v_ref[...], g_ref[...], st,
                                  l0_ref[...], l1_ref[...], on_ref[...])
        cat_ref[...] = out.astype(cat_ref.dtype)
        st_scr[...] = st_new

    vec = pl.BlockSpec((1, HD), lambda h, r: (0, h))
    return pl.pallas_call(
        body, name="hgrn2_fwd", grid=(A_HEADS, nb),
        in_specs=[_col_spec(RA, lambda h: h), _col_spec(RA, lambda h: 8 + h),
                  _col_spec(RA, lambda h: 16 + h), _col_spec(RA, lambda h: 24 + h),
                  vec, vec, _full_spec((1, HD))],
        out_specs=[_col_spec(RA, lambda h: h),
                   pl.BlockSpec((None, None, HD, HD), lambda h, r: (h, r, 0, 0))],
        out_shape=[_sds((T, 2 * A_HEADS * HD), BF16), _sds((A_HEADS, nb, HD, HD), F32)],
        scratch_shapes=[pltpu.VMEM((HD, HD), F32)],
        compiler_params=_params(2))(z, z, z, z, l0, l1, onorm)


def _hgrn2_bwd(z, l0, l1, onorm, sst, dcat, deps=()):
    nb = T // RA
    deps = _live(deps)

    def body(q_ref, f_ref, v_ref, g_ref, l0_ref, l1_ref, on_ref, sst_ref, dcat_ref,
             dq_ref, df_ref, dv_ref, dg_ref, dl0_ref, dl1_ref, don_ref, ds_scr):
        h, r = pl.program_id(0), pl.program_id(1)

        @pl.when(r == 0)
        def _():
            ds_scr[...] = jnp.zeros_like(ds_scr)

        _, vjp = jax.vjp(_hgrn2_rows, q_ref[...], f_ref[...], v_ref[...], g_ref[...], sst_ref[...],
                         l0_ref[...], l1_ref[...], on_ref[...])
        dq, dzf, dv, dga, dst, dl0, dl1, don = vjp((dcat_ref[...], ds_scr[...]))
        dq_ref[...] = dq.astype(dq_ref.dtype)
        df_ref[...] = dzf.astype(df_ref.dtype)
        dv_ref[...] = dv.astype(dv_ref.dtype)
        dg_ref[...] = dga.astype(dg_ref.dtype)
        ds_scr[...] = dst

        @pl.when(r == 0)
        def _():
            dl0_ref[...] = dl0
            dl1_ref[...] = dl1

        @pl.when(r > 0)
        def _():
            dl0_ref[...] += dl0
            dl1_ref[...] += dl1

        first = jnp.logical_and(h == 0, r == 0)

        @pl.when(first)
        def _():
            don_ref[...] = don

        @pl.when(jnp.logical_not(first))
        def _():
            don_ref[...] += don

    def rev(col_of):
        return pl.BlockSpec((RA, HD), lambda h, r, col_of=col_of: (nb - 1 - r, col_of(h)))

    vec = pl.BlockSpec((1, HD), lambda h, r: (0, h))
    grad = _sds((T, A_HEADS * HD), BF16)
    return pl.pallas_call(
        _skip_deps(body, 9, len(deps)), name="hgrn2_bwd", grid=(A_HEADS, nb),
        in_specs=[rev(lambda h: h), rev(lambda h: 8 + h), rev(lambda h: 16 + h), rev(lambda h: 24 + h),
                  vec, vec, _full_spec((1, HD)),
                  pl.BlockSpec((None, None, HD, HD), lambda h, r: (h, nb - 1 - r, 0, 0)),
                  rev(lambda h: h)] + [ANY_SPEC] * len(deps),
        out_specs=[rev(lambda h: h)] * 4 + [vec, vec, _full_spec((1, HD))],
        out_shape=[grad] * 4 + [_sds((1, A_HEADS * HD), F32)] * 2 + [_sds((1, HD), F32)],
        scratch_shapes=[pltpu.VMEM((HD, HD), F32)],
        compiler_params=_params(2))(z, z, z, z, l0, l1, onorm, sst, dcat, *deps)


def _gmlp_specs():
    vec = pl.BlockSpec((1, HD), lambda g, r: (0, g))
    ws = pl.BlockSpec((None, B_CHUNK, B_CHUNK), lambda g, r: (g, 0, 0))
    bias = pl.BlockSpec((None, B_CHUNK, 1), lambda g, r: (g, 0, 0))
    return vec, ws, bias


def _gmlp_fwd(z, cat, lnw, lnb, ws, bias):
    vec, ws_spec, bias_spec = _gmlp_specs()

    def body(u_ref, v_ref, g_ref, lnw_ref, lnb_ref, ws_ref, bias_ref, cat_in_ref, cat_ref):
        del cat_in_ref
        out = _gmlp_rows(u_ref[...], v_ref[...], g_ref[...], lnw_ref[...], lnb_ref[...],
                         ws_ref[...], bias_ref[...])
        cat_ref[...] = out.astype(cat_ref.dtype)

    return pl.pallas_call(
        body, name="gmlp_fwd", grid=(B_GROUPS, T // RB),
        in_specs=[_col_spec(RB, lambda g: 32 + g), _col_spec(RB, lambda g: 40 + g),
                  _col_spec(RB, lambda g: 48 + g), vec, vec, ws_spec, bias_spec,
                  pl.BlockSpec(memory_space=pl.ANY)],
        out_specs=_col_spec(RB, lambda g: A_HEADS + g),
        out_shape=_sds(cat.shape, cat.dtype), input_output_aliases={7: 0},
        compiler_params=_params(2))(z, z, z, lnw, lnb, ws, bias, cat)


def _gmlp_bwd(z, lnw, lnb, ws, bias, dcat):
    vec, ws_spec, bias_spec = _gmlp_specs()

    def fn(u, vb, gb, w, b, wsv, bv, dout):
        _, vjp = jax.vjp(_gmlp_rows, u, vb, gb, w, b, wsv, bv)
        return vjp(dout)

    def body(*refs):
        ins, outs = refs[:8], refs[8:]
        res = fn(*[r[...] for r in ins])
        for o, r in zip(outs[:3], res[:3]):
            o[...] = r.astype(o.dtype)
        first = pl.program_id(1) == 0
        for o, r in zip(outs[3:], res[3:]):
            @pl.when(first)
            def _(o=o, r=r):
                o[...] = r

            @pl.when(jnp.logical_not(first))
            def _(o=o, r=r):
                o[...] += r

    grad = _sds((T, B_GROUPS * HD), BF16)
    row_out = pl.BlockSpec((RB, HD), lambda g, r: (r, g))
    return pl.pallas_call(
        body, name="gmlp_bwd", grid=(B_GROUPS, T // RB),
        in_specs=[_col_spec(RB, lambda g: 32 + g), _col_spec(RB, lambda g: 40 + g),
                  _col_spec(RB, lambda g: 48 + g), vec, vec, ws_spec, bias_spec,
                  _col_spec(RB, lambda g: A_HEADS + g)],
        out_specs=[row_out] * 3 + [vec, vec, ws_spec, bias_spec],
        out_shape=[grad] * 3 + [_sds((1, B_GROUPS * HD), F32)] * 2
        + [_sds((B_GROUPS, B_CHUNK, B_CHUNK), F32), _sds((B_GROUPS, B_CHUNK, 1), F32)],
        compiler_params=_params(2))(z, z, z, lnw, lnb, ws, bias, dcat)


def _mla_pre(z1, qn, kvn, cos_t, sin_t):
    def fn(cq, ckv, kpe, cs, sn, wq, wkv):
        return _rms(cq, wq), _rms(ckv, wkv), _rope(kpe, cs, sn)
    return _pure_call("mla_pre", fn, (T // TM,),
                      [_row_spec(TM, C_RANK, 4), _row_spec(TM, C_RANK, 5), _row_spec(TM, HD, 24),
                       _row_spec(TM, HD), _row_spec(TM, HD),
                       _full_spec((1, C_RANK)), _full_spec((1, C_RANK))],
                      [_row_spec(TM, C_RANK), _row_spec(TM, C_RANK), _row_spec(TM, HD)],
                      [_sds((T, C_RANK), BF16), _sds((T, C_RANK), BF16), _sds((T, HD), BF16)],
                      (z1, z1, z1, cos_t, sin_t, qn, kvn))


def _mla_pre_bwd(z1, qn, kvn, cos_t, sin_t, dcqn, dckvn, dkp, deps=()):
    def fn(cq, ckv, cs, sn, wq, wkv, g_q, g_kv, g_kp):
        _, vjp_q = jax.vjp(_rms, cq, wq)
        dcq, dwq = vjp_q(g_q)
        _, vjp_kv = jax.vjp(_rms, ckv, wkv)
        dckv, dwkv = vjp_kv(g_kv)
        return dcq, dckv, _rope_transpose(g_kp, cs, sn), dwq, dwkv
    return _pure_call("mla_pre_bwd", fn, (T // TM,),
                      [_row_spec(TM, C_RANK, 4), _row_spec(TM, C_RANK, 5),
                       _row_spec(TM, HD), _row_spec(TM, HD),
                       _full_spec((1, C_RANK)), _full_spec((1, C_RANK)),
                       _row_spec(TM, C_RANK), _row_spec(TM, C_RANK), _row_spec(TM, HD)],
                      [_row_spec(TM, C_RANK), _row_spec(TM, C_RANK), _row_spec(TM, HD),
                       _full_spec((1, C_RANK)), _full_spec((1, C_RANK))],
                      [_sds((T, C_RANK), BF16), _sds((T, C_RANK), BF16), _sds((T, HD), BF16),
                       _sds((1, C_RANK), F32), _sds((1, C_RANK), F32)],
                      (z1, z1, cos_t, sin_t, qn, kvn, dcqn, dckvn, dkp), n_acc=2, deps=deps)


def _gate_out(o, z1):
    def fn(ov, gate):
        return ov * _silu(gate)
    return _pure_call("gate_out", fn, (T // TM,), [_row_spec(TM, D), _row_spec(TM, D, 0)],
                      [_row_spec(TM, D)], [_sds((T, D), BF16)], (o, z1))[0]


def _gate_out_bwd(o, z1, dog, deps=()):
    def fn(ov, gate, g):
        _, vjp = jax.vjp(lambda a, b: a * _silu(b), ov, gate)
        return vjp(g)
    return _pure_call("gate_out_bwd", fn, (T // TM,),
                      [_row_spec(TM, D), _row_spec(TM, D, 0), _row_spec(TM, D)],
                      [_row_spec(TM, D), _row_spec(TM, D)],
                      [_sds((T, D), F32), _sds((T, D), BF16)], (o, z1, dog), deps=deps)


TQ = 256


def _att_keys(kn_ref, kp_ref, k_scr):
    @pl.when(pl.program_id(1) == 0)
    def _():
        k_scr[:, 0:C_NOPE] = kn_ref[...]
        k_scr[:, C_NOPE:QP] = kp_ref[...]


def _att_scores(q_ref, cos_ref, sin_ref, k_scr, n):
    keys = (n + 1) * TQ
    q = q_ref[...]
    qr = jnp.concatenate([q[:, :C_NOPE], _rope(q[:, C_NOPE:], cos_ref[...], sin_ref[...])], axis=1).astype(BF16)
    return qr, _raw_nt(qr, k_scr[0:keys, :]) * ATT_SCALE


def _causal(x, n, fill):
    row = lax.broadcasted_iota(jnp.int32, (TQ, TQ), 0)
    col = lax.broadcasted_iota(jnp.int32, (TQ, TQ), 1)
    diag = jnp.where(col <= row, x[:, n * TQ:], fill)
    return diag if n == 0 else jnp.concatenate([x[:, :n * TQ], diag], axis=1)


def _per_query_block(fn):
    for n in range(T // TQ):
        pl.when(pl.program_id(1) == n)(functools.partial(fn, n))


def _att_in_specs():
    return [pl.BlockSpec((TQ, QP), lambda h, i: (i, h)),
            pl.BlockSpec((TQ, HD), lambda h, i: (i, 0)),
            pl.BlockSpec((TQ, HD), lambda h, i: (i, 0)),
            pl.BlockSpec((T, C_NOPE), lambda h, i: (0, 2 * h)),
            pl.BlockSpec((T, HD), lambda h, i: (0, 0)),
            pl.BlockSpec((T, C_V), lambda h, i: (0, 2 * h + 1))]


def _attention_fwd(q, cos_t, sin_t, kv, kp):
    def body(q_ref, cos_ref, sin_ref, kn_ref, kp_ref, v_ref, o_ref, lse_ref, k_scr):
        _att_keys(kn_ref, kp_ref, k_scr)

        def block(n):
            _, s = _att_scores(q_ref, cos_ref, sin_ref, k_scr, n)
            s = _causal(s, n, jnp.finfo(F32).min)
            m = jnp.max(s, axis=-1, keepdims=True)
            p = jnp.exp(s - m)
            l = jnp.sum(p, axis=-1, keepdims=True)
            o_ref[...] = _raw_nn(p, v_ref[0:(n + 1) * TQ, :]) / l
            lse_ref[...] = m + jnp.log(l)

        _per_query_block(block)

    return pl.pallas_call(
        body, name="attention_fwd", grid=(C_HEADS, T // TQ), in_specs=_att_in_specs(),
        out_specs=[pl.BlockSpec((TQ, C_V), lambda h, i: (i, h)),
                   pl.BlockSpec((None, TQ, 1), lambda h, i: (h, i, 0))],
        out_shape=[_sds((T, C_HEADS * C_V), F32), _sds((C_HEADS, T, 1), F32)],
        scratch_shapes=[pltpu.VMEM((T, QP), BF16)],
        compiler_params=_params(2))(q, cos_t, sin_t, kv, kp, kv)


def _attention_bwd(q, cos_t, sin_t, kv, kp, o, lse, do):
    nq = T // TQ

    def body(q_ref, cos_ref, sin_ref, kn_ref, kp_ref, v_ref, o_ref, lse_ref, do_ref,
             dq_ref, dkv_ref, dkp_ref, k_scr, dk_scr, dv_scr):
        h, i = pl.program_id(0), pl.program_id(1)
        _att_keys(kn_ref, kp_ref, k_scr)

        @pl.when(i == 0)
        def _():
            dv_scr[...] = jnp.zeros_like(dv_scr)
            dk_scr[...] = jnp.zeros_like(dk_scr)

        def block(n):
            keys = (n + 1) * TQ
            qr, s = _att_scores(q_ref, cos_ref, sin_ref, k_scr, n)
            p = _causal(jnp.exp(s - lse_ref[...]), n, 0.0)
            dov = do_ref[...]
            delta = jnp.sum(dov * o_ref[...], axis=-1, keepdims=True)
            dp = _raw_nt(dov, v_ref[0:keys, :])
            ds = p * (dp - delta) * ATT_SCALE
            dq = _raw_nn(ds, k_scr[0:keys, :])
            dq_ref[...] = jnp.concatenate(
                [dq[:, :C_NOPE], _rope_transpose(dq[:, C_NOPE:], cos_ref[...], sin_ref[...])], axis=1).astype(dq_ref.dtype)
            dv_scr[0:keys, :] += _raw_tn(p, dov)
            dk_scr[0:keys, :] += _raw_tn(ds, qr)

        _per_query_block(block)

        @pl.when(i == nq - 1)
        def _():
            dkv_ref[...] = jnp.concatenate([dk_scr[:, 0:C_NOPE], dv_scr[...]], axis=1).astype(dkv_ref.dtype)

        @pl.when(jnp.logical_and(i == nq - 1, h == 0))
        def _():
            dkp_ref[...] = dk_scr[:, C_NOPE:QP]

        @pl.when(jnp.logical_and(i == nq - 1, h > 0))
        def _():
            dkp_ref[...] += dk_scr[:, C_NOPE:QP]

    return pl.pallas_call(
        body, name="attention_bwd", grid=(C_HEADS, nq),
        in_specs=_att_in_specs() + [pl.BlockSpec((TQ, C_V), lambda h, i: (i, h)),
                                    pl.BlockSpec((None, TQ, 1), lambda h, i: (h, i, 0)),
                                    pl.BlockSpec((TQ, C_V), lambda h, i: (i, h))],
        out_specs=[pl.BlockSpec((TQ, QP), lambda h, i: (i, h)),
                   pl.BlockSpec((T, C_NOPE + C_V), lambda h, i: (0, h)),
                   _full_spec((T, HD))],
        out_shape=[_sds((T, C_HEADS * QP), BF16), _sds((T, C_HEADS * (C_NOPE + C_V)), BF16),
                   _sds((T, HD), F32)],
        scratch_shapes=[pltpu.VMEM((T, QP), BF16), pltpu.VMEM((T, QP), F32), pltpu.VMEM((T, C_V), F32)],
        compiler_params=_params(2))(q, cos_t, sin_t, kv, kp, kv, o, lse, do)


def _adamw_math(w, g, m, v):
    m = ADAM_B1 * m + (1.0 - ADAM_B1) * g
    v = ADAM_B2 * v + (1.0 - ADAM_B2) * (g * g)
    m_hat = m / (1.0 - ADAM_B1 ** ADAM_STEP)
    v_hat = v / (1.0 - ADAM_B2 ** ADAM_STEP)
    delta = -ADAM_LR * (m_hat / (jnp.sqrt(v_hat) + ADAM_EPS) + ADAM_WD * w)
    return delta, m, v


def _adamw(name, parts, w, m, v, tr, tc=None):
    rows, cols = w.shape

    def fn(*vals):
        pvs, (wv, mv, vv) = vals[:len(parts)], vals[len(parts):]
        g = None
        for pv in pvs:
            for d in range(pv.shape[0]):
                term = pv[d].astype(F32)
                g = term if g is None else g + term
        return (g,) + _adamw_math(wv, g, mv, vv)

    tc = cols if tc is None else tc
    blk = pl.BlockSpec((tr, tc), lambda i, j: (i, j))
    part_specs = [pl.BlockSpec((n, tr, tc), lambda i, j: (0, i, j)) for _, n in parts]
    return _pure_call(name, fn, (rows // tr, cols // tc), part_specs + [blk, blk, blk],
                      [blk] * 4, [_sds((rows, cols), F32)] * 4, tuple(p for p, _ in parts) + (w, m, v))


SMALL_PARAM_SHAPES = ((2, D), (2, D), (2, A_HEADS * HD), (1, HD), (1, B_GROUPS * HD), (1, B_GROUPS * HD),
                      (B_GROUPS, B_CHUNK, B_CHUNK), (B_GROUPS, B_CHUNK))
SMALL_PIECES = ((0, 0, 0, 0), (0, 1, 1, 0), (1, 0, 1, 1), (1, 1, 1, 2), (2, 0, 2, 0), (2, 1, 2, 1),
                (3, 0, 3, 8), (4, 0, 2, 2), (5, 0, 2, 3))


def _small_rows(dnpre1, dnpost0, dnpost1, dl0, dl1, donorm, dlnw, dlnb, dws, dbias, dqn, dkvn, loss_part):
    return [jnp.concatenate([dnpre1, dnpost0, dnpost1], axis=0),
            jnp.concatenate([dl0, dl1, dlnw, dlnb], axis=0),
            jnp.concatenate([dbias.reshape(B_GROUPS, B_CHUNK), donorm, loss_part], axis=0),
            dws,
            jnp.concatenate([dqn, dkvn], axis=0)]


def _adamw_small(late_all, early_all, wmv):
    n_in = 6 + 3 * len(wmv)

    def body(*refs):
        gathered, params, outs = refs[:6], refs[6:n_in], refs[n_in:]

        def total(ref):
            s = ref[0]
            for d in range(1, N_DEV):
                s = s + ref[d]
            return s

        g_late, g2048, g1024, g128, g_ws, g512 = [total(r) for r in gathered]
        arrays = (g_late, g2048, g1024, g128)

        def update(p, rows, g):
            w_ref, m_ref, v_ref = params[3 * p:3 * p + 3]
            delta, m, v = _adamw_math(w_ref[rows], g, m_ref[rows], v_ref[rows])
            for out, val in zip(outs[4 * p:4 * p + 4], (g, delta, m, v)):
                out[rows] = val

        for p, row, arr, arr_row in SMALL_PIECES:
            update(p, pl.ds(row, 1), arrays[arr][arr_row:arr_row + 1])
        update(6, slice(None), g_ws)
        update(7, slice(None), g128[0:B_GROUPS])
        outs[32][...] = g128[B_GROUPS + 1:B_GROUPS + 2]
        outs[33][...] = g512

    vmem = pl.BlockSpec(memory_space=pltpu.VMEM)
    flat = [a for t in wmv for a in t]
    out_shape = [_sds(s, F32) for s in SMALL_PARAM_SHAPES for _ in range(4)] + [_sds((1, 128), F32), _sds((2, C_RANK), F32)]
    res = pl.pallas_call(body, name="adamw_small", in_specs=[vmem] * n_in, out_specs=[vmem] * len(out_shape),
                         out_shape=out_shape,
                         compiler_params=pltpu.CompilerParams(vmem_limit_bytes=VMEM_LIMIT_V7X))(late_all, *early_all, *flat)
    return [res[4 * p:4 * p + 4] for p in range(8)], res[32], res[33]


def _exchange(name, arrs, gather, deps=()):
    n = len(arrs)
    deps = _live(deps)

    def body(*refs):
        ins, outs = refs[:n], refs[n + len(deps):2 * n + len(deps)]
        send_sems, recv_sems, local_sems = refs[2 * n + len(deps):]
        x, y, c = lax.axis_index("x"), lax.axis_index("y"), lax.axis_index("c")
        me = 4 * x + 2 * y + c

        def peer(k):
            return (x ^ (k >> 2), y ^ ((k >> 1) & 1), c ^ (k & 1))

        def copy(a, k):
            src = ins[a] if gather else ins[a].at[me ^ k]
            return pltpu.make_async_remote_copy(
                src_ref=src, dst_ref=outs[a].at[me], send_sem=send_sems.at[a, k - 1],
                recv_sem=recv_sems.at[a, k - 1], device_id=peer(k), device_id_type=MESH_ID)

        def arrival(a, k):
            src = ins[a] if gather else ins[a].at[me]
            return pltpu.make_async_remote_copy(
                src_ref=src, dst_ref=outs[a].at[me ^ k], send_sem=send_sems.at[a, k - 1],
                recv_sem=recv_sems.at[a, k - 1], device_id=peer(k), device_id_type=MESH_ID)

        own = [pltpu.make_async_copy(ins[a] if gather else ins[a].at[me], outs[a].at[me], local_sems.at[a])
               for a in range(n)]
        for cp in own:
            cp.start()
        for k in range(1, N_DEV):
            for a in range(n):
                copy(a, k).start()
        for k in range(1, N_DEV):
            for a in range(n):
                arrival(a, k).wait_recv()
        for k in range(1, N_DEV):
            for a in range(n):
                copy(a, k).wait_send()
        for cp in own:
            cp.wait()

    any_spec = pl.BlockSpec(memory_space=pl.ANY)
    out_shape = [_sds((N_DEV,) + a.shape if gather else a.shape, a.dtype) for a in arrs]
    return pl.pallas_call(
        body, name=name, in_specs=[any_spec] * (n + len(deps)), out_specs=[any_spec] * n, out_shape=out_shape,
        scratch_shapes=[pltpu.SemaphoreType.DMA((n, N_DEV - 1)), pltpu.SemaphoreType.DMA((n, N_DEV - 1)),
                        pltpu.SemaphoreType.DMA((n,))],
        compiler_params=pltpu.CompilerParams(has_side_effects=True))(*arrs, *deps)


HBM_SPEC = pl.BlockSpec(memory_space=pltpu.HBM)
SEM_SPEC = pl.BlockSpec(memory_space=pltpu.SEMAPHORE)
DATAFLOW = pltpu.SideEffectType.DATAFLOW_SIDE_EFFECTING


def _my_index():
    return 4 * lax.axis_index("x") + 2 * lax.axis_index("y") + lax.axis_index("c")


def _plan_copies(plan, refs, send_sems, recv_sems):
    x, y, c = lax.axis_index("x"), lax.axis_index("y"), lax.axis_index("c")
    return [pltpu.make_async_remote_copy(
        src_ref=src, dst_ref=dst, send_sem=send_sems.at[i], recv_sem=recv_sems.at[i],
        device_id=(x ^ (k >> 2), y ^ ((k >> 1) & 1), c ^ (k & 1)), device_id_type=MESH_ID)
        for i, (src, dst, k) in enumerate(plan(refs, 4 * x + 2 * y + c))]


def _split_call(name, bufs, waits=None, starts=None, deps=()):
    n = len(bufs)
    deps = _live(deps)
    n_wait = 2 if waits else 0

    def body(*refs):
        zones = refs[:n]
        if waits:
            for cp in _plan_copies(waits[2], zones, refs[n], refs[n + 1]):
                cp.wait_send()
                cp.wait_recv()
        if starts:
            first_out = n + n_wait + len(deps)
            for cp in _plan_copies(starts[0], zones, refs[first_out], refs[first_out + 1]):
                cp.start()
            refs[-1][...] = jnp.zeros_like(refs[-1])

    out_specs, out_shape = [], []
    if starts:
        sems = pltpu.SemaphoreType.DMA((starts[1],))
        out_specs, out_shape = [SEM_SPEC, SEM_SPEC], [sems, sems]
    out_specs += [HBM_SPEC] * n
    out_shape += [pltpu.HBM(b.shape, b.dtype) for b in bufs]
    if starts:
        out_specs.append(pl.BlockSpec(memory_space=pltpu.VMEM))
        out_shape.append(_sds((8, 128), F32))
    first_buf = 2 if starts else 0
    res = pl.pallas_call(
        body, name=name,
        in_specs=[HBM_SPEC] * n + [SEM_SPEC] * n_wait + [ANY_SPEC] * len(deps),
        out_specs=out_specs, out_shape=out_shape,
        input_output_aliases={i: first_buf + i for i in range(n)},
        compiler_params=pltpu.CompilerParams(has_side_effects=DATAFLOW),
    )(*[pltpu.with_memory_space_constraint(b, pltpu.HBM) for b in bufs], *(waits[:2] if waits else ()), *deps)
    out_bufs = list(res[first_buf:first_buf + n])
    return out_bufs, ((res[0], res[1]) if starts else None), (res[-1] if starts else None)


def _direct_plan(n, gather):
    def plan(refs, me):
        return [(refs[a] if gather else refs[a].at[me ^ k], refs[n + a].at[me], k)
                for k in range(1, N_DEV) for a in range(n)]
    return plan


def _own_slot_filled(a, gather):
    me = _my_index()
    if gather:
        return lax.dynamic_update_slice_in_dim(lax.empty((N_DEV,) + a.shape, a.dtype), a[None], me, 0)
    return lax.dynamic_update_slice_in_dim(lax.empty(a.shape, a.dtype), lax.dynamic_slice_in_dim(a, me, 1, 0), me, 0)


def _exchange_start(name, arrs, gather, deps=()):
    n = len(arrs)
    lands = [_own_slot_filled(a, gather) for a in arrs]
    plan = _direct_plan(n, gather)
    bufs, sems, token = _split_call(name, list(arrs) + lands, starts=(plan, n * (N_DEV - 1)), deps=deps)
    return (n, plan, sems, bufs, None), token


def _exchange_wait(name, handle, after):
    return _split_done(name, handle, after)


ICI_PEERS = (2, 4, 6)
SIBLING = 1


def _gather2_send(name, arrs, deps=()):
    n = len(arrs)
    lands = [_own_slot_filled(a, True) for a in arrs]

    def plan(refs, me_):
        return [(refs[a], refs[n + a].at[me_], k) for k in (SIBLING,) + ICI_PEERS for a in range(n)]

    bufs, sems, token = _split_call(name, list(arrs) + lands, starts=(plan, 4 * n), deps=deps)
    return (n, plan, sems, bufs, None), token


def _gather2_relay(name, handle, after):
    n, plan, sems, bufs, _ = handle

    def relay(refs, me_):
        return [(refs[n + a].at[me_ ^ k], refs[n + a].at[me_ ^ k], SIBLING) for k in ICI_PEERS for a in range(n)]

    bufs, sems2, token = _split_call(name, bufs, waits=(sems[0], sems[1], plan), starts=(relay, 3 * n), deps=[after])
    return (n, relay, sems2, bufs, None), token


def _split_done(name, handle, after, all_bufs=False):
    n, plan, sems, bufs, _ = handle
    bufs, _, _ = _split_call(name, bufs, waits=(sems[0], sems[1], plan), deps=[after])
    return bufs if all_bufs else bufs[n:]


def _scatter2_pair(name, stacks, deps=()):
    n = len(stacks)
    pairs = [lax.empty((4,) + s.shape[1:], s.dtype) for s in stacks]

    def plan(refs, me):
        return [(refs[a].at[(me ^ SIBLING) ^ (2 * j)], refs[n + a].at[j], SIBLING) for j in range(4) for a in range(n)]

    bufs, sems, token = _split_call(name, list(stacks) + pairs, starts=(plan, 4 * n), deps=deps)
    return (n, plan, sems, bufs, None), token


def _pair_add(name, stack, pair, me):
    _, rows, cols = stack.shape
    tr = rows // 8

    def body(me_ref, s_ref, p_ref, o_ref):
        del me_ref
        o_ref[...] = (s_ref[...].astype(F32) + p_ref[...].astype(F32)).astype(o_ref.dtype)

    grid_spec = pltpu.PrefetchScalarGridSpec(
        num_scalar_prefetch=1, grid=(4, rows // tr),
        in_specs=[pl.BlockSpec((None, tr, cols), lambda j, i, me_ref: (me_ref[0] ^ (2 * j), i, 0)),
                  pl.BlockSpec((None, tr, cols), lambda j, i, me_ref: (j, i, 0))],
        out_specs=pl.BlockSpec((None, tr, cols), lambda j, i, me_ref: (j, i, 0)))
    return pl.pallas_call(body, name=name, grid_spec=grid_spec, out_shape=_sds((4, rows, cols), stack.dtype),
                          compiler_params=_params(2))(me.reshape(1).astype(jnp.int32), stack, pair)


def _scatter2_send(name, chip_sums, deps=()):
    n = len(chip_sums)
    finals = [lax.empty((3,) + c.shape[1:], c.dtype) for c in chip_sums]

    def plan(refs, me):
        del me
        return [(refs[a].at[j], refs[n + a].at[j - 1], 2 * j) for j in range(1, 4) for a in range(n)]

    bufs, sems, token = _split_call(name, list(chip_sums) + finals, starts=(plan, 3 * n), deps=deps)
    return (n, plan, sems, bufs, None), token


def _pad_rope(p):
    z = jnp.zeros(p.shape[:-1] + (32,), p.dtype)
    return jnp.concatenate([p[..., :32], z, p[..., 32:], z], axis=-1)


def _unpad_rope(p):
    return jnp.concatenate([p[..., :32], p[..., 64:96]], axis=-1)


def _odd_in_layout(wt):
    wt = wt.reshape(ODD_IN, D)
    cq, ckv, kpe, gate = wt[:512], wt[512:1024], wt[1024:1088], wt[1088:]
    z = jnp.zeros((32, D), wt.dtype)
    return jnp.concatenate([gate, cq, ckv, kpe[:32], z, kpe[32:], z], axis=0)


def _odd_in_unlayout(dwt):
    gate, cq, ckv, kpe = dwt[:2048], dwt[2048:2560], dwt[2560:3072], dwt[3072:]
    wt = jnp.concatenate([cq, ckv, kpe[:32], kpe[64:96], gate], axis=0)
    return wt.reshape(N_DEV, ODD_IN // N_DEV, D)


def _qb_layout(w):
    w = w.transpose(1, 0, 2).reshape(C_RANK, C_HEADS, C_QK)
    w = jnp.concatenate([w[..., :C_NOPE], _pad_rope(w[..., C_NOPE:])], axis=-1)
    return w.reshape(C_RANK, C_HEADS * QP)


def _qb_unlayout(dw):
    dw = dw.reshape(C_RANK, C_HEADS, QP)
    dw = jnp.concatenate([dw[..., :C_NOPE], _unpad_rope(dw[..., C_NOPE:])], axis=-1)
    return dw.reshape(C_RANK, N_DEV, C_HEADS * C_QK // N_DEV).transpose(1, 0, 2)


def _rope_tables(positions):
    inv_freq = ROPE_THETA ** (-jnp.arange(0, C_ROPE, 2, dtype=F32) / C_ROPE)
    ang = positions.astype(F32)[0][:, None] * inv_freq
    cos, sin = jnp.cos(ang), jnp.sin(ang)
    z = jnp.zeros_like(cos)
    return jnp.concatenate([cos, z, cos, z], axis=1), jnp.concatenate([-sin, z, sin, z], axis=1)


def _forward_backward(x, cos_t, sin_t, target, norm_pre, norm_post, lb_logits, a_onorm, ln_w, ln_b,
                      b_ws, b_bias, get_w, put_g, put_small=None, start_dep=None):
    npre0, npre1 = norm_pre[0:1], norm_pre[1:2]
    npost0, npost1 = norm_post[0:1], norm_post[1:2]
    l0, l1 = lb_logits[0:1], lb_logits[1:2]
    bias_col = b_bias.reshape(B_GROUPS, B_CHUNK, 1)
    ws = b_ws.reshape(B_GROUPS, B_CHUNK, B_CHUNK)

    h0 = _pre_norm("pre_norm0", x, npre0, deps=[start_dep])
    w_ev_in = get_w("ev_in", h0)
    z0 = _mm_nn("ev_in", h0, w_ev_in, F32, 1024, 896)
    cat, sst = _hgrn2_fwd(z0, l0, l1, a_onorm)
    cat = _gmlp_fwd(z0, cat, ln_w, ln_b, ws, bias_col)
    w_ev_out = get_w("ev_out", cat)
    y0 = _mm_nn("ev_out", cat, w_ev_out, F32, 1024, 1024)
    x1, h1 = _post_pre_norm(x, y0, npost0, npre1)
    w_od_in, w_qb, w_kvb, q_norm, kv_norm = get_w("od_mid", h1)
    z1 = _mm_nt("od_in", h1, w_od_in[None], F32, 1024, 640)
    cqn, ckvn, kp = _mla_pre(z1, q_norm, kv_norm, cos_t, sin_t)
    q = _mm_nn("od_qb", cqn, w_qb[None], F32, 1024, 1024)
    kv = _mm_nn("od_kvb", ckvn, w_kvb, BF16, 1024, 512)
    o, lse = _attention_fwd(q, cos_t, sin_t, kv, kp)
    og = _gate_out(o, z1)
    w_od_out = get_w("od_out", og)
    y1 = _mm_nn("od_out", og, w_od_out, F32, 1024, 1024)
    dx2, dy1, loss_part, dnpost1 = _final_loss(x1, y1, npost1, target)

    g_od_out = _mm_tn("od_out_dw", og, dy1, 1, BF16, 1024, 1024)
    tok = put_g("od_out", [g_od_out.reshape(N_DEV, D // N_DEV, D)])
    dog = _mm_nt("od_out_dx", dy1, w_od_out, F32, 1024, 1024, deps=[tok])
    do, dgate = _gate_out_bwd(o, z1, dog)
    dq, dkv, dkp = _attention_bwd(q, cos_t, sin_t, kv, kp, o, lse, do)
    g_qb = _mm_tn("od_qb_dw", cqn, dq, 1, F32, 512, 1024)
    g_kvb = _mm_tn("od_kvb_dw", ckvn, dkv, N_DEV, BF16, 512, 512)
    tok = put_g("od_qkv", [_qb_unlayout(g_qb[0]).astype(BF16), g_kvb])
    dcqn = _mm_nt("od_qb_dx", dq, w_qb[None], F32, 1024, 512, deps=[tok])
    dckvn = _mm_nt("od_kvb_dx", dkv, w_kvb, F32, 1024, 512)
    dcq, dckv, dkpe, dqn, dkvn = _mla_pre_bwd(z1, q_norm, kv_norm, cos_t, sin_t, dcqn, dckvn, dkp)
    dz1 = jnp.concatenate([dgate, dcq, dckv, dkpe], axis=1)
    g_od_in = _mm_tn("od_in_dw", dz1, h1, 1, F32, 640, 1024)
    tok = put_g("od_in", [_odd_in_unlayout(g_od_in[0]).astype(BF16)])
    dh1 = _mm_nn("od_in_dx", dz1, w_od_in[None], F32, 1024, 1024, deps=[tok])
    dx1, dy0, dnpost0, dnpre1 = _post_pre_norm_bwd(y0, x1, npost0, npre1, dx2, dh1)

    g_ev_out = _mm_tn("ev_out_dw", cat, dy0, 1, BF16, 1024, 1024)
    tok = put_g("ev_out", [g_ev_out.reshape(N_DEV, D // N_DEV, D)])
    dcat = _mm_nt("ev_out_dx", dy0, w_ev_out, F32, 1024, 1024, deps=[tok])
    dqa, dfa, dia, dga, dl0, dl1, donorm = _hgrn2_bwd(z0, l0, l1, a_onorm, sst, dcat)
    dub, dvb, dgb, dlnw, dlnb, dws, dbias = _gmlp_bwd(z0, ln_w, ln_b, ws, bias_col, dcat)
    dz0 = jnp.concatenate([dqa, dfa, dia, dga, dub, dvb, dgb], axis=1)
    early = _small_rows(dnpre1, dnpost0, dnpost1, dl0, dl1, donorm, dlnw, dlnb, dws, dbias, dqn, dkvn, loss_part)
    tok = put_small(early) if put_small else None
    g_ev_in = _mm_tn("ev_in_dw", h0, dz0, N_DEV, BF16, 1024, 896, deps=[tok])
    tok = put_g("ev_in", [g_ev_in])
    dh0 = _mm_nt("ev_in_dx", dz0, w_ev_in, F32, 1024, 1024, deps=[tok])
    grad_x, dnpre0 = _pre_norm_bwd(x, npre0, dh0, dx1)
    return grad_x, early, dnpre0


def kernel(x, positions, norm_pre, norm_post, ev_w_in, ev_lb_logits, ev_a_onorm, ev_b_ln_w, ev_b_ln_b, ev_b_ws, ev_b_bias, ev_w_out, od_w_in, od_q_norm, od_w_qb, od_kv_norm, od_w_kvb, od_w_out, loss_target, m_norm_pre, m_norm_post, m_ev_w_in, m_ev_lb_logits, m_ev_a_onorm, m_ev_b_ln_w, m_ev_b_ln_b, m_ev_b_ws, m_ev_b_bias, m_ev_w_out, m_od_w_in, m_od_q_norm, m_od_w_qb, m_od_kv_norm, m_od_w_kvb, m_od_w_out, v_norm_pre, v_norm_post, v_ev_w_in, v_ev_lb_logits, v_ev_a_onorm, v_ev_b_ln_w, v_ev_b_ln_b, v_ev_b_ws, v_ev_b_bias, v_ev_w_out, v_od_w_in, v_od_q_norm, v_od_w_qb, v_od_kv_norm, v_od_w_kvb, v_od_w_out):
    me = 4 * lax.axis_index("x") + 2 * lax.axis_index("y") + lax.axis_index("c")
    bf = lambda w: w[0].astype(BF16)

    norms = jnp.pad(jnp.concatenate([od_q_norm, od_kv_norm], axis=1), ((0, 7), (0, 0)))
    first_h, tok = _gather2_send("gather_ev_in", [bf(ev_w_in)])
    rest_h, tok = _gather2_send("gather_rest", [bf(ev_w_out), od_w_in[0].T.astype(BF16), bf(od_w_qb), bf(od_w_kvb),
                                                norms, bf(od_w_out)], deps=[tok])
    first_h, tok = _gather2_relay("relay_ev_in", first_h, tok)
    rest = []

    def get_w(group, after):
        if group == "ev_in":
            return _split_done("arrived_ev_in", first_h, after)[0]
        if not rest:
            relayed, token = _gather2_relay("relay_rest", rest_h, after)
            rest.extend(_split_done("arrived_rest", relayed, token))
        w_ev_out, w_od_in, w_qb, w_kvb, norms_all, w_od_out = rest
        if group == "ev_out":
            return w_ev_out.reshape(1, D, D)
        if group == "od_out":
            return w_od_out.reshape(1, D, D)
        return (_odd_in_layout(w_od_in), _qb_layout(w_qb), w_kvb,
                norms_all[:, 0, :64].reshape(1, C_RANK), norms_all[:, 0, 64:].reshape(1, C_RANK))

    scatters = {}

    def put_g(group, grads):
        if group == "ev_in":
            paired, token = _scatter2_pair("pair_ev_in", grads)
            n = len(grads)
            bufs = _split_done("paired_ev_in", paired, token, all_bufs=True)
            chip_sums = [_pair_add("pair_add_ev_in", bufs[a], bufs[n + a], me) for a in range(n)]
            scatters[group], token = _scatter2_send("scatter_ev_in", chip_sums)
        else:
            scatters[group], token = _exchange_start("scatter_" + group, grads, False)
        return token

    def put_small(early):
        scatters["small"], token = _exchange_start("gather_small_early", early, True)
        return token

    cos_t, sin_t = _rope_tables(positions)
    grad_x, _, dnpre0 = _forward_backward(
        x[0], cos_t, sin_t, loss_target[0], norm_pre, norm_post, ev_lb_logits, ev_a_onorm, ev_b_ln_w,
        ev_b_ln_b, ev_b_ws, ev_b_bias, get_w, put_g, put_small, start_dep=tok)

    big_w = {"ev_w_in": ev_w_in, "ev_w_out": ev_w_out, "od_w_in": od_w_in, "od_w_qb": od_w_qb,
             "od_w_kvb": od_w_kvb, "od_w_out": od_w_out}
    big_m = {"ev_w_in": m_ev_w_in, "ev_w_out": m_ev_w_out, "od_w_in": m_od_w_in, "od_w_qb": m_od_w_qb,
             "od_w_kvb": m_od_w_kvb, "od_w_out": m_od_w_out}
    big_v = {"ev_w_in": v_ev_w_in, "ev_w_out": v_ev_w_out, "od_w_in": v_od_w_in, "od_w_qb": v_od_w_qb,
             "od_w_kvb": v_od_w_kvb, "od_w_out": v_od_w_out}
    big_out = {}
    after = grad_x
    for group, names in (("od_out", ["od_w_out"]), ("od_qkv", ["od_w_qb", "od_w_kvb"]), ("od_in", ["od_w_in"]),
                         ("ev_out", ["ev_w_out"])):
        parts = _exchange_wait("summed_" + group, scatters[group], after)
        for nm, p in zip(names, parts):
            w, m, v = big_w[nm][0], big_m[nm][0], big_v[nm][0]
            if nm == "od_w_in":
                res_t = _adamw("adamw_" + nm, [(p, N_DEV)], w.T, m.T, v.T, w.shape[1], 512)
                big_out[nm] = [r.T[None] for r in res_t]
            else:
                big_out[nm] = [r[None] for r in _adamw("adamw_" + nm, [(p, N_DEV)], w, m, v, w.shape[0] // 8)]
            after = big_out[nm][0]

    late_all = _exchange("gather_small_late", [dnpre0], gather=True, deps=[after])[0]
    early_all = _exchange_wait("arrived_small_early", scatters["small"], late_all)

    small_w = (norm_pre, norm_post, ev_lb_logits, ev_a_onorm, ev_b_ln_w, ev_b_ln_b, ev_b_ws, ev_b_bias)
    small_m = (m_norm_pre, m_norm_post, m_ev_lb_logits, m_ev_a_onorm, m_ev_b_ln_w, m_ev_b_ln_b, m_ev_b_ws, m_ev_b_bias)
    small_v = (v_norm_pre, v_norm_post, v_ev_lb_logits, v_ev_a_onorm, v_ev_b_ln_w, v_ev_b_ln_b, v_ev_b_ws, v_ev_b_bias)
    wmv = [tuple(a.reshape(s) for a in t) for s, t in zip(SMALL_PARAM_SHAPES, zip(small_w, small_m, small_v))]
    small_res, loss_row, g_norm_rows = _adamw_small(late_all, early_all, wmv)
    small_out = [[r.reshape(w.shape) for r in four] for four, w in zip(small_res, small_w)]
    loss = loss_row[0, 0]

    g_norms = jnp.concatenate([lax.dynamic_slice(g_norm_rows, (0, 64 * me), (1, 64)),
                               lax.dynamic_slice(g_norm_rows, (1, 64 * me), (1, 64))], axis=1)
    res_n = _adamw("adamw_norms", [(g_norms[None], 1)],
                   jnp.concatenate([od_q_norm, od_kv_norm], axis=1),
                   jnp.concatenate([m_od_q_norm, m_od_kv_norm], axis=1),
                   jnp.concatenate([v_od_q_norm, v_od_kv_norm], axis=1), 1)
    qn_out = [r[:, :64] for r in res_n]
    kvn_out = [r[:, 64:] for r in res_n]

    chip_sum, from_peers = _split_done("summed_ev_in", scatters["ev_in"], loss_row, all_bufs=True)
    w = ev_w_in[0]
    big_out["ev_w_in"] = [r[None] for r in _adamw("adamw_ev_w_in", [(chip_sum, 1), (from_peers, 3)], w, m_ev_w_in[0],
                                                  v_ev_w_in[0], w.shape[0] // 8)]

    order = ("norm_pre", "norm_post", "ev_w_in", "ev_lb_logits", "ev_a_onorm", "ev_b_ln_w", "ev_b_ln_b",
             "ev_b_ws", "ev_b_bias", "ev_w_out", "od_w_in", "od_q_norm", "od_w_qb", "od_kv_norm",
             "od_w_kvb", "od_w_out")
    small_names = ("norm_pre", "norm_post", "ev_lb_logits", "ev_a_onorm", "ev_b_ln_w", "ev_b_ln_b",
                   "ev_b_ws", "ev_b_bias")
    outs = [loss, grad_x[None]]
    for kind in range(4):
        for nm in order:
            if nm in big_out:
                outs.append(big_out[nm][kind])
            elif nm == "od_q_norm":
                outs.append(qn_out[kind])
            elif nm == "od_kv_norm":
                outs.append(kvn_out[kind])
            else:
                outs.append(small_out[small_names.index(nm)][kind])
    return tuple(outs)
```

```python
import functools

import jax
import jax.numpy as jnp
from jax import lax
from jax.experimental import pallas as pl
from jax.experimental.pallas import tpu as pltpu

F32 = jnp.float32
BF16 = jnp.bfloat16

N_DEV = 8
T = 2048
D = 2048
EPS = 1e-6
A_HEADS = 8
HD = 128
A_CHUNK = 64
A_SUB = 16
B_GROUPS = 8
B_CHUNK = 128
EVEN_IN = 7168
C_HEADS = 16
C_RANK = 512
C_NOPE = 128
C_ROPE = 64
C_QK = C_NOPE + C_ROPE
C_V = 128
ODD_IN = 3136
ODD_IN_PAD = 3200
QP = 256
ROPE_THETA = 10000.0
ATT_SCALE = C_QK ** -0.5

ADAM_LR = 0.001
ADAM_B1 = 0.9
ADAM_B2 = 0.999
ADAM_EPS = 1e-08
ADAM_WD = 0.01
ADAM_STEP = 10

VMEM_LIMIT_V7X = 56 * 1024 * 1024
MESH_ID = pl.DeviceIdType.MESH


def _params(n_grid):
    return pltpu.CompilerParams(dimension_semantics=("arbitrary",) * n_grid,
                                vmem_limit_bytes=VMEM_LIMIT_V7X)


def _dg(a, b, ca, cb):
    return lax.dot_general(a.astype(BF16), b.astype(BF16), (((ca,), (cb,)), ((), ())),
                           preferred_element_type=F32)


def _raw_nn(a, b):
    return _dg(a, b, 1, 0)


def _raw_nt(a, b):
    return _dg(a, b, 1, 1)


def _raw_tn(a, b):
    return _dg(a, b, 0, 0)


@jax.custom_vjp
def _dot_nn(a, b):
    return _raw_nn(a, b)


def _dot_nn_fwd(a, b):
    return _raw_nn(a, b), (a.astype(BF16), b.astype(BF16))


def _dot_nn_bwd(res, g):
    a, b = res
    return _raw_nt(g, b), _raw_tn(a, g)


_dot_nn.defvjp(_dot_nn_fwd, _dot_nn_bwd)


@jax.custom_vjp
def _dot_nt(a, b):
    return _raw_nt(a, b)


def _dot_nt_fwd(a, b):
    return _raw_nt(a, b), (a.astype(BF16), b.astype(BF16))


def _dot_nt_bwd(res, g):
    a, b = res
    return _raw_nn(g, b), _raw_tn(g, a)


_dot_nt.defvjp(_dot_nt_fwd, _dot_nt_bwd)


@jax.custom_vjp
def _dot_tn(a, b):
    return _raw_tn(a, b)


def _dot_tn_fwd(a, b):
    return _raw_tn(a, b), (a.astype(BF16), b.astype(BF16))


def _dot_tn_bwd(res, g):
    a, b = res
    return _raw_nt(b, g), _raw_nn(a, g)


_dot_tn.defvjp(_dot_tn_fwd, _dot_tn_bwd)


@jax.custom_vjp
def _sigmoid(x):
    e = jnp.exp(-jnp.abs(x))
    return jnp.where(x >= 0, 1.0 / (1.0 + e), e / (1.0 + e))


def _sigmoid_fwd(x):
    s = _sigmoid(x)
    return s, s


def _sigmoid_bwd(s, g):
    return (g * s * (1.0 - s),)


_sigmoid.defvjp(_sigmoid_fwd, _sigmoid_bwd)


def _silu(x):
    return x * _sigmoid(x)


def _rms(x, w):
    return x * lax.rsqrt(jnp.mean(x * x, axis=-1, keepdims=True) + EPS) * w


def _split3(x):
    hi = x.astype(BF16)
    r = x - hi.astype(F32)
    mid = r.astype(BF16)
    lo = (r - mid.astype(F32)).astype(BF16)
    return hi, mid, lo


def _mask_apply(mask_bf16, x, contract):
    out = None
    for piece in _split3(x):
        d = lax.dot_general(mask_bf16, piece, (((contract,), (0,)), ((), ())),
                            preferred_element_type=F32)
        out = d if out is None else out + d
    return out


def _chunk_tri(rows):
    r = lax.broadcasted_iota(jnp.int32, (rows, rows), 0)
    c = lax.broadcasted_iota(jnp.int32, (rows, rows), 1)
    return ((r >= c) & (r // A_CHUNK == c // A_CHUNK)).astype(BF16)


@jax.custom_vjp
def _chunk_cumsum(x):
    return _mask_apply(_chunk_tri(x.shape[0]), x, 1)


def _chunk_cumsum_fwd(x):
    return _chunk_cumsum(x), None


def _chunk_cumsum_bwd(_, g):
    return (_mask_apply(_chunk_tri(g.shape[0]), g, 0),)


_chunk_cumsum.defvjp(_chunk_cumsum_fwd, _chunk_cumsum_bwd)


def _hgrn2_rows(q, zf, v, ga, st, l0, l1, onorm):
    rows = q.shape[0]
    n_sub = A_CHUNK // A_SUB
    mx = jnp.maximum(l0, l1)
    e0 = jnp.exp(l0 - mx)
    e1 = jnp.exp(l1 - mx)
    lb = e0 / (e0 + e1)
    lf = jnp.log(lb + (1.0 - lb) * _sigmoid(zf))
    k = (1.0 - lb) * _sigmoid(-zf)
    b = _chunk_cumsum(lf)

    t_idx = lax.broadcasted_iota(jnp.int32, (A_CHUNK, n_sub * A_CHUNK), 0)
    c_idx = lax.broadcasted_iota(jnp.int32, (A_CHUNK, n_sub * A_CHUNK), 1)
    sel = (c_idx // A_CHUNK == t_idx // A_SUB) & (c_idx % A_CHUNK <= t_idx)
    key_row = lax.broadcasted_iota(jnp.int32, (A_CHUNK, HD), 0)

    outs = []
    for n in range(rows // A_CHUNK):
        lo = n * A_CHUNK
        qc, kc, vc = q[lo:lo + A_CHUNK], k[lo:lo + A_CHUNK], v[lo:lo + A_CHUNK]
        lfc, bc = lf[lo:lo + A_CHUNK], b[lo:lo + A_CHUNK]
        b_last = bc[A_CHUNK - 1:A_CHUNK]
        o_inter = _dot_nt(qc * jnp.exp(bc), st)
        kv_t = _dot_tn(vc, kc * jnp.exp(b_last - bc))
        st = st * jnp.exp(b_last) + kv_t
        g_rows, k_subs = [], []
        for i in range(n_sub):
            g_i = bc[i * A_SUB:i * A_SUB + 1] - lfc[i * A_SUB:i * A_SUB + 1]
            g_rows.append(jnp.broadcast_to(g_i, (A_SUB, HD)))
            expo = jnp.where(key_row < (i + 1) * A_SUB, g_i - bc, -jnp.inf)
            k_subs.append(kc * jnp.exp(expo))
        q_sub = qc * jnp.exp(bc - jnp.concatenate(g_rows, axis=0))
        scores = _dot_nt(q_sub, jnp.concatenate(k_subs, axis=0))
        scores = jnp.where(sel, scores, 0.0)
        o_intra = _dot_nn(scores, jnp.concatenate([vc] * n_sub, axis=0))
        outs.append(o_inter + o_intra)
    o = jnp.concatenate(outs, axis=0)
    return _rms(o, onorm) * _silu(ga), st


def _gmlp_rows(u, vb, gb, lnw, lnb, ws, bias):
    rows = u.shape[0]
    mu = jnp.mean(vb, axis=-1, keepdims=True)
    xc = vb - mu
    vg = xc * lax.rsqrt(jnp.mean(xc * xc, axis=-1, keepdims=True) + EPS) * lnw + lnb
    r = lax.broadcasted_iota(jnp.int32, (B_CHUNK, B_CHUNK), 0)
    c = lax.broadcasted_iota(jnp.int32, (B_CHUNK, B_CHUNK), 1)
    ws_causal = jnp.where(r >= c, ws, 0.0)
    svs = [_dot_nn(ws_causal, vg[n * B_CHUNK:(n + 1) * B_CHUNK]) + bias
           for n in range(rows // B_CHUNK)]
    return u * jnp.concatenate(svs, axis=0) * _silu(gb)


def _rope(x, cos_t, sin_t):
    return x * cos_t + pltpu.roll(x, 64, 1) * sin_t


def _rope_transpose(g, cos_t, sin_t):
    return g * cos_t + pltpu.roll(g * sin_t, 64, 1)


ANY_SPEC = pl.BlockSpec(memory_space=pl.ANY)


def _live(deps):
    return [d for d in deps if d is not None]


def _skip_deps(body, n_in, n_deps):
    def wrapped(*refs):
        return body(*refs[:n_in], *refs[n_in + n_deps:])
    return wrapped


def _pure_call(name, fn, grid, in_specs, out_specs, out_shape, args, n_acc=0, deps=()):
    deps = _live(deps)
    n_in, n_out, n_deps = len(in_specs), len(out_specs), len(deps)
    in_specs = list(in_specs) + [ANY_SPEC] * n_deps
    args = tuple(args) + tuple(deps)

    def body(*refs):
        res = fn(*[r[...] for r in refs[:n_in]])
        if not isinstance(res, (tuple, list)):
            res = (res,)
        outs = refs[n_in + n_deps:n_in + n_deps + n_out]
        for o, r in zip(outs[:n_out - n_acc], res[:n_out - n_acc]):
            o[...] = r.astype(o.dtype)
        if n_acc:
            first = functools.reduce(jnp.logical_and, [pl.program_id(i) == 0 for i in range(len(grid))])
            for o, r in zip(outs[n_out - n_acc:], res[n_out - n_acc:]):
                @pl.when(first)
                def _(o=o, r=r):
                    o[...] = r.astype(o.dtype)

                @pl.when(jnp.logical_not(first))
                def _(o=o, r=r):
                    o[...] += r.astype(o.dtype)

    return pl.pallas_call(body, name=name, grid=grid, in_specs=in_specs, out_specs=out_specs,
                          out_shape=out_shape, compiler_params=_params(len(grid)))(*args)


def _sds(shape, dtype):
    return jax.ShapeDtypeStruct(shape, dtype)


def _row_spec(tm, width, col=0):
    return pl.BlockSpec((tm, width), lambda i, col=col: (i, col))


def _full_spec(shape):
    nd = len(shape)
    return pl.BlockSpec(shape, lambda *_: (0,) * nd)


def _mm_nn(name, a, b, out_dtype, tm, tn, deps=()):
    deps = _live(deps)
    m, k = a.shape
    j, _, n = b.shape
    per = n // tn

    def body(a_ref, b_ref, o_ref):
        o_ref[...] = _raw_nn(a_ref[...], b_ref[...]).astype(o_ref.dtype)

    return pl.pallas_call(
        _skip_deps(body, 2, len(deps)), name=name, grid=(m // tm, j * per),
        in_specs=[pl.BlockSpec((tm, k), lambda i, c: (i, 0)),
                  pl.BlockSpec((None, k, tn), lambda i, c: (c // per, 0, c % per))] + [ANY_SPEC] * len(deps),
        out_specs=pl.BlockSpec((tm, tn), lambda i, c: (i, c)),
        out_shape=_sds((m, j * n), out_dtype), compiler_params=_params(2))(a, b, *deps)


def _mm_nt(name, a, b, out_dtype, tm, tn, deps=()):
    deps = _live(deps)
    m = a.shape[0]
    j, nn, n = b.shape

    def body(a_ref, b_ref, o_ref, acc_ref):
        part = _raw_nt(a_ref[...], b_ref[...])
        if j == 1:
            o_ref[...] = part.astype(o_ref.dtype)
        else:
            kk = pl.program_id(2)

            @pl.when(kk == 0)
            def _():
                acc_ref[...] = part

            @pl.when(kk > 0)
            def _():
                acc_ref[...] += part

            @pl.when(kk == j - 1)
            def _():
                o_ref[...] = acc_ref[...].astype(o_ref.dtype)

    acc_shape = (tm, tn) if j > 1 else (8, 128)
    return pl.pallas_call(
        _skip_deps(body, 2, len(deps)), name=name, grid=(m // tm, nn // tn, j),
        in_specs=[pl.BlockSpec((tm, n), lambda i, c, kk: (i, kk)),
                  pl.BlockSpec((None, tn, n), lambda i, c, kk: (kk, c, 0))] + [ANY_SPEC] * len(deps),
        out_specs=pl.BlockSpec((tm, tn), lambda i, c, kk: (i, c)),
        out_shape=_sds((m, nn), out_dtype),
        scratch_shapes=[pltpu.VMEM(acc_shape, F32)], compiler_params=_params(3))(a, b, *deps)


def _mm_tn(name, a, b, j, out_dtype, tm, tn, deps=()):
    deps = _live(deps)
    k, m = a.shape
    n = b.shape[1] // j
    per = n // tn

    def body(a_ref, b_ref, o_ref):
        o_ref[...] = _raw_tn(a_ref[...], b_ref[...]).astype(o_ref.dtype)

    return pl.pallas_call(
        _skip_deps(body, 2, len(deps)), name=name, grid=(m // tm, j * per),
        in_specs=[pl.BlockSpec((k, tm), lambda i, c: (0, i)),
                  pl.BlockSpec((k, tn), lambda i, c: (0, c))] + [ANY_SPEC] * len(deps),
        out_specs=pl.BlockSpec((None, tm, tn), lambda i, c: (c // per, i, c % per)),
        out_shape=_sds((j, m, n), out_dtype), compiler_params=_params(2))(a, b, *deps)


TM = 256


def _pre_norm(name, x, w_row, deps=()):
    def fn(xv, w):
        return _rms(xv, w)
    return _pure_call(name, fn, (T // TM,), [_row_spec(TM, D), _full_spec((1, D))],
                      [_row_spec(TM, D)], [_sds((T, D), BF16)], (x, w_row), deps=deps)[0]


def _post_pre_norm(x, y, w_post, w_pre):
    def fn(xv, yv, wp, wn):
        x1 = xv + _rms(yv, wp)
        return x1, _rms(x1, wn)
    return _pure_call("post_pre_norm", fn, (T // TM,),
                      [_row_spec(TM, D), _row_spec(TM, D), _full_spec((1, D)), _full_spec((1, D))],
                      [_row_spec(TM, D), _row_spec(TM, D)],
                      [_sds((T, D), F32), _sds((T, D), BF16)], (x, y, w_post, w_pre))


def _post_pre_norm_bwd(y, x1, w_post, w_pre, dx1_in, dh1, deps=()):
    def fn(yv, x1v, wp, wn, dx1v, dh1v):
        _, vjp_pre = jax.vjp(_rms, x1v, wn)
        dx1_h, dwn = vjp_pre(dh1v)
        dx1 = dx1v + dx1_h
        _, vjp_post = jax.vjp(_rms, yv, wp)
        dy, dwp = vjp_post(dx1)
        return dx1, dy, dwp, dwn
    return _pure_call("post_pre_norm_bwd", fn, (T // TM,),
                      [_row_spec(TM, D), _row_spec(TM, D), _full_spec((1, D)), _full_spec((1, D)),
                       _row_spec(TM, D), _row_spec(TM, D)],
                      [_row_spec(TM, D), _row_spec(TM, D), _full_spec((1, D)), _full_spec((1, D))],
                      [_sds((T, D), F32), _sds((T, D), BF16), _sds((1, D), F32), _sds((1, D), F32)],
                      (y, x1, w_post, w_pre, dx1_in, dh1), n_acc=2, deps=deps)


def _final_loss(x1, y, w_post, target):
    def fn(x1v, yv, wp, tv):
        r, vjp = jax.vjp(_rms, yv, wp)
        err = x1v + r - tv
        part = 0.5 * jnp.sum(jnp.mean(err * err, axis=-1, keepdims=True), axis=0, keepdims=True)
        dx2 = err * (1.0 / D)
        dy, dwp = vjp(dx2)
        return dx2, dy, jnp.broadcast_to(part, (1, 128)), dwp
    return _pure_call("final_loss", fn, (T // TM,),
                      [_row_spec(TM, D), _row_spec(TM, D), _full_spec((1, D)), _row_spec(TM, D)],
                      [_row_spec(TM, D), _row_spec(TM, D), _full_spec((1, 128)), _full_spec((1, D))],
                      [_sds((T, D), F32), _sds((T, D), BF16), _sds((1, 128), F32), _sds((1, D), F32)],
                      (x1, y, w_post, target), n_acc=2)


def _pre_norm_bwd(x, w_row, dh, dx_res, deps=()):
    def fn(xv, w, dhv, dxv):
        _, vjp = jax.vjp(_rms, xv, w)
        dx, dw = vjp(dhv)
        return dxv + dx, dw
    return _pure_call("pre_norm_bwd", fn, (T // TM,),
                      [_row_spec(TM, D), _full_spec((1, D)), _row_spec(TM, D), _row_spec(TM, D)],
                      [_row_spec(TM, D), _full_spec((1, D))],
                      [_sds((T, D), F32), _sds((1, D), F32)], (x, w_row, dh, dx_res), n_acc=1, deps=deps)


RA = 256
RB = 512


def _col_spec(rows, col_of):
    return pl.BlockSpec((rows, HD), lambda h, r, col_of=col_of: (r, col_of(h)))


def _hgrn2_fwd(z, l0, l1, onorm):
    nb = T // RA

    def body(q_ref, f_ref, v_ref, g_ref, l0_ref, l1_ref, on_ref, cat_ref, sst_ref, st_scr):
        @pl.when(pl.program_id(1) == 0)
        def _():
            st_scr[...] = jnp.zeros_like(st_scr)

        st = st_scr[...]
        sst_ref[...] = st
        out, st_new = _hgrn2_rows(q_ref[...], f_ref[...], v_ref[...], g_ref[...], st,
                                  l0_ref[...], l1_ref[...], on_ref[...])
        cat_ref[...] = out.astype(cat_ref.dtype)
        st_scr[...] = st_new

    vec = pl.BlockSpec((1, HD), lambda h, r: (0, h))
    return pl.pallas_call(
        body, name="hgrn2_fwd", grid=(A_HEADS, nb),
        in_specs=[_col_spec(RA, lambda h: h), _col_spec(RA, lambda h: 8 + h),
                  _col_spec(RA, lambda h: 16 + h), _col_spec(RA, lambda h: 24 + h),
                  vec, vec, _full_spec((1, HD))],
        out_specs=[_col_spec(RA, lambda h: h),
                   pl.BlockSpec((None, None, HD, HD), lambda h, r: (h, r, 0, 0))],
        out_shape=[_sds((T, 2 * A_HEADS * HD), BF16), _sds((A_HEADS, nb, HD, HD), F32)],
        scratch_shapes=[pltpu.VMEM((HD, HD), F32)],
        compiler_params=_params(2))(z, z, z, z, l0, l1, onorm)


def _hgrn2_bwd(z, l0, l1, onorm, sst, dcat, deps=()):
    nb = T // RA
    deps = _live(deps)

    def body(q_ref, f_ref, v_ref, g_ref, l0_ref, l1_ref, on_ref, sst_ref, dcat_ref,
             dq_ref, df_ref, dv_ref, dg_ref, dl0_ref, dl1_ref, don_ref, ds_scr):
        h, r = pl.program_id(0), pl.program_id(1)

        @pl.when(r == 0)
        def _():
            ds_scr[...] = jnp.zeros_like(ds_scr)

        _, vjp = jax.vjp(_hgrn2_rows, q_ref[...], f_ref[...], v_ref[...], g_ref[...], sst_ref[...],
                         l0_ref[...], l1_ref[...], on_ref[...])
        dq, dzf, dv, dga, dst, dl0, dl1, don = vjp((dcat_ref[...], ds_scr[...]))
        dq_ref[...] = dq.astype(dq_ref.dtype)
        df_ref[...] = dzf.astype(df_ref.dtype)
        dv_ref[...] = dv.astype(dv_ref.dtype)
        dg_ref[...] = dga.astype(dg_ref.dtype)
        ds_scr[...] = dst

        @pl.when(r == 0)
        def _():
            dl0_ref[...] = dl0
            dl1_ref[...] = dl1

        @pl.when(r > 0)
        def _():
            dl0_ref[...] += dl0
            dl1_ref[...] += dl1

        first = jnp.logical_and(h == 0, r == 0)

        @pl.when(first)
        def _():
            don_ref[...] = don

        @pl.when(jnp.logical_not(first))
        def _():
            don_ref[...] += don

    def rev(col_of):
        return pl.BlockSpec((RA, HD), lambda h, r, col_of=col_of: (nb - 1 - r, col_of(h)))

    vec = pl.BlockSpec((1, HD), lambda h, r: (0, h))
    grad = _sds((T, A_HEADS * HD), BF16)
    return pl.pallas_call(
        _skip_deps(body, 9, len(deps)), name="hgrn2_bwd", grid=(A_HEADS, nb),
        in_specs=[rev(lambda h: h), rev(lambda h: 8 + h), rev(lambda h: 16 + h), rev(lambda h: 24 + h),
                  vec, vec, _full_spec((1, HD)),
                  pl.BlockSpec((None, None, HD, HD), lambda h, r: (h, nb - 1 - r, 0, 0)),
                  rev(lambda h: h)] + [ANY_SPEC] * len(deps),
        out_specs=[rev(lambda h: h)] * 4 + [vec, vec, _full_spec((1, HD))],
        out_shape=[grad] * 4 + [_sds((1, A_HEADS * HD), F32)] * 2 + [_sds((1, HD), F32)],
        scratch_shapes=[pltpu.VMEM((HD, HD), F32)],
        compiler_params=_params(2))(z, z, z, z, l0, l1, onorm, sst, dcat, *deps)


def _gmlp_specs():
    vec = pl.BlockSpec((1, HD), lambda g, r: (0, g))
    ws = pl.BlockSpec((None, B_CHUNK, B_CHUNK), lambda g, r: (g, 0, 0))
    bias = pl.BlockSpec((None, B_CHUNK, 1), lambda g, r: (g, 0, 0))
    return vec, ws, bias


def _gmlp_fwd(z, cat, lnw, lnb, ws, bias):
    vec, ws_spec, bias_spec = _gmlp_specs()

    def body(u_ref, v_ref, g_ref, lnw_ref, lnb_ref, ws_ref, bias_ref, cat_in_ref, cat_ref):
        del cat_in_ref
        out = _gmlp_rows(u_ref[...], v_ref[...], g_ref[...], lnw_ref[...], lnb_ref[...],
                         ws_ref[...], bias_ref[...])
        cat_ref[...] = out.astype(cat_ref.dtype)

    return pl.pallas_call(
        body, name="gmlp_fwd", grid=(B_GROUPS, T // RB),
        in_specs=[_col_spec(RB, lambda g: 32 + g), _col_spec(RB, lambda g: 40 + g),
                  _col_spec(RB, lambda g: 48 + g), vec, vec, ws_spec, bias_spec,
                  pl.BlockSpec(memory_space=pl.ANY)],
        out_specs=_col_spec(RB, lambda g: A_HEADS + g),
        out_shape=_sds(cat.shape, cat.dtype), input_output_aliases={7: 0},
        compiler_params=_params(2))(z, z, z, lnw, lnb, ws, bias, cat)


def _gmlp_bwd(z, lnw, lnb, ws, bias, dcat):
    vec, ws_spec, bias_spec = _gmlp_specs()

    def fn(u, vb, gb, w, b, wsv, bv, dout):
        _, vjp = jax.vjp(_gmlp_rows, u, vb, gb, w, b, wsv, bv)
        return vjp(dout)

    def body(*refs):
        ins, outs = refs[:8], refs[8:]
        res = fn(*[r[...] for r in ins])
        for o, r in zip(outs[:3], res[:3]):
            o[...] = r.astype(o.dtype)
        first = pl.program_id(1) == 0
        for o, r in zip(outs[3:], res[3:]):
            @pl.when(first)
            def _(o=o, r=r):
                o[...] = r

            @pl.when(jnp.logical_not(first))
            def _(o=o, r=r):
                o[...] += r

    grad = _sds((T, B_GROUPS * HD), BF16)
    row_out = pl.BlockSpec((RB, HD), lambda g, r: (r, g))
    return pl.pallas_call(
        body, name="gmlp_bwd", grid=(B_GROUPS, T // RB),
        in_specs=[_col_spec(RB, lambda g: 32 + g), _col_spec(RB, lambda g: 40 + g),
                  _col_spec(RB, lambda g: 48 + g), vec, vec, ws_spec, bias_spec,
                  _col_spec(RB, lambda g: A_HEADS + g)],
        out_specs=[row_out] * 3 + [vec, vec, ws_spec, bias_spec],
        out_shape=[grad] * 3 + [_sds((1, B_GROUPS * HD), F32)] * 2
        + [_sds((B_GROUPS, B_CHUNK, B_CHUNK), F32), _sds((B_GROUPS, B_CHUNK, 1), F32)],
        compiler_params=_params(2))(z, z, z, lnw, lnb, ws, bias, dcat)


def _mla_pre(z1, qn, kvn, cos_t, sin_t):
    def fn(cq, ckv, kpe, cs, sn, wq, wkv):
        return _rms(cq, wq), _rms(ckv, wkv), _rope(kpe, cs, sn)
    return _pure_call("mla_pre", fn, (T // TM,),
                      [_row_spec(TM, C_RANK, 4), _row_spec(TM, C_RANK, 5), _row_spec(TM, HD, 24),
                       _row_spec(TM, HD), _row_spec(TM, HD),
                       _full_spec((1, C_RANK)), _full_spec((1, C_RANK))],
                      [_row_spec(TM, C_RANK), _row_spec(TM, C_RANK), _row_spec(TM, HD)],
                      [_sds((T, C_RANK), BF16), _sds((T, C_RANK), BF16), _sds((T, HD), BF16)],
                      (z1, z1, z1, cos_t, sin_t, qn, kvn))


def _mla_pre_bwd(z1, qn, kvn, cos_t, sin_t, dcqn, dckvn, dkp, deps=()):
    def fn(cq, ckv, cs, sn, wq, wkv, g_q, g_kv, g_kp):
        _, vjp_q = jax.vjp(_rms, cq, wq)
        dcq, dwq = vjp_q(g_q)
        _, vjp_kv = jax.vjp(_rms, ckv, wkv)
        dckv, dwkv = vjp_kv(g_kv)
        return dcq, dckv, _rope_transpose(g_kp, cs, sn), dwq, dwkv
    return _pure_call("mla_pre_bwd", fn, (T // TM,),
                      [_row_spec(TM, C_RANK, 4), _row_spec(TM, C_RANK, 5),
                       _row_spec(TM, HD), _row_spec(TM, HD),
                       _full_spec((1, C_RANK)), _full_spec((1, C_RANK)),
                       _row_spec(TM, C_RANK), _row_spec(TM, C_RANK), _row_spec(TM, HD)],
                      [_row_spec(TM, C_RANK), _row_spec(TM, C_RANK), _row_spec(TM, HD),
                       _full_spec((1, C_RANK)), _full_spec((1, C_RANK))],
                      [_sds((T, C_RANK), BF16), _sds((T, C_RANK), BF16), _sds((T, HD), BF16),
                       _sds((1, C_RANK), F32), _sds((1, C_RANK), F32)],
                      (z1, z1, cos_t, sin_t, qn, kvn, dcqn, dckvn, dkp), n_acc=2, deps=deps)


def _gate_out(o, z1):
    def fn(ov, gate):
        return ov * _silu(gate)
    return _pure_call("gate_out", fn, (T // TM,), [_row_spec(TM, D), _row_spec(TM, D, 0)],
                      [_row_spec(TM, D)], [_sds((T, D), BF16)], (o, z1))[0]


def _gate_out_bwd(o, z1, dog, deps=()):
    def fn(ov, gate, g):
        _, vjp = jax.vjp(lambda a, b: a * _silu(b), ov, gate)
        return vjp(g)
    return _pure_call("gate_out_bwd", fn, (T // TM,),
                      [_row_spec(TM, D), _row_spec(TM, D, 0), _row_spec(TM, D)],
                      [_row_spec(TM, D), _row_spec(TM, D)],
                      [_sds((T, D), F32), _sds((T, D), BF16)], (o, z1, dog), deps=deps)


TQ = 256
HP = 2
KVW = C_NOPE + C_V


def _att_keys(kv_ref, kp_ref, k_scr):
    @pl.when(pl.program_id(1) == 0)
    def _():
        for hh in range(HP):
            k_scr[hh, :, 0:C_NOPE] = kv_ref[:, hh * KVW:hh * KVW + C_NOPE]
            k_scr[hh, :, C_NOPE:QP] = kp_ref[...]


def _att_scores(q, cos_ref, sin_ref, k_scr, hh, n):
    keys = (n + 1) * TQ
    qr = jnp.concatenate([q[:, :C_NOPE], _rope(q[:, C_NOPE:], cos_ref[...], sin_ref[...])], axis=1).astype(BF16)
    return qr, _raw_nt(qr, k_scr[hh, 0:keys, :]) * ATT_SCALE


def _causal(x, n, fill):
    row = lax.broadcasted_iota(jnp.int32, (TQ, TQ), 0)
    col = lax.broadcasted_iota(jnp.int32, (TQ, TQ), 1)
    diag = jnp.where(col <= row, x[:, n * TQ:], fill)
    return diag if n == 0 else jnp.concatenate([x[:, :n * TQ], diag], axis=1)


def _per_query_block(fn):
    for n in range(T // TQ):
        pl.when(pl.program_id(1) == n)(functools.partial(fn, n))


def _att_in_specs():
    return [pl.BlockSpec((TQ, HP * QP), lambda g, i: (i, g)),
            pl.BlockSpec((TQ, HD), lambda g, i: (i, 0)),
            pl.BlockSpec((TQ, HD), lambda g, i: (i, 0)),
            pl.BlockSpec((T, HP * KVW), lambda g, i: (0, g)),
            pl.BlockSpec((T, HD), lambda g, i: (0, 0))]


def _attention_fwd(q, cos_t, sin_t, kv, kp):
    def body(q_ref, cos_ref, sin_ref, kv_ref, kp_ref, o_ref, lse_ref, k_scr):
        _att_keys(kv_ref, kp_ref, k_scr)

        def block(n):
            keys = (n + 1) * TQ
            for hh in range(HP):
                _, s = _att_scores(q_ref[:, hh * QP:(hh + 1) * QP], cos_ref, sin_ref, k_scr, hh, n)
                s = _causal(s, n, jnp.finfo(F32).min)
                m = jnp.max(s, axis=-1, keepdims=True)
                p = jnp.exp(s - m)
                l = jnp.sum(p, axis=-1, keepdims=True)
                v = kv_ref[0:keys, hh * KVW + C_NOPE:(hh + 1) * KVW]
                o_ref[:, hh * C_V:(hh + 1) * C_V] = _raw_nn(p, v) / l
                lse_ref[hh] = m + jnp.log(l)

        _per_query_block(block)

    return pl.pallas_call(
        body, name="attention_fwd", grid=(C_HEADS // HP, T // TQ), in_specs=_att_in_specs(),
        out_specs=[pl.BlockSpec((TQ, HP * C_V), lambda g, i: (i, g)),
                   pl.BlockSpec((HP, TQ, 1), lambda g, i: (g, i, 0))],
        out_shape=[_sds((T, C_HEADS * C_V), F32), _sds((C_HEADS, T, 1), F32)],
        scratch_shapes=[pltpu.VMEM((HP, T, QP), BF16)],
        compiler_params=_params(2))(q, cos_t, sin_t, kv, kp)


def _attention_bwd(q, cos_t, sin_t, kv, kp, o, lse, do):
    nq = T // TQ

    def body(q_ref, cos_ref, sin_ref, kv_ref, kp_ref, o_ref, lse_ref, do_ref,
             dq_ref, dkv_ref, dkp_ref, k_scr, dk_scr, dv_scr):
        g, i = pl.program_id(0), pl.program_id(1)
        _att_keys(kv_ref, kp_ref, k_scr)

        @pl.when(i == 0)
        def _():
            dv_scr[...] = jnp.zeros_like(dv_scr)
            dk_scr[...] = jnp.zeros_like(dk_scr)

        def block(n):
            keys = (n + 1) * TQ
            for hh in range(HP):
                qr, s = _att_scores(q_ref[:, hh * QP:(hh + 1) * QP], cos_ref, sin_ref, k_scr, hh, n)
                p = _causal(jnp.exp(s - lse_ref[hh]), n, 0.0)
                dov = do_ref[:, hh * C_V:(hh + 1) * C_V]
                delta = jnp.sum(dov * o_ref[:, hh * C_V:(hh + 1) * C_V], axis=-1, keepdims=True)
                dp = _raw_nt(dov, kv_ref[0:keys, hh * KVW + C_NOPE:(hh + 1) * KVW])
                ds = p * (dp - delta) * ATT_SCALE
                dq = _raw_nn(ds, k_scr[hh, 0:keys, :])
                dq_ref[:, hh * QP:(hh + 1) * QP] = jnp.concatenate(
                    [dq[:, :C_NOPE], _rope_transpose(dq[:, C_NOPE:], cos_ref[...], sin_ref[...])],
                    axis=1).astype(dq_ref.dtype)
                dv_scr[hh, 0:keys, :] += _raw_tn(p, dov)
                dk_scr[hh, 0:keys, :] += _raw_tn(ds, qr)

        _per_query_block(block)

        @pl.when(i == nq - 1)
        def _():
            for hh in range(HP):
                dkv_ref[:, hh * KVW:(hh + 1) * KVW] = jnp.concatenate(
                    [dk_scr[hh, :, 0:C_NOPE], dv_scr[hh]], axis=1).astype(dkv_ref.dtype)

        @pl.when(jnp.logical_and(i == nq - 1, g == 0))
        def _():
            dkp_ref[...] = dk_scr[0, :, C_NOPE:QP]

        @pl.when(jnp.logical_and(i == nq - 1, g > 0))
        def _():
            dkp_ref[...] += dk_scr[0, :, C_NOPE:QP]

        @pl.when(i == nq - 1)
        def _():
            for hh in range(1, HP):
                dkp_ref[...] += dk_scr[hh, :, C_NOPE:QP]

    return pl.pallas_call(
        body, name="attention_bwd", grid=(C_HEADS // HP, nq),
        in_specs=_att_in_specs() + [pl.BlockSpec((TQ, HP * C_V), lambda g, i: (i, g)),
                                    pl.BlockSpec((HP, TQ, 1), lambda g, i: (g, i, 0)),
                                    pl.BlockSpec((TQ, HP * C_V), lambda g, i: (i, g))],
        out_specs=[pl.BlockSpec((TQ, HP * QP), lambda g, i: (i, g)),
                   pl.BlockSpec((T, HP * KVW), lambda g, i: (0, g)),
                   _full_spec((T, HD))],
        out_shape=[_sds((T, C_HEADS * QP), BF16), _sds((T, C_HEADS * KVW), BF16), _sds((T, HD), F32)],
        scratch_shapes=[pltpu.VMEM((HP, T, QP), BF16), pltpu.VMEM((HP, T, QP), F32), pltpu.VMEM((HP, T, C_V), F32)],
        compiler_params=_params(2))(q, cos_t, sin_t, kv, kp, o, lse, do)


def _adamw_math(w, g, m, v):
    m = ADAM_B1 * m + (1.0 - ADAM_B1) * g
    v = ADAM_B2 * v + (1.0 - ADAM_B2) * (g * g)
    m_hat = m / (1.0 - ADAM_B1 ** ADAM_STEP)
    v_hat = v / (1.0 - ADAM_B2 ** ADAM_STEP)
    delta = -ADAM_LR * (m_hat / (jnp.sqrt(v_hat) + ADAM_EPS) + ADAM_WD * w)
    return delta, m, v


def _adamw(name, parts, w, m, v, tr, tc=None):
    rows, cols = w.shape

    def fn(*vals):
        pvs, (wv, mv, vv) = vals[:len(parts)], vals[len(parts):]
        g = None
        for pv in pvs:
            for d in range(pv.shape[0]):
                term = pv[d].astype(F32)
                g = term if g is None else g + term
        return (g,) + _adamw_math(wv, g, mv, vv)

    tc = cols if tc is None else tc
    blk = pl.BlockSpec((tr, tc), lambda i, j: (i, j))
    part_specs = [pl.BlockSpec((n, tr, tc), lambda i, j: (0, i, j)) for _, n in parts]
    return _pure_call(name, fn, (rows // tr, cols // tc), part_specs + [blk, blk, blk],
                      [blk] * 4, [_sds((rows, cols), F32)] * 4, tuple(p for p, _ in parts) + (w, m, v))


SMALL_PARAM_SHAPES = ((2, D), (2, D), (2, A_HEADS * HD), (1, HD), (1, B_GROUPS * HD), (1, B_GROUPS * HD),
                      (B_GROUPS, B_CHUNK, B_CHUNK), (B_GROUPS, B_CHUNK))
SMALL_PIECES = ((0, 0, 0, 0), (0, 1, 1, 0), (1, 0, 1, 1), (1, 1, 1, 2), (2, 0, 2, 0), (2, 1, 2, 1),
                (3, 0, 3, 8), (4, 0, 2, 2), (5, 0, 2, 3))


def _small_rows(dnpre1, dnpost0, dnpost1, dl0, dl1, donorm, dlnw, dlnb, dws, dbias, dqn, dkvn, loss_part):
    return [jnp.concatenate([dnpre1, dnpost0, dnpost1], axis=0),
            jnp.concatenate([dl0, dl1, dlnw, dlnb], axis=0),
            jnp.concatenate([dbias.reshape(B_GROUPS, B_CHUNK), donorm, loss_part], axis=0),
            dws,
            jnp.concatenate([dqn, dkvn], axis=0)]


def _adamw_small(late_all, early_all, wmv):
    n_in = 6 + 3 * len(wmv)

    def body(*refs):
        gathered, params, outs = refs[:6], refs[6:n_in], refs[n_in:]

        def total(ref):
            s = ref[0]
            for d in range(1, N_DEV):
                s = s + ref[d]
            return s

        g_late, g2048, g1024, g128, g_ws, g512 = [total(r) for r in gathered]
        arrays = (g_late, g2048, g1024, g128)

        def update(p, rows, g):
            w_ref, m_ref, v_ref = params[3 * p:3 * p + 3]
            delta, m, v = _adamw_math(w_ref[rows], g, m_ref[rows], v_ref[rows])
            for out, val in zip(outs[4 * p:4 * p + 4], (g, delta, m, v)):
                out[rows] = val

        for p, row, arr, arr_row in SMALL_PIECES:
            update(p, pl.ds(row, 1), arrays[arr][arr_row:arr_row + 1])
        update(6, slice(None), g_ws)
        update(7, slice(None), g128[0:B_GROUPS])
        outs[32][...] = g128[B_GROUPS + 1:B_GROUPS + 2]
        outs[33][...] = g512

    vmem = pl.BlockSpec(memory_space=pltpu.VMEM)
    flat = [a for t in wmv for a in t]
    out_shape = [_sds(s, F32) for s in SMALL_PARAM_SHAPES for _ in range(4)] + [_sds((1, 128), F32), _sds((2, C_RANK), F32)]
    res = pl.pallas_call(body, name="adamw_small", in_specs=[vmem] * n_in, out_specs=[vmem] * len(out_shape),
                         out_shape=out_shape,
                         compiler_params=pltpu.CompilerParams(vmem_limit_bytes=VMEM_LIMIT_V7X))(late_all, *early_all, *flat)
    return [res[4 * p:4 * p + 4] for p in range(8)], res[32], res[33]


def _exchange(name, arrs, gather, deps=()):
    n = len(arrs)
    deps = _live(deps)

    def body(*refs):
        ins, outs = refs[:n], refs[n + len(deps):2 * n + len(deps)]
        send_sems, recv_sems, local_sems = refs[2 * n + len(deps):]
        x, y, c = lax.axis_index("x"), lax.axis_index("y"), lax.axis_index("c")
        me = 4 * x + 2 * y + c

        def peer(k):
            return (x ^ (k >> 2), y ^ ((k >> 1) & 1), c ^ (k & 1))

        def copy(a, k):
            src = ins[a] if gather else ins[a].at[me ^ k]
            return pltpu.make_async_remote_copy(
                src_ref=src, dst_ref=outs[a].at[me], send_sem=send_sems.at[a, k - 1],
                recv_sem=recv_sems.at[a, k - 1], device_id=peer(k), device_id_type=MESH_ID)

        def arrival(a, k):
            src = ins[a] if gather else ins[a].at[me]
            return pltpu.make_async_remote_copy(
                src_ref=src, dst_ref=outs[a].at[me ^ k], send_sem=send_sems.at[a, k - 1],
                recv_sem=recv_sems.at[a, k - 1], device_id=peer(k), device_id_type=MESH_ID)

        own = [pltpu.make_async_copy(ins[a] if gather else ins[a].at[me], outs[a].at[me], local_sems.at[a])
               for a in range(n)]
        for cp in own:
            cp.start()
        for k in range(1, N_DEV):
            for a in range(n):
                copy(a, k).start()
        for k in range(1, N_DEV):
            for a in range(n):
                arrival(a, k).wait_recv()
        for k in range(1, N_DEV):
            for a in range(n):
                copy(a, k).wait_send()
        for cp in own:
            cp.wait()

    any_spec = pl.BlockSpec(memory_space=pl.ANY)
    out_shape = [_sds((N_DEV,) + a.shape if gather else a.shape, a.dtype) for a in arrs]
    return pl.pallas_call(
        body, name=name, in_specs=[any_spec] * (n + len(deps)), out_specs=[any_spec] * n, out_shape=out_shape,
        scratch_shapes=[pltpu.SemaphoreType.DMA((n, N_DEV - 1)), pltpu.SemaphoreType.DMA((n, N_DEV - 1)),
                        pltpu.SemaphoreType.DMA((n,))],
        compiler_params=pltpu.CompilerParams(has_side_effects=True))(*arrs, *deps)


HBM_SPEC = pl.BlockSpec(memory_space=pltpu.HBM)
SEM_SPEC = pl.BlockSpec(memory_space=pltpu.SEMAPHORE)
DATAFLOW = pltpu.SideEffectType.DATAFLOW_SIDE_EFFECTING


def _my_index():
    return 4 * lax.axis_index("x") + 2 * lax.axis_index("y") + lax.axis_index("c")


def _plan_copies(plan, refs, send_sems, recv_sems):
    x, y, c = lax.axis_index("x"), lax.axis_index("y"), lax.axis_index("c")
    return [pltpu.make_async_remote_copy(
        src_ref=src, dst_ref=dst, send_sem=send_sems.at[i], recv_sem=recv_sems.at[i],
        device_id=(x ^ (k >> 2), y ^ ((k >> 1) & 1), c ^ (k & 1)), device_id_type=MESH_ID)
        for i, (src, dst, k) in enumerate(plan(refs, 4 * x + 2 * y + c))]


def _split_call(name, bufs, waits=None, starts=None, deps=()):
    n = len(bufs)
    deps = _live(deps)
    n_wait = 2 if waits else 0

    def body(*refs):
        zones = refs[:n]
        if waits:
            for cp in _plan_copies(waits[2], zones, refs[n], refs[n + 1]):
                cp.wait_send()
                cp.wait_recv()
        if starts:
            first_out = n + n_wait + len(deps)
            for cp in _plan_copies(starts[0], zones, refs[first_out], refs[first_out + 1]):
                cp.start()
            refs[-1][...] = jnp.zeros_like(refs[-1])

    out_specs, out_shape = [], []
    if starts:
        sems = pltpu.SemaphoreType.DMA((starts[1],))
        out_specs, out_shape = [SEM_SPEC, SEM_SPEC], [sems, sems]
    out_specs += [HBM_SPEC] * n
    out_shape += [pltpu.HBM(b.shape, b.dtype) for b in bufs]
    if starts:
        out_specs.append(pl.BlockSpec(memory_space=pltpu.VMEM))
        out_shape.append(_sds((8, 128), F32))
    first_buf = 2 if starts else 0
    res = pl.pallas_call(
        body, name=name,
        in_specs=[HBM_SPEC] * n + [SEM_SPEC] * n_wait + [ANY_SPEC] * len(deps),
        out_specs=out_specs, out_shape=out_shape,
        input_output_aliases={i: first_buf + i for i in range(n)},
        compiler_params=pltpu.CompilerParams(has_side_effects=DATAFLOW),
    )(*[pltpu.with_memory_space_constraint(b, pltpu.HBM) for b in bufs], *(waits[:2] if waits else ()), *deps)
    out_bufs = list(res[first_buf:first_buf + n])
    return out_bufs, ((res[0], res[1]) if starts else None), (res[-1] if starts else None)


def _direct_plan(n, gather):
    def plan(refs, me):
        return [(refs[a] if gather else refs[a].at[me ^ k], refs[n + a].at[me], k)
                for k in range(1, N_DEV) for a in range(n)]
    return plan


def _own_slot_filled(a, gather):
    me = _my_index()
    if gather:
        return lax.dynamic_update_slice_in_dim(lax.empty((N_DEV,) + a.shape, a.dtype), a[None], me, 0)
    return lax.dynamic_update_slice_in_dim(lax.empty(a.shape, a.dtype), lax.dynamic_slice_in_dim(a, me, 1, 0), me, 0)


def _exchange_start(name, arrs, gather, deps=()):
    n = len(arrs)
    lands = [_own_slot_filled(a, gather) for a in arrs]
    plan = _direct_plan(n, gather)
    bufs, sems, token = _split_call(name, list(arrs) + lands, starts=(plan, n * (N_DEV - 1)), deps=deps)
    return (n, plan, sems, bufs, None), token


def _exchange_wait(name, handle, after):
    return _split_done(name, handle, after)


ICI_PEERS = (2, 4, 6)
SIBLING = 1


def _gather2_send(name, arrs, deps=()):
    n = len(arrs)
    lands = [_own_slot_filled(a, True) for a in arrs]

    def plan(refs, me_):
        return [(refs[a], refs[n + a].at[me_], k) for k in (SIBLING,) + ICI_PEERS for a in range(n)]

    bufs, sems, token = _split_call(name, list(arrs) + lands, starts=(plan, 4 * n), deps=deps)
    return (n, plan, sems, bufs, None), token


def _gather2_relay(name, handle, after):
    n, plan, sems, bufs, _ = handle

    def relay(refs, me_):
        return [(refs[n + a].at[me_ ^ k], refs[n + a].at[me_ ^ k], SIBLING) for k in ICI_PEERS for a in range(n)]

    bufs, sems2, token = _split_call(name, bufs, waits=(sems[0], sems[1], plan), starts=(relay, 3 * n), deps=[after])
    return (n, relay, sems2, bufs, None), token


def _split_done(name, handle, after, all_bufs=False):
    n, plan, sems, bufs, _ = handle
    bufs, _, _ = _split_call(name, bufs, waits=(sems[0], sems[1], plan), deps=[after])
    return bufs if all_bufs else bufs[n:]


def _scatter2_pair(name, stacks, deps=()):
    n = len(stacks)
    pairs = [lax.empty((4,) + s.shape[1:], s.dtype) for s in stacks]

    def plan(refs, me):
        return [(refs[a].at[(me ^ SIBLING) ^ (2 * j)], refs[n + a].at[j], SIBLING) for j in range(4) for a in range(n)]

    bufs, sems, token = _split_call(name, list(stacks) + pairs, starts=(plan, 4 * n), deps=deps)
    return (n, plan, sems, bufs, None), token


def _pair_add(name, stack, pair, me):
    _, rows, cols = stack.shape
    tr = rows // 2

    def body(me_ref, s_ref, p_ref, o_ref):
        del me_ref
        o_ref[...] = (s_ref[...].astype(F32) + p_ref[...].astype(F32)).astype(o_ref.dtype)

    grid_spec = pltpu.PrefetchScalarGridSpec(
        num_scalar_prefetch=1, grid=(4, rows // tr),
        in_specs=[pl.BlockSpec((None, tr, cols), lambda j, i, me_ref: (me_ref[0] ^ (2 * j), i, 0)),
                  pl.BlockSpec((None, tr, cols), lambda j, i, me_ref: (j, i, 0))],
        out_specs=pl.BlockSpec((None, tr, cols), lambda j, i, me_ref: (j, i, 0)))
    return pl.pallas_call(body, name=name, grid_spec=grid_spec, out_shape=_sds((4, rows, cols), stack.dtype),
                          compiler_params=_params(2))(me.reshape(1).astype(jnp.int32), stack, pair)


def _scatter2_send(name, chip_sums, deps=()):
    n = len(chip_sums)
    finals = [lax.empty((3,) + c.shape[1:], c.dtype) for c in chip_sums]

    def plan(refs, me):
        del me
        return [(refs[a].at[j], refs[n + a].at[j - 1], 2 * j) for j in range(1, 4) for a in range(n)]

    bufs, sems, token = _split_call(name, list(chip_sums) + finals, starts=(plan, 3 * n), deps=deps)
    return (n, plan, sems, bufs, None), token


def _pad_rope(p):
    z = jnp.zeros(p.shape[:-1] + (32,), p.dtype)
    return jnp.concatenate([p[..., :32], z, p[..., 32:], z], axis=-1)


def _unpad_rope(p):
    return jnp.concatenate([p[..., :32], p[..., 64:96]], axis=-1)


def _odd_in_layout(wt):
    wt = wt.reshape(ODD_IN, D)
    cq, ckv, kpe, gate = wt[:512], wt[512:1024], wt[1024:1088], wt[1088:]
    z = jnp.zeros((32, D), wt.dtype)
    return jnp.concatenate([gate, cq, ckv, kpe[:32], z, kpe[32:], z], axis=0)


def _odd_in_unlayout(dwt):
    gate, cq, ckv, kpe = dwt[:2048], dwt[2048:2560], dwt[2560:3072], dwt[3072:]
    wt = jnp.concatenate([cq, ckv, kpe[:32], kpe[64:96], gate], axis=0)
    return wt.reshape(N_DEV, ODD_IN // N_DEV, D)


def _qb_layout(w):
    w = w.transpose(1, 0, 2).reshape(C_RANK, C_HEADS, C_QK)
    w = jnp.concatenate([w[..., :C_NOPE], _pad_rope(w[..., C_NOPE:])], axis=-1)
    return w.reshape(C_RANK, C_HEADS * QP)


def _qb_unlayout(dw):
    dw = dw.reshape(C_RANK, C_HEADS, QP)
    dw = jnp.concatenate([dw[..., :C_NOPE], _unpad_rope(dw[..., C_NOPE:])], axis=-1)
    return dw.reshape(C_RANK, N_DEV, C_HEADS * C_QK // N_DEV).transpose(1, 0, 2)


def _rope_tables(positions):
    inv_freq = ROPE_THETA ** (-jnp.arange(0, C_ROPE, 2, dtype=F32) / C_ROPE)
    ang = positions.astype(F32)[0][:, None] * inv_freq
    cos, sin = jnp.cos(ang), jnp.sin(ang)
    z = jnp.zeros_like(cos)
    return jnp.concatenate([cos, z, cos, z], axis=1), jnp.concatenate([-sin, z, sin, z], axis=1)


def _forward_backward(x, cos_t, sin_t, target, norm_pre, norm_post, lb_logits, a_onorm, ln_w, ln_b,
                      b_ws, b_bias, get_w, put_g, put_small=None, start_dep=None):
    npre0, npre1 = norm_pre[0:1], norm_pre[1:2]
    npost0, npost1 = norm_post[0:1], norm_post[1:2]
    l0, l1 = lb_logits[0:1], lb_logits[1:2]
    bias_col = b_bias.reshape(B_GROUPS, B_CHUNK, 1)
    ws = b_ws.reshape(B_GROUPS, B_CHUNK, B_CHUNK)

    h0 = _pre_norm("pre_norm0", x, npre0, deps=[start_dep])
    w_ev_in = get_w("ev_in", h0)
    z0 = _mm_nn("ev_in", h0, w_ev_in, F32, 1024, 896)
    cat, sst = _hgrn2_fwd(z0, l0, l1, a_onorm)
    cat = _gmlp_fwd(z0, cat, ln_w, ln_b, ws, bias_col)
    w_ev_out = get_w("ev_out", cat)
    y0 = _mm_nn("ev_out", cat, w_ev_out, F32, 1024, 1024)
    x1, h1 = _post_pre_norm(x, y0, npost0, npre1)
    w_od_in, w_qb, w_kvb, q_norm, kv_norm = get_w("od_mid", h1)
    z1 = _mm_nt("od_in", h1, w_od_in[None], F32, 1024, 640)
    cqn, ckvn, kp = _mla_pre(z1, q_norm, kv_norm, cos_t, sin_t)
    q = _mm_nn("od_qb", cqn, w_qb[None], F32, 1024, 1024)
    kv = _mm_nn("od_kvb", ckvn, w_kvb, BF16, 1024, 512)
    o, lse = _attention_fwd(q, cos_t, sin_t, kv, kp)
    og = _gate_out(o, z1)
    w_od_out = get_w("od_out", og)
    y1 = _mm_nn("od_out", og, w_od_out, F32, 1024, 1024)
    dx2, dy1, loss_part, dnpost1 = _final_loss(x1, y1, npost1, target)

    g_od_out = _mm_tn("od_out_dw", og, dy1, 1, BF16, 1024, 1024)
    tok = put_g("od_out", [g_od_out.reshape(N_DEV, D // N_DEV, D)])
    dog = _mm_nt("od_out_dx", dy1, w_od_out, F32, 1024, 1024, deps=[tok])
    do, dgate = _gate_out_bwd(o, z1, dog)
    dq, dkv, dkp = _attention_bwd(q, cos_t, sin_t, kv, kp, o, lse, do)
    g_qb = _mm_tn("od_qb_dw", cqn, dq, 1, F32, 512, 1024)
    g_kvb = _mm_tn("od_kvb_dw", ckvn, dkv, N_DEV, BF16, 512, 512)
    tok = put_g("od_qkv", [_qb_unlayout(g_qb[0]).astype(BF16), g_kvb])
    dcqn = _mm_nt("od_qb_dx", dq, w_qb[None], F32, 1024, 512, deps=[tok])
    dckvn = _mm_nt("od_kvb_dx", dkv, w_kvb, F32, 1024, 512)
    dcq, dckv, dkpe, dqn, dkvn = _mla_pre_bwd(z1, q_norm, kv_norm, cos_t, sin_t, dcqn, dckvn, dkp)
    dz1 = jnp.concatenate([dgate, dcq, dckv, dkpe], axis=1)
    g_od_in = _mm_tn("od_in_dw", dz1, h1, 1, F32, 640, 1024)
    tok = put_g("od_in", [_odd_in_unlayout(g_od_in[0]).astype(BF16)])
    dh1 = _mm_nn("od_in_dx", dz1, w_od_in[None], F32, 1024, 1024, deps=[tok])
    dx1, dy0, dnpost0, dnpre1 = _post_pre_norm_bwd(y0, x1, npost0, npre1, dx2, dh1)

    g_ev_out = _mm_tn("ev_out_dw", cat, dy0, 1, BF16, 1024, 1024)
    tok = put_g("ev_out", [g_ev_out.reshape(N_DEV, D // N_DEV, D)])
    dcat = _mm_nt("ev_out_dx", dy0, w_ev_out, F32, 1024, 1024, deps=[tok])
    dqa, dfa, dia, dga, dl0, dl1, donorm = _hgrn2_bwd(z0, l0, l1, a_onorm, sst, dcat)
    dub, dvb, dgb, dlnw, dlnb, dws, dbias = _gmlp_bwd(z0, ln_w, ln_b, ws, bias_col, dcat)
    dz0 = jnp.concatenate([dqa, dfa, dia, dga, dub, dvb, dgb], axis=1)
    early = _small_rows(dnpre1, dnpost0, dnpost1, dl0, dl1, donorm, dlnw, dlnb, dws, dbias, dqn, dkvn, loss_part)
    tok = put_small(early) if put_small else None
    g_ev_in = _mm_tn("ev_in_dw", h0, dz0, N_DEV, BF16, 1024, 896, deps=[tok])
    tok = put_g("ev_in", [g_ev_in])
    dh0 = _mm_nt("ev_in_dx", dz0, w_ev_in, F32, 1024, 1024, deps=[tok])
    grad_x, dnpre0 = _pre_norm_bwd(x, npre0, dh0, dx1)
    return grad_x, early, dnpre0


def kernel(x, positions, norm_pre, norm_post, ev_w_in, ev_lb_logits, ev_a_onorm, ev_b_ln_w, ev_b_ln_b, ev_b_ws, ev_b_bias, ev_w_out, od_w_in, od_q_norm, od_w_qb, od_kv_norm, od_w_kvb, od_w_out, loss_target, m_norm_pre, m_norm_post, m_ev_w_in, m_ev_lb_logits, m_ev_a_onorm, m_ev_b_ln_w, m_ev_b_ln_b, m_ev_b_ws, m_ev_b_bias, m_ev_w_out, m_od_w_in, m_od_q_norm, m_od_w_qb, m_od_kv_norm, m_od_w_kvb, m_od_w_out, v_norm_pre, v_norm_post, v_ev_w_in, v_ev_lb_logits, v_ev_a_onorm, v_ev_b_ln_w, v_ev_b_ln_b, v_ev_b_ws, v_ev_b_bias, v_ev_w_out, v_od_w_in, v_od_q_norm, v_od_w_qb, v_od_kv_norm, v_od_w_kvb, v_od_w_out):
    me = 4 * lax.axis_index("x") + 2 * lax.axis_index("y") + lax.axis_index("c")
    bf = lambda w: w[0].astype(BF16)

    norms = jnp.pad(jnp.concatenate([od_q_norm, od_kv_norm], axis=1), ((0, 7), (0, 0)))
    first_h, tok = _gather2_send("gather_ev_in", [bf(ev_w_in)])
    rest_h, tok = _gather2_send("gather_rest", [bf(ev_w_out), od_w_in[0].T.astype(BF16), bf(od_w_qb), bf(od_w_kvb),
                                                norms, bf(od_w_out)], deps=[tok])
    first_h, tok = _gather2_relay("relay_ev_in", first_h, tok)
    rest = []

    def get_w(group, after):
        if group == "ev_in":
            return _split_done("arrived_ev_in", first_h, after)[0]
        if not rest:
            relayed, token = _gather2_relay("relay_rest", rest_h, after)
            rest.extend(_split_done("arrived_rest", relayed, token))
        w_ev_out, w_od_in, w_qb, w_kvb, norms_all, w_od_out = rest
        if group == "ev_out":
            return w_ev_out.reshape(1, D, D)
        if group == "od_out":
            return w_od_out.reshape(1, D, D)
        return (_odd_in_layout(w_od_in), _qb_layout(w_qb), w_kvb,
                norms_all[:, 0, :64].reshape(1, C_RANK), norms_all[:, 0, 64:].reshape(1, C_RANK))

    scatters = {}

    def put_g(group, grads):
        if group == "ev_in":
            paired, token = _scatter2_pair("pair_ev_in", grads)
            n = len(grads)
            bufs = _split_done("paired_ev_in", paired, token, all_bufs=True)
            chip_sums = [_pair_add("pair_add_ev_in", bufs[a], bufs[n + a], me) for a in range(n)]
            scatters[group], token = _scatter2_send("scatter_ev_in", chip_sums)
        else:
            scatters[group], token = _exchange_start("scatter_" + group, grads, False)
        return token

    def put_small(early):
        scatters["small"], token = _exchange_start("gather_small_early", early, True)
        return token

    cos_t, sin_t = _rope_tables(positions)
    grad_x, _, dnpre0 = _forward_backward(
        x[0], cos_t, sin_t, loss_target[0], norm_pre, norm_post, ev_lb_logits, ev_a_onorm, ev_b_ln_w,
        ev_b_ln_b, ev_b_ws, ev_b_bias, get_w, put_g, put_small, start_dep=tok)

    big_w = {"ev_w_in": ev_w_in, "ev_w_out": ev_w_out, "od_w_in": od_w_in, "od_w_qb": od_w_qb,
             "od_w_kvb": od_w_kvb, "od_w_out": od_w_out}
    big_m = {"ev_w_in": m_ev_w_in, "ev_w_out": m_ev_w_out, "od_w_in": m_od_w_in, "od_w_qb": m_od_w_qb,
             "od_w_kvb": m_od_w_kvb, "od_w_out": m_od_w_out}
    big_v = {"ev_w_in": v_ev_w_in, "ev_w_out": v_ev_w_out, "od_w_in": v_od_w_in, "od_w_qb": v_od_w_qb,
             "od_w_kvb": v_od_w_kvb, "od_w_out": v_od_w_out}
    big_out = {}
    after = grad_x
    for group, names in (("od_out", ["od_w_out"]), ("od_qkv", ["od_w_qb", "od_w_kvb"]), ("od_in", ["od_w_in"]),
                         ("ev_out", ["ev_w_out"])):
        parts = _exchange_wait("summed_" + group, scatters[group], after)
        for nm, p in zip(names, parts):
            w, m, v = big_w[nm][0], big_m[nm][0], big_v[nm][0]
            if nm == "od_w_in":
                res_t = _adamw("adamw_" + nm, [(p, N_DEV)], w.T, m.T, v.T, w.shape[1], 512)
                big_out[nm] = [r.T[None] for r in res_t]
            else:
                big_out[nm] = [r[None] for r in _adamw("adamw_" + nm, [(p, N_DEV)], w, m, v, w.shape[0] // 8)]
            after = big_out[nm][0]

    late_all = _exchange("gather_small_late", [dnpre0], gather=True, deps=[after])[0]
    early_all = _exchange_wait("arrived_small_early", scatters["small"], late_all)

    small_w = (norm_pre, norm_post, ev_lb_logits, ev_a_onorm, ev_b_ln_w, ev_b_ln_b, ev_b_ws, ev_b_bias)
    small_m = (m_norm_pre, m_norm_post, m_ev_lb_logits, m_ev_a_onorm, m_ev_b_ln_w, m_ev_b_ln_b, m_ev_b_ws, m_ev_b_bias)
    small_v = (v_norm_pre, v_norm_post, v_ev_lb_logits, v_ev_a_onorm, v_ev_b_ln_w, v_ev_b_ln_b, v_ev_b_ws, v_ev_b_bias)
    wmv = [tuple(a.reshape(s) for a in t) for s, t in zip(SMALL_PARAM_SHAPES, zip(small_w, small_m, small_v))]
    small_res, loss_row, g_norm_rows = _adamw_small(late_all, early_all, wmv)
    small_out = [[r.reshape(w.shape) for r in four] for four, w in zip(small_res, small_w)]
    loss = loss_row[0, 0]

    g_norms = jnp.concatenate([lax.dynamic_slice(g_norm_rows, (0, 64 * me), (1, 64)),
                               lax.dynamic_slice(g_norm_rows, (1, 64 * me), (1, 64))], axis=1)
    res_n = _adamw("adamw_norms", [(g_norms[None], 1)],
                   jnp.concatenate([od_q_norm, od_kv_norm], axis=1),
                   jnp.concatenate([m_od_q_norm, m_od_kv_norm], axis=1),
                   jnp.concatenate([v_od_q_norm, v_od_kv_norm], axis=1), 1)
    qn_out = [r[:, :64] for r in res_n]
    kvn_out = [r[:, 64:] for r in res_n]

    chip_sum, from_peers = _split_done("summed_ev_in", scatters["ev_in"], loss_row, all_bufs=True)
    w = ev_w_in[0]
    big_out["ev_w_in"] = [r[None] for r in _adamw("adamw_ev_w_in", [(chip_sum, 1), (from_peers, 3)], w, m_ev_w_in[0],
                                                  v_ev_w_in[0], w.shape[0] // 8)]

    order = ("norm_pre", "norm_post", "ev_w_in", "ev_lb_logits", "ev_a_onorm", "ev_b_ln_w", "ev_b_ln_b",
             "ev_b_ws", "ev_b_bias", "ev_w_out", "od_w_in", "od_q_norm", "od_w_qb", "od_kv_norm",
             "od_w_kvb", "od_w_out")
    small_names = ("norm_pre", "norm_post", "ev_lb_logits", "ev_a_onorm", "ev_b_ln_w", "ev_b_ln_b",
                   "ev_b_ws", "ev_b_bias")
    outs = [loss, grad_x[None]]
    for kind in range(4):
        for nm in order:
            if nm in big_out:
                outs.append(big_out[nm][kind])
            elif nm == "od_q_norm":
                outs.append(qn_out[kind])
            elif nm == "od_kv_norm":
                outs.append(kvn_out[kind])
            else:
                outs.append(small_out[small_names.index(nm)][kind])
    return tuple(outs)
```

```python
import functools

import jax
import jax.numpy as jnp
from jax import lax
from jax.experimental import pallas as pl
from jax.experimental.pallas import tpu as pltpu

F32 = jnp.float32
BF16 = jnp.bfloat16

N_DEV = 8
T = 2048
D = 2048
EPS = 1e-6
A_HEADS = 8
HD = 128
A_CHUNK = 64
A_SUB = 16
B_GROUPS = 8
B_CHUNK = 128
EVEN_IN = 7168
C_HEADS = 16
C_RANK = 512
C_NOPE = 128
C_ROPE = 64
C_QK = C_NOPE + C_ROPE
C_V = 128
ODD_IN = 3136
ODD_IN_PAD = 3200
QP = 256
ROPE_THETA = 10000.0
ATT_SCALE = C_QK ** -0.5

ADAM_LR = 0.001
ADAM_B1 = 0.9
ADAM_B2 = 0.999
ADAM_EPS = 1e-08
ADAM_WD = 0.01
ADAM_STEP = 10

VMEM_LIMIT_V7X = 56 * 1024 * 1024
MESH_ID = pl.DeviceIdType.MESH


def _params(n_grid):
    return pltpu.CompilerParams(dimension_semantics=("arbitrary",) * n_grid,
                                vmem_limit_bytes=VMEM_LIMIT_V7X)


def _dg(a, b, ca, cb):
    return lax.dot_general(a.astype(BF16), b.astype(BF16), (((ca,), (cb,)), ((), ())),
                           preferred_element_type=F32)


def _raw_nn(a, b):
    return _dg(a, b, 1, 0)


def _raw_nt(a, b):
    return _dg(a, b, 1, 1)


def _raw_tn(a, b):
    return _dg(a, b, 0, 0)


@jax.custom_vjp
def _dot_nn(a, b):
    return _raw_nn(a, b)


def _dot_nn_fwd(a, b):
    return _raw_nn(a, b), (a.astype(BF16), b.astype(BF16))


def _dot_nn_bwd(res, g):
    a, b = res
    return _raw_nt(g, b), _raw_tn(a, g)


_dot_nn.defvjp(_dot_nn_fwd, _dot_nn_bwd)


@jax.custom_vjp
def _dot_nt(a, b):
    return _raw_nt(a, b)


def _dot_nt_fwd(a, b):
    return _raw_nt(a, b), (a.astype(BF16), b.astype(BF16))


def _dot_nt_bwd(res, g):
    a, b = res
    return _raw_nn(g, b), _raw_tn(g, a)


_dot_nt.defvjp(_dot_nt_fwd, _dot_nt_bwd)


@jax.custom_vjp
def _dot_tn(a, b):
    return _raw_tn(a, b)


def _dot_tn_fwd(a, b):
    return _raw_tn(a, b), (a.astype(BF16), b.astype(BF16))


def _dot_tn_bwd(res, g):
    a, b = res
    return _raw_nt(b, g), _raw_nn(a, g)


_dot_tn.defvjp(_dot_tn_fwd, _dot_tn_bwd)


@jax.custom_vjp
def _sigmoid(x):
    e = jnp.exp(-jnp.abs(x))
    return jnp.where(x >= 0, 1.0 / (1.0 + e), e / (1.0 + e))


def _sigmoid_fwd(x):
    s = _sigmoid(x)
    return s, s


def _sigmoid_bwd(s, g):
    return (g * s * (1.0 - s),)


_sigmoid.defvjp(_sigmoid_fwd, _sigmoid_bwd)


def _silu(x):
    return x * _sigmoid(x)


def _rms(x, w):
    return x * lax.rsqrt(jnp.mean(x * x, axis=-1, keepdims=True) + EPS) * w


def _split3(x):
    hi = x.astype(BF16)
    r = x - hi.astype(F32)
    mid = r.astype(BF16)
    lo = (r - mid.astype(F32)).astype(BF16)
    return hi, mid, lo


def _mask_apply(mask_bf16, x, contract):
    out = None
    for piece in _split3(x):
        d = lax.dot_general(mask_bf16, piece, (((contract,), (0,)), ((), ())),
                            preferred_element_type=F32)
        out = d if out is None else out + d
    return out


def _chunk_tri(rows):
    r = lax.broadcasted_iota(jnp.int32, (rows, rows), 0)
    c = lax.broadcasted_iota(jnp.int32, (rows, rows), 1)
    return ((r >= c) & (r // A_CHUNK == c // A_CHUNK)).astype(BF16)


@jax.custom_vjp
def _chunk_cumsum(x):
    return _mask_apply(_chunk_tri(x.shape[0]), x, 1)


def _chunk_cumsum_fwd(x):
    return _chunk_cumsum(x), None


def _chunk_cumsum_bwd(_, g):
    return (_mask_apply(_chunk_tri(g.shape[0]), g, 0),)


_chunk_cumsum.defvjp(_chunk_cumsum_fwd, _chunk_cumsum_bwd)


def _hgrn2_rows(q, zf, v, ga, st, l0, l1, onorm):
    rows = q.shape[0]
    n_sub = A_CHUNK // A_SUB
    mx = jnp.maximum(l0, l1)
    e0 = jnp.exp(l0 - mx)
    e1 = jnp.exp(l1 - mx)
    lb = e0 / (e0 + e1)
    lf = jnp.log(lb + (1.0 - lb) * _sigmoid(zf))
    k = (1.0 - lb) * _sigmoid(-zf)
    b = _chunk_cumsum(lf)

    t_idx = lax.broadcasted_iota(jnp.int32, (A_CHUNK, n_sub * A_CHUNK), 0)
    c_idx = lax.broadcasted_iota(jnp.int32, (A_CHUNK, n_sub * A_CHUNK), 1)
    sel = (c_idx // A_CHUNK == t_idx // A_SUB) & (c_idx % A_CHUNK <= t_idx)
    key_row = lax.broadcasted_iota(jnp.int32, (A_CHUNK, HD), 0)

    outs = []
    for n in range(rows // A_CHUNK):
        lo = n * A_CHUNK
        qc, kc, vc = q[lo:lo + A_CHUNK], k[lo:lo + A_CHUNK], v[lo:lo + A_CHUNK]
        lfc, bc = lf[lo:lo + A_CHUNK], b[lo:lo + A_CHUNK]
        b_last = bc[A_CHUNK - 1:A_CHUNK]
        o_inter = _dot_nt(qc * jnp.exp(bc), st)
        kv_t = _dot_tn(vc, kc * jnp.exp(b_last - bc))
        st = st * jnp.exp(b_last) + kv_t
        g_rows, k_subs = [], []
        for i in range(n_sub):
            g_i = bc[i * A_SUB:i * A_SUB + 1] - lfc[i * A_SUB:i * A_SUB + 1]
            g_rows.append(jnp.broadcast_to(g_i, (A_SUB, HD)))
            expo = jnp.where(key_row < (i + 1) * A_SUB, g_i - bc, -jnp.inf)
            k_subs.append(kc * jnp.exp(expo))
        q_sub = qc * jnp.exp(bc - jnp.concatenate(g_rows, axis=0))
        scores = _dot_nt(q_sub, jnp.concatenate(k_subs, axis=0))
        scores = jnp.where(sel, scores, 0.0)
        o_intra = _dot_nn(scores, jnp.concatenate([vc] * n_sub, axis=0))
        outs.append(o_inter + o_intra)
    o = jnp.concatenate(outs, axis=0)
    return _rms(o, onorm) * _silu(ga), st


def _gmlp_rows(u, vb, gb, lnw, lnb, ws, bias):
    rows = u.shape[0]
    mu = jnp.mean(vb, axis=-1, keepdims=True)
    xc = vb - mu
    vg = xc * lax.rsqrt(jnp.mean(xc * xc, axis=-1, keepdims=True) + EPS) * lnw + lnb
    r = lax.broadcasted_iota(jnp.int32, (B_CHUNK, B_CHUNK), 0)
    c = lax.broadcasted_iota(jnp.int32, (B_CHUNK, B_CHUNK), 1)
    ws_causal = jnp.where(r >= c, ws, 0.0)
    svs = [_dot_nn(ws_causal, vg[n * B_CHUNK:(n + 1) * B_CHUNK]) + bias
           for n in range(rows // B_CHUNK)]
    return u * jnp.concatenate(svs, axis=0) * _silu(gb)


def _rope(x, cos_t, sin_t):
    return x * cos_t + pltpu.roll(x, 64, 1) * sin_t


def _rope_transpose(g, cos_t, sin_t):
    return g * cos_t + pltpu.roll(g * sin_t, 64, 1)


ANY_SPEC = pl.BlockSpec(memory_space=pl.ANY)


def _live(deps):
    return [d for d in deps if d is not None]


def _skip_deps(body, n_in, n_deps):
    def wrapped(*refs):
        return body(*refs[:n_in], *refs[n_in + n_deps:])
    return wrapped


def _pure_call(name, fn, grid, in_specs, out_specs, out_shape, args, n_acc=0, deps=()):
    deps = _live(deps)
    n_in, n_out, n_deps = len(in_specs), len(out_specs), len(deps)
    in_specs = list(in_specs) + [ANY_SPEC] * n_deps
    args = tuple(args) + tuple(deps)

    def body(*refs):
        res = fn(*[r[...] for r in refs[:n_in]])
        if not isinstance(res, (tuple, list)):
            res = (res,)
        outs = refs[n_in + n_deps:n_in + n_deps + n_out]
        for o, r in zip(outs[:n_out - n_acc], res[:n_out - n_acc]):
            o[...] = r.astype(o.dtype)
        if n_acc:
            first = functools.reduce(jnp.logical_and, [pl.program_id(i) == 0 for i in range(len(grid))])
            for o, r in zip(outs[n_out - n_acc:], res[n_out - n_acc:]):
                @pl.when(first)
                def _(o=o, r=r):
                    o[...] = r.astype(o.dtype)

                @pl.when(jnp.logical_not(first))
                def _(o=o, r=r):
                    o[...] += r.astype(o.dtype)

    return pl.pallas_call(body, name=name, grid=grid, in_specs=in_specs, out_specs=out_specs,
                          out_shape=out_shape, compiler_params=_params(len(grid)))(*args)


def _sds(shape, dtype):
    return jax.ShapeDtypeStruct(shape, dtype)


def _row_spec(tm, width, col=0):
    return pl.BlockSpec((tm, width), lambda i, col=col: (i, col))


def _full_spec(shape):
    nd = len(shape)
    return pl.BlockSpec(shape, lambda *_: (0,) * nd)


def _mm_nn(name, a, b, out_dtype, tm, tn, deps=()):
    deps = _live(deps)
    m, k = a.shape
    j, _, n = b.shape
    per = n // tn

    def body(a_ref, b_ref, o_ref):
        o_ref[...] = _raw_nn(a_ref[...], b_ref[...]).astype(o_ref.dtype)

    return pl.pallas_call(
        _skip_deps(body, 2, len(deps)), name=name, grid=(m // tm, j * per),
        in_specs=[pl.BlockSpec((tm, k), lambda i, c: (i, 0)),
                  pl.BlockSpec((None, k, tn), lambda i, c: (c // per, 0, c % per))] + [ANY_SPEC] * len(deps),
        out_specs=pl.BlockSpec((tm, tn), lambda i, c: (i, c)),
        out_shape=_sds((m, j * n), out_dtype), compiler_params=_params(2))(a, b, *deps)


def _mm_nt(name, a, b, out_dtype, tm, tn, deps=()):
    deps = _live(deps)
    m = a.shape[0]
    j, nn, n = b.shape

    def body(a_ref, b_ref, o_ref, acc_ref):
        part = _raw_nt(a_ref[...], b_ref[...])
        if j == 1:
            o_ref[...] = part.astype(o_ref.dtype)
        else:
            kk = pl.program_id(2)

            @pl.when(kk == 0)
            def _():
                acc_ref[...] = part

            @pl.when(kk > 0)
            def _():
                acc_ref[...] += part

            @pl.when(kk == j - 1)
            def _():
                o_ref[...] = acc_ref[...].astype(o_ref.dtype)

    acc_shape = (tm, tn) if j > 1 else (8, 128)
    return pl.pallas_call(
        _skip_deps(body, 2, len(deps)), name=name, grid=(m // tm, nn // tn, j),
        in_specs=[pl.BlockSpec((tm, n), lambda i, c, kk: (i, kk)),
                  pl.BlockSpec((None, tn, n), lambda i, c, kk: (kk, c, 0))] + [ANY_SPEC] * len(deps),
        out_specs=pl.BlockSpec((tm, tn), lambda i, c, kk: (i, c)),
        out_shape=_sds((m, nn), out_dtype),
        scratch_shapes=[pltpu.VMEM(acc_shape, F32)], compiler_params=_params(3))(a, b, *deps)


def _mm_tn(name, a, b, j, out_dtype, tm, tn, deps=()):
    deps = _live(deps)
    k, m = a.shape
    n = b.shape[1] // j
    per = n // tn

    def body(a_ref, b_ref, o_ref):
        o_ref[...] = _raw_tn(a_ref[...], b_ref[...]).astype(o_ref.dtype)

    return pl.pallas_call(
        _skip_deps(body, 2, len(deps)), name=name, grid=(m // tm, j * per),
        in_specs=[pl.BlockSpec((k, tm), lambda i, c: (0, i)),
                  pl.BlockSpec((k, tn), lambda i, c: (0, c))] + [ANY_SPEC] * len(deps),
        out_specs=pl.BlockSpec((None, tm, tn), lambda i, c: (c // per, i, c % per)),
        out_shape=_sds((j, m, n), out_dtype), compiler_params=_params(2))(a, b, *deps)


TM = 256


def _pre_norm(name, x, w_row, deps=()):
    def fn(xv, w):
        return _rms(xv, w)
    return _pure_call(name, fn, (T // TM,), [_row_spec(TM, D), _full_spec((1, D))],
                      [_row_spec(TM, D)], [_sds((T, D), BF16)], (x, w_row), deps=deps)[0]


def _post_pre_norm(x, y, w_post, w_pre):
    def fn(xv, yv, wp, wn):
        x1 = xv + _rms(yv, wp)
        return x1, _rms(x1, wn)
    return _pure_call("post_pre_norm", fn, (T // TM,),
                      [_row_spec(TM, D), _row_spec(TM, D), _full_spec((1, D)), _full_spec((1, D))],
                      [_row_spec(TM, D), _row_spec(TM, D)],
                      [_sds((T, D), F32), _sds((T, D), BF16)], (x, y, w_post, w_pre))


def _post_pre_norm_bwd(y, x1, w_post, w_pre, dx1_in, dh1, deps=()):
    def fn(yv, x1v, wp, wn, dx1v, dh1v):
        _, vjp_pre = jax.vjp(_rms, x1v, wn)
        dx1_h, dwn = vjp_pre(dh1v)
        dx1 = dx1v + dx1_h
        _, vjp_post = jax.vjp(_rms, yv, wp)
        dy, dwp = vjp_post(dx1)
        return dx1, dy, dwp, dwn
    return _pure_call("post_pre_norm_bwd", fn, (T // TM,),
                      [_row_spec(TM, D), _row_spec(TM, D), _full_spec((1, D)), _full_spec((1, D)),
                       _row_spec(TM, D), _row_spec(TM, D)],
                      [_row_spec(TM, D), _row_spec(TM, D), _full_spec((1, D)), _full_spec((1, D))],
                      [_sds((T, D), F32), _sds((T, D), BF16), _sds((1, D), F32), _sds((1, D), F32)],
                      (y, x1, w_post, w_pre, dx1_in, dh1), n_acc=2, deps=deps)


def _final_loss(x1, y, w_post, target):
    def fn(x1v, yv, wp, tv):
        r, vjp = jax.vjp(_rms, yv, wp)
        err = x1v + r - tv
        part = 0.5 * jnp.sum(jnp.mean(err * err, axis=-1, keepdims=True), axis=0, keepdims=True)
        dx2 = err * (1.0 / D)
        dy, dwp = vjp(dx2)
        return dx2, dy, jnp.broadcast_to(part, (1, 128)), dwp
    return _pure_call("final_loss", fn, (T // TM,),
                      [_row_spec(TM, D), _row_spec(TM, D), _full_spec((1, D)), _row_spec(TM, D)],
                      [_row_spec(TM, D), _row_spec(TM, D), _full_spec((1, 128)), _full_spec((1, D))],
                      [_sds((T, D), F32), _sds((T, D), BF16), _sds((1, 128), F32), _sds((1, D), F32)],
                      (x1, y, w_post, target), n_acc=2)


def _pre_norm_bwd(x, w_row, dh, dx_res, deps=()):
    def fn(xv, w, dhv, dxv):
        _, vjp = jax.vjp(_rms, xv, w)
        dx, dw = vjp(dhv)
        return dxv + dx, dw
    return _pure_call("pre_norm_bwd", fn, (T // TM,),
                      [_row_spec(TM, D), _full_spec((1, D)), _row_spec(TM, D), _row_spec(TM, D)],
                      [_row_spec(TM, D), _full_spec((1, D))],
                      [_sds((T, D), F32), _sds((1, D), F32)], (x, w_row, dh, dx_res), n_acc=1, deps=deps)


RA = 256
RB = 512


def _col_spec(rows, col_of):
    return pl.BlockSpec((rows, HD), lambda h, r, col_of=col_of: (r, col_of(h)))


HA = 4
A_GROUPS = A_HEADS // HA


def _head(ref, hh):
    return ref[:, hh * HD:(hh + 1) * HD]


def _hgrn2_fwd(z, l0, l1, onorm):
    nb = T // RA

    def body(q_ref, f_ref, v_ref, g_ref, l0_ref, l1_ref, on_ref, cat_ref, sst_ref, st_scr):
        @pl.when(pl.program_id(1) == 0)
        def _():
            st_scr[...] = jnp.zeros_like(st_scr)

        for hh in range(HA):
            st = st_scr[hh]
            sst_ref[hh] = st
            out, st_new = _hgrn2_rows(_head(q_ref, hh), _head(f_ref, hh), _head(v_ref, hh), _head(g_ref, hh), st,
                                      _head(l0_ref, hh), _head(l1_ref, hh), on_ref[...])
            cat_ref[:, hh * HD:(hh + 1) * HD] = out.astype(cat_ref.dtype)
            st_scr[hh] = st_new

    def cols(k):
        return pl.BlockSpec((RA, HA * HD), lambda g, r: (r, k * A_GROUPS + g))

    vec = pl.BlockSpec((1, HA * HD), lambda g, r: (0, g))
    return pl.pallas_call(
        body, name="hgrn2_fwd", grid=(A_GROUPS, nb),
        in_specs=[cols(0), cols(1), cols(2), cols(3), vec, vec, _full_spec((1, HD))],
        out_specs=[cols(0), pl.BlockSpec((HA, None, HD, HD), lambda g, r: (g, r, 0, 0))],
        out_shape=[_sds((T, 2 * A_HEADS * HD), BF16), _sds((A_HEADS, nb, HD, HD), F32)],
        scratch_shapes=[pltpu.VMEM((HA, HD, HD), F32)],
        compiler_params=_params(2))(z, z, z, z, l0, l1, onorm)


def _hgrn2_bwd(z, l0, l1, onorm, sst, dcat, deps=()):
    nb = T // RA
    deps = _live(deps)

    def body(q_ref, f_ref, v_ref, g_ref, l0_ref, l1_ref, on_ref, sst_ref, dcat_ref,
             dq_ref, df_ref, dv_ref, dg_ref, dl0_ref, dl1_ref, don_ref, ds_scr):
        g, r = pl.program_id(0), pl.program_id(1)

        @pl.when(r == 0)
        def _():
            ds_scr[...] = jnp.zeros_like(ds_scr)

        dl0s, dl1s, don = [], [], None
        for hh in range(HA):
            _, vjp = jax.vjp(_hgrn2_rows, _head(q_ref, hh), _head(f_ref, hh), _head(v_ref, hh), _head(g_ref, hh),
                             sst_ref[hh], _head(l0_ref, hh), _head(l1_ref, hh), on_ref[...])
            dq, dzf, dv, dga, dst, dl0, dl1, don_h = vjp((_head(dcat_ref, hh), ds_scr[hh]))
            for ref, val in ((dq_ref, dq), (df_ref, dzf), (dv_ref, dv), (dg_ref, dga)):
                ref[:, hh * HD:(hh + 1) * HD] = val.astype(ref.dtype)
            ds_scr[hh] = dst
            dl0s.append(dl0)
            dl1s.append(dl1)
            don = don_h if don is None else don + don_h
        dl0 = jnp.concatenate(dl0s, axis=1)
        dl1 = jnp.concatenate(dl1s, axis=1)

        @pl.when(r == 0)
        def _():
            dl0_ref[...] = dl0
            dl1_ref[...] = dl1

        @pl.when(r > 0)
        def _():
            dl0_ref[...] += dl0
            dl1_ref[...] += dl1

        first = jnp.logical_and(g == 0, r == 0)

        @pl.when(first)
        def _():
            don_ref[...] = don

        @pl.when(jnp.logical_not(first))
        def _():
            don_ref[...] += don

    def rev(k):
        return pl.BlockSpec((RA, HA * HD), lambda g, r: (nb - 1 - r, k * A_GROUPS + g))

    vec = pl.BlockSpec((1, HA * HD), lambda g, r: (0, g))
    grad = _sds((T, A_HEADS * HD), BF16)
    return pl.pallas_call(
        _skip_deps(body, 9, len(deps)), name="hgrn2_bwd", grid=(A_GROUPS, nb),
        in_specs=[rev(0), rev(1), rev(2), rev(3), vec, vec, _full_spec((1, HD)),
                  pl.BlockSpec((HA, None, HD, HD), lambda g, r: (g, nb - 1 - r, 0, 0)),
                  rev(0)] + [ANY_SPEC] * len(deps),
        out_specs=[rev(0)] * 4 + [vec, vec, _full_spec((1, HD))],
        out_shape=[grad] * 4 + [_sds((1, A_HEADS * HD), F32)] * 2 + [_sds((1, HD), F32)],
        scratch_shapes=[pltpu.VMEM((HA, HD, HD), F32)],
        compiler_params=_params(2))(z, z, z, z, l0, l1, onorm, sst, dcat, *deps)


def _gmlp_specs():
    vec = pl.BlockSpec((1, HD), lambda g, r: (0, g))
    ws = pl.BlockSpec((None, B_CHUNK, B_CHUNK), lambda g, r: (g, 0, 0))
    bias = pl.BlockSpec((None, B_CHUNK, 1), lambda g, r: (g, 0, 0))
    return vec, ws, bias


def _gmlp_fwd(z, cat, lnw, lnb, ws, bias):
    vec, ws_spec, bias_spec = _gmlp_specs()

    def body(u_ref, v_ref, g_ref, lnw_ref, lnb_ref, ws_ref, bias_ref, cat_in_ref, cat_ref):
        del cat_in_ref
        out = _gmlp_rows(u_ref[...], v_ref[...], g_ref[...], lnw_ref[...], lnb_ref[...],
                         ws_ref[...], bias_ref[...])
        cat_ref[...] = out.astype(cat_ref.dtype)

    return pl.pallas_call(
        body, name="gmlp_fwd", grid=(B_GROUPS, T // RB),
        in_specs=[_col_spec(RB, lambda g: 32 + g), _col_spec(RB, lambda g: 40 + g),
                  _col_spec(RB, lambda g: 48 + g), vec, vec, ws_spec, bias_spec,
                  pl.BlockSpec(memory_space=pl.ANY)],
        out_specs=_col_spec(RB, lambda g: A_HEADS + g),
        out_shape=_sds(cat.shape, cat.dtype), input_output_aliases={7: 0},
        compiler_params=_params(2))(z, z, z, lnw, lnb, ws, bias, cat)


def _gmlp_bwd(z, lnw, lnb, ws, bias, dcat):
    vec, ws_spec, bias_spec = _gmlp_specs()

    def fn(u, vb, gb, w, b, wsv, bv, dout):
        _, vjp = jax.vjp(_gmlp_rows, u, vb, gb, w, b, wsv, bv)
        return vjp(dout)

    def body(*refs):
        ins, outs = refs[:8], refs[8:]
        res = fn(*[r[...] for r in ins])
        for o, r in zip(outs[:3], res[:3]):
            o[...] = r.astype(o.dtype)
        first = pl.program_id(1) == 0
        for o, r in zip(outs[3:], res[3:]):
            @pl.when(first)
            def _(o=o, r=r):
                o[...] = r

            @pl.when(jnp.logical_not(first))
            def _(o=o, r=r):
                o[...] += r

    grad = _sds((T, B_GROUPS * HD), BF16)
    row_out = pl.BlockSpec((RB, HD), lambda g, r: (r, g))
    return pl.pallas_call(
        body, name="gmlp_bwd", grid=(B_GROUPS, T // RB),
        in_specs=[_col_spec(RB, lambda g: 32 + g), _col_spec(RB, lambda g: 40 + g),
                  _col_spec(RB, lambda g: 48 + g), vec, vec, ws_spec, bias_spec,
                  _col_spec(RB, lambda g: A_HEADS + g)],
        out_specs=[row_out] * 3 + [vec, vec, ws_spec, bias_spec],
        out_shape=[grad] * 3 + [_sds((1, B_GROUPS * HD), F32)] * 2
        + [_sds((B_GROUPS, B_CHUNK, B_CHUNK), F32), _sds((B_GROUPS, B_CHUNK, 1), F32)],
        compiler_params=_params(2))(z, z, z, lnw, lnb, ws, bias, dcat)


def _mla_pre(z1, qn, kvn, cos_t, sin_t):
    def fn(cq, ckv, kpe, cs, sn, wq, wkv):
        return _rms(cq, wq), _rms(ckv, wkv), _rope(kpe, cs, sn)
    return _pure_call("mla_pre", fn, (T // TM,),
                      [_row_spec(TM, C_RANK, 4), _row_spec(TM, C_RANK, 5), _row_spec(TM, HD, 24),
                       _row_spec(TM, HD), _row_spec(TM, HD),
                       _full_spec((1, C_RANK)), _full_spec((1, C_RANK))],
                      [_row_spec(TM, C_RANK), _row_spec(TM, C_RANK), _row_spec(TM, HD)],
                      [_sds((T, C_RANK), BF16), _sds((T, C_RANK), BF16), _sds((T, HD), BF16)],
                      (z1, z1, z1, cos_t, sin_t, qn, kvn))


def _mla_pre_bwd(z1, qn, kvn, cos_t, sin_t, dcqn, dckvn, dkp, deps=()):
    def fn(cq, ckv, cs, sn, wq, wkv, g_q, g_kv, g_kp):
        _, vjp_q = jax.vjp(_rms, cq, wq)
        dcq, dwq = vjp_q(g_q)
        _, vjp_kv = jax.vjp(_rms, ckv, wkv)
        dckv, dwkv = vjp_kv(g_kv)
        return dcq, dckv, _rope_transpose(g_kp, cs, sn), dwq, dwkv
    return _pure_call("mla_pre_bwd", fn, (T // TM,),
                      [_row_spec(TM, C_RANK, 4), _row_spec(TM, C_RANK, 5),
                       _row_spec(TM, HD), _row_spec(TM, HD),
                       _full_spec((1, C_RANK)), _full_spec((1, C_RANK)),
                       _row_spec(TM, C_RANK), _row_spec(TM, C_RANK), _row_spec(TM, HD)],
                      [_row_spec(TM, C_RANK), _row_spec(TM, C_RANK), _row_spec(TM, HD),
                       _full_spec((1, C_RANK)), _full_spec((1, C_RANK))],
                      [_sds((T, C_RANK), BF16), _sds((T, C_RANK), BF16), _sds((T, HD), BF16),
                       _sds((1, C_RANK), F32), _sds((1, C_RANK), F32)],
                      (z1, z1, cos_t, sin_t, qn, kvn, dcqn, dckvn, dkp), n_acc=2, deps=deps)


def _gate_out(o, z1):
    def fn(ov, gate):
        return ov * _silu(gate)
    return _pure_call("gate_out", fn, (T // TM,), [_row_spec(TM, D), _row_spec(TM, D, 0)],
                      [_row_spec(TM, D)], [_sds((T, D), BF16)], (o, z1))[0]


def _gate_out_bwd(o, z1, dog, deps=()):
    def fn(ov, gate, g):
        _, vjp = jax.vjp(lambda a, b: a * _silu(b), ov, gate)
        return vjp(g)
    return _pure_call("gate_out_bwd", fn, (T // TM,),
                      [_row_spec(TM, D), _row_spec(TM, D, 0), _row_spec(TM, D)],
                      [_row_spec(TM, D), _row_spec(TM, D)],
                      [_sds((T, D), F32), _sds((T, D), BF16)], (o, z1, dog), deps=deps)


TQ = 256
HP = 4
KVW = C_NOPE + C_V


def _att_keys(kv_ref, kp_ref, k_scr):
    @pl.when(pl.program_id(1) == 0)
    def _():
        for hh in range(HP):
            k_scr[hh, :, 0:C_NOPE] = kv_ref[:, hh * KVW:hh * KVW + C_NOPE]
            k_scr[hh, :, C_NOPE:QP] = kp_ref[...]


def _att_scores(q, cos_ref, sin_ref, k_scr, hh, n):
    keys = (n + 1) * TQ
    qr = jnp.concatenate([q[:, :C_NOPE], _rope(q[:, C_NOPE:], cos_ref[...], sin_ref[...])], axis=1).astype(BF16)
    return qr, _raw_nt(qr, k_scr[hh, 0:keys, :]) * ATT_SCALE


def _causal(x, n, fill):
    row = lax.broadcasted_iota(jnp.int32, (TQ, TQ), 0)
    col = lax.broadcasted_iota(jnp.int32, (TQ, TQ), 1)
    diag = jnp.where(col <= row, x[:, n * TQ:], fill)
    return diag if n == 0 else jnp.concatenate([x[:, :n * TQ], diag], axis=1)


def _per_query_block(fn):
    for n in range(T // TQ):
        pl.when(pl.program_id(1) == n)(functools.partial(fn, n))


def _att_in_specs():
    return [pl.BlockSpec((TQ, HP * QP), lambda g, i: (i, g)),
            pl.BlockSpec((TQ, HD), lambda g, i: (i, 0)),
            pl.BlockSpec((TQ, HD), lambda g, i: (i, 0)),
            pl.BlockSpec((T, HP * KVW), lambda g, i: (0, g)),
            pl.BlockSpec((T, HD), lambda g, i: (0, 0))]


def _attention_fwd(q, cos_t, sin_t, kv, kp):
    def body(q_ref, cos_ref, sin_ref, kv_ref, kp_ref, o_ref, lse_ref, k_scr):
        _att_keys(kv_ref, kp_ref, k_scr)

        def block(n):
            keys = (n + 1) * TQ
            for hh in range(HP):
                _, s = _att_scores(q_ref[:, hh * QP:(hh + 1) * QP], cos_ref, sin_ref, k_scr, hh, n)
                s = _causal(s, n, jnp.finfo(F32).min)
                m = jnp.max(s, axis=-1, keepdims=True)
                p = jnp.exp(s - m)
                l = jnp.sum(p, axis=-1, keepdims=True)
                v = kv_ref[0:keys, hh * KVW + C_NOPE:(hh + 1) * KVW]
                o_ref[:, hh * C_V:(hh + 1) * C_V] = _raw_nn(p, v) / l
                lse_ref[hh] = m + jnp.log(l)

        _per_query_block(block)

    return pl.pallas_call(
        body, name="attention_fwd", grid=(C_HEADS // HP, T // TQ), in_specs=_att_in_specs(),
        out_specs=[pl.BlockSpec((TQ, HP * C_V), lambda g, i: (i, g)),
                   pl.BlockSpec((HP, TQ, 1), lambda g, i: (g, i, 0))],
        out_shape=[_sds((T, C_HEADS * C_V), F32), _sds((C_HEADS, T, 1), F32)],
        scratch_shapes=[pltpu.VMEM((HP, T, QP), BF16)],
        compiler_params=_params(2))(q, cos_t, sin_t, kv, kp)


def _attention_bwd(q, cos_t, sin_t, kv, kp, o, lse, do):
    nq = T // TQ

    def body(q_ref, cos_ref, sin_ref, kv_ref, kp_ref, o_ref, lse_ref, do_ref,
             dq_ref, dkv_ref, dkp_ref, k_scr, dk_scr, dv_scr):
        g, i = pl.program_id(0), pl.program_id(1)
        _att_keys(kv_ref, kp_ref, k_scr)

        @pl.when(i == 0)
        def _():
            dv_scr[...] = jnp.zeros_like(dv_scr)
            dk_scr[...] = jnp.zeros_like(dk_scr)

        def block(n):
            keys = (n + 1) * TQ
            for hh in range(HP):
                qr, s = _att_scores(q_ref[:, hh * QP:(hh + 1) * QP], cos_ref, sin_ref, k_scr, hh, n)
                p = _causal(jnp.exp(s - lse_ref[hh]), n, 0.0)
                dov = do_ref[:, hh * C_V:(hh + 1) * C_V]
                delta = jnp.sum(dov * o_ref[:, hh * C_V:(hh + 1) * C_V], axis=-1, keepdims=True)
                dp = _raw_nt(dov, kv_ref[0:keys, hh * KVW + C_NOPE:(hh + 1) * KVW])
                ds = p * (dp - delta) * ATT_SCALE
                dq = _raw_nn(ds, k_scr[hh, 0:keys, :])
                dq_ref[:, hh * QP:(hh + 1) * QP] = jnp.concatenate(
                    [dq[:, :C_NOPE], _rope_transpose(dq[:, C_NOPE:], cos_ref[...], sin_ref[...])],
                    axis=1).astype(dq_ref.dtype)
                dv_scr[hh, 0:keys, :] += _raw_tn(p, dov)
                dk_scr[hh, 0:keys, :] += _raw_tn(ds, qr)

        _per_query_block(block)

        @pl.when(i == nq - 1)
        def _():
            for hh in range(HP):
                dkv_ref[:, hh * KVW:(hh + 1) * KVW] = jnp.concatenate(
                    [dk_scr[hh, :, 0:C_NOPE], dv_scr[hh]], axis=1).astype(dkv_ref.dtype)

        @pl.when(jnp.logical_and(i == nq - 1, g == 0))
        def _():
            dkp_ref[...] = dk_scr[0, :, C_NOPE:QP]

        @pl.when(jnp.logical_and(i == nq - 1, g > 0))
        def _():
            dkp_ref[...] += dk_scr[0, :, C_NOPE:QP]

        @pl.when(i == nq - 1)
        def _():
            for hh in range(1, HP):
                dkp_ref[...] += dk_scr[hh, :, C_NOPE:QP]

    return pl.pallas_call(
        body, name="attention_bwd", grid=(C_HEADS // HP, nq),
        in_specs=_att_in_specs() + [pl.BlockSpec((TQ, HP * C_V), lambda g, i: (i, g)),
                                    pl.BlockSpec((HP, TQ, 1), lambda g, i: (g, i, 0)),
                                    pl.BlockSpec((TQ, HP * C_V), lambda g, i: (i, g))],
        out_specs=[pl.BlockSpec((TQ, HP * QP), lambda g, i: (i, g)),
                   pl.BlockSpec((T, HP * KVW), lambda g, i: (0, g)),
                   _full_spec((T, HD))],
        out_shape=[_sds((T, C_HEADS * QP), BF16), _sds((T, C_HEADS * KVW), BF16), _sds((T, HD), F32)],
        scratch_shapes=[pltpu.VMEM((HP, T, QP), BF16), pltpu.VMEM((HP, T, QP), F32), pltpu.VMEM((HP, T, C_V), F32)],
        compiler_params=_params(2))(q, cos_t, sin_t, kv, kp, o, lse, do)


def _adamw_math(w, g, m, v):
    m = ADAM_B1 * m + (1.0 - ADAM_B1) * g
    v = ADAM_B2 * v + (1.0 - ADAM_B2) * (g * g)
    m_hat = m / (1.0 - ADAM_B1 ** ADAM_STEP)
    v_hat = v / (1.0 - ADAM_B2 ** ADAM_STEP)
    delta = -ADAM_LR * (m_hat / (jnp.sqrt(v_hat) + ADAM_EPS) + ADAM_WD * w)
    return delta, m, v


def _adamw(name, parts, w, m, v, tr, tc=None):
    rows, cols = w.shape

    def fn(*vals):
        pvs, (wv, mv, vv) = vals[:len(parts)], vals[len(parts):]
        g = None
        for pv in pvs:
            for d in range(pv.shape[0]):
                term = pv[d].astype(F32)
                g = term if g is None else g + term
        return (g,) + _adamw_math(wv, g, mv, vv)

    tc = cols if tc is None else tc
    blk = pl.BlockSpec((tr, tc), lambda i, j: (i, j))
    part_specs = [pl.BlockSpec((n, tr, tc), lambda i, j: (0, i, j)) for _, n in parts]
    return _pure_call(name, fn, (rows // tr, cols // tc), part_specs + [blk, blk, blk],
                      [blk] * 4, [_sds((rows, cols), F32)] * 4, tuple(p for p, _ in parts) + (w, m, v))


SMALL_PARAM_SHAPES = ((2, D), (2, D), (2, A_HEADS * HD), (1, HD), (1, B_GROUPS * HD), (1, B_GROUPS * HD),
                      (B_GROUPS, B_CHUNK, B_CHUNK), (B_GROUPS, B_CHUNK))
SMALL_PIECES = ((0, 0, 0, 0), (0, 1, 1, 0), (1, 0, 1, 1), (1, 1, 1, 2), (2, 0, 2, 0), (2, 1, 2, 1),
                (3, 0, 3, 8), (4, 0, 2, 2), (5, 0, 2, 3))


def _small_rows(dnpre1, dnpost0, dnpost1, dl0, dl1, donorm, dlnw, dlnb, dws, dbias, dqn, dkvn, loss_part):
    return [jnp.concatenate([dnpre1, dnpost0, dnpost1], axis=0),
            jnp.concatenate([dl0, dl1, dlnw, dlnb], axis=0),
            jnp.concatenate([dbias.reshape(B_GROUPS, B_CHUNK), donorm, loss_part], axis=0),
            dws,
            jnp.concatenate([dqn, dkvn], axis=0)]


def _adamw_small(late_all, early_all, wmv):
    n_in = 6 + 3 * len(wmv)

    def body(*refs):
        gathered, params, outs = refs[:6], refs[6:n_in], refs[n_in:]

        def total(ref):
            s = ref[0]
            for d in range(1, N_DEV):
                s = s + ref[d]
            return s

        g_late, g2048, g1024, g128, g_ws, g512 = [total(r) for r in gathered]
        arrays = (g_late, g2048, g1024, g128)

        def update(p, rows, g):
            w_ref, m_ref, v_ref = params[3 * p:3 * p + 3]
            delta, m, v = _adamw_math(w_ref[rows], g, m_ref[rows], v_ref[rows])
            for out, val in zip(outs[4 * p:4 * p + 4], (g, delta, m, v)):
                out[rows] = val

        for p, row, arr, arr_row in SMALL_PIECES:
            update(p, pl.ds(row, 1), arrays[arr][arr_row:arr_row + 1])
        update(6, slice(None), g_ws)
        update(7, slice(None), g128[0:B_GROUPS])
        outs[32][...] = g128[B_GROUPS + 1:B_GROUPS + 2]
        outs[33][...] = g512

    vmem = pl.BlockSpec(memory_space=pltpu.VMEM)
    flat = [a for t in wmv for a in t]
    out_shape = [_sds(s, F32) for s in SMALL_PARAM_SHAPES for _ in range(4)] + [_sds((1, 128), F32), _sds((2, C_RANK), F32)]
    res = pl.pallas_call(body, name="adamw_small", in_specs=[vmem] * n_in, out_specs=[vmem] * len(out_shape),
                         out_shape=out_shape,
                         compiler_params=pltpu.CompilerParams(vmem_limit_bytes=VMEM_LIMIT_V7X))(late_all, *early_all, *flat)
    return [res[4 * p:4 * p + 4] for p in range(8)], res[32], res[33]


def _exchange(name, arrs, gather, deps=()):
    n = len(arrs)
    deps = _live(deps)

    def body(*refs):
        ins, outs = refs[:n], refs[n + len(deps):2 * n + len(deps)]
        send_sems, recv_sems, local_sems = refs[2 * n + len(deps):]
        x, y, c = lax.axis_index("x"), lax.axis_index("y"), lax.axis_index("c")
        me = 4 * x + 2 * y + c

        def peer(k):
            return (x ^ (k >> 2), y ^ ((k >> 1) & 1), c ^ (k & 1))

        def copy(a, k):
            src = ins[a] if gather else ins[a].at[me ^ k]
            return pltpu.make_async_remote_copy(
                src_ref=src, dst_ref=outs[a].at[me], send_sem=send_sems.at[a, k - 1],
                recv_sem=recv_sems.at[a, k - 1], device_id=peer(k), device_id_type=MESH_ID)

        def arrival(a, k):
            src = ins[a] if gather else ins[a].at[me]
            return pltpu.make_async_remote_copy(
                src_ref=src, dst_ref=outs[a].at[me ^ k], send_sem=send_sems.at[a, k - 1],
                recv_sem=recv_sems.at[a, k - 1], device_id=peer(k), device_id_type=MESH_ID)

        own = [pltpu.make_async_copy(ins[a] if gather else ins[a].at[me], outs[a].at[me], local_sems.at[a])
               for a in range(n)]
        for cp in own:
            cp.start()
        for k in range(1, N_DEV):
            for a in range(n):
                copy(a, k).start()
        for k in range(1, N_DEV):
            for a in range(n):
                arrival(a, k).wait_recv()
        for k in range(1, N_DEV):
            for a in range(n):
                copy(a, k).wait_send()
        for cp in own:
            cp.wait()

    any_spec = pl.BlockSpec(memory_space=pl.ANY)
    out_shape = [_sds((N_DEV,) + a.shape if gather else a.shape, a.dtype) for a in arrs]
    return pl.pallas_call(
        body, name=name, in_specs=[any_spec] * (n + len(deps)), out_specs=[any_spec] * n, out_shape=out_shape,
        scratch_shapes=[pltpu.SemaphoreType.DMA((n, N_DEV - 1)), pltpu.SemaphoreType.DMA((n, N_DEV - 1)),
                        pltpu.SemaphoreType.DMA((n,))],
        compiler_params=pltpu.CompilerParams(has_side_effects=True))(*arrs, *deps)


HBM_SPEC = pl.BlockSpec(memory_space=pltpu.HBM)
SEM_SPEC = pl.BlockSpec(memory_space=pltpu.SEMAPHORE)
DATAFLOW = pltpu.SideEffectType.DATAFLOW_SIDE_EFFECTING


def _my_index():
    return 4 * lax.axis_index("x") + 2 * lax.axis_index("y") + lax.axis_index("c")


def _plan_copies(plan, refs, send_sems, recv_sems):
    x, y, c = lax.axis_index("x"), lax.axis_index("y"), lax.axis_index("c")
    return [pltpu.make_async_remote_copy(
        src_ref=src, dst_ref=dst, send_sem=send_sems.at[i], recv_sem=recv_sems.at[i],
        device_id=(x ^ (k >> 2), y ^ ((k >> 1) & 1), c ^ (k & 1)), device_id_type=MESH_ID)
        for i, (src, dst, k) in enumerate(plan(refs, 4 * x + 2 * y + c))]


def _split_call(name, bufs, waits=None, starts=None, deps=()):
    n = len(bufs)
    deps = _live(deps)
    n_wait = 2 if waits else 0

    def body(*refs):
        zones = refs[:n]
        if waits:
            for cp in _plan_copies(waits[2], zones, refs[n], refs[n + 1]):
                cp.wait_send()
                cp.wait_recv()
        if starts:
            first_out = n + n_wait + len(deps)
            for cp in _plan_copies(starts[0], zones, refs[first_out], refs[first_out + 1]):
                cp.start()
            refs[-1][...] = jnp.zeros_like(refs[-1])

    out_specs, out_shape = [], []
    if starts:
        sems = pltpu.SemaphoreType.DMA((starts[1],))
        out_specs, out_shape = [SEM_SPEC, SEM_SPEC], [sems, sems]
    out_specs += [HBM_SPEC] * n
    out_shape += [pltpu.HBM(b.shape, b.dtype) for b in bufs]
    if starts:
        out_specs.append(pl.BlockSpec(memory_space=pltpu.VMEM))
        out_shape.append(_sds((8, 128), F32))
    first_buf = 2 if starts else 0
    res = pl.pallas_call(
        body, name=name,
        in_specs=[HBM_SPEC] * n + [SEM_SPEC] * n_wait + [ANY_SPEC] * len(deps),
        out_specs=out_specs, out_shape=out_shape,
        input_output_aliases={i: first_buf + i for i in range(n)},
        compiler_params=pltpu.CompilerParams(has_side_effects=DATAFLOW),
    )(*[pltpu.with_memory_space_constraint(b, pltpu.HBM) for b in bufs], *(waits[:2] if waits else ()), *deps)
    out_bufs = list(res[first_buf:first_buf + n])
    return out_bufs, ((res[0], res[1]) if starts else None), (res[-1] if starts else None)


def _direct_plan(n, gather):
    def plan(refs, me):
        return [(refs[a] if gather else refs[a].at[me ^ k], refs[n + a].at[me], k)
                for k in range(1, N_DEV) for a in range(n)]
    return plan


def _own_slot_filled(a, gather):
    me = _my_index()
    if gather:
        return lax.dynamic_update_slice_in_dim(lax.empty((N_DEV,) + a.shape, a.dtype), a[None], me, 0)
    return lax.dynamic_update_slice_in_dim(lax.empty(a.shape, a.dtype), lax.dynamic_slice_in_dim(a, me, 1, 0), me, 0)


def _exchange_start(name, arrs, gather, deps=()):
    n = len(arrs)
    lands = [_own_slot_filled(a, gather) for a in arrs]
    plan = _direct_plan(n, gather)
    bufs, sems, token = _split_call(name, list(arrs) + lands, starts=(plan, n * (N_DEV - 1)), deps=deps)
    return (n, plan, sems, bufs, None), token


def _exchange_wait(name, handle, after):
    return _split_done(name, handle, after)


ICI_PEERS = (2, 4, 6)
SIBLING = 1


def _gather2_send(name, arrs, deps=()):
    n = len(arrs)
    lands = [_own_slot_filled(a, True) for a in arrs]

    def plan(refs, me_):
        return [(refs[a], refs[n + a].at[me_], k) for k in (SIBLING,) + ICI_PEERS for a in range(n)]

    bufs, sems, token = _split_call(name, list(arrs) + lands, starts=(plan, 4 * n), deps=deps)
    return (n, plan, sems, bufs, None), token


def _gather2_relay(name, handle, after):
    n, plan, sems, bufs, _ = handle

    def relay(refs, me_):
        return [(refs[n + a].at[me_ ^ k], refs[n + a].at[me_ ^ k], SIBLING) for k in ICI_PEERS for a in range(n)]

    bufs, sems2, token = _split_call(name, bufs, waits=(sems[0], sems[1], plan), starts=(relay, 3 * n), deps=[after])
    return (n, relay, sems2, bufs, None), token


def _split_done(name, handle, after, all_bufs=False):
    n, plan, sems, bufs, _ = handle
    bufs, _, _ = _split_call(name, bufs, waits=(sems[0], sems[1], plan), deps=[after])
    return bufs if all_bufs else bufs[n:]


def _scatter2_pair(name, stacks, deps=()):
    n = len(stacks)
    pairs = [lax.empty((4,) + s.shape[1:], s.dtype) for s in stacks]

    def plan(refs, me):
        return [(refs[a].at[(me ^ SIBLING) ^ (2 * j)], refs[n + a].at[j], SIBLING) for j in range(4) for a in range(n)]

    bufs, sems, token = _split_call(name, list(stacks) + pairs, starts=(plan, 4 * n), deps=deps)
    return (n, plan, sems, bufs, None), token


def _pair_add(name, stack, pair, me):
    _, rows, cols = stack.shape
    tr = rows // 2

    def body(me_ref, s_ref, p_ref, o_ref):
        del me_ref
        o_ref[...] = (s_ref[...].astype(F32) + p_ref[...].astype(F32)).astype(o_ref.dtype)

    grid_spec = pltpu.PrefetchScalarGridSpec(
        num_scalar_prefetch=1, grid=(4, rows // tr),
        in_specs=[pl.BlockSpec((None, tr, cols), lambda j, i, me_ref: (me_ref[0] ^ (2 * j), i, 0)),
                  pl.BlockSpec((None, tr, cols), lambda j, i, me_ref: (j, i, 0))],
        out_specs=pl.BlockSpec((None, tr, cols), lambda j, i, me_ref: (j, i, 0)))
    return pl.pallas_call(body, name=name, grid_spec=grid_spec, out_shape=_sds((4, rows, cols), stack.dtype),
                          compiler_params=_params(2))(me.reshape(1).astype(jnp.int32), stack, pair)


def _scatter2_send(name, chip_sums, deps=()):
    n = len(chip_sums)
    finals = [lax.empty((3,) + c.shape[1:], c.dtype) for c in chip_sums]

    def plan(refs, me):
        del me
        return [(refs[a].at[j], refs[n + a].at[j - 1], 2 * j) for j in range(1, 4) for a in range(n)]

    bufs, sems, token = _split_call(name, list(chip_sums) + finals, starts=(plan, 3 * n), deps=deps)
    return (n, plan, sems, bufs, None), token


def _pad_rope(p):
    z = jnp.zeros(p.shape[:-1] + (32,), p.dtype)
    return jnp.concatenate([p[..., :32], z, p[..., 32:], z], axis=-1)


def _unpad_rope(p):
    return jnp.concatenate([p[..., :32], p[..., 64:96]], axis=-1)


def _odd_in_layout(wt):
    wt = wt.reshape(ODD_IN, D)
    cq, ckv, kpe, gate = wt[:512], wt[512:1024], wt[1024:1088], wt[1088:]
    z = jnp.zeros((32, D), wt.dtype)
    return jnp.concatenate([gate, cq, ckv, kpe[:32], z, kpe[32:], z], axis=0)


def _odd_in_unlayout(dwt):
    gate, cq, ckv, kpe = dwt[:2048], dwt[2048:2560], dwt[2560:3072], dwt[3072:]
    wt = jnp.concatenate([cq, ckv, kpe[:32], kpe[64:96], gate], axis=0)
    return wt.reshape(N_DEV, ODD_IN // N_DEV, D)


def _qb_layout(w):
    w = w.transpose(1, 0, 2).reshape(C_RANK, C_HEADS, C_QK)
    w = jnp.concatenate([w[..., :C_NOPE], _pad_rope(w[..., C_NOPE:])], axis=-1)
    return w.reshape(C_RANK, C_HEADS * QP)


def _qb_unlayout(dw):
    dw = dw.reshape(C_RANK, C_HEADS, QP)
    dw = jnp.concatenate([dw[..., :C_NOPE], _unpad_rope(dw[..., C_NOPE:])], axis=-1)
    return dw.reshape(C_RANK, N_DEV, C_HEADS * C_QK // N_DEV).transpose(1, 0, 2)


def _rope_tables(positions):
    inv_freq = ROPE_THETA ** (-jnp.arange(0, C_ROPE, 2, dtype=F32) / C_ROPE)
    ang = positions.astype(F32)[0][:, None] * inv_freq
    cos, sin = jnp.cos(ang), jnp.sin(ang)
    z = jnp.zeros_like(cos)
    return jnp.concatenate([cos, z, cos, z], axis=1), jnp.concatenate([-sin, z, sin, z], axis=1)


def _forward_backward(x, cos_t, sin_t, target, norm_pre, norm_post, lb_logits, a_onorm, ln_w, ln_b,
                      b_ws, b_bias, get_w, put_g, put_small=None, start_dep=None):
    npre0, npre1 = norm_pre[0:1], norm_pre[1:2]
    npost0, npost1 = norm_post[0:1], norm_post[1:2]
    l0, l1 = lb_logits[0:1], lb_logits[1:2]
    bias_col = b_bias.reshape(B_GROUPS, B_CHUNK, 1)
    ws = b_ws.reshape(B_GROUPS, B_CHUNK, B_CHUNK)

    h0 = _pre_norm("pre_norm0", x, npre0, deps=[start_dep])
    w_ev_in = get_w("ev_in", h0)
    z0 = _mm_nn("ev_in", h0, w_ev_in, F32, 1024, 896)
    cat, sst = _hgrn2_fwd(z0, l0, l1, a_onorm)
    cat = _gmlp_fwd(z0, cat, ln_w, ln_b, ws, bias_col)
    w_ev_out = get_w("ev_out", cat)
    y0 = _mm_nn("ev_out", cat, w_ev_out, F32, 1024, 1024)
    x1, h1 = _post_pre_norm(x, y0, npost0, npre1)
    w_od_in, w_qb, w_kvb, q_norm, kv_norm = get_w("od_mid", h1)
    z1 = _mm_nt("od_in", h1, w_od_in[None], F32, 1024, 640)
    cqn, ckvn, kp = _mla_pre(z1, q_norm, kv_norm, cos_t, sin_t)
    q = _mm_nn("od_qb", cqn, w_qb[None], F32, 1024, 1024)
    kv = _mm_nn("od_kvb", ckvn, w_kvb, BF16, 1024, 512)
    o, lse = _attention_fwd(q, cos_t, sin_t, kv, kp)
    og = _gate_out(o, z1)
    w_od_out = get_w("od_out", og)
    y1 = _mm_nn("od_out", og, w_od_out, F32, 1024, 1024)
    dx2, dy1, loss_part, dnpost1 = _final_loss(x1, y1, npost1, target)

    g_od_out = _mm_tn("od_out_dw", og, dy1, 1, BF16, 1024, 1024)
    tok = put_g("od_out", [g_od_out.reshape(N_DEV, D // N_DEV, D)])
    dog = _mm_nt("od_out_dx", dy1, w_od_out, F32, 1024, 1024, deps=[tok])
    do, dgate = _gate_out_bwd(o, z1, dog)
    dq, dkv, dkp = _attention_bwd(q, cos_t, sin_t, kv, kp, o, lse, do)
    g_qb = _mm_tn("od_qb_dw", cqn, dq, 1, F32, 512, 1024)
    g_kvb = _mm_tn("od_kvb_dw", ckvn, dkv, N_DEV, BF16, 512, 512)
    tok = put_g("od_qkv", [_qb_unlayout(g_qb[0]).astype(BF16), g_kvb])
    dcqn = _mm_nt("od_qb_dx", dq, w_qb[None], F32, 1024, 512, deps=[tok])
    dckvn = _mm_nt("od_kvb_dx", dkv, w_kvb, F32, 1024, 512)
    dcq, dckv, dkpe, dqn, dkvn = _mla_pre_bwd(z1, q_norm, kv_norm, cos_t, sin_t, dcqn, dckvn, dkp)
    dz1 = jnp.concatenate([dgate, dcq, dckv, dkpe], axis=1)
    g_od_in = _mm_tn("od_in_dw", dz1, h1, 1, F32, 640, 1024)
    tok = put_g("od_in", [_odd_in_unlayout(g_od_in[0]).astype(BF16)])
    dh1 = _mm_nn("od_in_dx", dz1, w_od_in[None], F32, 1024, 1024, deps=[tok])
    dx1, dy0, dnpost0, dnpre1 = _post_pre_norm_bwd(y0, x1, npost0, npre1, dx2, dh1)

    g_ev_out = _mm_tn("ev_out_dw", cat, dy0, 1, BF16, 1024, 1024)
    tok = put_g("ev_out", [g_ev_out.reshape(N_DEV, D // N_DEV, D)])
    dcat = _mm_nt("ev_out_dx", dy0, w_ev_out, F32, 1024, 1024, deps=[tok])
    dqa, dfa, dia, dga, dl0, dl1, donorm = _hgrn2_bwd(z0, l0, l1, a_onorm, sst, dcat)
    dub, dvb, dgb, dlnw, dlnb, dws, dbias = _gmlp_bwd(z0, ln_w, ln_b, ws, bias_col, dcat)
    dz0 = jnp.concatenate([dqa, dfa, dia, dga, dub, dvb, dgb], axis=1)
    early = _small_rows(dnpre1, dnpost0, dnpost1, dl0, dl1, donorm, dlnw, dlnb, dws, dbias, dqn, dkvn, loss_part)
    tok = put_small(early) if put_small else None
    g_ev_in = _mm_tn("ev_in_dw", h0, dz0, N_DEV, BF16, 1024, 896, deps=[tok])
    tok = put_g("ev_in", [g_ev_in])
    dh0 = _mm_nt("ev_in_dx", dz0, w_ev_in, F32, 1024, 1024, deps=[tok])
    grad_x, dnpre0 = _pre_norm_bwd(x, npre0, dh0, dx1)
    return grad_x, early, dnpre0


def kernel(x, positions, norm_pre, norm_post, ev_w_in, ev_lb_logits, ev_a_onorm, ev_b_ln_w, ev_b_ln_b, ev_b_ws, ev_b_bias, ev_w_out, od_w_in, od_q_norm, od_w_qb, od_kv_norm, od_w_kvb, od_w_out, loss_target, m_norm_pre, m_norm_post, m_ev_w_in, m_ev_lb_logits, m_ev_a_onorm, m_ev_b_ln_w, m_ev_b_ln_b, m_ev_b_ws, m_ev_b_bias, m_ev_w_out, m_od_w_in, m_od_q_norm, m_od_w_qb, m_od_kv_norm, m_od_w_kvb, m_od_w_out, v_norm_pre, v_norm_post, v_ev_w_in, v_ev_lb_logits, v_ev_a_onorm, v_ev_b_ln_w, v_ev_b_ln_b, v_ev_b_ws, v_ev_b_bias, v_ev_w_out, v_od_w_in, v_od_q_norm, v_od_w_qb, v_od_kv_norm, v_od_w_kvb, v_od_w_out):
    me = 4 * lax.axis_index("x") + 2 * lax.axis_index("y") + lax.axis_index("c")
    bf = lambda w: w[0].astype(BF16)

    norms = jnp.pad(jnp.concatenate([od_q_norm, od_kv_norm], axis=1), ((0, 7), (0, 0)))
    first_h, tok = _gather2_send("gather_ev_in", [bf(ev_w_in)])
    rest_h, tok = _gather2_send("gather_rest", [bf(ev_w_out), od_w_in[0].T.astype(BF16), bf(od_w_qb), bf(od_w_kvb),
                                                norms, bf(od_w_out)], deps=[tok])
    first_h, tok = _gather2_relay("relay_ev_in", first_h, tok)
    rest = []

    def get_w(group, after):
        if group == "ev_in":
            return _split_done("arrived_ev_in", first_h, after)[0]
        if not rest:
            relayed, token = _gather2_relay("relay_rest", rest_h, after)
            rest.extend(_split_done("arrived_rest", relayed, token))
        w_ev_out, w_od_in, w_qb, w_kvb, norms_all, w_od_out = rest
        if group == "ev_out":
            return w_ev_out.reshape(1, D, D)
        if group == "od_out":
            return w_od_out.reshape(1, D, D)
        return (_odd_in_layout(w_od_in), _qb_layout(w_qb), w_kvb,
                norms_all[:, 0, :64].reshape(1, C_RANK), norms_all[:, 0, 64:].reshape(1, C_RANK))

    scatters = {}

    def put_g(group, grads):
        if group == "ev_in":
            paired, token = _scatter2_pair("pair_ev_in", grads)
            n = len(grads)
            bufs = _split_done("paired_ev_in", paired, token, all_bufs=True)
            chip_sums = [_pair_add("pair_add_ev_in", bufs[a], bufs[n + a], me) for a in range(n)]
            scatters[group], token = _scatter2_send("scatter_ev_in", chip_sums)
        else:
            scatters[group], token = _exchange_start("scatter_" + group, grads, False)
        return token

    def put_small(early):
        scatters["small"], token = _exchange_start("gather_small_early", early, True)
        return token

    cos_t, sin_t = _rope_tables(positions)
    grad_x, _, dnpre0 = _forward_backward(
        x[0], cos_t, sin_t, loss_target[0], norm_pre, norm_post, ev_lb_logits, ev_a_onorm, ev_b_ln_w,
        ev_b_ln_b, ev_b_ws, ev_b_bias, get_w, put_g, put_small, start_dep=tok)

    big_w = {"ev_w_in": ev_w_in, "ev_w_out": ev_w_out, "od_w_in": od_w_in, "od_w_qb": od_w_qb,
             "od_w_kvb": od_w_kvb, "od_w_out": od_w_out}
    big_m = {"ev_w_in": m_ev_w_in, "ev_w_out": m_ev_w_out, "od_w_in": m_od_w_in, "od_w_qb": m_od_w_qb,
             "od_w_kvb": m_od_w_kvb, "od_w_out": m_od_w_out}
    big_v = {"ev_w_in": v_ev_w_in, "ev_w_out": v_ev_w_out, "od_w_in": v_od_w_in, "od_w_qb": v_od_w_qb,
             "od_w_kvb": v_od_w_kvb, "od_w_out": v_od_w_out}
    big_out = {}
    after = grad_x
    for group, names in (("od_out", ["od_w_out"]), ("od_qkv", ["od_w_qb", "od_w_kvb"]), ("od_in", ["od_w_in"]),
                         ("ev_out", ["ev_w_out"])):
        parts = _exchange_wait("summed_" + group, scatters[group], after)
        for nm, p in zip(names, parts):
            w, m, v = big_w[nm][0], big_m[nm][0], big_v[nm][0]
            if nm == "od_w_in":
                res_t = _adamw("adamw_" + nm, [(p, N_DEV)], w.T, m.T, v.T, w.shape[1], 512)
                big_out[nm] = [r.T[None] for r in res_t]
            else:
                big_out[nm] = [r[None] for r in _adamw("adamw_" + nm, [(p, N_DEV)], w, m, v, w.shape[0] // 8)]
            after = big_out[nm][0]

    late_all = _exchange("gather_small_late", [dnpre0], gather=True, deps=[after])[0]
    early_all = _exchange_wait("arrived_small_early", scatters["small"], late_all)

    small_w = (norm_pre, norm_post, ev_lb_logits, ev_a_onorm, ev_b_ln_w, ev_b_ln_b, ev_b_ws, ev_b_bias)
    small_m = (m_norm_pre, m_norm_post, m_ev_lb_logits, m_ev_a_onorm, m_ev_b_ln_w, m_ev_b_ln_b, m_ev_b_ws, m_ev_b_bias)
    small_v = (v_norm_pre, v_norm_post, v_ev_lb_logits, v_ev_a_onorm, v_ev_b_ln_w, v_ev_b_ln_b, v_ev_b_ws, v_ev_b_bias)
    wmv = [tuple(a.reshape(s) for a in t) for s, t in zip(SMALL_PARAM_SHAPES, zip(small_w, small_m, small_v))]
    small_res, loss_row, g_norm_rows = _adamw_small(late_all, early_all, wmv)
    small_out = [[r.reshape(w.shape) for r in four] for four, w in zip(small_res, small_w)]
    loss = loss_row[0, 0]

    g_norms = jnp.concatenate([lax.dynamic_slice(g_norm_rows, (0, 64 * me), (1, 64)),
                               lax.dynamic_slice(g_norm_rows, (1, 64 * me), (1, 64))], axis=1)
    res_n = _adamw("adamw_norms", [(g_norms[None], 1)],
                   jnp.concatenate([od_q_norm, od_kv_norm], axis=1),
                   jnp.concatenate([m_od_q_norm, m_od_kv_norm], axis=1),
                   jnp.concatenate([v_od_q_norm, v_od_kv_norm], axis=1), 1)
    qn_out = [r[:, :64] for r in res_n]
    kvn_out = [r[:, 64:] for r in res_n]

    chip_sum, from_peers = _split_done("summed_ev_in", scatters["ev_in"], loss_row, all_bufs=True)
    w = ev_w_in[0]
    big_out["ev_w_in"] = [r[None] for r in _adamw("adamw_ev_w_in", [(chip_sum, 1), (from_peers, 3)], w, m_ev_w_in[0],
                                                  v_ev_w_in[0], w.shape[0] // 8)]

    order = ("norm_pre", "norm_post", "ev_w_in", "ev_lb_logits", "ev_a_onorm", "ev_b_ln_w", "ev_b_ln_b",
             "ev_b_ws", "ev_b_bias", "ev_w_out", "od_w_in", "od_q_norm", "od_w_qb", "od_kv_norm",
             "od_w_kvb", "od_w_out")
    small_names = ("norm_pre", "norm_post", "ev_lb_logits", "ev_a_onorm", "ev_b_ln_w", "ev_b_ln_b",
                   "ev_b_ws", "ev_b_bias")
    outs = [loss, grad_x[None]]
    for kind in range(4):
        for nm in order:
            if nm in big_out:
                outs.append(big_out[nm][kind])
            elif nm == "od_q_norm":
                outs.append(qn_out[kind])
            elif nm == "od_kv_norm":
                outs.append(kvn_out[kind])
            else:
                outs.append(small_out[small_names.index(nm)][kind])
    return tuple(outs)
```

```python
import functools

import jax
import jax.numpy as jnp
from jax import lax
from jax.experimental import pallas as pl
from jax.experimental.pallas import tpu as pltpu

F32 = jnp.float32
BF16 = jnp.bfloat16

N_DEV = 8
T = 2048
D = 2048
EPS = 1e-6
A_HEADS = 8
HD = 128
A_CHUNK = 64
A_SUB = 16
B_GROUPS = 8
B_CHUNK = 128
EVEN_IN = 7168
C_HEADS = 16
C_RANK = 512
C_NOPE = 128
C_ROPE = 64
C_QK = C_NOPE + C_ROPE
C_V = 128
ODD_IN = 3136
ODD_IN_PAD = 3200
QP = 256
ROPE_THETA = 10000.0
ATT_SCALE = C_QK ** -0.5

ADAM_LR = 0.001
ADAM_B1 = 0.9
ADAM_B2 = 0.999
ADAM_EPS = 1e-08
ADAM_WD = 0.01
ADAM_STEP = 10

VMEM_LIMIT_V7X = 56 * 1024 * 1024
MESH_ID = pl.DeviceIdType.MESH


def _params(n_grid):
    return pltpu.CompilerParams(dimension_semantics=("arbitrary",) * n_grid,
                                vmem_limit_bytes=VMEM_LIMIT_V7X)


def _dg(a, b, ca, cb):
    return lax.dot_general(a.astype(BF16), b.astype(BF16), (((ca,), (cb,)), ((), ())),
                           preferred_element_type=F32)


def _raw_nn(a, b):
    return _dg(a, b, 1, 0)


def _raw_nt(a, b):
    return _dg(a, b, 1, 1)


def _raw_tn(a, b):
    return _dg(a, b, 0, 0)


@jax.custom_vjp
def _dot_nn(a, b):
    return _raw_nn(a, b)


def _dot_nn_fwd(a, b):
    return _raw_nn(a, b), (a.astype(BF16), b.astype(BF16))


def _dot_nn_bwd(res, g):
    a, b = res
    return _raw_nt(g, b), _raw_tn(a, g)


_dot_nn.defvjp(_dot_nn_fwd, _dot_nn_bwd)


@jax.custom_vjp
def _dot_nt(a, b):
    return _raw_nt(a, b)


def _dot_nt_fwd(a, b):
    return _raw_nt(a, b), (a.astype(BF16), b.astype(BF16))


def _dot_nt_bwd(res, g):
    a, b = res
    return _raw_nn(g, b), _raw_tn(g, a)


_dot_nt.defvjp(_dot_nt_fwd, _dot_nt_bwd)


@jax.custom_vjp
def _dot_tn(a, b):
    return _raw_tn(a, b)


def _dot_tn_fwd(a, b):
    return _raw_tn(a, b), (a.astype(BF16), b.astype(BF16))


def _dot_tn_bwd(res, g):
    a, b = res
    return _raw_nt(b, g), _raw_nn(a, g)


_dot_tn.defvjp(_dot_tn_fwd, _dot_tn_bwd)


@jax.custom_vjp
def _sigmoid(x):
    e = jnp.exp(-jnp.abs(x))
    return jnp.where(x >= 0, 1.0 / (1.0 + e), e / (1.0 + e))


def _sigmoid_fwd(x):
    s = _sigmoid(x)
    return s, s


def _sigmoid_bwd(s, g):
    return (g * s * (1.0 - s),)


_sigmoid.defvjp(_sigmoid_fwd, _sigmoid_bwd)


def _silu(x):
    return x * _sigmoid(x)


def _rms(x, w):
    return x * lax.rsqrt(jnp.mean(x * x, axis=-1, keepdims=True) + EPS) * w


def _split3(x):
    hi = x.astype(BF16)
    r = x - hi.astype(F32)
    mid = r.astype(BF16)
    lo = (r - mid.astype(F32)).astype(BF16)
    return hi, mid, lo


def _mask_apply(mask_bf16, x, contract):
    out = None
    for piece in _split3(x):
        d = lax.dot_general(mask_bf16, piece, (((contract,), (0,)), ((), ())),
                            preferred_element_type=F32)
        out = d if out is None else out + d
    return out


def _chunk_tri(rows):
    r = lax.broadcasted_iota(jnp.int32, (rows, rows), 0)
    c = lax.broadcasted_iota(jnp.int32, (rows, rows), 1)
    return ((r >= c) & (r // A_CHUNK == c // A_CHUNK)).astype(BF16)


@jax.custom_vjp
def _chunk_cumsum(x):
    return _mask_apply(_chunk_tri(x.shape[0]), x, 1)


def _chunk_cumsum_fwd(x):
    return _chunk_cumsum(x), None


def _chunk_cumsum_bwd(_, g):
    return (_mask_apply(_chunk_tri(g.shape[0]), g, 0),)


_chunk_cumsum.defvjp(_chunk_cumsum_fwd, _chunk_cumsum_bwd)


def _hgrn2_rows(q, zf, v, ga, st, l0, l1, onorm):
    rows = q.shape[0]
    n_sub = A_CHUNK // A_SUB
    mx = jnp.maximum(l0, l1)
    e0 = jnp.exp(l0 - mx)
    e1 = jnp.exp(l1 - mx)
    lb = e0 / (e0 + e1)
    lf = jnp.log(lb + (1.0 - lb) * _sigmoid(zf))
    k = (1.0 - lb) * _sigmoid(-zf)
    b = _chunk_cumsum(lf)

    t_idx = lax.broadcasted_iota(jnp.int32, (A_CHUNK, n_sub * A_CHUNK), 0)
    c_idx = lax.broadcasted_iota(jnp.int32, (A_CHUNK, n_sub * A_CHUNK), 1)
    sel = (c_idx // A_CHUNK == t_idx // A_SUB) & (c_idx % A_CHUNK <= t_idx)
    key_row = lax.broadcasted_iota(jnp.int32, (A_CHUNK, HD), 0)

    outs = []
    for n in range(rows // A_CHUNK):
        lo = n * A_CHUNK
        qc, kc, vc = q[lo:lo + A_CHUNK], k[lo:lo + A_CHUNK], v[lo:lo + A_CHUNK]
        lfc, bc = lf[lo:lo + A_CHUNK], b[lo:lo + A_CHUNK]
        b_last = bc[A_CHUNK - 1:A_CHUNK]
        o_inter = _dot_nt(qc * jnp.exp(bc), st)
        kv_t = _dot_tn(vc, kc * jnp.exp(b_last - bc))
        st = st * jnp.exp(b_last) + kv_t
        g_rows, k_subs = [], []
        for i in range(n_sub):
            g_i = bc[i * A_SUB:i * A_SUB + 1] - lfc[i * A_SUB:i * A_SUB + 1]
            g_rows.append(jnp.broadcast_to(g_i, (A_SUB, HD)))
            expo = jnp.where(key_row < (i + 1) * A_SUB, g_i - bc, -jnp.inf)
            k_subs.append(kc * jnp.exp(expo))
        q_sub = qc * jnp.exp(bc - jnp.concatenate(g_rows, axis=0))
        scores = _dot_nt(q_sub, jnp.concatenate(k_subs, axis=0))
        scores = jnp.where(sel, scores, 0.0)
        o_intra = _dot_nn(scores, jnp.concatenate([vc] * n_sub, axis=0))
        outs.append(o_inter + o_intra)
    o = jnp.concatenate(outs, axis=0)
    return _rms(o, onorm) * _silu(ga), st


def _gmlp_rows(u, vb, gb, lnw, lnb, ws, bias):
    rows = u.shape[0]
    mu = jnp.mean(vb, axis=-1, keepdims=True)
    xc = vb - mu
    vg = xc * lax.rsqrt(jnp.mean(xc * xc, axis=-1, keepdims=True) + EPS) * lnw + lnb
    r = lax.broadcasted_iota(jnp.int32, (B_CHUNK, B_CHUNK), 0)
    c = lax.broadcasted_iota(jnp.int32, (B_CHUNK, B_CHUNK), 1)
    ws_causal = jnp.where(r >= c, ws, 0.0)
    svs = [_dot_nn(ws_causal, vg[n * B_CHUNK:(n + 1) * B_CHUNK]) + bias
           for n in range(rows // B_CHUNK)]
    return u * jnp.concatenate(svs, axis=0) * _silu(gb)


def _rope(x, cos_t, sin_t):
    return x * cos_t + pltpu.roll(x, 64, 1) * sin_t


def _rope_transpose(g, cos_t, sin_t):
    return g * cos_t + pltpu.roll(g * sin_t, 64, 1)


ANY_SPEC = pl.BlockSpec(memory_space=pl.ANY)


def _live(deps):
    return [d for d in deps if d is not None]


def _skip_deps(body, n_in, n_deps):
    def wrapped(*refs):
        return body(*refs[:n_in], *refs[n_in + n_deps:])
    return wrapped


def _pure_call(name, fn, grid, in_specs, out_specs, out_shape, args, n_acc=0, deps=()):
    deps = _live(deps)
    n_in, n_out, n_deps = len(in_specs), len(out_specs), len(deps)
    in_specs = list(in_specs) + [ANY_SPEC] * n_deps
    args = tuple(args) + tuple(deps)

    def body(*refs):
        res = fn(*[r[...] for r in refs[:n_in]])
        if not isinstance(res, (tuple, list)):
            res = (res,)
        outs = refs[n_in + n_deps:n_in + n_deps + n_out]
        for o, r in zip(outs[:n_out - n_acc], res[:n_out - n_acc]):
            o[...] = r.astype(o.dtype)
        if n_acc:
            first = functools.reduce(jnp.logical_and, [pl.program_id(i) == 0 for i in range(len(grid))])
            for o, r in zip(outs[n_out - n_acc:], res[n_out - n_acc:]):
                @pl.when(first)
                def _(o=o, r=r):
                    o[...] = r.astype(o.dtype)

                @pl.when(jnp.logical_not(first))
                def _(o=o, r=r):
                    o[...] += r.astype(o.dtype)

    return pl.pallas_call(body, name=name, grid=grid, in_specs=in_specs, out_specs=out_specs,
                          out_shape=out_shape, compiler_params=_params(len(grid)))(*args)


def _sds(shape, dtype):
    return jax.ShapeDtypeStruct(shape, dtype)


def _row_spec(tm, width, col=0):
    return pl.BlockSpec((tm, width), lambda i, col=col: (i, col))


def _full_spec(shape):
    nd = len(shape)
    return pl.BlockSpec(shape, lambda *_: (0,) * nd)


def _mm_nn(name, a, b, out_dtype, tm, tn, deps=()):
    deps = _live(deps)
    m, k = a.shape
    j, _, n = b.shape
    per = n // tn

    def body(a_ref, b_ref, o_ref):
        o_ref[...] = _raw_nn(a_ref[...], b_ref[...]).astype(o_ref.dtype)

    return pl.pallas_call(
        _skip_deps(body, 2, len(deps)), name=name, grid=(m // tm, j * per),
        in_specs=[pl.BlockSpec((tm, k), lambda i, c: (i, 0)),
                  pl.BlockSpec((None, k, tn), lambda i, c: (c // per, 0, c % per))] + [ANY_SPEC] * len(deps),
        out_specs=pl.BlockSpec((tm, tn), lambda i, c: (i, c)),
        out_shape=_sds((m, j * n), out_dtype), compiler_params=_params(2))(a, b, *deps)


def _mm_nt(name, a, b, out_dtype, tm, tn, deps=()):
    deps = _live(deps)
    m = a.shape[0]
    j, nn, n = b.shape

    def body(a_ref, b_ref, o_ref):
        b_all = b_ref[0] if j == 1 else jnp.concatenate([b_ref[s] for s in range(j)], axis=1)
        o_ref[...] = _raw_nt(a_ref[...], b_all).astype(o_ref.dtype)

    return pl.pallas_call(
        _skip_deps(body, 2, len(deps)), name=name, grid=(m // tm, nn // tn),
        in_specs=[pl.BlockSpec((tm, j * n), lambda i, c: (i, 0)),
                  pl.BlockSpec((j, tn, n), lambda i, c: (0, c, 0))] + [ANY_SPEC] * len(deps),
        out_specs=pl.BlockSpec((tm, tn), lambda i, c: (i, c)),
        out_shape=_sds((m, nn), out_dtype), compiler_params=_params(2))(a, b, *deps)


def _mm_tn(name, a, b, j, out_dtype, tm, tn, deps=()):
    deps = _live(deps)
    k, m = a.shape
    n = b.shape[1] // j
    per = n // tn

    def body(a_ref, b_ref, o_ref):
        o_ref[...] = _raw_tn(a_ref[...], b_ref[...]).astype(o_ref.dtype)

    return pl.pallas_call(
        _skip_deps(body, 2, len(deps)), name=name, grid=(m // tm, j * per),
        in_specs=[pl.BlockSpec((k, tm), lambda i, c: (0, i)),
                  pl.BlockSpec((k, tn), lambda i, c: (0, c))] + [ANY_SPEC] * len(deps),
        out_specs=pl.BlockSpec((None, tm, tn), lambda i, c: (c // per, i, c % per)),
        out_shape=_sds((j, m, n), out_dtype), compiler_params=_params(2))(a, b, *deps)


TM = 256


def _pre_norm(name, x, w_row, deps=()):
    def fn(xv, w):
        return _rms(xv, w)
    return _pure_call(name, fn, (T // TM,), [_row_spec(TM, D), _full_spec((1, D))],
                      [_row_spec(TM, D)], [_sds((T, D), BF16)], (x, w_row), deps=deps)[0]


def _post_pre_norm(x, y, w_post, w_pre):
    def fn(xv, yv, wp, wn):
        x1 = xv + _rms(yv, wp)
        return x1, _rms(x1, wn)
    return _pure_call("post_pre_norm", fn, (T // TM,),
                      [_row_spec(TM, D), _row_spec(TM, D), _full_spec((1, D)), _full_spec((1, D))],
                      [_row_spec(TM, D), _row_spec(TM, D)],
                      [_sds((T, D), F32), _sds((T, D), BF16)], (x, y, w_post, w_pre))


def _post_pre_norm_bwd(y, x1, w_post, w_pre, dx1_in, dh1, deps=()):
    def fn(yv, x1v, wp, wn, dx1v, dh1v):
        _, vjp_pre = jax.vjp(_rms, x1v, wn)
        dx1_h, dwn = vjp_pre(dh1v)
        dx1 = dx1v + dx1_h
        _, vjp_post = jax.vjp(_rms, yv, wp)
        dy, dwp = vjp_post(dx1)
        return dx1, dy, dwp, dwn
    return _pure_call("post_pre_norm_bwd", fn, (T // TM,),
                      [_row_spec(TM, D), _row_spec(TM, D), _full_spec((1, D)), _full_spec((1, D)),
                       _row_spec(TM, D), _row_spec(TM, D)],
                      [_row_spec(TM, D), _row_spec(TM, D), _full_spec((1, D)), _full_spec((1, D))],
                      [_sds((T, D), F32), _sds((T, D), BF16), _sds((1, D), F32), _sds((1, D), F32)],
                      (y, x1, w_post, w_pre, dx1_in, dh1), n_acc=2, deps=deps)


def _final_loss(x1, y, w_post, target):
    def fn(x1v, yv, wp, tv):
        r, vjp = jax.vjp(_rms, yv, wp)
        err = x1v + r - tv
        part = 0.5 * jnp.sum(jnp.mean(err * err, axis=-1, keepdims=True), axis=0, keepdims=True)
        dx2 = err * (1.0 / D)
        dy, dwp = vjp(dx2)
        return dx2, dy, jnp.broadcast_to(part, (1, 128)), dwp
    return _pure_call("final_loss", fn, (T // TM,),
                      [_row_spec(TM, D), _row_spec(TM, D), _full_spec((1, D)), _row_spec(TM, D)],
                      [_row_spec(TM, D), _row_spec(TM, D), _full_spec((1, 128)), _full_spec((1, D))],
                      [_sds((T, D), F32), _sds((T, D), BF16), _sds((1, 128), F32), _sds((1, D), F32)],
                      (x1, y, w_post, target), n_acc=2)


def _pre_norm_bwd(x, w_row, dh, dx_res, deps=()):
    def fn(xv, w, dhv, dxv):
        _, vjp = jax.vjp(_rms, xv, w)
        dx, dw = vjp(dhv)
        return dxv + dx, dw
    return _pure_call("pre_norm_bwd", fn, (T // TM,),
                      [_row_spec(TM, D), _full_spec((1, D)), _row_spec(TM, D), _row_spec(TM, D)],
                      [_row_spec(TM, D), _full_spec((1, D))],
                      [_sds((T, D), F32), _sds((1, D), F32)], (x, w_row, dh, dx_res), n_acc=1, deps=deps)


RA = 256
RB = 512


def _col_spec(rows, col_of):
    return pl.BlockSpec((rows, HD), lambda h, r, col_of=col_of: (r, col_of(h)))


HA = 4
A_GROUPS = A_HEADS // HA


def _head(ref, hh):
    return ref[:, hh * HD:(hh + 1) * HD]


def _hgrn2_fwd(z, l0, l1, onorm):
    nb = T // RA

    def body(q_ref, f_ref, v_ref, g_ref, l0_ref, l1_ref, on_ref, cat_ref, sst_ref, st_scr):
        @pl.when(pl.program_id(1) == 0)
        def _():
            st_scr[...] = jnp.zeros_like(st_scr)

        for hh in range(HA):
            st = st_scr[hh]
            sst_ref[hh] = st
            out, st_new = _hgrn2_rows(_head(q_ref, hh), _head(f_ref, hh), _head(v_ref, hh), _head(g_ref, hh), st,
                                      _head(l0_ref, hh), _head(l1_ref, hh), on_ref[...])
            cat_ref[:, hh * HD:(hh + 1) * HD] = out.astype(cat_ref.dtype)
            st_scr[hh] = st_new

    def cols(k):
        return pl.BlockSpec((RA, HA * HD), lambda g, r: (r, k * A_GROUPS + g))

    vec = pl.BlockSpec((1, HA * HD), lambda g, r: (0, g))
    return pl.pallas_call(
        body, name="hgrn2_fwd", grid=(A_GROUPS, nb),
        in_specs=[cols(0), cols(1), cols(2), cols(3), vec, vec, _full_spec((1, HD))],
        out_specs=[cols(0), pl.BlockSpec((HA, None, HD, HD), lambda g, r: (g, r, 0, 0))],
        out_shape=[_sds((T, 2 * A_HEADS * HD), BF16), _sds((A_HEADS, nb, HD, HD), F32)],
        scratch_shapes=[pltpu.VMEM((HA, HD, HD), F32)],
        compiler_params=_params(2))(z, z, z, z, l0, l1, onorm)


def _hgrn2_bwd(z, l0, l1, onorm, sst, dcat, deps=()):
    nb = T // RA
    deps = _live(deps)

    def body(q_ref, f_ref, v_ref, g_ref, l0_ref, l1_ref, on_ref, sst_ref, dcat_ref,
             dq_ref, df_ref, dv_ref, dg_ref, dl0_ref, dl1_ref, don_ref, ds_scr):
        g, r = pl.program_id(0), pl.program_id(1)

        @pl.when(r == 0)
        def _():
            ds_scr[...] = jnp.zeros_like(ds_scr)

        dl0s, dl1s, don = [], [], None
        for hh in range(HA):
            _, vjp = jax.vjp(_hgrn2_rows, _head(q_ref, hh), _head(f_ref, hh), _head(v_ref, hh), _head(g_ref, hh),
                             sst_ref[hh], _head(l0_ref, hh), _head(l1_ref, hh), on_ref[...])
            dq, dzf, dv, dga, dst, dl0, dl1, don_h = vjp((_head(dcat_ref, hh), ds_scr[hh]))
            for ref, val in ((dq_ref, dq), (df_ref, dzf), (dv_ref, dv), (dg_ref, dga)):
                ref[:, hh * HD:(hh + 1) * HD] = val.astype(ref.dtype)
            ds_scr[hh] = dst
            dl0s.append(dl0)
            dl1s.append(dl1)
            don = don_h if don is None else don + don_h
        dl0 = jnp.concatenate(dl0s, axis=1)
        dl1 = jnp.concatenate(dl1s, axis=1)

        @pl.when(r == 0)
        def _():
            dl0_ref[...] = dl0
            dl1_ref[...] = dl1

        @pl.when(r > 0)
        def _():
            dl0_ref[...] += dl0
            dl1_ref[...] += dl1

        first = jnp.logical_and(g == 0, r == 0)

        @pl.when(first)
        def _():
            don_ref[...] = don

        @pl.when(jnp.logical_not(first))
        def _():
            don_ref[...] += don

    def rev(k):
        return pl.BlockSpec((RA, HA * HD), lambda g, r: (nb - 1 - r, k * A_GROUPS + g))

    vec = pl.BlockSpec((1, HA * HD), lambda g, r: (0, g))
    grad = _sds((T, A_HEADS * HD), BF16)
    return pl.pallas_call(
        _skip_deps(body, 9, len(deps)), name="hgrn2_bwd", grid=(A_GROUPS, nb),
        in_specs=[rev(0), rev(1), rev(2), rev(3), vec, vec, _full_spec((1, HD)),
                  pl.BlockSpec((HA, None, HD, HD), lambda g, r: (g, nb - 1 - r, 0, 0)),
                  rev(0)] + [ANY_SPEC] * len(deps),
        out_specs=[rev(0)] * 4 + [vec, vec, _full_spec((1, HD))],
        out_shape=[grad] * 4 + [_sds((1, A_HEADS * HD), F32)] * 2 + [_sds((1, HD), F32)],
        scratch_shapes=[pltpu.VMEM((HA, HD, HD), F32)],
        compiler_params=_params(2))(z, z, z, z, l0, l1, onorm, sst, dcat, *deps)


def _gmlp_specs():
    vec = pl.BlockSpec((1, HD), lambda g, r: (0, g))
    ws = pl.BlockSpec((None, B_CHUNK, B_CHUNK), lambda g, r: (g, 0, 0))
    bias = pl.BlockSpec((None, B_CHUNK, 1), lambda g, r: (g, 0, 0))
    return vec, ws, bias


def _gmlp_fwd(z, cat, lnw, lnb, ws, bias):
    vec, ws_spec, bias_spec = _gmlp_specs()

    def body(u_ref, v_ref, g_ref, lnw_ref, lnb_ref, ws_ref, bias_ref, cat_in_ref, cat_ref):
        del cat_in_ref
        out = _gmlp_rows(u_ref[...], v_ref[...], g_ref[...], lnw_ref[...], lnb_ref[...],
                         ws_ref[...], bias_ref[...])
        cat_ref[...] = out.astype(cat_ref.dtype)

    return pl.pallas_call(
        body, name="gmlp_fwd", grid=(B_GROUPS, T // RB),
        in_specs=[_col_spec(RB, lambda g: 32 + g), _col_spec(RB, lambda g: 40 + g),
                  _col_spec(RB, lambda g: 48 + g), vec, vec, ws_spec, bias_spec,
                  pl.BlockSpec(memory_space=pl.ANY)],
        out_specs=_col_spec(RB, lambda g: A_HEADS + g),
        out_shape=_sds(cat.shape, cat.dtype), input_output_aliases={7: 0},
        compiler_params=_params(2))(z, z, z, lnw, lnb, ws, bias, cat)


def _gmlp_bwd(z, lnw, lnb, ws, bias, dcat):
    vec, ws_spec, bias_spec = _gmlp_specs()

    def fn(u, vb, gb, w, b, wsv, bv, dout):
        _, vjp = jax.vjp(_gmlp_rows, u, vb, gb, w, b, wsv, bv)
        return vjp(dout)

    def body(*refs):
        ins, outs = refs[:8], refs[8:]
        res = fn(*[r[...] for r in ins])
        for o, r in zip(outs[:3], res[:3]):
            o[...] = r.astype(o.dtype)
        first = pl.program_id(1) == 0
        for o, r in zip(outs[3:], res[3:]):
            @pl.when(first)
            def _(o=o, r=r):
                o[...] = r

            @pl.when(jnp.logical_not(first))
            def _(o=o, r=r):
                o[...] += r

    grad = _sds((T, B_GROUPS * HD), BF16)
    row_out = pl.BlockSpec((RB, HD), lambda g, r: (r, g))
    return pl.pallas_call(
        body, name="gmlp_bwd", grid=(B_GROUPS, T // RB),
        in_specs=[_col_spec(RB, lambda g: 32 + g), _col_spec(RB, lambda g: 40 + g),
                  _col_spec(RB, lambda g: 48 + g), vec, vec, ws_spec, bias_spec,
                  _col_spec(RB, lambda g: A_HEADS + g)],
        out_specs=[row_out] * 3 + [vec, vec, ws_spec, bias_spec],
        out_shape=[grad] * 3 + [_sds((1, B_GROUPS * HD), F32)] * 2
        + [_sds((B_GROUPS, B_CHUNK, B_CHUNK), F32), _sds((B_GROUPS, B_CHUNK, 1), F32)],
        compiler_params=_params(2))(z, z, z, lnw, lnb, ws, bias, dcat)


def _mla_pre(z1, qn, kvn, cos_t, sin_t):
    def fn(cq, ckv, kpe, cs, sn, wq, wkv):
        return _rms(cq, wq), _rms(ckv, wkv), _rope(kpe, cs, sn)
    return _pure_call("mla_pre", fn, (T // TM,),
                      [_row_spec(TM, C_RANK, 4), _row_spec(TM, C_RANK, 5), _row_spec(TM, HD, 24),
                       _row_spec(TM, HD), _row_spec(TM, HD),
                       _full_spec((1, C_RANK)), _full_spec((1, C_RANK))],
                      [_row_spec(TM, C_RANK), _row_spec(TM, C_RANK), _row_spec(TM, HD)],
                      [_sds((T, C_RANK), BF16), _sds((T, C_RANK), BF16), _sds((T, HD), BF16)],
                      (z1, z1, z1, cos_t, sin_t, qn, kvn))


def _mla_pre_bwd(z1, qn, kvn, cos_t, sin_t, dcqn, dckvn, dkp, deps=()):
    def fn(cq, ckv, cs, sn, wq, wkv, g_q, g_kv, g_kp):
        _, vjp_q = jax.vjp(_rms, cq, wq)
        dcq, dwq = vjp_q(g_q)
        _, vjp_kv = jax.vjp(_rms, ckv, wkv)
        dckv, dwkv = vjp_kv(g_kv)
        return dcq, dckv, _rope_transpose(g_kp, cs, sn), dwq, dwkv
    return _pure_call("mla_pre_bwd", fn, (T // TM,),
                      [_row_spec(TM, C_RANK, 4), _row_spec(TM, C_RANK, 5),
                       _row_spec(TM, HD), _row_spec(TM, HD),
                       _full_spec((1, C_RANK)), _full_spec((1, C_RANK)),
                       _row_spec(TM, C_RANK), _row_spec(TM, C_RANK), _row_spec(TM, HD)],
                      [_row_spec(TM, C_RANK), _row_spec(TM, C_RANK), _row_spec(TM, HD),
                       _full_spec((1, C_RANK)), _full_spec((1, C_RANK))],
                      [_sds((T, C_RANK), BF16), _sds((T, C_RANK), BF16), _sds((T, HD), BF16),
                       _sds((1, C_RANK), F32), _sds((1, C_RANK), F32)],
                      (z1, z1, cos_t, sin_t, qn, kvn, dcqn, dckvn, dkp), n_acc=2, deps=deps)


def _gate_out(o, z1):
    def fn(ov, gate):
        return ov * _silu(gate)
    return _pure_call("gate_out", fn, (T // TM,), [_row_spec(TM, D), _row_spec(TM, D, 0)],
                      [_row_spec(TM, D)], [_sds((T, D), BF16)], (o, z1))[0]


def _gate_out_bwd(o, z1, dog, deps=()):
    def fn(ov, gate, g):
        _, vjp = jax.vjp(lambda a, b: a * _silu(b), ov, gate)
        return vjp(g)
    return _pure_call("gate_out_bwd", fn, (T // TM,),
                      [_row_spec(TM, D), _row_spec(TM, D, 0), _row_spec(TM, D)],
                      [_row_spec(TM, D), _row_spec(TM, D)],
                      [_sds((T, D), F32), _sds((T, D), BF16)], (o, z1, dog), deps=deps)


TQ = 256
HP = 2
KVW = C_NOPE + C_V


def _att_keys(kv_ref, kp_ref, k_scr):
    @pl.when(pl.program_id(1) == 0)
    def _():
        for hh in range(HP):
            k_scr[hh, :, 0:C_NOPE] = kv_ref[:, hh * KVW:hh * KVW + C_NOPE]
            k_scr[hh, :, C_NOPE:QP] = kp_ref[...]


def _att_scores(q, cos_ref, sin_ref, k_scr, hh, n):
    keys = (n + 1) * TQ
    qr = jnp.concatenate([q[:, :C_NOPE], _rope(q[:, C_NOPE:], cos_ref[...], sin_ref[...])], axis=1).astype(BF16)
    return qr, _raw_nt(qr, k_scr[hh, 0:keys, :]) * ATT_SCALE


def _causal(x, n, fill):
    row = lax.broadcasted_iota(jnp.int32, (TQ, TQ), 0)
    col = lax.broadcasted_iota(jnp.int32, (TQ, TQ), 1)
    diag = jnp.where(col <= row, x[:, n * TQ:], fill)
    return diag if n == 0 else jnp.concatenate([x[:, :n * TQ], diag], axis=1)


def _per_query_block(fn):
    for n in range(T // TQ):
        pl.when(pl.program_id(1) == n)(functools.partial(fn, n))


def _att_in_specs():
    return [pl.BlockSpec((TQ, HP * QP), lambda g, i: (i, g)),
            pl.BlockSpec((TQ, HD), lambda g, i: (i, 0)),
            pl.BlockSpec((TQ, HD), lambda g, i: (i, 0)),
            pl.BlockSpec((T, HP * KVW), lambda g, i: (0, g)),
            pl.BlockSpec((T, HD), lambda g, i: (0, 0))]


def _attention_fwd(q, cos_t, sin_t, kv, kp):
    def body(q_ref, cos_ref, sin_ref, kv_ref, kp_ref, o_ref, lse_ref, k_scr):
        _att_keys(kv_ref, kp_ref, k_scr)

        def block(n):
            keys = (n + 1) * TQ
            for hh in range(HP):
                _, s = _att_scores(q_ref[:, hh * QP:(hh + 1) * QP], cos_ref, sin_ref, k_scr, hh, n)
                s = _causal(s, n, jnp.finfo(F32).min)
                m = jnp.max(s, axis=-1, keepdims=True)
                p = jnp.exp(s - m)
                l = jnp.sum(p, axis=-1, keepdims=True)
                v = kv_ref[0:keys, hh * KVW + C_NOPE:(hh + 1) * KVW]
                o_ref[:, hh * C_V:(hh + 1) * C_V] = _raw_nn(p, v) / l
                lse_ref[hh] = m + jnp.log(l)

        _per_query_block(block)

    return pl.pallas_call(
        body, name="attention_fwd", grid=(C_HEADS // HP, T // TQ), in_specs=_att_in_specs(),
        out_specs=[pl.BlockSpec((TQ, HP * C_V), lambda g, i: (i, g)),
                   pl.BlockSpec((HP, TQ, 1), lambda g, i: (g, i, 0))],
        out_shape=[_sds((T, C_HEADS * C_V), F32), _sds((C_HEADS, T, 1), F32)],
        scratch_shapes=[pltpu.VMEM((HP, T, QP), BF16)],
        compiler_params=_params(2))(q, cos_t, sin_t, kv, kp)


def _attention_bwd(q, cos_t, sin_t, kv, kp, o, lse, do):
    nq = T // TQ

    def body(q_ref, cos_ref, sin_ref, kv_ref, kp_ref, o_ref, lse_ref, do_ref,
             dq_ref, dkv_ref, dkp_ref, k_scr, dk_scr, dv_scr):
        g, i = pl.program_id(0), pl.program_id(1)
        _att_keys(kv_ref, kp_ref, k_scr)

        @pl.when(i == 0)
        def _():
            dv_scr[...] = jnp.zeros_like(dv_scr)
            dk_scr[...] = jnp.zeros_like(dk_scr)

        def block(n):
            keys = (n + 1) * TQ
            for hh in range(HP):
                qr, s = _att_scores(q_ref[:, hh * QP:(hh + 1) * QP], cos_ref, sin_ref, k_scr, hh, n)
                p = _causal(jnp.exp(s - lse_ref[hh]), n, 0.0)
                dov = do_ref[:, hh * C_V:(hh + 1) * C_V]
                delta = jnp.sum(dov * o_ref[:, hh * C_V:(hh + 1) * C_V], axis=-1, keepdims=True)
                dp = _raw_nt(dov, kv_ref[0:keys, hh * KVW + C_NOPE:(hh + 1) * KVW])
                ds = p * (dp - delta) * ATT_SCALE
                dq = _raw_nn(ds, k_scr[hh, 0:keys, :])
                dq_ref[:, hh * QP:(hh + 1) * QP] = jnp.concatenate(
                    [dq[:, :C_NOPE], _rope_transpose(dq[:, C_NOPE:], cos_ref[...], sin_ref[...])],
                    axis=1).astype(dq_ref.dtype)
                dv_scr[hh, 0:keys, :] += _raw_tn(p, dov)
                dk_scr[hh, 0:keys, :] += _raw_tn(ds, qr)

        _per_query_block(block)

        @pl.when(i == nq - 1)
        def _():
            for hh in range(HP):
                dkv_ref[:, hh * KVW:(hh + 1) * KVW] = jnp.concatenate(
                    [dk_scr[hh, :, 0:C_NOPE], dv_scr[hh]], axis=1).astype(dkv_ref.dtype)

        @pl.when(jnp.logical_and(i == nq - 1, g == 0))
        def _():
            dkp_ref[...] = dk_scr[0, :, C_NOPE:QP]

        @pl.when(jnp.logical_and(i == nq - 1, g > 0))
        def _():
            dkp_ref[...] += dk_scr[0, :, C_NOPE:QP]

        @pl.when(i == nq - 1)
        def _():
            for hh in range(1, HP):
                dkp_ref[...] += dk_scr[hh, :, C_NOPE:QP]

    return pl.pallas_call(
        body, name="attention_bwd", grid=(C_HEADS // HP, nq),
        in_specs=_att_in_specs() + [pl.BlockSpec((TQ, HP * C_V), lambda g, i: (i, g)),
                                    pl.BlockSpec((HP, TQ, 1), lambda g, i: (g, i, 0)),
                                    pl.BlockSpec((TQ, HP * C_V), lambda g, i: (i, g))],
        out_specs=[pl.BlockSpec((TQ, HP * QP), lambda g, i: (i, g)),
                   pl.BlockSpec((T, HP * KVW), lambda g, i: (0, g)),
                   _full_spec((T, HD))],
        out_shape=[_sds((T, C_HEADS * QP), BF16), _sds((T, C_HEADS * KVW), BF16), _sds((T, HD), F32)],
        scratch_shapes=[pltpu.VMEM((HP, T, QP), BF16), pltpu.VMEM((HP, T, QP), F32), pltpu.VMEM((HP, T, C_V), F32)],
        compiler_params=_params(2))(q, cos_t, sin_t, kv, kp, o, lse, do)


def _adamw_math(w, g, m, v):
    m = ADAM_B1 * m + (1.0 - ADAM_B1) * g
    v = ADAM_B2 * v + (1.0 - ADAM_B2) * (g * g)
    m_hat = m / (1.0 - ADAM_B1 ** ADAM_STEP)
    v_hat = v / (1.0 - ADAM_B2 ** ADAM_STEP)
    delta = -ADAM_LR * (m_hat / (jnp.sqrt(v_hat) + ADAM_EPS) + ADAM_WD * w)
    return delta, m, v


def _adamw(name, parts, w, m, v, tr, tc=None):
    rows, cols = w.shape

    def fn(*vals):
        pvs, (wv, mv, vv) = vals[:len(parts)], vals[len(parts):]
        g = None
        for pv in pvs:
            for d in range(pv.shape[0]):
                term = pv[d].astype(F32)
                g = term if g is None else g + term
        return (g,) + _adamw_math(wv, g, mv, vv)

    tc = cols if tc is None else tc
    blk = pl.BlockSpec((tr, tc), lambda i, j: (i, j))
    part_specs = [pl.BlockSpec((n, tr, tc), lambda i, j: (0, i, j)) for _, n in parts]
    return _pure_call(name, fn, (rows // tr, cols // tc), part_specs + [blk, blk, blk],
                      [blk] * 4, [_sds((rows, cols), F32)] * 4, tuple(p for p, _ in parts) + (w, m, v))


SMALL_PARAM_SHAPES = ((2, D), (2, D), (2, A_HEADS * HD), (1, HD), (1, B_GROUPS * HD), (1, B_GROUPS * HD),
                      (B_GROUPS, B_CHUNK, B_CHUNK), (B_GROUPS, B_CHUNK))
SMALL_PIECES = ((0, 0, 0, 0), (0, 1, 1, 0), (1, 0, 1, 1), (1, 1, 1, 2), (2, 0, 2, 0), (2, 1, 2, 1),
                (3, 0, 3, 8), (4, 0, 2, 2), (5, 0, 2, 3))


def _small_rows(dnpre1, dnpost0, dnpost1, dl0, dl1, donorm, dlnw, dlnb, dws, dbias, dqn, dkvn, loss_part):
    return [jnp.concatenate([dnpre1, dnpost0, dnpost1], axis=0),
            jnp.concatenate([dl0, dl1, dlnw, dlnb], axis=0),
            jnp.concatenate([dbias.reshape(B_GROUPS, B_CHUNK), donorm, loss_part], axis=0),
            dws,
            jnp.concatenate([dqn, dkvn], axis=0)]


def _adamw_small(late_all, early_all, wmv):
    n_in = 6 + 3 * len(wmv)

    def body(*refs):
        gathered, params, outs = refs[:6], refs[6:n_in], refs[n_in:]

        def total(ref):
            s = ref[0]
            for d in range(1, N_DEV):
                s = s + ref[d]
            return s

        g_late, g2048, g1024, g128, g_ws, g512 = [total(r) for r in gathered]
        arrays = (g_late, g2048, g1024, g128)

        def update(p, rows, g):
            w_ref, m_ref, v_ref = params[3 * p:3 * p + 3]
            delta, m, v = _adamw_math(w_ref[rows], g, m_ref[rows], v_ref[rows])
            for out, val in zip(outs[4 * p:4 * p + 4], (g, delta, m, v)):
                out[rows] = val

        for p, row, arr, arr_row in SMALL_PIECES:
            update(p, pl.ds(row, 1), arrays[arr][arr_row:arr_row + 1])
        update(6, slice(None), g_ws)
        update(7, slice(None), g128[0:B_GROUPS])
        outs[32][...] = g128[B_GROUPS + 1:B_GROUPS + 2]
        outs[33][...] = g512

    vmem = pl.BlockSpec(memory_space=pltpu.VMEM)
    flat = [a for t in wmv for a in t]
    out_shape = [_sds(s, F32) for s in SMALL_PARAM_SHAPES for _ in range(4)] + [_sds((1, 128), F32), _sds((2, C_RANK), F32)]
    res = pl.pallas_call(body, name="adamw_small", in_specs=[vmem] * n_in, out_specs=[vmem] * len(out_shape),
                         out_shape=out_shape,
                         compiler_params=pltpu.CompilerParams(vmem_limit_bytes=VMEM_LIMIT_V7X))(late_all, *early_all, *flat)
    return [res[4 * p:4 * p + 4] for p in range(8)], res[32], res[33]


def _exchange(name, arrs, gather, deps=()):
    n = len(arrs)
    deps = _live(deps)

    def body(*refs):
        ins, outs = refs[:n], refs[n + len(deps):2 * n + len(deps)]
        send_sems, recv_sems, local_sems = refs[2 * n + len(deps):]
        x, y, c = lax.axis_index("x"), lax.axis_index("y"), lax.axis_index("c")
        me = 4 * x + 2 * y + c

        def peer(k):
            return (x ^ (k >> 2), y ^ ((k >> 1) & 1), c ^ (k & 1))

        def copy(a, k):
            src = ins[a] if gather else ins[a].at[me ^ k]
            return pltpu.make_async_remote_copy(
                src_ref=src, dst_ref=outs[a].at[me], send_sem=send_sems.at[a, k - 1],
                recv_sem=recv_sems.at[a, k - 1], device_id=peer(k), device_id_type=MESH_ID)

        def arrival(a, k):
            src = ins[a] if gather else ins[a].at[me]
            return pltpu.make_async_remote_copy(
                src_ref=src, dst_ref=outs[a].at[me ^ k], send_sem=send_sems.at[a, k - 1],
                recv_sem=recv_sems.at[a, k - 1], device_id=peer(k), device_id_type=MESH_ID)

        own = [pltpu.make_async_copy(ins[a] if gather else ins[a].at[me], outs[a].at[me], local_sems.at[a])
               for a in range(n)]
        for cp in own:
            cp.start()
        for k in range(1, N_DEV):
            for a in range(n):
                copy(a, k).start()
        for k in range(1, N_DEV):
            for a in range(n):
                arrival(a, k).wait_recv()
        for k in range(1, N_DEV):
            for a in range(n):
                copy(a, k).wait_send()
        for cp in own:
            cp.wait()

    any_spec = pl.BlockSpec(memory_space=pl.ANY)
    out_shape = [_sds((N_DEV,) + a.shape if gather else a.shape, a.dtype) for a in arrs]
    return pl.pallas_call(
        body, name=name, in_specs=[any_spec] * (n + len(deps)), out_specs=[any_spec] * n, out_shape=out_shape,
        scratch_shapes=[pltpu.SemaphoreType.DMA((n, N_DEV - 1)), pltpu.SemaphoreType.DMA((n, N_DEV - 1)),
                        pltpu.SemaphoreType.DMA((n,))],
        compiler_params=pltpu.CompilerParams(has_side_effects=True))(*arrs, *deps)


HBM_SPEC = pl.BlockSpec(memory_space=pltpu.HBM)
SEM_SPEC = pl.BlockSpec(memory_space=pltpu.SEMAPHORE)
DATAFLOW = pltpu.SideEffectType.DATAFLOW_SIDE_EFFECTING


def _my_index():
    return 4 * lax.axis_index("x") + 2 * lax.axis_index("y") + lax.axis_index("c")


def _plan_copies(plan, refs, send_sems, recv_sems):
    x, y, c = lax.axis_index("x"), lax.axis_index("y"), lax.axis_index("c")
    return [pltpu.make_async_remote_copy(
        src_ref=src, dst_ref=dst, send_sem=send_sems.at[i], recv_sem=recv_sems.at[i],
        device_id=(x ^ (k >> 2), y ^ ((k >> 1) & 1), c ^ (k & 1)), device_id_type=MESH_ID)
        for i, (src, dst, k) in enumerate(plan(refs, 4 * x + 2 * y + c))]


def _split_call(name, bufs, waits=None, starts=None, deps=()):
    n = len(bufs)
    deps = _live(deps)
    n_wait = 2 if waits else 0

    def body(*refs):
        zones = refs[:n]
        if waits:
            for cp in _plan_copies(waits[2], zones, refs[n], refs[n + 1]):
                cp.wait_send()
                cp.wait_recv()
        if starts:
            first_out = n + n_wait + len(deps)
            for cp in _plan_copies(starts[0], zones, refs[first_out], refs[first_out + 1]):
                cp.start()
            refs[-1][...] = jnp.zeros_like(refs[-1])

    out_specs, out_shape = [], []
    if starts:
        sems = pltpu.SemaphoreType.DMA((starts[1],))
        out_specs, out_shape = [SEM_SPEC, SEM_SPEC], [sems, sems]
    out_specs += [HBM_SPEC] * n
    out_shape += [pltpu.HBM(b.shape, b.dtype) for b in bufs]
    if starts:
        out_specs.append(pl.BlockSpec(memory_space=pltpu.VMEM))
        out_shape.append(_sds((8, 128), F32))
    first_buf = 2 if starts else 0
    res = pl.pallas_call(
        body, name=name,
        in_specs=[HBM_SPEC] * n + [SEM_SPEC] * n_wait + [ANY_SPEC] * len(deps),
        out_specs=out_specs, out_shape=out_shape,
        input_output_aliases={i: first_buf + i for i in range(n)},
        compiler_params=pltpu.CompilerParams(has_side_effects=DATAFLOW),
    )(*[pltpu.with_memory_space_constraint(b, pltpu.HBM) for b in bufs], *(waits[:2] if waits else ()), *deps)
    out_bufs = list(res[first_buf:first_buf + n])
    return out_bufs, ((res[0], res[1]) if starts else None), (res[-1] if starts else None)


def _direct_plan(n, gather):
    def plan(refs, me):
        return [(refs[a] if gather else refs[a].at[me ^ k], refs[n + a].at[me], k)
                for k in range(1, N_DEV) for a in range(n)]
    return plan


def _own_slot_filled(a, gather):
    me = _my_index()
    if gather:
        return lax.dynamic_update_slice_in_dim(lax.empty((N_DEV,) + a.shape, a.dtype), a[None], me, 0)
    return lax.dynamic_update_slice_in_dim(lax.empty(a.shape, a.dtype), lax.dynamic_slice_in_dim(a, me, 1, 0), me, 0)


def _exchange_start(name, arrs, gather, deps=()):
    n = len(arrs)
    lands = [_own_slot_filled(a, gather) for a in arrs]
    plan = _direct_plan(n, gather)
    bufs, sems, token = _split_call(name, list(arrs) + lands, starts=(plan, n * (N_DEV - 1)), deps=deps)
    return (n, plan, sems, bufs, None), token


def _exchange_wait(name, handle, after):
    return _split_done(name, handle, after)


ICI_PEERS = (2, 4, 6)
SIBLING = 1


def _gather2_send(name, arrs, deps=()):
    n = len(arrs)
    lands = [_own_slot_filled(a, True) for a in arrs]

    def plan(refs, me_):
        return [(refs[a], refs[n + a].at[me_], k) for k in (SIBLING,) + ICI_PEERS for a in range(n)]

    bufs, sems, token = _split_call(name, list(arrs) + lands, starts=(plan, 4 * n), deps=deps)
    return (n, plan, sems, bufs, None), token


def _gather2_relay(name, handle, after):
    n, plan, sems, bufs, _ = handle
    after = after if isinstance(after, (list, tuple)) else [after]

    def relay(refs, me_):
        return [(refs[n + a].at[me_ ^ k], refs[n + a].at[me_ ^ k], SIBLING) for k in ICI_PEERS for a in range(n)]

    bufs, sems2, token = _split_call(name, bufs, waits=(sems[0], sems[1], plan), starts=(relay, 3 * n), deps=after)
    return (n, relay, sems2, bufs, None), token


def _split_done(name, handle, after, all_bufs=False):
    n, plan, sems, bufs, _ = handle
    bufs, _, _ = _split_call(name, bufs, waits=(sems[0], sems[1], plan), deps=[after])
    return bufs if all_bufs else bufs[n:]


def _scatter2_pair(name, stacks, deps=()):
    n = len(stacks)
    pairs = [lax.empty((4,) + s.shape[1:], s.dtype) for s in stacks]

    def plan(refs, me):
        return [(refs[a].at[(me ^ SIBLING) ^ (2 * j)], refs[n + a].at[j], SIBLING) for j in range(4) for a in range(n)]

    bufs, sems, token = _split_call(name, list(stacks) + pairs, starts=(plan, 4 * n), deps=deps)
    return (n, plan, sems, bufs, None), token


def _pair_add(name, stack, pair, me):
    _, rows, cols = stack.shape
    tr = rows // 2

    def body(me_ref, s_ref, p_ref, o_ref):
        del me_ref
        o_ref[...] = (s_ref[...].astype(F32) + p_ref[...].astype(F32)).astype(o_ref.dtype)

    grid_spec = pltpu.PrefetchScalarGridSpec(
        num_scalar_prefetch=1, grid=(4, rows // tr),
        in_specs=[pl.BlockSpec((None, tr, cols), lambda j, i, me_ref: (me_ref[0] ^ (2 * j), i, 0)),
                  pl.BlockSpec((None, tr, cols), lambda j, i, me_ref: (j, i, 0))],
        out_specs=pl.BlockSpec((None, tr, cols), lambda j, i, me_ref: (j, i, 0)))
    return pl.pallas_call(body, name=name, grid_spec=grid_spec, out_shape=_sds((4, rows, cols), stack.dtype),
                          compiler_params=_params(2))(me.reshape(1).astype(jnp.int32), stack, pair)


def _scatter2_send(name, chip_sums, deps=()):
    n = len(chip_sums)
    finals = [lax.empty((3,) + c.shape[1:], c.dtype) for c in chip_sums]

    def plan(refs, me):
        del me
        return [(refs[a].at[j], refs[n + a].at[j - 1], 2 * j) for j in range(1, 4) for a in range(n)]

    bufs, sems, token = _split_call(name, list(chip_sums) + finals, starts=(plan, 3 * n), deps=deps)
    return (n, plan, sems, bufs, None), token


def _pad_rope(p):
    z = jnp.zeros(p.shape[:-1] + (32,), p.dtype)
    return jnp.concatenate([p[..., :32], z, p[..., 32:], z], axis=-1)


def _unpad_rope(p):
    return jnp.concatenate([p[..., :32], p[..., 64:96]], axis=-1)


def _odd_in_layout(wt):
    wt = wt.reshape(ODD_IN, D)
    cq, ckv, kpe, gate = wt[:512], wt[512:1024], wt[1024:1088], wt[1088:]
    z = jnp.zeros((32, D), wt.dtype)
    return jnp.concatenate([gate, cq, ckv, kpe[:32], z, kpe[32:], z], axis=0)


def _odd_in_unlayout(dwt):
    gate, cq, ckv, kpe = dwt[:2048], dwt[2048:2560], dwt[2560:3072], dwt[3072:]
    wt = jnp.concatenate([cq, ckv, kpe[:32], kpe[64:96], gate], axis=0)
    return wt.reshape(N_DEV, ODD_IN // N_DEV, D)


def _qb_layout(w):
    w = w.transpose(1, 0, 2).reshape(C_RANK, C_HEADS, C_QK)
    w = jnp.concatenate([w[..., :C_NOPE], _pad_rope(w[..., C_NOPE:])], axis=-1)
    return w.reshape(C_RANK, C_HEADS * QP)


def _qb_unlayout(dw):
    dw = dw.reshape(C_RANK, C_HEADS, QP)
    dw = jnp.concatenate([dw[..., :C_NOPE], _unpad_rope(dw[..., C_NOPE:])], axis=-1)
    return dw.reshape(C_RANK, N_DEV, C_HEADS * C_QK // N_DEV).transpose(1, 0, 2)


def _rope_tables(positions):
    inv_freq = ROPE_THETA ** (-jnp.arange(0, C_ROPE, 2, dtype=F32) / C_ROPE)
    ang = positions.astype(F32)[0][:, None] * inv_freq
    cos, sin = jnp.cos(ang), jnp.sin(ang)
    z = jnp.zeros_like(cos)
    return jnp.concatenate([cos, z, cos, z], axis=1), jnp.concatenate([-sin, z, sin, z], axis=1)


def _forward_backward(x, cos_t, sin_t, target, norm_pre, norm_post, lb_logits, a_onorm, ln_w, ln_b,
                      b_ws, b_bias, get_w, put_g, put_small=None, start_dep=None):
    npre0, npre1 = norm_pre[0:1], norm_pre[1:2]
    npost0, npost1 = norm_post[0:1], norm_post[1:2]
    l0, l1 = lb_logits[0:1], lb_logits[1:2]
    bias_col = b_bias.reshape(B_GROUPS, B_CHUNK, 1)
    ws = b_ws.reshape(B_GROUPS, B_CHUNK, B_CHUNK)

    h0 = _pre_norm("pre_norm0", x, npre0, deps=[start_dep])
    w_ev_in = get_w("ev_in", h0)
    z0 = _mm_nn("ev_in", h0, w_ev_in, F32, 1024, 896)
    cat, sst = _hgrn2_fwd(z0, l0, l1, a_onorm)
    cat = _gmlp_fwd(z0, cat, ln_w, ln_b, ws, bias_col)
    w_ev_out = get_w("ev_out", cat)
    y0 = _mm_nn("ev_out", cat, w_ev_out, F32, 1024, 1024)
    x1, h1 = _post_pre_norm(x, y0, npost0, npre1)
    w_od_in, w_qb, w_kvb, q_norm, kv_norm = get_w("od_mid", h1)
    z1 = _mm_nt("od_in", h1, w_od_in[None], F32, 1024, 640)
    cqn, ckvn, kp = _mla_pre(z1, q_norm, kv_norm, cos_t, sin_t)
    q = _mm_nn("od_qb", cqn, w_qb[None], F32, 1024, 1024)
    kv = _mm_nn("od_kvb", ckvn, w_kvb, BF16, 1024, 512)
    o, lse = _attention_fwd(q, cos_t, sin_t, kv, kp)
    og = _gate_out(o, z1)
    w_od_out = get_w("od_out", og)
    y1 = _mm_nn("od_out", og, w_od_out, F32, 1024, 1024)
    dx2, dy1, loss_part, dnpost1 = _final_loss(x1, y1, npost1, target)

    g_od_out = _mm_tn("od_out_dw", og, dy1, 1, BF16, 1024, 1024)
    tok = put_g("od_out", [g_od_out.reshape(N_DEV, D // N_DEV, D)])
    dog = _mm_nt("od_out_dx", dy1, w_od_out, F32, 1024, 1024, deps=[tok])
    do, dgate = _gate_out_bwd(o, z1, dog)
    dq, dkv, dkp = _attention_bwd(q, cos_t, sin_t, kv, kp, o, lse, do)
    g_qb = _mm_tn("od_qb_dw", cqn, dq, 1, F32, 512, 1024)
    g_kvb = _mm_tn("od_kvb_dw", ckvn, dkv, N_DEV, BF16, 512, 512)
    tok = put_g("od_qkv", [_qb_unlayout(g_qb[0]).astype(BF16), g_kvb])
    dcqn = _mm_nt("od_qb_dx", dq, w_qb[None], F32, 1024, 512, deps=[tok])
    dckvn = _mm_nt("od_kvb_dx", dkv, w_kvb, F32, 1024, 512)
    dcq, dckv, dkpe, dqn, dkvn = _mla_pre_bwd(z1, q_norm, kv_norm, cos_t, sin_t, dcqn, dckvn, dkp)
    dz1 = jnp.concatenate([dgate, dcq, dckv, dkpe], axis=1)
    g_od_in = _mm_tn("od_in_dw", dz1, h1, 1, F32, 640, 1024)
    tok = put_g("od_in", [_odd_in_unlayout(g_od_in[0]).astype(BF16)])
    dh1 = _mm_nn("od_in_dx", dz1, w_od_in[None], F32, 1024, 1024, deps=[tok])
    dx1, dy0, dnpost0, dnpre1 = _post_pre_norm_bwd(y0, x1, npost0, npre1, dx2, dh1)

    g_ev_out = _mm_tn("ev_out_dw", cat, dy0, 1, BF16, 1024, 1024)
    tok = put_g("ev_out", [g_ev_out.reshape(N_DEV, D // N_DEV, D)])
    dcat = _mm_nt("ev_out_dx", dy0, w_ev_out, F32, 1024, 1024, deps=[tok])
    dqa, dfa, dia, dga, dl0, dl1, donorm = _hgrn2_bwd(z0, l0, l1, a_onorm, sst, dcat)
    dub, dvb, dgb, dlnw, dlnb, dws, dbias = _gmlp_bwd(z0, ln_w, ln_b, ws, bias_col, dcat)
    dz0 = jnp.concatenate([dqa, dfa, dia, dga, dub, dvb, dgb], axis=1)
    early = _small_rows(dnpre1, dnpost0, dnpost1, dl0, dl1, donorm, dlnw, dlnb, dws, dbias, dqn, dkvn, loss_part)
    tok = put_small(early) if put_small else None
    g_ev_in = _mm_tn("ev_in_dw", h0, dz0, N_DEV, BF16, 1024, 896, deps=[tok])
    tok = put_g("ev_in", [g_ev_in])
    dh0 = _mm_nt("ev_in_dx", dz0, w_ev_in, F32, 1024, 256, deps=[tok])
    grad_x, dnpre0 = _pre_norm_bwd(x, npre0, dh0, dx1)
    return grad_x, early, dnpre0


def kernel(x, positions, norm_pre, norm_post, ev_w_in, ev_lb_logits, ev_a_onorm, ev_b_ln_w, ev_b_ln_b, ev_b_ws, ev_b_bias, ev_w_out, od_w_in, od_q_norm, od_w_qb, od_kv_norm, od_w_kvb, od_w_out, loss_target, m_norm_pre, m_norm_post, m_ev_w_in, m_ev_lb_logits, m_ev_a_onorm, m_ev_b_ln_w, m_ev_b_ln_b, m_ev_b_ws, m_ev_b_bias, m_ev_w_out, m_od_w_in, m_od_q_norm, m_od_w_qb, m_od_kv_norm, m_od_w_kvb, m_od_w_out, v_norm_pre, v_norm_post, v_ev_w_in, v_ev_lb_logits, v_ev_a_onorm, v_ev_b_ln_w, v_ev_b_ln_b, v_ev_b_ws, v_ev_b_bias, v_ev_w_out, v_od_w_in, v_od_q_norm, v_od_w_qb, v_od_kv_norm, v_od_w_kvb, v_od_w_out):
    me = 4 * lax.axis_index("x") + 2 * lax.axis_index("y") + lax.axis_index("c")
    bf = lambda w: w[0].astype(BF16)

    norms = jnp.pad(jnp.concatenate([od_q_norm, od_kv_norm], axis=1), ((0, 7), (0, 0)))
    first_h, tok = _gather2_send("gather_ev_in", [bf(ev_w_in)])
    rest_h, tok = _gather2_send("gather_rest", [bf(ev_w_out), od_w_in[0].T.astype(BF16), bf(od_w_qb), bf(od_w_kvb),
                                                norms, bf(od_w_out)], deps=[tok])
    cos_t, sin_t = _rope_tables(positions)
    rest = []

    def get_w(group, after):
        if group == "ev_in":
            relayed, token = _gather2_relay("relay_ev_in", first_h, [after, cos_t, sin_t])
            return _split_done("arrived_ev_in", relayed, token)[0]
        if not rest:
            relayed, token = _gather2_relay("relay_rest", rest_h, after)
            rest.extend(_split_done("arrived_rest", relayed, token))
        w_ev_out, w_od_in, w_qb, w_kvb, norms_all, w_od_out = rest
        if group == "ev_out":
            return w_ev_out.reshape(1, D, D)
        if group == "od_out":
            return w_od_out.reshape(1, D, D)
        return (_odd_in_layout(w_od_in), _qb_layout(w_qb), w_kvb,
                norms_all[:, 0, :64].reshape(1, C_RANK), norms_all[:, 0, 64:].reshape(1, C_RANK))

    scatters = {}

    def put_g(group, grads):
        if group == "ev_in":
            paired, token = _scatter2_pair("pair_ev_in", grads)
            n = len(grads)
            bufs = _split_done("paired_ev_in", paired, token, all_bufs=True)
            chip_sums = [_pair_add("pair_add_ev_in", bufs[a], bufs[n + a], me) for a in range(n)]
            scatters[group], token = _scatter2_send("scatter_ev_in", chip_sums)
        else:
            scatters[group], token = _exchange_start("scatter_" + group, grads, False)
        return token

    def put_small(early):
        scatters["small"], token = _exchange_start("gather_small_early", early, True)
        return token

    grad_x, _, dnpre0 = _forward_backward(
        x[0], cos_t, sin_t, loss_target[0], norm_pre, norm_post, ev_lb_logits, ev_a_onorm, ev_b_ln_w,
        ev_b_ln_b, ev_b_ws, ev_b_bias, get_w, put_g, put_small, start_dep=tok)

    big_w = {"ev_w_in": ev_w_in, "ev_w_out": ev_w_out, "od_w_in": od_w_in, "od_w_qb": od_w_qb,
             "od_w_kvb": od_w_kvb, "od_w_out": od_w_out}
    big_m = {"ev_w_in": m_ev_w_in, "ev_w_out": m_ev_w_out, "od_w_in": m_od_w_in, "od_w_qb": m_od_w_qb,
             "od_w_kvb": m_od_w_kvb, "od_w_out": m_od_w_out}
    big_v = {"ev_w_in": v_ev_w_in, "ev_w_out": v_ev_w_out, "od_w_in": v_od_w_in, "od_w_qb": v_od_w_qb,
             "od_w_kvb": v_od_w_kvb, "od_w_out": v_od_w_out}
    big_out = {}
    after = grad_x
    for group, names in (("od_out", ["od_w_out"]), ("od_qkv", ["od_w_qb", "od_w_kvb"]), ("od_in", ["od_w_in"]),
                         ("ev_out", ["ev_w_out"])):
        parts = _exchange_wait("summed_" + group, scatters[group], after)
        for nm, p in zip(names, parts):
            w, m, v = big_w[nm][0], big_m[nm][0], big_v[nm][0]
            if nm == "od_w_in":
                res_t = _adamw("adamw_" + nm, [(p, N_DEV)], w.T, m.T, v.T, w.shape[1], 512)
                big_out[nm] = [r.T[None] for r in res_t]
            else:
                big_out[nm] = [r[None] for r in _adamw("adamw_" + nm, [(p, N_DEV)], w, m, v, w.shape[0] // 8)]
            after = big_out[nm][0]

    late_all = _exchange("gather_small_late", [dnpre0], gather=True, deps=[after])[0]
    early_all = _exchange_wait("arrived_small_early", scatters["small"], late_all)

    small_w = (norm_pre, norm_post, ev_lb_logits, ev_a_onorm, ev_b_ln_w, ev_b_ln_b, ev_b_ws, ev_b_bias)
    small_m = (m_norm_pre, m_norm_post, m_ev_lb_logits, m_ev_a_onorm, m_ev_b_ln_w, m_ev_b_ln_b, m_ev_b_ws, m_ev_b_bias)
    small_v = (v_norm_pre, v_norm_post, v_ev_lb_logits, v_ev_a_onorm, v_ev_b_ln_w, v_ev_b_ln_b, v_ev_b_ws, v_ev_b_bias)
    wmv = [tuple(a.reshape(s) for a in t) for s, t in zip(SMALL_PARAM_SHAPES, zip(small_w, small_m, small_v))]
    small_res, loss_row, g_norm_rows = _adamw_small(late_all, early_all, wmv)
    small_out = [[r.reshape(w.shape) for r in four] for four, w in zip(small_res, small_w)]
    loss = loss_row[0, 0]

    g_norms = jnp.concatenate([lax.dynamic_slice(g_norm_rows, (0, 64 * me), (1, 64)),
                               lax.dynamic_slice(g_norm_rows, (1, 64 * me), (1, 64))], axis=1)
    res_n = _adamw("adamw_norms", [(g_norms[None], 1)],
                   jnp.concatenate([od_q_norm, od_kv_norm], axis=1),
                   jnp.concatenate([m_od_q_norm, m_od_kv_norm], axis=1),
                   jnp.concatenate([v_od_q_norm, v_od_kv_norm], axis=1), 1)
    qn_out = [r[:, :64] for r in res_n]
    kvn_out = [r[:, 64:] for r in res_n]

    chip_sum, from_peers = _split_done("summed_ev_in", scatters["ev_in"], loss_row, all_bufs=True)
    w = ev_w_in[0]
    big_out["ev_w_in"] = [r[None] for r in _adamw("adamw_ev_w_in", [(chip_sum, 1), (from_peers, 3)], w, m_ev_w_in[0],
                                                  v_ev_w_in[0], w.shape[0] // 8)]

    order = ("norm_pre", "norm_post", "ev_w_in", "ev_lb_logits", "ev_a_onorm", "ev_b_ln_w", "ev_b_ln_b",
             "ev_b_ws", "ev_b_bias", "ev_w_out", "od_w_in", "od_q_norm", "od_w_qb", "od_kv_norm",
             "od_w_kvb", "od_w_out")
    small_names = ("norm_pre", "norm_post", "ev_lb_logits", "ev_a_onorm", "ev_b_ln_w", "ev_b_ln_b",
                   "ev_b_ws", "ev_b_bias")
    outs = [loss, grad_x[None]]
    for kind in range(4):
        for nm in order:
            if nm in big_out:
                outs.append(big_out[nm][kind])
            elif nm == "od_q_norm":
                outs.append(qn_out[kind])
            elif nm == "od_kv_norm":
                outs.append(kvn_out[kind])
            else:
                outs.append(small_out[small_names.index(nm)][kind])
    return tuple(outs)
```

```python
import functools

import jax
import jax.numpy as jnp
from jax import lax
from jax.experimental import pallas as pl
from jax.experimental.pallas import tpu as pltpu

F32 = jnp.float32
BF16 = jnp.bfloat16

N_DEV = 8
T = 2048
D = 2048
EPS = 1e-6
A_HEADS = 8
HD = 128
A_CHUNK = 64
A_SUB = 16
B_GROUPS = 8
B_CHUNK = 128
EVEN_IN = 7168
C_HEADS = 16
C_RANK = 512
C_NOPE = 128
C_ROPE = 64
C_QK = C_NOPE + C_ROPE
C_V = 128
ODD_IN = 3136
ODD_IN_PAD = 3200
QP = 256
ROPE_THETA = 10000.0
ATT_SCALE = C_QK ** -0.5

ADAM_LR = 0.001
ADAM_B1 = 0.9
ADAM_B2 = 0.999
ADAM_EPS = 1e-08
ADAM_WD = 0.01
ADAM_STEP = 10

VMEM_LIMIT_V7X = 56 * 1024 * 1024
MESH_ID = pl.DeviceIdType.MESH


def _params(n_grid):
    return pltpu.CompilerParams(dimension_semantics=("arbitrary",) * n_grid,
                                vmem_limit_bytes=VMEM_LIMIT_V7X)


def _dg(a, b, ca, cb):
    return lax.dot_general(a.astype(BF16), b.astype(BF16), (((ca,), (cb,)), ((), ())),
                           preferred_element_type=F32)


def _raw_nn(a, b):
    return _dg(a, b, 1, 0)


def _raw_nt(a, b):
    return _dg(a, b, 1, 1)


def _raw_tn(a, b):
    return _dg(a, b, 0, 0)


@jax.custom_vjp
def _dot_nn(a, b):
    return _raw_nn(a, b)


def _dot_nn_fwd(a, b):
    return _raw_nn(a, b), (a.astype(BF16), b.astype(BF16))


def _dot_nn_bwd(res, g):
    a, b = res
    return _raw_nt(g, b), _raw_tn(a, g)


_dot_nn.defvjp(_dot_nn_fwd, _dot_nn_bwd)


@jax.custom_vjp
def _dot_nt(a, b):
    return _raw_nt(a, b)


def _dot_nt_fwd(a, b):
    return _raw_nt(a, b), (a.astype(BF16), b.astype(BF16))


def _dot_nt_bwd(res, g):
    a, b = res
    return _raw_nn(g, b), _raw_tn(g, a)


_dot_nt.defvjp(_dot_nt_fwd, _dot_nt_bwd)


@jax.custom_vjp
def _dot_tn(a, b):
    return _raw_tn(a, b)


def _dot_tn_fwd(a, b):
    return _raw_tn(a, b), (a.astype(BF16), b.astype(BF16))


def _dot_tn_bwd(res, g):
    a, b = res
    return _raw_nt(b, g), _raw_nn(a, g)


_dot_tn.defvjp(_dot_tn_fwd, _dot_tn_bwd)


@jax.custom_vjp
def _sigmoid(x):
    e = jnp.exp(-jnp.abs(x))
    return jnp.where(x >= 0, 1.0 / (1.0 + e), e / (1.0 + e))


def _sigmoid_fwd(x):
    s = _sigmoid(x)
    return s, s


def _sigmoid_bwd(s, g):
    return (g * s * (1.0 - s),)


_sigmoid.defvjp(_sigmoid_fwd, _sigmoid_bwd)


def _silu(x):
    return x * _sigmoid(x)


def _rms(x, w):
    return x * lax.rsqrt(jnp.mean(x * x, axis=-1, keepdims=True) + EPS) * w


def _split3(x):
    hi = x.astype(BF16)
    r = x - hi.astype(F32)
    mid = r.astype(BF16)
    lo = (r - mid.astype(F32)).astype(BF16)
    return hi, mid, lo


def _mask_apply(mask_bf16, x, contract):
    out = None
    for piece in _split3(x):
        d = lax.dot_general(mask_bf16, piece, (((contract,), (0,)), ((), ())),
                            preferred_element_type=F32)
        out = d if out is None else out + d
    return out


def _chunk_tri(rows):
    r = lax.broadcasted_iota(jnp.int32, (rows, rows), 0)
    c = lax.broadcasted_iota(jnp.int32, (rows, rows), 1)
    return ((r >= c) & (r // A_CHUNK == c // A_CHUNK)).astype(BF16)


@jax.custom_vjp
def _chunk_cumsum(x):
    return _mask_apply(_chunk_tri(x.shape[0]), x, 1)


def _chunk_cumsum_fwd(x):
    return _chunk_cumsum(x), None


def _chunk_cumsum_bwd(_, g):
    return (_mask_apply(_chunk_tri(g.shape[0]), g, 0),)


_chunk_cumsum.defvjp(_chunk_cumsum_fwd, _chunk_cumsum_bwd)


def _hgrn2_rows(q, zf, v, ga, st, l0, l1, onorm):
    rows = q.shape[0]
    n_sub = A_CHUNK // A_SUB
    mx = jnp.maximum(l0, l1)
    e0 = jnp.exp(l0 - mx)
    e1 = jnp.exp(l1 - mx)
    lb = e0 / (e0 + e1)
    lf = jnp.log(lb + (1.0 - lb) * _sigmoid(zf))
    k = (1.0 - lb) * _sigmoid(-zf)
    b = _chunk_cumsum(lf)

    t_idx = lax.broadcasted_iota(jnp.int32, (A_CHUNK, n_sub * A_CHUNK), 0)
    c_idx = lax.broadcasted_iota(jnp.int32, (A_CHUNK, n_sub * A_CHUNK), 1)
    sel = (c_idx // A_CHUNK == t_idx // A_SUB) & (c_idx % A_CHUNK <= t_idx)
    key_row = lax.broadcasted_iota(jnp.int32, (A_CHUNK, HD), 0)

    outs = []
    for n in range(rows // A_CHUNK):
        lo = n * A_CHUNK
        qc, kc, vc = q[lo:lo + A_CHUNK], k[lo:lo + A_CHUNK], v[lo:lo + A_CHUNK]
        lfc, bc = lf[lo:lo + A_CHUNK], b[lo:lo + A_CHUNK]
        b_last = bc[A_CHUNK - 1:A_CHUNK]
        o_inter = _dot_nt(qc * jnp.exp(bc), st)
        kv_t = _dot_tn(vc, kc * jnp.exp(b_last - bc))
        st = st * jnp.exp(b_last) + kv_t
        g_rows, k_subs = [], []
        for i in range(n_sub):
            g_i = bc[i * A_SUB:i * A_SUB + 1] - lfc[i * A_SUB:i * A_SUB + 1]
            g_rows.append(jnp.broadcast_to(g_i, (A_SUB, HD)))
            expo = jnp.where(key_row < (i + 1) * A_SUB, g_i - bc, -jnp.inf)
            k_subs.append(kc * jnp.exp(expo))
        q_sub = qc * jnp.exp(bc - jnp.concatenate(g_rows, axis=0))
        scores = _dot_nt(q_sub, jnp.concatenate(k_subs, axis=0))
        scores = jnp.where(sel, scores, 0.0)
        o_intra = _dot_nn(scores, jnp.concatenate([vc] * n_sub, axis=0))
        outs.append(o_inter + o_intra)
    o = jnp.concatenate(outs, axis=0)
    return _rms(o, onorm) * _silu(ga), st


def _gmlp_rows(u, vb, gb, lnw, lnb, ws, bias):
    rows = u.shape[0]
    mu = jnp.mean(vb, axis=-1, keepdims=True)
    xc = vb - mu
    vg = xc * lax.rsqrt(jnp.mean(xc * xc, axis=-1, keepdims=True) + EPS) * lnw + lnb
    r = lax.broadcasted_iota(jnp.int32, (B_CHUNK, B_CHUNK), 0)
    c = lax.broadcasted_iota(jnp.int32, (B_CHUNK, B_CHUNK), 1)
    ws_causal = jnp.where(r >= c, ws, 0.0)
    svs = [_dot_nn(ws_causal, vg[n * B_CHUNK:(n + 1) * B_CHUNK]) + bias
           for n in range(rows // B_CHUNK)]
    return u * jnp.concatenate(svs, axis=0) * _silu(gb)


def _rope(x, cos_t, sin_t):
    return x * cos_t + pltpu.roll(x, 64, 1) * sin_t


def _rope_transpose(g, cos_t, sin_t):
    return g * cos_t + pltpu.roll(g * sin_t, 64, 1)


ANY_SPEC = pl.BlockSpec(memory_space=pl.ANY)


def _live(deps):
    return [d for d in deps if d is not None]


def _skip_deps(body, n_in, n_deps):
    def wrapped(*refs):
        return body(*refs[:n_in], *refs[n_in + n_deps:])
    return wrapped


def _pure_call(name, fn, grid, in_specs, out_specs, out_shape, args, n_acc=0, deps=()):
    deps = _live(deps)
    n_in, n_out, n_deps = len(in_specs), len(out_specs), len(deps)
    in_specs = list(in_specs) + [ANY_SPEC] * n_deps
    args = tuple(args) + tuple(deps)

    def body(*refs):
        res = fn(*[r[...] for r in refs[:n_in]])
        if not isinstance(res, (tuple, list)):
            res = (res,)
        outs = refs[n_in + n_deps:n_in + n_deps + n_out]
        for o, r in zip(outs[:n_out - n_acc], res[:n_out - n_acc]):
            o[...] = r.astype(o.dtype)
        if n_acc:
            first = functools.reduce(jnp.logical_and, [pl.program_id(i) == 0 for i in range(len(grid))])
            for o, r in zip(outs[n_out - n_acc:], res[n_out - n_acc:]):
                @pl.when(first)
                def _(o=o, r=r):
                    o[...] = r.astype(o.dtype)

                @pl.when(jnp.logical_not(first))
                def _(o=o, r=r):
                    o[...] += r.astype(o.dtype)

    return pl.pallas_call(body, name=name, grid=grid, in_specs=in_specs, out_specs=out_specs,
                          out_shape=out_shape, compiler_params=_params(len(grid)))(*args)


def _sds(shape, dtype):
    return jax.ShapeDtypeStruct(shape, dtype)


def _row_spec(tm, width, col=0):
    return pl.BlockSpec((tm, width), lambda i, col=col: (i, col))


def _full_spec(shape):
    nd = len(shape)
    return pl.BlockSpec(shape, lambda *_: (0,) * nd)


def _mm_nn(name, a, b, out_dtype, tm, tn, deps=()):
    deps = _live(deps)
    m, k = a.shape
    j, _, n = b.shape
    per = n // tn

    def body(a_ref, b_ref, o_ref):
        o_ref[...] = _raw_nn(a_ref[...], b_ref[...]).astype(o_ref.dtype)

    return pl.pallas_call(
        _skip_deps(body, 2, len(deps)), name=name, grid=(m // tm, j * per),
        in_specs=[pl.BlockSpec((tm, k), lambda i, c: (i, 0)),
                  pl.BlockSpec((None, k, tn), lambda i, c: (c // per, 0, c % per))] + [ANY_SPEC] * len(deps),
        out_specs=pl.BlockSpec((tm, tn), lambda i, c: (i, c)),
        out_shape=_sds((m, j * n), out_dtype), compiler_params=_params(2))(a, b, *deps)


def _mm_nt(name, a, b, out_dtype, tm, tn, deps=()):
    deps = _live(deps)
    m = a.shape[0]
    j, nn, n = b.shape

    def body(a_ref, b_ref, o_ref):
        b_all = b_ref[0] if j == 1 else jnp.concatenate([b_ref[s] for s in range(j)], axis=1)
        o_ref[...] = _raw_nt(a_ref[...], b_all).astype(o_ref.dtype)

    return pl.pallas_call(
        _skip_deps(body, 2, len(deps)), name=name, grid=(m // tm, nn // tn),
        in_specs=[pl.BlockSpec((tm, j * n), lambda i, c: (i, 0)),
                  pl.BlockSpec((j, tn, n), lambda i, c: (0, c, 0))] + [ANY_SPEC] * len(deps),
        out_specs=pl.BlockSpec((tm, tn), lambda i, c: (i, c)),
        out_shape=_sds((m, nn), out_dtype), compiler_params=_params(2))(a, b, *deps)


def _mm_tn(name, a, b, j, out_dtype, tm, tn, deps=()):
    deps = _live(deps)
    k, m = a.shape
    n = b.shape[1] // j
    per = n // tn

    def body(a_ref, b_ref, o_ref):
        o_ref[...] = _raw_tn(a_ref[...], b_ref[...]).astype(o_ref.dtype)

    return pl.pallas_call(
        _skip_deps(body, 2, len(deps)), name=name, grid=(m // tm, j * per),
        in_specs=[pl.BlockSpec((k, tm), lambda i, c: (0, i)),
                  pl.BlockSpec((k, tn), lambda i, c: (0, c))] + [ANY_SPEC] * len(deps),
        out_specs=pl.BlockSpec((None, tm, tn), lambda i, c: (c // per, i, c % per)),
        out_shape=_sds((j, m, n), out_dtype), compiler_params=_params(2))(a, b, *deps)


def _mm_tn_parity(name, a, b, j, parity, out_dtype, tm, deps=()):
    deps = _live(deps)
    k, m = a.shape
    n = b.shape[1] // j

    def body(par_ref, a_ref, b_ref, o_ref):
        del par_ref
        o_ref[...] = _raw_tn(a_ref[...], b_ref[...]).astype(o_ref.dtype)

    grid_spec = pltpu.PrefetchScalarGridSpec(
        num_scalar_prefetch=1, grid=(m // tm, j // 2),
        in_specs=[pl.BlockSpec((k, tm), lambda i, s, par: (0, i)),
                  pl.BlockSpec((k, n), lambda i, s, par: (0, 2 * s + par[0]))] + [ANY_SPEC] * len(deps),
        out_specs=pl.BlockSpec((None, tm, n), lambda i, s, par: (s, i, 0)))
    return pl.pallas_call(
        lambda par_ref, *refs: _skip_deps(functools.partial(body, par_ref), 2, len(deps))(*refs),
        name=name, grid_spec=grid_spec, out_shape=_sds((j // 2, m, n), out_dtype),
        compiler_params=_params(2))(parity, a, b, *deps)


TM = 256


def _pre_norm(name, x, w_row, deps=()):
    def fn(xv, w):
        return _rms(xv, w)
    return _pure_call(name, fn, (T // TM,), [_row_spec(TM, D), _full_spec((1, D))],
                      [_row_spec(TM, D)], [_sds((T, D), BF16)], (x, w_row), deps=deps)[0]


def _post_pre_norm(x, y, w_post, w_pre):
    def fn(xv, yv, wp, wn):
        x1 = xv + _rms(yv, wp)
        return x1, _rms(x1, wn)
    return _pure_call("post_pre_norm", fn, (T // TM,),
                      [_row_spec(TM, D), _row_spec(TM, D), _full_spec((1, D)), _full_spec((1, D))],
                      [_row_spec(TM, D), _row_spec(TM, D)],
                      [_sds((T, D), F32), _sds((T, D), BF16)], (x, y, w_post, w_pre))


def _post_pre_norm_bwd(y, x1, w_post, w_pre, dx1_in, dh1, deps=()):
    def fn(yv, x1v, wp, wn, dx1v, dh1v):
        _, vjp_pre = jax.vjp(_rms, x1v, wn)
        dx1_h, dwn = vjp_pre(dh1v)
        dx1 = dx1v + dx1_h
        _, vjp_post = jax.vjp(_rms, yv, wp)
        dy, dwp = vjp_post(dx1)
        return dx1, dy, dwp, dwn
    return _pure_call("post_pre_norm_bwd", fn, (T // TM,),
                      [_row_spec(TM, D), _row_spec(TM, D), _full_spec((1, D)), _full_spec((1, D)),
                       _row_spec(TM, D), _row_spec(TM, D)],
                      [_row_spec(TM, D), _row_spec(TM, D), _full_spec((1, D)), _full_spec((1, D))],
                      [_sds((T, D), F32), _sds((T, D), BF16), _sds((1, D), F32), _sds((1, D), F32)],
                      (y, x1, w_post, w_pre, dx1_in, dh1), n_acc=2, deps=deps)


def _final_loss(x1, y, w_post, target):
    def fn(x1v, yv, wp, tv):
        r, vjp = jax.vjp(_rms, yv, wp)
        err = x1v + r - tv
        part = 0.5 * jnp.sum(jnp.mean(err * err, axis=-1, keepdims=True), axis=0, keepdims=True)
        dx2 = err * (1.0 / D)
        dy, dwp = vjp(dx2)
        return dx2, dy, jnp.broadcast_to(part, (1, 128)), dwp
    return _pure_call("final_loss", fn, (T // TM,),
                      [_row_spec(TM, D), _row_spec(TM, D), _full_spec((1, D)), _row_spec(TM, D)],
                      [_row_spec(TM, D), _row_spec(TM, D), _full_spec((1, 128)), _full_spec((1, D))],
                      [_sds((T, D), F32), _sds((T, D), BF16), _sds((1, 128), F32), _sds((1, D), F32)],
                      (x1, y, w_post, target), n_acc=2)


def _pre_norm_bwd(x, w_row, dh, dx_res, deps=()):
    def fn(xv, w, dhv, dxv):
        _, vjp = jax.vjp(_rms, xv, w)
        dx, dw = vjp(dhv)
        return dxv + dx, dw
    return _pure_call("pre_norm_bwd", fn, (T // TM,),
                      [_row_spec(TM, D), _full_spec((1, D)), _row_spec(TM, D), _row_spec(TM, D)],
                      [_row_spec(TM, D), _full_spec((1, D))],
                      [_sds((T, D), F32), _sds((1, D), F32)], (x, w_row, dh, dx_res), n_acc=1, deps=deps)


RA = 256
RB = 512


def _col_spec(rows, col_of):
    return pl.BlockSpec((rows, HD), lambda h, r, col_of=col_of: (r, col_of(h)))


HA = 4
A_GROUPS = A_HEADS // HA


def _head(ref, hh):
    return ref[:, hh * HD:(hh + 1) * HD]


def _hgrn2_fwd(z, l0, l1, onorm):
    nb = T // RA

    def body(q_ref, f_ref, v_ref, g_ref, l0_ref, l1_ref, on_ref, cat_ref, sst_ref, st_scr):
        @pl.when(pl.program_id(1) == 0)
        def _():
            st_scr[...] = jnp.zeros_like(st_scr)

        for hh in range(HA):
            st = st_scr[hh]
            sst_ref[hh] = st
            out, st_new = _hgrn2_rows(_head(q_ref, hh), _head(f_ref, hh), _head(v_ref, hh), _head(g_ref, hh), st,
                                      _head(l0_ref, hh), _head(l1_ref, hh), on_ref[...])
            cat_ref[:, hh * HD:(hh + 1) * HD] = out.astype(cat_ref.dtype)
            st_scr[hh] = st_new

    def cols(k):
        return pl.BlockSpec((RA, HA * HD), lambda g, r: (r, k * A_GROUPS + g))

    vec = pl.BlockSpec((1, HA * HD), lambda g, r: (0, g))
    return pl.pallas_call(
        body, name="hgrn2_fwd", grid=(A_GROUPS, nb),
        in_specs=[cols(0), cols(1), cols(2), cols(3), vec, vec, _full_spec((1, HD))],
        out_specs=[cols(0), pl.BlockSpec((HA, None, HD, HD), lambda g, r: (g, r, 0, 0))],
        out_shape=[_sds((T, 2 * A_HEADS * HD), BF16), _sds((A_HEADS, nb, HD, HD), F32)],
        scratch_shapes=[pltpu.VMEM((HA, HD, HD), F32)],
        compiler_params=_params(2))(z, z, z, z, l0, l1, onorm)


def _hgrn2_bwd(z, l0, l1, onorm, sst, dcat, deps=()):
    nb = T // RA
    deps = _live(deps)

    def body(q_ref, f_ref, v_ref, g_ref, l0_ref, l1_ref, on_ref, sst_ref, dcat_ref,
             dq_ref, df_ref, dv_ref, dg_ref, dl0_ref, dl1_ref, don_ref, ds_scr):
        g, r = pl.program_id(0), pl.program_id(1)

        @pl.when(r == 0)
        def _():
            ds_scr[...] = jnp.zeros_like(ds_scr)

        dl0s, dl1s, don = [], [], None
        for hh in range(HA):
            _, vjp = jax.vjp(_hgrn2_rows, _head(q_ref, hh), _head(f_ref, hh), _head(v_ref, hh), _head(g_ref, hh),
                             sst_ref[hh], _head(l0_ref, hh), _head(l1_ref, hh), on_ref[...])
            dq, dzf, dv, dga, dst, dl0, dl1, don_h = vjp((_head(dcat_ref, hh), ds_scr[hh]))
            for ref, val in ((dq_ref, dq), (df_ref, dzf), (dv_ref, dv), (dg_ref, dga)):
                ref[:, hh * HD:(hh + 1) * HD] = val.astype(ref.dtype)
            ds_scr[hh] = dst
            dl0s.append(dl0)
            dl1s.append(dl1)
            don = don_h if don is None else don + don_h
        dl0 = jnp.concatenate(dl0s, axis=1)
        dl1 = jnp.concatenate(dl1s, axis=1)

        @pl.when(r == 0)
        def _():
            dl0_ref[...] = dl0
            dl1_ref[...] = dl1

        @pl.when(r > 0)
        def _():
            dl0_ref[...] += dl0
            dl1_ref[...] += dl1

        first = jnp.logical_and(g == 0, r == 0)

        @pl.when(first)
        def _():
            don_ref[...] = don

        @pl.when(jnp.logical_not(first))
        def _():
            don_ref[...] += don

    def rev(k):
        return pl.BlockSpec((RA, HA * HD), lambda g, r: (nb - 1 - r, k * A_GROUPS + g))

    vec = pl.BlockSpec((1, HA * HD), lambda g, r: (0, g))
    grad = _sds((T, A_HEADS * HD), BF16)
    return pl.pallas_call(
        _skip_deps(body, 9, len(deps)), name="hgrn2_bwd", grid=(A_GROUPS, nb),
        in_specs=[rev(0), rev(1), rev(2), rev(3), vec, vec, _full_spec((1, HD)),
                  pl.BlockSpec((HA, None, HD, HD), lambda g, r: (g, nb - 1 - r, 0, 0)),
                  rev(0)] + [ANY_SPEC] * len(deps),
        out_specs=[rev(0)] * 4 + [vec, vec, _full_spec((1, HD))],
        out_shape=[grad] * 4 + [_sds((1, A_HEADS * HD), F32)] * 2 + [_sds((1, HD), F32)],
        scratch_shapes=[pltpu.VMEM((HA, HD, HD), F32)],
        compiler_params=_params(2))(z, z, z, z, l0, l1, onorm, sst, dcat, *deps)


def _gmlp_specs():
    vec = pl.BlockSpec((1, HD), lambda g, r: (0, g))
    ws = pl.BlockSpec((None, B_CHUNK, B_CHUNK), lambda g, r: (g, 0, 0))
    bias = pl.BlockSpec((None, B_CHUNK, 1), lambda g, r: (g, 0, 0))
    return vec, ws, bias


def _gmlp_fwd(z, cat, lnw, lnb, ws, bias):
    vec, ws_spec, bias_spec = _gmlp_specs()

    def body(u_ref, v_ref, g_ref, lnw_ref, lnb_ref, ws_ref, bias_ref, cat_in_ref, cat_ref):
        del cat_in_ref
        out = _gmlp_rows(u_ref[...], v_ref[...], g_ref[...], lnw_ref[...], lnb_ref[...],
                         ws_ref[...], bias_ref[...])
        cat_ref[...] = out.astype(cat_ref.dtype)

    return pl.pallas_call(
        body, name="gmlp_fwd", grid=(B_GROUPS, T // RB),
        in_specs=[_col_spec(RB, lambda g: 32 + g), _col_spec(RB, lambda g: 40 + g),
                  _col_spec(RB, lambda g: 48 + g), vec, vec, ws_spec, bias_spec,
                  pl.BlockSpec(memory_space=pl.ANY)],
        out_specs=_col_spec(RB, lambda g: A_HEADS + g),
        out_shape=_sds(cat.shape, cat.dtype), input_output_aliases={7: 0},
        compiler_params=_params(2))(z, z, z, lnw, lnb, ws, bias, cat)


def _gmlp_bwd(z, lnw, lnb, ws, bias, dcat):
    vec, ws_spec, bias_spec = _gmlp_specs()

    def fn(u, vb, gb, w, b, wsv, bv, dout):
        _, vjp = jax.vjp(_gmlp_rows, u, vb, gb, w, b, wsv, bv)
        return vjp(dout)

    def body(*refs):
        ins, outs = refs[:8], refs[8:]
        res = fn(*[r[...] for r in ins])
        for o, r in zip(outs[:3], res[:3]):
            o[...] = r.astype(o.dtype)
        first = pl.program_id(1) == 0
        for o, r in zip(outs[3:], res[3:]):
            @pl.when(first)
            def _(o=o, r=r):
                o[...] = r

            @pl.when(jnp.logical_not(first))
            def _(o=o, r=r):
                o[...] += r

    grad = _sds((T, B_GROUPS * HD), BF16)
    row_out = pl.BlockSpec((RB, HD), lambda g, r: (r, g))
    return pl.pallas_call(
        body, name="gmlp_bwd", grid=(B_GROUPS, T // RB),
        in_specs=[_col_spec(RB, lambda g: 32 + g), _col_spec(RB, lambda g: 40 + g),
                  _col_spec(RB, lambda g: 48 + g), vec, vec, ws_spec, bias_spec,
                  _col_spec(RB, lambda g: A_HEADS + g)],
        out_specs=[row_out] * 3 + [vec, vec, ws_spec, bias_spec],
        out_shape=[grad] * 3 + [_sds((1, B_GROUPS * HD), F32)] * 2
        + [_sds((B_GROUPS, B_CHUNK, B_CHUNK), F32), _sds((B_GROUPS, B_CHUNK, 1), F32)],
        compiler_params=_params(2))(z, z, z, lnw, lnb, ws, bias, dcat)


def _mla_pre(z1, qn, kvn, cos_t, sin_t):
    def fn(cq, ckv, kpe, cs, sn, wq, wkv):
        return _rms(cq, wq), _rms(ckv, wkv), _rope(kpe, cs, sn)
    return _pure_call("mla_pre", fn, (T // TM,),
                      [_row_spec(TM, C_RANK, 4), _row_spec(TM, C_RANK, 5), _row_spec(TM, HD, 24),
                       _row_spec(TM, HD), _row_spec(TM, HD),
                       _full_spec((1, C_RANK)), _full_spec((1, C_RANK))],
                      [_row_spec(TM, C_RANK), _row_spec(TM, C_RANK), _row_spec(TM, HD)],
                      [_sds((T, C_RANK), BF16), _sds((T, C_RANK), BF16), _sds((T, HD), BF16)],
                      (z1, z1, z1, cos_t, sin_t, qn, kvn))


def _mla_pre_bwd(z1, qn, kvn, cos_t, sin_t, dcqn, dckvn, dkp, deps=()):
    def fn(cq, ckv, cs, sn, wq, wkv, g_q, g_kv, g_kp):
        _, vjp_q = jax.vjp(_rms, cq, wq)
        dcq, dwq = vjp_q(g_q)
        _, vjp_kv = jax.vjp(_rms, ckv, wkv)
        dckv, dwkv = vjp_kv(g_kv)
        return dcq, dckv, _rope_transpose(g_kp, cs, sn), dwq, dwkv
    return _pure_call("mla_pre_bwd", fn, (T // TM,),
                      [_row_spec(TM, C_RANK, 4), _row_spec(TM, C_RANK, 5),
                       _row_spec(TM, HD), _row_spec(TM, HD),
                       _full_spec((1, C_RANK)), _full_spec((1, C_RANK)),
                       _row_spec(TM, C_RANK), _row_spec(TM, C_RANK), _row_spec(TM, HD)],
                      [_row_spec(TM, C_RANK), _row_spec(TM, C_RANK), _row_spec(TM, HD),
                       _full_spec((1, C_RANK)), _full_spec((1, C_RANK))],
                      [_sds((T, C_RANK), BF16), _sds((T, C_RANK), BF16), _sds((T, HD), BF16),
                       _sds((1, C_RANK), F32), _sds((1, C_RANK), F32)],
                      (z1, z1, cos_t, sin_t, qn, kvn, dcqn, dckvn, dkp), n_acc=2, deps=deps)


def _gate_out(o, z1):
    def fn(ov, gate):
        return ov * _silu(gate)
    return _pure_call("gate_out", fn, (T // TM,), [_row_spec(TM, D), _row_spec(TM, D, 0)],
                      [_row_spec(TM, D)], [_sds((T, D), BF16)], (o, z1))[0]


def _gate_out_bwd(o, z1, dog, deps=()):
    def fn(ov, gate, g):
        _, vjp = jax.vjp(lambda a, b: a * _silu(b), ov, gate)
        return vjp(g)
    return _pure_call("gate_out_bwd", fn, (T // TM,),
                      [_row_spec(TM, D), _row_spec(TM, D, 0), _row_spec(TM, D)],
                      [_row_spec(TM, D), _row_spec(TM, D)],
                      [_sds((T, D), F32), _sds((T, D), BF16)], (o, z1, dog), deps=deps)


TQ = 256
HP = 2
KVW = C_NOPE + C_V


def _att_keys(kv_ref, kp_ref, k_scr):
    @pl.when(pl.program_id(1) == 0)
    def _():
        for hh in range(HP):
            k_scr[hh, :, 0:C_NOPE] = kv_ref[:, hh * KVW:hh * KVW + C_NOPE]
            k_scr[hh, :, C_NOPE:QP] = kp_ref[...]


def _att_scores(q, cos_ref, sin_ref, k_scr, hh, n):
    keys = (n + 1) * TQ
    qr = jnp.concatenate([q[:, :C_NOPE], _rope(q[:, C_NOPE:], cos_ref[...], sin_ref[...])], axis=1).astype(BF16)
    return qr, _raw_nt(qr, k_scr[hh, 0:keys, :]) * ATT_SCALE


def _causal(x, n, fill):
    row = lax.broadcasted_iota(jnp.int32, (TQ, TQ), 0)
    col = lax.broadcasted_iota(jnp.int32, (TQ, TQ), 1)
    diag = jnp.where(col <= row, x[:, n * TQ:], fill)
    return diag if n == 0 else jnp.concatenate([x[:, :n * TQ], diag], axis=1)


def _per_query_block(fn):
    for n in range(T // TQ):
        pl.when(pl.program_id(1) == n)(functools.partial(fn, n))


def _att_in_specs():
    return [pl.BlockSpec((TQ, HP * QP), lambda g, i: (i, g)),
            pl.BlockSpec((TQ, HD), lambda g, i: (i, 0)),
            pl.BlockSpec((TQ, HD), lambda g, i: (i, 0)),
            pl.BlockSpec((T, HP * KVW), lambda g, i: (0, g)),
            pl.BlockSpec((T, HD), lambda g, i: (0, 0))]


def _attention_fwd(q, cos_t, sin_t, kv, kp):
    def body(q_ref, cos_ref, sin_ref, kv_ref, kp_ref, o_ref, lse_ref, k_scr):
        _att_keys(kv_ref, kp_ref, k_scr)

        def block(n):
            keys = (n + 1) * TQ
            for hh in range(HP):
                _, s = _att_scores(q_ref[:, hh * QP:(hh + 1) * QP], cos_ref, sin_ref, k_scr, hh, n)
                s = _causal(s, n, jnp.finfo(F32).min)
                m = jnp.max(s, axis=-1, keepdims=True)
                p = jnp.exp(s - m)
                l = jnp.sum(p, axis=-1, keepdims=True)
                v = kv_ref[0:keys, hh * KVW + C_NOPE:(hh + 1) * KVW]
                o_ref[:, hh * C_V:(hh + 1) * C_V] = _raw_nn(p, v) / l
                lse_ref[hh] = m + jnp.log(l)

        _per_query_block(block)

    return pl.pallas_call(
        body, name="attention_fwd", grid=(C_HEADS // HP, T // TQ), in_specs=_att_in_specs(),
        out_specs=[pl.BlockSpec((TQ, HP * C_V), lambda g, i: (i, g)),
                   pl.BlockSpec((HP, TQ, 1), lambda g, i: (g, i, 0))],
        out_shape=[_sds((T, C_HEADS * C_V), F32), _sds((C_HEADS, T, 1), F32)],
        scratch_shapes=[pltpu.VMEM((HP, T, QP), BF16)],
        compiler_params=_params(2))(q, cos_t, sin_t, kv, kp)


def _attention_bwd(q, cos_t, sin_t, kv, kp, o, lse, do):
    nq = T // TQ

    def body(q_ref, cos_ref, sin_ref, kv_ref, kp_ref, o_ref, lse_ref, do_ref,
             dq_ref, dkv_ref, dkp_ref, k_scr, dk_scr, dv_scr):
        g, i = pl.program_id(0), pl.program_id(1)
        _att_keys(kv_ref, kp_ref, k_scr)

        @pl.when(i == 0)
        def _():
            dv_scr[...] = jnp.zeros_like(dv_scr)
            dk_scr[...] = jnp.zeros_like(dk_scr)

        def block(n):
            keys = (n + 1) * TQ
            for hh in range(HP):
                qr, s = _att_scores(q_ref[:, hh * QP:(hh + 1) * QP], cos_ref, sin_ref, k_scr, hh, n)
                p = _causal(jnp.exp(s - lse_ref[hh]), n, 0.0)
                dov = do_ref[:, hh * C_V:(hh + 1) * C_V]
                delta = jnp.sum(dov * o_ref[:, hh * C_V:(hh + 1) * C_V], axis=-1, keepdims=True)
                dp = _raw_nt(dov, kv_ref[0:keys, hh * KVW + C_NOPE:(hh + 1) * KVW])
                ds = p * (dp - delta) * ATT_SCALE
                dq = _raw_nn(ds, k_scr[hh, 0:keys, :])
                dq_ref[:, hh * QP:(hh + 1) * QP] = jnp.concatenate(
                    [dq[:, :C_NOPE], _rope_transpose(dq[:, C_NOPE:], cos_ref[...], sin_ref[...])],
                    axis=1).astype(dq_ref.dtype)
                dv_scr[hh, 0:keys, :] += _raw_tn(p, dov)
                dk_scr[hh, 0:keys, :] += _raw_tn(ds, qr)

        _per_query_block(block)

        @pl.when(i == nq - 1)
        def _():
            for hh in range(HP):
                dkv_ref[:, hh * KVW:(hh + 1) * KVW] = jnp.concatenate(
                    [dk_scr[hh, :, 0:C_NOPE], dv_scr[hh]], axis=1).astype(dkv_ref.dtype)

        @pl.when(jnp.logical_and(i == nq - 1, g == 0))
        def _():
            dkp_ref[...] = dk_scr[0, :, C_NOPE:QP]

        @pl.when(jnp.logical_and(i == nq - 1, g > 0))
        def _():
            dkp_ref[...] += dk_scr[0, :, C_NOPE:QP]

        @pl.when(i == nq - 1)
        def _():
            for hh in range(1, HP):
                dkp_ref[...] += dk_scr[hh, :, C_NOPE:QP]

    return pl.pallas_call(
        body, name="attention_bwd", grid=(C_HEADS // HP, nq),
        in_specs=_att_in_specs() + [pl.BlockSpec((TQ, HP * C_V), lambda g, i: (i, g)),
                                    pl.BlockSpec((HP, TQ, 1), lambda g, i: (g, i, 0)),
                                    pl.BlockSpec((TQ, HP * C_V), lambda g, i: (i, g))],
        out_specs=[pl.BlockSpec((TQ, HP * QP), lambda g, i: (i, g)),
                   pl.BlockSpec((T, HP * KVW), lambda g, i: (0, g)),
                   _full_spec((T, HD))],
        out_shape=[_sds((T, C_HEADS * QP), BF16), _sds((T, C_HEADS * KVW), BF16), _sds((T, HD), F32)],
        scratch_shapes=[pltpu.VMEM((HP, T, QP), BF16), pltpu.VMEM((HP, T, QP), F32), pltpu.VMEM((HP, T, C_V), F32)],
        compiler_params=_params(2))(q, cos_t, sin_t, kv, kp, o, lse, do)


def _adamw_math(w, g, m, v):
    m = ADAM_B1 * m + (1.0 - ADAM_B1) * g
    v = ADAM_B2 * v + (1.0 - ADAM_B2) * (g * g)
    m_hat = m / (1.0 - ADAM_B1 ** ADAM_STEP)
    v_hat = v / (1.0 - ADAM_B2 ** ADAM_STEP)
    delta = -ADAM_LR * (m_hat / (jnp.sqrt(v_hat) + ADAM_EPS) + ADAM_WD * w)
    return delta, m, v


def _adamw(name, parts, w, m, v, tr, tc=None):
    rows, cols = w.shape

    def fn(*vals):
        pvs, (wv, mv, vv) = vals[:len(parts)], vals[len(parts):]
        g = None
        for pv in pvs:
            for d in range(pv.shape[0]):
                term = pv[d].astype(F32)
                g = term if g is None else g + term
        return (g,) + _adamw_math(wv, g, mv, vv)

    tc = cols if tc is None else tc
    blk = pl.BlockSpec((tr, tc), lambda i, j: (i, j))
    part_specs = [pl.BlockSpec((n, tr, tc), lambda i, j: (0, i, j)) for _, n in parts]
    return _pure_call(name, fn, (rows // tr, cols // tc), part_specs + [blk, blk, blk],
                      [blk] * 4, [_sds((rows, cols), F32)] * 4, tuple(p for p, _ in parts) + (w, m, v))


SMALL_PARAM_SHAPES = ((2, D), (2, D), (2, A_HEADS * HD), (1, HD), (1, B_GROUPS * HD), (1, B_GROUPS * HD),
                      (B_GROUPS, B_CHUNK, B_CHUNK), (B_GROUPS, B_CHUNK))
SMALL_PIECES = ((0, 0, 0, 0), (0, 1, 1, 0), (1, 0, 1, 1), (1, 1, 1, 2), (2, 0, 2, 0), (2, 1, 2, 1),
                (3, 0, 3, 8), (4, 0, 2, 2), (5, 0, 2, 3))


def _small_rows(dnpre1, dnpost0, dnpost1, dl0, dl1, donorm, dlnw, dlnb, dws, dbias, dqn, dkvn, loss_part):
    return [jnp.concatenate([dnpre1, dnpost0, dnpost1], axis=0),
            jnp.concatenate([dl0, dl1, dlnw, dlnb], axis=0),
            jnp.concatenate([dbias.reshape(B_GROUPS, B_CHUNK), donorm, loss_part], axis=0),
            dws,
            jnp.concatenate([dqn, dkvn], axis=0)]


def _adamw_small(late_all, early_all, wmv):
    n_in = 6 + 3 * len(wmv)

    def body(*refs):
        gathered, params, outs = refs[:6], refs[6:n_in], refs[n_in:]

        def total(ref):
            s = ref[0]
            for d in range(1, N_DEV):
                s = s + ref[d]
            return s

        g_late, g2048, g1024, g128, g_ws, g512 = [total(r) for r in gathered]
        arrays = (g_late, g2048, g1024, g128)

        def update(p, rows, g):
            w_ref, m_ref, v_ref = params[3 * p:3 * p + 3]
            delta, m, v = _adamw_math(w_ref[rows], g, m_ref[rows], v_ref[rows])
            for out, val in zip(outs[4 * p:4 * p + 4], (g, delta, m, v)):
                out[rows] = val

        for p, row, arr, arr_row in SMALL_PIECES:
            update(p, pl.ds(row, 1), arrays[arr][arr_row:arr_row + 1])
        update(6, slice(None), g_ws)
        update(7, slice(None), g128[0:B_GROUPS])
        outs[32][...] = g128[B_GROUPS + 1:B_GROUPS + 2]
        outs[33][...] = g512

    vmem = pl.BlockSpec(memory_space=pltpu.VMEM)
    flat = [a for t in wmv for a in t]
    out_shape = [_sds(s, F32) for s in SMALL_PARAM_SHAPES for _ in range(4)] + [_sds((1, 128), F32), _sds((2, C_RANK), F32)]
    res = pl.pallas_call(body, name="adamw_small", in_specs=[vmem] * n_in, out_specs=[vmem] * len(out_shape),
                         out_shape=out_shape,
                         compiler_params=pltpu.CompilerParams(vmem_limit_bytes=VMEM_LIMIT_V7X))(late_all, *early_all, *flat)
    return [res[4 * p:4 * p + 4] for p in range(8)], res[32], res[33]


def _exchange(name, arrs, gather, deps=()):
    n = len(arrs)
    deps = _live(deps)

    def body(*refs):
        ins, outs = refs[:n], refs[n + len(deps):2 * n + len(deps)]
        send_sems, recv_sems, local_sems = refs[2 * n + len(deps):]
        x, y, c = lax.axis_index("x"), lax.axis_index("y"), lax.axis_index("c")
        me = 4 * x + 2 * y + c

        def peer(k):
            return (x ^ (k >> 2), y ^ ((k >> 1) & 1), c ^ (k & 1))

        def copy(a, k):
            src = ins[a] if gather else ins[a].at[me ^ k]
            return pltpu.make_async_remote_copy(
                src_ref=src, dst_ref=outs[a].at[me], send_sem=send_sems.at[a, k - 1],
                recv_sem=recv_sems.at[a, k - 1], device_id=peer(k), device_id_type=MESH_ID)

        def arrival(a, k):
            src = ins[a] if gather else ins[a].at[me]
            return pltpu.make_async_remote_copy(
                src_ref=src, dst_ref=outs[a].at[me ^ k], send_sem=send_sems.at[a, k - 1],
                recv_sem=recv_sems.at[a, k - 1], device_id=peer(k), device_id_type=MESH_ID)

        own = [pltpu.make_async_copy(ins[a] if gather else ins[a].at[me], outs[a].at[me], local_sems.at[a])
               for a in range(n)]
        for cp in own:
            cp.start()
        for k in range(1, N_DEV):
            for a in range(n):
                copy(a, k).start()
        for k in range(1, N_DEV):
            for a in range(n):
                arrival(a, k).wait_recv()
        for k in range(1, N_DEV):
            for a in range(n):
                copy(a, k).wait_send()
        for cp in own:
            cp.wait()

    any_spec = pl.BlockSpec(memory_space=pl.ANY)
    out_shape = [_sds((N_DEV,) + a.shape if gather else a.shape, a.dtype) for a in arrs]
    return pl.pallas_call(
        body, name=name, in_specs=[any_spec] * (n + len(deps)), out_specs=[any_spec] * n, out_shape=out_shape,
        scratch_shapes=[pltpu.SemaphoreType.DMA((n, N_DEV - 1)), pltpu.SemaphoreType.DMA((n, N_DEV - 1)),
                        pltpu.SemaphoreType.DMA((n,))],
        compiler_params=pltpu.CompilerParams(has_side_effects=True))(*arrs, *deps)


HBM_SPEC = pl.BlockSpec(memory_space=pltpu.HBM)
SEM_SPEC = pl.BlockSpec(memory_space=pltpu.SEMAPHORE)
DATAFLOW = pltpu.SideEffectType.DATAFLOW_SIDE_EFFECTING


def _my_index():
    return 4 * lax.axis_index("x") + 2 * lax.axis_index("y") + lax.axis_index("c")


def _plan_copies(plan, refs, send_sems, recv_sems):
    x, y, c = lax.axis_index("x"), lax.axis_index("y"), lax.axis_index("c")
    return [pltpu.make_async_remote_copy(
        src_ref=src, dst_ref=dst, send_sem=send_sems.at[i], recv_sem=recv_sems.at[i],
        device_id=(x ^ (k >> 2), y ^ ((k >> 1) & 1), c ^ (k & 1)), device_id_type=MESH_ID)
        for i, (src, dst, k) in enumerate(plan(refs, 4 * x + 2 * y + c))]


def _split_call(name, bufs, waits=None, starts=None, deps=()):
    n = len(bufs)
    deps = _live(deps)
    n_wait = 2 if waits else 0

    def body(*refs):
        zones = refs[:n]
        if waits:
            for cp in _plan_copies(waits[2], zones, refs[n], refs[n + 1]):
                cp.wait_send()
                cp.wait_recv()
        if starts:
            first_out = n + n_wait + len(deps)
            for cp in _plan_copies(starts[0], zones, refs[first_out], refs[first_out + 1]):
                cp.start()
            refs[-1][...] = jnp.zeros_like(refs[-1])

    out_specs, out_shape = [], []
    if starts:
        sems = pltpu.SemaphoreType.DMA((starts[1],))
        out_specs, out_shape = [SEM_SPEC, SEM_SPEC], [sems, sems]
    out_specs += [HBM_SPEC] * n
    out_shape += [pltpu.HBM(b.shape, b.dtype) for b in bufs]
    if starts:
        out_specs.append(pl.BlockSpec(memory_space=pltpu.VMEM))
        out_shape.append(_sds((8, 128), F32))
    first_buf = 2 if starts else 0
    res = pl.pallas_call(
        body, name=name,
        in_specs=[HBM_SPEC] * n + [SEM_SPEC] * n_wait + [ANY_SPEC] * len(deps),
        out_specs=out_specs, out_shape=out_shape,
        input_output_aliases={i: first_buf + i for i in range(n)},
        compiler_params=pltpu.CompilerParams(has_side_effects=DATAFLOW),
    )(*[pltpu.with_memory_space_constraint(b, pltpu.HBM) for b in bufs], *(waits[:2] if waits else ()), *deps)
    out_bufs = list(res[first_buf:first_buf + n])
    return out_bufs, ((res[0], res[1]) if starts else None), (res[-1] if starts else None)


def _direct_plan(n, gather):
    def plan(refs, me):
        return [(refs[a] if gather else refs[a].at[me ^ k], refs[n + a].at[me], k)
                for k in range(1, N_DEV) for a in range(n)]
    return plan


def _own_slot_filled(a, gather):
    me = _my_index()
    if gather:
        return lax.dynamic_update_slice_in_dim(lax.empty((N_DEV,) + a.shape, a.dtype), a[None], me, 0)
    return lax.dynamic_update_slice_in_dim(lax.empty(a.shape, a.dtype), lax.dynamic_slice_in_dim(a, me, 1, 0), me, 0)


def _exchange_start(name, arrs, gather, deps=()):
    n = len(arrs)
    lands = [_own_slot_filled(a, gather) for a in arrs]
    plan = _direct_plan(n, gather)
    bufs, sems, token = _split_call(name, list(arrs) + lands, starts=(plan, n * (N_DEV - 1)), deps=deps)
    return (n, plan, sems, bufs, None), token


def _exchange_wait(name, handle, after):
    return _split_done(name, handle, after)


ICI_PEERS = (2, 4, 6)
SIBLING = 1


def _gather2_send(name, arrs, deps=()):
    n = len(arrs)
    lands = [_own_slot_filled(a, True) for a in arrs]

    def plan(refs, me_):
        return [(refs[a], refs[n + a].at[me_], k) for k in (SIBLING,) + ICI_PEERS for a in range(n)]

    bufs, sems, token = _split_call(name, list(arrs) + lands, starts=(plan, 4 * n), deps=deps)
    return (n, plan, sems, bufs, None), token


def _gather2_relay(name, handle, after):
    n, plan, sems, bufs, _ = handle
    after = after if isinstance(after, (list, tuple)) else [after]

    def relay(refs, me_):
        return [(refs[n + a].at[me_ ^ k], refs[n + a].at[me_ ^ k], SIBLING) for k in ICI_PEERS for a in range(n)]

    bufs, sems2, token = _split_call(name, bufs, waits=(sems[0], sems[1], plan), starts=(relay, 3 * n), deps=after)
    return (n, relay, sems2, bufs, None), token


def _split_done(name, handle, after, all_bufs=False):
    n, plan, sems, bufs, _ = handle
    bufs, _, _ = _split_call(name, bufs, waits=(sems[0], sems[1], plan), deps=[after])
    return bufs if all_bufs else bufs[n:]


def _scatter2_pair(name, for_sibling, deps=()):
    n = len(for_sibling)
    pairs = [lax.empty(s.shape, s.dtype) for s in for_sibling]

    def plan(refs, me):
        del me
        return [(refs[a].at[s], refs[n + a].at[s], SIBLING) for s in range(4) for a in range(n)]

    bufs, sems, token = _split_call(name, list(for_sibling) + pairs, starts=(plan, 4 * n), deps=deps)
    return (n, plan, sems, bufs, None), token


def _pair_add(name, mine, pair):
    _, rows, cols = mine.shape
    tr = rows // 2

    def fn(a, b):
        return a.astype(F32) + b.astype(F32)

    blk = pl.BlockSpec((None, tr, cols), lambda s, i: (s, i, 0))
    return _pure_call(name, fn, (4, rows // tr), [blk, blk], [blk], [_sds(mine.shape, mine.dtype)], (mine, pair))[0]


def _scatter2_send(name, chip_sums, deps=()):
    n = len(chip_sums)
    finals = [lax.empty((3,) + c.shape[1:], c.dtype) for c in chip_sums]

    def plan(refs, me):
        return [(refs[a].at[(me >> 1) ^ j], refs[n + a].at[j - 1], 2 * j) for j in range(1, 4) for a in range(n)]

    bufs, sems, token = _split_call(name, list(chip_sums) + finals, starts=(plan, 3 * n), deps=deps)
    return (n, plan, sems, bufs, None), token


def _pad_rope(p):
    z = jnp.zeros(p.shape[:-1] + (32,), p.dtype)
    return jnp.concatenate([p[..., :32], z, p[..., 32:], z], axis=-1)


def _unpad_rope(p):
    return jnp.concatenate([p[..., :32], p[..., 64:96]], axis=-1)


def _odd_in_layout(wt):
    wt = wt.reshape(ODD_IN, D)
    cq, ckv, kpe, gate = wt[:512], wt[512:1024], wt[1024:1088], wt[1088:]
    z = jnp.zeros((32, D), wt.dtype)
    return jnp.concatenate([gate, cq, ckv, kpe[:32], z, kpe[32:], z], axis=0)


def _odd_in_unlayout(dwt):
    gate, cq, ckv, kpe = dwt[:2048], dwt[2048:2560], dwt[2560:3072], dwt[3072:]
    wt = jnp.concatenate([cq, ckv, kpe[:32], kpe[64:96], gate], axis=0)
    return wt.reshape(N_DEV, ODD_IN // N_DEV, D)


def _qb_layout(w):
    w = w.transpose(1, 0, 2).reshape(C_RANK, C_HEADS, C_QK)
    w = jnp.concatenate([w[..., :C_NOPE], _pad_rope(w[..., C_NOPE:])], axis=-1)
    return w.reshape(C_RANK, C_HEADS * QP)


def _qb_unlayout(dw):
    dw = dw.reshape(C_RANK, C_HEADS, QP)
    dw = jnp.concatenate([dw[..., :C_NOPE], _unpad_rope(dw[..., C_NOPE:])], axis=-1)
    return dw.reshape(C_RANK, N_DEV, C_HEADS * C_QK // N_DEV).transpose(1, 0, 2)


def _rope_tables(positions):
    inv_freq = ROPE_THETA ** (-jnp.arange(0, C_ROPE, 2, dtype=F32) / C_ROPE)
    ang = positions.astype(F32)[0][:, None] * inv_freq
    cos, sin = jnp.cos(ang), jnp.sin(ang)
    z = jnp.zeros_like(cos)
    return jnp.concatenate([cos, z, cos, z], axis=1), jnp.concatenate([-sin, z, sin, z], axis=1)


def _forward_backward(x, cos_t, sin_t, target, norm_pre, norm_post, lb_logits, a_onorm, ln_w, ln_b,
                      b_ws, b_bias, get_w, put_g, put_small=None, start_dep=None):
    npre0, npre1 = norm_pre[0:1], norm_pre[1:2]
    npost0, npost1 = norm_post[0:1], norm_post[1:2]
    l0, l1 = lb_logits[0:1], lb_logits[1:2]
    bias_col = b_bias.reshape(B_GROUPS, B_CHUNK, 1)
    ws = b_ws.reshape(B_GROUPS, B_CHUNK, B_CHUNK)

    h0 = _pre_norm("pre_norm0", x, npre0, deps=[start_dep])
    w_ev_in = get_w("ev_in", h0)
    z0 = _mm_nn("ev_in", h0, w_ev_in, F32, 1024, 896)
    cat, sst = _hgrn2_fwd(z0, l0, l1, a_onorm)
    cat = _gmlp_fwd(z0, cat, ln_w, ln_b, ws, bias_col)
    w_ev_out = get_w("ev_out", cat)
    y0 = _mm_nn("ev_out", cat, w_ev_out, F32, 1024, 1024)
    x1, h1 = _post_pre_norm(x, y0, npost0, npre1)
    w_od_in, w_qb, w_kvb, q_norm, kv_norm = get_w("od_mid", h1)
    z1 = _mm_nt("od_in", h1, w_od_in[None], F32, 1024, 640)
    cqn, ckvn, kp = _mla_pre(z1, q_norm, kv_norm, cos_t, sin_t)
    q = _mm_nn("od_qb", cqn, w_qb[None], F32, 1024, 1024)
    kv = _mm_nn("od_kvb", ckvn, w_kvb, BF16, 1024, 512)
    o, lse = _attention_fwd(q, cos_t, sin_t, kv, kp)
    og = _gate_out(o, z1)
    w_od_out = get_w("od_out", og)
    y1 = _mm_nn("od_out", og, w_od_out, F32, 1024, 1024)
    dx2, dy1, loss_part, dnpost1 = _final_loss(x1, y1, npost1, target)

    g_od_out = _mm_tn("od_out_dw", og, dy1, 1, BF16, 1024, 1024)
    tok = put_g("od_out", [g_od_out.reshape(N_DEV, D // N_DEV, D)])
    dog = _mm_nt("od_out_dx", dy1, w_od_out, F32, 1024, 1024, deps=[tok])
    do, dgate = _gate_out_bwd(o, z1, dog)
    dq, dkv, dkp = _attention_bwd(q, cos_t, sin_t, kv, kp, o, lse, do)
    g_qb = _mm_tn("od_qb_dw", cqn, dq, 1, F32, 512, 1024)
    g_kvb = _mm_tn("od_kvb_dw", ckvn, dkv, N_DEV, BF16, 512, 512)
    tok = put_g("od_qkv", [_qb_unlayout(g_qb[0]).astype(BF16), g_kvb])
    dcqn = _mm_nt("od_qb_dx", dq, w_qb[None], F32, 1024, 512, deps=[tok])
    dckvn = _mm_nt("od_kvb_dx", dkv, w_kvb, F32, 1024, 512)
    dcq, dckv, dkpe, dqn, dkvn = _mla_pre_bwd(z1, q_norm, kv_norm, cos_t, sin_t, dcqn, dckvn, dkp)
    dz1 = jnp.concatenate([dgate, dcq, dckv, dkpe], axis=1)
    g_od_in = _mm_tn("od_in_dw", dz1, h1, 1, F32, 640, 1024)
    tok = put_g("od_in", [_odd_in_unlayout(g_od_in[0]).astype(BF16)])
    dh1 = _mm_nn("od_in_dx", dz1, w_od_in[None], F32, 1024, 1024, deps=[tok])
    dx1, dy0, dnpost0, dnpre1 = _post_pre_norm_bwd(y0, x1, npost0, npre1, dx2, dh1)

    g_ev_out = _mm_tn("ev_out_dw", cat, dy0, 1, BF16, 1024, 1024)
    tok = put_g("ev_out", [g_ev_out.reshape(N_DEV, D // N_DEV, D)])
    dcat = _mm_nt("ev_out_dx", dy0, w_ev_out, F32, 1024, 1024, deps=[tok])
    dqa, dfa, dia, dga, dl0, dl1, donorm = _hgrn2_bwd(z0, l0, l1, a_onorm, sst, dcat)
    dub, dvb, dgb, dlnw, dlnb, dws, dbias = _gmlp_bwd(z0, ln_w, ln_b, ws, bias_col, dcat)
    dz0 = jnp.concatenate([dqa, dfa, dia, dga, dub, dvb, dgb], axis=1)
    early = _small_rows(dnpre1, dnpost0, dnpost1, dl0, dl1, donorm, dlnw, dlnb, dws, dbias, dqn, dkvn, loss_part)
    tok = put_small(early) if put_small else None
    small_tok = tok

    def ev_in_half(name, parity, deps=()):
        return _mm_tn_parity(name, h0, dz0, N_DEV, parity, BF16, 1024, deps=[small_tok] + list(deps))

    tok = put_g("ev_in", ev_in_half)
    dh0 = _mm_nt("ev_in_dx", dz0, w_ev_in, F32, 1024, 256, deps=[tok])
    grad_x, dnpre0 = _pre_norm_bwd(x, npre0, dh0, dx1)
    return grad_x, early, dnpre0


def kernel(x, positions, norm_pre, norm_post, ev_w_in, ev_lb_logits, ev_a_onorm, ev_b_ln_w, ev_b_ln_b, ev_b_ws, ev_b_bias, ev_w_out, od_w_in, od_q_norm, od_w_qb, od_kv_norm, od_w_kvb, od_w_out, loss_target, m_norm_pre, m_norm_post, m_ev_w_in, m_ev_lb_logits, m_ev_a_onorm, m_ev_b_ln_w, m_ev_b_ln_b, m_ev_b_ws, m_ev_b_bias, m_ev_w_out, m_od_w_in, m_od_q_norm, m_od_w_qb, m_od_kv_norm, m_od_w_kvb, m_od_w_out, v_norm_pre, v_norm_post, v_ev_w_in, v_ev_lb_logits, v_ev_a_onorm, v_ev_b_ln_w, v_ev_b_ln_b, v_ev_b_ws, v_ev_b_bias, v_ev_w_out, v_od_w_in, v_od_q_norm, v_od_w_qb, v_od_kv_norm, v_od_w_kvb, v_od_w_out):
    me = 4 * lax.axis_index("x") + 2 * lax.axis_index("y") + lax.axis_index("c")
    bf = lambda w: w[0].astype(BF16)

    norms = jnp.pad(jnp.concatenate([od_q_norm, od_kv_norm], axis=1), ((0, 7), (0, 0)))
    first_h, tok = _gather2_send("gather_ev_in", [bf(ev_w_in)])
    rest_h, tok = _gather2_send("gather_rest", [bf(ev_w_out), od_w_in[0].T.astype(BF16), bf(od_w_qb), bf(od_w_kvb),
                                                norms, bf(od_w_out)], deps=[tok])
    cos_t, sin_t = _rope_tables(positions)
    rest = []

    def get_w(group, after):
        if group == "ev_in":
            relayed, token = _gather2_relay("relay_ev_in", first_h, [after, cos_t, sin_t])
            return _split_done("arrived_ev_in", relayed, token)[0]
        if not rest:
            relayed, token = _gather2_relay("relay_rest", rest_h, after)
            rest.extend(_split_done("arrived_rest", relayed, token))
        w_ev_out, w_od_in, w_qb, w_kvb, norms_all, w_od_out = rest
        if group == "ev_out":
            return w_ev_out.reshape(1, D, D)
        if group == "od_out":
            return w_od_out.reshape(1, D, D)
        return (_odd_in_layout(w_od_in), _qb_layout(w_qb), w_kvb,
                norms_all[:, 0, :64].reshape(1, C_RANK), norms_all[:, 0, 64:].reshape(1, C_RANK))

    scatters = {}

    def put_g(group, grads):
        if group == "ev_in":
            core = lax.axis_index("c").astype(jnp.int32).reshape(1)
            paired, token = _scatter2_pair("pair_ev_in", [grads("ev_in_dw_sibling", 1 - core)])
            mine = grads("ev_in_dw_own", core, deps=[token])
            pair = _split_done("paired_ev_in", paired, mine)[0]
            scatters[group], token = _scatter2_send("scatter_ev_in", [_pair_add("pair_add_ev_in", mine, pair)])
        else:
            scatters[group], token = _exchange_start("scatter_" + group, grads, False)
        return token

    def put_small(early):
        scatters["small"], token = _exchange_start("gather_small_early", early, True)
        return token

    grad_x, _, dnpre0 = _forward_backward(
        x[0], cos_t, sin_t, loss_target[0], norm_pre, norm_post, ev_lb_logits, ev_a_onorm, ev_b_ln_w,
        ev_b_ln_b, ev_b_ws, ev_b_bias, get_w, put_g, put_small, start_dep=tok)

    big_w = {"ev_w_in": ev_w_in, "ev_w_out": ev_w_out, "od_w_in": od_w_in, "od_w_qb": od_w_qb,
             "od_w_kvb": od_w_kvb, "od_w_out": od_w_out}
    big_m = {"ev_w_in": m_ev_w_in, "ev_w_out": m_ev_w_out, "od_w_in": m_od_w_in, "od_w_qb": m_od_w_qb,
             "od_w_kvb": m_od_w_kvb, "od_w_out": m_od_w_out}
    big_v = {"ev_w_in": v_ev_w_in, "ev_w_out": v_ev_w_out, "od_w_in": v_od_w_in, "od_w_qb": v_od_w_qb,
             "od_w_kvb": v_od_w_kvb, "od_w_out": v_od_w_out}
    big_out = {}
    after = grad_x
    for group, names in (("od_out", ["od_w_out"]), ("od_qkv", ["od_w_qb", "od_w_kvb"]), ("od_in", ["od_w_in"]),
                         ("ev_out", ["ev_w_out"])):
        parts = _exchange_wait("summed_" + group, scatters[group], after)
        for nm, p in zip(names, parts):
            w, m, v = big_w[nm][0], big_m[nm][0], big_v[nm][0]
            if nm == "od_w_in":
                res_t = _adamw("adamw_" + nm, [(p, N_DEV)], w.T, m.T, v.T, w.shape[1], 512)
                big_out[nm] = [r.T[None] for r in res_t]
            else:
                big_out[nm] = [r[None] for r in _adamw("adamw_" + nm, [(p, N_DEV)], w, m, v, w.shape[0] // 8)]
            after = big_out[nm][0]

    late_all = _exchange("gather_small_late", [dnpre0], gather=True, deps=[after])[0]
    early_all = _exchange_wait("arrived_small_early", scatters["small"], late_all)

    small_w = (norm_pre, norm_post, ev_lb_logits, ev_a_onorm, ev_b_ln_w, ev_b_ln_b, ev_b_ws, ev_b_bias)
    small_m = (m_norm_pre, m_norm_post, m_ev_lb_logits, m_ev_a_onorm, m_ev_b_ln_w, m_ev_b_ln_b, m_ev_b_ws, m_ev_b_bias)
    small_v = (v_norm_pre, v_norm_post, v_ev_lb_logits, v_ev_a_onorm, v_ev_b_ln_w, v_ev_b_ln_b, v_ev_b_ws, v_ev_b_bias)
    wmv = [tuple(a.reshape(s) for a in t) for s, t in zip(SMALL_PARAM_SHAPES, zip(small_w, small_m, small_v))]
    small_res, loss_row, g_norm_rows = _adamw_small(late_all, early_all, wmv)
    small_out = [[r.reshape(w.shape) for r in four] for four, w in zip(small_res, small_w)]
    loss = loss_row[0, 0]

    g_norms = jnp.concatenate([lax.dynamic_slice(g_norm_rows, (0, 64 * me), (1, 64)),
                               lax.dynamic_slice(g_norm_rows, (1, 64 * me), (1, 64))], axis=1)
    res_n = _adamw("adamw_norms", [(g_norms[None], 1)],
                   jnp.concatenate([od_q_norm, od_kv_norm], axis=1),
                   jnp.concatenate([m_od_q_norm, m_od_kv_norm], axis=1),
                   jnp.concatenate([v_od_q_norm, v_od_kv_norm], axis=1), 1)
    qn_out = [r[:, :64] for r in res_n]
    kvn_out = [r[:, 64:] for r in res_n]

    chip_sums, from_peers = _split_done("summed_ev_in", scatters["ev_in"], loss_row, all_bufs=True)
    own_chip = lax.dynamic_slice_in_dim(chip_sums, me >> 1, 1, 0)
    w = ev_w_in[0]
    big_out["ev_w_in"] = [r[None] for r in _adamw("adamw_ev_w_in", [(own_chip, 1), (from_peers, 3)], w, m_ev_w_in[0],
                                                  v_ev_w_in[0], w.shape[0] // 8)]

    order = ("norm_pre", "norm_post", "ev_w_in", "ev_lb_logits", "ev_a_onorm", "ev_b_ln_w", "ev_b_ln_b",
             "ev_b_ws", "ev_b_bias", "ev_w_out", "od_w_in", "od_q_norm", "od_w_qb", "od_kv_norm",
             "od_w_kvb", "od_w_out")
    small_names = ("norm_pre", "norm_post", "ev_lb_logits", "ev_a_onorm", "ev_b_ln_w", "ev_b_ln_b",
                   "ev_b_ws", "ev_b_bias")
    outs = [loss, grad_x[None]]
    for kind in range(4):
        for nm in order:
            if nm in big_out:
                outs.append(big_out[nm][kind])
            elif nm == "od_q_norm":
                outs.append(qn_out[kind])
            elif nm == "od_kv_norm":
                outs.append(kvn_out[kind])
            else:
                outs.append(small_out[small_names.index(nm)][kind])
    return tuple(outs)
```

```python
import functools

import jax
import jax.numpy as jnp
from jax import lax
from jax.experimental import pallas as pl
from jax.experimental.pallas import tpu as pltpu

F32 = jnp.float32
BF16 = jnp.bfloat16

N_DEV = 8
T = 2048
D = 2048
EPS = 1e-6
A_HEADS = 8
HD = 128
A_CHUNK = 64
A_SUB = 16
B_GROUPS = 8
B_CHUNK = 128
EVEN_IN = 7168
C_HEADS = 16
C_RANK = 512
C_NOPE = 128
C_ROPE = 64
C_QK = C_NOPE + C_ROPE
C_V = 128
ODD_IN = 3136
ODD_IN_PAD = 3200
QP = 256
ROPE_THETA = 10000.0
ATT_SCALE = C_QK ** -0.5

ADAM_LR = 0.001
ADAM_B1 = 0.9
ADAM_B2 = 0.999
ADAM_EPS = 1e-08
ADAM_WD = 0.01
ADAM_STEP = 10

VMEM_LIMIT_V7X = 56 * 1024 * 1024
MESH_ID = pl.DeviceIdType.MESH


def _params(n_grid):
    return pltpu.CompilerParams(dimension_semantics=("arbitrary",) * n_grid,
                                vmem_limit_bytes=VMEM_LIMIT_V7X)


def _dg(a, b, ca, cb):
    return lax.dot_general(a.astype(BF16), b.astype(BF16), (((ca,), (cb,)), ((), ())),
                           preferred_element_type=F32)


def _raw_nn(a, b):
    return _dg(a, b, 1, 0)


def _raw_nt(a, b):
    return _dg(a, b, 1, 1)


def _raw_tn(a, b):
    return _dg(a, b, 0, 0)


@jax.custom_vjp
def _dot_nn(a, b):
    return _raw_nn(a, b)


def _dot_nn_fwd(a, b):
    return _raw_nn(a, b), (a.astype(BF16), b.astype(BF16))


def _dot_nn_bwd(res, g):
    a, b = res
    return _raw_nt(g, b), _raw_tn(a, g)


_dot_nn.defvjp(_dot_nn_fwd, _dot_nn_bwd)


@jax.custom_vjp
def _dot_nt(a, b):
    return _raw_nt(a, b)


def _dot_nt_fwd(a, b):
    return _raw_nt(a, b), (a.astype(BF16), b.astype(BF16))


def _dot_nt_bwd(res, g):
    a, b = res
    return _raw_nn(g, b), _raw_tn(g, a)


_dot_nt.defvjp(_dot_nt_fwd, _dot_nt_bwd)


@jax.custom_vjp
def _dot_tn(a, b):
    return _raw_tn(a, b)


def _dot_tn_fwd(a, b):
    return _raw_tn(a, b), (a.astype(BF16), b.astype(BF16))


def _dot_tn_bwd(res, g):
    a, b = res
    return _raw_nt(b, g), _raw_nn(a, g)


_dot_tn.defvjp(_dot_tn_fwd, _dot_tn_bwd)


@jax.custom_vjp
def _sigmoid(x):
    e = jnp.exp(-jnp.abs(x))
    return jnp.where(x >= 0, 1.0 / (1.0 + e), e / (1.0 + e))


def _sigmoid_fwd(x):
    s = _sigmoid(x)
    return s, s


def _sigmoid_bwd(s, g):
    return (g * s * (1.0 - s),)


_sigmoid.defvjp(_sigmoid_fwd, _sigmoid_bwd)


def _silu(x):
    return x * _sigmoid(x)


def _rms(x, w):
    return x * lax.rsqrt(jnp.mean(x * x, axis=-1, keepdims=True) + EPS) * w


def _split3(x):
    hi = x.astype(BF16)
    r = x - hi.astype(F32)
    mid = r.astype(BF16)
    lo = (r - mid.astype(F32)).astype(BF16)
    return hi, mid, lo


def _mask_apply(mask_bf16, x, contract):
    out = None
    for piece in _split3(x):
        d = lax.dot_general(mask_bf16, piece, (((contract,), (0,)), ((), ())),
                            preferred_element_type=F32)
        out = d if out is None else out + d
    return out


def _chunk_tri(rows):
    r = lax.broadcasted_iota(jnp.int32, (rows, rows), 0)
    c = lax.broadcasted_iota(jnp.int32, (rows, rows), 1)
    return ((r >= c) & (r // A_CHUNK == c // A_CHUNK)).astype(BF16)


@jax.custom_vjp
def _chunk_cumsum(x):
    return _mask_apply(_chunk_tri(x.shape[0]), x, 1)


def _chunk_cumsum_fwd(x):
    return _chunk_cumsum(x), None


def _chunk_cumsum_bwd(_, g):
    return (_mask_apply(_chunk_tri(g.shape[0]), g, 0),)


_chunk_cumsum.defvjp(_chunk_cumsum_fwd, _chunk_cumsum_bwd)


def _hgrn2_rows(q, zf, v, ga, st, l0, l1, onorm):
    rows = q.shape[0]
    n_sub = A_CHUNK // A_SUB
    mx = jnp.maximum(l0, l1)
    e0 = jnp.exp(l0 - mx)
    e1 = jnp.exp(l1 - mx)
    lb = e0 / (e0 + e1)
    lf = jnp.log(lb + (1.0 - lb) * _sigmoid(zf))
    k = (1.0 - lb) * _sigmoid(-zf)
    b = _chunk_cumsum(lf)

    t_idx = lax.broadcasted_iota(jnp.int32, (A_CHUNK, n_sub * A_CHUNK), 0)
    c_idx = lax.broadcasted_iota(jnp.int32, (A_CHUNK, n_sub * A_CHUNK), 1)
    sel = (c_idx // A_CHUNK == t_idx // A_SUB) & (c_idx % A_CHUNK <= t_idx)
    key_row = lax.broadcasted_iota(jnp.int32, (A_CHUNK, HD), 0)

    outs = []
    for n in range(rows // A_CHUNK):
        lo = n * A_CHUNK
        qc, kc, vc = q[lo:lo + A_CHUNK], k[lo:lo + A_CHUNK], v[lo:lo + A_CHUNK]
        lfc, bc = lf[lo:lo + A_CHUNK], b[lo:lo + A_CHUNK]
        b_last = bc[A_CHUNK - 1:A_CHUNK]
        o_inter = _dot_nt(qc * jnp.exp(bc), st)
        kv_t = _dot_tn(vc, kc * jnp.exp(b_last - bc))
        st = st * jnp.exp(b_last) + kv_t
        g_rows, k_subs = [], []
        for i in range(n_sub):
            g_i = bc[i * A_SUB:i * A_SUB + 1] - lfc[i * A_SUB:i * A_SUB + 1]
            g_rows.append(jnp.broadcast_to(g_i, (A_SUB, HD)))
            expo = jnp.where(key_row < (i + 1) * A_SUB, g_i - bc, -jnp.inf)
            k_subs.append(kc * jnp.exp(expo))
        q_sub = qc * jnp.exp(bc - jnp.concatenate(g_rows, axis=0))
        scores = _dot_nt(q_sub, jnp.concatenate(k_subs, axis=0))
        scores = jnp.where(sel, scores, 0.0)
        o_intra = _dot_nn(scores, jnp.concatenate([vc] * n_sub, axis=0))
        outs.append(o_inter + o_intra)
    o = jnp.concatenate(outs, axis=0)
    return _rms(o, onorm) * _silu(ga), st


def _gmlp_rows(u, vb, gb, lnw, lnb, ws, bias):
    rows = u.shape[0]
    mu = jnp.mean(vb, axis=-1, keepdims=True)
    xc = vb - mu
    vg = xc * lax.rsqrt(jnp.mean(xc * xc, axis=-1, keepdims=True) + EPS) * lnw + lnb
    r = lax.broadcasted_iota(jnp.int32, (B_CHUNK, B_CHUNK), 0)
    c = lax.broadcasted_iota(jnp.int32, (B_CHUNK, B_CHUNK), 1)
    ws_causal = jnp.where(r >= c, ws, 0.0)
    svs = [_dot_nn(ws_causal, vg[n * B_CHUNK:(n + 1) * B_CHUNK]) + bias
           for n in range(rows // B_CHUNK)]
    return u * jnp.concatenate(svs, axis=0) * _silu(gb)


def _rope(x, cos_t, sin_t):
    return x * cos_t + pltpu.roll(x, 64, 1) * sin_t


def _rope_transpose(g, cos_t, sin_t):
    return g * cos_t + pltpu.roll(g * sin_t, 64, 1)


ANY_SPEC = pl.BlockSpec(memory_space=pl.ANY)


def _live(deps):
    return [d for d in deps if d is not None]


def _skip_deps(body, n_in, n_deps):
    def wrapped(*refs):
        return body(*refs[:n_in], *refs[n_in + n_deps:])
    return wrapped


def _pure_call(name, fn, grid, in_specs, out_specs, out_shape, args, n_acc=0, deps=()):
    deps = _live(deps)
    n_in, n_out, n_deps = len(in_specs), len(out_specs), len(deps)
    in_specs = list(in_specs) + [ANY_SPEC] * n_deps
    args = tuple(args) + tuple(deps)

    def body(*refs):
        res = fn(*[r[...] for r in refs[:n_in]])
        if not isinstance(res, (tuple, list)):
            res = (res,)
        outs = refs[n_in + n_deps:n_in + n_deps + n_out]
        for o, r in zip(outs[:n_out - n_acc], res[:n_out - n_acc]):
            o[...] = r.astype(o.dtype)
        if n_acc:
            first = functools.reduce(jnp.logical_and, [pl.program_id(i) == 0 for i in range(len(grid))])
            for o, r in zip(outs[n_out - n_acc:], res[n_out - n_acc:]):
                @pl.when(first)
                def _(o=o, r=r):
                    o[...] = r.astype(o.dtype)

                @pl.when(jnp.logical_not(first))
                def _(o=o, r=r):
                    o[...] += r.astype(o.dtype)

    return pl.pallas_call(body, name=name, grid=grid, in_specs=in_specs, out_specs=out_specs,
                          out_shape=out_shape, compiler_params=_params(len(grid)))(*args)


def _sds(shape, dtype):
    return jax.ShapeDtypeStruct(shape, dtype)


def _row_spec(tm, width, col=0):
    return pl.BlockSpec((tm, width), lambda i, col=col: (i, col))


def _full_spec(shape):
    nd = len(shape)
    return pl.BlockSpec(shape, lambda *_: (0,) * nd)


def _mm_nn(name, a, b, out_dtype, tm, tn, deps=()):
    deps = _live(deps)
    m, k = a.shape
    j, _, n = b.shape
    per = n // tn

    def body(a_ref, b_ref, o_ref):
        o_ref[...] = _raw_nn(a_ref[...], b_ref[...]).astype(o_ref.dtype)

    return pl.pallas_call(
        _skip_deps(body, 2, len(deps)), name=name, grid=(m // tm, j * per),
        in_specs=[pl.BlockSpec((tm, k), lambda i, c: (i, 0)),
                  pl.BlockSpec((None, k, tn), lambda i, c: (c // per, 0, c % per))] + [ANY_SPEC] * len(deps),
        out_specs=pl.BlockSpec((tm, tn), lambda i, c: (i, c)),
        out_shape=_sds((m, j * n), out_dtype), compiler_params=_params(2))(a, b, *deps)


def _mm_nt(name, a, b, out_dtype, tm, tn, deps=()):
    deps = _live(deps)
    m = a.shape[0]
    j, nn, n = b.shape

    def body(a_ref, b_ref, o_ref):
        b_all = b_ref[0] if j == 1 else jnp.concatenate([b_ref[s] for s in range(j)], axis=1)
        o_ref[...] = _raw_nt(a_ref[...], b_all).astype(o_ref.dtype)

    return pl.pallas_call(
        _skip_deps(body, 2, len(deps)), name=name, grid=(m // tm, nn // tn),
        in_specs=[pl.BlockSpec((tm, j * n), lambda i, c: (i, 0)),
                  pl.BlockSpec((j, tn, n), lambda i, c: (0, c, 0))] + [ANY_SPEC] * len(deps),
        out_specs=pl.BlockSpec((tm, tn), lambda i, c: (i, c)),
        out_shape=_sds((m, nn), out_dtype), compiler_params=_params(2))(a, b, *deps)


def _mm_tn(name, a, b, j, out_dtype, tm, tn, deps=()):
    deps = _live(deps)
    k, m = a.shape
    n = b.shape[1] // j
    per = n // tn

    def body(a_ref, b_ref, o_ref):
        o_ref[...] = _raw_tn(a_ref[...], b_ref[...]).astype(o_ref.dtype)

    return pl.pallas_call(
        _skip_deps(body, 2, len(deps)), name=name, grid=(m // tm, j * per),
        in_specs=[pl.BlockSpec((k, tm), lambda i, c: (0, i)),
                  pl.BlockSpec((k, tn), lambda i, c: (0, c))] + [ANY_SPEC] * len(deps),
        out_specs=pl.BlockSpec((None, tm, tn), lambda i, c: (c // per, i, c % per)),
        out_shape=_sds((j, m, n), out_dtype), compiler_params=_params(2))(a, b, *deps)


def _mm_tn_parity(name, a, b, j, parity, out_dtype, tm, deps=()):
    deps = _live(deps)
    k, m = a.shape
    n = b.shape[1] // j

    def body(par_ref, a_ref, b_ref, o_ref):
        del par_ref
        o_ref[...] = _raw_tn(a_ref[...], b_ref[...]).astype(o_ref.dtype)

    grid_spec = pltpu.PrefetchScalarGridSpec(
        num_scalar_prefetch=1, grid=(m // tm, j // 2),
        in_specs=[pl.BlockSpec((k, tm), lambda i, s, par: (0, i)),
                  pl.BlockSpec((k, n), lambda i, s, par: (0, 2 * s + par[0]))] + [ANY_SPEC] * len(deps),
        out_specs=pl.BlockSpec((None, tm, n), lambda i, s, par: (s, i, 0)))
    return pl.pallas_call(
        lambda par_ref, *refs: _skip_deps(functools.partial(body, par_ref), 2, len(deps))(*refs),
        name=name, grid_spec=grid_spec, out_shape=_sds((j // 2, m, n), out_dtype),
        compiler_params=_params(2))(parity, a, b, *deps)


TM = 256


def _pre_norm(name, x, w_row, deps=()):
    def fn(xv, w):
        return _rms(xv, w)
    return _pure_call(name, fn, (T // TM,), [_row_spec(TM, D), _full_spec((1, D))],
                      [_row_spec(TM, D)], [_sds((T, D), BF16)], (x, w_row), deps=deps)[0]


def _post_pre_norm(x, y, w_post, w_pre):
    def fn(xv, yv, wp, wn):
        x1 = xv + _rms(yv, wp)
        return x1, _rms(x1, wn)
    return _pure_call("post_pre_norm", fn, (T // TM,),
                      [_row_spec(TM, D), _row_spec(TM, D), _full_spec((1, D)), _full_spec((1, D))],
                      [_row_spec(TM, D), _row_spec(TM, D)],
                      [_sds((T, D), F32), _sds((T, D), BF16)], (x, y, w_post, w_pre))


def _post_pre_norm_bwd(y, x1, w_post, w_pre, dx1_in, dh1, deps=()):
    def fn(yv, x1v, wp, wn, dx1v, dh1v):
        _, vjp_pre = jax.vjp(_rms, x1v, wn)
        dx1_h, dwn = vjp_pre(dh1v)
        dx1 = dx1v + dx1_h
        _, vjp_post = jax.vjp(_rms, yv, wp)
        dy, dwp = vjp_post(dx1)
        return dx1, dy, dwp, dwn
    return _pure_call("post_pre_norm_bwd", fn, (T // TM,),
                      [_row_spec(TM, D), _row_spec(TM, D), _full_spec((1, D)), _full_spec((1, D)),
                       _row_spec(TM, D), _row_spec(TM, D)],
                      [_row_spec(TM, D), _row_spec(TM, D), _full_spec((1, D)), _full_spec((1, D))],
                      [_sds((T, D), F32), _sds((T, D), BF16), _sds((1, D), F32), _sds((1, D), F32)],
                      (y, x1, w_post, w_pre, dx1_in, dh1), n_acc=2, deps=deps)


def _final_loss(x1, y, w_post, target):
    def fn(x1v, yv, wp, tv):
        r, vjp = jax.vjp(_rms, yv, wp)
        err = x1v + r - tv
        part = 0.5 * jnp.sum(jnp.mean(err * err, axis=-1, keepdims=True), axis=0, keepdims=True)
        dx2 = err * (1.0 / D)
        dy, dwp = vjp(dx2)
        return dx2, dy, jnp.broadcast_to(part, (1, 128)), dwp
    return _pure_call("final_loss", fn, (T // TM,),
                      [_row_spec(TM, D), _row_spec(TM, D), _full_spec((1, D)), _row_spec(TM, D)],
                      [_row_spec(TM, D), _row_spec(TM, D), _full_spec((1, 128)), _full_spec((1, D))],
                      [_sds((T, D), F32), _sds((T, D), BF16), _sds((1, 128), F32), _sds((1, D), F32)],
                      (x1, y, w_post, target), n_acc=2)


def _pre_norm_bwd(x, w_row, dh, dx_res, deps=()):
    def fn(xv, w, dhv, dxv):
        _, vjp = jax.vjp(_rms, xv, w)
        dx, dw = vjp(dhv)
        return dxv + dx, dw
    return _pure_call("pre_norm_bwd", fn, (T // TM,),
                      [_row_spec(TM, D), _full_spec((1, D)), _row_spec(TM, D), _row_spec(TM, D)],
                      [_row_spec(TM, D), _full_spec((1, D))],
                      [_sds((T, D), F32), _sds((1, D), F32)], (x, w_row, dh, dx_res), n_acc=1, deps=deps)


RA = 256
RB = 512


HA = 4
A_GROUPS = A_HEADS // HA


def _head(ref, hh):
    return ref[:, hh * HD:(hh + 1) * HD]


def _hgrn2_fwd(z, l0, l1, onorm):
    nb = T // RA

    def body(q_ref, f_ref, v_ref, g_ref, l0_ref, l1_ref, on_ref, cat_ref, sst_ref, st_scr):
        @pl.when(pl.program_id(1) == 0)
        def _():
            st_scr[...] = jnp.zeros_like(st_scr)

        for hh in range(HA):
            st = st_scr[hh]
            sst_ref[hh] = st
            out, st_new = _hgrn2_rows(_head(q_ref, hh), _head(f_ref, hh), _head(v_ref, hh), _head(g_ref, hh), st,
                                      _head(l0_ref, hh), _head(l1_ref, hh), on_ref[...])
            cat_ref[:, hh * HD:(hh + 1) * HD] = out.astype(cat_ref.dtype)
            st_scr[hh] = st_new

    def cols(k):
        return pl.BlockSpec((RA, HA * HD), lambda g, r: (r, k * A_GROUPS + g))

    vec = pl.BlockSpec((1, HA * HD), lambda g, r: (0, g))
    return pl.pallas_call(
        body, name="hgrn2_fwd", grid=(A_GROUPS, nb),
        in_specs=[cols(0), cols(1), cols(2), cols(3), vec, vec, _full_spec((1, HD))],
        out_specs=[cols(0), pl.BlockSpec((HA, None, HD, HD), lambda g, r: (g, r, 0, 0))],
        out_shape=[_sds((T, 2 * A_HEADS * HD), BF16), _sds((A_HEADS, nb, HD, HD), F32)],
        scratch_shapes=[pltpu.VMEM((HA, HD, HD), F32)],
        compiler_params=_params(2))(z, z, z, z, l0, l1, onorm)


def _hgrn2_bwd(z, l0, l1, onorm, sst, dcat, deps=()):
    nb = T // RA
    deps = _live(deps)

    def body(q_ref, f_ref, v_ref, g_ref, l0_ref, l1_ref, on_ref, sst_ref, dcat_ref,
             dq_ref, df_ref, dv_ref, dg_ref, dl0_ref, dl1_ref, don_ref, ds_scr):
        g, r = pl.program_id(0), pl.program_id(1)

        @pl.when(r == 0)
        def _():
            ds_scr[...] = jnp.zeros_like(ds_scr)

        dl0s, dl1s, don = [], [], None
        for hh in range(HA):
            _, vjp = jax.vjp(_hgrn2_rows, _head(q_ref, hh), _head(f_ref, hh), _head(v_ref, hh), _head(g_ref, hh),
                             sst_ref[hh], _head(l0_ref, hh), _head(l1_ref, hh), on_ref[...])
            dq, dzf, dv, dga, dst, dl0, dl1, don_h = vjp((_head(dcat_ref, hh), ds_scr[hh]))
            for ref, val in ((dq_ref, dq), (df_ref, dzf), (dv_ref, dv), (dg_ref, dga)):
                ref[:, hh * HD:(hh + 1) * HD] = val.astype(ref.dtype)
            ds_scr[hh] = dst
            dl0s.append(dl0)
            dl1s.append(dl1)
            don = don_h if don is None else don + don_h
        dl0 = jnp.concatenate(dl0s, axis=1)
        dl1 = jnp.concatenate(dl1s, axis=1)

        @pl.when(r == 0)
        def _():
            dl0_ref[...] = dl0
            dl1_ref[...] = dl1

        @pl.when(r > 0)
        def _():
            dl0_ref[...] += dl0
            dl1_ref[...] += dl1

        first = jnp.logical_and(g == 0, r == 0)

        @pl.when(first)
        def _():
            don_ref[...] = don

        @pl.when(jnp.logical_not(first))
        def _():
            don_ref[...] += don

    def rev(k):
        return pl.BlockSpec((RA, HA * HD), lambda g, r: (nb - 1 - r, k * A_GROUPS + g))

    vec = pl.BlockSpec((1, HA * HD), lambda g, r: (0, g))
    grad = _sds((T, A_HEADS * HD), BF16)
    return pl.pallas_call(
        _skip_deps(body, 9, len(deps)), name="hgrn2_bwd", grid=(A_GROUPS, nb),
        in_specs=[rev(0), rev(1), rev(2), rev(3), vec, vec, _full_spec((1, HD)),
                  pl.BlockSpec((HA, None, HD, HD), lambda g, r: (g, nb - 1 - r, 0, 0)),
                  rev(0)] + [ANY_SPEC] * len(deps),
        out_specs=[rev(0)] * 4 + [vec, vec, _full_spec((1, HD))],
        out_shape=[grad] * 4 + [_sds((1, A_HEADS * HD), F32)] * 2 + [_sds((1, HD), F32)],
        scratch_shapes=[pltpu.VMEM((HA, HD, HD), F32)],
        compiler_params=_params(2))(z, z, z, z, l0, l1, onorm, sst, dcat, *deps)


GB = 4
B_STEPS = B_GROUPS // GB


def _gmlp_specs():
    vec = pl.BlockSpec((1, GB * HD), lambda s, r: (0, s))
    ws = pl.BlockSpec((GB, B_CHUNK, B_CHUNK), lambda s, r: (s, 0, 0))
    bias = pl.BlockSpec((GB, B_CHUNK, 1), lambda s, r: (s, 0, 0))

    def cols(k):
        return pl.BlockSpec((RB, GB * HD), lambda s, r: (r, k * B_STEPS + s))
    return vec, ws, bias, cols


def _gmlp_fwd(z, cat, lnw, lnb, ws, bias):
    vec, ws_spec, bias_spec, cols = _gmlp_specs()

    def body(u_ref, v_ref, g_ref, lnw_ref, lnb_ref, ws_ref, bias_ref, cat_in_ref, cat_ref):
        del cat_in_ref
        for gg in range(GB):
            out = _gmlp_rows(_head(u_ref, gg), _head(v_ref, gg), _head(g_ref, gg), _head(lnw_ref, gg),
                             _head(lnb_ref, gg), ws_ref[gg], bias_ref[gg])
            cat_ref[:, gg * HD:(gg + 1) * HD] = out.astype(cat_ref.dtype)

    return pl.pallas_call(
        body, name="gmlp_fwd", grid=(B_STEPS, T // RB),
        in_specs=[cols(4), cols(5), cols(6), vec, vec, ws_spec, bias_spec, pl.BlockSpec(memory_space=pl.ANY)],
        out_specs=cols(1),
        out_shape=_sds(cat.shape, cat.dtype), input_output_aliases={7: 0},
        compiler_params=_params(2))(z, z, z, lnw, lnb, ws, bias, cat)


def _gmlp_bwd(z, lnw, lnb, ws, bias, dcat):
    vec, ws_spec, bias_spec, cols = _gmlp_specs()

    def body(u_ref, v_ref, g_ref, lnw_ref, lnb_ref, ws_ref, bias_ref, dcat_ref,
             du_ref, dv_ref, dg_ref, dlnw_ref, dlnb_ref, dws_ref, dbias_ref):
        first = pl.program_id(1) == 0
        for gg in range(GB):
            _, vjp = jax.vjp(_gmlp_rows, _head(u_ref, gg), _head(v_ref, gg), _head(g_ref, gg), _head(lnw_ref, gg),
                             _head(lnb_ref, gg), ws_ref[gg], bias_ref[gg])
            du, dv, dg, dlnw, dlnb, dws, dbias = vjp(_head(dcat_ref, gg))
            lanes = slice(gg * HD, (gg + 1) * HD)
            for ref, val in ((du_ref, du), (dv_ref, dv), (dg_ref, dg)):
                ref[:, lanes] = val.astype(ref.dtype)
            sums = ((dlnw_ref, (slice(None), lanes), dlnw), (dlnb_ref, (slice(None), lanes), dlnb),
                    (dws_ref, gg, dws), (dbias_ref, gg, dbias))
            for ref, idx, val in sums:
                @pl.when(first)
                def _(ref=ref, idx=idx, val=val):
                    ref[idx] = val

                @pl.when(jnp.logical_not(first))
                def _(ref=ref, idx=idx, val=val):
                    ref[idx] += val

    grad = _sds((T, B_GROUPS * HD), BF16)
    return pl.pallas_call(
        body, name="gmlp_bwd", grid=(B_STEPS, T // RB),
        in_specs=[cols(4), cols(5), cols(6), vec, vec, ws_spec, bias_spec, cols(1)],
        out_specs=[cols(0)] * 3 + [vec, vec, ws_spec, bias_spec],
        out_shape=[grad] * 3 + [_sds((1, B_GROUPS * HD), F32)] * 2
        + [_sds((B_GROUPS, B_CHUNK, B_CHUNK), F32), _sds((B_GROUPS, B_CHUNK, 1), F32)],
        compiler_params=_params(2))(z, z, z, lnw, lnb, ws, bias, dcat)


def _mla_pre(z1, qn, kvn, cos_t, sin_t):
    def fn(cq, ckv, kpe, cs, sn, wq, wkv):
        return _rms(cq, wq), _rms(ckv, wkv), _rope(kpe, cs, sn)
    return _pure_call("mla_pre", fn, (T // TM,),
                      [_row_spec(TM, C_RANK, 4), _row_spec(TM, C_RANK, 5), _row_spec(TM, HD, 24),
                       _row_spec(TM, HD), _row_spec(TM, HD),
                       _full_spec((1, C_RANK)), _full_spec((1, C_RANK))],
                      [_row_spec(TM, C_RANK), _row_spec(TM, C_RANK), _row_spec(TM, HD)],
                      [_sds((T, C_RANK), BF16), _sds((T, C_RANK), BF16), _sds((T, HD), BF16)],
                      (z1, z1, z1, cos_t, sin_t, qn, kvn))


def _mla_pre_bwd(z1, qn, kvn, cos_t, sin_t, dcqn, dckvn, dkp, deps=()):
    def fn(cq, ckv, cs, sn, wq, wkv, g_q, g_kv, g_kp):
        _, vjp_q = jax.vjp(_rms, cq, wq)
        dcq, dwq = vjp_q(g_q)
        _, vjp_kv = jax.vjp(_rms, ckv, wkv)
        dckv, dwkv = vjp_kv(g_kv)
        return dcq, dckv, _rope_transpose(g_kp, cs, sn), dwq, dwkv
    return _pure_call("mla_pre_bwd", fn, (T // TM,),
                      [_row_spec(TM, C_RANK, 4), _row_spec(TM, C_RANK, 5),
                       _row_spec(TM, HD), _row_spec(TM, HD),
                       _full_spec((1, C_RANK)), _full_spec((1, C_RANK)),
                       _row_spec(TM, C_RANK), _row_spec(TM, C_RANK), _row_spec(TM, HD)],
                      [_row_spec(TM, C_RANK), _row_spec(TM, C_RANK), _row_spec(TM, HD),
                       _full_spec((1, C_RANK)), _full_spec((1, C_RANK))],
                      [_sds((T, C_RANK), BF16), _sds((T, C_RANK), BF16), _sds((T, HD), BF16),
                       _sds((1, C_RANK), F32), _sds((1, C_RANK), F32)],
                      (z1, z1, cos_t, sin_t, qn, kvn, dcqn, dckvn, dkp), n_acc=2, deps=deps)


TQ = 256
HP = 2
KVW = C_NOPE + C_V


def _att_keys(kv_ref, kp_ref, k_scr):
    @pl.when(pl.program_id(1) == 0)
    def _():
        for hh in range(HP):
            k_scr[hh, :, 0:C_NOPE] = kv_ref[:, hh * KVW:hh * KVW + C_NOPE]
            k_scr[hh, :, C_NOPE:QP] = kp_ref[...]


def _att_scores(q, cos_ref, sin_ref, k_scr, hh, n):
    keys = (n + 1) * TQ
    qr = jnp.concatenate([q[:, :C_NOPE], _rope(q[:, C_NOPE:], cos_ref[...], sin_ref[...])], axis=1).astype(BF16)
    return qr, _raw_nt(qr, k_scr[hh, 0:keys, :]) * ATT_SCALE


def _causal(x, n, fill):
    row = lax.broadcasted_iota(jnp.int32, (TQ, TQ), 0)
    col = lax.broadcasted_iota(jnp.int32, (TQ, TQ), 1)
    diag = jnp.where(col <= row, x[:, n * TQ:], fill)
    return diag if n == 0 else jnp.concatenate([x[:, :n * TQ], diag], axis=1)


def _per_query_block(fn):
    for n in range(T // TQ):
        pl.when(pl.program_id(1) == n)(functools.partial(fn, n))


def _att_in_specs():
    return [pl.BlockSpec((TQ, HP * QP), lambda g, i: (i, g)),
            pl.BlockSpec((TQ, HD), lambda g, i: (i, 0)),
            pl.BlockSpec((TQ, HD), lambda g, i: (i, 0)),
            pl.BlockSpec((T, HP * KVW), lambda g, i: (0, g)),
            pl.BlockSpec((T, HD), lambda g, i: (0, 0))]


def _attention_fwd(q, cos_t, sin_t, kv, kp, z1):
    def body(q_ref, cos_ref, sin_ref, kv_ref, kp_ref, gate_ref, o_ref, lse_ref, og_ref, k_scr):
        _att_keys(kv_ref, kp_ref, k_scr)

        def block(n):
            keys = (n + 1) * TQ
            for hh in range(HP):
                _, s = _att_scores(q_ref[:, hh * QP:(hh + 1) * QP], cos_ref, sin_ref, k_scr, hh, n)
                s = _causal(s, n, jnp.finfo(F32).min)
                m = jnp.max(s, axis=-1, keepdims=True)
                p = jnp.exp(s - m)
                l = jnp.sum(p, axis=-1, keepdims=True)
                v = kv_ref[0:keys, hh * KVW + C_NOPE:(hh + 1) * KVW]
                o = _raw_nn(p, v) / l
                lanes = slice(hh * C_V, (hh + 1) * C_V)
                o_ref[:, lanes] = o
                og_ref[:, lanes] = (o * _silu(gate_ref[:, lanes])).astype(og_ref.dtype)
                lse_ref[hh] = m + jnp.log(l)

        _per_query_block(block)

    heads = pl.BlockSpec((TQ, HP * C_V), lambda g, i: (i, g))
    return pl.pallas_call(
        body, name="attention_fwd", grid=(C_HEADS // HP, T // TQ), in_specs=_att_in_specs() + [heads],
        out_specs=[heads, pl.BlockSpec((HP, TQ, 1), lambda g, i: (g, i, 0)), heads],
        out_shape=[_sds((T, C_HEADS * C_V), F32), _sds((C_HEADS, T, 1), F32), _sds((T, C_HEADS * C_V), BF16)],
        scratch_shapes=[pltpu.VMEM((HP, T, QP), BF16)],
        compiler_params=_params(2))(q, cos_t, sin_t, kv, kp, z1)


def _attention_bwd(q, cos_t, sin_t, kv, kp, o, lse, dog, z1):
    nq = T // TQ

    def body(q_ref, cos_ref, sin_ref, kv_ref, kp_ref, o_ref, lse_ref, dog_ref, gate_ref,
             dq_ref, dkv_ref, dkp_ref, dgate_ref, k_scr, dk_scr, dv_scr):
        g, i = pl.program_id(0), pl.program_id(1)
        _att_keys(kv_ref, kp_ref, k_scr)

        @pl.when(i == 0)
        def _():
            dv_scr[...] = jnp.zeros_like(dv_scr)
            dk_scr[...] = jnp.zeros_like(dk_scr)

        def block(n):
            keys = (n + 1) * TQ
            for hh in range(HP):
                qr, s = _att_scores(q_ref[:, hh * QP:(hh + 1) * QP], cos_ref, sin_ref, k_scr, hh, n)
                p = _causal(jnp.exp(s - lse_ref[hh]), n, 0.0)
                lanes = slice(hh * C_V, (hh + 1) * C_V)
                ov, gate, dogv = o_ref[:, lanes], gate_ref[:, lanes], dog_ref[:, lanes]
                sig = _sigmoid(gate)
                silu = gate * sig
                dov = dogv * silu
                dgate_ref[:, lanes] = (dogv * ov * (sig + silu * (1.0 - sig))).astype(dgate_ref.dtype)
                delta = jnp.sum(dov * ov, axis=-1, keepdims=True)
                dp = _raw_nt(dov, kv_ref[0:keys, hh * KVW + C_NOPE:(hh + 1) * KVW])
                ds = p * (dp - delta) * ATT_SCALE
                dq = _raw_nn(ds, k_scr[hh, 0:keys, :])
                dq_ref[:, hh * QP:(hh + 1) * QP] = jnp.concatenate(
                    [dq[:, :C_NOPE], _rope_transpose(dq[:, C_NOPE:], cos_ref[...], sin_ref[...])],
                    axis=1).astype(dq_ref.dtype)
                dv_scr[hh, 0:keys, :] += _raw_tn(p, dov)
                dk_scr[hh, 0:keys, :] += _raw_tn(ds, qr)

        _per_query_block(block)

        @pl.when(i == nq - 1)
        def _():
            for hh in range(HP):
                dkv_ref[:, hh * KVW:(hh + 1) * KVW] = jnp.concatenate(
                    [dk_scr[hh, :, 0:C_NOPE], dv_scr[hh]], axis=1).astype(dkv_ref.dtype)

        @pl.when(jnp.logical_and(i == nq - 1, g == 0))
        def _():
            dkp_ref[...] = dk_scr[0, :, C_NOPE:QP]

        @pl.when(jnp.logical_and(i == nq - 1, g > 0))
        def _():
            dkp_ref[...] += dk_scr[0, :, C_NOPE:QP]

        @pl.when(i == nq - 1)
        def _():
            for hh in range(1, HP):
                dkp_ref[...] += dk_scr[hh, :, C_NOPE:QP]

    heads = pl.BlockSpec((TQ, HP * C_V), lambda g, i: (i, g))
    return pl.pallas_call(
        body, name="attention_bwd", grid=(C_HEADS // HP, nq),
        in_specs=_att_in_specs() + [heads, pl.BlockSpec((HP, TQ, 1), lambda g, i: (g, i, 0)), heads, heads],
        out_specs=[pl.BlockSpec((TQ, HP * QP), lambda g, i: (i, g)),
                   pl.BlockSpec((T, HP * KVW), lambda g, i: (0, g)),
                   _full_spec((T, HD)), heads],
        out_shape=[_sds((T, C_HEADS * QP), BF16), _sds((T, C_HEADS * KVW), BF16), _sds((T, HD), F32),
                   _sds((T, C_HEADS * C_V), BF16)],
        scratch_shapes=[pltpu.VMEM((HP, T, QP), BF16), pltpu.VMEM((HP, T, QP), F32), pltpu.VMEM((HP, T, C_V), F32)],
        compiler_params=_params(2))(q, cos_t, sin_t, kv, kp, o, lse, dog, z1)


def _adamw_math(w, g, m, v):
    m = ADAM_B1 * m + (1.0 - ADAM_B1) * g
    v = ADAM_B2 * v + (1.0 - ADAM_B2) * (g * g)
    m_hat = m / (1.0 - ADAM_B1 ** ADAM_STEP)
    v_hat = v / (1.0 - ADAM_B2 ** ADAM_STEP)
    delta = -ADAM_LR * (m_hat / (jnp.sqrt(v_hat) + ADAM_EPS) + ADAM_WD * w)
    return delta, m, v


def _adamw(name, parts, w, m, v, tr, tc=None):
    rows, cols = w.shape

    def fn(*vals):
        pvs, (wv, mv, vv) = vals[:len(parts)], vals[len(parts):]
        g = None
        for pv in pvs:
            for d in range(pv.shape[0]):
                term = pv[d].astype(F32)
                g = term if g is None else g + term
        return (g,) + _adamw_math(wv, g, mv, vv)

    tc = cols if tc is None else tc
    blk = pl.BlockSpec((tr, tc), lambda i, j: (i, j))
    part_specs = [pl.BlockSpec((n, tr, tc), lambda i, j: (0, i, j)) for _, n in parts]
    return _pure_call(name, fn, (rows // tr, cols // tc), part_specs + [blk, blk, blk],
                      [blk] * 4, [_sds((rows, cols), F32)] * 4, tuple(p for p, _ in parts) + (w, m, v))


SMALL_PARAM_SHAPES = ((2, D), (2, D), (2, A_HEADS * HD), (1, HD), (1, B_GROUPS * HD), (1, B_GROUPS * HD),
                      (B_GROUPS, B_CHUNK, B_CHUNK), (B_GROUPS, B_CHUNK))
SMALL_PIECES = ((0, 0, 0, 0), (0, 1, 1, 0), (1, 0, 1, 1), (1, 1, 1, 2), (2, 0, 2, 0), (2, 1, 2, 1),
                (3, 0, 3, 8), (4, 0, 2, 2), (5, 0, 2, 3))


def _small_rows(dnpre1, dnpost0, dnpost1, dl0, dl1, donorm, dlnw, dlnb, dws, dbias, dqn, dkvn, loss_part):
    return [jnp.concatenate([dnpre1, dnpost0, dnpost1], axis=0),
            jnp.concatenate([dl0, dl1, dlnw, dlnb], axis=0),
            jnp.concatenate([dbias.reshape(B_GROUPS, B_CHUNK), donorm, loss_part], axis=0),
            dws,
            jnp.concatenate([dqn, dkvn], axis=0)]


def _adamw_small(late_all, early_all, wmv):
    n_in = 6 + 3 * len(wmv)

    def body(*refs):
        gathered, params, outs = refs[:6], refs[6:n_in], refs[n_in:]

        def total(ref):
            s = ref[0]
            for d in range(1, N_DEV):
                s = s + ref[d]
            return s

        g_late, g2048, g1024, g128, g_ws, g512 = [total(r) for r in gathered]
        arrays = (g_late, g2048, g1024, g128)

        def update(p, rows, g):
            w_ref, m_ref, v_ref = params[3 * p:3 * p + 3]
            delta, m, v = _adamw_math(w_ref[rows], g, m_ref[rows], v_ref[rows])
            for out, val in zip(outs[4 * p:4 * p + 4], (g, delta, m, v)):
                out[rows] = val

        for p, row, arr, arr_row in SMALL_PIECES:
            update(p, pl.ds(row, 1), arrays[arr][arr_row:arr_row + 1])
        update(6, slice(None), g_ws)
        update(7, slice(None), g128[0:B_GROUPS])
        outs[32][...] = g128[B_GROUPS + 1:B_GROUPS + 2]
        outs[33][...] = g512

    vmem = pl.BlockSpec(memory_space=pltpu.VMEM)
    flat = [a for t in wmv for a in t]
    out_shape = [_sds(s, F32) for s in SMALL_PARAM_SHAPES for _ in range(4)] + [_sds((1, 128), F32), _sds((2, C_RANK), F32)]
    res = pl.pallas_call(body, name="adamw_small", in_specs=[vmem] * n_in, out_specs=[vmem] * len(out_shape),
                         out_shape=out_shape,
                         compiler_params=pltpu.CompilerParams(vmem_limit_bytes=VMEM_LIMIT_V7X))(late_all, *early_all, *flat)
    return [res[4 * p:4 * p + 4] for p in range(8)], res[32], res[33]


def _exchange(name, arrs, gather, deps=()):
    n = len(arrs)
    deps = _live(deps)

    def body(*refs):
        ins, outs = refs[:n], refs[n + len(deps):2 * n + len(deps)]
        send_sems, recv_sems, local_sems = refs[2 * n + len(deps):]
        x, y, c = lax.axis_index("x"), lax.axis_index("y"), lax.axis_index("c")
        me = 4 * x + 2 * y + c

        def peer(k):
            return (x ^ (k >> 2), y ^ ((k >> 1) & 1), c ^ (k & 1))

        def copy(a, k):
            src = ins[a] if gather else ins[a].at[me ^ k]
            return pltpu.make_async_remote_copy(
                src_ref=src, dst_ref=outs[a].at[me], send_sem=send_sems.at[a, k - 1],
                recv_sem=recv_sems.at[a, k - 1], device_id=peer(k), device_id_type=MESH_ID)

        def arrival(a, k):
            src = ins[a] if gather else ins[a].at[me]
            return pltpu.make_async_remote_copy(
                src_ref=src, dst_ref=outs[a].at[me ^ k], send_sem=send_sems.at[a, k - 1],
                recv_sem=recv_sems.at[a, k - 1], device_id=peer(k), device_id_type=MESH_ID)

        own = [pltpu.make_async_copy(ins[a] if gather else ins[a].at[me], outs[a].at[me], local_sems.at[a])
               for a in range(n)]
        for cp in own:
            cp.start()
        for k in range(1, N_DEV):
            for a in range(n):
                copy(a, k).start()
        for k in range(1, N_DEV):
            for a in range(n):
                arrival(a, k).wait_recv()
        for k in range(1, N_DEV):
            for a in range(n):
                copy(a, k).wait_send()
        for cp in own:
            cp.wait()

    any_spec = pl.BlockSpec(memory_space=pl.ANY)
    out_shape = [_sds((N_DEV,) + a.shape if gather else a.shape, a.dtype) for a in arrs]
    return pl.pallas_call(
        body, name=name, in_specs=[any_spec] * (n + len(deps)), out_specs=[any_spec] * n, out_shape=out_shape,
        scratch_shapes=[pltpu.SemaphoreType.DMA((n, N_DEV - 1)), pltpu.SemaphoreType.DMA((n, N_DEV - 1)),
                        pltpu.SemaphoreType.DMA((n,))],
        compiler_params=pltpu.CompilerParams(has_side_effects=True))(*arrs, *deps)


HBM_SPEC = pl.BlockSpec(memory_space=pltpu.HBM)
SEM_SPEC = pl.BlockSpec(memory_space=pltpu.SEMAPHORE)
DATAFLOW = pltpu.SideEffectType.DATAFLOW_SIDE_EFFECTING


def _my_index():
    return 4 * lax.axis_index("x") + 2 * lax.axis_index("y") + lax.axis_index("c")


def _plan_copies(plan, refs, send_sems, recv_sems):
    x, y, c = lax.axis_index("x"), lax.axis_index("y"), lax.axis_index("c")
    return [pltpu.make_async_remote_copy(
        src_ref=src, dst_ref=dst, send_sem=send_sems.at[i], recv_sem=recv_sems.at[i],
        device_id=(x ^ (k >> 2), y ^ ((k >> 1) & 1), c ^ (k & 1)), device_id_type=MESH_ID)
        for i, (src, dst, k) in enumerate(plan(refs, 4 * x + 2 * y + c))]


def _split_call(name, bufs, waits=None, starts=None, deps=()):
    n = len(bufs)
    deps = _live(deps)
    n_wait = 2 if waits else 0

    def body(*refs):
        zones = refs[:n]
        if waits:
            for cp in _plan_copies(waits[2], zones, refs[n], refs[n + 1]):
                cp.wait_send()
                cp.wait_recv()
        if starts:
            first_out = n + n_wait + len(deps)
            for cp in _plan_copies(starts[0], zones, refs[first_out], refs[first_out + 1]):
                cp.start()
            refs[-1][...] = jnp.zeros_like(refs[-1])

    out_specs, out_shape = [], []
    if starts:
        sems = pltpu.SemaphoreType.DMA((starts[1],))
        out_specs, out_shape = [SEM_SPEC, SEM_SPEC], [sems, sems]
    out_specs += [HBM_SPEC] * n
    out_shape += [pltpu.HBM(b.shape, b.dtype) for b in bufs]
    if starts:
        out_specs.append(pl.BlockSpec(memory_space=pltpu.VMEM))
        out_shape.append(_sds((8, 128), F32))
    first_buf = 2 if starts else 0
    res = pl.pallas_call(
        body, name=name,
        in_specs=[HBM_SPEC] * n + [SEM_SPEC] * n_wait + [ANY_SPEC] * len(deps),
        out_specs=out_specs, out_shape=out_shape,
        input_output_aliases={i: first_buf + i for i in range(n)},
        compiler_params=pltpu.CompilerParams(has_side_effects=DATAFLOW),
    )(*[pltpu.with_memory_space_constraint(b, pltpu.HBM) for b in bufs], *(waits[:2] if waits else ()), *deps)
    out_bufs = list(res[first_buf:first_buf + n])
    return out_bufs, ((res[0], res[1]) if starts else None), (res[-1] if starts else None)


def _direct_plan(n, gather):
    def plan(refs, me):
        return [(refs[a] if gather else refs[a].at[me ^ k], refs[n + a].at[me], k)
                for k in range(1, N_DEV) for a in range(n)]
    return plan


def _own_slot_filled(a, gather):
    me = _my_index()
    if gather:
        return lax.dynamic_update_slice_in_dim(lax.empty((N_DEV,) + a.shape, a.dtype), a[None], me, 0)
    return lax.dynamic_update_slice_in_dim(lax.empty(a.shape, a.dtype), lax.dynamic_slice_in_dim(a, me, 1, 0), me, 0)


def _exchange_start(name, arrs, gather, deps=()):
    n = len(arrs)
    lands = [_own_slot_filled(a, gather) for a in arrs]
    plan = _direct_plan(n, gather)
    bufs, sems, token = _split_call(name, list(arrs) + lands, starts=(plan, n * (N_DEV - 1)), deps=deps)
    return (n, plan, sems, bufs, None), token


def _exchange_wait(name, handle, after):
    return _split_done(name, handle, after)


ICI_PEERS = (2, 4, 6)
SIBLING = 1


def _gather2_send(name, arrs, deps=()):
    n = len(arrs)
    lands = [_own_slot_filled(a, True) for a in arrs]

    def plan(refs, me_):
        return [(refs[a], refs[n + a].at[me_], k) for k in (SIBLING,) + ICI_PEERS for a in range(n)]

    bufs, sems, token = _split_call(name, list(arrs) + lands, starts=(plan, 4 * n), deps=deps)
    return (n, plan, sems, bufs, None), token


def _gather2_relay(name, handle, after):
    n, plan, sems, bufs, _ = handle
    after = after if isinstance(after, (list, tuple)) else [after]

    def relay(refs, me_):
        return [(refs[n + a].at[me_ ^ k], refs[n + a].at[me_ ^ k], SIBLING) for k in ICI_PEERS for a in range(n)]

    bufs, sems2, token = _split_call(name, bufs, waits=(sems[0], sems[1], plan), starts=(relay, 3 * n), deps=after)
    return (n, relay, sems2, bufs, None), token


def _split_done(name, handle, after, all_bufs=False):
    n, plan, sems, bufs, _ = handle
    bufs, _, _ = _split_call(name, bufs, waits=(sems[0], sems[1], plan), deps=[after])
    return bufs if all_bufs else bufs[n:]


def _scatter2_pair(name, for_sibling, deps=()):
    n = len(for_sibling)
    pairs = [lax.empty(s.shape, s.dtype) for s in for_sibling]

    def plan(refs, me):
        del me
        return [(refs[a].at[s], refs[n + a].at[s], SIBLING) for s in range(4) for a in range(n)]

    bufs, sems, token = _split_call(name, list(for_sibling) + pairs, starts=(plan, 4 * n), deps=deps)
    return (n, plan, sems, bufs, None), token


def _pair_add(name, mine, pair):
    _, rows, cols = mine.shape
    tr = rows // 2

    def fn(a, b):
        return a.astype(F32) + b.astype(F32)

    blk = pl.BlockSpec((None, tr, cols), lambda s, i: (s, i, 0))
    return _pure_call(name, fn, (4, rows // tr), [blk, blk], [blk], [_sds(mine.shape, mine.dtype)], (mine, pair))[0]


def _scatter2_send(name, chip_sums, deps=()):
    n = len(chip_sums)
    finals = [lax.empty((3,) + c.shape[1:], c.dtype) for c in chip_sums]

    def plan(refs, me):
        return [(refs[a].at[(me >> 1) ^ j], refs[n + a].at[j - 1], 2 * j) for j in range(1, 4) for a in range(n)]

    bufs, sems, token = _split_call(name, list(chip_sums) + finals, starts=(plan, 3 * n), deps=deps)
    return (n, plan, sems, bufs, None), token


def _pad_rope(p):
    z = jnp.zeros(p.shape[:-1] + (32,), p.dtype)
    return jnp.concatenate([p[..., :32], z, p[..., 32:], z], axis=-1)


def _unpad_rope(p):
    return jnp.concatenate([p[..., :32], p[..., 64:96]], axis=-1)


def _odd_in_layout(wt):
    wt = wt.reshape(ODD_IN, D)
    cq, ckv, kpe, gate = wt[:512], wt[512:1024], wt[1024:1088], wt[1088:]
    z = jnp.zeros((32, D), wt.dtype)
    return jnp.concatenate([gate, cq, ckv, kpe[:32], z, kpe[32:], z], axis=0)


def _odd_in_unlayout(dwt):
    gate, cq, ckv, kpe = dwt[:2048], dwt[2048:2560], dwt[2560:3072], dwt[3072:]
    wt = jnp.concatenate([cq, ckv, kpe[:32], kpe[64:96], gate], axis=0)
    return wt.reshape(N_DEV, ODD_IN // N_DEV, D)


def _qb_layout(w):
    w = w.transpose(1, 0, 2).reshape(C_RANK, C_HEADS, C_QK)
    w = jnp.concatenate([w[..., :C_NOPE], _pad_rope(w[..., C_NOPE:])], axis=-1)
    return w.reshape(C_RANK, C_HEADS * QP)


def _qb_unlayout(dw):
    dw = dw.reshape(C_RANK, C_HEADS, QP)
    dw = jnp.concatenate([dw[..., :C_NOPE], _unpad_rope(dw[..., C_NOPE:])], axis=-1)
    return dw.reshape(C_RANK, N_DEV, C_HEADS * C_QK // N_DEV).transpose(1, 0, 2)


def _rope_tables(positions):
    inv_freq = ROPE_THETA ** (-jnp.arange(0, C_ROPE, 2, dtype=F32) / C_ROPE)
    ang = positions.astype(F32)[0][:, None] * inv_freq
    cos, sin = jnp.cos(ang), jnp.sin(ang)
    z = jnp.zeros_like(cos)
    return jnp.concatenate([cos, z, cos, z], axis=1), jnp.concatenate([-sin, z, sin, z], axis=1)


def _forward_backward(x, cos_t, sin_t, target, norm_pre, norm_post, lb_logits, a_onorm, ln_w, ln_b,
                      b_ws, b_bias, get_w, put_g, put_small=None, start_dep=None):
    npre0, npre1 = norm_pre[0:1], norm_pre[1:2]
    npost0, npost1 = norm_post[0:1], norm_post[1:2]
    l0, l1 = lb_logits[0:1], lb_logits[1:2]
    bias_col = b_bias.reshape(B_GROUPS, B_CHUNK, 1)
    ws = b_ws.reshape(B_GROUPS, B_CHUNK, B_CHUNK)

    h0 = _pre_norm("pre_norm0", x, npre0, deps=[start_dep])
    w_ev_in = get_w("ev_in", h0)
    z0 = _mm_nn("ev_in", h0, w_ev_in, F32, 1024, 896)
    cat, sst = _hgrn2_fwd(z0, l0, l1, a_onorm)
    cat = _gmlp_fwd(z0, cat, ln_w, ln_b, ws, bias_col)
    w_ev_out = get_w("ev_out", cat)
    y0 = _mm_nn("ev_out", cat, w_ev_out, F32, 1024, 1024)
    x1, h1 = _post_pre_norm(x, y0, npost0, npre1)
    w_od_in, w_qb, w_kvb, q_norm, kv_norm = get_w("od_mid", h1)
    z1 = _mm_nt("od_in", h1, w_od_in[None], F32, 1024, 640)
    cqn, ckvn, kp = _mla_pre(z1, q_norm, kv_norm, cos_t, sin_t)
    q = _mm_nn("od_qb", cqn, w_qb[None], F32, 1024, 1024)
    kv = _mm_nn("od_kvb", ckvn, w_kvb, BF16, 1024, 512)
    o, lse, og = _attention_fwd(q, cos_t, sin_t, kv, kp, z1)
    w_od_out = get_w("od_out", og)
    y1 = _mm_nn("od_out", og, w_od_out, F32, 1024, 1024)
    dx2, dy1, loss_part, dnpost1 = _final_loss(x1, y1, npost1, target)

    g_od_out = _mm_tn("od_out_dw", og, dy1, 1, BF16, 1024, 1024)
    tok = put_g("od_out", [g_od_out.reshape(N_DEV, D // N_DEV, D)])
    dog = _mm_nt("od_out_dx", dy1, w_od_out, F32, 1024, 1024, deps=[tok])
    dq, dkv, dkp, dgate = _attention_bwd(q, cos_t, sin_t, kv, kp, o, lse, dog, z1)
    g_qb = _mm_tn("od_qb_dw", cqn, dq, 1, F32, 512, 1024)
    g_kvb = _mm_tn("od_kvb_dw", ckvn, dkv, N_DEV, BF16, 512, 512)
    tok = put_g("od_qkv", [_qb_unlayout(g_qb[0]).astype(BF16), g_kvb])
    dcqn = _mm_nt("od_qb_dx", dq, w_qb[None], F32, 1024, 512, deps=[tok])
    dckvn = _mm_nt("od_kvb_dx", dkv, w_kvb, F32, 1024, 512)
    dcq, dckv, dkpe, dqn, dkvn = _mla_pre_bwd(z1, q_norm, kv_norm, cos_t, sin_t, dcqn, dckvn, dkp)
    dz1 = jnp.concatenate([dgate, dcq, dckv, dkpe], axis=1)
    g_od_in = _mm_tn("od_in_dw", dz1, h1, 1, F32, 640, 1024)
    tok = put_g("od_in", [_odd_in_unlayout(g_od_in[0]).astype(BF16)])
    dh1 = _mm_nn("od_in_dx", dz1, w_od_in[None], F32, 1024, 1024, deps=[tok])
    dx1, dy0, dnpost0, dnpre1 = _post_pre_norm_bwd(y0, x1, npost0, npre1, dx2, dh1)

    g_ev_out = _mm_tn("ev_out_dw", cat, dy0, 1, BF16, 1024, 1024)
    tok = put_g("ev_out", [g_ev_out.reshape(N_DEV, D // N_DEV, D)])
    dcat = _mm_nt("ev_out_dx", dy0, w_ev_out, F32, 1024, 1024, deps=[tok])
    dqa, dfa, dia, dga, dl0, dl1, donorm = _hgrn2_bwd(z0, l0, l1, a_onorm, sst, dcat)
    dub, dvb, dgb, dlnw, dlnb, dws, dbias = _gmlp_bwd(z0, ln_w, ln_b, ws, bias_col, dcat)
    dz0 = jnp.concatenate([dqa, dfa, dia, dga, dub, dvb, dgb], axis=1)
    early = _small_rows(dnpre1, dnpost0, dnpost1, dl0, dl1, donorm, dlnw, dlnb, dws, dbias, dqn, dkvn, loss_part)
    tok = put_small(early) if put_small else None
    small_tok = tok

    def ev_in_half(name, parity, deps=()):
        return _mm_tn_parity(name, h0, dz0, N_DEV, parity, BF16, 1024, deps=[small_tok] + list(deps))

    tok = put_g("ev_in", ev_in_half)
    dh0 = _mm_nt("ev_in_dx", dz0, w_ev_in, F32, 1024, 256, deps=[tok])
    grad_x, dnpre0 = _pre_norm_bwd(x, npre0, dh0, dx1)
    return grad_x, early, dnpre0


def kernel(x, positions, norm_pre, norm_post, ev_w_in, ev_lb_logits, ev_a_onorm, ev_b_ln_w, ev_b_ln_b, ev_b_ws, ev_b_bias, ev_w_out, od_w_in, od_q_norm, od_w_qb, od_kv_norm, od_w_kvb, od_w_out, loss_target, m_norm_pre, m_norm_post, m_ev_w_in, m_ev_lb_logits, m_ev_a_onorm, m_ev_b_ln_w, m_ev_b_ln_b, m_ev_b_ws, m_ev_b_bias, m_ev_w_out, m_od_w_in, m_od_q_norm, m_od_w_qb, m_od_kv_norm, m_od_w_kvb, m_od_w_out, v_norm_pre, v_norm_post, v_ev_w_in, v_ev_lb_logits, v_ev_a_onorm, v_ev_b_ln_w, v_ev_b_ln_b, v_ev_b_ws, v_ev_b_bias, v_ev_w_out, v_od_w_in, v_od_q_norm, v_od_w_qb, v_od_kv_norm, v_od_w_kvb, v_od_w_out):
    me = 4 * lax.axis_index("x") + 2 * lax.axis_index("y") + lax.axis_index("c")
    bf = lambda w: w[0].astype(BF16)

    norms = jnp.pad(jnp.concatenate([od_q_norm, od_kv_norm], axis=1), ((0, 7), (0, 0)))
    first_h, tok = _gather2_send("gather_ev_in", [bf(ev_w_in)])
    rest_h, tok = _gather2_send("gather_rest", [bf(ev_w_out), od_w_in[0].T.astype(BF16), bf(od_w_qb), bf(od_w_kvb),
                                                norms, bf(od_w_out)], deps=[tok])
    cos_t, sin_t = _rope_tables(positions)
    rest = []

    def get_w(group, after):
        if group == "ev_in":
            relayed, token = _gather2_relay("relay_ev_in", first_h, [after, cos_t, sin_t])
            return _split_done("arrived_ev_in", relayed, token)[0]
        if not rest:
            relayed, token = _gather2_relay("relay_rest", rest_h, after)
            rest.extend(_split_done("arrived_rest", relayed, token))
        w_ev_out, w_od_in, w_qb, w_kvb, norms_all, w_od_out = rest
        if group == "ev_out":
            return w_ev_out.reshape(1, D, D)
        if group == "od_out":
            return w_od_out.reshape(1, D, D)
        return (_odd_in_layout(w_od_in), _qb_layout(w_qb), w_kvb,
                norms_all[:, 0, :64].reshape(1, C_RANK), norms_all[:, 0, 64:].reshape(1, C_RANK))

    scatters = {}

    def put_g(group, grads):
        if group == "ev_in":
            core = lax.axis_index("c").astype(jnp.int32).reshape(1)
            paired, token = _scatter2_pair("pair_ev_in", [grads("ev_in_dw_sibling", 1 - core)])
            mine = grads("ev_in_dw_own", core, deps=[token])
            pair = _split_done("paired_ev_in", paired, mine)[0]
            scatters[group], token = _scatter2_send("scatter_ev_in", [_pair_add("pair_add_ev_in", mine, pair)])
        else:
            scatters[group], token = _exchange_start("scatter_" + group, grads, False)
        return token

    def put_small(early):
        scatters["small"], token = _exchange_start("gather_small_early", early, True)
        return token

    grad_x, _, dnpre0 = _forward_backward(
        x[0], cos_t, sin_t, loss_target[0], norm_pre, norm_post, ev_lb_logits, ev_a_onorm, ev_b_ln_w,
        ev_b_ln_b, ev_b_ws, ev_b_bias, get_w, put_g, put_small, start_dep=tok)

    big_w = {"ev_w_in": ev_w_in, "ev_w_out": ev_w_out, "od_w_in": od_w_in, "od_w_qb": od_w_qb,
             "od_w_kvb": od_w_kvb, "od_w_out": od_w_out}
    big_m = {"ev_w_in": m_ev_w_in, "ev_w_out": m_ev_w_out, "od_w_in": m_od_w_in, "od_w_qb": m_od_w_qb,
             "od_w_kvb": m_od_w_kvb, "od_w_out": m_od_w_out}
    big_v = {"ev_w_in": v_ev_w_in, "ev_w_out": v_ev_w_out, "od_w_in": v_od_w_in, "od_w_qb": v_od_w_qb,
             "od_w_kvb": v_od_w_kvb, "od_w_out": v_od_w_out}
    big_out = {}
    after = grad_x
    for group, names in (("od_out", ["od_w_out"]), ("od_qkv", ["od_w_qb", "od_w_kvb"]), ("od_in", ["od_w_in"]),
                         ("ev_out", ["ev_w_out"])):
        parts = _exchange_wait("summed_" + group, scatters[group], after)
        for nm, p in zip(names, parts):
            w, m, v = big_w[nm][0], big_m[nm][0], big_v[nm][0]
            if nm == "od_w_in":
                res_t = _adamw("adamw_" + nm, [(p, N_DEV)], w.T, m.T, v.T, w.shape[1], 512)
                big_out[nm] = [r.T[None] for r in res_t]
            else:
                big_out[nm] = [r[None] for r in _adamw("adamw_" + nm, [(p, N_DEV)], w, m, v, w.shape[0] // 8)]
            after = big_out[nm][0]

    late_all = _exchange("gather_small_late", [dnpre0], gather=True, deps=[after])[0]
    early_all = _exchange_wait("arrived_small_early", scatters["small"], late_all)

    small_w = (norm_pre, norm_post, ev_lb_logits, ev_a_onorm, ev_b_ln_w, ev_b_ln_b, ev_b_ws, ev_b_bias)
    small_m = (m_norm_pre, m_norm_post, m_ev_lb_logits, m_ev_a_onorm, m_ev_b_ln_w, m_ev_b_ln_b, m_ev_b_ws, m_ev_b_bias)
    small_v = (v_norm_pre, v_norm_post, v_ev_lb_logits, v_ev_a_onorm, v_ev_b_ln_w, v_ev_b_ln_b, v_ev_b_ws, v_ev_b_bias)
    wmv = [tuple(a.reshape(s) for a in t) for s, t in zip(SMALL_PARAM_SHAPES, zip(small_w, small_m, small_v))]
    small_res, loss_row, g_norm_rows = _adamw_small(late_all, early_all, wmv)
    small_out = [[r.reshape(w.shape) for r in four] for four, w in zip(small_res, small_w)]
    loss = loss_row[0, 0]

    g_norms = jnp.concatenate([lax.dynamic_slice(g_norm_rows, (0, 64 * me), (1, 64)),
                               lax.dynamic_slice(g_norm_rows, (1, 64 * me), (1, 64))], axis=1)
    res_n = _adamw("adamw_norms", [(g_norms[None], 1)],
                   jnp.concatenate([od_q_norm, od_kv_norm], axis=1),
                   jnp.concatenate([m_od_q_norm, m_od_kv_norm], axis=1),
                   jnp.concatenate([v_od_q_norm, v_od_kv_norm], axis=1), 1)
    qn_out = [r[:, :64] for r in res_n]
    kvn_out = [r[:, 64:] for r in res_n]

    chip_sums, from_peers = _split_done("summed_ev_in", scatters["ev_in"], loss_row, all_bufs=True)
    own_chip = lax.dynamic_slice_in_dim(chip_sums, me >> 1, 1, 0)
    w = ev_w_in[0]
    big_out["ev_w_in"] = [r[None] for r in _adamw("adamw_ev_w_in", [(own_chip, 1), (from_peers, 3)], w, m_ev_w_in[0],
                                                  v_ev_w_in[0], w.shape[0] // 8)]

    order = ("norm_pre", "norm_post", "ev_w_in", "ev_lb_logits", "ev_a_onorm", "ev_b_ln_w", "ev_b_ln_b",
             "ev_b_ws", "ev_b_bias", "ev_w_out", "od_w_in", "od_q_norm", "od_w_qb", "od_kv_norm",
             "od_w_kvb", "od_w_out")
    small_names = ("norm_pre", "norm_post", "ev_lb_logits", "ev_a_onorm", "ev_b_ln_w", "ev_b_ln_b",
                   "ev_b_ws", "ev_b_bias")
    outs = [loss, grad_x[None]]
    for kind in range(4):
        for nm in order:
            if nm in big_out:
                outs.append(big_out[nm][kind])
            elif nm == "od_q_norm":
                outs.append(qn_out[kind])
            elif nm == "od_kv_norm":
                outs.append(kvn_out[kind])
            else:
                outs.append(small_out[small_names.index(nm)][kind])
    return tuple(outs)
```

```python
import functools

import jax
import jax.numpy as jnp
from jax import lax
from jax.experimental import pallas as pl
from jax.experimental.pallas import tpu as pltpu

F32 = jnp.float32
BF16 = jnp.bfloat16

N_DEV = 8
T = 2048
D = 2048
EPS = 1e-6
A_HEADS = 8
HD = 128
A_CHUNK = 64
A_SUB = 16
B_GROUPS = 8
B_CHUNK = 128
EVEN_IN = 7168
C_HEADS = 16
C_RANK = 512
C_NOPE = 128
C_ROPE = 64
C_QK = C_NOPE + C_ROPE
C_V = 128
ODD_IN = 3136
ODD_IN_PAD = 3200
QP = 256
ROPE_THETA = 10000.0
ATT_SCALE = C_QK ** -0.5

ADAM_LR = 0.001
ADAM_B1 = 0.9
ADAM_B2 = 0.999
ADAM_EPS = 1e-08
ADAM_WD = 0.01
ADAM_STEP = 10

VMEM_LIMIT_V7X = 56 * 1024 * 1024
MESH_ID = pl.DeviceIdType.MESH


def _params(n_grid):
    return pltpu.CompilerParams(dimension_semantics=("arbitrary",) * n_grid,
                                vmem_limit_bytes=VMEM_LIMIT_V7X)


def _dg(a, b, ca, cb):
    return lax.dot_general(a.astype(BF16), b.astype(BF16), (((ca,), (cb,)), ((), ())),
                           preferred_element_type=F32)


def _raw_nn(a, b):
    return _dg(a, b, 1, 0)


def _raw_nt(a, b):
    return _dg(a, b, 1, 1)


def _raw_tn(a, b):
    return _dg(a, b, 0, 0)


@jax.custom_vjp
def _dot_nn(a, b):
    return _raw_nn(a, b)


def _dot_nn_fwd(a, b):
    return _raw_nn(a, b), (a.astype(BF16), b.astype(BF16))


def _dot_nn_bwd(res, g):
    a, b = res
    return _raw_nt(g, b), _raw_tn(a, g)


_dot_nn.defvjp(_dot_nn_fwd, _dot_nn_bwd)


@jax.custom_vjp
def _dot_nt(a, b):
    return _raw_nt(a, b)


def _dot_nt_fwd(a, b):
    return _raw_nt(a, b), (a.astype(BF16), b.astype(BF16))


def _dot_nt_bwd(res, g):
    a, b = res
    return _raw_nn(g, b), _raw_tn(g, a)


_dot_nt.defvjp(_dot_nt_fwd, _dot_nt_bwd)


@jax.custom_vjp
def _dot_tn(a, b):
    return _raw_tn(a, b)


def _dot_tn_fwd(a, b):
    return _raw_tn(a, b), (a.astype(BF16), b.astype(BF16))


def _dot_tn_bwd(res, g):
    a, b = res
    return _raw_nt(b, g), _raw_nn(a, g)


_dot_tn.defvjp(_dot_tn_fwd, _dot_tn_bwd)


@jax.custom_vjp
def _sigmoid(x):
    e = jnp.exp(-jnp.abs(x))
    return jnp.where(x >= 0, 1.0 / (1.0 + e), e / (1.0 + e))


def _sigmoid_fwd(x):
    s = _sigmoid(x)
    return s, s


def _sigmoid_bwd(s, g):
    return (g * s * (1.0 - s),)


_sigmoid.defvjp(_sigmoid_fwd, _sigmoid_bwd)


def _silu(x):
    return x * _sigmoid(x)


def _rms(x, w):
    return x * lax.rsqrt(jnp.mean(x * x, axis=-1, keepdims=True) + EPS) * w


def _split3(x):
    hi = x.astype(BF16)
    r = x - hi.astype(F32)
    mid = r.astype(BF16)
    lo = (r - mid.astype(F32)).astype(BF16)
    return hi, mid, lo


def _mask_apply(mask_bf16, x, contract):
    out = None
    for piece in _split3(x):
        d = lax.dot_general(mask_bf16, piece, (((contract,), (0,)), ((), ())),
                            preferred_element_type=F32)
        out = d if out is None else out + d
    return out


def _chunk_tri(rows):
    r = lax.broadcasted_iota(jnp.int32, (rows, rows), 0)
    c = lax.broadcasted_iota(jnp.int32, (rows, rows), 1)
    return ((r >= c) & (r // A_CHUNK == c // A_CHUNK)).astype(BF16)


@jax.custom_vjp
def _chunk_cumsum(x):
    return _mask_apply(_chunk_tri(x.shape[0]), x, 1)


def _chunk_cumsum_fwd(x):
    return _chunk_cumsum(x), None


def _chunk_cumsum_bwd(_, g):
    return (_mask_apply(_chunk_tri(g.shape[0]), g, 0),)


_chunk_cumsum.defvjp(_chunk_cumsum_fwd, _chunk_cumsum_bwd)


def _hgrn2_rows(q, zf, v, ga, st, l0, l1, onorm):
    rows = q.shape[0]
    n_sub = A_CHUNK // A_SUB
    mx = jnp.maximum(l0, l1)
    e0 = jnp.exp(l0 - mx)
    e1 = jnp.exp(l1 - mx)
    lb = e0 / (e0 + e1)
    lf = jnp.log(lb + (1.0 - lb) * _sigmoid(zf))
    k = (1.0 - lb) * _sigmoid(-zf)
    b = _chunk_cumsum(lf)

    t_idx = lax.broadcasted_iota(jnp.int32, (A_CHUNK, n_sub * A_CHUNK), 0)
    c_idx = lax.broadcasted_iota(jnp.int32, (A_CHUNK, n_sub * A_CHUNK), 1)
    sel = (c_idx // A_CHUNK == t_idx // A_SUB) & (c_idx % A_CHUNK <= t_idx)
    key_row = lax.broadcasted_iota(jnp.int32, (A_CHUNK, HD), 0)

    outs = []
    for n in range(rows // A_CHUNK):
        lo = n * A_CHUNK
        qc, kc, vc = q[lo:lo + A_CHUNK], k[lo:lo + A_CHUNK], v[lo:lo + A_CHUNK]
        lfc, bc = lf[lo:lo + A_CHUNK], b[lo:lo + A_CHUNK]
        b_last = bc[A_CHUNK - 1:A_CHUNK]
        o_inter = _dot_nt(qc * jnp.exp(bc), st)
        kv_t = _dot_tn(vc, kc * jnp.exp(b_last - bc))
        st = st * jnp.exp(b_last) + kv_t
        g_rows, k_subs = [], []
        for i in range(n_sub):
            g_i = bc[i * A_SUB:i * A_SUB + 1] - lfc[i * A_SUB:i * A_SUB + 1]
            g_rows.append(jnp.broadcast_to(g_i, (A_SUB, HD)))
            expo = jnp.where(key_row < (i + 1) * A_SUB, g_i - bc, -jnp.inf)
            k_subs.append(kc * jnp.exp(expo))
        q_sub = qc * jnp.exp(bc - jnp.concatenate(g_rows, axis=0))
        scores = _dot_nt(q_sub, jnp.concatenate(k_subs, axis=0))
        scores = jnp.where(sel, scores, 0.0)
        o_intra = _dot_nn(scores, jnp.concatenate([vc] * n_sub, axis=0))
        outs.append(o_inter + o_intra)
    o = jnp.concatenate(outs, axis=0)
    return _rms(o, onorm) * _silu(ga), st


def _gmlp_rows(u, vb, gb, lnw, lnb, ws, bias):
    rows = u.shape[0]
    mu = jnp.mean(vb, axis=-1, keepdims=True)
    xc = vb - mu
    vg = xc * lax.rsqrt(jnp.mean(xc * xc, axis=-1, keepdims=True) + EPS) * lnw + lnb
    r = lax.broadcasted_iota(jnp.int32, (B_CHUNK, B_CHUNK), 0)
    c = lax.broadcasted_iota(jnp.int32, (B_CHUNK, B_CHUNK), 1)
    ws_causal = jnp.where(r >= c, ws, 0.0)
    svs = [_dot_nn(ws_causal, vg[n * B_CHUNK:(n + 1) * B_CHUNK]) + bias
           for n in range(rows // B_CHUNK)]
    return u * jnp.concatenate(svs, axis=0) * _silu(gb)


def _rope(x, cos_t, sin_t):
    return x * cos_t + pltpu.roll(x, 64, 1) * sin_t


def _rope_transpose(g, cos_t, sin_t):
    return g * cos_t + pltpu.roll(g * sin_t, 64, 1)


ANY_SPEC = pl.BlockSpec(memory_space=pl.ANY)


def _live(deps):
    return [d for d in deps if d is not None]


def _skip_deps(body, n_in, n_deps):
    def wrapped(*refs):
        return body(*refs[:n_in], *refs[n_in + n_deps:])
    return wrapped


def _pure_call(name, fn, grid, in_specs, out_specs, out_shape, args, n_acc=0, deps=()):
    deps = _live(deps)
    n_in, n_out, n_deps = len(in_specs), len(out_specs), len(deps)
    in_specs = list(in_specs) + [ANY_SPEC] * n_deps
    args = tuple(args) + tuple(deps)

    def body(*refs):
        res = fn(*[r[...] for r in refs[:n_in]])
        if not isinstance(res, (tuple, list)):
            res = (res,)
        outs = refs[n_in + n_deps:n_in + n_deps + n_out]
        for o, r in zip(outs[:n_out - n_acc], res[:n_out - n_acc]):
            o[...] = r.astype(o.dtype)
        if n_acc:
            first = functools.reduce(jnp.logical_and, [pl.program_id(i) == 0 for i in range(len(grid))])
            for o, r in zip(outs[n_out - n_acc:], res[n_out - n_acc:]):
                @pl.when(first)
                def _(o=o, r=r):
                    o[...] = r.astype(o.dtype)

                @pl.when(jnp.logical_not(first))
                def _(o=o, r=r):
                    o[...] += r.astype(o.dtype)

    return pl.pallas_call(body, name=name, grid=grid, in_specs=in_specs, out_specs=out_specs,
                          out_shape=out_shape, compiler_params=_params(len(grid)))(*args)


def _sds(shape, dtype):
    return jax.ShapeDtypeStruct(shape, dtype)


def _row_spec(tm, width, col=0):
    return pl.BlockSpec((tm, width), lambda i, col=col: (i, col))


def _full_spec(shape):
    nd = len(shape)
    return pl.BlockSpec(shape, lambda *_: (0,) * nd)


def _mm_nn(name, a, b, out_dtype, tm, tn, deps=()):
    deps = _live(deps)
    m, k = a.shape
    j, _, n = b.shape
    per = n // tn

    def body(a_ref, b_ref, o_ref):
        o_ref[...] = _raw_nn(a_ref[...], b_ref[...]).astype(o_ref.dtype)

    return pl.pallas_call(
        _skip_deps(body, 2, len(deps)), name=name, grid=(m // tm, j * per),
        in_specs=[pl.BlockSpec((tm, k), lambda i, c: (i, 0)),
                  pl.BlockSpec((None, k, tn), lambda i, c: (c // per, 0, c % per))] + [ANY_SPEC] * len(deps),
        out_specs=pl.BlockSpec((tm, tn), lambda i, c: (i, c)),
        out_shape=_sds((m, j * n), out_dtype), compiler_params=_params(2))(a, b, *deps)


def _mm_nt(name, a, b, out_dtype, tm, tn, deps=()):
    deps = _live(deps)
    m = a.shape[0]
    j, nn, n = b.shape

    def body(a_ref, b_ref, o_ref):
        b_all = b_ref[0] if j == 1 else jnp.concatenate([b_ref[s] for s in range(j)], axis=1)
        o_ref[...] = _raw_nt(a_ref[...], b_all).astype(o_ref.dtype)

    return pl.pallas_call(
        _skip_deps(body, 2, len(deps)), name=name, grid=(m // tm, nn // tn),
        in_specs=[pl.BlockSpec((tm, j * n), lambda i, c: (i, 0)),
                  pl.BlockSpec((j, tn, n), lambda i, c: (0, c, 0))] + [ANY_SPEC] * len(deps),
        out_specs=pl.BlockSpec((tm, tn), lambda i, c: (i, c)),
        out_shape=_sds((m, nn), out_dtype), compiler_params=_params(2))(a, b, *deps)


def _mm_tn(name, a, b, j, out_dtype, tm, tn, deps=()):
    deps = _live(deps)
    k, m = a.shape
    n = b.shape[1] // j
    per = n // tn

    def body(a_ref, b_ref, o_ref):
        o_ref[...] = _raw_tn(a_ref[...], b_ref[...]).astype(o_ref.dtype)

    return pl.pallas_call(
        _skip_deps(body, 2, len(deps)), name=name, grid=(m // tm, j * per),
        in_specs=[pl.BlockSpec((k, tm), lambda i, c: (0, i)),
                  pl.BlockSpec((k, tn), lambda i, c: (0, c))] + [ANY_SPEC] * len(deps),
        out_specs=pl.BlockSpec((None, tm, tn), lambda i, c: (c // per, i, c % per)),
        out_shape=_sds((j, m, n), out_dtype), compiler_params=_params(2))(a, b, *deps)


def _mm_tn_parity(name, a, b, j, parity, out_dtype, tm, deps=()):
    deps = _live(deps)
    k, m = a.shape
    n = b.shape[1] // j

    def body(par_ref, a_ref, b_ref, o_ref):
        del par_ref
        o_ref[...] = _raw_tn(a_ref[...], b_ref[...]).astype(o_ref.dtype)

    grid_spec = pltpu.PrefetchScalarGridSpec(
        num_scalar_prefetch=1, grid=(m // tm, j // 2),
        in_specs=[pl.BlockSpec((k, tm), lambda i, s, par: (0, i)),
                  pl.BlockSpec((k, n), lambda i, s, par: (0, 2 * s + par[0]))] + [ANY_SPEC] * len(deps),
        out_specs=pl.BlockSpec((None, tm, n), lambda i, s, par: (s, i, 0)))
    return pl.pallas_call(
        lambda par_ref, *refs: _skip_deps(functools.partial(body, par_ref), 2, len(deps))(*refs),
        name=name, grid_spec=grid_spec, out_shape=_sds((j // 2, m, n), out_dtype),
        compiler_params=_params(2))(parity, a, b, *deps)


TM = 256


def _pre_norm(name, x, w_row, deps=()):
    def fn(xv, w):
        return _rms(xv, w)
    return _pure_call(name, fn, (T // TM,), [_row_spec(TM, D), _full_spec((1, D))],
                      [_row_spec(TM, D)], [_sds((T, D), BF16)], (x, w_row), deps=deps)[0]


def _post_pre_norm(x, y, w_post, w_pre):
    def fn(xv, yv, wp, wn):
        x1 = xv + _rms(yv, wp)
        return x1, _rms(x1, wn)
    return _pure_call("post_pre_norm", fn, (T // TM,),
                      [_row_spec(TM, D), _row_spec(TM, D), _full_spec((1, D)), _full_spec((1, D))],
                      [_row_spec(TM, D), _row_spec(TM, D)],
                      [_sds((T, D), F32), _sds((T, D), BF16)], (x, y, w_post, w_pre))


def _post_pre_norm_bwd(y, x1, w_post, w_pre, dx1_in, dh1, deps=()):
    def fn(yv, x1v, wp, wn, dx1v, dh1v):
        _, vjp_pre = jax.vjp(_rms, x1v, wn)
        dx1_h, dwn = vjp_pre(dh1v)
        dx1 = dx1v + dx1_h
        _, vjp_post = jax.vjp(_rms, yv, wp)
        dy, dwp = vjp_post(dx1)
        return dx1, dy, dwp, dwn
    return _pure_call("post_pre_norm_bwd", fn, (T // TM,),
                      [_row_spec(TM, D), _row_spec(TM, D), _full_spec((1, D)), _full_spec((1, D)),
                       _row_spec(TM, D), _row_spec(TM, D)],
                      [_row_spec(TM, D), _row_spec(TM, D), _full_spec((1, D)), _full_spec((1, D))],
                      [_sds((T, D), F32), _sds((T, D), BF16), _sds((1, D), F32), _sds((1, D), F32)],
                      (y, x1, w_post, w_pre, dx1_in, dh1), n_acc=2, deps=deps)


def _final_loss(x1, y, w_post, target):
    def fn(x1v, yv, wp, tv):
        r, vjp = jax.vjp(_rms, yv, wp)
        err = x1v + r - tv
        part = 0.5 * jnp.sum(jnp.mean(err * err, axis=-1, keepdims=True), axis=0, keepdims=True)
        dx2 = err * (1.0 / D)
        dy, dwp = vjp(dx2)
        return dx2, dy, jnp.broadcast_to(part, (1, 128)), dwp
    return _pure_call("final_loss", fn, (T // TM,),
                      [_row_spec(TM, D), _row_spec(TM, D), _full_spec((1, D)), _row_spec(TM, D)],
                      [_row_spec(TM, D), _row_spec(TM, D), _full_spec((1, 128)), _full_spec((1, D))],
                      [_sds((T, D), F32), _sds((T, D), BF16), _sds((1, 128), F32), _sds((1, D), F32)],
                      (x1, y, w_post, target), n_acc=2)


def _pre_norm_bwd(x, w_row, dh, dx_res, deps=()):
    def fn(xv, w, dhv, dxv):
        _, vjp = jax.vjp(_rms, xv, w)
        dx, dw = vjp(dhv)
        return dxv + dx, dw
    return _pure_call("pre_norm_bwd", fn, (T // TM,),
                      [_row_spec(TM, D), _full_spec((1, D)), _row_spec(TM, D), _row_spec(TM, D)],
                      [_row_spec(TM, D), _full_spec((1, D))],
                      [_sds((T, D), F32), _sds((1, D), F32)], (x, w_row, dh, dx_res), n_acc=1, deps=deps)


RA = 256
RB = 512


HA = 4
A_GROUPS = A_HEADS // HA


def _head(ref, hh):
    return ref[:, hh * HD:(hh + 1) * HD]


def _hgrn2_fwd(z, l0, l1, onorm):
    nb = T // RA

    def body(q_ref, f_ref, v_ref, g_ref, l0_ref, l1_ref, on_ref, cat_ref, sst_ref, st_scr):
        @pl.when(pl.program_id(1) == 0)
        def _():
            st_scr[...] = jnp.zeros_like(st_scr)

        for hh in range(HA):
            st = st_scr[hh]
            sst_ref[hh] = st
            out, st_new = _hgrn2_rows(_head(q_ref, hh), _head(f_ref, hh), _head(v_ref, hh), _head(g_ref, hh), st,
                                      _head(l0_ref, hh), _head(l1_ref, hh), on_ref[...])
            cat_ref[:, hh * HD:(hh + 1) * HD] = out.astype(cat_ref.dtype)
            st_scr[hh] = st_new

    def cols(k):
        return pl.BlockSpec((RA, HA * HD), lambda g, r: (r, k * A_GROUPS + g))

    vec = pl.BlockSpec((1, HA * HD), lambda g, r: (0, g))
    return pl.pallas_call(
        body, name="hgrn2_fwd", grid=(A_GROUPS, nb),
        in_specs=[cols(0), cols(1), cols(2), cols(3), vec, vec, _full_spec((1, HD))],
        out_specs=[cols(0), pl.BlockSpec((HA, None, HD, HD), lambda g, r: (g, r, 0, 0))],
        out_shape=[_sds((T, 2 * A_HEADS * HD), BF16), _sds((A_HEADS, nb, HD, HD), F32)],
        scratch_shapes=[pltpu.VMEM((HA, HD, HD), F32)],
        compiler_params=_params(2))(z, z, z, z, l0, l1, onorm)


def _hgrn2_bwd(z, l0, l1, onorm, sst, dcat, deps=()):
    nb = T // RA
    deps = _live(deps)

    def body(q_ref, f_ref, v_ref, g_ref, l0_ref, l1_ref, on_ref, sst_ref, dcat_ref,
             dq_ref, df_ref, dv_ref, dg_ref, dl0_ref, dl1_ref, don_ref, ds_scr):
        g, r = pl.program_id(0), pl.program_id(1)

        @pl.when(r == 0)
        def _():
            ds_scr[...] = jnp.zeros_like(ds_scr)

        dl0s, dl1s, don = [], [], None
        for hh in range(HA):
            _, vjp = jax.vjp(_hgrn2_rows, _head(q_ref, hh), _head(f_ref, hh), _head(v_ref, hh), _head(g_ref, hh),
                             sst_ref[hh], _head(l0_ref, hh), _head(l1_ref, hh), on_ref[...])
            dq, dzf, dv, dga, dst, dl0, dl1, don_h = vjp((_head(dcat_ref, hh), ds_scr[hh]))
            for ref, val in ((dq_ref, dq), (df_ref, dzf), (dv_ref, dv), (dg_ref, dga)):
                ref[:, hh * HD:(hh + 1) * HD] = val.astype(ref.dtype)
            ds_scr[hh] = dst
            dl0s.append(dl0)
            dl1s.append(dl1)
            don = don_h if don is None else don + don_h
        dl0 = jnp.concatenate(dl0s, axis=1)
        dl1 = jnp.concatenate(dl1s, axis=1)

        @pl.when(r == 0)
        def _():
            dl0_ref[...] = dl0
            dl1_ref[...] = dl1

        @pl.when(r > 0)
        def _():
            dl0_ref[...] += dl0
            dl1_ref[...] += dl1

        first = jnp.logical_and(g == 0, r == 0)

        @pl.when(first)
        def _():
            don_ref[...] = don

        @pl.when(jnp.logical_not(first))
        def _():
            don_ref[...] += don

    def rev(k):
        return pl.BlockSpec((RA, HA * HD), lambda g, r: (nb - 1 - r, k * A_GROUPS + g))

    vec = pl.BlockSpec((1, HA * HD), lambda g, r: (0, g))
    grad = _sds((T, A_HEADS * HD), BF16)
    return pl.pallas_call(
        _skip_deps(body, 9, len(deps)), name="hgrn2_bwd", grid=(A_GROUPS, nb),
        in_specs=[rev(0), rev(1), rev(2), rev(3), vec, vec, _full_spec((1, HD)),
                  pl.BlockSpec((HA, None, HD, HD), lambda g, r: (g, nb - 1 - r, 0, 0)),
                  rev(0)] + [ANY_SPEC] * len(deps),
        out_specs=[rev(0)] * 4 + [vec, vec, _full_spec((1, HD))],
        out_shape=[grad] * 4 + [_sds((1, A_HEADS * HD), F32)] * 2 + [_sds((1, HD), F32)],
        scratch_shapes=[pltpu.VMEM((HA, HD, HD), F32)],
        compiler_params=_params(2))(z, z, z, z, l0, l1, onorm, sst, dcat, *deps)


GB = 4
B_STEPS = B_GROUPS // GB


def _gmlp_specs():
    vec = pl.BlockSpec((1, GB * HD), lambda s, r: (0, s))
    ws = pl.BlockSpec((GB, B_CHUNK, B_CHUNK), lambda s, r: (s, 0, 0))
    bias = pl.BlockSpec((GB, B_CHUNK, 1), lambda s, r: (s, 0, 0))

    def cols(k):
        return pl.BlockSpec((RB, GB * HD), lambda s, r: (r, k * B_STEPS + s))
    return vec, ws, bias, cols


def _gmlp_fwd(z, cat, lnw, lnb, ws, bias):
    vec, ws_spec, bias_spec, cols = _gmlp_specs()

    def body(u_ref, v_ref, g_ref, lnw_ref, lnb_ref, ws_ref, bias_ref, cat_in_ref, cat_ref):
        del cat_in_ref
        for gg in range(GB):
            out = _gmlp_rows(_head(u_ref, gg), _head(v_ref, gg), _head(g_ref, gg), _head(lnw_ref, gg),
                             _head(lnb_ref, gg), ws_ref[gg], bias_ref[gg])
            cat_ref[:, gg * HD:(gg + 1) * HD] = out.astype(cat_ref.dtype)

    return pl.pallas_call(
        body, name="gmlp_fwd", grid=(B_STEPS, T // RB),
        in_specs=[cols(4), cols(5), cols(6), vec, vec, ws_spec, bias_spec, pl.BlockSpec(memory_space=pl.ANY)],
        out_specs=cols(1),
        out_shape=_sds(cat.shape, cat.dtype), input_output_aliases={7: 0},
        compiler_params=_params(2))(z, z, z, lnw, lnb, ws, bias, cat)


def _gmlp_bwd(z, lnw, lnb, ws, bias, dcat):
    vec, ws_spec, bias_spec, cols = _gmlp_specs()

    def body(u_ref, v_ref, g_ref, lnw_ref, lnb_ref, ws_ref, bias_ref, dcat_ref,
             du_ref, dv_ref, dg_ref, dlnw_ref, dlnb_ref, dws_ref, dbias_ref):
        first = pl.program_id(1) == 0
        for gg in range(GB):
            _, vjp = jax.vjp(_gmlp_rows, _head(u_ref, gg), _head(v_ref, gg), _head(g_ref, gg), _head(lnw_ref, gg),
                             _head(lnb_ref, gg), ws_ref[gg], bias_ref[gg])
            du, dv, dg, dlnw, dlnb, dws, dbias = vjp(_head(dcat_ref, gg))
            lanes = slice(gg * HD, (gg + 1) * HD)
            for ref, val in ((du_ref, du), (dv_ref, dv), (dg_ref, dg)):
                ref[:, lanes] = val.astype(ref.dtype)
            sums = ((dlnw_ref, (slice(None), lanes), dlnw), (dlnb_ref, (slice(None), lanes), dlnb),
                    (dws_ref, gg, dws), (dbias_ref, gg, dbias))
            for ref, idx, val in sums:
                @pl.when(first)
                def _(ref=ref, idx=idx, val=val):
                    ref[idx] = val

                @pl.when(jnp.logical_not(first))
                def _(ref=ref, idx=idx, val=val):
                    ref[idx] += val

    grad = _sds((T, B_GROUPS * HD), BF16)
    return pl.pallas_call(
        body, name="gmlp_bwd", grid=(B_STEPS, T // RB),
        in_specs=[cols(4), cols(5), cols(6), vec, vec, ws_spec, bias_spec, cols(1)],
        out_specs=[cols(0)] * 3 + [vec, vec, ws_spec, bias_spec],
        out_shape=[grad] * 3 + [_sds((1, B_GROUPS * HD), F32)] * 2
        + [_sds((B_GROUPS, B_CHUNK, B_CHUNK), F32), _sds((B_GROUPS, B_CHUNK, 1), F32)],
        compiler_params=_params(2))(z, z, z, lnw, lnb, ws, bias, dcat)


def _mla_pre(z1, qn, kvn, cos_t, sin_t):
    def fn(cq, ckv, kpe, cs, sn, wq, wkv):
        return _rms(cq, wq), _rms(ckv, wkv), _rope(kpe, cs, sn)
    return _pure_call("mla_pre", fn, (T // TM,),
                      [_row_spec(TM, C_RANK, 4), _row_spec(TM, C_RANK, 5), _row_spec(TM, HD, 24),
                       _row_spec(TM, HD), _row_spec(TM, HD),
                       _full_spec((1, C_RANK)), _full_spec((1, C_RANK))],
                      [_row_spec(TM, C_RANK), _row_spec(TM, C_RANK), _row_spec(TM, HD)],
                      [_sds((T, C_RANK), BF16), _sds((T, C_RANK), BF16), _sds((T, HD), BF16)],
                      (z1, z1, z1, cos_t, sin_t, qn, kvn))


def _mla_pre_bwd(z1, qn, kvn, cos_t, sin_t, dcqn, dckvn, dkp, deps=()):
    def fn(cq, ckv, cs, sn, wq, wkv, g_q, g_kv, g_kp):
        _, vjp_q = jax.vjp(_rms, cq, wq)
        dcq, dwq = vjp_q(g_q)
        _, vjp_kv = jax.vjp(_rms, ckv, wkv)
        dckv, dwkv = vjp_kv(g_kv)
        return dcq, dckv, _rope_transpose(g_kp, cs, sn), dwq, dwkv
    return _pure_call("mla_pre_bwd", fn, (T // TM,),
                      [_row_spec(TM, C_RANK, 4), _row_spec(TM, C_RANK, 5),
                       _row_spec(TM, HD), _row_spec(TM, HD),
                       _full_spec((1, C_RANK)), _full_spec((1, C_RANK)),
                       _row_spec(TM, C_RANK), _row_spec(TM, C_RANK), _row_spec(TM, HD)],
                      [_row_spec(TM, C_RANK), _row_spec(TM, C_RANK), _row_spec(TM, HD),
                       _full_spec((1, C_RANK)), _full_spec((1, C_RANK))],
                      [_sds((T, C_RANK), BF16), _sds((T, C_RANK), BF16), _sds((T, HD), BF16),
                       _sds((1, C_RANK), F32), _sds((1, C_RANK), F32)],
                      (z1, z1, cos_t, sin_t, qn, kvn, dcqn, dckvn, dkp), n_acc=2, deps=deps)


TQ = 256
HP = 2
KVW = C_NOPE + C_V


def _att_keys(kv_ref, kp_ref, k_scr):
    @pl.when(pl.program_id(1) == 0)
    def _():
        for hh in range(HP):
            k_scr[hh, :, 0:C_NOPE] = kv_ref[:, hh * KVW:hh * KVW + C_NOPE]
            k_scr[hh, :, C_NOPE:QP] = kp_ref[...]


def _att_scores(q, cos_ref, sin_ref, k_scr, hh, n):
    keys = (n + 1) * TQ
    qr = jnp.concatenate([q[:, :C_NOPE], _rope(q[:, C_NOPE:], cos_ref[...], sin_ref[...])], axis=1).astype(BF16)
    return qr, _raw_nt(qr, k_scr[hh, 0:keys, :]) * ATT_SCALE


def _causal(x, n, fill):
    row = lax.broadcasted_iota(jnp.int32, (TQ, TQ), 0)
    col = lax.broadcasted_iota(jnp.int32, (TQ, TQ), 1)
    diag = jnp.where(col <= row, x[:, n * TQ:], fill)
    return diag if n == 0 else jnp.concatenate([x[:, :n * TQ], diag], axis=1)


def _per_query_block(fn):
    for n in range(T // TQ):
        pl.when(pl.program_id(1) == n)(functools.partial(fn, n))


def _att_in_specs():
    return [pl.BlockSpec((TQ, HP * QP), lambda g, i: (i, g)),
            pl.BlockSpec((TQ, HD), lambda g, i: (i, 0)),
            pl.BlockSpec((TQ, HD), lambda g, i: (i, 0)),
            pl.BlockSpec((T, HP * KVW), lambda g, i: (0, g)),
            pl.BlockSpec((T, HD), lambda g, i: (0, 0))]


def _attention_fwd(q, cos_t, sin_t, kv, kp, z1):
    def body(q_ref, cos_ref, sin_ref, kv_ref, kp_ref, gate_ref, o_ref, lse_ref, og_ref, k_scr):
        _att_keys(kv_ref, kp_ref, k_scr)

        def block(n):
            keys = (n + 1) * TQ
            for hh in range(HP):
                _, s = _att_scores(q_ref[:, hh * QP:(hh + 1) * QP], cos_ref, sin_ref, k_scr, hh, n)
                s = _causal(s, n, jnp.finfo(F32).min)
                m = jnp.max(s, axis=-1, keepdims=True)
                p = jnp.exp(s - m)
                l = jnp.sum(p, axis=-1, keepdims=True)
                v = kv_ref[0:keys, hh * KVW + C_NOPE:(hh + 1) * KVW]
                o = _raw_nn(p, v) / l
                lanes = slice(hh * C_V, (hh + 1) * C_V)
                o_ref[:, lanes] = o
                og_ref[:, lanes] = (o * _silu(gate_ref[:, lanes])).astype(og_ref.dtype)
                lse_ref[hh] = m + jnp.log(l)

        _per_query_block(block)

    heads = pl.BlockSpec((TQ, HP * C_V), lambda g, i: (i, g))
    return pl.pallas_call(
        body, name="attention_fwd", grid=(C_HEADS // HP, T // TQ), in_specs=_att_in_specs() + [heads],
        out_specs=[heads, pl.BlockSpec((HP, TQ, 1), lambda g, i: (g, i, 0)), heads],
        out_shape=[_sds((T, C_HEADS * C_V), F32), _sds((C_HEADS, T, 1), F32), _sds((T, C_HEADS * C_V), BF16)],
        scratch_shapes=[pltpu.VMEM((HP, T, QP), BF16)],
        compiler_params=_params(2))(q, cos_t, sin_t, kv, kp, z1)


def _attention_bwd(q, cos_t, sin_t, kv, kp, o, lse, dog, z1):
    nq = T // TQ

    def body(q_ref, cos_ref, sin_ref, kv_ref, kp_ref, o_ref, lse_ref, dog_ref, gate_ref,
             dq_ref, dkv_ref, dkp_ref, dgate_ref, k_scr, dk_scr, dv_scr):
        g, i = pl.program_id(0), pl.program_id(1)
        _att_keys(kv_ref, kp_ref, k_scr)

        @pl.when(i == 0)
        def _():
            dv_scr[...] = jnp.zeros_like(dv_scr)
            dk_scr[...] = jnp.zeros_like(dk_scr)

        def block(n):
            keys = (n + 1) * TQ
            for hh in range(HP):
                qr, s = _att_scores(q_ref[:, hh * QP:(hh + 1) * QP], cos_ref, sin_ref, k_scr, hh, n)
                p = _causal(jnp.exp(s - lse_ref[hh]), n, 0.0)
                lanes = slice(hh * C_V, (hh + 1) * C_V)
                ov, gate, dogv = o_ref[:, lanes], gate_ref[:, lanes], dog_ref[:, lanes]
                sig = _sigmoid(gate)
                silu = gate * sig
                dov = dogv * silu
                dgate_ref[:, lanes] = (dogv * ov * (sig + silu * (1.0 - sig))).astype(dgate_ref.dtype)
                delta = jnp.sum(dov * ov, axis=-1, keepdims=True)
                dp = _raw_nt(dov, kv_ref[0:keys, hh * KVW + C_NOPE:(hh + 1) * KVW])
                ds = p * (dp - delta) * ATT_SCALE
                dq = _raw_nn(ds, k_scr[hh, 0:keys, :])
                dq_ref[:, hh * QP:(hh + 1) * QP] = jnp.concatenate(
                    [dq[:, :C_NOPE], _rope_transpose(dq[:, C_NOPE:], cos_ref[...], sin_ref[...])],
                    axis=1).astype(dq_ref.dtype)
                dv_scr[hh, 0:keys, :] += _raw_tn(p, dov)
                dk_scr[hh, 0:keys, :] += _raw_tn(ds, qr)

        _per_query_block(block)

        @pl.when(i == nq - 1)
        def _():
            for hh in range(HP):
                dkv_ref[:, hh * KVW:(hh + 1) * KVW] = jnp.concatenate(
                    [dk_scr[hh, :, 0:C_NOPE], dv_scr[hh]], axis=1).astype(dkv_ref.dtype)

        @pl.when(jnp.logical_and(i == nq - 1, g == 0))
        def _():
            dkp_ref[...] = dk_scr[0, :, C_NOPE:QP]

        @pl.when(jnp.logical_and(i == nq - 1, g > 0))
        def _():
            dkp_ref[...] += dk_scr[0, :, C_NOPE:QP]

        @pl.when(i == nq - 1)
        def _():
            for hh in range(1, HP):
                dkp_ref[...] += dk_scr[hh, :, C_NOPE:QP]

    heads = pl.BlockSpec((TQ, HP * C_V), lambda g, i: (i, g))
    return pl.pallas_call(
        body, name="attention_bwd", grid=(C_HEADS // HP, nq),
        in_specs=_att_in_specs() + [heads, pl.BlockSpec((HP, TQ, 1), lambda g, i: (g, i, 0)), heads, heads],
        out_specs=[pl.BlockSpec((TQ, HP * QP), lambda g, i: (i, g)),
                   pl.BlockSpec((T, HP * KVW), lambda g, i: (0, g)),
                   _full_spec((T, HD)), heads],
        out_shape=[_sds((T, C_HEADS * QP), BF16), _sds((T, C_HEADS * KVW), BF16), _sds((T, HD), F32),
                   _sds((T, C_HEADS * C_V), BF16)],
        scratch_shapes=[pltpu.VMEM((HP, T, QP), BF16), pltpu.VMEM((HP, T, QP), F32), pltpu.VMEM((HP, T, C_V), F32)],
        compiler_params=_params(2))(q, cos_t, sin_t, kv, kp, o, lse, dog, z1)


def _adamw_math(w, g, m, v):
    m = ADAM_B1 * m + (1.0 - ADAM_B1) * g
    v = ADAM_B2 * v + (1.0 - ADAM_B2) * (g * g)
    m_hat = m / (1.0 - ADAM_B1 ** ADAM_STEP)
    v_hat = v / (1.0 - ADAM_B2 ** ADAM_STEP)
    delta = -ADAM_LR * (m_hat / (jnp.sqrt(v_hat) + ADAM_EPS) + ADAM_WD * w)
    return delta, m, v


def _adamw(name, parts, w, m, v, tr, tc=None):
    rows, cols = w.shape

    def fn(*vals):
        pvs, (wv, mv, vv) = vals[:len(parts)], vals[len(parts):]
        g = None
        for pv in pvs:
            for d in range(pv.shape[0]):
                term = pv[d].astype(F32)
                g = term if g is None else g + term
        return (g,) + _adamw_math(wv, g, mv, vv)

    tc = cols if tc is None else tc
    blk = pl.BlockSpec((tr, tc), lambda i, j: (i, j))
    part_specs = [pl.BlockSpec((n, tr, tc), lambda i, j: (0, i, j)) for _, n in parts]
    return _pure_call(name, fn, (rows // tr, cols // tc), part_specs + [blk, blk, blk],
                      [blk] * 4, [_sds((rows, cols), F32)] * 4, tuple(p for p, _ in parts) + (w, m, v))


SMALL_PARAM_SHAPES = ((2, D), (2, D), (2, A_HEADS * HD), (1, HD), (1, B_GROUPS * HD), (1, B_GROUPS * HD),
                      (B_GROUPS, B_CHUNK, B_CHUNK), (B_GROUPS, B_CHUNK))
SMALL_PIECES = ((0, 0, 0, 0), (0, 1, 1, 0), (1, 0, 1, 1), (1, 1, 1, 2), (2, 0, 2, 0), (2, 1, 2, 1),
                (3, 0, 3, 8), (4, 0, 2, 2), (5, 0, 2, 3))


def _small_rows(dnpre1, dnpost0, dnpost1, dl0, dl1, donorm, dlnw, dlnb, dws, dbias, dqn, dkvn, loss_part):
    return [jnp.concatenate([dnpre1, dnpost0, dnpost1], axis=0),
            jnp.concatenate([dl0, dl1, dlnw, dlnb], axis=0),
            jnp.concatenate([dbias.reshape(B_GROUPS, B_CHUNK), donorm, loss_part], axis=0),
            dws,
            jnp.concatenate([dqn, dkvn], axis=0)]


def _adamw_small(late_all, early_all, wmv):
    n_in = 6 + 3 * len(wmv)

    def body(*refs):
        gathered, params, outs = refs[:6], refs[6:n_in], refs[n_in:]

        def total(ref):
            s = ref[0]
            for d in range(1, N_DEV):
                s = s + ref[d]
            return s

        g_late, g2048, g1024, g128, g_ws, g512 = [total(r) for r in gathered]
        arrays = (g_late, g2048, g1024, g128)

        def update(p, rows, g):
            w_ref, m_ref, v_ref = params[3 * p:3 * p + 3]
            delta, m, v = _adamw_math(w_ref[rows], g, m_ref[rows], v_ref[rows])
            for out, val in zip(outs[4 * p:4 * p + 4], (g, delta, m, v)):
                out[rows] = val

        for p, row, arr, arr_row in SMALL_PIECES:
            update(p, pl.ds(row, 1), arrays[arr][arr_row:arr_row + 1])
        update(6, slice(None), g_ws)
        update(7, slice(None), g128[0:B_GROUPS])
        outs[32][...] = g128[B_GROUPS + 1:B_GROUPS + 2]
        outs[33][...] = g512

    vmem = pl.BlockSpec(memory_space=pltpu.VMEM)
    flat = [a for t in wmv for a in t]
    out_shape = [_sds(s, F32) for s in SMALL_PARAM_SHAPES for _ in range(4)] + [_sds((1, 128), F32), _sds((2, C_RANK), F32)]
    res = pl.pallas_call(body, name="adamw_small", in_specs=[vmem] * n_in, out_specs=[vmem] * len(out_shape),
                         out_shape=out_shape,
                         compiler_params=pltpu.CompilerParams(vmem_limit_bytes=VMEM_LIMIT_V7X))(late_all, *early_all, *flat)
    return [res[4 * p:4 * p + 4] for p in range(8)], res[32], res[33]


def _exchange(name, arrs, gather, deps=()):
    n = len(arrs)
    deps = _live(deps)

    def body(*refs):
        ins, outs = refs[:n], refs[n + len(deps):2 * n + len(deps)]
        send_sems, recv_sems, local_sems = refs[2 * n + len(deps):]
        x, y, c = lax.axis_index("x"), lax.axis_index("y"), lax.axis_index("c")
        me = 4 * x + 2 * y + c

        def peer(k):
            return (x ^ (k >> 2), y ^ ((k >> 1) & 1), c ^ (k & 1))

        def copy(a, k):
            src = ins[a] if gather else ins[a].at[me ^ k]
            return pltpu.make_async_remote_copy(
                src_ref=src, dst_ref=outs[a].at[me], send_sem=send_sems.at[a, k - 1],
                recv_sem=recv_sems.at[a, k - 1], device_id=peer(k), device_id_type=MESH_ID)

        def arrival(a, k):
            src = ins[a] if gather else ins[a].at[me]
            return pltpu.make_async_remote_copy(
                src_ref=src, dst_ref=outs[a].at[me ^ k], send_sem=send_sems.at[a, k - 1],
                recv_sem=recv_sems.at[a, k - 1], device_id=peer(k), device_id_type=MESH_ID)

        own = [pltpu.make_async_copy(ins[a] if gather else ins[a].at[me], outs[a].at[me], local_sems.at[a])
               for a in range(n)]
        for cp in own:
            cp.start()
        for k in range(1, N_DEV):
            for a in range(n):
                copy(a, k).start()
        for k in range(1, N_DEV):
            for a in range(n):
                arrival(a, k).wait_recv()
        for k in range(1, N_DEV):
            for a in range(n):
                copy(a, k).wait_send()
        for cp in own:
            cp.wait()

    any_spec = pl.BlockSpec(memory_space=pl.ANY)
    out_shape = [_sds((N_DEV,) + a.shape if gather else a.shape, a.dtype) for a in arrs]
    return pl.pallas_call(
        body, name=name, in_specs=[any_spec] * (n + len(deps)), out_specs=[any_spec] * n, out_shape=out_shape,
        scratch_shapes=[pltpu.SemaphoreType.DMA((n, N_DEV - 1)), pltpu.SemaphoreType.DMA((n, N_DEV - 1)),
                        pltpu.SemaphoreType.DMA((n,))],
        compiler_params=pltpu.CompilerParams(has_side_effects=True))(*arrs, *deps)


HBM_SPEC = pl.BlockSpec(memory_space=pltpu.HBM)
SEM_SPEC = pl.BlockSpec(memory_space=pltpu.SEMAPHORE)
DATAFLOW = pltpu.SideEffectType.DATAFLOW_SIDE_EFFECTING


def _my_index():
    return 4 * lax.axis_index("x") + 2 * lax.axis_index("y") + lax.axis_index("c")


def _plan_copies(plan, refs, send_sems, recv_sems):
    x, y, c = lax.axis_index("x"), lax.axis_index("y"), lax.axis_index("c")
    return [pltpu.make_async_remote_copy(
        src_ref=src, dst_ref=dst, send_sem=send_sems.at[i], recv_sem=recv_sems.at[i],
        device_id=(x ^ (k >> 2), y ^ ((k >> 1) & 1), c ^ (k & 1)), device_id_type=MESH_ID)
        for i, (src, dst, k) in enumerate(plan(refs, 4 * x + 2 * y + c))]


def _split_call(name, bufs, waits=None, starts=None, deps=()):
    n = len(bufs)
    deps = _live(deps)
    n_wait = 2 if waits else 0

    def body(*refs):
        zones = refs[:n]
        if waits:
            for cp in _plan_copies(waits[2], zones, refs[n], refs[n + 1]):
                cp.wait_send()
                cp.wait_recv()
        if starts:
            first_out = n + n_wait + len(deps)
            for cp in _plan_copies(starts[0], zones, refs[first_out], refs[first_out + 1]):
                cp.start()
            refs[-1][...] = jnp.zeros_like(refs[-1])

    out_specs, out_shape = [], []
    if starts:
        sems = pltpu.SemaphoreType.DMA((starts[1],))
        out_specs, out_shape = [SEM_SPEC, SEM_SPEC], [sems, sems]
    out_specs += [HBM_SPEC] * n
    out_shape += [pltpu.HBM(b.shape, b.dtype) for b in bufs]
    if starts:
        out_specs.append(pl.BlockSpec(memory_space=pltpu.VMEM))
        out_shape.append(_sds((8, 128), F32))
    first_buf = 2 if starts else 0
    res = pl.pallas_call(
        body, name=name,
        in_specs=[HBM_SPEC] * n + [SEM_SPEC] * n_wait + [ANY_SPEC] * len(deps),
        out_specs=out_specs, out_shape=out_shape,
        input_output_aliases={i: first_buf + i for i in range(n)},
        compiler_params=pltpu.CompilerParams(has_side_effects=DATAFLOW),
    )(*[pltpu.with_memory_space_constraint(b, pltpu.HBM) for b in bufs], *(waits[:2] if waits else ()), *deps)
    out_bufs = list(res[first_buf:first_buf + n])
    return out_bufs, ((res[0], res[1]) if starts else None), (res[-1] if starts else None)


def _direct_plan(n, gather):
    def plan(refs, me):
        return [(refs[a] if gather else refs[a].at[me ^ k], refs[n + a].at[me], k)
                for k in range(1, N_DEV) for a in range(n)]
    return plan


def _own_slot_filled(a, gather):
    me = _my_index()
    if gather:
        return lax.dynamic_update_slice_in_dim(lax.empty((N_DEV,) + a.shape, a.dtype), a[None], me, 0)
    return lax.dynamic_update_slice_in_dim(lax.empty(a.shape, a.dtype), lax.dynamic_slice_in_dim(a, me, 1, 0), me, 0)


def _exchange_start(name, arrs, gather, deps=()):
    n = len(arrs)
    lands = [_own_slot_filled(a, gather) for a in arrs]
    plan = _direct_plan(n, gather)
    bufs, sems, token = _split_call(name, list(arrs) + lands, starts=(plan, n * (N_DEV - 1)), deps=deps)
    return (n, plan, sems, bufs, None), token


def _exchange_wait(name, handle, after):
    return _split_done(name, handle, after)


ICI_PEERS = (2, 4, 6)
SIBLING = 1


def _gather2_send(name, arrs, deps=()):
    n = len(arrs)
    lands = [_own_slot_filled(a, True) for a in arrs]

    def plan(refs, me_):
        return [(refs[a], refs[n + a].at[me_], k) for k in (SIBLING,) + ICI_PEERS for a in range(n)]

    bufs, sems, token = _split_call(name, list(arrs) + lands, starts=(plan, 4 * n), deps=deps)
    return (n, plan, sems, bufs, None), token


def _gather2_relay(name, handle, after):
    n, plan, sems, bufs, _ = handle
    after = after if isinstance(after, (list, tuple)) else [after]

    def relay(refs, me_):
        return [(refs[n + a].at[me_ ^ k], refs[n + a].at[me_ ^ k], SIBLING) for k in ICI_PEERS for a in range(n)]

    bufs, sems2, token = _split_call(name, bufs, waits=(sems[0], sems[1], plan), starts=(relay, 3 * n), deps=after)
    return (n, relay, sems2, bufs, None), token


def _split_done(name, handle, after, all_bufs=False):
    n, plan, sems, bufs, _ = handle
    bufs, _, _ = _split_call(name, bufs, waits=(sems[0], sems[1], plan), deps=[after])
    return bufs if all_bufs else bufs[n:]


def _scatter2_pair(name, for_sibling, deps=()):
    n = len(for_sibling)
    pairs = [lax.empty(s.shape, s.dtype) for s in for_sibling]

    def plan(refs, me):
        del me
        return [(refs[a].at[s], refs[n + a].at[s], SIBLING) for s in range(4) for a in range(n)]

    bufs, sems, token = _split_call(name, list(for_sibling) + pairs, starts=(plan, 4 * n), deps=deps)
    return (n, plan, sems, bufs, None), token


def _pair_add(name, mine, pair):
    _, rows, cols = mine.shape
    tr = rows // 2

    def fn(a, b):
        return a.astype(F32) + b.astype(F32)

    blk = pl.BlockSpec((None, tr, cols), lambda s, i: (s, i, 0))
    return _pure_call(name, fn, (4, rows // tr), [blk, blk], [blk], [_sds(mine.shape, mine.dtype)], (mine, pair))[0]


def _scatter2_send(name, chip_sums, deps=()):
    n = len(chip_sums)
    finals = [lax.empty((3,) + c.shape[1:], c.dtype) for c in chip_sums]

    def plan(refs, me):
        return [(refs[a].at[(me >> 1) ^ j], refs[n + a].at[j - 1], 2 * j) for j in range(1, 4) for a in range(n)]

    bufs, sems, token = _split_call(name, list(chip_sums) + finals, starts=(plan, 3 * n), deps=deps)
    return (n, plan, sems, bufs, None), token


def _pad_rope(p):
    z = jnp.zeros(p.shape[:-1] + (32,), p.dtype)
    return jnp.concatenate([p[..., :32], z, p[..., 32:], z], axis=-1)


def _unpad_rope(p):
    return jnp.concatenate([p[..., :32], p[..., 64:96]], axis=-1)


def _odd_in_layout(wt):
    wt = wt.reshape(ODD_IN, D)
    cq, ckv, kpe, gate = wt[:512], wt[512:1024], wt[1024:1088], wt[1088:]
    z = jnp.zeros((32, D), wt.dtype)
    return jnp.concatenate([gate, cq, ckv, kpe[:32], z, kpe[32:], z], axis=0)


def _odd_in_unlayout(dwt):
    gate, cq, ckv, kpe = dwt[:2048], dwt[2048:2560], dwt[2560:3072], dwt[3072:]
    wt = jnp.concatenate([cq, ckv, kpe[:32], kpe[64:96], gate], axis=0)
    return wt.reshape(N_DEV, ODD_IN // N_DEV, D)


def _qb_layout(w):
    w = w.transpose(1, 0, 2).reshape(C_RANK, C_HEADS, C_QK)
    w = jnp.concatenate([w[..., :C_NOPE], _pad_rope(w[..., C_NOPE:])], axis=-1)
    return w.reshape(C_RANK, C_HEADS * QP)


def _qb_unlayout(dw):
    dw = dw.reshape(C_RANK, C_HEADS, QP)
    dw = jnp.concatenate([dw[..., :C_NOPE], _unpad_rope(dw[..., C_NOPE:])], axis=-1)
    return dw.reshape(C_RANK, N_DEV, C_HEADS * C_QK // N_DEV).transpose(1, 0, 2)


def _rope_tables(positions):
    inv_freq = ROPE_THETA ** (-jnp.arange(0, C_ROPE, 2, dtype=F32) / C_ROPE)
    ang = positions.astype(F32)[0][:, None] * inv_freq
    cos, sin = jnp.cos(ang), jnp.sin(ang)
    z = jnp.zeros_like(cos)
    return jnp.concatenate([cos, z, cos, z], axis=1), jnp.concatenate([-sin, z, sin, z], axis=1)


def _forward_backward(x, cos_t, sin_t, target, norm_pre, norm_post, lb_logits, a_onorm, ln_w, ln_b,
                      b_ws, b_bias, get_w, put_g, put_small=None, start_dep=None):
    npre0, npre1 = norm_pre[0:1], norm_pre[1:2]
    npost0, npost1 = norm_post[0:1], norm_post[1:2]
    l0, l1 = lb_logits[0:1], lb_logits[1:2]
    bias_col = b_bias.reshape(B_GROUPS, B_CHUNK, 1)
    ws = b_ws.reshape(B_GROUPS, B_CHUNK, B_CHUNK)

    h0 = _pre_norm("pre_norm0", x, npre0, deps=[start_dep])
    w_ev_in = get_w("ev_in", h0)
    z0 = _mm_nn("ev_in", h0, w_ev_in, F32, 1024, 896)
    cat, sst = _hgrn2_fwd(z0, l0, l1, a_onorm)
    cat = _gmlp_fwd(z0, cat, ln_w, ln_b, ws, bias_col)
    w_ev_out = get_w("ev_out", cat)
    y0 = _mm_nn("ev_out", cat, w_ev_out, F32, 1024, 1024)
    get_w("od_relay", y0)
    x1, h1 = _post_pre_norm(x, y0, npost0, npre1)
    w_od_in, w_qb, w_kvb, q_norm, kv_norm = get_w("od_mid", h1)
    z1 = _mm_nt("od_in", h1, w_od_in[None], F32, 1024, 640)
    cqn, ckvn, kp = _mla_pre(z1, q_norm, kv_norm, cos_t, sin_t)
    q = _mm_nn("od_qb", cqn, w_qb[None], F32, 1024, 1024)
    kv = _mm_nn("od_kvb", ckvn, w_kvb, BF16, 1024, 512)
    o, lse, og = _attention_fwd(q, cos_t, sin_t, kv, kp, z1)
    w_od_out = get_w("od_out", og)
    y1 = _mm_nn("od_out", og, w_od_out, F32, 1024, 1024)
    dx2, dy1, loss_part, dnpost1 = _final_loss(x1, y1, npost1, target)

    g_od_out = _mm_tn("od_out_dw", og, dy1, 1, BF16, 1024, 1024)
    tok = put_g("od_out", [g_od_out.reshape(N_DEV, D // N_DEV, D)])
    dog = _mm_nt("od_out_dx", dy1, w_od_out, F32, 1024, 1024, deps=[tok])
    dq, dkv, dkp, dgate = _attention_bwd(q, cos_t, sin_t, kv, kp, o, lse, dog, z1)
    g_qb = _mm_tn("od_qb_dw", cqn, dq, 1, F32, 512, 1024)
    g_kvb = _mm_tn("od_kvb_dw", ckvn, dkv, N_DEV, BF16, 512, 512)
    tok = put_g("od_qkv", [_qb_unlayout(g_qb[0]).astype(BF16), g_kvb])
    dcqn = _mm_nt("od_qb_dx", dq, w_qb[None], F32, 1024, 512, deps=[tok])
    dckvn = _mm_nt("od_kvb_dx", dkv, w_kvb, F32, 1024, 512)
    dcq, dckv, dkpe, dqn, dkvn = _mla_pre_bwd(z1, q_norm, kv_norm, cos_t, sin_t, dcqn, dckvn, dkp)
    dz1 = jnp.concatenate([dgate, dcq, dckv, dkpe], axis=1)
    g_od_in = _mm_tn("od_in_dw", dz1, h1, 1, F32, 640, 1024)
    tok = put_g("od_in", [_odd_in_unlayout(g_od_in[0]).astype(BF16)])
    dh1 = _mm_nn("od_in_dx", dz1, w_od_in[None], F32, 1024, 1024, deps=[tok])
    dx1, dy0, dnpost0, dnpre1 = _post_pre_norm_bwd(y0, x1, npost0, npre1, dx2, dh1)

    g_ev_out = _mm_tn("ev_out_dw", cat, dy0, 1, BF16, 1024, 1024)
    tok = put_g("ev_out", [g_ev_out.reshape(N_DEV, D // N_DEV, D)])
    dcat = _mm_nt("ev_out_dx", dy0, w_ev_out, F32, 1024, 1024, deps=[tok])
    dqa, dfa, dia, dga, dl0, dl1, donorm = _hgrn2_bwd(z0, l0, l1, a_onorm, sst, dcat)
    dub, dvb, dgb, dlnw, dlnb, dws, dbias = _gmlp_bwd(z0, ln_w, ln_b, ws, bias_col, dcat)
    dz0 = jnp.concatenate([dqa, dfa, dia, dga, dub, dvb, dgb], axis=1)
    early = _small_rows(dnpre1, dnpost0, dnpost1, dl0, dl1, donorm, dlnw, dlnb, dws, dbias, dqn, dkvn, loss_part)
    tok = put_small(early) if put_small else None
    small_tok = tok

    def ev_in_half(name, parity, deps=()):
        return _mm_tn_parity(name, h0, dz0, N_DEV, parity, BF16, 1024, deps=[small_tok] + list(deps))

    tok = put_g("ev_in", ev_in_half)
    dh0 = _mm_nt("ev_in_dx", dz0, w_ev_in, F32, 1024, 256, deps=[tok])
    grad_x, dnpre0 = _pre_norm_bwd(x, npre0, dh0, dx1)
    return grad_x, early, dnpre0


def kernel(x, positions, norm_pre, norm_post, ev_w_in, ev_lb_logits, ev_a_onorm, ev_b_ln_w, ev_b_ln_b, ev_b_ws, ev_b_bias, ev_w_out, od_w_in, od_q_norm, od_w_qb, od_kv_norm, od_w_kvb, od_w_out, loss_target, m_norm_pre, m_norm_post, m_ev_w_in, m_ev_lb_logits, m_ev_a_onorm, m_ev_b_ln_w, m_ev_b_ln_b, m_ev_b_ws, m_ev_b_bias, m_ev_w_out, m_od_w_in, m_od_q_norm, m_od_w_qb, m_od_kv_norm, m_od_w_kvb, m_od_w_out, v_norm_pre, v_norm_post, v_ev_w_in, v_ev_lb_logits, v_ev_a_onorm, v_ev_b_ln_w, v_ev_b_ln_b, v_ev_b_ws, v_ev_b_bias, v_ev_w_out, v_od_w_in, v_od_q_norm, v_od_w_qb, v_od_kv_norm, v_od_w_kvb, v_od_w_out):
    me = 4 * lax.axis_index("x") + 2 * lax.axis_index("y") + lax.axis_index("c")
    bf = lambda w: w[0].astype(BF16)

    norms = jnp.pad(jnp.concatenate([od_q_norm, od_kv_norm], axis=1), ((0, 7), (0, 0)))
    sent = {}
    sent["ev_in"], tok = _gather2_send("gather_ev_in", [bf(ev_w_in)])
    sent["ev_out"], tok = _gather2_send("gather_ev_out", [bf(ev_w_out)], deps=[tok])
    sent["od"], tok = _gather2_send("gather_od", [od_w_in[0].T.astype(BF16), bf(od_w_qb), bf(od_w_kvb), norms,
                                                 bf(od_w_out)], deps=[tok])
    cos_t, sin_t = _rope_tables(positions)
    od = []

    def get_w(group, after):
        if group == "ev_in":
            relayed, token = _gather2_relay("relay_ev_in", sent["ev_in"], [after, cos_t, sin_t])
            return _split_done("arrived_ev_in", relayed, token)[0]
        if group == "ev_out":
            relayed, token = _gather2_relay("relay_ev_out", sent["ev_out"], after)
            return _split_done("arrived_ev_out", relayed, token)[0].reshape(1, D, D)
        if group == "od_relay":
            sent["od_relayed"], _ = _gather2_relay("relay_od", sent["od"], after)
            return None
        if not od:
            od.extend(_split_done("arrived_od", sent["od_relayed"], after))
        w_od_in, w_qb, w_kvb, norms_all, w_od_out = od
        if group == "od_out":
            return w_od_out.reshape(1, D, D)
        return (_odd_in_layout(w_od_in), _qb_layout(w_qb), w_kvb,
                norms_all[:, 0, :64].reshape(1, C_RANK), norms_all[:, 0, 64:].reshape(1, C_RANK))

    scatters = {}

    def put_g(group, grads):
        if group == "ev_in":
            core = lax.axis_index("c").astype(jnp.int32).reshape(1)
            paired, token = _scatter2_pair("pair_ev_in", [grads("ev_in_dw_sibling", 1 - core)])
            mine = grads("ev_in_dw_own", core, deps=[token])
            pair = _split_done("paired_ev_in", paired, mine)[0]
            scatters[group], token = _scatter2_send("scatter_ev_in", [_pair_add("pair_add_ev_in", mine, pair)])
        else:
            scatters[group], token = _exchange_start("scatter_" + group, grads, False)
        return token

    def put_small(early):
        scatters["small"], token = _exchange_start("gather_small_early", early, True)
        return token

    grad_x, _, dnpre0 = _forward_backward(
        x[0], cos_t, sin_t, loss_target[0], norm_pre, norm_post, ev_lb_logits, ev_a_onorm, ev_b_ln_w,
        ev_b_ln_b, ev_b_ws, ev_b_bias, get_w, put_g, put_small, start_dep=tok)

    big_w = {"ev_w_in": ev_w_in, "ev_w_out": ev_w_out, "od_w_in": od_w_in, "od_w_qb": od_w_qb,
             "od_w_kvb": od_w_kvb, "od_w_out": od_w_out}
    big_m = {"ev_w_in": m_ev_w_in, "ev_w_out": m_ev_w_out, "od_w_in": m_od_w_in, "od_w_qb": m_od_w_qb,
             "od_w_kvb": m_od_w_kvb, "od_w_out": m_od_w_out}
    big_v = {"ev_w_in": v_ev_w_in, "ev_w_out": v_ev_w_out, "od_w_in": v_od_w_in, "od_w_qb": v_od_w_qb,
             "od_w_kvb": v_od_w_kvb, "od_w_out": v_od_w_out}
    big_out = {}
    after = grad_x
    for group, names in (("od_out", ["od_w_out"]), ("od_qkv", ["od_w_qb", "od_w_kvb"]), ("od_in", ["od_w_in"]),
                         ("ev_out", ["ev_w_out"])):
        parts = _exchange_wait("summed_" + group, scatters[group], after)
        for nm, p in zip(names, parts):
            w, m, v = big_w[nm][0], big_m[nm][0], big_v[nm][0]
            if nm == "od_w_in":
                res_t = _adamw("adamw_" + nm, [(p, N_DEV)], w.T, m.T, v.T, w.shape[1], 512)
                big_out[nm] = [r.T[None] for r in res_t]
            else:
                big_out[nm] = [r[None] for r in _adamw("adamw_" + nm, [(p, N_DEV)], w, m, v, w.shape[0] // 8)]
            after = big_out[nm][0]

    late_all = _exchange("gather_small_late", [dnpre0], gather=True, deps=[after])[0]
    early_all = _exchange_wait("arrived_small_early", scatters["small"], late_all)

    small_w = (norm_pre, norm_post, ev_lb_logits, ev_a_onorm, ev_b_ln_w, ev_b_ln_b, ev_b_ws, ev_b_bias)
    small_m = (m_norm_pre, m_norm_post, m_ev_lb_logits, m_ev_a_onorm, m_ev_b_ln_w, m_ev_b_ln_b, m_ev_b_ws, m_ev_b_bias)
    small_v = (v_norm_pre, v_norm_post, v_ev_lb_logits, v_ev_a_onorm, v_ev_b_ln_w, v_ev_b_ln_b, v_ev_b_ws, v_ev_b_bias)
    wmv = [tuple(a.reshape(s) for a in t) for s, t in zip(SMALL_PARAM_SHAPES, zip(small_w, small_m, small_v))]
    small_res, loss_row, g_norm_rows = _adamw_small(late_all, early_all, wmv)
    small_out = [[r.reshape(w.shape) for r in four] for four, w in zip(small_res, small_w)]
    loss = loss_row[0, 0]

    g_norms = jnp.concatenate([lax.dynamic_slice(g_norm_rows, (0, 64 * me), (1, 64)),
                               lax.dynamic_slice(g_norm_rows, (1, 64 * me), (1, 64))], axis=1)
    res_n = _adamw("adamw_norms", [(g_norms[None], 1)],
                   jnp.concatenate([od_q_norm, od_kv_norm], axis=1),
                   jnp.concatenate([m_od_q_norm, m_od_kv_norm], axis=1),
                   jnp.concatenate([v_od_q_norm, v_od_kv_norm], axis=1), 1)
    qn_out = [r[:, :64] for r in res_n]
    kvn_out = [r[:, 64:] for r in res_n]

    chip_sums, from_peers = _split_done("summed_ev_in", scatters["ev_in"], loss_row, all_bufs=True)
    own_chip = lax.dynamic_slice_in_dim(chip_sums, me >> 1, 1, 0)
    w = ev_w_in[0]
    big_out["ev_w_in"] = [r[None] for r in _adamw("adamw_ev_w_in", [(own_chip, 1), (from_peers, 3)], w, m_ev_w_in[0],
                                                  v_ev_w_in[0], w.shape[0] // 8)]

    order = ("norm_pre", "norm_post", "ev_w_in", "ev_lb_logits", "ev_a_onorm", "ev_b_ln_w", "ev_b_ln_b",
             "ev_b_ws", "ev_b_bias", "ev_w_out", "od_w_in", "od_q_norm", "od_w_qb", "od_kv_norm",
             "od_w_kvb", "od_w_out")
    small_names = ("norm_pre", "norm_post", "ev_lb_logits", "ev_a_onorm", "ev_b_ln_w", "ev_b_ln_b",
                   "ev_b_ws", "ev_b_bias")
    outs = [loss, grad_x[None]]
    for kind in range(4):
        for nm in order:
            if nm in big_out:
                outs.append(big_out[nm][kind])
            elif nm == "od_q_norm":
                outs.append(qn_out[kind])
            elif nm == "od_kv_norm":
                outs.append(kvn_out[kind])
            else:
                outs.append(small_out[small_names.index(nm)][kind])
    return tuple(outs)
```

```python
import functools

import jax
import jax.numpy as jnp
from jax import lax
from jax.experimental import pallas as pl
from jax.experimental.pallas import tpu as pltpu

F32 = jnp.float32
BF16 = jnp.bfloat16

N_DEV = 8
T = 2048
D = 2048
EPS = 1e-6
A_HEADS = 8
HD = 128
A_CHUNK = 64
A_SUB = 16
B_GROUPS = 8
B_CHUNK = 128
EVEN_IN = 7168
C_HEADS = 16
C_RANK = 512
C_NOPE = 128
C_ROPE = 64
C_QK = C_NOPE + C_ROPE
C_V = 128
ODD_IN = 3136
ODD_IN_PAD = 3200
QP = 256
ROPE_THETA = 10000.0
ATT_SCALE = C_QK ** -0.5

ADAM_LR = 0.001
ADAM_B1 = 0.9
ADAM_B2 = 0.999
ADAM_EPS = 1e-08
ADAM_WD = 0.01
ADAM_STEP = 10

VMEM_LIMIT_V7X = 56 * 1024 * 1024
MESH_ID = pl.DeviceIdType.MESH


def _params(n_grid):
    return pltpu.CompilerParams(dimension_semantics=("arbitrary",) * n_grid,
                                vmem_limit_bytes=VMEM_LIMIT_V7X)


def _dg(a, b, ca, cb):
    return lax.dot_general(a.astype(BF16), b.astype(BF16), (((ca,), (cb,)), ((), ())),
                           preferred_element_type=F32)


def _raw_nn(a, b):
    return _dg(a, b, 1, 0)


def _raw_nt(a, b):
    return _dg(a, b, 1, 1)


def _raw_tn(a, b):
    return _dg(a, b, 0, 0)


@jax.custom_vjp
def _dot_nn(a, b):
    return _raw_nn(a, b)


def _dot_nn_fwd(a, b):
    return _raw_nn(a, b), (a.astype(BF16), b.astype(BF16))


def _dot_nn_bwd(res, g):
    a, b = res
    return _raw_nt(g, b), _raw_tn(a, g)


_dot_nn.defvjp(_dot_nn_fwd, _dot_nn_bwd)


@jax.custom_vjp
def _dot_nt(a, b):
    return _raw_nt(a, b)


def _dot_nt_fwd(a, b):
    return _raw_nt(a, b), (a.astype(BF16), b.astype(BF16))


def _dot_nt_bwd(res, g):
    a, b = res
    return _raw_nn(g, b), _raw_tn(g, a)


_dot_nt.defvjp(_dot_nt_fwd, _dot_nt_bwd)


@jax.custom_vjp
def _dot_tn(a, b):
    return _raw_tn(a, b)


def _dot_tn_fwd(a, b):
    return _raw_tn(a, b), (a.astype(BF16), b.astype(BF16))


def _dot_tn_bwd(res, g):
    a, b = res
    return _raw_nt(b, g), _raw_nn(a, g)


_dot_tn.defvjp(_dot_tn_fwd, _dot_tn_bwd)


@jax.custom_vjp
def _sigmoid(x):
    e = jnp.exp(-jnp.abs(x))
    return jnp.where(x >= 0, 1.0 / (1.0 + e), e / (1.0 + e))


def _sigmoid_fwd(x):
    s = _sigmoid(x)
    return s, s


def _sigmoid_bwd(s, g):
    return (g * s * (1.0 - s),)


_sigmoid.defvjp(_sigmoid_fwd, _sigmoid_bwd)


def _silu(x):
    return x * _sigmoid(x)


def _rms(x, w):
    return x * lax.rsqrt(jnp.mean(x * x, axis=-1, keepdims=True) + EPS) * w


def _split3(x):
    hi = x.astype(BF16)
    r = x - hi.astype(F32)
    mid = r.astype(BF16)
    lo = (r - mid.astype(F32)).astype(BF16)
    return hi, mid, lo


def _mask_apply(mask_bf16, x, contract):
    out = None
    for piece in _split3(x):
        d = lax.dot_general(mask_bf16, piece, (((contract,), (0,)), ((), ())),
                            preferred_element_type=F32)
        out = d if out is None else out + d
    return out


def _chunk_tri(rows):
    r = lax.broadcasted_iota(jnp.int32, (rows, rows), 0)
    c = lax.broadcasted_iota(jnp.int32, (rows, rows), 1)
    return ((r >= c) & (r // A_CHUNK == c // A_CHUNK)).astype(BF16)


@jax.custom_vjp
def _chunk_cumsum(x):
    return _mask_apply(_chunk_tri(x.shape[0]), x, 1)


def _chunk_cumsum_fwd(x):
    return _chunk_cumsum(x), None


def _chunk_cumsum_bwd(_, g):
    return (_mask_apply(_chunk_tri(g.shape[0]), g, 0),)


_chunk_cumsum.defvjp(_chunk_cumsum_fwd, _chunk_cumsum_bwd)


def _hgrn2_rows(q, zf, v, ga, st, l0, l1, onorm):
    rows = q.shape[0]
    n_sub = A_CHUNK // A_SUB
    mx = jnp.maximum(l0, l1)
    e0 = jnp.exp(l0 - mx)
    e1 = jnp.exp(l1 - mx)
    lb = e0 / (e0 + e1)
    lf = jnp.log(lb + (1.0 - lb) * _sigmoid(zf))
    k = (1.0 - lb) * _sigmoid(-zf)
    b = _chunk_cumsum(lf)

    t_idx = lax.broadcasted_iota(jnp.int32, (A_CHUNK, n_sub * A_CHUNK), 0)
    c_idx = lax.broadcasted_iota(jnp.int32, (A_CHUNK, n_sub * A_CHUNK), 1)
    sel = (c_idx // A_CHUNK == t_idx // A_SUB) & (c_idx % A_CHUNK <= t_idx)
    key_row = lax.broadcasted_iota(jnp.int32, (A_CHUNK, HD), 0)

    outs = []
    for n in range(rows // A_CHUNK):
        lo = n * A_CHUNK
        qc, kc, vc = q[lo:lo + A_CHUNK], k[lo:lo + A_CHUNK], v[lo:lo + A_CHUNK]
        lfc, bc = lf[lo:lo + A_CHUNK], b[lo:lo + A_CHUNK]
        b_last = bc[A_CHUNK - 1:A_CHUNK]
        o_inter = _dot_nt(qc * jnp.exp(bc), st)
        kv_t = _dot_tn(vc, kc * jnp.exp(b_last - bc))
        st = st * jnp.exp(b_last) + kv_t
        g_rows, k_subs = [], []
        for i in range(n_sub):
            g_i = bc[i * A_SUB:i * A_SUB + 1] - lfc[i * A_SUB:i * A_SUB + 1]
            g_rows.append(jnp.broadcast_to(g_i, (A_SUB, HD)))
            expo = jnp.where(key_row < (i + 1) * A_SUB, g_i - bc, -jnp.inf)
            k_subs.append(kc * jnp.exp(expo))
        q_sub = qc * jnp.exp(bc - jnp.concatenate(g_rows, axis=0))
        scores = _dot_nt(q_sub, jnp.concatenate(k_subs, axis=0))
        scores = jnp.where(sel, scores, 0.0)
        o_intra = _dot_nn(scores, jnp.concatenate([vc] * n_sub, axis=0))
        outs.append(o_inter + o_intra)
    o = jnp.concatenate(outs, axis=0)
    return _rms(o, onorm) * _silu(ga), st


def _gmlp_rows(u, vb, gb, lnw, lnb, ws, bias):
    rows = u.shape[0]
    mu = jnp.mean(vb, axis=-1, keepdims=True)
    xc = vb - mu
    vg = xc * lax.rsqrt(jnp.mean(xc * xc, axis=-1, keepdims=True) + EPS) * lnw + lnb
    r = lax.broadcasted_iota(jnp.int32, (B_CHUNK, B_CHUNK), 0)
    c = lax.broadcasted_iota(jnp.int32, (B_CHUNK, B_CHUNK), 1)
    ws_causal = jnp.where(r >= c, ws, 0.0)
    svs = [_dot_nn(ws_causal, vg[n * B_CHUNK:(n + 1) * B_CHUNK]) + bias
           for n in range(rows // B_CHUNK)]
    return u * jnp.concatenate(svs, axis=0) * _silu(gb)


def _rope(x, cos_t, sin_t):
    return x * cos_t + pltpu.roll(x, 64, 1) * sin_t


def _rope_transpose(g, cos_t, sin_t):
    return g * cos_t + pltpu.roll(g * sin_t, 64, 1)


ANY_SPEC = pl.BlockSpec(memory_space=pl.ANY)


def _live(deps):
    return [d for d in deps if d is not None]


def _skip_deps(body, n_in, n_deps):
    def wrapped(*refs):
        return body(*refs[:n_in], *refs[n_in + n_deps:])
    return wrapped


def _pure_call(name, fn, grid, in_specs, out_specs, out_shape, args, n_acc=0, deps=()):
    deps = _live(deps)
    n_in, n_out, n_deps = len(in_specs), len(out_specs), len(deps)
    in_specs = list(in_specs) + [ANY_SPEC] * n_deps
    args = tuple(args) + tuple(deps)

    def body(*refs):
        res = fn(*[r[...] for r in refs[:n_in]])
        if not isinstance(res, (tuple, list)):
            res = (res,)
        outs = refs[n_in + n_deps:n_in + n_deps + n_out]
        for o, r in zip(outs[:n_out - n_acc], res[:n_out - n_acc]):
            o[...] = r.astype(o.dtype)
        if n_acc:
            first = functools.reduce(jnp.logical_and, [pl.program_id(i) == 0 for i in range(len(grid))])
            for o, r in zip(outs[n_out - n_acc:], res[n_out - n_acc:]):
                @pl.when(first)
                def _(o=o, r=r):
                    o[...] = r.astype(o.dtype)

                @pl.when(jnp.logical_not(first))
                def _(o=o, r=r):
                    o[...] += r.astype(o.dtype)

    return pl.pallas_call(body, name=name, grid=grid, in_specs=in_specs, out_specs=out_specs,
                          out_shape=out_shape, compiler_params=_params(len(grid)))(*args)


def _sds(shape, dtype):
    return jax.ShapeDtypeStruct(shape, dtype)


def _row_spec(tm, width, col=0):
    return pl.BlockSpec((tm, width), lambda i, col=col: (i, col))


def _full_spec(shape):
    nd = len(shape)
    return pl.BlockSpec(shape, lambda *_: (0,) * nd)


def _mm_nn(name, a, b, out_dtype, tm, tn, deps=()):
    deps = _live(deps)
    m, k = a.shape
    j, _, n = b.shape
    per = n // tn

    def body(a_ref, b_ref, o_ref):
        o_ref[...] = _raw_nn(a_ref[...], b_ref[...]).astype(o_ref.dtype)

    return pl.pallas_call(
        _skip_deps(body, 2, len(deps)), name=name, grid=(m // tm, j * per),
        in_specs=[pl.BlockSpec((tm, k), lambda i, c: (i, 0)),
                  pl.BlockSpec((None, k, tn), lambda i, c: (c // per, 0, c % per))] + [ANY_SPEC] * len(deps),
        out_specs=pl.BlockSpec((tm, tn), lambda i, c: (i, c)),
        out_shape=_sds((m, j * n), out_dtype), compiler_params=_params(2))(a, b, *deps)


def _mm_nn_some(name, a, b, slots, cols, n_cols, out_dtype, tm, prev=None):
    m, k = a.shape
    n = b.shape[2]
    n_sel = slots.shape[0]

    def body(slots_ref, cols_ref, a_ref, b_ref, *rest):
        del slots_ref, cols_ref
        rest[-1][...] = _raw_nn(a_ref[...], b_ref[...]).astype(rest[-1].dtype)

    grid_spec = pltpu.PrefetchScalarGridSpec(
        num_scalar_prefetch=2, grid=(m // tm, n_sel),
        in_specs=[pl.BlockSpec((tm, k), lambda i, s, sl, cl: (i, 0)),
                  pl.BlockSpec((None, k, n), lambda i, s, sl, cl: (sl[s], 0, 0))] + ([ANY_SPEC] if prev is not None else []),
        out_specs=pl.BlockSpec((tm, n), lambda i, s, sl, cl: (i, cl[s])))
    return pl.pallas_call(
        body, name=name, grid_spec=grid_spec, out_shape=_sds((m, n_cols * n), out_dtype),
        input_output_aliases={4: 0} if prev is not None else {},
        compiler_params=_params(2))(slots, cols, a, b, *([prev] if prev is not None else []))


def _mm_nt(name, a, b, out_dtype, tm, tn, deps=()):
    deps = _live(deps)
    m = a.shape[0]
    j, nn, n = b.shape

    def body(a_ref, b_ref, o_ref):
        b_all = b_ref[0] if j == 1 else jnp.concatenate([b_ref[s] for s in range(j)], axis=1)
        o_ref[...] = _raw_nt(a_ref[...], b_all).astype(o_ref.dtype)

    return pl.pallas_call(
        _skip_deps(body, 2, len(deps)), name=name, grid=(m // tm, nn // tn),
        in_specs=[pl.BlockSpec((tm, j * n), lambda i, c: (i, 0)),
                  pl.BlockSpec((j, tn, n), lambda i, c: (0, c, 0))] + [ANY_SPEC] * len(deps),
        out_specs=pl.BlockSpec((tm, tn), lambda i, c: (i, c)),
        out_shape=_sds((m, nn), out_dtype), compiler_params=_params(2))(a, b, *deps)


def _mm_tn(name, a, b, j, out_dtype, tm, tn, deps=()):
    deps = _live(deps)
    k, m = a.shape
    n = b.shape[1] // j
    per = n // tn

    def body(a_ref, b_ref, o_ref):
        o_ref[...] = _raw_tn(a_ref[...], b_ref[...]).astype(o_ref.dtype)

    return pl.pallas_call(
        _skip_deps(body, 2, len(deps)), name=name, grid=(m // tm, j * per),
        in_specs=[pl.BlockSpec((k, tm), lambda i, c: (0, i)),
                  pl.BlockSpec((k, tn), lambda i, c: (0, c))] + [ANY_SPEC] * len(deps),
        out_specs=pl.BlockSpec((None, tm, tn), lambda i, c: (c // per, i, c % per)),
        out_shape=_sds((j, m, n), out_dtype), compiler_params=_params(2))(a, b, *deps)


def _mm_tn_parity(name, a, b, j, parity, out_dtype, tm, deps=()):
    deps = _live(deps)
    k, m = a.shape
    n = b.shape[1] // j

    def body(par_ref, a_ref, b_ref, o_ref):
        del par_ref
        o_ref[...] = _raw_tn(a_ref[...], b_ref[...]).astype(o_ref.dtype)

    grid_spec = pltpu.PrefetchScalarGridSpec(
        num_scalar_prefetch=1, grid=(m // tm, j // 2),
        in_specs=[pl.BlockSpec((k, tm), lambda i, s, par: (0, i)),
                  pl.BlockSpec((k, n), lambda i, s, par: (0, 2 * s + par[0]))] + [ANY_SPEC] * len(deps),
        out_specs=pl.BlockSpec((None, tm, n), lambda i, s, par: (s, i, 0)))
    return pl.pallas_call(
        lambda par_ref, *refs: _skip_deps(functools.partial(body, par_ref), 2, len(deps))(*refs),
        name=name, grid_spec=grid_spec, out_shape=_sds((j // 2, m, n), out_dtype),
        compiler_params=_params(2))(parity, a, b, *deps)


TM = 256


def _pre_norm(name, x, w_row, deps=()):
    def fn(xv, w):
        return _rms(xv, w)
    return _pure_call(name, fn, (T // TM,), [_row_spec(TM, D), _full_spec((1, D))],
                      [_row_spec(TM, D)], [_sds((T, D), BF16)], (x, w_row), deps=deps)[0]


def _post_pre_norm(x, y, w_post, w_pre):
    def fn(xv, yv, wp, wn):
        x1 = xv + _rms(yv, wp)
        return x1, _rms(x1, wn)
    return _pure_call("post_pre_norm", fn, (T // TM,),
                      [_row_spec(TM, D), _row_spec(TM, D), _full_spec((1, D)), _full_spec((1, D))],
                      [_row_spec(TM, D), _row_spec(TM, D)],
                      [_sds((T, D), F32), _sds((T, D), BF16)], (x, y, w_post, w_pre))


def _post_pre_norm_bwd(y, x1, w_post, w_pre, dx1_in, dh1, deps=()):
    def fn(yv, x1v, wp, wn, dx1v, dh1v):
        _, vjp_pre = jax.vjp(_rms, x1v, wn)
        dx1_h, dwn = vjp_pre(dh1v)
        dx1 = dx1v + dx1_h
        _, vjp_post = jax.vjp(_rms, yv, wp)
        dy, dwp = vjp_post(dx1)
        return dx1, dy, dwp, dwn
    return _pure_call("post_pre_norm_bwd", fn, (T // TM,),
                      [_row_spec(TM, D), _row_spec(TM, D), _full_spec((1, D)), _full_spec((1, D)),
                       _row_spec(TM, D), _row_spec(TM, D)],
                      [_row_spec(TM, D), _row_spec(TM, D), _full_spec((1, D)), _full_spec((1, D))],
                      [_sds((T, D), F32), _sds((T, D), BF16), _sds((1, D), F32), _sds((1, D), F32)],
                      (y, x1, w_post, w_pre, dx1_in, dh1), n_acc=2, deps=deps)


def _final_loss(x1, y, w_post, target):
    def fn(x1v, yv, wp, tv):
        r, vjp = jax.vjp(_rms, yv, wp)
        err = x1v + r - tv
        part = 0.5 * jnp.sum(jnp.mean(err * err, axis=-1, keepdims=True), axis=0, keepdims=True)
        dx2 = err * (1.0 / D)
        dy, dwp = vjp(dx2)
        return dx2, dy, jnp.broadcast_to(part, (1, 128)), dwp
    return _pure_call("final_loss", fn, (T // TM,),
                      [_row_spec(TM, D), _row_spec(TM, D), _full_spec((1, D)), _row_spec(TM, D)],
                      [_row_spec(TM, D), _row_spec(TM, D), _full_spec((1, 128)), _full_spec((1, D))],
                      [_sds((T, D), F32), _sds((T, D), BF16), _sds((1, 128), F32), _sds((1, D), F32)],
                      (x1, y, w_post, target), n_acc=2)


def _pre_norm_bwd(x, w_row, dh, dx_res, deps=()):
    def fn(xv, w, dhv, dxv):
        _, vjp = jax.vjp(_rms, xv, w)
        dx, dw = vjp(dhv)
        return dxv + dx, dw
    return _pure_call("pre_norm_bwd", fn, (T // TM,),
                      [_row_spec(TM, D), _full_spec((1, D)), _row_spec(TM, D), _row_spec(TM, D)],
                      [_row_spec(TM, D), _full_spec((1, D))],
                      [_sds((T, D), F32), _sds((1, D), F32)], (x, w_row, dh, dx_res), n_acc=1, deps=deps)


RA = 256
RB = 512


HA = 4
A_GROUPS = A_HEADS // HA


def _head(ref, hh):
    return ref[:, hh * HD:(hh + 1) * HD]


def _hgrn2_fwd(z, l0, l1, onorm):
    nb = T // RA

    def body(q_ref, f_ref, v_ref, g_ref, l0_ref, l1_ref, on_ref, cat_ref, sst_ref, st_scr):
        @pl.when(pl.program_id(1) == 0)
        def _():
            st_scr[...] = jnp.zeros_like(st_scr)

        for hh in range(HA):
            st = st_scr[hh]
            sst_ref[hh] = st
            out, st_new = _hgrn2_rows(_head(q_ref, hh), _head(f_ref, hh), _head(v_ref, hh), _head(g_ref, hh), st,
                                      _head(l0_ref, hh), _head(l1_ref, hh), on_ref[...])
            cat_ref[:, hh * HD:(hh + 1) * HD] = out.astype(cat_ref.dtype)
            st_scr[hh] = st_new

    def cols(k):
        return pl.BlockSpec((RA, HA * HD), lambda g, r: (r, k * A_GROUPS + g))

    vec = pl.BlockSpec((1, HA * HD), lambda g, r: (0, g))
    return pl.pallas_call(
        body, name="hgrn2_fwd", grid=(A_GROUPS, nb),
        in_specs=[cols(0), cols(1), cols(2), cols(3), vec, vec, _full_spec((1, HD))],
        out_specs=[cols(0), pl.BlockSpec((HA, None, HD, HD), lambda g, r: (g, r, 0, 0))],
        out_shape=[_sds((T, 2 * A_HEADS * HD), BF16), _sds((A_HEADS, nb, HD, HD), F32)],
        scratch_shapes=[pltpu.VMEM((HA, HD, HD), F32)],
        compiler_params=_params(2))(z, z, z, z, l0, l1, onorm)


def _hgrn2_bwd(z, l0, l1, onorm, sst, dcat, deps=()):
    nb = T // RA
    deps = _live(deps)

    def body(q_ref, f_ref, v_ref, g_ref, l0_ref, l1_ref, on_ref, sst_ref, dcat_ref,
             dq_ref, df_ref, dv_ref, dg_ref, dl0_ref, dl1_ref, don_ref, ds_scr):
        g, r = pl.program_id(0), pl.program_id(1)

        @pl.when(r == 0)
        def _():
            ds_scr[...] = jnp.zeros_like(ds_scr)

        dl0s, dl1s, don = [], [], None
        for hh in range(HA):
            _, vjp = jax.vjp(_hgrn2_rows, _head(q_ref, hh), _head(f_ref, hh), _head(v_ref, hh), _head(g_ref, hh),
                             sst_ref[hh], _head(l0_ref, hh), _head(l1_ref, hh), on_ref[...])
            dq, dzf, dv, dga, dst, dl0, dl1, don_h = vjp((_head(dcat_ref, hh), ds_scr[hh]))
            for ref, val in ((dq_ref, dq), (df_ref, dzf), (dv_ref, dv), (dg_ref, dga)):
                ref[:, hh * HD:(hh + 1) * HD] = val.astype(ref.dtype)
            ds_scr[hh] = dst
            dl0s.append(dl0)
            dl1s.append(dl1)
            don = don_h if don is None else don + don_h
        dl0 = jnp.concatenate(dl0s, axis=1)
        dl1 = jnp.concatenate(dl1s, axis=1)

        @pl.when(r == 0)
        def _():
            dl0_ref[...] = dl0
            dl1_ref[...] = dl1

        @pl.when(r > 0)
        def _():
            dl0_ref[...] += dl0
            dl1_ref[...] += dl1

        first = jnp.logical_and(g == 0, r == 0)

        @pl.when(first)
        def _():
            don_ref[...] = don

        @pl.when(jnp.logical_not(first))
        def _():
            don_ref[...] += don

    def rev(k):
        return pl.BlockSpec((RA, HA * HD), lambda g, r: (nb - 1 - r, k * A_GROUPS + g))

    vec = pl.BlockSpec((1, HA * HD), lambda g, r: (0, g))
    grad = _sds((T, A_HEADS * HD), BF16)
    return pl.pallas_call(
        _skip_deps(body, 9, len(deps)), name="hgrn2_bwd", grid=(A_GROUPS, nb),
        in_specs=[rev(0), rev(1), rev(2), rev(3), vec, vec, _full_spec((1, HD)),
                  pl.BlockSpec((HA, None, HD, HD), lambda g, r: (g, nb - 1 - r, 0, 0)),
                  rev(0)] + [ANY_SPEC] * len(deps),
        out_specs=[rev(0)] * 4 + [vec, vec, _full_spec((1, HD))],
        out_shape=[grad] * 4 + [_sds((1, A_HEADS * HD), F32)] * 2 + [_sds((1, HD), F32)],
        scratch_shapes=[pltpu.VMEM((HA, HD, HD), F32)],
        compiler_params=_params(2))(z, z, z, z, l0, l1, onorm, sst, dcat, *deps)


GB = 4
B_STEPS = B_GROUPS // GB


def _gmlp_specs():
    vec = pl.BlockSpec((1, GB * HD), lambda s, r: (0, s))
    ws = pl.BlockSpec((GB, B_CHUNK, B_CHUNK), lambda s, r: (s, 0, 0))
    bias = pl.BlockSpec((GB, B_CHUNK, 1), lambda s, r: (s, 0, 0))

    def cols(k):
        return pl.BlockSpec((RB, GB * HD), lambda s, r: (r, k * B_STEPS + s))
    return vec, ws, bias, cols


def _gmlp_fwd(z, cat, lnw, lnb, ws, bias):
    vec, ws_spec, bias_spec, cols = _gmlp_specs()

    def body(u_ref, v_ref, g_ref, lnw_ref, lnb_ref, ws_ref, bias_ref, cat_in_ref, cat_ref):
        del cat_in_ref
        for gg in range(GB):
            out = _gmlp_rows(_head(u_ref, gg), _head(v_ref, gg), _head(g_ref, gg), _head(lnw_ref, gg),
                             _head(lnb_ref, gg), ws_ref[gg], bias_ref[gg])
            cat_ref[:, gg * HD:(gg + 1) * HD] = out.astype(cat_ref.dtype)

    return pl.pallas_call(
        body, name="gmlp_fwd", grid=(B_STEPS, T // RB),
        in_specs=[cols(4), cols(5), cols(6), vec, vec, ws_spec, bias_spec, pl.BlockSpec(memory_space=pl.ANY)],
        out_specs=cols(1),
        out_shape=_sds(cat.shape, cat.dtype), input_output_aliases={7: 0},
        compiler_params=_params(2))(z, z, z, lnw, lnb, ws, bias, cat)


def _gmlp_bwd(z, lnw, lnb, ws, bias, dcat):
    vec, ws_spec, bias_spec, cols = _gmlp_specs()

    def body(u_ref, v_ref, g_ref, lnw_ref, lnb_ref, ws_ref, bias_ref, dcat_ref,
             du_ref, dv_ref, dg_ref, dlnw_ref, dlnb_ref, dws_ref, dbias_ref):
        first = pl.program_id(1) == 0
        for gg in range(GB):
            _, vjp = jax.vjp(_gmlp_rows, _head(u_ref, gg), _head(v_ref, gg), _head(g_ref, gg), _head(lnw_ref, gg),
                             _head(lnb_ref, gg), ws_ref[gg], bias_ref[gg])
            du, dv, dg, dlnw, dlnb, dws, dbias = vjp(_head(dcat_ref, gg))
            lanes = slice(gg * HD, (gg + 1) * HD)
            for ref, val in ((du_ref, du), (dv_ref, dv), (dg_ref, dg)):
                ref[:, lanes] = val.astype(ref.dtype)
            sums = ((dlnw_ref, (slice(None), lanes), dlnw), (dlnb_ref, (slice(None), lanes), dlnb),
                    (dws_ref, gg, dws), (dbias_ref, gg, dbias))
            for ref, idx, val in sums:
                @pl.when(first)
                def _(ref=ref, idx=idx, val=val):
                    ref[idx] = val

                @pl.when(jnp.logical_not(first))
                def _(ref=ref, idx=idx, val=val):
                    ref[idx] += val

    grad = _sds((T, B_GROUPS * HD), BF16)
    return pl.pallas_call(
        body, name="gmlp_bwd", grid=(B_STEPS, T // RB),
        in_specs=[cols(4), cols(5), cols(6), vec, vec, ws_spec, bias_spec, cols(1)],
        out_specs=[cols(0)] * 3 + [vec, vec, ws_spec, bias_spec],
        out_shape=[grad] * 3 + [_sds((1, B_GROUPS * HD), F32)] * 2
        + [_sds((B_GROUPS, B_CHUNK, B_CHUNK), F32), _sds((B_GROUPS, B_CHUNK, 1), F32)],
        compiler_params=_params(2))(z, z, z, lnw, lnb, ws, bias, dcat)


def _mla_pre(z1, qn, kvn, cos_t, sin_t):
    def fn(cq, ckv, kpe, cs, sn, wq, wkv):
        return _rms(cq, wq), _rms(ckv, wkv), _rope(kpe, cs, sn)
    return _pure_call("mla_pre", fn, (T // TM,),
                      [_row_spec(TM, C_RANK, 4), _row_spec(TM, C_RANK, 5), _row_spec(TM, HD, 24),
                       _row_spec(TM, HD), _row_spec(TM, HD),
                       _full_spec((1, C_RANK)), _full_spec((1, C_RANK))],
                      [_row_spec(TM, C_RANK), _row_spec(TM, C_RANK), _row_spec(TM, HD)],
                      [_sds((T, C_RANK), BF16), _sds((T, C_RANK), BF16), _sds((T, HD), BF16)],
                      (z1, z1, z1, cos_t, sin_t, qn, kvn))


def _mla_pre_bwd(z1, qn, kvn, cos_t, sin_t, dcqn, dckvn, dkp, deps=()):
    def fn(cq, ckv, cs, sn, wq, wkv, g_q, g_kv, g_kp):
        _, vjp_q = jax.vjp(_rms, cq, wq)
        dcq, dwq = vjp_q(g_q)
        _, vjp_kv = jax.vjp(_rms, ckv, wkv)
        dckv, dwkv = vjp_kv(g_kv)
        return dcq, dckv, _rope_transpose(g_kp, cs, sn), dwq, dwkv
    return _pure_call("mla_pre_bwd", fn, (T // TM,),
                      [_row_spec(TM, C_RANK, 4), _row_spec(TM, C_RANK, 5),
                       _row_spec(TM, HD), _row_spec(TM, HD),
                       _full_spec((1, C_RANK)), _full_spec((1, C_RANK)),
                       _row_spec(TM, C_RANK), _row_spec(TM, C_RANK), _row_spec(TM, HD)],
                      [_row_spec(TM, C_RANK), _row_spec(TM, C_RANK), _row_spec(TM, HD),
                       _full_spec((1, C_RANK)), _full_spec((1, C_RANK))],
                      [_sds((T, C_RANK), BF16), _sds((T, C_RANK), BF16), _sds((T, HD), BF16),
                       _sds((1, C_RANK), F32), _sds((1, C_RANK), F32)],
                      (z1, z1, cos_t, sin_t, qn, kvn, dcqn, dckvn, dkp), n_acc=2, deps=deps)


TQ = 256
HP = 2
KVW = C_NOPE + C_V


def _att_keys(kv_ref, kp_ref, k_scr):
    @pl.when(pl.program_id(1) == 0)
    def _():
        for hh in range(HP):
            k_scr[hh, :, 0:C_NOPE] = kv_ref[:, hh * KVW:hh * KVW + C_NOPE]
            k_scr[hh, :, C_NOPE:QP] = kp_ref[...]


def _att_scores(q, cos_ref, sin_ref, k_scr, hh, n):
    keys = (n + 1) * TQ
    qr = jnp.concatenate([q[:, :C_NOPE], _rope(q[:, C_NOPE:], cos_ref[...], sin_ref[...])], axis=1).astype(BF16)
    return qr, _raw_nt(qr, k_scr[hh, 0:keys, :]) * ATT_SCALE


def _causal(x, n, fill):
    row = lax.broadcasted_iota(jnp.int32, (TQ, TQ), 0)
    col = lax.broadcasted_iota(jnp.int32, (TQ, TQ), 1)
    diag = jnp.where(col <= row, x[:, n * TQ:], fill)
    return diag if n == 0 else jnp.concatenate([x[:, :n * TQ], diag], axis=1)


def _per_query_block(fn):
    for n in range(T // TQ):
        pl.when(pl.program_id(1) == n)(functools.partial(fn, n))


def _att_in_specs():
    return [pl.BlockSpec((TQ, HP * QP), lambda g, i: (i, g)),
            pl.BlockSpec((TQ, HD), lambda g, i: (i, 0)),
            pl.BlockSpec((TQ, HD), lambda g, i: (i, 0)),
            pl.BlockSpec((T, HP * KVW), lambda g, i: (0, g)),
            pl.BlockSpec((T, HD), lambda g, i: (0, 0))]


def _attention_fwd(q, cos_t, sin_t, kv, kp, z1):
    def body(q_ref, cos_ref, sin_ref, kv_ref, kp_ref, gate_ref, o_ref, lse_ref, og_ref, k_scr):
        _att_keys(kv_ref, kp_ref, k_scr)

        def block(n):
            keys = (n + 1) * TQ
            for hh in range(HP):
                _, s = _att_scores(q_ref[:, hh * QP:(hh + 1) * QP], cos_ref, sin_ref, k_scr, hh, n)
                s = _causal(s, n, jnp.finfo(F32).min)
                m = jnp.max(s, axis=-1, keepdims=True)
                p = jnp.exp(s - m)
                l = jnp.sum(p, axis=-1, keepdims=True)
                v = kv_ref[0:keys, hh * KVW + C_NOPE:(hh + 1) * KVW]
                o = _raw_nn(p, v) / l
                lanes = slice(hh * C_V, (hh + 1) * C_V)
                o_ref[:, lanes] = o
                og_ref[:, lanes] = (o * _silu(gate_ref[:, lanes])).astype(og_ref.dtype)
                lse_ref[hh] = m + jnp.log(l)

        _per_query_block(block)

    heads = pl.BlockSpec((TQ, HP * C_V), lambda g, i: (i, g))
    return pl.pallas_call(
        body, name="attention_fwd", grid=(C_HEADS // HP, T // TQ), in_specs=_att_in_specs() + [heads],
        out_specs=[heads, pl.BlockSpec((HP, TQ, 1), lambda g, i: (g, i, 0)), heads],
        out_shape=[_sds((T, C_HEADS * C_V), F32), _sds((C_HEADS, T, 1), F32), _sds((T, C_HEADS * C_V), BF16)],
        scratch_shapes=[pltpu.VMEM((HP, T, QP), BF16)],
        compiler_params=_params(2))(q, cos_t, sin_t, kv, kp, z1)


def _attention_bwd(q, cos_t, sin_t, kv, kp, o, lse, dog, z1):
    nq = T // TQ

    def body(q_ref, cos_ref, sin_ref, kv_ref, kp_ref, o_ref, lse_ref, dog_ref, gate_ref,
             dq_ref, dkv_ref, dkp_ref, dgate_ref, k_scr, dk_scr, dv_scr):
        g, i = pl.program_id(0), pl.program_id(1)
        _att_keys(kv_ref, kp_ref, k_scr)

        @pl.when(i == 0)
        def _():
            dv_scr[...] = jnp.zeros_like(dv_scr)
            dk_scr[...] = jnp.zeros_like(dk_scr)

        def block(n):
            keys = (n + 1) * TQ
            for hh in range(HP):
                qr, s = _att_scores(q_ref[:, hh * QP:(hh + 1) * QP], cos_ref, sin_ref, k_scr, hh, n)
                p = _causal(jnp.exp(s - lse_ref[hh]), n, 0.0)
                lanes = slice(hh * C_V, (hh + 1) * C_V)
                ov, gate, dogv = o_ref[:, lanes], gate_ref[:, lanes], dog_ref[:, lanes]
                sig = _sigmoid(gate)
                silu = gate * sig
                dov = dogv * silu
                dgate_ref[:, lanes] = (dogv * ov * (sig + silu * (1.0 - sig))).astype(dgate_ref.dtype)
                delta = jnp.sum(dov * ov, axis=-1, keepdims=True)
                dp = _raw_nt(dov, kv_ref[0:keys, hh * KVW + C_NOPE:(hh + 1) * KVW])
                ds = p * (dp - delta) * ATT_SCALE
                dq = _raw_nn(ds, k_scr[hh, 0:keys, :])
                dq_ref[:, hh * QP:(hh + 1) * QP] = jnp.concatenate(
                    [dq[:, :C_NOPE], _rope_transpose(dq[:, C_NOPE:], cos_ref[...], sin_ref[...])],
                    axis=1).astype(dq_ref.dtype)
                dv_scr[hh, 0:keys, :] += _raw_tn(p, dov)
                dk_scr[hh, 0:keys, :] += _raw_tn(ds, qr)

        _per_query_block(block)

        @pl.when(i == nq - 1)
        def _():
            for hh in range(HP):
                dkv_ref[:, hh * KVW:(hh + 1) * KVW] = jnp.concatenate(
                    [dk_scr[hh, :, 0:C_NOPE], dv_scr[hh]], axis=1).astype(dkv_ref.dtype)

        @pl.when(jnp.logical_and(i == nq - 1, g == 0))
        def _():
            dkp_ref[...] = dk_scr[0, :, C_NOPE:QP]

        @pl.when(jnp.logical_and(i == nq - 1, g > 0))
        def _():
            dkp_ref[...] += dk_scr[0, :, C_NOPE:QP]

        @pl.when(i == nq - 1)
        def _():
            for hh in range(1, HP):
                dkp_ref[...] += dk_scr[hh, :, C_NOPE:QP]

    heads = pl.BlockSpec((TQ, HP * C_V), lambda g, i: (i, g))
    return pl.pallas_call(
        body, name="attention_bwd", grid=(C_HEADS // HP, nq),
        in_specs=_att_in_specs() + [heads, pl.BlockSpec((HP, TQ, 1), lambda g, i: (g, i, 0)), heads, heads],
        out_specs=[pl.BlockSpec((TQ, HP * QP), lambda g, i: (i, g)),
                   pl.BlockSpec((T, HP * KVW), lambda g, i: (0, g)),
                   _full_spec((T, HD)), heads],
        out_shape=[_sds((T, C_HEADS * QP), BF16), _sds((T, C_HEADS * KVW), BF16), _sds((T, HD), F32),
                   _sds((T, C_HEADS * C_V), BF16)],
        scratch_shapes=[pltpu.VMEM((HP, T, QP), BF16), pltpu.VMEM((HP, T, QP), F32), pltpu.VMEM((HP, T, C_V), F32)],
        compiler_params=_params(2))(q, cos_t, sin_t, kv, kp, o, lse, dog, z1)


def _adamw_math(w, g, m, v):
    m = ADAM_B1 * m + (1.0 - ADAM_B1) * g
    v = ADAM_B2 * v + (1.0 - ADAM_B2) * (g * g)
    m_hat = m / (1.0 - ADAM_B1 ** ADAM_STEP)
    v_hat = v / (1.0 - ADAM_B2 ** ADAM_STEP)
    delta = -ADAM_LR * (m_hat / (jnp.sqrt(v_hat) + ADAM_EPS) + ADAM_WD * w)
    return delta, m, v


def _adamw(name, parts, w, m, v, tr, tc=None):
    rows, cols = w.shape

    def fn(*vals):
        pvs, (wv, mv, vv) = vals[:len(parts)], vals[len(parts):]
        g = None
        for pv in pvs:
            for d in range(pv.shape[0]):
                term = pv[d].astype(F32)
                g = term if g is None else g + term
        return (g,) + _adamw_math(wv, g, mv, vv)

    tc = cols if tc is None else tc
    blk = pl.BlockSpec((tr, tc), lambda i, j: (i, j))
    part_specs = [pl.BlockSpec((n, tr, tc), lambda i, j: (0, i, j)) for _, n in parts]
    return _pure_call(name, fn, (rows // tr, cols // tc), part_specs + [blk, blk, blk],
                      [blk] * 4, [_sds((rows, cols), F32)] * 4, tuple(p for p, _ in parts) + (w, m, v))


SMALL_PARAM_SHAPES = ((2, D), (2, D), (2, A_HEADS * HD), (1, HD), (1, B_GROUPS * HD), (1, B_GROUPS * HD),
                      (B_GROUPS, B_CHUNK, B_CHUNK), (B_GROUPS, B_CHUNK))
SMALL_PIECES = ((0, 0, 0, 0), (0, 1, 1, 0), (1, 0, 1, 1), (1, 1, 1, 2), (2, 0, 2, 0), (2, 1, 2, 1),
                (3, 0, 3, 8), (4, 0, 2, 2), (5, 0, 2, 3))


def _small_rows(dnpre1, dnpost0, dnpost1, dl0, dl1, donorm, dlnw, dlnb, dws, dbias, dqn, dkvn, loss_part):
    return [jnp.concatenate([dnpre1, dnpost0, dnpost1], axis=0),
            jnp.concatenate([dl0, dl1, dlnw, dlnb], axis=0),
            jnp.concatenate([dbias.reshape(B_GROUPS, B_CHUNK), donorm, loss_part], axis=0),
            dws,
            jnp.concatenate([dqn, dkvn], axis=0)]


def _adamw_small(late_all, early_all, wmv):
    n_in = 6 + 3 * len(wmv)

    def body(*refs):
        gathered, params, outs = refs[:6], refs[6:n_in], refs[n_in:]

        def total(ref):
            s = ref[0]
            for d in range(1, N_DEV):
                s = s + ref[d]
            return s

        g_late, g2048, g1024, g128, g_ws, g512 = [total(r) for r in gathered]
        arrays = (g_late, g2048, g1024, g128)

        def update(p, rows, g):
            w_ref, m_ref, v_ref = params[3 * p:3 * p + 3]
            delta, m, v = _adamw_math(w_ref[rows], g, m_ref[rows], v_ref[rows])
            for out, val in zip(outs[4 * p:4 * p + 4], (g, delta, m, v)):
                out[rows] = val

        for p, row, arr, arr_row in SMALL_PIECES:
            update(p, pl.ds(row, 1), arrays[arr][arr_row:arr_row + 1])
        update(6, slice(None), g_ws)
        update(7, slice(None), g128[0:B_GROUPS])
        outs[32][...] = g128[B_GROUPS + 1:B_GROUPS + 2]
        outs[33][...] = g512

    vmem = pl.BlockSpec(memory_space=pltpu.VMEM)
    flat = [a for t in wmv for a in t]
    out_shape = [_sds(s, F32) for s in SMALL_PARAM_SHAPES for _ in range(4)] + [_sds((1, 128), F32), _sds((2, C_RANK), F32)]
    res = pl.pallas_call(body, name="adamw_small", in_specs=[vmem] * n_in, out_specs=[vmem] * len(out_shape),
                         out_shape=out_shape,
                         compiler_params=pltpu.CompilerParams(vmem_limit_bytes=VMEM_LIMIT_V7X))(late_all, *early_all, *flat)
    return [res[4 * p:4 * p + 4] for p in range(8)], res[32], res[33]


def _exchange(name, arrs, gather, deps=()):
    n = len(arrs)
    deps = _live(deps)

    def body(*refs):
        ins, outs = refs[:n], refs[n + len(deps):2 * n + len(deps)]
        send_sems, recv_sems, local_sems = refs[2 * n + len(deps):]
        x, y, c = lax.axis_index("x"), lax.axis_index("y"), lax.axis_index("c")
        me = 4 * x + 2 * y + c

        def peer(k):
            return (x ^ (k >> 2), y ^ ((k >> 1) & 1), c ^ (k & 1))

        def copy(a, k):
            src = ins[a] if gather else ins[a].at[me ^ k]
            return pltpu.make_async_remote_copy(
                src_ref=src, dst_ref=outs[a].at[me], send_sem=send_sems.at[a, k - 1],
                recv_sem=recv_sems.at[a, k - 1], device_id=peer(k), device_id_type=MESH_ID)

        def arrival(a, k):
            src = ins[a] if gather else ins[a].at[me]
            return pltpu.make_async_remote_copy(
                src_ref=src, dst_ref=outs[a].at[me ^ k], send_sem=send_sems.at[a, k - 1],
                recv_sem=recv_sems.at[a, k - 1], device_id=peer(k), device_id_type=MESH_ID)

        own = [pltpu.make_async_copy(ins[a] if gather else ins[a].at[me], outs[a].at[me], local_sems.at[a])
               for a in range(n)]
        for cp in own:
            cp.start()
        for k in range(1, N_DEV):
            for a in range(n):
                copy(a, k).start()
        for k in range(1, N_DEV):
            for a in range(n):
                arrival(a, k).wait_recv()
        for k in range(1, N_DEV):
            for a in range(n):
                copy(a, k).wait_send()
        for cp in own:
            cp.wait()

    any_spec = pl.BlockSpec(memory_space=pl.ANY)
    out_shape = [_sds((N_DEV,) + a.shape if gather else a.shape, a.dtype) for a in arrs]
    return pl.pallas_call(
        body, name=name, in_specs=[any_spec] * (n + len(deps)), out_specs=[any_spec] * n, out_shape=out_shape,
        scratch_shapes=[pltpu.SemaphoreType.DMA((n, N_DEV - 1)), pltpu.SemaphoreType.DMA((n, N_DEV - 1)),
                        pltpu.SemaphoreType.DMA((n,))],
        compiler_params=pltpu.CompilerParams(has_side_effects=True))(*arrs, *deps)


HBM_SPEC = pl.BlockSpec(memory_space=pltpu.HBM)
SEM_SPEC = pl.BlockSpec(memory_space=pltpu.SEMAPHORE)
DATAFLOW = pltpu.SideEffectType.DATAFLOW_SIDE_EFFECTING


def _my_index():
    return 4 * lax.axis_index("x") + 2 * lax.axis_index("y") + lax.axis_index("c")


def _plan_copies(plan, refs, send_sems, recv_sems):
    x, y, c = lax.axis_index("x"), lax.axis_index("y"), lax.axis_index("c")
    copies = []
    for place, entry in enumerate(plan(refs, 4 * x + 2 * y + c)):
        src, dst, k = entry[:3]
        i = entry[3] if len(entry) > 3 else place
        copies.append(pltpu.make_async_remote_copy(
            src_ref=src, dst_ref=dst, send_sem=send_sems.at[i], recv_sem=recv_sems.at[i],
            device_id=(x ^ (k >> 2), y ^ ((k >> 1) & 1), c ^ (k & 1)), device_id_type=MESH_ID))
    return copies


def _split_call(name, bufs, waits=None, starts=None, deps=()):
    n = len(bufs)
    deps = _live(deps)
    n_wait = 2 if waits else 0

    def body(*refs):
        zones = refs[:n]
        if waits:
            for cp in _plan_copies(waits[2], zones, refs[n], refs[n + 1]):
                cp.wait_send()
                cp.wait_recv()
        if starts:
            first_out = n + n_wait + len(deps)
            for cp in _plan_copies(starts[0], zones, refs[first_out], refs[first_out + 1]):
                cp.start()
            refs[-1][...] = jnp.zeros_like(refs[-1])

    out_specs, out_shape = [], []
    if starts:
        sems = pltpu.SemaphoreType.DMA((starts[1],))
        out_specs, out_shape = [SEM_SPEC, SEM_SPEC], [sems, sems]
    out_specs += [HBM_SPEC] * n
    out_shape += [pltpu.HBM(b.shape, b.dtype) for b in bufs]
    if starts:
        out_specs.append(pl.BlockSpec(memory_space=pltpu.VMEM))
        out_shape.append(_sds((8, 128), F32))
    first_buf = 2 if starts else 0
    res = pl.pallas_call(
        body, name=name,
        in_specs=[HBM_SPEC] * n + [SEM_SPEC] * n_wait + [ANY_SPEC] * len(deps),
        out_specs=out_specs, out_shape=out_shape,
        input_output_aliases={i: first_buf + i for i in range(n)},
        compiler_params=pltpu.CompilerParams(has_side_effects=DATAFLOW),
    )(*[pltpu.with_memory_space_constraint(b, pltpu.HBM) for b in bufs], *(waits[:2] if waits else ()), *deps)
    out_bufs = list(res[first_buf:first_buf + n])
    return out_bufs, ((res[0], res[1]) if starts else None), (res[-1] if starts else None)


def _direct_plan(n, gather):
    def plan(refs, me):
        return [(refs[a] if gather else refs[a].at[me ^ k], refs[n + a].at[me], k)
                for k in range(1, N_DEV) for a in range(n)]
    return plan


def _own_slot_filled(a, gather):
    me = _my_index()
    if gather:
        return lax.dynamic_update_slice_in_dim(lax.empty((N_DEV,) + a.shape, a.dtype), a[None], me, 0)
    return lax.dynamic_update_slice_in_dim(lax.empty(a.shape, a.dtype), lax.dynamic_slice_in_dim(a, me, 1, 0), me, 0)


def _exchange_start(name, arrs, gather, deps=()):
    n = len(arrs)
    lands = [_own_slot_filled(a, gather) for a in arrs]
    plan = _direct_plan(n, gather)
    bufs, sems, token = _split_call(name, list(arrs) + lands, starts=(plan, n * (N_DEV - 1)), deps=deps)
    return (n, plan, sems, bufs, None), token


def _exchange_wait(name, handle, after):
    return _split_done(name, handle, after)


ICI_PEERS = (2, 4, 6)
SIBLING = 1


def _gather2_send(name, arrs, deps=()):
    n = len(arrs)
    lands = [_own_slot_filled(a, True) for a in arrs]

    def plan(refs, me_):
        return [(refs[a], refs[n + a].at[me_], k) for k in (SIBLING,) + ICI_PEERS for a in range(n)]

    bufs, sems, token = _split_call(name, list(arrs) + lands, starts=(plan, 4 * n), deps=deps)
    return (n, plan, sems, bufs, None), token


def _gather2_relay(name, handle, after):
    n, plan, sems, bufs, _ = handle
    after = after if isinstance(after, (list, tuple)) else [after]

    def relay(refs, me_):
        return [(refs[n + a].at[me_ ^ k], refs[n + a].at[me_ ^ k], SIBLING) for k in ICI_PEERS for a in range(n)]

    bufs, sems2, token = _split_call(name, bufs, waits=(sems[0], sems[1], plan), starts=(relay, 3 * n), deps=after)
    return (n, relay, sems2, bufs, None), token


def _first_gather_send(name, shard):
    near = _own_slot_filled(shard, True)
    far = lax.empty((2,) + shard.shape, shard.dtype)

    def plan(refs, me):
        src, near_z, far_z = refs
        return [(src, near_z.at[me], SIBLING), (src, near_z.at[me], 2), (src, near_z.at[me], 4), (src, far_z.at[0], 6)]

    return _split_call(name, [shard, near, far], starts=(plan, 4))


def _first_gather_near(names, src, near, sems, after):
    def arrived(refs, me):
        return [(refs[0], refs[1].at[me], k, i) for i, k in enumerate((SIBLING, 2, 4))]

    def relay(refs, me):
        return [(refs[1].at[me ^ k], refs[1].at[me ^ k], SIBLING) for k in (2, 4)]

    bufs, sems2, token = _split_call(names[0], [src, near], waits=(sems[0], sems[1], arrived), starts=(relay, 2),
                                     deps=after)
    bufs, _, _ = _split_call(names[1], bufs, waits=(sems2[0], sems2[1], relay), deps=[token])
    return bufs[1]


def _first_gather_far(names, far, sems, after):
    def arrived(refs, me):
        del me
        return [(refs[0].at[0], refs[0].at[0], 6, 3)]

    def relay(refs, me):
        del me
        return [(refs[0].at[0], refs[0].at[1], SIBLING)]

    bufs, sems2, token = _split_call(names[0], [far], waits=(sems[0], sems[1], arrived), starts=(relay, 1), deps=[after])
    bufs, _, _ = _split_call(names[1], bufs, waits=(sems2[0], sems2[1], relay), deps=[token])
    return bufs[0]


def _split_done(name, handle, after, all_bufs=False):
    n, plan, sems, bufs, _ = handle
    bufs, _, _ = _split_call(name, bufs, waits=(sems[0], sems[1], plan), deps=[after])
    return bufs if all_bufs else bufs[n:]


def _scatter2_pair(name, for_sibling, deps=()):
    n = len(for_sibling)
    pairs = [lax.empty(s.shape, s.dtype) for s in for_sibling]

    def plan(refs, me):
        del me
        return [(refs[a].at[s], refs[n + a].at[s], SIBLING) for s in range(4) for a in range(n)]

    bufs, sems, token = _split_call(name, list(for_sibling) + pairs, starts=(plan, 4 * n), deps=deps)
    return (n, plan, sems, bufs, None), token


def _pair_add(name, mine, pair):
    _, rows, cols = mine.shape
    tr = rows // 2

    def fn(a, b):
        return a.astype(F32) + b.astype(F32)

    blk = pl.BlockSpec((None, tr, cols), lambda s, i: (s, i, 0))
    return _pure_call(name, fn, (4, rows // tr), [blk, blk], [blk], [_sds(mine.shape, mine.dtype)], (mine, pair))[0]


def _scatter2_send(name, chip_sums, deps=()):
    n = len(chip_sums)
    finals = [lax.empty((3,) + c.shape[1:], c.dtype) for c in chip_sums]

    def plan(refs, me):
        return [(refs[a].at[(me >> 1) ^ j], refs[n + a].at[j - 1], 2 * j) for j in range(1, 4) for a in range(n)]

    bufs, sems, token = _split_call(name, list(chip_sums) + finals, starts=(plan, 3 * n), deps=deps)
    return (n, plan, sems, bufs, None), token


def _pad_rope(p):
    z = jnp.zeros(p.shape[:-1] + (32,), p.dtype)
    return jnp.concatenate([p[..., :32], z, p[..., 32:], z], axis=-1)


def _unpad_rope(p):
    return jnp.concatenate([p[..., :32], p[..., 64:96]], axis=-1)


def _odd_in_layout(wt):
    wt = wt.reshape(ODD_IN, D)
    cq, ckv, kpe, gate = wt[:512], wt[512:1024], wt[1024:1088], wt[1088:]
    z = jnp.zeros((32, D), wt.dtype)
    return jnp.concatenate([gate, cq, ckv, kpe[:32], z, kpe[32:], z], axis=0)


def _odd_in_unlayout(dwt):
    gate, cq, ckv, kpe = dwt[:2048], dwt[2048:2560], dwt[2560:3072], dwt[3072:]
    wt = jnp.concatenate([cq, ckv, kpe[:32], kpe[64:96], gate], axis=0)
    return wt.reshape(N_DEV, ODD_IN // N_DEV, D)


def _qb_layout(w):
    w = w.transpose(1, 0, 2).reshape(C_RANK, C_HEADS, C_QK)
    w = jnp.concatenate([w[..., :C_NOPE], _pad_rope(w[..., C_NOPE:])], axis=-1)
    return w.reshape(C_RANK, C_HEADS * QP)


def _qb_unlayout(dw):
    dw = dw.reshape(C_RANK, C_HEADS, QP)
    dw = jnp.concatenate([dw[..., :C_NOPE], _unpad_rope(dw[..., C_NOPE:])], axis=-1)
    return dw.reshape(C_RANK, N_DEV, C_HEADS * C_QK // N_DEV).transpose(1, 0, 2)


def _rope_tables(positions):
    inv_freq = ROPE_THETA ** (-jnp.arange(0, C_ROPE, 2, dtype=F32) / C_ROPE)
    ang = positions.astype(F32)[0][:, None] * inv_freq
    cos, sin = jnp.cos(ang), jnp.sin(ang)
    z = jnp.zeros_like(cos)
    return jnp.concatenate([cos, z, cos, z], axis=1), jnp.concatenate([-sin, z, sin, z], axis=1)


def _forward_backward(x, cos_t, sin_t, target, norm_pre, norm_post, lb_logits, a_onorm, ln_w, ln_b,
                      b_ws, b_bias, get_w, put_g, put_small=None, start_dep=None):
    npre0, npre1 = norm_pre[0:1], norm_pre[1:2]
    npost0, npost1 = norm_post[0:1], norm_post[1:2]
    l0, l1 = lb_logits[0:1], lb_logits[1:2]
    bias_col = b_bias.reshape(B_GROUPS, B_CHUNK, 1)
    ws = b_ws.reshape(B_GROUPS, B_CHUNK, B_CHUNK)

    h0 = _pre_norm("pre_norm0", x, npre0, deps=[start_dep])
    z0, w_ev_in = get_w("ev_in", h0)(h0)
    cat, sst = _hgrn2_fwd(z0, l0, l1, a_onorm)
    cat = _gmlp_fwd(z0, cat, ln_w, ln_b, ws, bias_col)
    w_ev_out = get_w("ev_out", cat)
    y0 = _mm_nn("ev_out", cat, w_ev_out, F32, 1024, 1024)
    get_w("od_relay", y0)
    x1, h1 = _post_pre_norm(x, y0, npost0, npre1)
    w_od_in, w_qb, w_kvb, q_norm, kv_norm = get_w("od_mid", h1)
    z1 = _mm_nt("od_in", h1, w_od_in[None], F32, 1024, 640)
    cqn, ckvn, kp = _mla_pre(z1, q_norm, kv_norm, cos_t, sin_t)
    q = _mm_nn("od_qb", cqn, w_qb[None], F32, 1024, 1024)
    kv = _mm_nn("od_kvb", ckvn, w_kvb, BF16, 1024, 512)
    o, lse, og = _attention_fwd(q, cos_t, sin_t, kv, kp, z1)
    w_od_out = get_w("od_out", og)
    y1 = _mm_nn("od_out", og, w_od_out, F32, 1024, 1024)
    dx2, dy1, loss_part, dnpost1 = _final_loss(x1, y1, npost1, target)

    g_od_out = _mm_tn("od_out_dw", og, dy1, 1, BF16, 1024, 1024)
    tok = put_g("od_out", [g_od_out.reshape(N_DEV, D // N_DEV, D)])
    dog = _mm_nt("od_out_dx", dy1, w_od_out, F32, 1024, 1024, deps=[tok])
    dq, dkv, dkp, dgate = _attention_bwd(q, cos_t, sin_t, kv, kp, o, lse, dog, z1)
    g_qb = _mm_tn("od_qb_dw", cqn, dq, 1, F32, 512, 1024)
    g_kvb = _mm_tn("od_kvb_dw", ckvn, dkv, N_DEV, BF16, 512, 512)
    tok = put_g("od_qkv", [_qb_unlayout(g_qb[0]).astype(BF16), g_kvb])
    dcqn = _mm_nt("od_qb_dx", dq, w_qb[None], F32, 1024, 512, deps=[tok])
    dckvn = _mm_nt("od_kvb_dx", dkv, w_kvb, F32, 1024, 512)
    dcq, dckv, dkpe, dqn, dkvn = _mla_pre_bwd(z1, q_norm, kv_norm, cos_t, sin_t, dcqn, dckvn, dkp)
    dz1 = jnp.concatenate([dgate, dcq, dckv, dkpe], axis=1)
    g_od_in = _mm_tn("od_in_dw", dz1, h1, 1, F32, 640, 1024)
    tok = put_g("od_in", [_odd_in_unlayout(g_od_in[0]).astype(BF16)])
    dh1 = _mm_nn("od_in_dx", dz1, w_od_in[None], F32, 1024, 1024, deps=[tok])
    dx1, dy0, dnpost0, dnpre1 = _post_pre_norm_bwd(y0, x1, npost0, npre1, dx2, dh1)

    g_ev_out = _mm_tn("ev_out_dw", cat, dy0, 1, BF16, 1024, 1024)
    tok = put_g("ev_out", [g_ev_out.reshape(N_DEV, D // N_DEV, D)])
    dcat = _mm_nt("ev_out_dx", dy0, w_ev_out, F32, 1024, 1024, deps=[tok])
    dqa, dfa, dia, dga, dl0, dl1, donorm = _hgrn2_bwd(z0, l0, l1, a_onorm, sst, dcat)
    dub, dvb, dgb, dlnw, dlnb, dws, dbias = _gmlp_bwd(z0, ln_w, ln_b, ws, bias_col, dcat)
    dz0 = jnp.concatenate([dqa, dfa, dia, dga, dub, dvb, dgb], axis=1)
    early = _small_rows(dnpre1, dnpost0, dnpost1, dl0, dl1, donorm, dlnw, dlnb, dws, dbias, dqn, dkvn, loss_part)
    tok = put_small(early) if put_small else None
    small_tok = tok

    def ev_in_half(name, parity, deps=()):
        return _mm_tn_parity(name, h0, dz0, N_DEV, parity, BF16, 1024, deps=[small_tok] + list(deps))

    tok = put_g("ev_in", ev_in_half)
    dh0 = _mm_nt("ev_in_dx", dz0, w_ev_in, F32, 1024, 256, deps=[tok])
    grad_x, dnpre0 = _pre_norm_bwd(x, npre0, dh0, dx1)
    return grad_x, early, dnpre0


def kernel(x, positions, norm_pre, norm_post, ev_w_in, ev_lb_logits, ev_a_onorm, ev_b_ln_w, ev_b_ln_b, ev_b_ws, ev_b_bias, ev_w_out, od_w_in, od_q_norm, od_w_qb, od_kv_norm, od_w_kvb, od_w_out, loss_target, m_norm_pre, m_norm_post, m_ev_w_in, m_ev_lb_logits, m_ev_a_onorm, m_ev_b_ln_w, m_ev_b_ln_b, m_ev_b_ws, m_ev_b_bias, m_ev_w_out, m_od_w_in, m_od_q_norm, m_od_w_qb, m_od_kv_norm, m_od_w_kvb, m_od_w_out, v_norm_pre, v_norm_post, v_ev_w_in, v_ev_lb_logits, v_ev_a_onorm, v_ev_b_ln_w, v_ev_b_ln_b, v_ev_b_ws, v_ev_b_bias, v_ev_w_out, v_od_w_in, v_od_q_norm, v_od_w_qb, v_od_kv_norm, v_od_w_kvb, v_od_w_out):
    me = 4 * lax.axis_index("x") + 2 * lax.axis_index("y") + lax.axis_index("c")
    bf = lambda w: w[0].astype(BF16)

    norms = jnp.pad(jnp.concatenate([od_q_norm, od_kv_norm], axis=1), ((0, 7), (0, 0)))
    sent = {}
    (first_src, first_near, first_far), first_sems, tok = _first_gather_send("gather_ev_in", bf(ev_w_in))
    sent["ev_out"], tok = _gather2_send("gather_ev_out", [bf(ev_w_out)], deps=[tok])
    sent["od"], tok = _gather2_send("gather_od", [od_w_in[0].T.astype(BF16), bf(od_w_qb), bf(od_w_kvb), norms,
                                                 bf(od_w_out)], deps=[tok])
    cos_t, sin_t = _rope_tables(positions)
    od = []

    def get_w(group, after):
        if group == "ev_in":
            def first_matmul(h0):
                near = _first_gather_near(("relay_ev_in", "arrived_ev_in"), first_src, first_near, first_sems,
                                          [after, cos_t, sin_t])
                ids = jnp.stack([me, me ^ 1, me ^ 2, me ^ 4, me ^ 3, me ^ 5]).astype(jnp.int32)
                z0 = _mm_nn_some("ev_in", h0, near, ids, ids, N_DEV, F32, 1024)
                far = _first_gather_far(("relay_ev_in_far", "arrived_ev_in_far"), first_far, first_sems, z0)
                far_ids = jnp.stack([me ^ 6, me ^ 7]).astype(jnp.int32)
                z0 = _mm_nn_some("ev_in_far", h0, far, jnp.arange(2, dtype=jnp.int32), far_ids, N_DEV, F32, 1024,
                                 prev=z0)
                full = lax.dynamic_update_slice_in_dim(near, far[0:1], me ^ 6, 0)
                return z0, lax.dynamic_update_slice_in_dim(full, far[1:2], me ^ 7, 0)
            return first_matmul
        if group == "ev_out":
            relayed, token = _gather2_relay("relay_ev_out", sent["ev_out"], after)
            return _split_done("arrived_ev_out", relayed, token)[0].reshape(1, D, D)
        if group == "od_relay":
            sent["od_relayed"], _ = _gather2_relay("relay_od", sent["od"], after)
            return None
        if not od:
            od.extend(_split_done("arrived_od", sent["od_relayed"], after))
        w_od_in, w_qb, w_kvb, norms_all, w_od_out = od
        if group == "od_out":
            return w_od_out.reshape(1, D, D)
        return (_odd_in_layout(w_od_in), _qb_layout(w_qb), w_kvb,
                norms_all[:, 0, :64].reshape(1, C_RANK), norms_all[:, 0, 64:].reshape(1, C_RANK))

    scatters = {}

    def put_g(group, grads):
        if group == "ev_in":
            core = lax.axis_index("c").astype(jnp.int32).reshape(1)
            paired, token = _scatter2_pair("pair_ev_in", [grads("ev_in_dw_sibling", 1 - core)])
            mine = grads("ev_in_dw_own", core, deps=[token])
            pair = _split_done("paired_ev_in", paired, mine)[0]
            scatters[group], token = _scatter2_send("scatter_ev_in", [_pair_add("pair_add_ev_in", mine, pair)])
        else:
            scatters[group], token = _exchange_start("scatter_" + group, grads, False)
        return token

    def put_small(early):
        scatters["small"], token = _exchange_start("gather_small_early", early, True)
        return token

    grad_x, _, dnpre0 = _forward_backward(
        x[0], cos_t, sin_t, loss_target[0], norm_pre, norm_post, ev_lb_logits, ev_a_onorm, ev_b_ln_w,
        ev_b_ln_b, ev_b_ws, ev_b_bias, get_w, put_g, put_small, start_dep=tok)

    big_w = {"ev_w_in": ev_w_in, "ev_w_out": ev_w_out, "od_w_in": od_w_in, "od_w_qb": od_w_qb,
             "od_w_kvb": od_w_kvb, "od_w_out": od_w_out}
    big_m = {"ev_w_in": m_ev_w_in, "ev_w_out": m_ev_w_out, "od_w_in": m_od_w_in, "od_w_qb": m_od_w_qb,
             "od_w_kvb": m_od_w_kvb, "od_w_out": m_od_w_out}
    big_v = {"ev_w_in": v_ev_w_in, "ev_w_out": v_ev_w_out, "od_w_in": v_od_w_in, "od_w_qb": v_od_w_qb,
             "od_w_kvb": v_od_w_kvb, "od_w_out": v_od_w_out}
    big_out = {}
    after = grad_x
    for group, names in (("od_out", ["od_w_out"]), ("od_qkv", ["od_w_qb", "od_w_kvb"]), ("od_in", ["od_w_in"]),
                         ("ev_out", ["ev_w_out"])):
        parts = _exchange_wait("summed_" + group, scatters[group], after)
        for nm, p in zip(names, parts):
            w, m, v = big_w[nm][0], big_m[nm][0], big_v[nm][0]
            if nm == "od_w_in":
                res_t = _adamw("adamw_" + nm, [(p, N_DEV)], w.T, m.T, v.T, w.shape[1], 512)
                big_out[nm] = [r.T[None] for r in res_t]
            else:
                big_out[nm] = [r[None] for r in _adamw("adamw_" + nm, [(p, N_DEV)], w, m, v, w.shape[0] // 8)]
            after = big_out[nm][0]

    late_all = _exchange("gather_small_late", [dnpre0], gather=True, deps=[after])[0]
    early_all = _exchange_wait("arrived_small_early", scatters["small"], late_all)

    small_w = (norm_pre, norm_post, ev_lb_logits, ev_a_onorm, ev_b_ln_w, ev_b_ln_b, ev_b_ws, ev_b_bias)
    small_m = (m_norm_pre, m_norm_post, m_ev_lb_logits, m_ev_a_onorm, m_ev_b_ln_w, m_ev_b_ln_b, m_ev_b_ws, m_ev_b_bias)
    small_v = (v_norm_pre, v_norm_post, v_ev_lb_logits, v_ev_a_onorm, v_ev_b_ln_w, v_ev_b_ln_b, v_ev_b_ws, v_ev_b_bias)
    wmv = [tuple(a.reshape(s) for a in t) for s, t in zip(SMALL_PARAM_SHAPES, zip(small_w, small_m, small_v))]
    small_res, loss_row, g_norm_rows = _adamw_small(late_all, early_all, wmv)
    small_out = [[r.reshape(w.shape) for r in four] for four, w in zip(small_res, small_w)]
    loss = loss_row[0, 0]

    g_norms = jnp.concatenate([lax.dynamic_slice(g_norm_rows, (0, 64 * me), (1, 64)),
                               lax.dynamic_slice(g_norm_rows, (1, 64 * me), (1, 64))], axis=1)
    res_n = _adamw("adamw_norms", [(g_norms[None], 1)],
                   jnp.concatenate([od_q_norm, od_kv_norm], axis=1),
                   jnp.concatenate([m_od_q_norm, m_od_kv_norm], axis=1),
                   jnp.concatenate([v_od_q_norm, v_od_kv_norm], axis=1), 1)
    qn_out = [r[:, :64] for r in res_n]
    kvn_out = [r[:, 64:] for r in res_n]

    chip_sums, from_peers = _split_done("summed_ev_in", scatters["ev_in"], loss_row, all_bufs=True)
    own_chip = lax.dynamic_slice_in_dim(chip_sums, me >> 1, 1, 0)
    w = ev_w_in[0]
    big_out["ev_w_in"] = [r[None] for r in _adamw("adamw_ev_w_in", [(own_chip, 1), (from_peers, 3)], w, m_ev_w_in[0],
                                                  v_ev_w_in[0], w.shape[0] // 8)]

    order = ("norm_pre", "norm_post", "ev_w_in", "ev_lb_logits", "ev_a_onorm", "ev_b_ln_w", "ev_b_ln_b",
             "ev_b_ws", "ev_b_bias", "ev_w_out", "od_w_in", "od_q_norm", "od_w_qb", "od_kv_norm",
             "od_w_kvb", "od_w_out")
    small_names = ("norm_pre", "norm_post", "ev_lb_logits", "ev_a_onorm", "ev_b_ln_w", "ev_b_ln_b",
                   "ev_b_ws", "ev_b_bias")
    outs = [loss, grad_x[None]]
    for kind in range(4):
        for nm in order:
            if nm in big_out:
                outs.append(big_out[nm][kind])
            elif nm == "od_q_norm":
                outs.append(qn_out[kind])
            elif nm == "od_kv_norm":
                outs.append(kvn_out[kind])
            else:
                outs.append(small_out[small_names.index(nm)][kind])
    return tuple(outs)
```

```python
import functools

import jax
import jax.numpy as jnp
from jax import lax
from jax.experimental import pallas as pl
from jax.experimental.pallas import tpu as pltpu

F32 = jnp.float32
BF16 = jnp.bfloat16

N_DEV = 8
T = 2048
D = 2048
EPS = 1e-6
A_HEADS = 8
HD = 128
A_CHUNK = 64
A_SUB = 16
B_GROUPS = 8
B_CHUNK = 128
EVEN_IN = 7168
C_HEADS = 16
C_RANK = 512
C_NOPE = 128
C_ROPE = 64
C_QK = C_NOPE + C_ROPE
C_V = 128
ODD_IN = 3136
ODD_IN_PAD = 3200
QP = 256
ROPE_THETA = 10000.0
ATT_SCALE = C_QK ** -0.5

ADAM_LR = 0.001
ADAM_B1 = 0.9
ADAM_B2 = 0.999
ADAM_EPS = 1e-08
ADAM_WD = 0.01
ADAM_STEP = 10

VMEM_LIMIT_V7X = 56 * 1024 * 1024
MESH_ID = pl.DeviceIdType.MESH


def _params(n_grid):
    return pltpu.CompilerParams(dimension_semantics=("arbitrary",) * n_grid,
                                vmem_limit_bytes=VMEM_LIMIT_V7X)


def _dg(a, b, ca, cb):
    return lax.dot_general(a.astype(BF16), b.astype(BF16), (((ca,), (cb,)), ((), ())),
                           preferred_element_type=F32)


def _raw_nn(a, b):
    return _dg(a, b, 1, 0)


def _raw_nt(a, b):
    return _dg(a, b, 1, 1)


def _raw_tn(a, b):
    return _dg(a, b, 0, 0)


@jax.custom_vjp
def _dot_nn(a, b):
    return _raw_nn(a, b)


def _dot_nn_fwd(a, b):
    return _raw_nn(a, b), (a.astype(BF16), b.astype(BF16))


def _dot_nn_bwd(res, g):
    a, b = res
    return _raw_nt(g, b), _raw_tn(a, g)


_dot_nn.defvjp(_dot_nn_fwd, _dot_nn_bwd)


@jax.custom_vjp
def _dot_nt(a, b):
    return _raw_nt(a, b)


def _dot_nt_fwd(a, b):
    return _raw_nt(a, b), (a.astype(BF16), b.astype(BF16))


def _dot_nt_bwd(res, g):
    a, b = res
    return _raw_nn(g, b), _raw_tn(g, a)


_dot_nt.defvjp(_dot_nt_fwd, _dot_nt_bwd)


@jax.custom_vjp
def _dot_tn(a, b):
    return _raw_tn(a, b)


def _dot_tn_fwd(a, b):
    return _raw_tn(a, b), (a.astype(BF16), b.astype(BF16))


def _dot_tn_bwd(res, g):
    a, b = res
    return _raw_nt(b, g), _raw_nn(a, g)


_dot_tn.defvjp(_dot_tn_fwd, _dot_tn_bwd)


@jax.custom_vjp
def _sigmoid(x):
    e = jnp.exp(-jnp.abs(x))
    return jnp.where(x >= 0, 1.0 / (1.0 + e), e / (1.0 + e))


def _sigmoid_fwd(x):
    s = _sigmoid(x)
    return s, s


def _sigmoid_bwd(s, g):
    return (g * s * (1.0 - s),)


_sigmoid.defvjp(_sigmoid_fwd, _sigmoid_bwd)


def _silu(x):
    return x * _sigmoid(x)


def _rms(x, w):
    return x * lax.rsqrt(jnp.mean(x * x, axis=-1, keepdims=True) + EPS) * w


def _split3(x):
    hi = x.astype(BF16)
    r = x - hi.astype(F32)
    mid = r.astype(BF16)
    lo = (r - mid.astype(F32)).astype(BF16)
    return hi, mid, lo


def _mask_apply(mask_bf16, x, contract):
    out = None
    for piece in _split3(x):
        d = lax.dot_general(mask_bf16, piece, (((contract,), (0,)), ((), ())),
                            preferred_element_type=F32)
        out = d if out is None else out + d
    return out


def _chunk_tri(rows):
    r = lax.broadcasted_iota(jnp.int32, (rows, rows), 0)
    c = lax.broadcasted_iota(jnp.int32, (rows, rows), 1)
    return ((r >= c) & (r // A_CHUNK == c // A_CHUNK)).astype(BF16)


@jax.custom_vjp
def _chunk_cumsum(x):
    return _mask_apply(_chunk_tri(x.shape[0]), x, 1)


def _chunk_cumsum_fwd(x):
    return _chunk_cumsum(x), None


def _chunk_cumsum_bwd(_, g):
    return (_mask_apply(_chunk_tri(g.shape[0]), g, 0),)


_chunk_cumsum.defvjp(_chunk_cumsum_fwd, _chunk_cumsum_bwd)


def _hgrn2_rows(q, zf, v, ga, st, l0, l1, onorm):
    rows = q.shape[0]
    n_sub = A_CHUNK // A_SUB
    mx = jnp.maximum(l0, l1)
    e0 = jnp.exp(l0 - mx)
    e1 = jnp.exp(l1 - mx)
    lb = e0 / (e0 + e1)
    lf = jnp.log(lb + (1.0 - lb) * _sigmoid(zf))
    k = (1.0 - lb) * _sigmoid(-zf)
    b = _chunk_cumsum(lf)

    t_idx = lax.broadcasted_iota(jnp.int32, (A_CHUNK, n_sub * A_CHUNK), 0)
    c_idx = lax.broadcasted_iota(jnp.int32, (A_CHUNK, n_sub * A_CHUNK), 1)
    sel = (c_idx // A_CHUNK == t_idx // A_SUB) & (c_idx % A_CHUNK <= t_idx)
    key_row = lax.broadcasted_iota(jnp.int32, (A_CHUNK, HD), 0)

    outs = []
    for n in range(rows // A_CHUNK):
        lo = n * A_CHUNK
        qc, kc, vc = q[lo:lo + A_CHUNK], k[lo:lo + A_CHUNK], v[lo:lo + A_CHUNK]
        lfc, bc = lf[lo:lo + A_CHUNK], b[lo:lo + A_CHUNK]
        b_last = bc[A_CHUNK - 1:A_CHUNK]
        o_inter = _dot_nt(qc * jnp.exp(bc), st)
        kv_t = _dot_tn(vc, kc * jnp.exp(b_last - bc))
        st = st * jnp.exp(b_last) + kv_t
        g_rows, k_subs = [], []
        for i in range(n_sub):
            g_i = bc[i * A_SUB:i * A_SUB + 1] - lfc[i * A_SUB:i * A_SUB + 1]
            g_rows.append(jnp.broadcast_to(g_i, (A_SUB, HD)))
            expo = jnp.where(key_row < (i + 1) * A_SUB, g_i - bc, -jnp.inf)
            k_subs.append(kc * jnp.exp(expo))
        q_sub = qc * jnp.exp(bc - jnp.concatenate(g_rows, axis=0))
        scores = _dot_nt(q_sub, jnp.concatenate(k_subs, axis=0))
        scores = jnp.where(sel, scores, 0.0)
        o_intra = _dot_nn(scores, jnp.concatenate([vc] * n_sub, axis=0))
        outs.append(o_inter + o_intra)
    o = jnp.concatenate(outs, axis=0)
    return _rms(o, onorm) * _silu(ga), st


def _gmlp_rows(u, vb, gb, lnw, lnb, ws, bias):
    rows = u.shape[0]
    mu = jnp.mean(vb, axis=-1, keepdims=True)
    xc = vb - mu
    vg = xc * lax.rsqrt(jnp.mean(xc * xc, axis=-1, keepdims=True) + EPS) * lnw + lnb
    r = lax.broadcasted_iota(jnp.int32, (B_CHUNK, B_CHUNK), 0)
    c = lax.broadcasted_iota(jnp.int32, (B_CHUNK, B_CHUNK), 1)
    ws_causal = jnp.where(r >= c, ws, 0.0)
    svs = [_dot_nn(ws_causal, vg[n * B_CHUNK:(n + 1) * B_CHUNK]) + bias
           for n in range(rows // B_CHUNK)]
    return u * jnp.concatenate(svs, axis=0) * _silu(gb)


def _rope(x, cos_t, sin_t):
    return x * cos_t + pltpu.roll(x, 64, 1) * sin_t


def _rope_transpose(g, cos_t, sin_t):
    return g * cos_t + pltpu.roll(g * sin_t, 64, 1)


ANY_SPEC = pl.BlockSpec(memory_space=pl.ANY)


def _live(deps):
    return [d for d in deps if d is not None]


def _skip_deps(body, n_in, n_deps):
    def wrapped(*refs):
        return body(*refs[:n_in], *refs[n_in + n_deps:])
    return wrapped


def _pure_call(name, fn, grid, in_specs, out_specs, out_shape, args, n_acc=0, deps=()):
    deps = _live(deps)
    n_in, n_out, n_deps = len(in_specs), len(out_specs), len(deps)
    in_specs = list(in_specs) + [ANY_SPEC] * n_deps
    args = tuple(args) + tuple(deps)

    def body(*refs):
        res = fn(*[r[...] for r in refs[:n_in]])
        if not isinstance(res, (tuple, list)):
            res = (res,)
        outs = refs[n_in + n_deps:n_in + n_deps + n_out]
        for o, r in zip(outs[:n_out - n_acc], res[:n_out - n_acc]):
            o[...] = r.astype(o.dtype)
        if n_acc:
            first = functools.reduce(jnp.logical_and, [pl.program_id(i) == 0 for i in range(len(grid))])
            for o, r in zip(outs[n_out - n_acc:], res[n_out - n_acc:]):
                @pl.when(first)
                def _(o=o, r=r):
                    o[...] = r.astype(o.dtype)

                @pl.when(jnp.logical_not(first))
                def _(o=o, r=r):
                    o[...] += r.astype(o.dtype)

    return pl.pallas_call(body, name=name, grid=grid, in_specs=in_specs, out_specs=out_specs,
                          out_shape=out_shape, compiler_params=_params(len(grid)))(*args)


def _sds(shape, dtype):
    return jax.ShapeDtypeStruct(shape, dtype)


def _row_spec(tm, width, col=0):
    return pl.BlockSpec((tm, width), lambda i, col=col: (i, col))


def _full_spec(shape):
    nd = len(shape)
    return pl.BlockSpec(shape, lambda *_: (0,) * nd)


def _mm_nn(name, a, b, out_dtype, tm, tn, deps=()):
    deps = _live(deps)
    m, k = a.shape
    j, _, n = b.shape
    per = n // tn

    def body(a_ref, b_ref, o_ref):
        o_ref[...] = _raw_nn(a_ref[...], b_ref[...]).astype(o_ref.dtype)

    return pl.pallas_call(
        _skip_deps(body, 2, len(deps)), name=name, grid=(m // tm, j * per),
        in_specs=[pl.BlockSpec((tm, k), lambda i, c: (i, 0)),
                  pl.BlockSpec((None, k, tn), lambda i, c: (c // per, 0, c % per))] + [ANY_SPEC] * len(deps),
        out_specs=pl.BlockSpec((tm, tn), lambda i, c: (i, c)),
        out_shape=_sds((m, j * n), out_dtype), compiler_params=_params(2))(a, b, *deps)


def _mm_nn_some(name, a, b, slots, cols, n_cols, out_dtype, tm, prev=None):
    m, k = a.shape
    n = b.shape[2]
    n_sel = slots.shape[0]

    def body(slots_ref, cols_ref, a_ref, b_ref, *rest):
        del slots_ref, cols_ref
        rest[-1][...] = _raw_nn(a_ref[...], b_ref[...]).astype(rest[-1].dtype)

    grid_spec = pltpu.PrefetchScalarGridSpec(
        num_scalar_prefetch=2, grid=(m // tm, n_sel),
        in_specs=[pl.BlockSpec((tm, k), lambda i, s, sl, cl: (i, 0)),
                  pl.BlockSpec((None, k, n), lambda i, s, sl, cl: (sl[s], 0, 0))] + ([ANY_SPEC] if prev is not None else []),
        out_specs=pl.BlockSpec((tm, n), lambda i, s, sl, cl: (i, cl[s])))
    return pl.pallas_call(
        body, name=name, grid_spec=grid_spec, out_shape=_sds((m, n_cols * n), out_dtype),
        input_output_aliases={4: 0} if prev is not None else {},
        compiler_params=_params(2))(slots, cols, a, b, *([prev] if prev is not None else []))


def _mm_nt(name, a, b, out_dtype, tm, tn, deps=()):
    deps = _live(deps)
    m = a.shape[0]
    j, nn, n = b.shape

    def body(a_ref, b_ref, o_ref):
        b_all = b_ref[0] if j == 1 else jnp.concatenate([b_ref[s] for s in range(j)], axis=1)
        o_ref[...] = _raw_nt(a_ref[...], b_all).astype(o_ref.dtype)

    return pl.pallas_call(
        _skip_deps(body, 2, len(deps)), name=name, grid=(m // tm, nn // tn),
        in_specs=[pl.BlockSpec((tm, j * n), lambda i, c: (i, 0)),
                  pl.BlockSpec((j, tn, n), lambda i, c: (0, c, 0))] + [ANY_SPEC] * len(deps),
        out_specs=pl.BlockSpec((tm, tn), lambda i, c: (i, c)),
        out_shape=_sds((m, nn), out_dtype), compiler_params=_params(2))(a, b, *deps)


def _mm_tn(name, a, b, j, out_dtype, tm, tn, deps=()):
    deps = _live(deps)
    k, m = a.shape
    n = b.shape[1] // j
    per = n // tn

    def body(a_ref, b_ref, o_ref):
        o_ref[...] = _raw_tn(a_ref[...], b_ref[...]).astype(o_ref.dtype)

    return pl.pallas_call(
        _skip_deps(body, 2, len(deps)), name=name, grid=(m // tm, j * per),
        in_specs=[pl.BlockSpec((k, tm), lambda i, c: (0, i)),
                  pl.BlockSpec((k, tn), lambda i, c: (0, c))] + [ANY_SPEC] * len(deps),
        out_specs=pl.BlockSpec((None, tm, tn), lambda i, c: (c // per, i, c % per)),
        out_shape=_sds((j, m, n), out_dtype), compiler_params=_params(2))(a, b, *deps)


def _mm_tn_parity(name, a, b, j, parity, out_dtype, tm, deps=()):
    deps = _live(deps)
    k, m = a.shape
    n = b.shape[1] // j

    def body(par_ref, a_ref, b_ref, o_ref):
        del par_ref
        o_ref[...] = _raw_tn(a_ref[...], b_ref[...]).astype(o_ref.dtype)

    grid_spec = pltpu.PrefetchScalarGridSpec(
        num_scalar_prefetch=1, grid=(m // tm, j // 2),
        in_specs=[pl.BlockSpec((k, tm), lambda i, s, par: (0, i)),
                  pl.BlockSpec((k, n), lambda i, s, par: (0, 2 * s + par[0]))] + [ANY_SPEC] * len(deps),
        out_specs=pl.BlockSpec((None, tm, n), lambda i, s, par: (s, i, 0)))
    return pl.pallas_call(
        lambda par_ref, *refs: _skip_deps(functools.partial(body, par_ref), 2, len(deps))(*refs),
        name=name, grid_spec=grid_spec, out_shape=_sds((j // 2, m, n), out_dtype),
        compiler_params=_params(2))(parity, a, b, *deps)


TM = 256


def _pre_norm(name, x, w_row, deps=()):
    def fn(xv, w):
        return _rms(xv, w)
    return _pure_call(name, fn, (T // TM,), [_row_spec(TM, D), _full_spec((1, D))],
                      [_row_spec(TM, D)], [_sds((T, D), BF16)], (x, w_row), deps=deps)[0]


def _post_pre_norm(x, y, w_post, w_pre):
    def fn(xv, yv, wp, wn):
        x1 = xv + _rms(yv, wp)
        return x1, _rms(x1, wn)
    return _pure_call("post_pre_norm", fn, (T // TM,),
                      [_row_spec(TM, D), _row_spec(TM, D), _full_spec((1, D)), _full_spec((1, D))],
                      [_row_spec(TM, D), _row_spec(TM, D)],
                      [_sds((T, D), F32), _sds((T, D), BF16)], (x, y, w_post, w_pre))


def _post_pre_norm_bwd(y, x1, w_post, w_pre, dx1_in, dh1, deps=()):
    def fn(yv, x1v, wp, wn, dx1v, dh1v):
        _, vjp_pre = jax.vjp(_rms, x1v, wn)
        dx1_h, dwn = vjp_pre(dh1v)
        dx1 = dx1v + dx1_h
        _, vjp_post = jax.vjp(_rms, yv, wp)
        dy, dwp = vjp_post(dx1)
        return dx1, dy, dwp, dwn
    return _pure_call("post_pre_norm_bwd", fn, (T // TM,),
                      [_row_spec(TM, D), _row_spec(TM, D), _full_spec((1, D)), _full_spec((1, D)),
                       _row_spec(TM, D), _row_spec(TM, D)],
                      [_row_spec(TM, D), _row_spec(TM, D), _full_spec((1, D)), _full_spec((1, D))],
                      [_sds((T, D), F32), _sds((T, D), BF16), _sds((1, D), F32), _sds((1, D), F32)],
                      (y, x1, w_post, w_pre, dx1_in, dh1), n_acc=2, deps=deps)


def _final_loss(x1, y, w_post, target):
    def fn(x1v, yv, wp, tv):
        r, vjp = jax.vjp(_rms, yv, wp)
        err = x1v + r - tv
        part = 0.5 * jnp.sum(jnp.mean(err * err, axis=-1, keepdims=True), axis=0, keepdims=True)
        dx2 = err * (1.0 / D)
        dy, dwp = vjp(dx2)
        return dx2, dy, jnp.broadcast_to(part, (1, 128)), dwp
    return _pure_call("final_loss", fn, (T // TM,),
                      [_row_spec(TM, D), _row_spec(TM, D), _full_spec((1, D)), _row_spec(TM, D)],
                      [_row_spec(TM, D), _row_spec(TM, D), _full_spec((1, 128)), _full_spec((1, D))],
                      [_sds((T, D), F32), _sds((T, D), BF16), _sds((1, 128), F32), _sds((1, D), F32)],
                      (x1, y, w_post, target), n_acc=2)


def _pre_norm_bwd(x, w_row, dh, dx_res, deps=()):
    def fn(xv, w, dhv, dxv):
        _, vjp = jax.vjp(_rms, xv, w)
        dx, dw = vjp(dhv)
        return dxv + dx, dw
    return _pure_call("pre_norm_bwd", fn, (T // TM,),
                      [_row_spec(TM, D), _full_spec((1, D)), _row_spec(TM, D), _row_spec(TM, D)],
                      [_row_spec(TM, D), _full_spec((1, D))],
                      [_sds((T, D), F32), _sds((1, D), F32)], (x, w_row, dh, dx_res), n_acc=1, deps=deps)


RA = 256
RB = 512


HA = 4
A_GROUPS = A_HEADS // HA


def _head(ref, hh):
    return ref[:, hh * HD:(hh + 1) * HD]


def _hgrn2_fwd(z, l0, l1, onorm):
    nb = T // RA

    def body(q_ref, f_ref, v_ref, g_ref, l0_ref, l1_ref, on_ref, cat_ref, sst_ref, st_scr):
        @pl.when(pl.program_id(1) == 0)
        def _():
            st_scr[...] = jnp.zeros_like(st_scr)

        for hh in range(HA):
            st = st_scr[hh]
            sst_ref[hh] = st
            out, st_new = _hgrn2_rows(_head(q_ref, hh), _head(f_ref, hh), _head(v_ref, hh), _head(g_ref, hh), st,
                                      _head(l0_ref, hh), _head(l1_ref, hh), on_ref[...])
            cat_ref[:, hh * HD:(hh + 1) * HD] = out.astype(cat_ref.dtype)
            st_scr[hh] = st_new

    def cols(k):
        return pl.BlockSpec((RA, HA * HD), lambda g, r: (r, k * A_GROUPS + g))

    vec = pl.BlockSpec((1, HA * HD), lambda g, r: (0, g))
    return pl.pallas_call(
        body, name="hgrn2_fwd", grid=(A_GROUPS, nb),
        in_specs=[cols(0), cols(1), cols(2), cols(3), vec, vec, _full_spec((1, HD))],
        out_specs=[cols(0), pl.BlockSpec((HA, None, HD, HD), lambda g, r: (g, r, 0, 0))],
        out_shape=[_sds((T, 2 * A_HEADS * HD), BF16), _sds((A_HEADS, nb, HD, HD), F32)],
        scratch_shapes=[pltpu.VMEM((HA, HD, HD), F32)],
        compiler_params=_params(2))(z, z, z, z, l0, l1, onorm)


def _hgrn2_bwd(z, l0, l1, onorm, sst, dcat, deps=()):
    nb = T // RA
    deps = _live(deps)

    def body(q_ref, f_ref, v_ref, g_ref, l0_ref, l1_ref, on_ref, sst_ref, dcat_ref,
             dq_ref, df_ref, dv_ref, dg_ref, dl0_ref, dl1_ref, don_ref, ds_scr):
        g, r = pl.program_id(0), pl.program_id(1)

        @pl.when(r == 0)
        def _():
            ds_scr[...] = jnp.zeros_like(ds_scr)

        dl0s, dl1s, don = [], [], None
        for hh in range(HA):
            _, vjp = jax.vjp(_hgrn2_rows, _head(q_ref, hh), _head(f_ref, hh), _head(v_ref, hh), _head(g_ref, hh),
                             sst_ref[hh], _head(l0_ref, hh), _head(l1_ref, hh), on_ref[...])
            dq, dzf, dv, dga, dst, dl0, dl1, don_h = vjp((_head(dcat_ref, hh), ds_scr[hh]))
            for ref, val in ((dq_ref, dq), (df_ref, dzf), (dv_ref, dv), (dg_ref, dga)):
                ref[:, hh * HD:(hh + 1) * HD] = val.astype(ref.dtype)
            ds_scr[hh] = dst
            dl0s.append(dl0)
            dl1s.append(dl1)
            don = don_h if don is None else don + don_h
        dl0 = jnp.concatenate(dl0s, axis=1)
        dl1 = jnp.concatenate(dl1s, axis=1)

        @pl.when(r == 0)
        def _():
            dl0_ref[...] = dl0
            dl1_ref[...] = dl1

        @pl.when(r > 0)
        def _():
            dl0_ref[...] += dl0
            dl1_ref[...] += dl1

        first = jnp.logical_and(g == 0, r == 0)

        @pl.when(first)
        def _():
            don_ref[...] = don

        @pl.when(jnp.logical_not(first))
        def _():
            don_ref[...] += don

    def rev(k):
        return pl.BlockSpec((RA, HA * HD), lambda g, r: (nb - 1 - r, k * A_GROUPS + g))

    vec = pl.BlockSpec((1, HA * HD), lambda g, r: (0, g))
    grad = _sds((T, A_HEADS * HD), BF16)
    return pl.pallas_call(
        _skip_deps(body, 9, len(deps)), name="hgrn2_bwd", grid=(A_GROUPS, nb),
        in_specs=[rev(0), rev(1), rev(2), rev(3), vec, vec, _full_spec((1, HD)),
                  pl.BlockSpec((HA, None, HD, HD), lambda g, r: (g, nb - 1 - r, 0, 0)),
                  rev(0)] + [ANY_SPEC] * len(deps),
        out_specs=[rev(0)] * 4 + [vec, vec, _full_spec((1, HD))],
        out_shape=[grad] * 4 + [_sds((1, A_HEADS * HD), F32)] * 2 + [_sds((1, HD), F32)],
        scratch_shapes=[pltpu.VMEM((HA, HD, HD), F32)],
        compiler_params=_params(2))(z, z, z, z, l0, l1, onorm, sst, dcat, *deps)


GB = 4
B_STEPS = B_GROUPS // GB


def _gmlp_specs():
    vec = pl.BlockSpec((1, GB * HD), lambda s, r: (0, s))
    ws = pl.BlockSpec((GB, B_CHUNK, B_CHUNK), lambda s, r: (s, 0, 0))
    bias = pl.BlockSpec((GB, B_CHUNK, 1), lambda s, r: (s, 0, 0))

    def cols(k):
        return pl.BlockSpec((RB, GB * HD), lambda s, r: (r, k * B_STEPS + s))
    return vec, ws, bias, cols


def _gmlp_fwd(z, cat, lnw, lnb, ws, bias):
    vec, ws_spec, bias_spec, cols = _gmlp_specs()

    def body(u_ref, v_ref, g_ref, lnw_ref, lnb_ref, ws_ref, bias_ref, cat_in_ref, cat_ref):
        del cat_in_ref
        for gg in range(GB):
            out = _gmlp_rows(_head(u_ref, gg), _head(v_ref, gg), _head(g_ref, gg), _head(lnw_ref, gg),
                             _head(lnb_ref, gg), ws_ref[gg], bias_ref[gg])
            cat_ref[:, gg * HD:(gg + 1) * HD] = out.astype(cat_ref.dtype)

    return pl.pallas_call(
        body, name="gmlp_fwd", grid=(B_STEPS, T // RB),
        in_specs=[cols(4), cols(5), cols(6), vec, vec, ws_spec, bias_spec, pl.BlockSpec(memory_space=pl.ANY)],
        out_specs=cols(1),
        out_shape=_sds(cat.shape, cat.dtype), input_output_aliases={7: 0},
        compiler_params=_params(2))(z, z, z, lnw, lnb, ws, bias, cat)


def _gmlp_bwd(z, lnw, lnb, ws, bias, dcat):
    vec, ws_spec, bias_spec, cols = _gmlp_specs()

    def body(u_ref, v_ref, g_ref, lnw_ref, lnb_ref, ws_ref, bias_ref, dcat_ref,
             du_ref, dv_ref, dg_ref, dlnw_ref, dlnb_ref, dws_ref, dbias_ref):
        first = pl.program_id(1) == 0
        for gg in range(GB):
            _, vjp = jax.vjp(_gmlp_rows, _head(u_ref, gg), _head(v_ref, gg), _head(g_ref, gg), _head(lnw_ref, gg),
                             _head(lnb_ref, gg), ws_ref[gg], bias_ref[gg])
            du, dv, dg, dlnw, dlnb, dws, dbias = vjp(_head(dcat_ref, gg))
            lanes = slice(gg * HD, (gg + 1) * HD)
            for ref, val in ((du_ref, du), (dv_ref, dv), (dg_ref, dg)):
                ref[:, lanes] = val.astype(ref.dtype)
            sums = ((dlnw_ref, (slice(None), lanes), dlnw), (dlnb_ref, (slice(None), lanes), dlnb),
                    (dws_ref, gg, dws), (dbias_ref, gg, dbias))
            for ref, idx, val in sums:
                @pl.when(first)
                def _(ref=ref, idx=idx, val=val):
                    ref[idx] = val

                @pl.when(jnp.logical_not(first))
                def _(ref=ref, idx=idx, val=val):
                    ref[idx] += val

    grad = _sds((T, B_GROUPS * HD), BF16)
    return pl.pallas_call(
        body, name="gmlp_bwd", grid=(B_STEPS, T // RB),
        in_specs=[cols(4), cols(5), cols(6), vec, vec, ws_spec, bias_spec, cols(1)],
        out_specs=[cols(0)] * 3 + [vec, vec, ws_spec, bias_spec],
        out_shape=[grad] * 3 + [_sds((1, B_GROUPS * HD), F32)] * 2
        + [_sds((B_GROUPS, B_CHUNK, B_CHUNK), F32), _sds((B_GROUPS, B_CHUNK, 1), F32)],
        compiler_params=_params(2))(z, z, z, lnw, lnb, ws, bias, dcat)


def _mla_pre(z1, qn, kvn, cos_t, sin_t):
    def fn(cq, ckv, kpe, cs, sn, wq, wkv):
        return _rms(cq, wq), _rms(ckv, wkv), _rope(kpe, cs, sn)
    return _pure_call("mla_pre", fn, (T // TM,),
                      [_row_spec(TM, C_RANK, 4), _row_spec(TM, C_RANK, 5), _row_spec(TM, HD, 24),
                       _row_spec(TM, HD), _row_spec(TM, HD),
                       _full_spec((1, C_RANK)), _full_spec((1, C_RANK))],
                      [_row_spec(TM, C_RANK), _row_spec(TM, C_RANK), _row_spec(TM, HD)],
                      [_sds((T, C_RANK), BF16), _sds((T, C_RANK), BF16), _sds((T, HD), BF16)],
                      (z1, z1, z1, cos_t, sin_t, qn, kvn))


def _mla_pre_bwd(z1, qn, kvn, cos_t, sin_t, dcqn, dckvn, dkp, deps=()):
    def fn(cq, ckv, cs, sn, wq, wkv, g_q, g_kv, g_kp):
        _, vjp_q = jax.vjp(_rms, cq, wq)
        dcq, dwq = vjp_q(g_q)
        _, vjp_kv = jax.vjp(_rms, ckv, wkv)
        dckv, dwkv = vjp_kv(g_kv)
        return dcq, dckv, _rope_transpose(g_kp, cs, sn), dwq, dwkv
    return _pure_call("mla_pre_bwd", fn, (T // TM,),
                      [_row_spec(TM, C_RANK, 4), _row_spec(TM, C_RANK, 5),
                       _row_spec(TM, HD), _row_spec(TM, HD),
                       _full_spec((1, C_RANK)), _full_spec((1, C_RANK)),
                       _row_spec(TM, C_RANK), _row_spec(TM, C_RANK), _row_spec(TM, HD)],
                      [_row_spec(TM, C_RANK), _row_spec(TM, C_RANK), _row_spec(TM, HD),
                       _full_spec((1, C_RANK)), _full_spec((1, C_RANK))],
                      [_sds((T, C_RANK), BF16), _sds((T, C_RANK), BF16), _sds((T, HD), BF16),
                       _sds((1, C_RANK), F32), _sds((1, C_RANK), F32)],
                      (z1, z1, cos_t, sin_t, qn, kvn, dcqn, dckvn, dkp), n_acc=2, deps=deps)


TQ = 256
HP = 2
KVW = C_NOPE + C_V


def _att_keys(kv_ref, kp_ref, k_scr):
    @pl.when(pl.program_id(1) == 0)
    def _():
        for hh in range(HP):
            k_scr[hh, :, 0:C_NOPE] = kv_ref[:, hh * KVW:hh * KVW + C_NOPE]
            k_scr[hh, :, C_NOPE:QP] = kp_ref[...]


def _att_scores(q, cos_ref, sin_ref, k_scr, hh, n):
    keys = (n + 1) * TQ
    qr = jnp.concatenate([q[:, :C_NOPE], _rope(q[:, C_NOPE:], cos_ref[...], sin_ref[...])], axis=1).astype(BF16)
    return qr, _raw_nt(qr, k_scr[hh, 0:keys, :]) * ATT_SCALE


def _causal(x, n, fill):
    row = lax.broadcasted_iota(jnp.int32, (TQ, TQ), 0)
    col = lax.broadcasted_iota(jnp.int32, (TQ, TQ), 1)
    diag = jnp.where(col <= row, x[:, n * TQ:], fill)
    return diag if n == 0 else jnp.concatenate([x[:, :n * TQ], diag], axis=1)


def _per_query_block(fn):
    for n in range(T // TQ):
        pl.when(pl.program_id(1) == n)(functools.partial(fn, n))


def _att_in_specs():
    return [pl.BlockSpec((TQ, HP * QP), lambda g, i: (i, g)),
            pl.BlockSpec((TQ, HD), lambda g, i: (i, 0)),
            pl.BlockSpec((TQ, HD), lambda g, i: (i, 0)),
            pl.BlockSpec((T, HP * KVW), lambda g, i: (0, g)),
            pl.BlockSpec((T, HD), lambda g, i: (0, 0))]


def _attention_fwd(q, cos_t, sin_t, kv, kp, z1):
    def body(q_ref, cos_ref, sin_ref, kv_ref, kp_ref, gate_ref, o_ref, lse_ref, og_ref, k_scr):
        _att_keys(kv_ref, kp_ref, k_scr)

        def block(n):
            keys = (n + 1) * TQ
            for hh in range(HP):
                _, s = _att_scores(q_ref[:, hh * QP:(hh + 1) * QP], cos_ref, sin_ref, k_scr, hh, n)
                s = _causal(s, n, jnp.finfo(F32).min)
                m = jnp.max(s, axis=-1, keepdims=True)
                p = jnp.exp(s - m)
                l = jnp.sum(p, axis=-1, keepdims=True)
                v = kv_ref[0:keys, hh * KVW + C_NOPE:(hh + 1) * KVW]
                o = _raw_nn(p, v) / l
                lanes = slice(hh * C_V, (hh + 1) * C_V)
                o_ref[:, lanes] = o
                og_ref[:, lanes] = (o * _silu(gate_ref[:, lanes])).astype(og_ref.dtype)
                lse_ref[hh] = m + jnp.log(l)

        _per_query_block(block)

    heads = pl.BlockSpec((TQ, HP * C_V), lambda g, i: (i, g))
    return pl.pallas_call(
        body, name="attention_fwd", grid=(C_HEADS // HP, T // TQ), in_specs=_att_in_specs() + [heads],
        out_specs=[heads, pl.BlockSpec((HP, TQ, 1), lambda g, i: (g, i, 0)), heads],
        out_shape=[_sds((T, C_HEADS * C_V), F32), _sds((C_HEADS, T, 1), F32), _sds((T, C_HEADS * C_V), BF16)],
        scratch_shapes=[pltpu.VMEM((HP, T, QP), BF16)],
        compiler_params=_params(2))(q, cos_t, sin_t, kv, kp, z1)


def _attention_bwd(q, cos_t, sin_t, kv, kp, o, lse, dog, z1):
    nq = T // TQ

    def body(q_ref, cos_ref, sin_ref, kv_ref, kp_ref, o_ref, lse_ref, dog_ref, gate_ref,
             dq_ref, dkv_ref, dkp_ref, dgate_ref, k_scr, dk_scr, dv_scr):
        g, i = pl.program_id(0), pl.program_id(1)
        _att_keys(kv_ref, kp_ref, k_scr)

        @pl.when(i == 0)
        def _():
            dv_scr[...] = jnp.zeros_like(dv_scr)
            dk_scr[...] = jnp.zeros_like(dk_scr)

        def block(n):
            keys = (n + 1) * TQ
            for hh in range(HP):
                qr, s = _att_scores(q_ref[:, hh * QP:(hh + 1) * QP], cos_ref, sin_ref, k_scr, hh, n)
                p = _causal(jnp.exp(s - lse_ref[hh]), n, 0.0)
                lanes = slice(hh * C_V, (hh + 1) * C_V)
                ov, gate, dogv = o_ref[:, lanes], gate_ref[:, lanes], dog_ref[:, lanes]
                sig = _sigmoid(gate)
                silu = gate * sig
                dov = dogv * silu
                dgate_ref[:, lanes] = (dogv * ov * (sig + silu * (1.0 - sig))).astype(dgate_ref.dtype)
                delta = jnp.sum(dov * ov, axis=-1, keepdims=True)
                dp = _raw_nt(dov, kv_ref[0:keys, hh * KVW + C_NOPE:(hh + 1) * KVW])
                ds = p * (dp - delta) * ATT_SCALE
                dq = _raw_nn(ds, k_scr[hh, 0:keys, :])
                dq_ref[:, hh * QP:(hh + 1) * QP] = jnp.concatenate(
                    [dq[:, :C_NOPE], _rope_transpose(dq[:, C_NOPE:], cos_ref[...], sin_ref[...])],
                    axis=1).astype(dq_ref.dtype)
                dv_scr[hh, 0:keys, :] += _raw_tn(p, dov)
                dk_scr[hh, 0:keys, :] += _raw_tn(ds, qr)

        _per_query_block(block)

        @pl.when(i == nq - 1)
        def _():
            for hh in range(HP):
                dkv_ref[:, hh * KVW:(hh + 1) * KVW] = jnp.concatenate(
                    [dk_scr[hh, :, 0:C_NOPE], dv_scr[hh]], axis=1).astype(dkv_ref.dtype)

        @pl.when(jnp.logical_and(i == nq - 1, g == 0))
        def _():
            dkp_ref[...] = dk_scr[0, :, C_NOPE:QP]

        @pl.when(jnp.logical_and(i == nq - 1, g > 0))
        def _():
            dkp_ref[...] += dk_scr[0, :, C_NOPE:QP]

        @pl.when(i == nq - 1)
        def _():
            for hh in range(1, HP):
                dkp_ref[...] += dk_scr[hh, :, C_NOPE:QP]

    heads = pl.BlockSpec((TQ, HP * C_V), lambda g, i: (i, g))
    return pl.pallas_call(
        body, name="attention_bwd", grid=(C_HEADS // HP, nq),
        in_specs=_att_in_specs() + [heads, pl.BlockSpec((HP, TQ, 1), lambda g, i: (g, i, 0)), heads, heads],
        out_specs=[pl.BlockSpec((TQ, HP * QP), lambda g, i: (i, g)),
                   pl.BlockSpec((T, HP * KVW), lambda g, i: (0, g)),
                   _full_spec((T, HD)), heads],
        out_shape=[_sds((T, C_HEADS * QP), BF16), _sds((T, C_HEADS * KVW), BF16), _sds((T, HD), F32),
                   _sds((T, C_HEADS * C_V), BF16)],
        scratch_shapes=[pltpu.VMEM((HP, T, QP), BF16), pltpu.VMEM((HP, T, QP), F32), pltpu.VMEM((HP, T, C_V), F32)],
        compiler_params=_params(2))(q, cos_t, sin_t, kv, kp, o, lse, dog, z1)


def _adamw_math(w, g, m, v):
    m = ADAM_B1 * m + (1.0 - ADAM_B1) * g
    v = ADAM_B2 * v + (1.0 - ADAM_B2) * (g * g)
    m_hat = m / (1.0 - ADAM_B1 ** ADAM_STEP)
    v_hat = v / (1.0 - ADAM_B2 ** ADAM_STEP)
    delta = -ADAM_LR * (m_hat / (jnp.sqrt(v_hat) + ADAM_EPS) + ADAM_WD * w)
    return delta, m, v


def _adamw(name, parts, w, m, v, tr, tc=None):
    rows, cols = w.shape

    def fn(*vals):
        pvs, (wv, mv, vv) = vals[:len(parts)], vals[len(parts):]
        g = None
        for pv in pvs:
            for d in range(pv.shape[0]):
                term = pv[d].astype(F32)
                g = term if g is None else g + term
        return (g,) + _adamw_math(wv, g, mv, vv)

    tc = cols if tc is None else tc
    blk = pl.BlockSpec((tr, tc), lambda i, j: (i, j))
    part_specs = [pl.BlockSpec((n, tr, tc), lambda i, j: (0, i, j)) for _, n in parts]
    return _pure_call(name, fn, (rows // tr, cols // tc), part_specs + [blk, blk, blk],
                      [blk] * 4, [_sds((rows, cols), F32)] * 4, tuple(p for p, _ in parts) + (w, m, v))


SMALL_PARAM_SHAPES = ((2, D), (2, D), (2, A_HEADS * HD), (1, HD), (1, B_GROUPS * HD), (1, B_GROUPS * HD),
                      (B_GROUPS, B_CHUNK, B_CHUNK), (B_GROUPS, B_CHUNK))
SMALL_PIECES = ((0, 0, 0, 0), (0, 1, 1, 0), (1, 0, 1, 1), (1, 1, 1, 2), (2, 0, 2, 0), (2, 1, 2, 1),
                (3, 0, 3, 8), (4, 0, 2, 2), (5, 0, 2, 3))


def _small_rows(dnpre1, dnpost0, dnpost1, dl0, dl1, donorm, dlnw, dlnb, dws, dbias, dqn, dkvn, loss_part):
    return [jnp.concatenate([dnpre1, dnpost0, dnpost1], axis=0),
            jnp.concatenate([dl0, dl1, dlnw, dlnb], axis=0),
            jnp.concatenate([dbias.reshape(B_GROUPS, B_CHUNK), donorm, loss_part], axis=0),
            dws,
            jnp.concatenate([dqn, dkvn], axis=0)]


def _adamw_small(late_all, early_all, wmv):
    n_in = 6 + 3 * len(wmv)

    def body(*refs):
        gathered, params, outs = refs[:6], refs[6:n_in], refs[n_in:]

        def total(ref):
            s = ref[0]
            for d in range(1, N_DEV):
                s = s + ref[d]
            return s

        g_late, g2048, g1024, g128, g_ws, g512 = [total(r) for r in gathered]
        arrays = (g_late, g2048, g1024, g128)

        def update(p, rows, g):
            w_ref, m_ref, v_ref = params[3 * p:3 * p + 3]
            delta, m, v = _adamw_math(w_ref[rows], g, m_ref[rows], v_ref[rows])
            for out, val in zip(outs[4 * p:4 * p + 4], (g, delta, m, v)):
                out[rows] = val

        for p, row, arr, arr_row in SMALL_PIECES:
            update(p, pl.ds(row, 1), arrays[arr][arr_row:arr_row + 1])
        update(6, slice(None), g_ws)
        update(7, slice(None), g128[0:B_GROUPS])
        outs[32][...] = g128[B_GROUPS + 1:B_GROUPS + 2]
        outs[33][...] = g512

    vmem = pl.BlockSpec(memory_space=pltpu.VMEM)
    flat = [a for t in wmv for a in t]
    out_shape = [_sds(s, F32) for s in SMALL_PARAM_SHAPES for _ in range(4)] + [_sds((1, 128), F32), _sds((2, C_RANK), F32)]
    res = pl.pallas_call(body, name="adamw_small", in_specs=[vmem] * n_in, out_specs=[vmem] * len(out_shape),
                         out_shape=out_shape,
                         compiler_params=pltpu.CompilerParams(vmem_limit_bytes=VMEM_LIMIT_V7X))(late_all, *early_all, *flat)
    return [res[4 * p:4 * p + 4] for p in range(8)], res[32], res[33]


def _exchange(name, arrs, gather, deps=()):
    n = len(arrs)
    deps = _live(deps)

    def body(*refs):
        ins, outs = refs[:n], refs[n + len(deps):2 * n + len(deps)]
        send_sems, recv_sems, local_sems = refs[2 * n + len(deps):]
        x, y, c = lax.axis_index("x"), lax.axis_index("y"), lax.axis_index("c")
        me = 4 * x + 2 * y + c

        def peer(k):
            return (x ^ (k >> 2), y ^ ((k >> 1) & 1), c ^ (k & 1))

        def copy(a, k):
            src = ins[a] if gather else ins[a].at[me ^ k]
            return pltpu.make_async_remote_copy(
                src_ref=src, dst_ref=outs[a].at[me], send_sem=send_sems.at[a, k - 1],
                recv_sem=recv_sems.at[a, k - 1], device_id=peer(k), device_id_type=MESH_ID)

        def arrival(a, k):
            src = ins[a] if gather else ins[a].at[me]
            return pltpu.make_async_remote_copy(
                src_ref=src, dst_ref=outs[a].at[me ^ k], send_sem=send_sems.at[a, k - 1],
                recv_sem=recv_sems.at[a, k - 1], device_id=peer(k), device_id_type=MESH_ID)

        own = [pltpu.make_async_copy(ins[a] if gather else ins[a].at[me], outs[a].at[me], local_sems.at[a])
               for a in range(n)]
        for cp in own:
            cp.start()
        for k in range(1, N_DEV):
            for a in range(n):
                copy(a, k).start()
        for k in range(1, N_DEV):
            for a in range(n):
                arrival(a, k).wait_recv()
        for k in range(1, N_DEV):
            for a in range(n):
                copy(a, k).wait_send()
        for cp in own:
            cp.wait()

    any_spec = pl.BlockSpec(memory_space=pl.ANY)
    out_shape = [_sds((N_DEV,) + a.shape if gather else a.shape, a.dtype) for a in arrs]
    return pl.pallas_call(
        body, name=name, in_specs=[any_spec] * (n + len(deps)), out_specs=[any_spec] * n, out_shape=out_shape,
        scratch_shapes=[pltpu.SemaphoreType.DMA((n, N_DEV - 1)), pltpu.SemaphoreType.DMA((n, N_DEV - 1)),
                        pltpu.SemaphoreType.DMA((n,))],
        compiler_params=pltpu.CompilerParams(has_side_effects=True))(*arrs, *deps)


HBM_SPEC = pl.BlockSpec(memory_space=pltpu.HBM)
SEM_SPEC = pl.BlockSpec(memory_space=pltpu.SEMAPHORE)
DATAFLOW = pltpu.SideEffectType.DATAFLOW_SIDE_EFFECTING


def _my_index():
    return 4 * lax.axis_index("x") + 2 * lax.axis_index("y") + lax.axis_index("c")


def _plan_copies(plan, refs, send_sems, recv_sems):
    x, y, c = lax.axis_index("x"), lax.axis_index("y"), lax.axis_index("c")
    copies = []
    for place, entry in enumerate(plan(refs, 4 * x + 2 * y + c)):
        src, dst, k = entry[:3]
        i = entry[3] if len(entry) > 3 else place
        copies.append(pltpu.make_async_remote_copy(
            src_ref=src, dst_ref=dst, send_sem=send_sems.at[i], recv_sem=recv_sems.at[i],
            device_id=(x ^ (k >> 2), y ^ ((k >> 1) & 1), c ^ (k & 1)), device_id_type=MESH_ID))
    return copies


def _split_call(name, bufs, waits=None, starts=None, deps=()):
    n = len(bufs)
    deps = _live(deps)
    n_wait = 2 if waits else 0

    def body(*refs):
        zones = refs[:n]
        if waits:
            for cp in _plan_copies(waits[2], zones, refs[n], refs[n + 1]):
                cp.wait_send()
                cp.wait_recv()
        if starts:
            first_out = n + n_wait + len(deps)
            for cp in _plan_copies(starts[0], zones, refs[first_out], refs[first_out + 1]):
                cp.start()
            refs[-1][...] = jnp.zeros_like(refs[-1])

    out_specs, out_shape = [], []
    if starts:
        sems = pltpu.SemaphoreType.DMA((starts[1],))
        out_specs, out_shape = [SEM_SPEC, SEM_SPEC], [sems, sems]
    out_specs += [HBM_SPEC] * n
    out_shape += [pltpu.HBM(b.shape, b.dtype) for b in bufs]
    if starts:
        out_specs.append(pl.BlockSpec(memory_space=pltpu.VMEM))
        out_shape.append(_sds((8, 128), F32))
    first_buf = 2 if starts else 0
    res = pl.pallas_call(
        body, name=name,
        in_specs=[HBM_SPEC] * n + [SEM_SPEC] * n_wait + [ANY_SPEC] * len(deps),
        out_specs=out_specs, out_shape=out_shape,
        input_output_aliases={i: first_buf + i for i in range(n)},
        compiler_params=pltpu.CompilerParams(has_side_effects=DATAFLOW),
    )(*[pltpu.with_memory_space_constraint(b, pltpu.HBM) for b in bufs], *(waits[:2] if waits else ()), *deps)
    out_bufs = list(res[first_buf:first_buf + n])
    return out_bufs, ((res[0], res[1]) if starts else None), (res[-1] if starts else None)


def _direct_plan(n, gather):
    def plan(refs, me):
        return [(refs[a] if gather else refs[a].at[me ^ k], refs[n + a].at[me], k)
                for k in range(1, N_DEV) for a in range(n)]
    return plan


def _own_slot_filled(a, gather):
    me = _my_index()
    if gather:
        return lax.dynamic_update_slice_in_dim(lax.empty((N_DEV,) + a.shape, a.dtype), a[None], me, 0)
    return lax.dynamic_update_slice_in_dim(lax.empty(a.shape, a.dtype), lax.dynamic_slice_in_dim(a, me, 1, 0), me, 0)


def _exchange_start(name, arrs, gather, deps=()):
    n = len(arrs)
    lands = [_own_slot_filled(a, gather) for a in arrs]
    plan = _direct_plan(n, gather)
    bufs, sems, token = _split_call(name, list(arrs) + lands, starts=(plan, n * (N_DEV - 1)), deps=deps)
    return (n, plan, sems, bufs, None), token


def _exchange_wait(name, handle, after):
    return _split_done(name, handle, after)


ICI_PEERS = (2, 4, 6)
SIBLING = 1


def _gather2_send(name, arrs, deps=()):
    n = len(arrs)
    lands = [_own_slot_filled(a, True) for a in arrs]

    def plan(refs, me_):
        return [(refs[a], refs[n + a].at[me_], k) for k in (SIBLING,) + ICI_PEERS for a in range(n)]

    bufs, sems, token = _split_call(name, list(arrs) + lands, starts=(plan, 4 * n), deps=deps)
    return (n, plan, sems, bufs, None), token


def _gather2_relay(name, handle, after):
    n, plan, sems, bufs, _ = handle
    after = after if isinstance(after, (list, tuple)) else [after]

    def relay(refs, me_):
        return [(refs[n + a].at[me_ ^ k], refs[n + a].at[me_ ^ k], SIBLING) for k in ICI_PEERS for a in range(n)]

    bufs, sems2, token = _split_call(name, bufs, waits=(sems[0], sems[1], plan), starts=(relay, 3 * n), deps=after)
    return (n, relay, sems2, bufs, None), token


def _first_gather_send(name, shard):
    near = _own_slot_filled(shard, True)

    def plan(refs, me):
        return [(refs[0], refs[1].at[me], k) for k in (SIBLING, 2, 4)]

    return _split_call(name, [shard, near], starts=(plan, 3)), plan


def _first_gather_near(names, bufs, sems, plan, after):
    far = lax.empty((2,) + bufs[0].shape, bufs[0].dtype)

    def onward(refs, me):
        hop = 4 - 2 * (me & 1)
        return [(refs[1].at[me ^ 2], refs[1].at[me ^ 2], SIBLING), (refs[1].at[me ^ 4], refs[1].at[me ^ 4], SIBLING),
                (refs[1].at[me ^ hop], refs[2].at[0], 6 - hop)]

    out, sems2, token = _split_call(names[0], list(bufs) + [far], waits=(sems[0], sems[1], plan), starts=(onward, 3),
                                    deps=after)

    def relayed(refs, me):
        return [(refs[1].at[me ^ 2], refs[1].at[me ^ 2], SIBLING, 0), (refs[1].at[me ^ 4], refs[1].at[me ^ 4], SIBLING, 1)]

    near = _split_call(names[1], out[:2], waits=(sems2[0], sems2[1], relayed), deps=[token])[0][1]
    return near, (out[2], sems2)


def _first_gather_far(names, far_handle, after):
    far, sems = far_handle

    def forwarded(refs, me):
        return [(refs[0].at[0], refs[0].at[0], 2 + 2 * (me & 1), 2)]

    def relay(refs, me):
        del me
        return [(refs[0].at[0], refs[0].at[1], SIBLING)]

    bufs, sems2, token = _split_call(names[0], [far], waits=(sems[0], sems[1], forwarded), starts=(relay, 1), deps=[after])
    return _split_call(names[1], bufs, waits=(sems2[0], sems2[1], relay), deps=[token])[0][0]


def _split_done(name, handle, after, all_bufs=False):
    n, plan, sems, bufs, _ = handle
    bufs, _, _ = _split_call(name, bufs, waits=(sems[0], sems[1], plan), deps=[after])
    return bufs if all_bufs else bufs[n:]


def _scatter2_pair(name, for_sibling, deps=()):
    n = len(for_sibling)
    pairs = [lax.empty(s.shape, s.dtype) for s in for_sibling]

    def plan(refs, me):
        del me
        return [(refs[a].at[s], refs[n + a].at[s], SIBLING) for s in range(4) for a in range(n)]

    bufs, sems, token = _split_call(name, list(for_sibling) + pairs, starts=(plan, 4 * n), deps=deps)
    return (n, plan, sems, bufs, None), token


def _pair_add(name, mine, pair):
    _, rows, cols = mine.shape
    tr = rows // 2

    def fn(a, b):
        return a.astype(F32) + b.astype(F32)

    blk = pl.BlockSpec((None, tr, cols), lambda s, i: (s, i, 0))
    return _pure_call(name, fn, (4, rows // tr), [blk, blk], [blk], [_sds(mine.shape, mine.dtype)], (mine, pair))[0]


def _scatter2_send(name, chip_sums, deps=()):
    n = len(chip_sums)
    finals = [lax.empty((3,) + c.shape[1:], c.dtype) for c in chip_sums]

    def plan(refs, me):
        return [(refs[a].at[(me >> 1) ^ j], refs[n + a].at[j - 1], 2 * j) for j in range(1, 4) for a in range(n)]

    bufs, sems, token = _split_call(name, list(chip_sums) + finals, starts=(plan, 3 * n), deps=deps)
    return (n, plan, sems, bufs, None), token


def _pad_rope(p):
    z = jnp.zeros(p.shape[:-1] + (32,), p.dtype)
    return jnp.concatenate([p[..., :32], z, p[..., 32:], z], axis=-1)


def _unpad_rope(p):
    return jnp.concatenate([p[..., :32], p[..., 64:96]], axis=-1)


def _odd_in_layout(wt):
    wt = wt.reshape(ODD_IN, D)
    cq, ckv, kpe, gate = wt[:512], wt[512:1024], wt[1024:1088], wt[1088:]
    z = jnp.zeros((32, D), wt.dtype)
    return jnp.concatenate([gate, cq, ckv, kpe[:32], z, kpe[32:], z], axis=0)


def _odd_in_unlayout(dwt):
    gate, cq, ckv, kpe = dwt[:2048], dwt[2048:2560], dwt[2560:3072], dwt[3072:]
    wt = jnp.concatenate([cq, ckv, kpe[:32], kpe[64:96], gate], axis=0)
    return wt.reshape(N_DEV, ODD_IN // N_DEV, D)


def _qb_layout(w):
    w = w.transpose(1, 0, 2).reshape(C_RANK, C_HEADS, C_QK)
    w = jnp.concatenate([w[..., :C_NOPE], _pad_rope(w[..., C_NOPE:])], axis=-1)
    return w.reshape(C_RANK, C_HEADS * QP)


def _qb_unlayout(dw):
    dw = dw.reshape(C_RANK, C_HEADS, QP)
    dw = jnp.concatenate([dw[..., :C_NOPE], _unpad_rope(dw[..., C_NOPE:])], axis=-1)
    return dw.reshape(C_RANK, N_DEV, C_HEADS * C_QK // N_DEV).transpose(1, 0, 2)


def _rope_tables(positions):
    inv_freq = ROPE_THETA ** (-jnp.arange(0, C_ROPE, 2, dtype=F32) / C_ROPE)
    ang = positions.astype(F32)[0][:, None] * inv_freq
    cos, sin = jnp.cos(ang), jnp.sin(ang)
    z = jnp.zeros_like(cos)
    return jnp.concatenate([cos, z, cos, z], axis=1), jnp.concatenate([-sin, z, sin, z], axis=1)


def _forward_backward(x, cos_t, sin_t, target, norm_pre, norm_post, lb_logits, a_onorm, ln_w, ln_b,
                      b_ws, b_bias, get_w, put_g, put_small=None, start_dep=None):
    npre0, npre1 = norm_pre[0:1], norm_pre[1:2]
    npost0, npost1 = norm_post[0:1], norm_post[1:2]
    l0, l1 = lb_logits[0:1], lb_logits[1:2]
    bias_col = b_bias.reshape(B_GROUPS, B_CHUNK, 1)
    ws = b_ws.reshape(B_GROUPS, B_CHUNK, B_CHUNK)

    h0 = _pre_norm("pre_norm0", x, npre0, deps=[start_dep])
    z0, w_ev_in = get_w("ev_in", h0)(h0)
    cat, sst = _hgrn2_fwd(z0, l0, l1, a_onorm)
    cat = _gmlp_fwd(z0, cat, ln_w, ln_b, ws, bias_col)
    w_ev_out = get_w("ev_out", cat)
    y0 = _mm_nn("ev_out", cat, w_ev_out, F32, 1024, 1024)
    get_w("od_relay", y0)
    x1, h1 = _post_pre_norm(x, y0, npost0, npre1)
    w_od_in, w_qb, w_kvb, q_norm, kv_norm = get_w("od_mid", h1)
    z1 = _mm_nt("od_in", h1, w_od_in[None], F32, 1024, 640)
    cqn, ckvn, kp = _mla_pre(z1, q_norm, kv_norm, cos_t, sin_t)
    q = _mm_nn("od_qb", cqn, w_qb[None], F32, 1024, 1024)
    kv = _mm_nn("od_kvb", ckvn, w_kvb, BF16, 1024, 512)
    o, lse, og = _attention_fwd(q, cos_t, sin_t, kv, kp, z1)
    w_od_out = get_w("od_out", og)
    y1 = _mm_nn("od_out", og, w_od_out, F32, 1024, 1024)
    dx2, dy1, loss_part, dnpost1 = _final_loss(x1, y1, npost1, target)

    g_od_out = _mm_tn("od_out_dw", og, dy1, 1, BF16, 1024, 1024)
    tok = put_g("od_out", [g_od_out.reshape(N_DEV, D // N_DEV, D)])
    dog = _mm_nt("od_out_dx", dy1, w_od_out, F32, 1024, 1024, deps=[tok])
    dq, dkv, dkp, dgate = _attention_bwd(q, cos_t, sin_t, kv, kp, o, lse, dog, z1)
    g_qb = _mm_tn("od_qb_dw", cqn, dq, 1, F32, 512, 1024)
    g_kvb = _mm_tn("od_kvb_dw", ckvn, dkv, N_DEV, BF16, 512, 512)
    tok = put_g("od_qkv", [_qb_unlayout(g_qb[0]).astype(BF16), g_kvb])
    dcqn = _mm_nt("od_qb_dx", dq, w_qb[None], F32, 1024, 512, deps=[tok])
    dckvn = _mm_nt("od_kvb_dx", dkv, w_kvb, F32, 1024, 512)
    dcq, dckv, dkpe, dqn, dkvn = _mla_pre_bwd(z1, q_norm, kv_norm, cos_t, sin_t, dcqn, dckvn, dkp)
    dz1 = jnp.concatenate([dgate, dcq, dckv, dkpe], axis=1)
    g_od_in = _mm_tn("od_in_dw", dz1, h1, 1, F32, 640, 1024)
    tok = put_g("od_in", [_odd_in_unlayout(g_od_in[0]).astype(BF16)])
    dh1 = _mm_nn("od_in_dx", dz1, w_od_in[None], F32, 1024, 1024, deps=[tok])
    dx1, dy0, dnpost0, dnpre1 = _post_pre_norm_bwd(y0, x1, npost0, npre1, dx2, dh1)

    g_ev_out = _mm_tn("ev_out_dw", cat, dy0, 1, BF16, 1024, 1024)
    tok = put_g("ev_out", [g_ev_out.reshape(N_DEV, D // N_DEV, D)])
    dcat = _mm_nt("ev_out_dx", dy0, w_ev_out, F32, 1024, 1024, deps=[tok])
    dqa, dfa, dia, dga, dl0, dl1, donorm = _hgrn2_bwd(z0, l0, l1, a_onorm, sst, dcat)
    dub, dvb, dgb, dlnw, dlnb, dws, dbias = _gmlp_bwd(z0, ln_w, ln_b, ws, bias_col, dcat)
    dz0 = jnp.concatenate([dqa, dfa, dia, dga, dub, dvb, dgb], axis=1)
    early = _small_rows(dnpre1, dnpost0, dnpost1, dl0, dl1, donorm, dlnw, dlnb, dws, dbias, dqn, dkvn, loss_part)
    tok = put_small(early) if put_small else None
    small_tok = tok

    def ev_in_half(name, parity, deps=()):
        return _mm_tn_parity(name, h0, dz0, N_DEV, parity, BF16, 1024, deps=[small_tok] + list(deps))

    tok = put_g("ev_in", ev_in_half)
    dh0 = _mm_nt("ev_in_dx", dz0, w_ev_in, F32, 1024, 256, deps=[tok])
    grad_x, dnpre0 = _pre_norm_bwd(x, npre0, dh0, dx1)
    return grad_x, early, dnpre0


def kernel(x, positions, norm_pre, norm_post, ev_w_in, ev_lb_logits, ev_a_onorm, ev_b_ln_w, ev_b_ln_b, ev_b_ws, ev_b_bias, ev_w_out, od_w_in, od_q_norm, od_w_qb, od_kv_norm, od_w_kvb, od_w_out, loss_target, m_norm_pre, m_norm_post, m_ev_w_in, m_ev_lb_logits, m_ev_a_onorm, m_ev_b_ln_w, m_ev_b_ln_b, m_ev_b_ws, m_ev_b_bias, m_ev_w_out, m_od_w_in, m_od_q_norm, m_od_w_qb, m_od_kv_norm, m_od_w_kvb, m_od_w_out, v_norm_pre, v_norm_post, v_ev_w_in, v_ev_lb_logits, v_ev_a_onorm, v_ev_b_ln_w, v_ev_b_ln_b, v_ev_b_ws, v_ev_b_bias, v_ev_w_out, v_od_w_in, v_od_q_norm, v_od_w_qb, v_od_kv_norm, v_od_w_kvb, v_od_w_out):
    me = 4 * lax.axis_index("x") + 2 * lax.axis_index("y") + lax.axis_index("c")
    bf = lambda w: w[0].astype(BF16)

    norms = jnp.pad(jnp.concatenate([od_q_norm, od_kv_norm], axis=1), ((0, 7), (0, 0)))
    sent = {}
    (first_bufs, first_sems, tok), first_plan = _first_gather_send("gather_ev_in", bf(ev_w_in))
    sent["ev_out"], tok = _gather2_send("gather_ev_out", [bf(ev_w_out)], deps=[tok])
    sent["od"], tok = _gather2_send("gather_od", [od_w_in[0].T.astype(BF16), bf(od_w_qb), bf(od_w_kvb), norms,
                                                 bf(od_w_out)], deps=[tok])
    cos_t, sin_t = _rope_tables(positions)
    od = []

    def get_w(group, after):
        if group == "ev_in":
            def first_matmul(h0):
                near, far_handle = _first_gather_near(("relay_ev_in", "arrived_ev_in"), first_bufs, first_sems,
                                                      first_plan, [after, cos_t, sin_t])
                ids = jnp.stack([me, me ^ 1, me ^ 2, me ^ 4, me ^ 3, me ^ 5]).astype(jnp.int32)
                z0 = _mm_nn_some("ev_in", h0, near, ids, ids, N_DEV, F32, 1024)
                far = _first_gather_far(("relay_ev_in_far", "arrived_ev_in_far"), far_handle, z0)
                far_ids = jnp.stack([me ^ 6, me ^ 7]).astype(jnp.int32)
                z0 = _mm_nn_some("ev_in_far", h0, far, jnp.arange(2, dtype=jnp.int32), far_ids, N_DEV, F32, 1024,
                                 prev=z0)
                full = lax.dynamic_update_slice_in_dim(near, far[0:1], me ^ 6, 0)
                return z0, lax.dynamic_update_slice_in_dim(full, far[1:2], me ^ 7, 0)
            return first_matmul
        if group == "ev_out":
            relayed, token = _gather2_relay("relay_ev_out", sent["ev_out"], after)
            return _split_done("arrived_ev_out", relayed, token)[0].reshape(1, D, D)
        if group == "od_relay":
            sent["od_relayed"], _ = _gather2_relay("relay_od", sent["od"], after)
            return None
        if not od:
            od.extend(_split_done("arrived_od", sent["od_relayed"], after))
        w_od_in, w_qb, w_kvb, norms_all, w_od_out = od
        if group == "od_out":
            return w_od_out.reshape(1, D, D)
        return (_odd_in_layout(w_od_in), _qb_layout(w_qb), w_kvb,
                norms_all[:, 0, :64].reshape(1, C_RANK), norms_all[:, 0, 64:].reshape(1, C_RANK))

    scatters = {}

    def put_g(group, grads):
        if group == "ev_in":
            core = lax.axis_index("c").astype(jnp.int32).reshape(1)
            paired, token = _scatter2_pair("pair_ev_in", [grads("ev_in_dw_sibling", 1 - core)])
            mine = grads("ev_in_dw_own", core, deps=[token])
            pair = _split_done("paired_ev_in", paired, mine)[0]
            scatters[group], token = _scatter2_send("scatter_ev_in", [_pair_add("pair_add_ev_in", mine, pair)])
        else:
            scatters[group], token = _exchange_start("scatter_" + group, grads, False)
        return token

    def put_small(early):
        scatters["small"], token = _exchange_start("gather_small_early", early, True)
        return token

    grad_x, _, dnpre0 = _forward_backward(
        x[0], cos_t, sin_t, loss_target[0], norm_pre, norm_post, ev_lb_logits, ev_a_onorm, ev_b_ln_w,
        ev_b_ln_b, ev_b_ws, ev_b_bias, get_w, put_g, put_small, start_dep=tok)

    big_w = {"ev_w_in": ev_w_in, "ev_w_out": ev_w_out, "od_w_in": od_w_in, "od_w_qb": od_w_qb,
             "od_w_kvb": od_w_kvb, "od_w_out": od_w_out}
    big_m = {"ev_w_in": m_ev_w_in, "ev_w_out": m_ev_w_out, "od_w_in": m_od_w_in, "od_w_qb": m_od_w_qb,
             "od_w_kvb": m_od_w_kvb, "od_w_out": m_od_w_out}
    big_v = {"ev_w_in": v_ev_w_in, "ev_w_out": v_ev_w_out, "od_w_in": v_od_w_in, "od_w_qb": v_od_w_qb,
             "od_w_kvb": v_od_w_kvb, "od_w_out": v_od_w_out}
    big_out = {}
    after = grad_x
    for group, names in (("od_out", ["od_w_out"]), ("od_qkv", ["od_w_qb", "od_w_kvb"]), ("od_in", ["od_w_in"]),
                         ("ev_out", ["ev_w_out"])):
        parts = _exchange_wait("summed_" + group, scatters[group], after)
        for nm, p in zip(names, parts):
            w, m, v = big_w[nm][0], big_m[nm][0], big_v[nm][0]
            if nm == "od_w_in":
                res_t = _adamw("adamw_" + nm, [(p, N_DEV)], w.T, m.T, v.T, w.shape[1], 512)
                big_out[nm] = [r.T[None] for r in res_t]
            else:
                big_out[nm] = [r[None] for r in _adamw("adamw_" + nm, [(p, N_DEV)], w, m, v, w.shape[0] // 8)]
            after = big_out[nm][0]

    late_all = _exchange("gather_small_late", [dnpre0], gather=True, deps=[after])[0]
    early_all = _exchange_wait("arrived_small_early", scatters["small"], late_all)

    small_w = (norm_pre, norm_post, ev_lb_logits, ev_a_onorm, ev_b_ln_w, ev_b_ln_b, ev_b_ws, ev_b_bias)
    small_m = (m_norm_pre, m_norm_post, m_ev_lb_logits, m_ev_a_onorm, m_ev_b_ln_w, m_ev_b_ln_b, m_ev_b_ws, m_ev_b_bias)
    small_v = (v_norm_pre, v_norm_post, v_ev_lb_logits, v_ev_a_onorm, v_ev_b_ln_w, v_ev_b_ln_b, v_ev_b_ws, v_ev_b_bias)
    wmv = [tuple(a.reshape(s) for a in t) for s, t in zip(SMALL_PARAM_SHAPES, zip(small_w, small_m, small_v))]
    small_res, loss_row, g_norm_rows = _adamw_small(late_all, early_all, wmv)
    small_out = [[r.reshape(w.shape) for r in four] for four, w in zip(small_res, small_w)]
    loss = loss_row[0, 0]

    g_norms = jnp.concatenate([lax.dynamic_slice(g_norm_rows, (0, 64 * me), (1, 64)),
                               lax.dynamic_slice(g_norm_rows, (1, 64 * me), (1, 64))], axis=1)
    res_n = _adamw("adamw_norms", [(g_norms[None], 1)],
                   jnp.concatenate([od_q_norm, od_kv_norm], axis=1),
                   jnp.concatenate([m_od_q_norm, m_od_kv_norm], axis=1),
                   jnp.concatenate([v_od_q_norm, v_od_kv_norm], axis=1), 1)
    qn_out = [r[:, :64] for r in res_n]
    kvn_out = [r[:, 64:] for r in res_n]

    chip_sums, from_peers = _split_done("summed_ev_in", scatters["ev_in"], loss_row, all_bufs=True)
    own_chip = lax.dynamic_slice_in_dim(chip_sums, me >> 1, 1, 0)
    w = ev_w_in[0]
    big_out["ev_w_in"] = [r[None] for r in _adamw("adamw_ev_w_in", [(own_chip, 1), (from_peers, 3)], w, m_ev_w_in[0],
                                                  v_ev_w_in[0], w.shape[0] // 8)]

    order = ("norm_pre", "norm_post", "ev_w_in", "ev_lb_logits", "ev_a_onorm", "ev_b_ln_w", "ev_b_ln_b",
             "ev_b_ws", "ev_b_bias", "ev_w_out", "od_w_in", "od_q_norm", "od_w_qb", "od_kv_norm",
             "od_w_kvb", "od_w_out")
    small_names = ("norm_pre", "norm_post", "ev_lb_logits", "ev_a_onorm", "ev_b_ln_w", "ev_b_ln_b",
                   "ev_b_ws", "ev_b_bias")
    outs = [loss, grad_x[None]]
    for kind in range(4):
        for nm in order:
            if nm in big_out:
                outs.append(big_out[nm][kind])
            elif nm == "od_q_norm":
                outs.append(qn_out[kind])
            elif nm == "od_kv_norm":
                outs.append(kvn_out[kind])
            else:
                outs.append(small_out[small_names.index(nm)][kind])
    return tuple(outs)
```

```python
import functools

import jax
import jax.numpy as jnp
from jax import lax
from jax.experimental import pallas as pl
from jax.experimental.pallas import tpu as pltpu

F32 = jnp.float32
BF16 = jnp.bfloat16

N_DEV = 8
T = 2048
D = 2048
EPS = 1e-6
A_HEADS = 8
HD = 128
A_CHUNK = 64
A_SUB = 16
B_GROUPS = 8
B_CHUNK = 128
EVEN_IN = 7168
C_HEADS = 16
C_RANK = 512
C_NOPE = 128
C_ROPE = 64
C_QK = C_NOPE + C_ROPE
C_V = 128
ODD_IN = 3136
ODD_IN_PAD = 3200
QP = 256
ROPE_THETA = 10000.0
ATT_SCALE = C_QK ** -0.5

ADAM_LR = 0.001
ADAM_B1 = 0.9
ADAM_B2 = 0.999
ADAM_EPS = 1e-08
ADAM_WD = 0.01
ADAM_STEP = 10

VMEM_LIMIT_V7X = 56 * 1024 * 1024
MESH_ID = pl.DeviceIdType.MESH


def _params(n_grid):
    return pltpu.CompilerParams(dimension_semantics=("arbitrary",) * n_grid,
                                vmem_limit_bytes=VMEM_LIMIT_V7X)


def _dg(a, b, ca, cb):
    return lax.dot_general(a.astype(BF16), b.astype(BF16), (((ca,), (cb,)), ((), ())),
                           preferred_element_type=F32)


def _raw_nn(a, b):
    return _dg(a, b, 1, 0)


def _raw_nt(a, b):
    return _dg(a, b, 1, 1)


def _raw_tn(a, b):
    return _dg(a, b, 0, 0)


@jax.custom_vjp
def _dot_nn(a, b):
    return _raw_nn(a, b)


def _dot_nn_fwd(a, b):
    return _raw_nn(a, b), (a.astype(BF16), b.astype(BF16))


def _dot_nn_bwd(res, g):
    a, b = res
    return _raw_nt(g, b), _raw_tn(a, g)


_dot_nn.defvjp(_dot_nn_fwd, _dot_nn_bwd)


@jax.custom_vjp
def _dot_nt(a, b):
    return _raw_nt(a, b)


def _dot_nt_fwd(a, b):
    return _raw_nt(a, b), (a.astype(BF16), b.astype(BF16))


def _dot_nt_bwd(res, g):
    a, b = res
    return _raw_nn(g, b), _raw_tn(g, a)


_dot_nt.defvjp(_dot_nt_fwd, _dot_nt_bwd)


@jax.custom_vjp
def _dot_tn(a, b):
    return _raw_tn(a, b)


def _dot_tn_fwd(a, b):
    return _raw_tn(a, b), (a.astype(BF16), b.astype(BF16))


def _dot_tn_bwd(res, g):
    a, b = res
    return _raw_nt(b, g), _raw_nn(a, g)


_dot_tn.defvjp(_dot_tn_fwd, _dot_tn_bwd)


@jax.custom_vjp
def _sigmoid(x):
    e = jnp.exp(-jnp.abs(x))
    return jnp.where(x >= 0, 1.0 / (1.0 + e), e / (1.0 + e))


def _sigmoid_fwd(x):
    s = _sigmoid(x)
    return s, s


def _sigmoid_bwd(s, g):
    return (g * s * (1.0 - s),)


_sigmoid.defvjp(_sigmoid_fwd, _sigmoid_bwd)


def _silu(x):
    return x * _sigmoid(x)


def _rms(x, w):
    return x * lax.rsqrt(jnp.mean(x * x, axis=-1, keepdims=True) + EPS) * w


def _split3(x):
    hi = x.astype(BF16)
    r = x - hi.astype(F32)
    mid = r.astype(BF16)
    lo = (r - mid.astype(F32)).astype(BF16)
    return hi, mid, lo


def _mask_apply(mask_bf16, x, contract):
    out = None
    for piece in _split3(x):
        d = lax.dot_general(mask_bf16, piece, (((contract,), (0,)), ((), ())),
                            preferred_element_type=F32)
        out = d if out is None else out + d
    return out


def _chunk_tri(rows):
    r = lax.broadcasted_iota(jnp.int32, (rows, rows), 0)
    c = lax.broadcasted_iota(jnp.int32, (rows, rows), 1)
    return ((r >= c) & (r // A_CHUNK == c // A_CHUNK)).astype(BF16)


@jax.custom_vjp
def _chunk_cumsum(x):
    return _mask_apply(_chunk_tri(x.shape[0]), x, 1)


def _chunk_cumsum_fwd(x):
    return _chunk_cumsum(x), None


def _chunk_cumsum_bwd(_, g):
    return (_mask_apply(_chunk_tri(g.shape[0]), g, 0),)


_chunk_cumsum.defvjp(_chunk_cumsum_fwd, _chunk_cumsum_bwd)


def _hgrn2_rows(q, zf, v, ga, st, l0, l1, onorm):
    rows = q.shape[0]
    n_sub = A_CHUNK // A_SUB
    mx = jnp.maximum(l0, l1)
    e0 = jnp.exp(l0 - mx)
    e1 = jnp.exp(l1 - mx)
    lb = e0 / (e0 + e1)
    lf = jnp.log(lb + (1.0 - lb) * _sigmoid(zf))
    k = (1.0 - lb) * _sigmoid(-zf)
    b = _chunk_cumsum(lf)

    t_idx = lax.broadcasted_iota(jnp.int32, (A_CHUNK, n_sub * A_CHUNK), 0)
    c_idx = lax.broadcasted_iota(jnp.int32, (A_CHUNK, n_sub * A_CHUNK), 1)
    sel = (c_idx // A_CHUNK == t_idx // A_SUB) & (c_idx % A_CHUNK <= t_idx)
    key_row = lax.broadcasted_iota(jnp.int32, (A_CHUNK, HD), 0)

    outs = []
    for n in range(rows // A_CHUNK):
        lo = n * A_CHUNK
        qc, kc, vc = q[lo:lo + A_CHUNK], k[lo:lo + A_CHUNK], v[lo:lo + A_CHUNK]
        lfc, bc = lf[lo:lo + A_CHUNK], b[lo:lo + A_CHUNK]
        b_last = bc[A_CHUNK - 1:A_CHUNK]
        o_inter = _dot_nt(qc * jnp.exp(bc), st)
        kv_t = _dot_tn(vc, kc * jnp.exp(b_last - bc))
        st = st * jnp.exp(b_last) + kv_t
        g_rows, k_subs = [], []
        for i in range(n_sub):
            g_i = bc[i * A_SUB:i * A_SUB + 1] - lfc[i * A_SUB:i * A_SUB + 1]
            g_rows.append(jnp.broadcast_to(g_i, (A_SUB, HD)))
            expo = jnp.where(key_row < (i + 1) * A_SUB, g_i - bc, -jnp.inf)
            k_subs.append(kc * jnp.exp(expo))
        q_sub = qc * jnp.exp(bc - jnp.concatenate(g_rows, axis=0))
        scores = _dot_nt(q_sub, jnp.concatenate(k_subs, axis=0))
        scores = jnp.where(sel, scores, 0.0)
        o_intra = _dot_nn(scores, jnp.concatenate([vc] * n_sub, axis=0))
        outs.append(o_inter + o_intra)
    o = jnp.concatenate(outs, axis=0)
    return _rms(o, onorm) * _silu(ga), st


def _gmlp_rows(u, vb, gb, lnw, lnb, ws, bias):
    rows = u.shape[0]
    mu = jnp.mean(vb, axis=-1, keepdims=True)
    xc = vb - mu
    vg = xc * lax.rsqrt(jnp.mean(xc * xc, axis=-1, keepdims=True) + EPS) * lnw + lnb
    r = lax.broadcasted_iota(jnp.int32, (B_CHUNK, B_CHUNK), 0)
    c = lax.broadcasted_iota(jnp.int32, (B_CHUNK, B_CHUNK), 1)
    ws_causal = jnp.where(r >= c, ws, 0.0)
    svs = [_dot_nn(ws_causal, vg[n * B_CHUNK:(n + 1) * B_CHUNK]) + bias
           for n in range(rows // B_CHUNK)]
    return u * jnp.concatenate(svs, axis=0) * _silu(gb)


def _rope(x, cos_t, sin_t):
    return x * cos_t + pltpu.roll(x, 64, 1) * sin_t


def _rope_transpose(g, cos_t, sin_t):
    return g * cos_t + pltpu.roll(g * sin_t, 64, 1)


ANY_SPEC = pl.BlockSpec(memory_space=pl.ANY)


def _live(deps):
    return [d for d in deps if d is not None]


def _skip_deps(body, n_in, n_deps):
    def wrapped(*refs):
        return body(*refs[:n_in], *refs[n_in + n_deps:])
    return wrapped


def _pure_call(name, fn, grid, in_specs, out_specs, out_shape, args, n_acc=0, deps=()):
    deps = _live(deps)
    n_in, n_out, n_deps = len(in_specs), len(out_specs), len(deps)
    in_specs = list(in_specs) + [ANY_SPEC] * n_deps
    args = tuple(args) + tuple(deps)

    def body(*refs):
        res = fn(*[r[...] for r in refs[:n_in]])
        if not isinstance(res, (tuple, list)):
            res = (res,)
        outs = refs[n_in + n_deps:n_in + n_deps + n_out]
        for o, r in zip(outs[:n_out - n_acc], res[:n_out - n_acc]):
            o[...] = r.astype(o.dtype)
        if n_acc:
            first = functools.reduce(jnp.logical_and, [pl.program_id(i) == 0 for i in range(len(grid))])
            for o, r in zip(outs[n_out - n_acc:], res[n_out - n_acc:]):
                @pl.when(first)
                def _(o=o, r=r):
                    o[...] = r.astype(o.dtype)

                @pl.when(jnp.logical_not(first))
                def _(o=o, r=r):
                    o[...] += r.astype(o.dtype)

    return pl.pallas_call(body, name=name, grid=grid, in_specs=in_specs, out_specs=out_specs,
                          out_shape=out_shape, compiler_params=_params(len(grid)))(*args)


def _sds(shape, dtype):
    return jax.ShapeDtypeStruct(shape, dtype)


def _row_spec(tm, width, col=0):
    return pl.BlockSpec((tm, width), lambda i, col=col: (i, col))


def _full_spec(shape):
    nd = len(shape)
    return pl.BlockSpec(shape, lambda *_: (0,) * nd)


def _mm_nn(name, a, b, out_dtype, tm, tn, deps=()):
    deps = _live(deps)
    m, k = a.shape
    j, _, n = b.shape
    per = n // tn

    def body(a_ref, b_ref, o_ref):
        o_ref[...] = _raw_nn(a_ref[...], b_ref[...]).astype(o_ref.dtype)

    return pl.pallas_call(
        _skip_deps(body, 2, len(deps)), name=name, grid=(m // tm, j * per),
        in_specs=[pl.BlockSpec((tm, k), lambda i, c: (i, 0)),
                  pl.BlockSpec((None, k, tn), lambda i, c: (c // per, 0, c % per))] + [ANY_SPEC] * len(deps),
        out_specs=pl.BlockSpec((tm, tn), lambda i, c: (i, c)),
        out_shape=_sds((m, j * n), out_dtype), compiler_params=_params(2))(a, b, *deps)


def _mm_nn_some(name, a, b, slots, cols, n_cols, out_dtype, tm, prev=None):
    m, k = a.shape
    n = b.shape[2]
    n_sel = slots.shape[0]

    def body(slots_ref, cols_ref, a_ref, b_ref, *rest):
        del slots_ref, cols_ref
        rest[-1][...] = _raw_nn(a_ref[...], b_ref[...]).astype(rest[-1].dtype)

    grid_spec = pltpu.PrefetchScalarGridSpec(
        num_scalar_prefetch=2, grid=(m // tm, n_sel),
        in_specs=[pl.BlockSpec((tm, k), lambda i, s, sl, cl: (i, 0)),
                  pl.BlockSpec((None, k, n), lambda i, s, sl, cl: (sl[s], 0, 0))] + ([ANY_SPEC] if prev is not None else []),
        out_specs=pl.BlockSpec((tm, n), lambda i, s, sl, cl: (i, cl[s])))
    return pl.pallas_call(
        body, name=name, grid_spec=grid_spec, out_shape=_sds((m, n_cols * n), out_dtype),
        input_output_aliases={4: 0} if prev is not None else {},
        compiler_params=_params(2))(slots, cols, a, b, *([prev] if prev is not None else []))


def _mm_nt(name, a, b, out_dtype, tm, tn, deps=()):
    deps = _live(deps)
    m = a.shape[0]
    j, nn, n = b.shape

    def body(a_ref, b_ref, o_ref):
        b_all = b_ref[0] if j == 1 else jnp.concatenate([b_ref[s] for s in range(j)], axis=1)
        o_ref[...] = _raw_nt(a_ref[...], b_all).astype(o_ref.dtype)

    return pl.pallas_call(
        _skip_deps(body, 2, len(deps)), name=name, grid=(m // tm, nn // tn),
        in_specs=[pl.BlockSpec((tm, j * n), lambda i, c: (i, 0)),
                  pl.BlockSpec((j, tn, n), lambda i, c: (0, c, 0))] + [ANY_SPEC] * len(deps),
        out_specs=pl.BlockSpec((tm, tn), lambda i, c: (i, c)),
        out_shape=_sds((m, nn), out_dtype), compiler_params=_params(2))(a, b, *deps)


def _mm_tn(name, a, b, j, out_dtype, tm, tn, deps=()):
    deps = _live(deps)
    k, m = a.shape
    n = b.shape[1] // j
    per = n // tn

    def body(a_ref, b_ref, o_ref):
        o_ref[...] = _raw_tn(a_ref[...], b_ref[...]).astype(o_ref.dtype)

    return pl.pallas_call(
        _skip_deps(body, 2, len(deps)), name=name, grid=(m // tm, j * per),
        in_specs=[pl.BlockSpec((k, tm), lambda i, c: (0, i)),
                  pl.BlockSpec((k, tn), lambda i, c: (0, c))] + [ANY_SPEC] * len(deps),
        out_specs=pl.BlockSpec((None, tm, tn), lambda i, c: (c // per, i, c % per)),
        out_shape=_sds((j, m, n), out_dtype), compiler_params=_params(2))(a, b, *deps)


def _mm_tn_parity(name, a, b, j, parity, out_dtype, tm, deps=()):
    deps = _live(deps)
    k, m = a.shape
    n = b.shape[1] // j

    def body(par_ref, a_ref, b_ref, o_ref):
        del par_ref
        o_ref[...] = _raw_tn(a_ref[...], b_ref[...]).astype(o_ref.dtype)

    grid_spec = pltpu.PrefetchScalarGridSpec(
        num_scalar_prefetch=1, grid=(m // tm, j // 2),
        in_specs=[pl.BlockSpec((k, tm), lambda i, s, par: (0, i)),
                  pl.BlockSpec((k, n), lambda i, s, par: (0, 2 * s + par[0]))] + [ANY_SPEC] * len(deps),
        out_specs=pl.BlockSpec((None, tm, n), lambda i, s, par: (s, i, 0)))
    return pl.pallas_call(
        lambda par_ref, *refs: _skip_deps(functools.partial(body, par_ref), 2, len(deps))(*refs),
        name=name, grid_spec=grid_spec, out_shape=_sds((j // 2, m, n), out_dtype),
        compiler_params=_params(2))(parity, a, b, *deps)


TM = 256


def _pre_norm(name, x, w_row, deps=()):
    def fn(xv, w):
        return _rms(xv, w)
    return _pure_call(name, fn, (T // TM,), [_row_spec(TM, D), _full_spec((1, D))],
                      [_row_spec(TM, D)], [_sds((T, D), BF16)], (x, w_row), deps=deps)[0]


def _post_pre_norm(x, y, w_post, w_pre):
    def fn(xv, yv, wp, wn):
        x1 = xv + _rms(yv, wp)
        return x1, _rms(x1, wn)
    return _pure_call("post_pre_norm", fn, (T // TM,),
                      [_row_spec(TM, D), _row_spec(TM, D), _full_spec((1, D)), _full_spec((1, D))],
                      [_row_spec(TM, D), _row_spec(TM, D)],
                      [_sds((T, D), F32), _sds((T, D), BF16)], (x, y, w_post, w_pre))


def _post_pre_norm_bwd(y, x1, w_post, w_pre, dx1_in, dh1, deps=()):
    def fn(yv, x1v, wp, wn, dx1v, dh1v):
        _, vjp_pre = jax.vjp(_rms, x1v, wn)
        dx1_h, dwn = vjp_pre(dh1v)
        dx1 = dx1v + dx1_h
        _, vjp_post = jax.vjp(_rms, yv, wp)
        dy, dwp = vjp_post(dx1)
        return dx1, dy, dwp, dwn
    return _pure_call("post_pre_norm_bwd", fn, (T // TM,),
                      [_row_spec(TM, D), _row_spec(TM, D), _full_spec((1, D)), _full_spec((1, D)),
                       _row_spec(TM, D), _row_spec(TM, D)],
                      [_row_spec(TM, D), _row_spec(TM, D), _full_spec((1, D)), _full_spec((1, D))],
                      [_sds((T, D), F32), _sds((T, D), BF16), _sds((1, D), F32), _sds((1, D), F32)],
                      (y, x1, w_post, w_pre, dx1_in, dh1), n_acc=2, deps=deps)


def _final_loss(x1, y, w_post, target):
    def fn(x1v, yv, wp, tv):
        r, vjp = jax.vjp(_rms, yv, wp)
        err = x1v + r - tv
        part = 0.5 * jnp.sum(jnp.mean(err * err, axis=-1, keepdims=True), axis=0, keepdims=True)
        dx2 = err * (1.0 / D)
        dy, dwp = vjp(dx2)
        return dx2, dy, jnp.broadcast_to(part, (1, 128)), dwp
    return _pure_call("final_loss", fn, (T // TM,),
                      [_row_spec(TM, D), _row_spec(TM, D), _full_spec((1, D)), _row_spec(TM, D)],
                      [_row_spec(TM, D), _row_spec(TM, D), _full_spec((1, 128)), _full_spec((1, D))],
                      [_sds((T, D), F32), _sds((T, D), BF16), _sds((1, 128), F32), _sds((1, D), F32)],
                      (x1, y, w_post, target), n_acc=2)


def _pre_norm_bwd(x, w_row, dh, dx_res, deps=()):
    def fn(xv, w, dhv, dxv):
        _, vjp = jax.vjp(_rms, xv, w)
        dx, dw = vjp(dhv)
        return dxv + dx, dw
    return _pure_call("pre_norm_bwd", fn, (T // TM,),
                      [_row_spec(TM, D), _full_spec((1, D)), _row_spec(TM, D), _row_spec(TM, D)],
                      [_row_spec(TM, D), _full_spec((1, D))],
                      [_sds((T, D), F32), _sds((1, D), F32)], (x, w_row, dh, dx_res), n_acc=1, deps=deps)


RA = 256
RB = 512


HA = 4
A_GROUPS = A_HEADS // HA


def _head(ref, hh):
    return ref[:, hh * HD:(hh + 1) * HD]


def _hgrn2_fwd(z, l0, l1, onorm):
    nb = T // RA

    def body(q_ref, f_ref, v_ref, g_ref, l0_ref, l1_ref, on_ref, cat_ref, sst_ref, st_scr):
        @pl.when(pl.program_id(1) == 0)
        def _():
            st_scr[...] = jnp.zeros_like(st_scr)

        for hh in range(HA):
            st = st_scr[hh]
            sst_ref[hh] = st
            out, st_new = _hgrn2_rows(_head(q_ref, hh), _head(f_ref, hh), _head(v_ref, hh), _head(g_ref, hh), st,
                                      _head(l0_ref, hh), _head(l1_ref, hh), on_ref[...])
            cat_ref[:, hh * HD:(hh + 1) * HD] = out.astype(cat_ref.dtype)
            st_scr[hh] = st_new

    def cols(k):
        return pl.BlockSpec((RA, HA * HD), lambda g, r: (r, k * A_GROUPS + g))

    vec = pl.BlockSpec((1, HA * HD), lambda g, r: (0, g))
    return pl.pallas_call(
        body, name="hgrn2_fwd", grid=(A_GROUPS, nb),
        in_specs=[cols(0), cols(1), cols(2), cols(3), vec, vec, _full_spec((1, HD))],
        out_specs=[cols(0), pl.BlockSpec((HA, None, HD, HD), lambda g, r: (g, r, 0, 0))],
        out_shape=[_sds((T, 2 * A_HEADS * HD), BF16), _sds((A_HEADS, nb, HD, HD), F32)],
        scratch_shapes=[pltpu.VMEM((HA, HD, HD), F32)],
        compiler_params=_params(2))(z, z, z, z, l0, l1, onorm)


def _hgrn2_bwd(z, l0, l1, onorm, sst, dcat, deps=()):
    nb = T // RA
    deps = _live(deps)

    def body(q_ref, f_ref, v_ref, g_ref, l0_ref, l1_ref, on_ref, sst_ref, dcat_ref,
             dq_ref, df_ref, dv_ref, dg_ref, dl0_ref, dl1_ref, don_ref, ds_scr):
        g, r = pl.program_id(0), pl.program_id(1)

        @pl.when(r == 0)
        def _():
            ds_scr[...] = jnp.zeros_like(ds_scr)

        dl0s, dl1s, don = [], [], None
        for hh in range(HA):
            _, vjp = jax.vjp(_hgrn2_rows, _head(q_ref, hh), _head(f_ref, hh), _head(v_ref, hh), _head(g_ref, hh),
                             sst_ref[hh], _head(l0_ref, hh), _head(l1_ref, hh), on_ref[...])
            dq, dzf, dv, dga, dst, dl0, dl1, don_h = vjp((_head(dcat_ref, hh), ds_scr[hh]))
            for ref, val in ((dq_ref, dq), (df_ref, dzf), (dv_ref, dv), (dg_ref, dga)):
                ref[:, hh * HD:(hh + 1) * HD] = val.astype(ref.dtype)
            ds_scr[hh] = dst
            dl0s.append(dl0)
            dl1s.append(dl1)
            don = don_h if don is None else don + don_h
        dl0 = jnp.concatenate(dl0s, axis=1)
        dl1 = jnp.concatenate(dl1s, axis=1)

        @pl.when(r == 0)
        def _():
            dl0_ref[...] = dl0
            dl1_ref[...] = dl1

        @pl.when(r > 0)
        def _():
            dl0_ref[...] += dl0
            dl1_ref[...] += dl1

        first = jnp.logical_and(g == 0, r == 0)

        @pl.when(first)
        def _():
            don_ref[...] = don

        @pl.when(jnp.logical_not(first))
        def _():
            don_ref[...] += don

    def rev(k):
        return pl.BlockSpec((RA, HA * HD), lambda g, r: (nb - 1 - r, k * A_GROUPS + g))

    vec = pl.BlockSpec((1, HA * HD), lambda g, r: (0, g))
    grad = _sds((T, A_HEADS * HD), BF16)
    return pl.pallas_call(
        _skip_deps(body, 9, len(deps)), name="hgrn2_bwd", grid=(A_GROUPS, nb),
        in_specs=[rev(0), rev(1), rev(2), rev(3), vec, vec, _full_spec((1, HD)),
                  pl.BlockSpec((HA, None, HD, HD), lambda g, r: (g, nb - 1 - r, 0, 0)),
                  rev(0)] + [ANY_SPEC] * len(deps),
        out_specs=[rev(0)] * 4 + [vec, vec, _full_spec((1, HD))],
        out_shape=[grad] * 4 + [_sds((1, A_HEADS * HD), F32)] * 2 + [_sds((1, HD), F32)],
        scratch_shapes=[pltpu.VMEM((HA, HD, HD), F32)],
        compiler_params=_params(2))(z, z, z, z, l0, l1, onorm, sst, dcat, *deps)


GB = 4
B_STEPS = B_GROUPS // GB


def _gmlp_specs():
    vec = pl.BlockSpec((1, GB * HD), lambda s, r: (0, s))
    ws = pl.BlockSpec((GB, B_CHUNK, B_CHUNK), lambda s, r: (s, 0, 0))
    bias = pl.BlockSpec((GB, B_CHUNK, 1), lambda s, r: (s, 0, 0))

    def cols(k):
        return pl.BlockSpec((RB, GB * HD), lambda s, r: (r, k * B_STEPS + s))
    return vec, ws, bias, cols


def _gmlp_fwd(z, cat, lnw, lnb, ws, bias):
    vec, ws_spec, bias_spec, cols = _gmlp_specs()

    def body(u_ref, v_ref, g_ref, lnw_ref, lnb_ref, ws_ref, bias_ref, cat_in_ref, cat_ref):
        del cat_in_ref
        for gg in range(GB):
            out = _gmlp_rows(_head(u_ref, gg), _head(v_ref, gg), _head(g_ref, gg), _head(lnw_ref, gg),
                             _head(lnb_ref, gg), ws_ref[gg], bias_ref[gg])
            cat_ref[:, gg * HD:(gg + 1) * HD] = out.astype(cat_ref.dtype)

    return pl.pallas_call(
        body, name="gmlp_fwd", grid=(B_STEPS, T // RB),
        in_specs=[cols(4), cols(5), cols(6), vec, vec, ws_spec, bias_spec, pl.BlockSpec(memory_space=pl.ANY)],
        out_specs=cols(1),
        out_shape=_sds(cat.shape, cat.dtype), input_output_aliases={7: 0},
        compiler_params=_params(2))(z, z, z, lnw, lnb, ws, bias, cat)


def _gmlp_bwd(z, lnw, lnb, ws, bias, dcat):
    vec, ws_spec, bias_spec, cols = _gmlp_specs()

    def body(u_ref, v_ref, g_ref, lnw_ref, lnb_ref, ws_ref, bias_ref, dcat_ref,
             du_ref, dv_ref, dg_ref, dlnw_ref, dlnb_ref, dws_ref, dbias_ref):
        first = pl.program_id(1) == 0
        for gg in range(GB):
            _, vjp = jax.vjp(_gmlp_rows, _head(u_ref, gg), _head(v_ref, gg), _head(g_ref, gg), _head(lnw_ref, gg),
                             _head(lnb_ref, gg), ws_ref[gg], bias_ref[gg])
            du, dv, dg, dlnw, dlnb, dws, dbias = vjp(_head(dcat_ref, gg))
            lanes = slice(gg * HD, (gg + 1) * HD)
            for ref, val in ((du_ref, du), (dv_ref, dv), (dg_ref, dg)):
                ref[:, lanes] = val.astype(ref.dtype)
            sums = ((dlnw_ref, (slice(None), lanes), dlnw), (dlnb_ref, (slice(None), lanes), dlnb),
                    (dws_ref, gg, dws), (dbias_ref, gg, dbias))
            for ref, idx, val in sums:
                @pl.when(first)
                def _(ref=ref, idx=idx, val=val):
                    ref[idx] = val

                @pl.when(jnp.logical_not(first))
                def _(ref=ref, idx=idx, val=val):
                    ref[idx] += val

    grad = _sds((T, B_GROUPS * HD), BF16)
    return pl.pallas_call(
        body, name="gmlp_bwd", grid=(B_STEPS, T // RB),
        in_specs=[cols(4), cols(5), cols(6), vec, vec, ws_spec, bias_spec, cols(1)],
        out_specs=[cols(0)] * 3 + [vec, vec, ws_spec, bias_spec],
        out_shape=[grad] * 3 + [_sds((1, B_GROUPS * HD), F32)] * 2
        + [_sds((B_GROUPS, B_CHUNK, B_CHUNK), F32), _sds((B_GROUPS, B_CHUNK, 1), F32)],
        compiler_params=_params(2))(z, z, z, lnw, lnb, ws, bias, dcat)


def _mla_pre(z1, qn, kvn, cos_t, sin_t):
    def fn(cq, ckv, kpe, cs, sn, wq, wkv):
        return _rms(cq, wq), _rms(ckv, wkv), _rope(kpe, cs, sn)
    return _pure_call("mla_pre", fn, (T // TM,),
                      [_row_spec(TM, C_RANK, 4), _row_spec(TM, C_RANK, 5), _row_spec(TM, HD, 24),
                       _row_spec(TM, HD), _row_spec(TM, HD),
                       _full_spec((1, C_RANK)), _full_spec((1, C_RANK))],
                      [_row_spec(TM, C_RANK), _row_spec(TM, C_RANK), _row_spec(TM, HD)],
                      [_sds((T, C_RANK), BF16), _sds((T, C_RANK), BF16), _sds((T, HD), BF16)],
                      (z1, z1, z1, cos_t, sin_t, qn, kvn))


def _mla_pre_bwd(z1, qn, kvn, cos_t, sin_t, dcqn, dckvn, dkp, deps=()):
    def fn(cq, ckv, cs, sn, wq, wkv, g_q, g_kv, g_kp):
        _, vjp_q = jax.vjp(_rms, cq, wq)
        dcq, dwq = vjp_q(g_q)
        _, vjp_kv = jax.vjp(_rms, ckv, wkv)
        dckv, dwkv = vjp_kv(g_kv)
        return dcq, dckv, _rope_transpose(g_kp, cs, sn), dwq, dwkv
    return _pure_call("mla_pre_bwd", fn, (T // TM,),
                      [_row_spec(TM, C_RANK, 4), _row_spec(TM, C_RANK, 5),
                       _row_spec(TM, HD), _row_spec(TM, HD),
                       _full_spec((1, C_RANK)), _full_spec((1, C_RANK)),
                       _row_spec(TM, C_RANK), _row_spec(TM, C_RANK), _row_spec(TM, HD)],
                      [_row_spec(TM, C_RANK), _row_spec(TM, C_RANK), _row_spec(TM, HD),
                       _full_spec((1, C_RANK)), _full_spec((1, C_RANK))],
                      [_sds((T, C_RANK), BF16), _sds((T, C_RANK), BF16), _sds((T, HD), BF16),
                       _sds((1, C_RANK), F32), _sds((1, C_RANK), F32)],
                      (z1, z1, cos_t, sin_t, qn, kvn, dcqn, dckvn, dkp), n_acc=2, deps=deps)


TQ = 256
HP = 2
KVW = C_NOPE + C_V


def _att_keys(kv_ref, kp_ref, k_scr):
    @pl.when(pl.program_id(1) == 0)
    def _():
        for hh in range(HP):
            k_scr[hh, :, 0:C_NOPE] = kv_ref[:, hh * KVW:hh * KVW + C_NOPE]
            k_scr[hh, :, C_NOPE:QP] = kp_ref[...]


def _att_scores(q, cos_ref, sin_ref, k_scr, hh, n):
    keys = (n + 1) * TQ
    qr = jnp.concatenate([q[:, :C_NOPE], _rope(q[:, C_NOPE:], cos_ref[...], sin_ref[...])], axis=1).astype(BF16)
    return qr, _raw_nt(qr, k_scr[hh, 0:keys, :]) * ATT_SCALE


def _causal(x, n, fill):
    row = lax.broadcasted_iota(jnp.int32, (TQ, TQ), 0)
    col = lax.broadcasted_iota(jnp.int32, (TQ, TQ), 1)
    diag = jnp.where(col <= row, x[:, n * TQ:], fill)
    return diag if n == 0 else jnp.concatenate([x[:, :n * TQ], diag], axis=1)


def _per_query_block(fn):
    for n in range(T // TQ):
        pl.when(pl.program_id(1) == n)(functools.partial(fn, n))


def _att_in_specs():
    return [pl.BlockSpec((TQ, HP * QP), lambda g, i: (i, g)),
            pl.BlockSpec((TQ, HD), lambda g, i: (i, 0)),
            pl.BlockSpec((TQ, HD), lambda g, i: (i, 0)),
            pl.BlockSpec((T, HP * KVW), lambda g, i: (0, g)),
            pl.BlockSpec((T, HD), lambda g, i: (0, 0))]


def _attention_fwd(q, cos_t, sin_t, kv, kp, z1):
    def body(q_ref, cos_ref, sin_ref, kv_ref, kp_ref, gate_ref, o_ref, lse_ref, og_ref, k_scr):
        _att_keys(kv_ref, kp_ref, k_scr)

        def block(n):
            keys = (n + 1) * TQ
            for hh in range(HP):
                _, s = _att_scores(q_ref[:, hh * QP:(hh + 1) * QP], cos_ref, sin_ref, k_scr, hh, n)
                s = _causal(s, n, jnp.finfo(F32).min)
                m = jnp.max(s, axis=-1, keepdims=True)
                p = jnp.exp(s - m)
                l = jnp.sum(p, axis=-1, keepdims=True)
                v = kv_ref[0:keys, hh * KVW + C_NOPE:(hh + 1) * KVW]
                o = _raw_nn(p, v) / l
                lanes = slice(hh * C_V, (hh + 1) * C_V)
                o_ref[:, lanes] = o
                og_ref[:, lanes] = (o * _silu(gate_ref[:, lanes])).astype(og_ref.dtype)
                lse_ref[hh] = m + jnp.log(l)

        _per_query_block(block)

    heads = pl.BlockSpec((TQ, HP * C_V), lambda g, i: (i, g))
    return pl.pallas_call(
        body, name="attention_fwd", grid=(C_HEADS // HP, T // TQ), in_specs=_att_in_specs() + [heads],
        out_specs=[heads, pl.BlockSpec((HP, TQ, 1), lambda g, i: (g, i, 0)), heads],
        out_shape=[_sds((T, C_HEADS * C_V), F32), _sds((C_HEADS, T, 1), F32), _sds((T, C_HEADS * C_V), BF16)],
        scratch_shapes=[pltpu.VMEM((HP, T, QP), BF16)],
        compiler_params=_params(2))(q, cos_t, sin_t, kv, kp, z1)


def _attention_bwd(q, cos_t, sin_t, kv, kp, o, lse, dog, z1):
    nq = T // TQ

    def body(q_ref, cos_ref, sin_ref, kv_ref, kp_ref, o_ref, lse_ref, dog_ref, gate_ref,
             dq_ref, dkv_ref, dkp_ref, dgate_ref, k_scr, dk_scr, dv_scr):
        g, i = pl.program_id(0), pl.program_id(1)
        _att_keys(kv_ref, kp_ref, k_scr)

        @pl.when(i == 0)
        def _():
            dv_scr[...] = jnp.zeros_like(dv_scr)
            dk_scr[...] = jnp.zeros_like(dk_scr)

        def block(n):
            keys = (n + 1) * TQ
            for hh in range(HP):
                qr, s = _att_scores(q_ref[:, hh * QP:(hh + 1) * QP], cos_ref, sin_ref, k_scr, hh, n)
                p = _causal(jnp.exp(s - lse_ref[hh]), n, 0.0)
                lanes = slice(hh * C_V, (hh + 1) * C_V)
                ov, gate, dogv = o_ref[:, lanes], gate_ref[:, lanes], dog_ref[:, lanes]
                sig = _sigmoid(gate)
                silu = gate * sig
                dov = dogv * silu
                dgate_ref[:, lanes] = (dogv * ov * (sig + silu * (1.0 - sig))).astype(dgate_ref.dtype)
                delta = jnp.sum(dov * ov, axis=-1, keepdims=True)
                dp = _raw_nt(dov, kv_ref[0:keys, hh * KVW + C_NOPE:(hh + 1) * KVW])
                ds = p * (dp - delta) * ATT_SCALE
                dq = _raw_nn(ds, k_scr[hh, 0:keys, :])
                dq_ref[:, hh * QP:(hh + 1) * QP] = jnp.concatenate(
                    [dq[:, :C_NOPE], _rope_transpose(dq[:, C_NOPE:], cos_ref[...], sin_ref[...])],
                    axis=1).astype(dq_ref.dtype)
                dv_scr[hh, 0:keys, :] += _raw_tn(p, dov)
                dk_scr[hh, 0:keys, :] += _raw_tn(ds, qr)

        _per_query_block(block)

        @pl.when(i == nq - 1)
        def _():
            for hh in range(HP):
                dkv_ref[:, hh * KVW:(hh + 1) * KVW] = jnp.concatenate(
                    [dk_scr[hh, :, 0:C_NOPE], dv_scr[hh]], axis=1).astype(dkv_ref.dtype)

        @pl.when(jnp.logical_and(i == nq - 1, g == 0))
        def _():
            dkp_ref[...] = dk_scr[0, :, C_NOPE:QP]

        @pl.when(jnp.logical_and(i == nq - 1, g > 0))
        def _():
            dkp_ref[...] += dk_scr[0, :, C_NOPE:QP]

        @pl.when(i == nq - 1)
        def _():
            for hh in range(1, HP):
                dkp_ref[...] += dk_scr[hh, :, C_NOPE:QP]

    heads = pl.BlockSpec((TQ, HP * C_V), lambda g, i: (i, g))
    return pl.pallas_call(
        body, name="attention_bwd", grid=(C_HEADS // HP, nq),
        in_specs=_att_in_specs() + [heads, pl.BlockSpec((HP, TQ, 1), lambda g, i: (g, i, 0)), heads, heads],
        out_specs=[pl.BlockSpec((TQ, HP * QP), lambda g, i: (i, g)),
                   pl.BlockSpec((T, HP * KVW), lambda g, i: (0, g)),
                   _full_spec((T, HD)), heads],
        out_shape=[_sds((T, C_HEADS * QP), BF16), _sds((T, C_HEADS * KVW), BF16), _sds((T, HD), F32),
                   _sds((T, C_HEADS * C_V), BF16)],
        scratch_shapes=[pltpu.VMEM((HP, T, QP), BF16), pltpu.VMEM((HP, T, QP), F32), pltpu.VMEM((HP, T, C_V), F32)],
        compiler_params=_params(2))(q, cos_t, sin_t, kv, kp, o, lse, dog, z1)


def _adamw_math(w, g, m, v):
    m = ADAM_B1 * m + (1.0 - ADAM_B1) * g
    v = ADAM_B2 * v + (1.0 - ADAM_B2) * (g * g)
    m_hat = m / (1.0 - ADAM_B1 ** ADAM_STEP)
    v_hat = v / (1.0 - ADAM_B2 ** ADAM_STEP)
    delta = -ADAM_LR * (m_hat / (jnp.sqrt(v_hat) + ADAM_EPS) + ADAM_WD * w)
    return delta, m, v


def _adamw(name, parts, w, m, v, tr, tc=None):
    rows, cols = w.shape

    def fn(*vals):
        pvs, (wv, mv, vv) = vals[:len(parts)], vals[len(parts):]
        g = None
        for pv in pvs:
            for d in range(pv.shape[0]):
                term = pv[d].astype(F32)
                g = term if g is None else g + term
        return (g,) + _adamw_math(wv, g, mv, vv)

    tc = cols if tc is None else tc
    blk = pl.BlockSpec((tr, tc), lambda i, j: (i, j))
    part_specs = [pl.BlockSpec((n, tr, tc), lambda i, j: (0, i, j)) for _, n in parts]
    return _pure_call(name, fn, (rows // tr, cols // tc), part_specs + [blk, blk, blk],
                      [blk] * 4, [_sds((rows, cols), F32)] * 4, tuple(p for p, _ in parts) + (w, m, v))


SMALL_PARAM_SHAPES = ((2, D), (2, D), (2, A_HEADS * HD), (1, HD), (1, B_GROUPS * HD), (1, B_GROUPS * HD),
                      (B_GROUPS, B_CHUNK, B_CHUNK), (B_GROUPS, B_CHUNK))
SMALL_PIECES = ((0, 0, 0, 0), (0, 1, 1, 0), (1, 0, 1, 1), (1, 1, 1, 2), (2, 0, 2, 0), (2, 1, 2, 1),
                (3, 0, 3, 8), (4, 0, 2, 2), (5, 0, 2, 3))


def _small_rows(dnpre1, dnpost0, dnpost1, dl0, dl1, donorm, dlnw, dlnb, dws, dbias, dqn, dkvn, loss_part):
    return [jnp.concatenate([dnpre1, dnpost0, dnpost1], axis=0),
            jnp.concatenate([dl0, dl1, dlnw, dlnb], axis=0),
            jnp.concatenate([dbias.reshape(B_GROUPS, B_CHUNK), donorm, loss_part], axis=0),
            dws,
            jnp.concatenate([dqn, dkvn], axis=0)]


def _adamw_small(late_all, early_all, wmv):
    n_in = 6 + 3 * len(wmv)

    def body(*refs):
        gathered, params, outs = refs[:6], refs[6:n_in], refs[n_in:]

        def total(ref):
            s = ref[0]
            for d in range(1, N_DEV):
                s = s + ref[d]
            return s

        g_late, g2048, g1024, g128, g_ws, g512 = [total(r) for r in gathered]
        arrays = (g_late, g2048, g1024, g128)

        def update(p, rows, g):
            w_ref, m_ref, v_ref = params[3 * p:3 * p + 3]
            delta, m, v = _adamw_math(w_ref[rows], g, m_ref[rows], v_ref[rows])
            for out, val in zip(outs[4 * p:4 * p + 4], (g, delta, m, v)):
                out[rows] = val

        for p, row, arr, arr_row in SMALL_PIECES:
            update(p, pl.ds(row, 1), arrays[arr][arr_row:arr_row + 1])
        update(6, slice(None), g_ws)
        update(7, slice(None), g128[0:B_GROUPS])
        outs[32][...] = g128[B_GROUPS + 1:B_GROUPS + 2]
        outs[33][...] = g512

    vmem = pl.BlockSpec(memory_space=pltpu.VMEM)
    flat = [a for t in wmv for a in t]
    out_shape = [_sds(s, F32) for s in SMALL_PARAM_SHAPES for _ in range(4)] + [_sds((1, 128), F32), _sds((2, C_RANK), F32)]
    res = pl.pallas_call(body, name="adamw_small", in_specs=[vmem] * n_in, out_specs=[vmem] * len(out_shape),
                         out_shape=out_shape,
                         compiler_params=pltpu.CompilerParams(vmem_limit_bytes=VMEM_LIMIT_V7X))(late_all, *early_all, *flat)
    return [res[4 * p:4 * p + 4] for p in range(8)], res[32], res[33]


def _exchange(name, arrs, gather, deps=()):
    n = len(arrs)
    deps = _live(deps)

    def body(*refs):
        ins, outs = refs[:n], refs[n + len(deps):2 * n + len(deps)]
        send_sems, recv_sems, local_sems = refs[2 * n + len(deps):]
        x, y, c = lax.axis_index("x"), lax.axis_index("y"), lax.axis_index("c")
        me = 4 * x + 2 * y + c

        def peer(k):
            return (x ^ (k >> 2), y ^ ((k >> 1) & 1), c ^ (k & 1))

        def copy(a, k):
            src = ins[a] if gather else ins[a].at[me ^ k]
            return pltpu.make_async_remote_copy(
                src_ref=src, dst_ref=outs[a].at[me], send_sem=send_sems.at[a, k - 1],
                recv_sem=recv_sems.at[a, k - 1], device_id=peer(k), device_id_type=MESH_ID)

        def arrival(a, k):
            src = ins[a] if gather else ins[a].at[me]
            return pltpu.make_async_remote_copy(
                src_ref=src, dst_ref=outs[a].at[me ^ k], send_sem=send_sems.at[a, k - 1],
                recv_sem=recv_sems.at[a, k - 1], device_id=peer(k), device_id_type=MESH_ID)

        own = [pltpu.make_async_copy(ins[a] if gather else ins[a].at[me], outs[a].at[me], local_sems.at[a])
               for a in range(n)]
        for cp in own:
            cp.start()
        for k in range(1, N_DEV):
            for a in range(n):
                copy(a, k).start()
        for k in range(1, N_DEV):
            for a in range(n):
                arrival(a, k).wait_recv()
        for k in range(1, N_DEV):
            for a in range(n):
                copy(a, k).wait_send()
        for cp in own:
            cp.wait()

    any_spec = pl.BlockSpec(memory_space=pl.ANY)
    out_shape = [_sds((N_DEV,) + a.shape if gather else a.shape, a.dtype) for a in arrs]
    return pl.pallas_call(
        body, name=name, in_specs=[any_spec] * (n + len(deps)), out_specs=[any_spec] * n, out_shape=out_shape,
        scratch_shapes=[pltpu.SemaphoreType.DMA((n, N_DEV - 1)), pltpu.SemaphoreType.DMA((n, N_DEV - 1)),
                        pltpu.SemaphoreType.DMA((n,))],
        compiler_params=pltpu.CompilerParams(has_side_effects=True))(*arrs, *deps)


HBM_SPEC = pl.BlockSpec(memory_space=pltpu.HBM)
SEM_SPEC = pl.BlockSpec(memory_space=pltpu.SEMAPHORE)
DATAFLOW = pltpu.SideEffectType.DATAFLOW_SIDE_EFFECTING


def _my_index():
    return 4 * lax.axis_index("x") + 2 * lax.axis_index("y") + lax.axis_index("c")


def _plan_copies(plan, refs, send_sems, recv_sems):
    x, y, c = lax.axis_index("x"), lax.axis_index("y"), lax.axis_index("c")
    copies = []
    for place, entry in enumerate(plan(refs, 4 * x + 2 * y + c)):
        src, dst, k = entry[:3]
        i = entry[3] if len(entry) > 3 else place
        copies.append(pltpu.make_async_remote_copy(
            src_ref=src, dst_ref=dst, send_sem=send_sems.at[i], recv_sem=recv_sems.at[i],
            device_id=(x ^ (k >> 2), y ^ ((k >> 1) & 1), c ^ (k & 1)), device_id_type=MESH_ID))
    return copies


def _split_call(name, bufs, waits=None, starts=None, deps=()):
    n = len(bufs)
    deps = _live(deps)
    n_wait = 2 if waits else 0

    def body(*refs):
        zones = refs[:n]
        if waits:
            for cp in _plan_copies(waits[2], zones, refs[n], refs[n + 1]):
                cp.wait_send()
                cp.wait_recv()
        if starts:
            first_out = n + n_wait + len(deps)
            for cp in _plan_copies(starts[0], zones, refs[first_out], refs[first_out + 1]):
                cp.start()
            refs[-1][...] = jnp.zeros_like(refs[-1])

    out_specs, out_shape = [], []
    if starts:
        sems = pltpu.SemaphoreType.DMA((starts[1],))
        out_specs, out_shape = [SEM_SPEC, SEM_SPEC], [sems, sems]
    out_specs += [HBM_SPEC] * n
    out_shape += [pltpu.HBM(b.shape, b.dtype) for b in bufs]
    if starts:
        out_specs.append(pl.BlockSpec(memory_space=pltpu.VMEM))
        out_shape.append(_sds((8, 128), F32))
    first_buf = 2 if starts else 0
    res = pl.pallas_call(
        body, name=name,
        in_specs=[HBM_SPEC] * n + [SEM_SPEC] * n_wait + [ANY_SPEC] * len(deps),
        out_specs=out_specs, out_shape=out_shape,
        input_output_aliases={i: first_buf + i for i in range(n)},
        compiler_params=pltpu.CompilerParams(has_side_effects=DATAFLOW),
    )(*[pltpu.with_memory_space_constraint(b, pltpu.HBM) for b in bufs], *(waits[:2] if waits else ()), *deps)
    out_bufs = list(res[first_buf:first_buf + n])
    return out_bufs, ((res[0], res[1]) if starts else None), (res[-1] if starts else None)


def _direct_plan(n, gather):
    def plan(refs, me):
        return [(refs[a] if gather else refs[a].at[me ^ k], refs[n + a].at[me], k)
                for k in range(1, N_DEV) for a in range(n)]
    return plan


def _own_slot_filled(a, gather):
    me = _my_index()
    if gather:
        return lax.dynamic_update_slice_in_dim(lax.empty((N_DEV,) + a.shape, a.dtype), a[None], me, 0)
    return lax.dynamic_update_slice_in_dim(lax.empty(a.shape, a.dtype), lax.dynamic_slice_in_dim(a, me, 1, 0), me, 0)


def _exchange_start(name, arrs, gather, deps=()):
    n = len(arrs)
    lands = [_own_slot_filled(a, gather) for a in arrs]
    plan = _direct_plan(n, gather)
    bufs, sems, token = _split_call(name, list(arrs) + lands, starts=(plan, n * (N_DEV - 1)), deps=deps)
    return (n, plan, sems, bufs, None), token


def _exchange_wait(name, handle, after):
    return _split_done(name, handle, after)


ICI_PEERS = (2, 4, 6)
SIBLING = 1


def _gather2_send(name, arrs, deps=()):
    n = len(arrs)
    lands = [_own_slot_filled(a, True) for a in arrs]

    def plan(refs, me_):
        return [(refs[a], refs[n + a].at[me_], k) for k in (SIBLING,) + ICI_PEERS for a in range(n)]

    bufs, sems, token = _split_call(name, list(arrs) + lands, starts=(plan, 4 * n), deps=deps)
    return (n, plan, sems, bufs, None), token


def _gather2_relay(name, handle, after):
    n, plan, sems, bufs, _ = handle
    after = after if isinstance(after, (list, tuple)) else [after]

    def relay(refs, me_):
        return [(refs[n + a].at[me_ ^ k], refs[n + a].at[me_ ^ k], SIBLING) for k in ICI_PEERS for a in range(n)]

    bufs, sems2, token = _split_call(name, bufs, waits=(sems[0], sems[1], plan), starts=(relay, 3 * n), deps=after)
    return (n, relay, sems2, bufs, None), token


def _first_gather_send(name, shard):
    near = _own_slot_filled(shard, True)

    def plan(refs, me):
        return [(refs[0], refs[1].at[me], k) for k in (SIBLING, 2, 4)]

    return _split_call(name, [shard, near], starts=(plan, 3)), plan


def _first_gather_near(names, bufs, sems, plan, after):
    far = lax.empty((2,) + bufs[0].shape, bufs[0].dtype)

    def onward(refs, me):
        hop = 4 - 2 * (me & 1)
        return [(refs[1].at[me ^ 2], refs[1].at[me ^ 2], SIBLING), (refs[1].at[me ^ 4], refs[1].at[me ^ 4], SIBLING),
                (refs[1].at[me ^ hop], refs[2].at[0], 6 - hop)]

    out, sems2, token = _split_call(names[0], list(bufs) + [far], waits=(sems[0], sems[1], plan), starts=(onward, 3),
                                    deps=after)

    def relayed(refs, me):
        return [(refs[1].at[me ^ 2], refs[1].at[me ^ 2], SIBLING, 0), (refs[1].at[me ^ 4], refs[1].at[me ^ 4], SIBLING, 1)]

    near = _split_call(names[1], out[:2], waits=(sems2[0], sems2[1], relayed), deps=[token])[0][1]
    return near, (out[2], sems2)


def _first_gather_far(names, far_handle, after):
    far, sems = far_handle

    def forwarded(refs, me):
        return [(refs[0].at[0], refs[0].at[0], 2 + 2 * (me & 1), 2)]

    def relay(refs, me):
        del me
        return [(refs[0].at[0], refs[0].at[1], SIBLING)]

    bufs, sems2, token = _split_call(names[0], [far], waits=(sems[0], sems[1], forwarded), starts=(relay, 1), deps=[after])
    return _split_call(names[1], bufs, waits=(sems2[0], sems2[1], relay), deps=[token])[0][0]


def _split_done(name, handle, after, all_bufs=False):
    n, plan, sems, bufs, _ = handle
    bufs, _, _ = _split_call(name, bufs, waits=(sems[0], sems[1], plan), deps=[after])
    return bufs if all_bufs else bufs[n:]


def _scatter2_pair(name, for_sibling, deps=()):
    n = len(for_sibling)
    pairs = [lax.empty(s.shape, s.dtype) for s in for_sibling]

    def plan(refs, me):
        del me
        return [(refs[a].at[s], refs[n + a].at[s], SIBLING) for s in range(4) for a in range(n)]

    bufs, sems, token = _split_call(name, list(for_sibling) + pairs, starts=(plan, 4 * n), deps=deps)
    return (n, plan, sems, bufs, None), token


def _pair_add(name, mine, pair):
    _, rows, cols = mine.shape
    tr = rows // 2

    def fn(a, b):
        return a.astype(F32) + b.astype(F32)

    blk = pl.BlockSpec((None, tr, cols), lambda s, i: (s, i, 0))
    return _pure_call(name, fn, (4, rows // tr), [blk, blk], [blk], [_sds(mine.shape, mine.dtype)], (mine, pair))[0]


def _scatter2_send(name, chip_sums, deps=()):
    n = len(chip_sums)
    finals = [lax.empty((3,) + c.shape[1:], c.dtype) for c in chip_sums]

    def plan(refs, me):
        return [(refs[a].at[(me >> 1) ^ j], refs[n + a].at[j - 1], 2 * j) for j in range(1, 4) for a in range(n)]

    bufs, sems, token = _split_call(name, list(chip_sums) + finals, starts=(plan, 3 * n), deps=deps)
    return (n, plan, sems, bufs, None), token


def _pad_rope(p):
    z = jnp.zeros(p.shape[:-1] + (32,), p.dtype)
    return jnp.concatenate([p[..., :32], z, p[..., 32:], z], axis=-1)


def _unpad_rope(p):
    return jnp.concatenate([p[..., :32], p[..., 64:96]], axis=-1)


def _odd_in_layout(wt):
    wt = wt.reshape(ODD_IN, D)
    cq, ckv, kpe, gate = wt[:512], wt[512:1024], wt[1024:1088], wt[1088:]
    z = jnp.zeros((32, D), wt.dtype)
    return jnp.concatenate([gate, cq, ckv, kpe[:32], z, kpe[32:], z], axis=0)


def _odd_in_unlayout(dwt):
    gate, cq, ckv, kpe = dwt[:2048], dwt[2048:2560], dwt[2560:3072], dwt[3072:]
    wt = jnp.concatenate([cq, ckv, kpe[:32], kpe[64:96], gate], axis=0)
    return wt.reshape(N_DEV, ODD_IN // N_DEV, D)


def _qb_layout(w):
    w = w.transpose(1, 0, 2).reshape(C_RANK, C_HEADS, C_QK)
    w = jnp.concatenate([w[..., :C_NOPE], _pad_rope(w[..., C_NOPE:])], axis=-1)
    return w.reshape(C_RANK, C_HEADS * QP)


def _qb_unlayout(dw):
    dw = dw.reshape(C_RANK, C_HEADS, QP)
    dw = jnp.concatenate([dw[..., :C_NOPE], _unpad_rope(dw[..., C_NOPE:])], axis=-1)
    return dw.reshape(C_RANK, N_DEV, C_HEADS * C_QK // N_DEV).transpose(1, 0, 2)


def _rope_tables(positions):
    inv_freq = ROPE_THETA ** (-jnp.arange(0, C_ROPE, 2, dtype=F32) / C_ROPE)
    ang = positions.astype(F32)[0][:, None] * inv_freq
    cos, sin = jnp.cos(ang), jnp.sin(ang)
    z = jnp.zeros_like(cos)
    return jnp.concatenate([cos, z, cos, z], axis=1), jnp.concatenate([-sin, z, sin, z], axis=1)


def _forward_backward(x, cos_t, sin_t, target, norm_pre, norm_post, lb_logits, a_onorm, ln_w, ln_b,
                      b_ws, b_bias, get_w, put_g, put_small=None, start_dep=None):
    npre0, npre1 = norm_pre[0:1], norm_pre[1:2]
    npost0, npost1 = norm_post[0:1], norm_post[1:2]
    l0, l1 = lb_logits[0:1], lb_logits[1:2]
    bias_col = b_bias.reshape(B_GROUPS, B_CHUNK, 1)
    ws = b_ws.reshape(B_GROUPS, B_CHUNK, B_CHUNK)

    h0 = _pre_norm("pre_norm0", x, npre0, deps=[start_dep])
    z0, w_ev_in = get_w("ev_in", h0)(h0)
    cat, sst = _hgrn2_fwd(z0, l0, l1, a_onorm)
    cat = _gmlp_fwd(z0, cat, ln_w, ln_b, ws, bias_col)
    w_ev_out = get_w("ev_out", cat)
    y0 = _mm_nn("ev_out", cat, w_ev_out, F32, 1024, 1024)
    get_w("od_relay", y0)
    x1, h1 = _post_pre_norm(x, y0, npost0, npre1)
    w_od_in, w_qb, w_kvb, q_norm, kv_norm = get_w("od_mid", h1)
    z1 = _mm_nt("od_in", h1, w_od_in[None], F32, 1024, 640)
    cqn, ckvn, kp = _mla_pre(z1, q_norm, kv_norm, cos_t, sin_t)
    q = _mm_nn("od_qb", cqn, w_qb[None], F32, 1024, 1024)
    kv = _mm_nn("od_kvb", ckvn, w_kvb, BF16, 1024, 512)
    o, lse, og = _attention_fwd(q, cos_t, sin_t, kv, kp, z1)
    w_od_out = get_w("od_out", og)
    y1 = _mm_nn("od_out", og, w_od_out, F32, 1024, 1024)
    dx2, dy1, loss_part, dnpost1 = _final_loss(x1, y1, npost1, target)

    g_od_out = _mm_tn("od_out_dw", og, dy1, 1, BF16, 1024, 1024)
    tok = put_g("od_out", [g_od_out.reshape(N_DEV, D // N_DEV, D)])
    dog = _mm_nt("od_out_dx", dy1, w_od_out, F32, 1024, 1024, deps=[tok])
    dq, dkv, dkp, dgate = _attention_bwd(q, cos_t, sin_t, kv, kp, o, lse, dog, z1)
    g_qb = _mm_tn("od_qb_dw", cqn, dq, 1, F32, 512, 1024)
    g_kvb = _mm_tn("od_kvb_dw", ckvn, dkv, N_DEV, BF16, 512, 512)
    tok = put_g("od_qkv", [_qb_unlayout(g_qb[0]).astype(BF16), g_kvb])
    dcqn = _mm_nt("od_qb_dx", dq, w_qb[None], F32, 1024, 512, deps=[tok])
    dckvn = _mm_nt("od_kvb_dx", dkv, w_kvb, F32, 1024, 512)
    dcq, dckv, dkpe, dqn, dkvn = _mla_pre_bwd(z1, q_norm, kv_norm, cos_t, sin_t, dcqn, dckvn, dkp)
    dz1 = jnp.concatenate([dgate, dcq, dckv, dkpe], axis=1)
    g_od_in = _mm_tn("od_in_dw", dz1, h1, 1, F32, 640, 1024)
    tok = put_g("od_in", [_odd_in_unlayout(g_od_in[0]).astype(BF16)])
    dh1 = _mm_nn("od_in_dx", dz1, w_od_in[None], F32, 1024, 1024, deps=[tok])
    dx1, dy0, dnpost0, dnpre1 = _post_pre_norm_bwd(y0, x1, npost0, npre1, dx2, dh1)

    g_ev_out = _mm_tn("ev_out_dw", cat, dy0, 1, BF16, 1024, 1024)
    tok = put_g("ev_out", [g_ev_out.reshape(N_DEV, D // N_DEV, D)])
    dcat = _mm_nt("ev_out_dx", dy0, w_ev_out, F32, 1024, 1024, deps=[tok])
    dqa, dfa, dia, dga, dl0, dl1, donorm = _hgrn2_bwd(z0, l0, l1, a_onorm, sst, dcat)
    dub, dvb, dgb, dlnw, dlnb, dws, dbias = _gmlp_bwd(z0, ln_w, ln_b, ws, bias_col, dcat)
    dz0 = jnp.concatenate([dqa, dfa, dia, dga, dub, dvb, dgb], axis=1)
    early = _small_rows(dnpre1, dnpost0, dnpost1, dl0, dl1, donorm, dlnw, dlnb, dws, dbias, dqn, dkvn, loss_part)
    tok = put_small(early) if put_small else None
    small_tok = tok

    def ev_in_half(name, parity, deps=()):
        return _mm_tn_parity(name, h0, dz0, N_DEV, parity, BF16, 1024, deps=[small_tok] + list(deps))

    tok = put_g("ev_in", ev_in_half)
    dh0 = _mm_nt("ev_in_dx", dz0, w_ev_in, F32, 1024, 256, deps=[tok])
    grad_x, dnpre0 = _pre_norm_bwd(x, npre0, dh0, dx1)
    return grad_x, early, dnpre0


def kernel(x, positions, norm_pre, norm_post, ev_w_in, ev_lb_logits, ev_a_onorm, ev_b_ln_w, ev_b_ln_b, ev_b_ws, ev_b_bias, ev_w_out, od_w_in, od_q_norm, od_w_qb, od_kv_norm, od_w_kvb, od_w_out, loss_target, m_norm_pre, m_norm_post, m_ev_w_in, m_ev_lb_logits, m_ev_a_onorm, m_ev_b_ln_w, m_ev_b_ln_b, m_ev_b_ws, m_ev_b_bias, m_ev_w_out, m_od_w_in, m_od_q_norm, m_od_w_qb, m_od_kv_norm, m_od_w_kvb, m_od_w_out, v_norm_pre, v_norm_post, v_ev_w_in, v_ev_lb_logits, v_ev_a_onorm, v_ev_b_ln_w, v_ev_b_ln_b, v_ev_b_ws, v_ev_b_bias, v_ev_w_out, v_od_w_in, v_od_q_norm, v_od_w_qb, v_od_kv_norm, v_od_w_kvb, v_od_w_out):
    me = 4 * lax.axis_index("x") + 2 * lax.axis_index("y") + lax.axis_index("c")
    bf = lambda w: w[0].astype(BF16)

    norms = jnp.pad(jnp.concatenate([od_q_norm, od_kv_norm], axis=1), ((0, 7), (0, 0)))
    sent = {}
    (first_bufs, first_sems, tok), first_plan = _first_gather_send("gather_ev_in", bf(ev_w_in))
    later = [bf(ev_w_out), od_w_in[0].T.astype(BF16), bf(od_w_qb), bf(od_w_kvb), norms, bf(od_w_out)]
    cos_t, sin_t = _rope_tables(positions)
    od = []

    def get_w(group, after):
        if group == "ev_in":
            def first_matmul(h0):
                near, far_handle = _first_gather_near(("relay_ev_in", "arrived_ev_in"), first_bufs, first_sems,
                                                      first_plan, [after, cos_t, sin_t] + later)
                sent["ev_out"], token = _gather2_send("gather_ev_out", later[:1], deps=[near])
                sent["od"], _ = _gather2_send("gather_od", later[1:], deps=[token])
                ids = jnp.stack([me, me ^ 1, me ^ 2, me ^ 4, me ^ 3, me ^ 5]).astype(jnp.int32)
                z0 = _mm_nn_some("ev_in", h0, near, ids, ids, N_DEV, F32, 1024)
                far = _first_gather_far(("relay_ev_in_far", "arrived_ev_in_far"), far_handle, z0)
                far_ids = jnp.stack([me ^ 6, me ^ 7]).astype(jnp.int32)
                z0 = _mm_nn_some("ev_in_far", h0, far, jnp.arange(2, dtype=jnp.int32), far_ids, N_DEV, F32, 1024,
                                 prev=z0)
                full = lax.dynamic_update_slice_in_dim(near, far[0:1], me ^ 6, 0)
                return z0, lax.dynamic_update_slice_in_dim(full, far[1:2], me ^ 7, 0)
            return first_matmul
        if group == "ev_out":
            relayed, token = _gather2_relay("relay_ev_out", sent["ev_out"], after)
            return _split_done("arrived_ev_out", relayed, token)[0].reshape(1, D, D)
        if group == "od_relay":
            sent["od_relayed"], _ = _gather2_relay("relay_od", sent["od"], after)
            return None
        if not od:
            od.extend(_split_done("arrived_od", sent["od_relayed"], after))
        w_od_in, w_qb, w_kvb, norms_all, w_od_out = od
        if group == "od_out":
            return w_od_out.reshape(1, D, D)
        return (_odd_in_layout(w_od_in), _qb_layout(w_qb), w_kvb,
                norms_all[:, 0, :64].reshape(1, C_RANK), norms_all[:, 0, 64:].reshape(1, C_RANK))

    scatters = {}

    def put_g(group, grads):
        if group == "ev_in":
            core = lax.axis_index("c").astype(jnp.int32).reshape(1)
            paired, token = _scatter2_pair("pair_ev_in", [grads("ev_in_dw_sibling", 1 - core)])
            mine = grads("ev_in_dw_own", core, deps=[token])
            pair = _split_done("paired_ev_in", paired, mine)[0]
            scatters[group], token = _scatter2_send("scatter_ev_in", [_pair_add("pair_add_ev_in", mine, pair)])
        else:
            scatters[group], token = _exchange_start("scatter_" + group, grads, False)
        return token

    def put_small(early):
        scatters["small"], token = _exchange_start("gather_small_early", early, True)
        return token

    grad_x, _, dnpre0 = _forward_backward(
        x[0], cos_t, sin_t, loss_target[0], norm_pre, norm_post, ev_lb_logits, ev_a_onorm, ev_b_ln_w,
        ev_b_ln_b, ev_b_ws, ev_b_bias, get_w, put_g, put_small, start_dep=tok)

    big_w = {"ev_w_in": ev_w_in, "ev_w_out": ev_w_out, "od_w_in": od_w_in, "od_w_qb": od_w_qb,
             "od_w_kvb": od_w_kvb, "od_w_out": od_w_out}
    big_m = {"ev_w_in": m_ev_w_in, "ev_w_out": m_ev_w_out, "od_w_in": m_od_w_in, "od_w_qb": m_od_w_qb,
             "od_w_kvb": m_od_w_kvb, "od_w_out": m_od_w_out}
    big_v = {"ev_w_in": v_ev_w_in, "ev_w_out": v_ev_w_out, "od_w_in": v_od_w_in, "od_w_qb": v_od_w_qb,
             "od_w_kvb": v_od_w_kvb, "od_w_out": v_od_w_out}
    big_out = {}
    after = grad_x
    for group, names in (("od_out", ["od_w_out"]), ("od_qkv", ["od_w_qb", "od_w_kvb"]), ("od_in", ["od_w_in"]),
                         ("ev_out", ["ev_w_out"])):
        parts = _exchange_wait("summed_" + group, scatters[group], after)
        for nm, p in zip(names, parts):
            w, m, v = big_w[nm][0], big_m[nm][0], big_v[nm][0]
            if nm == "od_w_in":
                res_t = _adamw("adamw_" + nm, [(p, N_DEV)], w.T, m.T, v.T, w.shape[1], 512)
                big_out[nm] = [r.T[None] for r in res_t]
            else:
                big_out[nm] = [r[None] for r in _adamw("adamw_" + nm, [(p, N_DEV)], w, m, v, w.shape[0] // 8)]
            after = big_out[nm][0]

    late_all = _exchange("gather_small_late", [dnpre0], gather=True, deps=[after])[0]
    early_all = _exchange_wait("arrived_small_early", scatters["small"], late_all)

    small_w = (norm_pre, norm_post, ev_lb_logits, ev_a_onorm, ev_b_ln_w, ev_b_ln_b, ev_b_ws, ev_b_bias)
    small_m = (m_norm_pre, m_norm_post, m_ev_lb_logits, m_ev_a_onorm, m_ev_b_ln_w, m_ev_b_ln_b, m_ev_b_ws, m_ev_b_bias)
    small_v = (v_norm_pre, v_norm_post, v_ev_lb_logits, v_ev_a_onorm, v_ev_b_ln_w, v_ev_b_ln_b, v_ev_b_ws, v_ev_b_bias)
    wmv = [tuple(a.reshape(s) for a in t) for s, t in zip(SMALL_PARAM_SHAPES, zip(small_w, small_m, small_v))]
    small_res, loss_row, g_norm_rows = _adamw_small(late_all, early_all, wmv)
    small_out = [[r.reshape(w.shape) for r in four] for four, w in zip(small_res, small_w)]
    loss = loss_row[0, 0]

    g_norms = jnp.concatenate([lax.dynamic_slice(g_norm_rows, (0, 64 * me), (1, 64)),
                               lax.dynamic_slice(g_norm_rows, (1, 64 * me), (1, 64))], axis=1)
    res_n = _adamw("adamw_norms", [(g_norms[None], 1)],
                   jnp.concatenate([od_q_norm, od_kv_norm], axis=1),
                   jnp.concatenate([m_od_q_norm, m_od_kv_norm], axis=1),
                   jnp.concatenate([v_od_q_norm, v_od_kv_norm], axis=1), 1)
    qn_out = [r[:, :64] for r in res_n]
    kvn_out = [r[:, 64:] for r in res_n]

    chip_sums, from_peers = _split_done("summed_ev_in", scatters["ev_in"], loss_row, all_bufs=True)
    own_chip = lax.dynamic_slice_in_dim(chip_sums, me >> 1, 1, 0)
    w = ev_w_in[0]
    big_out["ev_w_in"] = [r[None] for r in _adamw("adamw_ev_w_in", [(own_chip, 1), (from_peers, 3)], w, m_ev_w_in[0],
                                                  v_ev_w_in[0], w.shape[0] // 8)]

    order = ("norm_pre", "norm_post", "ev_w_in", "ev_lb_logits", "ev_a_onorm", "ev_b_ln_w", "ev_b_ln_b",
             "ev_b_ws", "ev_b_bias", "ev_w_out", "od_w_in", "od_q_norm", "od_w_qb", "od_kv_norm",
             "od_w_kvb", "od_w_out")
    small_names = ("norm_pre", "norm_post", "ev_lb_logits", "ev_a_onorm", "ev_b_ln_w", "ev_b_ln_b",
                   "ev_b_ws", "ev_b_bias")
    outs = [loss, grad_x[None]]
    for kind in range(4):
        for nm in order:
            if nm in big_out:
                outs.append(big_out[nm][kind])
            elif nm == "od_q_norm":
                outs.append(qn_out[kind])
            elif nm == "od_kv_norm":
                outs.append(kvn_out[kind])
            else:
                outs.append(small_out[small_names.index(nm)][kind])
    return tuple(outs)
```

```python
import functools

import jax
import jax.numpy as jnp
from jax import lax
from jax.experimental import pallas as pl
from jax.experimental.pallas import tpu as pltpu

F32 = jnp.float32
BF16 = jnp.bfloat16

N_DEV = 8
T = 2048
D = 2048
EPS = 1e-6
A_HEADS = 8
HD = 128
A_CHUNK = 64
A_SUB = 16
B_GROUPS = 8
B_CHUNK = 128
EVEN_IN = 7168
C_HEADS = 16
C_RANK = 512
C_NOPE = 128
C_ROPE = 64
C_QK = C_NOPE + C_ROPE
C_V = 128
ODD_IN = 3136
ODD_IN_PAD = 3200
QP = 256
ROPE_THETA = 10000.0
ATT_SCALE = C_QK ** -0.5

ADAM_LR = 0.001
ADAM_B1 = 0.9
ADAM_B2 = 0.999
ADAM_EPS = 1e-08
ADAM_WD = 0.01
ADAM_STEP = 10

VMEM_LIMIT_V7X = 56 * 1024 * 1024
MESH_ID = pl.DeviceIdType.MESH


def _params(n_grid):
    return pltpu.CompilerParams(dimension_semantics=("arbitrary",) * n_grid,
                                vmem_limit_bytes=VMEM_LIMIT_V7X)


def _dg(a, b, ca, cb):
    return lax.dot_general(a.astype(BF16), b.astype(BF16), (((ca,), (cb,)), ((), ())),
                           preferred_element_type=F32)


def _raw_nn(a, b):
    return _dg(a, b, 1, 0)


def _raw_nt(a, b):
    return _dg(a, b, 1, 1)


def _raw_tn(a, b):
    return _dg(a, b, 0, 0)


@jax.custom_vjp
def _dot_nn(a, b):
    return _raw_nn(a, b)


def _dot_nn_fwd(a, b):
    return _raw_nn(a, b), (a.astype(BF16), b.astype(BF16))


def _dot_nn_bwd(res, g):
    a, b = res
    return _raw_nt(g, b), _raw_tn(a, g)


_dot_nn.defvjp(_dot_nn_fwd, _dot_nn_bwd)


@jax.custom_vjp
def _dot_nt(a, b):
    return _raw_nt(a, b)


def _dot_nt_fwd(a, b):
    return _raw_nt(a, b), (a.astype(BF16), b.astype(BF16))


def _dot_nt_bwd(res, g):
    a, b = res
    return _raw_nn(g, b), _raw_tn(g, a)


_dot_nt.defvjp(_dot_nt_fwd, _dot_nt_bwd)


@jax.custom_vjp
def _dot_tn(a, b):
    return _raw_tn(a, b)


def _dot_tn_fwd(a, b):
    return _raw_tn(a, b), (a.astype(BF16), b.astype(BF16))


def _dot_tn_bwd(res, g):
    a, b = res
    return _raw_nt(b, g), _raw_nn(a, g)


_dot_tn.defvjp(_dot_tn_fwd, _dot_tn_bwd)


@jax.custom_vjp
def _sigmoid(x):
    e = jnp.exp(-jnp.abs(x))
    return jnp.where(x >= 0, 1.0 / (1.0 + e), e / (1.0 + e))


def _sigmoid_fwd(x):
    s = _sigmoid(x)
    return s, s


def _sigmoid_bwd(s, g):
    return (g * s * (1.0 - s),)


_sigmoid.defvjp(_sigmoid_fwd, _sigmoid_bwd)


def _silu(x):
    return x * _sigmoid(x)


def _rms(x, w):
    return x * lax.rsqrt(jnp.mean(x * x, axis=-1, keepdims=True) + EPS) * w


def _split3(x):
    hi = x.astype(BF16)
    r = x - hi.astype(F32)
    mid = r.astype(BF16)
    lo = (r - mid.astype(F32)).astype(BF16)
    return hi, mid, lo


def _mask_apply(mask_bf16, x, contract):
    out = None
    for piece in _split3(x):
        d = lax.dot_general(mask_bf16, piece, (((contract,), (0,)), ((), ())),
                            preferred_element_type=F32)
        out = d if out is None else out + d
    return out


def _chunk_tri(rows):
    r = lax.broadcasted_iota(jnp.int32, (rows, rows), 0)
    c = lax.broadcasted_iota(jnp.int32, (rows, rows), 1)
    return ((r >= c) & (r // A_CHUNK == c // A_CHUNK)).astype(BF16)


@jax.custom_vjp
def _chunk_cumsum(x):
    return _mask_apply(_chunk_tri(x.shape[0]), x, 1)


def _chunk_cumsum_fwd(x):
    return _chunk_cumsum(x), None


def _chunk_cumsum_bwd(_, g):
    return (_mask_apply(_chunk_tri(g.shape[0]), g, 0),)


_chunk_cumsum.defvjp(_chunk_cumsum_fwd, _chunk_cumsum_bwd)


def _hgrn2_rows(q, zf, v, ga, st, l0, l1, onorm):
    rows = q.shape[0]
    n_sub = A_CHUNK // A_SUB
    mx = jnp.maximum(l0, l1)
    e0 = jnp.exp(l0 - mx)
    e1 = jnp.exp(l1 - mx)
    lb = e0 / (e0 + e1)
    lf = jnp.log(lb + (1.0 - lb) * _sigmoid(zf))
    k = (1.0 - lb) * _sigmoid(-zf)
    b = _chunk_cumsum(lf)

    t_idx = lax.broadcasted_iota(jnp.int32, (A_CHUNK, n_sub * A_CHUNK), 0)
    c_idx = lax.broadcasted_iota(jnp.int32, (A_CHUNK, n_sub * A_CHUNK), 1)
    sel = (c_idx // A_CHUNK == t_idx // A_SUB) & (c_idx % A_CHUNK <= t_idx)
    key_row = lax.broadcasted_iota(jnp.int32, (A_CHUNK, HD), 0)

    outs = []
    for n in range(rows // A_CHUNK):
        lo = n * A_CHUNK
        qc, kc, vc = q[lo:lo + A_CHUNK], k[lo:lo + A_CHUNK], v[lo:lo + A_CHUNK]
        lfc, bc = lf[lo:lo + A_CHUNK], b[lo:lo + A_CHUNK]
        b_last = bc[A_CHUNK - 1:A_CHUNK]
        o_inter = _dot_nt(qc * jnp.exp(bc), st)
        kv_t = _dot_tn(vc, kc * jnp.exp(b_last - bc))
        st = st * jnp.exp(b_last) + kv_t
        g_rows, k_subs = [], []
        for i in range(n_sub):
            g_i = bc[i * A_SUB:i * A_SUB + 1] - lfc[i * A_SUB:i * A_SUB + 1]
            g_rows.append(jnp.broadcast_to(g_i, (A_SUB, HD)))
            expo = jnp.where(key_row < (i + 1) * A_SUB, g_i - bc, -jnp.inf)
            k_subs.append(kc * jnp.exp(expo))
        q_sub = qc * jnp.exp(bc - jnp.concatenate(g_rows, axis=0))
        scores = _dot_nt(q_sub, jnp.concatenate(k_subs, axis=0))
        scores = jnp.where(sel, scores, 0.0)
        o_intra = _dot_nn(scores, jnp.concatenate([vc] * n_sub, axis=0))
        outs.append(o_inter + o_intra)
    o = jnp.concatenate(outs, axis=0)
    return _rms(o, onorm) * _silu(ga), st


def _gmlp_rows(u, vb, gb, lnw, lnb, ws, bias):
    rows = u.shape[0]
    mu = jnp.mean(vb, axis=-1, keepdims=True)
    xc = vb - mu
    vg = xc * lax.rsqrt(jnp.mean(xc * xc, axis=-1, keepdims=True) + EPS) * lnw + lnb
    r = lax.broadcasted_iota(jnp.int32, (B_CHUNK, B_CHUNK), 0)
    c = lax.broadcasted_iota(jnp.int32, (B_CHUNK, B_CHUNK), 1)
    ws_causal = jnp.where(r >= c, ws, 0.0)
    svs = [_dot_nn(ws_causal, vg[n * B_CHUNK:(n + 1) * B_CHUNK]) + bias
           for n in range(rows // B_CHUNK)]
    return u * jnp.concatenate(svs, axis=0) * _silu(gb)


def _rope(x, cos_t, sin_t):
    return x * cos_t + pltpu.roll(x, 64, 1) * sin_t


def _rope_transpose(g, cos_t, sin_t):
    return g * cos_t + pltpu.roll(g * sin_t, 64, 1)


ANY_SPEC = pl.BlockSpec(memory_space=pl.ANY)


def _live(deps):
    return [d for d in deps if d is not None]


def _skip_deps(body, n_in, n_deps):
    def wrapped(*refs):
        return body(*refs[:n_in], *refs[n_in + n_deps:])
    return wrapped


def _pure_call(name, fn, grid, in_specs, out_specs, out_shape, args, n_acc=0, deps=()):
    deps = _live(deps)
    n_in, n_out, n_deps = len(in_specs), len(out_specs), len(deps)
    in_specs = list(in_specs) + [ANY_SPEC] * n_deps
    args = tuple(args) + tuple(deps)

    def body(*refs):
        res = fn(*[r[...] for r in refs[:n_in]])
        if not isinstance(res, (tuple, list)):
            res = (res,)
        outs = refs[n_in + n_deps:n_in + n_deps + n_out]
        for o, r in zip(outs[:n_out - n_acc], res[:n_out - n_acc]):
            o[...] = r.astype(o.dtype)
        if n_acc:
            first = functools.reduce(jnp.logical_and, [pl.program_id(i) == 0 for i in range(len(grid))])
            for o, r in zip(outs[n_out - n_acc:], res[n_out - n_acc:]):
                @pl.when(first)
                def _(o=o, r=r):
                    o[...] = r.astype(o.dtype)

                @pl.when(jnp.logical_not(first))
                def _(o=o, r=r):
                    o[...] += r.astype(o.dtype)

    return pl.pallas_call(body, name=name, grid=grid, in_specs=in_specs, out_specs=out_specs,
                          out_shape=out_shape, compiler_params=_params(len(grid)))(*args)


def _sds(shape, dtype):
    return jax.ShapeDtypeStruct(shape, dtype)


def _row_spec(tm, width, col=0):
    return pl.BlockSpec((tm, width), lambda i, col=col: (i, col))


def _full_spec(shape):
    nd = len(shape)
    return pl.BlockSpec(shape, lambda *_: (0,) * nd)


def _mm_nn(name, a, b, out_dtype, tm, tn, deps=()):
    deps = _live(deps)
    m, k = a.shape
    j, _, n = b.shape
    per = n // tn

    def body(a_ref, b_ref, o_ref):
        o_ref[...] = _raw_nn(a_ref[...], b_ref[...]).astype(o_ref.dtype)

    return pl.pallas_call(
        _skip_deps(body, 2, len(deps)), name=name, grid=(m // tm, j * per),
        in_specs=[pl.BlockSpec((tm, k), lambda i, c: (i, 0)),
                  pl.BlockSpec((None, k, tn), lambda i, c: (c // per, 0, c % per))] + [ANY_SPEC] * len(deps),
        out_specs=pl.BlockSpec((tm, tn), lambda i, c: (i, c)),
        out_shape=_sds((m, j * n), out_dtype), compiler_params=_params(2))(a, b, *deps)


def _mm_nt(name, a, b, out_dtype, tm, tn, deps=()):
    deps = _live(deps)
    m = a.shape[0]
    j, nn, n = b.shape

    def body(a_ref, b_ref, o_ref):
        b_all = b_ref[0] if j == 1 else jnp.concatenate([b_ref[s] for s in range(j)], axis=1)
        o_ref[...] = _raw_nt(a_ref[...], b_all).astype(o_ref.dtype)

    return pl.pallas_call(
        _skip_deps(body, 2, len(deps)), name=name, grid=(m // tm, nn // tn),
        in_specs=[pl.BlockSpec((tm, j * n), lambda i, c: (i, 0)),
                  pl.BlockSpec((j, tn, n), lambda i, c: (0, c, 0))] + [ANY_SPEC] * len(deps),
        out_specs=pl.BlockSpec((tm, tn), lambda i, c: (i, c)),
        out_shape=_sds((m, nn), out_dtype), compiler_params=_params(2))(a, b, *deps)


def _mm_tn(name, a, b, j, out_dtype, tm, tn, deps=()):
    deps = _live(deps)
    k, m = a.shape
    n = b.shape[1] // j
    per = n // tn

    def body(a_ref, b_ref, o_ref):
        o_ref[...] = _raw_tn(a_ref[...], b_ref[...]).astype(o_ref.dtype)

    return pl.pallas_call(
        _skip_deps(body, 2, len(deps)), name=name, grid=(m // tm, j * per),
        in_specs=[pl.BlockSpec((k, tm), lambda i, c: (0, i)),
                  pl.BlockSpec((k, tn), lambda i, c: (0, c))] + [ANY_SPEC] * len(deps),
        out_specs=pl.BlockSpec((None, tm, tn), lambda i, c: (c // per, i, c % per)),
        out_shape=_sds((j, m, n), out_dtype), compiler_params=_params(2))(a, b, *deps)


def _mm_tn_parity(name, a, b, j, parity, out_dtype, tm, deps=()):
    deps = _live(deps)
    k, m = a.shape
    n = b.shape[1] // j

    def body(par_ref, a_ref, b_ref, o_ref):
        del par_ref
        o_ref[...] = _raw_tn(a_ref[...], b_ref[...]).astype(o_ref.dtype)

    grid_spec = pltpu.PrefetchScalarGridSpec(
        num_scalar_prefetch=1, grid=(m // tm, j // 2),
        in_specs=[pl.BlockSpec((k, tm), lambda i, s, par: (0, i)),
                  pl.BlockSpec((k, n), lambda i, s, par: (0, 2 * s + par[0]))] + [ANY_SPEC] * len(deps),
        out_specs=pl.BlockSpec((None, tm, n), lambda i, s, par: (s, i, 0)))
    return pl.pallas_call(
        lambda par_ref, *refs: _skip_deps(functools.partial(body, par_ref), 2, len(deps))(*refs),
        name=name, grid_spec=grid_spec, out_shape=_sds((j // 2, m, n), out_dtype),
        compiler_params=_params(2))(parity, a, b, *deps)


TM = 256


def _pre_norm(name, x, w_row, deps=()):
    def fn(xv, w):
        return _rms(xv, w)
    return _pure_call(name, fn, (T // TM,), [_row_spec(TM, D), _full_spec((1, D))],
                      [_row_spec(TM, D)], [_sds((T, D), BF16)], (x, w_row), deps=deps)[0]


def _post_pre_norm(x, y, w_post, w_pre):
    def fn(xv, yv, wp, wn):
        x1 = xv + _rms(yv, wp)
        return x1, _rms(x1, wn)
    return _pure_call("post_pre_norm", fn, (T // TM,),
                      [_row_spec(TM, D), _row_spec(TM, D), _full_spec((1, D)), _full_spec((1, D))],
                      [_row_spec(TM, D), _row_spec(TM, D)],
                      [_sds((T, D), F32), _sds((T, D), BF16)], (x, y, w_post, w_pre))


def _post_pre_norm_bwd(y, x1, w_post, w_pre, dx1_in, dh1, deps=()):
    def fn(yv, x1v, wp, wn, dx1v, dh1v):
        _, vjp_pre = jax.vjp(_rms, x1v, wn)
        dx1_h, dwn = vjp_pre(dh1v)
        dx1 = dx1v + dx1_h
        _, vjp_post = jax.vjp(_rms, yv, wp)
        dy, dwp = vjp_post(dx1)
        return dx1, dy, dwp, dwn
    return _pure_call("post_pre_norm_bwd", fn, (T // TM,),
                      [_row_spec(TM, D), _row_spec(TM, D), _full_spec((1, D)), _full_spec((1, D)),
                       _row_spec(TM, D), _row_spec(TM, D)],
                      [_row_spec(TM, D), _row_spec(TM, D), _full_spec((1, D)), _full_spec((1, D))],
                      [_sds((T, D), F32), _sds((T, D), BF16), _sds((1, D), F32), _sds((1, D), F32)],
                      (y, x1, w_post, w_pre, dx1_in, dh1), n_acc=2, deps=deps)


def _final_loss(x1, y, w_post, target):
    def fn(x1v, yv, wp, tv):
        r, vjp = jax.vjp(_rms, yv, wp)
        err = x1v + r - tv
        part = 0.5 * jnp.sum(jnp.mean(err * err, axis=-1, keepdims=True), axis=0, keepdims=True)
        dx2 = err * (1.0 / D)
        dy, dwp = vjp(dx2)
        return dx2, dy, jnp.broadcast_to(part, (1, 128)), dwp
    return _pure_call("final_loss", fn, (T // TM,),
                      [_row_spec(TM, D), _row_spec(TM, D), _full_spec((1, D)), _row_spec(TM, D)],
                      [_row_spec(TM, D), _row_spec(TM, D), _full_spec((1, 128)), _full_spec((1, D))],
                      [_sds((T, D), F32), _sds((T, D), BF16), _sds((1, 128), F32), _sds((1, D), F32)],
                      (x1, y, w_post, target), n_acc=2)


def _pre_norm_bwd(x, w_row, dh, dx_res, deps=()):
    def fn(xv, w, dhv, dxv):
        _, vjp = jax.vjp(_rms, xv, w)
        dx, dw = vjp(dhv)
        return dxv + dx, dw
    return _pure_call("pre_norm_bwd", fn, (T // TM,),
                      [_row_spec(TM, D), _full_spec((1, D)), _row_spec(TM, D), _row_spec(TM, D)],
                      [_row_spec(TM, D), _full_spec((1, D))],
                      [_sds((T, D), F32), _sds((1, D), F32)], (x, w_row, dh, dx_res), n_acc=1, deps=deps)


RA = 256
RB = 512


HA = 4
A_GROUPS = A_HEADS // HA


def _head(ref, hh):
    return ref[:, hh * HD:(hh + 1) * HD]


def _hgrn2_fwd(z, l0, l1, onorm):
    nb = T // RA

    def body(q_ref, f_ref, v_ref, g_ref, l0_ref, l1_ref, on_ref, cat_ref, sst_ref, st_scr):
        @pl.when(pl.program_id(1) == 0)
        def _():
            st_scr[...] = jnp.zeros_like(st_scr)

        for hh in range(HA):
            st = st_scr[hh]
            sst_ref[hh] = st
            out, st_new = _hgrn2_rows(_head(q_ref, hh), _head(f_ref, hh), _head(v_ref, hh), _head(g_ref, hh), st,
                                      _head(l0_ref, hh), _head(l1_ref, hh), on_ref[...])
            cat_ref[:, hh * HD:(hh + 1) * HD] = out.astype(cat_ref.dtype)
            st_scr[hh] = st_new

    def cols(k):
        return pl.BlockSpec((RA, HA * HD), lambda g, r: (r, k * A_GROUPS + g))

    vec = pl.BlockSpec((1, HA * HD), lambda g, r: (0, g))
    return pl.pallas_call(
        body, name="hgrn2_fwd", grid=(A_GROUPS, nb),
        in_specs=[cols(0), cols(1), cols(2), cols(3), vec, vec, _full_spec((1, HD))],
        out_specs=[cols(0), pl.BlockSpec((HA, None, HD, HD), lambda g, r: (g, r, 0, 0))],
        out_shape=[_sds((T, 2 * A_HEADS * HD), BF16), _sds((A_HEADS, nb, HD, HD), F32)],
        scratch_shapes=[pltpu.VMEM((HA, HD, HD), F32)],
        compiler_params=_params(2))(z, z, z, z, l0, l1, onorm)


def _hgrn2_bwd(z, l0, l1, onorm, sst, dcat, deps=()):
    nb = T // RA
    deps = _live(deps)

    def body(q_ref, f_ref, v_ref, g_ref, l0_ref, l1_ref, on_ref, sst_ref, dcat_ref,
             dq_ref, df_ref, dv_ref, dg_ref, dl0_ref, dl1_ref, don_ref, ds_scr):
        g, r = pl.program_id(0), pl.program_id(1)

        @pl.when(r == 0)
        def _():
            ds_scr[...] = jnp.zeros_like(ds_scr)

        dl0s, dl1s, don = [], [], None
        for hh in range(HA):
            _, vjp = jax.vjp(_hgrn2_rows, _head(q_ref, hh), _head(f_ref, hh), _head(v_ref, hh), _head(g_ref, hh),
                             sst_ref[hh], _head(l0_ref, hh), _head(l1_ref, hh), on_ref[...])
            dq, dzf, dv, dga, dst, dl0, dl1, don_h = vjp((_head(dcat_ref, hh), ds_scr[hh]))
            for ref, val in ((dq_ref, dq), (df_ref, dzf), (dv_ref, dv), (dg_ref, dga)):
                ref[:, hh * HD:(hh + 1) * HD] = val.astype(ref.dtype)
            ds_scr[hh] = dst
            dl0s.append(dl0)
            dl1s.append(dl1)
            don = don_h if don is None else don + don_h
        dl0 = jnp.concatenate(dl0s, axis=1)
        dl1 = jnp.concatenate(dl1s, axis=1)

        @pl.when(r == 0)
        def _():
            dl0_ref[...] = dl0
            dl1_ref[...] = dl1

        @pl.when(r > 0)
        def _():
            dl0_ref[...] += dl0
            dl1_ref[...] += dl1

        first = jnp.logical_and(g == 0, r == 0)

        @pl.when(first)
        def _():
            don_ref[...] = don

        @pl.when(jnp.logical_not(first))
        def _():
            don_ref[...] += don

    def rev(k):
        return pl.BlockSpec((RA, HA * HD), lambda g, r: (nb - 1 - r, k * A_GROUPS + g))

    vec = pl.BlockSpec((1, HA * HD), lambda g, r: (0, g))
    grad = _sds((T, A_HEADS * HD), BF16)
    return pl.pallas_call(
        _skip_deps(body, 9, len(deps)), name="hgrn2_bwd", grid=(A_GROUPS, nb),
        in_specs=[rev(0), rev(1), rev(2), rev(3), vec, vec, _full_spec((1, HD)),
                  pl.BlockSpec((HA, None, HD, HD), lambda g, r: (g, nb - 1 - r, 0, 0)),
                  rev(0)] + [ANY_SPEC] * len(deps),
        out_specs=[rev(0)] * 4 + [vec, vec, _full_spec((1, HD))],
        out_shape=[grad] * 4 + [_sds((1, A_HEADS * HD), F32)] * 2 + [_sds((1, HD), F32)],
        scratch_shapes=[pltpu.VMEM((HA, HD, HD), F32)],
        compiler_params=_params(2))(z, z, z, z, l0, l1, onorm, sst, dcat, *deps)


GB = 4
B_STEPS = B_GROUPS // GB


def _gmlp_specs():
    vec = pl.BlockSpec((1, GB * HD), lambda s, r: (0, s))
    ws = pl.BlockSpec((GB, B_CHUNK, B_CHUNK), lambda s, r: (s, 0, 0))
    bias = pl.BlockSpec((GB, B_CHUNK, 1), lambda s, r: (s, 0, 0))

    def cols(k):
        return pl.BlockSpec((RB, GB * HD), lambda s, r: (r, k * B_STEPS + s))
    return vec, ws, bias, cols


def _gmlp_fwd(z, cat, lnw, lnb, ws, bias):
    vec, ws_spec, bias_spec, cols = _gmlp_specs()

    def body(u_ref, v_ref, g_ref, lnw_ref, lnb_ref, ws_ref, bias_ref, cat_in_ref, cat_ref):
        del cat_in_ref
        for gg in range(GB):
            out = _gmlp_rows(_head(u_ref, gg), _head(v_ref, gg), _head(g_ref, gg), _head(lnw_ref, gg),
                             _head(lnb_ref, gg), ws_ref[gg], bias_ref[gg])
            cat_ref[:, gg * HD:(gg + 1) * HD] = out.astype(cat_ref.dtype)

    return pl.pallas_call(
        body, name="gmlp_fwd", grid=(B_STEPS, T // RB),
        in_specs=[cols(4), cols(5), cols(6), vec, vec, ws_spec, bias_spec, pl.BlockSpec(memory_space=pl.ANY)],
        out_specs=cols(1),
        out_shape=_sds(cat.shape, cat.dtype), input_output_aliases={7: 0},
        compiler_params=_params(2))(z, z, z, lnw, lnb, ws, bias, cat)


def _gmlp_bwd(z, lnw, lnb, ws, bias, dcat):
    vec, ws_spec, bias_spec, cols = _gmlp_specs()

    def body(u_ref, v_ref, g_ref, lnw_ref, lnb_ref, ws_ref, bias_ref, dcat_ref,
             du_ref, dv_ref, dg_ref, dlnw_ref, dlnb_ref, dws_ref, dbias_ref):
        first = pl.program_id(1) == 0
        for gg in range(GB):
            _, vjp = jax.vjp(_gmlp_rows, _head(u_ref, gg), _head(v_ref, gg), _head(g_ref, gg), _head(lnw_ref, gg),
                             _head(lnb_ref, gg), ws_ref[gg], bias_ref[gg])
            du, dv, dg, dlnw, dlnb, dws, dbias = vjp(_head(dcat_ref, gg))
            lanes = slice(gg * HD, (gg + 1) * HD)
            for ref, val in ((du_ref, du), (dv_ref, dv), (dg_ref, dg)):
                ref[:, lanes] = val.astype(ref.dtype)
            sums = ((dlnw_ref, (slice(None), lanes), dlnw), (dlnb_ref, (slice(None), lanes), dlnb),
                    (dws_ref, gg, dws), (dbias_ref, gg, dbias))
            for ref, idx, val in sums:
                @pl.when(first)
                def _(ref=ref, idx=idx, val=val):
                    ref[idx] = val

                @pl.when(jnp.logical_not(first))
                def _(ref=ref, idx=idx, val=val):
                    ref[idx] += val

    grad = _sds((T, B_GROUPS * HD), BF16)
    return pl.pallas_call(
        body, name="gmlp_bwd", grid=(B_STEPS, T // RB),
        in_specs=[cols(4), cols(5), cols(6), vec, vec, ws_spec, bias_spec, cols(1)],
        out_specs=[cols(0)] * 3 + [vec, vec, ws_spec, bias_spec],
        out_shape=[grad] * 3 + [_sds((1, B_GROUPS * HD), F32)] * 2
        + [_sds((B_GROUPS, B_CHUNK, B_CHUNK), F32), _sds((B_GROUPS, B_CHUNK, 1), F32)],
        compiler_params=_params(2))(z, z, z, lnw, lnb, ws, bias, dcat)


def _mla_pre(z1, qn, kvn, cos_t, sin_t):
    def fn(cq, ckv, kpe, cs, sn, wq, wkv):
        return _rms(cq, wq), _rms(ckv, wkv), _rope(kpe, cs, sn)
    return _pure_call("mla_pre", fn, (T // TM,),
                      [_row_spec(TM, C_RANK, 4), _row_spec(TM, C_RANK, 5), _row_spec(TM, HD, 24),
                       _row_spec(TM, HD), _row_spec(TM, HD),
                       _full_spec((1, C_RANK)), _full_spec((1, C_RANK))],
                      [_row_spec(TM, C_RANK), _row_spec(TM, C_RANK), _row_spec(TM, HD)],
                      [_sds((T, C_RANK), BF16), _sds((T, C_RANK), BF16), _sds((T, HD), BF16)],
                      (z1, z1, z1, cos_t, sin_t, qn, kvn))


def _mla_pre_bwd(z1, qn, kvn, cos_t, sin_t, dcqn, dckvn, dkp, dgate, deps=()):
    def fn(cq, ckv, cs, sn, wq, wkv, g_q, g_kv, g_kp, g_gate):
        _, vjp_q = jax.vjp(_rms, cq, wq)
        dcq, dwq = vjp_q(g_q)
        _, vjp_kv = jax.vjp(_rms, ckv, wkv)
        dckv, dwkv = vjp_kv(g_kv)
        dz1 = jnp.concatenate([g_gate, dcq.astype(BF16), dckv.astype(BF16),
                               _rope_transpose(g_kp, cs, sn).astype(BF16)], axis=1)
        return dz1, dwq, dwkv
    return _pure_call("mla_pre_bwd", fn, (T // TM,),
                      [_row_spec(TM, C_RANK, 4), _row_spec(TM, C_RANK, 5),
                       _row_spec(TM, HD), _row_spec(TM, HD),
                       _full_spec((1, C_RANK)), _full_spec((1, C_RANK)),
                       _row_spec(TM, C_RANK), _row_spec(TM, C_RANK), _row_spec(TM, HD), _row_spec(TM, D)],
                      [_row_spec(TM, ODD_IN_PAD), _full_spec((1, C_RANK)), _full_spec((1, C_RANK))],
                      [_sds((T, ODD_IN_PAD), BF16), _sds((1, C_RANK), F32), _sds((1, C_RANK), F32)],
                      (z1, z1, cos_t, sin_t, qn, kvn, dcqn, dckvn, dkp, dgate), n_acc=2, deps=deps)


TQ = 256
HP = 2
KVW = C_NOPE + C_V


def _att_keys(kv_ref, kp_ref, k_scr):
    @pl.when(pl.program_id(1) == 0)
    def _():
        for hh in range(HP):
            k_scr[hh, :, 0:C_NOPE] = kv_ref[:, hh * KVW:hh * KVW + C_NOPE]
            k_scr[hh, :, C_NOPE:QP] = kp_ref[...]


def _att_scores(q, cos_ref, sin_ref, k_scr, hh, n):
    keys = (n + 1) * TQ
    qr = jnp.concatenate([q[:, :C_NOPE], _rope(q[:, C_NOPE:], cos_ref[...], sin_ref[...])], axis=1).astype(BF16)
    return qr, _raw_nt(qr, k_scr[hh, 0:keys, :]) * ATT_SCALE


def _causal(x, n, fill):
    row = lax.broadcasted_iota(jnp.int32, (TQ, TQ), 0)
    col = lax.broadcasted_iota(jnp.int32, (TQ, TQ), 1)
    diag = jnp.where(col <= row, x[:, n * TQ:], fill)
    return diag if n == 0 else jnp.concatenate([x[:, :n * TQ], diag], axis=1)


def _per_query_block(fn):
    for n in range(T // TQ):
        pl.when(pl.program_id(1) == n)(functools.partial(fn, n))


def _att_in_specs():
    return [pl.BlockSpec((TQ, HP * QP), lambda g, i: (i, g)),
            pl.BlockSpec((TQ, HD), lambda g, i: (i, 0)),
            pl.BlockSpec((TQ, HD), lambda g, i: (i, 0)),
            pl.BlockSpec((T, HP * KVW), lambda g, i: (0, g)),
            pl.BlockSpec((T, HD), lambda g, i: (0, 0))]


def _attention_fwd(q, cos_t, sin_t, kv, kp, z1):
    def body(q_ref, cos_ref, sin_ref, kv_ref, kp_ref, gate_ref, o_ref, lse_ref, og_ref, k_scr):
        _att_keys(kv_ref, kp_ref, k_scr)

        def block(n):
            keys = (n + 1) * TQ
            for hh in range(HP):
                _, s = _att_scores(q_ref[:, hh * QP:(hh + 1) * QP], cos_ref, sin_ref, k_scr, hh, n)
                s = _causal(s, n, jnp.finfo(F32).min)
                m = jnp.max(s, axis=-1, keepdims=True)
                p = jnp.exp(s - m)
                l = jnp.sum(p, axis=-1, keepdims=True)
                v = kv_ref[0:keys, hh * KVW + C_NOPE:(hh + 1) * KVW]
                o = _raw_nn(p, v) / l
                lanes = slice(hh * C_V, (hh + 1) * C_V)
                o_ref[:, lanes] = o
                og_ref[:, lanes] = (o * _silu(gate_ref[:, lanes])).astype(og_ref.dtype)
                lse_ref[hh] = m + jnp.log(l)

        _per_query_block(block)

    heads = pl.BlockSpec((TQ, HP * C_V), lambda g, i: (i, g))
    return pl.pallas_call(
        body, name="attention_fwd", grid=(C_HEADS // HP, T // TQ), in_specs=_att_in_specs() + [heads],
        out_specs=[heads, pl.BlockSpec((HP, TQ, 1), lambda g, i: (g, i, 0)), heads],
        out_shape=[_sds((T, C_HEADS * C_V), F32), _sds((C_HEADS, T, 1), F32), _sds((T, C_HEADS * C_V), BF16)],
        scratch_shapes=[pltpu.VMEM((HP, T, QP), BF16)],
        compiler_params=_params(2))(q, cos_t, sin_t, kv, kp, z1)


def _attention_bwd(q, cos_t, sin_t, kv, kp, o, lse, dog, z1):
    nq = T // TQ

    def body(q_ref, cos_ref, sin_ref, kv_ref, kp_ref, o_ref, lse_ref, dog_ref, gate_ref,
             dq_ref, dkv_ref, dkp_ref, dgate_ref, k_scr, dk_scr, dv_scr):
        g, i = pl.program_id(0), pl.program_id(1)
        _att_keys(kv_ref, kp_ref, k_scr)

        @pl.when(i == 0)
        def _():
            dv_scr[...] = jnp.zeros_like(dv_scr)
            dk_scr[...] = jnp.zeros_like(dk_scr)

        def block(n):
            keys = (n + 1) * TQ
            for hh in range(HP):
                qr, s = _att_scores(q_ref[:, hh * QP:(hh + 1) * QP], cos_ref, sin_ref, k_scr, hh, n)
                p = _causal(jnp.exp(s - lse_ref[hh]), n, 0.0)
                lanes = slice(hh * C_V, (hh + 1) * C_V)
                ov, gate, dogv = o_ref[:, lanes], gate_ref[:, lanes], dog_ref[:, lanes]
                sig = _sigmoid(gate)
                silu = gate * sig
                dov = dogv * silu
                dgate_ref[:, lanes] = (dogv * ov * (sig + silu * (1.0 - sig))).astype(dgate_ref.dtype)
                delta = jnp.sum(dov * ov, axis=-1, keepdims=True)
                dp = _raw_nt(dov, kv_ref[0:keys, hh * KVW + C_NOPE:(hh + 1) * KVW])
                ds = p * (dp - delta) * ATT_SCALE
                dq = _raw_nn(ds, k_scr[hh, 0:keys, :])
                dq_ref[:, hh * QP:(hh + 1) * QP] = jnp.concatenate(
                    [dq[:, :C_NOPE], _rope_transpose(dq[:, C_NOPE:], cos_ref[...], sin_ref[...])],
                    axis=1).astype(dq_ref.dtype)
                dv_scr[hh, 0:keys, :] += _raw_tn(p, dov)
                dk_scr[hh, 0:keys, :] += _raw_tn(ds, qr)

        _per_query_block(block)

        @pl.when(i == nq - 1)
        def _():
            for hh in range(HP):
                dkv_ref[:, hh * KVW:(hh + 1) * KVW] = jnp.concatenate(
                    [dk_scr[hh, :, 0:C_NOPE], dv_scr[hh]], axis=1).astype(dkv_ref.dtype)

        @pl.when(jnp.logical_and(i == nq - 1, g == 0))
        def _():
            dkp_ref[...] = dk_scr[0, :, C_NOPE:QP]

        @pl.when(jnp.logical_and(i == nq - 1, g > 0))
        def _():
            dkp_ref[...] += dk_scr[0, :, C_NOPE:QP]

        @pl.when(i == nq - 1)
        def _():
            for hh in range(1, HP):
                dkp_ref[...] += dk_scr[hh, :, C_NOPE:QP]

    heads = pl.BlockSpec((TQ, HP * C_V), lambda g, i: (i, g))
    return pl.pallas_call(
        body, name="attention_bwd", grid=(C_HEADS // HP, nq),
        in_specs=_att_in_specs() + [heads, pl.BlockSpec((HP, TQ, 1), lambda g, i: (g, i, 0)), heads, heads],
        out_specs=[pl.BlockSpec((TQ, HP * QP), lambda g, i: (i, g)),
                   pl.BlockSpec((T, HP * KVW), lambda g, i: (0, g)),
                   _full_spec((T, HD)), heads],
        out_shape=[_sds((T, C_HEADS * QP), BF16), _sds((T, C_HEADS * KVW), BF16), _sds((T, HD), F32),
                   _sds((T, C_HEADS * C_V), BF16)],
        scratch_shapes=[pltpu.VMEM((HP, T, QP), BF16), pltpu.VMEM((HP, T, QP), F32), pltpu.VMEM((HP, T, C_V), F32)],
        compiler_params=_params(2))(q, cos_t, sin_t, kv, kp, o, lse, dog, z1)


def _adamw_math(w, g, m, v):
    m = ADAM_B1 * m + (1.0 - ADAM_B1) * g
    v = ADAM_B2 * v + (1.0 - ADAM_B2) * (g * g)
    m_hat = m / (1.0 - ADAM_B1 ** ADAM_STEP)
    v_hat = v / (1.0 - ADAM_B2 ** ADAM_STEP)
    delta = -ADAM_LR * (m_hat / (jnp.sqrt(v_hat) + ADAM_EPS) + ADAM_WD * w)
    return delta, m, v


def _adamw(name, parts, w, m, v, tr, tc=None):
    rows, cols = w.shape

    def fn(*vals):
        pvs, (wv, mv, vv) = vals[:len(parts)], vals[len(parts):]
        g = None
        for pv in pvs:
            for d in range(pv.shape[0]):
                term = pv[d].astype(F32)
                g = term if g is None else g + term
        return (g,) + _adamw_math(wv, g, mv, vv)

    tc = cols if tc is None else tc
    blk = pl.BlockSpec((tr, tc), lambda i, j: (i, j))
    part_specs = [pl.BlockSpec((n, tr, tc), lambda i, j: (0, i, j)) for _, n in parts]
    return _pure_call(name, fn, (rows // tr, cols // tc), part_specs + [blk, blk, blk],
                      [blk] * 4, [_sds((rows, cols), F32)] * 4, tuple(p for p, _ in parts) + (w, m, v))


SMALL_PARAM_SHAPES = ((2, D), (2, D), (2, A_HEADS * HD), (1, HD), (1, B_GROUPS * HD), (1, B_GROUPS * HD),
                      (B_GROUPS, B_CHUNK, B_CHUNK), (B_GROUPS, B_CHUNK))
SMALL_PIECES = ((0, 0, 0, 0), (0, 1, 1, 0), (1, 0, 1, 1), (1, 1, 1, 2), (2, 0, 2, 0), (2, 1, 2, 1),
                (3, 0, 3, 8), (4, 0, 2, 2), (5, 0, 2, 3))


def _small_rows(dnpre1, dnpost0, dnpost1, dl0, dl1, donorm, dlnw, dlnb, dws, dbias, dqn, dkvn, loss_part):
    return [jnp.concatenate([dnpre1, dnpost0, dnpost1], axis=0),
            jnp.concatenate([dl0, dl1, dlnw, dlnb], axis=0),
            jnp.concatenate([dbias.reshape(B_GROUPS, B_CHUNK), donorm, loss_part], axis=0),
            dws,
            jnp.concatenate([dqn, dkvn], axis=0)]


def _adamw_small(late_all, early_all, wmv):
    n_in = 6 + 3 * len(wmv)

    def body(*refs):
        gathered, params, outs = refs[:6], refs[6:n_in], refs[n_in:]

        def total(ref):
            s = ref[0]
            for d in range(1, N_DEV):
                s = s + ref[d]
            return s

        g_late, g2048, g1024, g128, g_ws, g512 = [total(r) for r in gathered]
        arrays = (g_late, g2048, g1024, g128)

        def update(p, rows, g):
            w_ref, m_ref, v_ref = params[3 * p:3 * p + 3]
            delta, m, v = _adamw_math(w_ref[rows], g, m_ref[rows], v_ref[rows])
            for out, val in zip(outs[4 * p:4 * p + 4], (g, delta, m, v)):
                out[rows] = val

        for p, row, arr, arr_row in SMALL_PIECES:
            update(p, pl.ds(row, 1), arrays[arr][arr_row:arr_row + 1])
        update(6, slice(None), g_ws)
        update(7, slice(None), g128[0:B_GROUPS])
        outs[32][...] = g128[B_GROUPS + 1:B_GROUPS + 2]
        outs[33][...] = g512

    vmem = pl.BlockSpec(memory_space=pltpu.VMEM)
    flat = [a for t in wmv for a in t]
    out_shape = [_sds(s, F32) for s in SMALL_PARAM_SHAPES for _ in range(4)] + [_sds((1, 128), F32), _sds((2, C_RANK), F32)]
    res = pl.pallas_call(body, name="adamw_small", in_specs=[vmem] * n_in, out_specs=[vmem] * len(out_shape),
                         out_shape=out_shape,
                         compiler_params=pltpu.CompilerParams(vmem_limit_bytes=VMEM_LIMIT_V7X))(late_all, *early_all, *flat)
    return [res[4 * p:4 * p + 4] for p in range(8)], res[32], res[33]


def _exchange(name, arrs, gather, deps=()):
    n = len(arrs)
    deps = _live(deps)

    def body(*refs):
        ins, outs = refs[:n], refs[n + len(deps):2 * n + len(deps)]
        send_sems, recv_sems, local_sems = refs[2 * n + len(deps):]
        x, y, c = lax.axis_index("x"), lax.axis_index("y"), lax.axis_index("c")
        me = 4 * x + 2 * y + c

        def peer(k):
            return (x ^ (k >> 2), y ^ ((k >> 1) & 1), c ^ (k & 1))

        def copy(a, k):
            src = ins[a] if gather else ins[a].at[me ^ k]
            return pltpu.make_async_remote_copy(
                src_ref=src, dst_ref=outs[a].at[me], send_sem=send_sems.at[a, k - 1],
                recv_sem=recv_sems.at[a, k - 1], device_id=peer(k), device_id_type=MESH_ID)

        def arrival(a, k):
            src = ins[a] if gather else ins[a].at[me]
            return pltpu.make_async_remote_copy(
                src_ref=src, dst_ref=outs[a].at[me ^ k], send_sem=send_sems.at[a, k - 1],
                recv_sem=recv_sems.at[a, k - 1], device_id=peer(k), device_id_type=MESH_ID)

        own = [pltpu.make_async_copy(ins[a] if gather else ins[a].at[me], outs[a].at[me], local_sems.at[a])
               for a in range(n)]
        for cp in own:
            cp.start()
        for k in range(1, N_DEV):
            for a in range(n):
                copy(a, k).start()
        for k in range(1, N_DEV):
            for a in range(n):
                arrival(a, k).wait_recv()
        for k in range(1, N_DEV):
            for a in range(n):
                copy(a, k).wait_send()
        for cp in own:
            cp.wait()

    any_spec = pl.BlockSpec(memory_space=pl.ANY)
    out_shape = [_sds((N_DEV,) + a.shape if gather else a.shape, a.dtype) for a in arrs]
    return pl.pallas_call(
        body, name=name, in_specs=[any_spec] * (n + len(deps)), out_specs=[any_spec] * n, out_shape=out_shape,
        scratch_shapes=[pltpu.SemaphoreType.DMA((n, N_DEV - 1)), pltpu.SemaphoreType.DMA((n, N_DEV - 1)),
                        pltpu.SemaphoreType.DMA((n,))],
        compiler_params=pltpu.CompilerParams(has_side_effects=True))(*arrs, *deps)


HBM_SPEC = pl.BlockSpec(memory_space=pltpu.HBM)
SEM_SPEC = pl.BlockSpec(memory_space=pltpu.SEMAPHORE)
DATAFLOW = pltpu.SideEffectType.DATAFLOW_SIDE_EFFECTING


def _my_index():
    return 4 * lax.axis_index("x") + 2 * lax.axis_index("y") + lax.axis_index("c")


def _plan_copies(plan, refs, send_sems, recv_sems):
    x, y, c = lax.axis_index("x"), lax.axis_index("y"), lax.axis_index("c")
    return [pltpu.make_async_remote_copy(
        src_ref=src, dst_ref=dst, send_sem=send_sems.at[i], recv_sem=recv_sems.at[i],
        device_id=(x ^ (k >> 2), y ^ ((k >> 1) & 1), c ^ (k & 1)), device_id_type=MESH_ID)
        for i, (src, dst, k) in enumerate(plan(refs, 4 * x + 2 * y + c))]


def _split_call(name, bufs, waits=None, starts=None, deps=()):
    n = len(bufs)
    deps = _live(deps)
    n_wait = 2 if waits else 0

    def body(*refs):
        zones = refs[:n]
        if waits:
            for cp in _plan_copies(waits[2], zones, refs[n], refs[n + 1]):
                cp.wait_send()
                cp.wait_recv()
        if starts:
            first_out = n + n_wait + len(deps)
            for cp in _plan_copies(starts[0], zones, refs[first_out], refs[first_out + 1]):
                cp.start()
            refs[-1][...] = jnp.zeros_like(refs[-1])

    out_specs, out_shape = [], []
    if starts:
        sems = pltpu.SemaphoreType.DMA((starts[1],))
        out_specs, out_shape = [SEM_SPEC, SEM_SPEC], [sems, sems]
    out_specs += [HBM_SPEC] * n
    out_shape += [pltpu.HBM(b.shape, b.dtype) for b in bufs]
    if starts:
        out_specs.append(pl.BlockSpec(memory_space=pltpu.VMEM))
        out_shape.append(_sds((8, 128), F32))
    first_buf = 2 if starts else 0
    res = pl.pallas_call(
        body, name=name,
        in_specs=[HBM_SPEC] * n + [SEM_SPEC] * n_wait + [ANY_SPEC] * len(deps),
        out_specs=out_specs, out_shape=out_shape,
        input_output_aliases={i: first_buf + i for i in range(n)},
        compiler_params=pltpu.CompilerParams(has_side_effects=DATAFLOW),
    )(*[pltpu.with_memory_space_constraint(b, pltpu.HBM) for b in bufs], *(waits[:2] if waits else ()), *deps)
    out_bufs = list(res[first_buf:first_buf + n])
    return out_bufs, ((res[0], res[1]) if starts else None), (res[-1] if starts else None)


def _direct_plan(n, gather):
    def plan(refs, me):
        return [(refs[a] if gather else refs[a].at[me ^ k], refs[n + a].at[me], k)
                for k in range(1, N_DEV) for a in range(n)]
    return plan


def _own_slot_filled(a, gather):
    me = _my_index()
    if gather:
        return lax.dynamic_update_slice_in_dim(lax.empty((N_DEV,) + a.shape, a.dtype), a[None], me, 0)
    return lax.dynamic_update_slice_in_dim(lax.empty(a.shape, a.dtype), lax.dynamic_slice_in_dim(a, me, 1, 0), me, 0)


def _exchange_start(name, arrs, gather, deps=()):
    n = len(arrs)
    lands = [_own_slot_filled(a, gather) for a in arrs]
    plan = _direct_plan(n, gather)
    bufs, sems, token = _split_call(name, list(arrs) + lands, starts=(plan, n * (N_DEV - 1)), deps=deps)
    return (n, plan, sems, bufs, None), token


def _exchange_wait(name, handle, after):
    return _split_done(name, handle, after)


ICI_PEERS = (2, 4, 6)
SIBLING = 1


def _gather2_send(name, arrs, deps=()):
    n = len(arrs)
    lands = [_own_slot_filled(a, True) for a in arrs]

    def plan(refs, me_):
        return [(refs[a], refs[n + a].at[me_], k) for k in (SIBLING,) + ICI_PEERS for a in range(n)]

    bufs, sems, token = _split_call(name, list(arrs) + lands, starts=(plan, 4 * n), deps=deps)
    return (n, plan, sems, bufs, None), token


def _gather2_relay(name, handle, after):
    n, plan, sems, bufs, _ = handle
    after = after if isinstance(after, (list, tuple)) else [after]

    def relay(refs, me_):
        return [(refs[n + a].at[me_ ^ k], refs[n + a].at[me_ ^ k], SIBLING) for k in ICI_PEERS for a in range(n)]

    bufs, sems2, token = _split_call(name, bufs, waits=(sems[0], sems[1], plan), starts=(relay, 3 * n), deps=after)
    return (n, relay, sems2, bufs, None), token


def _split_done(name, handle, after, all_bufs=False):
    n, plan, sems, bufs, _ = handle
    bufs, _, _ = _split_call(name, bufs, waits=(sems[0], sems[1], plan), deps=[after])
    return bufs if all_bufs else bufs[n:]


def _scatter2_pair(name, for_sibling, deps=()):
    n = len(for_sibling)
    pairs = [lax.empty(s.shape, s.dtype) for s in for_sibling]

    def plan(refs, me):
        del me
        return [(refs[a].at[s], refs[n + a].at[s], SIBLING) for s in range(4) for a in range(n)]

    bufs, sems, token = _split_call(name, list(for_sibling) + pairs, starts=(plan, 4 * n), deps=deps)
    return (n, plan, sems, bufs, None), token


def _pair_add(name, mine, pair):
    _, rows, cols = mine.shape
    tr = rows // 2

    def fn(a, b):
        return a.astype(F32) + b.astype(F32)

    blk = pl.BlockSpec((None, tr, cols), lambda s, i: (s, i, 0))
    return _pure_call(name, fn, (4, rows // tr), [blk, blk], [blk], [_sds(mine.shape, mine.dtype)], (mine, pair))[0]


def _scatter2_send(name, chip_sums, deps=()):
    n = len(chip_sums)
    finals = [lax.empty((3,) + c.shape[1:], c.dtype) for c in chip_sums]

    def plan(refs, me):
        return [(refs[a].at[(me >> 1) ^ j], refs[n + a].at[j - 1], 2 * j) for j in range(1, 4) for a in range(n)]

    bufs, sems, token = _split_call(name, list(chip_sums) + finals, starts=(plan, 3 * n), deps=deps)
    return (n, plan, sems, bufs, None), token


def _pad_rope(p):
    z = jnp.zeros(p.shape[:-1] + (32,), p.dtype)
    return jnp.concatenate([p[..., :32], z, p[..., 32:], z], axis=-1)


def _unpad_rope(p):
    return jnp.concatenate([p[..., :32], p[..., 64:96]], axis=-1)


def _odd_in_layout(wt):
    wt = wt.reshape(ODD_IN, D)
    cq, ckv, kpe, gate = wt[:512], wt[512:1024], wt[1024:1088], wt[1088:]
    z = jnp.zeros((32, D), wt.dtype)
    return jnp.concatenate([gate, cq, ckv, kpe[:32], z, kpe[32:], z], axis=0)


def _odd_in_unlayout(dwt):
    gate, cq, ckv, kpe = dwt[:2048], dwt[2048:2560], dwt[2560:3072], dwt[3072:]
    wt = jnp.concatenate([cq, ckv, kpe[:32], kpe[64:96], gate], axis=0)
    return wt.reshape(N_DEV, ODD_IN // N_DEV, D)


def _qb_layout(w):
    w = w.transpose(1, 0, 2).reshape(C_RANK, C_HEADS, C_QK)
    w = jnp.concatenate([w[..., :C_NOPE], _pad_rope(w[..., C_NOPE:])], axis=-1)
    return w.reshape(C_RANK, C_HEADS * QP)


def _qb_unlayout(dw):
    dw = dw.reshape(C_RANK, C_HEADS, QP)
    dw = jnp.concatenate([dw[..., :C_NOPE], _unpad_rope(dw[..., C_NOPE:])], axis=-1)
    return dw.reshape(C_RANK, N_DEV, C_HEADS * C_QK // N_DEV).transpose(1, 0, 2)


def _rope_tables(positions):
    inv_freq = ROPE_THETA ** (-jnp.arange(0, C_ROPE, 2, dtype=F32) / C_ROPE)
    ang = positions.astype(F32)[0][:, None] * inv_freq
    cos, sin = jnp.cos(ang), jnp.sin(ang)
    z = jnp.zeros_like(cos)
    return jnp.concatenate([cos, z, cos, z], axis=1), jnp.concatenate([-sin, z, sin, z], axis=1)


def _forward_backward(x, cos_t, sin_t, target, norm_pre, norm_post, lb_logits, a_onorm, ln_w, ln_b,
                      b_ws, b_bias, get_w, put_g, put_small=None, start_dep=None):
    npre0, npre1 = norm_pre[0:1], norm_pre[1:2]
    npost0, npost1 = norm_post[0:1], norm_post[1:2]
    l0, l1 = lb_logits[0:1], lb_logits[1:2]
    bias_col = b_bias.reshape(B_GROUPS, B_CHUNK, 1)
    ws = b_ws.reshape(B_GROUPS, B_CHUNK, B_CHUNK)

    h0 = _pre_norm("pre_norm0", x, npre0, deps=[start_dep])
    w_ev_in = get_w("ev_in", h0)
    z0 = _mm_nn("ev_in", h0, w_ev_in, F32, 1024, 896)
    cat, sst = _hgrn2_fwd(z0, l0, l1, a_onorm)
    cat = _gmlp_fwd(z0, cat, ln_w, ln_b, ws, bias_col)
    w_ev_out = get_w("ev_out", cat)
    y0 = _mm_nn("ev_out", cat, w_ev_out, F32, 1024, 1024)
    get_w("od_relay", y0)
    x1, h1 = _post_pre_norm(x, y0, npost0, npre1)
    w_od_in, w_qb, w_kvb, q_norm, kv_norm = get_w("od_mid", h1)
    z1 = _mm_nt("od_in", h1, w_od_in[None], F32, 1024, 640)
    cqn, ckvn, kp = _mla_pre(z1, q_norm, kv_norm, cos_t, sin_t)
    q = _mm_nn("od_qb", cqn, w_qb[None], F32, 1024, 1024)
    kv = _mm_nn("od_kvb", ckvn, w_kvb, BF16, 1024, 512)
    o, lse, og = _attention_fwd(q, cos_t, sin_t, kv, kp, z1)
    w_od_out = get_w("od_out", og)
    y1 = _mm_nn("od_out", og, w_od_out, F32, 1024, 1024)
    dx2, dy1, loss_part, dnpost1 = _final_loss(x1, y1, npost1, target)

    g_od_out = _mm_tn("od_out_dw", og, dy1, 1, BF16, 1024, 1024)
    tok = put_g("od_out", [g_od_out.reshape(N_DEV, D // N_DEV, D)])
    dog = _mm_nt("od_out_dx", dy1, w_od_out, F32, 1024, 1024, deps=[tok])
    dq, dkv, dkp, dgate = _attention_bwd(q, cos_t, sin_t, kv, kp, o, lse, dog, z1)
    g_qb = _mm_tn("od_qb_dw", cqn, dq, 1, F32, 512, 1024)
    g_kvb = _mm_tn("od_kvb_dw", ckvn, dkv, N_DEV, BF16, 512, 512)
    tok = put_g("od_qkv", [_qb_unlayout(g_qb[0]).astype(BF16), g_kvb])
    dcqn = _mm_nt("od_qb_dx", dq, w_qb[None], F32, 1024, 512, deps=[tok])
    dckvn = _mm_nt("od_kvb_dx", dkv, w_kvb, F32, 1024, 512)
    dz1, dqn, dkvn = _mla_pre_bwd(z1, q_norm, kv_norm, cos_t, sin_t, dcqn, dckvn, dkp, dgate)
    g_od_in = _mm_tn("od_in_dw", dz1, h1, 1, F32, 640, 1024)
    tok = put_g("od_in", [_odd_in_unlayout(g_od_in[0]).astype(BF16)])
    dh1 = _mm_nn("od_in_dx", dz1, w_od_in[None], F32, 1024, 1024, deps=[tok])
    dx1, dy0, dnpost0, dnpre1 = _post_pre_norm_bwd(y0, x1, npost0, npre1, dx2, dh1)

    g_ev_out = _mm_tn("ev_out_dw", cat, dy0, 1, BF16, 1024, 1024)
    tok = put_g("ev_out", [g_ev_out.reshape(N_DEV, D // N_DEV, D)])
    dcat = _mm_nt("ev_out_dx", dy0, w_ev_out, F32, 1024, 1024, deps=[tok])
    dqa, dfa, dia, dga, dl0, dl1, donorm = _hgrn2_bwd(z0, l0, l1, a_onorm, sst, dcat)
    dub, dvb, dgb, dlnw, dlnb, dws, dbias = _gmlp_bwd(z0, ln_w, ln_b, ws, bias_col, dcat)
    dz0 = jnp.concatenate([dqa, dfa, dia, dga, dub, dvb, dgb], axis=1)
    early = _small_rows(dnpre1, dnpost0, dnpost1, dl0, dl1, donorm, dlnw, dlnb, dws, dbias, dqn, dkvn, loss_part)
    tok = put_small(early) if put_small else None
    small_tok = tok

    def ev_in_half(name, parity, deps=()):
        return _mm_tn_parity(name, h0, dz0, N_DEV, parity, BF16, 1024, deps=[small_tok] + list(deps))

    tok = put_g("ev_in", ev_in_half)
    dh0 = _mm_nt("ev_in_dx", dz0, w_ev_in, F32, 1024, 256, deps=[tok])
    grad_x, dnpre0 = _pre_norm_bwd(x, npre0, dh0, dx1)
    return grad_x, early, dnpre0


def kernel(x, positions, norm_pre, norm_post, ev_w_in, ev_lb_logits, ev_a_onorm, ev_b_ln_w, ev_b_ln_b, ev_b_ws, ev_b_bias, ev_w_out, od_w_in, od_q_norm, od_w_qb, od_kv_norm, od_w_kvb, od_w_out, loss_target, m_norm_pre, m_norm_post, m_ev_w_in, m_ev_lb_logits, m_ev_a_onorm, m_ev_b_ln_w, m_ev_b_ln_b, m_ev_b_ws, m_ev_b_bias, m_ev_w_out, m_od_w_in, m_od_q_norm, m_od_w_qb, m_od_kv_norm, m_od_w_kvb, m_od_w_out, v_norm_pre, v_norm_post, v_ev_w_in, v_ev_lb_logits, v_ev_a_onorm, v_ev_b_ln_w, v_ev_b_ln_b, v_ev_b_ws, v_ev_b_bias, v_ev_w_out, v_od_w_in, v_od_q_norm, v_od_w_qb, v_od_kv_norm, v_od_w_kvb, v_od_w_out):
    me = 4 * lax.axis_index("x") + 2 * lax.axis_index("y") + lax.axis_index("c")
    bf = lambda w: w[0].astype(BF16)

    norms = jnp.pad(jnp.concatenate([od_q_norm, od_kv_norm], axis=1), ((0, 7), (0, 0)))
    sent = {}
    sent["ev_in"], tok = _gather2_send("gather_ev_in", [bf(ev_w_in)])
    sent["ev_out"], tok = _gather2_send("gather_ev_out", [bf(ev_w_out)], deps=[tok])
    sent["od"], tok = _gather2_send("gather_od", [od_w_in[0].T.astype(BF16), bf(od_w_qb), bf(od_w_kvb), norms,
                                                 bf(od_w_out)], deps=[tok])
    cos_t, sin_t = _rope_tables(positions)
    od = []

    def get_w(group, after):
        if group == "ev_in":
            relayed, token = _gather2_relay("relay_ev_in", sent["ev_in"], [after, cos_t, sin_t])
            return _split_done("arrived_ev_in", relayed, token)[0]
        if group == "ev_out":
            relayed, token = _gather2_relay("relay_ev_out", sent["ev_out"], after)
            return _split_done("arrived_ev_out", relayed, token)[0].reshape(1, D, D)
        if group == "od_relay":
            sent["od_relayed"], _ = _gather2_relay("relay_od", sent["od"], after)
            return None
        if not od:
            od.extend(_split_done("arrived_od", sent["od_relayed"], after))
        w_od_in, w_qb, w_kvb, norms_all, w_od_out = od
        if group == "od_out":
            return w_od_out.reshape(1, D, D)
        return (_odd_in_layout(w_od_in), _qb_layout(w_qb), w_kvb,
                norms_all[:, 0, :64].reshape(1, C_RANK), norms_all[:, 0, 64:].reshape(1, C_RANK))

    scatters = {}

    def put_g(group, grads):
        if group == "ev_in":
            core = lax.axis_index("c").astype(jnp.int32).reshape(1)
            paired, token = _scatter2_pair("pair_ev_in", [grads("ev_in_dw_sibling", 1 - core)])
            mine = grads("ev_in_dw_own", core, deps=[token])
            pair = _split_done("paired_ev_in", paired, mine)[0]
            scatters[group], token = _scatter2_send("scatter_ev_in", [_pair_add("pair_add_ev_in", mine, pair)])
        else:
            scatters[group], token = _exchange_start("scatter_" + group, grads, False)
        return token

    def put_small(early):
        scatters["small"], token = _exchange_start("gather_small_early", early, True)
        return token

    grad_x, _, dnpre0 = _forward_backward(
        x[0], cos_t, sin_t, loss_target[0], norm_pre, norm_post, ev_lb_logits, ev_a_onorm, ev_b_ln_w,
        ev_b_ln_b, ev_b_ws, ev_b_bias, get_w, put_g, put_small, start_dep=tok)

    big_w = {"ev_w_in": ev_w_in, "ev_w_out": ev_w_out, "od_w_in": od_w_in, "od_w_qb": od_w_qb,
             "od_w_kvb": od_w_kvb, "od_w_out": od_w_out}
    big_m = {"ev_w_in": m_ev_w_in, "ev_w_out": m_ev_w_out, "od_w_in": m_od_w_in, "od_w_qb": m_od_w_qb,
             "od_w_kvb": m_od_w_kvb, "od_w_out": m_od_w_out}
    big_v = {"ev_w_in": v_ev_w_in, "ev_w_out": v_ev_w_out, "od_w_in": v_od_w_in, "od_w_qb": v_od_w_qb,
             "od_w_kvb": v_od_w_kvb, "od_w_out": v_od_w_out}
    big_out = {}
    after = grad_x
    for group, names in (("od_out", ["od_w_out"]), ("od_qkv", ["od_w_qb", "od_w_kvb"]), ("od_in", ["od_w_in"]),
                         ("ev_out", ["ev_w_out"])):
        parts = _exchange_wait("summed_" + group, scatters[group], after)
        for nm, p in zip(names, parts):
            w, m, v = big_w[nm][0], big_m[nm][0], big_v[nm][0]
            if nm == "od_w_in":
                res_t = _adamw("adamw_" + nm, [(p, N_DEV)], w.T, m.T, v.T, w.shape[1], 512)
                big_out[nm] = [r.T[None] for r in res_t]
            else:
                big_out[nm] = [r[None] for r in _adamw("adamw_" + nm, [(p, N_DEV)], w, m, v, w.shape[0] // 8)]
            after = big_out[nm][0]

    late_all = _exchange("gather_small_late", [dnpre0], gather=True, deps=[after])[0]
    early_all = _exchange_wait("arrived_small_early", scatters["small"], late_all)

    small_w = (norm_pre, norm_post, ev_lb_logits, ev_a_onorm, ev_b_ln_w, ev_b_ln_b, ev_b_ws, ev_b_bias)
    small_m = (m_norm_pre, m_norm_post, m_ev_lb_logits, m_ev_a_onorm, m_ev_b_ln_w, m_ev_b_ln_b, m_ev_b_ws, m_ev_b_bias)
    small_v = (v_norm_pre, v_norm_post, v_ev_lb_logits, v_ev_a_onorm, v_ev_b_ln_w, v_ev_b_ln_b, v_ev_b_ws, v_ev_b_bias)
    wmv = [tuple(a.reshape(s) for a in t) for s, t in zip(SMALL_PARAM_SHAPES, zip(small_w, small_m, small_v))]
    small_res, loss_row, g_norm_rows = _adamw_small(late_all, early_all, wmv)
    small_out = [[r.reshape(w.shape) for r in four] for four, w in zip(small_res, small_w)]
    loss = loss_row[0, 0]

    g_norms = jnp.concatenate([lax.dynamic_slice(g_norm_rows, (0, 64 * me), (1, 64)),
                               lax.dynamic_slice(g_norm_rows, (1, 64 * me), (1, 64))], axis=1)
    res_n = _adamw("adamw_norms", [(g_norms[None], 1)],
                   jnp.concatenate([od_q_norm, od_kv_norm], axis=1),
                   jnp.concatenate([m_od_q_norm, m_od_kv_norm], axis=1),
                   jnp.concatenate([v_od_q_norm, v_od_kv_norm], axis=1), 1)
    qn_out = [r[:, :64] for r in res_n]
    kvn_out = [r[:, 64:] for r in res_n]

    chip_sums, from_peers = _split_done("summed_ev_in", scatters["ev_in"], loss_row, all_bufs=True)
    own_chip = lax.dynamic_slice_in_dim(chip_sums, me >> 1, 1, 0)
    w = ev_w_in[0]
    big_out["ev_w_in"] = [r[None] for r in _adamw("adamw_ev_w_in", [(own_chip, 1), (from_peers, 3)], w, m_ev_w_in[0],
                                                  v_ev_w_in[0], w.shape[0] // 8)]

    order = ("norm_pre", "norm_post", "ev_w_in", "ev_lb_logits", "ev_a_onorm", "ev_b_ln_w", "ev_b_ln_b",
             "ev_b_ws", "ev_b_bias", "ev_w_out", "od_w_in", "od_q_norm", "od_w_qb", "od_kv_norm",
             "od_w_kvb", "od_w_out")
    small_names = ("norm_pre", "norm_post", "ev_lb_logits", "ev_a_onorm", "ev_b_ln_w", "ev_b_ln_b",
                   "ev_b_ws", "ev_b_bias")
    outs = [loss, grad_x[None]]
    for kind in range(4):
        for nm in order:
            if nm in big_out:
                outs.append(big_out[nm][kind])
            elif nm == "od_q_norm":
                outs.append(qn_out[kind])
            elif nm == "od_kv_norm":
                outs.append(kvn_out[kind])
            else:
                outs.append(small_out[small_names.index(nm)][kind])
    return tuple(outs)
```

```python
import functools

import jax
import jax.numpy as jnp
from jax import lax
from jax.experimental import pallas as pl
from jax.experimental.pallas import tpu as pltpu

F32 = jnp.float32
BF16 = jnp.bfloat16

N_DEV = 8
T = 2048
D = 2048
EPS = 1e-6
A_HEADS = 8
HD = 128
A_CHUNK = 64
A_SUB = 16
B_GROUPS = 8
B_CHUNK = 128
EVEN_IN = 7168
C_HEADS = 16
C_RANK = 512
C_NOPE = 128
C_ROPE = 64
C_QK = C_NOPE + C_ROPE
C_V = 128
ODD_IN = 3136
ODD_IN_PAD = 3200
QP = 256
ROPE_THETA = 10000.0
ATT_SCALE = C_QK ** -0.5

ADAM_LR = 0.001
ADAM_B1 = 0.9
ADAM_B2 = 0.999
ADAM_EPS = 1e-08
ADAM_WD = 0.01
ADAM_STEP = 10

VMEM_LIMIT_V7X = 56 * 1024 * 1024
MESH_ID = pl.DeviceIdType.MESH


def _params(n_grid):
    return pltpu.CompilerParams(dimension_semantics=("arbitrary",) * n_grid,
                                vmem_limit_bytes=VMEM_LIMIT_V7X)


def _dg(a, b, ca, cb):
    return lax.dot_general(a.astype(BF16), b.astype(BF16), (((ca,), (cb,)), ((), ())),
                           preferred_element_type=F32)


def _raw_nn(a, b):
    return _dg(a, b, 1, 0)


def _raw_nt(a, b):
    return _dg(a, b, 1, 1)


def _raw_tn(a, b):
    return _dg(a, b, 0, 0)


@jax.custom_vjp
def _dot_nn(a, b):
    return _raw_nn(a, b)


def _dot_nn_fwd(a, b):
    return _raw_nn(a, b), (a.astype(BF16), b.astype(BF16))


def _dot_nn_bwd(res, g):
    a, b = res
    return _raw_nt(g, b), _raw_tn(a, g)


_dot_nn.defvjp(_dot_nn_fwd, _dot_nn_bwd)


@jax.custom_vjp
def _dot_nt(a, b):
    return _raw_nt(a, b)


def _dot_nt_fwd(a, b):
    return _raw_nt(a, b), (a.astype(BF16), b.astype(BF16))


def _dot_nt_bwd(res, g):
    a, b = res
    return _raw_nn(g, b), _raw_tn(g, a)


_dot_nt.defvjp(_dot_nt_fwd, _dot_nt_bwd)


@jax.custom_vjp
def _dot_tn(a, b):
    return _raw_tn(a, b)


def _dot_tn_fwd(a, b):
    return _raw_tn(a, b), (a.astype(BF16), b.astype(BF16))


def _dot_tn_bwd(res, g):
    a, b = res
    return _raw_nt(b, g), _raw_nn(a, g)


_dot_tn.defvjp(_dot_tn_fwd, _dot_tn_bwd)


@jax.custom_vjp
def _sigmoid(x):
    e = jnp.exp(-jnp.abs(x))
    return jnp.where(x >= 0, 1.0 / (1.0 + e), e / (1.0 + e))


def _sigmoid_fwd(x):
    s = _sigmoid(x)
    return s, s


def _sigmoid_bwd(s, g):
    return (g * s * (1.0 - s),)


_sigmoid.defvjp(_sigmoid_fwd, _sigmoid_bwd)


def _silu(x):
    return x * _sigmoid(x)


def _rms(x, w):
    return x * lax.rsqrt(jnp.mean(x * x, axis=-1, keepdims=True) + EPS) * w


def _split3(x):
    hi = x.astype(BF16)
    r = x - hi.astype(F32)
    mid = r.astype(BF16)
    lo = (r - mid.astype(F32)).astype(BF16)
    return hi, mid, lo


def _mask_apply(mask_bf16, x, contract):
    out = None
    for piece in _split3(x):
        d = lax.dot_general(mask_bf16, piece, (((contract,), (0,)), ((), ())),
                            preferred_element_type=F32)
        out = d if out is None else out + d
    return out


def _chunk_tri(rows):
    r = lax.broadcasted_iota(jnp.int32, (rows, rows), 0)
    c = lax.broadcasted_iota(jnp.int32, (rows, rows), 1)
    return ((r >= c) & (r // A_CHUNK == c // A_CHUNK)).astype(BF16)


@jax.custom_vjp
def _chunk_cumsum(x):
    return _mask_apply(_chunk_tri(x.shape[0]), x, 1)


def _chunk_cumsum_fwd(x):
    return _chunk_cumsum(x), None


def _chunk_cumsum_bwd(_, g):
    return (_mask_apply(_chunk_tri(g.shape[0]), g, 0),)


_chunk_cumsum.defvjp(_chunk_cumsum_fwd, _chunk_cumsum_bwd)


def _hgrn2_rows(q, zf, v, ga, st, l0, l1, onorm):
    rows = q.shape[0]
    n_sub = A_CHUNK // A_SUB
    mx = jnp.maximum(l0, l1)
    e0 = jnp.exp(l0 - mx)
    e1 = jnp.exp(l1 - mx)
    lb = e0 / (e0 + e1)
    lf = jnp.log(lb + (1.0 - lb) * _sigmoid(zf))
    k = (1.0 - lb) * _sigmoid(-zf)
    b = _chunk_cumsum(lf)

    t_idx = lax.broadcasted_iota(jnp.int32, (A_CHUNK, n_sub * A_CHUNK), 0)
    c_idx = lax.broadcasted_iota(jnp.int32, (A_CHUNK, n_sub * A_CHUNK), 1)
    sel = (c_idx // A_CHUNK == t_idx // A_SUB) & (c_idx % A_CHUNK <= t_idx)
    key_row = lax.broadcasted_iota(jnp.int32, (A_CHUNK, HD), 0)

    outs = []
    for n in range(rows // A_CHUNK):
        lo = n * A_CHUNK
        qc, kc, vc = q[lo:lo + A_CHUNK], k[lo:lo + A_CHUNK], v[lo:lo + A_CHUNK]
        lfc, bc = lf[lo:lo + A_CHUNK], b[lo:lo + A_CHUNK]
        b_last = bc[A_CHUNK - 1:A_CHUNK]
        o_inter = _dot_nt(qc * jnp.exp(bc), st)
        kv_t = _dot_tn(vc, kc * jnp.exp(b_last - bc))
        st = st * jnp.exp(b_last) + kv_t
        g_rows, k_subs = [], []
        for i in range(n_sub):
            g_i = bc[i * A_SUB:i * A_SUB + 1] - lfc[i * A_SUB:i * A_SUB + 1]
            g_rows.append(jnp.broadcast_to(g_i, (A_SUB, HD)))
            expo = jnp.where(key_row < (i + 1) * A_SUB, g_i - bc, -jnp.inf)
            k_subs.append(kc * jnp.exp(expo))
        q_sub = qc * jnp.exp(bc - jnp.concatenate(g_rows, axis=0))
        scores = _dot_nt(q_sub, jnp.concatenate(k_subs, axis=0))
        scores = jnp.where(sel, scores, 0.0)
        o_intra = _dot_nn(scores, jnp.concatenate([vc] * n_sub, axis=0))
        outs.append(o_inter + o_intra)
    o = jnp.concatenate(outs, axis=0)
    return _rms(o, onorm) * _silu(ga), st


def _gmlp_rows(u, vb, gb, lnw, lnb, ws, bias):
    rows = u.shape[0]
    mu = jnp.mean(vb, axis=-1, keepdims=True)
    xc = vb - mu
    vg = xc * lax.rsqrt(jnp.mean(xc * xc, axis=-1, keepdims=True) + EPS) * lnw + lnb
    r = lax.broadcasted_iota(jnp.int32, (B_CHUNK, B_CHUNK), 0)
    c = lax.broadcasted_iota(jnp.int32, (B_CHUNK, B_CHUNK), 1)
    ws_causal = jnp.where(r >= c, ws, 0.0)
    svs = [_dot_nn(ws_causal, vg[n * B_CHUNK:(n + 1) * B_CHUNK]) + bias
           for n in range(rows // B_CHUNK)]
    return u * jnp.concatenate(svs, axis=0) * _silu(gb)


def _rope(x, cos_t, sin_t):
    return x * cos_t + pltpu.roll(x, 64, 1) * sin_t


def _rope_transpose(g, cos_t, sin_t):
    return g * cos_t + pltpu.roll(g * sin_t, 64, 1)


ANY_SPEC = pl.BlockSpec(memory_space=pl.ANY)


def _live(deps):
    return [d for d in deps if d is not None]


def _skip_deps(body, n_in, n_deps):
    def wrapped(*refs):
        return body(*refs[:n_in], *refs[n_in + n_deps:])
    return wrapped


def _pure_call(name, fn, grid, in_specs, out_specs, out_shape, args, n_acc=0, deps=()):
    deps = _live(deps)
    n_in, n_out, n_deps = len(in_specs), len(out_specs), len(deps)
    in_specs = list(in_specs) + [ANY_SPEC] * n_deps
    args = tuple(args) + tuple(deps)

    def body(*refs):
        res = fn(*[r[...] for r in refs[:n_in]])
        if not isinstance(res, (tuple, list)):
            res = (res,)
        outs = refs[n_in + n_deps:n_in + n_deps + n_out]
        for o, r in zip(outs[:n_out - n_acc], res[:n_out - n_acc]):
            o[...] = r.astype(o.dtype)
        if n_acc:
            first = functools.reduce(jnp.logical_and, [pl.program_id(i) == 0 for i in range(len(grid))])
            for o, r in zip(outs[n_out - n_acc:], res[n_out - n_acc:]):
                @pl.when(first)
                def _(o=o, r=r):
                    o[...] = r.astype(o.dtype)

                @pl.when(jnp.logical_not(first))
                def _(o=o, r=r):
                    o[...] += r.astype(o.dtype)

    return pl.pallas_call(body, name=name, grid=grid, in_specs=in_specs, out_specs=out_specs,
                          out_shape=out_shape, compiler_params=_params(len(grid)))(*args)


def _sds(shape, dtype):
    return jax.ShapeDtypeStruct(shape, dtype)


def _row_spec(tm, width, col=0):
    return pl.BlockSpec((tm, width), lambda i, col=col: (i, col))


def _full_spec(shape):
    nd = len(shape)
    return pl.BlockSpec(shape, lambda *_: (0,) * nd)


def _mm_nn(name, a, b, out_dtype, tm, tn, deps=()):
    deps = _live(deps)
    m, k = a.shape
    j, _, n = b.shape
    per = n // tn

    def body(a_ref, b_ref, o_ref):
        o_ref[...] = _raw_nn(a_ref[...], b_ref[...]).astype(o_ref.dtype)

    return pl.pallas_call(
        _skip_deps(body, 2, len(deps)), name=name, grid=(m // tm, j * per),
        in_specs=[pl.BlockSpec((tm, k), lambda i, c: (i, 0)),
                  pl.BlockSpec((None, k, tn), lambda i, c: (c // per, 0, c % per))] + [ANY_SPEC] * len(deps),
        out_specs=pl.BlockSpec((tm, tn), lambda i, c: (i, c)),
        out_shape=_sds((m, j * n), out_dtype), compiler_params=_params(2))(a, b, *deps)


def _mm_nt(name, a, b, out_dtype, tm, tn, deps=()):
    deps = _live(deps)
    m = a.shape[0]
    j, nn, n = b.shape

    def body(a_ref, b_ref, o_ref):
        b_all = b_ref[0] if j == 1 else jnp.concatenate([b_ref[s] for s in range(j)], axis=1)
        o_ref[...] = _raw_nt(a_ref[...], b_all).astype(o_ref.dtype)

    return pl.pallas_call(
        _skip_deps(body, 2, len(deps)), name=name, grid=(m // tm, nn // tn),
        in_specs=[pl.BlockSpec((tm, j * n), lambda i, c: (i, 0)),
                  pl.BlockSpec((j, tn, n), lambda i, c: (0, c, 0))] + [ANY_SPEC] * len(deps),
        out_specs=pl.BlockSpec((tm, tn), lambda i, c: (i, c)),
        out_shape=_sds((m, nn), out_dtype), compiler_params=_params(2))(a, b, *deps)


def _mm_tn(name, a, b, j, out_dtype, tm, tn, deps=()):
    deps = _live(deps)
    k, m = a.shape
    n = b.shape[1] // j
    per = n // tn

    def body(a_ref, b_ref, o_ref):
        o_ref[...] = _raw_tn(a_ref[...], b_ref[...]).astype(o_ref.dtype)

    return pl.pallas_call(
        _skip_deps(body, 2, len(deps)), name=name, grid=(m // tm, j * per),
        in_specs=[pl.BlockSpec((k, tm), lambda i, c: (0, i)),
                  pl.BlockSpec((k, tn), lambda i, c: (0, c))] + [ANY_SPEC] * len(deps),
        out_specs=pl.BlockSpec((None, tm, tn), lambda i, c: (c // per, i, c % per)),
        out_shape=_sds((j, m, n), out_dtype), compiler_params=_params(2))(a, b, *deps)


def _mm_tn_parity(name, a, b, j, parity, out_dtype, tm, deps=()):
    deps = _live(deps)
    k, m = a.shape
    n = b.shape[1] // j

    def body(par_ref, a_ref, b_ref, o_ref):
        del par_ref
        o_ref[...] = _raw_tn(a_ref[...], b_ref[...]).astype(o_ref.dtype)

    grid_spec = pltpu.PrefetchScalarGridSpec(
        num_scalar_prefetch=1, grid=(m // tm, j // 2),
        in_specs=[pl.BlockSpec((k, tm), lambda i, s, par: (0, i)),
                  pl.BlockSpec((k, n), lambda i, s, par: (0, 2 * s + par[0]))] + [ANY_SPEC] * len(deps),
        out_specs=pl.BlockSpec((None, tm, n), lambda i, s, par: (s, i, 0)))
    return pl.pallas_call(
        lambda par_ref, *refs: _skip_deps(functools.partial(body, par_ref), 2, len(deps))(*refs),
        name=name, grid_spec=grid_spec, out_shape=_sds((j // 2, m, n), out_dtype),
        compiler_params=_params(2))(parity, a, b, *deps)


TM = 256


def _pre_norm(name, x, w_row, deps=()):
    def fn(xv, w):
        return _rms(xv, w)
    return _pure_call(name, fn, (T // TM,), [_row_spec(TM, D), _full_spec((1, D))],
                      [_row_spec(TM, D)], [_sds((T, D), BF16)], (x, w_row), deps=deps)[0]


def _post_pre_norm(x, y, w_post, w_pre):
    def fn(xv, yv, wp, wn):
        x1 = xv + _rms(yv, wp)
        return x1, _rms(x1, wn)
    return _pure_call("post_pre_norm", fn, (T // TM,),
                      [_row_spec(TM, D), _row_spec(TM, D), _full_spec((1, D)), _full_spec((1, D))],
                      [_row_spec(TM, D), _row_spec(TM, D)],
                      [_sds((T, D), F32), _sds((T, D), BF16)], (x, y, w_post, w_pre))


def _post_pre_norm_bwd(y, x1, w_post, w_pre, dx1_in, dh1, deps=()):
    def fn(yv, x1v, wp, wn, dx1v, dh1v):
        _, vjp_pre = jax.vjp(_rms, x1v, wn)
        dx1_h, dwn = vjp_pre(dh1v)
        dx1 = dx1v + dx1_h
        _, vjp_post = jax.vjp(_rms, yv, wp)
        dy, dwp = vjp_post(dx1)
        return dx1, dy, dwp, dwn
    return _pure_call("post_pre_norm_bwd", fn, (T // TM,),
                      [_row_spec(TM, D), _row_spec(TM, D), _full_spec((1, D)), _full_spec((1, D)),
                       _row_spec(TM, D), _row_spec(TM, D)],
                      [_row_spec(TM, D), _row_spec(TM, D), _full_spec((1, D)), _full_spec((1, D))],
                      [_sds((T, D), F32), _sds((T, D), BF16), _sds((1, D), F32), _sds((1, D), F32)],
                      (y, x1, w_post, w_pre, dx1_in, dh1), n_acc=2, deps=deps)


def _final_loss(x1, y, w_post, target):
    def fn(x1v, yv, wp, tv):
        r, vjp = jax.vjp(_rms, yv, wp)
        err = x1v + r - tv
        part = 0.5 * jnp.sum(jnp.mean(err * err, axis=-1, keepdims=True), axis=0, keepdims=True)
        dx2 = err * (1.0 / D)
        dy, dwp = vjp(dx2)
        return dx2, dy, jnp.broadcast_to(part, (1, 128)), dwp
    return _pure_call("final_loss", fn, (T // TM,),
                      [_row_spec(TM, D), _row_spec(TM, D), _full_spec((1, D)), _row_spec(TM, D)],
                      [_row_spec(TM, D), _row_spec(TM, D), _full_spec((1, 128)), _full_spec((1, D))],
                      [_sds((T, D), F32), _sds((T, D), BF16), _sds((1, 128), F32), _sds((1, D), F32)],
                      (x1, y, w_post, target), n_acc=2)


def _pre_norm_bwd(x, w_row, dh, dx_res, deps=()):
    def fn(xv, w, dhv, dxv):
        _, vjp = jax.vjp(_rms, xv, w)
        dx, dw = vjp(dhv)
        return dxv + dx, dw
    return _pure_call("pre_norm_bwd", fn, (T // TM,),
                      [_row_spec(TM, D), _full_spec((1, D)), _row_spec(TM, D), _row_spec(TM, D)],
                      [_row_spec(TM, D), _full_spec((1, D))],
                      [_sds((T, D), F32), _sds((1, D), F32)], (x, w_row, dh, dx_res), n_acc=1, deps=deps)


RA = 256


def _head(ref, hh):
    return ref[:, hh * HD:(hh + 1) * HD]


def _z_part(z_ref, k, h):
    lo = (k * A_HEADS + h) * HD
    return z_ref[:, lo:lo + HD]


def _even_specs():
    return [_full_spec((1, A_HEADS * HD)), _full_spec((1, A_HEADS * HD)), _full_spec((1, HD)),
            _full_spec((1, B_GROUPS * HD)), _full_spec((1, B_GROUPS * HD)),
            _full_spec((B_GROUPS, B_CHUNK, B_CHUNK)), _full_spec((B_GROUPS, B_CHUNK, 1))]


def _even_fwd(z, l0, l1, onorm, lnw, lnb, ws, bias):
    nb = T // RA

    def body(z_ref, l0_ref, l1_ref, on_ref, lnw_ref, lnb_ref, ws_ref, bias_ref, cat_ref, sst_ref, st_scr):
        @pl.when(pl.program_id(0) == 0)
        def _():
            st_scr[...] = jnp.zeros_like(st_scr)

        for hh in range(A_HEADS):
            st = st_scr[hh]
            sst_ref[hh] = st
            out, st_new = _hgrn2_rows(_z_part(z_ref, 0, hh), _z_part(z_ref, 1, hh), _z_part(z_ref, 2, hh),
                                      _z_part(z_ref, 3, hh), st, _head(l0_ref, hh), _head(l1_ref, hh), on_ref[...])
            cat_ref[:, hh * HD:(hh + 1) * HD] = out.astype(cat_ref.dtype)
            st_scr[hh] = st_new
        for gg in range(B_GROUPS):
            out = _gmlp_rows(_z_part(z_ref, 4, gg), _z_part(z_ref, 5, gg), _z_part(z_ref, 6, gg),
                             _head(lnw_ref, gg), _head(lnb_ref, gg), ws_ref[gg], bias_ref[gg])
            cat_ref[:, (A_HEADS + gg) * HD:(A_HEADS + gg + 1) * HD] = out.astype(cat_ref.dtype)

    return pl.pallas_call(
        body, name="even_mixers_fwd", grid=(nb,),
        in_specs=[pl.BlockSpec((RA, EVEN_IN), lambda r: (r, 0))] + _even_specs(),
        out_specs=[pl.BlockSpec((RA, 2 * A_HEADS * HD), lambda r: (r, 0)),
                   pl.BlockSpec((A_HEADS, None, HD, HD), lambda r: (0, r, 0, 0))],
        out_shape=[_sds((T, 2 * A_HEADS * HD), BF16), _sds((A_HEADS, nb, HD, HD), F32)],
        scratch_shapes=[pltpu.VMEM((A_HEADS, HD, HD), F32)],
        compiler_params=_params(1))(z, l0, l1, onorm, lnw, lnb, ws, bias)


def _even_bwd(z, l0, l1, onorm, lnw, lnb, ws, bias, sst, dcat, deps=()):
    nb = T // RA
    deps = _live(deps)

    def body(z_ref, l0_ref, l1_ref, on_ref, lnw_ref, lnb_ref, ws_ref, bias_ref, sst_ref, dcat_ref,
             dz_ref, dl0_ref, dl1_ref, don_ref, dlnw_ref, dlnb_ref, dws_ref, dbias_ref, ds_scr):
        first = pl.program_id(0) == 0

        @pl.when(first)
        def _():
            ds_scr[...] = jnp.zeros_like(ds_scr)

        def put(k, h, val):
            lo = (k * A_HEADS + h) * HD
            dz_ref[:, lo:lo + HD] = val.astype(dz_ref.dtype)

        sums = []
        don = None
        for hh in range(A_HEADS):
            lanes = slice(hh * HD, (hh + 1) * HD)
            _, vjp = jax.vjp(_hgrn2_rows, _z_part(z_ref, 0, hh), _z_part(z_ref, 1, hh), _z_part(z_ref, 2, hh),
                             _z_part(z_ref, 3, hh), sst_ref[hh], _head(l0_ref, hh), _head(l1_ref, hh), on_ref[...])
            dq, dzf, dv, dga, dst, dl0, dl1, don_h = vjp((dcat_ref[:, lanes], ds_scr[hh]))
            for k, val in enumerate((dq, dzf, dv, dga)):
                put(k, hh, val)
            ds_scr[hh] = dst
            sums += [(dl0_ref, (slice(None), lanes), dl0), (dl1_ref, (slice(None), lanes), dl1)]
            don = don_h if don is None else don + don_h
        sums.append((don_ref, slice(None), don))
        for gg in range(B_GROUPS):
            lanes = slice(gg * HD, (gg + 1) * HD)
            _, vjp = jax.vjp(_gmlp_rows, _z_part(z_ref, 4, gg), _z_part(z_ref, 5, gg), _z_part(z_ref, 6, gg),
                             _head(lnw_ref, gg), _head(lnb_ref, gg), ws_ref[gg], bias_ref[gg])
            du, dv, dg, dlnw, dlnb, dws, dbias = vjp(dcat_ref[:, (A_HEADS + gg) * HD:(A_HEADS + gg + 1) * HD])
            for k, val in enumerate((du, dv, dg)):
                put(4 + k, gg, val)
            sums += [(dlnw_ref, (slice(None), lanes), dlnw), (dlnb_ref, (slice(None), lanes), dlnb),
                     (dws_ref, gg, dws), (dbias_ref, gg, dbias)]
        for ref, idx, val in sums:
            @pl.when(first)
            def _(ref=ref, idx=idx, val=val):
                ref[idx] = val

            @pl.when(jnp.logical_not(first))
            def _(ref=ref, idx=idx, val=val):
                ref[idx] += val

    small = _even_specs()
    return pl.pallas_call(
        _skip_deps(body, 10, len(deps)), name="even_mixers_bwd", grid=(nb,),
        in_specs=[pl.BlockSpec((RA, EVEN_IN), lambda r: (nb - 1 - r, 0))] + small
        + [pl.BlockSpec((A_HEADS, None, HD, HD), lambda r: (0, nb - 1 - r, 0, 0)),
           pl.BlockSpec((RA, 2 * A_HEADS * HD), lambda r: (nb - 1 - r, 0))] + [ANY_SPEC] * len(deps),
        out_specs=[pl.BlockSpec((RA, EVEN_IN), lambda r: (nb - 1 - r, 0))] + small,
        out_shape=[_sds((T, EVEN_IN), BF16), _sds((1, A_HEADS * HD), F32), _sds((1, A_HEADS * HD), F32),
                   _sds((1, HD), F32), _sds((1, B_GROUPS * HD), F32), _sds((1, B_GROUPS * HD), F32),
                   _sds((B_GROUPS, B_CHUNK, B_CHUNK), F32), _sds((B_GROUPS, B_CHUNK, 1), F32)],
        scratch_shapes=[pltpu.VMEM((A_HEADS, HD, HD), F32)],
        compiler_params=_params(1))(z, l0, l1, onorm, lnw, lnb, ws, bias, sst, dcat, *deps)


def _mla_pre(z1, qn, kvn, cos_t, sin_t):
    def fn(cq, ckv, kpe, cs, sn, wq, wkv):
        return _rms(cq, wq), _rms(ckv, wkv), _rope(kpe, cs, sn)
    return _pure_call("mla_pre", fn, (T // TM,),
                      [_row_spec(TM, C_RANK, 4), _row_spec(TM, C_RANK, 5), _row_spec(TM, HD, 24),
                       _row_spec(TM, HD), _row_spec(TM, HD),
                       _full_spec((1, C_RANK)), _full_spec((1, C_RANK))],
                      [_row_spec(TM, C_RANK), _row_spec(TM, C_RANK), _row_spec(TM, HD)],
                      [_sds((T, C_RANK), BF16), _sds((T, C_RANK), BF16), _sds((T, HD), BF16)],
                      (z1, z1, z1, cos_t, sin_t, qn, kvn))


def _mla_pre_bwd(z1, qn, kvn, cos_t, sin_t, dcqn, dckvn, dkp, dgate, deps=()):
    def fn(cq, ckv, cs, sn, wq, wkv, g_q, g_kv, g_kp, g_gate):
        _, vjp_q = jax.vjp(_rms, cq, wq)
        dcq, dwq = vjp_q(g_q)
        _, vjp_kv = jax.vjp(_rms, ckv, wkv)
        dckv, dwkv = vjp_kv(g_kv)
        dz1 = jnp.concatenate([g_gate, dcq.astype(BF16), dckv.astype(BF16),
                               _rope_transpose(g_kp, cs, sn).astype(BF16)], axis=1)
        return dz1, dwq, dwkv
    return _pure_call("mla_pre_bwd", fn, (T // TM,),
                      [_row_spec(TM, C_RANK, 4), _row_spec(TM, C_RANK, 5),
                       _row_spec(TM, HD), _row_spec(TM, HD),
                       _full_spec((1, C_RANK)), _full_spec((1, C_RANK)),
                       _row_spec(TM, C_RANK), _row_spec(TM, C_RANK), _row_spec(TM, HD), _row_spec(TM, D)],
                      [_row_spec(TM, ODD_IN_PAD), _full_spec((1, C_RANK)), _full_spec((1, C_RANK))],
                      [_sds((T, ODD_IN_PAD), BF16), _sds((1, C_RANK), F32), _sds((1, C_RANK), F32)],
                      (z1, z1, cos_t, sin_t, qn, kvn, dcqn, dckvn, dkp, dgate), n_acc=2, deps=deps)


TQ = 256
HP = 2
KVW = C_NOPE + C_V


def _att_keys(kv_ref, kp_ref, k_scr):
    @pl.when(pl.program_id(1) == 0)
    def _():
        for hh in range(HP):
            k_scr[hh, :, 0:C_NOPE] = kv_ref[:, hh * KVW:hh * KVW + C_NOPE]
            k_scr[hh, :, C_NOPE:QP] = kp_ref[...]


def _att_scores(q, cos_ref, sin_ref, k_scr, hh, n):
    keys = (n + 1) * TQ
    qr = jnp.concatenate([q[:, :C_NOPE], _rope(q[:, C_NOPE:], cos_ref[...], sin_ref[...])], axis=1).astype(BF16)
    return qr, _raw_nt(qr, k_scr[hh, 0:keys, :]) * ATT_SCALE


def _causal(x, n, fill):
    row = lax.broadcasted_iota(jnp.int32, (TQ, TQ), 0)
    col = lax.broadcasted_iota(jnp.int32, (TQ, TQ), 1)
    diag = jnp.where(col <= row, x[:, n * TQ:], fill)
    return diag if n == 0 else jnp.concatenate([x[:, :n * TQ], diag], axis=1)


def _per_query_block(fn):
    for n in range(T // TQ):
        pl.when(pl.program_id(1) == n)(functools.partial(fn, n))


def _att_in_specs():
    return [pl.BlockSpec((TQ, HP * QP), lambda g, i: (i, g)),
            pl.BlockSpec((TQ, HD), lambda g, i: (i, 0)),
            pl.BlockSpec((TQ, HD), lambda g, i: (i, 0)),
            pl.BlockSpec((T, HP * KVW), lambda g, i: (0, g)),
            pl.BlockSpec((T, HD), lambda g, i: (0, 0))]


def _attention_fwd(q, cos_t, sin_t, kv, kp, z1):
    def body(q_ref, cos_ref, sin_ref, kv_ref, kp_ref, gate_ref, o_ref, lse_ref, og_ref, k_scr):
        _att_keys(kv_ref, kp_ref, k_scr)

        def block(n):
            keys = (n + 1) * TQ
            for hh in range(HP):
                _, s = _att_scores(q_ref[:, hh * QP:(hh + 1) * QP], cos_ref, sin_ref, k_scr, hh, n)
                s = _causal(s, n, jnp.finfo(F32).min)
                m = jnp.max(s, axis=-1, keepdims=True)
                p = jnp.exp(s - m)
                l = jnp.sum(p, axis=-1, keepdims=True)
                v = kv_ref[0:keys, hh * KVW + C_NOPE:(hh + 1) * KVW]
                o = _raw_nn(p, v) / l
                lanes = slice(hh * C_V, (hh + 1) * C_V)
                o_ref[:, lanes] = o
                og_ref[:, lanes] = (o * _silu(gate_ref[:, lanes])).astype(og_ref.dtype)
                lse_ref[hh] = m + jnp.log(l)

        _per_query_block(block)

    heads = pl.BlockSpec((TQ, HP * C_V), lambda g, i: (i, g))
    return pl.pallas_call(
        body, name="attention_fwd", grid=(C_HEADS // HP, T // TQ), in_specs=_att_in_specs() + [heads],
        out_specs=[heads, pl.BlockSpec((HP, TQ, 1), lambda g, i: (g, i, 0)), heads],
        out_shape=[_sds((T, C_HEADS * C_V), F32), _sds((C_HEADS, T, 1), F32), _sds((T, C_HEADS * C_V), BF16)],
        scratch_shapes=[pltpu.VMEM((HP, T, QP), BF16)],
        compiler_params=_params(2))(q, cos_t, sin_t, kv, kp, z1)


def _attention_bwd(q, cos_t, sin_t, kv, kp, o, lse, dog, z1):
    nq = T // TQ

    def body(q_ref, cos_ref, sin_ref, kv_ref, kp_ref, o_ref, lse_ref, dog_ref, gate_ref,
             dq_ref, dkv_ref, dkp_ref, dgate_ref, k_scr, dk_scr, dv_scr):
        g, i = pl.program_id(0), pl.program_id(1)
        _att_keys(kv_ref, kp_ref, k_scr)

        @pl.when(i == 0)
        def _():
            dv_scr[...] = jnp.zeros_like(dv_scr)
            dk_scr[...] = jnp.zeros_like(dk_scr)

        def block(n):
            keys = (n + 1) * TQ
            for hh in range(HP):
                qr, s = _att_scores(q_ref[:, hh * QP:(hh + 1) * QP], cos_ref, sin_ref, k_scr, hh, n)
                p = _causal(jnp.exp(s - lse_ref[hh]), n, 0.0)
                lanes = slice(hh * C_V, (hh + 1) * C_V)
                ov, gate, dogv = o_ref[:, lanes], gate_ref[:, lanes], dog_ref[:, lanes]
                sig = _sigmoid(gate)
                silu = gate * sig
                dov = dogv * silu
                dgate_ref[:, lanes] = (dogv * ov * (sig + silu * (1.0 - sig))).astype(dgate_ref.dtype)
                delta = jnp.sum(dov * ov, axis=-1, keepdims=True)
                dp = _raw_nt(dov, kv_ref[0:keys, hh * KVW + C_NOPE:(hh + 1) * KVW])
                ds = p * (dp - delta) * ATT_SCALE
                dq = _raw_nn(ds, k_scr[hh, 0:keys, :])
                dq_ref[:, hh * QP:(hh + 1) * QP] = jnp.concatenate(
                    [dq[:, :C_NOPE], _rope_transpose(dq[:, C_NOPE:], cos_ref[...], sin_ref[...])],
                    axis=1).astype(dq_ref.dtype)
                dv_scr[hh, 0:keys, :] += _raw_tn(p, dov)
                dk_scr[hh, 0:keys, :] += _raw_tn(ds, qr)

        _per_query_block(block)

        @pl.when(i == nq - 1)
        def _():
            for hh in range(HP):
                dkv_ref[:, hh * KVW:(hh + 1) * KVW] = jnp.concatenate(
                    [dk_scr[hh, :, 0:C_NOPE], dv_scr[hh]], axis=1).astype(dkv_ref.dtype)

        @pl.when(jnp.logical_and(i == nq - 1, g == 0))
        def _():
            dkp_ref[...] = dk_scr[0, :, C_NOPE:QP]

        @pl.when(jnp.logical_and(i == nq - 1, g > 0))
        def _():
            dkp_ref[...] += dk_scr[0, :, C_NOPE:QP]

        @pl.when(i == nq - 1)
        def _():
            for hh in range(1, HP):
                dkp_ref[...] += dk_scr[hh, :, C_NOPE:QP]

    heads = pl.BlockSpec((TQ, HP * C_V), lambda g, i: (i, g))
    return pl.pallas_call(
        body, name="attention_bwd", grid=(C_HEADS // HP, nq),
        in_specs=_att_in_specs() + [heads, pl.BlockSpec((HP, TQ, 1), lambda g, i: (g, i, 0)), heads, heads],
        out_specs=[pl.BlockSpec((TQ, HP * QP), lambda g, i: (i, g)),
                   pl.BlockSpec((T, HP * KVW), lambda g, i: (0, g)),
                   _full_spec((T, HD)), heads],
        out_shape=[_sds((T, C_HEADS * QP), BF16), _sds((T, C_HEADS * KVW), BF16), _sds((T, HD), F32),
                   _sds((T, C_HEADS * C_V), BF16)],
        scratch_shapes=[pltpu.VMEM((HP, T, QP), BF16), pltpu.VMEM((HP, T, QP), F32), pltpu.VMEM((HP, T, C_V), F32)],
        compiler_params=_params(2))(q, cos_t, sin_t, kv, kp, o, lse, dog, z1)


def _adamw_math(w, g, m, v):
    m = ADAM_B1 * m + (1.0 - ADAM_B1) * g
    v = ADAM_B2 * v + (1.0 - ADAM_B2) * (g * g)
    m_hat = m / (1.0 - ADAM_B1 ** ADAM_STEP)
    v_hat = v / (1.0 - ADAM_B2 ** ADAM_STEP)
    delta = -ADAM_LR * (m_hat / (jnp.sqrt(v_hat) + ADAM_EPS) + ADAM_WD * w)
    return delta, m, v


def _adamw(name, parts, w, m, v, tr, tc=None):
    rows, cols = w.shape

    def fn(*vals):
        pvs, (wv, mv, vv) = vals[:len(parts)], vals[len(parts):]
        g = None
        for pv in pvs:
            for d in range(pv.shape[0]):
                term = pv[d].astype(F32)
                g = term if g is None else g + term
        return (g,) + _adamw_math(wv, g, mv, vv)

    tc = cols if tc is None else tc
    blk = pl.BlockSpec((tr, tc), lambda i, j: (i, j))
    part_specs = [pl.BlockSpec((n, tr, tc), lambda i, j: (0, i, j)) for _, n in parts]
    return _pure_call(name, fn, (rows // tr, cols // tc), part_specs + [blk, blk, blk],
                      [blk] * 4, [_sds((rows, cols), F32)] * 4, tuple(p for p, _ in parts) + (w, m, v))


SMALL_PARAM_SHAPES = ((2, D), (2, D), (2, A_HEADS * HD), (1, HD), (1, B_GROUPS * HD), (1, B_GROUPS * HD),
                      (B_GROUPS, B_CHUNK, B_CHUNK), (B_GROUPS, B_CHUNK))
SMALL_PIECES = ((0, 0, 0, 0), (0, 1, 1, 0), (1, 0, 1, 1), (1, 1, 1, 2), (2, 0, 2, 0), (2, 1, 2, 1),
                (3, 0, 3, 8), (4, 0, 2, 2), (5, 0, 2, 3))


def _small_rows(dnpre1, dnpost0, dnpost1, dl0, dl1, donorm, dlnw, dlnb, dws, dbias, dqn, dkvn, loss_part):
    return [jnp.concatenate([dnpre1, dnpost0, dnpost1], axis=0),
            jnp.concatenate([dl0, dl1, dlnw, dlnb], axis=0),
            jnp.concatenate([dbias.reshape(B_GROUPS, B_CHUNK), donorm, loss_part], axis=0),
            dws,
            jnp.concatenate([dqn, dkvn], axis=0)]


def _adamw_small(late_all, early_all, wmv):
    n_in = 6 + 3 * len(wmv)

    def body(*refs):
        gathered, params, outs = refs[:6], refs[6:n_in], refs[n_in:]

        def total(ref):
            s = ref[0]
            for d in range(1, N_DEV):
                s = s + ref[d]
            return s

        g_late, g2048, g1024, g128, g_ws, g512 = [total(r) for r in gathered]
        arrays = (g_late, g2048, g1024, g128)

        def update(p, rows, g):
            w_ref, m_ref, v_ref = params[3 * p:3 * p + 3]
            delta, m, v = _adamw_math(w_ref[rows], g, m_ref[rows], v_ref[rows])
            for out, val in zip(outs[4 * p:4 * p + 4], (g, delta, m, v)):
                out[rows] = val

        for p, row, arr, arr_row in SMALL_PIECES:
            update(p, pl.ds(row, 1), arrays[arr][arr_row:arr_row + 1])
        update(6, slice(None), g_ws)
        update(7, slice(None), g128[0:B_GROUPS])
        outs[32][...] = g128[B_GROUPS + 1:B_GROUPS + 2]
        outs[33][...] = g512

    vmem = pl.BlockSpec(memory_space=pltpu.VMEM)
    flat = [a for t in wmv for a in t]
    out_shape = [_sds(s, F32) for s in SMALL_PARAM_SHAPES for _ in range(4)] + [_sds((1, 128), F32), _sds((2, C_RANK), F32)]
    res = pl.pallas_call(body, name="adamw_small", in_specs=[vmem] * n_in, out_specs=[vmem] * len(out_shape),
                         out_shape=out_shape,
                         compiler_params=pltpu.CompilerParams(vmem_limit_bytes=VMEM_LIMIT_V7X))(late_all, *early_all, *flat)
    return [res[4 * p:4 * p + 4] for p in range(8)], res[32], res[33]


def _exchange(name, arrs, gather, deps=()):
    n = len(arrs)
    deps = _live(deps)

    def body(*refs):
        ins, outs = refs[:n], refs[n + len(deps):2 * n + len(deps)]
        send_sems, recv_sems, local_sems = refs[2 * n + len(deps):]
        x, y, c = lax.axis_index("x"), lax.axis_index("y"), lax.axis_index("c")
        me = 4 * x + 2 * y + c

        def peer(k):
            return (x ^ (k >> 2), y ^ ((k >> 1) & 1), c ^ (k & 1))

        def copy(a, k):
            src = ins[a] if gather else ins[a].at[me ^ k]
            return pltpu.make_async_remote_copy(
                src_ref=src, dst_ref=outs[a].at[me], send_sem=send_sems.at[a, k - 1],
                recv_sem=recv_sems.at[a, k - 1], device_id=peer(k), device_id_type=MESH_ID)

        def arrival(a, k):
            src = ins[a] if gather else ins[a].at[me]
            return pltpu.make_async_remote_copy(
                src_ref=src, dst_ref=outs[a].at[me ^ k], send_sem=send_sems.at[a, k - 1],
                recv_sem=recv_sems.at[a, k - 1], device_id=peer(k), device_id_type=MESH_ID)

        own = [pltpu.make_async_copy(ins[a] if gather else ins[a].at[me], outs[a].at[me], local_sems.at[a])
               for a in range(n)]
        for cp in own:
            cp.start()
        for k in range(1, N_DEV):
            for a in range(n):
                copy(a, k).start()
        for k in range(1, N_DEV):
            for a in range(n):
                arrival(a, k).wait_recv()
        for k in range(1, N_DEV):
            for a in range(n):
                copy(a, k).wait_send()
        for cp in own:
            cp.wait()

    any_spec = pl.BlockSpec(memory_space=pl.ANY)
    out_shape = [_sds((N_DEV,) + a.shape if gather else a.shape, a.dtype) for a in arrs]
    return pl.pallas_call(
        body, name=name, in_specs=[any_spec] * (n + len(deps)), out_specs=[any_spec] * n, out_shape=out_shape,
        scratch_shapes=[pltpu.SemaphoreType.DMA((n, N_DEV - 1)), pltpu.SemaphoreType.DMA((n, N_DEV - 1)),
                        pltpu.SemaphoreType.DMA((n,))],
        compiler_params=pltpu.CompilerParams(has_side_effects=True))(*arrs, *deps)


HBM_SPEC = pl.BlockSpec(memory_space=pltpu.HBM)
SEM_SPEC = pl.BlockSpec(memory_space=pltpu.SEMAPHORE)
DATAFLOW = pltpu.SideEffectType.DATAFLOW_SIDE_EFFECTING


def _my_index():
    return 4 * lax.axis_index("x") + 2 * lax.axis_index("y") + lax.axis_index("c")


def _plan_copies(plan, refs, send_sems, recv_sems):
    x, y, c = lax.axis_index("x"), lax.axis_index("y"), lax.axis_index("c")
    return [pltpu.make_async_remote_copy(
        src_ref=src, dst_ref=dst, send_sem=send_sems.at[i], recv_sem=recv_sems.at[i],
        device_id=(x ^ (k >> 2), y ^ ((k >> 1) & 1), c ^ (k & 1)), device_id_type=MESH_ID)
        for i, (src, dst, k) in enumerate(plan(refs, 4 * x + 2 * y + c))]


def _split_call(name, bufs, waits=None, starts=None, deps=()):
    n = len(bufs)
    deps = _live(deps)
    n_wait = 2 if waits else 0

    def body(*refs):
        zones = refs[:n]
        if waits:
            for cp in _plan_copies(waits[2], zones, refs[n], refs[n + 1]):
                cp.wait_send()
                cp.wait_recv()
        if starts:
            first_out = n + n_wait + len(deps)
            for cp in _plan_copies(starts[0], zones, refs[first_out], refs[first_out + 1]):
                cp.start()
            refs[-1][...] = jnp.zeros_like(refs[-1])

    out_specs, out_shape = [], []
    if starts:
        sems = pltpu.SemaphoreType.DMA((starts[1],))
        out_specs, out_shape = [SEM_SPEC, SEM_SPEC], [sems, sems]
    out_specs += [HBM_SPEC] * n
    out_shape += [pltpu.HBM(b.shape, b.dtype) for b in bufs]
    if starts:
        out_specs.append(pl.BlockSpec(memory_space=pltpu.VMEM))
        out_shape.append(_sds((8, 128), F32))
    first_buf = 2 if starts else 0
    res = pl.pallas_call(
        body, name=name,
        in_specs=[HBM_SPEC] * n + [SEM_SPEC] * n_wait + [ANY_SPEC] * len(deps),
        out_specs=out_specs, out_shape=out_shape,
        input_output_aliases={i: first_buf + i for i in range(n)},
        compiler_params=pltpu.CompilerParams(has_side_effects=DATAFLOW),
    )(*[pltpu.with_memory_space_constraint(b, pltpu.HBM) for b in bufs], *(waits[:2] if waits else ()), *deps)
    out_bufs = list(res[first_buf:first_buf + n])
    return out_bufs, ((res[0], res[1]) if starts else None), (res[-1] if starts else None)


def _direct_plan(n, gather):
    def plan(refs, me):
        return [(refs[a] if gather else refs[a].at[me ^ k], refs[n + a].at[me], k)
                for k in range(1, N_DEV) for a in range(n)]
    return plan


def _own_slot_filled(a, gather):
    me = _my_index()
    if gather:
        return lax.dynamic_update_slice_in_dim(lax.empty((N_DEV,) + a.shape, a.dtype), a[None], me, 0)
    return lax.dynamic_update_slice_in_dim(lax.empty(a.shape, a.dtype), lax.dynamic_slice_in_dim(a, me, 1, 0), me, 0)


def _exchange_start(name, arrs, gather, deps=()):
    n = len(arrs)
    lands = [_own_slot_filled(a, gather) for a in arrs]
    plan = _direct_plan(n, gather)
    bufs, sems, token = _split_call(name, list(arrs) + lands, starts=(plan, n * (N_DEV - 1)), deps=deps)
    return (n, plan, sems, bufs, None), token


def _exchange_wait(name, handle, after):
    return _split_done(name, handle, after)


ICI_PEERS = (2, 4, 6)
SIBLING = 1


def _gather2_send(name, arrs, deps=()):
    n = len(arrs)
    lands = [_own_slot_filled(a, True) for a in arrs]

    def plan(refs, me_):
        return [(refs[a], refs[n + a].at[me_], k) for k in (SIBLING,) + ICI_PEERS for a in range(n)]

    bufs, sems, token = _split_call(name, list(arrs) + lands, starts=(plan, 4 * n), deps=deps)
    return (n, plan, sems, bufs, None), token


def _gather2_relay(name, handle, after):
    n, plan, sems, bufs, _ = handle
    after = after if isinstance(after, (list, tuple)) else [after]

    def relay(refs, me_):
        return [(refs[n + a].at[me_ ^ k], refs[n + a].at[me_ ^ k], SIBLING) for k in ICI_PEERS for a in range(n)]

    bufs, sems2, token = _split_call(name, bufs, waits=(sems[0], sems[1], plan), starts=(relay, 3 * n), deps=after)
    return (n, relay, sems2, bufs, None), token


def _split_done(name, handle, after, all_bufs=False):
    n, plan, sems, bufs, _ = handle
    bufs, _, _ = _split_call(name, bufs, waits=(sems[0], sems[1], plan), deps=[after])
    return bufs if all_bufs else bufs[n:]


def _scatter2_pair(name, for_sibling, deps=()):
    n = len(for_sibling)
    pairs = [lax.empty(s.shape, s.dtype) for s in for_sibling]

    def plan(refs, me):
        del me
        return [(refs[a].at[s], refs[n + a].at[s], SIBLING) for s in range(4) for a in range(n)]

    bufs, sems, token = _split_call(name, list(for_sibling) + pairs, starts=(plan, 4 * n), deps=deps)
    return (n, plan, sems, bufs, None), token


def _pair_add(name, mine, pair):
    _, rows, cols = mine.shape
    tr = rows // 2

    def fn(a, b):
        return a.astype(F32) + b.astype(F32)

    blk = pl.BlockSpec((None, tr, cols), lambda s, i: (s, i, 0))
    return _pure_call(name, fn, (4, rows // tr), [blk, blk], [blk], [_sds(mine.shape, mine.dtype)], (mine, pair))[0]


def _scatter2_send(name, chip_sums, deps=()):
    n = len(chip_sums)
    finals = [lax.empty((3,) + c.shape[1:], c.dtype) for c in chip_sums]

    def plan(refs, me):
        return [(refs[a].at[(me >> 1) ^ j], refs[n + a].at[j - 1], 2 * j) for j in range(1, 4) for a in range(n)]

    bufs, sems, token = _split_call(name, list(chip_sums) + finals, starts=(plan, 3 * n), deps=deps)
    return (n, plan, sems, bufs, None), token


def _pad_rope(p):
    z = jnp.zeros(p.shape[:-1] + (32,), p.dtype)
    return jnp.concatenate([p[..., :32], z, p[..., 32:], z], axis=-1)


def _unpad_rope(p):
    return jnp.concatenate([p[..., :32], p[..., 64:96]], axis=-1)


def _odd_in_layout(wt):
    wt = wt.reshape(ODD_IN, D)
    cq, ckv, kpe, gate = wt[:512], wt[512:1024], wt[1024:1088], wt[1088:]
    z = jnp.zeros((32, D), wt.dtype)
    return jnp.concatenate([gate, cq, ckv, kpe[:32], z, kpe[32:], z], axis=0)


def _odd_in_unlayout(dwt):
    gate, cq, ckv, kpe = dwt[:2048], dwt[2048:2560], dwt[2560:3072], dwt[3072:]
    wt = jnp.concatenate([cq, ckv, kpe[:32], kpe[64:96], gate], axis=0)
    return wt.reshape(N_DEV, ODD_IN // N_DEV, D)


def _qb_layout(w):
    w = w.transpose(1, 0, 2).reshape(C_RANK, C_HEADS, C_QK)
    w = jnp.concatenate([w[..., :C_NOPE], _pad_rope(w[..., C_NOPE:])], axis=-1)
    return w.reshape(C_RANK, C_HEADS * QP)


def _qb_unlayout(dw):
    dw = dw.reshape(C_RANK, C_HEADS, QP)
    dw = jnp.concatenate([dw[..., :C_NOPE], _unpad_rope(dw[..., C_NOPE:])], axis=-1)
    return dw.reshape(C_RANK, N_DEV, C_HEADS * C_QK // N_DEV).transpose(1, 0, 2)


def _rope_tables(positions):
    inv_freq = ROPE_THETA ** (-jnp.arange(0, C_ROPE, 2, dtype=F32) / C_ROPE)
    ang = positions.astype(F32)[0][:, None] * inv_freq
    cos, sin = jnp.cos(ang), jnp.sin(ang)
    z = jnp.zeros_like(cos)
    return jnp.concatenate([cos, z, cos, z], axis=1), jnp.concatenate([-sin, z, sin, z], axis=1)


def _forward_backward(x, cos_t, sin_t, target, norm_pre, norm_post, lb_logits, a_onorm, ln_w, ln_b,
                      b_ws, b_bias, get_w, put_g, put_small=None, start_dep=None):
    npre0, npre1 = norm_pre[0:1], norm_pre[1:2]
    npost0, npost1 = norm_post[0:1], norm_post[1:2]
    l0, l1 = lb_logits[0:1], lb_logits[1:2]
    bias_col = b_bias.reshape(B_GROUPS, B_CHUNK, 1)
    ws = b_ws.reshape(B_GROUPS, B_CHUNK, B_CHUNK)

    h0 = _pre_norm("pre_norm0", x, npre0, deps=[start_dep])
    w_ev_in = get_w("ev_in", h0)
    z0 = _mm_nn("ev_in", h0, w_ev_in, F32, 1024, 896)
    cat, sst = _even_fwd(z0, l0, l1, a_onorm, ln_w, ln_b, ws, bias_col)
    w_ev_out = get_w("ev_out", cat)
    y0 = _mm_nn("ev_out", cat, w_ev_out, F32, 1024, 1024)
    get_w("od_relay", y0)
    x1, h1 = _post_pre_norm(x, y0, npost0, npre1)
    w_od_in, w_qb, w_kvb, q_norm, kv_norm = get_w("od_mid", h1)
    z1 = _mm_nt("od_in", h1, w_od_in[None], F32, 1024, 640)
    cqn, ckvn, kp = _mla_pre(z1, q_norm, kv_norm, cos_t, sin_t)
    q = _mm_nn("od_qb", cqn, w_qb[None], F32, 1024, 1024)
    kv = _mm_nn("od_kvb", ckvn, w_kvb, BF16, 1024, 512)
    o, lse, og = _attention_fwd(q, cos_t, sin_t, kv, kp, z1)
    w_od_out = get_w("od_out", og)
    y1 = _mm_nn("od_out", og, w_od_out, F32, 1024, 1024)
    dx2, dy1, loss_part, dnpost1 = _final_loss(x1, y1, npost1, target)

    g_od_out = _mm_tn("od_out_dw", og, dy1, 1, BF16, 1024, 1024)
    tok = put_g("od_out", [g_od_out.reshape(N_DEV, D // N_DEV, D)])
    dog = _mm_nt("od_out_dx", dy1, w_od_out, F32, 1024, 1024, deps=[tok])
    dq, dkv, dkp, dgate = _attention_bwd(q, cos_t, sin_t, kv, kp, o, lse, dog, z1)
    g_qb = _mm_tn("od_qb_dw", cqn, dq, 1, F32, 512, 1024)
    g_kvb = _mm_tn("od_kvb_dw", ckvn, dkv, N_DEV, BF16, 512, 512)
    tok = put_g("od_qkv", [_qb_unlayout(g_qb[0]).astype(BF16), g_kvb])
    dcqn = _mm_nt("od_qb_dx", dq, w_qb[None], F32, 1024, 512, deps=[tok])
    dckvn = _mm_nt("od_kvb_dx", dkv, w_kvb, F32, 1024, 512)
    dz1, dqn, dkvn = _mla_pre_bwd(z1, q_norm, kv_norm, cos_t, sin_t, dcqn, dckvn, dkp, dgate)
    g_od_in = _mm_tn("od_in_dw", dz1, h1, 1, F32, 640, 1024)
    tok = put_g("od_in", [_odd_in_unlayout(g_od_in[0]).astype(BF16)])
    dh1 = _mm_nn("od_in_dx", dz1, w_od_in[None], F32, 1024, 1024, deps=[tok])
    dx1, dy0, dnpost0, dnpre1 = _post_pre_norm_bwd(y0, x1, npost0, npre1, dx2, dh1)

    g_ev_out = _mm_tn("ev_out_dw", cat, dy0, 1, BF16, 1024, 1024)
    tok = put_g("ev_out", [g_ev_out.reshape(N_DEV, D // N_DEV, D)])
    dcat = _mm_nt("ev_out_dx", dy0, w_ev_out, F32, 1024, 1024, deps=[tok])
    dz0, dl0, dl1, donorm, dlnw, dlnb, dws, dbias = _even_bwd(z0, l0, l1, a_onorm, ln_w, ln_b, ws, bias_col, sst, dcat)
    early = _small_rows(dnpre1, dnpost0, dnpost1, dl0, dl1, donorm, dlnw, dlnb, dws, dbias, dqn, dkvn, loss_part)
    tok = put_small(early) if put_small else None
    small_tok = tok

    def ev_in_half(name, parity, deps=()):
        return _mm_tn_parity(name, h0, dz0, N_DEV, parity, BF16, 1024, deps=[small_tok] + list(deps))

    tok = put_g("ev_in", ev_in_half)
    dh0 = _mm_nt("ev_in_dx", dz0, w_ev_in, F32, 1024, 256, deps=[tok])
    grad_x, dnpre0 = _pre_norm_bwd(x, npre0, dh0, dx1)
    return grad_x, early, dnpre0


def kernel(x, positions, norm_pre, norm_post, ev_w_in, ev_lb_logits, ev_a_onorm, ev_b_ln_w, ev_b_ln_b, ev_b_ws, ev_b_bias, ev_w_out, od_w_in, od_q_norm, od_w_qb, od_kv_norm, od_w_kvb, od_w_out, loss_target, m_norm_pre, m_norm_post, m_ev_w_in, m_ev_lb_logits, m_ev_a_onorm, m_ev_b_ln_w, m_ev_b_ln_b, m_ev_b_ws, m_ev_b_bias, m_ev_w_out, m_od_w_in, m_od_q_norm, m_od_w_qb, m_od_kv_norm, m_od_w_kvb, m_od_w_out, v_norm_pre, v_norm_post, v_ev_w_in, v_ev_lb_logits, v_ev_a_onorm, v_ev_b_ln_w, v_ev_b_ln_b, v_ev_b_ws, v_ev_b_bias, v_ev_w_out, v_od_w_in, v_od_q_norm, v_od_w_qb, v_od_kv_norm, v_od_w_kvb, v_od_w_out):
    me = 4 * lax.axis_index("x") + 2 * lax.axis_index("y") + lax.axis_index("c")
    bf = lambda w: w[0].astype(BF16)

    norms = jnp.pad(jnp.concatenate([od_q_norm, od_kv_norm], axis=1), ((0, 7), (0, 0)))
    sent = {}
    sent["ev_in"], tok = _gather2_send("gather_ev_in", [bf(ev_w_in)])
    sent["ev_out"], tok = _gather2_send("gather_ev_out", [bf(ev_w_out)], deps=[tok])
    sent["od"], tok = _gather2_send("gather_od", [od_w_in[0].T.astype(BF16), bf(od_w_qb), bf(od_w_kvb), norms,
                                                 bf(od_w_out)], deps=[tok])
    cos_t, sin_t = _rope_tables(positions)
    od = []

    def get_w(group, after):
        if group == "ev_in":
            relayed, token = _gather2_relay("relay_ev_in", sent["ev_in"], [after, cos_t, sin_t])
            return _split_done("arrived_ev_in", relayed, token)[0]
        if group == "ev_out":
            relayed, token = _gather2_relay("relay_ev_out", sent["ev_out"], after)
            return _split_done("arrived_ev_out", relayed, token)[0].reshape(1, D, D)
        if group == "od_relay":
            sent["od_relayed"], _ = _gather2_relay("relay_od", sent["od"], after)
            return None
        if not od:
            od.extend(_split_done("arrived_od", sent["od_relayed"], after))
        w_od_in, w_qb, w_kvb, norms_all, w_od_out = od
        if group == "od_out":
            return w_od_out.reshape(1, D, D)
        return (_odd_in_layout(w_od_in), _qb_layout(w_qb), w_kvb,
                norms_all[:, 0, :64].reshape(1, C_RANK), norms_all[:, 0, 64:].reshape(1, C_RANK))

    scatters = {}

    def put_g(group, grads):
        if group == "ev_in":
            core = lax.axis_index("c").astype(jnp.int32).reshape(1)
            paired, token = _scatter2_pair("pair_ev_in", [grads("ev_in_dw_sibling", 1 - core)])
            mine = grads("ev_in_dw_own", core, deps=[token])
            pair = _split_done("paired_ev_in", paired, mine)[0]
            scatters[group], token = _scatter2_send("scatter_ev_in", [_pair_add("pair_add_ev_in", mine, pair)])
        else:
            scatters[group], token = _exchange_start("scatter_" + group, grads, False)
        return token

    def put_small(early):
        scatters["small"], token = _exchange_start("gather_small_early", early, True)
        return token

    grad_x, _, dnpre0 = _forward_backward(
        x[0], cos_t, sin_t, loss_target[0], norm_pre, norm_post, ev_lb_logits, ev_a_onorm, ev_b_ln_w,
        ev_b_ln_b, ev_b_ws, ev_b_bias, get_w, put_g, put_small, start_dep=tok)

    big_w = {"ev_w_in": ev_w_in, "ev_w_out": ev_w_out, "od_w_in": od_w_in, "od_w_qb": od_w_qb,
             "od_w_kvb": od_w_kvb, "od_w_out": od_w_out}
    big_m = {"ev_w_in": m_ev_w_in, "ev_w_out": m_ev_w_out, "od_w_in": m_od_w_in, "od_w_qb": m_od_w_qb,
             "od_w_kvb": m_od_w_kvb, "od_w_out": m_od_w_out}
    big_v = {"ev_w_in": v_ev_w_in, "ev_w_out": v_ev_w_out, "od_w_in": v_od_w_in, "od_w_qb": v_od_w_qb,
             "od_w_kvb": v_od_w_kvb, "od_w_out": v_od_w_out}
    big_out = {}
    after = grad_x
    for group, names in (("od_out", ["od_w_out"]), ("od_qkv", ["od_w_qb", "od_w_kvb"]), ("od_in", ["od_w_in"]),
                         ("ev_out", ["ev_w_out"])):
        parts = _exchange_wait("summed_" + group, scatters[group], after)
        for nm, p in zip(names, parts):
            w, m, v = big_w[nm][0], big_m[nm][0], big_v[nm][0]
            if nm == "od_w_in":
                res_t = _adamw("adamw_" + nm, [(p, N_DEV)], w.T, m.T, v.T, w.shape[1], 512)
                big_out[nm] = [r.T[None] for r in res_t]
            else:
                big_out[nm] = [r[None] for r in _adamw("adamw_" + nm, [(p, N_DEV)], w, m, v, w.shape[0] // 8)]
            after = big_out[nm][0]

    late_all = _exchange("gather_small_late", [dnpre0], gather=True, deps=[after])[0]
    early_all = _exchange_wait("arrived_small_early", scatters["small"], late_all)

    small_w = (norm_pre, norm_post, ev_lb_logits, ev_a_onorm, ev_b_ln_w, ev_b_ln_b, ev_b_ws, ev_b_bias)
    small_m = (m_norm_pre, m_norm_post, m_ev_lb_logits, m_ev_a_onorm, m_ev_b_ln_w, m_ev_b_ln_b, m_ev_b_ws, m_ev_b_bias)
    small_v = (v_norm_pre, v_norm_post, v_ev_lb_logits, v_ev_a_onorm, v_ev_b_ln_w, v_ev_b_ln_b, v_ev_b_ws, v_ev_b_bias)
    wmv = [tuple(a.reshape(s) for a in t) for s, t in zip(SMALL_PARAM_SHAPES, zip(small_w, small_m, small_v))]
    small_res, loss_row, g_norm_rows = _adamw_small(late_all, early_all, wmv)
    small_out = [[r.reshape(w.shape) for r in four] for four, w in zip(small_res, small_w)]
    loss = loss_row[0, 0]

    g_norms = jnp.concatenate([lax.dynamic_slice(g_norm_rows, (0, 64 * me), (1, 64)),
                               lax.dynamic_slice(g_norm_rows, (1, 64 * me), (1, 64))], axis=1)
    res_n = _adamw("adamw_norms", [(g_norms[None], 1)],
                   jnp.concatenate([od_q_norm, od_kv_norm], axis=1),
                   jnp.concatenate([m_od_q_norm, m_od_kv_norm], axis=1),
                   jnp.concatenate([v_od_q_norm, v_od_kv_norm], axis=1), 1)
    qn_out = [r[:, :64] for r in res_n]
    kvn_out = [r[:, 64:] for r in res_n]

    chip_sums, from_peers = _split_done("summed_ev_in", scatters["ev_in"], loss_row, all_bufs=True)
    own_chip = lax.dynamic_slice_in_dim(chip_sums, me >> 1, 1, 0)
    w = ev_w_in[0]
    big_out["ev_w_in"] = [r[None] for r in _adamw("adamw_ev_w_in", [(own_chip, 1), (from_peers, 3)], w, m_ev_w_in[0],
                                                  v_ev_w_in[0], w.shape[0] // 8)]

    order = ("norm_pre", "norm_post", "ev_w_in", "ev_lb_logits", "ev_a_onorm", "ev_b_ln_w", "ev_b_ln_b",
             "ev_b_ws", "ev_b_bias", "ev_w_out", "od_w_in", "od_q_norm", "od_w_qb", "od_kv_norm",
             "od_w_kvb", "od_w_out")
    small_names = ("norm_pre", "norm_post", "ev_lb_logits", "ev_a_onorm", "ev_b_ln_w", "ev_b_ln_b",
                   "ev_b_ws", "ev_b_bias")
    outs = [loss, grad_x[None]]
    for kind in range(4):
        for nm in order:
            if nm in big_out:
                outs.append(big_out[nm][kind])
            elif nm == "od_q_norm":
                outs.append(qn_out[kind])
            elif nm == "od_kv_norm":
                outs.append(kvn_out[kind])
            else:
                outs.append(small_out[small_names.index(nm)][kind])
    return tuple(outs)
```

```python
import functools

import jax
import jax.numpy as jnp
from jax import lax
from jax.experimental import pallas as pl
from jax.experimental.pallas import tpu as pltpu

F32 = jnp.float32
BF16 = jnp.bfloat16

N_DEV = 8
T = 2048
D = 2048
EPS = 1e-6
A_HEADS = 8
HD = 128
A_CHUNK = 64
A_SUB = 16
B_GROUPS = 8
B_CHUNK = 128
EVEN_IN = 7168
C_HEADS = 16
C_RANK = 512
C_NOPE = 128
C_ROPE = 64
C_QK = C_NOPE + C_ROPE
C_V = 128
ODD_IN = 3136
ODD_IN_PAD = 3200
QP = 256
ROPE_THETA = 10000.0
ATT_SCALE = C_QK ** -0.5

ADAM_LR = 0.001
ADAM_B1 = 0.9
ADAM_B2 = 0.999
ADAM_EPS = 1e-08
ADAM_WD = 0.01
ADAM_STEP = 10

VMEM_LIMIT_V7X = 56 * 1024 * 1024
MESH_ID = pl.DeviceIdType.MESH


def _params(n_grid):
    return pltpu.CompilerParams(dimension_semantics=("arbitrary",) * n_grid,
                                vmem_limit_bytes=VMEM_LIMIT_V7X)


def _dg(a, b, ca, cb):
    return lax.dot_general(a.astype(BF16), b.astype(BF16), (((ca,), (cb,)), ((), ())),
                           preferred_element_type=F32)


def _raw_nn(a, b):
    return _dg(a, b, 1, 0)


def _raw_nt(a, b):
    return _dg(a, b, 1, 1)


def _raw_tn(a, b):
    return _dg(a, b, 0, 0)


@jax.custom_vjp
def _dot_nn(a, b):
    return _raw_nn(a, b)


def _dot_nn_fwd(a, b):
    return _raw_nn(a, b), (a.astype(BF16), b.astype(BF16))


def _dot_nn_bwd(res, g):
    a, b = res
    return _raw_nt(g, b), _raw_tn(a, g)


_dot_nn.defvjp(_dot_nn_fwd, _dot_nn_bwd)


@jax.custom_vjp
def _dot_nt(a, b):
    return _raw_nt(a, b)


def _dot_nt_fwd(a, b):
    return _raw_nt(a, b), (a.astype(BF16), b.astype(BF16))


def _dot_nt_bwd(res, g):
    a, b = res
    return _raw_nn(g, b), _raw_tn(g, a)


_dot_nt.defvjp(_dot_nt_fwd, _dot_nt_bwd)


@jax.custom_vjp
def _dot_tn(a, b):
    return _raw_tn(a, b)


def _dot_tn_fwd(a, b):
    return _raw_tn(a, b), (a.astype(BF16), b.astype(BF16))


def _dot_tn_bwd(res, g):
    a, b = res
    return _raw_nt(b, g), _raw_nn(a, g)


_dot_tn.defvjp(_dot_tn_fwd, _dot_tn_bwd)


@jax.custom_vjp
def _sigmoid(x):
    e = jnp.exp(-jnp.abs(x))
    return jnp.where(x >= 0, 1.0 / (1.0 + e), e / (1.0 + e))


def _sigmoid_fwd(x):
    s = _sigmoid(x)
    return s, s


def _sigmoid_bwd(s, g):
    return (g * s * (1.0 - s),)


_sigmoid.defvjp(_sigmoid_fwd, _sigmoid_bwd)


def _silu(x):
    return x * _sigmoid(x)


def _rms(x, w):
    return x * lax.rsqrt(jnp.mean(x * x, axis=-1, keepdims=True) + EPS) * w


def _split3(x):
    hi = x.astype(BF16)
    r = x - hi.astype(F32)
    mid = r.astype(BF16)
    lo = (r - mid.astype(F32)).astype(BF16)
    return hi, mid, lo


def _mask_apply(mask_bf16, x, contract):
    out = None
    for piece in _split3(x):
        d = lax.dot_general(mask_bf16, piece, (((contract,), (0,)), ((), ())),
                            preferred_element_type=F32)
        out = d if out is None else out + d
    return out


def _chunk_tri(rows):
    r = lax.broadcasted_iota(jnp.int32, (rows, rows), 0)
    c = lax.broadcasted_iota(jnp.int32, (rows, rows), 1)
    return ((r >= c) & (r // A_CHUNK == c // A_CHUNK)).astype(BF16)


@jax.custom_vjp
def _chunk_cumsum(x):
    return _mask_apply(_chunk_tri(x.shape[0]), x, 1)


def _chunk_cumsum_fwd(x):
    return _chunk_cumsum(x), None


def _chunk_cumsum_bwd(_, g):
    return (_mask_apply(_chunk_tri(g.shape[0]), g, 0),)


_chunk_cumsum.defvjp(_chunk_cumsum_fwd, _chunk_cumsum_bwd)


def _hgrn2_rows(q, zf, v, ga, st, l0, l1, onorm):
    rows = q.shape[0]
    n_sub = A_CHUNK // A_SUB
    mx = jnp.maximum(l0, l1)
    e0 = jnp.exp(l0 - mx)
    e1 = jnp.exp(l1 - mx)
    lb = e0 / (e0 + e1)
    lf = jnp.log(lb + (1.0 - lb) * _sigmoid(zf))
    k = (1.0 - lb) * _sigmoid(-zf)
    b = _chunk_cumsum(lf)

    t_idx = lax.broadcasted_iota(jnp.int32, (A_CHUNK, n_sub * A_CHUNK), 0)
    c_idx = lax.broadcasted_iota(jnp.int32, (A_CHUNK, n_sub * A_CHUNK), 1)
    sel = (c_idx // A_CHUNK == t_idx // A_SUB) & (c_idx % A_CHUNK <= t_idx)
    key_row = lax.broadcasted_iota(jnp.int32, (A_CHUNK, HD), 0)

    outs = []
    for n in range(rows // A_CHUNK):
        lo = n * A_CHUNK
        qc, kc, vc = q[lo:lo + A_CHUNK], k[lo:lo + A_CHUNK], v[lo:lo + A_CHUNK]
        lfc, bc = lf[lo:lo + A_CHUNK], b[lo:lo + A_CHUNK]
        b_last = bc[A_CHUNK - 1:A_CHUNK]
        o_inter = _dot_nt(qc * jnp.exp(bc), st)
        kv_t = _dot_tn(vc, kc * jnp.exp(b_last - bc))
        st = st * jnp.exp(b_last) + kv_t
        g_rows, k_subs = [], []
        for i in range(n_sub):
            g_i = bc[i * A_SUB:i * A_SUB + 1] - lfc[i * A_SUB:i * A_SUB + 1]
            g_rows.append(jnp.broadcast_to(g_i, (A_SUB, HD)))
            expo = jnp.where(key_row < (i + 1) * A_SUB, g_i - bc, -jnp.inf)
            k_subs.append(kc * jnp.exp(expo))
        q_sub = qc * jnp.exp(bc - jnp.concatenate(g_rows, axis=0))
        scores = _dot_nt(q_sub, jnp.concatenate(k_subs, axis=0))
        scores = jnp.where(sel, scores, 0.0)
        o_intra = _dot_nn(scores, jnp.concatenate([vc] * n_sub, axis=0))
        outs.append(o_inter + o_intra)
    o = jnp.concatenate(outs, axis=0)
    return _rms(o, onorm) * _silu(ga), st


def _gmlp_rows(u, vb, gb, lnw, lnb, ws, bias):
    rows = u.shape[0]
    mu = jnp.mean(vb, axis=-1, keepdims=True)
    xc = vb - mu
    vg = xc * lax.rsqrt(jnp.mean(xc * xc, axis=-1, keepdims=True) + EPS) * lnw + lnb
    r = lax.broadcasted_iota(jnp.int32, (B_CHUNK, B_CHUNK), 0)
    c = lax.broadcasted_iota(jnp.int32, (B_CHUNK, B_CHUNK), 1)
    ws_causal = jnp.where(r >= c, ws, 0.0)
    svs = [_dot_nn(ws_causal, vg[n * B_CHUNK:(n + 1) * B_CHUNK]) + bias
           for n in range(rows // B_CHUNK)]
    return u * jnp.concatenate(svs, axis=0) * _silu(gb)


def _rope(x, cos_t, sin_t):
    return x * cos_t + pltpu.roll(x, 64, 1) * sin_t


def _rope_transpose(g, cos_t, sin_t):
    return g * cos_t + pltpu.roll(g * sin_t, 64, 1)


ANY_SPEC = pl.BlockSpec(memory_space=pl.ANY)


def _live(deps):
    return [d for d in deps if d is not None]


def _skip_deps(body, n_in, n_deps):
    def wrapped(*refs):
        return body(*refs[:n_in], *refs[n_in + n_deps:])
    return wrapped


def _pure_call(name, fn, grid, in_specs, out_specs, out_shape, args, n_acc=0, deps=()):
    deps = _live(deps)
    n_in, n_out, n_deps = len(in_specs), len(out_specs), len(deps)
    in_specs = list(in_specs) + [ANY_SPEC] * n_deps
    args = tuple(args) + tuple(deps)

    def body(*refs):
        res = fn(*[r[...] for r in refs[:n_in]])
        if not isinstance(res, (tuple, list)):
            res = (res,)
        outs = refs[n_in + n_deps:n_in + n_deps + n_out]
        for o, r in zip(outs[:n_out - n_acc], res[:n_out - n_acc]):
            o[...] = r.astype(o.dtype)
        if n_acc:
            first = functools.reduce(jnp.logical_and, [pl.program_id(i) == 0 for i in range(len(grid))])
            for o, r in zip(outs[n_out - n_acc:], res[n_out - n_acc:]):
                @pl.when(first)
                def _(o=o, r=r):
                    o[...] = r.astype(o.dtype)

                @pl.when(jnp.logical_not(first))
                def _(o=o, r=r):
                    o[...] += r.astype(o.dtype)

    return pl.pallas_call(body, name=name, grid=grid, in_specs=in_specs, out_specs=out_specs,
                          out_shape=out_shape, compiler_params=_params(len(grid)))(*args)


def _sds(shape, dtype):
    return jax.ShapeDtypeStruct(shape, dtype)


def _row_spec(tm, width, col=0):
    return pl.BlockSpec((tm, width), lambda i, col=col: (i, col))


def _full_spec(shape):
    nd = len(shape)
    return pl.BlockSpec(shape, lambda *_: (0,) * nd)


def _mm_nn(name, a, b, out_dtype, tm, tn, deps=()):
    deps = _live(deps)
    m, k = a.shape
    j, _, n = b.shape
    per = n // tn

    def body(a_ref, b_ref, o_ref):
        o_ref[...] = _raw_nn(a_ref[...], b_ref[...]).astype(o_ref.dtype)

    return pl.pallas_call(
        _skip_deps(body, 2, len(deps)), name=name, grid=(m // tm, j * per),
        in_specs=[pl.BlockSpec((tm, k), lambda i, c: (i, 0)),
                  pl.BlockSpec((None, k, tn), lambda i, c: (c // per, 0, c % per))] + [ANY_SPEC] * len(deps),
        out_specs=pl.BlockSpec((tm, tn), lambda i, c: (i, c)),
        out_shape=_sds((m, j * n), out_dtype), compiler_params=_params(2))(a, b, *deps)


def _mm_nt(name, a, b, out_dtype, tm, tn, deps=()):
    deps = _live(deps)
    m = a.shape[0]
    j, nn, n = b.shape

    def body(a_ref, b_ref, o_ref):
        b_all = b_ref[0] if j == 1 else jnp.concatenate([b_ref[s] for s in range(j)], axis=1)
        o_ref[...] = _raw_nt(a_ref[...], b_all).astype(o_ref.dtype)

    return pl.pallas_call(
        _skip_deps(body, 2, len(deps)), name=name, grid=(m // tm, nn // tn),
        in_specs=[pl.BlockSpec((tm, j * n), lambda i, c: (i, 0)),
                  pl.BlockSpec((j, tn, n), lambda i, c: (0, c, 0))] + [ANY_SPEC] * len(deps),
        out_specs=pl.BlockSpec((tm, tn), lambda i, c: (i, c)),
        out_shape=_sds((m, nn), out_dtype), compiler_params=_params(2))(a, b, *deps)


def _mm_tn(name, a, b, j, out_dtype, tm, tn, deps=()):
    deps = _live(deps)
    k, m = a.shape
    n = b.shape[1] // j
    per = n // tn

    def body(a_ref, b_ref, o_ref):
        o_ref[...] = _raw_tn(a_ref[...], b_ref[...]).astype(o_ref.dtype)

    return pl.pallas_call(
        _skip_deps(body, 2, len(deps)), name=name, grid=(m // tm, j * per),
        in_specs=[pl.BlockSpec((k, tm), lambda i, c: (0, i)),
                  pl.BlockSpec((k, tn), lambda i, c: (0, c))] + [ANY_SPEC] * len(deps),
        out_specs=pl.BlockSpec((None, tm, tn), lambda i, c: (c // per, i, c % per)),
        out_shape=_sds((j, m, n), out_dtype), compiler_params=_params(2))(a, b, *deps)


def _mm_tn_parity(name, a, b, j, parity, out_dtype, tm, deps=()):
    deps = _live(deps)
    k, m = a.shape
    n = b.shape[1] // j

    def body(par_ref, a_ref, b_ref, o_ref):
        del par_ref
        o_ref[...] = _raw_tn(a_ref[...], b_ref[...]).astype(o_ref.dtype)

    grid_spec = pltpu.PrefetchScalarGridSpec(
        num_scalar_prefetch=1, grid=(m // tm, j // 2),
        in_specs=[pl.BlockSpec((k, tm), lambda i, s, par: (0, i)),
                  pl.BlockSpec((k, n), lambda i, s, par: (0, 2 * s + par[0]))] + [ANY_SPEC] * len(deps),
        out_specs=pl.BlockSpec((None, tm, n), lambda i, s, par: (s, i, 0)))
    return pl.pallas_call(
        lambda par_ref, *refs: _skip_deps(functools.partial(body, par_ref), 2, len(deps))(*refs),
        name=name, grid_spec=grid_spec, out_shape=_sds((j // 2, m, n), out_dtype),
        compiler_params=_params(2))(parity, a, b, *deps)


TM = 256


def _pre_norm(name, x, w_row, deps=()):
    def fn(xv, w):
        return _rms(xv, w)
    return _pure_call(name, fn, (T // TM,), [_row_spec(TM, D), _full_spec((1, D))],
                      [_row_spec(TM, D)], [_sds((T, D), BF16)], (x, w_row), deps=deps)[0]


def _post_pre_norm(x, y, w_post, w_pre):
    def fn(xv, yv, wp, wn):
        x1 = xv + _rms(yv, wp)
        return x1, _rms(x1, wn)
    return _pure_call("post_pre_norm", fn, (T // TM,),
                      [_row_spec(TM, D), _row_spec(TM, D), _full_spec((1, D)), _full_spec((1, D))],
                      [_row_spec(TM, D), _row_spec(TM, D)],
                      [_sds((T, D), F32), _sds((T, D), BF16)], (x, y, w_post, w_pre))


def _post_pre_norm_bwd(y, x1, w_post, w_pre, dx1_in, dh1, deps=()):
    def fn(yv, x1v, wp, wn, dx1v, dh1v):
        _, vjp_pre = jax.vjp(_rms, x1v, wn)
        dx1_h, dwn = vjp_pre(dh1v)
        dx1 = dx1v + dx1_h
        _, vjp_post = jax.vjp(_rms, yv, wp)
        dy, dwp = vjp_post(dx1)
        return dx1, dy, dwp, dwn
    return _pure_call("post_pre_norm_bwd", fn, (T // TM,),
                      [_row_spec(TM, D), _row_spec(TM, D), _full_spec((1, D)), _full_spec((1, D)),
                       _row_spec(TM, D), _row_spec(TM, D)],
                      [_row_spec(TM, D), _row_spec(TM, D), _full_spec((1, D)), _full_spec((1, D))],
                      [_sds((T, D), F32), _sds((T, D), BF16), _sds((1, D), F32), _sds((1, D), F32)],
                      (y, x1, w_post, w_pre, dx1_in, dh1), n_acc=2, deps=deps)


def _out_proj_loss(og, w_out, x1, w_post, target):
    tm = 512

    def fn(ogv, w, x1v, wp, tv):
        r, vjp = jax.vjp(_rms, _raw_nn(ogv, w[0]), wp)
        err = x1v + r - tv
        part = 0.5 * jnp.sum(jnp.mean(err * err, axis=-1, keepdims=True), axis=0, keepdims=True)
        dx2 = err * (1.0 / D)
        dy, dwp = vjp(dx2)
        return dx2, dy, jnp.broadcast_to(part, (1, 128)), dwp
    return _pure_call("od_out_loss", fn, (T // tm,),
                      [_row_spec(tm, D), _full_spec((1, D, D)), _row_spec(tm, D), _full_spec((1, D)), _row_spec(tm, D)],
                      [_row_spec(tm, D), _row_spec(tm, D), _full_spec((1, 128)), _full_spec((1, D))],
                      [_sds((T, D), F32), _sds((T, D), BF16), _sds((1, 128), F32), _sds((1, D), F32)],
                      (og, w_out, x1, w_post, target), n_acc=2)


def _pre_norm_bwd(x, w_row, dh, dx_res, deps=()):
    def fn(xv, w, dhv, dxv):
        _, vjp = jax.vjp(_rms, xv, w)
        dx, dw = vjp(dhv)
        return dxv + dx, dw
    return _pure_call("pre_norm_bwd", fn, (T // TM,),
                      [_row_spec(TM, D), _full_spec((1, D)), _row_spec(TM, D), _row_spec(TM, D)],
                      [_row_spec(TM, D), _full_spec((1, D))],
                      [_sds((T, D), F32), _sds((1, D), F32)], (x, w_row, dh, dx_res), n_acc=1, deps=deps)


RA = 256


def _head(ref, hh):
    return ref[:, hh * HD:(hh + 1) * HD]


def _z_part(z_ref, k, h):
    lo = (k * A_HEADS + h) * HD
    return z_ref[:, lo:lo + HD]


def _even_specs():
    return [_full_spec((1, A_HEADS * HD)), _full_spec((1, A_HEADS * HD)), _full_spec((1, HD)),
            _full_spec((1, B_GROUPS * HD)), _full_spec((1, B_GROUPS * HD)),
            _full_spec((B_GROUPS, B_CHUNK, B_CHUNK)), _full_spec((B_GROUPS, B_CHUNK, 1))]


def _even_fwd(z, l0, l1, onorm, lnw, lnb, ws, bias):
    nb = T // RA

    def body(z_ref, l0_ref, l1_ref, on_ref, lnw_ref, lnb_ref, ws_ref, bias_ref, cat_ref, sst_ref, st_scr):
        @pl.when(pl.program_id(0) == 0)
        def _():
            st_scr[...] = jnp.zeros_like(st_scr)

        for hh in range(A_HEADS):
            st = st_scr[hh]
            sst_ref[hh] = st
            out, st_new = _hgrn2_rows(_z_part(z_ref, 0, hh), _z_part(z_ref, 1, hh), _z_part(z_ref, 2, hh),
                                      _z_part(z_ref, 3, hh), st, _head(l0_ref, hh), _head(l1_ref, hh), on_ref[...])
            cat_ref[:, hh * HD:(hh + 1) * HD] = out.astype(cat_ref.dtype)
            st_scr[hh] = st_new
        for gg in range(B_GROUPS):
            out = _gmlp_rows(_z_part(z_ref, 4, gg), _z_part(z_ref, 5, gg), _z_part(z_ref, 6, gg),
                             _head(lnw_ref, gg), _head(lnb_ref, gg), ws_ref[gg], bias_ref[gg])
            cat_ref[:, (A_HEADS + gg) * HD:(A_HEADS + gg + 1) * HD] = out.astype(cat_ref.dtype)

    return pl.pallas_call(
        body, name="even_mixers_fwd", grid=(nb,),
        in_specs=[pl.BlockSpec((RA, EVEN_IN), lambda r: (r, 0))] + _even_specs(),
        out_specs=[pl.BlockSpec((RA, 2 * A_HEADS * HD), lambda r: (r, 0)),
                   pl.BlockSpec((A_HEADS, None, HD, HD), lambda r: (0, r, 0, 0))],
        out_shape=[_sds((T, 2 * A_HEADS * HD), BF16), _sds((A_HEADS, nb, HD, HD), F32)],
        scratch_shapes=[pltpu.VMEM((A_HEADS, HD, HD), F32)],
        compiler_params=_params(1))(z, l0, l1, onorm, lnw, lnb, ws, bias)


def _even_bwd(z, l0, l1, onorm, lnw, lnb, ws, bias, sst, dcat, deps=()):
    nb = T // RA
    deps = _live(deps)

    def body(z_ref, l0_ref, l1_ref, on_ref, lnw_ref, lnb_ref, ws_ref, bias_ref, sst_ref, dcat_ref,
             dz_ref, dl0_ref, dl1_ref, don_ref, dlnw_ref, dlnb_ref, dws_ref, dbias_ref, ds_scr):
        first = pl.program_id(0) == 0

        @pl.when(first)
        def _():
            ds_scr[...] = jnp.zeros_like(ds_scr)

        def put(k, h, val):
            lo = (k * A_HEADS + h) * HD
            dz_ref[:, lo:lo + HD] = val.astype(dz_ref.dtype)

        sums = []
        don = None
        for hh in range(A_HEADS):
            lanes = slice(hh * HD, (hh + 1) * HD)
            _, vjp = jax.vjp(_hgrn2_rows, _z_part(z_ref, 0, hh), _z_part(z_ref, 1, hh), _z_part(z_ref, 2, hh),
                             _z_part(z_ref, 3, hh), sst_ref[hh], _head(l0_ref, hh), _head(l1_ref, hh), on_ref[...])
            dq, dzf, dv, dga, dst, dl0, dl1, don_h = vjp((dcat_ref[:, lanes], ds_scr[hh]))
            for k, val in enumerate((dq, dzf, dv, dga)):
                put(k, hh, val)
            ds_scr[hh] = dst
            sums += [(dl0_ref, (slice(None), lanes), dl0), (dl1_ref, (slice(None), lanes), dl1)]
            don = don_h if don is None else don + don_h
        sums.append((don_ref, slice(None), don))
        for gg in range(B_GROUPS):
            lanes = slice(gg * HD, (gg + 1) * HD)
            _, vjp = jax.vjp(_gmlp_rows, _z_part(z_ref, 4, gg), _z_part(z_ref, 5, gg), _z_part(z_ref, 6, gg),
                             _head(lnw_ref, gg), _head(lnb_ref, gg), ws_ref[gg], bias_ref[gg])
            du, dv, dg, dlnw, dlnb, dws, dbias = vjp(dcat_ref[:, (A_HEADS + gg) * HD:(A_HEADS + gg + 1) * HD])
            for k, val in enumerate((du, dv, dg)):
                put(4 + k, gg, val)
            sums += [(dlnw_ref, (slice(None), lanes), dlnw), (dlnb_ref, (slice(None), lanes), dlnb),
                     (dws_ref, gg, dws), (dbias_ref, gg, dbias)]
        for ref, idx, val in sums:
            @pl.when(first)
            def _(ref=ref, idx=idx, val=val):
                ref[idx] = val

            @pl.when(jnp.logical_not(first))
            def _(ref=ref, idx=idx, val=val):
                ref[idx] += val

    small = _even_specs()
    return pl.pallas_call(
        _skip_deps(body, 10, len(deps)), name="even_mixers_bwd", grid=(nb,),
        in_specs=[pl.BlockSpec((RA, EVEN_IN), lambda r: (nb - 1 - r, 0))] + small
        + [pl.BlockSpec((A_HEADS, None, HD, HD), lambda r: (0, nb - 1 - r, 0, 0)),
           pl.BlockSpec((RA, 2 * A_HEADS * HD), lambda r: (nb - 1 - r, 0))] + [ANY_SPEC] * len(deps),
        out_specs=[pl.BlockSpec((RA, EVEN_IN), lambda r: (nb - 1 - r, 0))] + small,
        out_shape=[_sds((T, EVEN_IN), BF16), _sds((1, A_HEADS * HD), F32), _sds((1, A_HEADS * HD), F32),
                   _sds((1, HD), F32), _sds((1, B_GROUPS * HD), F32), _sds((1, B_GROUPS * HD), F32),
                   _sds((B_GROUPS, B_CHUNK, B_CHUNK), F32), _sds((B_GROUPS, B_CHUNK, 1), F32)],
        scratch_shapes=[pltpu.VMEM((A_HEADS, HD, HD), F32)],
        compiler_params=_params(1))(z, l0, l1, onorm, lnw, lnb, ws, bias, sst, dcat, *deps)


def _mla_pre(z1, qn, kvn, cos_t, sin_t):
    def fn(cq, ckv, kpe, cs, sn, wq, wkv):
        return _rms(cq, wq), _rms(ckv, wkv), _rope(kpe, cs, sn)
    return _pure_call("mla_pre", fn, (T // TM,),
                      [_row_spec(TM, C_RANK, 4), _row_spec(TM, C_RANK, 5), _row_spec(TM, HD, 24),
                       _row_spec(TM, HD), _row_spec(TM, HD),
                       _full_spec((1, C_RANK)), _full_spec((1, C_RANK))],
                      [_row_spec(TM, C_RANK), _row_spec(TM, C_RANK), _row_spec(TM, HD)],
                      [_sds((T, C_RANK), BF16), _sds((T, C_RANK), BF16), _sds((T, HD), BF16)],
                      (z1, z1, z1, cos_t, sin_t, qn, kvn))


def _mla_pre_bwd(z1, qn, kvn, cos_t, sin_t, dcqn, dckvn, dkp, dgate, deps=()):
    def fn(cq, ckv, cs, sn, wq, wkv, g_q, g_kv, g_kp, g_gate):
        _, vjp_q = jax.vjp(_rms, cq, wq)
        dcq, dwq = vjp_q(g_q)
        _, vjp_kv = jax.vjp(_rms, ckv, wkv)
        dckv, dwkv = vjp_kv(g_kv)
        dz1 = jnp.concatenate([g_gate, dcq.astype(BF16), dckv.astype(BF16),
                               _rope_transpose(g_kp, cs, sn).astype(BF16)], axis=1)
        return dz1, dwq, dwkv
    return _pure_call("mla_pre_bwd", fn, (T // TM,),
                      [_row_spec(TM, C_RANK, 4), _row_spec(TM, C_RANK, 5),
                       _row_spec(TM, HD), _row_spec(TM, HD),
                       _full_spec((1, C_RANK)), _full_spec((1, C_RANK)),
                       _row_spec(TM, C_RANK), _row_spec(TM, C_RANK), _row_spec(TM, HD), _row_spec(TM, D)],
                      [_row_spec(TM, ODD_IN_PAD), _full_spec((1, C_RANK)), _full_spec((1, C_RANK))],
                      [_sds((T, ODD_IN_PAD), BF16), _sds((1, C_RANK), F32), _sds((1, C_RANK), F32)],
                      (z1, z1, cos_t, sin_t, qn, kvn, dcqn, dckvn, dkp, dgate), n_acc=2, deps=deps)


TQ = 256
HP = 2
KVW = C_NOPE + C_V


def _att_keys(kv_ref, kp_ref, k_scr):
    @pl.when(pl.program_id(1) == 0)
    def _():
        for hh in range(HP):
            k_scr[hh, :, 0:C_NOPE] = kv_ref[:, hh * KVW:hh * KVW + C_NOPE]
            k_scr[hh, :, C_NOPE:QP] = kp_ref[...]


def _att_scores(q, cos_ref, sin_ref, k_scr, hh, n):
    keys = (n + 1) * TQ
    qr = jnp.concatenate([q[:, :C_NOPE], _rope(q[:, C_NOPE:], cos_ref[...], sin_ref[...])], axis=1).astype(BF16)
    return qr, _raw_nt(qr, k_scr[hh, 0:keys, :]) * ATT_SCALE


def _causal(x, n, fill):
    row = lax.broadcasted_iota(jnp.int32, (TQ, TQ), 0)
    col = lax.broadcasted_iota(jnp.int32, (TQ, TQ), 1)
    diag = jnp.where(col <= row, x[:, n * TQ:], fill)
    return diag if n == 0 else jnp.concatenate([x[:, :n * TQ], diag], axis=1)


def _per_query_block(fn):
    for n in range(T // TQ):
        pl.when(pl.program_id(1) == n)(functools.partial(fn, n))


def _att_in_specs():
    return [pl.BlockSpec((TQ, HP * QP), lambda g, i: (i, g)),
            pl.BlockSpec((TQ, HD), lambda g, i: (i, 0)),
            pl.BlockSpec((TQ, HD), lambda g, i: (i, 0)),
            pl.BlockSpec((T, HP * KVW), lambda g, i: (0, g)),
            pl.BlockSpec((T, HD), lambda g, i: (0, 0))]


def _attention_fwd(q, cos_t, sin_t, kv, kp, z1):
    def body(q_ref, cos_ref, sin_ref, kv_ref, kp_ref, gate_ref, o_ref, lse_ref, og_ref, k_scr):
        _att_keys(kv_ref, kp_ref, k_scr)

        def block(n):
            keys = (n + 1) * TQ
            for hh in range(HP):
                _, s = _att_scores(q_ref[:, hh * QP:(hh + 1) * QP], cos_ref, sin_ref, k_scr, hh, n)
                s = _causal(s, n, jnp.finfo(F32).min)
                m = jnp.max(s, axis=-1, keepdims=True)
                p = jnp.exp(s - m)
                l = jnp.sum(p, axis=-1, keepdims=True)
                v = kv_ref[0:keys, hh * KVW + C_NOPE:(hh + 1) * KVW]
                o = _raw_nn(p, v) / l
                lanes = slice(hh * C_V, (hh + 1) * C_V)
                o_ref[:, lanes] = o
                og_ref[:, lanes] = (o * _silu(gate_ref[:, lanes])).astype(og_ref.dtype)
                lse_ref[hh] = m + jnp.log(l)

        _per_query_block(block)

    heads = pl.BlockSpec((TQ, HP * C_V), lambda g, i: (i, g))
    return pl.pallas_call(
        body, name="attention_fwd", grid=(C_HEADS // HP, T // TQ), in_specs=_att_in_specs() + [heads],
        out_specs=[heads, pl.BlockSpec((HP, TQ, 1), lambda g, i: (g, i, 0)), heads],
        out_shape=[_sds((T, C_HEADS * C_V), F32), _sds((C_HEADS, T, 1), F32), _sds((T, C_HEADS * C_V), BF16)],
        scratch_shapes=[pltpu.VMEM((HP, T, QP), BF16)],
        compiler_params=_params(2))(q, cos_t, sin_t, kv, kp, z1)


def _attention_bwd(q, cos_t, sin_t, kv, kp, o, lse, dog, z1):
    nq = T // TQ

    def body(q_ref, cos_ref, sin_ref, kv_ref, kp_ref, o_ref, lse_ref, dog_ref, gate_ref,
             dq_ref, dkv_ref, dkp_ref, dgate_ref, k_scr, dk_scr, dv_scr):
        g, i = pl.program_id(0), pl.program_id(1)
        _att_keys(kv_ref, kp_ref, k_scr)

        @pl.when(i == 0)
        def _():
            dv_scr[...] = jnp.zeros_like(dv_scr)
            dk_scr[...] = jnp.zeros_like(dk_scr)

        def block(n):
            keys = (n + 1) * TQ
            for hh in range(HP):
                qr, s = _att_scores(q_ref[:, hh * QP:(hh + 1) * QP], cos_ref, sin_ref, k_scr, hh, n)
                p = _causal(jnp.exp(s - lse_ref[hh]), n, 0.0)
                lanes = slice(hh * C_V, (hh + 1) * C_V)
                ov, gate, dogv = o_ref[:, lanes], gate_ref[:, lanes], dog_ref[:, lanes]
                sig = _sigmoid(gate)
                silu = gate * sig
                dov = dogv * silu
                dgate_ref[:, lanes] = (dogv * ov * (sig + silu * (1.0 - sig))).astype(dgate_ref.dtype)
                delta = jnp.sum(dov * ov, axis=-1, keepdims=True)
                dp = _raw_nt(dov, kv_ref[0:keys, hh * KVW + C_NOPE:(hh + 1) * KVW])
                ds = p * (dp - delta) * ATT_SCALE
                dq = _raw_nn(ds, k_scr[hh, 0:keys, :])
                dq_ref[:, hh * QP:(hh + 1) * QP] = jnp.concatenate(
                    [dq[:, :C_NOPE], _rope_transpose(dq[:, C_NOPE:], cos_ref[...], sin_ref[...])],
                    axis=1).astype(dq_ref.dtype)
                dv_scr[hh, 0:keys, :] += _raw_tn(p, dov)
                dk_scr[hh, 0:keys, :] += _raw_tn(ds, qr)

        _per_query_block(block)

        @pl.when(i == nq - 1)
        def _():
            for hh in range(HP):
                dkv_ref[:, hh * KVW:(hh + 1) * KVW] = jnp.concatenate(
                    [dk_scr[hh, :, 0:C_NOPE], dv_scr[hh]], axis=1).astype(dkv_ref.dtype)

        @pl.when(jnp.logical_and(i == nq - 1, g == 0))
        def _():
            dkp_ref[...] = dk_scr[0, :, C_NOPE:QP]

        @pl.when(jnp.logical_and(i == nq - 1, g > 0))
        def _():
            dkp_ref[...] += dk_scr[0, :, C_NOPE:QP]

        @pl.when(i == nq - 1)
        def _():
            for hh in range(1, HP):
                dkp_ref[...] += dk_scr[hh, :, C_NOPE:QP]

    heads = pl.BlockSpec((TQ, HP * C_V), lambda g, i: (i, g))
    return pl.pallas_call(
        body, name="attention_bwd", grid=(C_HEADS // HP, nq),
        in_specs=_att_in_specs() + [heads, pl.BlockSpec((HP, TQ, 1), lambda g, i: (g, i, 0)), heads, heads],
        out_specs=[pl.BlockSpec((TQ, HP * QP), lambda g, i: (i, g)),
                   pl.BlockSpec((T, HP * KVW), lambda g, i: (0, g)),
                   _full_spec((T, HD)), heads],
        out_shape=[_sds((T, C_HEADS * QP), BF16), _sds((T, C_HEADS * KVW), BF16), _sds((T, HD), F32),
                   _sds((T, C_HEADS * C_V), BF16)],
        scratch_shapes=[pltpu.VMEM((HP, T, QP), BF16), pltpu.VMEM((HP, T, QP), F32), pltpu.VMEM((HP, T, C_V), F32)],
        compiler_params=_params(2))(q, cos_t, sin_t, kv, kp, o, lse, dog, z1)


def _adamw_math(w, g, m, v):
    m = ADAM_B1 * m + (1.0 - ADAM_B1) * g
    v = ADAM_B2 * v + (1.0 - ADAM_B2) * (g * g)
    m_hat = m / (1.0 - ADAM_B1 ** ADAM_STEP)
    v_hat = v / (1.0 - ADAM_B2 ** ADAM_STEP)
    delta = -ADAM_LR * (m_hat / (jnp.sqrt(v_hat) + ADAM_EPS) + ADAM_WD * w)
    return delta, m, v


def _adamw(name, parts, w, m, v, tr, tc=None):
    rows, cols = w.shape

    def fn(*vals):
        pvs, (wv, mv, vv) = vals[:len(parts)], vals[len(parts):]
        g = None
        for pv in pvs:
            for d in range(pv.shape[0]):
                term = pv[d].astype(F32)
                g = term if g is None else g + term
        return (g,) + _adamw_math(wv, g, mv, vv)

    tc = cols if tc is None else tc
    blk = pl.BlockSpec((tr, tc), lambda i, j: (i, j))
    part_specs = [pl.BlockSpec((n, tr, tc), lambda i, j: (0, i, j)) for _, n in parts]
    return _pure_call(name, fn, (rows // tr, cols // tc), part_specs + [blk, blk, blk],
                      [blk] * 4, [_sds((rows, cols), F32)] * 4, tuple(p for p, _ in parts) + (w, m, v))


SMALL_PARAM_SHAPES = ((2, D), (2, D), (2, A_HEADS * HD), (1, HD), (1, B_GROUPS * HD), (1, B_GROUPS * HD),
                      (B_GROUPS, B_CHUNK, B_CHUNK), (B_GROUPS, B_CHUNK))
SMALL_PIECES = ((0, 0, 0, 0), (0, 1, 1, 0), (1, 0, 1, 1), (1, 1, 1, 2), (2, 0, 2, 0), (2, 1, 2, 1),
                (3, 0, 3, 8), (4, 0, 2, 2), (5, 0, 2, 3))


def _small_rows(dnpre1, dnpost0, dnpost1, dl0, dl1, donorm, dlnw, dlnb, dws, dbias, dqn, dkvn, loss_part):
    return [jnp.concatenate([dnpre1, dnpost0, dnpost1], axis=0),
            jnp.concatenate([dl0, dl1, dlnw, dlnb], axis=0),
            jnp.concatenate([dbias.reshape(B_GROUPS, B_CHUNK), donorm, loss_part], axis=0),
            dws,
            jnp.concatenate([dqn, dkvn], axis=0)]


def _adamw_small(late_all, early_all, wmv):
    n_in = 6 + 3 * len(wmv)

    def body(*refs):
        gathered, params, outs = refs[:6], refs[6:n_in], refs[n_in:]

        def total(ref):
            s = ref[0]
            for d in range(1, N_DEV):
                s = s + ref[d]
            return s

        g_late, g2048, g1024, g128, g_ws, g512 = [total(r) for r in gathered]
        arrays = (g_late, g2048, g1024, g128)

        def update(p, rows, g):
            w_ref, m_ref, v_ref = params[3 * p:3 * p + 3]
            delta, m, v = _adamw_math(w_ref[rows], g, m_ref[rows], v_ref[rows])
            for out, val in zip(outs[4 * p:4 * p + 4], (g, delta, m, v)):
                out[rows] = val

        for p, row, arr, arr_row in SMALL_PIECES:
            update(p, pl.ds(row, 1), arrays[arr][arr_row:arr_row + 1])
        update(6, slice(None), g_ws)
        update(7, slice(None), g128[0:B_GROUPS])
        outs[32][...] = g128[B_GROUPS + 1:B_GROUPS + 2]
        outs[33][...] = g512

    vmem = pl.BlockSpec(memory_space=pltpu.VMEM)
    flat = [a for t in wmv for a in t]
    out_shape = [_sds(s, F32) for s in SMALL_PARAM_SHAPES for _ in range(4)] + [_sds((1, 128), F32), _sds((2, C_RANK), F32)]
    res = pl.pallas_call(body, name="adamw_small", in_specs=[vmem] * n_in, out_specs=[vmem] * len(out_shape),
                         out_shape=out_shape,
                         compiler_params=pltpu.CompilerParams(vmem_limit_bytes=VMEM_LIMIT_V7X))(late_all, *early_all, *flat)
    return [res[4 * p:4 * p + 4] for p in range(8)], res[32], res[33]


def _exchange(name, arrs, gather, deps=()):
    n = len(arrs)
    deps = _live(deps)

    def body(*refs):
        ins, outs = refs[:n], refs[n + len(deps):2 * n + len(deps)]
        send_sems, recv_sems, local_sems = refs[2 * n + len(deps):]
        x, y, c = lax.axis_index("x"), lax.axis_index("y"), lax.axis_index("c")
        me = 4 * x + 2 * y + c

        def peer(k):
            return (x ^ (k >> 2), y ^ ((k >> 1) & 1), c ^ (k & 1))

        def copy(a, k):
            src = ins[a] if gather else ins[a].at[me ^ k]
            return pltpu.make_async_remote_copy(
                src_ref=src, dst_ref=outs[a].at[me], send_sem=send_sems.at[a, k - 1],
                recv_sem=recv_sems.at[a, k - 1], device_id=peer(k), device_id_type=MESH_ID)

        def arrival(a, k):
            src = ins[a] if gather else ins[a].at[me]
            return pltpu.make_async_remote_copy(
                src_ref=src, dst_ref=outs[a].at[me ^ k], send_sem=send_sems.at[a, k - 1],
                recv_sem=recv_sems.at[a, k - 1], device_id=peer(k), device_id_type=MESH_ID)

        own = [pltpu.make_async_copy(ins[a] if gather else ins[a].at[me], outs[a].at[me], local_sems.at[a])
               for a in range(n)]
        for cp in own:
            cp.start()
        for k in range(1, N_DEV):
            for a in range(n):
                copy(a, k).start()
        for k in range(1, N_DEV):
            for a in range(n):
                arrival(a, k).wait_recv()
        for k in range(1, N_DEV):
            for a in range(n):
                copy(a, k).wait_send()
        for cp in own:
            cp.wait()

    any_spec = pl.BlockSpec(memory_space=pl.ANY)
    out_shape = [_sds((N_DEV,) + a.shape if gather else a.shape, a.dtype) for a in arrs]
    return pl.pallas_call(
        body, name=name, in_specs=[any_spec] * (n + len(deps)), out_specs=[any_spec] * n, out_shape=out_shape,
        scratch_shapes=[pltpu.SemaphoreType.DMA((n, N_DEV - 1)), pltpu.SemaphoreType.DMA((n, N_DEV - 1)),
                        pltpu.SemaphoreType.DMA((n,))],
        compiler_params=pltpu.CompilerParams(has_side_effects=True))(*arrs, *deps)


HBM_SPEC = pl.BlockSpec(memory_space=pltpu.HBM)
SEM_SPEC = pl.BlockSpec(memory_space=pltpu.SEMAPHORE)
DATAFLOW = pltpu.SideEffectType.DATAFLOW_SIDE_EFFECTING


def _my_index():
    return 4 * lax.axis_index("x") + 2 * lax.axis_index("y") + lax.axis_index("c")


def _plan_copies(plan, refs, send_sems, recv_sems):
    x, y, c = lax.axis_index("x"), lax.axis_index("y"), lax.axis_index("c")
    return [pltpu.make_async_remote_copy(
        src_ref=src, dst_ref=dst, send_sem=send_sems.at[i], recv_sem=recv_sems.at[i],
        device_id=(x ^ (k >> 2), y ^ ((k >> 1) & 1), c ^ (k & 1)), device_id_type=MESH_ID)
        for i, (src, dst, k) in enumerate(plan(refs, 4 * x + 2 * y + c))]


def _split_call(name, bufs, waits=None, starts=None, deps=()):
    n = len(bufs)
    deps = _live(deps)
    n_wait = 2 if waits else 0

    def body(*refs):
        zones = refs[:n]
        if waits:
            for cp in _plan_copies(waits[2], zones, refs[n], refs[n + 1]):
                cp.wait_send()
                cp.wait_recv()
        if starts:
            first_out = n + n_wait + len(deps)
            for cp in _plan_copies(starts[0], zones, refs[first_out], refs[first_out + 1]):
                cp.start()
            refs[-1][...] = jnp.zeros_like(refs[-1])

    out_specs, out_shape = [], []
    if starts:
        sems = pltpu.SemaphoreType.DMA((starts[1],))
        out_specs, out_shape = [SEM_SPEC, SEM_SPEC], [sems, sems]
    out_specs += [HBM_SPEC] * n
    out_shape += [pltpu.HBM(b.shape, b.dtype) for b in bufs]
    if starts:
        out_specs.append(pl.BlockSpec(memory_space=pltpu.VMEM))
        out_shape.append(_sds((8, 128), F32))
    first_buf = 2 if starts else 0
    res = pl.pallas_call(
        body, name=name,
        in_specs=[HBM_SPEC] * n + [SEM_SPEC] * n_wait + [ANY_SPEC] * len(deps),
        out_specs=out_specs, out_shape=out_shape,
        input_output_aliases={i: first_buf + i for i in range(n)},
        compiler_params=pltpu.CompilerParams(has_side_effects=DATAFLOW),
    )(*[pltpu.with_memory_space_constraint(b, pltpu.HBM) for b in bufs], *(waits[:2] if waits else ()), *deps)
    out_bufs = list(res[first_buf:first_buf + n])
    return out_bufs, ((res[0], res[1]) if starts else None), (res[-1] if starts else None)


def _direct_plan(n, gather):
    def plan(refs, me):
        return [(refs[a] if gather else refs[a].at[me ^ k], refs[n + a].at[me], k)
                for k in range(1, N_DEV) for a in range(n)]
    return plan


def _own_slot_filled(a, gather):
    me = _my_index()
    if gather:
        return lax.dynamic_update_slice_in_dim(lax.empty((N_DEV,) + a.shape, a.dtype), a[None], me, 0)
    return lax.dynamic_update_slice_in_dim(lax.empty(a.shape, a.dtype), lax.dynamic_slice_in_dim(a, me, 1, 0), me, 0)


def _exchange_start(name, arrs, gather, deps=()):
    n = len(arrs)
    lands = [_own_slot_filled(a, gather) for a in arrs]
    plan = _direct_plan(n, gather)
    bufs, sems, token = _split_call(name, list(arrs) + lands, starts=(plan, n * (N_DEV - 1)), deps=deps)
    return (n, plan, sems, bufs, None), token


def _exchange_wait(name, handle, after):
    return _split_done(name, handle, after)


ICI_PEERS = (2, 4, 6)
SIBLING = 1


def _gather2_send(name, arrs, deps=()):
    n = len(arrs)
    lands = [_own_slot_filled(a, True) for a in arrs]

    def plan(refs, me_):
        return [(refs[a], refs[n + a].at[me_], k) for k in (SIBLING,) + ICI_PEERS for a in range(n)]

    bufs, sems, token = _split_call(name, list(arrs) + lands, starts=(plan, 4 * n), deps=deps)
    return (n, plan, sems, bufs, None), token


def _gather2_relay(name, handle, after):
    n, plan, sems, bufs, _ = handle
    after = after if isinstance(after, (list, tuple)) else [after]

    def relay(refs, me_):
        return [(refs[n + a].at[me_ ^ k], refs[n + a].at[me_ ^ k], SIBLING) for k in ICI_PEERS for a in range(n)]

    bufs, sems2, token = _split_call(name, bufs, waits=(sems[0], sems[1], plan), starts=(relay, 3 * n), deps=after)
    return (n, relay, sems2, bufs, None), token


def _split_done(name, handle, after, all_bufs=False):
    n, plan, sems, bufs, _ = handle
    bufs, _, _ = _split_call(name, bufs, waits=(sems[0], sems[1], plan), deps=[after])
    return bufs if all_bufs else bufs[n:]


def _scatter2_pair(name, for_sibling, deps=()):
    n = len(for_sibling)
    pairs = [lax.empty(s.shape, s.dtype) for s in for_sibling]

    def plan(refs, me):
        del me
        return [(refs[a].at[s], refs[n + a].at[s], SIBLING) for s in range(4) for a in range(n)]

    bufs, sems, token = _split_call(name, list(for_sibling) + pairs, starts=(plan, 4 * n), deps=deps)
    return (n, plan, sems, bufs, None), token


def _pair_add(name, mine, pair):
    _, rows, cols = mine.shape
    tr = rows // 2

    def fn(a, b):
        return a.astype(F32) + b.astype(F32)

    blk = pl.BlockSpec((None, tr, cols), lambda s, i: (s, i, 0))
    return _pure_call(name, fn, (4, rows // tr), [blk, blk], [blk], [_sds(mine.shape, mine.dtype)], (mine, pair))[0]


def _scatter2_send(name, chip_sums, deps=()):
    n = len(chip_sums)
    finals = [lax.empty((3,) + c.shape[1:], c.dtype) for c in chip_sums]

    def plan(refs, me):
        return [(refs[a].at[(me >> 1) ^ j], refs[n + a].at[j - 1], 2 * j) for j in range(1, 4) for a in range(n)]

    bufs, sems, token = _split_call(name, list(chip_sums) + finals, starts=(plan, 3 * n), deps=deps)
    return (n, plan, sems, bufs, None), token


def _pad_rope(p):
    z = jnp.zeros(p.shape[:-1] + (32,), p.dtype)
    return jnp.concatenate([p[..., :32], z, p[..., 32:], z], axis=-1)


def _unpad_rope(p):
    return jnp.concatenate([p[..., :32], p[..., 64:96]], axis=-1)


def _odd_in_layout(wt):
    wt = wt.reshape(ODD_IN, D)
    cq, ckv, kpe, gate = wt[:512], wt[512:1024], wt[1024:1088], wt[1088:]
    z = jnp.zeros((32, D), wt.dtype)
    return jnp.concatenate([gate, cq, ckv, kpe[:32], z, kpe[32:], z], axis=0)


def _odd_in_unlayout(dwt):
    gate, cq, ckv, kpe = dwt[:2048], dwt[2048:2560], dwt[2560:3072], dwt[3072:]
    wt = jnp.concatenate([cq, ckv, kpe[:32], kpe[64:96], gate], axis=0)
    return wt.reshape(N_DEV, ODD_IN // N_DEV, D)


def _qb_layout(w):
    w = w.transpose(1, 0, 2).reshape(C_RANK, C_HEADS, C_QK)
    w = jnp.concatenate([w[..., :C_NOPE], _pad_rope(w[..., C_NOPE:])], axis=-1)
    return w.reshape(C_RANK, C_HEADS * QP)


def _qb_unlayout(dw):
    dw = dw.reshape(C_RANK, C_HEADS, QP)
    dw = jnp.concatenate([dw[..., :C_NOPE], _unpad_rope(dw[..., C_NOPE:])], axis=-1)
    return dw.reshape(C_RANK, N_DEV, C_HEADS * C_QK // N_DEV).transpose(1, 0, 2)


def _rope_tables(positions):
    inv_freq = ROPE_THETA ** (-jnp.arange(0, C_ROPE, 2, dtype=F32) / C_ROPE)
    ang = positions.astype(F32)[0][:, None] * inv_freq
    cos, sin = jnp.cos(ang), jnp.sin(ang)
    z = jnp.zeros_like(cos)
    return jnp.concatenate([cos, z, cos, z], axis=1), jnp.concatenate([-sin, z, sin, z], axis=1)


def _forward_backward(x, cos_t, sin_t, target, norm_pre, norm_post, lb_logits, a_onorm, ln_w, ln_b,
                      b_ws, b_bias, get_w, put_g, put_small=None, start_dep=None):
    npre0, npre1 = norm_pre[0:1], norm_pre[1:2]
    npost0, npost1 = norm_post[0:1], norm_post[1:2]
    l0, l1 = lb_logits[0:1], lb_logits[1:2]
    bias_col = b_bias.reshape(B_GROUPS, B_CHUNK, 1)
    ws = b_ws.reshape(B_GROUPS, B_CHUNK, B_CHUNK)

    h0 = _pre_norm("pre_norm0", x, npre0, deps=[start_dep])
    w_ev_in = get_w("ev_in", h0)
    z0 = _mm_nn("ev_in", h0, w_ev_in, F32, 1024, 896)
    cat, sst = _even_fwd(z0, l0, l1, a_onorm, ln_w, ln_b, ws, bias_col)
    w_ev_out = get_w("ev_out", cat)
    y0 = _mm_nn("ev_out", cat, w_ev_out, F32, 1024, 1024)
    get_w("od_relay", y0)
    x1, h1 = _post_pre_norm(x, y0, npost0, npre1)
    w_od_in, w_qb, w_kvb, q_norm, kv_norm = get_w("od_mid", h1)
    z1 = _mm_nt("od_in", h1, w_od_in[None], F32, 1024, 640)
    cqn, ckvn, kp = _mla_pre(z1, q_norm, kv_norm, cos_t, sin_t)
    q = _mm_nn("od_qb", cqn, w_qb[None], F32, 1024, 1024)
    kv = _mm_nn("od_kvb", ckvn, w_kvb, BF16, 1024, 512)
    o, lse, og = _attention_fwd(q, cos_t, sin_t, kv, kp, z1)
    w_od_out = get_w("od_out", og)
    dx2, dy1, loss_part, dnpost1 = _out_proj_loss(og, w_od_out, x1, npost1, target)

    g_od_out = _mm_tn("od_out_dw", og, dy1, 1, BF16, 1024, 1024)
    tok = put_g("od_out", [g_od_out.reshape(N_DEV, D // N_DEV, D)])
    dog = _mm_nt("od_out_dx", dy1, w_od_out, F32, 1024, 1024, deps=[tok])
    dq, dkv, dkp, dgate = _attention_bwd(q, cos_t, sin_t, kv, kp, o, lse, dog, z1)
    g_qb = _mm_tn("od_qb_dw", cqn, dq, 1, F32, 512, 1024)
    g_kvb = _mm_tn("od_kvb_dw", ckvn, dkv, N_DEV, BF16, 512, 512)
    tok = put_g("od_qkv", [_qb_unlayout(g_qb[0]).astype(BF16), g_kvb])
    dcqn = _mm_nt("od_qb_dx", dq, w_qb[None], F32, 1024, 512, deps=[tok])
    dckvn = _mm_nt("od_kvb_dx", dkv, w_kvb, F32, 1024, 512)
    dz1, dqn, dkvn = _mla_pre_bwd(z1, q_norm, kv_norm, cos_t, sin_t, dcqn, dckvn, dkp, dgate)
    g_od_in = _mm_tn("od_in_dw", dz1, h1, 1, F32, 640, 1024)
    tok = put_g("od_in", [_odd_in_unlayout(g_od_in[0]).astype(BF16)])
    dh1 = _mm_nn("od_in_dx", dz1, w_od_in[None], F32, 1024, 1024, deps=[tok])
    dx1, dy0, dnpost0, dnpre1 = _post_pre_norm_bwd(y0, x1, npost0, npre1, dx2, dh1)

    g_ev_out = _mm_tn("ev_out_dw", cat, dy0, 1, BF16, 1024, 1024)
    tok = put_g("ev_out", [g_ev_out.reshape(N_DEV, D // N_DEV, D)])
    dcat = _mm_nt("ev_out_dx", dy0, w_ev_out, F32, 1024, 1024, deps=[tok])
    dz0, dl0, dl1, donorm, dlnw, dlnb, dws, dbias = _even_bwd(z0, l0, l1, a_onorm, ln_w, ln_b, ws, bias_col, sst, dcat)
    early = _small_rows(dnpre1, dnpost0, dnpost1, dl0, dl1, donorm, dlnw, dlnb, dws, dbias, dqn, dkvn, loss_part)
    tok = put_small(early) if put_small else None
    small_tok = tok

    def ev_in_half(name, parity, deps=()):
        return _mm_tn_parity(name, h0, dz0, N_DEV, parity, BF16, 1024, deps=[small_tok] + list(deps))

    tok = put_g("ev_in", ev_in_half)
    dh0 = _mm_nt("ev_in_dx", dz0, w_ev_in, F32, 1024, 256, deps=[tok])
    grad_x, dnpre0 = _pre_norm_bwd(x, npre0, dh0, dx1)
    return grad_x, early, dnpre0


def kernel(x, positions, norm_pre, norm_post, ev_w_in, ev_lb_logits, ev_a_onorm, ev_b_ln_w, ev_b_ln_b, ev_b_ws, ev_b_bias, ev_w_out, od_w_in, od_q_norm, od_w_qb, od_kv_norm, od_w_kvb, od_w_out, loss_target, m_norm_pre, m_norm_post, m_ev_w_in, m_ev_lb_logits, m_ev_a_onorm, m_ev_b_ln_w, m_ev_b_ln_b, m_ev_b_ws, m_ev_b_bias, m_ev_w_out, m_od_w_in, m_od_q_norm, m_od_w_qb, m_od_kv_norm, m_od_w_kvb, m_od_w_out, v_norm_pre, v_norm_post, v_ev_w_in, v_ev_lb_logits, v_ev_a_onorm, v_ev_b_ln_w, v_ev_b_ln_b, v_ev_b_ws, v_ev_b_bias, v_ev_w_out, v_od_w_in, v_od_q_norm, v_od_w_qb, v_od_kv_norm, v_od_w_kvb, v_od_w_out):
    me = 4 * lax.axis_index("x") + 2 * lax.axis_index("y") + lax.axis_index("c")
    bf = lambda w: w[0].astype(BF16)

    norms = jnp.pad(jnp.concatenate([od_q_norm, od_kv_norm], axis=1), ((0, 7), (0, 0)))
    sent = {}
    sent["ev_in"], tok = _gather2_send("gather_ev_in", [bf(ev_w_in)])
    sent["ev_out"], tok = _gather2_send("gather_ev_out", [bf(ev_w_out)], deps=[tok])
    sent["od"], tok = _gather2_send("gather_od", [od_w_in[0].T.astype(BF16), bf(od_w_qb), bf(od_w_kvb), norms,
                                                 bf(od_w_out)], deps=[tok])
    cos_t, sin_t = _rope_tables(positions)
    od = []

    def get_w(group, after):
        if group == "ev_in":
            relayed, token = _gather2_relay("relay_ev_in", sent["ev_in"], [after, cos_t, sin_t])
            return _split_done("arrived_ev_in", relayed, token)[0]
        if group == "ev_out":
            relayed, token = _gather2_relay("relay_ev_out", sent["ev_out"], after)
            return _split_done("arrived_ev_out", relayed, token)[0].reshape(1, D, D)
        if group == "od_relay":
            sent["od_relayed"], _ = _gather2_relay("relay_od", sent["od"], after)
            return None
        if not od:
            od.extend(_split_done("arrived_od", sent["od_relayed"], after))
        w_od_in, w_qb, w_kvb, norms_all, w_od_out = od
        if group == "od_out":
            return w_od_out.reshape(1, D, D)
        return (_odd_in_layout(w_od_in), _qb_layout(w_qb), w_kvb,
                norms_all[:, 0, :64].reshape(1, C_RANK), norms_all[:, 0, 64:].reshape(1, C_RANK))

    scatters = {}

    def put_g(group, grads):
        if group == "ev_in":
            core = lax.axis_index("c").astype(jnp.int32).reshape(1)
            paired, token = _scatter2_pair("pair_ev_in", [grads("ev_in_dw_sibling", 1 - core)])
            mine = grads("ev_in_dw_own", core, deps=[token])
            pair = _split_done("paired_ev_in", paired, mine)[0]
            scatters[group], token = _scatter2_send("scatter_ev_in", [_pair_add("pair_add_ev_in", mine, pair)])
        else:
            scatters[group], token = _exchange_start("scatter_" + group, grads, False)
        return token

    def put_small(early):
        scatters["small"], token = _exchange_start("gather_small_early", early, True)
        return token

    grad_x, _, dnpre0 = _forward_backward(
        x[0], cos_t, sin_t, loss_target[0], norm_pre, norm_post, ev_lb_logits, ev_a_onorm, ev_b_ln_w,
        ev_b_ln_b, ev_b_ws, ev_b_bias, get_w, put_g, put_small, start_dep=tok)

    big_w = {"ev_w_in": ev_w_in, "ev_w_out": ev_w_out, "od_w_in": od_w_in, "od_w_qb": od_w_qb,
             "od_w_kvb": od_w_kvb, "od_w_out": od_w_out}
    big_m = {"ev_w_in": m_ev_w_in, "ev_w_out": m_ev_w_out, "od_w_in": m_od_w_in, "od_w_qb": m_od_w_qb,
             "od_w_kvb": m_od_w_kvb, "od_w_out": m_od_w_out}
    big_v = {"ev_w_in": v_ev_w_in, "ev_w_out": v_ev_w_out, "od_w_in": v_od_w_in, "od_w_qb": v_od_w_qb,
             "od_w_kvb": v_od_w_kvb, "od_w_out": v_od_w_out}
    big_out = {}
    after = grad_x
    for group, names in (("od_out", ["od_w_out"]), ("od_qkv", ["od_w_qb", "od_w_kvb"]), ("od_in", ["od_w_in"]),
                         ("ev_out", ["ev_w_out"])):
        parts = _exchange_wait("summed_" + group, scatters[group], after)
        for nm, p in zip(names, parts):
            w, m, v = big_w[nm][0], big_m[nm][0], big_v[nm][0]
            if nm == "od_w_in":
                res_t = _adamw("adamw_" + nm, [(p, N_DEV)], w.T, m.T, v.T, w.shape[1], 512)
                big_out[nm] = [r.T[None] for r in res_t]
            else:
                big_out[nm] = [r[None] for r in _adamw("adamw_" + nm, [(p, N_DEV)], w, m, v, w.shape[0] // 8)]
            after = big_out[nm][0]

    late_all = _exchange("gather_small_late", [dnpre0], gather=True, deps=[after])[0]
    early_all = _exchange_wait("arrived_small_early", scatters["small"], late_all)

    small_w = (norm_pre, norm_post, ev_lb_logits, ev_a_onorm, ev_b_ln_w, ev_b_ln_b, ev_b_ws, ev_b_bias)
    small_m = (m_norm_pre, m_norm_post, m_ev_lb_logits, m_ev_a_onorm, m_ev_b_ln_w, m_ev_b_ln_b, m_ev_b_ws, m_ev_b_bias)
    small_v = (v_norm_pre, v_norm_post, v_ev_lb_logits, v_ev_a_onorm, v_ev_b_ln_w, v_ev_b_ln_b, v_ev_b_ws, v_ev_b_bias)
    wmv = [tuple(a.reshape(s) for a in t) for s, t in zip(SMALL_PARAM_SHAPES, zip(small_w, small_m, small_v))]
    small_res, loss_row, g_norm_rows = _adamw_small(late_all, early_all, wmv)
    small_out = [[r.reshape(w.shape) for r in four] for four, w in zip(small_res, small_w)]
    loss = loss_row[0, 0]

    g_norms = jnp.concatenate([lax.dynamic_slice(g_norm_rows, (0, 64 * me), (1, 64)),
                               lax.dynamic_slice(g_norm_rows, (1, 64 * me), (1, 64))], axis=1)
    res_n = _adamw("adamw_norms", [(g_norms[None], 1)],
                   jnp.concatenate([od_q_norm, od_kv_norm], axis=1),
                   jnp.concatenate([m_od_q_norm, m_od_kv_norm], axis=1),
                   jnp.concatenate([v_od_q_norm, v_od_kv_norm], axis=1), 1)
    qn_out = [r[:, :64] for r in res_n]
    kvn_out = [r[:, 64:] for r in res_n]

    chip_sums, from_peers = _split_done("summed_ev_in", scatters["ev_in"], loss_row, all_bufs=True)
    own_chip = lax.dynamic_slice_in_dim(chip_sums, me >> 1, 1, 0)
    w = ev_w_in[0]
    big_out["ev_w_in"] = [r[None] for r in _adamw("adamw_ev_w_in", [(own_chip, 1), (from_peers, 3)], w, m_ev_w_in[0],
                                                  v_ev_w_in[0], w.shape[0] // 8)]

    order = ("norm_pre", "norm_post", "ev_w_in", "ev_lb_logits", "ev_a_onorm", "ev_b_ln_w", "ev_b_ln_b",
             "ev_b_ws", "ev_b_bias", "ev_w_out", "od_w_in", "od_q_norm", "od_w_qb", "od_kv_norm",
             "od_w_kvb", "od_w_out")
    small_names = ("norm_pre", "norm_post", "ev_lb_logits", "ev_a_onorm", "ev_b_ln_w", "ev_b_ln_b",
                   "ev_b_ws", "ev_b_bias")
    outs = [loss, grad_x[None]]
    for kind in range(4):
        for nm in order:
            if nm in big_out:
                outs.append(big_out[nm][kind])
            elif nm == "od_q_norm":
                outs.append(qn_out[kind])
            elif nm == "od_kv_norm":
                outs.append(kvn_out[kind])
            else:
                outs.append(small_out[small_names.index(nm)][kind])
    return tuple(outs)
```

```python
import functools

import jax
import jax.numpy as jnp
from jax import lax
from jax.experimental import pallas as pl
from jax.experimental.pallas import tpu as pltpu

F32 = jnp.float32
BF16 = jnp.bfloat16

N_DEV = 8
T = 2048
D = 2048
EPS = 1e-6
A_HEADS = 8
HD = 128
A_CHUNK = 64
A_SUB = 16
B_GROUPS = 8
B_CHUNK = 128
EVEN_IN = 7168
C_HEADS = 16
C_RANK = 512
C_NOPE = 128
C_ROPE = 64
C_QK = C_NOPE + C_ROPE
C_V = 128
ODD_IN = 3136
ODD_IN_PAD = 3200
QP = 256
ROPE_THETA = 10000.0
ATT_SCALE = C_QK ** -0.5

ADAM_LR = 0.001
ADAM_B1 = 0.9
ADAM_B2 = 0.999
ADAM_EPS = 1e-08
ADAM_WD = 0.01
ADAM_STEP = 10

VMEM_LIMIT_V7X = 56 * 1024 * 1024
MESH_ID = pl.DeviceIdType.MESH


def _params(n_grid):
    return pltpu.CompilerParams(dimension_semantics=("arbitrary",) * n_grid,
                                vmem_limit_bytes=VMEM_LIMIT_V7X)


def _dg(a, b, ca, cb):
    return lax.dot_general(a.astype(BF16), b.astype(BF16), (((ca,), (cb,)), ((), ())),
                           preferred_element_type=F32)


def _raw_nn(a, b):
    return _dg(a, b, 1, 0)


def _raw_nt(a, b):
    return _dg(a, b, 1, 1)


def _raw_tn(a, b):
    return _dg(a, b, 0, 0)


@jax.custom_vjp
def _dot_nn(a, b):
    return _raw_nn(a, b)


def _dot_nn_fwd(a, b):
    return _raw_nn(a, b), (a.astype(BF16), b.astype(BF16))


def _dot_nn_bwd(res, g):
    a, b = res
    return _raw_nt(g, b), _raw_tn(a, g)


_dot_nn.defvjp(_dot_nn_fwd, _dot_nn_bwd)


@jax.custom_vjp
def _dot_nt(a, b):
    return _raw_nt(a, b)


def _dot_nt_fwd(a, b):
    return _raw_nt(a, b), (a.astype(BF16), b.astype(BF16))


def _dot_nt_bwd(res, g):
    a, b = res
    return _raw_nn(g, b), _raw_tn(g, a)


_dot_nt.defvjp(_dot_nt_fwd, _dot_nt_bwd)


@jax.custom_vjp
def _dot_tn(a, b):
    return _raw_tn(a, b)


def _dot_tn_fwd(a, b):
    return _raw_tn(a, b), (a.astype(BF16), b.astype(BF16))


def _dot_tn_bwd(res, g):
    a, b = res
    return _raw_nt(b, g), _raw_nn(a, g)


_dot_tn.defvjp(_dot_tn_fwd, _dot_tn_bwd)


@jax.custom_vjp
def _sigmoid(x):
    e = jnp.exp(-jnp.abs(x))
    return jnp.where(x >= 0, 1.0 / (1.0 + e), e / (1.0 + e))


def _sigmoid_fwd(x):
    s = _sigmoid(x)
    return s, s


def _sigmoid_bwd(s, g):
    return (g * s * (1.0 - s),)


_sigmoid.defvjp(_sigmoid_fwd, _sigmoid_bwd)


def _silu(x):
    return x * _sigmoid(x)


def _rms(x, w):
    return x * lax.rsqrt(jnp.mean(x * x, axis=-1, keepdims=True) + EPS) * w


def _split3(x):
    hi = x.astype(BF16)
    r = x - hi.astype(F32)
    mid = r.astype(BF16)
    lo = (r - mid.astype(F32)).astype(BF16)
    return hi, mid, lo


def _mask_apply(mask_bf16, x, contract):
    out = None
    for piece in _split3(x):
        d = lax.dot_general(mask_bf16, piece, (((contract,), (0,)), ((), ())),
                            preferred_element_type=F32)
        out = d if out is None else out + d
    return out


def _chunk_tri(rows):
    r = lax.broadcasted_iota(jnp.int32, (rows, rows), 0)
    c = lax.broadcasted_iota(jnp.int32, (rows, rows), 1)
    return ((r >= c) & (r // A_CHUNK == c // A_CHUNK)).astype(BF16)


@jax.custom_vjp
def _chunk_cumsum(x):
    return _mask_apply(_chunk_tri(x.shape[0]), x, 1)


def _chunk_cumsum_fwd(x):
    return _chunk_cumsum(x), None


def _chunk_cumsum_bwd(_, g):
    return (_mask_apply(_chunk_tri(g.shape[0]), g, 0),)


_chunk_cumsum.defvjp(_chunk_cumsum_fwd, _chunk_cumsum_bwd)


def _hgrn2_rows(q, zf, v, ga, st, l0, l1, onorm):
    rows = q.shape[0]
    n_sub = A_CHUNK // A_SUB
    mx = jnp.maximum(l0, l1)
    e0 = jnp.exp(l0 - mx)
    e1 = jnp.exp(l1 - mx)
    lb = e0 / (e0 + e1)
    lf = jnp.log(lb + (1.0 - lb) * _sigmoid(zf))
    k = (1.0 - lb) * _sigmoid(-zf)
    b = _chunk_cumsum(lf)

    t_idx = lax.broadcasted_iota(jnp.int32, (A_CHUNK, n_sub * A_CHUNK), 0)
    c_idx = lax.broadcasted_iota(jnp.int32, (A_CHUNK, n_sub * A_CHUNK), 1)
    sel = (c_idx // A_CHUNK == t_idx // A_SUB) & (c_idx % A_CHUNK <= t_idx)
    key_row = lax.broadcasted_iota(jnp.int32, (A_CHUNK, HD), 0)

    outs = []
    for n in range(rows // A_CHUNK):
        lo = n * A_CHUNK
        qc, kc, vc = q[lo:lo + A_CHUNK], k[lo:lo + A_CHUNK], v[lo:lo + A_CHUNK]
        lfc, bc = lf[lo:lo + A_CHUNK], b[lo:lo + A_CHUNK]
        b_last = bc[A_CHUNK - 1:A_CHUNK]
        o_inter = _dot_nt(qc * jnp.exp(bc), st)
        kv_t = _dot_tn(vc, kc * jnp.exp(b_last - bc))
        st = st * jnp.exp(b_last) + kv_t
        g_rows, k_subs = [], []
        for i in range(n_sub):
            g_i = bc[i * A_SUB:i * A_SUB + 1] - lfc[i * A_SUB:i * A_SUB + 1]
            g_rows.append(jnp.broadcast_to(g_i, (A_SUB, HD)))
            expo = jnp.where(key_row < (i + 1) * A_SUB, g_i - bc, -jnp.inf)
            k_subs.append(kc * jnp.exp(expo))
        q_sub = qc * jnp.exp(bc - jnp.concatenate(g_rows, axis=0))
        scores = _dot_nt(q_sub, jnp.concatenate(k_subs, axis=0))
        scores = jnp.where(sel, scores, 0.0)
        o_intra = _dot_nn(scores, jnp.concatenate([vc] * n_sub, axis=0))
        outs.append(o_inter + o_intra)
    o = jnp.concatenate(outs, axis=0)
    return _rms(o, onorm) * _silu(ga), st


def _gmlp_rows(u, vb, gb, lnw, lnb, ws, bias):
    rows = u.shape[0]
    mu = jnp.mean(vb, axis=-1, keepdims=True)
    xc = vb - mu
    vg = xc * lax.rsqrt(jnp.mean(xc * xc, axis=-1, keepdims=True) + EPS) * lnw + lnb
    r = lax.broadcasted_iota(jnp.int32, (B_CHUNK, B_CHUNK), 0)
    c = lax.broadcasted_iota(jnp.int32, (B_CHUNK, B_CHUNK), 1)
    ws_causal = jnp.where(r >= c, ws, 0.0)
    svs = [_dot_nn(ws_causal, vg[n * B_CHUNK:(n + 1) * B_CHUNK]) + bias
           for n in range(rows // B_CHUNK)]
    return u * jnp.concatenate(svs, axis=0) * _silu(gb)


def _rope(x, cos_t, sin_t):
    return x * cos_t + pltpu.roll(x, 64, 1) * sin_t


def _rope_transpose(g, cos_t, sin_t):
    return g * cos_t + pltpu.roll(g * sin_t, 64, 1)


ANY_SPEC = pl.BlockSpec(memory_space=pl.ANY)


def _live(deps):
    return [d for d in deps if d is not None]


def _skip_deps(body, n_in, n_deps):
    def wrapped(*refs):
        return body(*refs[:n_in], *refs[n_in + n_deps:])
    return wrapped


def _pure_call(name, fn, grid, in_specs, out_specs, out_shape, args, n_acc=0, deps=()):
    deps = _live(deps)
    n_in, n_out, n_deps = len(in_specs), len(out_specs), len(deps)
    in_specs = list(in_specs) + [ANY_SPEC] * n_deps
    args = tuple(args) + tuple(deps)

    def body(*refs):
        res = fn(*[r[...] for r in refs[:n_in]])
        if not isinstance(res, (tuple, list)):
            res = (res,)
        outs = refs[n_in + n_deps:n_in + n_deps + n_out]
        for o, r in zip(outs[:n_out - n_acc], res[:n_out - n_acc]):
            o[...] = r.astype(o.dtype)
        if n_acc:
            first = functools.reduce(jnp.logical_and, [pl.program_id(i) == 0 for i in range(len(grid))])
            for o, r in zip(outs[n_out - n_acc:], res[n_out - n_acc:]):
                @pl.when(first)
                def _(o=o, r=r):
                    o[...] = r.astype(o.dtype)

                @pl.when(jnp.logical_not(first))
                def _(o=o, r=r):
                    o[...] += r.astype(o.dtype)

    return pl.pallas_call(body, name=name, grid=grid, in_specs=in_specs, out_specs=out_specs,
                          out_shape=out_shape, compiler_params=_params(len(grid)))(*args)


def _sds(shape, dtype):
    return jax.ShapeDtypeStruct(shape, dtype)


def _row_spec(tm, width, col=0):
    return pl.BlockSpec((tm, width), lambda i, col=col: (i, col))


def _full_spec(shape):
    nd = len(shape)
    return pl.BlockSpec(shape, lambda *_: (0,) * nd)


def _mm_nn(name, a, b, out_dtype, tm, tn, deps=()):
    deps = _live(deps)
    m, k = a.shape
    j, _, n = b.shape
    per = n // tn

    def body(a_ref, b_ref, o_ref):
        o_ref[...] = _raw_nn(a_ref[...], b_ref[...]).astype(o_ref.dtype)

    return pl.pallas_call(
        _skip_deps(body, 2, len(deps)), name=name, grid=(m // tm, j * per),
        in_specs=[pl.BlockSpec((tm, k), lambda i, c: (i, 0)),
                  pl.BlockSpec((None, k, tn), lambda i, c: (c // per, 0, c % per))] + [ANY_SPEC] * len(deps),
        out_specs=pl.BlockSpec((tm, tn), lambda i, c: (i, c)),
        out_shape=_sds((m, j * n), out_dtype), compiler_params=_params(2))(a, b, *deps)


def _mm_nt(name, a, b, out_dtype, tm, tn, deps=()):
    deps = _live(deps)
    m = a.shape[0]
    j, nn, n = b.shape

    def body(a_ref, b_ref, o_ref):
        b_all = b_ref[0] if j == 1 else jnp.concatenate([b_ref[s] for s in range(j)], axis=1)
        o_ref[...] = _raw_nt(a_ref[...], b_all).astype(o_ref.dtype)

    return pl.pallas_call(
        _skip_deps(body, 2, len(deps)), name=name, grid=(m // tm, nn // tn),
        in_specs=[pl.BlockSpec((tm, j * n), lambda i, c: (i, 0)),
                  pl.BlockSpec((j, tn, n), lambda i, c: (0, c, 0))] + [ANY_SPEC] * len(deps),
        out_specs=pl.BlockSpec((tm, tn), lambda i, c: (i, c)),
        out_shape=_sds((m, nn), out_dtype), compiler_params=_params(2))(a, b, *deps)


def _mm_tn(name, a, b, j, out_dtype, tm, tn, deps=()):
    deps = _live(deps)
    k, m = a.shape
    n = b.shape[1] // j
    per = n // tn

    def body(a_ref, b_ref, o_ref):
        o_ref[...] = _raw_tn(a_ref[...], b_ref[...]).astype(o_ref.dtype)

    return pl.pallas_call(
        _skip_deps(body, 2, len(deps)), name=name, grid=(m // tm, j * per),
        in_specs=[pl.BlockSpec((k, tm), lambda i, c: (0, i)),
                  pl.BlockSpec((k, tn), lambda i, c: (0, c))] + [ANY_SPEC] * len(deps),
        out_specs=pl.BlockSpec((None, tm, tn), lambda i, c: (c // per, i, c % per)),
        out_shape=_sds((j, m, n), out_dtype), compiler_params=_params(2))(a, b, *deps)


def _mm_tn_parity(name, a, b, j, parity, out_dtype, tm, deps=()):
    deps = _live(deps)
    k, m = a.shape
    n = b.shape[1] // j

    def body(par_ref, a_ref, b_ref, o_ref):
        del par_ref
        o_ref[...] = _raw_tn(a_ref[...], b_ref[...]).astype(o_ref.dtype)

    grid_spec = pltpu.PrefetchScalarGridSpec(
        num_scalar_prefetch=1, grid=(m // tm, j // 2),
        in_specs=[pl.BlockSpec((k, tm), lambda i, s, par: (0, i)),
                  pl.BlockSpec((k, n), lambda i, s, par: (0, 2 * s + par[0]))] + [ANY_SPEC] * len(deps),
        out_specs=pl.BlockSpec((None, tm, n), lambda i, s, par: (s, i, 0)))
    return pl.pallas_call(
        lambda par_ref, *refs: _skip_deps(functools.partial(body, par_ref), 2, len(deps))(*refs),
        name=name, grid_spec=grid_spec, out_shape=_sds((j // 2, m, n), out_dtype),
        compiler_params=_params(2))(parity, a, b, *deps)


TM = 256


def _pre_norm(name, x, w_row, deps=()):
    def fn(xv, w):
        return _rms(xv, w)
    return _pure_call(name, fn, (T // TM,), [_row_spec(TM, D), _full_spec((1, D))],
                      [_row_spec(TM, D)], [_sds((T, D), BF16)], (x, w_row), deps=deps)[0]


def _post_pre_norm(x, y, w_post, w_pre):
    def fn(xv, yv, wp, wn):
        x1 = xv + _rms(yv, wp)
        return x1, _rms(x1, wn)
    return _pure_call("post_pre_norm", fn, (T // TM,),
                      [_row_spec(TM, D), _row_spec(TM, D), _full_spec((1, D)), _full_spec((1, D))],
                      [_row_spec(TM, D), _row_spec(TM, D)],
                      [_sds((T, D), F32), _sds((T, D), BF16)], (x, y, w_post, w_pre))


def _in_proj_norms_bwd(dz, w_t, y, x1, w_post, w_pre, dx1_in, deps=()):
    def fn(dzv, w, yv, x1v, wp, wn, dx1v):
        _, vjp_pre = jax.vjp(_rms, x1v, wn)
        dx1_h, dwn = vjp_pre(_raw_nn(dzv, w[0]))
        dx1 = dx1v + dx1_h
        _, vjp_post = jax.vjp(_rms, yv, wp)
        dy, dwp = vjp_post(dx1)
        return dx1, dy, dwp, dwn
    return _pure_call("od_in_dx_norms", fn, (T // TM,),
                      [_row_spec(TM, ODD_IN_PAD), _full_spec((1, ODD_IN_PAD, D)), _row_spec(TM, D), _row_spec(TM, D),
                       _full_spec((1, D)), _full_spec((1, D)), _row_spec(TM, D)],
                      [_row_spec(TM, D), _row_spec(TM, D), _full_spec((1, D)), _full_spec((1, D))],
                      [_sds((T, D), F32), _sds((T, D), BF16), _sds((1, D), F32), _sds((1, D), F32)],
                      (dz, w_t, y, x1, w_post, w_pre, dx1_in), n_acc=2, deps=deps)


def _out_proj_loss(og, w_out, x1, w_post, target):
    tm = 512

    def fn(ogv, w, x1v, wp, tv):
        r, vjp = jax.vjp(_rms, _raw_nn(ogv, w[0]), wp)
        err = x1v + r - tv
        part = 0.5 * jnp.sum(jnp.mean(err * err, axis=-1, keepdims=True), axis=0, keepdims=True)
        dx2 = err * (1.0 / D)
        dy, dwp = vjp(dx2)
        return dx2, dy, jnp.broadcast_to(part, (1, 128)), dwp
    return _pure_call("od_out_loss", fn, (T // tm,),
                      [_row_spec(tm, D), _full_spec((1, D, D)), _row_spec(tm, D), _full_spec((1, D)), _row_spec(tm, D)],
                      [_row_spec(tm, D), _row_spec(tm, D), _full_spec((1, 128)), _full_spec((1, D))],
                      [_sds((T, D), F32), _sds((T, D), BF16), _sds((1, 128), F32), _sds((1, D), F32)],
                      (og, w_out, x1, w_post, target), n_acc=2)


def _pre_norm_bwd(x, w_row, dh, dx_res, deps=()):
    def fn(xv, w, dhv, dxv):
        _, vjp = jax.vjp(_rms, xv, w)
        dx, dw = vjp(dhv)
        return dxv + dx, dw
    return _pure_call("pre_norm_bwd", fn, (T // TM,),
                      [_row_spec(TM, D), _full_spec((1, D)), _row_spec(TM, D), _row_spec(TM, D)],
                      [_row_spec(TM, D), _full_spec((1, D))],
                      [_sds((T, D), F32), _sds((1, D), F32)], (x, w_row, dh, dx_res), n_acc=1, deps=deps)


RA = 256


def _head(ref, hh):
    return ref[:, hh * HD:(hh + 1) * HD]


def _z_part(z_ref, k, h):
    lo = (k * A_HEADS + h) * HD
    return z_ref[:, lo:lo + HD]


def _even_specs():
    return [_full_spec((1, A_HEADS * HD)), _full_spec((1, A_HEADS * HD)), _full_spec((1, HD)),
            _full_spec((1, B_GROUPS * HD)), _full_spec((1, B_GROUPS * HD)),
            _full_spec((B_GROUPS, B_CHUNK, B_CHUNK)), _full_spec((B_GROUPS, B_CHUNK, 1))]


def _even_fwd(z, l0, l1, onorm, lnw, lnb, ws, bias):
    nb = T // RA

    def body(z_ref, l0_ref, l1_ref, on_ref, lnw_ref, lnb_ref, ws_ref, bias_ref, cat_ref, sst_ref, st_scr):
        @pl.when(pl.program_id(0) == 0)
        def _():
            st_scr[...] = jnp.zeros_like(st_scr)

        for hh in range(A_HEADS):
            st = st_scr[hh]
            sst_ref[hh] = st
            out, st_new = _hgrn2_rows(_z_part(z_ref, 0, hh), _z_part(z_ref, 1, hh), _z_part(z_ref, 2, hh),
                                      _z_part(z_ref, 3, hh), st, _head(l0_ref, hh), _head(l1_ref, hh), on_ref[...])
            cat_ref[:, hh * HD:(hh + 1) * HD] = out.astype(cat_ref.dtype)
            st_scr[hh] = st_new
        for gg in range(B_GROUPS):
            out = _gmlp_rows(_z_part(z_ref, 4, gg), _z_part(z_ref, 5, gg), _z_part(z_ref, 6, gg),
                             _head(lnw_ref, gg), _head(lnb_ref, gg), ws_ref[gg], bias_ref[gg])
            cat_ref[:, (A_HEADS + gg) * HD:(A_HEADS + gg + 1) * HD] = out.astype(cat_ref.dtype)

    return pl.pallas_call(
        body, name="even_mixers_fwd", grid=(nb,),
        in_specs=[pl.BlockSpec((RA, EVEN_IN), lambda r: (r, 0))] + _even_specs(),
        out_specs=[pl.BlockSpec((RA, 2 * A_HEADS * HD), lambda r: (r, 0)),
                   pl.BlockSpec((A_HEADS, None, HD, HD), lambda r: (0, r, 0, 0))],
        out_shape=[_sds((T, 2 * A_HEADS * HD), BF16), _sds((A_HEADS, nb, HD, HD), F32)],
        scratch_shapes=[pltpu.VMEM((A_HEADS, HD, HD), F32)],
        compiler_params=_params(1))(z, l0, l1, onorm, lnw, lnb, ws, bias)


def _even_bwd(z, l0, l1, onorm, lnw, lnb, ws, bias, sst, dy, w_out, deps=()):
    nb = T // RA
    deps = _live(deps)

    def body(z_ref, l0_ref, l1_ref, on_ref, lnw_ref, lnb_ref, ws_ref, bias_ref, sst_ref, dy_ref, w_ref,
             dz_ref, dl0_ref, dl1_ref, don_ref, dlnw_ref, dlnb_ref, dws_ref, dbias_ref, ds_scr):
        first = pl.program_id(0) == 0

        @pl.when(first)
        def _():
            ds_scr[...] = jnp.zeros_like(ds_scr)

        dcat = _raw_nt(dy_ref[...], w_ref[0])

        def put(k, h, val):
            lo = (k * A_HEADS + h) * HD
            dz_ref[:, lo:lo + HD] = val.astype(dz_ref.dtype)

        sums = []
        don = None
        for hh in range(A_HEADS):
            lanes = slice(hh * HD, (hh + 1) * HD)
            _, vjp = jax.vjp(_hgrn2_rows, _z_part(z_ref, 0, hh), _z_part(z_ref, 1, hh), _z_part(z_ref, 2, hh),
                             _z_part(z_ref, 3, hh), sst_ref[hh], _head(l0_ref, hh), _head(l1_ref, hh), on_ref[...])
            dq, dzf, dv, dga, dst, dl0, dl1, don_h = vjp((dcat[:, lanes], ds_scr[hh]))
            for k, val in enumerate((dq, dzf, dv, dga)):
                put(k, hh, val)
            ds_scr[hh] = dst
            sums += [(dl0_ref, (slice(None), lanes), dl0), (dl1_ref, (slice(None), lanes), dl1)]
            don = don_h if don is None else don + don_h
        sums.append((don_ref, slice(None), don))
        for gg in range(B_GROUPS):
            lanes = slice(gg * HD, (gg + 1) * HD)
            _, vjp = jax.vjp(_gmlp_rows, _z_part(z_ref, 4, gg), _z_part(z_ref, 5, gg), _z_part(z_ref, 6, gg),
                             _head(lnw_ref, gg), _head(lnb_ref, gg), ws_ref[gg], bias_ref[gg])
            du, dv, dg, dlnw, dlnb, dws, dbias = vjp(dcat[:, (A_HEADS + gg) * HD:(A_HEADS + gg + 1) * HD])
            for k, val in enumerate((du, dv, dg)):
                put(4 + k, gg, val)
            sums += [(dlnw_ref, (slice(None), lanes), dlnw), (dlnb_ref, (slice(None), lanes), dlnb),
                     (dws_ref, gg, dws), (dbias_ref, gg, dbias)]
        for ref, idx, val in sums:
            @pl.when(first)
            def _(ref=ref, idx=idx, val=val):
                ref[idx] = val

            @pl.when(jnp.logical_not(first))
            def _(ref=ref, idx=idx, val=val):
                ref[idx] += val

    small = _even_specs()
    return pl.pallas_call(
        _skip_deps(body, 11, len(deps)), name="even_mixers_bwd", grid=(nb,),
        in_specs=[pl.BlockSpec((RA, EVEN_IN), lambda r: (nb - 1 - r, 0))] + small
        + [pl.BlockSpec((A_HEADS, None, HD, HD), lambda r: (0, nb - 1 - r, 0, 0)),
           pl.BlockSpec((RA, D), lambda r: (nb - 1 - r, 0)), _full_spec((1, 2 * A_HEADS * HD, D))]
        + [ANY_SPEC] * len(deps),
        out_specs=[pl.BlockSpec((RA, EVEN_IN), lambda r: (nb - 1 - r, 0))] + small,
        out_shape=[_sds((T, EVEN_IN), BF16), _sds((1, A_HEADS * HD), F32), _sds((1, A_HEADS * HD), F32),
                   _sds((1, HD), F32), _sds((1, B_GROUPS * HD), F32), _sds((1, B_GROUPS * HD), F32),
                   _sds((B_GROUPS, B_CHUNK, B_CHUNK), F32), _sds((B_GROUPS, B_CHUNK, 1), F32)],
        scratch_shapes=[pltpu.VMEM((A_HEADS, HD, HD), F32)],
        compiler_params=_params(1))(z, l0, l1, onorm, lnw, lnb, ws, bias, sst, dy, w_out, *deps)


def _mla_pre(z1, qn, kvn, cos_t, sin_t):
    def fn(cq, ckv, kpe, cs, sn, wq, wkv):
        return _rms(cq, wq), _rms(ckv, wkv), _rope(kpe, cs, sn)
    return _pure_call("mla_pre", fn, (T // TM,),
                      [_row_spec(TM, C_RANK, 4), _row_spec(TM, C_RANK, 5), _row_spec(TM, HD, 24),
                       _row_spec(TM, HD), _row_spec(TM, HD),
                       _full_spec((1, C_RANK)), _full_spec((1, C_RANK))],
                      [_row_spec(TM, C_RANK), _row_spec(TM, C_RANK), _row_spec(TM, HD)],
                      [_sds((T, C_RANK), BF16), _sds((T, C_RANK), BF16), _sds((T, HD), BF16)],
                      (z1, z1, z1, cos_t, sin_t, qn, kvn))


def _mla_pre_bwd(z1, qn, kvn, cos_t, sin_t, dq, w_qb, dkv, w_kvb, dkp, dgate, deps=()):
    def fn(cq, ckv, cs, sn, wq, wkv, dqv, wqb, dkvv, wkvb, g_kp, g_gate):
        _, vjp_q = jax.vjp(_rms, cq, wq)
        dcq, dwq = vjp_q(_raw_nt(dqv, wqb[0]))
        _, vjp_kv = jax.vjp(_rms, ckv, wkv)
        dckv, dwkv = vjp_kv(_raw_nt(dkvv, jnp.concatenate([wkvb[s] for s in range(N_DEV)], axis=1)))
        dz1 = jnp.concatenate([g_gate, dcq.astype(BF16), dckv.astype(BF16),
                               _rope_transpose(g_kp, cs, sn).astype(BF16)], axis=1)
        return dz1, dwq, dwkv
    return _pure_call("mla_pre_bwd", fn, (T // TM,),
                      [_row_spec(TM, C_RANK, 4), _row_spec(TM, C_RANK, 5),
                       _row_spec(TM, HD), _row_spec(TM, HD),
                       _full_spec((1, C_RANK)), _full_spec((1, C_RANK)),
                       _row_spec(TM, C_HEADS * QP), _full_spec((1, C_RANK, C_HEADS * QP)),
                       _row_spec(TM, C_HEADS * KVW), _full_spec((N_DEV, C_RANK, C_HEADS * KVW // N_DEV)),
                       _row_spec(TM, HD), _row_spec(TM, D)],
                      [_row_spec(TM, ODD_IN_PAD), _full_spec((1, C_RANK)), _full_spec((1, C_RANK))],
                      [_sds((T, ODD_IN_PAD), BF16), _sds((1, C_RANK), F32), _sds((1, C_RANK), F32)],
                      (z1, z1, cos_t, sin_t, qn, kvn, dq, w_qb, dkv, w_kvb, dkp, dgate), n_acc=2, deps=deps)


TQ = 256
HP = 2
KVW = C_NOPE + C_V


def _att_keys(kv_ref, kp_ref, k_scr):
    @pl.when(pl.program_id(1) == 0)
    def _():
        for hh in range(HP):
            k_scr[hh, :, 0:C_NOPE] = kv_ref[:, hh * KVW:hh * KVW + C_NOPE]
            k_scr[hh, :, C_NOPE:QP] = kp_ref[...]


def _att_scores(q, cos_ref, sin_ref, k_scr, hh, n):
    keys = (n + 1) * TQ
    qr = jnp.concatenate([q[:, :C_NOPE], _rope(q[:, C_NOPE:], cos_ref[...], sin_ref[...])], axis=1).astype(BF16)
    return qr, _raw_nt(qr, k_scr[hh, 0:keys, :]) * ATT_SCALE


def _causal(x, n, fill):
    row = lax.broadcasted_iota(jnp.int32, (TQ, TQ), 0)
    col = lax.broadcasted_iota(jnp.int32, (TQ, TQ), 1)
    diag = jnp.where(col <= row, x[:, n * TQ:], fill)
    return diag if n == 0 else jnp.concatenate([x[:, :n * TQ], diag], axis=1)


def _per_query_block(fn):
    for n in range(T // TQ):
        pl.when(pl.program_id(1) == n)(functools.partial(fn, n))


def _att_in_specs():
    return [pl.BlockSpec((TQ, HP * QP), lambda g, i: (i, g)),
            pl.BlockSpec((TQ, HD), lambda g, i: (i, 0)),
            pl.BlockSpec((TQ, HD), lambda g, i: (i, 0)),
            pl.BlockSpec((T, HP * KVW), lambda g, i: (0, g)),
            pl.BlockSpec((T, HD), lambda g, i: (0, 0))]


def _attention_fwd(q, cos_t, sin_t, kv, kp, z1):
    def body(q_ref, cos_ref, sin_ref, kv_ref, kp_ref, gate_ref, o_ref, lse_ref, og_ref, k_scr):
        _att_keys(kv_ref, kp_ref, k_scr)

        def block(n):
            keys = (n + 1) * TQ
            for hh in range(HP):
                _, s = _att_scores(q_ref[:, hh * QP:(hh + 1) * QP], cos_ref, sin_ref, k_scr, hh, n)
                s = _causal(s, n, jnp.finfo(F32).min)
                m = jnp.max(s, axis=-1, keepdims=True)
                p = jnp.exp(s - m)
                l = jnp.sum(p, axis=-1, keepdims=True)
                v = kv_ref[0:keys, hh * KVW + C_NOPE:(hh + 1) * KVW]
                o = _raw_nn(p, v) / l
                lanes = slice(hh * C_V, (hh + 1) * C_V)
                o_ref[:, lanes] = o
                og_ref[:, lanes] = (o * _silu(gate_ref[:, lanes])).astype(og_ref.dtype)
                lse_ref[hh] = m + jnp.log(l)

        _per_query_block(block)

    heads = pl.BlockSpec((TQ, HP * C_V), lambda g, i: (i, g))
    return pl.pallas_call(
        body, name="attention_fwd", grid=(C_HEADS // HP, T // TQ), in_specs=_att_in_specs() + [heads],
        out_specs=[heads, pl.BlockSpec((HP, TQ, 1), lambda g, i: (g, i, 0)), heads],
        out_shape=[_sds((T, C_HEADS * C_V), F32), _sds((C_HEADS, T, 1), F32), _sds((T, C_HEADS * C_V), BF16)],
        scratch_shapes=[pltpu.VMEM((HP, T, QP), BF16)],
        compiler_params=_params(2))(q, cos_t, sin_t, kv, kp, z1)


def _attention_bwd(q, cos_t, sin_t, kv, kp, o, lse, dog, z1):
    nq = T // TQ

    def body(q_ref, cos_ref, sin_ref, kv_ref, kp_ref, o_ref, lse_ref, dog_ref, gate_ref,
             dq_ref, dkv_ref, dkp_ref, dgate_ref, k_scr, dk_scr, dv_scr):
        g, i = pl.program_id(0), pl.program_id(1)
        _att_keys(kv_ref, kp_ref, k_scr)

        @pl.when(i == 0)
        def _():
            dv_scr[...] = jnp.zeros_like(dv_scr)
            dk_scr[...] = jnp.zeros_like(dk_scr)

        def block(n):
            keys = (n + 1) * TQ
            for hh in range(HP):
                qr, s = _att_scores(q_ref[:, hh * QP:(hh + 1) * QP], cos_ref, sin_ref, k_scr, hh, n)
                p = _causal(jnp.exp(s - lse_ref[hh]), n, 0.0)
                lanes = slice(hh * C_V, (hh + 1) * C_V)
                ov, gate, dogv = o_ref[:, lanes], gate_ref[:, lanes], dog_ref[:, lanes]
                sig = _sigmoid(gate)
                silu = gate * sig
                dov = dogv * silu
                dgate_ref[:, lanes] = (dogv * ov * (sig + silu * (1.0 - sig))).astype(dgate_ref.dtype)
                delta = jnp.sum(dov * ov, axis=-1, keepdims=True)
                dp = _raw_nt(dov, kv_ref[0:keys, hh * KVW + C_NOPE:(hh + 1) * KVW])
                ds = p * (dp - delta) * ATT_SCALE
                dq = _raw_nn(ds, k_scr[hh, 0:keys, :])
                dq_ref[:, hh * QP:(hh + 1) * QP] = jnp.concatenate(
                    [dq[:, :C_NOPE], _rope_transpose(dq[:, C_NOPE:], cos_ref[...], sin_ref[...])],
                    axis=1).astype(dq_ref.dtype)
                dv_scr[hh, 0:keys, :] += _raw_tn(p, dov)
                dk_scr[hh, 0:keys, :] += _raw_tn(ds, qr)

        _per_query_block(block)

        @pl.when(i == nq - 1)
        def _():
            for hh in range(HP):
                dkv_ref[:, hh * KVW:(hh + 1) * KVW] = jnp.concatenate(
                    [dk_scr[hh, :, 0:C_NOPE], dv_scr[hh]], axis=1).astype(dkv_ref.dtype)

        @pl.when(jnp.logical_and(i == nq - 1, g == 0))
        def _():
            dkp_ref[...] = dk_scr[0, :, C_NOPE:QP]

        @pl.when(jnp.logical_and(i == nq - 1, g > 0))
        def _():
            dkp_ref[...] += dk_scr[0, :, C_NOPE:QP]

        @pl.when(i == nq - 1)
        def _():
            for hh in range(1, HP):
                dkp_ref[...] += dk_scr[hh, :, C_NOPE:QP]

    heads = pl.BlockSpec((TQ, HP * C_V), lambda g, i: (i, g))
    return pl.pallas_call(
        body, name="attention_bwd", grid=(C_HEADS // HP, nq),
        in_specs=_att_in_specs() + [heads, pl.BlockSpec((HP, TQ, 1), lambda g, i: (g, i, 0)), heads, heads],
        out_specs=[pl.BlockSpec((TQ, HP * QP), lambda g, i: (i, g)),
                   pl.BlockSpec((T, HP * KVW), lambda g, i: (0, g)),
                   _full_spec((T, HD)), heads],
        out_shape=[_sds((T, C_HEADS * QP), BF16), _sds((T, C_HEADS * KVW), BF16), _sds((T, HD), F32),
                   _sds((T, C_HEADS * C_V), BF16)],
        scratch_shapes=[pltpu.VMEM((HP, T, QP), BF16), pltpu.VMEM((HP, T, QP), F32), pltpu.VMEM((HP, T, C_V), F32)],
        compiler_params=_params(2))(q, cos_t, sin_t, kv, kp, o, lse, dog, z1)


def _adamw_math(w, g, m, v):
    m = ADAM_B1 * m + (1.0 - ADAM_B1) * g
    v = ADAM_B2 * v + (1.0 - ADAM_B2) * (g * g)
    m_hat = m / (1.0 - ADAM_B1 ** ADAM_STEP)
    v_hat = v / (1.0 - ADAM_B2 ** ADAM_STEP)
    delta = -ADAM_LR * (m_hat / (jnp.sqrt(v_hat) + ADAM_EPS) + ADAM_WD * w)
    return delta, m, v


def _adamw(name, parts, w, m, v, tr, tc=None):
    rows, cols = w.shape

    def fn(*vals):
        pvs, (wv, mv, vv) = vals[:len(parts)], vals[len(parts):]
        g = None
        for pv in pvs:
            for d in range(pv.shape[0]):
                term = pv[d].astype(F32)
                g = term if g is None else g + term
        return (g,) + _adamw_math(wv, g, mv, vv)

    tc = cols if tc is None else tc
    blk = pl.BlockSpec((tr, tc), lambda i, j: (i, j))
    part_specs = [pl.BlockSpec((n, tr, tc), lambda i, j: (0, i, j)) for _, n in parts]
    return _pure_call(name, fn, (rows // tr, cols // tc), part_specs + [blk, blk, blk],
                      [blk] * 4, [_sds((rows, cols), F32)] * 4, tuple(p for p, _ in parts) + (w, m, v))


SMALL_PARAM_SHAPES = ((2, D), (2, D), (2, A_HEADS * HD), (1, HD), (1, B_GROUPS * HD), (1, B_GROUPS * HD),
                      (B_GROUPS, B_CHUNK, B_CHUNK), (B_GROUPS, B_CHUNK))
SMALL_PIECES = ((0, 0, 0, 0), (0, 1, 1, 0), (1, 0, 1, 1), (1, 1, 1, 2), (2, 0, 2, 0), (2, 1, 2, 1),
                (3, 0, 3, 8), (4, 0, 2, 2), (5, 0, 2, 3))


def _small_rows(dnpre1, dnpost0, dnpost1, dl0, dl1, donorm, dlnw, dlnb, dws, dbias, dqn, dkvn, loss_part):
    return [jnp.concatenate([dnpre1, dnpost0, dnpost1], axis=0),
            jnp.concatenate([dl0, dl1, dlnw, dlnb], axis=0),
            jnp.concatenate([dbias.reshape(B_GROUPS, B_CHUNK), donorm, loss_part], axis=0),
            dws,
            jnp.concatenate([dqn, dkvn], axis=0)]


def _adamw_small(late_all, early_all, wmv):
    n_in = 6 + 3 * len(wmv)

    def body(*refs):
        gathered, params, outs = refs[:6], refs[6:n_in], refs[n_in:]

        def total(ref):
            s = ref[0]
            for d in range(1, N_DEV):
                s = s + ref[d]
            return s

        g_late, g2048, g1024, g128, g_ws, g512 = [total(r) for r in gathered]
        arrays = (g_late, g2048, g1024, g128)

        def update(p, rows, g):
            w_ref, m_ref, v_ref = params[3 * p:3 * p + 3]
            delta, m, v = _adamw_math(w_ref[rows], g, m_ref[rows], v_ref[rows])
            for out, val in zip(outs[4 * p:4 * p + 4], (g, delta, m, v)):
                out[rows] = val

        for p, row, arr, arr_row in SMALL_PIECES:
            update(p, pl.ds(row, 1), arrays[arr][arr_row:arr_row + 1])
        update(6, slice(None), g_ws)
        update(7, slice(None), g128[0:B_GROUPS])
        outs[32][...] = g128[B_GROUPS + 1:B_GROUPS + 2]
        outs[33][...] = g512

    vmem = pl.BlockSpec(memory_space=pltpu.VMEM)
    flat = [a for t in wmv for a in t]
    out_shape = [_sds(s, F32) for s in SMALL_PARAM_SHAPES for _ in range(4)] + [_sds((1, 128), F32), _sds((2, C_RANK), F32)]
    res = pl.pallas_call(body, name="adamw_small", in_specs=[vmem] * n_in, out_specs=[vmem] * len(out_shape),
                         out_shape=out_shape,
                         compiler_params=pltpu.CompilerParams(vmem_limit_bytes=VMEM_LIMIT_V7X))(late_all, *early_all, *flat)
    return [res[4 * p:4 * p + 4] for p in range(8)], res[32], res[33]


def _exchange(name, arrs, gather, deps=()):
    n = len(arrs)
    deps = _live(deps)

    def body(*refs):
        ins, outs = refs[:n], refs[n + len(deps):2 * n + len(deps)]
        send_sems, recv_sems, local_sems = refs[2 * n + len(deps):]
        x, y, c = lax.axis_index("x"), lax.axis_index("y"), lax.axis_index("c")
        me = 4 * x + 2 * y + c

        def peer(k):
            return (x ^ (k >> 2), y ^ ((k >> 1) & 1), c ^ (k & 1))

        def copy(a, k):
            src = ins[a] if gather else ins[a].at[me ^ k]
            return pltpu.make_async_remote_copy(
                src_ref=src, dst_ref=outs[a].at[me], send_sem=send_sems.at[a, k - 1],
                recv_sem=recv_sems.at[a, k - 1], device_id=peer(k), device_id_type=MESH_ID)

        def arrival(a, k):
            src = ins[a] if gather else ins[a].at[me]
            return pltpu.make_async_remote_copy(
                src_ref=src, dst_ref=outs[a].at[me ^ k], send_sem=send_sems.at[a, k - 1],
                recv_sem=recv_sems.at[a, k - 1], device_id=peer(k), device_id_type=MESH_ID)

        own = [pltpu.make_async_copy(ins[a] if gather else ins[a].at[me], outs[a].at[me], local_sems.at[a])
               for a in range(n)]
        for cp in own:
            cp.start()
        for k in range(1, N_DEV):
            for a in range(n):
                copy(a, k).start()
        for k in range(1, N_DEV):
            for a in range(n):
                arrival(a, k).wait_recv()
        for k in range(1, N_DEV):
            for a in range(n):
                copy(a, k).wait_send()
        for cp in own:
            cp.wait()

    any_spec = pl.BlockSpec(memory_space=pl.ANY)
    out_shape = [_sds((N_DEV,) + a.shape if gather else a.shape, a.dtype) for a in arrs]
    return pl.pallas_call(
        body, name=name, in_specs=[any_spec] * (n + len(deps)), out_specs=[any_spec] * n, out_shape=out_shape,
        scratch_shapes=[pltpu.SemaphoreType.DMA((n, N_DEV - 1)), pltpu.SemaphoreType.DMA((n, N_DEV - 1)),
                        pltpu.SemaphoreType.DMA((n,))],
        compiler_params=pltpu.CompilerParams(has_side_effects=True))(*arrs, *deps)


HBM_SPEC = pl.BlockSpec(memory_space=pltpu.HBM)
SEM_SPEC = pl.BlockSpec(memory_space=pltpu.SEMAPHORE)
DATAFLOW = pltpu.SideEffectType.DATAFLOW_SIDE_EFFECTING


def _my_index():
    return 4 * lax.axis_index("x") + 2 * lax.axis_index("y") + lax.axis_index("c")


def _plan_copies(plan, refs, send_sems, recv_sems):
    x, y, c = lax.axis_index("x"), lax.axis_index("y"), lax.axis_index("c")
    return [pltpu.make_async_remote_copy(
        src_ref=src, dst_ref=dst, send_sem=send_sems.at[i], recv_sem=recv_sems.at[i],
        device_id=(x ^ (k >> 2), y ^ ((k >> 1) & 1), c ^ (k & 1)), device_id_type=MESH_ID)
        for i, (src, dst, k) in enumerate(plan(refs, 4 * x + 2 * y + c))]


def _split_call(name, bufs, waits=None, starts=None, deps=()):
    n = len(bufs)
    deps = _live(deps)
    n_wait = 2 if waits else 0

    def body(*refs):
        zones = refs[:n]
        if waits:
            for cp in _plan_copies(waits[2], zones, refs[n], refs[n + 1]):
                cp.wait_send()
                cp.wait_recv()
        if starts:
            first_out = n + n_wait + len(deps)
            for cp in _plan_copies(starts[0], zones, refs[first_out], refs[first_out + 1]):
                cp.start()
            refs[-1][...] = jnp.zeros_like(refs[-1])

    out_specs, out_shape = [], []
    if starts:
        sems = pltpu.SemaphoreType.DMA((starts[1],))
        out_specs, out_shape = [SEM_SPEC, SEM_SPEC], [sems, sems]
    out_specs += [HBM_SPEC] * n
    out_shape += [pltpu.HBM(b.shape, b.dtype) for b in bufs]
    if starts:
        out_specs.append(pl.BlockSpec(memory_space=pltpu.VMEM))
        out_shape.append(_sds((8, 128), F32))
    first_buf = 2 if starts else 0
    res = pl.pallas_call(
        body, name=name,
        in_specs=[HBM_SPEC] * n + [SEM_SPEC] * n_wait + [ANY_SPEC] * len(deps),
        out_specs=out_specs, out_shape=out_shape,
        input_output_aliases={i: first_buf + i for i in range(n)},
        compiler_params=pltpu.CompilerParams(has_side_effects=DATAFLOW),
    )(*[pltpu.with_memory_space_constraint(b, pltpu.HBM) for b in bufs], *(waits[:2] if waits else ()), *deps)
    out_bufs = list(res[first_buf:first_buf + n])
    return out_bufs, ((res[0], res[1]) if starts else None), (res[-1] if starts else None)


def _direct_plan(n, gather):
    def plan(refs, me):
        return [(refs[a] if gather else refs[a].at[me ^ k], refs[n + a].at[me], k)
                for k in range(1, N_DEV) for a in range(n)]
    return plan


def _own_slot_filled(a, gather):
    me = _my_index()
    if gather:
        return lax.dynamic_update_slice_in_dim(lax.empty((N_DEV,) + a.shape, a.dtype), a[None], me, 0)
    return lax.dynamic_update_slice_in_dim(lax.empty(a.shape, a.dtype), lax.dynamic_slice_in_dim(a, me, 1, 0), me, 0)


def _exchange_start(name, arrs, gather, deps=()):
    n = len(arrs)
    lands = [_own_slot_filled(a, gather) for a in arrs]
    plan = _direct_plan(n, gather)
    bufs, sems, token = _split_call(name, list(arrs) + lands, starts=(plan, n * (N_DEV - 1)), deps=deps)
    return (n, plan, sems, bufs, None), token


def _exchange_wait(name, handle, after):
    return _split_done(name, handle, after)


ICI_PEERS = (2, 4, 6)
SIBLING = 1


def _gather2_send(name, arrs, deps=()):
    n = len(arrs)
    lands = [_own_slot_filled(a, True) for a in arrs]

    def plan(refs, me_):
        return [(refs[a], refs[n + a].at[me_], k) for k in (SIBLING,) + ICI_PEERS for a in range(n)]

    bufs, sems, token = _split_call(name, list(arrs) + lands, starts=(plan, 4 * n), deps=deps)
    return (n, plan, sems, bufs, None), token


def _gather2_relay(name, handle, after):
    n, plan, sems, bufs, _ = handle
    after = after if isinstance(after, (list, tuple)) else [after]

    def relay(refs, me_):
        return [(refs[n + a].at[me_ ^ k], refs[n + a].at[me_ ^ k], SIBLING) for k in ICI_PEERS for a in range(n)]

    bufs, sems2, token = _split_call(name, bufs, waits=(sems[0], sems[1], plan), starts=(relay, 3 * n), deps=after)
    return (n, relay, sems2, bufs, None), token


def _split_done(name, handle, after, all_bufs=False):
    n, plan, sems, bufs, _ = handle
    bufs, _, _ = _split_call(name, bufs, waits=(sems[0], sems[1], plan), deps=[after])
    return bufs if all_bufs else bufs[n:]


def _scatter2_pair(name, for_sibling, deps=()):
    n = len(for_sibling)
    pairs = [lax.empty(s.shape, s.dtype) for s in for_sibling]

    def plan(refs, me):
        del me
        return [(refs[a].at[s], refs[n + a].at[s], SIBLING) for s in range(4) for a in range(n)]

    bufs, sems, token = _split_call(name, list(for_sibling) + pairs, starts=(plan, 4 * n), deps=deps)
    return (n, plan, sems, bufs, None), token


def _pair_add(name, mine, pair):
    _, rows, cols = mine.shape
    tr = rows // 2

    def fn(a, b):
        return a.astype(F32) + b.astype(F32)

    blk = pl.BlockSpec((None, tr, cols), lambda s, i: (s, i, 0))
    return _pure_call(name, fn, (4, rows // tr), [blk, blk], [blk], [_sds(mine.shape, mine.dtype)], (mine, pair))[0]


def _scatter2_send(name, chip_sums, deps=()):
    n = len(chip_sums)
    finals = [lax.empty((3,) + c.shape[1:], c.dtype) for c in chip_sums]

    def plan(refs, me):
        return [(refs[a].at[(me >> 1) ^ j], refs[n + a].at[j - 1], 2 * j) for j in range(1, 4) for a in range(n)]

    bufs, sems, token = _split_call(name, list(chip_sums) + finals, starts=(plan, 3 * n), deps=deps)
    return (n, plan, sems, bufs, None), token


def _pad_rope(p):
    z = jnp.zeros(p.shape[:-1] + (32,), p.dtype)
    return jnp.concatenate([p[..., :32], z, p[..., 32:], z], axis=-1)


def _unpad_rope(p):
    return jnp.concatenate([p[..., :32], p[..., 64:96]], axis=-1)


def _odd_in_layout(wt):
    wt = wt.reshape(ODD_IN, D)
    cq, ckv, kpe, gate = wt[:512], wt[512:1024], wt[1024:1088], wt[1088:]
    z = jnp.zeros((32, D), wt.dtype)
    return jnp.concatenate([gate, cq, ckv, kpe[:32], z, kpe[32:], z], axis=0)


def _odd_in_unlayout(dwt):
    gate, cq, ckv, kpe = dwt[:2048], dwt[2048:2560], dwt[2560:3072], dwt[3072:]
    wt = jnp.concatenate([cq, ckv, kpe[:32], kpe[64:96], gate], axis=0)
    return wt.reshape(N_DEV, ODD_IN // N_DEV, D)


def _qb_layout(w):
    w = w.transpose(1, 0, 2).reshape(C_RANK, C_HEADS, C_QK)
    w = jnp.concatenate([w[..., :C_NOPE], _pad_rope(w[..., C_NOPE:])], axis=-1)
    return w.reshape(C_RANK, C_HEADS * QP)


def _qb_unlayout(dw):
    dw = dw.reshape(C_RANK, C_HEADS, QP)
    dw = jnp.concatenate([dw[..., :C_NOPE], _unpad_rope(dw[..., C_NOPE:])], axis=-1)
    return dw.reshape(C_RANK, N_DEV, C_HEADS * C_QK // N_DEV).transpose(1, 0, 2)


def _rope_tables(positions):
    inv_freq = ROPE_THETA ** (-jnp.arange(0, C_ROPE, 2, dtype=F32) / C_ROPE)
    ang = positions.astype(F32)[0][:, None] * inv_freq
    cos, sin = jnp.cos(ang), jnp.sin(ang)
    z = jnp.zeros_like(cos)
    return jnp.concatenate([cos, z, cos, z], axis=1), jnp.concatenate([-sin, z, sin, z], axis=1)


def _forward_backward(x, cos_t, sin_t, target, norm_pre, norm_post, lb_logits, a_onorm, ln_w, ln_b,
                      b_ws, b_bias, get_w, put_g, put_small=None, start_dep=None):
    npre0, npre1 = norm_pre[0:1], norm_pre[1:2]
    npost0, npost1 = norm_post[0:1], norm_post[1:2]
    l0, l1 = lb_logits[0:1], lb_logits[1:2]
    bias_col = b_bias.reshape(B_GROUPS, B_CHUNK, 1)
    ws = b_ws.reshape(B_GROUPS, B_CHUNK, B_CHUNK)

    h0 = _pre_norm("pre_norm0", x, npre0, deps=[start_dep])
    w_ev_in = get_w("ev_in", h0)
    z0 = _mm_nn("ev_in", h0, w_ev_in, F32, 1024, 896)
    cat, sst = _even_fwd(z0, l0, l1, a_onorm, ln_w, ln_b, ws, bias_col)
    w_ev_out = get_w("ev_out", cat)
    y0 = _mm_nn("ev_out", cat, w_ev_out, F32, 1024, 1024)
    get_w("od_relay", y0)
    x1, h1 = _post_pre_norm(x, y0, npost0, npre1)
    w_od_in, w_qb, w_kvb, q_norm, kv_norm = get_w("od_mid", h1)
    z1 = _mm_nt("od_in", h1, w_od_in[None], F32, 1024, 640)
    cqn, ckvn, kp = _mla_pre(z1, q_norm, kv_norm, cos_t, sin_t)
    q = _mm_nn("od_qb", cqn, w_qb[None], F32, 1024, 1024)
    kv = _mm_nn("od_kvb", ckvn, w_kvb, BF16, 1024, 512)
    o, lse, og = _attention_fwd(q, cos_t, sin_t, kv, kp, z1)
    w_od_out = get_w("od_out", og)
    dx2, dy1, loss_part, dnpost1 = _out_proj_loss(og, w_od_out, x1, npost1, target)

    g_od_out = _mm_tn("od_out_dw", og, dy1, 1, BF16, 1024, 1024)
    tok = put_g("od_out", [g_od_out.reshape(N_DEV, D // N_DEV, D)])
    dog = _mm_nt("od_out_dx", dy1, w_od_out, F32, 1024, 1024, deps=[tok])
    dq, dkv, dkp, dgate = _attention_bwd(q, cos_t, sin_t, kv, kp, o, lse, dog, z1)
    g_qb = _mm_tn("od_qb_dw", cqn, dq, 1, F32, 512, 1024)
    g_kvb = _mm_tn("od_kvb_dw", ckvn, dkv, N_DEV, BF16, 512, 512)
    tok = put_g("od_qkv", [_qb_unlayout(g_qb[0]).astype(BF16), g_kvb])
    dz1, dqn, dkvn = _mla_pre_bwd(z1, q_norm, kv_norm, cos_t, sin_t, dq, w_qb[None], dkv, w_kvb, dkp, dgate,
                                  deps=[tok])
    g_od_in = _mm_tn("od_in_dw", dz1, h1, 1, F32, 640, 1024)
    tok = put_g("od_in", [_odd_in_unlayout(g_od_in[0]).astype(BF16)])
    dx1, dy0, dnpost0, dnpre1 = _in_proj_norms_bwd(dz1, w_od_in[None], y0, x1, npost0, npre1, dx2, deps=[tok])

    g_ev_out = _mm_tn("ev_out_dw", cat, dy0, 1, BF16, 1024, 1024)
    tok = put_g("ev_out", [g_ev_out.reshape(N_DEV, D // N_DEV, D)])
    dz0, dl0, dl1, donorm, dlnw, dlnb, dws, dbias = _even_bwd(z0, l0, l1, a_onorm, ln_w, ln_b, ws, bias_col, sst,
                                                              dy0, w_ev_out, deps=[tok])
    early = _small_rows(dnpre1, dnpost0, dnpost1, dl0, dl1, donorm, dlnw, dlnb, dws, dbias, dqn, dkvn, loss_part)
    tok = put_small(early) if put_small else None
    small_tok = tok

    def ev_in_half(name, parity, deps=()):
        return _mm_tn_parity(name, h0, dz0, N_DEV, parity, BF16, 1024, deps=[small_tok] + list(deps))

    tok = put_g("ev_in", ev_in_half)
    dh0 = _mm_nt("ev_in_dx", dz0, w_ev_in, F32, 1024, 256, deps=[tok])
    grad_x, dnpre0 = _pre_norm_bwd(x, npre0, dh0, dx1)
    return grad_x, early, dnpre0


def kernel(x, positions, norm_pre, norm_post, ev_w_in, ev_lb_logits, ev_a_onorm, ev_b_ln_w, ev_b_ln_b, ev_b_ws, ev_b_bias, ev_w_out, od_w_in, od_q_norm, od_w_qb, od_kv_norm, od_w_kvb, od_w_out, loss_target, m_norm_pre, m_norm_post, m_ev_w_in, m_ev_lb_logits, m_ev_a_onorm, m_ev_b_ln_w, m_ev_b_ln_b, m_ev_b_ws, m_ev_b_bias, m_ev_w_out, m_od_w_in, m_od_q_norm, m_od_w_qb, m_od_kv_norm, m_od_w_kvb, m_od_w_out, v_norm_pre, v_norm_post, v_ev_w_in, v_ev_lb_logits, v_ev_a_onorm, v_ev_b_ln_w, v_ev_b_ln_b, v_ev_b_ws, v_ev_b_bias, v_ev_w_out, v_od_w_in, v_od_q_norm, v_od_w_qb, v_od_kv_norm, v_od_w_kvb, v_od_w_out):
    me = 4 * lax.axis_index("x") + 2 * lax.axis_index("y") + lax.axis_index("c")
    bf = lambda w: w[0].astype(BF16)

    norms = jnp.pad(jnp.concatenate([od_q_norm, od_kv_norm], axis=1), ((0, 7), (0, 0)))
    sent = {}
    sent["ev_in"], tok = _gather2_send("gather_ev_in", [bf(ev_w_in)])
    sent["ev_out"], tok = _gather2_send("gather_ev_out", [bf(ev_w_out)], deps=[tok])
    sent["od"], tok = _gather2_send("gather_od", [od_w_in[0].T.astype(BF16), bf(od_w_qb), bf(od_w_kvb), norms,
                                                 bf(od_w_out)], deps=[tok])
    cos_t, sin_t = _rope_tables(positions)
    od = []

    def get_w(group, after):
        if group == "ev_in":
            relayed, token = _gather2_relay("relay_ev_in", sent["ev_in"], [after, cos_t, sin_t])
            return _split_done("arrived_ev_in", relayed, token)[0]
        if group == "ev_out":
            relayed, token = _gather2_relay("relay_ev_out", sent["ev_out"], after)
            return _split_done("arrived_ev_out", relayed, token)[0].reshape(1, D, D)
        if group == "od_relay":
            sent["od_relayed"], _ = _gather2_relay("relay_od", sent["od"], after)
            return None
        if not od:
            od.extend(_split_done("arrived_od", sent["od_relayed"], after))
        w_od_in, w_qb, w_kvb, norms_all, w_od_out = od
        if group == "od_out":
            return w_od_out.reshape(1, D, D)
        return (_odd_in_layout(w_od_in), _qb_layout(w_qb), w_kvb,
                norms_all[:, 0, :64].reshape(1, C_RANK), norms_all[:, 0, 64:].reshape(1, C_RANK))

    scatters = {}

    def put_g(group, grads):
        if group == "ev_in":
            core = lax.axis_index("c").astype(jnp.int32).reshape(1)
            paired, token = _scatter2_pair("pair_ev_in", [grads("ev_in_dw_sibling", 1 - core)])
            mine = grads("ev_in_dw_own", core, deps=[token])
            pair = _split_done("paired_ev_in", paired, mine)[0]
            scatters[group], token = _scatter2_send("scatter_ev_in", [_pair_add("pair_add_ev_in", mine, pair)])
        else:
            scatters[group], token = _exchange_start("scatter_" + group, grads, False)
        return token

    def put_small(early):
        scatters["small"], token = _exchange_start("gather_small_early", early, True)
        return token

    grad_x, _, dnpre0 = _forward_backward(
        x[0], cos_t, sin_t, loss_target[0], norm_pre, norm_post, ev_lb_logits, ev_a_onorm, ev_b_ln_w,
        ev_b_ln_b, ev_b_ws, ev_b_bias, get_w, put_g, put_small, start_dep=tok)

    big_w = {"ev_w_in": ev_w_in, "ev_w_out": ev_w_out, "od_w_in": od_w_in, "od_w_qb": od_w_qb,
             "od_w_kvb": od_w_kvb, "od_w_out": od_w_out}
    big_m = {"ev_w_in": m_ev_w_in, "ev_w_out": m_ev_w_out, "od_w_in": m_od_w_in, "od_w_qb": m_od_w_qb,
             "od_w_kvb": m_od_w_kvb, "od_w_out": m_od_w_out}
    big_v = {"ev_w_in": v_ev_w_in, "ev_w_out": v_ev_w_out, "od_w_in": v_od_w_in, "od_w_qb": v_od_w_qb,
             "od_w_kvb": v_od_w_kvb, "od_w_out": v_od_w_out}
    big_out = {}
    after = grad_x
    for group, names in (("od_out", ["od_w_out"]), ("od_qkv", ["od_w_qb", "od_w_kvb"]), ("od_in", ["od_w_in"]),
                         ("ev_out", ["ev_w_out"])):
        parts = _exchange_wait("summed_" + group, scatters[group], after)
        for nm, p in zip(names, parts):
            w, m, v = big_w[nm][0], big_m[nm][0], big_v[nm][0]
            if nm == "od_w_in":
                res_t = _adamw("adamw_" + nm, [(p, N_DEV)], w.T, m.T, v.T, w.shape[1], 512)
                big_out[nm] = [r.T[None] for r in res_t]
            else:
                big_out[nm] = [r[None] for r in _adamw("adamw_" + nm, [(p, N_DEV)], w, m, v, w.shape[0] // 8)]
            after = big_out[nm][0]

    late_all = _exchange("gather_small_late", [dnpre0], gather=True, deps=[after])[0]
    early_all = _exchange_wait("arrived_small_early", scatters["small"], late_all)

    small_w = (norm_pre, norm_post, ev_lb_logits, ev_a_onorm, ev_b_ln_w, ev_b_ln_b, ev_b_ws, ev_b_bias)
    small_m = (m_norm_pre, m_norm_post, m_ev_lb_logits, m_ev_a_onorm, m_ev_b_ln_w, m_ev_b_ln_b, m_ev_b_ws, m_ev_b_bias)
    small_v = (v_norm_pre, v_norm_post, v_ev_lb_logits, v_ev_a_onorm, v_ev_b_ln_w, v_ev_b_ln_b, v_ev_b_ws, v_ev_b_bias)
    wmv = [tuple(a.reshape(s) for a in t) for s, t in zip(SMALL_PARAM_SHAPES, zip(small_w, small_m, small_v))]
    small_res, loss_row, g_norm_rows = _adamw_small(late_all, early_all, wmv)
    small_out = [[r.reshape(w.shape) for r in four] for four, w in zip(small_res, small_w)]
    loss = loss_row[0, 0]

    g_norms = jnp.concatenate([lax.dynamic_slice(g_norm_rows, (0, 64 * me), (1, 64)),
                               lax.dynamic_slice(g_norm_rows, (1, 64 * me), (1, 64))], axis=1)
    res_n = _adamw("adamw_norms", [(g_norms[None], 1)],
                   jnp.concatenate([od_q_norm, od_kv_norm], axis=1),
                   jnp.concatenate([m_od_q_norm, m_od_kv_norm], axis=1),
                   jnp.concatenate([v_od_q_norm, v_od_kv_norm], axis=1), 1)
    qn_out = [r[:, :64] for r in res_n]
    kvn_out = [r[:, 64:] for r in res_n]

    chip_sums, from_peers = _split_done("summed_ev_in", scatters["ev_in"], loss_row, all_bufs=True)
    own_chip = lax.dynamic_slice_in_dim(chip_sums, me >> 1, 1, 0)
    w = ev_w_in[0]
    big_out["ev_w_in"] = [r[None] for r in _adamw("adamw_ev_w_in", [(own_chip, 1), (from_peers, 3)], w, m_ev_w_in[0],
                                                  v_ev_w_in[0], w.shape[0] // 8)]

    order = ("norm_pre", "norm_post", "ev_w_in", "ev_lb_logits", "ev_a_onorm", "ev_b_ln_w", "ev_b_ln_b",
             "ev_b_ws", "ev_b_bias", "ev_w_out", "od_w_in", "od_q_norm", "od_w_qb", "od_kv_norm",
             "od_w_kvb", "od_w_out")
    small_names = ("norm_pre", "norm_post", "ev_lb_logits", "ev_a_onorm", "ev_b_ln_w", "ev_b_ln_b",
                   "ev_b_ws", "ev_b_bias")
    outs = [loss, grad_x[None]]
    for kind in range(4):
        for nm in order:
            if nm in big_out:
                outs.append(big_out[nm][kind])
            elif nm == "od_q_norm":
                outs.append(qn_out[kind])
            elif nm == "od_kv_norm":
                outs.append(kvn_out[kind])
            else:
                outs.append(small_out[small_names.index(nm)][kind])
    return tuple(outs)
```

```python
import functools

import jax
import jax.numpy as jnp
from jax import lax
from jax.experimental import pallas as pl
from jax.experimental.pallas import tpu as pltpu

F32 = jnp.float32
BF16 = jnp.bfloat16

N_DEV = 8
T = 2048
D = 2048
EPS = 1e-6
A_HEADS = 8
HD = 128
A_CHUNK = 64
A_SUB = 16
B_GROUPS = 8
B_CHUNK = 128
EVEN_IN = 7168
C_HEADS = 16
C_RANK = 512
C_NOPE = 128
C_ROPE = 64
C_QK = C_NOPE + C_ROPE
C_V = 128
ODD_IN = 3136
ODD_IN_PAD = 3200
QP = 256
ROPE_THETA = 10000.0
ATT_SCALE = C_QK ** -0.5

ADAM_LR = 0.001
ADAM_B1 = 0.9
ADAM_B2 = 0.999
ADAM_EPS = 1e-08
ADAM_WD = 0.01
ADAM_STEP = 10

VMEM_LIMIT_V7X = 56 * 1024 * 1024
MESH_ID = pl.DeviceIdType.MESH


def _params(n_grid):
    return pltpu.CompilerParams(dimension_semantics=("arbitrary",) * n_grid,
                                vmem_limit_bytes=VMEM_LIMIT_V7X)


def _dg(a, b, ca, cb):
    return lax.dot_general(a.astype(BF16), b.astype(BF16), (((ca,), (cb,)), ((), ())),
                           preferred_element_type=F32)


def _raw_nn(a, b):
    return _dg(a, b, 1, 0)


def _raw_nt(a, b):
    return _dg(a, b, 1, 1)


def _raw_tn(a, b):
    return _dg(a, b, 0, 0)


@jax.custom_vjp
def _dot_nn(a, b):
    return _raw_nn(a, b)


def _dot_nn_fwd(a, b):
    return _raw_nn(a, b), (a.astype(BF16), b.astype(BF16))


def _dot_nn_bwd(res, g):
    a, b = res
    return _raw_nt(g, b), _raw_tn(a, g)


_dot_nn.defvjp(_dot_nn_fwd, _dot_nn_bwd)


@jax.custom_vjp
def _dot_nt(a, b):
    return _raw_nt(a, b)


def _dot_nt_fwd(a, b):
    return _raw_nt(a, b), (a.astype(BF16), b.astype(BF16))


def _dot_nt_bwd(res, g):
    a, b = res
    return _raw_nn(g, b), _raw_tn(g, a)


_dot_nt.defvjp(_dot_nt_fwd, _dot_nt_bwd)


@jax.custom_vjp
def _dot_tn(a, b):
    return _raw_tn(a, b)


def _dot_tn_fwd(a, b):
    return _raw_tn(a, b), (a.astype(BF16), b.astype(BF16))


def _dot_tn_bwd(res, g):
    a, b = res
    return _raw_nt(b, g), _raw_nn(a, g)


_dot_tn.defvjp(_dot_tn_fwd, _dot_tn_bwd)


@jax.custom_vjp
def _sigmoid(x):
    e = jnp.exp(-jnp.abs(x))
    return jnp.where(x >= 0, 1.0 / (1.0 + e), e / (1.0 + e))


def _sigmoid_fwd(x):
    s = _sigmoid(x)
    return s, s


def _sigmoid_bwd(s, g):
    return (g * s * (1.0 - s),)


_sigmoid.defvjp(_sigmoid_fwd, _sigmoid_bwd)


def _silu(x):
    return x * _sigmoid(x)


def _rms(x, w):
    return x * lax.rsqrt(jnp.mean(x * x, axis=-1, keepdims=True) + EPS) * w


def _split3(x):
    hi = x.astype(BF16)
    r = x - hi.astype(F32)
    mid = r.astype(BF16)
    lo = (r - mid.astype(F32)).astype(BF16)
    return hi, mid, lo


def _mask_apply(mask_bf16, x, contract):
    out = None
    for piece in _split3(x):
        d = lax.dot_general(mask_bf16, piece, (((contract,), (0,)), ((), ())),
                            preferred_element_type=F32)
        out = d if out is None else out + d
    return out


def _chunk_tri(rows):
    r = lax.broadcasted_iota(jnp.int32, (rows, rows), 0)
    c = lax.broadcasted_iota(jnp.int32, (rows, rows), 1)
    return ((r >= c) & (r // A_CHUNK == c // A_CHUNK)).astype(BF16)


@jax.custom_vjp
def _chunk_cumsum(x):
    return _mask_apply(_chunk_tri(x.shape[0]), x, 1)


def _chunk_cumsum_fwd(x):
    return _chunk_cumsum(x), None


def _chunk_cumsum_bwd(_, g):
    return (_mask_apply(_chunk_tri(g.shape[0]), g, 0),)


_chunk_cumsum.defvjp(_chunk_cumsum_fwd, _chunk_cumsum_bwd)


def _hgrn2_rows(q, zf, v, ga, st, l0, l1, onorm):
    rows = q.shape[0]
    n_sub = A_CHUNK // A_SUB
    mx = jnp.maximum(l0, l1)
    e0 = jnp.exp(l0 - mx)
    e1 = jnp.exp(l1 - mx)
    lb = e0 / (e0 + e1)
    lf = jnp.log(lb + (1.0 - lb) * _sigmoid(zf))
    k = (1.0 - lb) * _sigmoid(-zf)
    b = _chunk_cumsum(lf)

    t_idx = lax.broadcasted_iota(jnp.int32, (A_CHUNK, n_sub * A_CHUNK), 0)
    c_idx = lax.broadcasted_iota(jnp.int32, (A_CHUNK, n_sub * A_CHUNK), 1)
    sel = (c_idx // A_CHUNK == t_idx // A_SUB) & (c_idx % A_CHUNK <= t_idx)
    key_row = lax.broadcasted_iota(jnp.int32, (A_CHUNK, HD), 0)

    outs = []
    for n in range(rows // A_CHUNK):
        lo = n * A_CHUNK
        qc, kc, vc = q[lo:lo + A_CHUNK], k[lo:lo + A_CHUNK], v[lo:lo + A_CHUNK]
        lfc, bc = lf[lo:lo + A_CHUNK], b[lo:lo + A_CHUNK]
        b_last = bc[A_CHUNK - 1:A_CHUNK]
        o_inter = _dot_nt(qc * jnp.exp(bc), st)
        kv_t = _dot_tn(vc, kc * jnp.exp(b_last - bc))
        st = st * jnp.exp(b_last) + kv_t
        g_rows, k_subs = [], []
        for i in range(n_sub):
            g_i = bc[i * A_SUB:i * A_SUB + 1] - lfc[i * A_SUB:i * A_SUB + 1]
            g_rows.append(jnp.broadcast_to(g_i, (A_SUB, HD)))
            expo = jnp.where(key_row < (i + 1) * A_SUB, g_i - bc, -jnp.inf)
            k_subs.append(kc * jnp.exp(expo))
        q_sub = qc * jnp.exp(bc - jnp.concatenate(g_rows, axis=0))
        scores = _dot_nt(q_sub, jnp.concatenate(k_subs, axis=0))
        scores = jnp.where(sel, scores, 0.0)
        o_intra = _dot_nn(scores, jnp.concatenate([vc] * n_sub, axis=0))
        outs.append(o_inter + o_intra)
    o = jnp.concatenate(outs, axis=0)
    return _rms(o, onorm) * _silu(ga), st


def _gmlp_rows(u, vb, gb, lnw, lnb, ws, bias):
    rows = u.shape[0]
    mu = jnp.mean(vb, axis=-1, keepdims=True)
    xc = vb - mu
    vg = xc * lax.rsqrt(jnp.mean(xc * xc, axis=-1, keepdims=True) + EPS) * lnw + lnb
    r = lax.broadcasted_iota(jnp.int32, (B_CHUNK, B_CHUNK), 0)
    c = lax.broadcasted_iota(jnp.int32, (B_CHUNK, B_CHUNK), 1)
    ws_causal = jnp.where(r >= c, ws, 0.0)
    svs = [_dot_nn(ws_causal, vg[n * B_CHUNK:(n + 1) * B_CHUNK]) + bias
           for n in range(rows // B_CHUNK)]
    return u * jnp.concatenate(svs, axis=0) * _silu(gb)


def _rope(x, cos_t, sin_t):
    return x * cos_t + pltpu.roll(x, 64, 1) * sin_t


def _rope_transpose(g, cos_t, sin_t):
    return g * cos_t + pltpu.roll(g * sin_t, 64, 1)


ANY_SPEC = pl.BlockSpec(memory_space=pl.ANY)


def _live(deps):
    return [d for d in deps if d is not None]


def _skip_deps(body, n_in, n_deps):
    def wrapped(*refs):
        return body(*refs[:n_in], *refs[n_in + n_deps:])
    return wrapped


def _pure_call(name, fn, grid, in_specs, out_specs, out_shape, args, n_acc=0, deps=()):
    deps = _live(deps)
    n_in, n_out, n_deps = len(in_specs), len(out_specs), len(deps)
    in_specs = list(in_specs) + [ANY_SPEC] * n_deps
    args = tuple(args) + tuple(deps)

    def body(*refs):
        res = fn(*[r[...] for r in refs[:n_in]])
        if not isinstance(res, (tuple, list)):
            res = (res,)
        outs = refs[n_in + n_deps:n_in + n_deps + n_out]
        for o, r in zip(outs[:n_out - n_acc], res[:n_out - n_acc]):
            o[...] = r.astype(o.dtype)
        if n_acc:
            first = functools.reduce(jnp.logical_and, [pl.program_id(i) == 0 for i in range(len(grid))])
            for o, r in zip(outs[n_out - n_acc:], res[n_out - n_acc:]):
                @pl.when(first)
                def _(o=o, r=r):
                    o[...] = r.astype(o.dtype)

                @pl.when(jnp.logical_not(first))
                def _(o=o, r=r):
                    o[...] += r.astype(o.dtype)

    return pl.pallas_call(body, name=name, grid=grid, in_specs=in_specs, out_specs=out_specs,
                          out_shape=out_shape, compiler_params=_params(len(grid)))(*args)


def _sds(shape, dtype):
    return jax.ShapeDtypeStruct(shape, dtype)


def _row_spec(tm, width, col=0):
    return pl.BlockSpec((tm, width), lambda i, col=col: (i, col))


def _full_spec(shape):
    nd = len(shape)
    return pl.BlockSpec(shape, lambda *_: (0,) * nd)


def _mm_nn(name, a, b, out_dtype, tm, tn, deps=()):
    deps = _live(deps)
    m, k = a.shape
    j, _, n = b.shape
    per = n // tn

    def body(a_ref, b_ref, o_ref):
        o_ref[...] = _raw_nn(a_ref[...], b_ref[...]).astype(o_ref.dtype)

    return pl.pallas_call(
        _skip_deps(body, 2, len(deps)), name=name, grid=(m // tm, j * per),
        in_specs=[pl.BlockSpec((tm, k), lambda i, c: (i, 0)),
                  pl.BlockSpec((None, k, tn), lambda i, c: (c // per, 0, c % per))] + [ANY_SPEC] * len(deps),
        out_specs=pl.BlockSpec((tm, tn), lambda i, c: (i, c)),
        out_shape=_sds((m, j * n), out_dtype), compiler_params=_params(2))(a, b, *deps)


def _mm_nt(name, a, b, out_dtype, tm, tn, deps=()):
    deps = _live(deps)
    m = a.shape[0]
    j, nn, n = b.shape

    def body(a_ref, b_ref, o_ref):
        b_all = b_ref[0] if j == 1 else jnp.concatenate([b_ref[s] for s in range(j)], axis=1)
        o_ref[...] = _raw_nt(a_ref[...], b_all).astype(o_ref.dtype)

    return pl.pallas_call(
        _skip_deps(body, 2, len(deps)), name=name, grid=(m // tm, nn // tn),
        in_specs=[pl.BlockSpec((tm, j * n), lambda i, c: (i, 0)),
                  pl.BlockSpec((j, tn, n), lambda i, c: (0, c, 0))] + [ANY_SPEC] * len(deps),
        out_specs=pl.BlockSpec((tm, tn), lambda i, c: (i, c)),
        out_shape=_sds((m, nn), out_dtype), compiler_params=_params(2))(a, b, *deps)


def _mm_tn(name, a, b, j, out_dtype, tm, tn, deps=()):
    deps = _live(deps)
    k, m = a.shape
    n = b.shape[1] // j
    per = n // tn

    def body(a_ref, b_ref, o_ref):
        o_ref[...] = _raw_tn(a_ref[...], b_ref[...]).astype(o_ref.dtype)

    return pl.pallas_call(
        _skip_deps(body, 2, len(deps)), name=name, grid=(m // tm, j * per),
        in_specs=[pl.BlockSpec((k, tm), lambda i, c: (0, i)),
                  pl.BlockSpec((k, tn), lambda i, c: (0, c))] + [ANY_SPEC] * len(deps),
        out_specs=pl.BlockSpec((None, tm, tn), lambda i, c: (c // per, i, c % per)),
        out_shape=_sds((j, m, n), out_dtype), compiler_params=_params(2))(a, b, *deps)


def _mm_tn_parity(name, a, b, j, parity, out_dtype, tm, deps=()):
    deps = _live(deps)
    k, m = a.shape
    n = b.shape[1] // j

    def body(par_ref, a_ref, b_ref, o_ref):
        del par_ref
        o_ref[...] = _raw_tn(a_ref[...], b_ref[...]).astype(o_ref.dtype)

    grid_spec = pltpu.PrefetchScalarGridSpec(
        num_scalar_prefetch=1, grid=(m // tm, j // 2),
        in_specs=[pl.BlockSpec((k, tm), lambda i, s, par: (0, i)),
                  pl.BlockSpec((k, n), lambda i, s, par: (0, 2 * s + par[0]))] + [ANY_SPEC] * len(deps),
        out_specs=pl.BlockSpec((None, tm, n), lambda i, s, par: (s, i, 0)))
    return pl.pallas_call(
        lambda par_ref, *refs: _skip_deps(functools.partial(body, par_ref), 2, len(deps))(*refs),
        name=name, grid_spec=grid_spec, out_shape=_sds((j // 2, m, n), out_dtype),
        compiler_params=_params(2))(parity, a, b, *deps)


TM = 256


def _pre_norm(name, x, w_row, deps=()):
    def fn(xv, w):
        return _rms(xv, w)
    return _pure_call(name, fn, (T // TM,), [_row_spec(TM, D), _full_spec((1, D))],
                      [_row_spec(TM, D)], [_sds((T, D), BF16)], (x, w_row), deps=deps)[0]


def _post_pre_norm(x, y, w_post, w_pre):
    def fn(xv, yv, wp, wn):
        x1 = xv + _rms(yv, wp)
        return x1, _rms(x1, wn)
    return _pure_call("post_pre_norm", fn, (T // TM,),
                      [_row_spec(TM, D), _row_spec(TM, D), _full_spec((1, D)), _full_spec((1, D))],
                      [_row_spec(TM, D), _row_spec(TM, D)],
                      [_sds((T, D), F32), _sds((T, D), BF16)], (x, y, w_post, w_pre))


def _in_proj_norms_bwd(dz, w_t, y, x1, w_post, w_pre, dx1_in, deps=()):
    def fn(dzv, w, yv, x1v, wp, wn, dx1v):
        _, vjp_pre = jax.vjp(_rms, x1v, wn)
        dx1_h, dwn = vjp_pre(_raw_nn(dzv, w[0]))
        dx1 = dx1v + dx1_h
        _, vjp_post = jax.vjp(_rms, yv, wp)
        dy, dwp = vjp_post(dx1)
        return dx1, dy, dwp, dwn
    return _pure_call("od_in_dx_norms", fn, (T // TM,),
                      [_row_spec(TM, ODD_IN_PAD), _full_spec((1, ODD_IN_PAD, D)), _row_spec(TM, D), _row_spec(TM, D),
                       _full_spec((1, D)), _full_spec((1, D)), _row_spec(TM, D)],
                      [_row_spec(TM, D), _row_spec(TM, D), _full_spec((1, D)), _full_spec((1, D))],
                      [_sds((T, D), F32), _sds((T, D), BF16), _sds((1, D), F32), _sds((1, D), F32)],
                      (dz, w_t, y, x1, w_post, w_pre, dx1_in), n_acc=2, deps=deps)


def _out_proj_loss(og, w_out, x1, w_post, target):
    tm = 512

    def fn(ogv, w, x1v, wp, tv):
        r, vjp = jax.vjp(_rms, _raw_nn(ogv, w[0]), wp)
        err = x1v + r - tv
        part = 0.5 * jnp.sum(jnp.mean(err * err, axis=-1, keepdims=True), axis=0, keepdims=True)
        dx2 = err * (1.0 / D)
        dy, dwp = vjp(dx2)
        return dx2, dy, jnp.broadcast_to(part, (1, 128)), dwp
    return _pure_call("od_out_loss", fn, (T // tm,),
                      [_row_spec(tm, D), _full_spec((1, D, D)), _row_spec(tm, D), _full_spec((1, D)), _row_spec(tm, D)],
                      [_row_spec(tm, D), _row_spec(tm, D), _full_spec((1, 128)), _full_spec((1, D))],
                      [_sds((T, D), F32), _sds((T, D), BF16), _sds((1, 128), F32), _sds((1, D), F32)],
                      (og, w_out, x1, w_post, target), n_acc=2)


def _pre_norm_bwd(x, w_row, dh, dx_res, deps=()):
    def fn(xv, w, dhv, dxv):
        _, vjp = jax.vjp(_rms, xv, w)
        dx, dw = vjp(dhv)
        return dxv + dx, dw
    return _pure_call("pre_norm_bwd", fn, (T // TM,),
                      [_row_spec(TM, D), _full_spec((1, D)), _row_spec(TM, D), _row_spec(TM, D)],
                      [_row_spec(TM, D), _full_spec((1, D))],
                      [_sds((T, D), F32), _sds((1, D), F32)], (x, w_row, dh, dx_res), n_acc=1, deps=deps)


RA = 256


def _head(ref, hh):
    return ref[:, hh * HD:(hh + 1) * HD]


def _z_part(z_ref, k, h):
    lo = (k * A_HEADS + h) * HD
    return z_ref[:, lo:lo + HD]


def _even_specs():
    return [_full_spec((1, A_HEADS * HD)), _full_spec((1, A_HEADS * HD)), _full_spec((1, HD)),
            _full_spec((1, B_GROUPS * HD)), _full_spec((1, B_GROUPS * HD)),
            _full_spec((B_GROUPS, B_CHUNK, B_CHUNK)), _full_spec((B_GROUPS, B_CHUNK, 1))]


def _even_fwd(z, l0, l1, onorm, lnw, lnb, ws, bias):
    nb = T // RA

    def body(z_ref, l0_ref, l1_ref, on_ref, lnw_ref, lnb_ref, ws_ref, bias_ref, cat_ref, sst_ref, st_scr):
        @pl.when(pl.program_id(0) == 0)
        def _():
            st_scr[...] = jnp.zeros_like(st_scr)

        for hh in range(A_HEADS):
            st = st_scr[hh]
            sst_ref[hh] = st
            out, st_new = _hgrn2_rows(_z_part(z_ref, 0, hh), _z_part(z_ref, 1, hh), _z_part(z_ref, 2, hh),
                                      _z_part(z_ref, 3, hh), st, _head(l0_ref, hh), _head(l1_ref, hh), on_ref[...])
            cat_ref[:, hh * HD:(hh + 1) * HD] = out.astype(cat_ref.dtype)
            st_scr[hh] = st_new
        for gg in range(B_GROUPS):
            out = _gmlp_rows(_z_part(z_ref, 4, gg), _z_part(z_ref, 5, gg), _z_part(z_ref, 6, gg),
                             _head(lnw_ref, gg), _head(lnb_ref, gg), ws_ref[gg], bias_ref[gg])
            cat_ref[:, (A_HEADS + gg) * HD:(A_HEADS + gg + 1) * HD] = out.astype(cat_ref.dtype)

    return pl.pallas_call(
        body, name="even_mixers_fwd", grid=(nb,),
        in_specs=[pl.BlockSpec((RA, EVEN_IN), lambda r: (r, 0))] + _even_specs(),
        out_specs=[pl.BlockSpec((RA, 2 * A_HEADS * HD), lambda r: (r, 0)),
                   pl.BlockSpec((A_HEADS, None, HD, HD), lambda r: (0, r, 0, 0))],
        out_shape=[_sds((T, 2 * A_HEADS * HD), BF16), _sds((A_HEADS, nb, HD, HD), F32)],
        scratch_shapes=[pltpu.VMEM((A_HEADS, HD, HD), F32)],
        compiler_params=_params(1))(z, l0, l1, onorm, lnw, lnb, ws, bias)


def _even_bwd(z, l0, l1, onorm, lnw, lnb, ws, bias, sst, dy, w_out, deps=()):
    nb = T // RA
    deps = _live(deps)

    def body(z_ref, l0_ref, l1_ref, on_ref, lnw_ref, lnb_ref, ws_ref, bias_ref, sst_ref, dy_ref, w_ref,
             dz_ref, dl0_ref, dl1_ref, don_ref, dlnw_ref, dlnb_ref, dws_ref, dbias_ref, ds_scr):
        first = pl.program_id(0) == 0

        @pl.when(first)
        def _():
            ds_scr[...] = jnp.zeros_like(ds_scr)

        dcat = _raw_nt(dy_ref[...], w_ref[0])

        def put(k, h, val):
            lo = (k * A_HEADS + h) * HD
            dz_ref[:, lo:lo + HD] = val.astype(dz_ref.dtype)

        sums = []
        don = None
        for hh in range(A_HEADS):
            lanes = slice(hh * HD, (hh + 1) * HD)
            _, vjp = jax.vjp(_hgrn2_rows, _z_part(z_ref, 0, hh), _z_part(z_ref, 1, hh), _z_part(z_ref, 2, hh),
                             _z_part(z_ref, 3, hh), sst_ref[hh], _head(l0_ref, hh), _head(l1_ref, hh), on_ref[...])
            dq, dzf, dv, dga, dst, dl0, dl1, don_h = vjp((dcat[:, lanes], ds_scr[hh]))
            for k, val in enumerate((dq, dzf, dv, dga)):
                put(k, hh, val)
            ds_scr[hh] = dst
            sums += [(dl0_ref, (slice(None), lanes), dl0), (dl1_ref, (slice(None), lanes), dl1)]
            don = don_h if don is None else don + don_h
        sums.append((don_ref, slice(None), don))
        for gg in range(B_GROUPS):
            lanes = slice(gg * HD, (gg + 1) * HD)
            _, vjp = jax.vjp(_gmlp_rows, _z_part(z_ref, 4, gg), _z_part(z_ref, 5, gg), _z_part(z_ref, 6, gg),
                             _head(lnw_ref, gg), _head(lnb_ref, gg), ws_ref[gg], bias_ref[gg])
            du, dv, dg, dlnw, dlnb, dws, dbias = vjp(dcat[:, (A_HEADS + gg) * HD:(A_HEADS + gg + 1) * HD])
            for k, val in enumerate((du, dv, dg)):
                put(4 + k, gg, val)
            sums += [(dlnw_ref, (slice(None), lanes), dlnw), (dlnb_ref, (slice(None), lanes), dlnb),
                     (dws_ref, gg, dws), (dbias_ref, gg, dbias)]
        for ref, idx, val in sums:
            @pl.when(first)
            def _(ref=ref, idx=idx, val=val):
                ref[idx] = val

            @pl.when(jnp.logical_not(first))
            def _(ref=ref, idx=idx, val=val):
                ref[idx] += val

    small = _even_specs()
    return pl.pallas_call(
        _skip_deps(body, 11, len(deps)), name="even_mixers_bwd", grid=(nb,),
        in_specs=[pl.BlockSpec((RA, EVEN_IN), lambda r: (nb - 1 - r, 0))] + small
        + [pl.BlockSpec((A_HEADS, None, HD, HD), lambda r: (0, nb - 1 - r, 0, 0)),
           pl.BlockSpec((RA, D), lambda r: (nb - 1 - r, 0)), _full_spec((1, 2 * A_HEADS * HD, D))]
        + [ANY_SPEC] * len(deps),
        out_specs=[pl.BlockSpec((RA, EVEN_IN), lambda r: (nb - 1 - r, 0))] + small,
        out_shape=[_sds((T, EVEN_IN), BF16), _sds((1, A_HEADS * HD), F32), _sds((1, A_HEADS * HD), F32),
                   _sds((1, HD), F32), _sds((1, B_GROUPS * HD), F32), _sds((1, B_GROUPS * HD), F32),
                   _sds((B_GROUPS, B_CHUNK, B_CHUNK), F32), _sds((B_GROUPS, B_CHUNK, 1), F32)],
        scratch_shapes=[pltpu.VMEM((A_HEADS, HD, HD), F32)],
        compiler_params=_params(1))(z, l0, l1, onorm, lnw, lnb, ws, bias, sst, dy, w_out, *deps)


def _mla_pre(z1, qn, kvn, cos_t, sin_t):
    def fn(cq, ckv, kpe, cs, sn, wq, wkv):
        return _rms(cq, wq), _rms(ckv, wkv), _rope(kpe, cs, sn)
    return _pure_call("mla_pre", fn, (T // TM,),
                      [_row_spec(TM, C_RANK, 4), _row_spec(TM, C_RANK, 5), _row_spec(TM, HD, 24),
                       _row_spec(TM, HD), _row_spec(TM, HD),
                       _full_spec((1, C_RANK)), _full_spec((1, C_RANK))],
                      [_row_spec(TM, C_RANK), _row_spec(TM, C_RANK), _row_spec(TM, HD)],
                      [_sds((T, C_RANK), BF16), _sds((T, C_RANK), BF16), _sds((T, HD), BF16)],
                      (z1, z1, z1, cos_t, sin_t, qn, kvn))


def _mla_pre_bwd(z1, qn, kvn, cos_t, sin_t, dq, w_qb, dkv, w_kvb, dkp, dgate, deps=()):
    def fn(cq, ckv, cs, sn, wq, wkv, dqv, wqb, dkvv, wkvb, g_kp, g_gate):
        _, vjp_q = jax.vjp(_rms, cq, wq)
        dcq, dwq = vjp_q(_raw_nt(dqv, wqb[0]))
        _, vjp_kv = jax.vjp(_rms, ckv, wkv)
        dckv, dwkv = vjp_kv(_raw_nt(dkvv, jnp.concatenate([wkvb[s] for s in range(N_DEV)], axis=1)))
        dz1 = jnp.concatenate([g_gate, dcq.astype(BF16), dckv.astype(BF16),
                               _rope_transpose(g_kp, cs, sn).astype(BF16)], axis=1)
        return dz1, dwq, dwkv
    return _pure_call("mla_pre_bwd", fn, (T // TM,),
                      [_row_spec(TM, C_RANK, 4), _row_spec(TM, C_RANK, 5),
                       _row_spec(TM, HD), _row_spec(TM, HD),
                       _full_spec((1, C_RANK)), _full_spec((1, C_RANK)),
                       _row_spec(TM, C_HEADS * QP), _full_spec((1, C_RANK, C_HEADS * QP)),
                       _row_spec(TM, C_HEADS * KVW), _full_spec((N_DEV, C_RANK, C_HEADS * KVW // N_DEV)),
                       _row_spec(TM, HD), _row_spec(TM, D)],
                      [_row_spec(TM, ODD_IN_PAD), _full_spec((1, C_RANK)), _full_spec((1, C_RANK))],
                      [_sds((T, ODD_IN_PAD), BF16), _sds((1, C_RANK), F32), _sds((1, C_RANK), F32)],
                      (z1, z1, cos_t, sin_t, qn, kvn, dq, w_qb, dkv, w_kvb, dkp, dgate), n_acc=2, deps=deps)


TQ = 256
HP = 2
KVW = C_NOPE + C_V


def _att_keys(kv_ref, kp_ref, k_scr):
    @pl.when(pl.program_id(1) == 0)
    def _():
        for hh in range(HP):
            k_scr[hh, :, 0:C_NOPE] = kv_ref[:, hh * KVW:hh * KVW + C_NOPE]
            k_scr[hh, :, C_NOPE:QP] = kp_ref[...]


def _att_scores(q, cos_ref, sin_ref, k_scr, hh, n):
    keys = (n + 1) * TQ
    qr = jnp.concatenate([q[:, :C_NOPE], _rope(q[:, C_NOPE:], cos_ref[...], sin_ref[...])], axis=1).astype(BF16)
    return qr, _raw_nt(qr, k_scr[hh, 0:keys, :]) * ATT_SCALE


def _causal(x, n, fill):
    row = lax.broadcasted_iota(jnp.int32, (TQ, TQ), 0)
    col = lax.broadcasted_iota(jnp.int32, (TQ, TQ), 1)
    diag = jnp.where(col <= row, x[:, n * TQ:], fill)
    return diag if n == 0 else jnp.concatenate([x[:, :n * TQ], diag], axis=1)


def _per_query_block(fn):
    for n in range(T // TQ):
        pl.when(pl.program_id(1) == n)(functools.partial(fn, n))


def _att_in_specs():
    return [pl.BlockSpec((TQ, HP * QP), lambda g, i: (i, g)),
            pl.BlockSpec((TQ, HD), lambda g, i: (i, 0)),
            pl.BlockSpec((TQ, HD), lambda g, i: (i, 0)),
            pl.BlockSpec((T, HP * KVW), lambda g, i: (0, g)),
            pl.BlockSpec((T, HD), lambda g, i: (0, 0))]


def _attention_fwd(q, cos_t, sin_t, kv, kp, z1):
    def body(q_ref, cos_ref, sin_ref, kv_ref, kp_ref, gate_ref, o_ref, lse_ref, og_ref, k_scr):
        _att_keys(kv_ref, kp_ref, k_scr)

        def block(n):
            keys = (n + 1) * TQ
            for hh in range(HP):
                _, s = _att_scores(q_ref[:, hh * QP:(hh + 1) * QP], cos_ref, sin_ref, k_scr, hh, n)
                s = _causal(s, n, jnp.finfo(F32).min)
                m = jnp.max(s, axis=-1, keepdims=True)
                p = jnp.exp(s - m)
                l = jnp.sum(p, axis=-1, keepdims=True)
                v = kv_ref[0:keys, hh * KVW + C_NOPE:(hh + 1) * KVW]
                o = _raw_nn(p, v) / l
                lanes = slice(hh * C_V, (hh + 1) * C_V)
                o_ref[:, lanes] = o
                og_ref[:, lanes] = (o * _silu(gate_ref[:, lanes])).astype(og_ref.dtype)
                lse_ref[hh] = m + jnp.log(l)

        _per_query_block(block)

    heads = pl.BlockSpec((TQ, HP * C_V), lambda g, i: (i, g))
    return pl.pallas_call(
        body, name="attention_fwd", grid=(C_HEADS // HP, T // TQ), in_specs=_att_in_specs() + [heads],
        out_specs=[heads, pl.BlockSpec((HP, TQ, 1), lambda g, i: (g, i, 0)), heads],
        out_shape=[_sds((T, C_HEADS * C_V), F32), _sds((C_HEADS, T, 1), F32), _sds((T, C_HEADS * C_V), BF16)],
        scratch_shapes=[pltpu.VMEM((HP, T, QP), BF16)],
        compiler_params=_params(2))(q, cos_t, sin_t, kv, kp, z1)


def _attention_bwd(q, cos_t, sin_t, kv, kp, o, lse, dog, z1):
    nq = T // TQ

    def body(q_ref, cos_ref, sin_ref, kv_ref, kp_ref, o_ref, lse_ref, dog_ref, gate_ref,
             dq_ref, dkv_ref, dkp_ref, dgate_ref, k_scr, dk_scr, dv_scr):
        g, i = pl.program_id(0), pl.program_id(1)
        _att_keys(kv_ref, kp_ref, k_scr)

        @pl.when(i == 0)
        def _():
            dv_scr[...] = jnp.zeros_like(dv_scr)
            dk_scr[...] = jnp.zeros_like(dk_scr)

        def block(n):
            keys = (n + 1) * TQ
            for hh in range(HP):
                qr, s = _att_scores(q_ref[:, hh * QP:(hh + 1) * QP], cos_ref, sin_ref, k_scr, hh, n)
                p = _causal(jnp.exp(s - lse_ref[hh]), n, 0.0)
                lanes = slice(hh * C_V, (hh + 1) * C_V)
                ov, gate, dogv = o_ref[:, lanes], gate_ref[:, lanes], dog_ref[:, lanes]
                sig = _sigmoid(gate)
                silu = gate * sig
                dov = dogv * silu
                dgate_ref[:, lanes] = (dogv * ov * (sig + silu * (1.0 - sig))).astype(dgate_ref.dtype)
                delta = jnp.sum(dov * ov, axis=-1, keepdims=True)
                dp = _raw_nt(dov, kv_ref[0:keys, hh * KVW + C_NOPE:(hh + 1) * KVW])
                ds = p * (dp - delta) * ATT_SCALE
                dq = _raw_nn(ds, k_scr[hh, 0:keys, :])
                dq_ref[:, hh * QP:(hh + 1) * QP] = jnp.concatenate(
                    [dq[:, :C_NOPE], _rope_transpose(dq[:, C_NOPE:], cos_ref[...], sin_ref[...])],
                    axis=1).astype(dq_ref.dtype)
                dv_scr[hh, 0:keys, :] += _raw_tn(p, dov)
                dk_scr[hh, 0:keys, :] += _raw_tn(ds, qr)

        _per_query_block(block)

        @pl.when(i == nq - 1)
        def _():
            for hh in range(HP):
                dkv_ref[:, hh * KVW:(hh + 1) * KVW] = jnp.concatenate(
                    [dk_scr[hh, :, 0:C_NOPE], dv_scr[hh]], axis=1).astype(dkv_ref.dtype)

        @pl.when(jnp.logical_and(i == nq - 1, g == 0))
        def _():
            dkp_ref[...] = dk_scr[0, :, C_NOPE:QP]

        @pl.when(jnp.logical_and(i == nq - 1, g > 0))
        def _():
            dkp_ref[...] += dk_scr[0, :, C_NOPE:QP]

        @pl.when(i == nq - 1)
        def _():
            for hh in range(1, HP):
                dkp_ref[...] += dk_scr[hh, :, C_NOPE:QP]

    heads = pl.BlockSpec((TQ, HP * C_V), lambda g, i: (i, g))
    return pl.pallas_call(
        body, name="attention_bwd", grid=(C_HEADS // HP, nq),
        in_specs=_att_in_specs() + [heads, pl.BlockSpec((HP, TQ, 1), lambda g, i: (g, i, 0)), heads, heads],
        out_specs=[pl.BlockSpec((TQ, HP * QP), lambda g, i: (i, g)),
                   pl.BlockSpec((T, HP * KVW), lambda g, i: (0, g)),
                   _full_spec((T, HD)), heads],
        out_shape=[_sds((T, C_HEADS * QP), BF16), _sds((T, C_HEADS * KVW), BF16), _sds((T, HD), F32),
                   _sds((T, C_HEADS * C_V), BF16)],
        scratch_shapes=[pltpu.VMEM((HP, T, QP), BF16), pltpu.VMEM((HP, T, QP), F32), pltpu.VMEM((HP, T, C_V), F32)],
        compiler_params=_params(2))(q, cos_t, sin_t, kv, kp, o, lse, dog, z1)


def _adamw_math(w, g, m, v):
    m = ADAM_B1 * m + (1.0 - ADAM_B1) * g
    v = ADAM_B2 * v + (1.0 - ADAM_B2) * (g * g)
    m_hat = m / (1.0 - ADAM_B1 ** ADAM_STEP)
    v_hat = v / (1.0 - ADAM_B2 ** ADAM_STEP)
    delta = -ADAM_LR * (m_hat / (jnp.sqrt(v_hat) + ADAM_EPS) + ADAM_WD * w)
    return delta, m, v


def _adamw(name, parts, w, m, v, tr, tc=None):
    rows, cols = w.shape

    def fn(*vals):
        pvs, (wv, mv, vv) = vals[:len(parts)], vals[len(parts):]
        g = None
        for pv in pvs:
            for d in range(pv.shape[0]):
                term = pv[d].astype(F32)
                g = term if g is None else g + term
        return (g,) + _adamw_math(wv, g, mv, vv)

    tc = cols if tc is None else tc
    blk = pl.BlockSpec((tr, tc), lambda i, j: (i, j))
    part_specs = [pl.BlockSpec((n, tr, tc), lambda i, j: (0, i, j)) for _, n in parts]
    return _pure_call(name, fn, (rows // tr, cols // tc), part_specs + [blk, blk, blk],
                      [blk] * 4, [_sds((rows, cols), F32)] * 4, tuple(p for p, _ in parts) + (w, m, v))


SMALL_PARAM_SHAPES = ((2, D), (2, D), (2, A_HEADS * HD), (1, HD), (1, B_GROUPS * HD), (1, B_GROUPS * HD),
                      (B_GROUPS, B_CHUNK, B_CHUNK), (B_GROUPS, B_CHUNK))
SMALL_PIECES = ((0, 0, 0, 0), (0, 1, 1, 0), (1, 0, 1, 1), (1, 1, 1, 2), (2, 0, 2, 0), (2, 1, 2, 1),
                (3, 0, 3, 8), (4, 0, 2, 2), (5, 0, 2, 3))


def _small_rows(dnpre1, dnpost0, dnpost1, dl0, dl1, donorm, dlnw, dlnb, dws, dbias, dqn, dkvn, loss_part):
    return [jnp.concatenate([dnpre1, dnpost0, dnpost1], axis=0),
            jnp.concatenate([dl0, dl1, dlnw, dlnb], axis=0),
            jnp.concatenate([dbias.reshape(B_GROUPS, B_CHUNK), donorm, loss_part], axis=0),
            dws,
            jnp.concatenate([dqn, dkvn], axis=0)]


def _adamw_small(late_all, early_all, wmv):
    n_in = 6 + 3 * len(wmv)

    def body(*refs):
        gathered, params, outs = refs[:6], refs[6:n_in], refs[n_in:]

        def total(ref):
            s = ref[0]
            for d in range(1, N_DEV):
                s = s + ref[d]
            return s

        g_late, g2048, g1024, g128, g_ws, g512 = [total(r) for r in gathered]
        arrays = (g_late, g2048, g1024, g128)

        def update(p, rows, g):
            w_ref, m_ref, v_ref = params[3 * p:3 * p + 3]
            delta, m, v = _adamw_math(w_ref[rows], g, m_ref[rows], v_ref[rows])
            for out, val in zip(outs[4 * p:4 * p + 4], (g, delta, m, v)):
                out[rows] = val

        for p, row, arr, arr_row in SMALL_PIECES:
            update(p, pl.ds(row, 1), arrays[arr][arr_row:arr_row + 1])
        update(6, slice(None), g_ws)
        update(7, slice(None), g128[0:B_GROUPS])
        outs[32][...] = g128[B_GROUPS + 1:B_GROUPS + 2]
        outs[33][...] = g512

    vmem = pl.BlockSpec(memory_space=pltpu.VMEM)
    flat = [a for t in wmv for a in t]
    out_shape = [_sds(s, F32) for s in SMALL_PARAM_SHAPES for _ in range(4)] + [_sds((1, 128), F32), _sds((2, C_RANK), F32)]
    res = pl.pallas_call(body, name="adamw_small", in_specs=[vmem] * n_in, out_specs=[vmem] * len(out_shape),
                         out_shape=out_shape,
                         compiler_params=pltpu.CompilerParams(vmem_limit_bytes=VMEM_LIMIT_V7X))(late_all, *early_all, *flat)
    return [res[4 * p:4 * p + 4] for p in range(8)], res[32], res[33]


def _exchange(name, arrs, gather, deps=()):
    n = len(arrs)
    deps = _live(deps)

    def body(*refs):
        ins, outs = refs[:n], refs[n + len(deps):2 * n + len(deps)]
        send_sems, recv_sems, local_sems = refs[2 * n + len(deps):]
        x, y, c = lax.axis_index("x"), lax.axis_index("y"), lax.axis_index("c")
        me = 4 * x + 2 * y + c

        def peer(k):
            return (x ^ (k >> 2), y ^ ((k >> 1) & 1), c ^ (k & 1))

        def copy(a, k):
            src = ins[a] if gather else ins[a].at[me ^ k]
            return pltpu.make_async_remote_copy(
                src_ref=src, dst_ref=outs[a].at[me], send_sem=send_sems.at[a, k - 1],
                recv_sem=recv_sems.at[a, k - 1], device_id=peer(k), device_id_type=MESH_ID)

        def arrival(a, k):
            src = ins[a] if gather else ins[a].at[me]
            return pltpu.make_async_remote_copy(
                src_ref=src, dst_ref=outs[a].at[me ^ k], send_sem=send_sems.at[a, k - 1],
                recv_sem=recv_sems.at[a, k - 1], device_id=peer(k), device_id_type=MESH_ID)

        own = [pltpu.make_async_copy(ins[a] if gather else ins[a].at[me], outs[a].at[me], local_sems.at[a])
               for a in range(n)]
        for cp in own:
            cp.start()
        for k in range(1, N_DEV):
            for a in range(n):
                copy(a, k).start()
        for k in range(1, N_DEV):
            for a in range(n):
                arrival(a, k).wait_recv()
        for k in range(1, N_DEV):
            for a in range(n):
                copy(a, k).wait_send()
        for cp in own:
            cp.wait()

    any_spec = pl.BlockSpec(memory_space=pl.ANY)
    out_shape = [_sds((N_DEV,) + a.shape if gather else a.shape, a.dtype) for a in arrs]
    return pl.pallas_call(
        body, name=name, in_specs=[any_spec] * (n + len(deps)), out_specs=[any_spec] * n, out_shape=out_shape,
        scratch_shapes=[pltpu.SemaphoreType.DMA((n, N_DEV - 1)), pltpu.SemaphoreType.DMA((n, N_DEV - 1)),
                        pltpu.SemaphoreType.DMA((n,))],
        compiler_params=pltpu.CompilerParams(has_side_effects=True))(*arrs, *deps)


HBM_SPEC = pl.BlockSpec(memory_space=pltpu.HBM)
SEM_SPEC = pl.BlockSpec(memory_space=pltpu.SEMAPHORE)
DATAFLOW = pltpu.SideEffectType.DATAFLOW_SIDE_EFFECTING


def _my_index():
    return 4 * lax.axis_index("x") + 2 * lax.axis_index("y") + lax.axis_index("c")


def _plan_copies(plan, refs, send_sems, recv_sems):
    x, y, c = lax.axis_index("x"), lax.axis_index("y"), lax.axis_index("c")
    return [pltpu.make_async_remote_copy(
        src_ref=src, dst_ref=dst, send_sem=send_sems.at[i], recv_sem=recv_sems.at[i],
        device_id=(x ^ (k >> 2), y ^ ((k >> 1) & 1), c ^ (k & 1)), device_id_type=MESH_ID)
        for i, (src, dst, k) in enumerate(plan(refs, 4 * x + 2 * y + c))]


def _split_call(name, bufs, waits=None, starts=None, deps=()):
    n = len(bufs)
    deps = _live(deps)
    n_wait = 2 if waits else 0

    def body(*refs):
        zones = refs[:n]
        if waits:
            for cp in _plan_copies(waits[2], zones, refs[n], refs[n + 1]):
                cp.wait_send()
                cp.wait_recv()
        if starts:
            first_out = n + n_wait + len(deps)
            for cp in _plan_copies(starts[0], zones, refs[first_out], refs[first_out + 1]):
                cp.start()
            refs[-1][...] = jnp.zeros_like(refs[-1])

    out_specs, out_shape = [], []
    if starts:
        sems = pltpu.SemaphoreType.DMA((starts[1],))
        out_specs, out_shape = [SEM_SPEC, SEM_SPEC], [sems, sems]
    out_specs += [HBM_SPEC] * n
    out_shape += [pltpu.HBM(b.shape, b.dtype) for b in bufs]
    if starts:
        out_specs.append(pl.BlockSpec(memory_space=pltpu.VMEM))
        out_shape.append(_sds((8, 128), F32))
    first_buf = 2 if starts else 0
    res = pl.pallas_call(
        body, name=name,
        in_specs=[HBM_SPEC] * n + [SEM_SPEC] * n_wait + [ANY_SPEC] * len(deps),
        out_specs=out_specs, out_shape=out_shape,
        input_output_aliases={i: first_buf + i for i in range(n)},
        compiler_params=pltpu.CompilerParams(has_side_effects=DATAFLOW),
    )(*[pltpu.with_memory_space_constraint(b, pltpu.HBM) for b in bufs], *(waits[:2] if waits else ()), *deps)
    out_bufs = list(res[first_buf:first_buf + n])
    return out_bufs, ((res[0], res[1]) if starts else None), (res[-1] if starts else None)


def _direct_plan(n, gather):
    def plan(refs, me):
        return [(refs[a] if gather else refs[a].at[me ^ k], refs[n + a].at[me], k)
                for k in range(1, N_DEV) for a in range(n)]
    return plan


def _own_slot_filled(a, gather):
    me = _my_index()
    if gather:
        return lax.dynamic_update_slice_in_dim(lax.empty((N_DEV,) + a.shape, a.dtype), a[None], me, 0)
    return lax.dynamic_update_slice_in_dim(lax.empty(a.shape, a.dtype), lax.dynamic_slice_in_dim(a, me, 1, 0), me, 0)


def _exchange_start(name, arrs, gather, deps=()):
    n = len(arrs)
    lands = [_own_slot_filled(a, gather) for a in arrs]
    plan = _direct_plan(n, gather)
    bufs, sems, token = _split_call(name, list(arrs) + lands, starts=(plan, n * (N_DEV - 1)), deps=deps)
    return (n, plan, sems, bufs, None), token


def _exchange_wait(name, handle, after):
    return _split_done(name, handle, after)


ICI_PEERS = (2, 4, 6)
SIBLING = 1


def _gather2_send(name, arrs, deps=()):
    n = len(arrs)
    lands = [_own_slot_filled(a, True) for a in arrs]

    def plan(refs, me_):
        return [(refs[a], refs[n + a].at[me_], k) for k in (SIBLING,) + ICI_PEERS for a in range(n)]

    bufs, sems, token = _split_call(name, list(arrs) + lands, starts=(plan, 4 * n), deps=deps)
    return (n, plan, sems, bufs, None), token


def _gather2_relay(name, handle, after):
    n, plan, sems, bufs, _ = handle
    after = after if isinstance(after, (list, tuple)) else [after]

    def relay(refs, me_):
        return [(refs[n + a].at[me_ ^ k], refs[n + a].at[me_ ^ k], SIBLING) for k in ICI_PEERS for a in range(n)]

    bufs, sems2, token = _split_call(name, bufs, waits=(sems[0], sems[1], plan), starts=(relay, 3 * n), deps=after)
    return (n, relay, sems2, bufs, None), token


def _split_done(name, handle, after, all_bufs=False):
    n, plan, sems, bufs, _ = handle
    bufs, _, _ = _split_call(name, bufs, waits=(sems[0], sems[1], plan), deps=[after])
    return bufs if all_bufs else bufs[n:]


def _scatter2_pair(name, for_sibling, deps=()):
    n = len(for_sibling)
    pairs = [lax.empty(s.shape, s.dtype) for s in for_sibling]

    def plan(refs, me):
        del me
        return [(refs[a].at[s], refs[n + a].at[s], SIBLING) for s in range(4) for a in range(n)]

    bufs, sems, token = _split_call(name, list(for_sibling) + pairs, starts=(plan, 4 * n), deps=deps)
    return (n, plan, sems, bufs, None), token


def _pair_add(name, mine, pair):
    _, rows, cols = mine.shape
    tr = rows // 2

    def fn(a, b):
        return a.astype(F32) + b.astype(F32)

    blk = pl.BlockSpec((None, tr, cols), lambda s, i: (s, i, 0))
    return _pure_call(name, fn, (4, rows // tr), [blk, blk], [blk], [_sds(mine.shape, mine.dtype)], (mine, pair))[0]


def _scatter2_send(name, chip_sums, deps=()):
    n = len(chip_sums)
    finals = [lax.empty((3,) + c.shape[1:], c.dtype) for c in chip_sums]

    def plan(refs, me):
        return [(refs[a].at[(me >> 1) ^ j], refs[n + a].at[j - 1], 2 * j) for j in range(1, 4) for a in range(n)]

    bufs, sems, token = _split_call(name, list(chip_sums) + finals, starts=(plan, 3 * n), deps=deps)
    return (n, plan, sems, bufs, None), token


def _pad_rope(p):
    z = jnp.zeros(p.shape[:-1] + (32,), p.dtype)
    return jnp.concatenate([p[..., :32], z, p[..., 32:], z], axis=-1)


def _unpad_rope(p):
    return jnp.concatenate([p[..., :32], p[..., 64:96]], axis=-1)


def _odd_in_layout(wt):
    wt = wt.reshape(ODD_IN, D)
    cq, ckv, kpe, gate = wt[:512], wt[512:1024], wt[1024:1088], wt[1088:]
    z = jnp.zeros((32, D), wt.dtype)
    return jnp.concatenate([gate, cq, ckv, kpe[:32], z, kpe[32:], z], axis=0)


def _odd_in_unlayout(dwt):
    gate, cq, ckv, kpe = dwt[:2048], dwt[2048:2560], dwt[2560:3072], dwt[3072:]
    wt = jnp.concatenate([cq, ckv, kpe[:32], kpe[64:96], gate], axis=0)
    return wt.reshape(N_DEV, ODD_IN // N_DEV, D)


def _qb_layout(w):
    w = w.transpose(1, 0, 2).reshape(C_RANK, C_HEADS, C_QK)
    w = jnp.concatenate([w[..., :C_NOPE], _pad_rope(w[..., C_NOPE:])], axis=-1)
    return w.reshape(C_RANK, C_HEADS * QP)


def _qb_unlayout(dw):
    dw = dw.reshape(C_RANK, C_HEADS, QP)
    dw = jnp.concatenate([dw[..., :C_NOPE], _unpad_rope(dw[..., C_NOPE:])], axis=-1)
    return dw.reshape(C_RANK, N_DEV, C_HEADS * C_QK // N_DEV).transpose(1, 0, 2)


def _rope_tables(positions):
    inv_freq = ROPE_THETA ** (-jnp.arange(0, C_ROPE, 2, dtype=F32) / C_ROPE)
    ang = positions.astype(F32)[0][:, None] * inv_freq
    cos, sin = jnp.cos(ang), jnp.sin(ang)
    z = jnp.zeros_like(cos)
    return jnp.concatenate([cos, z, cos, z], axis=1), jnp.concatenate([-sin, z, sin, z], axis=1)


def _forward_backward(x, cos_t, sin_t, target, norm_pre, norm_post, lb_logits, a_onorm, ln_w, ln_b,
                      b_ws, b_bias, get_w, put_g, put_small=None, start_dep=None):
    npre0, npre1 = norm_pre[0:1], norm_pre[1:2]
    npost0, npost1 = norm_post[0:1], norm_post[1:2]
    l0, l1 = lb_logits[0:1], lb_logits[1:2]
    bias_col = b_bias.reshape(B_GROUPS, B_CHUNK, 1)
    ws = b_ws.reshape(B_GROUPS, B_CHUNK, B_CHUNK)

    h0 = _pre_norm("pre_norm0", x, npre0, deps=[start_dep])
    w_ev_in = get_w("ev_in", h0)
    z0 = _mm_nn("ev_in", h0, w_ev_in, F32, 1024, 896)
    cat, sst = _even_fwd(z0, l0, l1, a_onorm, ln_w, ln_b, ws, bias_col)
    w_ev_out = get_w("ev_out", cat)
    y0 = _mm_nn("ev_out", cat, w_ev_out, F32, 1024, 1024, deps=[get_w("od_relay", w_ev_out)])
    x1, h1 = _post_pre_norm(x, y0, npost0, npre1)
    w_od_in, w_qb, w_kvb, q_norm, kv_norm = get_w("od_mid", h1)
    z1 = _mm_nt("od_in", h1, w_od_in[None], F32, 1024, 640)
    cqn, ckvn, kp = _mla_pre(z1, q_norm, kv_norm, cos_t, sin_t)
    q = _mm_nn("od_qb", cqn, w_qb[None], F32, 1024, 1024)
    kv = _mm_nn("od_kvb", ckvn, w_kvb, BF16, 1024, 512)
    o, lse, og = _attention_fwd(q, cos_t, sin_t, kv, kp, z1)
    w_od_out = get_w("od_out", og)
    dx2, dy1, loss_part, dnpost1 = _out_proj_loss(og, w_od_out, x1, npost1, target)

    g_od_out = _mm_tn("od_out_dw", og, dy1, 1, BF16, 1024, 1024)
    tok = put_g("od_out", [g_od_out.reshape(N_DEV, D // N_DEV, D)])
    dog = _mm_nt("od_out_dx", dy1, w_od_out, F32, 1024, 1024, deps=[tok])
    dq, dkv, dkp, dgate = _attention_bwd(q, cos_t, sin_t, kv, kp, o, lse, dog, z1)
    g_qb = _mm_tn("od_qb_dw", cqn, dq, 1, F32, 512, 1024)
    g_kvb = _mm_tn("od_kvb_dw", ckvn, dkv, N_DEV, BF16, 512, 512)
    tok = put_g("od_qkv", [_qb_unlayout(g_qb[0]).astype(BF16), g_kvb])
    dz1, dqn, dkvn = _mla_pre_bwd(z1, q_norm, kv_norm, cos_t, sin_t, dq, w_qb[None], dkv, w_kvb, dkp, dgate,
                                  deps=[tok])
    g_od_in = _mm_tn("od_in_dw", dz1, h1, 1, F32, 640, 1024)
    tok = put_g("od_in", [_odd_in_unlayout(g_od_in[0]).astype(BF16)])
    dx1, dy0, dnpost0, dnpre1 = _in_proj_norms_bwd(dz1, w_od_in[None], y0, x1, npost0, npre1, dx2, deps=[tok])

    g_ev_out = _mm_tn("ev_out_dw", cat, dy0, 1, BF16, 1024, 1024)
    tok = put_g("ev_out", [g_ev_out.reshape(N_DEV, D // N_DEV, D)])
    dz0, dl0, dl1, donorm, dlnw, dlnb, dws, dbias = _even_bwd(z0, l0, l1, a_onorm, ln_w, ln_b, ws, bias_col, sst,
                                                              dy0, w_ev_out, deps=[tok])
    early = _small_rows(dnpre1, dnpost0, dnpost1, dl0, dl1, donorm, dlnw, dlnb, dws, dbias, dqn, dkvn, loss_part)
    tok = put_small(early) if put_small else None
    small_tok = tok

    def ev_in_half(name, parity, deps=()):
        return _mm_tn_parity(name, h0, dz0, N_DEV, parity, BF16, 1024, deps=[small_tok] + list(deps))

    tok = put_g("ev_in", ev_in_half)
    dh0 = _mm_nt("ev_in_dx", dz0, w_ev_in, F32, 1024, 256, deps=[tok])
    grad_x, dnpre0 = _pre_norm_bwd(x, npre0, dh0, dx1)
    return grad_x, early, dnpre0


def kernel(x, positions, norm_pre, norm_post, ev_w_in, ev_lb_logits, ev_a_onorm, ev_b_ln_w, ev_b_ln_b, ev_b_ws, ev_b_bias, ev_w_out, od_w_in, od_q_norm, od_w_qb, od_kv_norm, od_w_kvb, od_w_out, loss_target, m_norm_pre, m_norm_post, m_ev_w_in, m_ev_lb_logits, m_ev_a_onorm, m_ev_b_ln_w, m_ev_b_ln_b, m_ev_b_ws, m_ev_b_bias, m_ev_w_out, m_od_w_in, m_od_q_norm, m_od_w_qb, m_od_kv_norm, m_od_w_kvb, m_od_w_out, v_norm_pre, v_norm_post, v_ev_w_in, v_ev_lb_logits, v_ev_a_onorm, v_ev_b_ln_w, v_ev_b_ln_b, v_ev_b_ws, v_ev_b_bias, v_ev_w_out, v_od_w_in, v_od_q_norm, v_od_w_qb, v_od_kv_norm, v_od_w_kvb, v_od_w_out):
    me = 4 * lax.axis_index("x") + 2 * lax.axis_index("y") + lax.axis_index("c")
    bf = lambda w: w[0].astype(BF16)

    norms = jnp.pad(jnp.concatenate([od_q_norm, od_kv_norm], axis=1), ((0, 7), (0, 0)))
    sent = {}
    sent["ev_in"], tok = _gather2_send("gather_ev_in", [bf(ev_w_in)])
    sent["ev_out"], tok = _gather2_send("gather_ev_out", [bf(ev_w_out)], deps=[tok])
    sent["od"], tok = _gather2_send("gather_od", [od_w_in[0].T.astype(BF16), bf(od_w_qb), bf(od_w_kvb), norms,
                                                 bf(od_w_out)], deps=[tok])
    cos_t, sin_t = _rope_tables(positions)
    od = []

    def get_w(group, after):
        if group == "ev_in":
            relayed, token = _gather2_relay("relay_ev_in", sent["ev_in"], [after, cos_t, sin_t])
            return _split_done("arrived_ev_in", relayed, token)[0]
        if group == "ev_out":
            relayed, token = _gather2_relay("relay_ev_out", sent["ev_out"], after)
            return _split_done("arrived_ev_out", relayed, token)[0].reshape(1, D, D)
        if group == "od_relay":
            sent["od_relayed"], token = _gather2_relay("relay_od", sent["od"], after)
            return token
        if not od:
            od.extend(_split_done("arrived_od", sent["od_relayed"], after))
        w_od_in, w_qb, w_kvb, norms_all, w_od_out = od
        if group == "od_out":
            return w_od_out.reshape(1, D, D)
        return (_odd_in_layout(w_od_in), _qb_layout(w_qb), w_kvb,
                norms_all[:, 0, :64].reshape(1, C_RANK), norms_all[:, 0, 64:].reshape(1, C_RANK))

    scatters = {}

    def put_g(group, grads):
        if group == "ev_in":
            core = lax.axis_index("c").astype(jnp.int32).reshape(1)
            paired, token = _scatter2_pair("pair_ev_in", [grads("ev_in_dw_sibling", 1 - core)])
            mine = grads("ev_in_dw_own", core, deps=[token])
            pair = _split_done("paired_ev_in", paired, mine)[0]
            scatters[group], token = _scatter2_send("scatter_ev_in", [_pair_add("pair_add_ev_in", mine, pair)])
        else:
            scatters[group], token = _exchange_start("scatter_" + group, grads, False)
        return token

    def put_small(early):
        scatters["small"], token = _exchange_start("gather_small_early", early, True)
        return token

    grad_x, _, dnpre0 = _forward_backward(
        x[0], cos_t, sin_t, loss_target[0], norm_pre, norm_post, ev_lb_logits, ev_a_onorm, ev_b_ln_w,
        ev_b_ln_b, ev_b_ws, ev_b_bias, get_w, put_g, put_small, start_dep=tok)

    big_w = {"ev_w_in": ev_w_in, "ev_w_out": ev_w_out, "od_w_in": od_w_in, "od_w_qb": od_w_qb,
             "od_w_kvb": od_w_kvb, "od_w_out": od_w_out}
    big_m = {"ev_w_in": m_ev_w_in, "ev_w_out": m_ev_w_out, "od_w_in": m_od_w_in, "od_w_qb": m_od_w_qb,
             "od_w_kvb": m_od_w_kvb, "od_w_out": m_od_w_out}
    big_v = {"ev_w_in": v_ev_w_in, "ev_w_out": v_ev_w_out, "od_w_in": v_od_w_in, "od_w_qb": v_od_w_qb,
             "od_w_kvb": v_od_w_kvb, "od_w_out": v_od_w_out}
    big_out = {}
    after = grad_x
    for group, names in (("od_out", ["od_w_out"]), ("od_qkv", ["od_w_qb", "od_w_kvb"]), ("od_in", ["od_w_in"]),
                         ("ev_out", ["ev_w_out"])):
        parts = _exchange_wait("summed_" + group, scatters[group], after)
        for nm, p in zip(names, parts):
            w, m, v = big_w[nm][0], big_m[nm][0], big_v[nm][0]
            if nm == "od_w_in":
                res_t = _adamw("adamw_" + nm, [(p, N_DEV)], w.T, m.T, v.T, w.shape[1], 512)
                big_out[nm] = [r.T[None] for r in res_t]
            else:
                big_out[nm] = [r[None] for r in _adamw("adamw_" + nm, [(p, N_DEV)], w, m, v, w.shape[0] // 8)]
            after = big_out[nm][0]

    late_all = _exchange("gather_small_late", [dnpre0], gather=True, deps=[after])[0]
    early_all = _exchange_wait("arrived_small_early", scatters["small"], late_all)

    small_w = (norm_pre, norm_post, ev_lb_logits, ev_a_onorm, ev_b_ln_w, ev_b_ln_b, ev_b_ws, ev_b_bias)
    small_m = (m_norm_pre, m_norm_post, m_ev_lb_logits, m_ev_a_onorm, m_ev_b_ln_w, m_ev_b_ln_b, m_ev_b_ws, m_ev_b_bias)
    small_v = (v_norm_pre, v_norm_post, v_ev_lb_logits, v_ev_a_onorm, v_ev_b_ln_w, v_ev_b_ln_b, v_ev_b_ws, v_ev_b_bias)
    wmv = [tuple(a.reshape(s) for a in t) for s, t in zip(SMALL_PARAM_SHAPES, zip(small_w, small_m, small_v))]
    small_res, loss_row, g_norm_rows = _adamw_small(late_all, early_all, wmv)
    small_out = [[r.reshape(w.shape) for r in four] for four, w in zip(small_res, small_w)]
    loss = loss_row[0, 0]

    g_norms = jnp.concatenate([lax.dynamic_slice(g_norm_rows, (0, 64 * me), (1, 64)),
                               lax.dynamic_slice(g_norm_rows, (1, 64 * me), (1, 64))], axis=1)
    res_n = _adamw("adamw_norms", [(g_norms[None], 1)],
                   jnp.concatenate([od_q_norm, od_kv_norm], axis=1),
                   jnp.concatenate([m_od_q_norm, m_od_kv_norm], axis=1),
                   jnp.concatenate([v_od_q_norm, v_od_kv_norm], axis=1), 1)
    qn_out = [r[:, :64] for r in res_n]
    kvn_out = [r[:, 64:] for r in res_n]

    chip_sums, from_peers = _split_done("summed_ev_in", scatters["ev_in"], loss_row, all_bufs=True)
    own_chip = lax.dynamic_slice_in_dim(chip_sums, me >> 1, 1, 0)
    w = ev_w_in[0]
    big_out["ev_w_in"] = [r[None] for r in _adamw("adamw_ev_w_in", [(own_chip, 1), (from_peers, 3)], w, m_ev_w_in[0],
                                                  v_ev_w_in[0], w.shape[0] // 8)]

    order = ("norm_pre", "norm_post", "ev_w_in", "ev_lb_logits", "ev_a_onorm", "ev_b_ln_w", "ev_b_ln_b",
             "ev_b_ws", "ev_b_bias", "ev_w_out", "od_w_in", "od_q_norm", "od_w_qb", "od_kv_norm",
             "od_w_kvb", "od_w_out")
    small_names = ("norm_pre", "norm_post", "ev_lb_logits", "ev_a_onorm", "ev_b_ln_w", "ev_b_ln_b",
                   "ev_b_ws", "ev_b_bias")
    outs = [loss, grad_x[None]]
    for kind in range(4):
        for nm in order:
            if nm in big_out:
                outs.append(big_out[nm][kind])
            elif nm == "od_q_norm":
                outs.append(qn_out[kind])
            elif nm == "od_kv_norm":
                outs.append(kvn_out[kind])
            else:
                outs.append(small_out[small_names.index(nm)][kind])
    return tuple(outs)
```

```python
import functools

import jax
import jax.numpy as jnp
from jax import lax
from jax.experimental import pallas as pl
from jax.experimental.pallas import tpu as pltpu

F32 = jnp.float32
BF16 = jnp.bfloat16

N_DEV = 8
T = 2048
D = 2048
EPS = 1e-6
A_HEADS = 8
HD = 128
A_CHUNK = 64
A_SUB = 16
B_GROUPS = 8
B_CHUNK = 128
EVEN_IN = 7168
C_HEADS = 16
C_RANK = 512
C_NOPE = 128
C_ROPE = 64
C_QK = C_NOPE + C_ROPE
C_V = 128
ODD_IN = 3136
ODD_IN_PAD = 3200
QP = 256
ROPE_THETA = 10000.0
ATT_SCALE = C_QK ** -0.5

ADAM_LR = 0.001
ADAM_B1 = 0.9
ADAM_B2 = 0.999
ADAM_EPS = 1e-08
ADAM_WD = 0.01
ADAM_STEP = 10

VMEM_LIMIT_V7X = 56 * 1024 * 1024
MESH_ID = pl.DeviceIdType.MESH


def _params(n_grid):
    return pltpu.CompilerParams(dimension_semantics=("arbitrary",) * n_grid,
                                vmem_limit_bytes=VMEM_LIMIT_V7X)


def _dg(a, b, ca, cb):
    return lax.dot_general(a.astype(BF16), b.astype(BF16), (((ca,), (cb,)), ((), ())),
                           preferred_element_type=F32)


def _raw_nn(a, b):
    return _dg(a, b, 1, 0)


def _raw_nt(a, b):
    return _dg(a, b, 1, 1)


def _raw_tn(a, b):
    return _dg(a, b, 0, 0)


@jax.custom_vjp
def _dot_nn(a, b):
    return _raw_nn(a, b)


def _dot_nn_fwd(a, b):
    return _raw_nn(a, b), (a.astype(BF16), b.astype(BF16))


def _dot_nn_bwd(res, g):
    a, b = res
    return _raw_nt(g, b), _raw_tn(a, g)


_dot_nn.defvjp(_dot_nn_fwd, _dot_nn_bwd)


@jax.custom_vjp
def _dot_nt(a, b):
    return _raw_nt(a, b)


def _dot_nt_fwd(a, b):
    return _raw_nt(a, b), (a.astype(BF16), b.astype(BF16))


def _dot_nt_bwd(res, g):
    a, b = res
    return _raw_nn(g, b), _raw_tn(g, a)


_dot_nt.defvjp(_dot_nt_fwd, _dot_nt_bwd)


@jax.custom_vjp
def _dot_tn(a, b):
    return _raw_tn(a, b)


def _dot_tn_fwd(a, b):
    return _raw_tn(a, b), (a.astype(BF16), b.astype(BF16))


def _dot_tn_bwd(res, g):
    a, b = res
    return _raw_nt(b, g), _raw_nn(a, g)


_dot_tn.defvjp(_dot_tn_fwd, _dot_tn_bwd)


@jax.custom_vjp
def _sigmoid(x):
    e = jnp.exp(-jnp.abs(x))
    return jnp.where(x >= 0, 1.0 / (1.0 + e), e / (1.0 + e))


def _sigmoid_fwd(x):
    s = _sigmoid(x)
    return s, s


def _sigmoid_bwd(s, g):
    return (g * s * (1.0 - s),)


_sigmoid.defvjp(_sigmoid_fwd, _sigmoid_bwd)


def _silu(x):
    return x * _sigmoid(x)


def _rms(x, w):
    return x * lax.rsqrt(jnp.mean(x * x, axis=-1, keepdims=True) + EPS) * w


def _split3(x):
    hi = x.astype(BF16)
    r = x - hi.astype(F32)
    mid = r.astype(BF16)
    lo = (r - mid.astype(F32)).astype(BF16)
    return hi, mid, lo


def _mask_apply(mask_bf16, x, contract):
    out = None
    for piece in _split3(x):
        d = lax.dot_general(mask_bf16, piece, (((contract,), (0,)), ((), ())),
                            preferred_element_type=F32)
        out = d if out is None else out + d
    return out


def _chunk_tri(rows):
    r = lax.broadcasted_iota(jnp.int32, (rows, rows), 0)
    c = lax.broadcasted_iota(jnp.int32, (rows, rows), 1)
    return ((r >= c) & (r // A_CHUNK == c // A_CHUNK)).astype(BF16)


@jax.custom_vjp
def _chunk_cumsum(x):
    return _mask_apply(_chunk_tri(x.shape[0]), x, 1)


def _chunk_cumsum_fwd(x):
    return _chunk_cumsum(x), None


def _chunk_cumsum_bwd(_, g):
    return (_mask_apply(_chunk_tri(g.shape[0]), g, 0),)


_chunk_cumsum.defvjp(_chunk_cumsum_fwd, _chunk_cumsum_bwd)


def _hgrn2_rows(q, zf, v, ga, st, l0, l1, onorm):
    rows = q.shape[0]
    n_sub = A_CHUNK // A_SUB
    mx = jnp.maximum(l0, l1)
    e0 = jnp.exp(l0 - mx)
    e1 = jnp.exp(l1 - mx)
    lb = e0 / (e0 + e1)
    lf = jnp.log(lb + (1.0 - lb) * _sigmoid(zf))
    k = (1.0 - lb) * _sigmoid(-zf)
    b = _chunk_cumsum(lf)

    t_idx = lax.broadcasted_iota(jnp.int32, (A_CHUNK, n_sub * A_CHUNK), 0)
    c_idx = lax.broadcasted_iota(jnp.int32, (A_CHUNK, n_sub * A_CHUNK), 1)
    sel = (c_idx // A_CHUNK == t_idx // A_SUB) & (c_idx % A_CHUNK <= t_idx)
    key_row = lax.broadcasted_iota(jnp.int32, (A_CHUNK, HD), 0)

    outs = []
    for n in range(rows // A_CHUNK):
        lo = n * A_CHUNK
        qc, kc, vc = q[lo:lo + A_CHUNK], k[lo:lo + A_CHUNK], v[lo:lo + A_CHUNK]
        lfc, bc = lf[lo:lo + A_CHUNK], b[lo:lo + A_CHUNK]
        b_last = bc[A_CHUNK - 1:A_CHUNK]
        o_inter = _dot_nt(qc * jnp.exp(bc), st)
        kv_t = _dot_tn(vc, kc * jnp.exp(b_last - bc))
        st = st * jnp.exp(b_last) + kv_t
        g_rows, k_subs = [], []
        for i in range(n_sub):
            g_i = bc[i * A_SUB:i * A_SUB + 1] - lfc[i * A_SUB:i * A_SUB + 1]
            g_rows.append(jnp.broadcast_to(g_i, (A_SUB, HD)))
            expo = jnp.where(key_row < (i + 1) * A_SUB, g_i - bc, -jnp.inf)
            k_subs.append(kc * jnp.exp(expo))
        q_sub = qc * jnp.exp(bc - jnp.concatenate(g_rows, axis=0))
        scores = _dot_nt(q_sub, jnp.concatenate(k_subs, axis=0))
        scores = jnp.where(sel, scores, 0.0)
        o_intra = _dot_nn(scores, jnp.concatenate([vc] * n_sub, axis=0))
        outs.append(o_inter + o_intra)
    o = jnp.concatenate(outs, axis=0)
    return _rms(o, onorm) * _silu(ga), st


def _gmlp_rows(u, vb, gb, lnw, lnb, ws, bias):
    rows = u.shape[0]
    mu = jnp.mean(vb, axis=-1, keepdims=True)
    xc = vb - mu
    vg = xc * lax.rsqrt(jnp.mean(xc * xc, axis=-1, keepdims=True) + EPS) * lnw + lnb
    r = lax.broadcasted_iota(jnp.int32, (B_CHUNK, B_CHUNK), 0)
    c = lax.broadcasted_iota(jnp.int32, (B_CHUNK, B_CHUNK), 1)
    ws_causal = jnp.where(r >= c, ws, 0.0)
    svs = [_dot_nn(ws_causal, vg[n * B_CHUNK:(n + 1) * B_CHUNK]) + bias
           for n in range(rows // B_CHUNK)]
    return u * jnp.concatenate(svs, axis=0) * _silu(gb)


def _rope(x, cos_t, sin_t):
    return x * cos_t + pltpu.roll(x, 64, 1) * sin_t


def _rope_transpose(g, cos_t, sin_t):
    return g * cos_t + pltpu.roll(g * sin_t, 64, 1)


ANY_SPEC = pl.BlockSpec(memory_space=pl.ANY)


def _live(deps):
    return [d for d in deps if d is not None]


def _skip_deps(body, n_in, n_deps):
    def wrapped(*refs):
        return body(*refs[:n_in], *refs[n_in + n_deps:])
    return wrapped


def _pure_call(name, fn, grid, in_specs, out_specs, out_shape, args, n_acc=0, deps=()):
    deps = _live(deps)
    n_in, n_out, n_deps = len(in_specs), len(out_specs), len(deps)
    in_specs = list(in_specs) + [ANY_SPEC] * n_deps
    args = tuple(args) + tuple(deps)

    def body(*refs):
        res = fn(*[r[...] for r in refs[:n_in]])
        if not isinstance(res, (tuple, list)):
            res = (res,)
        outs = refs[n_in + n_deps:n_in + n_deps + n_out]
        for o, r in zip(outs[:n_out - n_acc], res[:n_out - n_acc]):
            o[...] = r.astype(o.dtype)
        if n_acc:
            first = functools.reduce(jnp.logical_and, [pl.program_id(i) == 0 for i in range(len(grid))])
            for o, r in zip(outs[n_out - n_acc:], res[n_out - n_acc:]):
                @pl.when(first)
                def _(o=o, r=r):
                    o[...] = r.astype(o.dtype)

                @pl.when(jnp.logical_not(first))
                def _(o=o, r=r):
                    o[...] += r.astype(o.dtype)

    return pl.pallas_call(body, name=name, grid=grid, in_specs=in_specs, out_specs=out_specs,
                          out_shape=out_shape, compiler_params=_params(len(grid)))(*args)


def _sds(shape, dtype):
    return jax.ShapeDtypeStruct(shape, dtype)


def _row_spec(tm, width, col=0):
    return pl.BlockSpec((tm, width), lambda i, col=col: (i, col))


def _full_spec(shape):
    nd = len(shape)
    return pl.BlockSpec(shape, lambda *_: (0,) * nd)


def _mm_nn(name, a, b, out_dtype, tm, tn, deps=()):
    deps = _live(deps)
    m, k = a.shape
    j, _, n = b.shape
    per = n // tn

    def body(a_ref, b_ref, o_ref):
        o_ref[...] = _raw_nn(a_ref[...], b_ref[...]).astype(o_ref.dtype)

    return pl.pallas_call(
        _skip_deps(body, 2, len(deps)), name=name, grid=(m // tm, j * per),
        in_specs=[pl.BlockSpec((tm, k), lambda i, c: (i, 0)),
                  pl.BlockSpec((None, k, tn), lambda i, c: (c // per, 0, c % per))] + [ANY_SPEC] * len(deps),
        out_specs=pl.BlockSpec((tm, tn), lambda i, c: (i, c)),
        out_shape=_sds((m, j * n), out_dtype), compiler_params=_params(2))(a, b, *deps)


def _mm_nt(name, a, b, out_dtype, tm, tn, deps=()):
    deps = _live(deps)
    m = a.shape[0]
    j, nn, n = b.shape

    def body(a_ref, b_ref, o_ref):
        b_all = b_ref[0] if j == 1 else jnp.concatenate([b_ref[s] for s in range(j)], axis=1)
        o_ref[...] = _raw_nt(a_ref[...], b_all).astype(o_ref.dtype)

    return pl.pallas_call(
        _skip_deps(body, 2, len(deps)), name=name, grid=(m // tm, nn // tn),
        in_specs=[pl.BlockSpec((tm, j * n), lambda i, c: (i, 0)),
                  pl.BlockSpec((j, tn, n), lambda i, c: (0, c, 0))] + [ANY_SPEC] * len(deps),
        out_specs=pl.BlockSpec((tm, tn), lambda i, c: (i, c)),
        out_shape=_sds((m, nn), out_dtype), compiler_params=_params(2))(a, b, *deps)


def _mm_tn(name, a, b, j, out_dtype, tm, tn, deps=()):
    deps = _live(deps)
    k, m = a.shape
    n = b.shape[1] // j
    per = n // tn

    def body(a_ref, b_ref, o_ref):
        o_ref[...] = _raw_tn(a_ref[...], b_ref[...]).astype(o_ref.dtype)

    return pl.pallas_call(
        _skip_deps(body, 2, len(deps)), name=name, grid=(m // tm, j * per),
        in_specs=[pl.BlockSpec((k, tm), lambda i, c: (0, i)),
                  pl.BlockSpec((k, tn), lambda i, c: (0, c))] + [ANY_SPEC] * len(deps),
        out_specs=pl.BlockSpec((None, tm, tn), lambda i, c: (c // per, i, c % per)),
        out_shape=_sds((j, m, n), out_dtype), compiler_params=_params(2))(a, b, *deps)


def _mm_tn_parity(name, a, b, j, parity, out_dtype, tm, deps=()):
    deps = _live(deps)
    k, m = a.shape
    n = b.shape[1] // j

    def body(par_ref, a_ref, b_ref, o_ref):
        del par_ref
        o_ref[...] = _raw_tn(a_ref[...], b_ref[...]).astype(o_ref.dtype)

    grid_spec = pltpu.PrefetchScalarGridSpec(
        num_scalar_prefetch=1, grid=(m // tm, j // 2),
        in_specs=[pl.BlockSpec((k, tm), lambda i, s, par: (0, i)),
                  pl.BlockSpec((k, n), lambda i, s, par: (0, 2 * s + par[0]))] + [ANY_SPEC] * len(deps),
        out_specs=pl.BlockSpec((None, tm, n), lambda i, s, par: (s, i, 0)))
    return pl.pallas_call(
        lambda par_ref, *refs: _skip_deps(functools.partial(body, par_ref), 2, len(deps))(*refs),
        name=name, grid_spec=grid_spec, out_shape=_sds((j // 2, m, n), out_dtype),
        compiler_params=_params(2))(parity, a, b, *deps)


TM = 256


def _pre_norm(name, x, w_row, deps=()):
    def fn(xv, w):
        return _rms(xv, w)
    return _pure_call(name, fn, (T // TM,), [_row_spec(TM, D), _full_spec((1, D))],
                      [_row_spec(TM, D)], [_sds((T, D), BF16)], (x, w_row), deps=deps)[0]


def _post_pre_norm(x, y, w_post, w_pre):
    def fn(xv, yv, wp, wn):
        x1 = xv + _rms(yv, wp)
        return x1, _rms(x1, wn)
    return _pure_call("post_pre_norm", fn, (T // TM,),
                      [_row_spec(TM, D), _row_spec(TM, D), _full_spec((1, D)), _full_spec((1, D))],
                      [_row_spec(TM, D), _row_spec(TM, D)],
                      [_sds((T, D), F32), _sds((T, D), BF16)], (x, y, w_post, w_pre))


def _in_proj_norms_bwd(dz, w_t, y, x1, w_post, w_pre, dx1_in, deps=()):
    def fn(dzv, w, yv, x1v, wp, wn, dx1v):
        _, vjp_pre = jax.vjp(_rms, x1v, wn)
        dx1_h, dwn = vjp_pre(_raw_nn(dzv, w[0]))
        dx1 = dx1v + dx1_h
        _, vjp_post = jax.vjp(_rms, yv, wp)
        dy, dwp = vjp_post(dx1)
        return dx1, dy, dwp, dwn
    return _pure_call("od_in_dx_norms", fn, (T // TM,),
                      [_row_spec(TM, ODD_IN_PAD), _full_spec((1, ODD_IN_PAD, D)), _row_spec(TM, D), _row_spec(TM, D),
                       _full_spec((1, D)), _full_spec((1, D)), _row_spec(TM, D)],
                      [_row_spec(TM, D), _row_spec(TM, D), _full_spec((1, D)), _full_spec((1, D))],
                      [_sds((T, D), F32), _sds((T, D), BF16), _sds((1, D), F32), _sds((1, D), F32)],
                      (dz, w_t, y, x1, w_post, w_pre, dx1_in), n_acc=2, deps=deps)


def _out_proj_loss(og, w_out, x1, w_post, target):
    tm = 512

    def fn(ogv, w, x1v, wp, tv):
        r, vjp = jax.vjp(_rms, _raw_nn(ogv, w[0]), wp)
        err = x1v + r - tv
        part = 0.5 * jnp.sum(jnp.mean(err * err, axis=-1, keepdims=True), axis=0, keepdims=True)
        dx2 = err * (1.0 / D)
        dy, dwp = vjp(dx2)
        return dx2, dy, jnp.broadcast_to(part, (1, 128)), dwp
    return _pure_call("od_out_loss", fn, (T // tm,),
                      [_row_spec(tm, D), _full_spec((1, D, D)), _row_spec(tm, D), _full_spec((1, D)), _row_spec(tm, D)],
                      [_row_spec(tm, D), _row_spec(tm, D), _full_spec((1, 128)), _full_spec((1, D))],
                      [_sds((T, D), F32), _sds((T, D), BF16), _sds((1, 128), F32), _sds((1, D), F32)],
                      (og, w_out, x1, w_post, target), n_acc=2)


def _pre_norm_bwd(x, w_row, dh, dx_res, deps=()):
    def fn(xv, w, dhv, dxv):
        _, vjp = jax.vjp(_rms, xv, w)
        dx, dw = vjp(dhv)
        return dxv + dx, dw
    return _pure_call("pre_norm_bwd", fn, (T // TM,),
                      [_row_spec(TM, D), _full_spec((1, D)), _row_spec(TM, D), _row_spec(TM, D)],
                      [_row_spec(TM, D), _full_spec((1, D))],
                      [_sds((T, D), F32), _sds((1, D), F32)], (x, w_row, dh, dx_res), n_acc=1, deps=deps)


RA = 256


def _head(ref, hh):
    return ref[:, hh * HD:(hh + 1) * HD]


def _z_part(z_ref, k, h):
    lo = (k * A_HEADS + h) * HD
    return z_ref[:, lo:lo + HD]


def _even_specs():
    return [_full_spec((1, A_HEADS * HD)), _full_spec((1, A_HEADS * HD)), _full_spec((1, HD)),
            _full_spec((1, B_GROUPS * HD)), _full_spec((1, B_GROUPS * HD)),
            _full_spec((B_GROUPS, B_CHUNK, B_CHUNK)), _full_spec((B_GROUPS, B_CHUNK, 1))]


def _even_fwd(z, l0, l1, onorm, lnw, lnb, ws, bias):
    nb = T // RA

    def body(z_ref, l0_ref, l1_ref, on_ref, lnw_ref, lnb_ref, ws_ref, bias_ref, cat_ref, sst_ref, st_scr):
        @pl.when(pl.program_id(0) == 0)
        def _():
            st_scr[...] = jnp.zeros_like(st_scr)

        for hh in range(A_HEADS):
            st = st_scr[hh]
            sst_ref[hh] = st
            out, st_new = _hgrn2_rows(_z_part(z_ref, 0, hh), _z_part(z_ref, 1, hh), _z_part(z_ref, 2, hh),
                                      _z_part(z_ref, 3, hh), st, _head(l0_ref, hh), _head(l1_ref, hh), on_ref[...])
            cat_ref[:, hh * HD:(hh + 1) * HD] = out.astype(cat_ref.dtype)
            st_scr[hh] = st_new
        for gg in range(B_GROUPS):
            out = _gmlp_rows(_z_part(z_ref, 4, gg), _z_part(z_ref, 5, gg), _z_part(z_ref, 6, gg),
                             _head(lnw_ref, gg), _head(lnb_ref, gg), ws_ref[gg], bias_ref[gg])
            cat_ref[:, (A_HEADS + gg) * HD:(A_HEADS + gg + 1) * HD] = out.astype(cat_ref.dtype)

    return pl.pallas_call(
        body, name="even_mixers_fwd", grid=(nb,),
        in_specs=[pl.BlockSpec((RA, EVEN_IN), lambda r: (r, 0))] + _even_specs(),
        out_specs=[pl.BlockSpec((RA, 2 * A_HEADS * HD), lambda r: (r, 0)),
                   pl.BlockSpec((A_HEADS, None, HD, HD), lambda r: (0, r, 0, 0))],
        out_shape=[_sds((T, 2 * A_HEADS * HD), BF16), _sds((A_HEADS, nb, HD, HD), F32)],
        scratch_shapes=[pltpu.VMEM((A_HEADS, HD, HD), F32)],
        compiler_params=_params(1))(z, l0, l1, onorm, lnw, lnb, ws, bias)


def _even_bwd(z, l0, l1, onorm, lnw, lnb, ws, bias, sst, dy, w_out, deps=()):
    nb = T // RA
    deps = _live(deps)

    def body(z_ref, l0_ref, l1_ref, on_ref, lnw_ref, lnb_ref, ws_ref, bias_ref, sst_ref, dy_ref, w_ref,
             dz_ref, dl0_ref, dl1_ref, don_ref, dlnw_ref, dlnb_ref, dws_ref, dbias_ref, ds_scr):
        first = pl.program_id(0) == 0

        @pl.when(first)
        def _():
            ds_scr[...] = jnp.zeros_like(ds_scr)

        dcat = _raw_nt(dy_ref[...], w_ref[0])

        def put(k, h, val):
            lo = (k * A_HEADS + h) * HD
            dz_ref[:, lo:lo + HD] = val.astype(dz_ref.dtype)

        sums = []
        don = None
        for hh in range(A_HEADS):
            lanes = slice(hh * HD, (hh + 1) * HD)
            _, vjp = jax.vjp(_hgrn2_rows, _z_part(z_ref, 0, hh), _z_part(z_ref, 1, hh), _z_part(z_ref, 2, hh),
                             _z_part(z_ref, 3, hh), sst_ref[hh], _head(l0_ref, hh), _head(l1_ref, hh), on_ref[...])
            dq, dzf, dv, dga, dst, dl0, dl1, don_h = vjp((dcat[:, lanes], ds_scr[hh]))
            for k, val in enumerate((dq, dzf, dv, dga)):
                put(k, hh, val)
            ds_scr[hh] = dst
            sums += [(dl0_ref, (slice(None), lanes), dl0), (dl1_ref, (slice(None), lanes), dl1)]
            don = don_h if don is None else don + don_h
        sums.append((don_ref, slice(None), don))
        for gg in range(B_GROUPS):
            lanes = slice(gg * HD, (gg + 1) * HD)
            _, vjp = jax.vjp(_gmlp_rows, _z_part(z_ref, 4, gg), _z_part(z_ref, 5, gg), _z_part(z_ref, 6, gg),
                             _head(lnw_ref, gg), _head(lnb_ref, gg), ws_ref[gg], bias_ref[gg])
            du, dv, dg, dlnw, dlnb, dws, dbias = vjp(dcat[:, (A_HEADS + gg) * HD:(A_HEADS + gg + 1) * HD])
            for k, val in enumerate((du, dv, dg)):
                put(4 + k, gg, val)
            sums += [(dlnw_ref, (slice(None), lanes), dlnw), (dlnb_ref, (slice(None), lanes), dlnb),
                     (dws_ref, gg, dws), (dbias_ref, gg, dbias)]
        for ref, idx, val in sums:
            @pl.when(first)
            def _(ref=ref, idx=idx, val=val):
                ref[idx] = val

            @pl.when(jnp.logical_not(first))
            def _(ref=ref, idx=idx, val=val):
                ref[idx] += val

    small = _even_specs()
    return pl.pallas_call(
        _skip_deps(body, 11, len(deps)), name="even_mixers_bwd", grid=(nb,),
        in_specs=[pl.BlockSpec((RA, EVEN_IN), lambda r: (nb - 1 - r, 0))] + small
        + [pl.BlockSpec((A_HEADS, None, HD, HD), lambda r: (0, nb - 1 - r, 0, 0)),
           pl.BlockSpec((RA, D), lambda r: (nb - 1 - r, 0)), _full_spec((1, 2 * A_HEADS * HD, D))]
        + [ANY_SPEC] * len(deps),
        out_specs=[pl.BlockSpec((RA, EVEN_IN), lambda r: (nb - 1 - r, 0))] + small,
        out_shape=[_sds((T, EVEN_IN), BF16), _sds((1, A_HEADS * HD), F32), _sds((1, A_HEADS * HD), F32),
                   _sds((1, HD), F32), _sds((1, B_GROUPS * HD), F32), _sds((1, B_GROUPS * HD), F32),
                   _sds((B_GROUPS, B_CHUNK, B_CHUNK), F32), _sds((B_GROUPS, B_CHUNK, 1), F32)],
        scratch_shapes=[pltpu.VMEM((A_HEADS, HD, HD), F32)],
        compiler_params=_params(1))(z, l0, l1, onorm, lnw, lnb, ws, bias, sst, dy, w_out, *deps)


def _mla_pre(z1, qn, kvn, cos_t, sin_t):
    def fn(cq, ckv, kpe, cs, sn, wq, wkv):
        return _rms(cq, wq), _rms(ckv, wkv), _rope(kpe, cs, sn)
    return _pure_call("mla_pre", fn, (T // TM,),
                      [_row_spec(TM, C_RANK, 4), _row_spec(TM, C_RANK, 5), _row_spec(TM, HD, 24),
                       _row_spec(TM, HD), _row_spec(TM, HD),
                       _full_spec((1, C_RANK)), _full_spec((1, C_RANK))],
                      [_row_spec(TM, C_RANK), _row_spec(TM, C_RANK), _row_spec(TM, HD)],
                      [_sds((T, C_RANK), BF16), _sds((T, C_RANK), BF16), _sds((T, HD), BF16)],
                      (z1, z1, z1, cos_t, sin_t, qn, kvn))


def _mla_pre_bwd(z1, qn, kvn, cos_t, sin_t, dq, w_qb, dkv, w_kvb, dkp, dgate, deps=()):
    def fn(cq, ckv, cs, sn, wq, wkv, dqv, wqb, dkvv, wkvb, g_kp, g_gate):
        _, vjp_q = jax.vjp(_rms, cq, wq)
        dcq, dwq = vjp_q(_raw_nt(dqv, wqb[0]))
        _, vjp_kv = jax.vjp(_rms, ckv, wkv)
        dckv, dwkv = vjp_kv(_raw_nt(dkvv, jnp.concatenate([wkvb[s] for s in range(N_DEV)], axis=1)))
        dz1 = jnp.concatenate([g_gate, dcq.astype(BF16), dckv.astype(BF16),
                               _rope_transpose(g_kp, cs, sn).astype(BF16)], axis=1)
        return dz1, dwq, dwkv
    return _pure_call("mla_pre_bwd", fn, (T // TM,),
                      [_row_spec(TM, C_RANK, 4), _row_spec(TM, C_RANK, 5),
                       _row_spec(TM, HD), _row_spec(TM, HD),
                       _full_spec((1, C_RANK)), _full_spec((1, C_RANK)),
                       _row_spec(TM, C_HEADS * QP), _full_spec((1, C_RANK, C_HEADS * QP)),
                       _row_spec(TM, C_HEADS * KVW), _full_spec((N_DEV, C_RANK, C_HEADS * KVW // N_DEV)),
                       _row_spec(TM, HD), _row_spec(TM, D)],
                      [_row_spec(TM, ODD_IN_PAD), _full_spec((1, C_RANK)), _full_spec((1, C_RANK))],
                      [_sds((T, ODD_IN_PAD), BF16), _sds((1, C_RANK), F32), _sds((1, C_RANK), F32)],
                      (z1, z1, cos_t, sin_t, qn, kvn, dq, w_qb, dkv, w_kvb, dkp, dgate), n_acc=2, deps=deps)


TQ = 256
HP = 2
KVW = C_NOPE + C_V


def _att_keys(kv_ref, kp_ref, k_scr):
    @pl.when(pl.program_id(1) == 0)
    def _():
        for hh in range(HP):
            k_scr[hh, :, 0:C_NOPE] = kv_ref[:, hh * KVW:hh * KVW + C_NOPE]
            k_scr[hh, :, C_NOPE:QP] = kp_ref[...]


def _att_scores(q, cos_ref, sin_ref, k_scr, hh, n):
    keys = (n + 1) * TQ
    qr = jnp.concatenate([q[:, :C_NOPE], _rope(q[:, C_NOPE:], cos_ref[...], sin_ref[...])], axis=1).astype(BF16)
    return qr, _raw_nt(qr, k_scr[hh, 0:keys, :]) * ATT_SCALE


def _causal(x, n, fill):
    row = lax.broadcasted_iota(jnp.int32, (TQ, TQ), 0)
    col = lax.broadcasted_iota(jnp.int32, (TQ, TQ), 1)
    diag = jnp.where(col <= row, x[:, n * TQ:], fill)
    return diag if n == 0 else jnp.concatenate([x[:, :n * TQ], diag], axis=1)


def _per_query_block(fn):
    for n in range(T // TQ):
        pl.when(pl.program_id(1) == n)(functools.partial(fn, n))


def _att_in_specs():
    return [pl.BlockSpec((TQ, HP * QP), lambda g, i: (i, g)),
            pl.BlockSpec((TQ, HD), lambda g, i: (i, 0)),
            pl.BlockSpec((TQ, HD), lambda g, i: (i, 0)),
            pl.BlockSpec((T, HP * KVW), lambda g, i: (0, g)),
            pl.BlockSpec((T, HD), lambda g, i: (0, 0))]


def _attention_fwd(q, cos_t, sin_t, kv, kp, z1):
    def body(q_ref, cos_ref, sin_ref, kv_ref, kp_ref, gate_ref, o_ref, lse_ref, og_ref, k_scr):
        _att_keys(kv_ref, kp_ref, k_scr)

        def block(n):
            keys = (n + 1) * TQ
            for hh in range(HP):
                _, s = _att_scores(q_ref[:, hh * QP:(hh + 1) * QP], cos_ref, sin_ref, k_scr, hh, n)
                s = _causal(s, n, jnp.finfo(F32).min)
                m = jnp.max(s, axis=-1, keepdims=True)
                p = jnp.exp(s - m)
                l = jnp.sum(p, axis=-1, keepdims=True)
                v = kv_ref[0:keys, hh * KVW + C_NOPE:(hh + 1) * KVW]
                o = _raw_nn(p, v) / l
                lanes = slice(hh * C_V, (hh + 1) * C_V)
                o_ref[:, lanes] = o
                og_ref[:, lanes] = (o * _silu(gate_ref[:, lanes])).astype(og_ref.dtype)
                lse_ref[hh] = m + jnp.log(l)

        _per_query_block(block)

    heads = pl.BlockSpec((TQ, HP * C_V), lambda g, i: (i, g))
    return pl.pallas_call(
        body, name="attention_fwd", grid=(C_HEADS // HP, T // TQ), in_specs=_att_in_specs() + [heads],
        out_specs=[heads, pl.BlockSpec((HP, TQ, 1), lambda g, i: (g, i, 0)), heads],
        out_shape=[_sds((T, C_HEADS * C_V), F32), _sds((C_HEADS, T, 1), F32), _sds((T, C_HEADS * C_V), BF16)],
        scratch_shapes=[pltpu.VMEM((HP, T, QP), BF16)],
        compiler_params=_params(2))(q, cos_t, sin_t, kv, kp, z1)


def _attention_bwd(q, cos_t, sin_t, kv, kp, o, lse, dog, z1):
    nq = T // TQ

    def body(q_ref, cos_ref, sin_ref, kv_ref, kp_ref, o_ref, lse_ref, dog_ref, gate_ref,
             dq_ref, dkv_ref, dkp_ref, dgate_ref, k_scr, dk_scr, dv_scr):
        g, i = pl.program_id(0), pl.program_id(1)
        _att_keys(kv_ref, kp_ref, k_scr)

        @pl.when(i == 0)
        def _():
            dv_scr[...] = jnp.zeros_like(dv_scr)
            dk_scr[...] = jnp.zeros_like(dk_scr)

        def block(n):
            keys = (n + 1) * TQ
            for hh in range(HP):
                qr, s = _att_scores(q_ref[:, hh * QP:(hh + 1) * QP], cos_ref, sin_ref, k_scr, hh, n)
                p = _causal(jnp.exp(s - lse_ref[hh]), n, 0.0)
                lanes = slice(hh * C_V, (hh + 1) * C_V)
                ov, gate, dogv = o_ref[:, lanes], gate_ref[:, lanes], dog_ref[:, lanes]
                sig = _sigmoid(gate)
                silu = gate * sig
                dov = dogv * silu
                dgate_ref[:, lanes] = (dogv * ov * (sig + silu * (1.0 - sig))).astype(dgate_ref.dtype)
                delta = jnp.sum(dov * ov, axis=-1, keepdims=True)
                dp = _raw_nt(dov, kv_ref[0:keys, hh * KVW + C_NOPE:(hh + 1) * KVW])
                ds = p * (dp - delta) * ATT_SCALE
                dq = _raw_nn(ds, k_scr[hh, 0:keys, :])
                dq_ref[:, hh * QP:(hh + 1) * QP] = jnp.concatenate(
                    [dq[:, :C_NOPE], _rope_transpose(dq[:, C_NOPE:], cos_ref[...], sin_ref[...])],
                    axis=1).astype(dq_ref.dtype)
                dv_scr[hh, 0:keys, :] += _raw_tn(p, dov)
                dk_scr[hh, 0:keys, :] += _raw_tn(ds, qr)

        _per_query_block(block)

        @pl.when(i == nq - 1)
        def _():
            for hh in range(HP):
                dkv_ref[:, hh * KVW:(hh + 1) * KVW] = jnp.concatenate(
                    [dk_scr[hh, :, 0:C_NOPE], dv_scr[hh]], axis=1).astype(dkv_ref.dtype)

        @pl.when(jnp.logical_and(i == nq - 1, g == 0))
        def _():
            dkp_ref[...] = dk_scr[0, :, C_NOPE:QP]

        @pl.when(jnp.logical_and(i == nq - 1, g > 0))
        def _():
            dkp_ref[...] += dk_scr[0, :, C_NOPE:QP]

        @pl.when(i == nq - 1)
        def _():
            for hh in range(1, HP):
                dkp_ref[...] += dk_scr[hh, :, C_NOPE:QP]

    heads = pl.BlockSpec((TQ, HP * C_V), lambda g, i: (i, g))
    return pl.pallas_call(
        body, name="attention_bwd", grid=(C_HEADS // HP, nq),
        in_specs=_att_in_specs() + [heads, pl.BlockSpec((HP, TQ, 1), lambda g, i: (g, i, 0)), heads, heads],
        out_specs=[pl.BlockSpec((TQ, HP * QP), lambda g, i: (i, g)),
                   pl.BlockSpec((T, HP * KVW), lambda g, i: (0, g)),
                   _full_spec((T, HD)), heads],
        out_shape=[_sds((T, C_HEADS * QP), BF16), _sds((T, C_HEADS * KVW), BF16), _sds((T, HD), F32),
                   _sds((T, C_HEADS * C_V), BF16)],
        scratch_shapes=[pltpu.VMEM((HP, T, QP), BF16), pltpu.VMEM((HP, T, QP), F32), pltpu.VMEM((HP, T, C_V), F32)],
        compiler_params=_params(2))(q, cos_t, sin_t, kv, kp, o, lse, dog, z1)


def _adamw_math(w, g, m, v):
    m = ADAM_B1 * m + (1.0 - ADAM_B1) * g
    v = ADAM_B2 * v + (1.0 - ADAM_B2) * (g * g)
    m_hat = m / (1.0 - ADAM_B1 ** ADAM_STEP)
    v_hat = v / (1.0 - ADAM_B2 ** ADAM_STEP)
    delta = -ADAM_LR * (m_hat / (jnp.sqrt(v_hat) + ADAM_EPS) + ADAM_WD * w)
    return delta, m, v


def _adamw(name, parts, w, m, v, tr, tc=None):
    rows, cols = w.shape

    def fn(*vals):
        pvs, (wv, mv, vv) = vals[:len(parts)], vals[len(parts):]
        g = None
        for pv in pvs:
            for d in range(pv.shape[0]):
                term = pv[d].astype(F32)
                g = term if g is None else g + term
        return (g,) + _adamw_math(wv, g, mv, vv)

    tc = cols if tc is None else tc
    blk = pl.BlockSpec((tr, tc), lambda i, j: (i, j))
    part_specs = [pl.BlockSpec((n, tr, tc), lambda i, j: (0, i, j)) for _, n in parts]
    return _pure_call(name, fn, (rows // tr, cols // tc), part_specs + [blk, blk, blk],
                      [blk] * 4, [_sds((rows, cols), F32)] * 4, tuple(p for p, _ in parts) + (w, m, v))


SMALL_PARAM_SHAPES = ((2, D), (2, D), (2, A_HEADS * HD), (1, HD), (1, B_GROUPS * HD), (1, B_GROUPS * HD),
                      (B_GROUPS, B_CHUNK, B_CHUNK), (B_GROUPS, B_CHUNK))
SMALL_PIECES = ((0, 0, 0, 0), (0, 1, 1, 0), (1, 0, 1, 1), (1, 1, 1, 2), (2, 0, 2, 0), (2, 1, 2, 1),
                (3, 0, 3, 8), (4, 0, 2, 2), (5, 0, 2, 3))


def _small_rows(dnpre1, dnpost0, dnpost1, dl0, dl1, donorm, dlnw, dlnb, dws, dbias, dqn, dkvn, loss_part):
    return [jnp.concatenate([dnpre1, dnpost0, dnpost1], axis=0),
            jnp.concatenate([dl0, dl1, dlnw, dlnb], axis=0),
            jnp.concatenate([dbias.reshape(B_GROUPS, B_CHUNK), donorm, loss_part], axis=0),
            dws,
            jnp.concatenate([dqn, dkvn], axis=0)]


def _adamw_small(late_all, early_all, wmv):
    n_in = 6 + 3 * len(wmv)

    def body(*refs):
        gathered, params, outs = refs[:6], refs[6:n_in], refs[n_in:]

        def total(ref):
            s = ref[0]
            for d in range(1, N_DEV):
                s = s + ref[d]
            return s

        g_late, g2048, g1024, g128, g_ws, g512 = [total(r) for r in gathered]
        arrays = (g_late, g2048, g1024, g128)

        def update(p, rows, g):
            w_ref, m_ref, v_ref = params[3 * p:3 * p + 3]
            delta, m, v = _adamw_math(w_ref[rows], g, m_ref[rows], v_ref[rows])
            for out, val in zip(outs[4 * p:4 * p + 4], (g, delta, m, v)):
                out[rows] = val

        for p, row, arr, arr_row in SMALL_PIECES:
            update(p, pl.ds(row, 1), arrays[arr][arr_row:arr_row + 1])
        update(6, slice(None), g_ws)
        update(7, slice(None), g128[0:B_GROUPS])
        outs[32][...] = g128[B_GROUPS + 1:B_GROUPS + 2]
        outs[33][...] = g512

    vmem = pl.BlockSpec(memory_space=pltpu.VMEM)
    flat = [a for t in wmv for a in t]
    out_shape = [_sds(s, F32) for s in SMALL_PARAM_SHAPES for _ in range(4)] + [_sds((1, 128), F32), _sds((2, C_RANK), F32)]
    res = pl.pallas_call(body, name="adamw_small", in_specs=[vmem] * n_in, out_specs=[vmem] * len(out_shape),
                         out_shape=out_shape,
                         compiler_params=pltpu.CompilerParams(vmem_limit_bytes=VMEM_LIMIT_V7X))(late_all, *early_all, *flat)
    return [res[4 * p:4 * p + 4] for p in range(8)], res[32], res[33]


def _exchange(name, arrs, gather, deps=()):
    n = len(arrs)
    deps = _live(deps)

    def body(*refs):
        ins, outs = refs[:n], refs[n + len(deps):2 * n + len(deps)]
        send_sems, recv_sems, local_sems = refs[2 * n + len(deps):]
        x, y, c = lax.axis_index("x"), lax.axis_index("y"), lax.axis_index("c")
        me = 4 * x + 2 * y + c

        def peer(k):
            return (x ^ (k >> 2), y ^ ((k >> 1) & 1), c ^ (k & 1))

        def copy(a, k):
            src = ins[a] if gather else ins[a].at[me ^ k]
            return pltpu.make_async_remote_copy(
                src_ref=src, dst_ref=outs[a].at[me], send_sem=send_sems.at[a, k - 1],
                recv_sem=recv_sems.at[a, k - 1], device_id=peer(k), device_id_type=MESH_ID)

        def arrival(a, k):
            src = ins[a] if gather else ins[a].at[me]
            return pltpu.make_async_remote_copy(
                src_ref=src, dst_ref=outs[a].at[me ^ k], send_sem=send_sems.at[a, k - 1],
                recv_sem=recv_sems.at[a, k - 1], device_id=peer(k), device_id_type=MESH_ID)

        own = [pltpu.make_async_copy(ins[a] if gather else ins[a].at[me], outs[a].at[me], local_sems.at[a])
               for a in range(n)]
        for cp in own:
            cp.start()
        for k in range(1, N_DEV):
            for a in range(n):
                copy(a, k).start()
        for k in range(1, N_DEV):
            for a in range(n):
                arrival(a, k).wait_recv()
        for k in range(1, N_DEV):
            for a in range(n):
                copy(a, k).wait_send()
        for cp in own:
            cp.wait()

    any_spec = pl.BlockSpec(memory_space=pl.ANY)
    out_shape = [_sds((N_DEV,) + a.shape if gather else a.shape, a.dtype) for a in arrs]
    return pl.pallas_call(
        body, name=name, in_specs=[any_spec] * (n + len(deps)), out_specs=[any_spec] * n, out_shape=out_shape,
        scratch_shapes=[pltpu.SemaphoreType.DMA((n, N_DEV - 1)), pltpu.SemaphoreType.DMA((n, N_DEV - 1)),
                        pltpu.SemaphoreType.DMA((n,))],
        compiler_params=pltpu.CompilerParams(has_side_effects=True))(*arrs, *deps)


HBM_SPEC = pl.BlockSpec(memory_space=pltpu.HBM)
SEM_SPEC = pl.BlockSpec(memory_space=pltpu.SEMAPHORE)
DATAFLOW = pltpu.SideEffectType.DATAFLOW_SIDE_EFFECTING


def _my_index():
    return 4 * lax.axis_index("x") + 2 * lax.axis_index("y") + lax.axis_index("c")


def _plan_copies(plan, refs, send_sems, recv_sems):
    x, y, c = lax.axis_index("x"), lax.axis_index("y"), lax.axis_index("c")
    return [pltpu.make_async_remote_copy(
        src_ref=src, dst_ref=dst, send_sem=send_sems.at[i], recv_sem=recv_sems.at[i],
        device_id=(x ^ (k >> 2), y ^ ((k >> 1) & 1), c ^ (k & 1)), device_id_type=MESH_ID)
        for i, (src, dst, k) in enumerate(plan(refs, 4 * x + 2 * y + c))]


def _split_call(name, bufs, waits=None, starts=None, deps=()):
    n = len(bufs)
    deps = _live(deps)
    n_wait = 2 if waits else 0

    def body(*refs):
        zones = refs[:n]
        if waits:
            for cp in _plan_copies(waits[2], zones, refs[n], refs[n + 1]):
                cp.wait_send()
                cp.wait_recv()
        if starts:
            first_out = n + n_wait + len(deps)
            for cp in _plan_copies(starts[0], zones, refs[first_out], refs[first_out + 1]):
                cp.start()
            refs[-1][...] = jnp.zeros_like(refs[-1])

    out_specs, out_shape = [], []
    if starts:
        sems = pltpu.SemaphoreType.DMA((starts[1],))
        out_specs, out_shape = [SEM_SPEC, SEM_SPEC], [sems, sems]
    out_specs += [HBM_SPEC] * n
    out_shape += [pltpu.HBM(b.shape, b.dtype) for b in bufs]
    if starts:
        out_specs.append(pl.BlockSpec(memory_space=pltpu.VMEM))
        out_shape.append(_sds((8, 128), F32))
    first_buf = 2 if starts else 0
    res = pl.pallas_call(
        body, name=name,
        in_specs=[HBM_SPEC] * n + [SEM_SPEC] * n_wait + [ANY_SPEC] * len(deps),
        out_specs=out_specs, out_shape=out_shape,
        input_output_aliases={i: first_buf + i for i in range(n)},
        compiler_params=pltpu.CompilerParams(has_side_effects=DATAFLOW),
    )(*[pltpu.with_memory_space_constraint(b, pltpu.HBM) for b in bufs], *(waits[:2] if waits else ()), *deps)
    out_bufs = list(res[first_buf:first_buf + n])
    return out_bufs, ((res[0], res[1]) if starts else None), (res[-1] if starts else None)


def _direct_plan(n, gather):
    def plan(refs, me):
        return [(refs[a] if gather else refs[a].at[me ^ k], refs[n + a].at[me], k)
                for k in range(1, N_DEV) for a in range(n)]
    return plan


def _own_slot_filled(a, gather):
    me = _my_index()
    if gather:
        return lax.dynamic_update_slice_in_dim(lax.empty((N_DEV,) + a.shape, a.dtype), a[None], me, 0)
    return lax.dynamic_update_slice_in_dim(lax.empty(a.shape, a.dtype), lax.dynamic_slice_in_dim(a, me, 1, 0), me, 0)


def _exchange_start(name, arrs, gather, deps=()):
    n = len(arrs)
    lands = [_own_slot_filled(a, gather) for a in arrs]
    plan = _direct_plan(n, gather)
    bufs, sems, token = _split_call(name, list(arrs) + lands, starts=(plan, n * (N_DEV - 1)), deps=deps)
    return (n, plan, sems, bufs, None), token


def _exchange_wait(name, handle, after):
    return _split_done(name, handle, after)


ICI_PEERS = (2, 4, 6)
SIBLING = 1


def _gather2_send(name, arrs, deps=(), fill_after=False):
    n = len(arrs)
    lands = [lax.empty((N_DEV,) + a.shape, a.dtype) if fill_after else _own_slot_filled(a, True) for a in arrs]

    def plan(refs, me_):
        return [(refs[a], refs[n + a].at[me_], k) for k in (SIBLING,) + ICI_PEERS for a in range(n)]

    bufs, sems, token = _split_call(name, list(arrs) + lands, starts=(plan, 4 * n), deps=deps)
    if fill_after:
        me = _my_index()
        bufs = bufs[:n] + [lax.dynamic_update_slice_in_dim(z, a[None], me, 0) for z, a in zip(bufs[n:], bufs[:n])]
    return (n, plan, sems, bufs, None), token


def _gather2_relay(name, handle, after):
    n, plan, sems, bufs, _ = handle
    after = after if isinstance(after, (list, tuple)) else [after]

    def relay(refs, me_):
        return [(refs[n + a].at[me_ ^ k], refs[n + a].at[me_ ^ k], SIBLING) for k in ICI_PEERS for a in range(n)]

    bufs, sems2, token = _split_call(name, bufs, waits=(sems[0], sems[1], plan), starts=(relay, 3 * n), deps=after)
    return (n, relay, sems2, bufs, None), token


def _split_done(name, handle, after, all_bufs=False):
    n, plan, sems, bufs, _ = handle
    bufs, _, _ = _split_call(name, bufs, waits=(sems[0], sems[1], plan), deps=[after])
    return bufs if all_bufs else bufs[n:]


def _scatter2_pair(name, for_sibling, deps=()):
    n = len(for_sibling)
    pairs = [lax.empty(s.shape, s.dtype) for s in for_sibling]

    def plan(refs, me):
        del me
        return [(refs[a].at[s], refs[n + a].at[s], SIBLING) for s in range(4) for a in range(n)]

    bufs, sems, token = _split_call(name, list(for_sibling) + pairs, starts=(plan, 4 * n), deps=deps)
    return (n, plan, sems, bufs, None), token


def _pair_add(name, mine, pair):
    _, rows, cols = mine.shape
    tr = rows // 2

    def fn(a, b):
        return a.astype(F32) + b.astype(F32)

    blk = pl.BlockSpec((None, tr, cols), lambda s, i: (s, i, 0))
    return _pure_call(name, fn, (4, rows // tr), [blk, blk], [blk], [_sds(mine.shape, mine.dtype)], (mine, pair))[0]


def _scatter2_send(name, chip_sums, deps=()):
    n = len(chip_sums)
    finals = [lax.empty((3,) + c.shape[1:], c.dtype) for c in chip_sums]

    def plan(refs, me):
        return [(refs[a].at[(me >> 1) ^ j], refs[n + a].at[j - 1], 2 * j) for j in range(1, 4) for a in range(n)]

    bufs, sems, token = _split_call(name, list(chip_sums) + finals, starts=(plan, 3 * n), deps=deps)
    return (n, plan, sems, bufs, None), token


def _pad_rope(p):
    z = jnp.zeros(p.shape[:-1] + (32,), p.dtype)
    return jnp.concatenate([p[..., :32], z, p[..., 32:], z], axis=-1)


def _unpad_rope(p):
    return jnp.concatenate([p[..., :32], p[..., 64:96]], axis=-1)


def _odd_in_layout(wt):
    wt = wt.reshape(ODD_IN, D)
    cq, ckv, kpe, gate = wt[:512], wt[512:1024], wt[1024:1088], wt[1088:]
    z = jnp.zeros((32, D), wt.dtype)
    return jnp.concatenate([gate, cq, ckv, kpe[:32], z, kpe[32:], z], axis=0)


def _odd_in_unlayout(dwt):
    gate, cq, ckv, kpe = dwt[:2048], dwt[2048:2560], dwt[2560:3072], dwt[3072:]
    wt = jnp.concatenate([cq, ckv, kpe[:32], kpe[64:96], gate], axis=0)
    return wt.reshape(N_DEV, ODD_IN // N_DEV, D)


def _qb_layout(w):
    w = w.transpose(1, 0, 2).reshape(C_RANK, C_HEADS, C_QK)
    w = jnp.concatenate([w[..., :C_NOPE], _pad_rope(w[..., C_NOPE:])], axis=-1)
    return w.reshape(C_RANK, C_HEADS * QP)


def _qb_unlayout(dw):
    dw = dw.reshape(C_RANK, C_HEADS, QP)
    dw = jnp.concatenate([dw[..., :C_NOPE], _unpad_rope(dw[..., C_NOPE:])], axis=-1)
    return dw.reshape(C_RANK, N_DEV, C_HEADS * C_QK // N_DEV).transpose(1, 0, 2)


def _rope_tables(positions):
    inv_freq = ROPE_THETA ** (-jnp.arange(0, C_ROPE, 2, dtype=F32) / C_ROPE)
    ang = positions.astype(F32)[0][:, None] * inv_freq
    cos, sin = jnp.cos(ang), jnp.sin(ang)
    z = jnp.zeros_like(cos)
    return jnp.concatenate([cos, z, cos, z], axis=1), jnp.concatenate([-sin, z, sin, z], axis=1)


def _forward_backward(x, cos_t, sin_t, target, norm_pre, norm_post, lb_logits, a_onorm, ln_w, ln_b,
                      b_ws, b_bias, get_w, put_g, put_small=None, start_dep=None):
    npre0, npre1 = norm_pre[0:1], norm_pre[1:2]
    npost0, npost1 = norm_post[0:1], norm_post[1:2]
    l0, l1 = lb_logits[0:1], lb_logits[1:2]
    bias_col = b_bias.reshape(B_GROUPS, B_CHUNK, 1)
    ws = b_ws.reshape(B_GROUPS, B_CHUNK, B_CHUNK)

    h0 = _pre_norm("pre_norm0", x, npre0, deps=[start_dep])
    w_ev_in = get_w("ev_in", h0)
    z0 = _mm_nn("ev_in", h0, w_ev_in, F32, 1024, 896)
    cat, sst = _even_fwd(z0, l0, l1, a_onorm, ln_w, ln_b, ws, bias_col)
    w_ev_out = get_w("ev_out", cat)
    y0 = _mm_nn("ev_out", cat, w_ev_out, F32, 1024, 1024)
    get_w("od_relay", y0)
    x1, h1 = _post_pre_norm(x, y0, npost0, npre1)
    w_od_in, w_qb, w_kvb, q_norm, kv_norm = get_w("od_mid", h1)
    z1 = _mm_nt("od_in", h1, w_od_in[None], F32, 1024, 640)
    cqn, ckvn, kp = _mla_pre(z1, q_norm, kv_norm, cos_t, sin_t)
    q = _mm_nn("od_qb", cqn, w_qb[None], F32, 1024, 1024)
    kv = _mm_nn("od_kvb", ckvn, w_kvb, BF16, 1024, 512)
    o, lse, og = _attention_fwd(q, cos_t, sin_t, kv, kp, z1)
    w_od_out = get_w("od_out", og)
    dx2, dy1, loss_part, dnpost1 = _out_proj_loss(og, w_od_out, x1, npost1, target)

    g_od_out = _mm_tn("od_out_dw", og, dy1, 1, BF16, 1024, 1024)
    tok = put_g("od_out", [g_od_out.reshape(N_DEV, D // N_DEV, D)])
    dog = _mm_nt("od_out_dx", dy1, w_od_out, F32, 1024, 1024, deps=[tok])
    dq, dkv, dkp, dgate = _attention_bwd(q, cos_t, sin_t, kv, kp, o, lse, dog, z1)
    g_qb = _mm_tn("od_qb_dw", cqn, dq, 1, F32, 512, 1024)
    g_kvb = _mm_tn("od_kvb_dw", ckvn, dkv, N_DEV, BF16, 512, 512)
    tok = put_g("od_qkv", [_qb_unlayout(g_qb[0]).astype(BF16), g_kvb])
    dz1, dqn, dkvn = _mla_pre_bwd(z1, q_norm, kv_norm, cos_t, sin_t, dq, w_qb[None], dkv, w_kvb, dkp, dgate,
                                  deps=[tok])
    g_od_in = _mm_tn("od_in_dw", dz1, h1, 1, F32, 640, 1024)
    tok = put_g("od_in", [_odd_in_unlayout(g_od_in[0]).astype(BF16)])
    dx1, dy0, dnpost0, dnpre1 = _in_proj_norms_bwd(dz1, w_od_in[None], y0, x1, npost0, npre1, dx2, deps=[tok])

    g_ev_out = _mm_tn("ev_out_dw", cat, dy0, 1, BF16, 1024, 1024)
    tok = put_g("ev_out", [g_ev_out.reshape(N_DEV, D // N_DEV, D)])
    dz0, dl0, dl1, donorm, dlnw, dlnb, dws, dbias = _even_bwd(z0, l0, l1, a_onorm, ln_w, ln_b, ws, bias_col, sst,
                                                              dy0, w_ev_out, deps=[tok])
    early = _small_rows(dnpre1, dnpost0, dnpost1, dl0, dl1, donorm, dlnw, dlnb, dws, dbias, dqn, dkvn, loss_part)
    tok = put_small(early) if put_small else None
    small_tok = tok

    def ev_in_half(name, parity, deps=()):
        return _mm_tn_parity(name, h0, dz0, N_DEV, parity, BF16, 1024, deps=[small_tok] + list(deps))

    tok = put_g("ev_in", ev_in_half)
    dh0 = _mm_nt("ev_in_dx", dz0, w_ev_in, F32, 1024, 256, deps=[tok])
    grad_x, dnpre0 = _pre_norm_bwd(x, npre0, dh0, dx1)
    return grad_x, early, dnpre0


def kernel(x, positions, norm_pre, norm_post, ev_w_in, ev_lb_logits, ev_a_onorm, ev_b_ln_w, ev_b_ln_b, ev_b_ws, ev_b_bias, ev_w_out, od_w_in, od_q_norm, od_w_qb, od_kv_norm, od_w_kvb, od_w_out, loss_target, m_norm_pre, m_norm_post, m_ev_w_in, m_ev_lb_logits, m_ev_a_onorm, m_ev_b_ln_w, m_ev_b_ln_b, m_ev_b_ws, m_ev_b_bias, m_ev_w_out, m_od_w_in, m_od_q_norm, m_od_w_qb, m_od_kv_norm, m_od_w_kvb, m_od_w_out, v_norm_pre, v_norm_post, v_ev_w_in, v_ev_lb_logits, v_ev_a_onorm, v_ev_b_ln_w, v_ev_b_ln_b, v_ev_b_ws, v_ev_b_bias, v_ev_w_out, v_od_w_in, v_od_q_norm, v_od_w_qb, v_od_kv_norm, v_od_w_kvb, v_od_w_out):
    me = 4 * lax.axis_index("x") + 2 * lax.axis_index("y") + lax.axis_index("c")
    bf = lambda w: w[0].astype(BF16)

    norms = jnp.pad(jnp.concatenate([od_q_norm, od_kv_norm], axis=1), ((0, 7), (0, 0)))
    sent = {}
    sent["ev_in"], tok = _gather2_send("gather_ev_in", [bf(ev_w_in)], fill_after=True)
    sent["ev_out"], tok = _gather2_send("gather_ev_out", [bf(ev_w_out)], deps=[tok])
    sent["od"], tok = _gather2_send("gather_od", [od_w_in[0].T.astype(BF16), bf(od_w_qb), bf(od_w_kvb), norms,
                                                 bf(od_w_out)], deps=[tok])
    cos_t, sin_t = _rope_tables(positions)
    od = []

    def get_w(group, after):
        if group == "ev_in":
            relayed, token = _gather2_relay("relay_ev_in", sent["ev_in"], [after, cos_t, sin_t])
            return _split_done("arrived_ev_in", relayed, token)[0]
        if group == "ev_out":
            relayed, token = _gather2_relay("relay_ev_out", sent["ev_out"], after)
            return _split_done("arrived_ev_out", relayed, token)[0].reshape(1, D, D)
        if group == "od_relay":
            sent["od_relayed"], _ = _gather2_relay("relay_od", sent["od"], after)
            return None
        if not od:
            od.extend(_split_done("arrived_od", sent["od_relayed"], after))
        w_od_in, w_qb, w_kvb, norms_all, w_od_out = od
        if group == "od_out":
            return w_od_out.reshape(1, D, D)
        return (_odd_in_layout(w_od_in), _qb_layout(w_qb), w_kvb,
                norms_all[:, 0, :64].reshape(1, C_RANK), norms_all[:, 0, 64:].reshape(1, C_RANK))

    scatters = {}

    def put_g(group, grads):
        if group == "ev_in":
            core = lax.axis_index("c").astype(jnp.int32).reshape(1)
            paired, token = _scatter2_pair("pair_ev_in", [grads("ev_in_dw_sibling", 1 - core)])
            mine = grads("ev_in_dw_own", core, deps=[token])
            pair = _split_done("paired_ev_in", paired, mine)[0]
            scatters[group], token = _scatter2_send("scatter_ev_in", [_pair_add("pair_add_ev_in", mine, pair)])
        else:
            scatters[group], token = _exchange_start("scatter_" + group, grads, False)
        return token

    def put_small(early):
        scatters["small"], token = _exchange_start("gather_small_early", early, True)
        return token

    grad_x, _, dnpre0 = _forward_backward(
        x[0], cos_t, sin_t, loss_target[0], norm_pre, norm_post, ev_lb_logits, ev_a_onorm, ev_b_ln_w,
        ev_b_ln_b, ev_b_ws, ev_b_bias, get_w, put_g, put_small, start_dep=tok)

    big_w = {"ev_w_in": ev_w_in, "ev_w_out": ev_w_out, "od_w_in": od_w_in, "od_w_qb": od_w_qb,
             "od_w_kvb": od_w_kvb, "od_w_out": od_w_out}
    big_m = {"ev_w_in": m_ev_w_in, "ev_w_out": m_ev_w_out, "od_w_in": m_od_w_in, "od_w_qb": m_od_w_qb,
             "od_w_kvb": m_od_w_kvb, "od_w_out": m_od_w_out}
    big_v = {"ev_w_in": v_ev_w_in, "ev_w_out": v_ev_w_out, "od_w_in": v_od_w_in, "od_w_qb": v_od_w_qb,
             "od_w_kvb": v_od_w_kvb, "od_w_out": v_od_w_out}
    big_out = {}
    after = grad_x
    for group, names in (("od_out", ["od_w_out"]), ("od_qkv", ["od_w_qb", "od_w_kvb"]), ("od_in", ["od_w_in"]),
                         ("ev_out", ["ev_w_out"])):
        parts = _exchange_wait("summed_" + group, scatters[group], after)
        for nm, p in zip(names, parts):
            w, m, v = big_w[nm][0], big_m[nm][0], big_v[nm][0]
            if nm == "od_w_in":
                res_t = _adamw("adamw_" + nm, [(p, N_DEV)], w.T, m.T, v.T, w.shape[1], 512)
                big_out[nm] = [r.T[None] for r in res_t]
            else:
                big_out[nm] = [r[None] for r in _adamw("adamw_" + nm, [(p, N_DEV)], w, m, v, w.shape[0] // 8)]
            after = big_out[nm][0]

    late_all = _exchange("gather_small_late", [dnpre0], gather=True, deps=[after])[0]
    early_all = _exchange_wait("arrived_small_early", scatters["small"], late_all)

    small_w = (norm_pre, norm_post, ev_lb_logits, ev_a_onorm, ev_b_ln_w, ev_b_ln_b, ev_b_ws, ev_b_bias)
    small_m = (m_norm_pre, m_norm_post, m_ev_lb_logits, m_ev_a_onorm, m_ev_b_ln_w, m_ev_b_ln_b, m_ev_b_ws, m_ev_b_bias)
    small_v = (v_norm_pre, v_norm_post, v_ev_lb_logits, v_ev_a_onorm, v_ev_b_ln_w, v_ev_b_ln_b, v_ev_b_ws, v_ev_b_bias)
    wmv = [tuple(a.reshape(s) for a in t) for s, t in zip(SMALL_PARAM_SHAPES, zip(small_w, small_m, small_v))]
    small_res, loss_row, g_norm_rows = _adamw_small(late_all, early_all, wmv)
    small_out = [[r.reshape(w.shape) for r in four] for four, w in zip(small_res, small_w)]
    loss = loss_row[0, 0]

    g_norms = jnp.concatenate([lax.dynamic_slice(g_norm_rows, (0, 64 * me), (1, 64)),
                               lax.dynamic_slice(g_norm_rows, (1, 64 * me), (1, 64))], axis=1)
    res_n = _adamw("adamw_norms", [(g_norms[None], 1)],
                   jnp.concatenate([od_q_norm, od_kv_norm], axis=1),
                   jnp.concatenate([m_od_q_norm, m_od_kv_norm], axis=1),
                   jnp.concatenate([v_od_q_norm, v_od_kv_norm], axis=1), 1)
    qn_out = [r[:, :64] for r in res_n]
    kvn_out = [r[:, 64:] for r in res_n]

    chip_sums, from_peers = _split_done("summed_ev_in", scatters["ev_in"], loss_row, all_bufs=True)
    own_chip = lax.dynamic_slice_in_dim(chip_sums, me >> 1, 1, 0)
    w = ev_w_in[0]
    big_out["ev_w_in"] = [r[None] for r in _adamw("adamw_ev_w_in", [(own_chip, 1), (from_peers, 3)], w, m_ev_w_in[0],
                                                  v_ev_w_in[0], w.shape[0] // 8)]

    order = ("norm_pre", "norm_post", "ev_w_in", "ev_lb_logits", "ev_a_onorm", "ev_b_ln_w", "ev_b_ln_b",
             "ev_b_ws", "ev_b_bias", "ev_w_out", "od_w_in", "od_q_norm", "od_w_qb", "od_kv_norm",
             "od_w_kvb", "od_w_out")
    small_names = ("norm_pre", "norm_post", "ev_lb_logits", "ev_a_onorm", "ev_b_ln_w", "ev_b_ln_b",
                   "ev_b_ws", "ev_b_bias")
    outs = [loss, grad_x[None]]
    for kind in range(4):
        for nm in order:
            if nm in big_out:
                outs.append(big_out[nm][kind])
            elif nm == "od_q_norm":
                outs.append(qn_out[kind])
            elif nm == "od_kv_norm":
                outs.append(kvn_out[kind])
            else:
                outs.append(small_out[small_names.index(nm)][kind])
    return tuple(outs)
```

```python
import functools

import jax
import jax.numpy as jnp
from jax import lax
from jax.experimental import pallas as pl
from jax.experimental.pallas import tpu as pltpu

F32 = jnp.float32
BF16 = jnp.bfloat16

N_DEV = 8
T = 2048
D = 2048
EPS = 1e-6
A_HEADS = 8
HD = 128
A_CHUNK = 64
A_SUB = 16
B_GROUPS = 8
B_CHUNK = 128
EVEN_IN = 7168
C_HEADS = 16
C_RANK = 512
C_NOPE = 128
C_ROPE = 64
C_QK = C_NOPE + C_ROPE
C_V = 128
ODD_IN = 3136
ODD_IN_PAD = 3200
QP = 256
ROPE_THETA = 10000.0
ATT_SCALE = C_QK ** -0.5

ADAM_LR = 0.001
ADAM_B1 = 0.9
ADAM_B2 = 0.999
ADAM_EPS = 1e-08
ADAM_WD = 0.01
ADAM_STEP = 10

VMEM_LIMIT_V7X = 56 * 1024 * 1024
MESH_ID = pl.DeviceIdType.MESH


def _params(n_grid):
    return pltpu.CompilerParams(dimension_semantics=("arbitrary",) * n_grid,
                                vmem_limit_bytes=VMEM_LIMIT_V7X)


def _dg(a, b, ca, cb):
    return lax.dot_general(a.astype(BF16), b.astype(BF16), (((ca,), (cb,)), ((), ())),
                           preferred_element_type=F32)


def _raw_nn(a, b):
    return _dg(a, b, 1, 0)


def _raw_nt(a, b):
    return _dg(a, b, 1, 1)


def _raw_tn(a, b):
    return _dg(a, b, 0, 0)


@jax.custom_vjp
def _dot_nn(a, b):
    return _raw_nn(a, b)


def _dot_nn_fwd(a, b):
    return _raw_nn(a, b), (a.astype(BF16), b.astype(BF16))


def _dot_nn_bwd(res, g):
    a, b = res
    return _raw_nt(g, b), _raw_tn(a, g)


_dot_nn.defvjp(_dot_nn_fwd, _dot_nn_bwd)


@jax.custom_vjp
def _dot_nt(a, b):
    return _raw_nt(a, b)


def _dot_nt_fwd(a, b):
    return _raw_nt(a, b), (a.astype(BF16), b.astype(BF16))


def _dot_nt_bwd(res, g):
    a, b = res
    return _raw_nn(g, b), _raw_tn(g, a)


_dot_nt.defvjp(_dot_nt_fwd, _dot_nt_bwd)


@jax.custom_vjp
def _dot_tn(a, b):
    return _raw_tn(a, b)


def _dot_tn_fwd(a, b):
    return _raw_tn(a, b), (a.astype(BF16), b.astype(BF16))


def _dot_tn_bwd(res, g):
    a, b = res
    return _raw_nt(b, g), _raw_nn(a, g)


_dot_tn.defvjp(_dot_tn_fwd, _dot_tn_bwd)


@jax.custom_vjp
def _sigmoid(x):
    e = jnp.exp(-jnp.abs(x))
    return jnp.where(x >= 0, 1.0 / (1.0 + e), e / (1.0 + e))


def _sigmoid_fwd(x):
    s = _sigmoid(x)
    return s, s


def _sigmoid_bwd(s, g):
    return (g * s * (1.0 - s),)


_sigmoid.defvjp(_sigmoid_fwd, _sigmoid_bwd)


def _silu(x):
    return x * _sigmoid(x)


def _rms(x, w):
    return x * lax.rsqrt(jnp.mean(x * x, axis=-1, keepdims=True) + EPS) * w


def _split3(x):
    hi = x.astype(BF16)
    r = x - hi.astype(F32)
    mid = r.astype(BF16)
    lo = (r - mid.astype(F32)).astype(BF16)
    return hi, mid, lo


def _mask_apply(mask_bf16, x, contract):
    out = None
    for piece in _split3(x):
        d = lax.dot_general(mask_bf16, piece, (((contract,), (0,)), ((), ())),
                            preferred_element_type=F32)
        out = d if out is None else out + d
    return out


def _chunk_tri(rows):
    r = lax.broadcasted_iota(jnp.int32, (rows, rows), 0)
    c = lax.broadcasted_iota(jnp.int32, (rows, rows), 1)
    return ((r >= c) & (r // A_CHUNK == c // A_CHUNK)).astype(BF16)


@jax.custom_vjp
def _chunk_cumsum(x):
    return _mask_apply(_chunk_tri(x.shape[0]), x, 1)


def _chunk_cumsum_fwd(x):
    return _chunk_cumsum(x), None


def _chunk_cumsum_bwd(_, g):
    return (_mask_apply(_chunk_tri(g.shape[0]), g, 0),)


_chunk_cumsum.defvjp(_chunk_cumsum_fwd, _chunk_cumsum_bwd)


def _hgrn2_rows(q, zf, v, ga, st, l0, l1, onorm):
    rows = q.shape[0]
    n_sub = A_CHUNK // A_SUB
    mx = jnp.maximum(l0, l1)
    e0 = jnp.exp(l0 - mx)
    e1 = jnp.exp(l1 - mx)
    lb = e0 / (e0 + e1)
    lf = jnp.log(lb + (1.0 - lb) * _sigmoid(zf))
    k = (1.0 - lb) * _sigmoid(-zf)
    b = _chunk_cumsum(lf)

    t_idx = lax.broadcasted_iota(jnp.int32, (A_CHUNK, n_sub * A_CHUNK), 0)
    c_idx = lax.broadcasted_iota(jnp.int32, (A_CHUNK, n_sub * A_CHUNK), 1)
    sel = (c_idx // A_CHUNK == t_idx // A_SUB) & (c_idx % A_CHUNK <= t_idx)
    key_row = lax.broadcasted_iota(jnp.int32, (A_CHUNK, HD), 0)

    outs = []
    for n in range(rows // A_CHUNK):
        lo = n * A_CHUNK
        qc, kc, vc = q[lo:lo + A_CHUNK], k[lo:lo + A_CHUNK], v[lo:lo + A_CHUNK]
        lfc, bc = lf[lo:lo + A_CHUNK], b[lo:lo + A_CHUNK]
        b_last = bc[A_CHUNK - 1:A_CHUNK]
        o_inter = _dot_nt(qc * jnp.exp(bc), st)
        kv_t = _dot_tn(vc, kc * jnp.exp(b_last - bc))
        st = st * jnp.exp(b_last) + kv_t
        g_rows, k_subs = [], []
        for i in range(n_sub):
            g_i = bc[i * A_SUB:i * A_SUB + 1] - lfc[i * A_SUB:i * A_SUB + 1]
            g_rows.append(jnp.broadcast_to(g_i, (A_SUB, HD)))
            expo = jnp.where(key_row < (i + 1) * A_SUB, g_i - bc, -jnp.inf)
            k_subs.append(kc * jnp.exp(expo))
        q_sub = qc * jnp.exp(bc - jnp.concatenate(g_rows, axis=0))
        scores = _dot_nt(q_sub, jnp.concatenate(k_subs, axis=0))
        scores = jnp.where(sel, scores, 0.0)
        o_intra = _dot_nn(scores, jnp.concatenate([vc] * n_sub, axis=0))
        outs.append(o_inter + o_intra)
    o = jnp.concatenate(outs, axis=0)
    return _rms(o, onorm) * _silu(ga), st


def _gmlp_rows(u, vb, gb, lnw, lnb, ws, bias):
    rows = u.shape[0]
    mu = jnp.mean(vb, axis=-1, keepdims=True)
    xc = vb - mu
    vg = xc * lax.rsqrt(jnp.mean(xc * xc, axis=-1, keepdims=True) + EPS) * lnw + lnb
    r = lax.broadcasted_iota(jnp.int32, (B_CHUNK, B_CHUNK), 0)
    c = lax.broadcasted_iota(jnp.int32, (B_CHUNK, B_CHUNK), 1)
    ws_causal = jnp.where(r >= c, ws, 0.0)
    svs = [_dot_nn(ws_causal, vg[n * B_CHUNK:(n + 1) * B_CHUNK]) + bias
           for n in range(rows // B_CHUNK)]
    return u * jnp.concatenate(svs, axis=0) * _silu(gb)


def _rope(x, cos_t, sin_t):
    return x * cos_t + pltpu.roll(x, 64, 1) * sin_t


def _rope_transpose(g, cos_t, sin_t):
    return g * cos_t + pltpu.roll(g * sin_t, 64, 1)


ANY_SPEC = pl.BlockSpec(memory_space=pl.ANY)


def _live(deps):
    return [d for d in deps if d is not None]


def _skip_deps(body, n_in, n_deps):
    def wrapped(*refs):
        return body(*refs[:n_in], *refs[n_in + n_deps:])
    return wrapped


def _pure_call(name, fn, grid, in_specs, out_specs, out_shape, args, n_acc=0, deps=()):
    deps = _live(deps)
    n_in, n_out, n_deps = len(in_specs), len(out_specs), len(deps)
    in_specs = list(in_specs) + [ANY_SPEC] * n_deps
    args = tuple(args) + tuple(deps)

    def body(*refs):
        res = fn(*[r[...] for r in refs[:n_in]])
        if not isinstance(res, (tuple, list)):
            res = (res,)
        outs = refs[n_in + n_deps:n_in + n_deps + n_out]
        for o, r in zip(outs[:n_out - n_acc], res[:n_out - n_acc]):
            o[...] = r.astype(o.dtype)
        if n_acc:
            first = functools.reduce(jnp.logical_and, [pl.program_id(i) == 0 for i in range(len(grid))])
            for o, r in zip(outs[n_out - n_acc:], res[n_out - n_acc:]):
                @pl.when(first)
                def _(o=o, r=r):
                    o[...] = r.astype(o.dtype)

                @pl.when(jnp.logical_not(first))
                def _(o=o, r=r):
                    o[...] += r.astype(o.dtype)

    return pl.pallas_call(body, name=name, grid=grid, in_specs=in_specs, out_specs=out_specs,
                          out_shape=out_shape, compiler_params=_params(len(grid)))(*args)


def _sds(shape, dtype):
    return jax.ShapeDtypeStruct(shape, dtype)


def _row_spec(tm, width, col=0):
    return pl.BlockSpec((tm, width), lambda i, col=col: (i, col))


def _full_spec(shape):
    nd = len(shape)
    return pl.BlockSpec(shape, lambda *_: (0,) * nd)


def _mm_nn(name, a, b, out_dtype, tm, tn, deps=()):
    deps = _live(deps)
    m, k = a.shape
    j, _, n = b.shape
    per = n // tn

    def body(a_ref, b_ref, o_ref):
        o_ref[...] = _raw_nn(a_ref[...], b_ref[...]).astype(o_ref.dtype)

    return pl.pallas_call(
        _skip_deps(body, 2, len(deps)), name=name, grid=(m // tm, j * per),
        in_specs=[pl.BlockSpec((tm, k), lambda i, c: (i, 0)),
                  pl.BlockSpec((None, k, tn), lambda i, c: (c // per, 0, c % per))] + [ANY_SPEC] * len(deps),
        out_specs=pl.BlockSpec((tm, tn), lambda i, c: (i, c)),
        out_shape=_sds((m, j * n), out_dtype), compiler_params=_params(2))(a, b, *deps)


def _mm_nt(name, a, b, out_dtype, tm, tn, deps=()):
    deps = _live(deps)
    m = a.shape[0]
    j, nn, n = b.shape

    def body(a_ref, b_ref, o_ref):
        b_all = b_ref[0] if j == 1 else jnp.concatenate([b_ref[s] for s in range(j)], axis=1)
        o_ref[...] = _raw_nt(a_ref[...], b_all).astype(o_ref.dtype)

    return pl.pallas_call(
        _skip_deps(body, 2, len(deps)), name=name, grid=(m // tm, nn // tn),
        in_specs=[pl.BlockSpec((tm, j * n), lambda i, c: (i, 0)),
                  pl.BlockSpec((j, tn, n), lambda i, c: (0, c, 0))] + [ANY_SPEC] * len(deps),
        out_specs=pl.BlockSpec((tm, tn), lambda i, c: (i, c)),
        out_shape=_sds((m, nn), out_dtype), compiler_params=_params(2))(a, b, *deps)


def _mm_tn(name, a, b, j, out_dtype, tm, tn, deps=()):
    deps = _live(deps)
    k, m = a.shape
    n = b.shape[1] // j
    per = n // tn

    def body(a_ref, b_ref, o_ref):
        o_ref[...] = _raw_tn(a_ref[...], b_ref[...]).astype(o_ref.dtype)

    return pl.pallas_call(
        _skip_deps(body, 2, len(deps)), name=name, grid=(m // tm, j * per),
        in_specs=[pl.BlockSpec((k, tm), lambda i, c: (0, i)),
                  pl.BlockSpec((k, tn), lambda i, c: (0, c))] + [ANY_SPEC] * len(deps),
        out_specs=pl.BlockSpec((None, tm, tn), lambda i, c: (c // per, i, c % per)),
        out_shape=_sds((j, m, n), out_dtype), compiler_params=_params(2))(a, b, *deps)


def _mm_tn_parity(name, a, b, j, parity, out_dtype, tm, deps=()):
    deps = _live(deps)
    k, m = a.shape
    n = b.shape[1] // j

    def body(par_ref, a_ref, b_ref, o_ref):
        del par_ref
        o_ref[...] = _raw_tn(a_ref[...], b_ref[...]).astype(o_ref.dtype)

    grid_spec = pltpu.PrefetchScalarGridSpec(
        num_scalar_prefetch=1, grid=(m // tm, j // 2),
        in_specs=[pl.BlockSpec((k, tm), lambda i, s, par: (0, i)),
                  pl.BlockSpec((k, n), lambda i, s, par: (0, 2 * s + par[0]))] + [ANY_SPEC] * len(deps),
        out_specs=pl.BlockSpec((None, tm, n), lambda i, s, par: (s, i, 0)))
    return pl.pallas_call(
        lambda par_ref, *refs: _skip_deps(functools.partial(body, par_ref), 2, len(deps))(*refs),
        name=name, grid_spec=grid_spec, out_shape=_sds((j // 2, m, n), out_dtype),
        compiler_params=_params(2))(parity, a, b, *deps)


TM = 256


def _pre_norm(name, x, w_row, deps=()):
    def fn(xv, w):
        return _rms(xv, w)
    return _pure_call(name, fn, (T // TM,), [_row_spec(TM, D), _full_spec((1, D))],
                      [_row_spec(TM, D)], [_sds((T, D), BF16)], (x, w_row), deps=deps)[0]


def _post_pre_norm(x, y, w_post, w_pre):
    def fn(xv, yv, wp, wn):
        x1 = xv + _rms(yv, wp)
        return x1, _rms(x1, wn)
    return _pure_call("post_pre_norm", fn, (T // TM,),
                      [_row_spec(TM, D), _row_spec(TM, D), _full_spec((1, D)), _full_spec((1, D))],
                      [_row_spec(TM, D), _row_spec(TM, D)],
                      [_sds((T, D), F32), _sds((T, D), BF16)], (x, y, w_post, w_pre))


def _in_proj_norms_bwd(dz, w_t, y, x1, w_post, w_pre, dx1_in, deps=()):
    def fn(dzv, w, yv, x1v, wp, wn, dx1v):
        _, vjp_pre = jax.vjp(_rms, x1v, wn)
        dx1_h, dwn = vjp_pre(_raw_nn(dzv, w[0]))
        dx1 = dx1v + dx1_h
        _, vjp_post = jax.vjp(_rms, yv, wp)
        dy, dwp = vjp_post(dx1)
        return dx1, dy, dwp, dwn
    return _pure_call("od_in_dx_norms", fn, (T // TM,),
                      [_row_spec(TM, ODD_IN_PAD), _full_spec((1, ODD_IN_PAD, D)), _row_spec(TM, D), _row_spec(TM, D),
                       _full_spec((1, D)), _full_spec((1, D)), _row_spec(TM, D)],
                      [_row_spec(TM, D), _row_spec(TM, D), _full_spec((1, D)), _full_spec((1, D))],
                      [_sds((T, D), F32), _sds((T, D), BF16), _sds((1, D), F32), _sds((1, D), F32)],
                      (dz, w_t, y, x1, w_post, w_pre, dx1_in), n_acc=2, deps=deps)


def _out_proj_loss(og, w_out, x1, w_post, target):
    tm = 512

    def fn(ogv, w, x1v, wp, tv):
        r, vjp = jax.vjp(_rms, _raw_nn(ogv, w[0]), wp)
        err = x1v + r - tv
        part = 0.5 * jnp.sum(jnp.mean(err * err, axis=-1, keepdims=True), axis=0, keepdims=True)
        dx2 = err * (1.0 / D)
        dy, dwp = vjp(dx2)
        return dx2, dy, jnp.broadcast_to(part, (1, 128)), dwp
    return _pure_call("od_out_loss", fn, (T // tm,),
                      [_row_spec(tm, D), _full_spec((1, D, D)), _row_spec(tm, D), _full_spec((1, D)), _row_spec(tm, D)],
                      [_row_spec(tm, D), _row_spec(tm, D), _full_spec((1, 128)), _full_spec((1, D))],
                      [_sds((T, D), F32), _sds((T, D), BF16), _sds((1, 128), F32), _sds((1, D), F32)],
                      (og, w_out, x1, w_post, target), n_acc=2)


def _pre_norm_bwd(x, w_row, dh, dx_res, deps=()):
    def fn(xv, w, dhv, dxv):
        _, vjp = jax.vjp(_rms, xv, w)
        dx, dw = vjp(dhv)
        return dxv + dx, dw
    return _pure_call("pre_norm_bwd", fn, (T // TM,),
                      [_row_spec(TM, D), _full_spec((1, D)), _row_spec(TM, D), _row_spec(TM, D)],
                      [_row_spec(TM, D), _full_spec((1, D))],
                      [_sds((T, D), F32), _sds((1, D), F32)], (x, w_row, dh, dx_res), n_acc=1, deps=deps)


RA = 256


def _head(ref, hh):
    return ref[:, hh * HD:(hh + 1) * HD]


def _z_part(z_ref, k, h):
    lo = (k * A_HEADS + h) * HD
    return z_ref[:, lo:lo + HD]


def _even_specs():
    return [_full_spec((1, A_HEADS * HD)), _full_spec((1, A_HEADS * HD)), _full_spec((1, HD)),
            _full_spec((1, B_GROUPS * HD)), _full_spec((1, B_GROUPS * HD)),
            _full_spec((B_GROUPS, B_CHUNK, B_CHUNK)), _full_spec((B_GROUPS, B_CHUNK, 1))]


def _even_fwd(z, l0, l1, onorm, lnw, lnb, ws, bias):
    nb = T // RA

    def body(z_ref, l0_ref, l1_ref, on_ref, lnw_ref, lnb_ref, ws_ref, bias_ref, cat_ref, sst_ref, st_scr):
        @pl.when(pl.program_id(0) == 0)
        def _():
            st_scr[...] = jnp.zeros_like(st_scr)

        for hh in range(A_HEADS):
            st = st_scr[hh]
            sst_ref[hh] = st
            out, st_new = _hgrn2_rows(_z_part(z_ref, 0, hh), _z_part(z_ref, 1, hh), _z_part(z_ref, 2, hh),
                                      _z_part(z_ref, 3, hh), st, _head(l0_ref, hh), _head(l1_ref, hh), on_ref[...])
            cat_ref[:, hh * HD:(hh + 1) * HD] = out.astype(cat_ref.dtype)
            st_scr[hh] = st_new
        for gg in range(B_GROUPS):
            out = _gmlp_rows(_z_part(z_ref, 4, gg), _z_part(z_ref, 5, gg), _z_part(z_ref, 6, gg),
                             _head(lnw_ref, gg), _head(lnb_ref, gg), ws_ref[gg], bias_ref[gg])
            cat_ref[:, (A_HEADS + gg) * HD:(A_HEADS + gg + 1) * HD] = out.astype(cat_ref.dtype)

    return pl.pallas_call(
        body, name="even_mixers_fwd", grid=(nb,),
        in_specs=[pl.BlockSpec((RA, EVEN_IN), lambda r: (r, 0))] + _even_specs(),
        out_specs=[pl.BlockSpec((RA, 2 * A_HEADS * HD), lambda r: (r, 0)),
                   pl.BlockSpec((A_HEADS, None, HD, HD), lambda r: (0, r, 0, 0))],
        out_shape=[_sds((T, 2 * A_HEADS * HD), BF16), _sds((A_HEADS, nb, HD, HD), F32)],
        scratch_shapes=[pltpu.VMEM((A_HEADS, HD, HD), F32)],
        compiler_params=_params(1))(z, l0, l1, onorm, lnw, lnb, ws, bias)


def _even_bwd(z, l0, l1, onorm, lnw, lnb, ws, bias, sst, dy, w_out, deps=()):
    nb = T // RA
    deps = _live(deps)

    def body(z_ref, l0_ref, l1_ref, on_ref, lnw_ref, lnb_ref, ws_ref, bias_ref, sst_ref, dy_ref, w_ref,
             dz_ref, dl0_ref, dl1_ref, don_ref, dlnw_ref, dlnb_ref, dws_ref, dbias_ref, ds_scr):
        first = pl.program_id(0) == 0

        @pl.when(first)
        def _():
            ds_scr[...] = jnp.zeros_like(ds_scr)

        dcat = _raw_nt(dy_ref[...], w_ref[0])

        def put(k, h, val):
            lo = (k * A_HEADS + h) * HD
            dz_ref[:, lo:lo + HD] = val.astype(dz_ref.dtype)

        sums = []
        don = None
        for hh in range(A_HEADS):
            lanes = slice(hh * HD, (hh + 1) * HD)
            _, vjp = jax.vjp(_hgrn2_rows, _z_part(z_ref, 0, hh), _z_part(z_ref, 1, hh), _z_part(z_ref, 2, hh),
                             _z_part(z_ref, 3, hh), sst_ref[hh], _head(l0_ref, hh), _head(l1_ref, hh), on_ref[...])
            dq, dzf, dv, dga, dst, dl0, dl1, don_h = vjp((dcat[:, lanes], ds_scr[hh]))
            for k, val in enumerate((dq, dzf, dv, dga)):
                put(k, hh, val)
            ds_scr[hh] = dst
            sums += [(dl0_ref, (slice(None), lanes), dl0), (dl1_ref, (slice(None), lanes), dl1)]
            don = don_h if don is None else don + don_h
        sums.append((don_ref, slice(None), don))
        for gg in range(B_GROUPS):
            lanes = slice(gg * HD, (gg + 1) * HD)
            _, vjp = jax.vjp(_gmlp_rows, _z_part(z_ref, 4, gg), _z_part(z_ref, 5, gg), _z_part(z_ref, 6, gg),
                             _head(lnw_ref, gg), _head(lnb_ref, gg), ws_ref[gg], bias_ref[gg])
            du, dv, dg, dlnw, dlnb, dws, dbias = vjp(dcat[:, (A_HEADS + gg) * HD:(A_HEADS + gg + 1) * HD])
            for k, val in enumerate((du, dv, dg)):
                put(4 + k, gg, val)
            sums += [(dlnw_ref, (slice(None), lanes), dlnw), (dlnb_ref, (slice(None), lanes), dlnb),
                     (dws_ref, gg, dws), (dbias_ref, gg, dbias)]
        for ref, idx, val in sums:
            @pl.when(first)
            def _(ref=ref, idx=idx, val=val):
                ref[idx] = val

            @pl.when(jnp.logical_not(first))
            def _(ref=ref, idx=idx, val=val):
                ref[idx] += val

    small = _even_specs()
    return pl.pallas_call(
        _skip_deps(body, 11, len(deps)), name="even_mixers_bwd", grid=(nb,),
        in_specs=[pl.BlockSpec((RA, EVEN_IN), lambda r: (nb - 1 - r, 0))] + small
        + [pl.BlockSpec((A_HEADS, None, HD, HD), lambda r: (0, nb - 1 - r, 0, 0)),
           pl.BlockSpec((RA, D), lambda r: (nb - 1 - r, 0)), _full_spec((1, 2 * A_HEADS * HD, D))]
        + [ANY_SPEC] * len(deps),
        out_specs=[pl.BlockSpec((RA, EVEN_IN), lambda r: (nb - 1 - r, 0))] + small,
        out_shape=[_sds((T, EVEN_IN), BF16), _sds((1, A_HEADS * HD), F32), _sds((1, A_HEADS * HD), F32),
                   _sds((1, HD), F32), _sds((1, B_GROUPS * HD), F32), _sds((1, B_GROUPS * HD), F32),
                   _sds((B_GROUPS, B_CHUNK, B_CHUNK), F32), _sds((B_GROUPS, B_CHUNK, 1), F32)],
        scratch_shapes=[pltpu.VMEM((A_HEADS, HD, HD), F32)],
        compiler_params=_params(1))(z, l0, l1, onorm, lnw, lnb, ws, bias, sst, dy, w_out, *deps)


def _mla_pre(z1, qn, kvn, cos_t, sin_t):
    def fn(cq, ckv, kpe, cs, sn, wq, wkv):
        return _rms(cq, wq), _rms(ckv, wkv), _rope(kpe, cs, sn)
    return _pure_call("mla_pre", fn, (T // TM,),
                      [_row_spec(TM, C_RANK, 4), _row_spec(TM, C_RANK, 5), _row_spec(TM, HD, 24),
                       _row_spec(TM, HD), _row_spec(TM, HD),
                       _full_spec((1, C_RANK)), _full_spec((1, C_RANK))],
                      [_row_spec(TM, C_RANK), _row_spec(TM, C_RANK), _row_spec(TM, HD)],
                      [_sds((T, C_RANK), BF16), _sds((T, C_RANK), BF16), _sds((T, HD), BF16)],
                      (z1, z1, z1, cos_t, sin_t, qn, kvn))


def _mla_pre_bwd(z1, qn, kvn, cos_t, sin_t, dq, w_qb, dkv, w_kvb, dkp, dgate, deps=()):
    def fn(cq, ckv, cs, sn, wq, wkv, dqv, wqb, dkvv, wkvb, g_kp, g_gate):
        _, vjp_q = jax.vjp(_rms, cq, wq)
        dcq, dwq = vjp_q(_raw_nt(dqv, wqb[0]))
        _, vjp_kv = jax.vjp(_rms, ckv, wkv)
        dckv, dwkv = vjp_kv(_raw_nt(dkvv, jnp.concatenate([wkvb[s] for s in range(N_DEV)], axis=1)))
        dz1 = jnp.concatenate([g_gate, dcq.astype(BF16), dckv.astype(BF16),
                               _rope_transpose(g_kp, cs, sn).astype(BF16)], axis=1)
        return dz1, dwq, dwkv
    return _pure_call("mla_pre_bwd", fn, (T // TM,),
                      [_row_spec(TM, C_RANK, 4), _row_spec(TM, C_RANK, 5),
                       _row_spec(TM, HD), _row_spec(TM, HD),
                       _full_spec((1, C_RANK)), _full_spec((1, C_RANK)),
                       _row_spec(TM, C_HEADS * QP), _full_spec((1, C_RANK, C_HEADS * QP)),
                       _row_spec(TM, C_HEADS * KVW), _full_spec((N_DEV, C_RANK, C_HEADS * KVW // N_DEV)),
                       _row_spec(TM, HD), _row_spec(TM, D)],
                      [_row_spec(TM, ODD_IN_PAD), _full_spec((1, C_RANK)), _full_spec((1, C_RANK))],
                      [_sds((T, ODD_IN_PAD), BF16), _sds((1, C_RANK), F32), _sds((1, C_RANK), F32)],
                      (z1, z1, cos_t, sin_t, qn, kvn, dq, w_qb, dkv, w_kvb, dkp, dgate), n_acc=2, deps=deps)


TQ = 256
HP = 4
KVW = C_NOPE + C_V


def _att_keys(kv_ref, kp_ref, k_scr):
    @pl.when(pl.program_id(1) == 0)
    def _():
        for hh in range(HP):
            k_scr[hh, :, 0:C_NOPE] = kv_ref[:, hh * KVW:hh * KVW + C_NOPE]
            k_scr[hh, :, C_NOPE:QP] = kp_ref[...]


def _att_scores(q, cos_ref, sin_ref, k_scr, hh, n):
    keys = (n + 1) * TQ
    qr = jnp.concatenate([q[:, :C_NOPE], _rope(q[:, C_NOPE:], cos_ref[...], sin_ref[...])], axis=1).astype(BF16)
    return qr, _raw_nt(qr, k_scr[hh, 0:keys, :]) * ATT_SCALE


def _causal(x, n, fill):
    row = lax.broadcasted_iota(jnp.int32, (TQ, TQ), 0)
    col = lax.broadcasted_iota(jnp.int32, (TQ, TQ), 1)
    diag = jnp.where(col <= row, x[:, n * TQ:], fill)
    return diag if n == 0 else jnp.concatenate([x[:, :n * TQ], diag], axis=1)


def _per_query_block(fn):
    for n in range(T // TQ):
        pl.when(pl.program_id(1) == n)(functools.partial(fn, n))


def _att_in_specs():
    return [pl.BlockSpec((TQ, HP * QP), lambda g, i: (i, g)),
            pl.BlockSpec((TQ, HD), lambda g, i: (i, 0)),
            pl.BlockSpec((TQ, HD), lambda g, i: (i, 0)),
            pl.BlockSpec((T, HP * KVW), lambda g, i: (0, g)),
            pl.BlockSpec((T, HD), lambda g, i: (0, 0))]


def _attention_fwd(q, cos_t, sin_t, kv, kp, z1):
    def body(q_ref, cos_ref, sin_ref, kv_ref, kp_ref, gate_ref, o_ref, lse_ref, og_ref, k_scr):
        _att_keys(kv_ref, kp_ref, k_scr)

        def block(n):
            keys = (n + 1) * TQ
            for hh in range(HP):
                _, s = _att_scores(q_ref[:, hh * QP:(hh + 1) * QP], cos_ref, sin_ref, k_scr, hh, n)
                s = _causal(s, n, jnp.finfo(F32).min)
                m = jnp.max(s, axis=-1, keepdims=True)
                p = jnp.exp(s - m)
                l = jnp.sum(p, axis=-1, keepdims=True)
                v = kv_ref[0:keys, hh * KVW + C_NOPE:(hh + 1) * KVW]
                o = _raw_nn(p, v) / l
                lanes = slice(hh * C_V, (hh + 1) * C_V)
                o_ref[:, lanes] = o
                og_ref[:, lanes] = (o * _silu(gate_ref[:, lanes])).astype(og_ref.dtype)
                lse_ref[hh] = m + jnp.log(l)

        _per_query_block(block)

    heads = pl.BlockSpec((TQ, HP * C_V), lambda g, i: (i, g))
    return pl.pallas_call(
        body, name="attention_fwd", grid=(C_HEADS // HP, T // TQ), in_specs=_att_in_specs() + [heads],
        out_specs=[heads, pl.BlockSpec((HP, TQ, 1), lambda g, i: (g, i, 0)), heads],
        out_shape=[_sds((T, C_HEADS * C_V), F32), _sds((C_HEADS, T, 1), F32), _sds((T, C_HEADS * C_V), BF16)],
        scratch_shapes=[pltpu.VMEM((HP, T, QP), BF16)],
        compiler_params=_params(2))(q, cos_t, sin_t, kv, kp, z1)


def _attention_bwd(q, cos_t, sin_t, kv, kp, o, lse, dog, z1):
    nq = T // TQ

    def body(q_ref, cos_ref, sin_ref, kv_ref, kp_ref, o_ref, lse_ref, dog_ref, gate_ref,
             dq_ref, dkv_ref, dkp_ref, dgate_ref, k_scr, dk_scr, dv_scr):
        g, i = pl.program_id(0), pl.program_id(1)
        _att_keys(kv_ref, kp_ref, k_scr)

        @pl.when(i == 0)
        def _():
            dv_scr[...] = jnp.zeros_like(dv_scr)
            dk_scr[...] = jnp.zeros_like(dk_scr)

        def block(n):
            keys = (n + 1) * TQ
            for hh in range(HP):
                qr, s = _att_scores(q_ref[:, hh * QP:(hh + 1) * QP], cos_ref, sin_ref, k_scr, hh, n)
                p = _causal(jnp.exp(s - lse_ref[hh]), n, 0.0)
                lanes = slice(hh * C_V, (hh + 1) * C_V)
                ov, gate, dogv = o_ref[:, lanes], gate_ref[:, lanes], dog_ref[:, lanes]
                sig = _sigmoid(gate)
                silu = gate * sig
                dov = dogv * silu
                dgate_ref[:, lanes] = (dogv * ov * (sig + silu * (1.0 - sig))).astype(dgate_ref.dtype)
                delta = jnp.sum(dov * ov, axis=-1, keepdims=True)
                dp = _raw_nt(dov, kv_ref[0:keys, hh * KVW + C_NOPE:(hh + 1) * KVW])
                ds = p * (dp - delta) * ATT_SCALE
                dq = _raw_nn(ds, k_scr[hh, 0:keys, :])
                dq_ref[:, hh * QP:(hh + 1) * QP] = jnp.concatenate(
                    [dq[:, :C_NOPE], _rope_transpose(dq[:, C_NOPE:], cos_ref[...], sin_ref[...])],
                    axis=1).astype(dq_ref.dtype)
                dv_scr[hh, 0:keys, :] += _raw_tn(p, dov)
                dk_scr[hh, 0:keys, :] += _raw_tn(ds, qr)

        _per_query_block(block)

        @pl.when(i == nq - 1)
        def _():
            for hh in range(HP):
                dkv_ref[:, hh * KVW:(hh + 1) * KVW] = jnp.concatenate(
                    [dk_scr[hh, :, 0:C_NOPE], dv_scr[hh]], axis=1).astype(dkv_ref.dtype)

        @pl.when(jnp.logical_and(i == nq - 1, g == 0))
        def _():
            dkp_ref[...] = dk_scr[0, :, C_NOPE:QP]

        @pl.when(jnp.logical_and(i == nq - 1, g > 0))
        def _():
            dkp_ref[...] += dk_scr[0, :, C_NOPE:QP]

        @pl.when(i == nq - 1)
        def _():
            for hh in range(1, HP):
                dkp_ref[...] += dk_scr[hh, :, C_NOPE:QP]

    heads = pl.BlockSpec((TQ, HP * C_V), lambda g, i: (i, g))
    return pl.pallas_call(
        body, name="attention_bwd", grid=(C_HEADS // HP, nq),
        in_specs=_att_in_specs() + [heads, pl.BlockSpec((HP, TQ, 1), lambda g, i: (g, i, 0)), heads, heads],
        out_specs=[pl.BlockSpec((TQ, HP * QP), lambda g, i: (i, g)),
                   pl.BlockSpec((T, HP * KVW), lambda g, i: (0, g)),
                   _full_spec((T, HD)), heads],
        out_shape=[_sds((T, C_HEADS * QP), BF16), _sds((T, C_HEADS * KVW), BF16), _sds((T, HD), F32),
                   _sds((T, C_HEADS * C_V), BF16)],
        scratch_shapes=[pltpu.VMEM((HP, T, QP), BF16), pltpu.VMEM((HP, T, QP), F32), pltpu.VMEM((HP, T, C_V), F32)],
        compiler_params=_params(2))(q, cos_t, sin_t, kv, kp, o, lse, dog, z1)


def _adamw_math(w, g, m, v):
    m = ADAM_B1 * m + (1.0 - ADAM_B1) * g
    v = ADAM_B2 * v + (1.0 - ADAM_B2) * (g * g)
    m_hat = m / (1.0 - ADAM_B1 ** ADAM_STEP)
    v_hat = v / (1.0 - ADAM_B2 ** ADAM_STEP)
    delta = -ADAM_LR * (m_hat / (jnp.sqrt(v_hat) + ADAM_EPS) + ADAM_WD * w)
    return delta, m, v


def _adamw(name, parts, w, m, v, tr, tc=None):
    rows, cols = w.shape

    def fn(*vals):
        pvs, (wv, mv, vv) = vals[:len(parts)], vals[len(parts):]
        g = None
        for pv in pvs:
            for d in range(pv.shape[0]):
                term = pv[d].astype(F32)
                g = term if g is None else g + term
        return (g,) + _adamw_math(wv, g, mv, vv)

    tc = cols if tc is None else tc
    blk = pl.BlockSpec((tr, tc), lambda i, j: (i, j))
    part_specs = [pl.BlockSpec((n, tr, tc), lambda i, j: (0, i, j)) for _, n in parts]
    return _pure_call(name, fn, (rows // tr, cols // tc), part_specs + [blk, blk, blk],
                      [blk] * 4, [_sds((rows, cols), F32)] * 4, tuple(p for p, _ in parts) + (w, m, v))


SMALL_PARAM_SHAPES = ((2, D), (2, D), (2, A_HEADS * HD), (1, HD), (1, B_GROUPS * HD), (1, B_GROUPS * HD),
                      (B_GROUPS, B_CHUNK, B_CHUNK), (B_GROUPS, B_CHUNK))
SMALL_PIECES = ((0, 0, 0, 0), (0, 1, 1, 0), (1, 0, 1, 1), (1, 1, 1, 2), (2, 0, 2, 0), (2, 1, 2, 1),
                (3, 0, 3, 8), (4, 0, 2, 2), (5, 0, 2, 3))


def _small_rows(dnpre1, dnpost0, dnpost1, dl0, dl1, donorm, dlnw, dlnb, dws, dbias, dqn, dkvn, loss_part):
    return [jnp.concatenate([dnpre1, dnpost0, dnpost1], axis=0),
            jnp.concatenate([dl0, dl1, dlnw, dlnb], axis=0),
            jnp.concatenate([dbias.reshape(B_GROUPS, B_CHUNK), donorm, loss_part], axis=0),
            dws,
            jnp.concatenate([dqn, dkvn], axis=0)]


def _adamw_small(late_all, early_all, wmv):
    n_in = 6 + 3 * len(wmv)

    def body(*refs):
        gathered, params, outs = refs[:6], refs[6:n_in], refs[n_in:]

        def total(ref):
            s = ref[0]
            for d in range(1, N_DEV):
                s = s + ref[d]
            return s

        g_late, g2048, g1024, g128, g_ws, g512 = [total(r) for r in gathered]
        arrays = (g_late, g2048, g1024, g128)

        def update(p, rows, g):
            w_ref, m_ref, v_ref = params[3 * p:3 * p + 3]
            delta, m, v = _adamw_math(w_ref[rows], g, m_ref[rows], v_ref[rows])
            for out, val in zip(outs[4 * p:4 * p + 4], (g, delta, m, v)):
                out[rows] = val

        for p, row, arr, arr_row in SMALL_PIECES:
            update(p, pl.ds(row, 1), arrays[arr][arr_row:arr_row + 1])
        update(6, slice(None), g_ws)
        update(7, slice(None), g128[0:B_GROUPS])
        outs[32][...] = g128[B_GROUPS + 1:B_GROUPS + 2]
        outs[33][...] = g512

    vmem = pl.BlockSpec(memory_space=pltpu.VMEM)
    flat = [a for t in wmv for a in t]
    out_shape = [_sds(s, F32) for s in SMALL_PARAM_SHAPES for _ in range(4)] + [_sds((1, 128), F32), _sds((2, C_RANK), F32)]
    res = pl.pallas_call(body, name="adamw_small", in_specs=[vmem] * n_in, out_specs=[vmem] * len(out_shape),
                         out_shape=out_shape,
                         compiler_params=pltpu.CompilerParams(vmem_limit_bytes=VMEM_LIMIT_V7X))(late_all, *early_all, *flat)
    return [res[4 * p:4 * p + 4] for p in range(8)], res[32], res[33]


def _exchange(name, arrs, gather, deps=()):
    n = len(arrs)
    deps = _live(deps)

    def body(*refs):
        ins, outs = refs[:n], refs[n + len(deps):2 * n + len(deps)]
        send_sems, recv_sems, local_sems = refs[2 * n + len(deps):]
        x, y, c = lax.axis_index("x"), lax.axis_index("y"), lax.axis_index("c")
        me = 4 * x + 2 * y + c

        def peer(k):
            return (x ^ (k >> 2), y ^ ((k >> 1) & 1), c ^ (k & 1))

        def copy(a, k):
            src = ins[a] if gather else ins[a].at[me ^ k]
            return pltpu.make_async_remote_copy(
                src_ref=src, dst_ref=outs[a].at[me], send_sem=send_sems.at[a, k - 1],
                recv_sem=recv_sems.at[a, k - 1], device_id=peer(k), device_id_type=MESH_ID)

        def arrival(a, k):
            src = ins[a] if gather else ins[a].at[me]
            return pltpu.make_async_remote_copy(
                src_ref=src, dst_ref=outs[a].at[me ^ k], send_sem=send_sems.at[a, k - 1],
                recv_sem=recv_sems.at[a, k - 1], device_id=peer(k), device_id_type=MESH_ID)

        own = [pltpu.make_async_copy(ins[a] if gather else ins[a].at[me], outs[a].at[me], local_sems.at[a])
               for a in range(n)]
        for cp in own:
            cp.start()
        for k in range(1, N_DEV):
            for a in range(n):
                copy(a, k).start()
        for k in range(1, N_DEV):
            for a in range(n):
                arrival(a, k).wait_recv()
        for k in range(1, N_DEV):
            for a in range(n):
                copy(a, k).wait_send()
        for cp in own:
            cp.wait()

    any_spec = pl.BlockSpec(memory_space=pl.ANY)
    out_shape = [_sds((N_DEV,) + a.shape if gather else a.shape, a.dtype) for a in arrs]
    return pl.pallas_call(
        body, name=name, in_specs=[any_spec] * (n + len(deps)), out_specs=[any_spec] * n, out_shape=out_shape,
        scratch_shapes=[pltpu.SemaphoreType.DMA((n, N_DEV - 1)), pltpu.SemaphoreType.DMA((n, N_DEV - 1)),
                        pltpu.SemaphoreType.DMA((n,))],
        compiler_params=pltpu.CompilerParams(has_side_effects=True))(*arrs, *deps)


HBM_SPEC = pl.BlockSpec(memory_space=pltpu.HBM)
SEM_SPEC = pl.BlockSpec(memory_space=pltpu.SEMAPHORE)
DATAFLOW = pltpu.SideEffectType.DATAFLOW_SIDE_EFFECTING


def _my_index():
    return 4 * lax.axis_index("x") + 2 * lax.axis_index("y") + lax.axis_index("c")


def _plan_copies(plan, refs, send_sems, recv_sems):
    x, y, c = lax.axis_index("x"), lax.axis_index("y"), lax.axis_index("c")
    return [pltpu.make_async_remote_copy(
        src_ref=src, dst_ref=dst, send_sem=send_sems.at[i], recv_sem=recv_sems.at[i],
        device_id=(x ^ (k >> 2), y ^ ((k >> 1) & 1), c ^ (k & 1)), device_id_type=MESH_ID)
        for i, (src, dst, k) in enumerate(plan(refs, 4 * x + 2 * y + c))]


def _split_call(name, bufs, waits=None, starts=None, deps=()):
    n = len(bufs)
    deps = _live(deps)
    n_wait = 2 if waits else 0

    def body(*refs):
        zones = refs[:n]
        if waits:
            for cp in _plan_copies(waits[2], zones, refs[n], refs[n + 1]):
                cp.wait_send()
                cp.wait_recv()
        if starts:
            first_out = n + n_wait + len(deps)
            for cp in _plan_copies(starts[0], zones, refs[first_out], refs[first_out + 1]):
                cp.start()
            refs[-1][...] = jnp.zeros_like(refs[-1])

    out_specs, out_shape = [], []
    if starts:
        sems = pltpu.SemaphoreType.DMA((starts[1],))
        out_specs, out_shape = [SEM_SPEC, SEM_SPEC], [sems, sems]
    out_specs += [HBM_SPEC] * n
    out_shape += [pltpu.HBM(b.shape, b.dtype) for b in bufs]
    if starts:
        out_specs.append(pl.BlockSpec(memory_space=pltpu.VMEM))
        out_shape.append(_sds((8, 128), F32))
    first_buf = 2 if starts else 0
    res = pl.pallas_call(
        body, name=name,
        in_specs=[HBM_SPEC] * n + [SEM_SPEC] * n_wait + [ANY_SPEC] * len(deps),
        out_specs=out_specs, out_shape=out_shape,
        input_output_aliases={i: first_buf + i for i in range(n)},
        compiler_params=pltpu.CompilerParams(has_side_effects=DATAFLOW),
    )(*[pltpu.with_memory_space_constraint(b, pltpu.HBM) for b in bufs], *(waits[:2] if waits else ()), *deps)
    out_bufs = list(res[first_buf:first_buf + n])
    return out_bufs, ((res[0], res[1]) if starts else None), (res[-1] if starts else None)


def _direct_plan(n, gather):
    def plan(refs, me):
        return [(refs[a] if gather else refs[a].at[me ^ k], refs[n + a].at[me], k)
                for k in range(1, N_DEV) for a in range(n)]
    return plan


def _own_slot_filled(a, gather):
    me = _my_index()
    if gather:
        return lax.dynamic_update_slice_in_dim(lax.empty((N_DEV,) + a.shape, a.dtype), a[None], me, 0)
    return lax.dynamic_update_slice_in_dim(lax.empty(a.shape, a.dtype), lax.dynamic_slice_in_dim(a, me, 1, 0), me, 0)


def _exchange_start(name, arrs, gather, deps=()):
    n = len(arrs)
    lands = [_own_slot_filled(a, gather) for a in arrs]
    plan = _direct_plan(n, gather)
    bufs, sems, token = _split_call(name, list(arrs) + lands, starts=(plan, n * (N_DEV - 1)), deps=deps)
    return (n, plan, sems, bufs, None), token


def _exchange_wait(name, handle, after):
    return _split_done(name, handle, after)


ICI_PEERS = (2, 4, 6)
SIBLING = 1


def _gather2_send(name, arrs, deps=(), fill_after=False):
    n = len(arrs)
    lands = [lax.empty((N_DEV,) + a.shape, a.dtype) if fill_after else _own_slot_filled(a, True) for a in arrs]

    def plan(refs, me_):
        return [(refs[a], refs[n + a].at[me_], k) for k in (SIBLING,) + ICI_PEERS for a in range(n)]

    bufs, sems, token = _split_call(name, list(arrs) + lands, starts=(plan, 4 * n), deps=deps)
    if fill_after:
        me = _my_index()
        bufs = bufs[:n] + [lax.dynamic_update_slice_in_dim(z, a[None], me, 0) for z, a in zip(bufs[n:], bufs[:n])]
    return (n, plan, sems, bufs, None), token


def _gather2_relay(name, handle, after):
    n, plan, sems, bufs, _ = handle
    after = after if isinstance(after, (list, tuple)) else [after]

    def relay(refs, me_):
        return [(refs[n + a].at[me_ ^ k], refs[n + a].at[me_ ^ k], SIBLING) for k in ICI_PEERS for a in range(n)]

    bufs, sems2, token = _split_call(name, bufs, waits=(sems[0], sems[1], plan), starts=(relay, 3 * n), deps=after)
    return (n, relay, sems2, bufs, None), token


def _split_done(name, handle, after, all_bufs=False):
    n, plan, sems, bufs, _ = handle
    bufs, _, _ = _split_call(name, bufs, waits=(sems[0], sems[1], plan), deps=[after])
    return bufs if all_bufs else bufs[n:]


def _scatter2_pair(name, for_sibling, deps=()):
    n = len(for_sibling)
    pairs = [lax.empty(s.shape, s.dtype) for s in for_sibling]

    def plan(refs, me):
        del me
        return [(refs[a].at[s], refs[n + a].at[s], SIBLING) for s in range(4) for a in range(n)]

    bufs, sems, token = _split_call(name, list(for_sibling) + pairs, starts=(plan, 4 * n), deps=deps)
    return (n, plan, sems, bufs, None), token


def _pair_add(name, mine, pair):
    _, rows, cols = mine.shape
    tr = rows // 2

    def fn(a, b):
        return a.astype(F32) + b.astype(F32)

    blk = pl.BlockSpec((None, tr, cols), lambda s, i: (s, i, 0))
    return _pure_call(name, fn, (4, rows // tr), [blk, blk], [blk], [_sds(mine.shape, mine.dtype)], (mine, pair))[0]


def _scatter2_send(name, chip_sums, deps=()):
    n = len(chip_sums)
    finals = [lax.empty((3,) + c.shape[1:], c.dtype) for c in chip_sums]

    def plan(refs, me):
        return [(refs[a].at[(me >> 1) ^ j], refs[n + a].at[j - 1], 2 * j) for j in range(1, 4) for a in range(n)]

    bufs, sems, token = _split_call(name, list(chip_sums) + finals, starts=(plan, 3 * n), deps=deps)
    return (n, plan, sems, bufs, None), token


def _pad_rope(p):
    z = jnp.zeros(p.shape[:-1] + (32,), p.dtype)
    return jnp.concatenate([p[..., :32], z, p[..., 32:], z], axis=-1)


def _unpad_rope(p):
    return jnp.concatenate([p[..., :32], p[..., 64:96]], axis=-1)


def _odd_in_layout(wt):
    wt = wt.reshape(ODD_IN, D)
    cq, ckv, kpe, gate = wt[:512], wt[512:1024], wt[1024:1088], wt[1088:]
    z = jnp.zeros((32, D), wt.dtype)
    return jnp.concatenate([gate, cq, ckv, kpe[:32], z, kpe[32:], z], axis=0)


def _odd_in_unlayout(dwt):
    gate, cq, ckv, kpe = dwt[:2048], dwt[2048:2560], dwt[2560:3072], dwt[3072:]
    wt = jnp.concatenate([cq, ckv, kpe[:32], kpe[64:96], gate], axis=0)
    return wt.reshape(N_DEV, ODD_IN // N_DEV, D)


def _qb_layout(w):
    w = w.transpose(1, 0, 2).reshape(C_RANK, C_HEADS, C_QK)
    w = jnp.concatenate([w[..., :C_NOPE], _pad_rope(w[..., C_NOPE:])], axis=-1)
    return w.reshape(C_RANK, C_HEADS * QP)


def _qb_unlayout(dw):
    dw = dw.reshape(C_RANK, C_HEADS, QP)
    dw = jnp.concatenate([dw[..., :C_NOPE], _unpad_rope(dw[..., C_NOPE:])], axis=-1)
    return dw.reshape(C_RANK, N_DEV, C_HEADS * C_QK // N_DEV).transpose(1, 0, 2)


def _rope_tables(positions):
    inv_freq = ROPE_THETA ** (-jnp.arange(0, C_ROPE, 2, dtype=F32) / C_ROPE)
    ang = positions.astype(F32)[0][:, None] * inv_freq
    cos, sin = jnp.cos(ang), jnp.sin(ang)
    z = jnp.zeros_like(cos)
    return jnp.concatenate([cos, z, cos, z], axis=1), jnp.concatenate([-sin, z, sin, z], axis=1)


def _forward_backward(x, cos_t, sin_t, target, norm_pre, norm_post, lb_logits, a_onorm, ln_w, ln_b,
                      b_ws, b_bias, get_w, put_g, put_small=None, start_dep=None):
    npre0, npre1 = norm_pre[0:1], norm_pre[1:2]
    npost0, npost1 = norm_post[0:1], norm_post[1:2]
    l0, l1 = lb_logits[0:1], lb_logits[1:2]
    bias_col = b_bias.reshape(B_GROUPS, B_CHUNK, 1)
    ws = b_ws.reshape(B_GROUPS, B_CHUNK, B_CHUNK)

    h0 = _pre_norm("pre_norm0", x, npre0, deps=[start_dep])
    w_ev_in = get_w("ev_in", h0)
    z0 = _mm_nn("ev_in", h0, w_ev_in, F32, 1024, 896)
    cat, sst = _even_fwd(z0, l0, l1, a_onorm, ln_w, ln_b, ws, bias_col)
    w_ev_out = get_w("ev_out", cat)
    y0 = _mm_nn("ev_out", cat, w_ev_out, F32, 1024, 1024)
    get_w("od_relay", y0)
    x1, h1 = _post_pre_norm(x, y0, npost0, npre1)
    w_od_in, w_qb, w_kvb, q_norm, kv_norm = get_w("od_mid", h1)
    z1 = _mm_nt("od_in", h1, w_od_in[None], F32, 1024, 640)
    cqn, ckvn, kp = _mla_pre(z1, q_norm, kv_norm, cos_t, sin_t)
    q = _mm_nn("od_qb", cqn, w_qb[None], F32, 1024, 1024)
    kv = _mm_nn("od_kvb", ckvn, w_kvb, BF16, 1024, 512)
    o, lse, og = _attention_fwd(q, cos_t, sin_t, kv, kp, z1)
    w_od_out = get_w("od_out", og)
    dx2, dy1, loss_part, dnpost1 = _out_proj_loss(og, w_od_out, x1, npost1, target)

    g_od_out = _mm_tn("od_out_dw", og, dy1, 1, BF16, 1024, 1024)
    tok = put_g("od_out", [g_od_out.reshape(N_DEV, D // N_DEV, D)])
    dog = _mm_nt("od_out_dx", dy1, w_od_out, F32, 1024, 1024, deps=[tok])
    dq, dkv, dkp, dgate = _attention_bwd(q, cos_t, sin_t, kv, kp, o, lse, dog, z1)
    g_qb = _mm_tn("od_qb_dw", cqn, dq, 1, F32, 512, 1024)
    g_kvb = _mm_tn("od_kvb_dw", ckvn, dkv, N_DEV, BF16, 512, 512)
    tok = put_g("od_qkv", [_qb_unlayout(g_qb[0]).astype(BF16), g_kvb])
    dz1, dqn, dkvn = _mla_pre_bwd(z1, q_norm, kv_norm, cos_t, sin_t, dq, w_qb[None], dkv, w_kvb, dkp, dgate,
                                  deps=[tok])
    g_od_in = _mm_tn("od_in_dw", dz1, h1, 1, F32, 640, 1024)
    tok = put_g("od_in", [_odd_in_unlayout(g_od_in[0]).astype(BF16)])
    dx1, dy0, dnpost0, dnpre1 = _in_proj_norms_bwd(dz1, w_od_in[None], y0, x1, npost0, npre1, dx2, deps=[tok])

    g_ev_out = _mm_tn("ev_out_dw", cat, dy0, 1, BF16, 1024, 1024)
    tok = put_g("ev_out", [g_ev_out.reshape(N_DEV, D // N_DEV, D)])
    dz0, dl0, dl1, donorm, dlnw, dlnb, dws, dbias = _even_bwd(z0, l0, l1, a_onorm, ln_w, ln_b, ws, bias_col, sst,
                                                              dy0, w_ev_out, deps=[tok])
    early = _small_rows(dnpre1, dnpost0, dnpost1, dl0, dl1, donorm, dlnw, dlnb, dws, dbias, dqn, dkvn, loss_part)
    tok = put_small(early) if put_small else None
    small_tok = tok

    def ev_in_half(name, parity, deps=()):
        return _mm_tn_parity(name, h0, dz0, N_DEV, parity, BF16, 1024, deps=[small_tok] + list(deps))

    tok = put_g("ev_in", ev_in_half)
    dh0 = _mm_nt("ev_in_dx", dz0, w_ev_in, F32, 1024, 256, deps=[tok])
    grad_x, dnpre0 = _pre_norm_bwd(x, npre0, dh0, dx1)
    return grad_x, early, dnpre0


def kernel(x, positions, norm_pre, norm_post, ev_w_in, ev_lb_logits, ev_a_onorm, ev_b_ln_w, ev_b_ln_b, ev_b_ws, ev_b_bias, ev_w_out, od_w_in, od_q_norm, od_w_qb, od_kv_norm, od_w_kvb, od_w_out, loss_target, m_norm_pre, m_norm_post, m_ev_w_in, m_ev_lb_logits, m_ev_a_onorm, m_ev_b_ln_w, m_ev_b_ln_b, m_ev_b_ws, m_ev_b_bias, m_ev_w_out, m_od_w_in, m_od_q_norm, m_od_w_qb, m_od_kv_norm, m_od_w_kvb, m_od_w_out, v_norm_pre, v_norm_post, v_ev_w_in, v_ev_lb_logits, v_ev_a_onorm, v_ev_b_ln_w, v_ev_b_ln_b, v_ev_b_ws, v_ev_b_bias, v_ev_w_out, v_od_w_in, v_od_q_norm, v_od_w_qb, v_od_kv_norm, v_od_w_kvb, v_od_w_out):
    me = 4 * lax.axis_index("x") + 2 * lax.axis_index("y") + lax.axis_index("c")
    bf = lambda w: w[0].astype(BF16)

    norms = jnp.pad(jnp.concatenate([od_q_norm, od_kv_norm], axis=1), ((0, 7), (0, 0)))
    sent = {}
    sent["ev_in"], tok = _gather2_send("gather_ev_in", [bf(ev_w_in)], fill_after=True)
    sent["ev_out"], tok = _gather2_send("gather_ev_out", [bf(ev_w_out)], deps=[tok])
    sent["od"], tok = _gather2_send("gather_od", [od_w_in[0].T.astype(BF16), bf(od_w_qb), bf(od_w_kvb), norms,
                                                 bf(od_w_out)], deps=[tok])
    cos_t, sin_t = _rope_tables(positions)
    od = []

    def get_w(group, after):
        if group == "ev_in":
            relayed, token = _gather2_relay("relay_ev_in", sent["ev_in"], [after, cos_t, sin_t])
            return _split_done("arrived_ev_in", relayed, token)[0]
        if group == "ev_out":
            relayed, token = _gather2_relay("relay_ev_out", sent["ev_out"], after)
            return _split_done("arrived_ev_out", relayed, token)[0].reshape(1, D, D)
        if group == "od_relay":
            sent["od_relayed"], _ = _gather2_relay("relay_od", sent["od"], after)
            return None
        if not od:
            od.extend(_split_done("arrived_od", sent["od_relayed"], after))
        w_od_in, w_qb, w_kvb, norms_all, w_od_out = od
        if group == "od_out":
            return w_od_out.reshape(1, D, D)
        return (_odd_in_layout(w_od_in), _qb_layout(w_qb), w_kvb,
                norms_all[:, 0, :64].reshape(1, C_RANK), norms_all[:, 0, 64:].reshape(1, C_RANK))

    scatters = {}

    def put_g(group, grads):
        if group == "ev_in":
            core = lax.axis_index("c").astype(jnp.int32).reshape(1)
            paired, token = _scatter2_pair("pair_ev_in", [grads("ev_in_dw_sibling", 1 - core)])
            mine = grads("ev_in_dw_own", core, deps=[token])
            pair = _split_done("paired_ev_in", paired, mine)[0]
            scatters[group], token = _scatter2_send("scatter_ev_in", [_pair_add("pair_add_ev_in", mine, pair)])
        else:
            scatters[group], token = _exchange_start("scatter_" + group, grads, False)
        return token

    def put_small(early):
        scatters["small"], token = _exchange_start("gather_small_early", early, True)
        return token

    grad_x, _, dnpre0 = _forward_backward(
        x[0], cos_t, sin_t, loss_target[0], norm_pre, norm_post, ev_lb_logits, ev_a_onorm, ev_b_ln_w,
        ev_b_ln_b, ev_b_ws, ev_b_bias, get_w, put_g, put_small, start_dep=tok)

    big_w = {"ev_w_in": ev_w_in, "ev_w_out": ev_w_out, "od_w_in": od_w_in, "od_w_qb": od_w_qb,
             "od_w_kvb": od_w_kvb, "od_w_out": od_w_out}
    big_m = {"ev_w_in": m_ev_w_in, "ev_w_out": m_ev_w_out, "od_w_in": m_od_w_in, "od_w_qb": m_od_w_qb,
             "od_w_kvb": m_od_w_kvb, "od_w_out": m_od_w_out}
    big_v = {"ev_w_in": v_ev_w_in, "ev_w_out": v_ev_w_out, "od_w_in": v_od_w_in, "od_w_qb": v_od_w_qb,
             "od_w_kvb": v_od_w_kvb, "od_w_out": v_od_w_out}
    big_out = {}
    after = grad_x
    for group, names in (("od_out", ["od_w_out"]), ("od_qkv", ["od_w_qb", "od_w_kvb"]), ("od_in", ["od_w_in"]),
                         ("ev_out", ["ev_w_out"])):
        parts = _exchange_wait("summed_" + group, scatters[group], after)
        for nm, p in zip(names, parts):
            w, m, v = big_w[nm][0], big_m[nm][0], big_v[nm][0]
            if nm == "od_w_in":
                res_t = _adamw("adamw_" + nm, [(p, N_DEV)], w.T, m.T, v.T, w.shape[1], 512)
                big_out[nm] = [r.T[None] for r in res_t]
            else:
                big_out[nm] = [r[None] for r in _adamw("adamw_" + nm, [(p, N_DEV)], w, m, v, w.shape[0] // 8)]
            after = big_out[nm][0]

    late_all = _exchange("gather_small_late", [dnpre0], gather=True, deps=[after])[0]
    early_all = _exchange_wait("arrived_small_early", scatters["small"], late_all)

    small_w = (norm_pre, norm_post, ev_lb_logits, ev_a_onorm, ev_b_ln_w, ev_b_ln_b, ev_b_ws, ev_b_bias)
    small_m = (m_norm_pre, m_norm_post, m_ev_lb_logits, m_ev_a_onorm, m_ev_b_ln_w, m_ev_b_ln_b, m_ev_b_ws, m_ev_b_bias)
    small_v = (v_norm_pre, v_norm_post, v_ev_lb_logits, v_ev_a_onorm, v_ev_b_ln_w, v_ev_b_ln_b, v_ev_b_ws, v_ev_b_bias)
    wmv = [tuple(a.reshape(s) for a in t) for s, t in zip(SMALL_PARAM_SHAPES, zip(small_w, small_m, small_v))]
    small_res, loss_row, g_norm_rows = _adamw_small(late_all, early_all, wmv)
    small_out = [[r.reshape(w.shape) for r in four] for four, w in zip(small_res, small_w)]
    loss = loss_row[0, 0]

    g_norms = jnp.concatenate([lax.dynamic_slice(g_norm_rows, (0, 64 * me), (1, 64)),
                               lax.dynamic_slice(g_norm_rows, (1, 64 * me), (1, 64))], axis=1)
    res_n = _adamw("adamw_norms", [(g_norms[None], 1)],
                   jnp.concatenate([od_q_norm, od_kv_norm], axis=1),
                   jnp.concatenate([m_od_q_norm, m_od_kv_norm], axis=1),
                   jnp.concatenate([v_od_q_norm, v_od_kv_norm], axis=1), 1)
    qn_out = [r[:, :64] for r in res_n]
    kvn_out = [r[:, 64:] for r in res_n]

    chip_sums, from_peers = _split_done("summed_ev_in", scatters["ev_in"], loss_row, all_bufs=True)
    own_chip = lax.dynamic_slice_in_dim(chip_sums, me >> 1, 1, 0)
    w = ev_w_in[0]
    big_out["ev_w_in"] = [r[None] for r in _adamw("adamw_ev_w_in", [(own_chip, 1), (from_peers, 3)], w, m_ev_w_in[0],
                                                  v_ev_w_in[0], w.shape[0] // 8)]

    order = ("norm_pre", "norm_post", "ev_w_in", "ev_lb_logits", "ev_a_onorm", "ev_b_ln_w", "ev_b_ln_b",
             "ev_b_ws", "ev_b_bias", "ev_w_out", "od_w_in", "od_q_norm", "od_w_qb", "od_kv_norm",
             "od_w_kvb", "od_w_out")
    small_names = ("norm_pre", "norm_post", "ev_lb_logits", "ev_a_onorm", "ev_b_ln_w", "ev_b_ln_b",
                   "ev_b_ws", "ev_b_bias")
    outs = [loss, grad_x[None]]
    for kind in range(4):
        for nm in order:
            if nm in big_out:
                outs.append(big_out[nm][kind])
            elif nm == "od_q_norm":
                outs.append(qn_out[kind])
            elif nm == "od_kv_norm":
                outs.append(kvn_out[kind])
            else:
                outs.append(small_out[small_names.index(nm)][kind])
    return tuple(outs)
```

```python
import functools

import jax
import jax.numpy as jnp
from jax import lax
from jax.experimental import pallas as pl
from jax.experimental.pallas import tpu as pltpu

F32 = jnp.float32
BF16 = jnp.bfloat16

N_DEV = 8
T = 2048
D = 2048
EPS = 1e-6
A_HEADS = 8
HD = 128
A_CHUNK = 64
A_SUB = 16
B_GROUPS = 8
B_CHUNK = 128
EVEN_IN = 7168
C_HEADS = 16
C_RANK = 512
C_NOPE = 128
C_ROPE = 64
C_QK = C_NOPE + C_ROPE
C_V = 128
ODD_IN = 3136
ODD_IN_PAD = 3200
QP = 256
ROPE_THETA = 10000.0
ATT_SCALE = C_QK ** -0.5

ADAM_LR = 0.001
ADAM_B1 = 0.9
ADAM_B2 = 0.999
ADAM_EPS = 1e-08
ADAM_WD = 0.01
ADAM_STEP = 10

VMEM_LIMIT_V7X = 56 * 1024 * 1024
MESH_ID = pl.DeviceIdType.MESH


def _params(n_grid):
    return pltpu.CompilerParams(dimension_semantics=("arbitrary",) * n_grid,
                                vmem_limit_bytes=VMEM_LIMIT_V7X)


def _dg(a, b, ca, cb):
    return lax.dot_general(a.astype(BF16), b.astype(BF16), (((ca,), (cb,)), ((), ())),
                           preferred_element_type=F32)


def _raw_nn(a, b):
    return _dg(a, b, 1, 0)


def _raw_nt(a, b):
    return _dg(a, b, 1, 1)


def _raw_tn(a, b):
    return _dg(a, b, 0, 0)


@jax.custom_vjp
def _dot_nn(a, b):
    return _raw_nn(a, b)


def _dot_nn_fwd(a, b):
    return _raw_nn(a, b), (a.astype(BF16), b.astype(BF16))


def _dot_nn_bwd(res, g):
    a, b = res
    return _raw_nt(g, b), _raw_tn(a, g)


_dot_nn.defvjp(_dot_nn_fwd, _dot_nn_bwd)


@jax.custom_vjp
def _dot_nt(a, b):
    return _raw_nt(a, b)


def _dot_nt_fwd(a, b):
    return _raw_nt(a, b), (a.astype(BF16), b.astype(BF16))


def _dot_nt_bwd(res, g):
    a, b = res
    return _raw_nn(g, b), _raw_tn(g, a)


_dot_nt.defvjp(_dot_nt_fwd, _dot_nt_bwd)


@jax.custom_vjp
def _dot_tn(a, b):
    return _raw_tn(a, b)


def _dot_tn_fwd(a, b):
    return _raw_tn(a, b), (a.astype(BF16), b.astype(BF16))


def _dot_tn_bwd(res, g):
    a, b = res
    return _raw_nt(b, g), _raw_nn(a, g)


_dot_tn.defvjp(_dot_tn_fwd, _dot_tn_bwd)


@jax.custom_vjp
def _sigmoid(x):
    e = jnp.exp(-jnp.abs(x))
    return jnp.where(x >= 0, 1.0 / (1.0 + e), e / (1.0 + e))


def _sigmoid_fwd(x):
    s = _sigmoid(x)
    return s, s


def _sigmoid_bwd(s, g):
    return (g * s * (1.0 - s),)


_sigmoid.defvjp(_sigmoid_fwd, _sigmoid_bwd)


def _silu(x):
    return x * _sigmoid(x)


def _rms(x, w):
    return x * lax.rsqrt(jnp.mean(x * x, axis=-1, keepdims=True) + EPS) * w


def _split3(x):
    hi = x.astype(BF16)
    r = x - hi.astype(F32)
    mid = r.astype(BF16)
    lo = (r - mid.astype(F32)).astype(BF16)
    return hi, mid, lo


def _mask_apply(mask_bf16, x, contract):
    out = None
    for piece in _split3(x):
        d = lax.dot_general(mask_bf16, piece, (((contract,), (0,)), ((), ())),
                            preferred_element_type=F32)
        out = d if out is None else out + d
    return out


def _chunk_tri(rows):
    r = lax.broadcasted_iota(jnp.int32, (rows, rows), 0)
    c = lax.broadcasted_iota(jnp.int32, (rows, rows), 1)
    return ((r >= c) & (r // A_CHUNK == c // A_CHUNK)).astype(BF16)


@jax.custom_vjp
def _chunk_cumsum(x):
    return _mask_apply(_chunk_tri(x.shape[0]), x, 1)


def _chunk_cumsum_fwd(x):
    return _chunk_cumsum(x), None


def _chunk_cumsum_bwd(_, g):
    return (_mask_apply(_chunk_tri(g.shape[0]), g, 0),)


_chunk_cumsum.defvjp(_chunk_cumsum_fwd, _chunk_cumsum_bwd)


def _hgrn2_rows(q, zf, v, ga, st, l0, l1, onorm):
    rows = q.shape[0]
    n_sub = A_CHUNK // A_SUB
    mx = jnp.maximum(l0, l1)
    e0 = jnp.exp(l0 - mx)
    e1 = jnp.exp(l1 - mx)
    lb = e0 / (e0 + e1)
    lf = jnp.log(lb + (1.0 - lb) * _sigmoid(zf))
    k = (1.0 - lb) * _sigmoid(-zf)
    b = _chunk_cumsum(lf)

    t_idx = lax.broadcasted_iota(jnp.int32, (A_CHUNK, n_sub * A_CHUNK), 0)
    c_idx = lax.broadcasted_iota(jnp.int32, (A_CHUNK, n_sub * A_CHUNK), 1)
    sel = (c_idx // A_CHUNK == t_idx // A_SUB) & (c_idx % A_CHUNK <= t_idx)
    key_row = lax.broadcasted_iota(jnp.int32, (A_CHUNK, HD), 0)

    outs = []
    for n in range(rows // A_CHUNK):
        lo = n * A_CHUNK
        qc, kc, vc = q[lo:lo + A_CHUNK], k[lo:lo + A_CHUNK], v[lo:lo + A_CHUNK]
        lfc, bc = lf[lo:lo + A_CHUNK], b[lo:lo + A_CHUNK]
        b_last = bc[A_CHUNK - 1:A_CHUNK]
        o_inter = _dot_nt(qc * jnp.exp(bc), st)
        kv_t = _dot_tn(vc, kc * jnp.exp(b_last - bc))
        st = st * jnp.exp(b_last) + kv_t
        g_rows, k_subs = [], []
        for i in range(n_sub):
            g_i = bc[i * A_SUB:i * A_SUB + 1] - lfc[i * A_SUB:i * A_SUB + 1]
            g_rows.append(jnp.broadcast_to(g_i, (A_SUB, HD)))
            expo = jnp.where(key_row < (i + 1) * A_SUB, g_i - bc, -jnp.inf)
            k_subs.append(kc * jnp.exp(expo))
        q_sub = qc * jnp.exp(bc - jnp.concatenate(g_rows, axis=0))
        scores = _dot_nt(q_sub, jnp.concatenate(k_subs, axis=0))
        scores = jnp.where(sel, scores, 0.0)
        o_intra = _dot_nn(scores, jnp.concatenate([vc] * n_sub, axis=0))
        outs.append(o_inter + o_intra)
    o = jnp.concatenate(outs, axis=0)
    return _rms(o, onorm) * _silu(ga), st


def _gmlp_rows(u, vb, gb, lnw, lnb, ws, bias):
    rows = u.shape[0]
    mu = jnp.mean(vb, axis=-1, keepdims=True)
    xc = vb - mu
    vg = xc * lax.rsqrt(jnp.mean(xc * xc, axis=-1, keepdims=True) + EPS) * lnw + lnb
    r = lax.broadcasted_iota(jnp.int32, (B_CHUNK, B_CHUNK), 0)
    c = lax.broadcasted_iota(jnp.int32, (B_CHUNK, B_CHUNK), 1)
    ws_causal = jnp.where(r >= c, ws, 0.0)
    svs = [_dot_nn(ws_causal, vg[n * B_CHUNK:(n + 1) * B_CHUNK]) + bias
           for n in range(rows // B_CHUNK)]
    return u * jnp.concatenate(svs, axis=0) * _silu(gb)


def _rope(x, cos_t, sin_t):
    return x * cos_t + pltpu.roll(x, 64, 1) * sin_t


def _rope_transpose(g, cos_t, sin_t):
    return g * cos_t + pltpu.roll(g * sin_t, 64, 1)


ANY_SPEC = pl.BlockSpec(memory_space=pl.ANY)


def _live(deps):
    return [d for d in deps if d is not None]


def _skip_deps(body, n_in, n_deps):
    def wrapped(*refs):
        return body(*refs[:n_in], *refs[n_in + n_deps:])
    return wrapped


def _pure_call(name, fn, grid, in_specs, out_specs, out_shape, args, n_acc=0, deps=()):
    deps = _live(deps)
    n_in, n_out, n_deps = len(in_specs), len(out_specs), len(deps)
    in_specs = list(in_specs) + [ANY_SPEC] * n_deps
    args = tuple(args) + tuple(deps)

    def body(*refs):
        res = fn(*[r[...] for r in refs[:n_in]])
        if not isinstance(res, (tuple, list)):
            res = (res,)
        outs = refs[n_in + n_deps:n_in + n_deps + n_out]
        for o, r in zip(outs[:n_out - n_acc], res[:n_out - n_acc]):
            o[...] = r.astype(o.dtype)
        if n_acc:
            first = functools.reduce(jnp.logical_and, [pl.program_id(i) == 0 for i in range(len(grid))])
            for o, r in zip(outs[n_out - n_acc:], res[n_out - n_acc:]):
                @pl.when(first)
                def _(o=o, r=r):
                    o[...] = r.astype(o.dtype)

                @pl.when(jnp.logical_not(first))
                def _(o=o, r=r):
                    o[...] += r.astype(o.dtype)

    return pl.pallas_call(body, name=name, grid=grid, in_specs=in_specs, out_specs=out_specs,
                          out_shape=out_shape, compiler_params=_params(len(grid)))(*args)


def _sds(shape, dtype):
    return jax.ShapeDtypeStruct(shape, dtype)


def _row_spec(tm, width, col=0):
    return pl.BlockSpec((tm, width), lambda i, col=col: (i, col))


def _full_spec(shape):
    nd = len(shape)
    return pl.BlockSpec(shape, lambda *_: (0,) * nd)


def _mm_nn(name, a, b, out_dtype, tm, tn, deps=()):
    deps = _live(deps)
    m, k = a.shape
    j, _, n = b.shape
    per = n // tn

    def body(a_ref, b_ref, o_ref):
        o_ref[...] = _raw_nn(a_ref[...], b_ref[...]).astype(o_ref.dtype)

    return pl.pallas_call(
        _skip_deps(body, 2, len(deps)), name=name, grid=(m // tm, j * per),
        in_specs=[pl.BlockSpec((tm, k), lambda i, c: (i, 0)),
                  pl.BlockSpec((None, k, tn), lambda i, c: (c // per, 0, c % per))] + [ANY_SPEC] * len(deps),
        out_specs=pl.BlockSpec((tm, tn), lambda i, c: (i, c)),
        out_shape=_sds((m, j * n), out_dtype), compiler_params=_params(2))(a, b, *deps)


def _mm_nt(name, a, b, out_dtype, tm, tn, deps=()):
    deps = _live(deps)
    m = a.shape[0]
    j, nn, n = b.shape

    def body(a_ref, b_ref, o_ref):
        b_all = b_ref[0] if j == 1 else jnp.concatenate([b_ref[s] for s in range(j)], axis=1)
        o_ref[...] = _raw_nt(a_ref[...], b_all).astype(o_ref.dtype)

    return pl.pallas_call(
        _skip_deps(body, 2, len(deps)), name=name, grid=(m // tm, nn // tn),
        in_specs=[pl.BlockSpec((tm, j * n), lambda i, c: (i, 0)),
                  pl.BlockSpec((j, tn, n), lambda i, c: (0, c, 0))] + [ANY_SPEC] * len(deps),
        out_specs=pl.BlockSpec((tm, tn), lambda i, c: (i, c)),
        out_shape=_sds((m, nn), out_dtype), compiler_params=_params(2))(a, b, *deps)


def _mm_tn(name, a, b, j, out_dtype, tm, tn, deps=()):
    deps = _live(deps)
    k, m = a.shape
    n = b.shape[1] // j
    per = n // tn

    def body(a_ref, b_ref, o_ref):
        o_ref[...] = _raw_tn(a_ref[...], b_ref[...]).astype(o_ref.dtype)

    return pl.pallas_call(
        _skip_deps(body, 2, len(deps)), name=name, grid=(m // tm, j * per),
        in_specs=[pl.BlockSpec((k, tm), lambda i, c: (0, i)),
                  pl.BlockSpec((k, tn), lambda i, c: (0, c))] + [ANY_SPEC] * len(deps),
        out_specs=pl.BlockSpec((None, tm, tn), lambda i, c: (c // per, i, c % per)),
        out_shape=_sds((j, m, n), out_dtype), compiler_params=_params(2))(a, b, *deps)


def _mm_tn_parity(name, a, b, j, parity, out_dtype, tm, deps=()):
    deps = _live(deps)
    k, m = a.shape
    n = b.shape[1] // j

    def body(par_ref, a_ref, b_ref, o_ref):
        del par_ref
        o_ref[...] = _raw_tn(a_ref[...], b_ref[...]).astype(o_ref.dtype)

    grid_spec = pltpu.PrefetchScalarGridSpec(
        num_scalar_prefetch=1, grid=(m // tm, j // 2),
        in_specs=[pl.BlockSpec((k, tm), lambda i, s, par: (0, i)),
                  pl.BlockSpec((k, n), lambda i, s, par: (0, 2 * s + par[0]))] + [ANY_SPEC] * len(deps),
        out_specs=pl.BlockSpec((None, tm, n), lambda i, s, par: (s, i, 0)))
    return pl.pallas_call(
        lambda par_ref, *refs: _skip_deps(functools.partial(body, par_ref), 2, len(deps))(*refs),
        name=name, grid_spec=grid_spec, out_shape=_sds((j // 2, m, n), out_dtype),
        compiler_params=_params(2))(parity, a, b, *deps)


TM = 256


def _pre_norm(name, x, w_row, deps=()):
    def fn(xv, w):
        return _rms(xv, w)
    return _pure_call(name, fn, (T // TM,), [_row_spec(TM, D), _full_spec((1, D))],
                      [_row_spec(TM, D)], [_sds((T, D), BF16)], (x, w_row), deps=deps)[0]


def _post_pre_norm(x, y, w_post, w_pre):
    def fn(xv, yv, wp, wn):
        x1 = xv + _rms(yv, wp)
        return x1, _rms(x1, wn)
    return _pure_call("post_pre_norm", fn, (T // TM,),
                      [_row_spec(TM, D), _row_spec(TM, D), _full_spec((1, D)), _full_spec((1, D))],
                      [_row_spec(TM, D), _row_spec(TM, D)],
                      [_sds((T, D), F32), _sds((T, D), BF16)], (x, y, w_post, w_pre))


def _in_proj_norms_bwd(dz, w_t, y, x1, w_post, w_pre, dx1_in, deps=()):
    def fn(dzv, w, yv, x1v, wp, wn, dx1v):
        _, vjp_pre = jax.vjp(_rms, x1v, wn)
        dx1_h, dwn = vjp_pre(_raw_nn(dzv, w[0]))
        dx1 = dx1v + dx1_h
        _, vjp_post = jax.vjp(_rms, yv, wp)
        dy, dwp = vjp_post(dx1)
        return dx1, dy, dwp, dwn
    return _pure_call("od_in_dx_norms", fn, (T // TM,),
                      [_row_spec(TM, ODD_IN_PAD), _full_spec((1, ODD_IN_PAD, D)), _row_spec(TM, D), _row_spec(TM, D),
                       _full_spec((1, D)), _full_spec((1, D)), _row_spec(TM, D)],
                      [_row_spec(TM, D), _row_spec(TM, D), _full_spec((1, D)), _full_spec((1, D))],
                      [_sds((T, D), F32), _sds((T, D), BF16), _sds((1, D), F32), _sds((1, D), F32)],
                      (dz, w_t, y, x1, w_post, w_pre, dx1_in), n_acc=2, deps=deps)


def _out_proj_loss(og, w_out, x1, w_post, target):
    tm = 512

    def fn(ogv, w, x1v, wp, tv):
        r, vjp = jax.vjp(_rms, _raw_nn(ogv, w[0]), wp)
        err = x1v + r - tv
        part = 0.5 * jnp.sum(jnp.mean(err * err, axis=-1, keepdims=True), axis=0, keepdims=True)
        dx2 = err * (1.0 / D)
        dy, dwp = vjp(dx2)
        return dx2, dy, jnp.broadcast_to(part, (1, 128)), dwp
    return _pure_call("od_out_loss", fn, (T // tm,),
                      [_row_spec(tm, D), _full_spec((1, D, D)), _row_spec(tm, D), _full_spec((1, D)), _row_spec(tm, D)],
                      [_row_spec(tm, D), _row_spec(tm, D), _full_spec((1, 128)), _full_spec((1, D))],
                      [_sds((T, D), F32), _sds((T, D), BF16), _sds((1, 128), F32), _sds((1, D), F32)],
                      (og, w_out, x1, w_post, target), n_acc=2)


def _pre_norm_bwd(x, w_row, dh, dx_res, deps=()):
    def fn(xv, w, dhv, dxv):
        _, vjp = jax.vjp(_rms, xv, w)
        dx, dw = vjp(dhv)
        return dxv + dx, dw
    return _pure_call("pre_norm_bwd", fn, (T // TM,),
                      [_row_spec(TM, D), _full_spec((1, D)), _row_spec(TM, D), _row_spec(TM, D)],
                      [_row_spec(TM, D), _full_spec((1, D))],
                      [_sds((T, D), F32), _sds((1, D), F32)], (x, w_row, dh, dx_res), n_acc=1, deps=deps)


RA = 256


def _head(ref, hh):
    return ref[:, hh * HD:(hh + 1) * HD]


def _z_part(z_ref, k, h):
    lo = (k * A_HEADS + h) * HD
    return z_ref[:, lo:lo + HD]


def _even_specs():
    return [_full_spec((1, A_HEADS * HD)), _full_spec((1, A_HEADS * HD)), _full_spec((1, HD)),
            _full_spec((1, B_GROUPS * HD)), _full_spec((1, B_GROUPS * HD)),
            _full_spec((B_GROUPS, B_CHUNK, B_CHUNK)), _full_spec((B_GROUPS, B_CHUNK, 1))]


def _even_fwd(z, l0, l1, onorm, lnw, lnb, ws, bias):
    nb = T // RA

    def body(z_ref, l0_ref, l1_ref, on_ref, lnw_ref, lnb_ref, ws_ref, bias_ref, cat_ref, sst_ref, st_scr):
        @pl.when(pl.program_id(0) == 0)
        def _():
            st_scr[...] = jnp.zeros_like(st_scr)

        for hh in range(A_HEADS):
            st = st_scr[hh]
            sst_ref[hh] = st
            out, st_new = _hgrn2_rows(_z_part(z_ref, 0, hh), _z_part(z_ref, 1, hh), _z_part(z_ref, 2, hh),
                                      _z_part(z_ref, 3, hh), st, _head(l0_ref, hh), _head(l1_ref, hh), on_ref[...])
            cat_ref[:, hh * HD:(hh + 1) * HD] = out.astype(cat_ref.dtype)
            st_scr[hh] = st_new
        for gg in range(B_GROUPS):
            out = _gmlp_rows(_z_part(z_ref, 4, gg), _z_part(z_ref, 5, gg), _z_part(z_ref, 6, gg),
                             _head(lnw_ref, gg), _head(lnb_ref, gg), ws_ref[gg], bias_ref[gg])
            cat_ref[:, (A_HEADS + gg) * HD:(A_HEADS + gg + 1) * HD] = out.astype(cat_ref.dtype)

    return pl.pallas_call(
        body, name="even_mixers_fwd", grid=(nb,),
        in_specs=[pl.BlockSpec((RA, EVEN_IN), lambda r: (r, 0))] + _even_specs(),
        out_specs=[pl.BlockSpec((RA, 2 * A_HEADS * HD), lambda r: (r, 0)),
                   pl.BlockSpec((A_HEADS, None, HD, HD), lambda r: (0, r, 0, 0))],
        out_shape=[_sds((T, 2 * A_HEADS * HD), BF16), _sds((A_HEADS, nb, HD, HD), F32)],
        scratch_shapes=[pltpu.VMEM((A_HEADS, HD, HD), F32)],
        compiler_params=_params(1))(z, l0, l1, onorm, lnw, lnb, ws, bias)


def _even_bwd(z, l0, l1, onorm, lnw, lnb, ws, bias, sst, dy, w_out, deps=()):
    nb = T // RA
    deps = _live(deps)

    def body(z_ref, l0_ref, l1_ref, on_ref, lnw_ref, lnb_ref, ws_ref, bias_ref, sst_ref, dy_ref, w_ref,
             dz_ref, dl0_ref, dl1_ref, don_ref, dlnw_ref, dlnb_ref, dws_ref, dbias_ref, ds_scr):
        first = pl.program_id(0) == 0

        @pl.when(first)
        def _():
            ds_scr[...] = jnp.zeros_like(ds_scr)

        dcat = _raw_nt(dy_ref[...], w_ref[0])

        def put(k, h, val):
            lo = (k * A_HEADS + h) * HD
            dz_ref[:, lo:lo + HD] = val.astype(dz_ref.dtype)

        sums = []
        don = None
        for hh in range(A_HEADS):
            lanes = slice(hh * HD, (hh + 1) * HD)
            _, vjp = jax.vjp(_hgrn2_rows, _z_part(z_ref, 0, hh), _z_part(z_ref, 1, hh), _z_part(z_ref, 2, hh),
                             _z_part(z_ref, 3, hh), sst_ref[hh], _head(l0_ref, hh), _head(l1_ref, hh), on_ref[...])
            dq, dzf, dv, dga, dst, dl0, dl1, don_h = vjp((dcat[:, lanes], ds_scr[hh]))
            for k, val in enumerate((dq, dzf, dv, dga)):
                put(k, hh, val)
            ds_scr[hh] = dst
            sums += [(dl0_ref, (slice(None), lanes), dl0), (dl1_ref, (slice(None), lanes), dl1)]
            don = don_h if don is None else don + don_h
        sums.append((don_ref, slice(None), don))
        for gg in range(B_GROUPS):
            lanes = slice(gg * HD, (gg + 1) * HD)
            _, vjp = jax.vjp(_gmlp_rows, _z_part(z_ref, 4, gg), _z_part(z_ref, 5, gg), _z_part(z_ref, 6, gg),
                             _head(lnw_ref, gg), _head(lnb_ref, gg), ws_ref[gg], bias_ref[gg])
            du, dv, dg, dlnw, dlnb, dws, dbias = vjp(dcat[:, (A_HEADS + gg) * HD:(A_HEADS + gg + 1) * HD])
            for k, val in enumerate((du, dv, dg)):
                put(4 + k, gg, val)
            sums += [(dlnw_ref, (slice(None), lanes), dlnw), (dlnb_ref, (slice(None), lanes), dlnb),
                     (dws_ref, gg, dws), (dbias_ref, gg, dbias)]
        for ref, idx, val in sums:
            @pl.when(first)
            def _(ref=ref, idx=idx, val=val):
                ref[idx] = val

            @pl.when(jnp.logical_not(first))
            def _(ref=ref, idx=idx, val=val):
                ref[idx] += val

    small = _even_specs()
    return pl.pallas_call(
        _skip_deps(body, 11, len(deps)), name="even_mixers_bwd", grid=(nb,),
        in_specs=[pl.BlockSpec((RA, EVEN_IN), lambda r: (nb - 1 - r, 0))] + small
        + [pl.BlockSpec((A_HEADS, None, HD, HD), lambda r: (0, nb - 1 - r, 0, 0)),
           pl.BlockSpec((RA, D), lambda r: (nb - 1 - r, 0)), _full_spec((1, 2 * A_HEADS * HD, D))]
        + [ANY_SPEC] * len(deps),
        out_specs=[pl.BlockSpec((RA, EVEN_IN), lambda r: (nb - 1 - r, 0))] + small,
        out_shape=[_sds((T, EVEN_IN), BF16), _sds((1, A_HEADS * HD), F32), _sds((1, A_HEADS * HD), F32),
                   _sds((1, HD), F32), _sds((1, B_GROUPS * HD), F32), _sds((1, B_GROUPS * HD), F32),
                   _sds((B_GROUPS, B_CHUNK, B_CHUNK), F32), _sds((B_GROUPS, B_CHUNK, 1), F32)],
        scratch_shapes=[pltpu.VMEM((A_HEADS, HD, HD), F32)],
        compiler_params=_params(1))(z, l0, l1, onorm, lnw, lnb, ws, bias, sst, dy, w_out, *deps)


def _mla_pre(z1, qn, kvn, cos_t, sin_t):
    def fn(cq, ckv, kpe, cs, sn, wq, wkv):
        return _rms(cq, wq), _rms(ckv, wkv), _rope(kpe, cs, sn)
    return _pure_call("mla_pre", fn, (T // TM,),
                      [_row_spec(TM, C_RANK, 4), _row_spec(TM, C_RANK, 5), _row_spec(TM, HD, 24),
                       _row_spec(TM, HD), _row_spec(TM, HD),
                       _full_spec((1, C_RANK)), _full_spec((1, C_RANK))],
                      [_row_spec(TM, C_RANK), _row_spec(TM, C_RANK), _row_spec(TM, HD)],
                      [_sds((T, C_RANK), BF16), _sds((T, C_RANK), BF16), _sds((T, HD), BF16)],
                      (z1, z1, z1, cos_t, sin_t, qn, kvn))


def _mla_pre_bwd(z1, qn, kvn, cos_t, sin_t, dq, w_qb, dkv, w_kvb, dkp, dgate, deps=()):
    def fn(cq, ckv, cs, sn, wq, wkv, dqv, wqb, dkvv, wkvb, g_kp, g_gate):
        _, vjp_q = jax.vjp(_rms, cq, wq)
        dcq, dwq = vjp_q(_raw_nt(dqv, wqb[0]))
        _, vjp_kv = jax.vjp(_rms, ckv, wkv)
        dckv, dwkv = vjp_kv(_raw_nt(dkvv, jnp.concatenate([wkvb[s] for s in range(N_DEV)], axis=1)))
        dz1 = jnp.concatenate([g_gate, dcq.astype(BF16), dckv.astype(BF16),
                               _rope_transpose(g_kp, cs, sn).astype(BF16)], axis=1)
        return dz1, dwq, dwkv
    return _pure_call("mla_pre_bwd", fn, (T // TM,),
                      [_row_spec(TM, C_RANK, 4), _row_spec(TM, C_RANK, 5),
                       _row_spec(TM, HD), _row_spec(TM, HD),
                       _full_spec((1, C_RANK)), _full_spec((1, C_RANK)),
                       _row_spec(TM, C_HEADS * QP), _full_spec((1, C_RANK, C_HEADS * QP)),
                       _row_spec(TM, C_HEADS * KVW), _full_spec((N_DEV, C_RANK, C_HEADS * KVW // N_DEV)),
                       _row_spec(TM, HD), _row_spec(TM, D)],
                      [_row_spec(TM, ODD_IN_PAD), _full_spec((1, C_RANK)), _full_spec((1, C_RANK))],
                      [_sds((T, ODD_IN_PAD), BF16), _sds((1, C_RANK), F32), _sds((1, C_RANK), F32)],
                      (z1, z1, cos_t, sin_t, qn, kvn, dq, w_qb, dkv, w_kvb, dkp, dgate), n_acc=2, deps=deps)


TQ = 256
HP = 2
KVW = C_NOPE + C_V


def _att_keys(kv_ref, kp_ref, k_scr):
    @pl.when(pl.program_id(1) == 0)
    def _():
        for hh in range(HP):
            k_scr[hh, :, 0:C_NOPE] = kv_ref[:, hh * KVW:hh * KVW + C_NOPE]
            k_scr[hh, :, C_NOPE:QP] = kp_ref[...]


def _att_scores(q, cos_ref, sin_ref, k_scr, hh, n):
    keys = (n + 1) * TQ
    qr = jnp.concatenate([q[:, :C_NOPE], _rope(q[:, C_NOPE:], cos_ref[...], sin_ref[...])], axis=1).astype(BF16)
    return qr, _raw_nt(qr, k_scr[hh, 0:keys, :]) * ATT_SCALE


def _causal(x, n, fill):
    row = lax.broadcasted_iota(jnp.int32, (TQ, TQ), 0)
    col = lax.broadcasted_iota(jnp.int32, (TQ, TQ), 1)
    diag = jnp.where(col <= row, x[:, n * TQ:], fill)
    return diag if n == 0 else jnp.concatenate([x[:, :n * TQ], diag], axis=1)


def _per_query_block(fn):
    for n in range(T // TQ):
        pl.when(pl.program_id(1) == n)(functools.partial(fn, n))


def _att_in_specs():
    return [pl.BlockSpec((TQ, HP * QP), lambda g, i: (i, g)),
            pl.BlockSpec((TQ, HD), lambda g, i: (i, 0)),
            pl.BlockSpec((TQ, HD), lambda g, i: (i, 0)),
            pl.BlockSpec((T, HP * KVW), lambda g, i: (0, g)),
            pl.BlockSpec((T, HD), lambda g, i: (0, 0))]


def _attention_fwd(q, cos_t, sin_t, kv, kp, z1):
    def body(q_ref, cos_ref, sin_ref, kv_ref, kp_ref, gate_ref, o_ref, lse_ref, og_ref, k_scr):
        _att_keys(kv_ref, kp_ref, k_scr)

        def block(n):
            keys = (n + 1) * TQ
            for hh in range(HP):
                _, s = _att_scores(q_ref[:, hh * QP:(hh + 1) * QP], cos_ref, sin_ref, k_scr, hh, n)
                s = _causal(s, n, jnp.finfo(F32).min)
                m = jnp.max(s, axis=-1, keepdims=True)
                p = jnp.exp(s - m)
                l = jnp.sum(p, axis=-1, keepdims=True)
                v = kv_ref[0:keys, hh * KVW + C_NOPE:(hh + 1) * KVW]
                o = _raw_nn(p, v) / l
                lanes = slice(hh * C_V, (hh + 1) * C_V)
                o_ref[:, lanes] = o
                og_ref[:, lanes] = (o * _silu(gate_ref[:, lanes])).astype(og_ref.dtype)
                lse_ref[hh] = m + jnp.log(l)

        _per_query_block(block)

    heads = pl.BlockSpec((TQ, HP * C_V), lambda g, i: (i, g))
    return pl.pallas_call(
        body, name="attention_fwd", grid=(C_HEADS // HP, T // TQ), in_specs=_att_in_specs() + [heads],
        out_specs=[heads, pl.BlockSpec((HP, TQ, 1), lambda g, i: (g, i, 0)), heads],
        out_shape=[_sds((T, C_HEADS * C_V), F32), _sds((C_HEADS, T, 1), F32), _sds((T, C_HEADS * C_V), BF16)],
        scratch_shapes=[pltpu.VMEM((HP, T, QP), BF16)],
        compiler_params=_params(2))(q, cos_t, sin_t, kv, kp, z1)


def _attention_bwd(q, cos_t, sin_t, kv, kp, o, lse, dog, z1):
    nq = T // TQ

    def body(q_ref, cos_ref, sin_ref, kv_ref, kp_ref, o_ref, lse_ref, dog_ref, gate_ref,
             dq_ref, dkv_ref, dkp_ref, dgate_ref, k_scr, dk_scr, dv_scr):
        g, i = pl.program_id(0), pl.program_id(1)
        _att_keys(kv_ref, kp_ref, k_scr)

        @pl.when(i == 0)
        def _():
            dv_scr[...] = jnp.zeros_like(dv_scr)
            dk_scr[...] = jnp.zeros_like(dk_scr)

        def block(n):
            keys = (n + 1) * TQ
            for hh in range(HP):
                qr, s = _att_scores(q_ref[:, hh * QP:(hh + 1) * QP], cos_ref, sin_ref, k_scr, hh, n)
                p = _causal(jnp.exp(s - lse_ref[hh]), n, 0.0)
                lanes = slice(hh * C_V, (hh + 1) * C_V)
                ov, gate, dogv = o_ref[:, lanes], gate_ref[:, lanes], dog_ref[:, lanes]
                sig = _sigmoid(gate)
                silu = gate * sig
                dov = dogv * silu
                dgate_ref[:, lanes] = (dogv * ov * (sig + silu * (1.0 - sig))).astype(dgate_ref.dtype)
                delta = jnp.sum(dov * ov, axis=-1, keepdims=True)
                dp = _raw_nt(dov, kv_ref[0:keys, hh * KVW + C_NOPE:(hh + 1) * KVW])
                ds = p * (dp - delta) * ATT_SCALE
                dq = _raw_nn(ds, k_scr[hh, 0:keys, :])
                dq_ref[:, hh * QP:(hh + 1) * QP] = jnp.concatenate(
                    [dq[:, :C_NOPE], _rope_transpose(dq[:, C_NOPE:], cos_ref[...], sin_ref[...])],
                    axis=1).astype(dq_ref.dtype)
                dv_scr[hh, 0:keys, :] += _raw_tn(p, dov)
                dk_scr[hh, 0:keys, :] += _raw_tn(ds, qr)

        _per_query_block(block)

        @pl.when(i == nq - 1)
        def _():
            for hh in range(HP):
                dkv_ref[:, hh * KVW:(hh + 1) * KVW] = jnp.concatenate(
                    [dk_scr[hh, :, 0:C_NOPE], dv_scr[hh]], axis=1).astype(dkv_ref.dtype)

        @pl.when(jnp.logical_and(i == nq - 1, g == 0))
        def _():
            dkp_ref[...] = dk_scr[0, :, C_NOPE:QP]

        @pl.when(jnp.logical_and(i == nq - 1, g > 0))
        def _():
            dkp_ref[...] += dk_scr[0, :, C_NOPE:QP]

        @pl.when(i == nq - 1)
        def _():
            for hh in range(1, HP):
                dkp_ref[...] += dk_scr[hh, :, C_NOPE:QP]

    heads = pl.BlockSpec((TQ, HP * C_V), lambda g, i: (i, g))
    return pl.pallas_call(
        body, name="attention_bwd", grid=(C_HEADS // HP, nq),
        in_specs=_att_in_specs() + [heads, pl.BlockSpec((HP, TQ, 1), lambda g, i: (g, i, 0)), heads, heads],
        out_specs=[pl.BlockSpec((TQ, HP * QP), lambda g, i: (i, g)),
                   pl.BlockSpec((T, HP * KVW), lambda g, i: (0, g)),
                   _full_spec((T, HD)), heads],
        out_shape=[_sds((T, C_HEADS * QP), BF16), _sds((T, C_HEADS * KVW), BF16), _sds((T, HD), F32),
                   _sds((T, C_HEADS * C_V), BF16)],
        scratch_shapes=[pltpu.VMEM((HP, T, QP), BF16), pltpu.VMEM((HP, T, QP), F32), pltpu.VMEM((HP, T, C_V), F32)],
        compiler_params=_params(2))(q, cos_t, sin_t, kv, kp, o, lse, dog, z1)


def _adamw_math(w, g, m, v):
    m = ADAM_B1 * m + (1.0 - ADAM_B1) * g
    v = ADAM_B2 * v + (1.0 - ADAM_B2) * (g * g)
    m_hat = m / (1.0 - ADAM_B1 ** ADAM_STEP)
    v_hat = v / (1.0 - ADAM_B2 ** ADAM_STEP)
    delta = -ADAM_LR * (m_hat / (jnp.sqrt(v_hat) + ADAM_EPS) + ADAM_WD * w)
    return delta, m, v


def _adamw(name, parts, w, m, v, tr, tc=None):
    rows, cols = w.shape

    def fn(*vals):
        pvs, (wv, mv, vv) = vals[:len(parts)], vals[len(parts):]
        g = None
        for pv in pvs:
            for d in range(pv.shape[0]):
                term = pv[d].astype(F32)
                g = term if g is None else g + term
        return (g,) + _adamw_math(wv, g, mv, vv)

    tc = cols if tc is None else tc
    blk = pl.BlockSpec((tr, tc), lambda i, j: (i, j))
    part_specs = [pl.BlockSpec((n, tr, tc), lambda i, j: (0, i, j)) for _, n in parts]
    return _pure_call(name, fn, (rows // tr, cols // tc), part_specs + [blk, blk, blk],
                      [blk] * 4, [_sds((rows, cols), F32)] * 4, tuple(p for p, _ in parts) + (w, m, v))


SMALL_PARAM_SHAPES = ((2, D), (2, D), (2, A_HEADS * HD), (1, HD), (1, B_GROUPS * HD), (1, B_GROUPS * HD),
                      (B_GROUPS, B_CHUNK, B_CHUNK), (B_GROUPS, B_CHUNK))
SMALL_PIECES = ((0, 0, 0, 0), (0, 1, 1, 0), (1, 0, 1, 1), (1, 1, 1, 2), (2, 0, 2, 0), (2, 1, 2, 1),
                (3, 0, 3, 8), (4, 0, 2, 2), (5, 0, 2, 3))


def _small_rows(dnpre1, dnpost0, dnpost1, dl0, dl1, donorm, dlnw, dlnb, dws, dbias, dqn, dkvn, loss_part):
    return [jnp.concatenate([dnpre1, dnpost0, dnpost1], axis=0),
            jnp.concatenate([dl0, dl1, dlnw, dlnb], axis=0),
            jnp.concatenate([dbias.reshape(B_GROUPS, B_CHUNK), donorm, loss_part], axis=0),
            dws,
            jnp.concatenate([dqn, dkvn], axis=0)]


def _adamw_small(late_all, early_all, wmv):
    n_in = 6 + 3 * len(wmv)

    def body(*refs):
        gathered, params, outs = refs[:6], refs[6:n_in], refs[n_in:]

        def total(ref):
            s = ref[0]
            for d in range(1, N_DEV):
                s = s + ref[d]
            return s

        g_late, g2048, g1024, g128, g_ws, g512 = [total(r) for r in gathered]
        arrays = (g_late, g2048, g1024, g128)

        def update(p, rows, g):
            w_ref, m_ref, v_ref = params[3 * p:3 * p + 3]
            delta, m, v = _adamw_math(w_ref[rows], g, m_ref[rows], v_ref[rows])
            for out, val in zip(outs[4 * p:4 * p + 4], (g, delta, m, v)):
                out[rows] = val

        for p, row, arr, arr_row in SMALL_PIECES:
            update(p, pl.ds(row, 1), arrays[arr][arr_row:arr_row + 1])
        update(6, slice(None), g_ws)
        update(7, slice(None), g128[0:B_GROUPS])
        outs[32][...] = g128[B_GROUPS + 1:B_GROUPS + 2]
        outs[33][...] = g512

    vmem = pl.BlockSpec(memory_space=pltpu.VMEM)
    flat = [a for t in wmv for a in t]
    out_shape = [_sds(s, F32) for s in SMALL_PARAM_SHAPES for _ in range(4)] + [_sds((1, 128), F32), _sds((2, C_RANK), F32)]
    res = pl.pallas_call(body, name="adamw_small", in_specs=[vmem] * n_in, out_specs=[vmem] * len(out_shape),
                         out_shape=out_shape,
                         compiler_params=pltpu.CompilerParams(vmem_limit_bytes=VMEM_LIMIT_V7X))(late_all, *early_all, *flat)
    return [res[4 * p:4 * p + 4] for p in range(8)], res[32], res[33]


def _exchange(name, arrs, gather, deps=()):
    n = len(arrs)
    deps = _live(deps)

    def body(*refs):
        ins, outs = refs[:n], refs[n + len(deps):2 * n + len(deps)]
        send_sems, recv_sems, local_sems = refs[2 * n + len(deps):]
        x, y, c = lax.axis_index("x"), lax.axis_index("y"), lax.axis_index("c")
        me = 4 * x + 2 * y + c

        def peer(k):
            return (x ^ (k >> 2), y ^ ((k >> 1) & 1), c ^ (k & 1))

        def copy(a, k):
            src = ins[a] if gather else ins[a].at[me ^ k]
            return pltpu.make_async_remote_copy(
                src_ref=src, dst_ref=outs[a].at[me], send_sem=send_sems.at[a, k - 1],
                recv_sem=recv_sems.at[a, k - 1], device_id=peer(k), device_id_type=MESH_ID)

        def arrival(a, k):
            src = ins[a] if gather else ins[a].at[me]
            return pltpu.make_async_remote_copy(
                src_ref=src, dst_ref=outs[a].at[me ^ k], send_sem=send_sems.at[a, k - 1],
                recv_sem=recv_sems.at[a, k - 1], device_id=peer(k), device_id_type=MESH_ID)

        own = [pltpu.make_async_copy(ins[a] if gather else ins[a].at[me], outs[a].at[me], local_sems.at[a])
               for a in range(n)]
        for cp in own:
            cp.start()
        for k in range(1, N_DEV):
            for a in range(n):
                copy(a, k).start()
        for k in range(1, N_DEV):
            for a in range(n):
                arrival(a, k).wait_recv()
        for k in range(1, N_DEV):
            for a in range(n):
                copy(a, k).wait_send()
        for cp in own:
            cp.wait()

    any_spec = pl.BlockSpec(memory_space=pl.ANY)
    out_shape = [_sds((N_DEV,) + a.shape if gather else a.shape, a.dtype) for a in arrs]
    return pl.pallas_call(
        body, name=name, in_specs=[any_spec] * (n + len(deps)), out_specs=[any_spec] * n, out_shape=out_shape,
        scratch_shapes=[pltpu.SemaphoreType.DMA((n, N_DEV - 1)), pltpu.SemaphoreType.DMA((n, N_DEV - 1)),
                        pltpu.SemaphoreType.DMA((n,))],
        compiler_params=pltpu.CompilerParams(has_side_effects=True))(*arrs, *deps)


HBM_SPEC = pl.BlockSpec(memory_space=pltpu.HBM)
SEM_SPEC = pl.BlockSpec(memory_space=pltpu.SEMAPHORE)
DATAFLOW = pltpu.SideEffectType.DATAFLOW_SIDE_EFFECTING


def _my_index():
    return 4 * lax.axis_index("x") + 2 * lax.axis_index("y") + lax.axis_index("c")


def _plan_copies(plan, refs, send_sems, recv_sems):
    x, y, c = lax.axis_index("x"), lax.axis_index("y"), lax.axis_index("c")
    return [pltpu.make_async_remote_copy(
        src_ref=src, dst_ref=dst, send_sem=send_sems.at[i], recv_sem=recv_sems.at[i],
        device_id=(x ^ (k >> 2), y ^ ((k >> 1) & 1), c ^ (k & 1)), device_id_type=MESH_ID)
        for i, (src, dst, k) in enumerate(plan(refs, 4 * x + 2 * y + c))]


def _split_call(name, bufs, waits=None, starts=None, deps=()):
    n = len(bufs)
    deps = _live(deps)
    n_wait = 2 if waits else 0

    def body(*refs):
        zones = refs[:n]
        if waits:
            for cp in _plan_copies(waits[2], zones, refs[n], refs[n + 1]):
                cp.wait_send()
                cp.wait_recv()
        if starts:
            first_out = n + n_wait + len(deps)
            for cp in _plan_copies(starts[0], zones, refs[first_out], refs[first_out + 1]):
                cp.start()
            refs[-1][...] = jnp.zeros_like(refs[-1])

    out_specs, out_shape = [], []
    if starts:
        sems = pltpu.SemaphoreType.DMA((starts[1],))
        out_specs, out_shape = [SEM_SPEC, SEM_SPEC], [sems, sems]
    out_specs += [HBM_SPEC] * n
    out_shape += [pltpu.HBM(b.shape, b.dtype) for b in bufs]
    if starts:
        out_specs.append(pl.BlockSpec(memory_space=pltpu.VMEM))
        out_shape.append(_sds((8, 128), F32))
    first_buf = 2 if starts else 0
    res = pl.pallas_call(
        body, name=name,
        in_specs=[HBM_SPEC] * n + [SEM_SPEC] * n_wait + [ANY_SPEC] * len(deps),
        out_specs=out_specs, out_shape=out_shape,
        input_output_aliases={i: first_buf + i for i in range(n)},
        compiler_params=pltpu.CompilerParams(has_side_effects=DATAFLOW),
    )(*[pltpu.with_memory_space_constraint(b, pltpu.HBM) for b in bufs], *(waits[:2] if waits else ()), *deps)
    out_bufs = list(res[first_buf:first_buf + n])
    return out_bufs, ((res[0], res[1]) if starts else None), (res[-1] if starts else None)


def _direct_plan(n, gather):
    def plan(refs, me):
        return [(refs[a] if gather else refs[a].at[me ^ k], refs[n + a].at[me], k)
                for k in range(1, N_DEV) for a in range(n)]
    return plan


def _own_slot_filled(a, gather):
    me = _my_index()
    if gather:
        return lax.dynamic_update_slice_in_dim(lax.empty((N_DEV,) + a.shape, a.dtype), a[None], me, 0)
    return lax.dynamic_update_slice_in_dim(lax.empty(a.shape, a.dtype), lax.dynamic_slice_in_dim(a, me, 1, 0), me, 0)


def _exchange_start(name, arrs, gather, deps=()):
    n = len(arrs)
    lands = [_own_slot_filled(a, gather) for a in arrs]
    plan = _direct_plan(n, gather)
    bufs, sems, token = _split_call(name, list(arrs) + lands, starts=(plan, n * (N_DEV - 1)), deps=deps)
    return (n, plan, sems, bufs, None), token


def _exchange_wait(name, handle, after):
    return _split_done(name, handle, after)


ICI_PEERS = (2, 4, 6)
SIBLING = 1


def _gather2_send(name, arrs, deps=(), fill_after=False):
    n = len(arrs)
    lands = [lax.empty((N_DEV,) + a.shape, a.dtype) if fill_after else _own_slot_filled(a, True) for a in arrs]

    def plan(refs, me_):
        return [(refs[a], refs[n + a].at[me_], k) for k in (SIBLING,) + ICI_PEERS for a in range(n)]

    bufs, sems, token = _split_call(name, list(arrs) + lands, starts=(plan, 4 * n), deps=deps)
    if fill_after:
        me = _my_index()
        bufs = bufs[:n] + [lax.dynamic_update_slice_in_dim(z, a[None], me, 0) for z, a in zip(bufs[n:], bufs[:n])]
    return (n, plan, sems, bufs, None), token


def _gather2_relay(name, handle, after):
    n, plan, sems, bufs, _ = handle
    after = after if isinstance(after, (list, tuple)) else [after]

    def relay(refs, me_):
        return [(refs[n + a].at[me_ ^ k], refs[n + a].at[me_ ^ k], SIBLING) for k in ICI_PEERS for a in range(n)]

    bufs, sems2, token = _split_call(name, bufs, waits=(sems[0], sems[1], plan), starts=(relay, 3 * n), deps=after)
    return (n, relay, sems2, bufs, None), token


def _split_done(name, handle, after, all_bufs=False):
    n, plan, sems, bufs, _ = handle
    bufs, _, _ = _split_call(name, bufs, waits=(sems[0], sems[1], plan), deps=[after])
    return bufs if all_bufs else bufs[n:]


def _scatter2_pair(name, for_sibling, deps=()):
    n = len(for_sibling)
    pairs = [lax.empty(s.shape, s.dtype) for s in for_sibling]

    def plan(refs, me):
        del me
        return [(refs[a].at[s], refs[n + a].at[s], SIBLING) for s in range(4) for a in range(n)]

    bufs, sems, token = _split_call(name, list(for_sibling) + pairs, starts=(plan, 4 * n), deps=deps)
    return (n, plan, sems, bufs, None), token


def _pair_add(name, mine, pair):
    _, rows, cols = mine.shape
    tr = rows // 2

    def fn(a, b):
        return a.astype(F32) + b.astype(F32)

    blk = pl.BlockSpec((None, tr, cols), lambda s, i: (s, i, 0))
    return _pure_call(name, fn, (4, rows // tr), [blk, blk], [blk], [_sds(mine.shape, mine.dtype)], (mine, pair))[0]


def _scatter2_send(name, chip_sums, deps=()):
    n = len(chip_sums)
    finals = [lax.empty((3,) + c.shape[1:], c.dtype) for c in chip_sums]

    def plan(refs, me):
        return [(refs[a].at[(me >> 1) ^ j], refs[n + a].at[j - 1], 2 * j) for j in range(1, 4) for a in range(n)]

    bufs, sems, token = _split_call(name, list(chip_sums) + finals, starts=(plan, 3 * n), deps=deps)
    return (n, plan, sems, bufs, None), token


def _pad_rope(p):
    z = jnp.zeros(p.shape[:-1] + (32,), p.dtype)
    return jnp.concatenate([p[..., :32], z, p[..., 32:], z], axis=-1)


def _unpad_rope(p):
    return jnp.concatenate([p[..., :32], p[..., 64:96]], axis=-1)


def _odd_in_layout(wt):
    wt = wt.reshape(ODD_IN, D)
    cq, ckv, kpe, gate = wt[:512], wt[512:1024], wt[1024:1088], wt[1088:]
    z = jnp.zeros((32, D), wt.dtype)
    return jnp.concatenate([gate, cq, ckv, kpe[:32], z, kpe[32:], z], axis=0)


def _odd_in_unlayout(dwt):
    gate, cq, ckv, kpe = dwt[:2048], dwt[2048:2560], dwt[2560:3072], dwt[3072:]
    wt = jnp.concatenate([cq, ckv, kpe[:32], kpe[64:96], gate], axis=0)
    return wt.reshape(N_DEV, ODD_IN // N_DEV, D)


def _qb_layout(w):
    w = w.transpose(1, 0, 2).reshape(C_RANK, C_HEADS, C_QK)
    w = jnp.concatenate([w[..., :C_NOPE], _pad_rope(w[..., C_NOPE:])], axis=-1)
    return w.reshape(C_RANK, C_HEADS * QP)


def _qb_unlayout(dw):
    dw = dw.reshape(C_RANK, C_HEADS, QP)
    dw = jnp.concatenate([dw[..., :C_NOPE], _unpad_rope(dw[..., C_NOPE:])], axis=-1)
    return dw.reshape(C_RANK, N_DEV, C_HEADS * C_QK // N_DEV).transpose(1, 0, 2)


def _rope_tables(positions):
    inv_freq = ROPE_THETA ** (-jnp.arange(0, C_ROPE, 2, dtype=F32) / C_ROPE)
    ang = positions.astype(F32)[0][:, None] * inv_freq
    cos, sin = jnp.cos(ang), jnp.sin(ang)
    z = jnp.zeros_like(cos)
    return jnp.concatenate([cos, z, cos, z], axis=1), jnp.concatenate([-sin, z, sin, z], axis=1)


def _forward_backward(x, cos_t, sin_t, target, norm_pre, norm_post, lb_logits, a_onorm, ln_w, ln_b,
                      b_ws, b_bias, get_w, put_g, put_small=None, start_dep=None):
    npre0, npre1 = norm_pre[0:1], norm_pre[1:2]
    npost0, npost1 = norm_post[0:1], norm_post[1:2]
    l0, l1 = lb_logits[0:1], lb_logits[1:2]
    bias_col = b_bias.reshape(B_GROUPS, B_CHUNK, 1)
    ws = b_ws.reshape(B_GROUPS, B_CHUNK, B_CHUNK)

    h0 = _pre_norm("pre_norm0", x, npre0, deps=[start_dep])
    w_ev_in = get_w("ev_in", h0)
    z0 = _mm_nn("ev_in", h0, w_ev_in, F32, 1024, 896)
    cat, sst = _even_fwd(z0, l0, l1, a_onorm, ln_w, ln_b, ws, bias_col)
    w_ev_out = get_w("ev_out", cat)
    y0 = _mm_nn("ev_out", cat, w_ev_out, F32, 1024, 1024)
    get_w("od_relay", y0)
    x1, h1 = _post_pre_norm(x, y0, npost0, npre1)
    w_od_in, w_qb, w_kvb, q_norm, kv_norm = get_w("od_mid", h1)
    z1 = _mm_nt("od_in", h1, w_od_in[None], F32, 1024, 640)
    cqn, ckvn, kp = _mla_pre(z1, q_norm, kv_norm, cos_t, sin_t)
    q = _mm_nn("od_qb", cqn, w_qb[None], F32, 1024, 1024)
    kv = _mm_nn("od_kvb", ckvn, w_kvb, BF16, 1024, 512)
    o, lse, og = _attention_fwd(q, cos_t, sin_t, kv, kp, z1)
    w_od_out = get_w("od_out", og)
    dx2, dy1, loss_part, dnpost1 = _out_proj_loss(og, w_od_out, x1, npost1, target)

    g_od_out = _mm_tn("od_out_dw", og, dy1, 1, BF16, 1024, 1024)
    tok = put_g("od_out", [g_od_out.reshape(N_DEV, D // N_DEV, D)])
    dog = _mm_nt("od_out_dx", dy1, w_od_out, F32, 1024, 1024, deps=[tok])
    dq, dkv, dkp, dgate = _attention_bwd(q, cos_t, sin_t, kv, kp, o, lse, dog, z1)
    g_qb = _mm_tn("od_qb_dw", cqn, dq, 1, BF16, 512, 1024)
    g_kvb = _mm_tn("od_kvb_dw", ckvn, dkv, N_DEV, BF16, 512, 512)
    tok = put_g("od_qkv", [_qb_unlayout(g_qb[0]), g_kvb])
    dz1, dqn, dkvn = _mla_pre_bwd(z1, q_norm, kv_norm, cos_t, sin_t, dq, w_qb[None], dkv, w_kvb, dkp, dgate,
                                  deps=[tok])
    g_od_in = _mm_tn("od_in_dw", dz1, h1, 1, BF16, 640, 1024)
    tok = put_g("od_in", [_odd_in_unlayout(g_od_in[0])])
    dx1, dy0, dnpost0, dnpre1 = _in_proj_norms_bwd(dz1, w_od_in[None], y0, x1, npost0, npre1, dx2, deps=[tok])

    g_ev_out = _mm_tn("ev_out_dw", cat, dy0, 1, BF16, 1024, 1024)
    tok = put_g("ev_out", [g_ev_out.reshape(N_DEV, D // N_DEV, D)])
    dz0, dl0, dl1, donorm, dlnw, dlnb, dws, dbias = _even_bwd(z0, l0, l1, a_onorm, ln_w, ln_b, ws, bias_col, sst,
                                                              dy0, w_ev_out, deps=[tok])
    early = _small_rows(dnpre1, dnpost0, dnpost1, dl0, dl1, donorm, dlnw, dlnb, dws, dbias, dqn, dkvn, loss_part)
    tok = put_small(early) if put_small else None
    small_tok = tok

    def ev_in_half(name, parity, deps=()):
        return _mm_tn_parity(name, h0, dz0, N_DEV, parity, BF16, 1024, deps=[small_tok] + list(deps))

    tok = put_g("ev_in", ev_in_half)
    dh0 = _mm_nt("ev_in_dx", dz0, w_ev_in, F32, 1024, 256, deps=[tok])
    grad_x, dnpre0 = _pre_norm_bwd(x, npre0, dh0, dx1)
    return grad_x, early, dnpre0


def kernel(x, positions, norm_pre, norm_post, ev_w_in, ev_lb_logits, ev_a_onorm, ev_b_ln_w, ev_b_ln_b, ev_b_ws, ev_b_bias, ev_w_out, od_w_in, od_q_norm, od_w_qb, od_kv_norm, od_w_kvb, od_w_out, loss_target, m_norm_pre, m_norm_post, m_ev_w_in, m_ev_lb_logits, m_ev_a_onorm, m_ev_b_ln_w, m_ev_b_ln_b, m_ev_b_ws, m_ev_b_bias, m_ev_w_out, m_od_w_in, m_od_q_norm, m_od_w_qb, m_od_kv_norm, m_od_w_kvb, m_od_w_out, v_norm_pre, v_norm_post, v_ev_w_in, v_ev_lb_logits, v_ev_a_onorm, v_ev_b_ln_w, v_ev_b_ln_b, v_ev_b_ws, v_ev_b_bias, v_ev_w_out, v_od_w_in, v_od_q_norm, v_od_w_qb, v_od_kv_norm, v_od_w_kvb, v_od_w_out):
    me = 4 * lax.axis_index("x") + 2 * lax.axis_index("y") + lax.axis_index("c")
    bf = lambda w: w[0].astype(BF16)

    norms = jnp.pad(jnp.concatenate([od_q_norm, od_kv_norm], axis=1), ((0, 7), (0, 0)))
    sent = {}
    sent["ev_in"], tok = _gather2_send("gather_ev_in", [bf(ev_w_in)], fill_after=True)
    sent["ev_out"], tok = _gather2_send("gather_ev_out", [bf(ev_w_out)], deps=[tok])
    sent["od"], tok = _gather2_send("gather_od", [od_w_in[0].T.astype(BF16), bf(od_w_qb), bf(od_w_kvb), norms,
                                                 bf(od_w_out)], deps=[tok])
    cos_t, sin_t = _rope_tables(positions)
    od = []

    def get_w(group, after):
        if group == "ev_in":
            relayed, token = _gather2_relay("relay_ev_in", sent["ev_in"], [after, cos_t, sin_t])
            return _split_done("arrived_ev_in", relayed, token)[0]
        if group == "ev_out":
            relayed, token = _gather2_relay("relay_ev_out", sent["ev_out"], after)
            return _split_done("arrived_ev_out", relayed, token)[0].reshape(1, D, D)
        if group == "od_relay":
            sent["od_relayed"], _ = _gather2_relay("relay_od", sent["od"], after)
            return None
        if not od:
            od.extend(_split_done("arrived_od", sent["od_relayed"], after))
        w_od_in, w_qb, w_kvb, norms_all, w_od_out = od
        if group == "od_out":
            return w_od_out.reshape(1, D, D)
        return (_odd_in_layout(w_od_in), _qb_layout(w_qb), w_kvb,
                norms_all[:, 0, :64].reshape(1, C_RANK), norms_all[:, 0, 64:].reshape(1, C_RANK))

    scatters = {}

    def put_g(group, grads):
        if group == "ev_in":
            core = lax.axis_index("c").astype(jnp.int32).reshape(1)
            paired, token = _scatter2_pair("pair_ev_in", [grads("ev_in_dw_sibling", 1 - core)])
            mine = grads("ev_in_dw_own", core, deps=[token])
            pair = _split_done("paired_ev_in", paired, mine)[0]
            scatters[group], token = _scatter2_send("scatter_ev_in", [_pair_add("pair_add_ev_in", mine, pair)])
        else:
            scatters[group], token = _exchange_start("scatter_" + group, grads, False)
        return token

    def put_small(early):
        scatters["small"], token = _exchange_start("gather_small_early", early, True)
        return token

    grad_x, _, dnpre0 = _forward_backward(
        x[0], cos_t, sin_t, loss_target[0], norm_pre, norm_post, ev_lb_logits, ev_a_onorm, ev_b_ln_w,
        ev_b_ln_b, ev_b_ws, ev_b_bias, get_w, put_g, put_small, start_dep=tok)

    big_w = {"ev_w_in": ev_w_in, "ev_w_out": ev_w_out, "od_w_in": od_w_in, "od_w_qb": od_w_qb,
             "od_w_kvb": od_w_kvb, "od_w_out": od_w_out}
    big_m = {"ev_w_in": m_ev_w_in, "ev_w_out": m_ev_w_out, "od_w_in": m_od_w_in, "od_w_qb": m_od_w_qb,
             "od_w_kvb": m_od_w_kvb, "od_w_out": m_od_w_out}
    big_v = {"ev_w_in": v_ev_w_in, "ev_w_out": v_ev_w_out, "od_w_in": v_od_w_in, "od_w_qb": v_od_w_qb,
             "od_w_kvb": v_od_w_kvb, "od_w_out": v_od_w_out}
    big_out = {}
    after = grad_x
    for group, names in (("od_out", ["od_w_out"]), ("od_qkv", ["od_w_qb", "od_w_kvb"]), ("od_in", ["od_w_in"]),
                         ("ev_out", ["ev_w_out"])):
        parts = _exchange_wait("summed_" + group, scatters[group], after)
        for nm, p in zip(names, parts):
            w, m, v = big_w[nm][0], big_m[nm][0], big_v[nm][0]
            if nm == "od_w_in":
                res_t = _adamw("adamw_" + nm, [(p, N_DEV)], w.T, m.T, v.T, w.shape[1], 512)
                big_out[nm] = [r.T[None] for r in res_t]
            else:
                big_out[nm] = [r[None] for r in _adamw("adamw_" + nm, [(p, N_DEV)], w, m, v, w.shape[0] // 8)]
            after = big_out[nm][0]

    late_all = _exchange("gather_small_late", [dnpre0], gather=True, deps=[after])[0]
    early_all = _exchange_wait("arrived_small_early", scatters["small"], late_all)

    small_w = (norm_pre, norm_post, ev_lb_logits, ev_a_onorm, ev_b_ln_w, ev_b_ln_b, ev_b_ws, ev_b_bias)
    small_m = (m_norm_pre, m_norm_post, m_ev_lb_logits, m_ev_a_onorm, m_ev_b_ln_w, m_ev_b_ln_b, m_ev_b_ws, m_ev_b_bias)
    small_v = (v_norm_pre, v_norm_post, v_ev_lb_logits, v_ev_a_onorm, v_ev_b_ln_w, v_ev_b_ln_b, v_ev_b_ws, v_ev_b_bias)
    wmv = [tuple(a.reshape(s) for a in t) for s, t in zip(SMALL_PARAM_SHAPES, zip(small_w, small_m, small_v))]
    small_res, loss_row, g_norm_rows = _adamw_small(late_all, early_all, wmv)
    small_out = [[r.reshape(w.shape) for r in four] for four, w in zip(small_res, small_w)]
    loss = loss_row[0, 0]

    g_norms = jnp.concatenate([lax.dynamic_slice(g_norm_rows, (0, 64 * me), (1, 64)),
                               lax.dynamic_slice(g_norm_rows, (1, 64 * me), (1, 64))], axis=1)
    res_n = _adamw("adamw_norms", [(g_norms[None], 1)],
                   jnp.concatenate([od_q_norm, od_kv_norm], axis=1),
                   jnp.concatenate([m_od_q_norm, m_od_kv_norm], axis=1),
                   jnp.concatenate([v_od_q_norm, v_od_kv_norm], axis=1), 1)
    qn_out = [r[:, :64] for r in res_n]
    kvn_out = [r[:, 64:] for r in res_n]

    chip_sums, from_peers = _split_done("summed_ev_in", scatters["ev_in"], loss_row, all_bufs=True)
    own_chip = lax.dynamic_slice_in_dim(chip_sums, me >> 1, 1, 0)
    w = ev_w_in[0]
    big_out["ev_w_in"] = [r[None] for r in _adamw("adamw_ev_w_in", [(own_chip, 1), (from_peers, 3)], w, m_ev_w_in[0],
                                                  v_ev_w_in[0], w.shape[0] // 8)]

    order = ("norm_pre", "norm_post", "ev_w_in", "ev_lb_logits", "ev_a_onorm", "ev_b_ln_w", "ev_b_ln_b",
             "ev_b_ws", "ev_b_bias", "ev_w_out", "od_w_in", "od_q_norm", "od_w_qb", "od_kv_norm",
             "od_w_kvb", "od_w_out")
    small_names = ("norm_pre", "norm_post", "ev_lb_logits", "ev_a_onorm", "ev_b_ln_w", "ev_b_ln_b",
                   "ev_b_ws", "ev_b_bias")
    outs = [loss, grad_x[None]]
    for kind in range(4):
        for nm in order:
            if nm in big_out:
                outs.append(big_out[nm][kind])
            elif nm == "od_q_norm":
                outs.append(qn_out[kind])
            elif nm == "od_kv_norm":
                outs.append(kvn_out[kind])
            else:
                outs.append(small_out[small_names.index(nm)][kind])
    return tuple(outs)
```

```python
import functools

import jax
import jax.numpy as jnp
from jax import lax
from jax.experimental import pallas as pl
from jax.experimental.pallas import tpu as pltpu

F32 = jnp.float32
BF16 = jnp.bfloat16

N_DEV = 8
T = 2048
D = 2048
EPS = 1e-6
A_HEADS = 8
HD = 128
A_CHUNK = 64
A_SUB = 16
B_GROUPS = 8
B_CHUNK = 128
EVEN_IN = 7168
C_HEADS = 16
C_RANK = 512
C_NOPE = 128
C_ROPE = 64
C_QK = C_NOPE + C_ROPE
C_V = 128
ODD_IN = 3136
ODD_IN_PAD = 3200
QP = 256
ROPE_THETA = 10000.0
ATT_SCALE = C_QK ** -0.5

ADAM_LR = 0.001
ADAM_B1 = 0.9
ADAM_B2 = 0.999
ADAM_EPS = 1e-08
ADAM_WD = 0.01
ADAM_STEP = 10

VMEM_LIMIT_V7X = 56 * 1024 * 1024
MESH_ID = pl.DeviceIdType.MESH


def _params(n_grid):
    return pltpu.CompilerParams(dimension_semantics=("arbitrary",) * n_grid,
                                vmem_limit_bytes=VMEM_LIMIT_V7X)


def _dg(a, b, ca, cb):
    return lax.dot_general(a.astype(BF16), b.astype(BF16), (((ca,), (cb,)), ((), ())),
                           preferred_element_type=F32)


def _raw_nn(a, b):
    return _dg(a, b, 1, 0)


def _raw_nt(a, b):
    return _dg(a, b, 1, 1)


def _raw_tn(a, b):
    return _dg(a, b, 0, 0)


@jax.custom_vjp
def _dot_nn(a, b):
    return _raw_nn(a, b)


def _dot_nn_fwd(a, b):
    return _raw_nn(a, b), (a.astype(BF16), b.astype(BF16))


def _dot_nn_bwd(res, g):
    a, b = res
    return _raw_nt(g, b), _raw_tn(a, g)


_dot_nn.defvjp(_dot_nn_fwd, _dot_nn_bwd)


@jax.custom_vjp
def _dot_nt(a, b):
    return _raw_nt(a, b)


def _dot_nt_fwd(a, b):
    return _raw_nt(a, b), (a.astype(BF16), b.astype(BF16))


def _dot_nt_bwd(res, g):
    a, b = res
    return _raw_nn(g, b), _raw_tn(g, a)


_dot_nt.defvjp(_dot_nt_fwd, _dot_nt_bwd)


@jax.custom_vjp
def _dot_tn(a, b):
    return _raw_tn(a, b)


def _dot_tn_fwd(a, b):
    return _raw_tn(a, b), (a.astype(BF16), b.astype(BF16))


def _dot_tn_bwd(res, g):
    a, b = res
    return _raw_nt(b, g), _raw_nn(a, g)


_dot_tn.defvjp(_dot_tn_fwd, _dot_tn_bwd)


@jax.custom_vjp
def _sigmoid(x):
    e = jnp.exp(-jnp.abs(x))
    return jnp.where(x >= 0, 1.0 / (1.0 + e), e / (1.0 + e))


def _sigmoid_fwd(x):
    s = _sigmoid(x)
    return s, s


def _sigmoid_bwd(s, g):
    return (g * s * (1.0 - s),)


_sigmoid.defvjp(_sigmoid_fwd, _sigmoid_bwd)


def _silu(x):
    return x * _sigmoid(x)


def _rms(x, w):
    return x * lax.rsqrt(jnp.mean(x * x, axis=-1, keepdims=True) + EPS) * w


def _split3(x):
    hi = x.astype(BF16)
    r = x - hi.astype(F32)
    mid = r.astype(BF16)
    lo = (r - mid.astype(F32)).astype(BF16)
    return hi, mid, lo


def _mask_apply(mask_bf16, x, contract):
    out = None
    for piece in _split3(x):
        d = lax.dot_general(mask_bf16, piece, (((contract,), (0,)), ((), ())),
                            preferred_element_type=F32)
        out = d if out is None else out + d
    return out


def _chunk_tri(rows):
    r = lax.broadcasted_iota(jnp.int32, (rows, rows), 0)
    c = lax.broadcasted_iota(jnp.int32, (rows, rows), 1)
    return ((r >= c) & (r // A_CHUNK == c // A_CHUNK)).astype(BF16)


@jax.custom_vjp
def _chunk_cumsum(x):
    return _mask_apply(_chunk_tri(x.shape[0]), x, 1)


def _chunk_cumsum_fwd(x):
    return _chunk_cumsum(x), None


def _chunk_cumsum_bwd(_, g):
    return (_mask_apply(_chunk_tri(g.shape[0]), g, 0),)


_chunk_cumsum.defvjp(_chunk_cumsum_fwd, _chunk_cumsum_bwd)


def _hgrn2_rows(q, zf, v, ga, st, l0, l1, onorm):
    rows = q.shape[0]
    n_sub = A_CHUNK // A_SUB
    mx = jnp.maximum(l0, l1)
    e0 = jnp.exp(l0 - mx)
    e1 = jnp.exp(l1 - mx)
    lb = e0 / (e0 + e1)
    lf = jnp.log(lb + (1.0 - lb) * _sigmoid(zf))
    k = (1.0 - lb) * _sigmoid(-zf)
    b = _chunk_cumsum(lf)

    t_idx = lax.broadcasted_iota(jnp.int32, (A_CHUNK, n_sub * A_CHUNK), 0)
    c_idx = lax.broadcasted_iota(jnp.int32, (A_CHUNK, n_sub * A_CHUNK), 1)
    sel = (c_idx // A_CHUNK == t_idx // A_SUB) & (c_idx % A_CHUNK <= t_idx)
    key_row = lax.broadcasted_iota(jnp.int32, (A_CHUNK, HD), 0)

    outs = []
    for n in range(rows // A_CHUNK):
        lo = n * A_CHUNK
        qc, kc, vc = q[lo:lo + A_CHUNK], k[lo:lo + A_CHUNK], v[lo:lo + A_CHUNK]
        lfc, bc = lf[lo:lo + A_CHUNK], b[lo:lo + A_CHUNK]
        b_last = bc[A_CHUNK - 1:A_CHUNK]
        o_inter = _dot_nt(qc * jnp.exp(bc), st)
        kv_t = _dot_tn(vc, kc * jnp.exp(b_last - bc))
        st = st * jnp.exp(b_last) + kv_t
        g_rows, k_subs = [], []
        for i in range(n_sub):
            g_i = bc[i * A_SUB:i * A_SUB + 1] - lfc[i * A_SUB:i * A_SUB + 1]
            g_rows.append(jnp.broadcast_to(g_i, (A_SUB, HD)))
            expo = jnp.where(key_row < (i + 1) * A_SUB, g_i - bc, -jnp.inf)
            k_subs.append(kc * jnp.exp(expo))
        q_sub = qc * jnp.exp(bc - jnp.concatenate(g_rows, axis=0))
        scores = _dot_nt(q_sub, jnp.concatenate(k_subs, axis=0))
        scores = jnp.where(sel, scores, 0.0)
        o_intra = _dot_nn(scores, jnp.concatenate([vc] * n_sub, axis=0))
        outs.append(o_inter + o_intra)
    o = jnp.concatenate(outs, axis=0)
    return _rms(o, onorm) * _silu(ga), st


def _gmlp_rows(u, vb, gb, lnw, lnb, ws, bias):
    rows = u.shape[0]
    mu = jnp.mean(vb, axis=-1, keepdims=True)
    xc = vb - mu
    vg = xc * lax.rsqrt(jnp.mean(xc * xc, axis=-1, keepdims=True) + EPS) * lnw + lnb
    r = lax.broadcasted_iota(jnp.int32, (B_CHUNK, B_CHUNK), 0)
    c = lax.broadcasted_iota(jnp.int32, (B_CHUNK, B_CHUNK), 1)
    ws_causal = jnp.where(r >= c, ws, 0.0)
    svs = [_dot_nn(ws_causal, vg[n * B_CHUNK:(n + 1) * B_CHUNK]) + bias
           for n in range(rows // B_CHUNK)]
    return u * jnp.concatenate(svs, axis=0) * _silu(gb)


def _rope(x, cos_t, sin_t):
    return x * cos_t + pltpu.roll(x, 64, 1) * sin_t


def _rope_transpose(g, cos_t, sin_t):
    return g * cos_t + pltpu.roll(g * sin_t, 64, 1)


ANY_SPEC = pl.BlockSpec(memory_space=pl.ANY)


def _live(deps):
    return [d for d in deps if d is not None]


def _skip_deps(body, n_in, n_deps):
    def wrapped(*refs):
        return body(*refs[:n_in], *refs[n_in + n_deps:])
    return wrapped


def _pure_call(name, fn, grid, in_specs, out_specs, out_shape, args, n_acc=0, deps=()):
    deps = _live(deps)
    n_in, n_out, n_deps = len(in_specs), len(out_specs), len(deps)
    in_specs = list(in_specs) + [ANY_SPEC] * n_deps
    args = tuple(args) + tuple(deps)

    def body(*refs):
        res = fn(*[r[...] for r in refs[:n_in]])
        if not isinstance(res, (tuple, list)):
            res = (res,)
        outs = refs[n_in + n_deps:n_in + n_deps + n_out]
        for o, r in zip(outs[:n_out - n_acc], res[:n_out - n_acc]):
            o[...] = r.astype(o.dtype)
        if n_acc:
            first = functools.reduce(jnp.logical_and, [pl.program_id(i) == 0 for i in range(len(grid))])
            for o, r in zip(outs[n_out - n_acc:], res[n_out - n_acc:]):
                @pl.when(first)
                def _(o=o, r=r):
                    o[...] = r.astype(o.dtype)

                @pl.when(jnp.logical_not(first))
                def _(o=o, r=r):
                    o[...] += r.astype(o.dtype)

    return pl.pallas_call(body, name=name, grid=grid, in_specs=in_specs, out_specs=out_specs,
                          out_shape=out_shape, compiler_params=_params(len(grid)))(*args)


def _sds(shape, dtype):
    return jax.ShapeDtypeStruct(shape, dtype)


def _row_spec(tm, width, col=0):
    return pl.BlockSpec((tm, width), lambda i, col=col: (i, col))


def _full_spec(shape):
    nd = len(shape)
    return pl.BlockSpec(shape, lambda *_: (0,) * nd)


def _mm_nn(name, a, b, out_dtype, tm, tn, deps=()):
    deps = _live(deps)
    m, k = a.shape
    j, _, n = b.shape
    per = n // tn

    def body(a_ref, b_ref, o_ref):
        o_ref[...] = _raw_nn(a_ref[...], b_ref[...]).astype(o_ref.dtype)

    return pl.pallas_call(
        _skip_deps(body, 2, len(deps)), name=name, grid=(m // tm, j * per),
        in_specs=[pl.BlockSpec((tm, k), lambda i, c: (i, 0)),
                  pl.BlockSpec((None, k, tn), lambda i, c: (c // per, 0, c % per))] + [ANY_SPEC] * len(deps),
        out_specs=pl.BlockSpec((tm, tn), lambda i, c: (i, c)),
        out_shape=_sds((m, j * n), out_dtype), compiler_params=_params(2))(a, b, *deps)


def _mm_nt(name, a, b, out_dtype, tm, tn, deps=()):
    deps = _live(deps)
    m = a.shape[0]
    j, nn, n = b.shape

    def body(a_ref, b_ref, o_ref):
        b_all = b_ref[0] if j == 1 else jnp.concatenate([b_ref[s] for s in range(j)], axis=1)
        o_ref[...] = _raw_nt(a_ref[...], b_all).astype(o_ref.dtype)

    return pl.pallas_call(
        _skip_deps(body, 2, len(deps)), name=name, grid=(m // tm, nn // tn),
        in_specs=[pl.BlockSpec((tm, j * n), lambda i, c: (i, 0)),
                  pl.BlockSpec((j, tn, n), lambda i, c: (0, c, 0))] + [ANY_SPEC] * len(deps),
        out_specs=pl.BlockSpec((tm, tn), lambda i, c: (i, c)),
        out_shape=_sds((m, nn), out_dtype), compiler_params=_params(2))(a, b, *deps)


def _mm_tn(name, a, b, j, out_dtype, tm, tn, deps=()):
    deps = _live(deps)
    k, m = a.shape
    n = b.shape[1] // j
    per = n // tn

    def body(a_ref, b_ref, o_ref):
        o_ref[...] = _raw_tn(a_ref[...], b_ref[...]).astype(o_ref.dtype)

    return pl.pallas_call(
        _skip_deps(body, 2, len(deps)), name=name, grid=(m // tm, j * per),
        in_specs=[pl.BlockSpec((k, tm), lambda i, c: (0, i)),
                  pl.BlockSpec((k, tn), lambda i, c: (0, c))] + [ANY_SPEC] * len(deps),
        out_specs=pl.BlockSpec((None, tm, tn), lambda i, c: (c // per, i, c % per)),
        out_shape=_sds((j, m, n), out_dtype), compiler_params=_params(2))(a, b, *deps)


def _mm_tn_parity(name, a, b, j, parity, out_dtype, tm, deps=()):
    deps = _live(deps)
    k, m = a.shape
    n = b.shape[1] // j

    def body(par_ref, a_ref, b_ref, o_ref):
        del par_ref
        o_ref[...] = _raw_tn(a_ref[...], b_ref[...]).astype(o_ref.dtype)

    grid_spec = pltpu.PrefetchScalarGridSpec(
        num_scalar_prefetch=1, grid=(m // tm, j // 2),
        in_specs=[pl.BlockSpec((k, tm), lambda i, s, par: (0, i)),
                  pl.BlockSpec((k, n), lambda i, s, par: (0, 2 * s + par[0]))] + [ANY_SPEC] * len(deps),
        out_specs=pl.BlockSpec((None, tm, n), lambda i, s, par: (s, i, 0)))
    return pl.pallas_call(
        lambda par_ref, *refs: _skip_deps(functools.partial(body, par_ref), 2, len(deps))(*refs),
        name=name, grid_spec=grid_spec, out_shape=_sds((j // 2, m, n), out_dtype),
        compiler_params=_params(2))(parity, a, b, *deps)


TM = 256


def _pre_norm(name, x, w_row, deps=()):
    def fn(xv, w):
        return _rms(xv, w)
    return _pure_call(name, fn, (T // TM,), [_row_spec(TM, D), _full_spec((1, D))],
                      [_row_spec(TM, D)], [_sds((T, D), BF16)], (x, w_row), deps=deps)[0]


def _post_pre_norm(x, y, w_post, w_pre):
    def fn(xv, yv, wp, wn):
        x1 = xv + _rms(yv, wp)
        return x1, _rms(x1, wn)
    return _pure_call("post_pre_norm", fn, (T // TM,),
                      [_row_spec(TM, D), _row_spec(TM, D), _full_spec((1, D)), _full_spec((1, D))],
                      [_row_spec(TM, D), _row_spec(TM, D)],
                      [_sds((T, D), F32), _sds((T, D), BF16)], (x, y, w_post, w_pre))


def _in_proj_norms_bwd(dz, w_t, y, x1, w_post, w_pre, dx1_in, deps=()):
    def fn(dzv, w, yv, x1v, wp, wn, dx1v):
        _, vjp_pre = jax.vjp(_rms, x1v, wn)
        dx1_h, dwn = vjp_pre(_raw_nn(dzv, w[0]))
        dx1 = dx1v + dx1_h
        _, vjp_post = jax.vjp(_rms, yv, wp)
        dy, dwp = vjp_post(dx1)
        return dx1, dy, dwp, dwn
    return _pure_call("od_in_dx_norms", fn, (T // TM,),
                      [_row_spec(TM, ODD_IN_PAD), _full_spec((1, ODD_IN_PAD, D)), _row_spec(TM, D), _row_spec(TM, D),
                       _full_spec((1, D)), _full_spec((1, D)), _row_spec(TM, D)],
                      [_row_spec(TM, D), _row_spec(TM, D), _full_spec((1, D)), _full_spec((1, D))],
                      [_sds((T, D), F32), _sds((T, D), BF16), _sds((1, D), F32), _sds((1, D), F32)],
                      (dz, w_t, y, x1, w_post, w_pre, dx1_in), n_acc=2, deps=deps)


def _out_proj_loss(og, w_out, x1, w_post, target):
    tm = 512

    def fn(ogv, w, x1v, wp, tv):
        r, vjp = jax.vjp(_rms, _raw_nn(ogv, w[0]), wp)
        err = x1v + r - tv
        part = 0.5 * jnp.sum(jnp.mean(err * err, axis=-1, keepdims=True), axis=0, keepdims=True)
        dx2 = err * (1.0 / D)
        dy, dwp = vjp(dx2)
        return dx2, dy, jnp.broadcast_to(part, (1, 128)), dwp
    return _pure_call("od_out_loss", fn, (T // tm,),
                      [_row_spec(tm, D), _full_spec((1, D, D)), _row_spec(tm, D), _full_spec((1, D)), _row_spec(tm, D)],
                      [_row_spec(tm, D), _row_spec(tm, D), _full_spec((1, 128)), _full_spec((1, D))],
                      [_sds((T, D), F32), _sds((T, D), BF16), _sds((1, 128), F32), _sds((1, D), F32)],
                      (og, w_out, x1, w_post, target), n_acc=2)


def _pre_norm_bwd(x, w_row, dh, dx_res, deps=()):
    def fn(xv, w, dhv, dxv):
        _, vjp = jax.vjp(_rms, xv, w)
        dx, dw = vjp(dhv)
        return dxv + dx, dw
    return _pure_call("pre_norm_bwd", fn, (T // TM,),
                      [_row_spec(TM, D), _full_spec((1, D)), _row_spec(TM, D), _row_spec(TM, D)],
                      [_row_spec(TM, D), _full_spec((1, D))],
                      [_sds((T, D), F32), _sds((1, D), F32)], (x, w_row, dh, dx_res), n_acc=1, deps=deps)


RA = 256


def _head(ref, hh):
    return ref[:, hh * HD:(hh + 1) * HD]


def _z_part(z_ref, k, h):
    lo = (k * A_HEADS + h) * HD
    return z_ref[:, lo:lo + HD]


def _even_specs():
    return [_full_spec((1, A_HEADS * HD)), _full_spec((1, A_HEADS * HD)), _full_spec((1, HD)),
            _full_spec((1, B_GROUPS * HD)), _full_spec((1, B_GROUPS * HD)),
            _full_spec((B_GROUPS, B_CHUNK, B_CHUNK)), _full_spec((B_GROUPS, B_CHUNK, 1))]


def _even_fwd(z, l0, l1, onorm, lnw, lnb, ws, bias):
    nb = T // RA

    def body(z_ref, l0_ref, l1_ref, on_ref, lnw_ref, lnb_ref, ws_ref, bias_ref, cat_ref, sst_ref, st_scr):
        @pl.when(pl.program_id(0) == 0)
        def _():
            st_scr[...] = jnp.zeros_like(st_scr)

        for hh in range(A_HEADS):
            st = st_scr[hh]
            sst_ref[hh] = st
            out, st_new = _hgrn2_rows(_z_part(z_ref, 0, hh), _z_part(z_ref, 1, hh), _z_part(z_ref, 2, hh),
                                      _z_part(z_ref, 3, hh), st, _head(l0_ref, hh), _head(l1_ref, hh), on_ref[...])
            cat_ref[:, hh * HD:(hh + 1) * HD] = out.astype(cat_ref.dtype)
            st_scr[hh] = st_new
        for gg in range(B_GROUPS):
            out = _gmlp_rows(_z_part(z_ref, 4, gg), _z_part(z_ref, 5, gg), _z_part(z_ref, 6, gg),
                             _head(lnw_ref, gg), _head(lnb_ref, gg), ws_ref[gg], bias_ref[gg])
            cat_ref[:, (A_HEADS + gg) * HD:(A_HEADS + gg + 1) * HD] = out.astype(cat_ref.dtype)

    return pl.pallas_call(
        body, name="even_mixers_fwd", grid=(nb,),
        in_specs=[pl.BlockSpec((RA, EVEN_IN), lambda r: (r, 0))] + _even_specs(),
        out_specs=[pl.BlockSpec((RA, 2 * A_HEADS * HD), lambda r: (r, 0)),
                   pl.BlockSpec((A_HEADS, None, HD, HD), lambda r: (0, r, 0, 0))],
        out_shape=[_sds((T, 2 * A_HEADS * HD), BF16), _sds((A_HEADS, nb, HD, HD), F32)],
        scratch_shapes=[pltpu.VMEM((A_HEADS, HD, HD), F32)],
        compiler_params=_params(1))(z, l0, l1, onorm, lnw, lnb, ws, bias)


def _even_bwd(z, l0, l1, onorm, lnw, lnb, ws, bias, sst, dy, w_out, deps=()):
    nb = T // RA
    deps = _live(deps)

    def body(z_ref, l0_ref, l1_ref, on_ref, lnw_ref, lnb_ref, ws_ref, bias_ref, sst_ref, dy_ref, w_ref,
             dz_ref, dl0_ref, dl1_ref, don_ref, dlnw_ref, dlnb_ref, dws_ref, dbias_ref, ds_scr):
        first = pl.program_id(0) == 0

        @pl.when(first)
        def _():
            ds_scr[...] = jnp.zeros_like(ds_scr)

        dcat = _raw_nt(dy_ref[...], w_ref[0])

        def put(k, h, val):
            lo = (k * A_HEADS + h) * HD
            dz_ref[:, lo:lo + HD] = val.astype(dz_ref.dtype)

        sums = []
        don = None
        for hh in range(A_HEADS):
            lanes = slice(hh * HD, (hh + 1) * HD)
            _, vjp = jax.vjp(_hgrn2_rows, _z_part(z_ref, 0, hh), _z_part(z_ref, 1, hh), _z_part(z_ref, 2, hh),
                             _z_part(z_ref, 3, hh), sst_ref[hh], _head(l0_ref, hh), _head(l1_ref, hh), on_ref[...])
            dq, dzf, dv, dga, dst, dl0, dl1, don_h = vjp((dcat[:, lanes], ds_scr[hh]))
            for k, val in enumerate((dq, dzf, dv, dga)):
                put(k, hh, val)
            ds_scr[hh] = dst
            sums += [(dl0_ref, (slice(None), lanes), dl0), (dl1_ref, (slice(None), lanes), dl1)]
            don = don_h if don is None else don + don_h
        sums.append((don_ref, slice(None), don))
        for gg in range(B_GROUPS):
            lanes = slice(gg * HD, (gg + 1) * HD)
            _, vjp = jax.vjp(_gmlp_rows, _z_part(z_ref, 4, gg), _z_part(z_ref, 5, gg), _z_part(z_ref, 6, gg),
                             _head(lnw_ref, gg), _head(lnb_ref, gg), ws_ref[gg], bias_ref[gg])
            du, dv, dg, dlnw, dlnb, dws, dbias = vjp(dcat[:, (A_HEADS + gg) * HD:(A_HEADS + gg + 1) * HD])
            for k, val in enumerate((du, dv, dg)):
                put(4 + k, gg, val)
            sums += [(dlnw_ref, (slice(None), lanes), dlnw), (dlnb_ref, (slice(None), lanes), dlnb),
                     (dws_ref, gg, dws), (dbias_ref, gg, dbias)]
        for ref, idx, val in sums:
            @pl.when(first)
            def _(ref=ref, idx=idx, val=val):
                ref[idx] = val

            @pl.when(jnp.logical_not(first))
            def _(ref=ref, idx=idx, val=val):
                ref[idx] += val

    small = _even_specs()
    return pl.pallas_call(
        _skip_deps(body, 11, len(deps)), name="even_mixers_bwd", grid=(nb,),
        in_specs=[pl.BlockSpec((RA, EVEN_IN), lambda r: (nb - 1 - r, 0))] + small
        + [pl.BlockSpec((A_HEADS, None, HD, HD), lambda r: (0, nb - 1 - r, 0, 0)),
           pl.BlockSpec((RA, D), lambda r: (nb - 1 - r, 0)), _full_spec((1, 2 * A_HEADS * HD, D))]
        + [ANY_SPEC] * len(deps),
        out_specs=[pl.BlockSpec((RA, EVEN_IN), lambda r: (nb - 1 - r, 0))] + small,
        out_shape=[_sds((T, EVEN_IN), BF16), _sds((1, A_HEADS * HD), F32), _sds((1, A_HEADS * HD), F32),
                   _sds((1, HD), F32), _sds((1, B_GROUPS * HD), F32), _sds((1, B_GROUPS * HD), F32),
                   _sds((B_GROUPS, B_CHUNK, B_CHUNK), F32), _sds((B_GROUPS, B_CHUNK, 1), F32)],
        scratch_shapes=[pltpu.VMEM((A_HEADS, HD, HD), F32)],
        compiler_params=_params(1))(z, l0, l1, onorm, lnw, lnb, ws, bias, sst, dy, w_out, *deps)


def _mla_pre(z1, qn, kvn, cos_t, sin_t):
    def fn(cq, ckv, kpe, cs, sn, wq, wkv):
        return _rms(cq, wq), _rms(ckv, wkv), _rope(kpe, cs, sn)
    return _pure_call("mla_pre", fn, (T // TM,),
                      [_row_spec(TM, C_RANK, 4), _row_spec(TM, C_RANK, 5), _row_spec(TM, HD, 24),
                       _row_spec(TM, HD), _row_spec(TM, HD),
                       _full_spec((1, C_RANK)), _full_spec((1, C_RANK))],
                      [_row_spec(TM, C_RANK), _row_spec(TM, C_RANK), _row_spec(TM, HD)],
                      [_sds((T, C_RANK), BF16), _sds((T, C_RANK), BF16), _sds((T, HD), BF16)],
                      (z1, z1, z1, cos_t, sin_t, qn, kvn))


def _mla_pre_bwd(z1, qn, kvn, cos_t, sin_t, dq, w_qb, dkv, w_kvb, dkp, dgate, deps=()):
    def fn(cq, ckv, cs, sn, wq, wkv, dqv, wqb, dkvv, wkvb, g_kp, g_gate):
        _, vjp_q = jax.vjp(_rms, cq, wq)
        dcq, dwq = vjp_q(_raw_nt(dqv, wqb[0]))
        _, vjp_kv = jax.vjp(_rms, ckv, wkv)
        dckv, dwkv = vjp_kv(_raw_nt(dkvv, jnp.concatenate([wkvb[s] for s in range(N_DEV)], axis=1)))
        dz1 = jnp.concatenate([g_gate, dcq.astype(BF16), dckv.astype(BF16),
                               _rope_transpose(g_kp, cs, sn).astype(BF16)], axis=1)
        return dz1, dwq, dwkv
    return _pure_call("mla_pre_bwd", fn, (T // TM,),
                      [_row_spec(TM, C_RANK, 4), _row_spec(TM, C_RANK, 5),
                       _row_spec(TM, HD), _row_spec(TM, HD),
                       _full_spec((1, C_RANK)), _full_spec((1, C_RANK)),
                       _row_spec(TM, C_HEADS * QP), _full_spec((1, C_RANK, C_HEADS * QP)),
                       _row_spec(TM, C_HEADS * KVW), _full_spec((N_DEV, C_RANK, C_HEADS * KVW // N_DEV)),
                       _row_spec(TM, HD), _row_spec(TM, D)],
                      [_row_spec(TM, ODD_IN_PAD), _full_spec((1, C_RANK)), _full_spec((1, C_RANK))],
                      [_sds((T, ODD_IN_PAD), BF16), _sds((1, C_RANK), F32), _sds((1, C_RANK), F32)],
                      (z1, z1, cos_t, sin_t, qn, kvn, dq, w_qb, dkv, w_kvb, dkp, dgate), n_acc=2, deps=deps)


TQ = 256
HP = 2
KVW = C_NOPE + C_V


def _att_keys(kv_ref, kp_ref, k_scr):
    @pl.when(pl.program_id(1) == 0)
    def _():
        for hh in range(HP):
            k_scr[hh, :, 0:C_NOPE] = kv_ref[:, hh * KVW:hh * KVW + C_NOPE]
            k_scr[hh, :, C_NOPE:QP] = kp_ref[...]


def _att_scores(q, cos_ref, sin_ref, k_scr, hh, n):
    keys = (n + 1) * TQ
    qr = jnp.concatenate([q[:, :C_NOPE], _rope(q[:, C_NOPE:], cos_ref[...], sin_ref[...])], axis=1).astype(BF16)
    return qr, _raw_nt(qr, k_scr[hh, 0:keys, :]) * ATT_SCALE


def _causal(x, n, fill):
    row = lax.broadcasted_iota(jnp.int32, (TQ, TQ), 0)
    col = lax.broadcasted_iota(jnp.int32, (TQ, TQ), 1)
    diag = jnp.where(col <= row, x[:, n * TQ:], fill)
    return diag if n == 0 else jnp.concatenate([x[:, :n * TQ], diag], axis=1)


def _per_query_block(fn):
    for n in range(T // TQ):
        pl.when(pl.program_id(1) == n)(functools.partial(fn, n))


def _att_in_specs():
    return [pl.BlockSpec((TQ, HP * QP), lambda g, i: (i, g)),
            pl.BlockSpec((TQ, HD), lambda g, i: (i, 0)),
            pl.BlockSpec((TQ, HD), lambda g, i: (i, 0)),
            pl.BlockSpec((T, HP * KVW), lambda g, i: (0, g)),
            pl.BlockSpec((T, HD), lambda g, i: (0, 0))]


def _attention_fwd(q, cos_t, sin_t, kv, kp, z1):
    def body(q_ref, cos_ref, sin_ref, kv_ref, kp_ref, gate_ref, o_ref, lse_ref, og_ref, k_scr):
        _att_keys(kv_ref, kp_ref, k_scr)

        def block(n):
            keys = (n + 1) * TQ
            for hh in range(HP):
                _, s = _att_scores(q_ref[:, hh * QP:(hh + 1) * QP], cos_ref, sin_ref, k_scr, hh, n)
                s = _causal(s, n, jnp.finfo(F32).min)
                m = jnp.max(s, axis=-1, keepdims=True)
                p = jnp.exp(s - m)
                l = jnp.sum(p, axis=-1, keepdims=True)
                v = kv_ref[0:keys, hh * KVW + C_NOPE:(hh + 1) * KVW]
                o = _raw_nn(p, v) / l
                lanes = slice(hh * C_V, (hh + 1) * C_V)
                o_ref[:, lanes] = o
                og_ref[:, lanes] = (o * _silu(gate_ref[:, lanes])).astype(og_ref.dtype)
                lse_ref[hh] = m + jnp.log(l)

        _per_query_block(block)

    heads = pl.BlockSpec((TQ, HP * C_V), lambda g, i: (i, g))
    return pl.pallas_call(
        body, name="attention_fwd", grid=(C_HEADS // HP, T // TQ), in_specs=_att_in_specs() + [heads],
        out_specs=[heads, pl.BlockSpec((HP, TQ, 1), lambda g, i: (g, i, 0)), heads],
        out_shape=[_sds((T, C_HEADS * C_V), F32), _sds((C_HEADS, T, 1), F32), _sds((T, C_HEADS * C_V), BF16)],
        scratch_shapes=[pltpu.VMEM((HP, T, QP), BF16)],
        compiler_params=_params(2))(q, cos_t, sin_t, kv, kp, z1)


def _attention_bwd(q, cos_t, sin_t, kv, kp, o, lse, dog, z1):
    nq = T // TQ

    def body(q_ref, cos_ref, sin_ref, kv_ref, kp_ref, o_ref, lse_ref, dog_ref, gate_ref,
             dq_ref, dkv_ref, dkp_ref, dgate_ref, k_scr, dk_scr, dv_scr):
        g, i = pl.program_id(0), pl.program_id(1)
        _att_keys(kv_ref, kp_ref, k_scr)

        @pl.when(i == 0)
        def _():
            dv_scr[...] = jnp.zeros_like(dv_scr)
            dk_scr[...] = jnp.zeros_like(dk_scr)

        def block(n):
            keys = (n + 1) * TQ
            for hh in range(HP):
                qr, s = _att_scores(q_ref[:, hh * QP:(hh + 1) * QP], cos_ref, sin_ref, k_scr, hh, n)
                p = _causal(jnp.exp(s - lse_ref[hh]), n, 0.0)
                lanes = slice(hh * C_V, (hh + 1) * C_V)
                ov, gate, dogv = o_ref[:, lanes], gate_ref[:, lanes], dog_ref[:, lanes]
                sig = _sigmoid(gate)
                silu = gate * sig
                dov = dogv * silu
                dgate_ref[:, lanes] = (dogv * ov * (sig + silu * (1.0 - sig))).astype(dgate_ref.dtype)
                delta = jnp.sum(dov * ov, axis=-1, keepdims=True)
                dp = _raw_nt(dov, kv_ref[0:keys, hh * KVW + C_NOPE:(hh + 1) * KVW])
                ds = p * (dp - delta) * ATT_SCALE
                dq = _raw_nn(ds, k_scr[hh, 0:keys, :])
                dq_ref[:, hh * QP:(hh + 1) * QP] = jnp.concatenate(
                    [dq[:, :C_NOPE], _rope_transpose(dq[:, C_NOPE:], cos_ref[...], sin_ref[...])],
                    axis=1).astype(dq_ref.dtype)
                dv_scr[hh, 0:keys, :] += _raw_tn(p, dov)
                dk_scr[hh, 0:keys, :] += _raw_tn(ds, qr)

        _per_query_block(block)

        @pl.when(i == nq - 1)
        def _():
            for hh in range(HP):
                dkv_ref[:, hh * KVW:(hh + 1) * KVW] = jnp.concatenate(
                    [dk_scr[hh, :, 0:C_NOPE], dv_scr[hh]], axis=1).astype(dkv_ref.dtype)

        @pl.when(jnp.logical_and(i == nq - 1, g == 0))
        def _():
            dkp_ref[...] = dk_scr[0, :, C_NOPE:QP]

        @pl.when(jnp.logical_and(i == nq - 1, g > 0))
        def _():
            dkp_ref[...] += dk_scr[0, :, C_NOPE:QP]

        @pl.when(i == nq - 1)
        def _():
            for hh in range(1, HP):
                dkp_ref[...] += dk_scr[hh, :, C_NOPE:QP]

    heads = pl.BlockSpec((TQ, HP * C_V), lambda g, i: (i, g))
    return pl.pallas_call(
        body, name="attention_bwd", grid=(C_HEADS // HP, nq),
        in_specs=_att_in_specs() + [heads, pl.BlockSpec((HP, TQ, 1), lambda g, i: (g, i, 0)), heads, heads],
        out_specs=[pl.BlockSpec((TQ, HP * QP), lambda g, i: (i, g)),
                   pl.BlockSpec((T, HP * KVW), lambda g, i: (0, g)),
                   _full_spec((T, HD)), heads],
        out_shape=[_sds((T, C_HEADS * QP), BF16), _sds((T, C_HEADS * KVW), BF16), _sds((T, HD), F32),
                   _sds((T, C_HEADS * C_V), BF16)],
        scratch_shapes=[pltpu.VMEM((HP, T, QP), BF16), pltpu.VMEM((HP, T, QP), F32), pltpu.VMEM((HP, T, C_V), F32)],
        compiler_params=_params(2))(q, cos_t, sin_t, kv, kp, o, lse, dog, z1)


def _adamw_math(w, g, m, v):
    m = ADAM_B1 * m + (1.0 - ADAM_B1) * g
    v = ADAM_B2 * v + (1.0 - ADAM_B2) * (g * g)
    m_hat = m / (1.0 - ADAM_B1 ** ADAM_STEP)
    v_hat = v / (1.0 - ADAM_B2 ** ADAM_STEP)
    delta = -ADAM_LR * (m_hat / (jnp.sqrt(v_hat) + ADAM_EPS) + ADAM_WD * w)
    return delta, m, v


def _adamw(name, parts, w, m, v, tr, tc=None):
    rows, cols = w.shape

    def fn(*vals):
        pvs, (wv, mv, vv) = vals[:len(parts)], vals[len(parts):]
        g = None
        for pv in pvs:
            for d in range(pv.shape[0]):
                term = pv[d].astype(F32)
                g = term if g is None else g + term
        return (g,) + _adamw_math(wv, g, mv, vv)

    tc = cols if tc is None else tc
    blk = pl.BlockSpec((tr, tc), lambda i, j: (i, j))
    part_specs = [pl.BlockSpec((n, tr, tc), lambda i, j: (0, i, j)) for _, n in parts]
    return _pure_call(name, fn, (rows // tr, cols // tc), part_specs + [blk, blk, blk],
                      [blk] * 4, [_sds((rows, cols), F32)] * 4, tuple(p for p, _ in parts) + (w, m, v))


SMALL_PARAM_SHAPES = ((2, D), (2, D), (2, A_HEADS * HD), (1, HD), (1, B_GROUPS * HD), (1, B_GROUPS * HD),
                      (B_GROUPS, B_CHUNK, B_CHUNK), (B_GROUPS, B_CHUNK))
SMALL_PIECES = ((0, 0, 0, 0), (0, 1, 1, 0), (1, 0, 1, 1), (1, 1, 1, 2), (2, 0, 2, 0), (2, 1, 2, 1),
                (3, 0, 3, 8), (4, 0, 2, 2), (5, 0, 2, 3))


def _small_rows(dnpre1, dnpost0, dnpost1, dl0, dl1, donorm, dlnw, dlnb, dws, dbias, dqn, dkvn, loss_part):
    return [jnp.concatenate([dnpre1, dnpost0, dnpost1], axis=0),
            jnp.concatenate([dl0, dl1, dlnw, dlnb], axis=0),
            jnp.concatenate([dbias.reshape(B_GROUPS, B_CHUNK), donorm, loss_part], axis=0),
            dws,
            jnp.concatenate([dqn, dkvn], axis=0)]


def _adamw_small(late_all, early_all, wmv):
    n_in = 6 + 3 * len(wmv)

    def body(*refs):
        gathered, params, outs = refs[:6], refs[6:n_in], refs[n_in:]

        def total(ref):
            s = ref[0]
            for d in range(1, N_DEV):
                s = s + ref[d]
            return s

        g_late, g2048, g1024, g128, g_ws, g512 = [total(r) for r in gathered]
        arrays = (g_late, g2048, g1024, g128)

        def update(p, rows, g):
            w_ref, m_ref, v_ref = params[3 * p:3 * p + 3]
            delta, m, v = _adamw_math(w_ref[rows], g, m_ref[rows], v_ref[rows])
            for out, val in zip(outs[4 * p:4 * p + 4], (g, delta, m, v)):
                out[rows] = val

        for p, row, arr, arr_row in SMALL_PIECES:
            update(p, pl.ds(row, 1), arrays[arr][arr_row:arr_row + 1])
        update(6, slice(None), g_ws)
        update(7, slice(None), g128[0:B_GROUPS])
        outs[32][...] = g128[B_GROUPS + 1:B_GROUPS + 2]
        outs[33][...] = g512

    vmem = pl.BlockSpec(memory_space=pltpu.VMEM)
    flat = [a for t in wmv for a in t]
    out_shape = [_sds(s, F32) for s in SMALL_PARAM_SHAPES for _ in range(4)] + [_sds((1, 128), F32), _sds((2, C_RANK), F32)]
    res = pl.pallas_call(body, name="adamw_small", in_specs=[vmem] * n_in, out_specs=[vmem] * len(out_shape),
                         out_shape=out_shape,
                         compiler_params=pltpu.CompilerParams(vmem_limit_bytes=VMEM_LIMIT_V7X))(late_all, *early_all, *flat)
    return [res[4 * p:4 * p + 4] for p in range(8)], res[32], res[33]


def _exchange(name, arrs, gather, deps=()):
    n = len(arrs)
    deps = _live(deps)

    def body(*refs):
        ins, outs = refs[:n], refs[n + len(deps):2 * n + len(deps)]
        send_sems, recv_sems, local_sems = refs[2 * n + len(deps):]
        x, y, c = lax.axis_index("x"), lax.axis_index("y"), lax.axis_index("c")
        me = 4 * x + 2 * y + c

        def peer(k):
            return (x ^ (k >> 2), y ^ ((k >> 1) & 1), c ^ (k & 1))

        def copy(a, k):
            src = ins[a] if gather else ins[a].at[me ^ k]
            return pltpu.make_async_remote_copy(
                src_ref=src, dst_ref=outs[a].at[me], send_sem=send_sems.at[a, k - 1],
                recv_sem=recv_sems.at[a, k - 1], device_id=peer(k), device_id_type=MESH_ID)

        def arrival(a, k):
            src = ins[a] if gather else ins[a].at[me]
            return pltpu.make_async_remote_copy(
                src_ref=src, dst_ref=outs[a].at[me ^ k], send_sem=send_sems.at[a, k - 1],
                recv_sem=recv_sems.at[a, k - 1], device_id=peer(k), device_id_type=MESH_ID)

        own = [pltpu.make_async_copy(ins[a] if gather else ins[a].at[me], outs[a].at[me], local_sems.at[a])
               for a in range(n)]
        for cp in own:
            cp.start()
        for k in range(1, N_DEV):
            for a in range(n):
                copy(a, k).start()
        for k in range(1, N_DEV):
            for a in range(n):
                arrival(a, k).wait_recv()
        for k in range(1, N_DEV):
            for a in range(n):
                copy(a, k).wait_send()
        for cp in own:
            cp.wait()

    any_spec = pl.BlockSpec(memory_space=pl.ANY)
    out_shape = [_sds((N_DEV,) + a.shape if gather else a.shape, a.dtype) for a in arrs]
    return pl.pallas_call(
        body, name=name, in_specs=[any_spec] * (n + len(deps)), out_specs=[any_spec] * n, out_shape=out_shape,
        scratch_shapes=[pltpu.SemaphoreType.DMA((n, N_DEV - 1)), pltpu.SemaphoreType.DMA((n, N_DEV - 1)),
                        pltpu.SemaphoreType.DMA((n,))],
        compiler_params=pltpu.CompilerParams(has_side_effects=True))(*arrs, *deps)


HBM_SPEC = pl.BlockSpec(memory_space=pltpu.HBM)
SEM_SPEC = pl.BlockSpec(memory_space=pltpu.SEMAPHORE)
DATAFLOW = pltpu.SideEffectType.DATAFLOW_SIDE_EFFECTING


def _my_index():
    return 4 * lax.axis_index("x") + 2 * lax.axis_index("y") + lax.axis_index("c")


def _plan_copies(plan, refs, send_sems, recv_sems):
    x, y, c = lax.axis_index("x"), lax.axis_index("y"), lax.axis_index("c")
    return [pltpu.make_async_remote_copy(
        src_ref=src, dst_ref=dst, send_sem=send_sems.at[i], recv_sem=recv_sems.at[i],
        device_id=(x ^ (k >> 2), y ^ ((k >> 1) & 1), c ^ (k & 1)), device_id_type=MESH_ID)
        for i, (src, dst, k) in enumerate(plan(refs, 4 * x + 2 * y + c))]


def _split_call(name, bufs, waits=None, starts=None, deps=()):
    n = len(bufs)
    deps = _live(deps)
    n_wait = 2 if waits else 0

    def body(*refs):
        zones = refs[:n]
        if waits:
            for cp in _plan_copies(waits[2], zones, refs[n], refs[n + 1]):
                cp.wait_send()
                cp.wait_recv()
        if starts:
            first_out = n + n_wait + len(deps)
            for cp in _plan_copies(starts[0], zones, refs[first_out], refs[first_out + 1]):
                cp.start()
            refs[-1][...] = jnp.zeros_like(refs[-1])

    out_specs, out_shape = [], []
    if starts:
        sems = pltpu.SemaphoreType.DMA((starts[1],))
        out_specs, out_shape = [SEM_SPEC, SEM_SPEC], [sems, sems]
    out_specs += [HBM_SPEC] * n
    out_shape += [pltpu.HBM(b.shape, b.dtype) for b in bufs]
    if starts:
        out_specs.append(pl.BlockSpec(memory_space=pltpu.VMEM))
        out_shape.append(_sds((8, 128), F32))
    first_buf = 2 if starts else 0
    res = pl.pallas_call(
        body, name=name,
        in_specs=[HBM_SPEC] * n + [SEM_SPEC] * n_wait + [ANY_SPEC] * len(deps),
        out_specs=out_specs, out_shape=out_shape,
        input_output_aliases={i: first_buf + i for i in range(n)},
        compiler_params=pltpu.CompilerParams(has_side_effects=DATAFLOW),
    )(*[pltpu.with_memory_space_constraint(b, pltpu.HBM) for b in bufs], *(waits[:2] if waits else ()), *deps)
    out_bufs = list(res[first_buf:first_buf + n])
    return out_bufs, ((res[0], res[1]) if starts else None), (res[-1] if starts else None)


def _direct_plan(n, gather):
    def plan(refs, me):
        return [(refs[a] if gather else refs[a].at[me ^ k], refs[n + a].at[me], k)
                for k in range(1, N_DEV) for a in range(n)]
    return plan


def _own_slot_filled(a, gather):
    me = _my_index()
    if gather:
        return lax.dynamic_update_slice_in_dim(lax.empty((N_DEV,) + a.shape, a.dtype), a[None], me, 0)
    return lax.dynamic_update_slice_in_dim(lax.empty(a.shape, a.dtype), lax.dynamic_slice_in_dim(a, me, 1, 0), me, 0)


def _exchange_start(name, arrs, gather, deps=()):
    n = len(arrs)
    lands = [_own_slot_filled(a, gather) for a in arrs]
    plan = _direct_plan(n, gather)
    bufs, sems, token = _split_call(name, list(arrs) + lands, starts=(plan, n * (N_DEV - 1)), deps=deps)
    return (n, plan, sems, bufs, None), token


def _exchange_wait(name, handle, after):
    return _split_done(name, handle, after)


ICI_PEERS = (2, 4, 6)
SIBLING = 1


def _gather2_send(name, arrs, deps=(), fill_after=False):
    n = len(arrs)
    lands = [lax.empty((N_DEV,) + a.shape, a.dtype) if fill_after else _own_slot_filled(a, True) for a in arrs]

    def plan(refs, me_):
        return [(refs[a], refs[n + a].at[me_], k) for k in (SIBLING,) + ICI_PEERS for a in range(n)]

    bufs, sems, token = _split_call(name, list(arrs) + lands, starts=(plan, 4 * n), deps=deps)
    if fill_after:
        me = _my_index()
        bufs = bufs[:n] + [lax.dynamic_update_slice_in_dim(z, a[None], me, 0) for z, a in zip(bufs[n:], bufs[:n])]
    return (n, plan, sems, bufs, None), token


def _gather2_relay(name, handle, after):
    n, plan, sems, bufs, _ = handle
    after = after if isinstance(after, (list, tuple)) else [after]

    def relay(refs, me_):
        return [(refs[n + a].at[me_ ^ k], refs[n + a].at[me_ ^ k], SIBLING) for k in ICI_PEERS for a in range(n)]

    bufs, sems2, token = _split_call(name, bufs, waits=(sems[0], sems[1], plan), starts=(relay, 3 * n), deps=after)
    return (n, relay, sems2, bufs, None), token


def _split_done(name, handle, after, all_bufs=False):
    n, plan, sems, bufs, _ = handle
    bufs, _, _ = _split_call(name, bufs, waits=(sems[0], sems[1], plan), deps=[after])
    return bufs if all_bufs else bufs[n:]


def _scatter2_pair(name, for_sibling, deps=()):
    n = len(for_sibling)
    pairs = [lax.empty(s.shape, s.dtype) for s in for_sibling]

    def plan(refs, me):
        del me
        return [(refs[a].at[s], refs[n + a].at[s], SIBLING) for s in range(4) for a in range(n)]

    bufs, sems, token = _split_call(name, list(for_sibling) + pairs, starts=(plan, 4 * n), deps=deps)
    return (n, plan, sems, bufs, None), token


def _pair_add(name, mine, pair):
    _, rows, cols = mine.shape
    tr = rows // 2

    def fn(a, b):
        return a.astype(F32) + b.astype(F32)

    blk = pl.BlockSpec((None, tr, cols), lambda s, i: (s, i, 0))
    return _pure_call(name, fn, (4, rows // tr), [blk, blk], [blk], [_sds(mine.shape, mine.dtype)], (mine, pair))[0]


def _scatter2_send(name, chip_sums, deps=()):
    n = len(chip_sums)
    finals = [lax.empty((3,) + c.shape[1:], c.dtype) for c in chip_sums]

    def plan(refs, me):
        return [(refs[a].at[(me >> 1) ^ j], refs[n + a].at[j - 1], 2 * j) for j in range(1, 4) for a in range(n)]

    bufs, sems, token = _split_call(name, list(chip_sums) + finals, starts=(plan, 3 * n), deps=deps)
    return (n, plan, sems, bufs, None), token


def _pad_rope(p):
    z = jnp.zeros(p.shape[:-1] + (32,), p.dtype)
    return jnp.concatenate([p[..., :32], z, p[..., 32:], z], axis=-1)


def _unpad_rope(p):
    return jnp.concatenate([p[..., :32], p[..., 64:96]], axis=-1)


def _odd_in_layout(wt):
    wt = wt.reshape(ODD_IN, D)
    cq, ckv, kpe, gate = wt[:512], wt[512:1024], wt[1024:1088], wt[1088:]
    z = jnp.zeros((32, D), wt.dtype)
    return jnp.concatenate([gate, cq, ckv, kpe[:32], z, kpe[32:], z], axis=0)


def _odd_in_unlayout(dwt):
    gate, cq, ckv, kpe = dwt[:2048], dwt[2048:2560], dwt[2560:3072], dwt[3072:]
    wt = jnp.concatenate([cq, ckv, kpe[:32], kpe[64:96], gate], axis=0)
    return wt.reshape(N_DEV, ODD_IN // N_DEV, D)


def _qb_layout(w):
    w = w.transpose(1, 0, 2).reshape(C_RANK, C_HEADS, C_QK)
    w = jnp.concatenate([w[..., :C_NOPE], _pad_rope(w[..., C_NOPE:])], axis=-1)
    return w.reshape(C_RANK, C_HEADS * QP)


def _qb_unlayout(dw):
    dw = dw.reshape(C_RANK, C_HEADS, QP)
    dw = jnp.concatenate([dw[..., :C_NOPE], _unpad_rope(dw[..., C_NOPE:])], axis=-1)
    return dw.reshape(C_RANK, N_DEV, C_HEADS * C_QK // N_DEV).transpose(1, 0, 2)


def _rope_tables(positions):
    inv_freq = ROPE_THETA ** (-jnp.arange(0, C_ROPE, 2, dtype=F32) / C_ROPE)
    ang = positions.astype(F32)[0][:, None] * inv_freq
    cos, sin = jnp.cos(ang), jnp.sin(ang)
    z = jnp.zeros_like(cos)
    return jnp.concatenate([cos, z, cos, z], axis=1), jnp.concatenate([-sin, z, sin, z], axis=1)


def _forward_backward(x, cos_t, sin_t, target, norm_pre, norm_post, lb_logits, a_onorm, ln_w, ln_b,
                      b_ws, b_bias, get_w, put_g, put_small=None, start_dep=None):
    npre0, npre1 = norm_pre[0:1], norm_pre[1:2]
    npost0, npost1 = norm_post[0:1], norm_post[1:2]
    l0, l1 = lb_logits[0:1], lb_logits[1:2]
    bias_col = b_bias.reshape(B_GROUPS, B_CHUNK, 1)
    ws = b_ws.reshape(B_GROUPS, B_CHUNK, B_CHUNK)

    h0 = _pre_norm("pre_norm0", x, npre0, deps=[start_dep])
    w_ev_in = get_w("ev_in", h0)
    z0 = _mm_nn("ev_in", h0, w_ev_in, F32, 1024, 896)
    cat, sst = _even_fwd(z0, l0, l1, a_onorm, ln_w, ln_b, ws, bias_col)
    w_ev_out = get_w("ev_out", cat)
    y0 = _mm_nn("ev_out", cat, w_ev_out, F32, 1024, 1024)
    get_w("od_relay", y0)
    x1, h1 = _post_pre_norm(x, y0, npost0, npre1)
    w_od_in, w_qb, w_kvb, q_norm, kv_norm = get_w("od_mid", h1)
    z1 = _mm_nt("od_in", h1, w_od_in[None], F32, 1024, 640)
    cqn, ckvn, kp = _mla_pre(z1, q_norm, kv_norm, cos_t, sin_t)
    q = _mm_nn("od_qb", cqn, w_qb[None], F32, 1024, 1024)
    kv = _mm_nn("od_kvb", ckvn, w_kvb, BF16, 1024, 512)
    o, lse, og = _attention_fwd(q, cos_t, sin_t, kv, kp, z1)
    w_od_out = get_w("od_out", og)
    dx2, dy1, loss_part, dnpost1 = _out_proj_loss(og, w_od_out, x1, npost1, target)

    g_od_out = _mm_tn("od_out_dw", og, dy1, 1, BF16, 1024, 1024)
    tok = put_g("od_out", [g_od_out.reshape(N_DEV, D // N_DEV, D)])
    dog = _mm_nt("od_out_dx", dy1, w_od_out, F32, 1024, 1024, deps=[tok])
    dq, dkv, dkp, dgate = _attention_bwd(q, cos_t, sin_t, kv, kp, o, lse, dog, z1)
    g_qb = _mm_tn("od_qb_dw", cqn, dq, 1, BF16, 512, 1024)
    g_kvb = _mm_tn("od_kvb_dw", ckvn, dkv, N_DEV, BF16, 512, 512)
    tok = put_g("od_qkv", [_qb_unlayout(g_qb[0]), g_kvb])
    dz1, dqn, dkvn = _mla_pre_bwd(z1, q_norm, kv_norm, cos_t, sin_t, dq, w_qb[None], dkv, w_kvb, dkp, dgate,
                                  deps=[tok])
    g_od_in = _mm_tn("od_in_dw", dz1, h1, 1, BF16, 640, 1024)
    tok = put_g("od_in", [_odd_in_unlayout(g_od_in[0])])
    dx1, dy0, dnpost0, dnpre1 = _in_proj_norms_bwd(dz1, w_od_in[None], y0, x1, npost0, npre1, dx2, deps=[tok])

    g_ev_out = _mm_tn("ev_out_dw", cat, dy0, 1, BF16, 1024, 1024)
    tok = put_g("ev_out", [g_ev_out.reshape(N_DEV, D // N_DEV, D)])
    dz0, dl0, dl1, donorm, dlnw, dlnb, dws, dbias = _even_bwd(z0, l0, l1, a_onorm, ln_w, ln_b, ws, bias_col, sst,
                                                              dy0, w_ev_out, deps=[tok])
    early = _small_rows(dnpre1, dnpost0, dnpost1, dl0, dl1, donorm, dlnw, dlnb, dws, dbias, dqn, dkvn, loss_part)
    tok = put_small(early) if put_small else None
    small_tok = tok

    def ev_in_half(name, parity, deps=()):
        return _mm_tn_parity(name, h0, dz0, N_DEV, parity, BF16, 1024, deps=[small_tok] + list(deps))

    tok = put_g("ev_in", ev_in_half)
    dh0 = _mm_nt("ev_in_dx", dz0, w_ev_in, F32, 1024, 256, deps=[tok])
    grad_x, dnpre0 = _pre_norm_bwd(x, npre0, dh0, dx1)
    return grad_x, early, dnpre0


def kernel(x, positions, norm_pre, norm_post, ev_w_in, ev_lb_logits, ev_a_onorm, ev_b_ln_w, ev_b_ln_b, ev_b_ws, ev_b_bias, ev_w_out, od_w_in, od_q_norm, od_w_qb, od_kv_norm, od_w_kvb, od_w_out, loss_target, m_norm_pre, m_norm_post, m_ev_w_in, m_ev_lb_logits, m_ev_a_onorm, m_ev_b_ln_w, m_ev_b_ln_b, m_ev_b_ws, m_ev_b_bias, m_ev_w_out, m_od_w_in, m_od_q_norm, m_od_w_qb, m_od_kv_norm, m_od_w_kvb, m_od_w_out, v_norm_pre, v_norm_post, v_ev_w_in, v_ev_lb_logits, v_ev_a_onorm, v_ev_b_ln_w, v_ev_b_ln_b, v_ev_b_ws, v_ev_b_bias, v_ev_w_out, v_od_w_in, v_od_q_norm, v_od_w_qb, v_od_kv_norm, v_od_w_kvb, v_od_w_out):
    me = 4 * lax.axis_index("x") + 2 * lax.axis_index("y") + lax.axis_index("c")
    bf = lambda w: w[0].astype(BF16)

    norms = jnp.pad(jnp.concatenate([od_q_norm, od_kv_norm], axis=1), ((0, 7), (0, 0)))
    sent = {}
    sent["ev_in"], tok = _gather2_send("gather_ev_in", [bf(ev_w_in)], fill_after=True)
    sent["ev_out"], tok = _gather2_send("gather_ev_out", [bf(ev_w_out)], deps=[tok])
    sent["od"], tok = _gather2_send("gather_od", [od_w_in[0].T.astype(BF16), bf(od_w_qb), bf(od_w_kvb), norms,
                                                 bf(od_w_out)], deps=[tok])
    cos_t, sin_t = _rope_tables(positions)
    od = []

    def get_w(group, after):
        if group == "ev_in":
            relayed, token = _gather2_relay("relay_ev_in", sent["ev_in"], [after, cos_t, sin_t])
            return _split_done("arrived_ev_in", relayed, token)[0]
        if group == "ev_out":
            relayed, token = _gather2_relay("relay_ev_out", sent["ev_out"], after)
            return _split_done("arrived_ev_out", relayed, token)[0].reshape(1, D, D)
        if group == "od_relay":
            sent["od_relayed"], _ = _gather2_relay("relay_od", sent["od"], after)
            return None
        if not od:
            od.extend(_split_done("arrived_od", sent["od_relayed"], after))
        w_od_in, w_qb, w_kvb, norms_all, w_od_out = od
        if group == "od_out":
            return w_od_out.reshape(1, D, D)
        return (_odd_in_layout(w_od_in), _qb_layout(w_qb), w_kvb,
                norms_all[:, 0, :64].reshape(1, C_RANK), norms_all[:, 0, 64:].reshape(1, C_RANK))

    scatters = {}

    def put_g(group, grads):
        if group == "ev_in":
            core = lax.axis_index("c").astype(jnp.int32).reshape(1)
            paired, token = _scatter2_pair("pair_ev_in", [grads("ev_in_dw_sibling", 1 - core)])
            mine = grads("ev_in_dw_own", core, deps=[token])
            pair = _split_done("paired_ev_in", paired, mine)[0]
            scatters[group], token = _scatter2_send("scatter_ev_in", [_pair_add("pair_add_ev_in", mine, pair)])
        else:
            scatters[group], token = _exchange_start("scatter_" + group, grads, False)
        return token

    def put_small(early):
        scatters["small"], token = _exchange_start("gather_small_early", early, True)
        return token

    grad_x, _, dnpre0 = _forward_backward(
        x[0], cos_t, sin_t, loss_target[0], norm_pre, norm_post, ev_lb_logits, ev_a_onorm, ev_b_ln_w,
        ev_b_ln_b, ev_b_ws, ev_b_bias, get_w, put_g, put_small, start_dep=tok)

    big_w = {"ev_w_in": ev_w_in, "ev_w_out": ev_w_out, "od_w_in": od_w_in, "od_w_qb": od_w_qb,
             "od_w_kvb": od_w_kvb, "od_w_out": od_w_out}
    big_m = {"ev_w_in": m_ev_w_in, "ev_w_out": m_ev_w_out, "od_w_in": m_od_w_in, "od_w_qb": m_od_w_qb,
             "od_w_kvb": m_od_w_kvb, "od_w_out": m_od_w_out}
    big_v = {"ev_w_in": v_ev_w_in, "ev_w_out": v_ev_w_out, "od_w_in": v_od_w_in, "od_w_qb": v_od_w_qb,
             "od_w_kvb": v_od_w_kvb, "od_w_out": v_od_w_out}
    big_out = {}
    after = grad_x
    for group, names in (("od_out", ["od_w_out"]), ("od_qkv", ["od_w_qb", "od_w_kvb"]), ("od_in", ["od_w_in"]),
                         ("ev_out", ["ev_w_out"])):
        parts = _exchange_wait("summed_" + group, scatters[group], after)
        for nm, p in zip(names, parts):
            w, m, v = big_w[nm][0], big_m[nm][0], big_v[nm][0]
            if nm == "od_w_in":
                res = _adamw("adamw_" + nm, [(p, N_DEV)], w.T, m.T, v.T, w.shape[1], 512)
                big_out[nm] = [r.T[None] for r in res]
            else:
                res = _adamw("adamw_" + nm, [(p, N_DEV)], w, m, v, w.shape[0] // 8)
                big_out[nm] = [r[None] for r in res]
            after = res[0]

    late_all = _exchange("gather_small_late", [dnpre0], gather=True, deps=[after])[0]
    early_all = _exchange_wait("arrived_small_early", scatters["small"], late_all)

    small_w = (norm_pre, norm_post, ev_lb_logits, ev_a_onorm, ev_b_ln_w, ev_b_ln_b, ev_b_ws, ev_b_bias)
    small_m = (m_norm_pre, m_norm_post, m_ev_lb_logits, m_ev_a_onorm, m_ev_b_ln_w, m_ev_b_ln_b, m_ev_b_ws, m_ev_b_bias)
    small_v = (v_norm_pre, v_norm_post, v_ev_lb_logits, v_ev_a_onorm, v_ev_b_ln_w, v_ev_b_ln_b, v_ev_b_ws, v_ev_b_bias)
    wmv = [tuple(a.reshape(s) for a in t) for s, t in zip(SMALL_PARAM_SHAPES, zip(small_w, small_m, small_v))]
    small_res, loss_row, g_norm_rows = _adamw_small(late_all, early_all, wmv)
    small_out = [[r.reshape(w.shape) for r in four] for four, w in zip(small_res, small_w)]
    loss = loss_row[0, 0]

    g_norms = jnp.concatenate([lax.dynamic_slice(g_norm_rows, (0, 64 * me), (1, 64)),
                               lax.dynamic_slice(g_norm_rows, (1, 64 * me), (1, 64))], axis=1)
    res_n = _adamw("adamw_norms", [(g_norms[None], 1)],
                   jnp.concatenate([od_q_norm, od_kv_norm], axis=1),
                   jnp.concatenate([m_od_q_norm, m_od_kv_norm], axis=1),
                   jnp.concatenate([v_od_q_norm, v_od_kv_norm], axis=1), 1)
    qn_out = [r[:, :64] for r in res_n]
    kvn_out = [r[:, 64:] for r in res_n]

    chip_sums, from_peers = _split_done("summed_ev_in", scatters["ev_in"], loss_row, all_bufs=True)
    own_chip = lax.dynamic_slice_in_dim(chip_sums, me >> 1, 1, 0)
    w = ev_w_in[0]
    big_out["ev_w_in"] = [r[None] for r in _adamw("adamw_ev_w_in", [(own_chip, 1), (from_peers, 3)], w, m_ev_w_in[0],
                                                  v_ev_w_in[0], w.shape[0] // 8)]

    order = ("norm_pre", "norm_post", "ev_w_in", "ev_lb_logits", "ev_a_onorm", "ev_b_ln_w", "ev_b_ln_b",
             "ev_b_ws", "ev_b_bias", "ev_w_out", "od_w_in", "od_q_norm", "od_w_qb", "od_kv_norm",
             "od_w_kvb", "od_w_out")
    small_names = ("norm_pre", "norm_post", "ev_lb_logits", "ev_a_onorm", "ev_b_ln_w", "ev_b_ln_b",
                   "ev_b_ws", "ev_b_bias")
    outs = [loss, grad_x[None]]
    for kind in range(4):
        for nm in order:
            if nm in big_out:
                outs.append(big_out[nm][kind])
            elif nm == "od_q_norm":
                outs.append(qn_out[kind])
            elif nm == "od_kv_norm":
                outs.append(kvn_out[kind])
            else:
                outs.append(small_out[small_names.index(nm)][kind])
    return tuple(outs)
```
